```python
import jax
import jax.numpy as jnp
from jax import lax
import numpy as np

D_MODEL = 2048
BATCH = 8
SEQ = 2048
DEPTH = 1

GRID_W = 64
CTX_LEN = 256
HGRN_HEADS = 8
HGRN_HEAD_DIM = 128
HGRN_WIDTH = HGRN_HEADS * HGRN_HEAD_DIM
HGRN_CHUNK = 64
NA_HEADS = 8
NA_HEAD_DIM = 128
NA_WIDTH = NA_HEADS * NA_HEAD_DIM
WIN_R = 8
WIN_C = 16
ROPE_THETA = 10000.0
FFN_HIDDEN = 5632
CONV_W = 3
N_MOD = 6
EPS = 1e-6
IN_COLS = 5 * HGRN_WIDTH + 3 * NA_WIDTH + 2 * D_MODEL

kernel_name = 'hybrid_hgrn2_natten_convffn_dit'


def rmsnorm(x, g):
    xf = x.astype(jnp.float32)
    y = xf * lax.rsqrt(jnp.mean(xf * xf, axis=-1, keepdims=True) + EPS)
    return (y * g.astype(jnp.float32)).astype(x.dtype)


def modulate(h, shift, scale):
    return h * (1.0 + scale) + shift


def split_heads(t, n_heads):
    return t.reshape(t.shape[:-1] + (n_heads, t.shape[-1] // n_heads))


def split_columns(p):
    sizes = (HGRN_WIDTH,) * 5 + (NA_WIDTH,) * 3 + (D_MODEL, D_MODEL)
    outs, off = [], 0
    for s in sizes:
        outs.append(p[..., off:off + s])
        off += s
    return outs


def hgrn2_chunk_scan(q, logf, k, v, s0):
    bsz, L, nh, d = q.shape
    nc = L // HGRN_CHUNK

    def to_chunks(t):
        return t.reshape(bsz, nc, HGRN_CHUNK, nh, d).transpose(1, 0, 3, 2, 4)

    causal = jnp.tril(jnp.ones((HGRN_CHUNK, HGRN_CHUNK), dtype=bool))[:, :, None]

    def step(S, inp):
        qc, gc, kc, vc = inp
        cum = jnp.cumsum(gc, axis=2)
        o_inter = jnp.einsum('bhtk,bhkv->bhtv', qc * jnp.exp(cum), S)
        diff = cum[:, :, :, None, :] - cum[:, :, None, :, :]
        decay = jnp.where(causal, jnp.exp(jnp.where(causal, diff, 0.0)), 0.0)
        scores = jnp.einsum('bhtk,bhsk,bhtsk->bhts', qc, kc, decay)
        o_intra = jnp.einsum('bhts,bhsv->bhtv', scores, vc)
        last = cum[:, :, -1:, :]
        S_new = jnp.exp(last[:, :, 0, :])[..., None] * S + jnp.einsum('bhsk,bhsv->bhkv', kc * jnp.exp(last - cum), vc)
        return S_new, o_inter + o_intra

    s_fin, o = lax.scan(step, s0, (to_chunks(q), to_chunks(logf), to_chunks(k), to_chunks(v)))
    return o.transpose(1, 0, 3, 2, 4).reshape(bsz, L, nh, d), s_fin


def hgrn2_prep(q, f_logit, i_val, lb):
    f = lb + (1.0 - lb) * jax.nn.sigmoid(f_logit.astype(jnp.float32))
    return (split_heads(q.astype(jnp.float32), HGRN_HEADS),
            split_heads(jnp.log(f), HGRN_HEADS),
            split_heads(1.0 - f, HGRN_HEADS),
            split_heads(i_val.astype(jnp.float32), HGRN_HEADS))


def hgrn2_direction(ctx_in, lat_in, lb, reverse):
    ctx_t = hgrn2_prep(ctx_in[0], ctx_in[1], ctx_in[2], lb)
    lat_t = hgrn2_prep(lat_in[0], lat_in[1], lat_in[2], lb)
    if reverse:
        ctx_t = tuple(jnp.flip(t, axis=1) for t in ctx_t)
        lat_t = tuple(jnp.flip(t, axis=1) for t in lat_t)
    bsz = lat_t[0].shape[0]
    s0 = jnp.zeros((bsz, HGRN_HEADS, HGRN_HEAD_DIM, HGRN_HEAD_DIM), jnp.float32)
    o_ctx, s_ctx = hgrn2_chunk_scan(ctx_t[0], ctx_t[1], ctx_t[2], ctx_t[3], s0)
    o_lat, _ = hgrn2_chunk_scan(lat_t[0], lat_t[1], lat_t[2], lat_t[3], s_ctx)
    if reverse:
        o_ctx = jnp.flip(o_ctx, axis=1)
        o_lat = jnp.flip(o_lat, axis=1)
    return o_lat, o_ctx


def hgrn2_readout(o, g, norm_g, dtype):
    on = o * lax.rsqrt(jnp.mean(o * o, axis=-1, keepdims=True) + EPS) * norm_g.astype(jnp.float32)
    y = on.reshape(o.shape[:2] + (HGRN_WIDTH,)) * jax.nn.silu(g.astype(jnp.float32))
    return y.astype(dtype)


def qk_norm(t, g):
    tf = t.astype(jnp.float32)
    return tf * lax.rsqrt(jnp.mean(tf * tf, axis=-1, keepdims=True) + EPS) * g.astype(jnp.float32)


def axial_rope(t):
    L, d = t.shape[1], t.shape[-1]
    pos = jnp.arange(L, dtype=jnp.int32)
    row = (pos // GRID_W).astype(jnp.float32)
    col = (pos % GRID_W).astype(jnp.float32)
    half = d // 2
    nf = half // 2
    inv = ROPE_THETA ** (-jnp.arange(nf, dtype=jnp.float32) / nf)

    def rot(u, p):
        ang = p[:, None] * inv[None, :]
        cos = jnp.cos(ang)[None, :, None, :]
        sin = jnp.sin(ang)[None, :, None, :]
        u1, u2 = u[..., :nf], u[..., nf:]
        return jnp.concatenate([u1 * cos - u2 * sin, u1 * sin + u2 * cos], axis=-1)

    return jnp.concatenate([rot(t[..., :half], row), rot(t[..., half:], col)], axis=-1)


def neighbourhood_attention(q, k, v, k_ctx, v_ctx, rel_bias):
    bsz, L, nh, d = q.shape
    rows = L // GRID_W
    kr = min(WIN_R, rows)
    scale = d ** -0.5
    r = jnp.arange(rows)
    w = jnp.arange(GRID_W)
    row_start = jnp.clip(r - WIN_R // 2, 0, rows - kr)
    row_idx = row_start[:, None] + jnp.arange(kr)[None, :]
    col_start = jnp.clip(w - WIN_C // 2, 0, GRID_W - WIN_C)
    col_in = (w[None, :] >= col_start[:, None]) & (w[None, :] < col_start[:, None] + WIN_C)
    qg = q.reshape(bsz, rows, GRID_W, nh, d)
    kg = k.reshape(bsz, rows, GRID_W, nh, d)[:, row_idx]
    vg = v.reshape(bsz, rows, GRID_W, nh, d)[:, row_idx]
    s_band = jnp.einsum('brchd,brjwhd->bhrcjw', qg, kg).astype(jnp.float32) * scale
    dr = row_idx - r[:, None]
    dc = jnp.clip(w[None, :] - w[:, None], -(WIN_C - 1), WIN_C - 1)
    bias = rel_bias[:, (dr + WIN_R - 1)[:, None, :, None], (dc + WIN_C - 1)[None, :, None, :]]
    s_band = jnp.where(col_in[:, None, :], s_band + bias.astype(jnp.float32)[None], -jnp.inf)
    s_ctx = jnp.einsum('brchd,bnhd->bhrcn', qg, k_ctx).astype(jnp.float32) * scale
    n_band = kr * GRID_W
    s = jnp.concatenate([s_band.reshape(bsz, nh, rows, GRID_W, n_band), s_ctx], axis=-1)
    p = jax.nn.softmax(s, axis=-1)
    p_band = p[..., :n_band].reshape(bsz, nh, rows, GRID_W, kr, GRID_W)
    p_ctx = p[..., n_band:]
    o = jnp.einsum('bhrcjw,brjwhd->brchd', p_band, vg) + jnp.einsum('bhrcn,bnhd->brchd', p_ctx, v_ctx)
    return o.reshape(bsz, L, nh * d)


def context_attention(q, k, v):
    bsz, n, nh, d = q.shape
    s = jnp.einsum('bnhd,bmhd->bhnm', q, k).astype(jnp.float32) * (d ** -0.5)
    p = jax.nn.softmax(s, axis=-1)
    return jnp.einsum('bhnm,bmhd->bnhd', p, v).reshape(bsz, n, nh * d)


def branch_merge(y_a, y_b, gate_a, gate_b, w_a, w_b, w_o):
    z = jax.nn.sigmoid(gate_a) * (y_a @ w_a) + jax.nn.sigmoid(gate_b) * (y_b @ w_b)
    return z @ w_o


def dwconv_centred(u, w, b):
    L = u.shape[1]
    up = jnp.pad(u, ((0, 0), (1, 1), (0, 0)))
    return up[:, 0:L] * w[0] + up[:, 1:L + 1] * w[1] + up[:, 2:L + 2] * w[2] + b


def conv_ffn(h, w1, w3, cw, cb, w2):
    u = dwconv_centred(h @ w1, cw, cb)
    return (jax.nn.silu(u) * (h @ w3)) @ w2


def _fwd_setup_inputs(seed: int = 0) -> dict:
    key = jax.random.key(seed)
    ks = jax.random.split(key, 24)

    def nrm(k, shape, scale):
        return jax.random.normal(k, shape, jnp.float32) * scale

    return {
        'x': nrm(ks[0], (BATCH, SEQ, D_MODEL), 1.0),
        'c': nrm(ks[1], (BATCH, D_MODEL), 1.0),
        'ctx': nrm(ks[2], (BATCH, CTX_LEN, D_MODEL), 1.0),
        'c_ctx': nrm(ks[3], (D_MODEL,), 1.0),
        'ada_w': nrm(ks[4], (DEPTH, D_MODEL, N_MOD * D_MODEL), D_MODEL ** -0.5),
        'ada_b': nrm(ks[5], (DEPTH, N_MOD * D_MODEL), 0.01),
        'norm1_g': 1.0 + nrm(ks[6], (DEPTH, D_MODEL), 0.02),
        'norm2_g': 1.0 + nrm(ks[7], (DEPTH, D_MODEL), 0.02),
        'w_in': nrm(ks[8], (DEPTH, D_MODEL, IN_COLS), D_MODEL ** -0.5),
        'hgrn_lb_logits': nrm(ks[9], (2, DEPTH + 1, HGRN_WIDTH), 0.5),
        'hgrn_norm_g': 1.0 + nrm(ks[10], (DEPTH, HGRN_HEAD_DIM), 0.02),
        'na_q_norm_g': 1.0 + nrm(ks[11], (DEPTH, NA_HEAD_DIM), 0.02),
        'na_k_norm_g': 1.0 + nrm(ks[12], (DEPTH, NA_HEAD_DIM), 0.02),
        'na_rel_bias': nrm(ks[13], (DEPTH, NA_HEADS, 2 * WIN_R - 1, 2 * WIN_C - 1), 0.1),
        'w_branch_a': nrm(ks[14], (DEPTH, HGRN_WIDTH, D_MODEL), HGRN_WIDTH ** -0.5),
        'w_branch_b': nrm(ks[15], (DEPTH, NA_WIDTH, D_MODEL), NA_WIDTH ** -0.5),
        'w_out': nrm(ks[16], (DEPTH, D_MODEL, D_MODEL), D_MODEL ** -0.5),
        'ffn_w1': nrm(ks[17], (DEPTH, D_MODEL, FFN_HIDDEN), D_MODEL ** -0.5),
        'ffn_w3': nrm(ks[18], (DEPTH, D_MODEL, FFN_HIDDEN), D_MODEL ** -0.5),
        'ffn_conv_w': nrm(ks[19], (DEPTH, CONV_W, FFN_HIDDEN), CONV_W ** -0.5),
        'ffn_conv_b': nrm(ks[20], (DEPTH, FFN_HIDDEN), 0.01),
        'ffn_w2': nrm(ks[21], (DEPTH, FFN_HIDDEN, D_MODEL), FFN_HIDDEN ** -0.5),
    }


def _fwd_reference(x, c, ctx, c_ctx, ada_w, ada_b, norm1_g, norm2_g, w_in, hgrn_lb_logits, hgrn_norm_g,
              na_q_norm_g, na_k_norm_g, na_rel_bias, w_branch_a, w_branch_b, w_out,
              ffn_w1, ffn_w3, ffn_conv_w, ffn_conv_b, ffn_w2):
    lower_bounds = jnp.cumsum(jax.nn.softmax(hgrn_lb_logits.astype(jnp.float32), axis=1), axis=1)
    xc = ctx
    for l in range(DEPTH):
        last_layer = l == DEPTH - 1
        mod_l = jax.nn.silu(c) @ ada_w[l] + ada_b[l]
        mod_c = jax.nn.silu(c_ctx) @ ada_w[l] + ada_b[l]
        sh1, sc1, g1, sh2, sc2, g2 = [m[:, None, :] for m in jnp.split(mod_l, N_MOD, axis=-1)]
        sh1c, sc1c, g1c, sh2c, sc2c, g2c = jnp.split(mod_c, N_MOD, axis=-1)

        h = modulate(rmsnorm(x, norm1_g[l]), sh1, sc1)
        hc = modulate(rmsnorm(xc, norm1_g[l]), sh1c, sc1c)
        qa, fwa, fba, ia, ga, qn, kn, vn, gta, gtb = split_columns(h @ w_in[l])
        qa_c, fwa_c, fba_c, ia_c, ga_c, qn_c, kn_c, vn_c, gta_c, gtb_c = split_columns(hc @ w_in[l])

        o_lf, o_cf = hgrn2_direction((qa_c, fwa_c, ia_c), (qa, fwa, ia), lower_bounds[0, l], False)
        o_lb, o_cb = hgrn2_direction((qa_c, fba_c, ia_c), (qa, fba, ia), lower_bounds[1, l], True)
        y_a = hgrn2_readout(o_lf + o_lb, ga, hgrn_norm_g[l], x.dtype)

        q_n = axial_rope(qk_norm(split_heads(qn, NA_HEADS), na_q_norm_g[l]))
        k_n = axial_rope(qk_norm(split_heads(kn, NA_HEADS), na_k_norm_g[l]))
        v_n = split_heads(vn, NA_HEADS)
        k_c = qk_norm(split_heads(kn_c, NA_HEADS), na_k_norm_g[l])
        v_c = split_heads(vn_c, NA_HEADS)
        y_b = neighbourhood_attention(q_n, k_n, v_n, k_c, v_c, na_rel_bias[l]).astype(x.dtype)

        x_mid = x + g1 * branch_merge(y_a, y_b, gta, gtb, w_branch_a[l], w_branch_b[l], w_out[l])

        h2 = modulate(rmsnorm(x_mid, norm2_g[l]), sh2, sc2)
        x_new = x_mid + g2 * conv_ffn(h2, ffn_w1[l], ffn_w3[l], ffn_conv_w[l], ffn_conv_b[l], ffn_w2[l])

        if not last_layer:
            y_a_c = hgrn2_readout(o_cf + o_cb, ga_c, hgrn_norm_g[l], x.dtype)
            q_c = qk_norm(split_heads(qn_c, NA_HEADS), na_q_norm_g[l])
            y_b_c = context_attention(q_c, k_c, v_c).astype(x.dtype)
            xc_mid = xc + g1c * branch_merge(y_a_c, y_b_c, gta_c, gtb_c, w_branch_a[l], w_branch_b[l], w_out[l])
            h2c = modulate(rmsnorm(xc_mid, norm2_g[l]), sh2c, sc2c)
            xc = xc_mid + g2c * conv_ffn(h2c, ffn_w1[l], ffn_w3[l], ffn_conv_w[l], ffn_conv_b[l], ffn_w2[l])
        x = x_new
    return x


import jax as _jax
import jax.numpy as _jnp

TWIN_FORMAT = 'train_step'
FWD_PARAMS = ['x', 'c', 'ctx', 'c_ctx', 'ada_w', 'ada_b', 'norm1_g', 'norm2_g', 'w_in', 'hgrn_lb_logits', 'hgrn_norm_g', 'na_q_norm_g', 'na_k_norm_g', 'na_rel_bias', 'w_branch_a', 'w_branch_b', 'w_out', 'ffn_w1', 'ffn_w3', 'ffn_conv_w', 'ffn_conv_b', 'ffn_w2']
TWIN_WEIGHTS = ['c_ctx', 'ada_w', 'ada_b', 'norm1_g', 'norm2_g', 'w_in', 'hgrn_lb_logits', 'hgrn_norm_g', 'na_q_norm_g', 'na_k_norm_g', 'na_rel_bias', 'w_branch_a', 'w_branch_b', 'w_out', 'ffn_w1', 'ffn_w3', 'ffn_conv_w', 'ffn_conv_b', 'ffn_w2']
TWIN_DIFF_INPUT = 'x'
TWIN_INPUTS = ['x', 'c', 'ctx', 'c_ctx', 'ada_w', 'ada_b', 'norm1_g', 'norm2_g', 'w_in', 'hgrn_lb_logits', 'hgrn_norm_g', 'na_q_norm_g', 'na_k_norm_g', 'na_rel_bias', 'w_branch_a', 'w_branch_b', 'w_out', 'ffn_w1', 'ffn_w3', 'ffn_conv_w', 'ffn_conv_b', 'ffn_w2', 'loss_target', 'm_c_ctx', 'm_ada_w', 'm_ada_b', 'm_norm1_g', 'm_norm2_g', 'm_w_in', 'm_hgrn_lb_logits', 'm_hgrn_norm_g', 'm_na_q_norm_g', 'm_na_k_norm_g', 'm_na_rel_bias', 'm_w_branch_a', 'm_w_branch_b', 'm_w_out', 'm_ffn_w1', 'm_ffn_w3', 'm_ffn_conv_w', 'm_ffn_conv_b', 'm_ffn_w2', 'v_c_ctx', 'v_ada_w', 'v_ada_b', 'v_norm1_g', 'v_norm2_g', 'v_w_in', 'v_hgrn_lb_logits', 'v_hgrn_norm_g', 'v_na_q_norm_g', 'v_na_k_norm_g', 'v_na_rel_bias', 'v_w_branch_a', 'v_w_branch_b', 'v_w_out', 'v_ffn_w1', 'v_ffn_w3', 'v_ffn_conv_w', 'v_ffn_conv_b', 'v_ffn_w2']
TWIN_OUTPUTS = ['loss', 'grad_x', 'grad_c_ctx', 'grad_ada_w', 'grad_ada_b', 'grad_norm1_g', 'grad_norm2_g', 'grad_w_in', 'grad_hgrn_lb_logits', 'grad_hgrn_norm_g', 'grad_na_q_norm_g', 'grad_na_k_norm_g', 'grad_na_rel_bias', 'grad_w_branch_a', 'grad_w_branch_b', 'grad_w_out', 'grad_ffn_w1', 'grad_ffn_w3', 'grad_ffn_conv_w', 'grad_ffn_conv_b', 'grad_ffn_w2', 'delta_c_ctx', 'delta_ada_w', 'delta_ada_b', 'delta_norm1_g', 'delta_norm2_g', 'delta_w_in', 'delta_hgrn_lb_logits', 'delta_hgrn_norm_g', 'delta_na_q_norm_g', 'delta_na_k_norm_g', 'delta_na_rel_bias', 'delta_w_branch_a', 'delta_w_branch_b', 'delta_w_out', 'delta_ffn_w1', 'delta_ffn_w3', 'delta_ffn_conv_w', 'delta_ffn_conv_b', 'delta_ffn_w2', 'new_m_c_ctx', 'new_m_ada_w', 'new_m_ada_b', 'new_m_norm1_g', 'new_m_norm2_g', 'new_m_w_in', 'new_m_hgrn_lb_logits', 'new_m_hgrn_norm_g', 'new_m_na_q_norm_g', 'new_m_na_k_norm_g', 'new_m_na_rel_bias', 'new_m_w_branch_a', 'new_m_w_branch_b', 'new_m_w_out', 'new_m_ffn_w1', 'new_m_ffn_w3', 'new_m_ffn_conv_w', 'new_m_ffn_conv_b', 'new_m_ffn_w2', 'new_v_c_ctx', 'new_v_ada_w', 'new_v_ada_b', 'new_v_norm1_g', 'new_v_norm2_g', 'new_v_w_in', 'new_v_hgrn_lb_logits', 'new_v_hgrn_norm_g', 'new_v_na_q_norm_g', 'new_v_na_k_norm_g', 'new_v_na_rel_bias', 'new_v_w_branch_a', 'new_v_w_branch_b', 'new_v_w_out', 'new_v_ffn_w1', 'new_v_ffn_w3', 'new_v_ffn_conv_w', 'new_v_ffn_conv_b', 'new_v_ffn_w2']
TWIN_LEAF_KINDS = {'loss': 'loss', 'grad_x': 'grad_x', 'grad_c_ctx': 'grad_w', 'grad_ada_w': 'grad_w', 'grad_ada_b': 'grad_w', 'grad_norm1_g': 'grad_w', 'grad_norm2_g': 'grad_w', 'grad_w_in': 'grad_w', 'grad_hgrn_lb_logits': 'grad_w', 'grad_hgrn_norm_g': 'grad_w', 'grad_na_q_norm_g': 'grad_w', 'grad_na_k_norm_g': 'grad_w', 'grad_na_rel_bias': 'grad_w', 'grad_w_branch_a': 'grad_w', 'grad_w_branch_b': 'grad_w', 'grad_w_out': 'grad_w', 'grad_ffn_w1': 'grad_w', 'grad_ffn_w3': 'grad_w', 'grad_ffn_conv_w': 'grad_w', 'grad_ffn_conv_b': 'grad_w', 'grad_ffn_w2': 'grad_w', 'delta_c_ctx': 'delta_w', 'delta_ada_w': 'delta_w', 'delta_ada_b': 'delta_w', 'delta_norm1_g': 'delta_w', 'delta_norm2_g': 'delta_w', 'delta_w_in': 'delta_w', 'delta_hgrn_lb_logits': 'delta_w', 'delta_hgrn_norm_g': 'delta_w', 'delta_na_q_norm_g': 'delta_w', 'delta_na_k_norm_g': 'delta_w', 'delta_na_rel_bias': 'delta_w', 'delta_w_branch_a': 'delta_w', 'delta_w_branch_b': 'delta_w', 'delta_w_out': 'delta_w', 'delta_ffn_w1': 'delta_w', 'delta_ffn_w3': 'delta_w', 'delta_ffn_conv_w': 'delta_w', 'delta_ffn_conv_b': 'delta_w', 'delta_ffn_w2': 'delta_w', 'new_m_c_ctx': 'new_m', 'new_m_ada_w': 'new_m', 'new_m_ada_b': 'new_m', 'new_m_norm1_g': 'new_m', 'new_m_norm2_g': 'new_m', 'new_m_w_in': 'new_m', 'new_m_hgrn_lb_logits': 'new_m', 'new_m_hgrn_norm_g': 'new_m', 'new_m_na_q_norm_g': 'new_m', 'new_m_na_k_norm_g': 'new_m', 'new_m_na_rel_bias': 'new_m', 'new_m_w_branch_a': 'new_m', 'new_m_w_branch_b': 'new_m', 'new_m_w_out': 'new_m', 'new_m_ffn_w1': 'new_m', 'new_m_ffn_w3': 'new_m', 'new_m_ffn_conv_w': 'new_m', 'new_m_ffn_conv_b': 'new_m', 'new_m_ffn_w2': 'new_m', 'new_v_c_ctx': 'new_v', 'new_v_ada_w': 'new_v', 'new_v_ada_b': 'new_v', 'new_v_norm1_g': 'new_v', 'new_v_norm2_g': 'new_v', 'new_v_w_in': 'new_v', 'new_v_hgrn_lb_logits': 'new_v', 'new_v_hgrn_norm_g': 'new_v', 'new_v_na_q_norm_g': 'new_v', 'new_v_na_k_norm_g': 'new_v', 'new_v_na_rel_bias': 'new_v', 'new_v_w_branch_a': 'new_v', 'new_v_w_branch_b': 'new_v', 'new_v_w_out': 'new_v', 'new_v_ffn_w1': 'new_v', 'new_v_ffn_w3': 'new_v', 'new_v_ffn_conv_w': 'new_v', 'new_v_ffn_conv_b': 'new_v', 'new_v_ffn_w2': 'new_v'}


def _forward(args):
    return _fwd_reference(*[args[k] for k in FWD_PARAMS])


def _output_shape():
    out = _jax.eval_shape(lambda: _forward(_fwd_setup_inputs(0)))
    return out.shape, out.dtype

N_MICROBATCH = 1
ADAM_LR = 0.001
ADAM_B1 = 0.9
ADAM_B2 = 0.999
ADAM_EPS = 1e-08
ADAM_WD = 0.01
ADAM_STEP = 10
PER_EXAMPLE_BATCH_AXIS = {'x': 0, 'c': 0, 'ctx': 0, 'loss_target': 0}
SHARED_INPUTS = []
_WEIGHT_DTYPES = {'c_ctx': _jnp.float32, 'ada_w': _jnp.float32, 'ada_b': _jnp.float32, 'norm1_g': _jnp.float32, 'norm2_g': _jnp.float32, 'w_in': _jnp.float32, 'hgrn_lb_logits': _jnp.float32, 'hgrn_norm_g': _jnp.float32, 'na_q_norm_g': _jnp.float32, 'na_k_norm_g': _jnp.float32, 'na_rel_bias': _jnp.float32, 'w_branch_a': _jnp.float32, 'w_branch_b': _jnp.float32, 'w_out': _jnp.float32, 'ffn_w1': _jnp.float32, 'ffn_w3': _jnp.float32, 'ffn_conv_w': _jnp.float32, 'ffn_conv_b': _jnp.float32, 'ffn_w2': _jnp.float32}
MOMENT_SCALE = {'c_ctx': 4.006410e-01, 'ada_w': 1.214054e+00, 'ada_b': 2.709620e+00, 'norm1_g': 6.756956e-01, 'norm2_g': 6.685937e+00, 'w_in': 2.167454e-01, 'hgrn_lb_logits': 2.125671e-02, 'hgrn_norm_g': 9.565575e+00, 'na_q_norm_g': 3.157639e-01, 'na_k_norm_g': 3.122017e-01, 'na_rel_bias': 9.412243e-03, 'w_branch_a': 1.186112e-01, 'w_branch_b': 3.941871e-01, 'w_out': 3.748116e-01, 'ffn_w1': 5.860206e-01, 'ffn_w3': 3.332970e-01, 'ffn_conv_w': 1.102663e+00, 'ffn_conv_b': 9.000466e-01, 'ffn_w2': 2.633744e-01}


def _to_microbatches(a, axis):
    t = _jnp.moveaxis(a, axis, 0)
    t = t.reshape((N_MICROBATCH, t.shape[0] // N_MICROBATCH) + t.shape[1:])
    return _jnp.moveaxis(t, 1, axis + 1)


def setup_inputs(seed: int = 0) -> dict:
    inp = _fwd_setup_inputs(seed)
    key = _jax.random.fold_in(_jax.random.key(seed), 7919)
    shape, _ = _output_shape()
    out = dict(inp)
    out["loss_target"] = _jax.random.normal(_jax.random.fold_in(key, 0), shape, _jnp.float32)
    for i, name in enumerate(TWIN_WEIGHTS):
        w = inp[name].astype(_jnp.float32)
        if MOMENT_SCALE is None:
            s = _jnp.sqrt(_jnp.mean(_jnp.square(w)) + 1e-30)
        else:
            s = MOMENT_SCALE[name]
        km, kv = _jax.random.split(_jax.random.fold_in(key, i + 1))
        out[name] = w
        out["m_" + name] = s * _jax.random.normal(km, w.shape, _jnp.float32)
        out["v_" + name] = (s * s) * _jax.random.uniform(kv, w.shape, _jnp.float32, 0.5, 1.5)
    if N_MICROBATCH > 1:
        for name, axis in PER_EXAMPLE_BATCH_AXIS.items():
            out[name] = _to_microbatches(out[name], axis)
    return {'x': out['x'], 'c': out['c'], 'ctx': out['ctx'], 'c_ctx': out['c_ctx'], 'ada_w': out['ada_w'], 'ada_b': out['ada_b'], 'norm1_g': out['norm1_g'], 'norm2_g': out['norm2_g'], 'w_in': out['w_in'], 'hgrn_lb_logits': out['hgrn_lb_logits'], 'hgrn_norm_g': out['hgrn_norm_g'], 'na_q_norm_g': out['na_q_norm_g'], 'na_k_norm_g': out['na_k_norm_g'], 'na_rel_bias': out['na_rel_bias'], 'w_branch_a': out['w_branch_a'], 'w_branch_b': out['w_branch_b'], 'w_out': out['w_out'], 'ffn_w1': out['ffn_w1'], 'ffn_w3': out['ffn_w3'], 'ffn_conv_w': out['ffn_conv_w'], 'ffn_conv_b': out['ffn_conv_b'], 'ffn_w2': out['ffn_w2'], 'loss_target': out['loss_target'], 'm_c_ctx': out['m_c_ctx'], 'm_ada_w': out['m_ada_w'], 'm_ada_b': out['m_ada_b'], 'm_norm1_g': out['m_norm1_g'], 'm_norm2_g': out['m_norm2_g'], 'm_w_in': out['m_w_in'], 'm_hgrn_lb_logits': out['m_hgrn_lb_logits'], 'm_hgrn_norm_g': out['m_hgrn_norm_g'], 'm_na_q_norm_g': out['m_na_q_norm_g'], 'm_na_k_norm_g': out['m_na_k_norm_g'], 'm_na_rel_bias': out['m_na_rel_bias'], 'm_w_branch_a': out['m_w_branch_a'], 'm_w_branch_b': out['m_w_branch_b'], 'm_w_out': out['m_w_out'], 'm_ffn_w1': out['m_ffn_w1'], 'm_ffn_w3': out['m_ffn_w3'], 'm_ffn_conv_w': out['m_ffn_conv_w'], 'm_ffn_conv_b': out['m_ffn_conv_b'], 'm_ffn_w2': out['m_ffn_w2'], 'v_c_ctx': out['v_c_ctx'], 'v_ada_w': out['v_ada_w'], 'v_ada_b': out['v_ada_b'], 'v_norm1_g': out['v_norm1_g'], 'v_norm2_g': out['v_norm2_g'], 'v_w_in': out['v_w_in'], 'v_hgrn_lb_logits': out['v_hgrn_lb_logits'], 'v_hgrn_norm_g': out['v_hgrn_norm_g'], 'v_na_q_norm_g': out['v_na_q_norm_g'], 'v_na_k_norm_g': out['v_na_k_norm_g'], 'v_na_rel_bias': out['v_na_rel_bias'], 'v_w_branch_a': out['v_w_branch_a'], 'v_w_branch_b': out['v_w_branch_b'], 'v_w_out': out['v_w_out'], 'v_ffn_w1': out['v_ffn_w1'], 'v_ffn_w3': out['v_ffn_w3'], 'v_ffn_conv_w': out['v_ffn_conv_w'], 'v_ffn_conv_b': out['v_ffn_conv_b'], 'v_ffn_w2': out['v_ffn_w2']}


def _loss(weights, diff, rest, loss_target):
    with _jax.named_scope("forward"):
        args = {**rest, TWIN_DIFF_INPUT: diff, **{k: w.astype(_WEIGHT_DTYPES[k]) for k, w in weights.items()}}
        y = _forward(args)
    with _jax.named_scope("loss_head"):
        err = _jnp.square(y.astype(_jnp.float32) - loss_target)
        return 0.5 * _jnp.sum(_jnp.mean(err, axis=-1)) if err.ndim else 0.5 * err


def _adamw(w, g, m, v):
    m = ADAM_B1 * m + (1.0 - ADAM_B1) * g
    v = ADAM_B2 * v + (1.0 - ADAM_B2) * _jnp.square(g)
    m_hat = m / (1.0 - ADAM_B1 ** ADAM_STEP)
    v_hat = v / (1.0 - ADAM_B2 ** ADAM_STEP)
    delta = -ADAM_LR * (m_hat / (_jnp.sqrt(v_hat) + ADAM_EPS) + ADAM_WD * w)
    return delta, m, v


def reference(x, c, ctx, c_ctx, ada_w, ada_b, norm1_g, norm2_g, w_in, hgrn_lb_logits, hgrn_norm_g, na_q_norm_g, na_k_norm_g, na_rel_bias, w_branch_a, w_branch_b, w_out, ffn_w1, ffn_w3, ffn_conv_w, ffn_conv_b, ffn_w2, loss_target, m_c_ctx, m_ada_w, m_ada_b, m_norm1_g, m_norm2_g, m_w_in, m_hgrn_lb_logits, m_hgrn_norm_g, m_na_q_norm_g, m_na_k_norm_g, m_na_rel_bias, m_w_branch_a, m_w_branch_b, m_w_out, m_ffn_w1, m_ffn_w3, m_ffn_conv_w, m_ffn_conv_b, m_ffn_w2, v_c_ctx, v_ada_w, v_ada_b, v_norm1_g, v_norm2_g, v_w_in, v_hgrn_lb_logits, v_hgrn_norm_g, v_na_q_norm_g, v_na_k_norm_g, v_na_rel_bias, v_w_branch_a, v_w_branch_b, v_w_out, v_ffn_w1, v_ffn_w3, v_ffn_conv_w, v_ffn_conv_b, v_ffn_w2):
    given = dict(x=x, c=c, ctx=ctx, c_ctx=c_ctx, ada_w=ada_w, ada_b=ada_b, norm1_g=norm1_g, norm2_g=norm2_g, w_in=w_in, hgrn_lb_logits=hgrn_lb_logits, hgrn_norm_g=hgrn_norm_g, na_q_norm_g=na_q_norm_g, na_k_norm_g=na_k_norm_g, na_rel_bias=na_rel_bias, w_branch_a=w_branch_a, w_branch_b=w_branch_b, w_out=w_out, ffn_w1=ffn_w1, ffn_w3=ffn_w3, ffn_conv_w=ffn_conv_w, ffn_conv_b=ffn_conv_b, ffn_w2=ffn_w2, loss_target=loss_target, m_c_ctx=m_c_ctx, m_ada_w=m_ada_w, m_ada_b=m_ada_b, m_norm1_g=m_norm1_g, m_norm2_g=m_norm2_g, m_w_in=m_w_in, m_hgrn_lb_logits=m_hgrn_lb_logits, m_hgrn_norm_g=m_hgrn_norm_g, m_na_q_norm_g=m_na_q_norm_g, m_na_k_norm_g=m_na_k_norm_g, m_na_rel_bias=m_na_rel_bias, m_w_branch_a=m_w_branch_a, m_w_branch_b=m_w_branch_b, m_w_out=m_w_out, m_ffn_w1=m_ffn_w1, m_ffn_w3=m_ffn_w3, m_ffn_conv_w=m_ffn_conv_w, m_ffn_conv_b=m_ffn_conv_b, m_ffn_w2=m_ffn_w2, v_c_ctx=v_c_ctx, v_ada_w=v_ada_w, v_ada_b=v_ada_b, v_norm1_g=v_norm1_g, v_norm2_g=v_norm2_g, v_w_in=v_w_in, v_hgrn_lb_logits=v_hgrn_lb_logits, v_hgrn_norm_g=v_hgrn_norm_g, v_na_q_norm_g=v_na_q_norm_g, v_na_k_norm_g=v_na_k_norm_g, v_na_rel_bias=v_na_rel_bias, v_w_branch_a=v_w_branch_a, v_w_branch_b=v_w_branch_b, v_w_out=v_w_out, v_ffn_w1=v_ffn_w1, v_ffn_w3=v_ffn_w3, v_ffn_conv_w=v_ffn_conv_w, v_ffn_conv_b=v_ffn_conv_b, v_ffn_w2=v_ffn_w2)
    weights = {n: given[n] for n in TWIN_WEIGHTS}
    shared = {n: given[n] for n in SHARED_INPUTS}
    per_example = {n: given[n] for n in ['x', 'c', 'ctx']}
    grad_fn = _jax.value_and_grad(_loss, argnums=(0, 1))

    def one_microbatch(ex, loss_target):
        ex = dict(ex)
        diff = ex.pop(TWIN_DIFF_INPUT)
        return grad_fn(weights, diff, {**shared, **ex}, loss_target)

    if N_MICROBATCH == 1:
        loss, (grad_w, grad_x) = one_microbatch(per_example, given["loss_target"])
    else:
        def body(carry, xs):
            loss_sum, grad_sum = carry
            l_k, (gw_k, gx_k) = one_microbatch(xs[0], xs[1])
            with _jax.named_scope("update"):
                return (loss_sum + l_k, _jax.tree.map(_jnp.add, grad_sum, gw_k)), gx_k

        init = (_jnp.zeros((), _jnp.float32), _jax.tree.map(_jnp.zeros_like, weights))
        (loss, grad_w), grad_x = _jax.lax.scan(body, init, (per_example, given["loss_target"]))
    with _jax.named_scope("update"):
        delta_w, new_m, new_v = {}, {}, {}
        for n in TWIN_WEIGHTS:
            delta_w[n], new_m[n], new_v[n] = _adamw(weights[n], grad_w[n], given["m_" + n], given["v_" + n])
    return (loss, grad_x, *[grad_w[n] for n in TWIN_WEIGHTS], *[delta_w[n] for n in TWIN_WEIGHTS],
            *[new_m[n] for n in TWIN_WEIGHTS], *[new_v[n] for n in TWIN_WEIGHTS])
```

```python
import functools

import numpy as np
import jax
import jax.numpy as jnp
from jax import lax
from jax.experimental import pallas as pl
from jax.experimental.pallas import tpu as pltpu

F32 = jnp.float32
BF16 = jnp.bfloat16
MESH = pl.DeviceIdType.MESH

HEAD = 128
GRID_W = 64
WIN_R = 8
WIN_C = 16
ROPE_THETA = 10000.0
EPS = 1e-6
N_MOD = 6
CHUNK = 16
ADAM_LR = 0.001
ADAM_B1 = 0.9
ADAM_B2 = 0.999
ADAM_EPS = 1e-08
ADAM_WD = 0.01
ADAM_STEP = 10
NEG = -1e30
VMEM_LIMIT = 56 * 1024 * 1024
N_DEV = 8
N_CHIP = 4
HI = lax.Precision.HIGHEST


def _pick(n, cands):
    for c in cands:
        if n % c == 0:
            return c
    return n


def _row_tile(rows, cols, target_bytes=1 << 20):
    want = max(16, target_bytes // (4 * cols))
    for t in (512, 256, 128, 64, 32, 16, 8):
        if t <= want and rows % t == 0:
            return t
    return rows


def _params(sem=None):
    return pltpu.CompilerParams(dimension_semantics=sem, vmem_limit_bytes=VMEM_LIMIT)


def _dot(a, b):
    return jnp.dot(a, b, preferred_element_type=F32)


def _dot_nt(a, b):
    return lax.dot_general(a, b, (((1,), (1,)), ((), ())), preferred_element_type=F32)


def _dot_tn(a, b):
    return lax.dot_general(a, b, (((0,), (0,)), ((), ())), preferred_element_type=F32)


def _sigmoid(x):
    return 1.0 / (1.0 + jnp.exp(-x))


def _col_tile(n):
    return n if n <= 1536 else _pick(n, (1024, 768, 512, 384, 256, 128))


def _mm_nn(x, w3, out_dtype, name):
    M, K = x.shape
    S, _, n = w3.shape
    tm = _pick(M, (768, 512, 256, 128, 64))
    tn = _col_tile(n)
    nb = n // tn

    def body(x_ref, w_ref, o_ref):
        o_ref[...] = _dot(x_ref[...].astype(BF16), w_ref[0]).astype(o_ref.dtype)

    return pl.pallas_call(
        body, name=name, grid=(M // tm, S * nb),
        in_specs=[pl.BlockSpec((tm, K), lambda i, j: (i, 0)),
                  pl.BlockSpec((1, K, tn), lambda i, j: (j // nb, 0, j % nb))],
        out_specs=pl.BlockSpec((tm, tn), lambda i, j: (i, j)),
        out_shape=jax.ShapeDtypeStruct((M, S * n), out_dtype),
        compiler_params=_params(("parallel", "parallel")),
    )(x, w3)


def _mm_nt(dy, w3, out_dtype, name):
    M = dy.shape[0]
    S, K, n = w3.shape
    tm = _pick(M, (768, 512, 256, 128, 64))
    tk = _pick(K, (512, 256, 128))
    tc = _col_tile(n)
    nb = n // tc
    nsteps = S * nb

    def body(dy_ref, w_ref, o_ref, acc_ref):
        s = pl.program_id(2)

        @pl.when(s == 0)
        def _():
            acc_ref[...] = jnp.zeros_like(acc_ref)

        acc_ref[...] += _dot_nt(dy_ref[...].astype(BF16), w_ref[0])

        @pl.when(s == nsteps - 1)
        def _():
            o_ref[...] = acc_ref[...].astype(o_ref.dtype)

    return pl.pallas_call(
        body, name=name, grid=(M // tm, K // tk, nsteps),
        in_specs=[pl.BlockSpec((tm, tc), lambda i, k, s: (i, s)),
                  pl.BlockSpec((1, tk, tc), lambda i, k, s: (s // nb, k, s % nb))],
        out_specs=pl.BlockSpec((tm, tk), lambda i, k, s: (i, k)),
        out_shape=jax.ShapeDtypeStruct((M, K), out_dtype),
        scratch_shapes=[pltpu.VMEM((tm, tk), F32)],
        compiler_params=_params(("parallel", "parallel", "arbitrary")),
    )(dy, w3)


def _mm_tn(x, dy, S, name):
    M, K = x.shape
    n = dy.shape[1] // S
    tk = _pick(K, (512, 256, 128))
    tn = _col_tile(n)
    nb = n // tn

    def body(x_ref, dy_ref, o_ref):
        o_ref[0] = _dot_tn(x_ref[...].astype(BF16), dy_ref[...].astype(BF16))

    return pl.pallas_call(
        body, name=name, grid=(S * nb, K // tk),
        in_specs=[pl.BlockSpec((M, tk), lambda j, k: (0, k)),
                  pl.BlockSpec((M, tn), lambda j, k: (0, j))],
        out_specs=pl.BlockSpec((1, tk, tn), lambda j, k: (j // nb, k, j % nb)),
        out_shape=jax.ShapeDtypeStruct((S, K, n), F32),
        compiler_params=_params(("parallel", "parallel")),
    )(x, dy)


def _cast_bf16(w, name):
    R, C = w.shape
    tr = _row_tile(R, C, 2 << 20)

    def body(w_ref, o_ref):
        o_ref[...] = w_ref[...].astype(BF16)

    return pl.pallas_call(
        body, name=name, grid=(R // tr,),
        in_specs=[pl.BlockSpec((tr, C), lambda i: (i, 0))],
        out_specs=pl.BlockSpec((tr, C), lambda i: (i, 0)),
        out_shape=jax.ShapeDtypeStruct((R, C), BF16),
        compiler_params=_params(("parallel",)),
    )(w)


def _pos():
    return lax.axis_index("x"), lax.axis_index("y"), lax.axis_index("c")


def _other_chips(x, y):
    return [(x, 1 - y), (1 - x, y), (1 - x, 1 - y)]


def _allgather8(v, name):
    R, C = v.shape

    def body(x_ref, out_ref, send_sems, recv_sems, local_sem):
        x, y, c = _pos()
        me, sibling = (x, y, c), (x, y, 1 - c)
        chips = _other_chips(x, y)

        def slot(px, py, pc):
            return out_ref.at[4 * px + 2 * py + pc]

        def copy(k, block, to, src=None):
            return pltpu.make_async_remote_copy(
                src_ref=slot(*block) if src is None else src, dst_ref=slot(*block),
                send_sem=send_sems.at[k], recv_sem=recv_sems.at[k], device_id=to, device_id_type=MESH)

        mine = pltpu.make_async_copy(x_ref, slot(*me), local_sem)
        mine.start()
        first = [copy(0, me, sibling, src=x_ref)]
        first += [copy(1 + j, me, (*chip, c), src=x_ref) for j, chip in enumerate(chips)]
        for cp in first:
            cp.start()
        passed = [copy(4 + j, (*chip, c), sibling) for j, chip in enumerate(chips)]
        for j, chip in enumerate(chips):
            copy(1 + j, (*chip, c), me).wait_recv()
            passed[j].start()
        copy(0, sibling, me).wait_recv()
        for j, chip in enumerate(chips):
            copy(4 + j, (*chip, 1 - c), me).wait_recv()
        for cp in first + passed:
            cp.wait_send()
        mine.wait()

    return pl.pallas_call(
        body, name=name,
        out_shape=jax.ShapeDtypeStruct((N_DEV, R, C), v.dtype),
        in_specs=[pl.BlockSpec(memory_space=pltpu.VMEM)],
        out_specs=pl.BlockSpec(memory_space=pltpu.VMEM),
        scratch_shapes=[pltpu.SemaphoreType.DMA((7,)), pltpu.SemaphoreType.DMA((7,)), pltpu.SemaphoreType.DMA],
        compiler_params=pltpu.CompilerParams(vmem_limit_bytes=VMEM_LIMIT),
    )(v)


def _gather_shards(shards, name):
    n = len(shards)

    def body(*refs):
        ins, outs = refs[:n], refs[n:2 * n]
        send_sems, recv_sems, local_sems = refs[2 * n:]
        x, y, c = _pos()
        sibling = (x, y, 1 - c)
        chips = _other_chips(x, y)
        j = 2 * x + y

        def half(t, hc):
            h = ins[t].shape[0] // 2
            return pl.ds(hc * h, h)

        def copy(t, k, chip, hc, to):
            dst = outs[t].at[2 * chip[0] + chip[1], half(t, hc)]
            return pltpu.make_async_remote_copy(
                src_ref=dst, dst_ref=dst, send_sem=send_sems.at[6 * t + k], recv_sem=recv_sems.at[6 * t + k],
                device_id=to, device_id_type=MESH)

        def first(t, k, chip):
            return pltpu.make_async_remote_copy(
                src_ref=ins[t].at[half(t, c)], dst_ref=outs[t].at[j, half(t, c)],
                send_sem=send_sems.at[6 * t + k], recv_sem=recv_sems.at[6 * t + k],
                device_id=(*chip, c), device_id_type=MESH)

        sends = []
        locals_ = []
        for t in range(n):
            mine = pltpu.make_async_copy(ins[t], outs[t].at[j], local_sems.at[t])
            mine.start()
            locals_.append(mine)
            for k, chip in enumerate(chips):
                cp = first(t, k, chip)
                cp.start()
                sends.append(cp)
        for t in range(n):
            for k, chip in enumerate(chips):
                copy(t, k, chip, c, (x, y, c)).wait_recv()
                fwd = copy(t, 3 + k, chip, c, sibling)
                fwd.start()
                sends.append(fwd)
        for t in range(n):
            for k, chip in enumerate(chips):
                copy(t, 3 + k, chip, 1 - c, (x, y, c)).wait_recv()
        for cp in sends:
            cp.wait_send()
        for mine in locals_:
            mine.wait()

    any_spec = pl.BlockSpec(memory_space=pl.ANY)
    return pl.pallas_call(
        body, name=name,
        out_shape=[jax.ShapeDtypeStruct((N_CHIP,) + s.shape, s.dtype) for s in shards],
        in_specs=[any_spec] * n, out_specs=[any_spec] * n,
        scratch_shapes=[pltpu.SemaphoreType.DMA((6 * n,)), pltpu.SemaphoreType.DMA((6 * n,)),
                        pltpu.SemaphoreType.DMA((n,))],
    )(*shards)


def _pair_swap_halves(gs, name):
    n = len(gs)

    def body(*refs):
        ins, outs = refs[:n], refs[n:2 * n]
        send_sems, recv_sems = refs[2 * n:]
        x, y, c = _pos()
        cps = []
        for t in range(n):
            h = ins[t].shape[1] // 2
            cp = pltpu.make_async_remote_copy(
                src_ref=ins[t].at[:, pl.ds((1 - c) * h, h)], dst_ref=outs[t],
                send_sem=send_sems.at[t], recv_sem=recv_sems.at[t], device_id=(x, y, 1 - c), device_id_type=MESH)
            cp.start()
            cps.append(cp)
        for cp in cps:
            cp.wait()

    any_spec = pl.BlockSpec(memory_space=pl.ANY)
    return pl.pallas_call(
        body, name=name,
        out_shape=[jax.ShapeDtypeStruct((g.shape[0], g.shape[1] // 2, g.shape[2]), g.dtype) for g in gs],
        in_specs=[any_spec] * n, out_specs=[any_spec] * n,
        scratch_shapes=[pltpu.SemaphoreType.DMA((n,)), pltpu.SemaphoreType.DMA((n,))],
    )(*gs)


def _chip_scatter(ps, name):
    n = len(ps)

    def body(*refs):
        ins, outs = refs[:n], refs[n:2 * n]
        send_sems, recv_sems = refs[2 * n:]
        x, y, c = _pos()
        cps = []
        for t in range(n):
            for k, chip in enumerate(_other_chips(x, y)):
                cp = pltpu.make_async_remote_copy(
                    src_ref=ins[t].at[2 * chip[0] + chip[1]], dst_ref=outs[t].at[k],
                    send_sem=send_sems.at[3 * t + k], recv_sem=recv_sems.at[3 * t + k],
                    device_id=(*chip, c), device_id_type=MESH)
                cp.start()
                cps.append(cp)
        for cp in cps:
            cp.wait()

    any_spec = pl.BlockSpec(memory_space=pl.ANY)
    return pl.pallas_call(
        body, name=name,
        out_shape=[jax.ShapeDtypeStruct((3,) + p.shape[1:], p.dtype) for p in ps],
        in_specs=[any_spec] * n, out_specs=[any_spec] * n,
        scratch_shapes=[pltpu.SemaphoreType.DMA((3 * n,)), pltpu.SemaphoreType.DMA((3 * n,))],
    )(*ps)


def _pair_join_halves(hs, name):
    n = len(hs)

    def body(*refs):
        ins, outs = refs[:n], refs[n:2 * n]
        send_sems, recv_sems, local_sems = refs[2 * n:]
        x, y, c = _pos()
        cps, locals_ = [], []
        for t in range(n):
            h = ins[t].shape[0]
            mine = pltpu.make_async_copy(ins[t], outs[t].at[pl.ds(c * h, h)], local_sems.at[t])
            mine.start()
            locals_.append(mine)
            cp = pltpu.make_async_remote_copy(
                src_ref=ins[t], dst_ref=outs[t].at[pl.ds(c * h, h)],
                send_sem=send_sems.at[t], recv_sem=recv_sems.at[t], device_id=(x, y, 1 - c), device_id_type=MESH)
            cp.start()
            cps.append(cp)
        for t in range(n):
            h = ins[t].shape[0]
            pltpu.make_async_remote_copy(
                src_ref=ins[t], dst_ref=outs[t].at[pl.ds((1 - c) * h, h)],
                send_sem=send_sems.at[t], recv_sem=recv_sems.at[t], device_id=(x, y, 1 - c),
                device_id_type=MESH).wait_recv()
        for cp in cps:
            cp.wait_send()
        for mine in locals_:
            mine.wait()

    any_spec = pl.BlockSpec(memory_space=pl.ANY)
    return pl.pallas_call(
        body, name=name,
        out_shape=[jax.ShapeDtypeStruct((2 * h.shape[0], h.shape[1]), h.dtype) for h in hs],
        in_specs=[any_spec] * n, out_specs=[any_spec] * n,
        scratch_shapes=[pltpu.SemaphoreType.DMA((n,)), pltpu.SemaphoreType.DMA((n,)),
                        pltpu.SemaphoreType.DMA((n,))],
    )(*hs)


def _pair_add(g, r, name):
    S, R, C = g.shape
    h = R // 2
    tr = _row_tile(h, C)
    nb = h // tr

    def body(c_ref, g_ref, r_ref, o_ref):
        o_ref[...] = g_ref[...] + r_ref[...]

    return pl.pallas_call(
        body, name=name,
        grid_spec=pltpu.PrefetchScalarGridSpec(
            num_scalar_prefetch=1, grid=(S, nb),
            in_specs=[pl.BlockSpec((1, tr, C), lambda s, i, c_ref: (s, c_ref[0] * nb + i, 0)),
                      pl.BlockSpec((1, tr, C), lambda s, i, c_ref: (s, i, 0))],
            out_specs=pl.BlockSpec((1, tr, C), lambda s, i, c_ref: (s, i, 0))),
        out_shape=jax.ShapeDtypeStruct((S, h, C), F32),
        compiler_params=_params(("parallel", "parallel")),
    )(lax.axis_index("c").astype(jnp.int32).reshape(1), g, r)


def _chip_sum(p, rb, name):
    S, h, C = p.shape
    tr = _row_tile(h, C)
    j = 2 * lax.axis_index("x") + lax.axis_index("y")

    def body(j_ref, p_ref, r_ref, o_ref):
        o_ref[...] = ((p_ref[0] + r_ref[0]) + r_ref[1]) + r_ref[2]

    return pl.pallas_call(
        body, name=name,
        grid_spec=pltpu.PrefetchScalarGridSpec(
            num_scalar_prefetch=1, grid=(h // tr,),
            in_specs=[pl.BlockSpec((1, tr, C), lambda i, j_ref: (j_ref[0], i, 0)),
                      pl.BlockSpec((3, tr, C), lambda i, j_ref: (0, i, 0))],
            out_specs=pl.BlockSpec((tr, C), lambda i, j_ref: (i, 0))),
        out_shape=jax.ShapeDtypeStruct((h, C), F32),
        compiler_params=_params(("parallel",)),
    )(j.astype(jnp.int32).reshape(1), p, rb)


def _reduce_scatter(gs):
    recv = _pair_swap_halves(gs, "rs_pair_swap")
    ps = [_pair_add(g, r, f"rs_pair_add{t}") for t, (g, r) in enumerate(zip(gs, recv))]
    rbs = _chip_scatter(ps, "rs_chip_scatter")
    hs = [_chip_sum(p, rb, f"rs_chip_sum{t}") for t, (p, rb) in enumerate(zip(ps, rbs))]
    return _pair_join_halves(hs, "rs_pair_join")


def _sum8(g, name):
    _, R, C = g.shape

    def body(g_ref, o_ref):
        acc = g_ref[0]
        for d in range(1, N_DEV):
            acc = acc + g_ref[d]
        o_ref[...] = acc

    return pl.pallas_call(body, name=name, out_shape=jax.ShapeDtypeStruct((R, C), F32),
                          compiler_params=_params())(g)


def _ada_fwd(cs, w, b):
    D, n = w.shape
    tn = _pick(n, (512, 384, 256, 128))

    def body(c_ref, w_ref, b_ref, o_ref):
        cv = c_ref[...]
        a = (cv * _sigmoid(cv)).astype(BF16)
        o_ref[...] = _dot(a, w_ref[...].astype(BF16)) + b_ref[...]

    return pl.pallas_call(
        body, name="ada_fwd", grid=(n // tn,),
        in_specs=[pl.BlockSpec((16, D), lambda j: (0, 0)), pl.BlockSpec((D, tn), lambda j: (0, j)),
                  pl.BlockSpec((1, tn), lambda j: (0, j))],
        out_specs=pl.BlockSpec((16, tn), lambda j: (0, j)),
        out_shape=jax.ShapeDtypeStruct((16, n), F32),
        compiler_params=_params(("parallel",)),
    )(cs, w, b)


def _ada_bwd(cs, w, dmod):
    D, n = w.shape
    tn = _pick(n, (512, 384, 256, 128))

    def body(c_ref, w_ref, d_ref, gw_ref, da_ref):
        j = pl.program_id(0)
        cv = c_ref[...]
        a = cv * _sigmoid(cv)
        d = d_ref[...]
        gw_ref[...] = lax.dot_general(a, d, (((0,), (0,)), ((), ())), precision=HI, preferred_element_type=F32)

        @pl.when(j == 0)
        def _():
            da_ref[...] = jnp.zeros_like(da_ref)

        da_ref[...] += _dot_nt(d.astype(BF16), w_ref[...].astype(BF16))

    return pl.pallas_call(
        body, name="ada_bwd", grid=(n // tn,),
        in_specs=[pl.BlockSpec((16, D), lambda j: (0, 0)), pl.BlockSpec((D, tn), lambda j: (0, j)),
                  pl.BlockSpec((16, tn), lambda j: (0, j))],
        out_specs=[pl.BlockSpec((D, tn), lambda j: (0, j)), pl.BlockSpec((16, D), lambda j: (0, 0))],
        out_shape=[jax.ShapeDtypeStruct((D, n), F32), jax.ShapeDtypeStruct((16, D), F32)],
        compiler_params=_params(("arbitrary",)),
    )(cs, w, dmod)


def _rms1_fwd(xall, gain, shift2, scale2, n_ctx):
    T, D = xall.shape
    tb = _pick(n_ctx, (256, 128, 64, 32, 16))
    nctx = n_ctx // tb

    def body(x_ref, g_ref, sh_ref, sc_ref, o_ref):
        i = pl.program_id(0)
        xv = x_ref[...]
        r = lax.rsqrt(jnp.mean(xv * xv, axis=-1, keepdims=True) + EPS)
        nrm = xv * r * g_ref[...]
        lat = i >= nctx
        sh = jnp.where(lat, sh_ref[1:2, :], sh_ref[0:1, :])
        sc = jnp.where(lat, sc_ref[1:2, :], sc_ref[0:1, :])
        o_ref[...] = (nrm * (1.0 + sc) + sh).astype(BF16)

    vec = lambda r: pl.BlockSpec((r, D), lambda i: (0, 0))
    return pl.pallas_call(
        body, name="rms1_fwd", grid=(T // tb,),
        in_specs=[pl.BlockSpec((tb, D), lambda i: (i, 0)), vec(1), vec(2), vec(2)],
        out_specs=pl.BlockSpec((tb, D), lambda i: (i, 0)),
        out_shape=jax.ShapeDtypeStruct((T, D), BF16),
        compiler_params=_params(("parallel",)),
    )(xall, gain, shift2, scale2)


def _rms1_bwd(xall, dh, dxmid, gain, scale2, n_ctx):
    T, D = xall.shape
    L = T - n_ctx
    tb = _pick(n_ctx, (256, 128, 64, 32, 16))
    nctx = n_ctx // tb

    def body(x_ref, dh_ref, dxm_ref, g_ref, sc_ref, dx_ref, cs_ref):
        i = pl.program_id(0)
        lat = i >= nctx
        xv = x_ref[...]
        r = lax.rsqrt(jnp.mean(xv * xv, axis=-1, keepdims=True) + EPS)
        xh = xv * r
        g = g_ref[...]
        nrm = xh * g
        sc = jnp.where(lat, sc_ref[1:2, :], sc_ref[0:1, :])
        dhv = dh_ref[...]
        dn = dhv * (1.0 + sc)
        dxh = dn * g
        dxv = r * (dxh - xh * jnp.mean(dxh * xh, axis=-1, keepdims=True))
        s_sh = jnp.sum(dhv, axis=0, keepdims=True)
        s_sc = jnp.sum(dhv * nrm, axis=0, keepdims=True)
        s_g = jnp.sum(dn * xh, axis=0, keepdims=True)
        zero = jnp.zeros_like(s_sh)
        rows = lax.broadcasted_iota(jnp.int32, (8, D), 0)
        upd = jnp.where(rows == 0, jnp.where(lat, zero, s_sh),
              jnp.where(rows == 1, jnp.where(lat, zero, s_sc),
              jnp.where(rows == 2, jnp.where(lat, s_sh, zero),
              jnp.where(rows == 3, jnp.where(lat, s_sc, zero),
              jnp.where(rows == 4, s_g, 0.0)))))

        @pl.when(i == 0)
        def _():
            cs_ref[...] = jnp.zeros_like(cs_ref)

        cs_ref[...] += upd

        @pl.when(lat)
        def _():
            dx_ref[...] = dxv + dxm_ref[...]

    lat_blk = lambda i: (jnp.maximum(i - nctx, 0), 0)
    vec = lambda r: pl.BlockSpec((r, D), lambda i: (0, 0))
    return pl.pallas_call(
        body, name="rms1_bwd", grid=(T // tb,),
        in_specs=[pl.BlockSpec((tb, D), lambda i: (i, 0)), pl.BlockSpec((tb, D), lambda i: (i, 0)),
                  pl.BlockSpec((tb, D), lat_blk), vec(1), vec(2)],
        out_specs=[pl.BlockSpec((tb, D), lat_blk), vec(8)],
        out_shape=[jax.ShapeDtypeStruct((L, D), F32), jax.ShapeDtypeStruct((8, D), F32)],
        compiler_params=_params(("arbitrary",)),
    )(xall, dh, dxmid, gain, scale2)


def _resid_rms2_fwd(x, mo, vecs):
    L, D = x.shape
    tb = _pick(L, (256, 128, 64))

    def body(x_ref, mo_ref, v_ref, xm_ref, h_ref):
        xm = x_ref[...] + v_ref[0:1, :] * mo_ref[...]
        xm_ref[...] = xm
        r = lax.rsqrt(jnp.mean(xm * xm, axis=-1, keepdims=True) + EPS)
        h_ref[...] = (xm * r * v_ref[1:2, :] * (1.0 + v_ref[3:4, :]) + v_ref[2:3, :]).astype(BF16)

    blk = pl.BlockSpec((tb, D), lambda i: (i, 0))
    return pl.pallas_call(
        body, name="resid_rms2_fwd", grid=(L // tb,),
        in_specs=[blk, blk, pl.BlockSpec((8, D), lambda i: (0, 0))],
        out_specs=[blk, blk],
        out_shape=[jax.ShapeDtypeStruct((L, D), F32), jax.ShapeDtypeStruct((L, D), BF16)],
        compiler_params=_params(("parallel",)),
    )(x, mo, vecs)


def _resid_rms2_bwd(xmid, dh_a, dh_b, dy, mo, vecs):
    L, D = xmid.shape
    tb = _pick(L, (256, 128, 64))

    def body(xm_ref, da_ref, db_ref, dy_ref, mo_ref, v_ref, dxm_ref, dmo_ref, cs_ref):
        i = pl.program_id(0)
        xm = xm_ref[...]
        r = lax.rsqrt(jnp.mean(xm * xm, axis=-1, keepdims=True) + EPS)
        xh = xm * r
        g = v_ref[1:2, :]
        nrm = xh * g
        dhv = da_ref[...] + db_ref[...]
        dn = dhv * (1.0 + v_ref[3:4, :])
        dxh = dn * g
        dxm = dy_ref[...] + r * (dxh - xh * jnp.mean(dxh * xh, axis=-1, keepdims=True))
        dxm_ref[...] = dxm
        dmo_ref[...] = (dxm * v_ref[0:1, :]).astype(BF16)
        s0 = jnp.sum(dhv, axis=0, keepdims=True)
        s1 = jnp.sum(dhv * nrm, axis=0, keepdims=True)
        s2 = jnp.sum(dn * xh, axis=0, keepdims=True)
        s3 = jnp.sum(dxm * mo_ref[...], axis=0, keepdims=True)
        rows = lax.broadcasted_iota(jnp.int32, (8, D), 0)
        upd = jnp.where(rows == 0, s0, jnp.where(rows == 1, s1, jnp.where(rows == 2, s2,
              jnp.where(rows == 3, s3, 0.0))))

        @pl.when(i == 0)
        def _():
            cs_ref[...] = jnp.zeros_like(cs_ref)

        cs_ref[...] += upd

    blk = pl.BlockSpec((tb, D), lambda i: (i, 0))
    vec = pl.BlockSpec((8, D), lambda i: (0, 0))
    return pl.pallas_call(
        body, name="resid_rms2_bwd", grid=(L // tb,),
        in_specs=[blk, blk, blk, blk, blk, vec],
        out_specs=[blk, blk, vec],
        out_shape=[jax.ShapeDtypeStruct((L, D), F32), jax.ShapeDtypeStruct((L, D), BF16),
                   jax.ShapeDtypeStruct((8, D), F32)],
        compiler_params=_params(("arbitrary",)),
    )(xmid, dh_a, dh_b, dy, mo, vecs)


def _loss_head(xmid, f, g2, target):
    L, D = xmid.shape
    tb = _pick(L, (256, 128, 64))

    def body(xm_ref, f_ref, g_ref, t_ref, dy_ref, df_ref, s_ref):
        i = pl.program_id(0)
        fv = f_ref[...]
        g = g_ref[...]
        err = xm_ref[...] + g * fv - t_ref[...]
        dy = err * (1.0 / D)
        dy_ref[...] = dy
        df_ref[...] = (dy * g).astype(BF16)
        s0 = jnp.sum(dy * fv, axis=0, keepdims=True)
        part = 0.5 * jnp.sum(jnp.mean(err * err, axis=-1, keepdims=True), axis=0, keepdims=True)
        rows = lax.broadcasted_iota(jnp.int32, (8, D), 0)
        upd = jnp.where(rows == 0, s0, jnp.where(rows == 1, part, 0.0))

        @pl.when(i == 0)
        def _():
            s_ref[...] = jnp.zeros_like(s_ref)

        s_ref[...] += upd

    blk = pl.BlockSpec((tb, D), lambda i: (i, 0))
    return pl.pallas_call(
        body, name="loss_head", grid=(L // tb,),
        in_specs=[blk, blk, pl.BlockSpec((1, D), lambda i: (0, 0)), blk],
        out_specs=[blk, blk, pl.BlockSpec((8, D), lambda i: (0, 0))],
        out_shape=[jax.ShapeDtypeStruct((L, D), F32), jax.ShapeDtypeStruct((L, D), BF16),
                   jax.ShapeDtypeStruct((8, D), F32)],
        compiler_params=_params(("arbitrary",)),
    )(xmid, f, g2, target)


def _gate_cols(D, off):
    tc = _pick(np.gcd(D, off), (512, 256, 128))
    return tc, off // tc


def _merge_fwd(za, zb, p, n_ctx, off_a, off_b):
    L, D = za.shape
    tb = _pick(n_ctx, (256, 128, 64, 32, 16))
    nctx = n_ctx // tb
    tc, oa = _gate_cols(D, off_a)
    _, ob = _gate_cols(D, off_b)
    if off_b % tc:
        raise ValueError("gate column offsets must share a column tile")
    ob = off_b // tc

    def body(za_ref, zb_ref, ga_ref, gb_ref, z_ref):
        z_ref[...] = (_sigmoid(ga_ref[...]) * za_ref[...] + _sigmoid(gb_ref[...]) * zb_ref[...]).astype(BF16)

    blk = pl.BlockSpec((tb, tc), lambda i, j: (i, j))
    return pl.pallas_call(
        body, name="merge_fwd", grid=(L // tb, D // tc),
        in_specs=[blk, blk, pl.BlockSpec((tb, tc), lambda i, j: (i + nctx, oa + j)),
                  pl.BlockSpec((tb, tc), lambda i, j: (i + nctx, ob + j))],
        out_specs=blk,
        out_shape=jax.ShapeDtypeStruct((L, D), BF16),
        compiler_params=_params(("parallel", "parallel")),
    )(za, zb, p, p)


def _merge_bwd(dz, za, zb, p, n_ctx, off_a, off_b):
    L, D = za.shape
    T = L + n_ctx
    tb = _pick(n_ctx, (256, 128, 64, 32, 16))
    nctx = n_ctx // tb
    tc = _gate_cols(D, off_a)[0]
    oa, ob = off_a // tc, off_b // tc

    def body(dz_ref, za_ref, zb_ref, ga_ref, gb_ref, dza_ref, dzb_ref, dga_ref, dgb_ref):
        i = pl.program_id(1)

        @pl.when(i < nctx)
        def _():
            dga_ref[...] = jnp.zeros_like(dga_ref)
            dgb_ref[...] = jnp.zeros_like(dgb_ref)

        @pl.when(i >= nctx)
        def _():
            dzv = dz_ref[...]
            sa = _sigmoid(ga_ref[...])
            sb = _sigmoid(gb_ref[...])
            dza_ref[...] = (dzv * sa).astype(BF16)
            dzb_ref[...] = (dzv * sb).astype(BF16)
            dga_ref[...] = (dzv * za_ref[...] * sa * (1.0 - sa)).astype(BF16)
            dgb_ref[...] = (dzv * zb_ref[...] * sb * (1.0 - sb)).astype(BF16)

    lat = pl.BlockSpec((tb, tc), lambda j, i: (jnp.maximum(i - nctx, 0), j))
    allr = pl.BlockSpec((tb, tc), lambda j, i: (i, j))
    return pl.pallas_call(
        body, name="merge_bwd", grid=(D // tc, T // tb),
        in_specs=[lat, lat, lat, pl.BlockSpec((tb, tc), lambda j, i: (i, oa + j)),
                  pl.BlockSpec((tb, tc), lambda j, i: (i, ob + j))],
        out_specs=[lat, lat, allr, allr],
        out_shape=[jax.ShapeDtypeStruct((L, D), BF16), jax.ShapeDtypeStruct((L, D), BF16),
                   jax.ShapeDtypeStruct((T, D), BF16), jax.ShapeDtypeStruct((T, D), BF16)],
        compiler_params=_params(("arbitrary", "arbitrary")),
    )(dz, za, zb, p, p)


def _shift_down(u, rows):
    return jnp.where(rows == 0, 0.0, pltpu.roll(u, 1, 0))


def _shift_up(u, rows):
    n = u.shape[0]
    return jnp.where(rows == n - 1, 0.0, pltpu.roll(u, n - 1, 0))


def _convgate_fwd(u1, u3, cw, cb):
    L, F = u1.shape
    tc = _pick(F, (256, 128))

    def body(u1_ref, u3_ref, w_ref, b_ref, a_ref):
        u = u1_ref[...]
        rows = lax.broadcasted_iota(jnp.int32, u.shape, 0)
        cv = _shift_down(u, rows) * w_ref[0:1, :] + u * w_ref[1:2, :] + _shift_up(u, rows) * w_ref[2:3, :] + b_ref[...]
        a_ref[...] = (cv * _sigmoid(cv) * u3_ref[...]).astype(BF16)

    blk = pl.BlockSpec((L, tc), lambda j: (0, j))
    return pl.pallas_call(
        body, name="convgate_fwd", grid=(F // tc,),
        in_specs=[blk, blk, pl.BlockSpec((8, tc), lambda j: (0, j)), pl.BlockSpec((1, tc), lambda j: (0, j))],
        out_specs=blk,
        out_shape=jax.ShapeDtypeStruct((L, F), BF16),
        compiler_params=_params(("parallel",)),
    )(u1, u3, cw, cb)


def _convgate_bwd(u1, u3, da, cw, cb):
    L, F = u1.shape
    tc = _pick(F, (256, 128))

    def body(u1_ref, u3_ref, da_ref, w_ref, b_ref, du1_ref, du3_ref, s_ref):
        u = u1_ref[...]
        rows = lax.broadcasted_iota(jnp.int32, u.shape, 0)
        um, up = _shift_down(u, rows), _shift_up(u, rows)
        w0, w1, w2 = w_ref[0:1, :], w_ref[1:2, :], w_ref[2:3, :]
        cv = um * w0 + u * w1 + up * w2 + b_ref[...]
        s = _sigmoid(cv)
        dav = da_ref[...]
        du3_ref[...] = (dav * cv * s).astype(BF16)
        dcv = dav * u3_ref[...] * (s * (1.0 + cv * (1.0 - s)))
        du1_ref[...] = (_shift_up(dcv, rows) * w0 + dcv * w1 + _shift_down(dcv, rows) * w2).astype(BF16)
        r8 = lax.broadcasted_iota(jnp.int32, (8, tc), 0)
        s0 = jnp.sum(dcv * um, axis=0, keepdims=True)
        s1 = jnp.sum(dcv * u, axis=0, keepdims=True)
        s2 = jnp.sum(dcv * up, axis=0, keepdims=True)
        s3 = jnp.sum(dcv, axis=0, keepdims=True)
        s_ref[...] = jnp.where(r8 == 0, s0, jnp.where(r8 == 1, s1, jnp.where(r8 == 2, s2,
                     jnp.where(r8 == 3, s3, 0.0))))

    blk = pl.BlockSpec((L, tc), lambda j: (0, j))
    v8 = pl.BlockSpec((8, tc), lambda j: (0, j))
    return pl.pallas_call(
        body, name="convgate_bwd", grid=(F // tc,),
        in_specs=[blk, blk, blk, v8, pl.BlockSpec((1, tc), lambda j: (0, j))],
        out_specs=[blk, blk, v8],
        out_shape=[jax.ShapeDtypeStruct((L, F), BF16), jax.ShapeDtypeStruct((L, F), BF16),
                   jax.ShapeDtypeStruct((8, F), F32)],
        compiler_params=_params(("parallel",)),
    )(u1, u3, da, cw, cb)


def _lower_bound(lbl_ref, d):
    l0, l1 = lbl_ref[d, 0:1, :], lbl_ref[d, 1:2, :]
    m = jnp.maximum(l0, l1)
    e0, e1 = jnp.exp(l0 - m), jnp.exp(l1 - m)
    return e0 / (e0 + e1)


def _tri(rev):
    t = lax.broadcasted_iota(jnp.int32, (CHUNK, CHUNK), 0)
    s = lax.broadcasted_iota(jnp.int32, (CHUNK, CHUNK), 1)
    return jnp.where((s >= t) if rev else (s <= t), 1.0, 0.0).astype(F32)


def _chunk_terms(z, lb, rev):
    sg = _sigmoid(z)
    f = lb + (1.0 - lb) * sg
    g = jnp.log(f)
    c = jnp.dot(_tri(rev), g, precision=HI, preferred_element_type=F32)
    tot = c[0:1, :] if rev else c[CHUNK - 1:CHUNK, :]
    return sg, f, 1.0 - f, c, tot


def _pair_decay(c, s, rev):
    t = lax.broadcasted_iota(jnp.int32, (CHUNK, 1), 0)
    later = (t <= s) if rev else (t >= s)
    return jnp.where(later, jnp.exp(jnp.minimum(c - c[s:s + 1, :], 0.0)), 0.0)


def _scan_chunk(i, n_ctx_chunks, n_chunks, rev):
    if not rev:
        return i
    return jnp.where(i < n_ctx_chunks, n_ctx_chunks - 1 - i, n_chunks + n_ctx_chunks - 1 - i)


def _rows(ci):
    return pl.ds(pl.multiple_of(ci * CHUNK, CHUNK), CHUNK)


def _hgrn_cols(HA):
    return HA // HEAD


def _hgrn_fwd(p, lbl, ng, n_ctx, HA):
    T = p.shape[0]
    L = T - n_ctx
    nh = _hgrn_cols(HA)
    nc, ncc = T // CHUNK, n_ctx // CHUNK

    def body(q_ref, zf_ref, zb_ref, v_ref, og_ref, lbl_ref, ng_ref, ya_ref, o_ref, oacc):
        for d, rev in ((0, False), (1, True)):
            z_ref = zb_ref if rev else zf_ref
            lb = _lower_bound(lbl_ref, d)

            def step(i, St, rev=rev, z_ref=z_ref, lb=lb, first=(d == 0)):
                ci = _scan_chunk(i, ncc, nc, rev)
                rows = _rows(ci)
                q, v = q_ref[rows, :], v_ref[rows, :]
                _, _, k, c, tot = _chunk_terms(z_ref[rows, :], lb, rev)
                ke = k * jnp.exp(tot - c)
                o = _dot_nt((q * jnp.exp(c)).astype(BF16), St.astype(BF16))
                for s in range(CHUNK):
                    E = _pair_decay(c, s, rev)
                    a = jnp.sum(q * E * k[s:s + 1, :], axis=1, keepdims=True)
                    o = o + a * v[s:s + 1, :]
                if first:
                    oacc[rows, :] = o
                else:
                    oacc[rows, :] += o
                return St * jnp.exp(tot) + _dot_tn(v.astype(BF16), ke.astype(BF16))

            lax.fori_loop(0, nc, step, jnp.zeros((HEAD, HEAD), F32))

        o = oacc[pl.ds(n_ctx, L), :]
        o_ref[...] = o
        r = lax.rsqrt(jnp.mean(o * o, axis=-1, keepdims=True) + EPS)
        og = og_ref[pl.ds(n_ctx, L), :]
        ya_ref[...] =(o * r * ng_ref[...] * (og * _sigmoid(og))).astype(BF16)

    cb = HA // HEAD
    col = lambda kk: pl.BlockSpec((T, HEAD), lambda h: (0, kk * cb + h))
    return pl.pallas_call(
        body, name="hgrn_fwd", grid=(nh,),
        in_specs=[col(0), col(1), col(2), col(3), col(4),
                  pl.BlockSpec((2, 2, HEAD), lambda h: (0, 0, h)), pl.BlockSpec((1, HEAD), lambda h: (0, 0))],
        out_specs=[pl.BlockSpec((L, HEAD), lambda h: (0, h)), pl.BlockSpec((L, HEAD), lambda h: (0, h))],
        out_shape=[jax.ShapeDtypeStruct((L, HA), BF16), jax.ShapeDtypeStruct((L, HA), F32)],
        scratch_shapes=[pltpu.VMEM((T, HEAD), F32)],
        compiler_params=_params(("parallel",)),
    )(p, p, p, p, p, lbl, ng)


def _hgrn_bwd(p, lbl, ng, o, dya, n_ctx, HA):
    T = p.shape[0]
    L = T - n_ctx
    nh = _hgrn_cols(HA)
    nc, ncc = T // CHUNK, n_ctx // CHUNK

    def body(q_ref, zf_ref, zb_ref, v_ref, og_ref, lbl_ref, ng_ref, o_ref, dya_ref,
             dq_ref, dzf_ref, dzb_ref, dv_ref, dog_ref, dlbl_ref, dng_ref, do_scr, st_scr, dq_acc, dv_acc):
        h = pl.program_id(0)
        ov = o_ref[...]
        r = lax.rsqrt(jnp.mean(ov * ov, axis=-1, keepdims=True) + EPS)
        oh = ov * r
        ogv = og_ref[pl.ds(n_ctx, L), :]
        sg_o = _sigmoid(ogv)
        dyv = dya_ref[...]
        ngv = ng_ref[...]
        dog_ref[pl.ds(0, n_ctx), :] = jnp.zeros((n_ctx, HEAD), BF16)
        dog_ref[pl.ds(n_ctx, L), :] = (dyv * oh * ngv * (sg_o * (1.0 + ogv * (1.0 - sg_o)))).astype(BF16)
        don = dyv * (ogv * sg_o)
        dng = jnp.sum(don * oh, axis=0, keepdims=True)
        doh = don * ngv
        do_scr[pl.ds(0, n_ctx), :] = jnp.zeros((n_ctx, HEAD), F32)
        do_scr[pl.ds(n_ctx, L), :] = r * (doh - oh * jnp.mean(doh * oh, axis=-1, keepdims=True))

        @pl.when(h == 0)
        def _():
            dng_ref[...] = jnp.zeros_like(dng_ref)

        dng_ref[0:1, :] += dng

        t16 = lax.broadcasted_iota(jnp.int32, (CHUNK, HEAD), 0)
        for d, rev in ((0, False), (1, True)):
            z_ref = zb_ref if rev else zf_ref
            dz_ref = dzb_ref if rev else dzf_ref
            lb = _lower_bound(lbl_ref, d)

            def fwd_step(i, St, rev=rev, z_ref=z_ref, lb=lb):
                ci = _scan_chunk(i, ncc, nc, rev)
                rows = _rows(ci)
                st_scr[ci] = St.astype(BF16)
                _, _, k, c, tot = _chunk_terms(z_ref[rows, :], lb, rev)
                ke = k * jnp.exp(tot - c)
                return St * jnp.exp(tot) + _dot_tn(v_ref[rows, :].astype(BF16), ke.astype(BF16))

            lax.fori_loop(0, nc, fwd_step, jnp.zeros((HEAD, HEAD), F32))

            def bwd_step(ii, carry, rev=rev, z_ref=z_ref, dz_ref=dz_ref, lb=lb, first=(d == 0)):
                dSt, dlb = carry
                i = nc - 1 - ii
                ci = _scan_chunk(i, ncc, nc, rev)
                rows = _rows(ci)
                q, v, do = q_ref[rows, :], v_ref[rows, :], do_scr[rows, :]
                sg, f, k, c, tot = _chunk_terms(z_ref[rows, :], lb, rev)
                ec, et = jnp.exp(c), jnp.exp(tot - c)
                qe, ke = q * ec, k * et
                St = st_scr[ci]
                dSb = dSt.astype(BF16)
                do_b = do.astype(BF16)
                dq = _dot(do_b, St) * ec
                dk = _dot(v.astype(BF16), dSb) * et
                dv = _dot_nt(ke.astype(BF16), dSb)
                dtot = (jnp.sum(St.astype(F32) * dSt, axis=0, keepdims=True) * jnp.exp(tot)
                        + jnp.sum(k * dk, axis=0, keepdims=True))
                for s in range(CHUNK):
                    E = _pair_decay(c, s, rev)
                    XE = E * k[s:s + 1, :]
                    a = jnp.sum(q * XE, axis=1, keepdims=True)
                    da = jnp.sum(do * v[s:s + 1, :], axis=1, keepdims=True)
                    dq = dq + da * XE
                    dk_row = jnp.sum(da * q * E, axis=0, keepdims=True)
                    dv_row = jnp.sum(a * do, axis=0, keepdims=True)
                    dk = dk + jnp.where(t16 == s, dk_row, 0.0)
                    dv = dv + jnp.where(t16 == s, dv_row, 0.0)
                dcum = q * dq - k * dk
                dg = jnp.dot(_tri(not rev), dcum, precision=HI, preferred_element_type=F32) + dtot
                df = dg / f - dk
                dz_ref[rows, :] = (df * (1.0 - lb) * sg * (1.0 - sg)).astype(BF16)
                if first:
                    dq_acc[rows, :] = dq
                    dv_acc[rows, :] = dv
                else:
                    dq_ref[rows, :] = (dq_acc[rows, :] + dq).astype(BF16)
                    dv_ref[rows, :] = (dv_acc[rows, :] + dv).astype(BF16)
                dSt_new = dSt * jnp.exp(tot) + _dot_tn(do_b, qe.astype(BF16))
                dlb_new = dlb + jnp.sum(df * (1.0 - sg), axis=0, keepdims=True)
                return dSt_new, dlb_new

            _, dlb = lax.fori_loop(0, nc, bwd_step, (jnp.zeros((HEAD, HEAD), F32), jnp.zeros((1, HEAD), F32)))
            dl0 = dlb * lb * (1.0 - lb)
            dlbl_ref[d, 0:1, :] = dl0
            dlbl_ref[d, 1:2, :] = -dl0

    cb = HA // HEAD
    col = lambda kk: pl.BlockSpec((T, HEAD), lambda h: (0, kk * cb + h))
    tcol = pl.BlockSpec((T, HEAD), lambda h: (0, h))
    lcol = pl.BlockSpec((L, HEAD), lambda h: (0, h))
    outs = pl.pallas_call(
        body, name="hgrn_bwd", grid=(nh,),
        in_specs=[col(0), col(1), col(2), col(3), col(4),
                  pl.BlockSpec((2, 2, HEAD), lambda h: (0, 0, h)), pl.BlockSpec((1, HEAD), lambda h: (0, 0)),
                  lcol, lcol],
        out_specs=[tcol, tcol, tcol, tcol, tcol, pl.BlockSpec((2, 2, HEAD), lambda h: (0, 0, h)),
                   pl.BlockSpec((8, HEAD), lambda h: (0, 0))],
        out_shape=[jax.ShapeDtypeStruct((T, HA), BF16)] * 5 + [jax.ShapeDtypeStruct((2, 2, HA), F32),
                                                               jax.ShapeDtypeStruct((8, HEAD), F32)],
        scratch_shapes=[pltpu.VMEM((T, HEAD), F32), pltpu.VMEM((nc, HEAD, HEAD), BF16),
                        pltpu.VMEM((T, HEAD), F32), pltpu.VMEM((T, HEAD), F32)],
        compiler_params=_params(("arbitrary",)),
    )(p, p, p, p, p, lbl, ng, o, dya)
    return outs


def _swap_halves(t, lane):
    q = HEAD // 4
    return jnp.where((lane % (2 * q)) < q, pltpu.roll(t, HEAD - q, 1), pltpu.roll(t, q, 1))


def _qk_norm(t, g):
    r = lax.rsqrt(jnp.mean(t * t, axis=-1, keepdims=True) + EPS)
    return t * r, r


def _rope(t, cos, sin, lane):
    return t * cos + _swap_halves(t, lane) * sin


def _qk_norm_bwd(dy, th, r, g):
    dth = dy * g
    return r * (dth - th * jnp.mean(dth * th, axis=-1, keepdims=True)), jnp.sum(dy * th, axis=0, keepdims=True)


def _rope_bwd(dy, cos, sin, lane):
    return dy * cos + _swap_halves(dy * sin, lane)


def _na_geometry(L):
    n_rows = L // GRID_W
    kr = min(WIN_R, n_rows)
    return n_rows, kr


def _na_prep(q_ref, k_ref, v_ref, gq_ref, gk_ref, cos_ref, sin_ref, qs, ks, vs, n_ctx, L):
    lane = lax.broadcasted_iota(jnp.int32, (L, HEAD), 1)
    cos, sin = cos_ref[...], sin_ref[...]
    qh, _ = _qk_norm(q_ref[pl.ds(n_ctx, L), :], None)
    qs[...] = _rope(qh * gq_ref[...], cos, sin, lane).astype(BF16)
    kh, _ = _qk_norm(k_ref[pl.ds(n_ctx, L), :], None)
    ks[pl.ds(n_ctx, L), :] = _rope(kh * gk_ref[...], cos, sin, lane).astype(BF16)
    kc, _ = _qk_norm(k_ref[pl.ds(0, n_ctx), :], None)
    ks[pl.ds(0, n_ctx), :] = (kc * gk_ref[...]).astype(BF16)
    vs[...] = v_ref[...].astype(BF16)


def _na_scores(r, qs, ks, bias_ref, n_ctx, n_rows, kr):
    scale = HEAD ** -0.5
    r0 = jnp.clip(r - WIN_R // 2, 0, n_rows - kr)
    qrows = pl.ds(pl.multiple_of(r * GRID_W, GRID_W), GRID_W)
    krows = pl.ds(pl.multiple_of(n_ctx + r0 * GRID_W, GRID_W), kr * GRID_W)
    qv = qs[qrows, :]
    sb = _dot_nt(qv, ks[krows, :]) * scale
    b0 = r0 - r + (WIN_R - 1)
    sb = sb + jnp.concatenate([bias_ref[0, b0 + 2 * jj] for jj in range(kr // 2)], axis=1)
    sc = _dot_nt(qv, ks[pl.ds(0, n_ctx), :]) * scale
    m = jnp.maximum(jnp.max(sb, axis=1, keepdims=True), jnp.max(sc, axis=1, keepdims=True))
    eb, ec = jnp.exp(sb - m), jnp.exp(sc - m)
    inv = 1.0 / (jnp.sum(eb, axis=1, keepdims=True) + jnp.sum(ec, axis=1, keepdims=True))
    return eb * inv, ec * inv, qrows, krows, b0


def _na_fwd(p, bias, gq, gk, cos, sin, n_ctx, off, HB):
    T = p.shape[0]
    L = T - n_ctx
    nh = HB // HEAD
    n_rows, kr = _na_geometry(L)
    ob = off // HEAD

    def body(q_ref, k_ref, v_ref, bias_ref, gq_ref, gk_ref, cos_ref, sin_ref, y_ref, qs, ks, vs):
        _na_prep(q_ref, k_ref, v_ref, gq_ref, gk_ref, cos_ref, sin_ref, qs, ks, vs, n_ctx, L)

        def step(r, carry):
            pb, pc, qrows, krows, _ = _na_scores(r, qs, ks, bias_ref, n_ctx, n_rows, kr)
            y = _dot(pb.astype(BF16), vs[krows, :]) + _dot(pc.astype(BF16), vs[pl.ds(0, n_ctx), :])
            y_ref[qrows, :] = y.astype(BF16)
            return carry

        lax.fori_loop(0, n_rows, step, 0)

    col = lambda kk: pl.BlockSpec((T, HEAD), lambda h: (0, ob + kk * nh + h))
    vec = pl.BlockSpec((1, HEAD), lambda h: (0, 0))
    tab = pl.BlockSpec((L, HEAD), lambda h: (0, 0))
    return pl.pallas_call(
        body, name="na_fwd", grid=(nh,),
        in_specs=[col(0), col(1), col(2), pl.BlockSpec((1,) + bias.shape[1:], lambda h: (h, 0, 0, 0)),
                  vec, vec, tab, tab],
        out_specs=pl.BlockSpec((L, HEAD), lambda h: (0, h)),
        out_shape=jax.ShapeDtypeStruct((L, HB), BF16),
        scratch_shapes=[pltpu.VMEM((L, HEAD), BF16), pltpu.VMEM((T, HEAD), BF16), pltpu.VMEM((T, HEAD), BF16)],
        compiler_params=_params(("parallel",)),
    )(p, p, p, bias, gq, gk, cos, sin)


def _na_bwd(p, bias, gq, gk, cos, sin, dyb, n_ctx, off, HB):
    T = p.shape[0]
    L = T - n_ctx
    nh = HB // HEAD
    n_rows, kr = _na_geometry(L)
    ob = off // HEAD
    scale = HEAD ** -0.5

    def body(q_ref, k_ref, v_ref, bias_ref, gq_ref, gk_ref, cos_ref, sin_ref, dy_ref,
             dq_ref, dk_ref, dv_ref, dbias_ref, dg_ref, qs, ks, vs, dqa, dka, dva):
        h = pl.program_id(0)
        _na_prep(q_ref, k_ref, v_ref, gq_ref, gk_ref, cos_ref, sin_ref, qs, ks, vs, n_ctx, L)
        dka[...] = jnp.zeros_like(dka)
        dva[...] = jnp.zeros_like(dva)
        dbias_ref[...] = jnp.zeros_like(dbias_ref)

        def step(r, carry):
            pb, pc, qrows, krows, b0 = _na_scores(r, qs, ks, bias_ref, n_ctx, n_rows, kr)
            crows = pl.ds(0, n_ctx)
            do = dy_ref[qrows, :]
            qv = qs[qrows, :]
            dpb = _dot_nt(do, vs[krows, :])
            dpc = _dot_nt(do, vs[crows, :])
            delta = jnp.sum(pb * dpb, axis=1, keepdims=True) + jnp.sum(pc * dpc, axis=1, keepdims=True)
            dsb = pb * (dpb - delta)
            dsc = pc * (dpc - delta)
            for jj in range(kr // 2):
                dbias_ref[0, b0 + 2 * jj] += dsb[:, jj * 2 * GRID_W:(jj + 1) * 2 * GRID_W]
            dsb_b, dsc_b = dsb.astype(BF16), dsc.astype(BF16)
            dqa[qrows, :] = (_dot(dsb_b, ks[krows, :]) + _dot(dsc_b, ks[crows, :])) * scale
            dka[krows, :] += _dot_tn(dsb_b, qv) * scale
            dka[crows, :] += _dot_tn(dsc_b, qv) * scale
            dva[krows, :] += _dot_tn(pb.astype(BF16), do)
            dva[crows, :] += _dot_tn(pc.astype(BF16), do)
            return carry

        lax.fori_loop(0, n_rows, step, 0)

        lane = lax.broadcasted_iota(jnp.int32, (L, HEAD), 1)
        cos, sin = cos_ref[...], sin_ref[...]
        lat, ctx = pl.ds(n_ctx, L), pl.ds(0, n_ctx)
        gqv, gkv = gq_ref[...], gk_ref[...]
        qh, rq = _qk_norm(q_ref[lat, :], None)
        dq, dgq = _qk_norm_bwd(_rope_bwd(dqa[...], cos, sin, lane), qh, rq, gqv)
        dq_ref[ctx, :] = jnp.zeros((n_ctx, HEAD), BF16)
        dq_ref[lat, :] = dq.astype(BF16)
        kh, rk = _qk_norm(k_ref[lat, :], None)
        dk, dgk = _qk_norm_bwd(_rope_bwd(dka[lat, :], cos, sin, lane), kh, rk, gkv)
        dk_ref[lat, :] = dk.astype(BF16)
        kch, rkc = _qk_norm(k_ref[ctx, :], None)
        dkc, dgkc = _qk_norm_bwd(dka[ctx, :], kch, rkc, gkv)
        dk_ref[ctx, :] = dkc.astype(BF16)
        dv_ref[...] = dva[...].astype(BF16)

        @pl.when(h == 0)
        def _():
            dg_ref[...] = jnp.zeros_like(dg_ref)

        dg_ref[0:1, :] += dgq
        dg_ref[1:2, :] += dgk + dgkc

    col = lambda kk: pl.BlockSpec((T, HEAD), lambda h: (0, ob + kk * nh + h))
    vec = pl.BlockSpec((1, HEAD), lambda h: (0, 0))
    tab = pl.BlockSpec((L, HEAD), lambda h: (0, 0))
    tcol = pl.BlockSpec((T, HEAD), lambda h: (0, h))
    bspec = pl.BlockSpec((1,) + bias.shape[1:], lambda h: (h, 0, 0, 0))
    return pl.pallas_call(
        body, name="na_bwd", grid=(nh,),
        in_specs=[col(0), col(1), col(2), bspec, vec, vec, tab, tab, pl.BlockSpec((L, HEAD), lambda h: (0, h))],
        out_specs=[tcol, tcol, tcol, bspec, pl.BlockSpec((8, HEAD), lambda h: (0, 0))],
        out_shape=[jax.ShapeDtypeStruct((T, HB), BF16)] * 3 + [jax.ShapeDtypeStruct(bias.shape, F32),
                                                               jax.ShapeDtypeStruct((8, HEAD), F32)],
        scratch_shapes=[pltpu.VMEM((L, HEAD), BF16), pltpu.VMEM((T, HEAD), BF16), pltpu.VMEM((T, HEAD), BF16),
                        pltpu.VMEM((L, HEAD), F32), pltpu.VMEM((T, HEAD), F32), pltpu.VMEM((T, HEAD), F32)],
        compiler_params=_params(("arbitrary",)),
    )(p, p, p, bias, gq, gk, cos, sin, dyb)


def _bias_tables():
    w = np.arange(GRID_W)
    col_start = np.clip(w - WIN_C // 2, 0, GRID_W - WIN_C)
    col_in = (w[None, :] >= col_start[:, None]) & (w[None, :] < col_start[:, None] + WIN_C)
    dc = np.clip(w[None, :] - w[:, None], -(WIN_C - 1), WIN_C - 1) + WIN_C - 1
    n_pair = 2 * WIN_R
    ridx = np.zeros((n_pair, GRID_W, 2 * GRID_W), np.int32)
    cidx = np.zeros((n_pair, GRID_W, 2 * GRID_W), np.int32)
    valid = np.zeros((n_pair, GRID_W, 2 * GRID_W), bool)
    for i in range(n_pair):
        for half in range(2):
            row = i + half
            sl = slice(half * GRID_W, (half + 1) * GRID_W)
            ridx[i, :, sl] = min(row, 2 * WIN_R - 2)
            cidx[i, :, sl] = dc
            valid[i, :, sl] = col_in & (row <= 2 * WIN_R - 2)
    return ridx, cidx, valid


def _expand_bias(table):
    ridx, cidx, valid = _bias_tables()
    return jnp.where(valid[None], table[:, ridx, cidx], NEG)


def _bias_grad(dbias):
    H = dbias.shape[0]
    n_pair, n_dc = 2 * WIN_R, 2 * WIN_C - 1
    _, cidx, valid = _bias_tables()
    K = GRID_W * 2 * GRID_W
    oh = np.zeros((K, 128), np.float32)
    l = np.arange(2 * GRID_W)
    for cq in range(GRID_W):
        for ll in l:
            if valid[0, cq, ll] or valid[1, cq, ll]:
                oh[cq * 2 * GRID_W + ll, (ll // GRID_W) * 64 + cidx[0, cq, ll]] = 1.0
    flat = dbias.reshape(H * n_pair, K)

    def body(d_ref, oh_ref, o_ref):
        o_ref[...] = jnp.dot(d_ref[...], oh_ref[...], precision=HI, preferred_element_type=F32)

    g = pl.pallas_call(body, name="bias_grad", out_shape=jax.ShapeDtypeStruct((H * n_pair, 128), F32),
                       compiler_params=_params())(flat, jnp.asarray(oh))
    g = g.reshape(H, n_pair, 128)
    left, right = g[:, :, :n_dc], g[:, :, 64:64 + n_dc]
    out = left[:, :n_pair - 1]
    return out.at[:, 1:].add(right[:, :n_pair - 2])


def _rope_tables(L):
    pos = np.arange(L)
    row = (pos // GRID_W).astype(np.float32)
    colp = (pos % GRID_W).astype(np.float32)
    half = HEAD // 2
    nf = half // 2
    inv = (ROPE_THETA ** (-np.arange(nf, dtype=np.float32) / nf)).astype(np.float32)

    def tabs(pv):
        ang = pv[:, None] * inv[None, :]
        c, s = np.cos(ang), np.sin(ang)
        return np.concatenate([c, c], axis=1), np.concatenate([-s, s], axis=1)

    cr, sr = tabs(row)
    cc, sc = tabs(colp)
    return (jnp.asarray(np.concatenate([cr, cc], axis=1), F32), jnp.asarray(np.concatenate([sr, sc], axis=1), F32))


def _adamw(w, g, m, v, name):
    R, C = w.shape
    tr = _row_tile(R, C)
    c1 = 1.0 - ADAM_B1 ** ADAM_STEP
    c2 = 1.0 - ADAM_B2 ** ADAM_STEP

    def body(w_ref, g_ref, m_ref, v_ref, d_ref, mo_ref, vo_ref):
        gv = g_ref[...]
        mn = ADAM_B1 * m_ref[...] + (1.0 - ADAM_B1) * gv
        vn = ADAM_B2 * v_ref[...] + (1.0 - ADAM_B2) * (gv * gv)
        mo_ref[...] = mn
        vo_ref[...] = vn
        d_ref[...] = -ADAM_LR * ((mn / c1) / (jnp.sqrt(vn / c2) + ADAM_EPS) + ADAM_WD * w_ref[...])

    blk = pl.BlockSpec((tr, C), lambda i: (i, 0))
    return pl.pallas_call(
        body, name=name, grid=(R // tr,),
        in_specs=[blk] * 4, out_specs=[blk] * 3,
        out_shape=[jax.ShapeDtypeStruct((R, C), F32)] * 3,
        compiler_params=_params(("parallel",)),
    )(w, g, m, v)


PACK_W = 1024


def _pack(parts):
    flat, offs, pos = [], [], 0
    for a in parts:
        n = a.size
        padn = -n % PACK_W
        flat.append(jnp.pad(a.reshape(-1).astype(F32), (0, padn)))
        offs.append((pos, n, a.shape))
        pos += n + padn
    tail = -pos % (8 * PACK_W)
    if tail:
        flat.append(jnp.zeros((tail,), F32))
    return jnp.concatenate(flat).reshape(-1, PACK_W), offs


def _unpack(buf, offs, i):
    pos, n, shape = offs[i]
    return buf.reshape(buf.shape[:-2] + (-1,))[..., pos:pos + n].reshape(buf.shape[:-2] + shape)


def kernel(x, c, ctx, c_ctx, ada_w, ada_b, norm1_g, norm2_g, w_in, hgrn_lb_logits, hgrn_norm_g, na_q_norm_g, na_k_norm_g, na_rel_bias, w_branch_a, w_branch_b, w_out, ffn_w1, ffn_w3, ffn_conv_w, ffn_conv_b, ffn_w2, loss_target, m_c_ctx, m_ada_w, m_ada_b, m_norm1_g, m_norm2_g, m_w_in, m_hgrn_lb_logits, m_hgrn_norm_g, m_na_q_norm_g, m_na_k_norm_g, m_na_rel_bias, m_w_branch_a, m_w_branch_b, m_w_out, m_ffn_w1, m_ffn_w3, m_ffn_conv_w, m_ffn_conv_b, m_ffn_w2, v_c_ctx, v_ada_w, v_ada_b, v_norm1_g, v_norm2_g, v_w_in, v_hgrn_lb_logits, v_hgrn_norm_g, v_na_q_norm_g, v_na_k_norm_g, v_na_rel_bias, v_w_branch_a, v_w_branch_b, v_w_out, v_ffn_w1, v_ffn_w3, v_ffn_conv_w, v_ffn_conv_b, v_ffn_w2):
    weights = dict(c_ctx=c_ctx, ada_w=ada_w, ada_b=ada_b, norm1_g=norm1_g, norm2_g=norm2_g, w_in=w_in,
                   hgrn_lb_logits=hgrn_lb_logits, hgrn_norm_g=hgrn_norm_g, na_q_norm_g=na_q_norm_g,
                   na_k_norm_g=na_k_norm_g, na_rel_bias=na_rel_bias, w_branch_a=w_branch_a, w_branch_b=w_branch_b,
                   w_out=w_out, ffn_w1=ffn_w1, ffn_w3=ffn_w3, ffn_conv_w=ffn_conv_w, ffn_conv_b=ffn_conv_b,
                   ffn_w2=ffn_w2)
    moms = dict(c_ctx=(m_c_ctx, v_c_ctx), ada_w=(m_ada_w, v_ada_w), ada_b=(m_ada_b, v_ada_b),
                norm1_g=(m_norm1_g, v_norm1_g), norm2_g=(m_norm2_g, v_norm2_g), w_in=(m_w_in, v_w_in),
                hgrn_lb_logits=(m_hgrn_lb_logits, v_hgrn_lb_logits), hgrn_norm_g=(m_hgrn_norm_g, v_hgrn_norm_g),
                na_q_norm_g=(m_na_q_norm_g, v_na_q_norm_g), na_k_norm_g=(m_na_k_norm_g, v_na_k_norm_g),
                na_rel_bias=(m_na_rel_bias, v_na_rel_bias), w_branch_a=(m_w_branch_a, v_w_branch_a),
                w_branch_b=(m_w_branch_b, v_w_branch_b), w_out=(m_w_out, v_w_out), ffn_w1=(m_ffn_w1, v_ffn_w1),
                ffn_w3=(m_ffn_w3, v_ffn_w3), ffn_conv_w=(m_ffn_conv_w, v_ffn_conv_w),
                ffn_conv_b=(m_ffn_conv_b, v_ffn_conv_b), ffn_w2=(m_ffn_w2, v_ffn_w2))
    order = list(weights)

    L, D = x.shape[1], x.shape[2]
    N = ctx.shape[1]
    T = N + L
    HA = w_branch_a.shape[1]
    HB = w_branch_b.shape[1]
    F = ffn_conv_b.shape[1]
    IN = 5 * HA + 3 * HB + 2 * D
    n_ada = ada_w.shape[2]
    ix, iy, ic = _pos()
    chip = 2 * ix + iy
    dev = 2 * chip + ic

    pk0, offs0 = _pack([c[0], hgrn_lb_logits, ffn_conv_w[0]])
    g0 = _allgather8(pk0, "gather_small0")
    c_all = _unpack(g0, offs0, 0)
    lbl_parts = _unpack(g0, offs0, 1)
    lbl = jnp.concatenate([lbl_parts[2 * j] for j in range(N_CHIP)], axis=-1)
    cw_parts = _unpack(g0, offs0, 2)
    cw = jnp.concatenate([cw_parts[2 * j] for j in range(N_CHIP)], axis=-1)
    cw8 = jnp.pad(cw, ((0, 5), (0, 0)))

    cs = jnp.concatenate([c_all, c_ctx[None, :], jnp.zeros((7, D), F32)], axis=0)
    ada_b_mine = lax.dynamic_slice(ada_b, (0, chip * n_ada), (1, n_ada))
    mod_mine = _ada_fwd(cs, ada_w[0], ada_b_mine)
    gm = _allgather8(mod_mine, "gather_mod")
    mod = jnp.concatenate([gm[2 * j] for j in range(N_CHIP)], axis=-1)
    mod_l = lax.dynamic_slice(mod, (dev, 0), (1, N_MOD * D)).reshape(N_MOD, D)
    mod_c = mod[8].reshape(N_MOD, D)
    sh1, sc1, g1, sh2, sc2, g2 = [mod_l[i:i + 1] for i in range(N_MOD)]
    shift1 = jnp.concatenate([mod_c[0:1], sh1], axis=0)
    scale1 = jnp.concatenate([mod_c[1:2], sc1], axis=0)

    shards = [w_in[0], w_branch_a[0], w_branch_b[0], w_out[0], ffn_w1[0], ffn_w3[0], ffn_w2[0]]
    names = ["w_in", "w_a", "w_b", "w_out", "w1", "w3", "w2"]
    shards_bf = [_cast_bf16(s, "cast_" + nm) for s, nm in zip(shards, names)]
    Win, Wa, Wb, Wo, W1, W3, W2 = _gather_shards(shards_bf, "gather_weights")
    Wo = Wo.reshape(1, D, D)
    W2 = W2.reshape(1, F, D)

    xall = jnp.concatenate([ctx[0], x[0]], axis=0)
    h_all = _rms1_fwd(xall, norm1_g, shift1, scale1, N)
    p = _mm_nn(h_all, Win, F32, "mm_p")
    y_a, o_a = _hgrn_fwd(p, lbl, hgrn_norm_g, N, HA)
    bias = _expand_bias(na_rel_bias[0])
    cos, sin = _rope_tables(L)
    off_na = 5 * HA
    y_b = _na_fwd(p, bias, na_q_norm_g, na_k_norm_g, cos, sin, N, off_na, HB)
    za = _mm_nn(y_a, Wa, F32, "mm_za")
    zb = _mm_nn(y_b, Wb, F32, "mm_zb")
    off_ga, off_gb = 5 * HA + 3 * HB, 5 * HA + 3 * HB + D
    z = _merge_fwd(za, zb, p, N, off_ga, off_gb)
    mo = _mm_nn(z, Wo, F32, "mm_mo")
    vec2 = jnp.concatenate([g1, norm2_g, sh2, sc2, jnp.zeros((4, D), F32)], axis=0)
    x_mid, h2 = _resid_rms2_fwd(x[0], mo, vec2)
    u1 = _mm_nn(h2, W1, F32, "mm_u1")
    u3 = _mm_nn(h2, W3, F32, "mm_u3")
    a = _convgate_fwd(u1, u3, cw8, ffn_conv_b)
    f = _mm_nn(a, W2, F32, "mm_f")
    dy, df, s_loss = _loss_head(x_mid, f, g2, loss_target[0])
    loss = lax.psum(s_loss[1, 0], ("x", "y", "c"))
    d_g2 = s_loss[0:1]

    gW2 = _mm_tn(a, df, 1, "mm_gw2").reshape(N_CHIP, F // N_CHIP, D)
    da = _mm_nt(df, W2, F32, "mm_da")
    du1, du3, s_conv = _convgate_bwd(u1, u3, da, cw8, ffn_conv_b)
    gW1 = _mm_tn(h2, du1, N_CHIP, "mm_gw1")
    gW3 = _mm_tn(h2, du3, N_CHIP, "mm_gw3")
    dh2a = _mm_nt(du1, W1, F32, "mm_dh2a")
    dh2b = _mm_nt(du3, W3, F32, "mm_dh2b")
    dxm, dmo, s_rms2 = _resid_rms2_bwd(x_mid, dh2a, dh2b, dy, mo, vec2)
    gWo = _mm_tn(z, dmo, 1, "mm_gwo").reshape(N_CHIP, D // N_CHIP, D)
    dz = _mm_nt(dmo, Wo, F32, "mm_dz")
    dza, dzb, dga, dgb = _merge_bwd(dz, za, zb, p, N, off_ga, off_gb)
    gWa = _mm_tn(y_a, dza, N_CHIP, "mm_gwa")
    gWb = _mm_tn(y_b, dzb, N_CHIP, "mm_gwb")
    dya = _mm_nt(dza, Wa, F32, "mm_dya")
    dyb = _mm_nt(dzb, Wb, BF16, "mm_dyb")
    dq_a, dzf, dzbk, di_a, dog, dlbl, s_ng = _hgrn_bwd(p, lbl, hgrn_norm_g, o_a, dya, N, HA)
    dq_n, dk_n, dv_n, dbias, s_qk = _na_bwd(p, bias, na_q_norm_g, na_k_norm_g, cos, sin, dyb, N, off_na, HB)
    dp = jnp.concatenate([dq_a, dzf, dzbk, di_a, dog, dq_n, dk_n, dv_n, dga, dgb], axis=1)
    gWin = _mm_tn(h_all, dp, N_CHIP, "mm_gwin")
    dh = _mm_nt(dp, Win, F32, "mm_dh")
    grad_x, s_rms1 = _rms1_bwd(xall, dh, dxm, norm1_g, scale1, N)
    d_table = _bias_grad(dbias)

    zD = jnp.zeros((1, D), F32)
    dmod_l = jnp.concatenate([s_rms1[2:3], s_rms1[3:4], s_rms2[3:4], s_rms2[0:1], s_rms2[1:2], d_g2], axis=0)
    dmod_c = jnp.concatenate([s_rms1[0:1], s_rms1[1:2], zD, zD, zD, zD], axis=0)
    pk1, offs1 = _pack([dmod_l, dmod_c, s_rms1[4], s_rms2[2], dlbl, s_ng[0], s_qk[0], s_qk[1], d_table,
                        s_conv[0:3], s_conv[3]])
    g1all = _allgather8(pk1, "gather_small1")
    tot1 = _sum8(g1all, "sum_small1")
    dmod_rows = _unpack(g1all, offs1, 0).reshape(N_DEV, N_MOD * D)
    dmod_c_tot = _unpack(tot1, offs1, 1).reshape(1, N_MOD * D)
    dmod16 = jnp.concatenate([dmod_rows, dmod_c_tot, jnp.zeros((7, N_MOD * D), F32)], axis=0)
    dmod16_mine = lax.dynamic_slice(dmod16, (0, chip * n_ada), (16, n_ada))
    g_ada_w, dact = _ada_bwd(cs, ada_w[0], dmod16_mine)
    pk2, offs2 = _pack([dact[8]])
    g2all = _allgather8(pk2, "gather_small2")
    dact_rows = _unpack(g2all, offs2, 0)
    dact_sel = jnp.concatenate([dact_rows[2 * j][None] for j in range(N_CHIP)] + [jnp.zeros((4, D), F32)], axis=0)

    grads = {}
    grads["ada_w"] = g_ada_w[None]
    grads["ada_b"] = (_unpack(tot1, offs1, 0) + _unpack(tot1, offs1, 1)).reshape(1, N_MOD * D)
    grads["norm1_g"] = _unpack(tot1, offs1, 2)[None]
    grads["norm2_g"] = _unpack(tot1, offs1, 3)[None]
    g_lbl = _unpack(tot1, offs1, 4)
    n_lb = HA // N_CHIP
    grads["hgrn_lb_logits"] = lax.dynamic_slice(g_lbl, (0, 0, chip * n_lb), (2, 2, n_lb))
    grads["hgrn_norm_g"] = _unpack(tot1, offs1, 5)[None]
    grads["na_q_norm_g"] = _unpack(tot1, offs1, 6)[None]
    grads["na_k_norm_g"] = _unpack(tot1, offs1, 7)[None]
    grads["na_rel_bias"] = _unpack(tot1, offs1, 8)[None]
    g_cw = _unpack(tot1, offs1, 9)
    n_f = F // N_CHIP
    grads["ffn_conv_w"] = lax.dynamic_slice(g_cw, (0, chip * n_f), (3, n_f))[None]
    grads["ffn_conv_b"] = _unpack(tot1, offs1, 10)[None]

    big = _reduce_scatter([gWin, gWa, gWb, gWo, gW1, gW3, gW2])
    for nm, g in zip(["w_in", "w_branch_a", "w_branch_b", "w_out", "ffn_w1", "ffn_w3", "ffn_w2"], big):
        grads[nm] = g[None]

    grads["c_ctx"] = _dsilu_rows(dact_sel, c_ctx[None, :], "grad_c_ctx")[0]

    big_names = ["ada_w", "w_in", "w_branch_a", "w_branch_b", "w_out", "ffn_w1", "ffn_w3", "ffn_w2"]
    small_names = [n for n in order if n not in big_names]
    delta, new_m, new_v = {}, {}, {}
    for nm in big_names:
        w2 = weights[nm][0]
        d_, m_, v_ = _adamw(w2, grads[nm][0], moms[nm][0][0], moms[nm][1][0], "adamw_" + nm)
        delta[nm], new_m[nm], new_v[nm] = d_[None], m_[None], v_[None]
    pw, offw = _pack([weights[n] for n in small_names])
    pg, _ = _pack([grads[n] for n in small_names])
    pm, _ = _pack([moms[n][0] for n in small_names])
    pv, _ = _pack([moms[n][1] for n in small_names])
    d_, m_, v_ = _adamw(pw, pg, pm, pv, "adamw_small")
    for i, nm in enumerate(small_names):
        delta[nm], new_m[nm], new_v[nm] = _unpack(d_, offw, i), _unpack(m_, offw, i), _unpack(v_, offw, i)

    return (loss, grad_x[None], *[grads[n] for n in order], *[delta[n] for n in order],
            *[new_m[n] for n in order], *[new_v[n] for n in order])


def _dsilu_rows(v, cv, name):
    D = v.shape[1]

    def body(v_ref, c_ref, o_ref):
        t = c_ref[...]
        s = _sigmoid(t)
        o_ref[...] = (((v_ref[0:1, :] + v_ref[1:2, :]) + v_ref[2:3, :]) + v_ref[3:4, :]) * (s * (1.0 + t * (1.0 - s)))

    return pl.pallas_call(body, name=name, out_shape=jax.ShapeDtypeStruct((1, D), F32),
                          compiler_params=_params())(v, cv)
```

```python
import functools

import numpy as np
import jax
import jax.numpy as jnp
from jax import lax
from jax.experimental import pallas as pl
from jax.experimental.pallas import tpu as pltpu

F32 = jnp.float32
BF16 = jnp.bfloat16
MESH = pl.DeviceIdType.MESH

HEAD = 128
GRID_W = 64
WIN_R = 8
WIN_C = 16
ROPE_THETA = 10000.0
EPS = 1e-6
N_MOD = 6
CHUNK = 16
ADAM_LR = 0.001
ADAM_B1 = 0.9
ADAM_B2 = 0.999
ADAM_EPS = 1e-08
ADAM_WD = 0.01
ADAM_STEP = 10
NEG = -1e30
VMEM_LIMIT = 56 * 1024 * 1024
N_DEV = 8
N_CHIP = 4
HI = lax.Precision.HIGHEST


def _pick(n, cands):
    for c in cands:
        if n % c == 0:
            return c
    return n


def _row_tile(rows, cols, target_bytes=1 << 20):
    want = max(16, target_bytes // (4 * cols))
    for t in (512, 256, 128, 64, 32, 16, 8):
        if t <= want and rows % t == 0:
            return t
    return rows


def _params(sem=None):
    return pltpu.CompilerParams(dimension_semantics=sem, vmem_limit_bytes=VMEM_LIMIT)


def _dot(a, b):
    return jnp.dot(a, b, preferred_element_type=F32)


def _dot_nt(a, b):
    return lax.dot_general(a, b, (((1,), (1,)), ((), ())), preferred_element_type=F32)


def _dot_tn(a, b):
    return lax.dot_general(a, b, (((0,), (0,)), ((), ())), preferred_element_type=F32)


def _sigmoid(x):
    return 1.0 / (1.0 + jnp.exp(-x))


def _col_tile(n):
    return n if n <= 1536 else _pick(n, (1024, 768, 512, 384, 256, 128))


def _mm_nn(x, w3, out_dtype, name):
    M, K = x.shape
    S, _, n = w3.shape
    tm = _pick(M, (768, 512, 256, 128, 64))
    tn = _col_tile(n)
    nb = n // tn

    def body(x_ref, w_ref, o_ref):
        o_ref[...] = _dot(x_ref[...].astype(BF16), w_ref[0]).astype(o_ref.dtype)

    return pl.pallas_call(
        body, name=name, grid=(M // tm, S * nb),
        in_specs=[pl.BlockSpec((tm, K), lambda i, j: (i, 0)),
                  pl.BlockSpec((1, K, tn), lambda i, j: (j // nb, 0, j % nb))],
        out_specs=pl.BlockSpec((tm, tn), lambda i, j: (i, j)),
        out_shape=jax.ShapeDtypeStruct((M, S * n), out_dtype),
        compiler_params=_params(("parallel", "parallel")),
    )(x, w3)


def _mm_nt(dy, w3, out_dtype, name):
    M = dy.shape[0]
    S, K, n = w3.shape
    tm = _pick(M, (768, 512, 256, 128, 64))
    tk = _pick(K, (512, 256, 128))
    tc = _col_tile(n)
    nb = n // tc
    nsteps = S * nb

    def body(dy_ref, w_ref, o_ref, acc_ref):
        s = pl.program_id(2)

        @pl.when(s == 0)
        def _():
            acc_ref[...] = jnp.zeros_like(acc_ref)

        acc_ref[...] += _dot_nt(dy_ref[...].astype(BF16), w_ref[0])

        @pl.when(s == nsteps - 1)
        def _():
            o_ref[...] = acc_ref[...].astype(o_ref.dtype)

    return pl.pallas_call(
        body, name=name, grid=(M // tm, K // tk, nsteps),
        in_specs=[pl.BlockSpec((tm, tc), lambda i, k, s: (i, s)),
                  pl.BlockSpec((1, tk, tc), lambda i, k, s: (s // nb, k, s % nb))],
        out_specs=pl.BlockSpec((tm, tk), lambda i, k, s: (i, k)),
        out_shape=jax.ShapeDtypeStruct((M, K), out_dtype),
        scratch_shapes=[pltpu.VMEM((tm, tk), F32)],
        compiler_params=_params(("parallel", "parallel", "arbitrary")),
    )(dy, w3)


def _mm_tn(x, dy, S, name):
    M, K = x.shape
    n = dy.shape[1] // S
    tk = _pick(K, (512, 256, 128))
    tn = _col_tile(n)
    nb = n // tn

    def body(x_ref, dy_ref, o_ref):
        o_ref[0] = _dot_tn(x_ref[...].astype(BF16), dy_ref[...].astype(BF16))

    return pl.pallas_call(
        body, name=name, grid=(S * nb, K // tk),
        in_specs=[pl.BlockSpec((M, tk), lambda j, k: (0, k)),
                  pl.BlockSpec((M, tn), lambda j, k: (0, j))],
        out_specs=pl.BlockSpec((1, tk, tn), lambda j, k: (j // nb, k, j % nb)),
        out_shape=jax.ShapeDtypeStruct((S, K, n), F32),
        compiler_params=_params(("parallel", "parallel")),
    )(x, dy)


def _chip_index():
    return (2 * lax.axis_index("x") + lax.axis_index("y")).astype(jnp.int32).reshape(1)


def _cast_bf16_slot(w, name):
    R, C = w.shape
    tr = _row_tile(R, C, 2 << 20)

    def body(j_ref, w_ref, o_ref):
        o_ref[0] = w_ref[...].astype(BF16)

    return pl.pallas_call(
        body, name=name,
        grid_spec=pltpu.PrefetchScalarGridSpec(
            num_scalar_prefetch=1, grid=(R // tr,),
            in_specs=[pl.BlockSpec((tr, C), lambda i, j_ref: (i, 0))],
            out_specs=pl.BlockSpec((1, tr, C), lambda i, j_ref: (j_ref[0], i, 0))),
        out_shape=jax.ShapeDtypeStruct((N_CHIP, R, C), BF16),
        compiler_params=_params(("parallel",)),
    )(_chip_index(), w)


def _pos():
    return lax.axis_index("x"), lax.axis_index("y"), lax.axis_index("c")


def _other_chips(x, y):
    return [(x, 1 - y), (1 - x, y), (1 - x, 1 - y)]


def _allgather8(v, name):
    R, C = v.shape

    def body(x_ref, out_ref, send_sems, recv_sems, local_sem):
        x, y, c = _pos()
        me, sibling = (x, y, c), (x, y, 1 - c)
        chips = _other_chips(x, y)

        def slot(px, py, pc):
            return out_ref.at[4 * px + 2 * py + pc]

        def copy(k, block, to, src=None):
            return pltpu.make_async_remote_copy(
                src_ref=slot(*block) if src is None else src, dst_ref=slot(*block),
                send_sem=send_sems.at[k], recv_sem=recv_sems.at[k], device_id=to, device_id_type=MESH)

        mine = pltpu.make_async_copy(x_ref, slot(*me), local_sem)
        mine.start()
        first = [copy(0, me, sibling, src=x_ref)]
        first += [copy(1 + j, me, (*chip, c), src=x_ref) for j, chip in enumerate(chips)]
        for cp in first:
            cp.start()
        passed = [copy(4 + j, (*chip, c), sibling) for j, chip in enumerate(chips)]
        for j, chip in enumerate(chips):
            copy(1 + j, (*chip, c), me).wait_recv()
            passed[j].start()
        copy(0, sibling, me).wait_recv()
        for j, chip in enumerate(chips):
            copy(4 + j, (*chip, 1 - c), me).wait_recv()
        for cp in first + passed:
            cp.wait_send()
        mine.wait()

    return pl.pallas_call(
        body, name=name,
        out_shape=jax.ShapeDtypeStruct((N_DEV, R, C), v.dtype),
        in_specs=[pl.BlockSpec(memory_space=pltpu.VMEM)],
        out_specs=pl.BlockSpec(memory_space=pltpu.VMEM),
        scratch_shapes=[pltpu.SemaphoreType.DMA((7,)), pltpu.SemaphoreType.DMA((7,)), pltpu.SemaphoreType.DMA],
        compiler_params=pltpu.CompilerParams(vmem_limit_bytes=VMEM_LIMIT),
    )(v)


def _gather_shards(bufs, name):
    n = len(bufs)

    def body(*refs):
        outs = refs[n:2 * n]
        send_sems, recv_sems = refs[2 * n:]
        x, y, c = _pos()
        sibling = (x, y, 1 - c)
        chips = _other_chips(x, y)

        def copy(t, k, chip, hc, to):
            h = outs[t].shape[1] // 2
            blk = outs[t].at[2 * chip[0] + chip[1], pl.ds(hc * h, h)]
            return pltpu.make_async_remote_copy(
                src_ref=blk, dst_ref=blk, send_sem=send_sems.at[6 * t + k], recv_sem=recv_sems.at[6 * t + k],
                device_id=to, device_id_type=MESH)

        sends = []
        for t in range(n):
            for k, chip in enumerate(chips):
                cp = copy(t, k, (x, y), c, (*chip, c))
                cp.start()
                sends.append(cp)
        for t in range(n):
            for k, chip in enumerate(chips):
                copy(t, k, chip, c, (x, y, c)).wait_recv()
                fwd = copy(t, 3 + k, chip, c, sibling)
                fwd.start()
                sends.append(fwd)
        for t in range(n):
            for k, chip in enumerate(chips):
                copy(t, 3 + k, chip, 1 - c, (x, y, c)).wait_recv()
        for cp in sends:
            cp.wait_send()

    any_spec = pl.BlockSpec(memory_space=pl.ANY)
    return pl.pallas_call(
        body, name=name,
        out_shape=[jax.ShapeDtypeStruct(b.shape, b.dtype) for b in bufs],
        in_specs=[any_spec] * n, out_specs=[any_spec] * n,
        input_output_aliases={t: t for t in range(n)},
        scratch_shapes=[pltpu.SemaphoreType.DMA((6 * n,)), pltpu.SemaphoreType.DMA((6 * n,))],
    )(*bufs)


def _pair_swap_halves(gs, name):
    n = len(gs)

    def body(*refs):
        ins, outs = refs[:n], refs[n:2 * n]
        send_sems, recv_sems = refs[2 * n:]
        x, y, c = _pos()
        cps = []
        for t in range(n):
            h = ins[t].shape[1] // 2
            cp = pltpu.make_async_remote_copy(
                src_ref=ins[t].at[:, pl.ds((1 - c) * h, h)], dst_ref=outs[t],
                send_sem=send_sems.at[t], recv_sem=recv_sems.at[t], device_id=(x, y, 1 - c), device_id_type=MESH)
            cp.start()
            cps.append(cp)
        for cp in cps:
            cp.wait()

    any_spec = pl.BlockSpec(memory_space=pl.ANY)
    return pl.pallas_call(
        body, name=name,
        out_shape=[jax.ShapeDtypeStruct((g.shape[0], g.shape[1] // 2, g.shape[2]), g.dtype) for g in gs],
        in_specs=[any_spec] * n, out_specs=[any_spec] * n,
        scratch_shapes=[pltpu.SemaphoreType.DMA((n,)), pltpu.SemaphoreType.DMA((n,))],
    )(*gs)


def _chip_scatter(ps, name):
    n = len(ps)

    def body(*refs):
        ins, outs = refs[:n], refs[n:2 * n]
        send_sems, recv_sems = refs[2 * n:]
        x, y, c = _pos()
        cps = []
        for t in range(n):
            for k, chip in enumerate(_other_chips(x, y)):
                cp = pltpu.make_async_remote_copy(
                    src_ref=ins[t].at[2 * chip[0] + chip[1]], dst_ref=outs[t].at[k],
                    send_sem=send_sems.at[3 * t + k], recv_sem=recv_sems.at[3 * t + k],
                    device_id=(*chip, c), device_id_type=MESH)
                cp.start()
                cps.append(cp)
        for cp in cps:
            cp.wait()

    any_spec = pl.BlockSpec(memory_space=pl.ANY)
    return pl.pallas_call(
        body, name=name,
        out_shape=[jax.ShapeDtypeStruct((3,) + p.shape[1:], p.dtype) for p in ps],
        in_specs=[any_spec] * n, out_specs=[any_spec] * n,
        scratch_shapes=[pltpu.SemaphoreType.DMA((3 * n,)), pltpu.SemaphoreType.DMA((3 * n,))],
    )(*ps)


def _pair_join_halves(fs, name):
    n = len(fs)

    def body(*refs):
        outs = refs[n:2 * n]
        send_sems, recv_sems = refs[2 * n:]
        x, y, c = _pos()

        def copy(t, hc):
            h = outs[t].shape[0] // 2
            blk = outs[t].at[pl.ds(hc * h, h)]
            return pltpu.make_async_remote_copy(
                src_ref=blk, dst_ref=blk, send_sem=send_sems.at[t], recv_sem=recv_sems.at[t],
                device_id=(x, y, 1 - c), device_id_type=MESH)

        cps = [copy(t, c) for t in range(n)]
        for cp in cps:
            cp.start()
        for t in range(n):
            copy(t, 1 - c).wait_recv()
        for cp in cps:
            cp.wait_send()

    any_spec = pl.BlockSpec(memory_space=pl.ANY)
    return pl.pallas_call(
        body, name=name,
        out_shape=[jax.ShapeDtypeStruct(f.shape, f.dtype) for f in fs],
        in_specs=[any_spec] * n, out_specs=[any_spec] * n,
        input_output_aliases={t: t for t in range(n)},
        scratch_shapes=[pltpu.SemaphoreType.DMA((n,)), pltpu.SemaphoreType.DMA((n,))],
    )(*fs)


def _pair_add(g, r, name):
    S, R, C = g.shape
    h = R // 2
    tr = _row_tile(h, C)
    nb = h // tr

    def body(c_ref, g_ref, r_ref, o_ref):
        o_ref[...] = g_ref[...] + r_ref[...]

    return pl.pallas_call(
        body, name=name,
        grid_spec=pltpu.PrefetchScalarGridSpec(
            num_scalar_prefetch=1, grid=(S, nb),
            in_specs=[pl.BlockSpec((1, tr, C), lambda s, i, c_ref: (s, c_ref[0] * nb + i, 0)),
                      pl.BlockSpec((1, tr, C), lambda s, i, c_ref: (s, i, 0))],
            out_specs=pl.BlockSpec((1, tr, C), lambda s, i, c_ref: (s, i, 0))),
        out_shape=jax.ShapeDtypeStruct((S, h, C), F32),
        compiler_params=_params(("parallel", "parallel")),
    )(lax.axis_index("c").astype(jnp.int32).reshape(1), g, r)


def _chip_sum(p, rb, name):
    S, h, C = p.shape
    tr = _row_tile(h, C)
    nb = h // tr
    jc = jnp.concatenate([_chip_index(), lax.axis_index("c").astype(jnp.int32).reshape(1)])

    def body(jc_ref, p_ref, r_ref, o_ref):
        o_ref[...] = ((p_ref[0] + r_ref[0]) + r_ref[1]) + r_ref[2]

    return pl.pallas_call(
        body, name=name,
        grid_spec=pltpu.PrefetchScalarGridSpec(
            num_scalar_prefetch=1, grid=(nb,),
            in_specs=[pl.BlockSpec((1, tr, C), lambda i, jc_ref: (jc_ref[0], i, 0)),
                      pl.BlockSpec((3, tr, C), lambda i, jc_ref: (0, i, 0))],
            out_specs=pl.BlockSpec((tr, C), lambda i, jc_ref: (jc_ref[1] * nb + i, 0))),
        out_shape=jax.ShapeDtypeStruct((2 * h, C), F32),
        compiler_params=_params(("parallel",)),
    )(jc, p, rb)


def _reduce_scatter(gs):
    recv = _pair_swap_halves(gs, "rs_pair_swap")
    ps = [_pair_add(g, r, f"rs_pair_add{t}") for t, (g, r) in enumerate(zip(gs, recv))]
    rbs = _chip_scatter(ps, "rs_chip_scatter")
    hs = [_chip_sum(p, rb, f"rs_chip_sum{t}") for t, (p, rb) in enumerate(zip(ps, rbs))]
    return _pair_join_halves(hs, "rs_pair_join")


def _sum8(g, name):
    _, R, C = g.shape

    def body(g_ref, o_ref):
        acc = g_ref[0]
        for d in range(1, N_DEV):
            acc = acc + g_ref[d]
        o_ref[...] = acc

    return pl.pallas_call(body, name=name, out_shape=jax.ShapeDtypeStruct((R, C), F32),
                          compiler_params=_params())(g)


def _ada_fwd(cs, w, b):
    D, n = w.shape
    tn = _pick(n, (512, 384, 256, 128))

    def body(c_ref, w_ref, b_ref, o_ref):
        cv = c_ref[...]
        a = (cv * _sigmoid(cv)).astype(BF16)
        o_ref[...] = _dot(a, w_ref[...].astype(BF16)) + b_ref[...]

    return pl.pallas_call(
        body, name="ada_fwd", grid=(n // tn,),
        in_specs=[pl.BlockSpec((16, D), lambda j: (0, 0)), pl.BlockSpec((D, tn), lambda j: (0, j)),
                  pl.BlockSpec((1, tn), lambda j: (0, j))],
        out_specs=pl.BlockSpec((16, tn), lambda j: (0, j)),
        out_shape=jax.ShapeDtypeStruct((16, n), F32),
        compiler_params=_params(("parallel",)),
    )(cs, w, b)


def _ada_bwd(cs, w, dmod):
    D, n = w.shape
    tn = _pick(n, (512, 384, 256, 128))

    def body(c_ref, w_ref, d_ref, gw_ref, da_ref):
        j = pl.program_id(0)
        cv = c_ref[...]
        a = cv * _sigmoid(cv)
        d = d_ref[...]
        gw_ref[...] = lax.dot_general(a, d, (((0,), (0,)), ((), ())), precision=HI, preferred_element_type=F32)

        @pl.when(j == 0)
        def _():
            da_ref[...] = jnp.zeros_like(da_ref)

        da_ref[...] += _dot_nt(d.astype(BF16), w_ref[...].astype(BF16))

    return pl.pallas_call(
        body, name="ada_bwd", grid=(n // tn,),
        in_specs=[pl.BlockSpec((16, D), lambda j: (0, 0)), pl.BlockSpec((D, tn), lambda j: (0, j)),
                  pl.BlockSpec((16, tn), lambda j: (0, j))],
        out_specs=[pl.BlockSpec((D, tn), lambda j: (0, j)), pl.BlockSpec((16, D), lambda j: (0, 0))],
        out_shape=[jax.ShapeDtypeStruct((D, n), F32), jax.ShapeDtypeStruct((16, D), F32)],
        compiler_params=_params(("arbitrary",)),
    )(cs, w, dmod)


def _rms1_fwd(xall, gain, shift2, scale2, n_ctx):
    T, D = xall.shape
    tb = _pick(n_ctx, (256, 128, 64, 32, 16))
    nctx = n_ctx // tb

    def body(x_ref, g_ref, sh_ref, sc_ref, o_ref):
        i = pl.program_id(0)
        xv = x_ref[...]
        r = lax.rsqrt(jnp.mean(xv * xv, axis=-1, keepdims=True) + EPS)
        nrm = xv * r * g_ref[...]
        lat = i >= nctx
        sh = jnp.where(lat, sh_ref[1:2, :], sh_ref[0:1, :])
        sc = jnp.where(lat, sc_ref[1:2, :], sc_ref[0:1, :])
        o_ref[...] = (nrm * (1.0 + sc) + sh).astype(BF16)

    vec = lambda r: pl.BlockSpec((r, D), lambda i: (0, 0))
    return pl.pallas_call(
        body, name="rms1_fwd", grid=(T // tb,),
        in_specs=[pl.BlockSpec((tb, D), lambda i: (i, 0)), vec(1), vec(2), vec(2)],
        out_specs=pl.BlockSpec((tb, D), lambda i: (i, 0)),
        out_shape=jax.ShapeDtypeStruct((T, D), BF16),
        compiler_params=_params(("parallel",)),
    )(xall, gain, shift2, scale2)


def _rms1_bwd(xall, dh, dxmid, gain, scale2, n_ctx):
    T, D = xall.shape
    L = T - n_ctx
    tb = _pick(n_ctx, (256, 128, 64, 32, 16))
    nctx = n_ctx // tb

    def body(x_ref, dh_ref, dxm_ref, g_ref, sc_ref, dx_ref, cs_ref):
        i = pl.program_id(0)
        lat = i >= nctx
        xv = x_ref[...]
        r = lax.rsqrt(jnp.mean(xv * xv, axis=-1, keepdims=True) + EPS)
        xh = xv * r
        g = g_ref[...]
        nrm = xh * g
        sc = jnp.where(lat, sc_ref[1:2, :], sc_ref[0:1, :])
        dhv = dh_ref[...]
        dn = dhv * (1.0 + sc)
        dxh = dn * g
        dxv = r * (dxh - xh * jnp.mean(dxh * xh, axis=-1, keepdims=True))
        s_sh = jnp.sum(dhv, axis=0, keepdims=True)
        s_sc = jnp.sum(dhv * nrm, axis=0, keepdims=True)
        s_g = jnp.sum(dn * xh, axis=0, keepdims=True)
        zero = jnp.zeros_like(s_sh)
        rows = lax.broadcasted_iota(jnp.int32, (8, D), 0)
        upd = jnp.where(rows == 0, jnp.where(lat, zero, s_sh),
              jnp.where(rows == 1, jnp.where(lat, zero, s_sc),
              jnp.where(rows == 2, jnp.where(lat, s_sh, zero),
              jnp.where(rows == 3, jnp.where(lat, s_sc, zero),
              jnp.where(rows == 4, s_g, 0.0)))))

        @pl.when(i == 0)
        def _():
            cs_ref[...] = jnp.zeros_like(cs_ref)

        cs_ref[...] += upd

        @pl.when(lat)
        def _():
            dx_ref[...] = dxv + dxm_ref[...]

    lat_blk = lambda i: (jnp.maximum(i - nctx, 0), 0)
    vec = lambda r: pl.BlockSpec((r, D), lambda i: (0, 0))
    return pl.pallas_call(
        body, name="rms1_bwd", grid=(T // tb,),
        in_specs=[pl.BlockSpec((tb, D), lambda i: (i, 0)), pl.BlockSpec((tb, D), lambda i: (i, 0)),
                  pl.BlockSpec((tb, D), lat_blk), vec(1), vec(2)],
        out_specs=[pl.BlockSpec((tb, D), lat_blk), vec(8)],
        out_shape=[jax.ShapeDtypeStruct((L, D), F32), jax.ShapeDtypeStruct((8, D), F32)],
        compiler_params=_params(("arbitrary",)),
    )(xall, dh, dxmid, gain, scale2)


def _resid_rms2_fwd(x, mo, vecs):
    L, D = x.shape
    tb = _pick(L, (256, 128, 64))

    def body(x_ref, mo_ref, v_ref, xm_ref, h_ref):
        xm = x_ref[...] + v_ref[0:1, :] * mo_ref[...]
        xm_ref[...] = xm
        r = lax.rsqrt(jnp.mean(xm * xm, axis=-1, keepdims=True) + EPS)
        h_ref[...] = (xm * r * v_ref[1:2, :] * (1.0 + v_ref[3:4, :]) + v_ref[2:3, :]).astype(BF16)

    blk = pl.BlockSpec((tb, D), lambda i: (i, 0))
    return pl.pallas_call(
        body, name="resid_rms2_fwd", grid=(L // tb,),
        in_specs=[blk, blk, pl.BlockSpec((8, D), lambda i: (0, 0))],
        out_specs=[blk, blk],
        out_shape=[jax.ShapeDtypeStruct((L, D), F32), jax.ShapeDtypeStruct((L, D), BF16)],
        compiler_params=_params(("parallel",)),
    )(x, mo, vecs)


def _resid_rms2_bwd(xmid, dh_a, dh_b, dy, mo, vecs):
    L, D = xmid.shape
    tb = _pick(L, (256, 128, 64))

    def body(xm_ref, da_ref, db_ref, dy_ref, mo_ref, v_ref, dxm_ref, dmo_ref, cs_ref):
        i = pl.program_id(0)
        xm = xm_ref[...]
        r = lax.rsqrt(jnp.mean(xm * xm, axis=-1, keepdims=True) + EPS)
        xh = xm * r
        g = v_ref[1:2, :]
        nrm = xh * g
        dhv = da_ref[...] + db_ref[...]
        dn = dhv * (1.0 + v_ref[3:4, :])
        dxh = dn * g
        dxm = dy_ref[...] + r * (dxh - xh * jnp.mean(dxh * xh, axis=-1, keepdims=True))
        dxm_ref[...] = dxm
        dmo_ref[...] = (dxm * v_ref[0:1, :]).astype(BF16)
        s0 = jnp.sum(dhv, axis=0, keepdims=True)
        s1 = jnp.sum(dhv * nrm, axis=0, keepdims=True)
        s2 = jnp.sum(dn * xh, axis=0, keepdims=True)
        s3 = jnp.sum(dxm * mo_ref[...], axis=0, keepdims=True)
        rows = lax.broadcasted_iota(jnp.int32, (8, D), 0)
        upd = jnp.where(rows == 0, s0, jnp.where(rows == 1, s1, jnp.where(rows == 2, s2,
              jnp.where(rows == 3, s3, 0.0))))

        @pl.when(i == 0)
        def _():
            cs_ref[...] = jnp.zeros_like(cs_ref)

        cs_ref[...] += upd

    blk = pl.BlockSpec((tb, D), lambda i: (i, 0))
    vec = pl.BlockSpec((8, D), lambda i: (0, 0))
    return pl.pallas_call(
        body, name="resid_rms2_bwd", grid=(L // tb,),
        in_specs=[blk, blk, blk, blk, blk, vec],
        out_specs=[blk, blk, vec],
        out_shape=[jax.ShapeDtypeStruct((L, D), F32), jax.ShapeDtypeStruct((L, D), BF16),
                   jax.ShapeDtypeStruct((8, D), F32)],
        compiler_params=_params(("arbitrary",)),
    )(xmid, dh_a, dh_b, dy, mo, vecs)


def _loss_head(xmid, f, g2, target):
    L, D = xmid.shape
    tb = _pick(L, (256, 128, 64))

    def body(xm_ref, f_ref, g_ref, t_ref, dy_ref, df_ref, s_ref):
        i = pl.program_id(0)
        fv = f_ref[...]
        g = g_ref[...]
        err = xm_ref[...] + g * fv - t_ref[...]
        dy = err * (1.0 / D)
        dy_ref[...] = dy
        df_ref[...] = (dy * g).astype(BF16)
        s0 = jnp.sum(dy * fv, axis=0, keepdims=True)
        part = 0.5 * jnp.sum(jnp.mean(err * err, axis=-1, keepdims=True), axis=0, keepdims=True)
        rows = lax.broadcasted_iota(jnp.int32, (8, D), 0)
        upd = jnp.where(rows == 0, s0, jnp.where(rows == 1, part, 0.0))

        @pl.when(i == 0)
        def _():
            s_ref[...] = jnp.zeros_like(s_ref)

        s_ref[...] += upd

    blk = pl.BlockSpec((tb, D), lambda i: (i, 0))
    return pl.pallas_call(
        body, name="loss_head", grid=(L // tb,),
        in_specs=[blk, blk, pl.BlockSpec((1, D), lambda i: (0, 0)), blk],
        out_specs=[blk, blk, pl.BlockSpec((8, D), lambda i: (0, 0))],
        out_shape=[jax.ShapeDtypeStruct((L, D), F32), jax.ShapeDtypeStruct((L, D), BF16),
                   jax.ShapeDtypeStruct((8, D), F32)],
        compiler_params=_params(("arbitrary",)),
    )(xmid, f, g2, target)


def _gate_cols(D, off):
    tc = _pick(np.gcd(D, off), (512, 256, 128))
    return tc, off // tc


def _merge_fwd(za, zb, p, n_ctx, off_a, off_b):
    L, D = za.shape
    tb = _pick(n_ctx, (256, 128, 64, 32, 16))
    nctx = n_ctx // tb
    tc, oa = _gate_cols(D, off_a)
    _, ob = _gate_cols(D, off_b)
    if off_b % tc:
        raise ValueError("gate column offsets must share a column tile")
    ob = off_b // tc

    def body(za_ref, zb_ref, ga_ref, gb_ref, z_ref):
        z_ref[...] = (_sigmoid(ga_ref[...]) * za_ref[...] + _sigmoid(gb_ref[...]) * zb_ref[...]).astype(BF16)

    blk = pl.BlockSpec((tb, tc), lambda i, j: (i, j))
    return pl.pallas_call(
        body, name="merge_fwd", grid=(L // tb, D // tc),
        in_specs=[blk, blk, pl.BlockSpec((tb, tc), lambda i, j: (i + nctx, oa + j)),
                  pl.BlockSpec((tb, tc), lambda i, j: (i + nctx, ob + j))],
        out_specs=blk,
        out_shape=jax.ShapeDtypeStruct((L, D), BF16),
        compiler_params=_params(("parallel", "parallel")),
    )(za, zb, p, p)


def _merge_bwd(dz, za, zb, p, n_ctx, off_a, off_b):
    L, D = za.shape
    T = L + n_ctx
    tb = _pick(n_ctx, (256, 128, 64, 32, 16))
    nctx = n_ctx // tb
    tc = _gate_cols(D, off_a)[0]
    oa, ob = off_a // tc, off_b // tc

    def body(dz_ref, za_ref, zb_ref, ga_ref, gb_ref, dza_ref, dzb_ref, dga_ref, dgb_ref):
        i = pl.program_id(1)

        @pl.when(i < nctx)
        def _():
            dga_ref[...] = jnp.zeros_like(dga_ref)
            dgb_ref[...] = jnp.zeros_like(dgb_ref)

        @pl.when(i >= nctx)
        def _():
            dzv = dz_ref[...]
            sa = _sigmoid(ga_ref[...])
            sb = _sigmoid(gb_ref[...])
            dza_ref[...] = (dzv * sa).astype(BF16)
            dzb_ref[...] = (dzv * sb).astype(BF16)
            dga_ref[...] = (dzv * za_ref[...] * sa * (1.0 - sa)).astype(BF16)
            dgb_ref[...] = (dzv * zb_ref[...] * sb * (1.0 - sb)).astype(BF16)

    lat = pl.BlockSpec((tb, tc), lambda j, i: (jnp.maximum(i - nctx, 0), j))
    allr = pl.BlockSpec((tb, tc), lambda j, i: (i, j))
    return pl.pallas_call(
        body, name="merge_bwd", grid=(D // tc, T // tb),
        in_specs=[lat, lat, lat, pl.BlockSpec((tb, tc), lambda j, i: (i, oa + j)),
                  pl.BlockSpec((tb, tc), lambda j, i: (i, ob + j))],
        out_specs=[lat, lat, allr, allr],
        out_shape=[jax.ShapeDtypeStruct((L, D), BF16), jax.ShapeDtypeStruct((L, D), BF16),
                   jax.ShapeDtypeStruct((T, D), BF16), jax.ShapeDtypeStruct((T, D), BF16)],
        compiler_params=_params(("arbitrary", "arbitrary")),
    )(dz, za, zb, p, p)


def _shift_down(u, rows):
    return jnp.where(rows == 0, 0.0, pltpu.roll(u, 1, 0))


def _shift_up(u, rows):
    n = u.shape[0]
    return jnp.where(rows == n - 1, 0.0, pltpu.roll(u, n - 1, 0))


def _convgate_fwd(u1, u3, cw, cb):
    L, F = u1.shape
    tc = _pick(F, (256, 128))

    def body(u1_ref, u3_ref, w_ref, b_ref, a_ref):
        u = u1_ref[...]
        rows = lax.broadcasted_iota(jnp.int32, u.shape, 0)
        cv = _shift_down(u, rows) * w_ref[0:1, :] + u * w_ref[1:2, :] + _shift_up(u, rows) * w_ref[2:3, :] + b_ref[...]
        a_ref[...] = (cv * _sigmoid(cv) * u3_ref[...]).astype(BF16)

    blk = pl.BlockSpec((L, tc), lambda j: (0, j))
    return pl.pallas_call(
        body, name="convgate_fwd", grid=(F // tc,),
        in_specs=[blk, blk, pl.BlockSpec((8, tc), lambda j: (0, j)), pl.BlockSpec((1, tc), lambda j: (0, j))],
        out_specs=blk,
        out_shape=jax.ShapeDtypeStruct((L, F), BF16),
        compiler_params=_params(("parallel",)),
    )(u1, u3, cw, cb)


def _convgate_bwd(u1, u3, da, cw, cb):
    L, F = u1.shape
    tc = _pick(F, (256, 128))

    def body(u1_ref, u3_ref, da_ref, w_ref, b_ref, du1_ref, du3_ref, s_ref):
        u = u1_ref[...]
        rows = lax.broadcasted_iota(jnp.int32, u.shape, 0)
        um, up = _shift_down(u, rows), _shift_up(u, rows)
        w0, w1, w2 = w_ref[0:1, :], w_ref[1:2, :], w_ref[2:3, :]
        cv = um * w0 + u * w1 + up * w2 + b_ref[...]
        s = _sigmoid(cv)
        dav = da_ref[...]
        du3_ref[...] = (dav * cv * s).astype(BF16)
        dcv = dav * u3_ref[...] * (s * (1.0 + cv * (1.0 - s)))
        du1_ref[...] = (_shift_up(dcv, rows) * w0 + dcv * w1 + _shift_down(dcv, rows) * w2).astype(BF16)
        r8 = lax.broadcasted_iota(jnp.int32, (8, tc), 0)
        s0 = jnp.sum(dcv * um, axis=0, keepdims=True)
        s1 = jnp.sum(dcv * u, axis=0, keepdims=True)
        s2 = jnp.sum(dcv * up, axis=0, keepdims=True)
        s3 = jnp.sum(dcv, axis=0, keepdims=True)
        s_ref[...] = jnp.where(r8 == 0, s0, jnp.where(r8 == 1, s1, jnp.where(r8 == 2, s2,
                     jnp.where(r8 == 3, s3, 0.0))))

    blk = pl.BlockSpec((L, tc), lambda j: (0, j))
    v8 = pl.BlockSpec((8, tc), lambda j: (0, j))
    return pl.pallas_call(
        body, name="convgate_bwd", grid=(F // tc,),
        in_specs=[blk, blk, blk, v8, pl.BlockSpec((1, tc), lambda j: (0, j))],
        out_specs=[blk, blk, v8],
        out_shape=[jax.ShapeDtypeStruct((L, F), BF16), jax.ShapeDtypeStruct((L, F), BF16),
                   jax.ShapeDtypeStruct((8, F), F32)],
        compiler_params=_params(("parallel",)),
    )(u1, u3, da, cw, cb)


def _lower_bound(lbl_ref, d):
    l0, l1 = lbl_ref[d, 0:1, :], lbl_ref[d, 1:2, :]
    m = jnp.maximum(l0, l1)
    e0, e1 = jnp.exp(l0 - m), jnp.exp(l1 - m)
    return e0 / (e0 + e1)


def _tri(rev):
    t = lax.broadcasted_iota(jnp.int32, (CHUNK, CHUNK), 0)
    s = lax.broadcasted_iota(jnp.int32, (CHUNK, CHUNK), 1)
    return jnp.where((s >= t) if rev else (s <= t), 1.0, 0.0).astype(F32)


def _chunk_terms(z, lb, rev):
    sg = _sigmoid(z)
    f = lb + (1.0 - lb) * sg
    g = jnp.log(f)
    c = jnp.dot(_tri(rev), g, precision=HI, preferred_element_type=F32)
    tot = c[0:1, :] if rev else c[CHUNK - 1:CHUNK, :]
    return sg, f, 1.0 - f, c, tot


def _pair_decay(c, s, rev):
    t = lax.broadcasted_iota(jnp.int32, (CHUNK, 1), 0)
    later = (t <= s) if rev else (t >= s)
    return jnp.where(later, jnp.exp(jnp.minimum(c - c[s:s + 1, :], 0.0)), 0.0)


def _scan_chunk(i, n_ctx_chunks, n_chunks, rev):
    if not rev:
        return i
    return jnp.where(i < n_ctx_chunks, n_ctx_chunks - 1 - i, n_chunks + n_ctx_chunks - 1 - i)


def _rows(ci):
    return pl.ds(pl.multiple_of(ci * CHUNK, CHUNK), CHUNK)


def _hgrn_cols(HA):
    return HA // HEAD


def _hgrn_fwd(p, lbl, ng, n_ctx, HA):
    T = p.shape[0]
    L = T - n_ctx
    nh = _hgrn_cols(HA)
    nc, ncc = T // CHUNK, n_ctx // CHUNK

    def body(q_ref, zf_ref, zb_ref, v_ref, og_ref, lbl_ref, ng_ref, ya_ref, o_ref, oacc):
        for d, rev in ((0, False), (1, True)):
            z_ref = zb_ref if rev else zf_ref
            lb = _lower_bound(lbl_ref, d)

            def step(i, St, rev=rev, z_ref=z_ref, lb=lb, first=(d == 0)):
                ci = _scan_chunk(i, ncc, nc, rev)
                rows = _rows(ci)
                q, v = q_ref[rows, :], v_ref[rows, :]
                _, _, k, c, tot = _chunk_terms(z_ref[rows, :], lb, rev)
                ke = k * jnp.exp(tot - c)
                o = _dot_nt((q * jnp.exp(c)).astype(BF16), St.astype(BF16))
                for s in range(CHUNK):
                    E = _pair_decay(c, s, rev)
                    a = jnp.sum(q * E * k[s:s + 1, :], axis=1, keepdims=True)
                    o = o + a * v[s:s + 1, :]
                if first:
                    oacc[rows, :] = o
                else:
                    oacc[rows, :] += o
                return St * jnp.exp(tot) + _dot_tn(v.astype(BF16), ke.astype(BF16))

            lax.fori_loop(0, nc, step, jnp.zeros((HEAD, HEAD), F32))

        o = oacc[pl.ds(n_ctx, L), :]
        o_ref[...] = o
        r = lax.rsqrt(jnp.mean(o * o, axis=-1, keepdims=True) + EPS)
        og = og_ref[pl.ds(n_ctx, L), :]
        ya_ref[...] =(o * r * ng_ref[...] * (og * _sigmoid(og))).astype(BF16)

    cb = HA // HEAD
    col = lambda kk: pl.BlockSpec((T, HEAD), lambda h: (0, kk * cb + h))
    return pl.pallas_call(
        body, name="hgrn_fwd", grid=(nh,),
        in_specs=[col(0), col(1), col(2), col(3), col(4),
                  pl.BlockSpec((2, 2, HEAD), lambda h: (0, 0, h)), pl.BlockSpec((1, HEAD), lambda h: (0, 0))],
        out_specs=[pl.BlockSpec((L, HEAD), lambda h: (0, h)), pl.BlockSpec((L, HEAD), lambda h: (0, h))],
        out_shape=[jax.ShapeDtypeStruct((L, HA), BF16), jax.ShapeDtypeStruct((L, HA), F32)],
        scratch_shapes=[pltpu.VMEM((T, HEAD), F32)],
        compiler_params=_params(("parallel",)),
    )(p, p, p, p, p, lbl, ng)


def _hgrn_bwd(p, lbl, ng, o, dya, n_ctx, HA):
    T = p.shape[0]
    L = T - n_ctx
    nh = _hgrn_cols(HA)
    nc, ncc = T // CHUNK, n_ctx // CHUNK

    def body(q_ref, zf_ref, zb_ref, v_ref, og_ref, lbl_ref, ng_ref, o_ref, dya_ref,
             dq_ref, dzf_ref, dzb_ref, dv_ref, dog_ref, dlbl_ref, dng_ref, do_scr, st_scr, dq_acc, dv_acc):
        h = pl.program_id(0)
        ov = o_ref[...]
        r = lax.rsqrt(jnp.mean(ov * ov, axis=-1, keepdims=True) + EPS)
        oh = ov * r
        ogv = og_ref[pl.ds(n_ctx, L), :]
        sg_o = _sigmoid(ogv)
        dyv = dya_ref[...]
        ngv = ng_ref[...]
        dog_ref[pl.ds(0, n_ctx), :] = jnp.zeros((n_ctx, HEAD), BF16)
        dog_ref[pl.ds(n_ctx, L), :] = (dyv * oh * ngv * (sg_o * (1.0 + ogv * (1.0 - sg_o)))).astype(BF16)
        don = dyv * (ogv * sg_o)
        dng = jnp.sum(don * oh, axis=0, keepdims=True)
        doh = don * ngv
        do_scr[pl.ds(0, n_ctx), :] = jnp.zeros((n_ctx, HEAD), F32)
        do_scr[pl.ds(n_ctx, L), :] = r * (doh - oh * jnp.mean(doh * oh, axis=-1, keepdims=True))

        @pl.when(h == 0)
        def _():
            dng_ref[...] = jnp.zeros_like(dng_ref)

        dng_ref[0:1, :] += dng

        t16 = lax.broadcasted_iota(jnp.int32, (CHUNK, HEAD), 0)
        for d, rev in ((0, False), (1, True)):
            z_ref = zb_ref if rev else zf_ref
            dz_ref = dzb_ref if rev else dzf_ref
            lb = _lower_bound(lbl_ref, d)

            def fwd_step(i, St, rev=rev, z_ref=z_ref, lb=lb):
                ci = _scan_chunk(i, ncc, nc, rev)
                rows = _rows(ci)
                st_scr[ci] = St.astype(BF16)
                _, _, k, c, tot = _chunk_terms(z_ref[rows, :], lb, rev)
                ke = k * jnp.exp(tot - c)
                return St * jnp.exp(tot) + _dot_tn(v_ref[rows, :].astype(BF16), ke.astype(BF16))

            lax.fori_loop(0, nc, fwd_step, jnp.zeros((HEAD, HEAD), F32))

            def bwd_step(ii, carry, rev=rev, z_ref=z_ref, dz_ref=dz_ref, lb=lb, first=(d == 0)):
                dSt, dlb = carry
                i = nc - 1 - ii
                ci = _scan_chunk(i, ncc, nc, rev)
                rows = _rows(ci)
                q, v, do = q_ref[rows, :], v_ref[rows, :], do_scr[rows, :]
                sg, f, k, c, tot = _chunk_terms(z_ref[rows, :], lb, rev)
                ec, et = jnp.exp(c), jnp.exp(tot - c)
                qe, ke = q * ec, k * et
                St = st_scr[ci]
                dSb = dSt.astype(BF16)
                do_b = do.astype(BF16)
                dq = _dot(do_b, St) * ec
                dk = _dot(v.astype(BF16), dSb) * et
                dv = _dot_nt(ke.astype(BF16), dSb)
                dtot = (jnp.sum(St.astype(F32) * dSt, axis=0, keepdims=True) * jnp.exp(tot)
                        + jnp.sum(k * dk, axis=0, keepdims=True))
                for s in range(CHUNK):
                    E = _pair_decay(c, s, rev)
                    XE = E * k[s:s + 1, :]
                    a = jnp.sum(q * XE, axis=1, keepdims=True)
                    da = jnp.sum(do * v[s:s + 1, :], axis=1, keepdims=True)
                    dq = dq + da * XE
                    dk_row = jnp.sum(da * q * E, axis=0, keepdims=True)
                    dv_row = jnp.sum(a * do, axis=0, keepdims=True)
                    dk = dk + jnp.where(t16 == s, dk_row, 0.0)
                    dv = dv + jnp.where(t16 == s, dv_row, 0.0)
                dcum = q * dq - k * dk
                dg = jnp.dot(_tri(not rev), dcum, precision=HI, preferred_element_type=F32) + dtot
                df = dg / f - dk
                dz_ref[rows, :] = (df * (1.0 - lb) * sg * (1.0 - sg)).astype(BF16)
                if first:
                    dq_acc[rows, :] = dq
                    dv_acc[rows, :] = dv
                else:
                    dq_ref[rows, :] = (dq_acc[rows, :] + dq).astype(BF16)
                    dv_ref[rows, :] = (dv_acc[rows, :] + dv).astype(BF16)
                dSt_new = dSt * jnp.exp(tot) + _dot_tn(do_b, qe.astype(BF16))
                dlb_new = dlb + jnp.sum(df * (1.0 - sg), axis=0, keepdims=True)
                return dSt_new, dlb_new

            _, dlb = lax.fori_loop(0, nc, bwd_step, (jnp.zeros((HEAD, HEAD), F32), jnp.zeros((1, HEAD), F32)))
            dl0 = dlb * lb * (1.0 - lb)
            dlbl_ref[d, 0:1, :] = dl0
            dlbl_ref[d, 1:2, :] = -dl0

    cb = HA // HEAD
    col = lambda kk: pl.BlockSpec((T, HEAD), lambda h: (0, kk * cb + h))
    tcol = pl.BlockSpec((T, HEAD), lambda h: (0, h))
    lcol = pl.BlockSpec((L, HEAD), lambda h: (0, h))
    outs = pl.pallas_call(
        body, name="hgrn_bwd", grid=(nh,),
        in_specs=[col(0), col(1), col(2), col(3), col(4),
                  pl.BlockSpec((2, 2, HEAD), lambda h: (0, 0, h)), pl.BlockSpec((1, HEAD), lambda h: (0, 0)),
                  lcol, lcol],
        out_specs=[tcol, tcol, tcol, tcol, tcol, pl.BlockSpec((2, 2, HEAD), lambda h: (0, 0, h)),
                   pl.BlockSpec((8, HEAD), lambda h: (0, 0))],
        out_shape=[jax.ShapeDtypeStruct((T, HA), BF16)] * 5 + [jax.ShapeDtypeStruct((2, 2, HA), F32),
                                                               jax.ShapeDtypeStruct((8, HEAD), F32)],
        scratch_shapes=[pltpu.VMEM((T, HEAD), F32), pltpu.VMEM((nc, HEAD, HEAD), BF16),
                        pltpu.VMEM((T, HEAD), F32), pltpu.VMEM((T, HEAD), F32)],
        compiler_params=_params(("arbitrary",)),
    )(p, p, p, p, p, lbl, ng, o, dya)
    return outs


def _swap_halves(t, lane):
    q = HEAD // 4
    return jnp.where((lane % (2 * q)) < q, pltpu.roll(t, HEAD - q, 1), pltpu.roll(t, q, 1))


def _qk_norm(t, g):
    r = lax.rsqrt(jnp.mean(t * t, axis=-1, keepdims=True) + EPS)
    return t * r, r


def _rope(t, cos, sin, lane):
    return t * cos + _swap_halves(t, lane) * sin


def _qk_norm_bwd(dy, th, r, g):
    dth = dy * g
    return r * (dth - th * jnp.mean(dth * th, axis=-1, keepdims=True)), jnp.sum(dy * th, axis=0, keepdims=True)


def _rope_bwd(dy, cos, sin, lane):
    return dy * cos + _swap_halves(dy * sin, lane)


def _na_geometry(L):
    n_rows = L // GRID_W
    kr = min(WIN_R, n_rows)
    return n_rows, kr


def _na_prep(q_ref, k_ref, v_ref, gq_ref, gk_ref, cos_ref, sin_ref, qs, ks, vs, n_ctx, L):
    lane = lax.broadcasted_iota(jnp.int32, (L, HEAD), 1)
    cos, sin = cos_ref[...], sin_ref[...]
    qh, _ = _qk_norm(q_ref[pl.ds(n_ctx, L), :], None)
    qs[...] = _rope(qh * gq_ref[...], cos, sin, lane).astype(BF16)
    kh, _ = _qk_norm(k_ref[pl.ds(n_ctx, L), :], None)
    ks[pl.ds(n_ctx, L), :] = _rope(kh * gk_ref[...], cos, sin, lane).astype(BF16)
    kc, _ = _qk_norm(k_ref[pl.ds(0, n_ctx), :], None)
    ks[pl.ds(0, n_ctx), :] = (kc * gk_ref[...]).astype(BF16)
    vs[...] = v_ref[...].astype(BF16)


def _na_scores(r, qs, ks, bias_ref, n_ctx, n_rows, kr):
    scale = HEAD ** -0.5
    r0 = jnp.clip(r - WIN_R // 2, 0, n_rows - kr)
    qrows = pl.ds(pl.multiple_of(r * GRID_W, GRID_W), GRID_W)
    krows = pl.ds(pl.multiple_of(n_ctx + r0 * GRID_W, GRID_W), kr * GRID_W)
    qv = qs[qrows, :]
    sb = _dot_nt(qv, ks[krows, :]) * scale
    b0 = r0 - r + (WIN_R - 1)
    sb = sb + jnp.concatenate([bias_ref[0, b0 + 2 * jj] for jj in range(kr // 2)], axis=1)
    sc = _dot_nt(qv, ks[pl.ds(0, n_ctx), :]) * scale
    m = jnp.maximum(jnp.max(sb, axis=1, keepdims=True), jnp.max(sc, axis=1, keepdims=True))
    eb, ec = jnp.exp(sb - m), jnp.exp(sc - m)
    inv = 1.0 / (jnp.sum(eb, axis=1, keepdims=True) + jnp.sum(ec, axis=1, keepdims=True))
    return eb * inv, ec * inv, qrows, krows, b0


def _na_fwd(p, bias, gq, gk, cos, sin, n_ctx, off, HB):
    T = p.shape[0]
    L = T - n_ctx
    nh = HB // HEAD
    n_rows, kr = _na_geometry(L)
    ob = off // HEAD

    def body(q_ref, k_ref, v_ref, bias_ref, gq_ref, gk_ref, cos_ref, sin_ref, y_ref, qs, ks, vs):
        _na_prep(q_ref, k_ref, v_ref, gq_ref, gk_ref, cos_ref, sin_ref, qs, ks, vs, n_ctx, L)

        def step(r, carry):
            pb, pc, qrows, krows, _ = _na_scores(r, qs, ks, bias_ref, n_ctx, n_rows, kr)
            y = _dot(pb.astype(BF16), vs[krows, :]) + _dot(pc.astype(BF16), vs[pl.ds(0, n_ctx), :])
            y_ref[qrows, :] = y.astype(BF16)
            return carry

        lax.fori_loop(0, n_rows, step, 0)

    col = lambda kk: pl.BlockSpec((T, HEAD), lambda h: (0, ob + kk * nh + h))
    vec = pl.BlockSpec((1, HEAD), lambda h: (0, 0))
    tab = pl.BlockSpec((L, HEAD), lambda h: (0, 0))
    return pl.pallas_call(
        body, name="na_fwd", grid=(nh,),
        in_specs=[col(0), col(1), col(2), pl.BlockSpec((1,) + bias.shape[1:], lambda h: (h, 0, 0, 0)),
                  vec, vec, tab, tab],
        out_specs=pl.BlockSpec((L, HEAD), lambda h: (0, h)),
        out_shape=jax.ShapeDtypeStruct((L, HB), BF16),
        scratch_shapes=[pltpu.VMEM((L, HEAD), BF16), pltpu.VMEM((T, HEAD), BF16), pltpu.VMEM((T, HEAD), BF16)],
        compiler_params=_params(("parallel",)),
    )(p, p, p, bias, gq, gk, cos, sin)


def _na_bwd(p, bias, gq, gk, cos, sin, dyb, n_ctx, off, HB):
    T = p.shape[0]
    L = T - n_ctx
    nh = HB // HEAD
    n_rows, kr = _na_geometry(L)
    ob = off // HEAD
    scale = HEAD ** -0.5

    def body(q_ref, k_ref, v_ref, bias_ref, gq_ref, gk_ref, cos_ref, sin_ref, dy_ref,
             dq_ref, dk_ref, dv_ref, dbias_ref, dg_ref, qs, ks, vs, dqa, dka, dva):
        h = pl.program_id(0)
        _na_prep(q_ref, k_ref, v_ref, gq_ref, gk_ref, cos_ref, sin_ref, qs, ks, vs, n_ctx, L)
        dka[...] = jnp.zeros_like(dka)
        dva[...] = jnp.zeros_like(dva)
        dbias_ref[...] = jnp.zeros_like(dbias_ref)

        def step(r, carry):
            pb, pc, qrows, krows, b0 = _na_scores(r, qs, ks, bias_ref, n_ctx, n_rows, kr)
            crows = pl.ds(0, n_ctx)
            do = dy_ref[qrows, :]
            qv = qs[qrows, :]
            dpb = _dot_nt(do, vs[krows, :])
            dpc = _dot_nt(do, vs[crows, :])
            delta = jnp.sum(pb * dpb, axis=1, keepdims=True) + jnp.sum(pc * dpc, axis=1, keepdims=True)
            dsb = pb * (dpb - delta)
            dsc = pc * (dpc - delta)
            for jj in range(kr // 2):
                dbias_ref[0, b0 + 2 * jj] += dsb[:, jj * 2 * GRID_W:(jj + 1) * 2 * GRID_W]
            dsb_b, dsc_b = dsb.astype(BF16), dsc.astype(BF16)
            dqa[qrows, :] = (_dot(dsb_b, ks[krows, :]) + _dot(dsc_b, ks[crows, :])) * scale
            dka[krows, :] += _dot_tn(dsb_b, qv) * scale
            dka[crows, :] += _dot_tn(dsc_b, qv) * scale
            dva[krows, :] += _dot_tn(pb.astype(BF16), do)
            dva[crows, :] += _dot_tn(pc.astype(BF16), do)
            return carry

        lax.fori_loop(0, n_rows, step, 0)

        lane = lax.broadcasted_iota(jnp.int32, (L, HEAD), 1)
        cos, sin = cos_ref[...], sin_ref[...]
        lat, ctx = pl.ds(n_ctx, L), pl.ds(0, n_ctx)
        gqv, gkv = gq_ref[...], gk_ref[...]
        qh, rq = _qk_norm(q_ref[lat, :], None)
        dq, dgq = _qk_norm_bwd(_rope_bwd(dqa[...], cos, sin, lane), qh, rq, gqv)
        dq_ref[ctx, :] = jnp.zeros((n_ctx, HEAD), BF16)
        dq_ref[lat, :] = dq.astype(BF16)
        kh, rk = _qk_norm(k_ref[lat, :], None)
        dk, dgk = _qk_norm_bwd(_rope_bwd(dka[lat, :], cos, sin, lane), kh, rk, gkv)
        dk_ref[lat, :] = dk.astype(BF16)
        kch, rkc = _qk_norm(k_ref[ctx, :], None)
        dkc, dgkc = _qk_norm_bwd(dka[ctx, :], kch, rkc, gkv)
        dk_ref[ctx, :] = dkc.astype(BF16)
        dv_ref[...] = dva[...].astype(BF16)

        @pl.when(h == 0)
        def _():
            dg_ref[...] = jnp.zeros_like(dg_ref)

        dg_ref[0:1, :] += dgq
        dg_ref[1:2, :] += dgk + dgkc

    col = lambda kk: pl.BlockSpec((T, HEAD), lambda h: (0, ob + kk * nh + h))
    vec = pl.BlockSpec((1, HEAD), lambda h: (0, 0))
    tab = pl.BlockSpec((L, HEAD), lambda h: (0, 0))
    tcol = pl.BlockSpec((T, HEAD), lambda h: (0, h))
    bspec = pl.BlockSpec((1,) + bias.shape[1:], lambda h: (h, 0, 0, 0))
    return pl.pallas_call(
        body, name="na_bwd", grid=(nh,),
        in_specs=[col(0), col(1), col(2), bspec, vec, vec, tab, tab, pl.BlockSpec((L, HEAD), lambda h: (0, h))],
        out_specs=[tcol, tcol, tcol, bspec, pl.BlockSpec((8, HEAD), lambda h: (0, 0))],
        out_shape=[jax.ShapeDtypeStruct((T, HB), BF16)] * 3 + [jax.ShapeDtypeStruct(bias.shape, F32),
                                                               jax.ShapeDtypeStruct((8, HEAD), F32)],
        scratch_shapes=[pltpu.VMEM((L, HEAD), BF16), pltpu.VMEM((T, HEAD), BF16), pltpu.VMEM((T, HEAD), BF16),
                        pltpu.VMEM((L, HEAD), F32), pltpu.VMEM((T, HEAD), F32), pltpu.VMEM((T, HEAD), F32)],
        compiler_params=_params(("arbitrary",)),
    )(p, p, p, bias, gq, gk, cos, sin, dyb)


def _bias_tables():
    w = np.arange(GRID_W)
    col_start = np.clip(w - WIN_C // 2, 0, GRID_W - WIN_C)
    col_in = (w[None, :] >= col_start[:, None]) & (w[None, :] < col_start[:, None] + WIN_C)
    dc = np.clip(w[None, :] - w[:, None], -(WIN_C - 1), WIN_C - 1) + WIN_C - 1
    n_pair = 2 * WIN_R
    ridx = np.zeros((n_pair, GRID_W, 2 * GRID_W), np.int32)
    cidx = np.zeros((n_pair, GRID_W, 2 * GRID_W), np.int32)
    valid = np.zeros((n_pair, GRID_W, 2 * GRID_W), bool)
    for i in range(n_pair):
        for half in range(2):
            row = i + half
            sl = slice(half * GRID_W, (half + 1) * GRID_W)
            ridx[i, :, sl] = min(row, 2 * WIN_R - 2)
            cidx[i, :, sl] = dc
            valid[i, :, sl] = col_in & (row <= 2 * WIN_R - 2)
    return ridx, cidx, valid


def _bias_onehot():
    _, cidx, valid = _bias_tables()
    K = GRID_W * 2 * GRID_W
    oh = np.zeros((K, 128), np.float32)
    neg = np.full((1, K), NEG, np.float32)
    for cq in range(GRID_W):
        for ll in range(2 * GRID_W):
            if valid[0, cq, ll]:
                oh[cq * 2 * GRID_W + ll, (ll // GRID_W) * 64 + cidx[0, cq, ll]] = 1.0
                neg[0, cq * 2 * GRID_W + ll] = 0.0
    return oh, neg


def _expand_bias(table):
    H = table.shape[0]
    n_pair, n_dc = 2 * WIN_R, 2 * WIN_C - 1
    tp = jnp.pad(table, ((0, 0), (0, n_pair + 1 - table.shape[1]), (0, 64 - n_dc)))
    t2 = jnp.concatenate([tp[:, :n_pair], tp[:, 1:n_pair + 1]], axis=-1).reshape(H * n_pair, 128)
    oh, neg = _bias_onehot()

    def body(t_ref, oh_ref, neg_ref, o_ref):
        o_ref[...] = lax.dot_general(t_ref[...], oh_ref[...], (((1,), (1,)), ((), ())), precision=HI,
                                     preferred_element_type=F32) + neg_ref[...]

    out = pl.pallas_call(body, name="bias_expand", out_shape=jax.ShapeDtypeStruct((H * n_pair, oh.shape[0]), F32),
                         compiler_params=_params())(t2, jnp.asarray(oh), jnp.asarray(neg))
    return out.reshape(H, n_pair, GRID_W, 2 * GRID_W)


def _bias_grad(dbias):
    H = dbias.shape[0]
    n_pair, n_dc = 2 * WIN_R, 2 * WIN_C - 1
    K = GRID_W * 2 * GRID_W
    oh, _ = _bias_onehot()
    flat = dbias.reshape(H * n_pair, K)

    def body(d_ref, oh_ref, o_ref):
        o_ref[...] = jnp.dot(d_ref[...], oh_ref[...], precision=HI, preferred_element_type=F32)

    g = pl.pallas_call(body, name="bias_grad", out_shape=jax.ShapeDtypeStruct((H * n_pair, 128), F32),
                       compiler_params=_params())(flat, jnp.asarray(oh))
    g = g.reshape(H, n_pair, 128)
    left, right = g[:, :, :n_dc], g[:, :, 64:64 + n_dc]
    out = left[:, :n_pair - 1]
    return out.at[:, 1:].add(right[:, :n_pair - 2])


def _rope_tables(L):
    pos = np.arange(L)
    row = (pos // GRID_W).astype(np.float32)
    colp = (pos % GRID_W).astype(np.float32)
    half = HEAD // 2
    nf = half // 2
    inv = (ROPE_THETA ** (-np.arange(nf, dtype=np.float32) / nf)).astype(np.float32)

    def tabs(pv):
        ang = pv[:, None] * inv[None, :]
        c, s = np.cos(ang), np.sin(ang)
        return np.concatenate([c, c], axis=1), np.concatenate([-s, s], axis=1)

    cr, sr = tabs(row)
    cc, sc = tabs(colp)
    return (jnp.asarray(np.concatenate([cr, cc], axis=1), F32), jnp.asarray(np.concatenate([sr, sc], axis=1), F32))


def _adamw(w, g, m, v, name):
    R, C = w.shape
    tr = _row_tile(R, C)
    c1 = 1.0 - ADAM_B1 ** ADAM_STEP
    c2 = 1.0 - ADAM_B2 ** ADAM_STEP

    def body(w_ref, g_ref, m_ref, v_ref, d_ref, mo_ref, vo_ref):
        gv = g_ref[...]
        mn = ADAM_B1 * m_ref[...] + (1.0 - ADAM_B1) * gv
        vn = ADAM_B2 * v_ref[...] + (1.0 - ADAM_B2) * (gv * gv)
        mo_ref[...] = mn
        vo_ref[...] = vn
        d_ref[...] = -ADAM_LR * ((mn / c1) / (jnp.sqrt(vn / c2) + ADAM_EPS) + ADAM_WD * w_ref[...])

    blk = pl.BlockSpec((tr, C), lambda i: (i, 0))
    return pl.pallas_call(
        body, name=name, grid=(R // tr,),
        in_specs=[blk] * 4, out_specs=[blk] * 3,
        out_shape=[jax.ShapeDtypeStruct((R, C), F32)] * 3,
        compiler_params=_params(("parallel",)),
    )(w, g, m, v)


PACK_W = 1024


def _pack(parts):
    flat, offs, pos = [], [], 0
    for a in parts:
        n = a.size
        padn = -n % PACK_W
        flat.append(jnp.pad(a.reshape(-1).astype(F32), (0, padn)))
        offs.append((pos, n, a.shape))
        pos += n + padn
    tail = -pos % (8 * PACK_W)
    if tail:
        flat.append(jnp.zeros((tail,), F32))
    return jnp.concatenate(flat).reshape(-1, PACK_W), offs


def _unpack(buf, offs, i):
    pos, n, shape = offs[i]
    return buf.reshape(buf.shape[:-2] + (-1,))[..., pos:pos + n].reshape(buf.shape[:-2] + shape)


def kernel(x, c, ctx, c_ctx, ada_w, ada_b, norm1_g, norm2_g, w_in, hgrn_lb_logits, hgrn_norm_g, na_q_norm_g, na_k_norm_g, na_rel_bias, w_branch_a, w_branch_b, w_out, ffn_w1, ffn_w3, ffn_conv_w, ffn_conv_b, ffn_w2, loss_target, m_c_ctx, m_ada_w, m_ada_b, m_norm1_g, m_norm2_g, m_w_in, m_hgrn_lb_logits, m_hgrn_norm_g, m_na_q_norm_g, m_na_k_norm_g, m_na_rel_bias, m_w_branch_a, m_w_branch_b, m_w_out, m_ffn_w1, m_ffn_w3, m_ffn_conv_w, m_ffn_conv_b, m_ffn_w2, v_c_ctx, v_ada_w, v_ada_b, v_norm1_g, v_norm2_g, v_w_in, v_hgrn_lb_logits, v_hgrn_norm_g, v_na_q_norm_g, v_na_k_norm_g, v_na_rel_bias, v_w_branch_a, v_w_branch_b, v_w_out, v_ffn_w1, v_ffn_w3, v_ffn_conv_w, v_ffn_conv_b, v_ffn_w2):
    weights = dict(c_ctx=c_ctx, ada_w=ada_w, ada_b=ada_b, norm1_g=norm1_g, norm2_g=norm2_g, w_in=w_in,
                   hgrn_lb_logits=hgrn_lb_logits, hgrn_norm_g=hgrn_norm_g, na_q_norm_g=na_q_norm_g,
                   na_k_norm_g=na_k_norm_g, na_rel_bias=na_rel_bias, w_branch_a=w_branch_a, w_branch_b=w_branch_b,
                   w_out=w_out, ffn_w1=ffn_w1, ffn_w3=ffn_w3, ffn_conv_w=ffn_conv_w, ffn_conv_b=ffn_conv_b,
                   ffn_w2=ffn_w2)
    moms = dict(c_ctx=(m_c_ctx, v_c_ctx), ada_w=(m_ada_w, v_ada_w), ada_b=(m_ada_b, v_ada_b),
                norm1_g=(m_norm1_g, v_norm1_g), norm2_g=(m_norm2_g, v_norm2_g), w_in=(m_w_in, v_w_in),
                hgrn_lb_logits=(m_hgrn_lb_logits, v_hgrn_lb_logits), hgrn_norm_g=(m_hgrn_norm_g, v_hgrn_norm_g),
                na_q_norm_g=(m_na_q_norm_g, v_na_q_norm_g), na_k_norm_g=(m_na_k_norm_g, v_na_k_norm_g),
                na_rel_bias=(m_na_rel_bias, v_na_rel_bias), w_branch_a=(m_w_branch_a, v_w_branch_a),
                w_branch_b=(m_w_branch_b, v_w_branch_b), w_out=(m_w_out, v_w_out), ffn_w1=(m_ffn_w1, v_ffn_w1),
                ffn_w3=(m_ffn_w3, v_ffn_w3), ffn_conv_w=(m_ffn_conv_w, v_ffn_conv_w),
                ffn_conv_b=(m_ffn_conv_b, v_ffn_conv_b), ffn_w2=(m_ffn_w2, v_ffn_w2))
    order = list(weights)

    L, D = x.shape[1], x.shape[2]
    N = ctx.shape[1]
    T = N + L
    HA = w_branch_a.shape[1]
    HB = w_branch_b.shape[1]
    F = ffn_conv_b.shape[1]
    IN = 5 * HA + 3 * HB + 2 * D
    n_ada = ada_w.shape[2]
    ix, iy, ic = _pos()
    chip = 2 * ix + iy
    dev = 2 * chip + ic

    pk0, offs0 = _pack([c[0], hgrn_lb_logits, ffn_conv_w[0]])
    g0 = _allgather8(pk0, "gather_small0")
    c_all = _unpack(g0, offs0, 0)
    lbl_parts = _unpack(g0, offs0, 1)
    lbl = jnp.concatenate([lbl_parts[2 * j] for j in range(N_CHIP)], axis=-1)
    cw_parts = _unpack(g0, offs0, 2)
    cw = jnp.concatenate([cw_parts[2 * j] for j in range(N_CHIP)], axis=-1)
    cw8 = jnp.pad(cw, ((0, 5), (0, 0)))

    cs = jnp.concatenate([c_all, c_ctx[None, :], jnp.zeros((7, D), F32)], axis=0)
    ada_b_mine = lax.dynamic_slice(ada_b, (0, chip * n_ada), (1, n_ada))
    mod_mine = _ada_fwd(cs, ada_w[0], ada_b_mine)
    gm = _allgather8(mod_mine, "gather_mod")
    mod = jnp.concatenate([gm[2 * j] for j in range(N_CHIP)], axis=-1)
    mod_l = lax.dynamic_slice(mod, (dev, 0), (1, N_MOD * D)).reshape(N_MOD, D)
    mod_c = mod[8].reshape(N_MOD, D)
    sh1, sc1, g1, sh2, sc2, g2 = [mod_l[i:i + 1] for i in range(N_MOD)]
    shift1 = jnp.concatenate([mod_c[0:1], sh1], axis=0)
    scale1 = jnp.concatenate([mod_c[1:2], sc1], axis=0)

    shards = [w_in[0], w_branch_a[0], w_branch_b[0], w_out[0], ffn_w1[0], ffn_w3[0], ffn_w2[0]]
    names = ["w_in", "w_a", "w_b", "w_out", "w1", "w3", "w2"]
    shards_bf = [_cast_bf16_slot(s, "cast_" + nm) for s, nm in zip(shards, names)]
    Win, Wa, Wb, Wo, W1, W3, W2 = _gather_shards(shards_bf, "gather_weights")
    Wo = Wo.reshape(1, D, D)
    W2 = W2.reshape(1, F, D)

    xall = jnp.concatenate([ctx[0], x[0]], axis=0)
    h_all = _rms1_fwd(xall, norm1_g, shift1, scale1, N)
    p = _mm_nn(h_all, Win, F32, "mm_p")
    y_a, o_a = _hgrn_fwd(p, lbl, hgrn_norm_g, N, HA)
    bias = _expand_bias(na_rel_bias[0])
    cos, sin = _rope_tables(L)
    off_na = 5 * HA
    y_b = _na_fwd(p, bias, na_q_norm_g, na_k_norm_g, cos, sin, N, off_na, HB)
    za = _mm_nn(y_a, Wa, F32, "mm_za")
    zb = _mm_nn(y_b, Wb, F32, "mm_zb")
    off_ga, off_gb = 5 * HA + 3 * HB, 5 * HA + 3 * HB + D
    z = _merge_fwd(za, zb, p, N, off_ga, off_gb)
    mo = _mm_nn(z, Wo, F32, "mm_mo")
    vec2 = jnp.concatenate([g1, norm2_g, sh2, sc2, jnp.zeros((4, D), F32)], axis=0)
    x_mid, h2 = _resid_rms2_fwd(x[0], mo, vec2)
    u1 = _mm_nn(h2, W1, F32, "mm_u1")
    u3 = _mm_nn(h2, W3, F32, "mm_u3")
    a = _convgate_fwd(u1, u3, cw8, ffn_conv_b)
    f = _mm_nn(a, W2, F32, "mm_f")
    dy, df, s_loss = _loss_head(x_mid, f, g2, loss_target[0])
    loss = lax.psum(s_loss[1, 0], ("x", "y", "c"))
    d_g2 = s_loss[0:1]

    gW2 = _mm_tn(a, df, 1, "mm_gw2").reshape(N_CHIP, F // N_CHIP, D)
    da = _mm_nt(df, W2, F32, "mm_da")
    du1, du3, s_conv = _convgate_bwd(u1, u3, da, cw8, ffn_conv_b)
    gW1 = _mm_tn(h2, du1, N_CHIP, "mm_gw1")
    gW3 = _mm_tn(h2, du3, N_CHIP, "mm_gw3")
    dh2a = _mm_nt(du1, W1, F32, "mm_dh2a")
    dh2b = _mm_nt(du3, W3, F32, "mm_dh2b")
    dxm, dmo, s_rms2 = _resid_rms2_bwd(x_mid, dh2a, dh2b, dy, mo, vec2)
    gWo = _mm_tn(z, dmo, 1, "mm_gwo").reshape(N_CHIP, D // N_CHIP, D)
    dz = _mm_nt(dmo, Wo, F32, "mm_dz")
    dza, dzb, dga, dgb = _merge_bwd(dz, za, zb, p, N, off_ga, off_gb)
    gWa = _mm_tn(y_a, dza, N_CHIP, "mm_gwa")
    gWb = _mm_tn(y_b, dzb, N_CHIP, "mm_gwb")
    dya = _mm_nt(dza, Wa, F32, "mm_dya")
    dyb = _mm_nt(dzb, Wb, BF16, "mm_dyb")
    dq_a, dzf, dzbk, di_a, dog, dlbl, s_ng = _hgrn_bwd(p, lbl, hgrn_norm_g, o_a, dya, N, HA)
    dq_n, dk_n, dv_n, dbias, s_qk = _na_bwd(p, bias, na_q_norm_g, na_k_norm_g, cos, sin, dyb, N, off_na, HB)
    dp = jnp.concatenate([dq_a, dzf, dzbk, di_a, dog, dq_n, dk_n, dv_n, dga, dgb], axis=1)
    gWin = _mm_tn(h_all, dp, N_CHIP, "mm_gwin")
    dh = _mm_nt(dp, Win, F32, "mm_dh")
    grad_x, s_rms1 = _rms1_bwd(xall, dh, dxm, norm1_g, scale1, N)
    d_table = _bias_grad(dbias)

    zD = jnp.zeros((1, D), F32)
    dmod_l = jnp.concatenate([s_rms1[2:3], s_rms1[3:4], s_rms2[3:4], s_rms2[0:1], s_rms2[1:2], d_g2], axis=0)
    dmod_c = jnp.concatenate([s_rms1[0:1], s_rms1[1:2], zD, zD, zD, zD], axis=0)
    pk1, offs1 = _pack([dmod_l, dmod_c, s_rms1[4], s_rms2[2], dlbl, s_ng[0], s_qk[0], s_qk[1], d_table,
                        s_conv[0:3], s_conv[3]])
    g1all = _allgather8(pk1, "gather_small1")
    tot1 = _sum8(g1all, "sum_small1")
    dmod_rows = _unpack(g1all, offs1, 0).reshape(N_DEV, N_MOD * D)
    dmod_c_tot = _unpack(tot1, offs1, 1).reshape(1, N_MOD * D)
    dmod16 = jnp.concatenate([dmod_rows, dmod_c_tot, jnp.zeros((7, N_MOD * D), F32)], axis=0)
    dmod16_mine = lax.dynamic_slice(dmod16, (0, chip * n_ada), (16, n_ada))
    g_ada_w, dact = _ada_bwd(cs, ada_w[0], dmod16_mine)
    pk2, offs2 = _pack([dact[8]])
    g2all = _allgather8(pk2, "gather_small2")
    dact_rows = _unpack(g2all, offs2, 0)
    dact_sel = jnp.concatenate([dact_rows[2 * j][None] for j in range(N_CHIP)] + [jnp.zeros((4, D), F32)], axis=0)

    grads = {}
    grads["ada_w"] = g_ada_w[None]
    grads["ada_b"] = (_unpack(tot1, offs1, 0) + _unpack(tot1, offs1, 1)).reshape(1, N_MOD * D)
    grads["norm1_g"] = _unpack(tot1, offs1, 2)[None]
    grads["norm2_g"] = _unpack(tot1, offs1, 3)[None]
    g_lbl = _unpack(tot1, offs1, 4)
    n_lb = HA // N_CHIP
    grads["hgrn_lb_logits"] = lax.dynamic_slice(g_lbl, (0, 0, chip * n_lb), (2, 2, n_lb))
    grads["hgrn_norm_g"] = _unpack(tot1, offs1, 5)[None]
    grads["na_q_norm_g"] = _unpack(tot1, offs1, 6)[None]
    grads["na_k_norm_g"] = _unpack(tot1, offs1, 7)[None]
    grads["na_rel_bias"] = _unpack(tot1, offs1, 8)[None]
    g_cw = _unpack(tot1, offs1, 9)
    n_f = F // N_CHIP
    grads["ffn_conv_w"] = lax.dynamic_slice(g_cw, (0, chip * n_f), (3, n_f))[None]
    grads["ffn_conv_b"] = _unpack(tot1, offs1, 10)[None]

    big = _reduce_scatter([gWin, gWa, gWb, gWo, gW1, gW3, gW2])
    for nm, g in zip(["w_in", "w_branch_a", "w_branch_b", "w_out", "ffn_w1", "ffn_w3", "ffn_w2"], big):
        grads[nm] = g[None]

    grads["c_ctx"] = _dsilu_rows(dact_sel, c_ctx[None, :], "grad_c_ctx")[0]

    big_names = ["ada_w", "w_in", "w_branch_a", "w_branch_b", "w_out", "ffn_w1", "ffn_w3", "ffn_w2"]
    small_names = [n for n in order if n not in big_names]
    delta, new_m, new_v = {}, {}, {}
    for nm in big_names:
        w2 = weights[nm][0]
        d_, m_, v_ = _adamw(w2, grads[nm][0], moms[nm][0][0], moms[nm][1][0], "adamw_" + nm)
        delta[nm], new_m[nm], new_v[nm] = d_[None], m_[None], v_[None]
    pw, offw = _pack([weights[n] for n in small_names])
    pg, _ = _pack([grads[n] for n in small_names])
    pm, _ = _pack([moms[n][0] for n in small_names])
    pv, _ = _pack([moms[n][1] for n in small_names])
    d_, m_, v_ = _adamw(pw, pg, pm, pv, "adamw_small")
    for i, nm in enumerate(small_names):
        delta[nm], new_m[nm], new_v[nm] = _unpack(d_, offw, i), _unpack(m_, offw, i), _unpack(v_, offw, i)

    return (loss, grad_x[None], *[grads[n] for n in order], *[delta[n] for n in order],
            *[new_m[n] for n in order], *[new_v[n] for n in order])


def _dsilu_rows(v, cv, name):
    D = v.shape[1]

    def body(v_ref, c_ref, o_ref):
        t = c_ref[...]
        s = _sigmoid(t)
        o_ref[...] = (((v_ref[0:1, :] + v_ref[1:2, :]) + v_ref[2:3, :]) + v_ref[3:4, :]) * (s * (1.0 + t * (1.0 - s)))

    return pl.pallas_call(body, name=name, out_shape=jax.ShapeDtypeStruct((1, D), F32),
                          compiler_params=_params())(v, cv)
```

```python
import functools

import numpy as np
import jax
import jax.numpy as jnp
from jax import lax
from jax.experimental import pallas as pl
from jax.experimental.pallas import tpu as pltpu

F32 = jnp.float32
BF16 = jnp.bfloat16
MESH = pl.DeviceIdType.MESH

HEAD = 128
GRID_W = 64
WIN_R = 8
WIN_C = 16
ROPE_THETA = 10000.0
EPS = 1e-6
N_MOD = 6
CHUNK = 16
ADAM_LR = 0.001
ADAM_B1 = 0.9
ADAM_B2 = 0.999
ADAM_EPS = 1e-08
ADAM_WD = 0.01
ADAM_STEP = 10
NEG = -1e30
VMEM_LIMIT = 56 * 1024 * 1024
N_DEV = 8
N_CHIP = 4
HI = lax.Precision.HIGHEST


def _pick(n, cands):
    for c in cands:
        if n % c == 0:
            return c
    return n


def _row_tile(rows, cols, target_bytes=1 << 20):
    want = max(16, target_bytes // (4 * cols))
    for t in (512, 256, 128, 64, 32, 16, 8):
        if t <= want and rows % t == 0:
            return t
    return rows


def _params(sem=None):
    return pltpu.CompilerParams(dimension_semantics=sem, vmem_limit_bytes=VMEM_LIMIT)


def _dot(a, b):
    return jnp.dot(a, b, preferred_element_type=F32)


def _dot_nt(a, b):
    return lax.dot_general(a, b, (((1,), (1,)), ((), ())), preferred_element_type=F32)


def _dot_tn(a, b):
    return lax.dot_general(a, b, (((0,), (0,)), ((), ())), preferred_element_type=F32)


def _sigmoid(x):
    return 1.0 / (1.0 + jnp.exp(-x))


def _col_tile(n):
    return n if n <= 1536 else _pick(n, (1024, 768, 512, 384, 256, 128))


def _mm_nn(x, w3, out_dtype, name):
    M, K = x.shape
    S, _, n = w3.shape
    tm = _pick(M, (768, 512, 256, 128, 64))
    tn = _col_tile(n)
    nb = n // tn

    def body(x_ref, w_ref, o_ref):
        o_ref[...] = _dot(x_ref[...].astype(BF16), w_ref[0]).astype(o_ref.dtype)

    return pl.pallas_call(
        body, name=name, grid=(M // tm, S * nb),
        in_specs=[pl.BlockSpec((tm, K), lambda i, j: (i, 0)),
                  pl.BlockSpec((1, K, tn), lambda i, j: (j // nb, 0, j % nb))],
        out_specs=pl.BlockSpec((tm, tn), lambda i, j: (i, j)),
        out_shape=jax.ShapeDtypeStruct((M, S * n), out_dtype),
        compiler_params=_params(("parallel", "parallel")),
    )(x, w3)


def _mm_nt(dy, w3, out_dtype, name):
    M = dy.shape[0]
    S, K, n = w3.shape
    tm = _pick(M, (768, 512, 256, 128, 64))
    tk = _pick(K, (512, 256, 128))
    tc = _col_tile(n)
    nb = n // tc
    nsteps = S * nb

    def body(dy_ref, w_ref, o_ref, acc_ref):
        s = pl.program_id(2)

        @pl.when(s == 0)
        def _():
            acc_ref[...] = jnp.zeros_like(acc_ref)

        acc_ref[...] += _dot_nt(dy_ref[...].astype(BF16), w_ref[0])

        @pl.when(s == nsteps - 1)
        def _():
            o_ref[...] = acc_ref[...].astype(o_ref.dtype)

    return pl.pallas_call(
        body, name=name, grid=(M // tm, K // tk, nsteps),
        in_specs=[pl.BlockSpec((tm, tc), lambda i, k, s: (i, s)),
                  pl.BlockSpec((1, tk, tc), lambda i, k, s: (s // nb, k, s % nb))],
        out_specs=pl.BlockSpec((tm, tk), lambda i, k, s: (i, k)),
        out_shape=jax.ShapeDtypeStruct((M, K), out_dtype),
        scratch_shapes=[pltpu.VMEM((tm, tk), F32)],
        compiler_params=_params(("parallel", "parallel", "arbitrary")),
    )(dy, w3)


def _mm_tn(x, dy, S, name):
    M, K = x.shape
    n = dy.shape[1] // S
    tk = _pick(K, (512, 256, 128))
    tn = _col_tile(n)
    nb = n // tn

    def body(x_ref, dy_ref, o_ref):
        o_ref[0] = _dot_tn(x_ref[...].astype(BF16), dy_ref[...].astype(BF16))

    return pl.pallas_call(
        body, name=name, grid=(S * nb, K // tk),
        in_specs=[pl.BlockSpec((M, tk), lambda j, k: (0, k)),
                  pl.BlockSpec((M, tn), lambda j, k: (0, j))],
        out_specs=pl.BlockSpec((1, tk, tn), lambda j, k: (j // nb, k, j % nb)),
        out_shape=jax.ShapeDtypeStruct((S, K, n), F32),
        compiler_params=_params(("parallel", "parallel")),
    )(x, dy)


def _chip_index():
    return (2 * lax.axis_index("x") + lax.axis_index("y")).astype(jnp.int32).reshape(1)


def _cast_bf16_slot(w, name):
    R, C = w.shape
    tr = _row_tile(R, C, 2 << 20)

    def body(j_ref, w_ref, o_ref):
        o_ref[0] = w_ref[...].astype(BF16)

    return pl.pallas_call(
        body, name=name,
        grid_spec=pltpu.PrefetchScalarGridSpec(
            num_scalar_prefetch=1, grid=(R // tr,),
            in_specs=[pl.BlockSpec((tr, C), lambda i, j_ref: (i, 0))],
            out_specs=pl.BlockSpec((1, tr, C), lambda i, j_ref: (j_ref[0], i, 0))),
        out_shape=jax.ShapeDtypeStruct((N_CHIP, R, C), BF16),
        compiler_params=_params(("parallel",)),
    )(_chip_index(), w)


def _pos():
    return lax.axis_index("x"), lax.axis_index("y"), lax.axis_index("c")


def _other_chips(x, y):
    return [(x, 1 - y), (1 - x, y), (1 - x, 1 - y)]


def _allgather8(v, name):
    R, C = v.shape

    def body(x_ref, out_ref, send_sems, recv_sems, local_sem):
        x, y, c = _pos()
        me, sibling = (x, y, c), (x, y, 1 - c)
        chips = _other_chips(x, y)

        def slot(px, py, pc):
            return out_ref.at[4 * px + 2 * py + pc]

        def copy(k, block, to, src=None):
            return pltpu.make_async_remote_copy(
                src_ref=slot(*block) if src is None else src, dst_ref=slot(*block),
                send_sem=send_sems.at[k], recv_sem=recv_sems.at[k], device_id=to, device_id_type=MESH)

        mine = pltpu.make_async_copy(x_ref, slot(*me), local_sem)
        mine.start()
        first = [copy(0, me, sibling, src=x_ref)]
        first += [copy(1 + j, me, (*chip, c), src=x_ref) for j, chip in enumerate(chips)]
        for cp in first:
            cp.start()
        passed = [copy(4 + j, (*chip, c), sibling) for j, chip in enumerate(chips)]
        for j, chip in enumerate(chips):
            copy(1 + j, (*chip, c), me).wait_recv()
            passed[j].start()
        copy(0, sibling, me).wait_recv()
        for j, chip in enumerate(chips):
            copy(4 + j, (*chip, 1 - c), me).wait_recv()
        for cp in first + passed:
            cp.wait_send()
        mine.wait()

    return pl.pallas_call(
        body, name=name,
        out_shape=jax.ShapeDtypeStruct((N_DEV, R, C), v.dtype),
        in_specs=[pl.BlockSpec(memory_space=pltpu.VMEM)],
        out_specs=pl.BlockSpec(memory_space=pltpu.VMEM),
        scratch_shapes=[pltpu.SemaphoreType.DMA((7,)), pltpu.SemaphoreType.DMA((7,)), pltpu.SemaphoreType.DMA],
        compiler_params=pltpu.CompilerParams(vmem_limit_bytes=VMEM_LIMIT),
    )(v)


def _gather_shards(bufs, name):
    n = len(bufs)

    def body(*refs):
        outs = refs[n:2 * n]
        send_sems, recv_sems = refs[2 * n:]
        x, y, c = _pos()
        sibling = (x, y, 1 - c)
        chips = _other_chips(x, y)

        def copy(t, k, chip, hc, to):
            h = outs[t].shape[1] // 2
            blk = outs[t].at[2 * chip[0] + chip[1], pl.ds(hc * h, h)]
            return pltpu.make_async_remote_copy(
                src_ref=blk, dst_ref=blk, send_sem=send_sems.at[6 * t + k], recv_sem=recv_sems.at[6 * t + k],
                device_id=to, device_id_type=MESH)

        sends = []
        for t in range(n):
            for k, chip in enumerate(chips):
                cp = copy(t, k, (x, y), c, (*chip, c))
                cp.start()
                sends.append(cp)
        for t in range(n):
            for k, chip in enumerate(chips):
                copy(t, k, chip, c, (x, y, c)).wait_recv()
                fwd = copy(t, 3 + k, chip, c, sibling)
                fwd.start()
                sends.append(fwd)
        for t in range(n):
            for k, chip in enumerate(chips):
                copy(t, 3 + k, chip, 1 - c, (x, y, c)).wait_recv()
        for cp in sends:
            cp.wait_send()

    any_spec = pl.BlockSpec(memory_space=pl.ANY)
    return pl.pallas_call(
        body, name=name,
        out_shape=[jax.ShapeDtypeStruct(b.shape, b.dtype) for b in bufs],
        in_specs=[any_spec] * n, out_specs=[any_spec] * n,
        input_output_aliases={t: t for t in range(n)},
        scratch_shapes=[pltpu.SemaphoreType.DMA((6 * n,)), pltpu.SemaphoreType.DMA((6 * n,))],
    )(*bufs)


def _pair_swap_halves(gs, name):
    n = len(gs)

    def body(*refs):
        ins, outs = refs[:n], refs[n:2 * n]
        send_sems, recv_sems = refs[2 * n:]
        x, y, c = _pos()
        cps = []
        for t in range(n):
            h = ins[t].shape[1] // 2
            cp = pltpu.make_async_remote_copy(
                src_ref=ins[t].at[:, pl.ds((1 - c) * h, h)], dst_ref=outs[t],
                send_sem=send_sems.at[t], recv_sem=recv_sems.at[t], device_id=(x, y, 1 - c), device_id_type=MESH)
            cp.start()
            cps.append(cp)
        for cp in cps:
            cp.wait()

    any_spec = pl.BlockSpec(memory_space=pl.ANY)
    return pl.pallas_call(
        body, name=name,
        out_shape=[jax.ShapeDtypeStruct((g.shape[0], g.shape[1] // 2, g.shape[2]), g.dtype) for g in gs],
        in_specs=[any_spec] * n, out_specs=[any_spec] * n,
        scratch_shapes=[pltpu.SemaphoreType.DMA((n,)), pltpu.SemaphoreType.DMA((n,))],
    )(*gs)


def _chip_scatter(ps, name):
    n = len(ps)

    def body(*refs):
        ins, outs = refs[:n], refs[n:2 * n]
        send_sems, recv_sems = refs[2 * n:]
        x, y, c = _pos()
        cps = []
        for t in range(n):
            for k, chip in enumerate(_other_chips(x, y)):
                cp = pltpu.make_async_remote_copy(
                    src_ref=ins[t].at[2 * chip[0] + chip[1]], dst_ref=outs[t].at[k],
                    send_sem=send_sems.at[3 * t + k], recv_sem=recv_sems.at[3 * t + k],
                    device_id=(*chip, c), device_id_type=MESH)
                cp.start()
                cps.append(cp)
        for cp in cps:
            cp.wait()

    any_spec = pl.BlockSpec(memory_space=pl.ANY)
    return pl.pallas_call(
        body, name=name,
        out_shape=[jax.ShapeDtypeStruct((3,) + p.shape[1:], p.dtype) for p in ps],
        in_specs=[any_spec] * n, out_specs=[any_spec] * n,
        scratch_shapes=[pltpu.SemaphoreType.DMA((3 * n,)), pltpu.SemaphoreType.DMA((3 * n,))],
    )(*ps)


def _pair_join_halves(fs, name):
    n = len(fs)

    def body(*refs):
        outs = refs[n:2 * n]
        send_sems, recv_sems = refs[2 * n:]
        x, y, c = _pos()

        def copy(t, hc):
            h = outs[t].shape[0] // 2
            blk = outs[t].at[pl.ds(hc * h, h)]
            return pltpu.make_async_remote_copy(
                src_ref=blk, dst_ref=blk, send_sem=send_sems.at[t], recv_sem=recv_sems.at[t],
                device_id=(x, y, 1 - c), device_id_type=MESH)

        cps = [copy(t, c) for t in range(n)]
        for cp in cps:
            cp.start()
        for t in range(n):
            copy(t, 1 - c).wait_recv()
        for cp in cps:
            cp.wait_send()

    any_spec = pl.BlockSpec(memory_space=pl.ANY)
    return pl.pallas_call(
        body, name=name,
        out_shape=[jax.ShapeDtypeStruct(f.shape, f.dtype) for f in fs],
        in_specs=[any_spec] * n, out_specs=[any_spec] * n,
        input_output_aliases={t: t for t in range(n)},
        scratch_shapes=[pltpu.SemaphoreType.DMA((n,)), pltpu.SemaphoreType.DMA((n,))],
    )(*fs)


def _pair_add(g, r, name):
    S, R, C = g.shape
    h = R // 2
    tr = _row_tile(h, C)
    nb = h // tr

    def body(c_ref, g_ref, r_ref, o_ref):
        o_ref[...] = (g_ref[...] + r_ref[...]).astype(BF16)

    return pl.pallas_call(
        body, name=name,
        grid_spec=pltpu.PrefetchScalarGridSpec(
            num_scalar_prefetch=1, grid=(S, nb),
            in_specs=[pl.BlockSpec((1, tr, C), lambda s, i, c_ref: (s, c_ref[0] * nb + i, 0)),
                      pl.BlockSpec((1, tr, C), lambda s, i, c_ref: (s, i, 0))],
            out_specs=pl.BlockSpec((1, tr, C), lambda s, i, c_ref: (s, i, 0))),
        out_shape=jax.ShapeDtypeStruct((S, h, C), BF16),
        compiler_params=_params(("parallel", "parallel")),
    )(lax.axis_index("c").astype(jnp.int32).reshape(1), g, r)


def _chip_sum(p, rb, name):
    S, h, C = p.shape
    tr = _row_tile(h, C)
    nb = h // tr
    jc = jnp.concatenate([_chip_index(), lax.axis_index("c").astype(jnp.int32).reshape(1)])

    def body(jc_ref, p_ref, r_ref, o_ref):
        o_ref[...] = ((p_ref[0].astype(F32) + r_ref[0].astype(F32)) + r_ref[1].astype(F32)) + r_ref[2].astype(F32)

    return pl.pallas_call(
        body, name=name,
        grid_spec=pltpu.PrefetchScalarGridSpec(
            num_scalar_prefetch=1, grid=(nb,),
            in_specs=[pl.BlockSpec((1, tr, C), lambda i, jc_ref: (jc_ref[0], i, 0)),
                      pl.BlockSpec((3, tr, C), lambda i, jc_ref: (0, i, 0))],
            out_specs=pl.BlockSpec((tr, C), lambda i, jc_ref: (jc_ref[1] * nb + i, 0))),
        out_shape=jax.ShapeDtypeStruct((2 * h, C), F32),
        compiler_params=_params(("parallel",)),
    )(jc, p, rb)


def _reduce_scatter(gs):
    recv = _pair_swap_halves(gs, "rs_pair_swap")
    ps = [_pair_add(g, r, f"rs_pair_add{t}") for t, (g, r) in enumerate(zip(gs, recv))]
    rbs = _chip_scatter(ps, "rs_chip_scatter")
    hs = [_chip_sum(p, rb, f"rs_chip_sum{t}") for t, (p, rb) in enumerate(zip(ps, rbs))]
    return _pair_join_halves(hs, "rs_pair_join")


def _sum8(g, name):
    _, R, C = g.shape

    def body(g_ref, o_ref):
        acc = g_ref[0]
        for d in range(1, N_DEV):
            acc = acc + g_ref[d]
        o_ref[...] = acc

    return pl.pallas_call(body, name=name, out_shape=jax.ShapeDtypeStruct((R, C), F32),
                          compiler_params=_params())(g)


def _ada_fwd(cs, w, b):
    D, n = w.shape
    tn = _pick(n, (512, 384, 256, 128))

    def body(c_ref, w_ref, b_ref, o_ref):
        cv = c_ref[...]
        a = (cv * _sigmoid(cv)).astype(BF16)
        o_ref[...] = _dot(a, w_ref[...].astype(BF16)) + b_ref[...]

    return pl.pallas_call(
        body, name="ada_fwd", grid=(n // tn,),
        in_specs=[pl.BlockSpec((16, D), lambda j: (0, 0)), pl.BlockSpec((D, tn), lambda j: (0, j)),
                  pl.BlockSpec((1, tn), lambda j: (0, j))],
        out_specs=pl.BlockSpec((16, tn), lambda j: (0, j)),
        out_shape=jax.ShapeDtypeStruct((16, n), F32),
        compiler_params=_params(("parallel",)),
    )(cs, w, b)


def _ada_bwd(cs, w, dmod):
    D, n = w.shape
    tn = _pick(n, (512, 384, 256, 128))

    def body(c_ref, w_ref, d_ref, gw_ref, da_ref):
        j = pl.program_id(0)
        cv = c_ref[...]
        a = cv * _sigmoid(cv)
        d = d_ref[...]
        gw_ref[...] = lax.dot_general(a, d, (((0,), (0,)), ((), ())), precision=HI, preferred_element_type=F32)

        @pl.when(j == 0)
        def _():
            da_ref[...] = jnp.zeros_like(da_ref)

        da_ref[...] += _dot_nt(d.astype(BF16), w_ref[...].astype(BF16))

    return pl.pallas_call(
        body, name="ada_bwd", grid=(n // tn,),
        in_specs=[pl.BlockSpec((16, D), lambda j: (0, 0)), pl.BlockSpec((D, tn), lambda j: (0, j)),
                  pl.BlockSpec((16, tn), lambda j: (0, j))],
        out_specs=[pl.BlockSpec((D, tn), lambda j: (0, j)), pl.BlockSpec((16, D), lambda j: (0, 0))],
        out_shape=[jax.ShapeDtypeStruct((D, n), F32), jax.ShapeDtypeStruct((16, D), F32)],
        compiler_params=_params(("arbitrary",)),
    )(cs, w, dmod)


def _rms1_fwd(xall, gain, shift2, scale2, n_ctx):
    T, D = xall.shape
    tb = _pick(n_ctx, (256, 128, 64, 32, 16))
    nctx = n_ctx // tb

    def body(x_ref, g_ref, sh_ref, sc_ref, o_ref):
        i = pl.program_id(0)
        xv = x_ref[...]
        r = lax.rsqrt(jnp.mean(xv * xv, axis=-1, keepdims=True) + EPS)
        nrm = xv * r * g_ref[...]
        lat = i >= nctx
        sh = jnp.where(lat, sh_ref[1:2, :], sh_ref[0:1, :])
        sc = jnp.where(lat, sc_ref[1:2, :], sc_ref[0:1, :])
        o_ref[...] = (nrm * (1.0 + sc) + sh).astype(BF16)

    vec = lambda r: pl.BlockSpec((r, D), lambda i: (0, 0))
    return pl.pallas_call(
        body, name="rms1_fwd", grid=(T // tb,),
        in_specs=[pl.BlockSpec((tb, D), lambda i: (i, 0)), vec(1), vec(2), vec(2)],
        out_specs=pl.BlockSpec((tb, D), lambda i: (i, 0)),
        out_shape=jax.ShapeDtypeStruct((T, D), BF16),
        compiler_params=_params(("parallel",)),
    )(xall, gain, shift2, scale2)


def _rms1_bwd(xall, dh, dxmid, gain, scale2, n_ctx):
    T, D = xall.shape
    L = T - n_ctx
    tb = _pick(n_ctx, (256, 128, 64, 32, 16))
    nctx = n_ctx // tb

    def body(x_ref, dh_ref, dxm_ref, g_ref, sc_ref, dx_ref, cs_ref):
        i = pl.program_id(0)
        lat = i >= nctx
        xv = x_ref[...]
        r = lax.rsqrt(jnp.mean(xv * xv, axis=-1, keepdims=True) + EPS)
        xh = xv * r
        g = g_ref[...]
        nrm = xh * g
        sc = jnp.where(lat, sc_ref[1:2, :], sc_ref[0:1, :])
        dhv = dh_ref[...]
        dn = dhv * (1.0 + sc)
        dxh = dn * g
        dxv = r * (dxh - xh * jnp.mean(dxh * xh, axis=-1, keepdims=True))
        s_sh = jnp.sum(dhv, axis=0, keepdims=True)
        s_sc = jnp.sum(dhv * nrm, axis=0, keepdims=True)
        s_g = jnp.sum(dn * xh, axis=0, keepdims=True)
        zero = jnp.zeros_like(s_sh)
        rows = lax.broadcasted_iota(jnp.int32, (8, D), 0)
        upd = jnp.where(rows == 0, jnp.where(lat, zero, s_sh),
              jnp.where(rows == 1, jnp.where(lat, zero, s_sc),
              jnp.where(rows == 2, jnp.where(lat, s_sh, zero),
              jnp.where(rows == 3, jnp.where(lat, s_sc, zero),
              jnp.where(rows == 4, s_g, 0.0)))))

        @pl.when(i == 0)
        def _():
            cs_ref[...] = jnp.zeros_like(cs_ref)

        cs_ref[...] += upd

        @pl.when(lat)
        def _():
            dx_ref[...] = dxv + dxm_ref[...]

    lat_blk = lambda i: (jnp.maximum(i - nctx, 0), 0)
    vec = lambda r: pl.BlockSpec((r, D), lambda i: (0, 0))
    return pl.pallas_call(
        body, name="rms1_bwd", grid=(T // tb,),
        in_specs=[pl.BlockSpec((tb, D), lambda i: (i, 0)), pl.BlockSpec((tb, D), lambda i: (i, 0)),
                  pl.BlockSpec((tb, D), lat_blk), vec(1), vec(2)],
        out_specs=[pl.BlockSpec((tb, D), lat_blk), vec(8)],
        out_shape=[jax.ShapeDtypeStruct((L, D), F32), jax.ShapeDtypeStruct((8, D), F32)],
        compiler_params=_params(("arbitrary",)),
    )(xall, dh, dxmid, gain, scale2)


def _resid_rms2_fwd(x, mo, vecs):
    L, D = x.shape
    tb = _pick(L, (256, 128, 64))

    def body(x_ref, mo_ref, v_ref, xm_ref, h_ref):
        xm = x_ref[...] + v_ref[0:1, :] * mo_ref[...]
        xm_ref[...] = xm
        r = lax.rsqrt(jnp.mean(xm * xm, axis=-1, keepdims=True) + EPS)
        h_ref[...] = (xm * r * v_ref[1:2, :] * (1.0 + v_ref[3:4, :]) + v_ref[2:3, :]).astype(BF16)

    blk = pl.BlockSpec((tb, D), lambda i: (i, 0))
    return pl.pallas_call(
        body, name="resid_rms2_fwd", grid=(L // tb,),
        in_specs=[blk, blk, pl.BlockSpec((8, D), lambda i: (0, 0))],
        out_specs=[blk, blk],
        out_shape=[jax.ShapeDtypeStruct((L, D), F32), jax.ShapeDtypeStruct((L, D), BF16)],
        compiler_params=_params(("parallel",)),
    )(x, mo, vecs)


def _resid_rms2_bwd(xmid, dh_a, dh_b, dy, mo, vecs):
    L, D = xmid.shape
    tb = _pick(L, (256, 128, 64))

    def body(xm_ref, da_ref, db_ref, dy_ref, mo_ref, v_ref, dxm_ref, dmo_ref, cs_ref):
        i = pl.program_id(0)
        xm = xm_ref[...]
        r = lax.rsqrt(jnp.mean(xm * xm, axis=-1, keepdims=True) + EPS)
        xh = xm * r
        g = v_ref[1:2, :]
        nrm = xh * g
        dhv = da_ref[...] + db_ref[...]
        dn = dhv * (1.0 + v_ref[3:4, :])
        dxh = dn * g
        dxm = dy_ref[...] + r * (dxh - xh * jnp.mean(dxh * xh, axis=-1, keepdims=True))
        dxm_ref[...] = dxm
        dmo_ref[...] = (dxm * v_ref[0:1, :]).astype(BF16)
        s0 = jnp.sum(dhv, axis=0, keepdims=True)
        s1 = jnp.sum(dhv * nrm, axis=0, keepdims=True)
        s2 = jnp.sum(dn * xh, axis=0, keepdims=True)
        s3 = jnp.sum(dxm * mo_ref[...], axis=0, keepdims=True)
        rows = lax.broadcasted_iota(jnp.int32, (8, D), 0)
        upd = jnp.where(rows == 0, s0, jnp.where(rows == 1, s1, jnp.where(rows == 2, s2,
              jnp.where(rows == 3, s3, 0.0))))

        @pl.when(i == 0)
        def _():
            cs_ref[...] = jnp.zeros_like(cs_ref)

        cs_ref[...] += upd

    blk = pl.BlockSpec((tb, D), lambda i: (i, 0))
    vec = pl.BlockSpec((8, D), lambda i: (0, 0))
    return pl.pallas_call(
        body, name="resid_rms2_bwd", grid=(L // tb,),
        in_specs=[blk, blk, blk, blk, blk, vec],
        out_specs=[blk, blk, vec],
        out_shape=[jax.ShapeDtypeStruct((L, D), F32), jax.ShapeDtypeStruct((L, D), BF16),
                   jax.ShapeDtypeStruct((8, D), F32)],
        compiler_params=_params(("arbitrary",)),
    )(xmid, dh_a, dh_b, dy, mo, vecs)


def _loss_head(xmid, f, g2, target):
    L, D = xmid.shape
    tb = _pick(L, (256, 128, 64))

    def body(xm_ref, f_ref, g_ref, t_ref, dy_ref, df_ref, s_ref):
        i = pl.program_id(0)
        fv = f_ref[...]
        g = g_ref[...]
        err = xm_ref[...] + g * fv - t_ref[...]
        dy = err * (1.0 / D)
        dy_ref[...] = dy
        df_ref[...] = (dy * g).astype(BF16)
        s0 = jnp.sum(dy * fv, axis=0, keepdims=True)
        part = 0.5 * jnp.sum(jnp.mean(err * err, axis=-1, keepdims=True), axis=0, keepdims=True)
        rows = lax.broadcasted_iota(jnp.int32, (8, D), 0)
        upd = jnp.where(rows == 0, s0, jnp.where(rows == 1, part, 0.0))

        @pl.when(i == 0)
        def _():
            s_ref[...] = jnp.zeros_like(s_ref)

        s_ref[...] += upd

    blk = pl.BlockSpec((tb, D), lambda i: (i, 0))
    return pl.pallas_call(
        body, name="loss_head", grid=(L // tb,),
        in_specs=[blk, blk, pl.BlockSpec((1, D), lambda i: (0, 0)), blk],
        out_specs=[blk, blk, pl.BlockSpec((8, D), lambda i: (0, 0))],
        out_shape=[jax.ShapeDtypeStruct((L, D), F32), jax.ShapeDtypeStruct((L, D), BF16),
                   jax.ShapeDtypeStruct((8, D), F32)],
        compiler_params=_params(("arbitrary",)),
    )(xmid, f, g2, target)


def _gate_cols(D, off):
    tc = _pick(np.gcd(D, off), (512, 256, 128))
    return tc, off // tc


def _merge_fwd(za, zb, p, n_ctx, off_a, off_b):
    L, D = za.shape
    tb = _pick(n_ctx, (256, 128, 64, 32, 16))
    nctx = n_ctx // tb
    tc, oa = _gate_cols(D, off_a)
    _, ob = _gate_cols(D, off_b)
    if off_b % tc:
        raise ValueError("gate column offsets must share a column tile")
    ob = off_b // tc

    def body(za_ref, zb_ref, ga_ref, gb_ref, z_ref):
        z_ref[...] = (_sigmoid(ga_ref[...]) * za_ref[...] + _sigmoid(gb_ref[...]) * zb_ref[...]).astype(BF16)

    blk = pl.BlockSpec((tb, tc), lambda i, j: (i, j))
    return pl.pallas_call(
        body, name="merge_fwd", grid=(L // tb, D // tc),
        in_specs=[blk, blk, pl.BlockSpec((tb, tc), lambda i, j: (i + nctx, oa + j)),
                  pl.BlockSpec((tb, tc), lambda i, j: (i + nctx, ob + j))],
        out_specs=blk,
        out_shape=jax.ShapeDtypeStruct((L, D), BF16),
        compiler_params=_params(("parallel", "parallel")),
    )(za, zb, p, p)


def _merge_bwd(dz, za, zb, p, n_ctx, off_a, off_b):
    L, D = za.shape
    T = L + n_ctx
    tb = _pick(n_ctx, (256, 128, 64, 32, 16))
    nctx = n_ctx // tb
    tc = _gate_cols(D, off_a)[0]
    oa, ob = off_a // tc, off_b // tc

    def body(dz_ref, za_ref, zb_ref, ga_ref, gb_ref, dza_ref, dzb_ref, dga_ref, dgb_ref):
        i = pl.program_id(1)

        @pl.when(i < nctx)
        def _():
            dga_ref[...] = jnp.zeros_like(dga_ref)
            dgb_ref[...] = jnp.zeros_like(dgb_ref)

        @pl.when(i >= nctx)
        def _():
            dzv = dz_ref[...]
            sa = _sigmoid(ga_ref[...])
            sb = _sigmoid(gb_ref[...])
            dza_ref[...] = (dzv * sa).astype(BF16)
            dzb_ref[...] = (dzv * sb).astype(BF16)
            dga_ref[...] = (dzv * za_ref[...] * sa * (1.0 - sa)).astype(BF16)
            dgb_ref[...] = (dzv * zb_ref[...] * sb * (1.0 - sb)).astype(BF16)

    lat = pl.BlockSpec((tb, tc), lambda j, i: (jnp.maximum(i - nctx, 0), j))
    allr = pl.BlockSpec((tb, tc), lambda j, i: (i, j))
    return pl.pallas_call(
        body, name="merge_bwd", grid=(D // tc, T // tb),
        in_specs=[lat, lat, lat, pl.BlockSpec((tb, tc), lambda j, i: (i, oa + j)),
                  pl.BlockSpec((tb, tc), lambda j, i: (i, ob + j))],
        out_specs=[lat, lat, allr, allr],
        out_shape=[jax.ShapeDtypeStruct((L, D), BF16), jax.ShapeDtypeStruct((L, D), BF16),
                   jax.ShapeDtypeStruct((T, D), BF16), jax.ShapeDtypeStruct((T, D), BF16)],
        compiler_params=_params(("arbitrary", "arbitrary")),
    )(dz, za, zb, p, p)


def _shift_down(u, rows):
    return jnp.where(rows == 0, 0.0, pltpu.roll(u, 1, 0))


def _shift_up(u, rows):
    n = u.shape[0]
    return jnp.where(rows == n - 1, 0.0, pltpu.roll(u, n - 1, 0))


def _convgate_fwd(u1, u3, cw, cb):
    L, F = u1.shape
    tc = _pick(F, (256, 128))

    def body(u1_ref, u3_ref, w_ref, b_ref, a_ref):
        u = u1_ref[...]
        rows = lax.broadcasted_iota(jnp.int32, u.shape, 0)
        cv = _shift_down(u, rows) * w_ref[0:1, :] + u * w_ref[1:2, :] + _shift_up(u, rows) * w_ref[2:3, :] + b_ref[...]
        a_ref[...] = (cv * _sigmoid(cv) * u3_ref[...]).astype(BF16)

    blk = pl.BlockSpec((L, tc), lambda j: (0, j))
    return pl.pallas_call(
        body, name="convgate_fwd", grid=(F // tc,),
        in_specs=[blk, blk, pl.BlockSpec((8, tc), lambda j: (0, j)), pl.BlockSpec((1, tc), lambda j: (0, j))],
        out_specs=blk,
        out_shape=jax.ShapeDtypeStruct((L, F), BF16),
        compiler_params=_params(("parallel",)),
    )(u1, u3, cw, cb)


def _convgate_bwd(u1, u3, da, cw, cb):
    L, F = u1.shape
    tc = _pick(F, (256, 128))

    def body(u1_ref, u3_ref, da_ref, w_ref, b_ref, du1_ref, du3_ref, s_ref):
        u = u1_ref[...]
        rows = lax.broadcasted_iota(jnp.int32, u.shape, 0)
        um, up = _shift_down(u, rows), _shift_up(u, rows)
        w0, w1, w2 = w_ref[0:1, :], w_ref[1:2, :], w_ref[2:3, :]
        cv = um * w0 + u * w1 + up * w2 + b_ref[...]
        s = _sigmoid(cv)
        dav = da_ref[...]
        du3_ref[...] = (dav * cv * s).astype(BF16)
        dcv = dav * u3_ref[...] * (s * (1.0 + cv * (1.0 - s)))
        du1_ref[...] = (_shift_up(dcv, rows) * w0 + dcv * w1 + _shift_down(dcv, rows) * w2).astype(BF16)
        r8 = lax.broadcasted_iota(jnp.int32, (8, tc), 0)
        s0 = jnp.sum(dcv * um, axis=0, keepdims=True)
        s1 = jnp.sum(dcv * u, axis=0, keepdims=True)
        s2 = jnp.sum(dcv * up, axis=0, keepdims=True)
        s3 = jnp.sum(dcv, axis=0, keepdims=True)
        s_ref[...] = jnp.where(r8 == 0, s0, jnp.where(r8 == 1, s1, jnp.where(r8 == 2, s2,
                     jnp.where(r8 == 3, s3, 0.0))))

    blk = pl.BlockSpec((L, tc), lambda j: (0, j))
    v8 = pl.BlockSpec((8, tc), lambda j: (0, j))
    return pl.pallas_call(
        body, name="convgate_bwd", grid=(F // tc,),
        in_specs=[blk, blk, blk, v8, pl.BlockSpec((1, tc), lambda j: (0, j))],
        out_specs=[blk, blk, v8],
        out_shape=[jax.ShapeDtypeStruct((L, F), BF16), jax.ShapeDtypeStruct((L, F), BF16),
                   jax.ShapeDtypeStruct((8, F), F32)],
        compiler_params=_params(("parallel",)),
    )(u1, u3, da, cw, cb)


def _lower_bound(lbl_ref, d):
    l0, l1 = lbl_ref[d, 0:1, :], lbl_ref[d, 1:2, :]
    m = jnp.maximum(l0, l1)
    e0, e1 = jnp.exp(l0 - m), jnp.exp(l1 - m)
    return e0 / (e0 + e1)


def _chunk_cumsum(x, rev):
    n = x.shape[0]
    r = lax.broadcasted_iota(jnp.int32, x.shape, 0) % CHUNK
    k = 1
    while k < CHUNK:
        if rev:
            x = x + jnp.where(r < CHUNK - k, pltpu.roll(x, n - k, 0), 0.0)
        else:
            x = x + jnp.where(r >= k, pltpu.roll(x, k, 0), 0.0)
        k *= 2
    return x


def _gate_terms(z, lb):
    sg = _sigmoid(z)
    f = lb + (1.0 - lb) * sg
    return sg, f


def _decay_terms(z, lb, rev):
    _, f = _gate_terms(z, lb)
    g = jnp.log(f)
    return 1.0 - f, _chunk_cumsum(g, rev), _chunk_cumsum(g, not rev) - g


def _chunk_total(c, rev):
    return c[0:1, :] if rev else c[CHUNK - 1:CHUNK, :]


def _pair_decay(c, s, rev):
    t = lax.broadcasted_iota(jnp.int32, (CHUNK, 1), 0)
    later = (t <= s) if rev else (t >= s)
    return jnp.where(later, jnp.exp(jnp.minimum(c - c[s:s + 1, :], 0.0)), 0.0)


def _scan_chunk(i, n_ctx_chunks, n_chunks, rev):
    if not rev:
        return i
    return jnp.where(i < n_ctx_chunks, n_ctx_chunks - 1 - i, n_chunks + n_ctx_chunks - 1 - i)


def _rows(ci):
    return pl.ds(pl.multiple_of(ci * CHUNK, CHUNK), CHUNK)


def _hgrn_cols(HA):
    return HA // HEAD


def _hgrn_fwd(p, lbl, ng, n_ctx, HA):
    T = p.shape[0]
    L = T - n_ctx
    nh = _hgrn_cols(HA)
    nc, ncc = T // CHUNK, n_ctx // CHUNK

    def body(q_ref, zf_ref, zb_ref, v_ref, og_ref, lbl_ref, ng_ref, ya_ref, o_ref,
             c_scr, k_scr, qe_scr, ke_scr, o_scr):
        dirs = ((0, False, zf_ref), (1, True, zb_ref))
        for d, rev, z_ref in dirs:
            k, c, rest = _decay_terms(z_ref[...], _lower_bound(lbl_ref, d), rev)
            c_scr[d] = c
            k_scr[d] = k
            qe_scr[d] = (q_ref[...] * jnp.exp(c)).astype(BF16)
            ke_scr[d] = (k * jnp.exp(rest)).astype(BF16)

        def step(i, states):
            new = []
            for (d, rev, _), St in zip(dirs, states):
                rows = _rows(_scan_chunk(i, ncc, nc, rev))
                q, v, c, k = q_ref[rows, :], v_ref[rows, :], c_scr[d, rows, :], k_scr[d, rows, :]
                o = _dot_nt(qe_scr[d, rows, :], St.astype(BF16))
                for s in range(CHUNK):
                    E = _pair_decay(c, s, rev)
                    a = jnp.sum(q * E * k[s:s + 1, :], axis=1, keepdims=True)
                    o = o + a * v[s:s + 1, :]
                o_scr[d, rows, :] = o
                new.append(St * jnp.exp(_chunk_total(c, rev)) + _dot_tn(v.astype(BF16), ke_scr[d, rows, :]))
            return tuple(new)

        zero = jnp.zeros((HEAD, HEAD), F32)
        lax.fori_loop(0, nc, step, (zero, zero), unroll=2)

        o = o_scr[0, pl.ds(n_ctx, L), :] + o_scr[1, pl.ds(n_ctx, L), :]
        o_ref[...] = o
        r = lax.rsqrt(jnp.mean(o * o, axis=-1, keepdims=True) + EPS)
        og = og_ref[pl.ds(n_ctx, L), :]
        ya_ref[...] =(o * r * ng_ref[...] * (og * _sigmoid(og))).astype(BF16)

    cb = HA // HEAD
    col = lambda kk: pl.BlockSpec((T, HEAD), lambda h: (0, kk * cb + h))
    return pl.pallas_call(
        body, name="hgrn_fwd", grid=(nh,),
        in_specs=[col(0), col(1), col(2), col(3), col(4),
                  pl.BlockSpec((2, 2, HEAD), lambda h: (0, 0, h)), pl.BlockSpec((1, HEAD), lambda h: (0, 0))],
        out_specs=[pl.BlockSpec((L, HEAD), lambda h: (0, h)), pl.BlockSpec((L, HEAD), lambda h: (0, h))],
        out_shape=[jax.ShapeDtypeStruct((L, HA), BF16), jax.ShapeDtypeStruct((L, HA), F32)],
        scratch_shapes=[pltpu.VMEM((2, T, HEAD), F32), pltpu.VMEM((2, T, HEAD), F32),
                        pltpu.VMEM((2, T, HEAD), BF16), pltpu.VMEM((2, T, HEAD), BF16),
                        pltpu.VMEM((2, T, HEAD), F32)],
        compiler_params=_params(("parallel",)),
    )(p, p, p, p, p, lbl, ng)


def _hgrn_bwd(p, lbl, ng, o, dya, n_ctx, HA):
    T = p.shape[0]
    L = T - n_ctx
    nh = _hgrn_cols(HA)
    nc, ncc = T // CHUNK, n_ctx // CHUNK

    def body(q_ref, zf_ref, zb_ref, v_ref, og_ref, lbl_ref, ng_ref, o_ref, dya_ref,
             dq_ref, dzf_ref, dzb_ref, dv_ref, dog_ref, dlbl_ref, dng_ref,
             do_scr, st_scr, c_scr, k_scr, qe_scr, ke_scr, dg_scr, dk_scr, dq_scr, dv_scr):
        h = pl.program_id(0)
        ov = o_ref[...]
        r = lax.rsqrt(jnp.mean(ov * ov, axis=-1, keepdims=True) + EPS)
        oh = ov * r
        ogv = og_ref[pl.ds(n_ctx, L), :]
        sg_o = _sigmoid(ogv)
        dyv = dya_ref[...]
        ngv = ng_ref[...]
        dog_ref[pl.ds(0, n_ctx), :] = jnp.zeros((n_ctx, HEAD), BF16)
        dog_ref[pl.ds(n_ctx, L), :] = (dyv * oh * ngv * (sg_o * (1.0 + ogv * (1.0 - sg_o)))).astype(BF16)
        don = dyv * (ogv * sg_o)
        dng = jnp.sum(don * oh, axis=0, keepdims=True)
        doh = don * ngv
        do_scr[pl.ds(0, n_ctx), :] = jnp.zeros((n_ctx, HEAD), F32)
        do_scr[pl.ds(n_ctx, L), :] = r * (doh - oh * jnp.mean(doh * oh, axis=-1, keepdims=True))

        @pl.when(h == 0)
        def _():
            dng_ref[...] = jnp.zeros_like(dng_ref)

        dng_ref[0:1, :] += dng

        t16 = lax.broadcasted_iota(jnp.int32, (CHUNK, HEAD), 0)
        dirs = ((0, False, zf_ref, dzf_ref), (1, True, zb_ref, dzb_ref))
        for d, rev, z_ref, _ in dirs:
            k, c, rest = _decay_terms(z_ref[...], _lower_bound(lbl_ref, d), rev)
            c_scr[d] = c
            k_scr[d] = k
            qe_scr[d] = (q_ref[...] * jnp.exp(c)).astype(BF16)
            ke_scr[d] = (k * jnp.exp(rest)).astype(BF16)
        dq_scr[...] = jnp.zeros_like(dq_scr)
        dv_scr[...] = jnp.zeros_like(dv_scr)

        def fwd_step(i, states):
            new = []
            for (d, rev, _, _), St in zip(dirs, states):
                ci = _scan_chunk(i, ncc, nc, rev)
                rows = _rows(ci)
                st_scr[d, ci] = St.astype(BF16)
                etot = jnp.exp(_chunk_total(c_scr[d, rows, :], rev))
                new.append(St * etot + _dot_tn(v_ref[rows, :].astype(BF16), ke_scr[d, rows, :]))
            return tuple(new)

        zero = jnp.zeros((HEAD, HEAD), F32)
        lax.fori_loop(0, nc, fwd_step, (zero, zero), unroll=2)

        def bwd_step(ii, carry):
            i = nc - 1 - ii
            new = []
            for (d, rev, _, _), dSt in zip(dirs, carry):
                ci = _scan_chunk(i, ncc, nc, rev)
                rows = _rows(ci)
                q, v, do = q_ref[rows, :], v_ref[rows, :], do_scr[rows, :]
                c, k = c_scr[d, rows, :], k_scr[d, rows, :]
                tot = _chunk_total(c, rev)
                etot = jnp.exp(tot)
                St = st_scr[d, ci]
                dSb = dSt.astype(BF16)
                do_b = do.astype(BF16)
                dq = _dot(do_b, St) * jnp.exp(c)
                dk = _dot(v.astype(BF16), dSb) * jnp.exp(tot - c)
                dv = _dot_nt(ke_scr[d, rows, :], dSb)
                dtot = (jnp.sum(St.astype(F32) * dSt, axis=0, keepdims=True) * etot
                        + jnp.sum(k * dk, axis=0, keepdims=True))
                for s in range(CHUNK):
                    E = _pair_decay(c, s, rev)
                    XE = E * k[s:s + 1, :]
                    a = jnp.sum(q * XE, axis=1, keepdims=True)
                    da = jnp.sum(do * v[s:s + 1, :], axis=1, keepdims=True)
                    dq = dq + da * XE
                    dk_row = jnp.sum(da * q * E, axis=0, keepdims=True)
                    dv_row = jnp.sum(a * do, axis=0, keepdims=True)
                    dk = dk + jnp.where(t16 == s, dk_row, 0.0)
                    dv = dv + jnp.where(t16 == s, dv_row, 0.0)
                dg_scr[d, rows, :] = _chunk_cumsum(q * dq - k * dk, not rev) + dtot
                dk_scr[d, rows, :] = dk
                dq_scr[rows, :] += dq
                dv_scr[rows, :] += dv
                new.append(dSt * etot + _dot_tn(do_b, qe_scr[d, rows, :]))
            return tuple(new)

        lax.fori_loop(0, nc, bwd_step, (zero, zero), unroll=2)

        for d, _, z_ref, dz_ref in dirs:
            lb = _lower_bound(lbl_ref, d)
            sg, f = _gate_terms(z_ref[...], lb)
            df = dg_scr[d] / f - dk_scr[d]
            dz_ref[...] = (df * (1.0 - lb) * sg * (1.0 - sg)).astype(BF16)
            dl0 = jnp.sum(df * (1.0 - sg), axis=0, keepdims=True) * lb * (1.0 - lb)
            dlbl_ref[d, 0:1, :] = dl0
            dlbl_ref[d, 1:2, :] = -dl0
        dq_ref[...] = dq_scr[...].astype(BF16)
        dv_ref[...] = dv_scr[...].astype(BF16)

    cb = HA // HEAD
    col = lambda kk: pl.BlockSpec((T, HEAD), lambda h: (0, kk * cb + h))
    tcol = pl.BlockSpec((T, HEAD), lambda h: (0, h))
    lcol = pl.BlockSpec((L, HEAD), lambda h: (0, h))
    outs = pl.pallas_call(
        body, name="hgrn_bwd", grid=(nh,),
        in_specs=[col(0), col(1), col(2), col(3), col(4),
                  pl.BlockSpec((2, 2, HEAD), lambda h: (0, 0, h)), pl.BlockSpec((1, HEAD), lambda h: (0, 0)),
                  lcol, lcol],
        out_specs=[tcol, tcol, tcol, tcol, tcol, pl.BlockSpec((2, 2, HEAD), lambda h: (0, 0, h)),
                   pl.BlockSpec((8, HEAD), lambda h: (0, 0))],
        out_shape=[jax.ShapeDtypeStruct((T, HA), BF16)] * 5 + [jax.ShapeDtypeStruct((2, 2, HA), F32),
                                                               jax.ShapeDtypeStruct((8, HEAD), F32)],
        scratch_shapes=[pltpu.VMEM((T, HEAD), F32), pltpu.VMEM((2, nc, HEAD, HEAD), BF16),
                        pltpu.VMEM((2, T, HEAD), F32), pltpu.VMEM((2, T, HEAD), F32),
                        pltpu.VMEM((2, T, HEAD), BF16), pltpu.VMEM((2, T, HEAD), BF16),
                        pltpu.VMEM((2, T, HEAD), F32), pltpu.VMEM((2, T, HEAD), F32),
                        pltpu.VMEM((T, HEAD), F32), pltpu.VMEM((T, HEAD), F32)],
        compiler_params=_params(("arbitrary",)),
    )(p, p, p, p, p, lbl, ng, o, dya)
    return outs


def _swap_halves(t, lane):
    q = HEAD // 4
    return jnp.where((lane % (2 * q)) < q, pltpu.roll(t, HEAD - q, 1), pltpu.roll(t, q, 1))


def _qk_norm(t, g):
    r = lax.rsqrt(jnp.mean(t * t, axis=-1, keepdims=True) + EPS)
    return t * r, r


def _rope(t, cos, sin, lane):
    return t * cos + _swap_halves(t, lane) * sin


def _qk_norm_bwd(dy, th, r, g):
    dth = dy * g
    return r * (dth - th * jnp.mean(dth * th, axis=-1, keepdims=True)), jnp.sum(dy * th, axis=0, keepdims=True)


def _rope_bwd(dy, cos, sin, lane):
    return dy * cos + _swap_halves(dy * sin, lane)


def _na_geometry(L):
    n_rows = L // GRID_W
    kr = min(WIN_R, n_rows)
    return n_rows, kr


def _na_prep(q_ref, k_ref, v_ref, gq_ref, gk_ref, cos_ref, sin_ref, qs, ks, vs, n_ctx, L):
    lane = lax.broadcasted_iota(jnp.int32, (L, HEAD), 1)
    cos, sin = cos_ref[...], sin_ref[...]
    qh, _ = _qk_norm(q_ref[pl.ds(n_ctx, L), :], None)
    qs[...] = _rope(qh * gq_ref[...], cos, sin, lane).astype(BF16)
    kh, _ = _qk_norm(k_ref[pl.ds(n_ctx, L), :], None)
    ks[pl.ds(n_ctx, L), :] = _rope(kh * gk_ref[...], cos, sin, lane).astype(BF16)
    kc, _ = _qk_norm(k_ref[pl.ds(0, n_ctx), :], None)
    ks[pl.ds(0, n_ctx), :] = (kc * gk_ref[...]).astype(BF16)
    vs[...] = v_ref[...].astype(BF16)


def _na_scores(r, qs, ks, bias_ref, n_ctx, n_rows, kr):
    scale = HEAD ** -0.5
    r0 = jnp.clip(r - WIN_R // 2, 0, n_rows - kr)
    qrows = pl.ds(pl.multiple_of(r * GRID_W, GRID_W), GRID_W)
    krows = pl.ds(pl.multiple_of(n_ctx + r0 * GRID_W, GRID_W), kr * GRID_W)
    qv = qs[qrows, :]
    sb = _dot_nt(qv, ks[krows, :]) * scale
    b0 = r0 - r + (WIN_R - 1)
    sb = sb + jnp.concatenate([bias_ref[0, b0 + 2 * jj] for jj in range(kr // 2)], axis=1)
    sc = _dot_nt(qv, ks[pl.ds(0, n_ctx), :]) * scale
    m = jnp.maximum(jnp.max(sb, axis=1, keepdims=True), jnp.max(sc, axis=1, keepdims=True))
    eb, ec = jnp.exp(sb - m), jnp.exp(sc - m)
    inv = 1.0 / (jnp.sum(eb, axis=1, keepdims=True) + jnp.sum(ec, axis=1, keepdims=True))
    return eb * inv, ec * inv, qrows, krows, b0


def _na_fwd(p, bias, gq, gk, cos, sin, n_ctx, off, HB):
    T = p.shape[0]
    L = T - n_ctx
    nh = HB // HEAD
    n_rows, kr = _na_geometry(L)
    ob = off // HEAD

    def body(q_ref, k_ref, v_ref, bias_ref, gq_ref, gk_ref, cos_ref, sin_ref, y_ref, qs, ks, vs):
        _na_prep(q_ref, k_ref, v_ref, gq_ref, gk_ref, cos_ref, sin_ref, qs, ks, vs, n_ctx, L)

        def step(r, carry):
            pb, pc, qrows, krows, _ = _na_scores(r, qs, ks, bias_ref, n_ctx, n_rows, kr)
            y = _dot(pb.astype(BF16), vs[krows, :]) + _dot(pc.astype(BF16), vs[pl.ds(0, n_ctx), :])
            y_ref[qrows, :] = y.astype(BF16)
            return carry

        lax.fori_loop(0, n_rows, step, 0)

    col = lambda kk: pl.BlockSpec((T, HEAD), lambda h: (0, ob + kk * nh + h))
    vec = pl.BlockSpec((1, HEAD), lambda h: (0, 0))
    tab = pl.BlockSpec((L, HEAD), lambda h: (0, 0))
    return pl.pallas_call(
        body, name="na_fwd", grid=(nh,),
        in_specs=[col(0), col(1), col(2), pl.BlockSpec((1,) + bias.shape[1:], lambda h: (h, 0, 0, 0)),
                  vec, vec, tab, tab],
        out_specs=pl.BlockSpec((L, HEAD), lambda h: (0, h)),
        out_shape=jax.ShapeDtypeStruct((L, HB), BF16),
        scratch_shapes=[pltpu.VMEM((L, HEAD), BF16), pltpu.VMEM((T, HEAD), BF16), pltpu.VMEM((T, HEAD), BF16)],
        compiler_params=_params(("parallel",)),
    )(p, p, p, bias, gq, gk, cos, sin)


def _na_bwd(p, bias, gq, gk, cos, sin, dyb, n_ctx, off, HB):
    T = p.shape[0]
    L = T - n_ctx
    nh = HB // HEAD
    n_rows, kr = _na_geometry(L)
    ob = off // HEAD
    scale = HEAD ** -0.5

    def body(q_ref, k_ref, v_ref, bias_ref, gq_ref, gk_ref, cos_ref, sin_ref, dy_ref,
             dq_ref, dk_ref, dv_ref, dbias_ref, dg_ref, qs, ks, vs, dqa, dka, dva):
        h = pl.program_id(0)
        _na_prep(q_ref, k_ref, v_ref, gq_ref, gk_ref, cos_ref, sin_ref, qs, ks, vs, n_ctx, L)
        dka[...] = jnp.zeros_like(dka)
        dva[...] = jnp.zeros_like(dva)
        dbias_ref[...] = jnp.zeros_like(dbias_ref)

        def step(r, carry):
            pb, pc, qrows, krows, b0 = _na_scores(r, qs, ks, bias_ref, n_ctx, n_rows, kr)
            crows = pl.ds(0, n_ctx)
            do = dy_ref[qrows, :]
            qv = qs[qrows, :]
            dpb = _dot_nt(do, vs[krows, :])
            dpc = _dot_nt(do, vs[crows, :])
            delta = jnp.sum(pb * dpb, axis=1, keepdims=True) + jnp.sum(pc * dpc, axis=1, keepdims=True)
            dsb = pb * (dpb - delta)
            dsc = pc * (dpc - delta)
            for jj in range(kr // 2):
                dbias_ref[0, b0 + 2 * jj] += dsb[:, jj * 2 * GRID_W:(jj + 1) * 2 * GRID_W]
            dsb_b, dsc_b = dsb.astype(BF16), dsc.astype(BF16)
            dqa[qrows, :] = (_dot(dsb_b, ks[krows, :]) + _dot(dsc_b, ks[crows, :])) * scale
            dka[krows, :] += _dot_tn(dsb_b, qv) * scale
            dka[crows, :] += _dot_tn(dsc_b, qv) * scale
            dva[krows, :] += _dot_tn(pb.astype(BF16), do)
            dva[crows, :] += _dot_tn(pc.astype(BF16), do)
            return carry

        lax.fori_loop(0, n_rows, step, 0)

        lane = lax.broadcasted_iota(jnp.int32, (L, HEAD), 1)
        cos, sin = cos_ref[...], sin_ref[...]
        lat, ctx = pl.ds(n_ctx, L), pl.ds(0, n_ctx)
        gqv, gkv = gq_ref[...], gk_ref[...]
        qh, rq = _qk_norm(q_ref[lat, :], None)
        dq, dgq = _qk_norm_bwd(_rope_bwd(dqa[...], cos, sin, lane), qh, rq, gqv)
        dq_ref[ctx, :] = jnp.zeros((n_ctx, HEAD), BF16)
        dq_ref[lat, :] = dq.astype(BF16)
        kh, rk = _qk_norm(k_ref[lat, :], None)
        dk, dgk = _qk_norm_bwd(_rope_bwd(dka[lat, :], cos, sin, lane), kh, rk, gkv)
        dk_ref[lat, :] = dk.astype(BF16)
        kch, rkc = _qk_norm(k_ref[ctx, :], None)
        dkc, dgkc = _qk_norm_bwd(dka[ctx, :], kch, rkc, gkv)
        dk_ref[ctx, :] = dkc.astype(BF16)
        dv_ref[...] = dva[...].astype(BF16)

        @pl.when(h == 0)
        def _():
            dg_ref[...] = jnp.zeros_like(dg_ref)

        dg_ref[0:1, :] += dgq
        dg_ref[1:2, :] += dgk + dgkc

    col = lambda kk: pl.BlockSpec((T, HEAD), lambda h: (0, ob + kk * nh + h))
    vec = pl.BlockSpec((1, HEAD), lambda h: (0, 0))
    tab = pl.BlockSpec((L, HEAD), lambda h: (0, 0))
    tcol = pl.BlockSpec((T, HEAD), lambda h: (0, h))
    bspec = pl.BlockSpec((1,) + bias.shape[1:], lambda h: (h, 0, 0, 0))
    return pl.pallas_call(
        body, name="na_bwd", grid=(nh,),
        in_specs=[col(0), col(1), col(2), bspec, vec, vec, tab, tab, pl.BlockSpec((L, HEAD), lambda h: (0, h))],
        out_specs=[tcol, tcol, tcol, bspec, pl.BlockSpec((8, HEAD), lambda h: (0, 0))],
        out_shape=[jax.ShapeDtypeStruct((T, HB), BF16)] * 3 + [jax.ShapeDtypeStruct(bias.shape, F32),
                                                               jax.ShapeDtypeStruct((8, HEAD), F32)],
        scratch_shapes=[pltpu.VMEM((L, HEAD), BF16), pltpu.VMEM((T, HEAD), BF16), pltpu.VMEM((T, HEAD), BF16),
                        pltpu.VMEM((L, HEAD), F32), pltpu.VMEM((T, HEAD), F32), pltpu.VMEM((T, HEAD), F32)],
        compiler_params=_params(("arbitrary",)),
    )(p, p, p, bias, gq, gk, cos, sin, dyb)


def _bias_tables():
    w = np.arange(GRID_W)
    col_start = np.clip(w - WIN_C // 2, 0, GRID_W - WIN_C)
    col_in = (w[None, :] >= col_start[:, None]) & (w[None, :] < col_start[:, None] + WIN_C)
    dc = np.clip(w[None, :] - w[:, None], -(WIN_C - 1), WIN_C - 1) + WIN_C - 1
    n_pair = 2 * WIN_R
    ridx = np.zeros((n_pair, GRID_W, 2 * GRID_W), np.int32)
    cidx = np.zeros((n_pair, GRID_W, 2 * GRID_W), np.int32)
    valid = np.zeros((n_pair, GRID_W, 2 * GRID_W), bool)
    for i in range(n_pair):
        for half in range(2):
            row = i + half
            sl = slice(half * GRID_W, (half + 1) * GRID_W)
            ridx[i, :, sl] = min(row, 2 * WIN_R - 2)
            cidx[i, :, sl] = dc
            valid[i, :, sl] = col_in & (row <= 2 * WIN_R - 2)
    return ridx, cidx, valid


def _bias_onehot():
    _, cidx, valid = _bias_tables()
    K = GRID_W * 2 * GRID_W
    oh = np.zeros((K, 128), np.float32)
    neg = np.full((1, K), NEG, np.float32)
    for cq in range(GRID_W):
        for ll in range(2 * GRID_W):
            if valid[0, cq, ll]:
                oh[cq * 2 * GRID_W + ll, (ll // GRID_W) * 64 + cidx[0, cq, ll]] = 1.0
                neg[0, cq * 2 * GRID_W + ll] = 0.0
    return oh, neg


def _expand_bias(table):
    H = table.shape[0]
    n_pair, n_dc = 2 * WIN_R, 2 * WIN_C - 1
    tp = jnp.pad(table, ((0, 0), (0, n_pair + 1 - table.shape[1]), (0, 64 - n_dc)))
    t2 = jnp.concatenate([tp[:, :n_pair], tp[:, 1:n_pair + 1]], axis=-1).reshape(H * n_pair, 128)
    oh, neg = _bias_onehot()

    def body(t_ref, oh_ref, neg_ref, o_ref):
        o_ref[...] = lax.dot_general(t_ref[...], oh_ref[...], (((1,), (1,)), ((), ())), precision=HI,
                                     preferred_element_type=F32) + neg_ref[...]

    out = pl.pallas_call(body, name="bias_expand", out_shape=jax.ShapeDtypeStruct((H * n_pair, oh.shape[0]), F32),
                         compiler_params=_params())(t2, jnp.asarray(oh), jnp.asarray(neg))
    return out.reshape(H, n_pair, GRID_W, 2 * GRID_W)


def _bias_grad(dbias):
    H = dbias.shape[0]
    n_pair, n_dc = 2 * WIN_R, 2 * WIN_C - 1
    K = GRID_W * 2 * GRID_W
    oh, _ = _bias_onehot()
    flat = dbias.reshape(H * n_pair, K)

    def body(d_ref, oh_ref, o_ref):
        o_ref[...] = jnp.dot(d_ref[...], oh_ref[...], precision=HI, preferred_element_type=F32)

    g = pl.pallas_call(body, name="bias_grad", out_shape=jax.ShapeDtypeStruct((H * n_pair, 128), F32),
                       compiler_params=_params())(flat, jnp.asarray(oh))
    g = g.reshape(H, n_pair, 128)
    left, right = g[:, :, :n_dc], g[:, :, 64:64 + n_dc]
    out = left[:, :n_pair - 1]
    return out.at[:, 1:].add(right[:, :n_pair - 2])


def _rope_tables(L):
    pos = np.arange(L)
    row = (pos // GRID_W).astype(np.float32)
    colp = (pos % GRID_W).astype(np.float32)
    half = HEAD // 2
    nf = half // 2
    inv = (ROPE_THETA ** (-np.arange(nf, dtype=np.float32) / nf)).astype(np.float32)

    def tabs(pv):
        ang = pv[:, None] * inv[None, :]
        c, s = np.cos(ang), np.sin(ang)
        return np.concatenate([c, c], axis=1), np.concatenate([-s, s], axis=1)

    cr, sr = tabs(row)
    cc, sc = tabs(colp)
    return (jnp.asarray(np.concatenate([cr, cc], axis=1), F32), jnp.asarray(np.concatenate([sr, sc], axis=1), F32))


def _adamw(w, g, m, v, name):
    R, C = w.shape
    tr = _row_tile(R, C)
    c1 = 1.0 - ADAM_B1 ** ADAM_STEP
    c2 = 1.0 - ADAM_B2 ** ADAM_STEP

    def body(w_ref, g_ref, m_ref, v_ref, d_ref, mo_ref, vo_ref):
        gv = g_ref[...]
        mn = ADAM_B1 * m_ref[...] + (1.0 - ADAM_B1) * gv
        vn = ADAM_B2 * v_ref[...] + (1.0 - ADAM_B2) * (gv * gv)
        mo_ref[...] = mn
        vo_ref[...] = vn
        d_ref[...] = -ADAM_LR * ((mn / c1) / (jnp.sqrt(vn / c2) + ADAM_EPS) + ADAM_WD * w_ref[...])

    blk = pl.BlockSpec((tr, C), lambda i: (i, 0))
    return pl.pallas_call(
        body, name=name, grid=(R // tr,),
        in_specs=[blk] * 4, out_specs=[blk] * 3,
        out_shape=[jax.ShapeDtypeStruct((R, C), F32)] * 3,
        compiler_params=_params(("parallel",)),
    )(w, g, m, v)


PACK_W = 1024


def _pack(parts):
    flat, offs, pos = [], [], 0
    for a in parts:
        n = a.size
        padn = -n % PACK_W
        flat.append(jnp.pad(a.reshape(-1).astype(F32), (0, padn)))
        offs.append((pos, n, a.shape))
        pos += n + padn
    tail = -pos % (8 * PACK_W)
    if tail:
        flat.append(jnp.zeros((tail,), F32))
    return jnp.concatenate(flat).reshape(-1, PACK_W), offs


def _unpack(buf, offs, i):
    pos, n, shape = offs[i]
    return buf.reshape(buf.shape[:-2] + (-1,))[..., pos:pos + n].reshape(buf.shape[:-2] + shape)


def kernel(x, c, ctx, c_ctx, ada_w, ada_b, norm1_g, norm2_g, w_in, hgrn_lb_logits, hgrn_norm_g, na_q_norm_g, na_k_norm_g, na_rel_bias, w_branch_a, w_branch_b, w_out, ffn_w1, ffn_w3, ffn_conv_w, ffn_conv_b, ffn_w2, loss_target, m_c_ctx, m_ada_w, m_ada_b, m_norm1_g, m_norm2_g, m_w_in, m_hgrn_lb_logits, m_hgrn_norm_g, m_na_q_norm_g, m_na_k_norm_g, m_na_rel_bias, m_w_branch_a, m_w_branch_b, m_w_out, m_ffn_w1, m_ffn_w3, m_ffn_conv_w, m_ffn_conv_b, m_ffn_w2, v_c_ctx, v_ada_w, v_ada_b, v_norm1_g, v_norm2_g, v_w_in, v_hgrn_lb_logits, v_hgrn_norm_g, v_na_q_norm_g, v_na_k_norm_g, v_na_rel_bias, v_w_branch_a, v_w_branch_b, v_w_out, v_ffn_w1, v_ffn_w3, v_ffn_conv_w, v_ffn_conv_b, v_ffn_w2):
    weights = dict(c_ctx=c_ctx, ada_w=ada_w, ada_b=ada_b, norm1_g=norm1_g, norm2_g=norm2_g, w_in=w_in,
                   hgrn_lb_logits=hgrn_lb_logits, hgrn_norm_g=hgrn_norm_g, na_q_norm_g=na_q_norm_g,
                   na_k_norm_g=na_k_norm_g, na_rel_bias=na_rel_bias, w_branch_a=w_branch_a, w_branch_b=w_branch_b,
                   w_out=w_out, ffn_w1=ffn_w1, ffn_w3=ffn_w3, ffn_conv_w=ffn_conv_w, ffn_conv_b=ffn_conv_b,
                   ffn_w2=ffn_w2)
    moms = dict(c_ctx=(m_c_ctx, v_c_ctx), ada_w=(m_ada_w, v_ada_w), ada_b=(m_ada_b, v_ada_b),
                norm1_g=(m_norm1_g, v_norm1_g), norm2_g=(m_norm2_g, v_norm2_g), w_in=(m_w_in, v_w_in),
                hgrn_lb_logits=(m_hgrn_lb_logits, v_hgrn_lb_logits), hgrn_norm_g=(m_hgrn_norm_g, v_hgrn_norm_g),
                na_q_norm_g=(m_na_q_norm_g, v_na_q_norm_g), na_k_norm_g=(m_na_k_norm_g, v_na_k_norm_g),
                na_rel_bias=(m_na_rel_bias, v_na_rel_bias), w_branch_a=(m_w_branch_a, v_w_branch_a),
                w_branch_b=(m_w_branch_b, v_w_branch_b), w_out=(m_w_out, v_w_out), ffn_w1=(m_ffn_w1, v_ffn_w1),
                ffn_w3=(m_ffn_w3, v_ffn_w3), ffn_conv_w=(m_ffn_conv_w, v_ffn_conv_w),
                ffn_conv_b=(m_ffn_conv_b, v_ffn_conv_b), ffn_w2=(m_ffn_w2, v_ffn_w2))
    order = list(weights)

    L, D = x.shape[1], x.shape[2]
    N = ctx.shape[1]
    T = N + L
    HA = w_branch_a.shape[1]
    HB = w_branch_b.shape[1]
    F = ffn_conv_b.shape[1]
    IN = 5 * HA + 3 * HB + 2 * D
    n_ada = ada_w.shape[2]
    ix, iy, ic = _pos()
    chip = 2 * ix + iy
    dev = 2 * chip + ic

    pk0, offs0 = _pack([c[0], hgrn_lb_logits, ffn_conv_w[0]])
    g0 = _allgather8(pk0, "gather_small0")
    c_all = _unpack(g0, offs0, 0)
    lbl_parts = _unpack(g0, offs0, 1)
    lbl = jnp.concatenate([lbl_parts[2 * j] for j in range(N_CHIP)], axis=-1)
    cw_parts = _unpack(g0, offs0, 2)
    cw = jnp.concatenate([cw_parts[2 * j] for j in range(N_CHIP)], axis=-1)
    cw8 = jnp.pad(cw, ((0, 5), (0, 0)))

    cs = jnp.concatenate([c_all, c_ctx[None, :], jnp.zeros((7, D), F32)], axis=0)
    ada_b_mine = lax.dynamic_slice(ada_b, (0, chip * n_ada), (1, n_ada))
    mod_mine = _ada_fwd(cs, ada_w[0], ada_b_mine)
    gm = _allgather8(mod_mine, "gather_mod")
    mod = jnp.concatenate([gm[2 * j] for j in range(N_CHIP)], axis=-1)
    mod_l = lax.dynamic_slice(mod, (dev, 0), (1, N_MOD * D)).reshape(N_MOD, D)
    mod_c = mod[8].reshape(N_MOD, D)
    sh1, sc1, g1, sh2, sc2, g2 = [mod_l[i:i + 1] for i in range(N_MOD)]
    shift1 = jnp.concatenate([mod_c[0:1], sh1], axis=0)
    scale1 = jnp.concatenate([mod_c[1:2], sc1], axis=0)

    shards = [w_in[0], w_branch_a[0], w_branch_b[0], w_out[0], ffn_w1[0], ffn_w3[0], ffn_w2[0]]
    names = ["w_in", "w_a", "w_b", "w_out", "w1", "w3", "w2"]
    shards_bf = [_cast_bf16_slot(s, "cast_" + nm) for s, nm in zip(shards, names)]
    Win, Wa, Wb, Wo, W1, W3, W2 = _gather_shards(shards_bf, "gather_weights")
    Wo = Wo.reshape(1, D, D)
    W2 = W2.reshape(1, F, D)

    xall = jnp.concatenate([ctx[0], x[0]], axis=0)
    h_all = _rms1_fwd(xall, norm1_g, shift1, scale1, N)
    p = _mm_nn(h_all, Win, F32, "mm_p")
    y_a, o_a = _hgrn_fwd(p, lbl, hgrn_norm_g, N, HA)
    bias = _expand_bias(na_rel_bias[0])
    cos, sin = _rope_tables(L)
    off_na = 5 * HA
    y_b = _na_fwd(p, bias, na_q_norm_g, na_k_norm_g, cos, sin, N, off_na, HB)
    za = _mm_nn(y_a, Wa, F32, "mm_za")
    zb = _mm_nn(y_b, Wb, F32, "mm_zb")
    off_ga, off_gb = 5 * HA + 3 * HB, 5 * HA + 3 * HB + D
    z = _merge_fwd(za, zb, p, N, off_ga, off_gb)
    mo = _mm_nn(z, Wo, F32, "mm_mo")
    vec2 = jnp.concatenate([g1, norm2_g, sh2, sc2, jnp.zeros((4, D), F32)], axis=0)
    x_mid, h2 = _resid_rms2_fwd(x[0], mo, vec2)
    u1 = _mm_nn(h2, W1, F32, "mm_u1")
    u3 = _mm_nn(h2, W3, F32, "mm_u3")
    a = _convgate_fwd(u1, u3, cw8, ffn_conv_b)
    f = _mm_nn(a, W2, F32, "mm_f")
    dy, df, s_loss = _loss_head(x_mid, f, g2, loss_target[0])
    loss = lax.psum(s_loss[1, 0], ("x", "y", "c"))
    d_g2 = s_loss[0:1]

    gW2 = _mm_tn(a, df, 1, "mm_gw2").reshape(N_CHIP, F // N_CHIP, D)
    da = _mm_nt(df, W2, F32, "mm_da")
    du1, du3, s_conv = _convgate_bwd(u1, u3, da, cw8, ffn_conv_b)
    gW1 = _mm_tn(h2, du1, N_CHIP, "mm_gw1")
    gW3 = _mm_tn(h2, du3, N_CHIP, "mm_gw3")
    dh2a = _mm_nt(du1, W1, F32, "mm_dh2a")
    dh2b = _mm_nt(du3, W3, F32, "mm_dh2b")
    dxm, dmo, s_rms2 = _resid_rms2_bwd(x_mid, dh2a, dh2b, dy, mo, vec2)
    gWo = _mm_tn(z, dmo, 1, "mm_gwo").reshape(N_CHIP, D // N_CHIP, D)
    dz = _mm_nt(dmo, Wo, F32, "mm_dz")
    dza, dzb, dga, dgb = _merge_bwd(dz, za, zb, p, N, off_ga, off_gb)
    gWa = _mm_tn(y_a, dza, N_CHIP, "mm_gwa")
    gWb = _mm_tn(y_b, dzb, N_CHIP, "mm_gwb")
    dya = _mm_nt(dza, Wa, F32, "mm_dya")
    dyb = _mm_nt(dzb, Wb, BF16, "mm_dyb")
    dq_a, dzf, dzbk, di_a, dog, dlbl, s_ng = _hgrn_bwd(p, lbl, hgrn_norm_g, o_a, dya, N, HA)
    dq_n, dk_n, dv_n, dbias, s_qk = _na_bwd(p, bias, na_q_norm_g, na_k_norm_g, cos, sin, dyb, N, off_na, HB)
    dp = jnp.concatenate([dq_a, dzf, dzbk, di_a, dog, dq_n, dk_n, dv_n, dga, dgb], axis=1)
    gWin = _mm_tn(h_all, dp, N_CHIP, "mm_gwin")
    dh = _mm_nt(dp, Win, F32, "mm_dh")
    grad_x, s_rms1 = _rms1_bwd(xall, dh, dxm, norm1_g, scale1, N)
    d_table = _bias_grad(dbias)

    zD = jnp.zeros((1, D), F32)
    dmod_l = jnp.concatenate([s_rms1[2:3], s_rms1[3:4], s_rms2[3:4], s_rms2[0:1], s_rms2[1:2], d_g2], axis=0)
    dmod_c = jnp.concatenate([s_rms1[0:1], s_rms1[1:2], zD, zD, zD, zD], axis=0)
    pk1, offs1 = _pack([dmod_l, dmod_c, s_rms1[4], s_rms2[2], dlbl, s_ng[0], s_qk[0], s_qk[1], d_table,
                        s_conv[0:3], s_conv[3]])
    g1all = _allgather8(pk1, "gather_small1")
    tot1 = _sum8(g1all, "sum_small1")
    dmod_rows = _unpack(g1all, offs1, 0).reshape(N_DEV, N_MOD * D)
    dmod_c_tot = _unpack(tot1, offs1, 1).reshape(1, N_MOD * D)
    dmod16 = jnp.concatenate([dmod_rows, dmod_c_tot, jnp.zeros((7, N_MOD * D), F32)], axis=0)
    dmod16_mine = lax.dynamic_slice(dmod16, (0, chip * n_ada), (16, n_ada))
    g_ada_w, dact = _ada_bwd(cs, ada_w[0], dmod16_mine)
    pk2, offs2 = _pack([dact[8]])
    g2all = _allgather8(pk2, "gather_small2")
    dact_rows = _unpack(g2all, offs2, 0)
    dact_sel = jnp.concatenate([dact_rows[2 * j][None] for j in range(N_CHIP)] + [jnp.zeros((4, D), F32)], axis=0)

    grads = {}
    grads["ada_w"] = g_ada_w[None]
    grads["ada_b"] = (_unpack(tot1, offs1, 0) + _unpack(tot1, offs1, 1)).reshape(1, N_MOD * D)
    grads["norm1_g"] = _unpack(tot1, offs1, 2)[None]
    grads["norm2_g"] = _unpack(tot1, offs1, 3)[None]
    g_lbl = _unpack(tot1, offs1, 4)
    n_lb = HA // N_CHIP
    grads["hgrn_lb_logits"] = lax.dynamic_slice(g_lbl, (0, 0, chip * n_lb), (2, 2, n_lb))
    grads["hgrn_norm_g"] = _unpack(tot1, offs1, 5)[None]
    grads["na_q_norm_g"] = _unpack(tot1, offs1, 6)[None]
    grads["na_k_norm_g"] = _unpack(tot1, offs1, 7)[None]
    grads["na_rel_bias"] = _unpack(tot1, offs1, 8)[None]
    g_cw = _unpack(tot1, offs1, 9)
    n_f = F // N_CHIP
    grads["ffn_conv_w"] = lax.dynamic_slice(g_cw, (0, chip * n_f), (3, n_f))[None]
    grads["ffn_conv_b"] = _unpack(tot1, offs1, 10)[None]

    big = _reduce_scatter([gWin, gWa, gWb, gWo, gW1, gW3, gW2])
    for nm, g in zip(["w_in", "w_branch_a", "w_branch_b", "w_out", "ffn_w1", "ffn_w3", "ffn_w2"], big):
        grads[nm] = g[None]

    grads["c_ctx"] = _dsilu_rows(dact_sel, c_ctx[None, :], "grad_c_ctx")[0]

    big_names = ["ada_w", "w_in", "w_branch_a", "w_branch_b", "w_out", "ffn_w1", "ffn_w3", "ffn_w2"]
    small_names = [n for n in order if n not in big_names]
    delta, new_m, new_v = {}, {}, {}
    for nm in big_names:
        w2 = weights[nm][0]
        d_, m_, v_ = _adamw(w2, grads[nm][0], moms[nm][0][0], moms[nm][1][0], "adamw_" + nm)
        delta[nm], new_m[nm], new_v[nm] = d_[None], m_[None], v_[None]
    pw, offw = _pack([weights[n] for n in small_names])
    pg, _ = _pack([grads[n] for n in small_names])
    pm, _ = _pack([moms[n][0] for n in small_names])
    pv, _ = _pack([moms[n][1] for n in small_names])
    d_, m_, v_ = _adamw(pw, pg, pm, pv, "adamw_small")
    for i, nm in enumerate(small_names):
        delta[nm], new_m[nm], new_v[nm] = _unpack(d_, offw, i), _unpack(m_, offw, i), _unpack(v_, offw, i)

    return (loss, grad_x[None], *[grads[n] for n in order], *[delta[n] for n in order],
            *[new_m[n] for n in order], *[new_v[n] for n in order])


def _dsilu_rows(v, cv, name):
    D = v.shape[1]

    def body(v_ref, c_ref, o_ref):
        t = c_ref[...]
        s = _sigmoid(t)
        o_ref[...] = (((v_ref[0:1, :] + v_ref[1:2, :]) + v_ref[2:3, :]) + v_ref[3:4, :]) * (s * (1.0 + t * (1.0 - s)))

    return pl.pallas_call(body, name=name, out_shape=jax.ShapeDtypeStruct((1, D), F32),
                          compiler_params=_params())(v, cv)
```

```python
import functools

import numpy as np
import jax
import jax.numpy as jnp
from jax import lax
from jax.experimental import pallas as pl
from jax.experimental.pallas import tpu as pltpu

F32 = jnp.float32
BF16 = jnp.bfloat16
MESH = pl.DeviceIdType.MESH

HEAD = 128
GRID_W = 64
WIN_R = 8
WIN_C = 16
ROPE_THETA = 10000.0
EPS = 1e-6
N_MOD = 6
CHUNK = 16
ADAM_LR = 0.001
ADAM_B1 = 0.9
ADAM_B2 = 0.999
ADAM_EPS = 1e-08
ADAM_WD = 0.01
ADAM_STEP = 10
NEG = -1e30
VMEM_LIMIT = 56 * 1024 * 1024
N_DEV = 8
N_CHIP = 4
HI = lax.Precision.HIGHEST


def _pick(n, cands):
    for c in cands:
        if n % c == 0:
            return c
    return n


def _row_tile(rows, cols, target_bytes=1 << 20):
    want = max(16, target_bytes // (4 * cols))
    for t in (512, 256, 128, 64, 32, 16, 8):
        if t <= want and rows % t == 0:
            return t
    return rows


def _params(sem=None):
    return pltpu.CompilerParams(dimension_semantics=sem, vmem_limit_bytes=VMEM_LIMIT)


def _dot(a, b):
    return jnp.dot(a, b, preferred_element_type=F32)


def _dot_nt(a, b):
    return lax.dot_general(a, b, (((1,), (1,)), ((), ())), preferred_element_type=F32)


def _dot_tn(a, b):
    return lax.dot_general(a, b, (((0,), (0,)), ((), ())), preferred_element_type=F32)


def _sigmoid(x):
    return 1.0 / (1.0 + jnp.exp(-x))


def _col_tile(n):
    return n if n <= 1536 else _pick(n, (1024, 768, 512, 384, 256, 128))


def _mm_nn(x, w3, out_dtype, name):
    M, K = x.shape
    S, _, n = w3.shape
    tm = _pick(M, (768, 512, 256, 128, 64))
    tn = _col_tile(n)
    nb = n // tn

    def body(x_ref, w_ref, o_ref):
        o_ref[...] = _dot(x_ref[...].astype(BF16), w_ref[0]).astype(o_ref.dtype)

    return _pcall(
        body, name=name, grid=(M // tm, S * nb),
        in_specs=[pl.BlockSpec((tm, K), lambda i, j: (i, 0)),
                  pl.BlockSpec((1, K, tn), lambda i, j: (j // nb, 0, j % nb))],
        out_specs=pl.BlockSpec((tm, tn), lambda i, j: (i, j)),
        out_shape=jax.ShapeDtypeStruct((M, S * n), out_dtype),
        compiler_params=_params(("parallel", "parallel")),
    )(x, w3)


def _mm_nt(dy, w3, out_dtype, name):
    M = dy.shape[0]
    S, K, n = w3.shape
    tm = _pick(M, (768, 512, 256, 128, 64))
    tk = _pick(K, (512, 256, 128))
    tc = _col_tile(n)
    nb = n // tc
    nsteps = S * nb

    def body(dy_ref, w_ref, o_ref, acc_ref):
        s = pl.program_id(2)

        @pl.when(s == 0)
        def _():
            acc_ref[...] = jnp.zeros_like(acc_ref)

        acc_ref[...] += _dot_nt(dy_ref[...].astype(BF16), w_ref[0])

        @pl.when(s == nsteps - 1)
        def _():
            o_ref[...] = acc_ref[...].astype(o_ref.dtype)

    return _pcall(
        body, name=name, grid=(M // tm, K // tk, nsteps),
        in_specs=[pl.BlockSpec((tm, tc), lambda i, k, s: (i, s)),
                  pl.BlockSpec((1, tk, tc), lambda i, k, s: (s // nb, k, s % nb))],
        out_specs=pl.BlockSpec((tm, tk), lambda i, k, s: (i, k)),
        out_shape=jax.ShapeDtypeStruct((M, K), out_dtype),
        scratch_shapes=[pltpu.VMEM((tm, tk), F32)],
        compiler_params=_params(("parallel", "parallel", "arbitrary")),
    )(dy, w3)


def _mm_tn(x, dy, S, name):
    M, K = x.shape
    n = dy.shape[1] // S
    tk = _pick(K, (512, 256, 128))
    tn = _col_tile(n)
    nb = n // tn

    def body(x_ref, dy_ref, o_ref):
        o_ref[0] = _dot_tn(x_ref[...].astype(BF16), dy_ref[...].astype(BF16))

    return _pcall(
        body, name=name, grid=(S * nb, K // tk),
        in_specs=[pl.BlockSpec((M, tk), lambda j, k: (0, k)),
                  pl.BlockSpec((M, tn), lambda j, k: (0, j))],
        out_specs=pl.BlockSpec((1, tk, tn), lambda j, k: (j // nb, k, j % nb)),
        out_shape=jax.ShapeDtypeStruct((S, K, n), F32),
        compiler_params=_params(("parallel", "parallel")),
    )(x, dy)


def _chip_index():
    return (2 * lax.axis_index("x") + lax.axis_index("y")).astype(jnp.int32).reshape(1)


def _cast_bf16_slot(w, name):
    R, C = w.shape
    tr = _row_tile(R, C, 2 << 20)

    def body(j_ref, w_ref, o_ref):
        o_ref[0] = w_ref[...].astype(BF16)

    return _pcall(
        body, name=name,
        grid_spec=pltpu.PrefetchScalarGridSpec(
            num_scalar_prefetch=1, grid=(R // tr,),
            in_specs=[pl.BlockSpec((tr, C), lambda i, j_ref: (i, 0))],
            out_specs=pl.BlockSpec((1, tr, C), lambda i, j_ref: (j_ref[0], i, 0))),
        out_shape=jax.ShapeDtypeStruct((N_CHIP, R, C), BF16),
        compiler_params=_params(("parallel",)),
    )(_chip_index(), w)


def _pos():
    return lax.axis_index("x"), lax.axis_index("y"), lax.axis_index("c")


def _other_chips(x, y):
    return [(x, 1 - y), (1 - x, y), (1 - x, 1 - y)]


def _allgather8(v, name):
    R, C = v.shape

    def body(x_ref, out_ref, send_sems, recv_sems, local_sem):
        x, y, c = _pos()
        me, sibling = (x, y, c), (x, y, 1 - c)
        chips = _other_chips(x, y)

        def slot(px, py, pc):
            return out_ref.at[4 * px + 2 * py + pc]

        def copy(k, block, to, src=None):
            return pltpu.make_async_remote_copy(
                src_ref=slot(*block) if src is None else src, dst_ref=slot(*block),
                send_sem=send_sems.at[k], recv_sem=recv_sems.at[k], device_id=to, device_id_type=MESH)

        mine = pltpu.make_async_copy(x_ref, slot(*me), local_sem)
        mine.start()
        first = [copy(0, me, sibling, src=x_ref)]
        first += [copy(1 + j, me, (*chip, c), src=x_ref) for j, chip in enumerate(chips)]
        for cp in first:
            cp.start()
        passed = [copy(4 + j, (*chip, c), sibling) for j, chip in enumerate(chips)]
        for j, chip in enumerate(chips):
            copy(1 + j, (*chip, c), me).wait_recv()
            passed[j].start()
        copy(0, sibling, me).wait_recv()
        for j, chip in enumerate(chips):
            copy(4 + j, (*chip, 1 - c), me).wait_recv()
        for cp in first + passed:
            cp.wait_send()
        mine.wait()

    return _pcall(
        body, name=name,
        out_shape=jax.ShapeDtypeStruct((N_DEV, R, C), v.dtype),
        in_specs=[pl.BlockSpec(memory_space=pltpu.VMEM)],
        out_specs=pl.BlockSpec(memory_space=pltpu.VMEM),
        scratch_shapes=[pltpu.SemaphoreType.DMA((7,)), pltpu.SemaphoreType.DMA((7,)), pltpu.SemaphoreType.DMA],
        compiler_params=pltpu.CompilerParams(vmem_limit_bytes=VMEM_LIMIT),
    )(v)


_HBM = pl.BlockSpec(memory_space=pltpu.HBM)
_SEM = pl.BlockSpec(memory_space=pltpu.SEMAPHORE)
_ANY = pl.BlockSpec(memory_space=pl.ANY)
_EFFECT = pltpu.SideEffectType.DATAFLOW_SIDE_EFFECTING
_PENDING = []


def _pcall(body, **kw):
    def run(*operands):
        if not _PENDING or "in_specs" not in kw:
            return pl.pallas_call(body, **kw)(*operands)
        deps = list(_PENDING)
        _PENDING.clear()
        n = len(operands)

        def tied(*refs):
            return body(*refs[:n], *refs[n + len(deps):])

        return pl.pallas_call(tied, **{**kw, "in_specs": list(kw["in_specs"]) + [_ANY] * len(deps)})(*operands, *deps)
    return run


def _copies(plan, refs, send_sems, recv_sems):
    return [pltpu.make_async_remote_copy(src_ref=src, dst_ref=dst, send_sem=send_sems.at[k], recv_sem=recv_sems.at[k],
                                         device_id=dev, device_id_type=MESH)
            for k, (src, dst, dev) in enumerate(plan(refs))]


def _xfer_start(name, bufs, plan, n_copies):
    n = len(bufs)

    def body(*refs):
        for cp in _copies(plan, refs[:n], refs[n], refs[n + 1]):
            cp.start()
        refs[-1][...] = jnp.zeros_like(refs[-1])

    outs = pl.pallas_call(
        body, name=name,
        out_shape=(pltpu.SemaphoreType.DMA((n_copies,)), pltpu.SemaphoreType.DMA((n_copies,)),
                   *[pltpu.HBM(b.shape, b.dtype) for b in bufs], jax.ShapeDtypeStruct((8, 128), F32)),
        in_specs=[_HBM] * n,
        out_specs=(_SEM, _SEM, *[_HBM] * n, pl.BlockSpec(memory_space=pltpu.VMEM)),
        input_output_aliases={t: 2 + t for t in range(n)},
        compiler_params=pltpu.CompilerParams(has_side_effects=_EFFECT),
    )(*[pltpu.with_memory_space_constraint(b, pltpu.HBM) for b in bufs])
    _PENDING.append(outs[-1])
    return (outs[0], outs[1]), list(outs[2:2 + n])


def _xfer_wait(name, sems, bufs, plan, after):
    n = len(bufs)

    def body(*refs):
        cps = _copies(plan, refs[:n], refs[n], refs[n + 1])
        for cp in cps:
            cp.wait_send()
        for cp in cps:
            cp.wait_recv()

    outs = pl.pallas_call(
        body, name=name,
        out_shape=tuple(pltpu.HBM(b.shape, b.dtype) for b in bufs),
        in_specs=[_HBM] * n + [_SEM, _SEM, _ANY],
        out_specs=tuple([_HBM] * n),
        input_output_aliases={t: t for t in range(n)},
        compiler_params=pltpu.CompilerParams(has_side_effects=_EFFECT),
    )(*bufs, sems[0], sems[1], after)
    return list(outs)


def _half(ref_rows, hc):
    h = ref_rows // 2
    return pl.ds(hc * h, h)


def _plan_gather_ici(bufs):
    x, y, c = _pos()
    j = 2 * x + y
    return [(b.at[j, _half(b.shape[1], c)], b.at[j, _half(b.shape[1], c)], (*chip, c))
            for b in bufs for chip in _other_chips(x, y)]


def _plan_gather_d2d(bufs):
    x, y, c = _pos()
    out = []
    for b in bufs:
        for chip in _other_chips(x, y):
            blk = b.at[2 * chip[0] + chip[1], _half(b.shape[1], c)]
            out.append((blk, blk, (x, y, 1 - c)))
    return out


def _plan_pair_swap(n):
    def plan(bufs):
        x, y, c = _pos()
        return [(g.at[:, _half(g.shape[1], 1 - c)], land, (x, y, 1 - c)) for g, land in zip(bufs[:n], bufs[n:])]
    return plan


def _plan_chip_scatter(n):
    def plan(bufs):
        x, y, c = _pos()
        return [(p.at[2 * chip[0] + chip[1]], land.at[k], (*chip, c))
                for p, land in zip(bufs[:n], bufs[n:]) for k, chip in enumerate(_other_chips(x, y))]
    return plan


def _plan_pair_join(bufs):
    x, y, c = _pos()
    return [(b.at[_half(b.shape[0], c)], b.at[_half(b.shape[0], c)], (x, y, 1 - c)) for b in bufs]


def _empty_hbm(shape, dtype):
    return pltpu.with_memory_space_constraint(lax.empty(shape, dtype), pltpu.HBM)


def _gather_start(tag, bufs):
    sems, bufs = _xfer_start(f"gather_ici_start_{tag}", bufs, _plan_gather_ici, 3 * len(bufs))
    return dict(tag=tag, sems=sems, bufs=bufs)


def _gather_mid(st, after):
    tag = st["tag"]
    bufs = _xfer_wait(f"gather_ici_wait_{tag}", st["sems"], st["bufs"], _plan_gather_ici, after)
    sems, bufs = _xfer_start(f"gather_d2d_start_{tag}", bufs, _plan_gather_d2d, 3 * len(bufs))
    return dict(tag=tag, sems=sems, bufs=bufs)


def _gather_finish(st, after):
    return _xfer_wait(f"gather_d2d_wait_{st['tag']}", st["sems"], st["bufs"], _plan_gather_d2d, after)


def _pair_add(g, r, name):
    S, R, C = g.shape
    h = R // 2
    tr = _row_tile(h, C)
    nb = h // tr

    def body(c_ref, g_ref, r_ref, o_ref):
        o_ref[...] = (g_ref[...] + r_ref[...]).astype(BF16)

    return _pcall(
        body, name=name,
        grid_spec=pltpu.PrefetchScalarGridSpec(
            num_scalar_prefetch=1, grid=(S, nb),
            in_specs=[pl.BlockSpec((1, tr, C), lambda s, i, c_ref: (s, c_ref[0] * nb + i, 0)),
                      pl.BlockSpec((1, tr, C), lambda s, i, c_ref: (s, i, 0))],
            out_specs=pl.BlockSpec((1, tr, C), lambda s, i, c_ref: (s, i, 0))),
        out_shape=jax.ShapeDtypeStruct((S, h, C), BF16),
        compiler_params=_params(("parallel", "parallel")),
    )(lax.axis_index("c").astype(jnp.int32).reshape(1), g, r)


def _chip_sum(p, rb, name):
    S, h, C = p.shape
    tr = _row_tile(h, C)
    nb = h // tr
    jc = jnp.concatenate([_chip_index(), lax.axis_index("c").astype(jnp.int32).reshape(1)])

    def body(jc_ref, p_ref, r_ref, o_ref):
        o_ref[...] = ((p_ref[0].astype(F32) + r_ref[0].astype(F32)) + r_ref[1].astype(F32)) + r_ref[2].astype(F32)

    return _pcall(
        body, name=name,
        grid_spec=pltpu.PrefetchScalarGridSpec(
            num_scalar_prefetch=1, grid=(nb,),
            in_specs=[pl.BlockSpec((1, tr, C), lambda i, jc_ref: (jc_ref[0], i, 0)),
                      pl.BlockSpec((3, tr, C), lambda i, jc_ref: (0, i, 0))],
            out_specs=pl.BlockSpec((tr, C), lambda i, jc_ref: (jc_ref[1] * nb + i, 0))),
        out_shape=jax.ShapeDtypeStruct((2 * h, C), F32),
        compiler_params=_params(("parallel",)),
    )(jc, p, rb)


def _rs_start(tag, gs):
    n = len(gs)
    lands = [_empty_hbm((g.shape[0], g.shape[1] // 2, g.shape[2]), g.dtype) for g in gs]
    sems, bufs = _xfer_start(f"rs_swap_start_{tag}", list(gs) + lands, _plan_pair_swap(n), n)
    return dict(tag=tag, n=n, sems=sems, bufs=bufs)


def _rs_scatter(st, after):
    tag, n = st["tag"], st["n"]
    bufs = _xfer_wait(f"rs_swap_wait_{tag}", st["sems"], st["bufs"], _plan_pair_swap(n), after)
    ps = [_pair_add(g, r, f"rs_pair_add_{tag}{t}") for t, (g, r) in enumerate(zip(bufs[:n], bufs[n:]))]
    lands = [_empty_hbm((3,) + p.shape[1:], p.dtype) for p in ps]
    sems, bufs = _xfer_start(f"rs_scatter_start_{tag}", ps + lands, _plan_chip_scatter(n), 3 * n)
    return dict(tag=tag, n=n, sems=sems, bufs=bufs)


def _rs_join(st, after):
    tag, n = st["tag"], st["n"]
    bufs = _xfer_wait(f"rs_scatter_wait_{tag}", st["sems"], st["bufs"], _plan_chip_scatter(n), after)
    fs = [_chip_sum(p, rb, f"rs_chip_sum_{tag}{t}") for t, (p, rb) in enumerate(zip(bufs[:n], bufs[n:]))]
    sems, bufs = _xfer_start(f"rs_join_start_{tag}", fs, _plan_pair_join, n)
    return dict(tag=tag, n=n, sems=sems, bufs=bufs)


def _rs_finish(st, after):
    return _xfer_wait(f"rs_join_wait_{st['tag']}", st["sems"], st["bufs"], _plan_pair_join, after)


def _sum8(g, name):
    _, R, C = g.shape

    def body(g_ref, o_ref):
        acc = g_ref[0]
        for d in range(1, N_DEV):
            acc = acc + g_ref[d]
        o_ref[...] = acc

    return _pcall(body, name=name, out_shape=jax.ShapeDtypeStruct((R, C), F32),
                          compiler_params=_params())(g)


def _ada_fwd(cs, w, b):
    D, n = w.shape
    tn = _pick(n, (512, 384, 256, 128))

    def body(c_ref, w_ref, b_ref, o_ref):
        cv = c_ref[...]
        a = (cv * _sigmoid(cv)).astype(BF16)
        o_ref[...] = _dot(a, w_ref[...].astype(BF16)) + b_ref[...]

    return _pcall(
        body, name="ada_fwd", grid=(n // tn,),
        in_specs=[pl.BlockSpec((16, D), lambda j: (0, 0)), pl.BlockSpec((D, tn), lambda j: (0, j)),
                  pl.BlockSpec((1, tn), lambda j: (0, j))],
        out_specs=pl.BlockSpec((16, tn), lambda j: (0, j)),
        out_shape=jax.ShapeDtypeStruct((16, n), F32),
        compiler_params=_params(("parallel",)),
    )(cs, w, b)


def _ada_bwd(cs, w, dmod):
    D, n = w.shape
    tn = _pick(n, (512, 384, 256, 128))

    def body(c_ref, w_ref, d_ref, gw_ref, da_ref):
        j = pl.program_id(0)
        cv = c_ref[...]
        a = cv * _sigmoid(cv)
        d = d_ref[...]
        gw_ref[...] = lax.dot_general(a, d, (((0,), (0,)), ((), ())), precision=HI, preferred_element_type=F32)

        @pl.when(j == 0)
        def _():
            da_ref[...] = jnp.zeros_like(da_ref)

        da_ref[...] += _dot_nt(d.astype(BF16), w_ref[...].astype(BF16))

    return _pcall(
        body, name="ada_bwd", grid=(n // tn,),
        in_specs=[pl.BlockSpec((16, D), lambda j: (0, 0)), pl.BlockSpec((D, tn), lambda j: (0, j)),
                  pl.BlockSpec((16, tn), lambda j: (0, j))],
        out_specs=[pl.BlockSpec((D, tn), lambda j: (0, j)), pl.BlockSpec((16, D), lambda j: (0, 0))],
        out_shape=[jax.ShapeDtypeStruct((D, n), F32), jax.ShapeDtypeStruct((16, D), F32)],
        compiler_params=_params(("arbitrary",)),
    )(cs, w, dmod)


def _rms1_fwd(xall, gain, shift2, scale2, n_ctx):
    T, D = xall.shape
    tb = _pick(n_ctx, (256, 128, 64, 32, 16))
    nctx = n_ctx // tb

    def body(x_ref, g_ref, sh_ref, sc_ref, o_ref):
        i = pl.program_id(0)
        xv = x_ref[...]
        r = lax.rsqrt(jnp.mean(xv * xv, axis=-1, keepdims=True) + EPS)
        nrm = xv * r * g_ref[...]
        lat = i >= nctx
        sh = jnp.where(lat, sh_ref[1:2, :], sh_ref[0:1, :])
        sc = jnp.where(lat, sc_ref[1:2, :], sc_ref[0:1, :])
        o_ref[...] = (nrm * (1.0 + sc) + sh).astype(BF16)

    vec = lambda r: pl.BlockSpec((r, D), lambda i: (0, 0))
    return _pcall(
        body, name="rms1_fwd", grid=(T // tb,),
        in_specs=[pl.BlockSpec((tb, D), lambda i: (i, 0)), vec(1), vec(2), vec(2)],
        out_specs=pl.BlockSpec((tb, D), lambda i: (i, 0)),
        out_shape=jax.ShapeDtypeStruct((T, D), BF16),
        compiler_params=_params(("parallel",)),
    )(xall, gain, shift2, scale2)


def _rms1_bwd(xall, dh, dxmid, gain, scale2, n_ctx):
    T, D = xall.shape
    L = T - n_ctx
    tb = _pick(n_ctx, (256, 128, 64, 32, 16))
    nctx = n_ctx // tb

    def body(x_ref, dh_ref, dxm_ref, g_ref, sc_ref, dx_ref, cs_ref):
        i = pl.program_id(0)
        lat = i >= nctx
        xv = x_ref[...]
        r = lax.rsqrt(jnp.mean(xv * xv, axis=-1, keepdims=True) + EPS)
        xh = xv * r
        g = g_ref[...]
        nrm = xh * g
        sc = jnp.where(lat, sc_ref[1:2, :], sc_ref[0:1, :])
        dhv = dh_ref[...]
        dn = dhv * (1.0 + sc)
        dxh = dn * g
        dxv = r * (dxh - xh * jnp.mean(dxh * xh, axis=-1, keepdims=True))
        s_sh = jnp.sum(dhv, axis=0, keepdims=True)
        s_sc = jnp.sum(dhv * nrm, axis=0, keepdims=True)
        s_g = jnp.sum(dn * xh, axis=0, keepdims=True)
        zero = jnp.zeros_like(s_sh)
        rows = lax.broadcasted_iota(jnp.int32, (8, D), 0)
        upd = jnp.where(rows == 0, jnp.where(lat, zero, s_sh),
              jnp.where(rows == 1, jnp.where(lat, zero, s_sc),
              jnp.where(rows == 2, jnp.where(lat, s_sh, zero),
              jnp.where(rows == 3, jnp.where(lat, s_sc, zero),
              jnp.where(rows == 4, s_g, 0.0)))))

        @pl.when(i == 0)
        def _():
            cs_ref[...] = jnp.zeros_like(cs_ref)

        cs_ref[...] += upd

        @pl.when(lat)
        def _():
            dx_ref[...] = dxv + dxm_ref[...]

    lat_blk = lambda i: (jnp.maximum(i - nctx, 0), 0)
    vec = lambda r: pl.BlockSpec((r, D), lambda i: (0, 0))
    return _pcall(
        body, name="rms1_bwd", grid=(T // tb,),
        in_specs=[pl.BlockSpec((tb, D), lambda i: (i, 0)), pl.BlockSpec((tb, D), lambda i: (i, 0)),
                  pl.BlockSpec((tb, D), lat_blk), vec(1), vec(2)],
        out_specs=[pl.BlockSpec((tb, D), lat_blk), vec(8)],
        out_shape=[jax.ShapeDtypeStruct((L, D), F32), jax.ShapeDtypeStruct((8, D), F32)],
        compiler_params=_params(("arbitrary",)),
    )(xall, dh, dxmid, gain, scale2)


def _resid_rms2_fwd(x, mo, vecs):
    L, D = x.shape
    tb = _pick(L, (256, 128, 64))

    def body(x_ref, mo_ref, v_ref, xm_ref, h_ref):
        xm = x_ref[...] + v_ref[0:1, :] * mo_ref[...]
        xm_ref[...] = xm
        r = lax.rsqrt(jnp.mean(xm * xm, axis=-1, keepdims=True) + EPS)
        h_ref[...] = (xm * r * v_ref[1:2, :] * (1.0 + v_ref[3:4, :]) + v_ref[2:3, :]).astype(BF16)

    blk = pl.BlockSpec((tb, D), lambda i: (i, 0))
    return _pcall(
        body, name="resid_rms2_fwd", grid=(L // tb,),
        in_specs=[blk, blk, pl.BlockSpec((8, D), lambda i: (0, 0))],
        out_specs=[blk, blk],
        out_shape=[jax.ShapeDtypeStruct((L, D), F32), jax.ShapeDtypeStruct((L, D), BF16)],
        compiler_params=_params(("parallel",)),
    )(x, mo, vecs)


def _resid_rms2_bwd(xmid, dh_a, dh_b, dy, mo, vecs):
    L, D = xmid.shape
    tb = _pick(L, (256, 128, 64))

    def body(xm_ref, da_ref, db_ref, dy_ref, mo_ref, v_ref, dxm_ref, dmo_ref, cs_ref):
        i = pl.program_id(0)
        xm = xm_ref[...]
        r = lax.rsqrt(jnp.mean(xm * xm, axis=-1, keepdims=True) + EPS)
        xh = xm * r
        g = v_ref[1:2, :]
        nrm = xh * g
        dhv = da_ref[...] + db_ref[...]
        dn = dhv * (1.0 + v_ref[3:4, :])
        dxh = dn * g
        dxm = dy_ref[...] + r * (dxh - xh * jnp.mean(dxh * xh, axis=-1, keepdims=True))
        dxm_ref[...] = dxm
        dmo_ref[...] = (dxm * v_ref[0:1, :]).astype(BF16)
        s0 = jnp.sum(dhv, axis=0, keepdims=True)
        s1 = jnp.sum(dhv * nrm, axis=0, keepdims=True)
        s2 = jnp.sum(dn * xh, axis=0, keepdims=True)
        s3 = jnp.sum(dxm * mo_ref[...], axis=0, keepdims=True)
        rows = lax.broadcasted_iota(jnp.int32, (8, D), 0)
        upd = jnp.where(rows == 0, s0, jnp.where(rows == 1, s1, jnp.where(rows == 2, s2,
              jnp.where(rows == 3, s3, 0.0))))

        @pl.when(i == 0)
        def _():
            cs_ref[...] = jnp.zeros_like(cs_ref)

        cs_ref[...] += upd

    blk = pl.BlockSpec((tb, D), lambda i: (i, 0))
    vec = pl.BlockSpec((8, D), lambda i: (0, 0))
    return _pcall(
        body, name="resid_rms2_bwd", grid=(L // tb,),
        in_specs=[blk, blk, blk, blk, blk, vec],
        out_specs=[blk, blk, vec],
        out_shape=[jax.ShapeDtypeStruct((L, D), F32), jax.ShapeDtypeStruct((L, D), BF16),
                   jax.ShapeDtypeStruct((8, D), F32)],
        compiler_params=_params(("arbitrary",)),
    )(xmid, dh_a, dh_b, dy, mo, vecs)


def _loss_head(xmid, f, g2, target):
    L, D = xmid.shape
    tb = _pick(L, (256, 128, 64))

    def body(xm_ref, f_ref, g_ref, t_ref, dy_ref, df_ref, s_ref):
        i = pl.program_id(0)
        fv = f_ref[...]
        g = g_ref[...]
        err = xm_ref[...] + g * fv - t_ref[...]
        dy = err * (1.0 / D)
        dy_ref[...] = dy
        df_ref[...] = (dy * g).astype(BF16)
        s0 = jnp.sum(dy * fv, axis=0, keepdims=True)
        part = 0.5 * jnp.sum(jnp.mean(err * err, axis=-1, keepdims=True), axis=0, keepdims=True)
        rows = lax.broadcasted_iota(jnp.int32, (8, D), 0)
        upd = jnp.where(rows == 0, s0, jnp.where(rows == 1, part, 0.0))

        @pl.when(i == 0)
        def _():
            s_ref[...] = jnp.zeros_like(s_ref)

        s_ref[...] += upd

    blk = pl.BlockSpec((tb, D), lambda i: (i, 0))
    return _pcall(
        body, name="loss_head", grid=(L // tb,),
        in_specs=[blk, blk, pl.BlockSpec((1, D), lambda i: (0, 0)), blk],
        out_specs=[blk, blk, pl.BlockSpec((8, D), lambda i: (0, 0))],
        out_shape=[jax.ShapeDtypeStruct((L, D), F32), jax.ShapeDtypeStruct((L, D), BF16),
                   jax.ShapeDtypeStruct((8, D), F32)],
        compiler_params=_params(("arbitrary",)),
    )(xmid, f, g2, target)


def _gate_cols(D, off):
    tc = _pick(np.gcd(D, off), (512, 256, 128))
    return tc, off // tc


def _merge_fwd(za, zb, p, n_ctx, off_a, off_b):
    L, D = za.shape
    tb = _pick(n_ctx, (256, 128, 64, 32, 16))
    nctx = n_ctx // tb
    tc, oa = _gate_cols(D, off_a)
    _, ob = _gate_cols(D, off_b)
    if off_b % tc:
        raise ValueError("gate column offsets must share a column tile")
    ob = off_b // tc

    def body(za_ref, zb_ref, ga_ref, gb_ref, z_ref):
        z_ref[...] = (_sigmoid(ga_ref[...]) * za_ref[...] + _sigmoid(gb_ref[...]) * zb_ref[...]).astype(BF16)

    blk = pl.BlockSpec((tb, tc), lambda i, j: (i, j))
    return _pcall(
        body, name="merge_fwd", grid=(L // tb, D // tc),
        in_specs=[blk, blk, pl.BlockSpec((tb, tc), lambda i, j: (i + nctx, oa + j)),
                  pl.BlockSpec((tb, tc), lambda i, j: (i + nctx, ob + j))],
        out_specs=blk,
        out_shape=jax.ShapeDtypeStruct((L, D), BF16),
        compiler_params=_params(("parallel", "parallel")),
    )(za, zb, p, p)


def _merge_bwd(dz, za, zb, p, n_ctx, off_a, off_b):
    L, D = za.shape
    T = L + n_ctx
    tb = _pick(n_ctx, (256, 128, 64, 32, 16))
    nctx = n_ctx // tb
    tc = _gate_cols(D, off_a)[0]
    oa, ob = off_a // tc, off_b // tc

    def body(dz_ref, za_ref, zb_ref, ga_ref, gb_ref, dza_ref, dzb_ref, dga_ref, dgb_ref):
        i = pl.program_id(1)

        @pl.when(i < nctx)
        def _():
            dga_ref[...] = jnp.zeros_like(dga_ref)
            dgb_ref[...] = jnp.zeros_like(dgb_ref)

        @pl.when(i >= nctx)
        def _():
            dzv = dz_ref[...]
            sa = _sigmoid(ga_ref[...])
            sb = _sigmoid(gb_ref[...])
            dza_ref[...] = (dzv * sa).astype(BF16)
            dzb_ref[...] = (dzv * sb).astype(BF16)
            dga_ref[...] = (dzv * za_ref[...] * sa * (1.0 - sa)).astype(BF16)
            dgb_ref[...] = (dzv * zb_ref[...] * sb * (1.0 - sb)).astype(BF16)

    lat = pl.BlockSpec((tb, tc), lambda j, i: (jnp.maximum(i - nctx, 0), j))
    allr = pl.BlockSpec((tb, tc), lambda j, i: (i, j))
    return _pcall(
        body, name="merge_bwd", grid=(D // tc, T // tb),
        in_specs=[lat, lat, lat, pl.BlockSpec((tb, tc), lambda j, i: (i, oa + j)),
                  pl.BlockSpec((tb, tc), lambda j, i: (i, ob + j))],
        out_specs=[lat, lat, allr, allr],
        out_shape=[jax.ShapeDtypeStruct((L, D), BF16), jax.ShapeDtypeStruct((L, D), BF16),
                   jax.ShapeDtypeStruct((T, D), BF16), jax.ShapeDtypeStruct((T, D), BF16)],
        compiler_params=_params(("arbitrary", "arbitrary")),
    )(dz, za, zb, p, p)


def _shift_down(u, rows):
    return jnp.where(rows == 0, 0.0, pltpu.roll(u, 1, 0))


def _shift_up(u, rows):
    n = u.shape[0]
    return jnp.where(rows == n - 1, 0.0, pltpu.roll(u, n - 1, 0))


def _convgate_fwd(u1, u3, cw, cb):
    L, F = u1.shape
    tc = _pick(F, (256, 128))

    def body(u1_ref, u3_ref, w_ref, b_ref, a_ref):
        u = u1_ref[...]
        rows = lax.broadcasted_iota(jnp.int32, u.shape, 0)
        cv = _shift_down(u, rows) * w_ref[0:1, :] + u * w_ref[1:2, :] + _shift_up(u, rows) * w_ref[2:3, :] + b_ref[...]
        a_ref[...] = (cv * _sigmoid(cv) * u3_ref[...]).astype(BF16)

    blk = pl.BlockSpec((L, tc), lambda j: (0, j))
    return _pcall(
        body, name="convgate_fwd", grid=(F // tc,),
        in_specs=[blk, blk, pl.BlockSpec((8, tc), lambda j: (0, j)), pl.BlockSpec((1, tc), lambda j: (0, j))],
        out_specs=blk,
        out_shape=jax.ShapeDtypeStruct((L, F), BF16),
        compiler_params=_params(("parallel",)),
    )(u1, u3, cw, cb)


def _convgate_bwd(u1, u3, da, cw, cb):
    L, F = u1.shape
    tc = _pick(F, (256, 128))

    def body(u1_ref, u3_ref, da_ref, w_ref, b_ref, du1_ref, du3_ref, s_ref):
        u = u1_ref[...]
        rows = lax.broadcasted_iota(jnp.int32, u.shape, 0)
        um, up = _shift_down(u, rows), _shift_up(u, rows)
        w0, w1, w2 = w_ref[0:1, :], w_ref[1:2, :], w_ref[2:3, :]
        cv = um * w0 + u * w1 + up * w2 + b_ref[...]
        s = _sigmoid(cv)
        dav = da_ref[...]
        du3_ref[...] = (dav * cv * s).astype(BF16)
        dcv = dav * u3_ref[...] * (s * (1.0 + cv * (1.0 - s)))
        du1_ref[...] = (_shift_up(dcv, rows) * w0 + dcv * w1 + _shift_down(dcv, rows) * w2).astype(BF16)
        r8 = lax.broadcasted_iota(jnp.int32, (8, tc), 0)
        s0 = jnp.sum(dcv * um, axis=0, keepdims=True)
        s1 = jnp.sum(dcv * u, axis=0, keepdims=True)
        s2 = jnp.sum(dcv * up, axis=0, keepdims=True)
        s3 = jnp.sum(dcv, axis=0, keepdims=True)
        s_ref[...] = jnp.where(r8 == 0, s0, jnp.where(r8 == 1, s1, jnp.where(r8 == 2, s2,
                     jnp.where(r8 == 3, s3, 0.0))))

    blk = pl.BlockSpec((L, tc), lambda j: (0, j))
    v8 = pl.BlockSpec((8, tc), lambda j: (0, j))
    return _pcall(
        body, name="convgate_bwd", grid=(F // tc,),
        in_specs=[blk, blk, blk, v8, pl.BlockSpec((1, tc), lambda j: (0, j))],
        out_specs=[blk, blk, v8],
        out_shape=[jax.ShapeDtypeStruct((L, F), BF16), jax.ShapeDtypeStruct((L, F), BF16),
                   jax.ShapeDtypeStruct((8, F), F32)],
        compiler_params=_params(("parallel",)),
    )(u1, u3, da, cw, cb)


def _lower_bound(lbl_ref, d):
    l0, l1 = lbl_ref[d, 0:1, :], lbl_ref[d, 1:2, :]
    m = jnp.maximum(l0, l1)
    e0, e1 = jnp.exp(l0 - m), jnp.exp(l1 - m)
    return e0 / (e0 + e1)


def _chunk_cumsum(x, rev):
    n = x.shape[0]
    r = lax.broadcasted_iota(jnp.int32, x.shape, 0) % CHUNK
    k = 1
    while k < CHUNK:
        if rev:
            x = x + jnp.where(r < CHUNK - k, pltpu.roll(x, n - k, 0), 0.0)
        else:
            x = x + jnp.where(r >= k, pltpu.roll(x, k, 0), 0.0)
        k *= 2
    return x


def _gate_terms(z, lb):
    sg = _sigmoid(z)
    f = lb + (1.0 - lb) * sg
    return sg, f


def _decay_terms(z, lb, rev):
    _, f = _gate_terms(z, lb)
    g = jnp.log(f)
    return 1.0 - f, _chunk_cumsum(g, rev), _chunk_cumsum(g, not rev) - g


def _chunk_total(c, rev):
    return c[0:1, :] if rev else c[CHUNK - 1:CHUNK, :]


def _pair_decay(c, s, rev):
    t = lax.broadcasted_iota(jnp.int32, (CHUNK, 1), 0)
    later = (t <= s) if rev else (t >= s)
    return jnp.where(later, jnp.exp(jnp.minimum(c - c[s:s + 1, :], 0.0)), 0.0)


def _scan_chunk(i, n_ctx_chunks, n_chunks, rev):
    if not rev:
        return i
    return jnp.where(i < n_ctx_chunks, n_ctx_chunks - 1 - i, n_chunks + n_ctx_chunks - 1 - i)


def _rows(ci):
    return pl.ds(pl.multiple_of(ci * CHUNK, CHUNK), CHUNK)


def _hgrn_cols(HA):
    return HA // HEAD


def _hgrn_fwd(p, lbl, ng, n_ctx, HA):
    T = p.shape[0]
    L = T - n_ctx
    nh = _hgrn_cols(HA)
    nc, ncc = T // CHUNK, n_ctx // CHUNK

    def body(q_ref, zf_ref, zb_ref, v_ref, og_ref, lbl_ref, ng_ref, ya_ref, o_ref,
             c_scr, k_scr, qe_scr, ke_scr, o_scr):
        dirs = ((0, False, zf_ref), (1, True, zb_ref))
        for d, rev, z_ref in dirs:
            k, c, rest = _decay_terms(z_ref[...], _lower_bound(lbl_ref, d), rev)
            c_scr[d] = c
            k_scr[d] = k
            qe_scr[d] = (q_ref[...] * jnp.exp(c)).astype(BF16)
            ke_scr[d] = (k * jnp.exp(rest)).astype(BF16)

        def step(i, states):
            new = []
            for (d, rev, _), St in zip(dirs, states):
                rows = _rows(_scan_chunk(i, ncc, nc, rev))
                q, v, c, k = q_ref[rows, :], v_ref[rows, :], c_scr[d, rows, :], k_scr[d, rows, :]
                o = _dot_nt(qe_scr[d, rows, :], St.astype(BF16))
                for s in range(CHUNK):
                    E = _pair_decay(c, s, rev)
                    a = jnp.sum(q * E * k[s:s + 1, :], axis=1, keepdims=True)
                    o = o + a * v[s:s + 1, :]
                o_scr[d, rows, :] = o
                new.append(St * jnp.exp(_chunk_total(c, rev)) + _dot_tn(v.astype(BF16), ke_scr[d, rows, :]))
            return tuple(new)

        zero = jnp.zeros((HEAD, HEAD), F32)
        lax.fori_loop(0, nc, step, (zero, zero), unroll=2)

        o = o_scr[0, pl.ds(n_ctx, L), :] + o_scr[1, pl.ds(n_ctx, L), :]
        o_ref[...] = o
        r = lax.rsqrt(jnp.mean(o * o, axis=-1, keepdims=True) + EPS)
        og = og_ref[pl.ds(n_ctx, L), :]
        ya_ref[...] =(o * r * ng_ref[...] * (og * _sigmoid(og))).astype(BF16)

    cb = HA // HEAD
    col = lambda kk: pl.BlockSpec((T, HEAD), lambda h: (0, kk * cb + h))
    return _pcall(
        body, name="hgrn_fwd", grid=(nh,),
        in_specs=[col(0), col(1), col(2), col(3), col(4),
                  pl.BlockSpec((2, 2, HEAD), lambda h: (0, 0, h)), pl.BlockSpec((1, HEAD), lambda h: (0, 0))],
        out_specs=[pl.BlockSpec((L, HEAD), lambda h: (0, h)), pl.BlockSpec((L, HEAD), lambda h: (0, h))],
        out_shape=[jax.ShapeDtypeStruct((L, HA), BF16), jax.ShapeDtypeStruct((L, HA), F32)],
        scratch_shapes=[pltpu.VMEM((2, T, HEAD), F32), pltpu.VMEM((2, T, HEAD), F32),
                        pltpu.VMEM((2, T, HEAD), BF16), pltpu.VMEM((2, T, HEAD), BF16),
                        pltpu.VMEM((2, T, HEAD), F32)],
        compiler_params=_params(("parallel",)),
    )(p, p, p, p, p, lbl, ng)


def _hgrn_bwd(p, lbl, ng, o, dya, n_ctx, HA):
    T = p.shape[0]
    L = T - n_ctx
    nh = _hgrn_cols(HA)
    nc, ncc = T // CHUNK, n_ctx // CHUNK

    def body(q_ref, zf_ref, zb_ref, v_ref, og_ref, lbl_ref, ng_ref, o_ref, dya_ref,
             dq_ref, dzf_ref, dzb_ref, dv_ref, dog_ref, dlbl_ref, dng_ref,
             do_scr, st_scr, c_scr, k_scr, qe_scr, ke_scr, dg_scr, dk_scr, dq_scr, dv_scr):
        h = pl.program_id(0)
        ov = o_ref[...]
        r = lax.rsqrt(jnp.mean(ov * ov, axis=-1, keepdims=True) + EPS)
        oh = ov * r
        ogv = og_ref[pl.ds(n_ctx, L), :]
        sg_o = _sigmoid(ogv)
        dyv = dya_ref[...]
        ngv = ng_ref[...]
        dog_ref[pl.ds(0, n_ctx), :] = jnp.zeros((n_ctx, HEAD), BF16)
        dog_ref[pl.ds(n_ctx, L), :] = (dyv * oh * ngv * (sg_o * (1.0 + ogv * (1.0 - sg_o)))).astype(BF16)
        don = dyv * (ogv * sg_o)
        dng = jnp.sum(don * oh, axis=0, keepdims=True)
        doh = don * ngv
        do_scr[pl.ds(0, n_ctx), :] = jnp.zeros((n_ctx, HEAD), F32)
        do_scr[pl.ds(n_ctx, L), :] = r * (doh - oh * jnp.mean(doh * oh, axis=-1, keepdims=True))

        @pl.when(h == 0)
        def _():
            dng_ref[...] = jnp.zeros_like(dng_ref)

        dng_ref[0:1, :] += dng

        t16 = lax.broadcasted_iota(jnp.int32, (CHUNK, HEAD), 0)
        dirs = ((0, False, zf_ref, dzf_ref), (1, True, zb_ref, dzb_ref))
        for d, rev, z_ref, _ in dirs:
            k, c, rest = _decay_terms(z_ref[...], _lower_bound(lbl_ref, d), rev)
            c_scr[d] = c
            k_scr[d] = k
            qe_scr[d] = (q_ref[...] * jnp.exp(c)).astype(BF16)
            ke_scr[d] = (k * jnp.exp(rest)).astype(BF16)
        dq_scr[...] = jnp.zeros_like(dq_scr)
        dv_scr[...] = jnp.zeros_like(dv_scr)

        def fwd_step(i, states):
            new = []
            for (d, rev, _, _), St in zip(dirs, states):
                ci = _scan_chunk(i, ncc, nc, rev)
                rows = _rows(ci)
                st_scr[d, ci] = St.astype(BF16)
                etot = jnp.exp(_chunk_total(c_scr[d, rows, :], rev))
                new.append(St * etot + _dot_tn(v_ref[rows, :].astype(BF16), ke_scr[d, rows, :]))
            return tuple(new)

        zero = jnp.zeros((HEAD, HEAD), F32)
        lax.fori_loop(0, nc, fwd_step, (zero, zero), unroll=2)

        def bwd_step(ii, carry):
            i = nc - 1 - ii
            new = []
            for (d, rev, _, _), dSt in zip(dirs, carry):
                ci = _scan_chunk(i, ncc, nc, rev)
                rows = _rows(ci)
                q, v, do = q_ref[rows, :], v_ref[rows, :], do_scr[rows, :]
                c, k = c_scr[d, rows, :], k_scr[d, rows, :]
                tot = _chunk_total(c, rev)
                etot = jnp.exp(tot)
                St = st_scr[d, ci]
                dSb = dSt.astype(BF16)
                do_b = do.astype(BF16)
                dq = _dot(do_b, St) * jnp.exp(c)
                dk = _dot(v.astype(BF16), dSb) * jnp.exp(tot - c)
                dv = _dot_nt(ke_scr[d, rows, :], dSb)
                dtot = (jnp.sum(St.astype(F32) * dSt, axis=0, keepdims=True) * etot
                        + jnp.sum(k * dk, axis=0, keepdims=True))
                for s in range(CHUNK):
                    E = _pair_decay(c, s, rev)
                    XE = E * k[s:s + 1, :]
                    a = jnp.sum(q * XE, axis=1, keepdims=True)
                    da = jnp.sum(do * v[s:s + 1, :], axis=1, keepdims=True)
                    dq = dq + da * XE
                    dk_row = jnp.sum(da * q * E, axis=0, keepdims=True)
                    dv_row = jnp.sum(a * do, axis=0, keepdims=True)
                    dk = dk + jnp.where(t16 == s, dk_row, 0.0)
                    dv = dv + jnp.where(t16 == s, dv_row, 0.0)
                dg_scr[d, rows, :] = _chunk_cumsum(q * dq - k * dk, not rev) + dtot
                dk_scr[d, rows, :] = dk
                dq_scr[rows, :] += dq
                dv_scr[rows, :] += dv
                new.append(dSt * etot + _dot_tn(do_b, qe_scr[d, rows, :]))
            return tuple(new)

        lax.fori_loop(0, nc, bwd_step, (zero, zero), unroll=2)

        for d, _, z_ref, dz_ref in dirs:
            lb = _lower_bound(lbl_ref, d)
            sg, f = _gate_terms(z_ref[...], lb)
            df = dg_scr[d] / f - dk_scr[d]
            dz_ref[...] = (df * (1.0 - lb) * sg * (1.0 - sg)).astype(BF16)
            dl0 = jnp.sum(df * (1.0 - sg), axis=0, keepdims=True) * lb * (1.0 - lb)
            dlbl_ref[d, 0:1, :] = dl0
            dlbl_ref[d, 1:2, :] = -dl0
        dq_ref[...] = dq_scr[...].astype(BF16)
        dv_ref[...] = dv_scr[...].astype(BF16)

    cb = HA // HEAD
    col = lambda kk: pl.BlockSpec((T, HEAD), lambda h: (0, kk * cb + h))
    tcol = pl.BlockSpec((T, HEAD), lambda h: (0, h))
    lcol = pl.BlockSpec((L, HEAD), lambda h: (0, h))
    outs = _pcall(
        body, name="hgrn_bwd", grid=(nh,),
        in_specs=[col(0), col(1), col(2), col(3), col(4),
                  pl.BlockSpec((2, 2, HEAD), lambda h: (0, 0, h)), pl.BlockSpec((1, HEAD), lambda h: (0, 0)),
                  lcol, lcol],
        out_specs=[tcol, tcol, tcol, tcol, tcol, pl.BlockSpec((2, 2, HEAD), lambda h: (0, 0, h)),
                   pl.BlockSpec((8, HEAD), lambda h: (0, 0))],
        out_shape=[jax.ShapeDtypeStruct((T, HA), BF16)] * 5 + [jax.ShapeDtypeStruct((2, 2, HA), F32),
                                                               jax.ShapeDtypeStruct((8, HEAD), F32)],
        scratch_shapes=[pltpu.VMEM((T, HEAD), F32), pltpu.VMEM((2, nc, HEAD, HEAD), BF16),
                        pltpu.VMEM((2, T, HEAD), F32), pltpu.VMEM((2, T, HEAD), F32),
                        pltpu.VMEM((2, T, HEAD), BF16), pltpu.VMEM((2, T, HEAD), BF16),
                        pltpu.VMEM((2, T, HEAD), F32), pltpu.VMEM((2, T, HEAD), F32),
                        pltpu.VMEM((T, HEAD), F32), pltpu.VMEM((T, HEAD), F32)],
        compiler_params=_params(("arbitrary",)),
    )(p, p, p, p, p, lbl, ng, o, dya)
    return outs


def _swap_halves(t, lane):
    q = HEAD // 4
    return jnp.where((lane % (2 * q)) < q, pltpu.roll(t, HEAD - q, 1), pltpu.roll(t, q, 1))


def _qk_norm(t, g):
    r = lax.rsqrt(jnp.mean(t * t, axis=-1, keepdims=True) + EPS)
    return t * r, r


def _rope(t, cos, sin, lane):
    return t * cos + _swap_halves(t, lane) * sin


def _qk_norm_bwd(dy, th, r, g):
    dth = dy * g
    return r * (dth - th * jnp.mean(dth * th, axis=-1, keepdims=True)), jnp.sum(dy * th, axis=0, keepdims=True)


def _rope_bwd(dy, cos, sin, lane):
    return dy * cos + _swap_halves(dy * sin, lane)


def _na_geometry(L):
    n_rows = L // GRID_W
    kr = min(WIN_R, n_rows)
    return n_rows, kr


def _na_prep(q_ref, k_ref, v_ref, gq_ref, gk_ref, cos_ref, sin_ref, qs, ks, vs, n_ctx, L):
    lane = lax.broadcasted_iota(jnp.int32, (L, HEAD), 1)
    cos, sin = cos_ref[...], sin_ref[...]
    qh, _ = _qk_norm(q_ref[pl.ds(n_ctx, L), :], None)
    qs[...] = _rope(qh * gq_ref[...], cos, sin, lane).astype(BF16)
    kh, _ = _qk_norm(k_ref[pl.ds(n_ctx, L), :], None)
    ks[pl.ds(n_ctx, L), :] = _rope(kh * gk_ref[...], cos, sin, lane).astype(BF16)
    kc, _ = _qk_norm(k_ref[pl.ds(0, n_ctx), :], None)
    ks[pl.ds(0, n_ctx), :] = (kc * gk_ref[...]).astype(BF16)
    vs[...] = v_ref[...].astype(BF16)


def _na_scores(r, qs, ks, bias_ref, n_ctx, n_rows, kr):
    scale = HEAD ** -0.5
    r0 = jnp.clip(r - WIN_R // 2, 0, n_rows - kr)
    qrows = pl.ds(pl.multiple_of(r * GRID_W, GRID_W), GRID_W)
    krows = pl.ds(pl.multiple_of(n_ctx + r0 * GRID_W, GRID_W), kr * GRID_W)
    qv = qs[qrows, :]
    sb = _dot_nt(qv, ks[krows, :]) * scale
    b0 = r0 - r + (WIN_R - 1)
    sb = sb + jnp.concatenate([bias_ref[0, b0 + 2 * jj] for jj in range(kr // 2)], axis=1)
    sc = _dot_nt(qv, ks[pl.ds(0, n_ctx), :]) * scale
    m = jnp.maximum(jnp.max(sb, axis=1, keepdims=True), jnp.max(sc, axis=1, keepdims=True))
    eb, ec = jnp.exp(sb - m), jnp.exp(sc - m)
    inv = 1.0 / (jnp.sum(eb, axis=1, keepdims=True) + jnp.sum(ec, axis=1, keepdims=True))
    return eb * inv, ec * inv, qrows, krows, b0


def _na_fwd(p, bias, gq, gk, cos, sin, n_ctx, off, HB):
    T = p.shape[0]
    L = T - n_ctx
    nh = HB // HEAD
    n_rows, kr = _na_geometry(L)
    ob = off // HEAD

    def body(q_ref, k_ref, v_ref, bias_ref, gq_ref, gk_ref, cos_ref, sin_ref, y_ref, qs, ks, vs):
        _na_prep(q_ref, k_ref, v_ref, gq_ref, gk_ref, cos_ref, sin_ref, qs, ks, vs, n_ctx, L)

        def step(r, carry):
            pb, pc, qrows, krows, _ = _na_scores(r, qs, ks, bias_ref, n_ctx, n_rows, kr)
            y = _dot(pb.astype(BF16), vs[krows, :]) + _dot(pc.astype(BF16), vs[pl.ds(0, n_ctx), :])
            y_ref[qrows, :] = y.astype(BF16)
            return carry

        lax.fori_loop(0, n_rows, step, 0)

    col = lambda kk: pl.BlockSpec((T, HEAD), lambda h: (0, ob + kk * nh + h))
    vec = pl.BlockSpec((1, HEAD), lambda h: (0, 0))
    tab = pl.BlockSpec((L, HEAD), lambda h: (0, 0))
    return _pcall(
        body, name="na_fwd", grid=(nh,),
        in_specs=[col(0), col(1), col(2), pl.BlockSpec((1,) + bias.shape[1:], lambda h: (h, 0, 0, 0)),
                  vec, vec, tab, tab],
        out_specs=pl.BlockSpec((L, HEAD), lambda h: (0, h)),
        out_shape=jax.ShapeDtypeStruct((L, HB), BF16),
        scratch_shapes=[pltpu.VMEM((L, HEAD), BF16), pltpu.VMEM((T, HEAD), BF16), pltpu.VMEM((T, HEAD), BF16)],
        compiler_params=_params(("parallel",)),
    )(p, p, p, bias, gq, gk, cos, sin)


def _na_bwd(p, bias, gq, gk, cos, sin, dyb, n_ctx, off, HB):
    T = p.shape[0]
    L = T - n_ctx
    nh = HB // HEAD
    n_rows, kr = _na_geometry(L)
    ob = off // HEAD
    scale = HEAD ** -0.5

    def body(q_ref, k_ref, v_ref, bias_ref, gq_ref, gk_ref, cos_ref, sin_ref, dy_ref,
             dq_ref, dk_ref, dv_ref, dbias_ref, dg_ref, qs, ks, vs, dqa, dka, dva):
        h = pl.program_id(0)
        _na_prep(q_ref, k_ref, v_ref, gq_ref, gk_ref, cos_ref, sin_ref, qs, ks, vs, n_ctx, L)
        dka[...] = jnp.zeros_like(dka)
        dva[...] = jnp.zeros_like(dva)
        dbias_ref[...] = jnp.zeros_like(dbias_ref)

        def step(r, carry):
            pb, pc, qrows, krows, b0 = _na_scores(r, qs, ks, bias_ref, n_ctx, n_rows, kr)
            crows = pl.ds(0, n_ctx)
            do = dy_ref[qrows, :]
            qv = qs[qrows, :]
            dpb = _dot_nt(do, vs[krows, :])
            dpc = _dot_nt(do, vs[crows, :])
            delta = jnp.sum(pb * dpb, axis=1, keepdims=True) + jnp.sum(pc * dpc, axis=1, keepdims=True)
            dsb = pb * (dpb - delta)
            dsc = pc * (dpc - delta)
            for jj in range(kr // 2):
                dbias_ref[0, b0 + 2 * jj] += dsb[:, jj * 2 * GRID_W:(jj + 1) * 2 * GRID_W]
            dsb_b, dsc_b = dsb.astype(BF16), dsc.astype(BF16)
            dqa[qrows, :] = (_dot(dsb_b, ks[krows, :]) + _dot(dsc_b, ks[crows, :])) * scale
            dka[krows, :] += _dot_tn(dsb_b, qv) * scale
            dka[crows, :] += _dot_tn(dsc_b, qv) * scale
            dva[krows, :] += _dot_tn(pb.astype(BF16), do)
            dva[crows, :] += _dot_tn(pc.astype(BF16), do)
            return carry

        lax.fori_loop(0, n_rows, step, 0)

        lane = lax.broadcasted_iota(jnp.int32, (L, HEAD), 1)
        cos, sin = cos_ref[...], sin_ref[...]
        lat, ctx = pl.ds(n_ctx, L), pl.ds(0, n_ctx)
        gqv, gkv = gq_ref[...], gk_ref[...]
        qh, rq = _qk_norm(q_ref[lat, :], None)
        dq, dgq = _qk_norm_bwd(_rope_bwd(dqa[...], cos, sin, lane), qh, rq, gqv)
        dq_ref[ctx, :] = jnp.zeros((n_ctx, HEAD), BF16)
        dq_ref[lat, :] = dq.astype(BF16)
        kh, rk = _qk_norm(k_ref[lat, :], None)
        dk, dgk = _qk_norm_bwd(_rope_bwd(dka[lat, :], cos, sin, lane), kh, rk, gkv)
        dk_ref[lat, :] = dk.astype(BF16)
        kch, rkc = _qk_norm(k_ref[ctx, :], None)
        dkc, dgkc = _qk_norm_bwd(dka[ctx, :], kch, rkc, gkv)
        dk_ref[ctx, :] = dkc.astype(BF16)
        dv_ref[...] = dva[...].astype(BF16)

        @pl.when(h == 0)
        def _():
            dg_ref[...] = jnp.zeros_like(dg_ref)

        dg_ref[0:1, :] += dgq
        dg_ref[1:2, :] += dgk + dgkc

    col = lambda kk: pl.BlockSpec((T, HEAD), lambda h: (0, ob + kk * nh + h))
    vec = pl.BlockSpec((1, HEAD), lambda h: (0, 0))
    tab = pl.BlockSpec((L, HEAD), lambda h: (0, 0))
    tcol = pl.BlockSpec((T, HEAD), lambda h: (0, h))
    bspec = pl.BlockSpec((1,) + bias.shape[1:], lambda h: (h, 0, 0, 0))
    return _pcall(
        body, name="na_bwd", grid=(nh,),
        in_specs=[col(0), col(1), col(2), bspec, vec, vec, tab, tab, pl.BlockSpec((L, HEAD), lambda h: (0, h))],
        out_specs=[tcol, tcol, tcol, bspec, pl.BlockSpec((8, HEAD), lambda h: (0, 0))],
        out_shape=[jax.ShapeDtypeStruct((T, HB), BF16)] * 3 + [jax.ShapeDtypeStruct(bias.shape, F32),
                                                               jax.ShapeDtypeStruct((8, HEAD), F32)],
        scratch_shapes=[pltpu.VMEM((L, HEAD), BF16), pltpu.VMEM((T, HEAD), BF16), pltpu.VMEM((T, HEAD), BF16),
                        pltpu.VMEM((L, HEAD), F32), pltpu.VMEM((T, HEAD), F32), pltpu.VMEM((T, HEAD), F32)],
        compiler_params=_params(("arbitrary",)),
    )(p, p, p, bias, gq, gk, cos, sin, dyb)


def _bias_tables():
    w = np.arange(GRID_W)
    col_start = np.clip(w - WIN_C // 2, 0, GRID_W - WIN_C)
    col_in = (w[None, :] >= col_start[:, None]) & (w[None, :] < col_start[:, None] + WIN_C)
    dc = np.clip(w[None, :] - w[:, None], -(WIN_C - 1), WIN_C - 1) + WIN_C - 1
    n_pair = 2 * WIN_R
    ridx = np.zeros((n_pair, GRID_W, 2 * GRID_W), np.int32)
    cidx = np.zeros((n_pair, GRID_W, 2 * GRID_W), np.int32)
    valid = np.zeros((n_pair, GRID_W, 2 * GRID_W), bool)
    for i in range(n_pair):
        for half in range(2):
            row = i + half
            sl = slice(half * GRID_W, (half + 1) * GRID_W)
            ridx[i, :, sl] = min(row, 2 * WIN_R - 2)
            cidx[i, :, sl] = dc
            valid[i, :, sl] = col_in & (row <= 2 * WIN_R - 2)
    return ridx, cidx, valid


def _bias_onehot():
    _, cidx, valid = _bias_tables()
    K = GRID_W * 2 * GRID_W
    oh = np.zeros((K, 128), np.float32)
    neg = np.full((1, K), NEG, np.float32)
    for cq in range(GRID_W):
        for ll in range(2 * GRID_W):
            if valid[0, cq, ll]:
                oh[cq * 2 * GRID_W + ll, (ll // GRID_W) * 64 + cidx[0, cq, ll]] = 1.0
                neg[0, cq * 2 * GRID_W + ll] = 0.0
    return oh, neg


def _expand_bias(table):
    H = table.shape[0]
    n_pair, n_dc = 2 * WIN_R, 2 * WIN_C - 1
    tp = jnp.pad(table, ((0, 0), (0, n_pair + 1 - table.shape[1]), (0, 64 - n_dc)))
    t2 = jnp.concatenate([tp[:, :n_pair], tp[:, 1:n_pair + 1]], axis=-1).reshape(H * n_pair, 128)
    oh, neg = _bias_onehot()

    def body(t_ref, oh_ref, neg_ref, o_ref):
        o_ref[...] = lax.dot_general(t_ref[...], oh_ref[...], (((1,), (1,)), ((), ())), precision=HI,
                                     preferred_element_type=F32) + neg_ref[...]

    out = _pcall(body, name="bias_expand", out_shape=jax.ShapeDtypeStruct((H * n_pair, oh.shape[0]), F32),
                         compiler_params=_params())(t2, jnp.asarray(oh), jnp.asarray(neg))
    return out.reshape(H, n_pair, GRID_W, 2 * GRID_W)


def _bias_grad(dbias):
    H = dbias.shape[0]
    n_pair, n_dc = 2 * WIN_R, 2 * WIN_C - 1
    K = GRID_W * 2 * GRID_W
    oh, _ = _bias_onehot()
    flat = dbias.reshape(H * n_pair, K)

    def body(d_ref, oh_ref, o_ref):
        o_ref[...] = jnp.dot(d_ref[...], oh_ref[...], precision=HI, preferred_element_type=F32)

    g = _pcall(body, name="bias_grad", out_shape=jax.ShapeDtypeStruct((H * n_pair, 128), F32),
                       compiler_params=_params())(flat, jnp.asarray(oh))
    g = g.reshape(H, n_pair, 128)
    left, right = g[:, :, :n_dc], g[:, :, 64:64 + n_dc]
    out = left[:, :n_pair - 1]
    return out.at[:, 1:].add(right[:, :n_pair - 2])


def _rope_tables(L):
    pos = np.arange(L)
    row = (pos // GRID_W).astype(np.float32)
    colp = (pos % GRID_W).astype(np.float32)
    half = HEAD // 2
    nf = half // 2
    inv = (ROPE_THETA ** (-np.arange(nf, dtype=np.float32) / nf)).astype(np.float32)

    def tabs(pv):
        ang = pv[:, None] * inv[None, :]
        c, s = np.cos(ang), np.sin(ang)
        return np.concatenate([c, c], axis=1), np.concatenate([-s, s], axis=1)

    cr, sr = tabs(row)
    cc, sc = tabs(colp)
    return (jnp.asarray(np.concatenate([cr, cc], axis=1), F32), jnp.asarray(np.concatenate([sr, sc], axis=1), F32))


def _adamw(w, g, m, v, name):
    R, C = w.shape
    tr = _row_tile(R, C)
    c1 = 1.0 - ADAM_B1 ** ADAM_STEP
    c2 = 1.0 - ADAM_B2 ** ADAM_STEP

    def body(w_ref, g_ref, m_ref, v_ref, d_ref, mo_ref, vo_ref):
        gv = g_ref[...]
        mn = ADAM_B1 * m_ref[...] + (1.0 - ADAM_B1) * gv
        vn = ADAM_B2 * v_ref[...] + (1.0 - ADAM_B2) * (gv * gv)
        mo_ref[...] = mn
        vo_ref[...] = vn
        d_ref[...] = -ADAM_LR * ((mn / c1) / (jnp.sqrt(vn / c2) + ADAM_EPS) + ADAM_WD * w_ref[...])

    blk = pl.BlockSpec((tr, C), lambda i: (i, 0))
    return _pcall(
        body, name=name, grid=(R // tr,),
        in_specs=[blk] * 4, out_specs=[blk] * 3,
        out_shape=[jax.ShapeDtypeStruct((R, C), F32)] * 3,
        compiler_params=_params(("parallel",)),
    )(w, g, m, v)


PACK_W = 1024


def _pack(parts):
    flat, offs, pos = [], [], 0
    for a in parts:
        n = a.size
        padn = -n % PACK_W
        flat.append(jnp.pad(a.reshape(-1).astype(F32), (0, padn)))
        offs.append((pos, n, a.shape))
        pos += n + padn
    tail = -pos % (8 * PACK_W)
    if tail:
        flat.append(jnp.zeros((tail,), F32))
    return jnp.concatenate(flat).reshape(-1, PACK_W), offs


def _unpack(buf, offs, i):
    pos, n, shape = offs[i]
    return buf.reshape(buf.shape[:-2] + (-1,))[..., pos:pos + n].reshape(buf.shape[:-2] + shape)


def kernel(x, c, ctx, c_ctx, ada_w, ada_b, norm1_g, norm2_g, w_in, hgrn_lb_logits, hgrn_norm_g, na_q_norm_g, na_k_norm_g, na_rel_bias, w_branch_a, w_branch_b, w_out, ffn_w1, ffn_w3, ffn_conv_w, ffn_conv_b, ffn_w2, loss_target, m_c_ctx, m_ada_w, m_ada_b, m_norm1_g, m_norm2_g, m_w_in, m_hgrn_lb_logits, m_hgrn_norm_g, m_na_q_norm_g, m_na_k_norm_g, m_na_rel_bias, m_w_branch_a, m_w_branch_b, m_w_out, m_ffn_w1, m_ffn_w3, m_ffn_conv_w, m_ffn_conv_b, m_ffn_w2, v_c_ctx, v_ada_w, v_ada_b, v_norm1_g, v_norm2_g, v_w_in, v_hgrn_lb_logits, v_hgrn_norm_g, v_na_q_norm_g, v_na_k_norm_g, v_na_rel_bias, v_w_branch_a, v_w_branch_b, v_w_out, v_ffn_w1, v_ffn_w3, v_ffn_conv_w, v_ffn_conv_b, v_ffn_w2):
    weights = dict(c_ctx=c_ctx, ada_w=ada_w, ada_b=ada_b, norm1_g=norm1_g, norm2_g=norm2_g, w_in=w_in,
                   hgrn_lb_logits=hgrn_lb_logits, hgrn_norm_g=hgrn_norm_g, na_q_norm_g=na_q_norm_g,
                   na_k_norm_g=na_k_norm_g, na_rel_bias=na_rel_bias, w_branch_a=w_branch_a, w_branch_b=w_branch_b,
                   w_out=w_out, ffn_w1=ffn_w1, ffn_w3=ffn_w3, ffn_conv_w=ffn_conv_w, ffn_conv_b=ffn_conv_b,
                   ffn_w2=ffn_w2)
    moms = dict(c_ctx=(m_c_ctx, v_c_ctx), ada_w=(m_ada_w, v_ada_w), ada_b=(m_ada_b, v_ada_b),
                norm1_g=(m_norm1_g, v_norm1_g), norm2_g=(m_norm2_g, v_norm2_g), w_in=(m_w_in, v_w_in),
                hgrn_lb_logits=(m_hgrn_lb_logits, v_hgrn_lb_logits), hgrn_norm_g=(m_hgrn_norm_g, v_hgrn_norm_g),
                na_q_norm_g=(m_na_q_norm_g, v_na_q_norm_g), na_k_norm_g=(m_na_k_norm_g, v_na_k_norm_g),
                na_rel_bias=(m_na_rel_bias, v_na_rel_bias), w_branch_a=(m_w_branch_a, v_w_branch_a),
                w_branch_b=(m_w_branch_b, v_w_branch_b), w_out=(m_w_out, v_w_out), ffn_w1=(m_ffn_w1, v_ffn_w1),
                ffn_w3=(m_ffn_w3, v_ffn_w3), ffn_conv_w=(m_ffn_conv_w, v_ffn_conv_w),
                ffn_conv_b=(m_ffn_conv_b, v_ffn_conv_b), ffn_w2=(m_ffn_w2, v_ffn_w2))
    order = list(weights)

    L, D = x.shape[1], x.shape[2]
    N = ctx.shape[1]
    T = N + L
    HA = w_branch_a.shape[1]
    HB = w_branch_b.shape[1]
    F = ffn_conv_b.shape[1]
    IN = 5 * HA + 3 * HB + 2 * D
    n_ada = ada_w.shape[2]
    ix, iy, ic = _pos()
    chip = 2 * ix + iy
    dev = 2 * chip + ic

    shards = [w_in[0], w_branch_a[0], w_branch_b[0], w_out[0], ffn_w1[0], ffn_w3[0], ffn_w2[0]]
    names = ["w_in", "w_a", "w_b", "w_out", "w1", "w3", "w2"]
    slots = [_cast_bf16_slot(s, "cast_" + nm) for s, nm in zip(shards, names)]
    gat_in = _gather_start("in", slots[0:1])
    gat_mix = _gather_start("mix", slots[1:4])
    gat_ffn = _gather_start("ffn", slots[4:7])

    pk0, offs0 = _pack([c[0], hgrn_lb_logits, ffn_conv_w[0]])
    g0 = _allgather8(pk0, "gather_small0")
    c_all = _unpack(g0, offs0, 0)
    lbl_parts = _unpack(g0, offs0, 1)
    lbl = jnp.concatenate([lbl_parts[2 * j] for j in range(N_CHIP)], axis=-1)
    cw_parts = _unpack(g0, offs0, 2)
    cw = jnp.concatenate([cw_parts[2 * j] for j in range(N_CHIP)], axis=-1)
    cw8 = jnp.pad(cw, ((0, 5), (0, 0)))

    cs = jnp.concatenate([c_all, c_ctx[None, :], jnp.zeros((7, D), F32)], axis=0)
    ada_b_mine = lax.dynamic_slice(ada_b, (0, chip * n_ada), (1, n_ada))
    mod_mine = _ada_fwd(cs, ada_w[0], ada_b_mine)
    gm = _allgather8(mod_mine, "gather_mod")
    mod = jnp.concatenate([gm[2 * j] for j in range(N_CHIP)], axis=-1)
    mod_l = lax.dynamic_slice(mod, (dev, 0), (1, N_MOD * D)).reshape(N_MOD, D)
    mod_c = mod[8].reshape(N_MOD, D)
    sh1, sc1, g1, sh2, sc2, g2 = [mod_l[i:i + 1] for i in range(N_MOD)]
    shift1 = jnp.concatenate([mod_c[0:1], sh1], axis=0)
    scale1 = jnp.concatenate([mod_c[1:2], sc1], axis=0)

    xall = jnp.concatenate([ctx[0], x[0]], axis=0)
    h_all = _rms1_fwd(xall, norm1_g, shift1, scale1, N)
    gat_in = _gather_mid(gat_in, h_all)
    (Win,) = _gather_finish(gat_in, h_all)
    p = _mm_nn(h_all, Win, F32, "mm_p")
    gat_mix = _gather_mid(gat_mix, p)
    y_a, o_a = _hgrn_fwd(p, lbl, hgrn_norm_g, N, HA)
    Wa, Wb, Wo = _gather_finish(gat_mix, y_a)
    Wo = Wo.reshape(1, D, D)
    gat_ffn = _gather_mid(gat_ffn, y_a)
    bias = _expand_bias(na_rel_bias[0])
    cos, sin = _rope_tables(L)
    off_na = 5 * HA
    y_b = _na_fwd(p, bias, na_q_norm_g, na_k_norm_g, cos, sin, N, off_na, HB)
    za = _mm_nn(y_a, Wa, F32, "mm_za")
    zb = _mm_nn(y_b, Wb, F32, "mm_zb")
    off_ga, off_gb = 5 * HA + 3 * HB, 5 * HA + 3 * HB + D
    z = _merge_fwd(za, zb, p, N, off_ga, off_gb)
    mo = _mm_nn(z, Wo, F32, "mm_mo")
    vec2 = jnp.concatenate([g1, norm2_g, sh2, sc2, jnp.zeros((4, D), F32)], axis=0)
    x_mid, h2 = _resid_rms2_fwd(x[0], mo, vec2)
    W1, W3, W2 = _gather_finish(gat_ffn, h2)
    W2 = W2.reshape(1, F, D)
    u1 = _mm_nn(h2, W1, F32, "mm_u1")
    u3 = _mm_nn(h2, W3, F32, "mm_u3")
    a = _convgate_fwd(u1, u3, cw8, ffn_conv_b)
    f = _mm_nn(a, W2, F32, "mm_f")
    dy, df, s_loss = _loss_head(x_mid, f, g2, loss_target[0])
    loss = lax.psum(s_loss[1, 0], ("x", "y", "c"))
    d_g2 = s_loss[0:1]

    gW2 = _mm_tn(a, df, 1, "mm_gw2").reshape(N_CHIP, F // N_CHIP, D)
    da = _mm_nt(df, W2, F32, "mm_da")
    du1, du3, s_conv = _convgate_bwd(u1, u3, da, cw8, ffn_conv_b)
    gW1 = _mm_tn(h2, du1, N_CHIP, "mm_gw1")
    gW3 = _mm_tn(h2, du3, N_CHIP, "mm_gw3")
    rs_ffn = _rs_start("ffn", [gW2, gW1, gW3])
    dh2a = _mm_nt(du1, W1, F32, "mm_dh2a")
    dh2b = _mm_nt(du3, W3, F32, "mm_dh2b")
    rs_ffn = _rs_scatter(rs_ffn, dh2b)
    dxm, dmo, s_rms2 = _resid_rms2_bwd(x_mid, dh2a, dh2b, dy, mo, vec2)
    gWo = _mm_tn(z, dmo, 1, "mm_gwo").reshape(N_CHIP, D // N_CHIP, D)
    dz = _mm_nt(dmo, Wo, F32, "mm_dz")
    dza, dzb, dga, dgb = _merge_bwd(dz, za, zb, p, N, off_ga, off_gb)
    gWa = _mm_tn(y_a, dza, N_CHIP, "mm_gwa")
    gWb = _mm_tn(y_b, dzb, N_CHIP, "mm_gwb")
    rs_mix = _rs_start("mix", [gWo, gWa, gWb])
    dya = _mm_nt(dza, Wa, F32, "mm_dya")
    dyb = _mm_nt(dzb, Wb, BF16, "mm_dyb")
    rs_mix = _rs_scatter(rs_mix, dyb)
    dq_a, dzf, dzbk, di_a, dog, dlbl, s_ng = _hgrn_bwd(p, lbl, hgrn_norm_g, o_a, dya, N, HA)
    rs_ffn = _rs_join(rs_ffn, dq_a)
    dq_n, dk_n, dv_n, dbias, s_qk = _na_bwd(p, bias, na_q_norm_g, na_k_norm_g, cos, sin, dyb, N, off_na, HB)
    rs_mix = _rs_join(rs_mix, dq_n)
    dp = jnp.concatenate([dq_a, dzf, dzbk, di_a, dog, dq_n, dk_n, dv_n, dga, dgb], axis=1)
    gWin = _mm_tn(h_all, dp, N_CHIP, "mm_gwin")
    rs_in = _rs_start("in", [gWin])
    dh = _mm_nt(dp, Win, F32, "mm_dh")
    rs_in = _rs_scatter(rs_in, dh)
    grad_x, s_rms1 = _rms1_bwd(xall, dh, dxm, norm1_g, scale1, N)
    d_table = _bias_grad(dbias)

    zD = jnp.zeros((1, D), F32)
    dmod_l = jnp.concatenate([s_rms1[2:3], s_rms1[3:4], s_rms2[3:4], s_rms2[0:1], s_rms2[1:2], d_g2], axis=0)
    dmod_c = jnp.concatenate([s_rms1[0:1], s_rms1[1:2], zD, zD, zD, zD], axis=0)
    pk1, offs1 = _pack([dmod_l, dmod_c, s_rms1[4], s_rms2[2], dlbl, s_ng[0], s_qk[0], s_qk[1], d_table,
                        s_conv[0:3], s_conv[3]])
    g1all = _allgather8(pk1, "gather_small1")
    tot1 = _sum8(g1all, "sum_small1")
    dmod_rows = _unpack(g1all, offs1, 0).reshape(N_DEV, N_MOD * D)
    dmod_c_tot = _unpack(tot1, offs1, 1).reshape(1, N_MOD * D)
    dmod16 = jnp.concatenate([dmod_rows, dmod_c_tot, jnp.zeros((7, N_MOD * D), F32)], axis=0)
    dmod16_mine = lax.dynamic_slice(dmod16, (0, chip * n_ada), (16, n_ada))
    g_ada_w, dact = _ada_bwd(cs, ada_w[0], dmod16_mine)
    pk2, offs2 = _pack([dact[8]])
    g2all = _allgather8(pk2, "gather_small2")
    dact_rows = _unpack(g2all, offs2, 0)
    dact_sel = jnp.concatenate([dact_rows[2 * j][None] for j in range(N_CHIP)] + [jnp.zeros((4, D), F32)], axis=0)

    grads = {}
    grads["ada_w"] = g_ada_w[None]
    grads["ada_b"] = (_unpack(tot1, offs1, 0) + _unpack(tot1, offs1, 1)).reshape(1, N_MOD * D)
    grads["norm1_g"] = _unpack(tot1, offs1, 2)[None]
    grads["norm2_g"] = _unpack(tot1, offs1, 3)[None]
    g_lbl = _unpack(tot1, offs1, 4)
    n_lb = HA // N_CHIP
    grads["hgrn_lb_logits"] = lax.dynamic_slice(g_lbl, (0, 0, chip * n_lb), (2, 2, n_lb))
    grads["hgrn_norm_g"] = _unpack(tot1, offs1, 5)[None]
    grads["na_q_norm_g"] = _unpack(tot1, offs1, 6)[None]
    grads["na_k_norm_g"] = _unpack(tot1, offs1, 7)[None]
    grads["na_rel_bias"] = _unpack(tot1, offs1, 8)[None]
    g_cw = _unpack(tot1, offs1, 9)
    n_f = F // N_CHIP
    grads["ffn_conv_w"] = lax.dynamic_slice(g_cw, (0, chip * n_f), (3, n_f))[None]
    grads["ffn_conv_b"] = _unpack(tot1, offs1, 10)[None]

    grads["c_ctx"] = _dsilu_rows(dact_sel, c_ctx[None, :], "grad_c_ctx")[0]

    big_names = ["ada_w", "w_in", "w_branch_a", "w_branch_b", "w_out", "ffn_w1", "ffn_w3", "ffn_w2"]
    small_names = [n for n in order if n not in big_names]
    delta, new_m, new_v = {}, {}, {}

    def update(nm):
        d_, m_, v_ = _adamw(weights[nm][0], grads[nm][0], moms[nm][0][0], moms[nm][1][0], "adamw_" + nm)
        delta[nm], new_m[nm], new_v[nm] = d_[None], m_[None], v_[None]
        return d_

    last = update("ada_w")
    for nm, g in zip(["ffn_w2", "ffn_w1", "ffn_w3"], _rs_finish(rs_ffn, last)):
        grads[nm] = g[None]
        last = update(nm)
    rs_in = _rs_join(rs_in, last)
    for nm, g in zip(["w_out", "w_branch_a", "w_branch_b"], _rs_finish(rs_mix, last)):
        grads[nm] = g[None]
        last = update(nm)
    grads["w_in"] = _rs_finish(rs_in, last)[0][None]
    update("w_in")
    pw, offw = _pack([weights[n] for n in small_names])
    pg, _ = _pack([grads[n] for n in small_names])
    pm, _ = _pack([moms[n][0] for n in small_names])
    pv, _ = _pack([moms[n][1] for n in small_names])
    d_, m_, v_ = _adamw(pw, pg, pm, pv, "adamw_small")
    for i, nm in enumerate(small_names):
        delta[nm], new_m[nm], new_v[nm] = _unpack(d_, offw, i), _unpack(m_, offw, i), _unpack(v_, offw, i)

    return (loss, grad_x[None], *[grads[n] for n in order], *[delta[n] for n in order],
            *[new_m[n] for n in order], *[new_v[n] for n in order])


def _dsilu_rows(v, cv, name):
    D = v.shape[1]

    def body(v_ref, c_ref, o_ref):
        t = c_ref[...]
        s = _sigmoid(t)
        o_ref[...] = (((v_ref[0:1, :] + v_ref[1:2, :]) + v_ref[2:3, :]) + v_ref[3:4, :]) * (s * (1.0 + t * (1.0 - s)))

    return _pcall(body, name=name, out_shape=jax.ShapeDtypeStruct((1, D), F32),
                          compiler_params=_params())(v, cv)
```

```python
import functools

import numpy as np
import jax
import jax.numpy as jnp
from jax import lax
from jax.experimental import pallas as pl
from jax.experimental.pallas import tpu as pltpu

F32 = jnp.float32
BF16 = jnp.bfloat16
MESH = pl.DeviceIdType.MESH

HEAD = 128
GRID_W = 64
WIN_R = 8
WIN_C = 16
ROPE_THETA = 10000.0
EPS = 1e-6
N_MOD = 6
CHUNK = 16
ADAM_LR = 0.001
ADAM_B1 = 0.9
ADAM_B2 = 0.999
ADAM_EPS = 1e-08
ADAM_WD = 0.01
ADAM_STEP = 10
NEG = -1e30
VMEM_LIMIT = 56 * 1024 * 1024
N_DEV = 8
N_CHIP = 4
HI = lax.Precision.HIGHEST


def _pick(n, cands):
    for c in cands:
        if n % c == 0:
            return c
    return n


def _row_tile(rows, cols, target_bytes=1 << 20):
    want = max(16, target_bytes // (4 * cols))
    for t in (512, 256, 128, 64, 32, 16, 8):
        if t <= want and rows % t == 0:
            return t
    return rows


def _params(sem=None):
    return pltpu.CompilerParams(dimension_semantics=sem, vmem_limit_bytes=VMEM_LIMIT)


def _dot(a, b):
    return jnp.dot(a, b, preferred_element_type=F32)


def _dot_nt(a, b):
    return lax.dot_general(a, b, (((1,), (1,)), ((), ())), preferred_element_type=F32)


def _dot_tn(a, b):
    return lax.dot_general(a, b, (((0,), (0,)), ((), ())), preferred_element_type=F32)


def _sigmoid(x):
    return 1.0 / (1.0 + jnp.exp(-x))


def _col_tile(n):
    return n if n <= 1536 else _pick(n, (1024, 768, 512, 384, 256, 128))


def _mm_nn(x, w3, out_dtype, name):
    M, K = x.shape
    S, _, n = w3.shape
    tm = _pick(M, (768, 512, 256, 128, 64))
    tn = _col_tile(n)
    nb = n // tn

    def body(x_ref, w_ref, o_ref):
        o_ref[...] = _dot(x_ref[...].astype(BF16), w_ref[0]).astype(o_ref.dtype)

    return _pcall(
        body, name=name, grid=(M // tm, S * nb),
        in_specs=[pl.BlockSpec((tm, K), lambda i, j: (i, 0)),
                  pl.BlockSpec((1, K, tn), lambda i, j: (j // nb, 0, j % nb))],
        out_specs=pl.BlockSpec((tm, tn), lambda i, j: (i, j)),
        out_shape=jax.ShapeDtypeStruct((M, S * n), out_dtype),
        compiler_params=_params(("parallel", "parallel")),
    )(x, w3)


def _mm_nt(dy, w3, out_dtype, name):
    M = dy.shape[0]
    S, K, n = w3.shape
    tm = _pick(M, (768, 512, 256, 128, 64))
    tk = _pick(K, (512, 256, 128))
    tc = _col_tile(n)
    nb = n // tc
    nsteps = S * nb

    def body(dy_ref, w_ref, o_ref, acc_ref):
        s = pl.program_id(2)

        @pl.when(s == 0)
        def _():
            acc_ref[...] = jnp.zeros_like(acc_ref)

        acc_ref[...] += _dot_nt(dy_ref[...].astype(BF16), w_ref[0])

        @pl.when(s == nsteps - 1)
        def _():
            o_ref[...] = acc_ref[...].astype(o_ref.dtype)

    return _pcall(
        body, name=name, grid=(M // tm, K // tk, nsteps),
        in_specs=[pl.BlockSpec((tm, tc), lambda i, k, s: (i, s)),
                  pl.BlockSpec((1, tk, tc), lambda i, k, s: (s // nb, k, s % nb))],
        out_specs=pl.BlockSpec((tm, tk), lambda i, k, s: (i, k)),
        out_shape=jax.ShapeDtypeStruct((M, K), out_dtype),
        scratch_shapes=[pltpu.VMEM((tm, tk), F32)],
        compiler_params=_params(("parallel", "parallel", "arbitrary")),
    )(dy, w3)


def _mm_tn(x, dy, S, name):
    M, K = x.shape
    n = dy.shape[1] // S
    tk = _pick(K, (512, 256, 128))
    tn = _col_tile(n)
    nb = n // tn

    def body(x_ref, dy_ref, o_ref):
        o_ref[0] = _dot_tn(x_ref[...].astype(BF16), dy_ref[...].astype(BF16))

    return _pcall(
        body, name=name, grid=(S * nb, K // tk),
        in_specs=[pl.BlockSpec((M, tk), lambda j, k: (0, k)),
                  pl.BlockSpec((M, tn), lambda j, k: (0, j))],
        out_specs=pl.BlockSpec((1, tk, tn), lambda j, k: (j // nb, k, j % nb)),
        out_shape=jax.ShapeDtypeStruct((S, K, n), F32),
        compiler_params=_params(("parallel", "parallel")),
    )(x, dy)


def _chip_index():
    return (2 * lax.axis_index("x") + lax.axis_index("y")).astype(jnp.int32).reshape(1)


def _cast_bf16_slot(w, name):
    R, C = w.shape
    tr = _row_tile(R, C, 2 << 20)

    def body(j_ref, w_ref, o_ref):
        o_ref[0] = w_ref[...].astype(BF16)

    return _pcall(
        body, name=name,
        grid_spec=pltpu.PrefetchScalarGridSpec(
            num_scalar_prefetch=1, grid=(R // tr,),
            in_specs=[pl.BlockSpec((tr, C), lambda i, j_ref: (i, 0))],
            out_specs=pl.BlockSpec((1, tr, C), lambda i, j_ref: (j_ref[0], i, 0))),
        out_shape=jax.ShapeDtypeStruct((N_CHIP, R, C), BF16),
        compiler_params=_params(("parallel",)),
    )(_chip_index(), w)


def _pos():
    return lax.axis_index("x"), lax.axis_index("y"), lax.axis_index("c")


def _other_chips(x, y):
    return [(x, 1 - y), (1 - x, y), (1 - x, 1 - y)]


def _allgather8(v, name):
    R, C = v.shape

    def body(x_ref, out_ref, send_sems, recv_sems, local_sem):
        x, y, c = _pos()
        me, sibling = (x, y, c), (x, y, 1 - c)
        chips = _other_chips(x, y)

        def slot(px, py, pc):
            return out_ref.at[4 * px + 2 * py + pc]

        def copy(k, block, to, src=None):
            return pltpu.make_async_remote_copy(
                src_ref=slot(*block) if src is None else src, dst_ref=slot(*block),
                send_sem=send_sems.at[k], recv_sem=recv_sems.at[k], device_id=to, device_id_type=MESH)

        mine = pltpu.make_async_copy(x_ref, slot(*me), local_sem)
        mine.start()
        first = [copy(0, me, sibling, src=x_ref)]
        first += [copy(1 + j, me, (*chip, c), src=x_ref) for j, chip in enumerate(chips)]
        for cp in first:
            cp.start()
        passed = [copy(4 + j, (*chip, c), sibling) for j, chip in enumerate(chips)]
        for j, chip in enumerate(chips):
            copy(1 + j, (*chip, c), me).wait_recv()
            passed[j].start()
        copy(0, sibling, me).wait_recv()
        for j, chip in enumerate(chips):
            copy(4 + j, (*chip, 1 - c), me).wait_recv()
        for cp in first + passed:
            cp.wait_send()
        mine.wait()

    return _pcall(
        body, name=name,
        out_shape=jax.ShapeDtypeStruct((N_DEV, R, C), v.dtype),
        in_specs=[pl.BlockSpec(memory_space=pltpu.VMEM)],
        out_specs=pl.BlockSpec(memory_space=pltpu.VMEM),
        scratch_shapes=[pltpu.SemaphoreType.DMA((7,)), pltpu.SemaphoreType.DMA((7,)), pltpu.SemaphoreType.DMA],
        compiler_params=pltpu.CompilerParams(vmem_limit_bytes=VMEM_LIMIT),
    )(v)


_HBM = pl.BlockSpec(memory_space=pltpu.HBM)
_SEM = pl.BlockSpec(memory_space=pltpu.SEMAPHORE)
_ANY = pl.BlockSpec(memory_space=pl.ANY)
_EFFECT = pltpu.SideEffectType.DATAFLOW_SIDE_EFFECTING
_PENDING = []


def _pcall(body, **kw):
    def run(*operands):
        if not _PENDING or "in_specs" not in kw:
            return pl.pallas_call(body, **kw)(*operands)
        deps = list(_PENDING)
        n = len(operands)

        def tied(*refs):
            return body(*refs[:n], *refs[n + len(deps):])

        return pl.pallas_call(tied, **{**kw, "in_specs": list(kw["in_specs"]) + [_ANY] * len(deps)})(*operands, *deps)
    return run


def _copies(plan, refs, send_sems, recv_sems):
    return [pltpu.make_async_remote_copy(src_ref=src, dst_ref=dst, send_sem=send_sems.at[k], recv_sem=recv_sems.at[k],
                                         device_id=dev, device_id_type=MESH)
            for k, (src, dst, dev) in enumerate(plan(refs))]


def _xfer_start(name, bufs, plan, n_copies, after=None):
    n = len(bufs)
    deps = list(_PENDING) + ([after] if after is not None else [])
    nd = len(deps)

    def body(*refs):
        for cp in _copies(plan, refs[:n], refs[n + nd], refs[n + nd + 1]):
            cp.start()
        refs[-1][...] = jnp.zeros_like(refs[-1])

    outs = pl.pallas_call(
        body, name=name,
        out_shape=(pltpu.SemaphoreType.DMA((n_copies,)), pltpu.SemaphoreType.DMA((n_copies,)),
                   *[pltpu.HBM(b.shape, b.dtype) for b in bufs], jax.ShapeDtypeStruct((8, 128), F32)),
        in_specs=[_HBM] * n + [_ANY] * nd,
        out_specs=(_SEM, _SEM, *[_HBM] * n, pl.BlockSpec(memory_space=pltpu.VMEM)),
        input_output_aliases={t: 2 + t for t in range(n)},
        compiler_params=pltpu.CompilerParams(has_side_effects=_EFFECT),
    )(*[pltpu.with_memory_space_constraint(b, pltpu.HBM) for b in bufs], *deps)
    _PENDING[:] = [outs[-1]]
    return (outs[0], outs[1]), list(outs[2:2 + n])


def _xfer_wait(name, sems, bufs, plan, after):
    n = len(bufs)

    def body(*refs):
        cps = _copies(plan, refs[:n], refs[n], refs[n + 1])
        for cp in cps:
            cp.wait_send()
        for cp in cps:
            cp.wait_recv()

    outs = pl.pallas_call(
        body, name=name,
        out_shape=tuple(pltpu.HBM(b.shape, b.dtype) for b in bufs),
        in_specs=[_HBM] * n + [_SEM, _SEM, _ANY],
        out_specs=tuple([_HBM] * n),
        input_output_aliases={t: t for t in range(n)},
        compiler_params=pltpu.CompilerParams(has_side_effects=_EFFECT),
    )(*bufs, sems[0], sems[1], after)
    return list(outs)


def _half(ref_rows, hc):
    h = ref_rows // 2
    return pl.ds(hc * h, h)


def _plan_gather_ici(bufs):
    x, y, c = _pos()
    j = 2 * x + y
    return [(b.at[j, _half(b.shape[1], c)], b.at[j, _half(b.shape[1], c)], (*chip, c))
            for b in bufs for chip in _other_chips(x, y)]


def _plan_gather_d2d(bufs):
    x, y, c = _pos()
    out = []
    for b in bufs:
        for chip in _other_chips(x, y):
            blk = b.at[2 * chip[0] + chip[1], _half(b.shape[1], c)]
            out.append((blk, blk, (x, y, 1 - c)))
    return out


def _plan_pair_swap(n):
    def plan(bufs):
        x, y, c = _pos()
        return [(g.at[:, _half(g.shape[1], 1 - c)], land, (x, y, 1 - c)) for g, land in zip(bufs[:n], bufs[n:])]
    return plan


def _plan_chip_scatter(n):
    def plan(bufs):
        x, y, c = _pos()
        return [(p.at[2 * chip[0] + chip[1]], land.at[k], (*chip, c))
                for p, land in zip(bufs[:n], bufs[n:]) for k, chip in enumerate(_other_chips(x, y))]
    return plan


def _plan_pair_join(bufs):
    x, y, c = _pos()
    return [(b.at[_half(b.shape[0], c)], b.at[_half(b.shape[0], c)], (x, y, 1 - c)) for b in bufs]


def _empty_hbm(shape, dtype):
    return pltpu.with_memory_space_constraint(lax.empty(shape, dtype), pltpu.HBM)


def _gather_start(tag, bufs, after=None):
    sems, bufs = _xfer_start(f"gather_ici_start_{tag}", bufs, _plan_gather_ici, 3 * len(bufs), after)
    return dict(tag=tag, sems=sems, bufs=bufs)


def _gather_mid(st, after):
    tag = st["tag"]
    bufs = _xfer_wait(f"gather_ici_wait_{tag}", st["sems"], st["bufs"], _plan_gather_ici, after)
    sems, bufs = _xfer_start(f"gather_d2d_start_{tag}", bufs, _plan_gather_d2d, 3 * len(bufs))
    return dict(tag=tag, sems=sems, bufs=bufs)


def _gather_finish(st, after):
    return _xfer_wait(f"gather_d2d_wait_{st['tag']}", st["sems"], st["bufs"], _plan_gather_d2d, after)


def _pair_add(g, r, name):
    S, R, C = g.shape
    h = R // 2
    tr = _row_tile(h, C)
    nb = h // tr

    def body(c_ref, g_ref, r_ref, o_ref):
        o_ref[...] = (g_ref[...] + r_ref[...]).astype(BF16)

    return _pcall(
        body, name=name,
        grid_spec=pltpu.PrefetchScalarGridSpec(
            num_scalar_prefetch=1, grid=(S, nb),
            in_specs=[pl.BlockSpec((1, tr, C), lambda s, i, c_ref: (s, c_ref[0] * nb + i, 0)),
                      pl.BlockSpec((1, tr, C), lambda s, i, c_ref: (s, i, 0))],
            out_specs=pl.BlockSpec((1, tr, C), lambda s, i, c_ref: (s, i, 0))),
        out_shape=jax.ShapeDtypeStruct((S, h, C), BF16),
        compiler_params=_params(("parallel", "parallel")),
    )(lax.axis_index("c").astype(jnp.int32).reshape(1), g, r)


def _chip_sum(p, rb, name):
    S, h, C = p.shape
    tr = _row_tile(h, C)
    nb = h // tr
    jc = jnp.concatenate([_chip_index(), lax.axis_index("c").astype(jnp.int32).reshape(1)])

    def body(jc_ref, p_ref, r_ref, o_ref):
        o_ref[...] = ((p_ref[0].astype(F32) + r_ref[0].astype(F32)) + r_ref[1].astype(F32)) + r_ref[2].astype(F32)

    return _pcall(
        body, name=name,
        grid_spec=pltpu.PrefetchScalarGridSpec(
            num_scalar_prefetch=1, grid=(nb,),
            in_specs=[pl.BlockSpec((1, tr, C), lambda i, jc_ref: (jc_ref[0], i, 0)),
                      pl.BlockSpec((3, tr, C), lambda i, jc_ref: (0, i, 0))],
            out_specs=pl.BlockSpec((tr, C), lambda i, jc_ref: (jc_ref[1] * nb + i, 0))),
        out_shape=jax.ShapeDtypeStruct((2 * h, C), F32),
        compiler_params=_params(("parallel",)),
    )(jc, p, rb)


def _rs_start(tag, gs):
    n = len(gs)
    lands = [_empty_hbm((g.shape[0], g.shape[1] // 2, g.shape[2]), g.dtype) for g in gs]
    sems, bufs = _xfer_start(f"rs_swap_start_{tag}", list(gs) + lands, _plan_pair_swap(n), n)
    return dict(tag=tag, n=n, sems=sems, bufs=bufs)


def _rs_scatter(st, after):
    tag, n = st["tag"], st["n"]
    bufs = _xfer_wait(f"rs_swap_wait_{tag}", st["sems"], st["bufs"], _plan_pair_swap(n), after)
    ps = [_pair_add(g, r, f"rs_pair_add_{tag}{t}") for t, (g, r) in enumerate(zip(bufs[:n], bufs[n:]))]
    lands = [_empty_hbm((3,) + p.shape[1:], p.dtype) for p in ps]
    sems, bufs = _xfer_start(f"rs_scatter_start_{tag}", ps + lands, _plan_chip_scatter(n), 3 * n)
    return dict(tag=tag, n=n, sems=sems, bufs=bufs)


def _rs_join(st, after):
    tag, n = st["tag"], st["n"]
    bufs = _xfer_wait(f"rs_scatter_wait_{tag}", st["sems"], st["bufs"], _plan_chip_scatter(n), after)
    fs = [_chip_sum(p, rb, f"rs_chip_sum_{tag}{t}") for t, (p, rb) in enumerate(zip(bufs[:n], bufs[n:]))]
    sems, bufs = _xfer_start(f"rs_join_start_{tag}", fs, _plan_pair_join, n)
    return dict(tag=tag, n=n, sems=sems, bufs=bufs)


def _rs_finish(st, after):
    return _xfer_wait(f"rs_join_wait_{st['tag']}", st["sems"], st["bufs"], _plan_pair_join, after)


def _sum8(g, name):
    _, R, C = g.shape

    def body(g_ref, o_ref):
        acc = g_ref[0]
        for d in range(1, N_DEV):
            acc = acc + g_ref[d]
        o_ref[...] = acc

    return _pcall(body, name=name, out_shape=jax.ShapeDtypeStruct((R, C), F32),
                          compiler_params=_params())(g)


def _ada_fwd(cs, w, b):
    D, n = w.shape
    tn = _pick(n, (512, 384, 256, 128))

    def body(c_ref, w_ref, b_ref, o_ref):
        cv = c_ref[...]
        a = (cv * _sigmoid(cv)).astype(BF16)
        o_ref[...] = _dot(a, w_ref[...].astype(BF16)) + b_ref[...]

    return _pcall(
        body, name="ada_fwd", grid=(n // tn,),
        in_specs=[pl.BlockSpec((16, D), lambda j: (0, 0)), pl.BlockSpec((D, tn), lambda j: (0, j)),
                  pl.BlockSpec((1, tn), lambda j: (0, j))],
        out_specs=pl.BlockSpec((16, tn), lambda j: (0, j)),
        out_shape=jax.ShapeDtypeStruct((16, n), F32),
        compiler_params=_params(("parallel",)),
    )(cs, w, b)


def _ada_bwd(cs, w, dmod):
    D, n = w.shape
    tn = _pick(n, (512, 384, 256, 128))

    def body(c_ref, w_ref, d_ref, gw_ref, da_ref):
        j = pl.program_id(0)
        cv = c_ref[...]
        a = cv * _sigmoid(cv)
        d = d_ref[...]
        gw_ref[...] = lax.dot_general(a, d, (((0,), (0,)), ((), ())), precision=HI, preferred_element_type=F32)

        @pl.when(j == 0)
        def _():
            da_ref[...] = jnp.zeros_like(da_ref)

        da_ref[...] += _dot_nt(d.astype(BF16), w_ref[...].astype(BF16))

    return _pcall(
        body, name="ada_bwd", grid=(n // tn,),
        in_specs=[pl.BlockSpec((16, D), lambda j: (0, 0)), pl.BlockSpec((D, tn), lambda j: (0, j)),
                  pl.BlockSpec((16, tn), lambda j: (0, j))],
        out_specs=[pl.BlockSpec((D, tn), lambda j: (0, j)), pl.BlockSpec((16, D), lambda j: (0, 0))],
        out_shape=[jax.ShapeDtypeStruct((D, n), F32), jax.ShapeDtypeStruct((16, D), F32)],
        compiler_params=_params(("arbitrary",)),
    )(cs, w, dmod)


def _rms1_fwd(xall, gain, shift2, scale2, n_ctx):
    T, D = xall.shape
    tb = _pick(n_ctx, (256, 128, 64, 32, 16))
    nctx = n_ctx // tb

    def body(x_ref, g_ref, sh_ref, sc_ref, o_ref):
        i = pl.program_id(0)
        xv = x_ref[...]
        r = lax.rsqrt(jnp.mean(xv * xv, axis=-1, keepdims=True) + EPS)
        nrm = xv * r * g_ref[...]
        lat = i >= nctx
        sh = jnp.where(lat, sh_ref[1:2, :], sh_ref[0:1, :])
        sc = jnp.where(lat, sc_ref[1:2, :], sc_ref[0:1, :])
        o_ref[...] = (nrm * (1.0 + sc) + sh).astype(BF16)

    vec = lambda r: pl.BlockSpec((r, D), lambda i: (0, 0))
    return _pcall(
        body, name="rms1_fwd", grid=(T // tb,),
        in_specs=[pl.BlockSpec((tb, D), lambda i: (i, 0)), vec(1), vec(2), vec(2)],
        out_specs=pl.BlockSpec((tb, D), lambda i: (i, 0)),
        out_shape=jax.ShapeDtypeStruct((T, D), BF16),
        compiler_params=_params(("parallel",)),
    )(xall, gain, shift2, scale2)


def _rms1_bwd(xall, dh, dxmid, gain, scale2, n_ctx):
    T, D = xall.shape
    L = T - n_ctx
    tb = _pick(n_ctx, (256, 128, 64, 32, 16))
    nctx = n_ctx // tb

    def body(x_ref, dh_ref, dxm_ref, g_ref, sc_ref, dx_ref, cs_ref):
        i = pl.program_id(0)
        lat = i >= nctx
        xv = x_ref[...]
        r = lax.rsqrt(jnp.mean(xv * xv, axis=-1, keepdims=True) + EPS)
        xh = xv * r
        g = g_ref[...]
        nrm = xh * g
        sc = jnp.where(lat, sc_ref[1:2, :], sc_ref[0:1, :])
        dhv = dh_ref[...]
        dn = dhv * (1.0 + sc)
        dxh = dn * g
        dxv = r * (dxh - xh * jnp.mean(dxh * xh, axis=-1, keepdims=True))
        s_sh = jnp.sum(dhv, axis=0, keepdims=True)
        s_sc = jnp.sum(dhv * nrm, axis=0, keepdims=True)
        s_g = jnp.sum(dn * xh, axis=0, keepdims=True)
        zero = jnp.zeros_like(s_sh)
        rows = lax.broadcasted_iota(jnp.int32, (8, D), 0)
        upd = jnp.where(rows == 0, jnp.where(lat, zero, s_sh),
              jnp.where(rows == 1, jnp.where(lat, zero, s_sc),
              jnp.where(rows == 2, jnp.where(lat, s_sh, zero),
              jnp.where(rows == 3, jnp.where(lat, s_sc, zero),
              jnp.where(rows == 4, s_g, 0.0)))))

        @pl.when(i == 0)
        def _():
            cs_ref[...] = jnp.zeros_like(cs_ref)

        cs_ref[...] += upd

        @pl.when(lat)
        def _():
            dx_ref[...] = dxv + dxm_ref[...]

    lat_blk = lambda i: (jnp.maximum(i - nctx, 0), 0)
    vec = lambda r: pl.BlockSpec((r, D), lambda i: (0, 0))
    return _pcall(
        body, name="rms1_bwd", grid=(T // tb,),
        in_specs=[pl.BlockSpec((tb, D), lambda i: (i, 0)), pl.BlockSpec((tb, D), lambda i: (i, 0)),
                  pl.BlockSpec((tb, D), lat_blk), vec(1), vec(2)],
        out_specs=[pl.BlockSpec((tb, D), lat_blk), vec(8)],
        out_shape=[jax.ShapeDtypeStruct((L, D), F32), jax.ShapeDtypeStruct((8, D), F32)],
        compiler_params=_params(("arbitrary",)),
    )(xall, dh, dxmid, gain, scale2)


def _resid_rms2_fwd(x, mo, vecs):
    L, D = x.shape
    tb = _pick(L, (256, 128, 64))

    def body(x_ref, mo_ref, v_ref, xm_ref, h_ref):
        xm = x_ref[...] + v_ref[0:1, :] * mo_ref[...]
        xm_ref[...] = xm
        r = lax.rsqrt(jnp.mean(xm * xm, axis=-1, keepdims=True) + EPS)
        h_ref[...] = (xm * r * v_ref[1:2, :] * (1.0 + v_ref[3:4, :]) + v_ref[2:3, :]).astype(BF16)

    blk = pl.BlockSpec((tb, D), lambda i: (i, 0))
    return _pcall(
        body, name="resid_rms2_fwd", grid=(L // tb,),
        in_specs=[blk, blk, pl.BlockSpec((8, D), lambda i: (0, 0))],
        out_specs=[blk, blk],
        out_shape=[jax.ShapeDtypeStruct((L, D), F32), jax.ShapeDtypeStruct((L, D), BF16)],
        compiler_params=_params(("parallel",)),
    )(x, mo, vecs)


def _resid_rms2_bwd(xmid, dh_a, dh_b, dy, mo, vecs):
    L, D = xmid.shape
    tb = _pick(L, (256, 128, 64))

    def body(xm_ref, da_ref, db_ref, dy_ref, mo_ref, v_ref, dxm_ref, dmo_ref, cs_ref):
        i = pl.program_id(0)
        xm = xm_ref[...]
        r = lax.rsqrt(jnp.mean(xm * xm, axis=-1, keepdims=True) + EPS)
        xh = xm * r
        g = v_ref[1:2, :]
        nrm = xh * g
        dhv = da_ref[...] + db_ref[...]
        dn = dhv * (1.0 + v_ref[3:4, :])
        dxh = dn * g
        dxm = dy_ref[...] + r * (dxh - xh * jnp.mean(dxh * xh, axis=-1, keepdims=True))
        dxm_ref[...] = dxm
        dmo_ref[...] = (dxm * v_ref[0:1, :]).astype(BF16)
        s0 = jnp.sum(dhv, axis=0, keepdims=True)
        s1 = jnp.sum(dhv * nrm, axis=0, keepdims=True)
        s2 = jnp.sum(dn * xh, axis=0, keepdims=True)
        s3 = jnp.sum(dxm * mo_ref[...], axis=0, keepdims=True)
        rows = lax.broadcasted_iota(jnp.int32, (8, D), 0)
        upd = jnp.where(rows == 0, s0, jnp.where(rows == 1, s1, jnp.where(rows == 2, s2,
              jnp.where(rows == 3, s3, 0.0))))

        @pl.when(i == 0)
        def _():
            cs_ref[...] = jnp.zeros_like(cs_ref)

        cs_ref[...] += upd

    blk = pl.BlockSpec((tb, D), lambda i: (i, 0))
    vec = pl.BlockSpec((8, D), lambda i: (0, 0))
    return _pcall(
        body, name="resid_rms2_bwd", grid=(L // tb,),
        in_specs=[blk, blk, blk, blk, blk, vec],
        out_specs=[blk, blk, vec],
        out_shape=[jax.ShapeDtypeStruct((L, D), F32), jax.ShapeDtypeStruct((L, D), BF16),
                   jax.ShapeDtypeStruct((8, D), F32)],
        compiler_params=_params(("arbitrary",)),
    )(xmid, dh_a, dh_b, dy, mo, vecs)


def _loss_head(xmid, f, g2, target):
    L, D = xmid.shape
    tb = _pick(L, (256, 128, 64))

    def body(xm_ref, f_ref, g_ref, t_ref, dy_ref, df_ref, s_ref):
        i = pl.program_id(0)
        fv = f_ref[...]
        g = g_ref[...]
        err = xm_ref[...] + g * fv - t_ref[...]
        dy = err * (1.0 / D)
        dy_ref[...] = dy
        df_ref[...] = (dy * g).astype(BF16)
        s0 = jnp.sum(dy * fv, axis=0, keepdims=True)
        part = 0.5 * jnp.sum(jnp.mean(err * err, axis=-1, keepdims=True), axis=0, keepdims=True)
        rows = lax.broadcasted_iota(jnp.int32, (8, D), 0)
        upd = jnp.where(rows == 0, s0, jnp.where(rows == 1, part, 0.0))

        @pl.when(i == 0)
        def _():
            s_ref[...] = jnp.zeros_like(s_ref)

        s_ref[...] += upd

    blk = pl.BlockSpec((tb, D), lambda i: (i, 0))
    return _pcall(
        body, name="loss_head", grid=(L // tb,),
        in_specs=[blk, blk, pl.BlockSpec((1, D), lambda i: (0, 0)), blk],
        out_specs=[blk, blk, pl.BlockSpec((8, D), lambda i: (0, 0))],
        out_shape=[jax.ShapeDtypeStruct((L, D), F32), jax.ShapeDtypeStruct((L, D), BF16),
                   jax.ShapeDtypeStruct((8, D), F32)],
        compiler_params=_params(("arbitrary",)),
    )(xmid, f, g2, target)


def _gate_cols(D, off):
    tc = _pick(np.gcd(D, off), (512, 256, 128))
    return tc, off // tc


def _merge_fwd(za, zb, p, n_ctx, off_a, off_b):
    L, D = za.shape
    tb = _pick(n_ctx, (256, 128, 64, 32, 16))
    nctx = n_ctx // tb
    tc, oa = _gate_cols(D, off_a)
    _, ob = _gate_cols(D, off_b)
    if off_b % tc:
        raise ValueError("gate column offsets must share a column tile")
    ob = off_b // tc

    def body(za_ref, zb_ref, ga_ref, gb_ref, z_ref):
        z_ref[...] = (_sigmoid(ga_ref[...]) * za_ref[...] + _sigmoid(gb_ref[...]) * zb_ref[...]).astype(BF16)

    blk = pl.BlockSpec((tb, tc), lambda i, j: (i, j))
    return _pcall(
        body, name="merge_fwd", grid=(L // tb, D // tc),
        in_specs=[blk, blk, pl.BlockSpec((tb, tc), lambda i, j: (i + nctx, oa + j)),
                  pl.BlockSpec((tb, tc), lambda i, j: (i + nctx, ob + j))],
        out_specs=blk,
        out_shape=jax.ShapeDtypeStruct((L, D), BF16),
        compiler_params=_params(("parallel", "parallel")),
    )(za, zb, p, p)


def _merge_bwd(dz, za, zb, p, n_ctx, off_a, off_b):
    L, D = za.shape
    T = L + n_ctx
    tb = _pick(n_ctx, (256, 128, 64, 32, 16))
    nctx = n_ctx // tb
    tc = _gate_cols(D, off_a)[0]
    oa, ob = off_a // tc, off_b // tc

    def body(dz_ref, za_ref, zb_ref, ga_ref, gb_ref, dza_ref, dzb_ref, dga_ref, dgb_ref):
        i = pl.program_id(1)

        @pl.when(i < nctx)
        def _():
            dga_ref[...] = jnp.zeros_like(dga_ref)
            dgb_ref[...] = jnp.zeros_like(dgb_ref)

        @pl.when(i >= nctx)
        def _():
            dzv = dz_ref[...]
            sa = _sigmoid(ga_ref[...])
            sb = _sigmoid(gb_ref[...])
            dza_ref[...] = (dzv * sa).astype(BF16)
            dzb_ref[...] = (dzv * sb).astype(BF16)
            dga_ref[...] = (dzv * za_ref[...] * sa * (1.0 - sa)).astype(BF16)
            dgb_ref[...] = (dzv * zb_ref[...] * sb * (1.0 - sb)).astype(BF16)

    lat = pl.BlockSpec((tb, tc), lambda j, i: (jnp.maximum(i - nctx, 0), j))
    allr = pl.BlockSpec((tb, tc), lambda j, i: (i, j))
    return _pcall(
        body, name="merge_bwd", grid=(D // tc, T // tb),
        in_specs=[lat, lat, lat, pl.BlockSpec((tb, tc), lambda j, i: (i, oa + j)),
                  pl.BlockSpec((tb, tc), lambda j, i: (i, ob + j))],
        out_specs=[lat, lat, allr, allr],
        out_shape=[jax.ShapeDtypeStruct((L, D), BF16), jax.ShapeDtypeStruct((L, D), BF16),
                   jax.ShapeDtypeStruct((T, D), BF16), jax.ShapeDtypeStruct((T, D), BF16)],
        compiler_params=_params(("arbitrary", "arbitrary")),
    )(dz, za, zb, p, p)


def _shift_down(u, rows):
    return jnp.where(rows == 0, 0.0, pltpu.roll(u, 1, 0))


def _shift_up(u, rows):
    n = u.shape[0]
    return jnp.where(rows == n - 1, 0.0, pltpu.roll(u, n - 1, 0))


def _convgate_fwd(u1, u3, cw, cb):
    L, F = u1.shape
    tc = _pick(F, (256, 128))

    def body(u1_ref, u3_ref, w_ref, b_ref, a_ref):
        u = u1_ref[...]
        rows = lax.broadcasted_iota(jnp.int32, u.shape, 0)
        cv = _shift_down(u, rows) * w_ref[0:1, :] + u * w_ref[1:2, :] + _shift_up(u, rows) * w_ref[2:3, :] + b_ref[...]
        a_ref[...] = (cv * _sigmoid(cv) * u3_ref[...]).astype(BF16)

    blk = pl.BlockSpec((L, tc), lambda j: (0, j))
    return _pcall(
        body, name="convgate_fwd", grid=(F // tc,),
        in_specs=[blk, blk, pl.BlockSpec((8, tc), lambda j: (0, j)), pl.BlockSpec((1, tc), lambda j: (0, j))],
        out_specs=blk,
        out_shape=jax.ShapeDtypeStruct((L, F), BF16),
        compiler_params=_params(("parallel",)),
    )(u1, u3, cw, cb)


def _convgate_bwd(u1, u3, da, cw, cb):
    L, F = u1.shape
    tc = _pick(F, (256, 128))

    def body(u1_ref, u3_ref, da_ref, w_ref, b_ref, du1_ref, du3_ref, s_ref):
        u = u1_ref[...]
        rows = lax.broadcasted_iota(jnp.int32, u.shape, 0)
        um, up = _shift_down(u, rows), _shift_up(u, rows)
        w0, w1, w2 = w_ref[0:1, :], w_ref[1:2, :], w_ref[2:3, :]
        cv = um * w0 + u * w1 + up * w2 + b_ref[...]
        s = _sigmoid(cv)
        dav = da_ref[...]
        du3_ref[...] = (dav * cv * s).astype(BF16)
        dcv = dav * u3_ref[...] * (s * (1.0 + cv * (1.0 - s)))
        du1_ref[...] = (_shift_up(dcv, rows) * w0 + dcv * w1 + _shift_down(dcv, rows) * w2).astype(BF16)
        r8 = lax.broadcasted_iota(jnp.int32, (8, tc), 0)
        s0 = jnp.sum(dcv * um, axis=0, keepdims=True)
        s1 = jnp.sum(dcv * u, axis=0, keepdims=True)
        s2 = jnp.sum(dcv * up, axis=0, keepdims=True)
        s3 = jnp.sum(dcv, axis=0, keepdims=True)
        s_ref[...] = jnp.where(r8 == 0, s0, jnp.where(r8 == 1, s1, jnp.where(r8 == 2, s2,
                     jnp.where(r8 == 3, s3, 0.0))))

    blk = pl.BlockSpec((L, tc), lambda j: (0, j))
    v8 = pl.BlockSpec((8, tc), lambda j: (0, j))
    return _pcall(
        body, name="convgate_bwd", grid=(F // tc,),
        in_specs=[blk, blk, blk, v8, pl.BlockSpec((1, tc), lambda j: (0, j))],
        out_specs=[blk, blk, v8],
        out_shape=[jax.ShapeDtypeStruct((L, F), BF16), jax.ShapeDtypeStruct((L, F), BF16),
                   jax.ShapeDtypeStruct((8, F), F32)],
        compiler_params=_params(("parallel",)),
    )(u1, u3, da, cw, cb)


def _lower_bound(lbl_ref, d):
    l0, l1 = lbl_ref[d, 0:1, :], lbl_ref[d, 1:2, :]
    m = jnp.maximum(l0, l1)
    e0, e1 = jnp.exp(l0 - m), jnp.exp(l1 - m)
    return e0 / (e0 + e1)


def _chunk_cumsum(x, rev):
    n = x.shape[0]
    r = lax.broadcasted_iota(jnp.int32, x.shape, 0) % CHUNK
    k = 1
    while k < CHUNK:
        if rev:
            x = x + jnp.where(r < CHUNK - k, pltpu.roll(x, n - k, 0), 0.0)
        else:
            x = x + jnp.where(r >= k, pltpu.roll(x, k, 0), 0.0)
        k *= 2
    return x


def _gate_terms(z, lb):
    sg = _sigmoid(z)
    f = lb + (1.0 - lb) * sg
    return sg, f


def _decay_terms(z, lb, rev):
    _, f = _gate_terms(z, lb)
    g = jnp.log(f)
    return 1.0 - f, _chunk_cumsum(g, rev), _chunk_cumsum(g, not rev) - g


def _chunk_total(c, rev):
    return c[0:1, :] if rev else c[CHUNK - 1:CHUNK, :]


def _pair_decay(c, s, rev):
    t = lax.broadcasted_iota(jnp.int32, (CHUNK, 1), 0)
    later = (t <= s) if rev else (t >= s)
    return jnp.where(later, jnp.exp(jnp.minimum(c - c[s:s + 1, :], 0.0)), 0.0)


def _scan_chunk(i, n_ctx_chunks, n_chunks, rev):
    if not rev:
        return i
    return jnp.where(i < n_ctx_chunks, n_ctx_chunks - 1 - i, n_chunks + n_ctx_chunks - 1 - i)


def _rows(ci):
    return pl.ds(pl.multiple_of(ci * CHUNK, CHUNK), CHUNK)


def _hgrn_cols(HA):
    return HA // HEAD


def _hgrn_fwd(p, lbl, ng, n_ctx, HA):
    T = p.shape[0]
    L = T - n_ctx
    nh = _hgrn_cols(HA)
    nc, ncc = T // CHUNK, n_ctx // CHUNK

    def body(q_ref, zf_ref, zb_ref, v_ref, og_ref, lbl_ref, ng_ref, ya_ref, o_ref,
             c_scr, k_scr, qe_scr, ke_scr, o_scr):
        dirs = ((0, False, zf_ref), (1, True, zb_ref))
        for d, rev, z_ref in dirs:
            k, c, rest = _decay_terms(z_ref[...], _lower_bound(lbl_ref, d), rev)
            c_scr[d] = c
            k_scr[d] = k
            qe_scr[d] = (q_ref[...] * jnp.exp(c)).astype(BF16)
            ke_scr[d] = (k * jnp.exp(rest)).astype(BF16)

        def step(i, states):
            new = []
            for (d, rev, _), St in zip(dirs, states):
                rows = _rows(_scan_chunk(i, ncc, nc, rev))
                q, v, c, k = q_ref[rows, :], v_ref[rows, :], c_scr[d, rows, :], k_scr[d, rows, :]
                o = _dot_nt(qe_scr[d, rows, :], St.astype(BF16))
                for s in range(CHUNK):
                    E = _pair_decay(c, s, rev)
                    a = jnp.sum(q * E * k[s:s + 1, :], axis=1, keepdims=True)
                    o = o + a * v[s:s + 1, :]
                o_scr[d, rows, :] = o
                new.append(St * jnp.exp(_chunk_total(c, rev)) + _dot_tn(v.astype(BF16), ke_scr[d, rows, :]))
            return tuple(new)

        zero = jnp.zeros((HEAD, HEAD), F32)
        lax.fori_loop(0, nc, step, (zero, zero), unroll=2)

        o = o_scr[0, pl.ds(n_ctx, L), :] + o_scr[1, pl.ds(n_ctx, L), :]
        o_ref[...] = o
        r = lax.rsqrt(jnp.mean(o * o, axis=-1, keepdims=True) + EPS)
        og = og_ref[pl.ds(n_ctx, L), :]
        ya_ref[...] =(o * r * ng_ref[...] * (og * _sigmoid(og))).astype(BF16)

    cb = HA // HEAD
    col = lambda kk: pl.BlockSpec((T, HEAD), lambda h: (0, kk * cb + h))
    return _pcall(
        body, name="hgrn_fwd", grid=(nh,),
        in_specs=[col(0), col(1), col(2), col(3), col(4),
                  pl.BlockSpec((2, 2, HEAD), lambda h: (0, 0, h)), pl.BlockSpec((1, HEAD), lambda h: (0, 0))],
        out_specs=[pl.BlockSpec((L, HEAD), lambda h: (0, h)), pl.BlockSpec((L, HEAD), lambda h: (0, h))],
        out_shape=[jax.ShapeDtypeStruct((L, HA), BF16), jax.ShapeDtypeStruct((L, HA), F32)],
        scratch_shapes=[pltpu.VMEM((2, T, HEAD), F32), pltpu.VMEM((2, T, HEAD), F32),
                        pltpu.VMEM((2, T, HEAD), BF16), pltpu.VMEM((2, T, HEAD), BF16),
                        pltpu.VMEM((2, T, HEAD), F32)],
        compiler_params=_params(("parallel",)),
    )(p, p, p, p, p, lbl, ng)


def _hgrn_bwd(p, lbl, ng, o, dya, n_ctx, HA):
    T = p.shape[0]
    L = T - n_ctx
    nh = _hgrn_cols(HA)
    nc, ncc = T // CHUNK, n_ctx // CHUNK

    def body(q_ref, zf_ref, zb_ref, v_ref, og_ref, lbl_ref, ng_ref, o_ref, dya_ref,
             dq_ref, dzf_ref, dzb_ref, dv_ref, dog_ref, dlbl_ref, dng_ref,
             do_scr, st_scr, c_scr, k_scr, qe_scr, ke_scr, dg_scr, dk_scr, dq_scr, dv_scr):
        h = pl.program_id(0)
        ov = o_ref[...]
        r = lax.rsqrt(jnp.mean(ov * ov, axis=-1, keepdims=True) + EPS)
        oh = ov * r
        ogv = og_ref[pl.ds(n_ctx, L), :]
        sg_o = _sigmoid(ogv)
        dyv = dya_ref[...]
        ngv = ng_ref[...]
        dog_ref[pl.ds(0, n_ctx), :] = jnp.zeros((n_ctx, HEAD), BF16)
        dog_ref[pl.ds(n_ctx, L), :] = (dyv * oh * ngv * (sg_o * (1.0 + ogv * (1.0 - sg_o)))).astype(BF16)
        don = dyv * (ogv * sg_o)
        dng = jnp.sum(don * oh, axis=0, keepdims=True)
        doh = don * ngv
        do_scr[pl.ds(0, n_ctx), :] = jnp.zeros((n_ctx, HEAD), F32)
        do_scr[pl.ds(n_ctx, L), :] = r * (doh - oh * jnp.mean(doh * oh, axis=-1, keepdims=True))

        @pl.when(h == 0)
        def _():
            dng_ref[...] = jnp.zeros_like(dng_ref)

        dng_ref[0:1, :] += dng

        t16 = lax.broadcasted_iota(jnp.int32, (CHUNK, HEAD), 0)
        dirs = ((0, False, zf_ref, dzf_ref), (1, True, zb_ref, dzb_ref))
        for d, rev, z_ref, _ in dirs:
            k, c, rest = _decay_terms(z_ref[...], _lower_bound(lbl_ref, d), rev)
            c_scr[d] = c
            k_scr[d] = k
            qe_scr[d] = (q_ref[...] * jnp.exp(c)).astype(BF16)
            ke_scr[d] = (k * jnp.exp(rest)).astype(BF16)
        dq_scr[...] = jnp.zeros_like(dq_scr)
        dv_scr[...] = jnp.zeros_like(dv_scr)

        def fwd_step(i, states):
            new = []
            for (d, rev, _, _), St in zip(dirs, states):
                ci = _scan_chunk(i, ncc, nc, rev)
                rows = _rows(ci)
                st_scr[d, ci] = St.astype(BF16)
                etot = jnp.exp(_chunk_total(c_scr[d, rows, :], rev))
                new.append(St * etot + _dot_tn(v_ref[rows, :].astype(BF16), ke_scr[d, rows, :]))
            return tuple(new)

        zero = jnp.zeros((HEAD, HEAD), F32)
        lax.fori_loop(0, nc, fwd_step, (zero, zero), unroll=2)

        def bwd_step(ii, carry):
            i = nc - 1 - ii
            new = []
            for (d, rev, _, _), dSt in zip(dirs, carry):
                ci = _scan_chunk(i, ncc, nc, rev)
                rows = _rows(ci)
                q, v, do = q_ref[rows, :], v_ref[rows, :], do_scr[rows, :]
                c, k = c_scr[d, rows, :], k_scr[d, rows, :]
                tot = _chunk_total(c, rev)
                etot = jnp.exp(tot)
                St = st_scr[d, ci]
                dSb = dSt.astype(BF16)
                do_b = do.astype(BF16)
                dq = _dot(do_b, St) * jnp.exp(c)
                dk = _dot(v.astype(BF16), dSb) * jnp.exp(tot - c)
                dv = _dot_nt(ke_scr[d, rows, :], dSb)
                dtot = (jnp.sum(St.astype(F32) * dSt, axis=0, keepdims=True) * etot
                        + jnp.sum(k * dk, axis=0, keepdims=True))
                for s in range(CHUNK):
                    E = _pair_decay(c, s, rev)
                    XE = E * k[s:s + 1, :]
                    a = jnp.sum(q * XE, axis=1, keepdims=True)
                    da = jnp.sum(do * v[s:s + 1, :], axis=1, keepdims=True)
                    dq = dq + da * XE
                    dk_row = jnp.sum(da * q * E, axis=0, keepdims=True)
                    dv_row = jnp.sum(a * do, axis=0, keepdims=True)
                    dk = dk + jnp.where(t16 == s, dk_row, 0.0)
                    dv = dv + jnp.where(t16 == s, dv_row, 0.0)
                dg_scr[d, rows, :] = _chunk_cumsum(q * dq - k * dk, not rev) + dtot
                dk_scr[d, rows, :] = dk
                dq_scr[rows, :] += dq
                dv_scr[rows, :] += dv
                new.append(dSt * etot + _dot_tn(do_b, qe_scr[d, rows, :]))
            return tuple(new)

        lax.fori_loop(0, nc, bwd_step, (zero, zero), unroll=2)

        for d, _, z_ref, dz_ref in dirs:
            lb = _lower_bound(lbl_ref, d)
            sg, f = _gate_terms(z_ref[...], lb)
            df = dg_scr[d] / f - dk_scr[d]
            dz_ref[...] = (df * (1.0 - lb) * sg * (1.0 - sg)).astype(BF16)
            dl0 = jnp.sum(df * (1.0 - sg), axis=0, keepdims=True) * lb * (1.0 - lb)
            dlbl_ref[d, 0:1, :] = dl0
            dlbl_ref[d, 1:2, :] = -dl0
        dq_ref[...] = dq_scr[...].astype(BF16)
        dv_ref[...] = dv_scr[...].astype(BF16)

    cb = HA // HEAD
    col = lambda kk: pl.BlockSpec((T, HEAD), lambda h: (0, kk * cb + h))
    tcol = pl.BlockSpec((T, HEAD), lambda h: (0, h))
    lcol = pl.BlockSpec((L, HEAD), lambda h: (0, h))
    outs = _pcall(
        body, name="hgrn_bwd", grid=(nh,),
        in_specs=[col(0), col(1), col(2), col(3), col(4),
                  pl.BlockSpec((2, 2, HEAD), lambda h: (0, 0, h)), pl.BlockSpec((1, HEAD), lambda h: (0, 0)),
                  lcol, lcol],
        out_specs=[tcol, tcol, tcol, tcol, tcol, pl.BlockSpec((2, 2, HEAD), lambda h: (0, 0, h)),
                   pl.BlockSpec((8, HEAD), lambda h: (0, 0))],
        out_shape=[jax.ShapeDtypeStruct((T, HA), BF16)] * 5 + [jax.ShapeDtypeStruct((2, 2, HA), F32),
                                                               jax.ShapeDtypeStruct((8, HEAD), F32)],
        scratch_shapes=[pltpu.VMEM((T, HEAD), F32), pltpu.VMEM((2, nc, HEAD, HEAD), BF16),
                        pltpu.VMEM((2, T, HEAD), F32), pltpu.VMEM((2, T, HEAD), F32),
                        pltpu.VMEM((2, T, HEAD), BF16), pltpu.VMEM((2, T, HEAD), BF16),
                        pltpu.VMEM((2, T, HEAD), F32), pltpu.VMEM((2, T, HEAD), F32),
                        pltpu.VMEM((T, HEAD), F32), pltpu.VMEM((T, HEAD), F32)],
        compiler_params=_params(("arbitrary",)),
    )(p, p, p, p, p, lbl, ng, o, dya)
    return outs


def _swap_halves(t, lane):
    q = HEAD // 4
    return jnp.where((lane % (2 * q)) < q, pltpu.roll(t, HEAD - q, 1), pltpu.roll(t, q, 1))


def _qk_norm(t, g):
    r = lax.rsqrt(jnp.mean(t * t, axis=-1, keepdims=True) + EPS)
    return t * r, r


def _rope(t, cos, sin, lane):
    return t * cos + _swap_halves(t, lane) * sin


def _qk_norm_bwd(dy, th, r, g):
    dth = dy * g
    return r * (dth - th * jnp.mean(dth * th, axis=-1, keepdims=True)), jnp.sum(dy * th, axis=0, keepdims=True)


def _rope_bwd(dy, cos, sin, lane):
    return dy * cos + _swap_halves(dy * sin, lane)


def _na_geometry(L):
    n_rows = L // GRID_W
    kr = min(WIN_R, n_rows)
    return n_rows, kr


def _na_prep(q_ref, k_ref, v_ref, gq_ref, gk_ref, cos_ref, sin_ref, qs, ks, vs, n_ctx, L):
    lane = lax.broadcasted_iota(jnp.int32, (L, HEAD), 1)
    cos, sin = cos_ref[...], sin_ref[...]
    qh, _ = _qk_norm(q_ref[pl.ds(n_ctx, L), :], None)
    qs[...] = _rope(qh * gq_ref[...], cos, sin, lane).astype(BF16)
    kh, _ = _qk_norm(k_ref[pl.ds(n_ctx, L), :], None)
    ks[pl.ds(n_ctx, L), :] = _rope(kh * gk_ref[...], cos, sin, lane).astype(BF16)
    kc, _ = _qk_norm(k_ref[pl.ds(0, n_ctx), :], None)
    ks[pl.ds(0, n_ctx), :] = (kc * gk_ref[...]).astype(BF16)
    vs[...] = v_ref[...].astype(BF16)


def _na_scores(r, qs, ks, bias_ref, n_ctx, n_rows, kr):
    scale = HEAD ** -0.5
    r0 = jnp.clip(r - WIN_R // 2, 0, n_rows - kr)
    qrows = pl.ds(pl.multiple_of(r * GRID_W, GRID_W), GRID_W)
    krows = pl.ds(pl.multiple_of(n_ctx + r0 * GRID_W, GRID_W), kr * GRID_W)
    qv = qs[qrows, :]
    sb = _dot_nt(qv, ks[krows, :]) * scale
    b0 = r0 - r + (WIN_R - 1)
    sb = sb + jnp.concatenate([bias_ref[0, b0 + 2 * jj] for jj in range(kr // 2)], axis=1)
    sc = _dot_nt(qv, ks[pl.ds(0, n_ctx), :]) * scale
    m = jnp.maximum(jnp.max(sb, axis=1, keepdims=True), jnp.max(sc, axis=1, keepdims=True))
    eb, ec = jnp.exp(sb - m), jnp.exp(sc - m)
    inv = 1.0 / (jnp.sum(eb, axis=1, keepdims=True) + jnp.sum(ec, axis=1, keepdims=True))
    return eb * inv, ec * inv, qrows, krows, b0


def _na_fwd(p, bias, gq, gk, cos, sin, n_ctx, off, HB):
    T = p.shape[0]
    L = T - n_ctx
    nh = HB // HEAD
    n_rows, kr = _na_geometry(L)
    ob = off // HEAD

    def body(q_ref, k_ref, v_ref, bias_ref, gq_ref, gk_ref, cos_ref, sin_ref, y_ref, qs, ks, vs):
        _na_prep(q_ref, k_ref, v_ref, gq_ref, gk_ref, cos_ref, sin_ref, qs, ks, vs, n_ctx, L)

        def step(r, carry):
            pb, pc, qrows, krows, _ = _na_scores(r, qs, ks, bias_ref, n_ctx, n_rows, kr)
            y = _dot(pb.astype(BF16), vs[krows, :]) + _dot(pc.astype(BF16), vs[pl.ds(0, n_ctx), :])
            y_ref[qrows, :] = y.astype(BF16)
            return carry

        lax.fori_loop(0, n_rows, step, 0)

    col = lambda kk: pl.BlockSpec((T, HEAD), lambda h: (0, ob + kk * nh + h))
    vec = pl.BlockSpec((1, HEAD), lambda h: (0, 0))
    tab = pl.BlockSpec((L, HEAD), lambda h: (0, 0))
    return _pcall(
        body, name="na_fwd", grid=(nh,),
        in_specs=[col(0), col(1), col(2), pl.BlockSpec((1,) + bias.shape[1:], lambda h: (h, 0, 0, 0)),
                  vec, vec, tab, tab],
        out_specs=pl.BlockSpec((L, HEAD), lambda h: (0, h)),
        out_shape=jax.ShapeDtypeStruct((L, HB), BF16),
        scratch_shapes=[pltpu.VMEM((L, HEAD), BF16), pltpu.VMEM((T, HEAD), BF16), pltpu.VMEM((T, HEAD), BF16)],
        compiler_params=_params(("parallel",)),
    )(p, p, p, bias, gq, gk, cos, sin)


def _na_bwd(p, bias, gq, gk, cos, sin, dyb, n_ctx, off, HB):
    T = p.shape[0]
    L = T - n_ctx
    nh = HB // HEAD
    n_rows, kr = _na_geometry(L)
    ob = off // HEAD
    scale = HEAD ** -0.5

    def body(q_ref, k_ref, v_ref, bias_ref, gq_ref, gk_ref, cos_ref, sin_ref, dy_ref,
             dq_ref, dk_ref, dv_ref, dbias_ref, dg_ref, qs, ks, vs, dqa, dka, dva):
        h = pl.program_id(0)
        _na_prep(q_ref, k_ref, v_ref, gq_ref, gk_ref, cos_ref, sin_ref, qs, ks, vs, n_ctx, L)
        dka[...] = jnp.zeros_like(dka)
        dva[...] = jnp.zeros_like(dva)
        dbias_ref[...] = jnp.zeros_like(dbias_ref)

        def step(r, carry):
            pb, pc, qrows, krows, b0 = _na_scores(r, qs, ks, bias_ref, n_ctx, n_rows, kr)
            crows = pl.ds(0, n_ctx)
            do = dy_ref[qrows, :]
            qv = qs[qrows, :]
            dpb = _dot_nt(do, vs[krows, :])
            dpc = _dot_nt(do, vs[crows, :])
            delta = jnp.sum(pb * dpb, axis=1, keepdims=True) + jnp.sum(pc * dpc, axis=1, keepdims=True)
            dsb = pb * (dpb - delta)
            dsc = pc * (dpc - delta)
            for jj in range(kr // 2):
                dbias_ref[0, b0 + 2 * jj] += dsb[:, jj * 2 * GRID_W:(jj + 1) * 2 * GRID_W]
            dsb_b, dsc_b = dsb.astype(BF16), dsc.astype(BF16)
            dqa[qrows, :] = (_dot(dsb_b, ks[krows, :]) + _dot(dsc_b, ks[crows, :])) * scale
            dka[krows, :] += _dot_tn(dsb_b, qv) * scale
            dka[crows, :] += _dot_tn(dsc_b, qv) * scale
            dva[krows, :] += _dot_tn(pb.astype(BF16), do)
            dva[crows, :] += _dot_tn(pc.astype(BF16), do)
            return carry

        lax.fori_loop(0, n_rows, step, 0)

        lane = lax.broadcasted_iota(jnp.int32, (L, HEAD), 1)
        cos, sin = cos_ref[...], sin_ref[...]
        lat, ctx = pl.ds(n_ctx, L), pl.ds(0, n_ctx)
        gqv, gkv = gq_ref[...], gk_ref[...]
        qh, rq = _qk_norm(q_ref[lat, :], None)
        dq, dgq = _qk_norm_bwd(_rope_bwd(dqa[...], cos, sin, lane), qh, rq, gqv)
        dq_ref[ctx, :] = jnp.zeros((n_ctx, HEAD), BF16)
        dq_ref[lat, :] = dq.astype(BF16)
        kh, rk = _qk_norm(k_ref[lat, :], None)
        dk, dgk = _qk_norm_bwd(_rope_bwd(dka[lat, :], cos, sin, lane), kh, rk, gkv)
        dk_ref[lat, :] = dk.astype(BF16)
        kch, rkc = _qk_norm(k_ref[ctx, :], None)
        dkc, dgkc = _qk_norm_bwd(dka[ctx, :], kch, rkc, gkv)
        dk_ref[ctx, :] = dkc.astype(BF16)
        dv_ref[...] = dva[...].astype(BF16)

        @pl.when(h == 0)
        def _():
            dg_ref[...] = jnp.zeros_like(dg_ref)

        dg_ref[0:1, :] += dgq
        dg_ref[1:2, :] += dgk + dgkc

    col = lambda kk: pl.BlockSpec((T, HEAD), lambda h: (0, ob + kk * nh + h))
    vec = pl.BlockSpec((1, HEAD), lambda h: (0, 0))
    tab = pl.BlockSpec((L, HEAD), lambda h: (0, 0))
    tcol = pl.BlockSpec((T, HEAD), lambda h: (0, h))
    bspec = pl.BlockSpec((1,) + bias.shape[1:], lambda h: (h, 0, 0, 0))
    return _pcall(
        body, name="na_bwd", grid=(nh,),
        in_specs=[col(0), col(1), col(2), bspec, vec, vec, tab, tab, pl.BlockSpec((L, HEAD), lambda h: (0, h))],
        out_specs=[tcol, tcol, tcol, bspec, pl.BlockSpec((8, HEAD), lambda h: (0, 0))],
        out_shape=[jax.ShapeDtypeStruct((T, HB), BF16)] * 3 + [jax.ShapeDtypeStruct(bias.shape, F32),
                                                               jax.ShapeDtypeStruct((8, HEAD), F32)],
        scratch_shapes=[pltpu.VMEM((L, HEAD), BF16), pltpu.VMEM((T, HEAD), BF16), pltpu.VMEM((T, HEAD), BF16),
                        pltpu.VMEM((L, HEAD), F32), pltpu.VMEM((T, HEAD), F32), pltpu.VMEM((T, HEAD), F32)],
        compiler_params=_params(("arbitrary",)),
    )(p, p, p, bias, gq, gk, cos, sin, dyb)


def _bias_tables():
    w = np.arange(GRID_W)
    col_start = np.clip(w - WIN_C // 2, 0, GRID_W - WIN_C)
    col_in = (w[None, :] >= col_start[:, None]) & (w[None, :] < col_start[:, None] + WIN_C)
    dc = np.clip(w[None, :] - w[:, None], -(WIN_C - 1), WIN_C - 1) + WIN_C - 1
    n_pair = 2 * WIN_R
    ridx = np.zeros((n_pair, GRID_W, 2 * GRID_W), np.int32)
    cidx = np.zeros((n_pair, GRID_W, 2 * GRID_W), np.int32)
    valid = np.zeros((n_pair, GRID_W, 2 * GRID_W), bool)
    for i in range(n_pair):
        for half in range(2):
            row = i + half
            sl = slice(half * GRID_W, (half + 1) * GRID_W)
            ridx[i, :, sl] = min(row, 2 * WIN_R - 2)
            cidx[i, :, sl] = dc
            valid[i, :, sl] = col_in & (row <= 2 * WIN_R - 2)
    return ridx, cidx, valid


def _bias_onehot():
    _, cidx, valid = _bias_tables()
    K = GRID_W * 2 * GRID_W
    oh = np.zeros((K, 128), np.float32)
    neg = np.full((1, K), NEG, np.float32)
    for cq in range(GRID_W):
        for ll in range(2 * GRID_W):
            if valid[0, cq, ll]:
                oh[cq * 2 * GRID_W + ll, (ll // GRID_W) * 64 + cidx[0, cq, ll]] = 1.0
                neg[0, cq * 2 * GRID_W + ll] = 0.0
    return oh, neg


def _expand_bias(table):
    H = table.shape[0]
    n_pair, n_dc = 2 * WIN_R, 2 * WIN_C - 1
    tp = jnp.pad(table, ((0, 0), (0, n_pair + 1 - table.shape[1]), (0, 64 - n_dc)))
    t2 = jnp.concatenate([tp[:, :n_pair], tp[:, 1:n_pair + 1]], axis=-1).reshape(H * n_pair, 128)
    oh, neg = _bias_onehot()

    def body(t_ref, oh_ref, neg_ref, o_ref):
        o_ref[...] = lax.dot_general(t_ref[...], oh_ref[...], (((1,), (1,)), ((), ())), precision=HI,
                                     preferred_element_type=F32) + neg_ref[...]

    out = _pcall(body, name="bias_expand", out_shape=jax.ShapeDtypeStruct((H * n_pair, oh.shape[0]), F32),
                         compiler_params=_params())(t2, jnp.asarray(oh), jnp.asarray(neg))
    return out.reshape(H, n_pair, GRID_W, 2 * GRID_W)


def _bias_grad(dbias):
    H = dbias.shape[0]
    n_pair, n_dc = 2 * WIN_R, 2 * WIN_C - 1
    K = GRID_W * 2 * GRID_W
    oh, _ = _bias_onehot()
    flat = dbias.reshape(H * n_pair, K)

    def body(d_ref, oh_ref, o_ref):
        o_ref[...] = jnp.dot(d_ref[...], oh_ref[...], precision=HI, preferred_element_type=F32)

    g = _pcall(body, name="bias_grad", out_shape=jax.ShapeDtypeStruct((H * n_pair, 128), F32),
                       compiler_params=_params())(flat, jnp.asarray(oh))
    g = g.reshape(H, n_pair, 128)
    left, right = g[:, :, :n_dc], g[:, :, 64:64 + n_dc]
    out = left[:, :n_pair - 1]
    return out.at[:, 1:].add(right[:, :n_pair - 2])


def _rope_tables(L):
    pos = np.arange(L)
    row = (pos // GRID_W).astype(np.float32)
    colp = (pos % GRID_W).astype(np.float32)
    half = HEAD // 2
    nf = half // 2
    inv = (ROPE_THETA ** (-np.arange(nf, dtype=np.float32) / nf)).astype(np.float32)

    def tabs(pv):
        ang = pv[:, None] * inv[None, :]
        c, s = np.cos(ang), np.sin(ang)
        return np.concatenate([c, c], axis=1), np.concatenate([-s, s], axis=1)

    cr, sr = tabs(row)
    cc, sc = tabs(colp)
    return (jnp.asarray(np.concatenate([cr, cc], axis=1), F32), jnp.asarray(np.concatenate([sr, sc], axis=1), F32))


def _adamw(w, g, m, v, name):
    R, C = w.shape
    tr = _row_tile(R, C)
    c1 = 1.0 - ADAM_B1 ** ADAM_STEP
    c2 = 1.0 - ADAM_B2 ** ADAM_STEP

    def body(w_ref, g_ref, m_ref, v_ref, d_ref, mo_ref, vo_ref):
        gv = g_ref[...]
        mn = ADAM_B1 * m_ref[...] + (1.0 - ADAM_B1) * gv
        vn = ADAM_B2 * v_ref[...] + (1.0 - ADAM_B2) * (gv * gv)
        mo_ref[...] = mn
        vo_ref[...] = vn
        d_ref[...] = -ADAM_LR * ((mn / c1) / (jnp.sqrt(vn / c2) + ADAM_EPS) + ADAM_WD * w_ref[...])

    blk = pl.BlockSpec((tr, C), lambda i: (i, 0))
    return _pcall(
        body, name=name, grid=(R // tr,),
        in_specs=[blk] * 4, out_specs=[blk] * 3,
        out_shape=[jax.ShapeDtypeStruct((R, C), F32)] * 3,
        compiler_params=_params(("parallel",)),
    )(w, g, m, v)


PACK_W = 1024


def _pack(parts):
    flat, offs, pos = [], [], 0
    for a in parts:
        n = a.size
        padn = -n % PACK_W
        flat.append(jnp.pad(a.reshape(-1).astype(F32), (0, padn)))
        offs.append((pos, n, a.shape))
        pos += n + padn
    tail = -pos % (8 * PACK_W)
    if tail:
        flat.append(jnp.zeros((tail,), F32))
    return jnp.concatenate(flat).reshape(-1, PACK_W), offs


def _unpack(buf, offs, i):
    pos, n, shape = offs[i]
    return buf.reshape(buf.shape[:-2] + (-1,))[..., pos:pos + n].reshape(buf.shape[:-2] + shape)


def kernel(x, c, ctx, c_ctx, ada_w, ada_b, norm1_g, norm2_g, w_in, hgrn_lb_logits, hgrn_norm_g, na_q_norm_g, na_k_norm_g, na_rel_bias, w_branch_a, w_branch_b, w_out, ffn_w1, ffn_w3, ffn_conv_w, ffn_conv_b, ffn_w2, loss_target, m_c_ctx, m_ada_w, m_ada_b, m_norm1_g, m_norm2_g, m_w_in, m_hgrn_lb_logits, m_hgrn_norm_g, m_na_q_norm_g, m_na_k_norm_g, m_na_rel_bias, m_w_branch_a, m_w_branch_b, m_w_out, m_ffn_w1, m_ffn_w3, m_ffn_conv_w, m_ffn_conv_b, m_ffn_w2, v_c_ctx, v_ada_w, v_ada_b, v_norm1_g, v_norm2_g, v_w_in, v_hgrn_lb_logits, v_hgrn_norm_g, v_na_q_norm_g, v_na_k_norm_g, v_na_rel_bias, v_w_branch_a, v_w_branch_b, v_w_out, v_ffn_w1, v_ffn_w3, v_ffn_conv_w, v_ffn_conv_b, v_ffn_w2):
    weights = dict(c_ctx=c_ctx, ada_w=ada_w, ada_b=ada_b, norm1_g=norm1_g, norm2_g=norm2_g, w_in=w_in,
                   hgrn_lb_logits=hgrn_lb_logits, hgrn_norm_g=hgrn_norm_g, na_q_norm_g=na_q_norm_g,
                   na_k_norm_g=na_k_norm_g, na_rel_bias=na_rel_bias, w_branch_a=w_branch_a, w_branch_b=w_branch_b,
                   w_out=w_out, ffn_w1=ffn_w1, ffn_w3=ffn_w3, ffn_conv_w=ffn_conv_w, ffn_conv_b=ffn_conv_b,
                   ffn_w2=ffn_w2)
    moms = dict(c_ctx=(m_c_ctx, v_c_ctx), ada_w=(m_ada_w, v_ada_w), ada_b=(m_ada_b, v_ada_b),
                norm1_g=(m_norm1_g, v_norm1_g), norm2_g=(m_norm2_g, v_norm2_g), w_in=(m_w_in, v_w_in),
                hgrn_lb_logits=(m_hgrn_lb_logits, v_hgrn_lb_logits), hgrn_norm_g=(m_hgrn_norm_g, v_hgrn_norm_g),
                na_q_norm_g=(m_na_q_norm_g, v_na_q_norm_g), na_k_norm_g=(m_na_k_norm_g, v_na_k_norm_g),
                na_rel_bias=(m_na_rel_bias, v_na_rel_bias), w_branch_a=(m_w_branch_a, v_w_branch_a),
                w_branch_b=(m_w_branch_b, v_w_branch_b), w_out=(m_w_out, v_w_out), ffn_w1=(m_ffn_w1, v_ffn_w1),
                ffn_w3=(m_ffn_w3, v_ffn_w3), ffn_conv_w=(m_ffn_conv_w, v_ffn_conv_w),
                ffn_conv_b=(m_ffn_conv_b, v_ffn_conv_b), ffn_w2=(m_ffn_w2, v_ffn_w2))
    order = list(weights)

    L, D = x.shape[1], x.shape[2]
    N = ctx.shape[1]
    T = N + L
    HA = w_branch_a.shape[1]
    HB = w_branch_b.shape[1]
    F = ffn_conv_b.shape[1]
    IN = 5 * HA + 3 * HB + 2 * D
    n_ada = ada_w.shape[2]
    ix, iy, ic = _pos()
    chip = 2 * ix + iy
    dev = 2 * chip + ic

    _PENDING.clear()
    pk0, offs0 = _pack([c[0], hgrn_lb_logits, ffn_conv_w[0]])
    g0 = _allgather8(pk0, "gather_small0")
    c_all = _unpack(g0, offs0, 0)
    lbl_parts = _unpack(g0, offs0, 1)
    lbl = jnp.concatenate([lbl_parts[2 * j] for j in range(N_CHIP)], axis=-1)
    cw_parts = _unpack(g0, offs0, 2)
    cw = jnp.concatenate([cw_parts[2 * j] for j in range(N_CHIP)], axis=-1)
    cw8 = jnp.pad(cw, ((0, 5), (0, 0)))

    cs = jnp.concatenate([c_all, c_ctx[None, :], jnp.zeros((7, D), F32)], axis=0)
    ada_b_mine = lax.dynamic_slice(ada_b, (0, chip * n_ada), (1, n_ada))
    mod_mine = _ada_fwd(cs, ada_w[0], ada_b_mine)
    gm = _allgather8(mod_mine, "gather_mod")
    mod = jnp.concatenate([gm[2 * j] for j in range(N_CHIP)], axis=-1)
    mod_l = lax.dynamic_slice(mod, (dev, 0), (1, N_MOD * D)).reshape(N_MOD, D)
    mod_c = mod[8].reshape(N_MOD, D)
    sh1, sc1, g1, sh2, sc2, g2 = [mod_l[i:i + 1] for i in range(N_MOD)]
    shift1 = jnp.concatenate([mod_c[0:1], sh1], axis=0)
    scale1 = jnp.concatenate([mod_c[1:2], sc1], axis=0)

    shards = [w_in[0], w_branch_a[0], w_branch_b[0], w_out[0], ffn_w1[0], ffn_w3[0], ffn_w2[0]]
    names = ["w_in", "w_a", "w_b", "w_out", "w1", "w3", "w2"]
    slots = [_cast_bf16_slot(s, "cast_" + nm) for s, nm in zip(shards, names)]
    gat_in = _gather_start("in", slots[0:1], gm)
    gat_mix = _gather_start("mix", slots[1:4])
    gat_ffn = _gather_start("ffn", slots[4:7])

    xall = jnp.concatenate([ctx[0], x[0]], axis=0)
    h_all = _rms1_fwd(xall, norm1_g, shift1, scale1, N)
    gat_in = _gather_mid(gat_in, h_all)
    (Win,) = _gather_finish(gat_in, h_all)
    p = _mm_nn(h_all, Win, F32, "mm_p")
    gat_mix = _gather_mid(gat_mix, p)
    y_a, o_a = _hgrn_fwd(p, lbl, hgrn_norm_g, N, HA)
    Wa, Wb, Wo = _gather_finish(gat_mix, y_a)
    Wo = Wo.reshape(1, D, D)
    gat_ffn = _gather_mid(gat_ffn, y_a)
    bias = _expand_bias(na_rel_bias[0])
    cos, sin = _rope_tables(L)
    off_na = 5 * HA
    y_b = _na_fwd(p, bias, na_q_norm_g, na_k_norm_g, cos, sin, N, off_na, HB)
    za = _mm_nn(y_a, Wa, F32, "mm_za")
    zb = _mm_nn(y_b, Wb, F32, "mm_zb")
    off_ga, off_gb = 5 * HA + 3 * HB, 5 * HA + 3 * HB + D
    z = _merge_fwd(za, zb, p, N, off_ga, off_gb)
    mo = _mm_nn(z, Wo, F32, "mm_mo")
    vec2 = jnp.concatenate([g1, norm2_g, sh2, sc2, jnp.zeros((4, D), F32)], axis=0)
    x_mid, h2 = _resid_rms2_fwd(x[0], mo, vec2)
    W1, W3, W2 = _gather_finish(gat_ffn, h2)
    W2 = W2.reshape(1, F, D)
    u1 = _mm_nn(h2, W1, F32, "mm_u1")
    u3 = _mm_nn(h2, W3, F32, "mm_u3")
    a = _convgate_fwd(u1, u3, cw8, ffn_conv_b)
    f = _mm_nn(a, W2, F32, "mm_f")
    dy, df, s_loss = _loss_head(x_mid, f, g2, loss_target[0])
    loss = lax.psum(s_loss[1, 0], ("x", "y", "c"))
    d_g2 = s_loss[0:1]

    gW2 = _mm_tn(a, df, 1, "mm_gw2").reshape(N_CHIP, F // N_CHIP, D)
    da = _mm_nt(df, W2, F32, "mm_da")
    du1, du3, s_conv = _convgate_bwd(u1, u3, da, cw8, ffn_conv_b)
    gW1 = _mm_tn(h2, du1, N_CHIP, "mm_gw1")
    gW3 = _mm_tn(h2, du3, N_CHIP, "mm_gw3")
    rs_ffn = _rs_start("ffn", [gW2, gW1, gW3])
    dh2a = _mm_nt(du1, W1, F32, "mm_dh2a")
    dh2b = _mm_nt(du3, W3, F32, "mm_dh2b")
    rs_ffn = _rs_scatter(rs_ffn, dh2b)
    dxm, dmo, s_rms2 = _resid_rms2_bwd(x_mid, dh2a, dh2b, dy, mo, vec2)
    gWo = _mm_tn(z, dmo, 1, "mm_gwo").reshape(N_CHIP, D // N_CHIP, D)
    dz = _mm_nt(dmo, Wo, F32, "mm_dz")
    dza, dzb, dga, dgb = _merge_bwd(dz, za, zb, p, N, off_ga, off_gb)
    gWa = _mm_tn(y_a, dza, N_CHIP, "mm_gwa")
    gWb = _mm_tn(y_b, dzb, N_CHIP, "mm_gwb")
    rs_mix = _rs_start("mix", [gWo, gWa, gWb])
    dya = _mm_nt(dza, Wa, F32, "mm_dya")
    dyb = _mm_nt(dzb, Wb, BF16, "mm_dyb")
    rs_mix = _rs_scatter(rs_mix, dyb)
    dq_a, dzf, dzbk, di_a, dog, dlbl, s_ng = _hgrn_bwd(p, lbl, hgrn_norm_g, o_a, dya, N, HA)
    rs_ffn = _rs_join(rs_ffn, dq_a)
    dq_n, dk_n, dv_n, dbias, s_qk = _na_bwd(p, bias, na_q_norm_g, na_k_norm_g, cos, sin, dyb, N, off_na, HB)
    rs_mix = _rs_join(rs_mix, dq_n)
    dp = jnp.concatenate([dq_a, dzf, dzbk, di_a, dog, dq_n, dk_n, dv_n, dga, dgb], axis=1)
    gWin = _mm_tn(h_all, dp, N_CHIP, "mm_gwin")
    rs_in = _rs_start("in", [gWin])
    dh = _mm_nt(dp, Win, F32, "mm_dh")
    grad_x, s_rms1 = _rms1_bwd(xall, dh, dxm, norm1_g, scale1, N)
    d_table = _bias_grad(dbias)

    zD = jnp.zeros((1, D), F32)
    dmod_l = jnp.concatenate([s_rms1[2:3], s_rms1[3:4], s_rms2[3:4], s_rms2[0:1], s_rms2[1:2], d_g2], axis=0)
    dmod_c = jnp.concatenate([s_rms1[0:1], s_rms1[1:2], zD, zD, zD, zD], axis=0)
    pk1, offs1 = _pack([dmod_l, dmod_c, s_rms1[4], s_rms2[2], dlbl, s_ng[0], s_qk[0], s_qk[1], d_table,
                        s_conv[0:3], s_conv[3]])
    g1all = _allgather8(pk1, "gather_small1")
    tot1 = _sum8(g1all, "sum_small1")
    dmod_rows = _unpack(g1all, offs1, 0).reshape(N_DEV, N_MOD * D)
    dmod_c_tot = _unpack(tot1, offs1, 1).reshape(1, N_MOD * D)
    dmod16 = jnp.concatenate([dmod_rows, dmod_c_tot, jnp.zeros((7, N_MOD * D), F32)], axis=0)
    dmod16_mine = lax.dynamic_slice(dmod16, (0, chip * n_ada), (16, n_ada))
    g_ada_w, dact = _ada_bwd(cs, ada_w[0], dmod16_mine)
    pk2, offs2 = _pack([dact[8]])
    g2all = _allgather8(pk2, "gather_small2")
    dact_rows = _unpack(g2all, offs2, 0)
    dact_sel = jnp.concatenate([dact_rows[2 * j][None] for j in range(N_CHIP)] + [jnp.zeros((4, D), F32)], axis=0)

    grads = {}
    grads["ada_w"] = g_ada_w[None]
    grads["ada_b"] = (_unpack(tot1, offs1, 0) + _unpack(tot1, offs1, 1)).reshape(1, N_MOD * D)
    grads["norm1_g"] = _unpack(tot1, offs1, 2)[None]
    grads["norm2_g"] = _unpack(tot1, offs1, 3)[None]
    g_lbl = _unpack(tot1, offs1, 4)
    n_lb = HA // N_CHIP
    grads["hgrn_lb_logits"] = lax.dynamic_slice(g_lbl, (0, 0, chip * n_lb), (2, 2, n_lb))
    grads["hgrn_norm_g"] = _unpack(tot1, offs1, 5)[None]
    grads["na_q_norm_g"] = _unpack(tot1, offs1, 6)[None]
    grads["na_k_norm_g"] = _unpack(tot1, offs1, 7)[None]
    grads["na_rel_bias"] = _unpack(tot1, offs1, 8)[None]
    g_cw = _unpack(tot1, offs1, 9)
    n_f = F // N_CHIP
    grads["ffn_conv_w"] = lax.dynamic_slice(g_cw, (0, chip * n_f), (3, n_f))[None]
    grads["ffn_conv_b"] = _unpack(tot1, offs1, 10)[None]

    g_c_ctx = _dsilu_rows(dact_sel, c_ctx[None, :], "grad_c_ctx")
    grads["c_ctx"] = g_c_ctx[0]

    rs_in = _rs_scatter(rs_in, g_c_ctx)
    big_names = ["ada_w", "w_in", "w_branch_a", "w_branch_b", "w_out", "ffn_w1", "ffn_w3", "ffn_w2"]
    small_names = [n for n in order if n not in big_names]
    delta, new_m, new_v = {}, {}, {}

    def update(nm):
        d_, m_, v_ = _adamw(weights[nm][0], grads[nm][0], moms[nm][0][0], moms[nm][1][0], "adamw_" + nm)
        delta[nm], new_m[nm], new_v[nm] = d_[None], m_[None], v_[None]
        return d_

    last = update("ada_w")
    for nm, g in zip(["ffn_w2", "ffn_w1", "ffn_w3"], _rs_finish(rs_ffn, last)):
        grads[nm] = g[None]
        last = update(nm)
    rs_in = _rs_join(rs_in, last)
    for nm, g in zip(["w_out", "w_branch_a", "w_branch_b"], _rs_finish(rs_mix, last)):
        grads[nm] = g[None]
        last = update(nm)
    grads["w_in"] = _rs_finish(rs_in, last)[0][None]
    update("w_in")
    pw, offw = _pack([weights[n] for n in small_names])
    pg, _ = _pack([grads[n] for n in small_names])
    pm, _ = _pack([moms[n][0] for n in small_names])
    pv, _ = _pack([moms[n][1] for n in small_names])
    d_, m_, v_ = _adamw(pw, pg, pm, pv, "adamw_small")
    for i, nm in enumerate(small_names):
        delta[nm], new_m[nm], new_v[nm] = _unpack(d_, offw, i), _unpack(m_, offw, i), _unpack(v_, offw, i)

    return (loss, grad_x[None], *[grads[n] for n in order], *[delta[n] for n in order],
            *[new_m[n] for n in order], *[new_v[n] for n in order])


def _dsilu_rows(v, cv, name):
    D = v.shape[1]

    def body(v_ref, c_ref, o_ref):
        t = c_ref[...]
        s = _sigmoid(t)
        o_ref[...] = (((v_ref[0:1, :] + v_ref[1:2, :]) + v_ref[2:3, :]) + v_ref[3:4, :]) * (s * (1.0 + t * (1.0 - s)))

    return _pcall(body, name=name, out_shape=jax.ShapeDtypeStruct((1, D), F32),
                          compiler_params=_params())(v, cv)
```

```python
import functools

import numpy as np
import jax
import jax.numpy as jnp
from jax import lax
from jax.experimental import pallas as pl
from jax.experimental.pallas import tpu as pltpu

F32 = jnp.float32
BF16 = jnp.bfloat16
MESH = pl.DeviceIdType.MESH

HEAD = 128
GRID_W = 64
WIN_R = 8
WIN_C = 16
ROPE_THETA = 10000.0
EPS = 1e-6
N_MOD = 6
CHUNK = 16
ADAM_LR = 0.001
ADAM_B1 = 0.9
ADAM_B2 = 0.999
ADAM_EPS = 1e-08
ADAM_WD = 0.01
ADAM_STEP = 10
NEG = -1e30
VMEM_LIMIT = 56 * 1024 * 1024
N_DEV = 8
N_CHIP = 4
HI = lax.Precision.HIGHEST


def _pick(n, cands):
    for c in cands:
        if n % c == 0:
            return c
    return n


def _row_tile(rows, cols, target_bytes=1 << 20):
    want = max(16, target_bytes // (4 * cols))
    for t in (512, 256, 128, 64, 32, 16, 8):
        if t <= want and rows % t == 0:
            return t
    return rows


def _params(sem=None):
    return pltpu.CompilerParams(dimension_semantics=sem, vmem_limit_bytes=VMEM_LIMIT)


def _dot(a, b):
    return jnp.dot(a, b, preferred_element_type=F32)


def _dot_nt(a, b):
    return lax.dot_general(a, b, (((1,), (1,)), ((), ())), preferred_element_type=F32)


def _dot_tn(a, b):
    return lax.dot_general(a, b, (((0,), (0,)), ((), ())), preferred_element_type=F32)


def _sigmoid(x):
    return 1.0 / (1.0 + jnp.exp(-x))


def _col_tile(n):
    return n if n <= 1536 else _pick(n, (1024, 768, 512, 384, 256, 128))


def _mm_nn(x, w3, out_dtype, name):
    M, K = x.shape
    S, _, n = w3.shape
    tm = _pick(M, (768, 512, 256, 128, 64))
    tn = _col_tile(n)
    nb = n // tn

    def body(x_ref, w_ref, o_ref):
        o_ref[...] = _dot(x_ref[...].astype(BF16), w_ref[0]).astype(o_ref.dtype)

    return _pcall(
        body, name=name, grid=(M // tm, S * nb),
        in_specs=[pl.BlockSpec((tm, K), lambda i, j: (i, 0)),
                  pl.BlockSpec((1, K, tn), lambda i, j: (j // nb, 0, j % nb))],
        out_specs=pl.BlockSpec((tm, tn), lambda i, j: (i, j)),
        out_shape=jax.ShapeDtypeStruct((M, S * n), out_dtype),
        compiler_params=_params(("parallel", "parallel")),
    )(x, w3)


def _mm_nt(dy, w3, out_dtype, name):
    M = dy.shape[0]
    S, K, n = w3.shape
    tm = _pick(M, (768, 512, 256, 128, 64))
    tk = K if K <= 2048 else _pick(K, (1408, 1024, 512, 256, 128))
    tc = _col_tile(n)
    nb = n // tc
    nsteps = S * nb

    def body(dy_ref, w_ref, o_ref, acc_ref):
        s = pl.program_id(2)

        @pl.when(s == 0)
        def _():
            acc_ref[...] = jnp.zeros_like(acc_ref)

        acc_ref[...] += _dot_nt(dy_ref[...].astype(BF16), w_ref[0])

        @pl.when(s == nsteps - 1)
        def _():
            o_ref[...] = acc_ref[...].astype(o_ref.dtype)

    return _pcall(
        body, name=name, grid=(M // tm, K // tk, nsteps),
        in_specs=[pl.BlockSpec((tm, tc), lambda i, k, s: (i, s)),
                  pl.BlockSpec((1, tk, tc), lambda i, k, s: (s // nb, k, s % nb))],
        out_specs=pl.BlockSpec((tm, tk), lambda i, k, s: (i, k)),
        out_shape=jax.ShapeDtypeStruct((M, K), out_dtype),
        scratch_shapes=[pltpu.VMEM((tm, tk), F32)],
        compiler_params=_params(("parallel", "parallel", "arbitrary")),
    )(dy, w3)


def _mm_tn(x, dy, S, name):
    M, K = x.shape
    n = dy.shape[1] // S
    tk = _pick(K, (512, 256, 128))
    tn = _col_tile(n)
    nb = n // tn

    def body(x_ref, dy_ref, o_ref):
        o_ref[0] = _dot_tn(x_ref[...].astype(BF16), dy_ref[...].astype(BF16))

    return _pcall(
        body, name=name, grid=(S * nb, K // tk),
        in_specs=[pl.BlockSpec((M, tk), lambda j, k: (0, k)),
                  pl.BlockSpec((M, tn), lambda j, k: (0, j))],
        out_specs=pl.BlockSpec((1, tk, tn), lambda j, k: (j // nb, k, j % nb)),
        out_shape=jax.ShapeDtypeStruct((S, K, n), F32),
        compiler_params=_params(("parallel", "parallel")),
    )(x, dy)


def _chip_index():
    return (2 * lax.axis_index("x") + lax.axis_index("y")).astype(jnp.int32).reshape(1)


def _cast_bf16_slot(w, name):
    R, C = w.shape
    tr = _row_tile(R, C, 2 << 20)

    def body(j_ref, w_ref, o_ref):
        o_ref[0] = w_ref[...].astype(BF16)

    return _pcall(
        body, name=name,
        grid_spec=pltpu.PrefetchScalarGridSpec(
            num_scalar_prefetch=1, grid=(R // tr,),
            in_specs=[pl.BlockSpec((tr, C), lambda i, j_ref: (i, 0))],
            out_specs=pl.BlockSpec((1, tr, C), lambda i, j_ref: (j_ref[0], i, 0))),
        out_shape=jax.ShapeDtypeStruct((N_CHIP, R, C), BF16),
        compiler_params=_params(("parallel",)),
    )(_chip_index(), w)


def _pos():
    return lax.axis_index("x"), lax.axis_index("y"), lax.axis_index("c")


def _other_chips(x, y):
    return [(x, 1 - y), (1 - x, y), (1 - x, 1 - y)]


def _allgather8(v, name):
    R, C = v.shape

    def body(x_ref, out_ref, send_sems, recv_sems, local_sem):
        x, y, c = _pos()
        me, sibling = (x, y, c), (x, y, 1 - c)
        chips = _other_chips(x, y)

        def slot(px, py, pc):
            return out_ref.at[4 * px + 2 * py + pc]

        def copy(k, block, to, src=None):
            return pltpu.make_async_remote_copy(
                src_ref=slot(*block) if src is None else src, dst_ref=slot(*block),
                send_sem=send_sems.at[k], recv_sem=recv_sems.at[k], device_id=to, device_id_type=MESH)

        mine = pltpu.make_async_copy(x_ref, slot(*me), local_sem)
        mine.start()
        first = [copy(0, me, sibling, src=x_ref)]
        first += [copy(1 + j, me, (*chip, c), src=x_ref) for j, chip in enumerate(chips)]
        for cp in first:
            cp.start()
        passed = [copy(4 + j, (*chip, c), sibling) for j, chip in enumerate(chips)]
        for j, chip in enumerate(chips):
            copy(1 + j, (*chip, c), me).wait_recv()
            passed[j].start()
        copy(0, sibling, me).wait_recv()
        for j, chip in enumerate(chips):
            copy(4 + j, (*chip, 1 - c), me).wait_recv()
        for cp in first + passed:
            cp.wait_send()
        mine.wait()

    return _pcall(
        body, name=name,
        out_shape=jax.ShapeDtypeStruct((N_DEV, R, C), v.dtype),
        in_specs=[pl.BlockSpec(memory_space=pltpu.VMEM)],
        out_specs=pl.BlockSpec(memory_space=pltpu.VMEM),
        scratch_shapes=[pltpu.SemaphoreType.DMA((7,)), pltpu.SemaphoreType.DMA((7,)), pltpu.SemaphoreType.DMA],
        compiler_params=pltpu.CompilerParams(vmem_limit_bytes=VMEM_LIMIT),
    )(v)


_HBM = pl.BlockSpec(memory_space=pltpu.HBM)
_SEM = pl.BlockSpec(memory_space=pltpu.SEMAPHORE)
_ANY = pl.BlockSpec(memory_space=pl.ANY)
_EFFECT = pltpu.SideEffectType.DATAFLOW_SIDE_EFFECTING
_PENDING = []


def _pcall(body, **kw):
    def run(*operands):
        if not _PENDING or "in_specs" not in kw:
            return pl.pallas_call(body, **kw)(*operands)
        deps = list(_PENDING)
        n = len(operands)

        def tied(*refs):
            return body(*refs[:n], *refs[n + len(deps):])

        return pl.pallas_call(tied, **{**kw, "in_specs": list(kw["in_specs"]) + [_ANY] * len(deps)})(*operands, *deps)
    return run


def _copies(plan, refs, send_sems, recv_sems):
    return [pltpu.make_async_remote_copy(src_ref=src, dst_ref=dst, send_sem=send_sems.at[k], recv_sem=recv_sems.at[k],
                                         device_id=dev, device_id_type=MESH)
            for k, (src, dst, dev) in enumerate(plan(refs))]


def _xfer_start(name, bufs, plan, n_copies, after=None):
    n = len(bufs)
    deps = list(_PENDING) + ([after] if after is not None else [])
    nd = len(deps)

    def body(*refs):
        for cp in _copies(plan, refs[:n], refs[n + nd], refs[n + nd + 1]):
            cp.start()
        refs[-1][...] = jnp.zeros_like(refs[-1])

    outs = pl.pallas_call(
        body, name=name,
        out_shape=(pltpu.SemaphoreType.DMA((n_copies,)), pltpu.SemaphoreType.DMA((n_copies,)),
                   *[pltpu.HBM(b.shape, b.dtype) for b in bufs], jax.ShapeDtypeStruct((8, 128), F32)),
        in_specs=[_HBM] * n + [_ANY] * nd,
        out_specs=(_SEM, _SEM, *[_HBM] * n, pl.BlockSpec(memory_space=pltpu.VMEM)),
        input_output_aliases={t: 2 + t for t in range(n)},
        compiler_params=pltpu.CompilerParams(has_side_effects=_EFFECT),
    )(*[pltpu.with_memory_space_constraint(b, pltpu.HBM) for b in bufs], *deps)
    _PENDING[:] = [outs[-1]]
    return (outs[0], outs[1]), list(outs[2:2 + n])


def _xfer_wait(name, sems, bufs, plan, after):
    n = len(bufs)

    def body(*refs):
        cps = _copies(plan, refs[:n], refs[n], refs[n + 1])
        for cp in cps:
            cp.wait_send()
        for cp in cps:
            cp.wait_recv()

    outs = pl.pallas_call(
        body, name=name,
        out_shape=tuple(pltpu.HBM(b.shape, b.dtype) for b in bufs),
        in_specs=[_HBM] * n + [_SEM, _SEM, _ANY],
        out_specs=tuple([_HBM] * n),
        input_output_aliases={t: t for t in range(n)},
        compiler_params=pltpu.CompilerParams(has_side_effects=_EFFECT),
    )(*bufs, sems[0], sems[1], after)
    return list(outs)


def _half(ref_rows, hc):
    h = ref_rows // 2
    return pl.ds(hc * h, h)


def _plan_gather_ici(bufs):
    x, y, c = _pos()
    j = 2 * x + y
    return [(b.at[j, _half(b.shape[1], c)], b.at[j, _half(b.shape[1], c)], (*chip, c))
            for b in bufs for chip in _other_chips(x, y)]


def _plan_gather_d2d(bufs):
    x, y, c = _pos()
    out = []
    for b in bufs:
        for chip in _other_chips(x, y):
            blk = b.at[2 * chip[0] + chip[1], _half(b.shape[1], c)]
            out.append((blk, blk, (x, y, 1 - c)))
    return out


def _plan_pair_swap(n):
    def plan(bufs):
        x, y, c = _pos()
        return [(g.at[:, _half(g.shape[1], 1 - c)], land, (x, y, 1 - c)) for g, land in zip(bufs[:n], bufs[n:])]
    return plan


def _plan_chip_scatter(n):
    def plan(bufs):
        x, y, c = _pos()
        return [(p.at[2 * chip[0] + chip[1]], land.at[k], (*chip, c))
                for p, land in zip(bufs[:n], bufs[n:]) for k, chip in enumerate(_other_chips(x, y))]
    return plan


def _plan_pair_join(bufs):
    x, y, c = _pos()
    return [(b.at[_half(b.shape[0], c)], b.at[_half(b.shape[0], c)], (x, y, 1 - c)) for b in bufs]


def _empty_hbm(shape, dtype):
    return pltpu.with_memory_space_constraint(lax.empty(shape, dtype), pltpu.HBM)


def _gather_start(tag, bufs, after=None):
    sems, bufs = _xfer_start(f"gather_ici_start_{tag}", bufs, _plan_gather_ici, 3 * len(bufs), after)
    return dict(tag=tag, sems=sems, bufs=bufs)


def _gather_mid(st, after):
    tag = st["tag"]
    bufs = _xfer_wait(f"gather_ici_wait_{tag}", st["sems"], st["bufs"], _plan_gather_ici, after)
    sems, bufs = _xfer_start(f"gather_d2d_start_{tag}", bufs, _plan_gather_d2d, 3 * len(bufs))
    return dict(tag=tag, sems=sems, bufs=bufs)


def _gather_finish(st, after):
    return _xfer_wait(f"gather_d2d_wait_{st['tag']}", st["sems"], st["bufs"], _plan_gather_d2d, after)


def _pair_add(g, r, name):
    S, R, C = g.shape
    h = R // 2
    tr = _row_tile(h, C)
    nb = h // tr

    def body(c_ref, g_ref, r_ref, o_ref):
        o_ref[...] = (g_ref[...] + r_ref[...]).astype(BF16)

    return _pcall(
        body, name=name,
        grid_spec=pltpu.PrefetchScalarGridSpec(
            num_scalar_prefetch=1, grid=(S, nb),
            in_specs=[pl.BlockSpec((1, tr, C), lambda s, i, c_ref: (s, c_ref[0] * nb + i, 0)),
                      pl.BlockSpec((1, tr, C), lambda s, i, c_ref: (s, i, 0))],
            out_specs=pl.BlockSpec((1, tr, C), lambda s, i, c_ref: (s, i, 0))),
        out_shape=jax.ShapeDtypeStruct((S, h, C), BF16),
        compiler_params=_params(("parallel", "parallel")),
    )(lax.axis_index("c").astype(jnp.int32).reshape(1), g, r)


def _chip_sum(p, rb, name):
    S, h, C = p.shape
    tr = _row_tile(h, C)
    nb = h // tr
    jc = jnp.concatenate([_chip_index(), lax.axis_index("c").astype(jnp.int32).reshape(1)])

    def body(jc_ref, p_ref, r_ref, o_ref):
        o_ref[...] = ((p_ref[0].astype(F32) + r_ref[0].astype(F32)) + r_ref[1].astype(F32)) + r_ref[2].astype(F32)

    return _pcall(
        body, name=name,
        grid_spec=pltpu.PrefetchScalarGridSpec(
            num_scalar_prefetch=1, grid=(nb,),
            in_specs=[pl.BlockSpec((1, tr, C), lambda i, jc_ref: (jc_ref[0], i, 0)),
                      pl.BlockSpec((3, tr, C), lambda i, jc_ref: (0, i, 0))],
            out_specs=pl.BlockSpec((tr, C), lambda i, jc_ref: (jc_ref[1] * nb + i, 0))),
        out_shape=jax.ShapeDtypeStruct((2 * h, C), F32),
        compiler_params=_params(("parallel",)),
    )(jc, p, rb)


def _rs_start(tag, gs):
    n = len(gs)
    lands = [_empty_hbm((g.shape[0], g.shape[1] // 2, g.shape[2]), g.dtype) for g in gs]
    sems, bufs = _xfer_start(f"rs_swap_start_{tag}", list(gs) + lands, _plan_pair_swap(n), n)
    return dict(tag=tag, n=n, sems=sems, bufs=bufs)


def _rs_scatter(st, after):
    tag, n = st["tag"], st["n"]
    bufs = _xfer_wait(f"rs_swap_wait_{tag}", st["sems"], st["bufs"], _plan_pair_swap(n), after)
    ps = [_pair_add(g, r, f"rs_pair_add_{tag}{t}") for t, (g, r) in enumerate(zip(bufs[:n], bufs[n:]))]
    lands = [_empty_hbm((3,) + p.shape[1:], p.dtype) for p in ps]
    sems, bufs = _xfer_start(f"rs_scatter_start_{tag}", ps + lands, _plan_chip_scatter(n), 3 * n)
    return dict(tag=tag, n=n, sems=sems, bufs=bufs)


def _rs_join(st, after):
    tag, n = st["tag"], st["n"]
    bufs = _xfer_wait(f"rs_scatter_wait_{tag}", st["sems"], st["bufs"], _plan_chip_scatter(n), after)
    fs = [_chip_sum(p, rb, f"rs_chip_sum_{tag}{t}") for t, (p, rb) in enumerate(zip(bufs[:n], bufs[n:]))]
    sems, bufs = _xfer_start(f"rs_join_start_{tag}", fs, _plan_pair_join, n)
    return dict(tag=tag, n=n, sems=sems, bufs=bufs)


def _rs_finish(st, after):
    return _xfer_wait(f"rs_join_wait_{st['tag']}", st["sems"], st["bufs"], _plan_pair_join, after)


def _sum8(g, name):
    _, R, C = g.shape

    def body(g_ref, o_ref):
        acc = g_ref[0]
        for d in range(1, N_DEV):
            acc = acc + g_ref[d]
        o_ref[...] = acc

    return _pcall(body, name=name, out_shape=jax.ShapeDtypeStruct((R, C), F32),
                          compiler_params=_params())(g)


def _ada_fwd(cs, w, b):
    D, n = w.shape
    tn = _pick(n, (512, 384, 256, 128))

    def body(c_ref, w_ref, b_ref, o_ref):
        cv = c_ref[...]
        a = (cv * _sigmoid(cv)).astype(BF16)
        o_ref[...] = _dot(a, w_ref[...].astype(BF16)) + b_ref[...]

    return _pcall(
        body, name="ada_fwd", grid=(n // tn,),
        in_specs=[pl.BlockSpec((16, D), lambda j: (0, 0)), pl.BlockSpec((D, tn), lambda j: (0, j)),
                  pl.BlockSpec((1, tn), lambda j: (0, j))],
        out_specs=pl.BlockSpec((16, tn), lambda j: (0, j)),
        out_shape=jax.ShapeDtypeStruct((16, n), F32),
        compiler_params=_params(("parallel",)),
    )(cs, w, b)


def _ada_bwd(cs, w, dmod):
    D, n = w.shape
    tn = _pick(n, (512, 384, 256, 128))

    def body(c_ref, w_ref, d_ref, gw_ref, da_ref):
        j = pl.program_id(0)
        cv = c_ref[...]
        a = cv * _sigmoid(cv)
        d = d_ref[...]
        gw_ref[...] = lax.dot_general(a, d, (((0,), (0,)), ((), ())), precision=HI, preferred_element_type=F32)

        @pl.when(j == 0)
        def _():
            da_ref[...] = jnp.zeros_like(da_ref)

        da_ref[...] += _dot_nt(d.astype(BF16), w_ref[...].astype(BF16))

    return _pcall(
        body, name="ada_bwd", grid=(n // tn,),
        in_specs=[pl.BlockSpec((16, D), lambda j: (0, 0)), pl.BlockSpec((D, tn), lambda j: (0, j)),
                  pl.BlockSpec((16, tn), lambda j: (0, j))],
        out_specs=[pl.BlockSpec((D, tn), lambda j: (0, j)), pl.BlockSpec((16, D), lambda j: (0, 0))],
        out_shape=[jax.ShapeDtypeStruct((D, n), F32), jax.ShapeDtypeStruct((16, D), F32)],
        compiler_params=_params(("arbitrary",)),
    )(cs, w, dmod)


def _rms1_fwd(xall, gain, shift2, scale2, n_ctx):
    T, D = xall.shape
    tb = _pick(n_ctx, (256, 128, 64, 32, 16))
    nctx = n_ctx // tb

    def body(x_ref, g_ref, sh_ref, sc_ref, o_ref):
        i = pl.program_id(0)
        xv = x_ref[...]
        r = lax.rsqrt(jnp.mean(xv * xv, axis=-1, keepdims=True) + EPS)
        nrm = xv * r * g_ref[...]
        lat = i >= nctx
        sh = jnp.where(lat, sh_ref[1:2, :], sh_ref[0:1, :])
        sc = jnp.where(lat, sc_ref[1:2, :], sc_ref[0:1, :])
        o_ref[...] = (nrm * (1.0 + sc) + sh).astype(BF16)

    vec = lambda r: pl.BlockSpec((r, D), lambda i: (0, 0))
    return _pcall(
        body, name="rms1_fwd", grid=(T // tb,),
        in_specs=[pl.BlockSpec((tb, D), lambda i: (i, 0)), vec(1), vec(2), vec(2)],
        out_specs=pl.BlockSpec((tb, D), lambda i: (i, 0)),
        out_shape=jax.ShapeDtypeStruct((T, D), BF16),
        compiler_params=_params(("parallel",)),
    )(xall, gain, shift2, scale2)


def _rms1_bwd(xall, dh, dxmid, gain, scale2, n_ctx):
    T, D = xall.shape
    L = T - n_ctx
    tb = _pick(n_ctx, (256, 128, 64, 32, 16))
    nctx = n_ctx // tb

    def body(x_ref, dh_ref, dxm_ref, g_ref, sc_ref, dx_ref, cs_ref):
        i = pl.program_id(0)
        lat = i >= nctx
        xv = x_ref[...]
        r = lax.rsqrt(jnp.mean(xv * xv, axis=-1, keepdims=True) + EPS)
        xh = xv * r
        g = g_ref[...]
        nrm = xh * g
        sc = jnp.where(lat, sc_ref[1:2, :], sc_ref[0:1, :])
        dhv = dh_ref[...]
        dn = dhv * (1.0 + sc)
        dxh = dn * g
        dxv = r * (dxh - xh * jnp.mean(dxh * xh, axis=-1, keepdims=True))
        s_sh = jnp.sum(dhv, axis=0, keepdims=True)
        s_sc = jnp.sum(dhv * nrm, axis=0, keepdims=True)
        s_g = jnp.sum(dn * xh, axis=0, keepdims=True)
        zero = jnp.zeros_like(s_sh)
        rows = lax.broadcasted_iota(jnp.int32, (8, D), 0)
        upd = jnp.where(rows == 0, jnp.where(lat, zero, s_sh),
              jnp.where(rows == 1, jnp.where(lat, zero, s_sc),
              jnp.where(rows == 2, jnp.where(lat, s_sh, zero),
              jnp.where(rows == 3, jnp.where(lat, s_sc, zero),
              jnp.where(rows == 4, s_g, 0.0)))))

        @pl.when(i == 0)
        def _():
            cs_ref[...] = jnp.zeros_like(cs_ref)

        cs_ref[...] += upd

        @pl.when(lat)
        def _():
            dx_ref[...] = dxv + dxm_ref[...]

    lat_blk = lambda i: (jnp.maximum(i - nctx, 0), 0)
    vec = lambda r: pl.BlockSpec((r, D), lambda i: (0, 0))
    return _pcall(
        body, name="rms1_bwd", grid=(T // tb,),
        in_specs=[pl.BlockSpec((tb, D), lambda i: (i, 0)), pl.BlockSpec((tb, D), lambda i: (i, 0)),
                  pl.BlockSpec((tb, D), lat_blk), vec(1), vec(2)],
        out_specs=[pl.BlockSpec((tb, D), lat_blk), vec(8)],
        out_shape=[jax.ShapeDtypeStruct((L, D), F32), jax.ShapeDtypeStruct((8, D), F32)],
        compiler_params=_params(("arbitrary",)),
    )(xall, dh, dxmid, gain, scale2)


def _resid_rms2_fwd(x, mo, vecs):
    L, D = x.shape
    tb = _pick(L, (256, 128, 64))

    def body(x_ref, mo_ref, v_ref, xm_ref, h_ref):
        xm = x_ref[...] + v_ref[0:1, :] * mo_ref[...]
        xm_ref[...] = xm
        r = lax.rsqrt(jnp.mean(xm * xm, axis=-1, keepdims=True) + EPS)
        h_ref[...] = (xm * r * v_ref[1:2, :] * (1.0 + v_ref[3:4, :]) + v_ref[2:3, :]).astype(BF16)

    blk = pl.BlockSpec((tb, D), lambda i: (i, 0))
    return _pcall(
        body, name="resid_rms2_fwd", grid=(L // tb,),
        in_specs=[blk, blk, pl.BlockSpec((8, D), lambda i: (0, 0))],
        out_specs=[blk, blk],
        out_shape=[jax.ShapeDtypeStruct((L, D), F32), jax.ShapeDtypeStruct((L, D), BF16)],
        compiler_params=_params(("parallel",)),
    )(x, mo, vecs)


def _resid_rms2_bwd(xmid, dh_a, dh_b, dy, mo, vecs):
    L, D = xmid.shape
    tb = _pick(L, (256, 128, 64))

    def body(xm_ref, da_ref, db_ref, dy_ref, mo_ref, v_ref, dxm_ref, dmo_ref, cs_ref):
        i = pl.program_id(0)
        xm = xm_ref[...]
        r = lax.rsqrt(jnp.mean(xm * xm, axis=-1, keepdims=True) + EPS)
        xh = xm * r
        g = v_ref[1:2, :]
        nrm = xh * g
        dhv = da_ref[...] + db_ref[...]
        dn = dhv * (1.0 + v_ref[3:4, :])
        dxh = dn * g
        dxm = dy_ref[...] + r * (dxh - xh * jnp.mean(dxh * xh, axis=-1, keepdims=True))
        dxm_ref[...] = dxm
        dmo_ref[...] = (dxm * v_ref[0:1, :]).astype(BF16)
        s0 = jnp.sum(dhv, axis=0, keepdims=True)
        s1 = jnp.sum(dhv * nrm, axis=0, keepdims=True)
        s2 = jnp.sum(dn * xh, axis=0, keepdims=True)
        s3 = jnp.sum(dxm * mo_ref[...], axis=0, keepdims=True)
        rows = lax.broadcasted_iota(jnp.int32, (8, D), 0)
        upd = jnp.where(rows == 0, s0, jnp.where(rows == 1, s1, jnp.where(rows == 2, s2,
              jnp.where(rows == 3, s3, 0.0))))

        @pl.when(i == 0)
        def _():
            cs_ref[...] = jnp.zeros_like(cs_ref)

        cs_ref[...] += upd

    blk = pl.BlockSpec((tb, D), lambda i: (i, 0))
    vec = pl.BlockSpec((8, D), lambda i: (0, 0))
    return _pcall(
        body, name="resid_rms2_bwd", grid=(L // tb,),
        in_specs=[blk, blk, blk, blk, blk, vec],
        out_specs=[blk, blk, vec],
        out_shape=[jax.ShapeDtypeStruct((L, D), F32), jax.ShapeDtypeStruct((L, D), BF16),
                   jax.ShapeDtypeStruct((8, D), F32)],
        compiler_params=_params(("arbitrary",)),
    )(xmid, dh_a, dh_b, dy, mo, vecs)


def _loss_head(xmid, f, g2, target):
    L, D = xmid.shape
    tb = _pick(L, (256, 128, 64))

    def body(xm_ref, f_ref, g_ref, t_ref, dy_ref, df_ref, s_ref):
        i = pl.program_id(0)
        fv = f_ref[...]
        g = g_ref[...]
        err = xm_ref[...] + g * fv - t_ref[...]
        dy = err * (1.0 / D)
        dy_ref[...] = dy
        df_ref[...] = (dy * g).astype(BF16)
        s0 = jnp.sum(dy * fv, axis=0, keepdims=True)
        part = 0.5 * jnp.sum(jnp.mean(err * err, axis=-1, keepdims=True), axis=0, keepdims=True)
        rows = lax.broadcasted_iota(jnp.int32, (8, D), 0)
        upd = jnp.where(rows == 0, s0, jnp.where(rows == 1, part, 0.0))

        @pl.when(i == 0)
        def _():
            s_ref[...] = jnp.zeros_like(s_ref)

        s_ref[...] += upd

    blk = pl.BlockSpec((tb, D), lambda i: (i, 0))
    return _pcall(
        body, name="loss_head", grid=(L // tb,),
        in_specs=[blk, blk, pl.BlockSpec((1, D), lambda i: (0, 0)), blk],
        out_specs=[blk, blk, pl.BlockSpec((8, D), lambda i: (0, 0))],
        out_shape=[jax.ShapeDtypeStruct((L, D), F32), jax.ShapeDtypeStruct((L, D), BF16),
                   jax.ShapeDtypeStruct((8, D), F32)],
        compiler_params=_params(("arbitrary",)),
    )(xmid, f, g2, target)


def _gate_cols(D, off):
    tc = _pick(np.gcd(D, off), (512, 256, 128))
    return tc, off // tc


def _merge_fwd(za, zb, p, n_ctx, off_a, off_b):
    L, D = za.shape
    tb = _pick(n_ctx, (256, 128, 64, 32, 16))
    nctx = n_ctx // tb
    tc, oa = _gate_cols(D, off_a)
    _, ob = _gate_cols(D, off_b)
    if off_b % tc:
        raise ValueError("gate column offsets must share a column tile")
    ob = off_b // tc

    def body(za_ref, zb_ref, ga_ref, gb_ref, z_ref):
        z_ref[...] = (_sigmoid(ga_ref[...]) * za_ref[...] + _sigmoid(gb_ref[...]) * zb_ref[...]).astype(BF16)

    blk = pl.BlockSpec((tb, tc), lambda i, j: (i, j))
    return _pcall(
        body, name="merge_fwd", grid=(L // tb, D // tc),
        in_specs=[blk, blk, pl.BlockSpec((tb, tc), lambda i, j: (i + nctx, oa + j)),
                  pl.BlockSpec((tb, tc), lambda i, j: (i + nctx, ob + j))],
        out_specs=blk,
        out_shape=jax.ShapeDtypeStruct((L, D), BF16),
        compiler_params=_params(("parallel", "parallel")),
    )(za, zb, p, p)


def _merge_bwd(dz, za, zb, p, n_ctx, off_a, off_b):
    L, D = za.shape
    T = L + n_ctx
    tb = _pick(n_ctx, (256, 128, 64, 32, 16))
    nctx = n_ctx // tb
    tc = _gate_cols(D, off_a)[0]
    oa, ob = off_a // tc, off_b // tc

    def body(dz_ref, za_ref, zb_ref, ga_ref, gb_ref, dza_ref, dzb_ref, dga_ref, dgb_ref):
        i = pl.program_id(1)

        @pl.when(i < nctx)
        def _():
            dga_ref[...] = jnp.zeros_like(dga_ref)
            dgb_ref[...] = jnp.zeros_like(dgb_ref)

        @pl.when(i >= nctx)
        def _():
            dzv = dz_ref[...]
            sa = _sigmoid(ga_ref[...])
            sb = _sigmoid(gb_ref[...])
            dza_ref[...] = (dzv * sa).astype(BF16)
            dzb_ref[...] = (dzv * sb).astype(BF16)
            dga_ref[...] = (dzv * za_ref[...] * sa * (1.0 - sa)).astype(BF16)
            dgb_ref[...] = (dzv * zb_ref[...] * sb * (1.0 - sb)).astype(BF16)

    lat = pl.BlockSpec((tb, tc), lambda j, i: (jnp.maximum(i - nctx, 0), j))
    allr = pl.BlockSpec((tb, tc), lambda j, i: (i, j))
    return _pcall(
        body, name="merge_bwd", grid=(D // tc, T // tb),
        in_specs=[lat, lat, lat, pl.BlockSpec((tb, tc), lambda j, i: (i, oa + j)),
                  pl.BlockSpec((tb, tc), lambda j, i: (i, ob + j))],
        out_specs=[lat, lat, allr, allr],
        out_shape=[jax.ShapeDtypeStruct((L, D), BF16), jax.ShapeDtypeStruct((L, D), BF16),
                   jax.ShapeDtypeStruct((T, D), BF16), jax.ShapeDtypeStruct((T, D), BF16)],
        compiler_params=_params(("arbitrary", "arbitrary")),
    )(dz, za, zb, p, p)


def _shift_down(u, rows):
    return jnp.where(rows == 0, 0.0, pltpu.roll(u, 1, 0))


def _shift_up(u, rows):
    n = u.shape[0]
    return jnp.where(rows == n - 1, 0.0, pltpu.roll(u, n - 1, 0))


def _convgate_fwd(u1, u3, cw, cb):
    L, F = u1.shape
    tc = _pick(F, (256, 128))

    def body(u1_ref, u3_ref, w_ref, b_ref, a_ref):
        u = u1_ref[...]
        rows = lax.broadcasted_iota(jnp.int32, u.shape, 0)
        cv = _shift_down(u, rows) * w_ref[0:1, :] + u * w_ref[1:2, :] + _shift_up(u, rows) * w_ref[2:3, :] + b_ref[...]
        a_ref[...] = (cv * _sigmoid(cv) * u3_ref[...]).astype(BF16)

    blk = pl.BlockSpec((L, tc), lambda j: (0, j))
    return _pcall(
        body, name="convgate_fwd", grid=(F // tc,),
        in_specs=[blk, blk, pl.BlockSpec((8, tc), lambda j: (0, j)), pl.BlockSpec((1, tc), lambda j: (0, j))],
        out_specs=blk,
        out_shape=jax.ShapeDtypeStruct((L, F), BF16),
        compiler_params=_params(("parallel",)),
    )(u1, u3, cw, cb)


def _convgate_bwd(u1, u3, da, cw, cb):
    L, F = u1.shape
    tc = _pick(F, (256, 128))

    def body(u1_ref, u3_ref, da_ref, w_ref, b_ref, du1_ref, du3_ref, s_ref):
        u = u1_ref[...]
        rows = lax.broadcasted_iota(jnp.int32, u.shape, 0)
        um, up = _shift_down(u, rows), _shift_up(u, rows)
        w0, w1, w2 = w_ref[0:1, :], w_ref[1:2, :], w_ref[2:3, :]
        cv = um * w0 + u * w1 + up * w2 + b_ref[...]
        s = _sigmoid(cv)
        dav = da_ref[...]
        du3_ref[...] = (dav * cv * s).astype(BF16)
        dcv = dav * u3_ref[...] * (s * (1.0 + cv * (1.0 - s)))
        du1_ref[...] = (_shift_up(dcv, rows) * w0 + dcv * w1 + _shift_down(dcv, rows) * w2).astype(BF16)
        r8 = lax.broadcasted_iota(jnp.int32, (8, tc), 0)
        s0 = jnp.sum(dcv * um, axis=0, keepdims=True)
        s1 = jnp.sum(dcv * u, axis=0, keepdims=True)
        s2 = jnp.sum(dcv * up, axis=0, keepdims=True)
        s3 = jnp.sum(dcv, axis=0, keepdims=True)
        s_ref[...] = jnp.where(r8 == 0, s0, jnp.where(r8 == 1, s1, jnp.where(r8 == 2, s2,
                     jnp.where(r8 == 3, s3, 0.0))))

    blk = pl.BlockSpec((L, tc), lambda j: (0, j))
    v8 = pl.BlockSpec((8, tc), lambda j: (0, j))
    return _pcall(
        body, name="convgate_bwd", grid=(F // tc,),
        in_specs=[blk, blk, blk, v8, pl.BlockSpec((1, tc), lambda j: (0, j))],
        out_specs=[blk, blk, v8],
        out_shape=[jax.ShapeDtypeStruct((L, F), BF16), jax.ShapeDtypeStruct((L, F), BF16),
                   jax.ShapeDtypeStruct((8, F), F32)],
        compiler_params=_params(("parallel",)),
    )(u1, u3, da, cw, cb)


def _lower_bound(lbl_ref, d):
    l0, l1 = lbl_ref[d, 0:1, :], lbl_ref[d, 1:2, :]
    m = jnp.maximum(l0, l1)
    e0, e1 = jnp.exp(l0 - m), jnp.exp(l1 - m)
    return e0 / (e0 + e1)


def _chunk_cumsum(x, rev):
    n = x.shape[0]
    r = lax.broadcasted_iota(jnp.int32, x.shape, 0) % CHUNK
    k = 1
    while k < CHUNK:
        if rev:
            x = x + jnp.where(r < CHUNK - k, pltpu.roll(x, n - k, 0), 0.0)
        else:
            x = x + jnp.where(r >= k, pltpu.roll(x, k, 0), 0.0)
        k *= 2
    return x


def _gate_terms(z, lb):
    sg = _sigmoid(z)
    f = lb + (1.0 - lb) * sg
    return sg, f


def _decay_terms(z, lb, rev):
    _, f = _gate_terms(z, lb)
    g = jnp.log(f)
    return 1.0 - f, _chunk_cumsum(g, rev), _chunk_cumsum(g, not rev) - g


def _chunk_total(c, rev):
    return c[0:1, :] if rev else c[CHUNK - 1:CHUNK, :]


def _pair_decay(c, s, rev):
    t = lax.broadcasted_iota(jnp.int32, (CHUNK, 1), 0)
    later = (t <= s) if rev else (t >= s)
    return jnp.where(later, jnp.exp(jnp.minimum(c - c[s:s + 1, :], 0.0)), 0.0)


def _scan_chunk(i, n_ctx_chunks, n_chunks, rev):
    if not rev:
        return i
    return jnp.where(i < n_ctx_chunks, n_ctx_chunks - 1 - i, n_chunks + n_ctx_chunks - 1 - i)


def _rows(ci):
    return pl.ds(pl.multiple_of(ci * CHUNK, CHUNK), CHUNK)


def _hgrn_cols(HA):
    return HA // HEAD


def _hgrn_fwd(p, lbl, ng, n_ctx, HA):
    T = p.shape[0]
    L = T - n_ctx
    nh = _hgrn_cols(HA)
    nc, ncc = T // CHUNK, n_ctx // CHUNK

    def body(q_ref, zf_ref, zb_ref, v_ref, og_ref, lbl_ref, ng_ref, ya_ref, o_ref, st_ref,
             c_scr, k_scr, qe_scr, ke_scr, o_scr):
        dirs = ((0, False, zf_ref), (1, True, zb_ref))
        for d, rev, z_ref in dirs:
            k, c, rest = _decay_terms(z_ref[...], _lower_bound(lbl_ref, d), rev)
            c_scr[d] = c
            k_scr[d] = k
            qe_scr[d] = (q_ref[...] * jnp.exp(c)).astype(BF16)
            ke_scr[d] = (k * jnp.exp(rest)).astype(BF16)

        def step(i, states):
            new = []
            for (d, rev, _), St in zip(dirs, states):
                ci = _scan_chunk(i, ncc, nc, rev)
                rows = _rows(ci)
                q, v, c, k = q_ref[rows, :], v_ref[rows, :], c_scr[d, rows, :], k_scr[d, rows, :]
                st_ref[0, d, ci] = St.astype(BF16)
                o = _dot_nt(qe_scr[d, rows, :], St.astype(BF16))
                for s in range(CHUNK):
                    E = _pair_decay(c, s, rev)
                    a = jnp.sum(q * E * k[s:s + 1, :], axis=1, keepdims=True)
                    o = o + a * v[s:s + 1, :]
                o_scr[d, rows, :] = o
                new.append(St * jnp.exp(_chunk_total(c, rev)) + _dot_tn(v.astype(BF16), ke_scr[d, rows, :]))
            return tuple(new)

        zero = jnp.zeros((HEAD, HEAD), F32)
        lax.fori_loop(0, nc, step, (zero, zero), unroll=2)

        o = o_scr[0, pl.ds(n_ctx, L), :] + o_scr[1, pl.ds(n_ctx, L), :]
        o_ref[...] = o
        r = lax.rsqrt(jnp.mean(o * o, axis=-1, keepdims=True) + EPS)
        og = og_ref[pl.ds(n_ctx, L), :]
        ya_ref[...] =(o * r * ng_ref[...] * (og * _sigmoid(og))).astype(BF16)

    cb = HA // HEAD
    col = lambda kk: pl.BlockSpec((T, HEAD), lambda h: (0, kk * cb + h))
    return _pcall(
        body, name="hgrn_fwd", grid=(nh,),
        in_specs=[col(0), col(1), col(2), col(3), col(4),
                  pl.BlockSpec((2, 2, HEAD), lambda h: (0, 0, h)), pl.BlockSpec((1, HEAD), lambda h: (0, 0))],
        out_specs=[pl.BlockSpec((L, HEAD), lambda h: (0, h)), pl.BlockSpec((L, HEAD), lambda h: (0, h)),
                   pl.BlockSpec((1, 2, nc, HEAD, HEAD), lambda h: (h, 0, 0, 0, 0))],
        out_shape=[jax.ShapeDtypeStruct((L, HA), BF16), jax.ShapeDtypeStruct((L, HA), F32),
                   jax.ShapeDtypeStruct((nh, 2, nc, HEAD, HEAD), BF16)],
        scratch_shapes=[pltpu.VMEM((2, T, HEAD), F32), pltpu.VMEM((2, T, HEAD), F32),
                        pltpu.VMEM((2, T, HEAD), BF16), pltpu.VMEM((2, T, HEAD), BF16),
                        pltpu.VMEM((2, T, HEAD), F32)],
        compiler_params=_params(("parallel",)),
    )(p, p, p, p, p, lbl, ng)


def _hgrn_bwd(p, lbl, ng, o, dya, st, n_ctx, HA):
    T = p.shape[0]
    L = T - n_ctx
    nh = _hgrn_cols(HA)
    nc, ncc = T // CHUNK, n_ctx // CHUNK

    def body(q_ref, zf_ref, zb_ref, v_ref, og_ref, lbl_ref, ng_ref, o_ref, dya_ref, st_ref,
             dq_ref, dzf_ref, dzb_ref, dv_ref, dog_ref, dlbl_ref, dng_ref,
             do_scr, c_scr, k_scr, qe_scr, ke_scr, dg_scr, dk_scr, dq_scr, dv_scr):
        h = pl.program_id(0)
        ov = o_ref[...]
        r = lax.rsqrt(jnp.mean(ov * ov, axis=-1, keepdims=True) + EPS)
        oh = ov * r
        ogv = og_ref[pl.ds(n_ctx, L), :]
        sg_o = _sigmoid(ogv)
        dyv = dya_ref[...]
        ngv = ng_ref[...]
        dog_ref[pl.ds(0, n_ctx), :] = jnp.zeros((n_ctx, HEAD), BF16)
        dog_ref[pl.ds(n_ctx, L), :] = (dyv * oh * ngv * (sg_o * (1.0 + ogv * (1.0 - sg_o)))).astype(BF16)
        don = dyv * (ogv * sg_o)
        dng = jnp.sum(don * oh, axis=0, keepdims=True)
        doh = don * ngv
        do_scr[pl.ds(0, n_ctx), :] = jnp.zeros((n_ctx, HEAD), F32)
        do_scr[pl.ds(n_ctx, L), :] = r * (doh - oh * jnp.mean(doh * oh, axis=-1, keepdims=True))

        @pl.when(h == 0)
        def _():
            dng_ref[...] = jnp.zeros_like(dng_ref)

        dng_ref[0:1, :] += dng

        t16 = lax.broadcasted_iota(jnp.int32, (CHUNK, HEAD), 0)
        dirs = ((0, False, zf_ref, dzf_ref), (1, True, zb_ref, dzb_ref))
        for d, rev, z_ref, _ in dirs:
            k, c, rest = _decay_terms(z_ref[...], _lower_bound(lbl_ref, d), rev)
            c_scr[d] = c
            k_scr[d] = k
            qe_scr[d] = (q_ref[...] * jnp.exp(c)).astype(BF16)
            ke_scr[d] = (k * jnp.exp(rest)).astype(BF16)
        dq_scr[...] = jnp.zeros_like(dq_scr)
        dv_scr[...] = jnp.zeros_like(dv_scr)

        zero = jnp.zeros((HEAD, HEAD), F32)

        def bwd_step(ii, carry):
            i = nc - 1 - ii
            new = []
            for (d, rev, _, _), dSt in zip(dirs, carry):
                ci = _scan_chunk(i, ncc, nc, rev)
                rows = _rows(ci)
                q, v, do = q_ref[rows, :], v_ref[rows, :], do_scr[rows, :]
                c, k = c_scr[d, rows, :], k_scr[d, rows, :]
                tot = _chunk_total(c, rev)
                etot = jnp.exp(tot)
                St = st_ref[0, d, ci]
                dSb = dSt.astype(BF16)
                do_b = do.astype(BF16)
                dq = _dot(do_b, St) * jnp.exp(c)
                dk = _dot(v.astype(BF16), dSb) * jnp.exp(tot - c)
                dv = _dot_nt(ke_scr[d, rows, :], dSb)
                dtot = (jnp.sum(St.astype(F32) * dSt, axis=0, keepdims=True) * etot
                        + jnp.sum(k * dk, axis=0, keepdims=True))
                for s in range(CHUNK):
                    E = _pair_decay(c, s, rev)
                    XE = E * k[s:s + 1, :]
                    a = jnp.sum(q * XE, axis=1, keepdims=True)
                    da = jnp.sum(do * v[s:s + 1, :], axis=1, keepdims=True)
                    dq = dq + da * XE
                    dk_row = jnp.sum(da * q * E, axis=0, keepdims=True)
                    dv_row = jnp.sum(a * do, axis=0, keepdims=True)
                    dk = dk + jnp.where(t16 == s, dk_row, 0.0)
                    dv = dv + jnp.where(t16 == s, dv_row, 0.0)
                dg_scr[d, rows, :] = _chunk_cumsum(q * dq - k * dk, not rev) + dtot
                dk_scr[d, rows, :] = dk
                dq_scr[rows, :] += dq
                dv_scr[rows, :] += dv
                new.append(dSt * etot + _dot_tn(do_b, qe_scr[d, rows, :]))
            return tuple(new)

        lax.fori_loop(0, nc, bwd_step, (zero, zero), unroll=2)

        for d, _, z_ref, dz_ref in dirs:
            lb = _lower_bound(lbl_ref, d)
            sg, f = _gate_terms(z_ref[...], lb)
            df = dg_scr[d] / f - dk_scr[d]
            dz_ref[...] = (df * (1.0 - lb) * sg * (1.0 - sg)).astype(BF16)
            dl0 = jnp.sum(df * (1.0 - sg), axis=0, keepdims=True) * lb * (1.0 - lb)
            dlbl_ref[d, 0:1, :] = dl0
            dlbl_ref[d, 1:2, :] = -dl0
        dq_ref[...] = dq_scr[...].astype(BF16)
        dv_ref[...] = dv_scr[...].astype(BF16)

    cb = HA // HEAD
    col = lambda kk: pl.BlockSpec((T, HEAD), lambda h: (0, kk * cb + h))
    tcol = pl.BlockSpec((T, HEAD), lambda h: (0, h))
    lcol = pl.BlockSpec((L, HEAD), lambda h: (0, h))
    outs = _pcall(
        body, name="hgrn_bwd", grid=(nh,),
        in_specs=[col(0), col(1), col(2), col(3), col(4),
                  pl.BlockSpec((2, 2, HEAD), lambda h: (0, 0, h)), pl.BlockSpec((1, HEAD), lambda h: (0, 0)),
                  lcol, lcol,
                  pl.BlockSpec((1, 2, nc, HEAD, HEAD), lambda h: (h, 0, 0, 0, 0), pipeline_mode=pl.Buffered(1))],
        out_specs=[tcol, tcol, tcol, tcol, tcol, pl.BlockSpec((2, 2, HEAD), lambda h: (0, 0, h)),
                   pl.BlockSpec((8, HEAD), lambda h: (0, 0))],
        out_shape=[jax.ShapeDtypeStruct((T, HA), BF16)] * 5 + [jax.ShapeDtypeStruct((2, 2, HA), F32),
                                                               jax.ShapeDtypeStruct((8, HEAD), F32)],
        scratch_shapes=[pltpu.VMEM((T, HEAD), F32),
                        pltpu.VMEM((2, T, HEAD), F32), pltpu.VMEM((2, T, HEAD), F32),
                        pltpu.VMEM((2, T, HEAD), BF16), pltpu.VMEM((2, T, HEAD), BF16),
                        pltpu.VMEM((2, T, HEAD), F32), pltpu.VMEM((2, T, HEAD), F32),
                        pltpu.VMEM((T, HEAD), F32), pltpu.VMEM((T, HEAD), F32)],
        compiler_params=_params(("arbitrary",)),
    )(p, p, p, p, p, lbl, ng, o, dya, st)
    return outs


def _swap_halves(t, lane):
    q = HEAD // 4
    return jnp.where((lane % (2 * q)) < q, pltpu.roll(t, HEAD - q, 1), pltpu.roll(t, q, 1))


def _qk_norm(t, g):
    r = lax.rsqrt(jnp.mean(t * t, axis=-1, keepdims=True) + EPS)
    return t * r, r


def _rope(t, cos, sin, lane):
    return t * cos + _swap_halves(t, lane) * sin


def _qk_norm_bwd(dy, th, r, g):
    dth = dy * g
    return r * (dth - th * jnp.mean(dth * th, axis=-1, keepdims=True)), jnp.sum(dy * th, axis=0, keepdims=True)


def _rope_bwd(dy, cos, sin, lane):
    return dy * cos + _swap_halves(dy * sin, lane)


def _na_geometry(L):
    n_rows = L // GRID_W
    kr = min(WIN_R, n_rows)
    return n_rows, kr


def _na_prep(q_ref, k_ref, v_ref, gq_ref, gk_ref, cos_ref, sin_ref, qs, ks, vs, n_ctx, L):
    lane = lax.broadcasted_iota(jnp.int32, (L, HEAD), 1)
    cos, sin = cos_ref[...], sin_ref[...]
    qh, _ = _qk_norm(q_ref[pl.ds(n_ctx, L), :], None)
    qs[...] = _rope(qh * gq_ref[...], cos, sin, lane).astype(BF16)
    kh, _ = _qk_norm(k_ref[pl.ds(n_ctx, L), :], None)
    ks[pl.ds(n_ctx, L), :] = _rope(kh * gk_ref[...], cos, sin, lane).astype(BF16)
    kc, _ = _qk_norm(k_ref[pl.ds(0, n_ctx), :], None)
    ks[pl.ds(0, n_ctx), :] = (kc * gk_ref[...]).astype(BF16)
    vs[...] = v_ref[...].astype(BF16)


def _na_scores(r, qs, ks, bias_ref, n_ctx, n_rows, kr):
    scale = HEAD ** -0.5
    r0 = jnp.clip(r - WIN_R // 2, 0, n_rows - kr)
    qrows = pl.ds(pl.multiple_of(r * GRID_W, GRID_W), GRID_W)
    krows = pl.ds(pl.multiple_of(n_ctx + r0 * GRID_W, GRID_W), kr * GRID_W)
    qv = qs[qrows, :]
    sb = _dot_nt(qv, ks[krows, :]) * scale
    b0 = r0 - r + (WIN_R - 1)
    sb = sb + jnp.concatenate([bias_ref[0, b0 + 2 * jj] for jj in range(kr // 2)], axis=1)
    sc = _dot_nt(qv, ks[pl.ds(0, n_ctx), :]) * scale
    m = jnp.maximum(jnp.max(sb, axis=1, keepdims=True), jnp.max(sc, axis=1, keepdims=True))
    eb, ec = jnp.exp(sb - m), jnp.exp(sc - m)
    inv = 1.0 / (jnp.sum(eb, axis=1, keepdims=True) + jnp.sum(ec, axis=1, keepdims=True))
    return eb * inv, ec * inv, qrows, krows, b0


def _na_fwd(p, bias, gq, gk, cos, sin, n_ctx, off, HB):
    T = p.shape[0]
    L = T - n_ctx
    nh = HB // HEAD
    n_rows, kr = _na_geometry(L)
    ob = off // HEAD

    def body(q_ref, k_ref, v_ref, bias_ref, gq_ref, gk_ref, cos_ref, sin_ref, y_ref, qs, ks, vs):
        _na_prep(q_ref, k_ref, v_ref, gq_ref, gk_ref, cos_ref, sin_ref, qs, ks, vs, n_ctx, L)

        def step(r, carry):
            pb, pc, qrows, krows, _ = _na_scores(r, qs, ks, bias_ref, n_ctx, n_rows, kr)
            y = _dot(pb.astype(BF16), vs[krows, :]) + _dot(pc.astype(BF16), vs[pl.ds(0, n_ctx), :])
            y_ref[qrows, :] = y.astype(BF16)
            return carry

        lax.fori_loop(0, n_rows, step, 0)

    col = lambda kk: pl.BlockSpec((T, HEAD), lambda h: (0, ob + kk * nh + h))
    vec = pl.BlockSpec((1, HEAD), lambda h: (0, 0))
    tab = pl.BlockSpec((L, HEAD), lambda h: (0, 0))
    return _pcall(
        body, name="na_fwd", grid=(nh,),
        in_specs=[col(0), col(1), col(2), pl.BlockSpec((1,) + bias.shape[1:], lambda h: (h, 0, 0, 0)),
                  vec, vec, tab, tab],
        out_specs=pl.BlockSpec((L, HEAD), lambda h: (0, h)),
        out_shape=jax.ShapeDtypeStruct((L, HB), BF16),
        scratch_shapes=[pltpu.VMEM((L, HEAD), BF16), pltpu.VMEM((T, HEAD), BF16), pltpu.VMEM((T, HEAD), BF16)],
        compiler_params=_params(("parallel",)),
    )(p, p, p, bias, gq, gk, cos, sin)


def _na_bwd(p, bias, gq, gk, cos, sin, dyb, n_ctx, off, HB):
    T = p.shape[0]
    L = T - n_ctx
    nh = HB // HEAD
    n_rows, kr = _na_geometry(L)
    ob = off // HEAD
    scale = HEAD ** -0.5

    def body(q_ref, k_ref, v_ref, bias_ref, gq_ref, gk_ref, cos_ref, sin_ref, dy_ref,
             dq_ref, dk_ref, dv_ref, dbias_ref, dg_ref, qs, ks, vs, dqa, dka, dva):
        h = pl.program_id(0)
        _na_prep(q_ref, k_ref, v_ref, gq_ref, gk_ref, cos_ref, sin_ref, qs, ks, vs, n_ctx, L)
        dka[...] = jnp.zeros_like(dka)
        dva[...] = jnp.zeros_like(dva)
        dbias_ref[...] = jnp.zeros_like(dbias_ref)

        def step(r, carry):
            pb, pc, qrows, krows, b0 = _na_scores(r, qs, ks, bias_ref, n_ctx, n_rows, kr)
            crows = pl.ds(0, n_ctx)
            do = dy_ref[qrows, :]
            qv = qs[qrows, :]
            dpb = _dot_nt(do, vs[krows, :])
            dpc = _dot_nt(do, vs[crows, :])
            delta = jnp.sum(pb * dpb, axis=1, keepdims=True) + jnp.sum(pc * dpc, axis=1, keepdims=True)
            dsb = pb * (dpb - delta)
            dsc = pc * (dpc - delta)
            for jj in range(kr // 2):
                dbias_ref[0, b0 + 2 * jj] += dsb[:, jj * 2 * GRID_W:(jj + 1) * 2 * GRID_W]
            dsb_b, dsc_b = dsb.astype(BF16), dsc.astype(BF16)
            dqa[qrows, :] = (_dot(dsb_b, ks[krows, :]) + _dot(dsc_b, ks[crows, :])) * scale
            dka[krows, :] += _dot_tn(dsb_b, qv) * scale
            dka[crows, :] += _dot_tn(dsc_b, qv) * scale
            dva[krows, :] += _dot_tn(pb.astype(BF16), do)
            dva[crows, :] += _dot_tn(pc.astype(BF16), do)
            return carry

        lax.fori_loop(0, n_rows, step, 0)

        lane = lax.broadcasted_iota(jnp.int32, (L, HEAD), 1)
        cos, sin = cos_ref[...], sin_ref[...]
        lat, ctx = pl.ds(n_ctx, L), pl.ds(0, n_ctx)
        gqv, gkv = gq_ref[...], gk_ref[...]
        qh, rq = _qk_norm(q_ref[lat, :], None)
        dq, dgq = _qk_norm_bwd(_rope_bwd(dqa[...], cos, sin, lane), qh, rq, gqv)
        dq_ref[ctx, :] = jnp.zeros((n_ctx, HEAD), BF16)
        dq_ref[lat, :] = dq.astype(BF16)
        kh, rk = _qk_norm(k_ref[lat, :], None)
        dk, dgk = _qk_norm_bwd(_rope_bwd(dka[lat, :], cos, sin, lane), kh, rk, gkv)
        dk_ref[lat, :] = dk.astype(BF16)
        kch, rkc = _qk_norm(k_ref[ctx, :], None)
        dkc, dgkc = _qk_norm_bwd(dka[ctx, :], kch, rkc, gkv)
        dk_ref[ctx, :] = dkc.astype(BF16)
        dv_ref[...] = dva[...].astype(BF16)

        @pl.when(h == 0)
        def _():
            dg_ref[...] = jnp.zeros_like(dg_ref)

        dg_ref[0:1, :] += dgq
        dg_ref[1:2, :] += dgk + dgkc

    col = lambda kk: pl.BlockSpec((T, HEAD), lambda h: (0, ob + kk * nh + h))
    vec = pl.BlockSpec((1, HEAD), lambda h: (0, 0))
    tab = pl.BlockSpec((L, HEAD), lambda h: (0, 0))
    tcol = pl.BlockSpec((T, HEAD), lambda h: (0, h))
    bspec = pl.BlockSpec((1,) + bias.shape[1:], lambda h: (h, 0, 0, 0))
    return _pcall(
        body, name="na_bwd", grid=(nh,),
        in_specs=[col(0), col(1), col(2), bspec, vec, vec, tab, tab, pl.BlockSpec((L, HEAD), lambda h: (0, h))],
        out_specs=[tcol, tcol, tcol, bspec, pl.BlockSpec((8, HEAD), lambda h: (0, 0))],
        out_shape=[jax.ShapeDtypeStruct((T, HB), BF16)] * 3 + [jax.ShapeDtypeStruct(bias.shape, F32),
                                                               jax.ShapeDtypeStruct((8, HEAD), F32)],
        scratch_shapes=[pltpu.VMEM((L, HEAD), BF16), pltpu.VMEM((T, HEAD), BF16), pltpu.VMEM((T, HEAD), BF16),
                        pltpu.VMEM((L, HEAD), F32), pltpu.VMEM((T, HEAD), F32), pltpu.VMEM((T, HEAD), F32)],
        compiler_params=_params(("arbitrary",)),
    )(p, p, p, bias, gq, gk, cos, sin, dyb)


def _bias_tables():
    w = np.arange(GRID_W)
    col_start = np.clip(w - WIN_C // 2, 0, GRID_W - WIN_C)
    col_in = (w[None, :] >= col_start[:, None]) & (w[None, :] < col_start[:, None] + WIN_C)
    dc = np.clip(w[None, :] - w[:, None], -(WIN_C - 1), WIN_C - 1) + WIN_C - 1
    n_pair = 2 * WIN_R
    ridx = np.zeros((n_pair, GRID_W, 2 * GRID_W), np.int32)
    cidx = np.zeros((n_pair, GRID_W, 2 * GRID_W), np.int32)
    valid = np.zeros((n_pair, GRID_W, 2 * GRID_W), bool)
    for i in range(n_pair):
        for half in range(2):
            row = i + half
            sl = slice(half * GRID_W, (half + 1) * GRID_W)
            ridx[i, :, sl] = min(row, 2 * WIN_R - 2)
            cidx[i, :, sl] = dc
            valid[i, :, sl] = col_in & (row <= 2 * WIN_R - 2)
    return ridx, cidx, valid


def _bias_onehot():
    _, cidx, valid = _bias_tables()
    K = GRID_W * 2 * GRID_W
    oh = np.zeros((K, 128), np.float32)
    neg = np.full((1, K), NEG, np.float32)
    for cq in range(GRID_W):
        for ll in range(2 * GRID_W):
            if valid[0, cq, ll]:
                oh[cq * 2 * GRID_W + ll, (ll // GRID_W) * 64 + cidx[0, cq, ll]] = 1.0
                neg[0, cq * 2 * GRID_W + ll] = 0.0
    return oh, neg


def _expand_bias(table):
    H = table.shape[0]
    n_pair, n_dc = 2 * WIN_R, 2 * WIN_C - 1
    tp = jnp.pad(table, ((0, 0), (0, n_pair + 1 - table.shape[1]), (0, 64 - n_dc)))
    t2 = jnp.concatenate([tp[:, :n_pair], tp[:, 1:n_pair + 1]], axis=-1).reshape(H * n_pair, 128)
    oh, neg = _bias_onehot()

    def body(t_ref, oh_ref, neg_ref, o_ref):
        o_ref[...] = lax.dot_general(t_ref[...], oh_ref[...], (((1,), (1,)), ((), ())), precision=HI,
                                     preferred_element_type=F32) + neg_ref[...]

    out = _pcall(body, name="bias_expand", out_shape=jax.ShapeDtypeStruct((H * n_pair, oh.shape[0]), F32),
                         compiler_params=_params())(t2, jnp.asarray(oh), jnp.asarray(neg))
    return out.reshape(H, n_pair, GRID_W, 2 * GRID_W)


def _bias_grad(dbias):
    H = dbias.shape[0]
    n_pair, n_dc = 2 * WIN_R, 2 * WIN_C - 1
    K = GRID_W * 2 * GRID_W
    oh, _ = _bias_onehot()
    flat = dbias.reshape(H * n_pair, K)

    def body(d_ref, oh_ref, o_ref):
        o_ref[...] = jnp.dot(d_ref[...], oh_ref[...], precision=HI, preferred_element_type=F32)

    g = _pcall(body, name="bias_grad", out_shape=jax.ShapeDtypeStruct((H * n_pair, 128), F32),
                       compiler_params=_params())(flat, jnp.asarray(oh))
    g = g.reshape(H, n_pair, 128)
    left, right = g[:, :, :n_dc], g[:, :, 64:64 + n_dc]
    out = left[:, :n_pair - 1]
    return out.at[:, 1:].add(right[:, :n_pair - 2])


def _rope_tables(L):
    pos = np.arange(L)
    row = (pos // GRID_W).astype(np.float32)
    colp = (pos % GRID_W).astype(np.float32)
    half = HEAD // 2
    nf = half // 2
    inv = (ROPE_THETA ** (-np.arange(nf, dtype=np.float32) / nf)).astype(np.float32)

    def tabs(pv):
        ang = pv[:, None] * inv[None, :]
        c, s = np.cos(ang), np.sin(ang)
        return np.concatenate([c, c], axis=1), np.concatenate([-s, s], axis=1)

    cr, sr = tabs(row)
    cc, sc = tabs(colp)
    return (jnp.asarray(np.concatenate([cr, cc], axis=1), F32), jnp.asarray(np.concatenate([sr, sc], axis=1), F32))


def _adamw(w, g, m, v, name, after=None):
    R, C = w.shape
    tr = _row_tile(R, C)
    c1 = 1.0 - ADAM_B1 ** ADAM_STEP
    c2 = 1.0 - ADAM_B2 ** ADAM_STEP
    deps = [] if after is None else [after]

    def body(w_ref, g_ref, m_ref, v_ref, *rest):
        d_ref, mo_ref, vo_ref = rest[len(deps):]
        gv = g_ref[...]
        mn = ADAM_B1 * m_ref[...] + (1.0 - ADAM_B1) * gv
        vn = ADAM_B2 * v_ref[...] + (1.0 - ADAM_B2) * (gv * gv)
        mo_ref[...] = mn
        vo_ref[...] = vn
        d_ref[...] = -ADAM_LR * ((mn / c1) / (jnp.sqrt(vn / c2) + ADAM_EPS) + ADAM_WD * w_ref[...])

    blk = pl.BlockSpec((tr, C), lambda i: (i, 0))
    return _pcall(
        body, name=name, grid=(R // tr,),
        in_specs=[blk] * 4 + [_ANY] * len(deps), out_specs=[blk] * 3,
        out_shape=[jax.ShapeDtypeStruct((R, C), F32)] * 3,
        compiler_params=_params(("parallel",)),
    )(w, g, m, v, *deps)


PACK_W = 1024


def _pack(parts):
    flat, offs, pos = [], [], 0
    for a in parts:
        n = a.size
        padn = -n % PACK_W
        flat.append(jnp.pad(a.reshape(-1).astype(F32), (0, padn)))
        offs.append((pos, n, a.shape))
        pos += n + padn
    tail = -pos % (8 * PACK_W)
    if tail:
        flat.append(jnp.zeros((tail,), F32))
    return jnp.concatenate(flat).reshape(-1, PACK_W), offs


def _unpack(buf, offs, i):
    pos, n, shape = offs[i]
    return buf.reshape(buf.shape[:-2] + (-1,))[..., pos:pos + n].reshape(buf.shape[:-2] + shape)


def kernel(x, c, ctx, c_ctx, ada_w, ada_b, norm1_g, norm2_g, w_in, hgrn_lb_logits, hgrn_norm_g, na_q_norm_g, na_k_norm_g, na_rel_bias, w_branch_a, w_branch_b, w_out, ffn_w1, ffn_w3, ffn_conv_w, ffn_conv_b, ffn_w2, loss_target, m_c_ctx, m_ada_w, m_ada_b, m_norm1_g, m_norm2_g, m_w_in, m_hgrn_lb_logits, m_hgrn_norm_g, m_na_q_norm_g, m_na_k_norm_g, m_na_rel_bias, m_w_branch_a, m_w_branch_b, m_w_out, m_ffn_w1, m_ffn_w3, m_ffn_conv_w, m_ffn_conv_b, m_ffn_w2, v_c_ctx, v_ada_w, v_ada_b, v_norm1_g, v_norm2_g, v_w_in, v_hgrn_lb_logits, v_hgrn_norm_g, v_na_q_norm_g, v_na_k_norm_g, v_na_rel_bias, v_w_branch_a, v_w_branch_b, v_w_out, v_ffn_w1, v_ffn_w3, v_ffn_conv_w, v_ffn_conv_b, v_ffn_w2):
    weights = dict(c_ctx=c_ctx, ada_w=ada_w, ada_b=ada_b, norm1_g=norm1_g, norm2_g=norm2_g, w_in=w_in,
                   hgrn_lb_logits=hgrn_lb_logits, hgrn_norm_g=hgrn_norm_g, na_q_norm_g=na_q_norm_g,
                   na_k_norm_g=na_k_norm_g, na_rel_bias=na_rel_bias, w_branch_a=w_branch_a, w_branch_b=w_branch_b,
                   w_out=w_out, ffn_w1=ffn_w1, ffn_w3=ffn_w3, ffn_conv_w=ffn_conv_w, ffn_conv_b=ffn_conv_b,
                   ffn_w2=ffn_w2)
    moms = dict(c_ctx=(m_c_ctx, v_c_ctx), ada_w=(m_ada_w, v_ada_w), ada_b=(m_ada_b, v_ada_b),
                norm1_g=(m_norm1_g, v_norm1_g), norm2_g=(m_norm2_g, v_norm2_g), w_in=(m_w_in, v_w_in),
                hgrn_lb_logits=(m_hgrn_lb_logits, v_hgrn_lb_logits), hgrn_norm_g=(m_hgrn_norm_g, v_hgrn_norm_g),
                na_q_norm_g=(m_na_q_norm_g, v_na_q_norm_g), na_k_norm_g=(m_na_k_norm_g, v_na_k_norm_g),
                na_rel_bias=(m_na_rel_bias, v_na_rel_bias), w_branch_a=(m_w_branch_a, v_w_branch_a),
                w_branch_b=(m_w_branch_b, v_w_branch_b), w_out=(m_w_out, v_w_out), ffn_w1=(m_ffn_w1, v_ffn_w1),
                ffn_w3=(m_ffn_w3, v_ffn_w3), ffn_conv_w=(m_ffn_conv_w, v_ffn_conv_w),
                ffn_conv_b=(m_ffn_conv_b, v_ffn_conv_b), ffn_w2=(m_ffn_w2, v_ffn_w2))
    order = list(weights)

    L, D = x.shape[1], x.shape[2]
    N = ctx.shape[1]
    T = N + L
    HA = w_branch_a.shape[1]
    HB = w_branch_b.shape[1]
    F = ffn_conv_b.shape[1]
    IN = 5 * HA + 3 * HB + 2 * D
    n_ada = ada_w.shape[2]
    ix, iy, ic = _pos()
    chip = 2 * ix + iy
    dev = 2 * chip + ic

    _PENDING.clear()
    pk0, offs0 = _pack([c[0], hgrn_lb_logits, ffn_conv_w[0]])
    g0 = _allgather8(pk0, "gather_small0")
    c_all = _unpack(g0, offs0, 0)
    lbl_parts = _unpack(g0, offs0, 1)
    lbl = jnp.concatenate([lbl_parts[2 * j] for j in range(N_CHIP)], axis=-1)
    cw_parts = _unpack(g0, offs0, 2)
    cw = jnp.concatenate([cw_parts[2 * j] for j in range(N_CHIP)], axis=-1)
    cw8 = jnp.pad(cw, ((0, 5), (0, 0)))

    cs = jnp.concatenate([c_all, c_ctx[None, :], jnp.zeros((7, D), F32)], axis=0)
    ada_b_mine = lax.dynamic_slice(ada_b, (0, chip * n_ada), (1, n_ada))
    mod_mine = _ada_fwd(cs, ada_w[0], ada_b_mine)
    gm = _allgather8(mod_mine, "gather_mod")
    mod = jnp.concatenate([gm[2 * j] for j in range(N_CHIP)], axis=-1)
    mod_l = lax.dynamic_slice(mod, (dev, 0), (1, N_MOD * D)).reshape(N_MOD, D)
    mod_c = mod[8].reshape(N_MOD, D)
    sh1, sc1, g1, sh2, sc2, g2 = [mod_l[i:i + 1] for i in range(N_MOD)]
    shift1 = jnp.concatenate([mod_c[0:1], sh1], axis=0)
    scale1 = jnp.concatenate([mod_c[1:2], sc1], axis=0)

    shards = [w_in[0], w_branch_a[0], w_branch_b[0], w_out[0], ffn_w1[0], ffn_w3[0], ffn_w2[0]]
    names = ["w_in", "w_a", "w_b", "w_out", "w1", "w3", "w2"]
    slots = [_cast_bf16_slot(s, "cast_" + nm) for s, nm in zip(shards, names)]
    gat_in = _gather_start("in", slots[0:1], gm)
    gat_mix = _gather_start("mix", slots[1:4])
    gat_ffn = _gather_start("ffn", slots[4:7])

    xall = jnp.concatenate([ctx[0], x[0]], axis=0)
    h_all = _rms1_fwd(xall, norm1_g, shift1, scale1, N)
    gat_in = _gather_mid(gat_in, h_all)
    (Win,) = _gather_finish(gat_in, h_all)
    p = _mm_nn(h_all, Win, F32, "mm_p")
    gat_mix = _gather_mid(gat_mix, p)
    y_a, o_a, st_a = _hgrn_fwd(p, lbl, hgrn_norm_g, N, HA)
    Wa, Wb, Wo = _gather_finish(gat_mix, y_a)
    Wo = Wo.reshape(1, D, D)
    gat_ffn = _gather_mid(gat_ffn, y_a)
    bias = _expand_bias(na_rel_bias[0])
    cos, sin = _rope_tables(L)
    off_na = 5 * HA
    y_b = _na_fwd(p, bias, na_q_norm_g, na_k_norm_g, cos, sin, N, off_na, HB)
    za = _mm_nn(y_a, Wa, F32, "mm_za")
    zb = _mm_nn(y_b, Wb, F32, "mm_zb")
    off_ga, off_gb = 5 * HA + 3 * HB, 5 * HA + 3 * HB + D
    z = _merge_fwd(za, zb, p, N, off_ga, off_gb)
    mo = _mm_nn(z, Wo, F32, "mm_mo")
    vec2 = jnp.concatenate([g1, norm2_g, sh2, sc2, jnp.zeros((4, D), F32)], axis=0)
    x_mid, h2 = _resid_rms2_fwd(x[0], mo, vec2)
    W1, W3, W2 = _gather_finish(gat_ffn, h2)
    W2 = W2.reshape(1, F, D)
    u1 = _mm_nn(h2, W1, F32, "mm_u1")
    u3 = _mm_nn(h2, W3, F32, "mm_u3")
    a = _convgate_fwd(u1, u3, cw8, ffn_conv_b)
    f = _mm_nn(a, W2, F32, "mm_f")
    dy, df, s_loss = _loss_head(x_mid, f, g2, loss_target[0])
    loss = lax.psum(s_loss[1, 0], ("x", "y", "c"))
    d_g2 = s_loss[0:1]

    gW2 = _mm_tn(a, df, 1, "mm_gw2").reshape(N_CHIP, F // N_CHIP, D)
    da = _mm_nt(df, W2, F32, "mm_da")
    du1, du3, s_conv = _convgate_bwd(u1, u3, da, cw8, ffn_conv_b)
    gW1 = _mm_tn(h2, du1, N_CHIP, "mm_gw1")
    gW3 = _mm_tn(h2, du3, N_CHIP, "mm_gw3")
    rs_ffn = _rs_start("ffn", [gW2, gW1, gW3])
    dh2a = _mm_nt(du1, W1, F32, "mm_dh2a")
    dh2b = _mm_nt(du3, W3, F32, "mm_dh2b")
    rs_ffn = _rs_scatter(rs_ffn, dh2b)
    dxm, dmo, s_rms2 = _resid_rms2_bwd(x_mid, dh2a, dh2b, dy, mo, vec2)
    gWo = _mm_tn(z, dmo, 1, "mm_gwo").reshape(N_CHIP, D // N_CHIP, D)
    dz = _mm_nt(dmo, Wo, F32, "mm_dz")
    dza, dzb, dga, dgb = _merge_bwd(dz, za, zb, p, N, off_ga, off_gb)
    gWa = _mm_tn(y_a, dza, N_CHIP, "mm_gwa")
    gWb = _mm_tn(y_b, dzb, N_CHIP, "mm_gwb")
    rs_mix = _rs_start("mix", [gWo, gWa, gWb])
    dya = _mm_nt(dza, Wa, F32, "mm_dya")
    dyb = _mm_nt(dzb, Wb, BF16, "mm_dyb")
    rs_mix = _rs_scatter(rs_mix, dyb)
    dq_a, dzf, dzbk, di_a, dog, dlbl, s_ng = _hgrn_bwd(p, lbl, hgrn_norm_g, o_a, dya, st_a, N, HA)
    rs_ffn = _rs_join(rs_ffn, dq_a)
    dq_n, dk_n, dv_n, dbias, s_qk = _na_bwd(p, bias, na_q_norm_g, na_k_norm_g, cos, sin, dyb, N, off_na, HB)
    rs_mix = _rs_join(rs_mix, dq_n)
    dp = jnp.concatenate([dq_a, dzf, dzbk, di_a, dog, dq_n, dk_n, dv_n, dga, dgb], axis=1)
    gWin = _mm_tn(h_all, dp, N_CHIP, "mm_gwin")
    rs_in = _rs_start("in", [gWin])
    dh = _mm_nt(dp, Win, F32, "mm_dh")
    grad_x, s_rms1 = _rms1_bwd(xall, dh, dxm, norm1_g, scale1, N)
    d_table = _bias_grad(dbias)

    zD = jnp.zeros((1, D), F32)
    dmod_l = jnp.concatenate([s_rms1[2:3], s_rms1[3:4], s_rms2[3:4], s_rms2[0:1], s_rms2[1:2], d_g2], axis=0)
    dmod_c = jnp.concatenate([s_rms1[0:1], s_rms1[1:2], zD, zD, zD, zD], axis=0)
    pk1, offs1 = _pack([dmod_l, dmod_c, s_rms1[4], s_rms2[2], dlbl, s_ng[0], s_qk[0], s_qk[1], d_table,
                        s_conv[0:3], s_conv[3]])
    g1all = _allgather8(pk1, "gather_small1")
    tot1 = _sum8(g1all, "sum_small1")
    dmod_rows = _unpack(g1all, offs1, 0).reshape(N_DEV, N_MOD * D)
    dmod_c_tot = _unpack(tot1, offs1, 1).reshape(1, N_MOD * D)
    dmod16 = jnp.concatenate([dmod_rows, dmod_c_tot, jnp.zeros((7, N_MOD * D), F32)], axis=0)
    dmod16_mine = lax.dynamic_slice(dmod16, (0, chip * n_ada), (16, n_ada))
    g_ada_w, dact = _ada_bwd(cs, ada_w[0], dmod16_mine)
    pk2, offs2 = _pack([dact[8]])
    g2all = _allgather8(pk2, "gather_small2")
    dact_rows = _unpack(g2all, offs2, 0)
    dact_sel = jnp.concatenate([dact_rows[2 * j][None] for j in range(N_CHIP)] + [jnp.zeros((4, D), F32)], axis=0)

    grads = {}
    grads["ada_w"] = g_ada_w[None]
    grads["ada_b"] = (_unpack(tot1, offs1, 0) + _unpack(tot1, offs1, 1)).reshape(1, N_MOD * D)
    grads["norm1_g"] = _unpack(tot1, offs1, 2)[None]
    grads["norm2_g"] = _unpack(tot1, offs1, 3)[None]
    g_lbl = _unpack(tot1, offs1, 4)
    n_lb = HA // N_CHIP
    grads["hgrn_lb_logits"] = lax.dynamic_slice(g_lbl, (0, 0, chip * n_lb), (2, 2, n_lb))
    grads["hgrn_norm_g"] = _unpack(tot1, offs1, 5)[None]
    grads["na_q_norm_g"] = _unpack(tot1, offs1, 6)[None]
    grads["na_k_norm_g"] = _unpack(tot1, offs1, 7)[None]
    grads["na_rel_bias"] = _unpack(tot1, offs1, 8)[None]
    g_cw = _unpack(tot1, offs1, 9)
    n_f = F // N_CHIP
    grads["ffn_conv_w"] = lax.dynamic_slice(g_cw, (0, chip * n_f), (3, n_f))[None]
    grads["ffn_conv_b"] = _unpack(tot1, offs1, 10)[None]

    g_c_ctx = _dsilu_rows(dact_sel, c_ctx[None, :], "grad_c_ctx")
    grads["c_ctx"] = g_c_ctx[0]

    rs_in = _rs_scatter(rs_in, g_c_ctx)
    big_names = ["ada_w", "w_in", "w_branch_a", "w_branch_b", "w_out", "ffn_w1", "ffn_w3", "ffn_w2"]
    small_names = [n for n in order if n not in big_names]
    delta, new_m, new_v = {}, {}, {}

    def update(nm, after=None):
        d_, m_, v_ = _adamw(weights[nm][0], grads[nm][0], moms[nm][0][0], moms[nm][1][0], "adamw_" + nm, after)
        delta[nm], new_m[nm], new_v[nm] = d_[None], m_[None], v_[None]
        return d_

    last = update("ada_w")
    for nm, g in zip(["ffn_w2", "ffn_w1", "ffn_w3"], _rs_finish(rs_ffn, last)):
        grads[nm] = g[None]
        last = update(nm, last)
    for nm, g in zip(["w_out", "w_branch_a", "w_branch_b"], _rs_finish(rs_mix, last)):
        grads[nm] = g[None]
        last = update(nm, last)
    rs_in = _rs_join(rs_in, last)
    grads["w_in"] = _rs_finish(rs_in, last)[0][None]
    update("w_in")
    pw, offw = _pack([weights[n] for n in small_names])
    pg, _ = _pack([grads[n] for n in small_names])
    pm, _ = _pack([moms[n][0] for n in small_names])
    pv, _ = _pack([moms[n][1] for n in small_names])
    d_, m_, v_ = _adamw(pw, pg, pm, pv, "adamw_small")
    for i, nm in enumerate(small_names):
        delta[nm], new_m[nm], new_v[nm] = _unpack(d_, offw, i), _unpack(m_, offw, i), _unpack(v_, offw, i)

    return (loss, grad_x[None], *[grads[n] for n in order], *[delta[n] for n in order],
            *[new_m[n] for n in order], *[new_v[n] for n in order])


def _dsilu_rows(v, cv, name):
    D = v.shape[1]

    def body(v_ref, c_ref, o_ref):
        t = c_ref[...]
        s = _sigmoid(t)
        o_ref[...] = (((v_ref[0:1, :] + v_ref[1:2, :]) + v_ref[2:3, :]) + v_ref[3:4, :]) * (s * (1.0 + t * (1.0 - s)))

    return _pcall(body, name=name, out_shape=jax.ShapeDtypeStruct((1, D), F32),
                          compiler_params=_params())(v, cv)
```

```python
import functools

import numpy as np
import jax
import jax.numpy as jnp
from jax import lax
from jax.experimental import pallas as pl
from jax.experimental.pallas import tpu as pltpu

F32 = jnp.float32
BF16 = jnp.bfloat16
MESH = pl.DeviceIdType.MESH

HEAD = 128
GRID_W = 64
WIN_R = 8
WIN_C = 16
ROPE_THETA = 10000.0
EPS = 1e-6
N_MOD = 6
CHUNK = 16
ADAM_LR = 0.001
ADAM_B1 = 0.9
ADAM_B2 = 0.999
ADAM_EPS = 1e-08
ADAM_WD = 0.01
ADAM_STEP = 10
NEG = -1e30
VMEM_LIMIT = 56 * 1024 * 1024
N_DEV = 8
N_CHIP = 4
HI = lax.Precision.HIGHEST


def _pick(n, cands):
    for c in cands:
        if n % c == 0:
            return c
    return n


def _row_tile(rows, cols, target_bytes=1 << 20):
    want = max(16, target_bytes // (4 * cols))
    for t in (512, 256, 128, 64, 32, 16, 8):
        if t <= want and rows % t == 0:
            return t
    return rows


def _params(sem=None):
    return pltpu.CompilerParams(dimension_semantics=sem, vmem_limit_bytes=VMEM_LIMIT)


def _dot(a, b):
    return jnp.dot(a, b, preferred_element_type=F32)


def _dot_nt(a, b):
    return lax.dot_general(a, b, (((1,), (1,)), ((), ())), preferred_element_type=F32)


def _dot_tn(a, b):
    return lax.dot_general(a, b, (((0,), (0,)), ((), ())), preferred_element_type=F32)


def _sigmoid(x):
    return 1.0 / (1.0 + jnp.exp(-x))


def _col_tile(n):
    return n if n <= 1536 else _pick(n, (1024, 768, 512, 384, 256, 128))


def _mm_nn(x, w3, out_dtype, name):
    M, K = x.shape
    S, _, n = w3.shape
    tm = _pick(M, (768, 512, 256, 128, 64))
    tn = _col_tile(n)
    nb = n // tn

    def body(x_ref, w_ref, o_ref):
        o_ref[...] = _dot(x_ref[...].astype(BF16), w_ref[0]).astype(o_ref.dtype)

    return _pcall(
        body, name=name, grid=(M // tm, S * nb),
        in_specs=[pl.BlockSpec((tm, K), lambda i, j: (i, 0)),
                  pl.BlockSpec((1, K, tn), lambda i, j: (j // nb, 0, j % nb))],
        out_specs=pl.BlockSpec((tm, tn), lambda i, j: (i, j)),
        out_shape=jax.ShapeDtypeStruct((M, S * n), out_dtype),
        compiler_params=_params(("parallel", "parallel")),
    )(x, w3)


def _mm_nt(dy, w3, out_dtype, name):
    M = dy.shape[0]
    S, K, n = w3.shape
    tm = _pick(M, (768, 512, 256, 128, 64))
    tk = K if K <= 2048 else _pick(K, (1408, 1024, 512, 256, 128))
    tc = _col_tile(n)
    nb = n // tc
    nsteps = S * nb

    def body(dy_ref, w_ref, o_ref, acc_ref):
        s = pl.program_id(2)

        @pl.when(s == 0)
        def _():
            acc_ref[...] = jnp.zeros_like(acc_ref)

        acc_ref[...] += _dot_nt(dy_ref[...].astype(BF16), w_ref[0])

        @pl.when(s == nsteps - 1)
        def _():
            o_ref[...] = acc_ref[...].astype(o_ref.dtype)

    return _pcall(
        body, name=name, grid=(M // tm, K // tk, nsteps),
        in_specs=[pl.BlockSpec((tm, tc), lambda i, k, s: (i, s)),
                  pl.BlockSpec((1, tk, tc), lambda i, k, s: (s // nb, k, s % nb))],
        out_specs=pl.BlockSpec((tm, tk), lambda i, k, s: (i, k)),
        out_shape=jax.ShapeDtypeStruct((M, K), out_dtype),
        scratch_shapes=[pltpu.VMEM((tm, tk), F32)],
        compiler_params=_params(("parallel", "parallel", "arbitrary")),
    )(dy, w3)


def _mm_tn(x, dy, S, name):
    M, K = x.shape
    n = dy.shape[1] // S
    tk = _pick(K, (512, 256, 128))
    tn = _col_tile(n)
    nb = n // tn

    def body(x_ref, dy_ref, o_ref):
        o_ref[0] = _dot_tn(x_ref[...].astype(BF16), dy_ref[...].astype(BF16))

    return _pcall(
        body, name=name, grid=(S * nb, K // tk),
        in_specs=[pl.BlockSpec((M, tk), lambda j, k: (0, k)),
                  pl.BlockSpec((M, tn), lambda j, k: (0, j))],
        out_specs=pl.BlockSpec((1, tk, tn), lambda j, k: (j // nb, k, j % nb)),
        out_shape=jax.ShapeDtypeStruct((S, K, n), F32),
        compiler_params=_params(("parallel", "parallel")),
    )(x, dy)


def _chip_index():
    return (2 * lax.axis_index("x") + lax.axis_index("y")).astype(jnp.int32).reshape(1)


def _cast_bf16_slot(w, name):
    R, C = w.shape
    tr = _row_tile(R, C, 2 << 20)

    def body(j_ref, w_ref, o_ref):
        o_ref[0] = w_ref[...].astype(BF16)

    return _pcall(
        body, name=name,
        grid_spec=pltpu.PrefetchScalarGridSpec(
            num_scalar_prefetch=1, grid=(R // tr,),
            in_specs=[pl.BlockSpec((tr, C), lambda i, j_ref: (i, 0))],
            out_specs=pl.BlockSpec((1, tr, C), lambda i, j_ref: (j_ref[0], i, 0))),
        out_shape=jax.ShapeDtypeStruct((N_CHIP, R, C), BF16),
        compiler_params=_params(("parallel",)),
    )(_chip_index(), w)


def _pos():
    return lax.axis_index("x"), lax.axis_index("y"), lax.axis_index("c")


def _other_chips(x, y):
    return [(x, 1 - y), (1 - x, y), (1 - x, 1 - y)]


def _allgather8(v, name):
    R, C = v.shape

    def body(x_ref, out_ref, send_sems, recv_sems, local_sem):
        x, y, c = _pos()
        me, sibling = (x, y, c), (x, y, 1 - c)
        chips = _other_chips(x, y)

        def slot(px, py, pc):
            return out_ref.at[4 * px + 2 * py + pc]

        def copy(k, block, to, src=None):
            return pltpu.make_async_remote_copy(
                src_ref=slot(*block) if src is None else src, dst_ref=slot(*block),
                send_sem=send_sems.at[k], recv_sem=recv_sems.at[k], device_id=to, device_id_type=MESH)

        mine = pltpu.make_async_copy(x_ref, slot(*me), local_sem)
        mine.start()
        first = [copy(0, me, sibling, src=x_ref)]
        first += [copy(1 + j, me, (*chip, c), src=x_ref) for j, chip in enumerate(chips)]
        for cp in first:
            cp.start()
        passed = [copy(4 + j, (*chip, c), sibling) for j, chip in enumerate(chips)]
        for j, chip in enumerate(chips):
            copy(1 + j, (*chip, c), me).wait_recv()
            passed[j].start()
        copy(0, sibling, me).wait_recv()
        for j, chip in enumerate(chips):
            copy(4 + j, (*chip, 1 - c), me).wait_recv()
        for cp in first + passed:
            cp.wait_send()
        mine.wait()

    return _pcall(
        body, name=name,
        out_shape=jax.ShapeDtypeStruct((N_DEV, R, C), v.dtype),
        in_specs=[pl.BlockSpec(memory_space=pltpu.VMEM)],
        out_specs=pl.BlockSpec(memory_space=pltpu.VMEM),
        scratch_shapes=[pltpu.SemaphoreType.DMA((7,)), pltpu.SemaphoreType.DMA((7,)), pltpu.SemaphoreType.DMA],
        compiler_params=pltpu.CompilerParams(vmem_limit_bytes=VMEM_LIMIT),
    )(v)


_HBM = pl.BlockSpec(memory_space=pltpu.HBM)
_SEM = pl.BlockSpec(memory_space=pltpu.SEMAPHORE)
_ANY = pl.BlockSpec(memory_space=pl.ANY)
_EFFECT = pltpu.SideEffectType.DATAFLOW_SIDE_EFFECTING
_PENDING = []


def _pcall(body, **kw):
    def run(*operands):
        if not _PENDING or "in_specs" not in kw:
            return pl.pallas_call(body, **kw)(*operands)
        deps = list(_PENDING)
        n = len(operands)

        def tied(*refs):
            return body(*refs[:n], *refs[n + len(deps):])

        return pl.pallas_call(tied, **{**kw, "in_specs": list(kw["in_specs"]) + [_ANY] * len(deps)})(*operands, *deps)
    return run


def _copies(plan, refs, send_sems, recv_sems):
    return [pltpu.make_async_remote_copy(src_ref=src, dst_ref=dst, send_sem=send_sems.at[k], recv_sem=recv_sems.at[k],
                                         device_id=dev, device_id_type=MESH)
            for k, (src, dst, dev) in enumerate(plan(refs))]


def _xfer_start(name, bufs, plan, n_copies, after=None):
    n = len(bufs)
    deps = list(_PENDING) + ([after] if after is not None else [])
    nd = len(deps)

    def body(*refs):
        for cp in _copies(plan, refs[:n], refs[n + nd], refs[n + nd + 1]):
            cp.start()
        refs[-1][...] = jnp.zeros_like(refs[-1])

    outs = pl.pallas_call(
        body, name=name,
        out_shape=(pltpu.SemaphoreType.DMA((n_copies,)), pltpu.SemaphoreType.DMA((n_copies,)),
                   *[pltpu.HBM(b.shape, b.dtype) for b in bufs], jax.ShapeDtypeStruct((8, 128), F32)),
        in_specs=[_HBM] * n + [_ANY] * nd,
        out_specs=(_SEM, _SEM, *[_HBM] * n, pl.BlockSpec(memory_space=pltpu.VMEM)),
        input_output_aliases={t: 2 + t for t in range(n)},
        compiler_params=pltpu.CompilerParams(has_side_effects=_EFFECT),
    )(*[pltpu.with_memory_space_constraint(b, pltpu.HBM) for b in bufs], *deps)
    _PENDING[:] = [outs[-1]]
    return (outs[0], outs[1]), list(outs[2:2 + n])


def _xfer_wait(name, sems, bufs, plan, after):
    n = len(bufs)

    def body(*refs):
        cps = _copies(plan, refs[:n], refs[n], refs[n + 1])
        for cp in cps:
            cp.wait_send()
        for cp in cps:
            cp.wait_recv()

    outs = pl.pallas_call(
        body, name=name,
        out_shape=tuple(pltpu.HBM(b.shape, b.dtype) for b in bufs),
        in_specs=[_HBM] * n + [_SEM, _SEM, _ANY],
        out_specs=tuple([_HBM] * n),
        input_output_aliases={t: t for t in range(n)},
        compiler_params=pltpu.CompilerParams(has_side_effects=_EFFECT),
    )(*bufs, sems[0], sems[1], after)
    return list(outs)


def _half(ref_rows, hc):
    h = ref_rows // 2
    return pl.ds(hc * h, h)


def _plan_gather_ici(bufs):
    x, y, c = _pos()
    j = 2 * x + y
    return [(b.at[j, _half(b.shape[1], c)], b.at[j, _half(b.shape[1], c)], (*chip, c))
            for b in bufs for chip in _other_chips(x, y)]


def _plan_gather_d2d(bufs):
    x, y, c = _pos()
    out = []
    for b in bufs:
        for chip in _other_chips(x, y):
            blk = b.at[2 * chip[0] + chip[1], _half(b.shape[1], c)]
            out.append((blk, blk, (x, y, 1 - c)))
    return out


def _plan_pair_swap(n):
    def plan(bufs):
        x, y, c = _pos()
        return [(g.at[:, _half(g.shape[1], 1 - c)], land, (x, y, 1 - c)) for g, land in zip(bufs[:n], bufs[n:])]
    return plan


def _plan_chip_scatter(n):
    def plan(bufs):
        x, y, c = _pos()
        return [(p.at[2 * chip[0] + chip[1]], land.at[k], (*chip, c))
                for p, land in zip(bufs[:n], bufs[n:]) for k, chip in enumerate(_other_chips(x, y))]
    return plan


def _plan_pair_join(bufs):
    x, y, c = _pos()
    return [(b.at[_half(b.shape[0], c)], b.at[_half(b.shape[0], c)], (x, y, 1 - c)) for b in bufs]


def _empty_hbm(shape, dtype):
    return pltpu.with_memory_space_constraint(lax.empty(shape, dtype), pltpu.HBM)


def _gather_start(tag, bufs, after=None):
    sems, bufs = _xfer_start(f"gather_ici_start_{tag}", bufs, _plan_gather_ici, 3 * len(bufs), after)
    return dict(tag=tag, sems=sems, bufs=bufs)


def _gather_mid(st, after):
    tag = st["tag"]
    bufs = _xfer_wait(f"gather_ici_wait_{tag}", st["sems"], st["bufs"], _plan_gather_ici, after)
    sems, bufs = _xfer_start(f"gather_d2d_start_{tag}", bufs, _plan_gather_d2d, 3 * len(bufs))
    return dict(tag=tag, sems=sems, bufs=bufs)


def _gather_finish(st, after):
    return _xfer_wait(f"gather_d2d_wait_{st['tag']}", st["sems"], st["bufs"], _plan_gather_d2d, after)


def _pair_add(g, r, name):
    S, R, C = g.shape
    h = R // 2
    tr = _row_tile(h, C)
    nb = h // tr

    def body(c_ref, g_ref, r_ref, o_ref):
        o_ref[...] = (g_ref[...] + r_ref[...]).astype(BF16)

    return _pcall(
        body, name=name,
        grid_spec=pltpu.PrefetchScalarGridSpec(
            num_scalar_prefetch=1, grid=(S, nb),
            in_specs=[pl.BlockSpec((1, tr, C), lambda s, i, c_ref: (s, c_ref[0] * nb + i, 0)),
                      pl.BlockSpec((1, tr, C), lambda s, i, c_ref: (s, i, 0))],
            out_specs=pl.BlockSpec((1, tr, C), lambda s, i, c_ref: (s, i, 0))),
        out_shape=jax.ShapeDtypeStruct((S, h, C), BF16),
        compiler_params=_params(("parallel", "parallel")),
    )(lax.axis_index("c").astype(jnp.int32).reshape(1), g, r)


def _chip_sum(p, rb, name):
    S, h, C = p.shape
    tr = _row_tile(h, C)
    nb = h // tr
    jc = jnp.concatenate([_chip_index(), lax.axis_index("c").astype(jnp.int32).reshape(1)])

    def body(jc_ref, p_ref, r_ref, o_ref):
        o_ref[...] = ((p_ref[0].astype(F32) + r_ref[0].astype(F32)) + r_ref[1].astype(F32)) + r_ref[2].astype(F32)

    return _pcall(
        body, name=name,
        grid_spec=pltpu.PrefetchScalarGridSpec(
            num_scalar_prefetch=1, grid=(nb,),
            in_specs=[pl.BlockSpec((1, tr, C), lambda i, jc_ref: (jc_ref[0], i, 0)),
                      pl.BlockSpec((3, tr, C), lambda i, jc_ref: (0, i, 0))],
            out_specs=pl.BlockSpec((tr, C), lambda i, jc_ref: (jc_ref[1] * nb + i, 0))),
        out_shape=jax.ShapeDtypeStruct((2 * h, C), F32),
        compiler_params=_params(("parallel",)),
    )(jc, p, rb)


def _rs_start(tag, gs):
    n = len(gs)
    lands = [_empty_hbm((g.shape[0], g.shape[1] // 2, g.shape[2]), g.dtype) for g in gs]
    sems, bufs = _xfer_start(f"rs_swap_start_{tag}", list(gs) + lands, _plan_pair_swap(n), n)
    return dict(tag=tag, n=n, sems=sems, bufs=bufs)


def _rs_scatter(st, after):
    tag, n = st["tag"], st["n"]
    bufs = _xfer_wait(f"rs_swap_wait_{tag}", st["sems"], st["bufs"], _plan_pair_swap(n), after)
    ps = [_pair_add(g, r, f"rs_pair_add_{tag}{t}") for t, (g, r) in enumerate(zip(bufs[:n], bufs[n:]))]
    lands = [_empty_hbm((3,) + p.shape[1:], p.dtype) for p in ps]
    sems, bufs = _xfer_start(f"rs_scatter_start_{tag}", ps + lands, _plan_chip_scatter(n), 3 * n)
    return dict(tag=tag, n=n, sems=sems, bufs=bufs)


def _rs_join(st, after):
    tag, n = st["tag"], st["n"]
    bufs = _xfer_wait(f"rs_scatter_wait_{tag}", st["sems"], st["bufs"], _plan_chip_scatter(n), after)
    fs = [_chip_sum(p, rb, f"rs_chip_sum_{tag}{t}") for t, (p, rb) in enumerate(zip(bufs[:n], bufs[n:]))]
    sems, bufs = _xfer_start(f"rs_join_start_{tag}", fs, _plan_pair_join, n)
    return dict(tag=tag, n=n, sems=sems, bufs=bufs)


def _rs_finish(st, after):
    return _xfer_wait(f"rs_join_wait_{st['tag']}", st["sems"], st["bufs"], _plan_pair_join, after)


def _sum8(g, name):
    _, R, C = g.shape

    def body(g_ref, o_ref):
        acc = g_ref[0]
        for d in range(1, N_DEV):
            acc = acc + g_ref[d]
        o_ref[...] = acc

    return _pcall(body, name=name, out_shape=jax.ShapeDtypeStruct((R, C), F32),
                          compiler_params=_params())(g)


def _ada_fwd(cs, w, b):
    D, n = w.shape
    tn = _pick(n, (512, 384, 256, 128))

    def body(c_ref, w_ref, b_ref, o_ref):
        cv = c_ref[...]
        a = (cv * _sigmoid(cv)).astype(BF16)
        o_ref[...] = _dot(a, w_ref[...].astype(BF16)) + b_ref[...]

    return _pcall(
        body, name="ada_fwd", grid=(n // tn,),
        in_specs=[pl.BlockSpec((16, D), lambda j: (0, 0)), pl.BlockSpec((D, tn), lambda j: (0, j)),
                  pl.BlockSpec((1, tn), lambda j: (0, j))],
        out_specs=pl.BlockSpec((16, tn), lambda j: (0, j)),
        out_shape=jax.ShapeDtypeStruct((16, n), F32),
        compiler_params=_params(("parallel",)),
    )(cs, w, b)


def _ada_bwd(cs, w, dmod):
    D, n = w.shape
    tn = _pick(n, (512, 384, 256, 128))

    def body(c_ref, w_ref, d_ref, gw_ref, da_ref):
        j = pl.program_id(0)
        cv = c_ref[...]
        a = cv * _sigmoid(cv)
        d = d_ref[...]
        gw_ref[...] = lax.dot_general(a, d, (((0,), (0,)), ((), ())), precision=HI, preferred_element_type=F32)

        @pl.when(j == 0)
        def _():
            da_ref[...] = jnp.zeros_like(da_ref)

        da_ref[...] += _dot_nt(d.astype(BF16), w_ref[...].astype(BF16))

    return _pcall(
        body, name="ada_bwd", grid=(n // tn,),
        in_specs=[pl.BlockSpec((16, D), lambda j: (0, 0)), pl.BlockSpec((D, tn), lambda j: (0, j)),
                  pl.BlockSpec((16, tn), lambda j: (0, j))],
        out_specs=[pl.BlockSpec((D, tn), lambda j: (0, j)), pl.BlockSpec((16, D), lambda j: (0, 0))],
        out_shape=[jax.ShapeDtypeStruct((D, n), F32), jax.ShapeDtypeStruct((16, D), F32)],
        compiler_params=_params(("arbitrary",)),
    )(cs, w, dmod)


def _rms1_fwd(xall, gain, shift2, scale2, n_ctx):
    T, D = xall.shape
    tb = _pick(n_ctx, (256, 128, 64, 32, 16))
    nctx = n_ctx // tb

    def body(x_ref, g_ref, sh_ref, sc_ref, o_ref):
        i = pl.program_id(0)
        xv = x_ref[...]
        r = lax.rsqrt(jnp.mean(xv * xv, axis=-1, keepdims=True) + EPS)
        nrm = xv * r * g_ref[...]
        lat = i >= nctx
        sh = jnp.where(lat, sh_ref[1:2, :], sh_ref[0:1, :])
        sc = jnp.where(lat, sc_ref[1:2, :], sc_ref[0:1, :])
        o_ref[...] = (nrm * (1.0 + sc) + sh).astype(BF16)

    vec = lambda r: pl.BlockSpec((r, D), lambda i: (0, 0))
    return _pcall(
        body, name="rms1_fwd", grid=(T // tb,),
        in_specs=[pl.BlockSpec((tb, D), lambda i: (i, 0)), vec(1), vec(2), vec(2)],
        out_specs=pl.BlockSpec((tb, D), lambda i: (i, 0)),
        out_shape=jax.ShapeDtypeStruct((T, D), BF16),
        compiler_params=_params(("parallel",)),
    )(xall, gain, shift2, scale2)


def _rms1_bwd(xall, dh, dxmid, gain, scale2, n_ctx):
    T, D = xall.shape
    L = T - n_ctx
    tb = _pick(n_ctx, (256, 128, 64, 32, 16))
    nctx = n_ctx // tb

    def body(x_ref, dh_ref, dxm_ref, g_ref, sc_ref, dx_ref, cs_ref):
        i = pl.program_id(0)
        lat = i >= nctx
        xv = x_ref[...]
        r = lax.rsqrt(jnp.mean(xv * xv, axis=-1, keepdims=True) + EPS)
        xh = xv * r
        g = g_ref[...]
        nrm = xh * g
        sc = jnp.where(lat, sc_ref[1:2, :], sc_ref[0:1, :])
        dhv = dh_ref[...]
        dn = dhv * (1.0 + sc)
        dxh = dn * g
        dxv = r * (dxh - xh * jnp.mean(dxh * xh, axis=-1, keepdims=True))
        s_sh = jnp.sum(dhv, axis=0, keepdims=True)
        s_sc = jnp.sum(dhv * nrm, axis=0, keepdims=True)
        s_g = jnp.sum(dn * xh, axis=0, keepdims=True)
        zero = jnp.zeros_like(s_sh)
        rows = lax.broadcasted_iota(jnp.int32, (8, D), 0)
        upd = jnp.where(rows == 0, jnp.where(lat, zero, s_sh),
              jnp.where(rows == 1, jnp.where(lat, zero, s_sc),
              jnp.where(rows == 2, jnp.where(lat, s_sh, zero),
              jnp.where(rows == 3, jnp.where(lat, s_sc, zero),
              jnp.where(rows == 4, s_g, 0.0)))))

        @pl.when(i == 0)
        def _():
            cs_ref[...] = jnp.zeros_like(cs_ref)

        cs_ref[...] += upd

        @pl.when(lat)
        def _():
            dx_ref[...] = dxv + dxm_ref[...]

    lat_blk = lambda i: (jnp.maximum(i - nctx, 0), 0)
    vec = lambda r: pl.BlockSpec((r, D), lambda i: (0, 0))
    return _pcall(
        body, name="rms1_bwd", grid=(T // tb,),
        in_specs=[pl.BlockSpec((tb, D), lambda i: (i, 0)), pl.BlockSpec((tb, D), lambda i: (i, 0)),
                  pl.BlockSpec((tb, D), lat_blk), vec(1), vec(2)],
        out_specs=[pl.BlockSpec((tb, D), lat_blk), vec(8)],
        out_shape=[jax.ShapeDtypeStruct((L, D), F32), jax.ShapeDtypeStruct((8, D), F32)],
        compiler_params=_params(("arbitrary",)),
    )(xall, dh, dxmid, gain, scale2)


def _resid_rms2_fwd(x, mo, vecs):
    L, D = x.shape
    tb = _pick(L, (256, 128, 64))

    def body(x_ref, mo_ref, v_ref, xm_ref, h_ref):
        xm = x_ref[...] + v_ref[0:1, :] * mo_ref[...]
        xm_ref[...] = xm
        r = lax.rsqrt(jnp.mean(xm * xm, axis=-1, keepdims=True) + EPS)
        h_ref[...] = (xm * r * v_ref[1:2, :] * (1.0 + v_ref[3:4, :]) + v_ref[2:3, :]).astype(BF16)

    blk = pl.BlockSpec((tb, D), lambda i: (i, 0))
    return _pcall(
        body, name="resid_rms2_fwd", grid=(L // tb,),
        in_specs=[blk, blk, pl.BlockSpec((8, D), lambda i: (0, 0))],
        out_specs=[blk, blk],
        out_shape=[jax.ShapeDtypeStruct((L, D), F32), jax.ShapeDtypeStruct((L, D), BF16)],
        compiler_params=_params(("parallel",)),
    )(x, mo, vecs)


def _resid_rms2_bwd(xmid, dh_a, dh_b, dy, mo, vecs):
    L, D = xmid.shape
    tb = _pick(L, (256, 128, 64))

    def body(xm_ref, da_ref, db_ref, dy_ref, mo_ref, v_ref, dxm_ref, dmo_ref, cs_ref):
        i = pl.program_id(0)
        xm = xm_ref[...]
        r = lax.rsqrt(jnp.mean(xm * xm, axis=-1, keepdims=True) + EPS)
        xh = xm * r
        g = v_ref[1:2, :]
        nrm = xh * g
        dhv = da_ref[...] + db_ref[...]
        dn = dhv * (1.0 + v_ref[3:4, :])
        dxh = dn * g
        dxm = dy_ref[...] + r * (dxh - xh * jnp.mean(dxh * xh, axis=-1, keepdims=True))
        dxm_ref[...] = dxm
        dmo_ref[...] = (dxm * v_ref[0:1, :]).astype(BF16)
        s0 = jnp.sum(dhv, axis=0, keepdims=True)
        s1 = jnp.sum(dhv * nrm, axis=0, keepdims=True)
        s2 = jnp.sum(dn * xh, axis=0, keepdims=True)
        s3 = jnp.sum(dxm * mo_ref[...], axis=0, keepdims=True)
        rows = lax.broadcasted_iota(jnp.int32, (8, D), 0)
        upd = jnp.where(rows == 0, s0, jnp.where(rows == 1, s1, jnp.where(rows == 2, s2,
              jnp.where(rows == 3, s3, 0.0))))

        @pl.when(i == 0)
        def _():
            cs_ref[...] = jnp.zeros_like(cs_ref)

        cs_ref[...] += upd

    blk = pl.BlockSpec((tb, D), lambda i: (i, 0))
    vec = pl.BlockSpec((8, D), lambda i: (0, 0))
    return _pcall(
        body, name="resid_rms2_bwd", grid=(L // tb,),
        in_specs=[blk, blk, blk, blk, blk, vec],
        out_specs=[blk, blk, vec],
        out_shape=[jax.ShapeDtypeStruct((L, D), F32), jax.ShapeDtypeStruct((L, D), BF16),
                   jax.ShapeDtypeStruct((8, D), F32)],
        compiler_params=_params(("arbitrary",)),
    )(xmid, dh_a, dh_b, dy, mo, vecs)


def _loss_head(xmid, f, g2, target):
    L, D = xmid.shape
    tb = _pick(L, (256, 128, 64))

    def body(xm_ref, f_ref, g_ref, t_ref, dy_ref, df_ref, s_ref):
        i = pl.program_id(0)
        fv = f_ref[...]
        g = g_ref[...]
        err = xm_ref[...] + g * fv - t_ref[...]
        dy = err * (1.0 / D)
        dy_ref[...] = dy
        df_ref[...] = (dy * g).astype(BF16)
        s0 = jnp.sum(dy * fv, axis=0, keepdims=True)
        part = 0.5 * jnp.sum(jnp.mean(err * err, axis=-1, keepdims=True), axis=0, keepdims=True)
        rows = lax.broadcasted_iota(jnp.int32, (8, D), 0)
        upd = jnp.where(rows == 0, s0, jnp.where(rows == 1, part, 0.0))

        @pl.when(i == 0)
        def _():
            s_ref[...] = jnp.zeros_like(s_ref)

        s_ref[...] += upd

    blk = pl.BlockSpec((tb, D), lambda i: (i, 0))
    return _pcall(
        body, name="loss_head", grid=(L // tb,),
        in_specs=[blk, blk, pl.BlockSpec((1, D), lambda i: (0, 0)), blk],
        out_specs=[blk, blk, pl.BlockSpec((8, D), lambda i: (0, 0))],
        out_shape=[jax.ShapeDtypeStruct((L, D), F32), jax.ShapeDtypeStruct((L, D), BF16),
                   jax.ShapeDtypeStruct((8, D), F32)],
        compiler_params=_params(("arbitrary",)),
    )(xmid, f, g2, target)


def _gate_cols(D, off):
    tc = _pick(np.gcd(D, off), (512, 256, 128))
    return tc, off // tc


def _merge_fwd(za, zb, p, n_ctx, off_a, off_b):
    L, D = za.shape
    tb = _pick(n_ctx, (256, 128, 64, 32, 16))
    nctx = n_ctx // tb
    tc, oa = _gate_cols(D, off_a)
    _, ob = _gate_cols(D, off_b)
    if off_b % tc:
        raise ValueError("gate column offsets must share a column tile")
    ob = off_b // tc

    def body(za_ref, zb_ref, ga_ref, gb_ref, z_ref):
        z_ref[...] = (_sigmoid(ga_ref[...]) * za_ref[...] + _sigmoid(gb_ref[...]) * zb_ref[...]).astype(BF16)

    blk = pl.BlockSpec((tb, tc), lambda i, j: (i, j))
    return _pcall(
        body, name="merge_fwd", grid=(L // tb, D // tc),
        in_specs=[blk, blk, pl.BlockSpec((tb, tc), lambda i, j: (i + nctx, oa + j)),
                  pl.BlockSpec((tb, tc), lambda i, j: (i + nctx, ob + j))],
        out_specs=blk,
        out_shape=jax.ShapeDtypeStruct((L, D), BF16),
        compiler_params=_params(("parallel", "parallel")),
    )(za, zb, p, p)


def _merge_bwd(dz, za, zb, p, n_ctx, off_a, off_b):
    L, D = za.shape
    T = L + n_ctx
    tb = _pick(n_ctx, (256, 128, 64, 32, 16))
    nctx = n_ctx // tb
    tc = _gate_cols(D, off_a)[0]
    oa, ob = off_a // tc, off_b // tc

    def body(dz_ref, za_ref, zb_ref, ga_ref, gb_ref, dza_ref, dzb_ref, dga_ref, dgb_ref):
        i = pl.program_id(1)

        @pl.when(i < nctx)
        def _():
            dga_ref[...] = jnp.zeros_like(dga_ref)
            dgb_ref[...] = jnp.zeros_like(dgb_ref)

        @pl.when(i >= nctx)
        def _():
            dzv = dz_ref[...]
            sa = _sigmoid(ga_ref[...])
            sb = _sigmoid(gb_ref[...])
            dza_ref[...] = (dzv * sa).astype(BF16)
            dzb_ref[...] = (dzv * sb).astype(BF16)
            dga_ref[...] = (dzv * za_ref[...] * sa * (1.0 - sa)).astype(BF16)
            dgb_ref[...] = (dzv * zb_ref[...] * sb * (1.0 - sb)).astype(BF16)

    lat = pl.BlockSpec((tb, tc), lambda j, i: (jnp.maximum(i - nctx, 0), j))
    allr = pl.BlockSpec((tb, tc), lambda j, i: (i, j))
    return _pcall(
        body, name="merge_bwd", grid=(D // tc, T // tb),
        in_specs=[lat, lat, lat, pl.BlockSpec((tb, tc), lambda j, i: (i, oa + j)),
                  pl.BlockSpec((tb, tc), lambda j, i: (i, ob + j))],
        out_specs=[lat, lat, allr, allr],
        out_shape=[jax.ShapeDtypeStruct((L, D), BF16), jax.ShapeDtypeStruct((L, D), BF16),
                   jax.ShapeDtypeStruct((T, D), BF16), jax.ShapeDtypeStruct((T, D), BF16)],
        compiler_params=_params(("arbitrary", "arbitrary")),
    )(dz, za, zb, p, p)


def _shift_down(u, rows):
    return jnp.where(rows == 0, 0.0, pltpu.roll(u, 1, 0))


def _shift_up(u, rows):
    n = u.shape[0]
    return jnp.where(rows == n - 1, 0.0, pltpu.roll(u, n - 1, 0))


def _convgate_fwd(u1, u3, cw, cb):
    L, F = u1.shape
    tc = _pick(F, (256, 128))

    def body(u1_ref, u3_ref, w_ref, b_ref, a_ref):
        u = u1_ref[...]
        rows = lax.broadcasted_iota(jnp.int32, u.shape, 0)
        cv = _shift_down(u, rows) * w_ref[0:1, :] + u * w_ref[1:2, :] + _shift_up(u, rows) * w_ref[2:3, :] + b_ref[...]
        a_ref[...] = (cv * _sigmoid(cv) * u3_ref[...]).astype(BF16)

    blk = pl.BlockSpec((L, tc), lambda j: (0, j))
    return _pcall(
        body, name="convgate_fwd", grid=(F // tc,),
        in_specs=[blk, blk, pl.BlockSpec((8, tc), lambda j: (0, j)), pl.BlockSpec((1, tc), lambda j: (0, j))],
        out_specs=blk,
        out_shape=jax.ShapeDtypeStruct((L, F), BF16),
        compiler_params=_params(("parallel",)),
    )(u1, u3, cw, cb)


def _convgate_bwd(u1, u3, da, cw, cb):
    L, F = u1.shape
    tc = _pick(F, (256, 128))

    def body(u1_ref, u3_ref, da_ref, w_ref, b_ref, du1_ref, du3_ref, s_ref):
        u = u1_ref[...]
        rows = lax.broadcasted_iota(jnp.int32, u.shape, 0)
        um, up = _shift_down(u, rows), _shift_up(u, rows)
        w0, w1, w2 = w_ref[0:1, :], w_ref[1:2, :], w_ref[2:3, :]
        cv = um * w0 + u * w1 + up * w2 + b_ref[...]
        s = _sigmoid(cv)
        dav = da_ref[...]
        du3_ref[...] = (dav * cv * s).astype(BF16)
        dcv = dav * u3_ref[...] * (s * (1.0 + cv * (1.0 - s)))
        du1_ref[...] = (_shift_up(dcv, rows) * w0 + dcv * w1 + _shift_down(dcv, rows) * w2).astype(BF16)
        r8 = lax.broadcasted_iota(jnp.int32, (8, tc), 0)
        s0 = jnp.sum(dcv * um, axis=0, keepdims=True)
        s1 = jnp.sum(dcv * u, axis=0, keepdims=True)
        s2 = jnp.sum(dcv * up, axis=0, keepdims=True)
        s3 = jnp.sum(dcv, axis=0, keepdims=True)
        s_ref[...] = jnp.where(r8 == 0, s0, jnp.where(r8 == 1, s1, jnp.where(r8 == 2, s2,
                     jnp.where(r8 == 3, s3, 0.0))))

    blk = pl.BlockSpec((L, tc), lambda j: (0, j))
    v8 = pl.BlockSpec((8, tc), lambda j: (0, j))
    return _pcall(
        body, name="convgate_bwd", grid=(F // tc,),
        in_specs=[blk, blk, blk, v8, pl.BlockSpec((1, tc), lambda j: (0, j))],
        out_specs=[blk, blk, v8],
        out_shape=[jax.ShapeDtypeStruct((L, F), BF16), jax.ShapeDtypeStruct((L, F), BF16),
                   jax.ShapeDtypeStruct((8, F), F32)],
        compiler_params=_params(("parallel",)),
    )(u1, u3, da, cw, cb)


def _lower_bound(lbl_ref, d):
    l0, l1 = lbl_ref[d, 0:1, :], lbl_ref[d, 1:2, :]
    m = jnp.maximum(l0, l1)
    e0, e1 = jnp.exp(l0 - m), jnp.exp(l1 - m)
    return e0 / (e0 + e1)


def _chunk_cumsum(x, rev):
    n = x.shape[0]
    r = lax.broadcasted_iota(jnp.int32, x.shape, 0) % CHUNK
    k = 1
    while k < CHUNK:
        if rev:
            x = x + jnp.where(r < CHUNK - k, pltpu.roll(x, n - k, 0), 0.0)
        else:
            x = x + jnp.where(r >= k, pltpu.roll(x, k, 0), 0.0)
        k *= 2
    return x


def _gate_terms(z, lb):
    sg = _sigmoid(z)
    f = lb + (1.0 - lb) * sg
    return sg, f


def _decay_terms(z, lb, rev):
    _, f = _gate_terms(z, lb)
    g = jnp.log(f)
    return 1.0 - f, _chunk_cumsum(g, rev), _chunk_cumsum(g, not rev) - g


def _chunk_total(c, rev):
    return c[0:1, :] if rev else c[CHUNK - 1:CHUNK, :]


def _pair_decay(c, s, rev):
    t = lax.broadcasted_iota(jnp.int32, (CHUNK, 1), 0)
    later = (t <= s) if rev else (t >= s)
    return jnp.where(later, jnp.exp(jnp.minimum(c - c[s:s + 1, :], 0.0)), 0.0)


def _scan_chunk(i, n_ctx_chunks, n_chunks, rev):
    if not rev:
        return i
    return jnp.where(i < n_ctx_chunks, n_ctx_chunks - 1 - i, n_chunks + n_ctx_chunks - 1 - i)


def _rows(ci):
    return pl.ds(pl.multiple_of(ci * CHUNK, CHUNK), CHUNK)


def _hgrn_cols(HA):
    return HA // HEAD


def _hgrn_fwd(p, lbl, ng, n_ctx, HA):
    T = p.shape[0]
    L = T - n_ctx
    nh = _hgrn_cols(HA)
    nc, ncc = T // CHUNK, n_ctx // CHUNK

    def body(q_ref, zf_ref, zb_ref, v_ref, og_ref, lbl_ref, ng_ref, ya_ref, o_ref, st_ref,
             c_scr, k_scr, qe_scr, ke_scr, o_scr):
        dirs = ((0, False, zf_ref), (1, True, zb_ref))
        for d, rev, z_ref in dirs:
            k, c, rest = _decay_terms(z_ref[...], _lower_bound(lbl_ref, d), rev)
            c_scr[d] = c
            k_scr[d] = k
            qe_scr[d] = (q_ref[...] * jnp.exp(c)).astype(BF16)
            ke_scr[d] = (k * jnp.exp(rest)).astype(BF16)

        def step(i, states):
            new = []
            for (d, rev, _), St in zip(dirs, states):
                ci = _scan_chunk(i, ncc, nc, rev)
                rows = _rows(ci)
                q, v, c, k = q_ref[rows, :], v_ref[rows, :], c_scr[d, rows, :], k_scr[d, rows, :]
                st_ref[0, d, ci] = St.astype(BF16)
                o = _dot_nt(qe_scr[d, rows, :], St.astype(BF16))
                for s in range(CHUNK):
                    E = _pair_decay(c, s, rev)
                    a = jnp.sum(q * E * k[s:s + 1, :], axis=1, keepdims=True)
                    o = o + a * v[s:s + 1, :]
                o_scr[d, rows, :] = o
                new.append(St * jnp.exp(_chunk_total(c, rev)) + _dot_tn(v.astype(BF16), ke_scr[d, rows, :]))
            return tuple(new)

        zero = jnp.zeros((HEAD, HEAD), F32)
        lax.fori_loop(0, nc, step, (zero, zero), unroll=2)

        o = o_scr[0, pl.ds(n_ctx, L), :] + o_scr[1, pl.ds(n_ctx, L), :]
        o_ref[...] = o
        r = lax.rsqrt(jnp.mean(o * o, axis=-1, keepdims=True) + EPS)
        og = og_ref[pl.ds(n_ctx, L), :]
        ya_ref[...] =(o * r * ng_ref[...] * (og * _sigmoid(og))).astype(BF16)

    cb = HA // HEAD
    col = lambda kk: pl.BlockSpec((T, HEAD), lambda h: (0, kk * cb + h))
    return _pcall(
        body, name="hgrn_fwd", grid=(nh,),
        in_specs=[col(0), col(1), col(2), col(3), col(4),
                  pl.BlockSpec((2, 2, HEAD), lambda h: (0, 0, h)), pl.BlockSpec((1, HEAD), lambda h: (0, 0))],
        out_specs=[pl.BlockSpec((L, HEAD), lambda h: (0, h)), pl.BlockSpec((L, HEAD), lambda h: (0, h)),
                   pl.BlockSpec((1, 2, nc, HEAD, HEAD), lambda h: (h, 0, 0, 0, 0))],
        out_shape=[jax.ShapeDtypeStruct((L, HA), BF16), jax.ShapeDtypeStruct((L, HA), F32),
                   jax.ShapeDtypeStruct((nh, 2, nc, HEAD, HEAD), BF16)],
        scratch_shapes=[pltpu.VMEM((2, T, HEAD), F32), pltpu.VMEM((2, T, HEAD), F32),
                        pltpu.VMEM((2, T, HEAD), BF16), pltpu.VMEM((2, T, HEAD), BF16),
                        pltpu.VMEM((2, T, HEAD), F32)],
        compiler_params=_params(("parallel",)),
    )(p, p, p, p, p, lbl, ng)


def _hgrn_bwd(p, lbl, ng, o, dya, st, n_ctx, HA):
    T = p.shape[0]
    L = T - n_ctx
    nh = _hgrn_cols(HA)
    nc, ncc = T // CHUNK, n_ctx // CHUNK

    def body(q_ref, zf_ref, zb_ref, v_ref, og_ref, lbl_ref, ng_ref, o_ref, dya_ref, st_ref,
             dq_ref, dzf_ref, dzb_ref, dv_ref, dog_ref, dlbl_ref, dng_ref,
             do_scr, c_scr, k_scr, qe_scr, ke_scr, dg_scr, dk_scr, dq_scr, dv_scr):
        h = pl.program_id(0)
        ov = o_ref[...]
        r = lax.rsqrt(jnp.mean(ov * ov, axis=-1, keepdims=True) + EPS)
        oh = ov * r
        ogv = og_ref[pl.ds(n_ctx, L), :]
        sg_o = _sigmoid(ogv)
        dyv = dya_ref[...]
        ngv = ng_ref[...]
        dog_ref[pl.ds(0, n_ctx), :] = jnp.zeros((n_ctx, HEAD), BF16)
        dog_ref[pl.ds(n_ctx, L), :] = (dyv * oh * ngv * (sg_o * (1.0 + ogv * (1.0 - sg_o)))).astype(BF16)
        don = dyv * (ogv * sg_o)
        dng = jnp.sum(don * oh, axis=0, keepdims=True)
        doh = don * ngv
        do_scr[pl.ds(0, n_ctx), :] = jnp.zeros((n_ctx, HEAD), F32)
        do_scr[pl.ds(n_ctx, L), :] = r * (doh - oh * jnp.mean(doh * oh, axis=-1, keepdims=True))

        @pl.when(h == 0)
        def _():
            dng_ref[...] = jnp.zeros_like(dng_ref)

        dng_ref[0:1, :] += dng

        t16 = lax.broadcasted_iota(jnp.int32, (CHUNK, HEAD), 0)
        dirs = ((0, False, zf_ref, dzf_ref), (1, True, zb_ref, dzb_ref))
        for d, rev, z_ref, _ in dirs:
            k, c, rest = _decay_terms(z_ref[...], _lower_bound(lbl_ref, d), rev)
            c_scr[d] = c
            k_scr[d] = k
            qe_scr[d] = (q_ref[...] * jnp.exp(c)).astype(BF16)
            ke_scr[d] = (k * jnp.exp(rest)).astype(BF16)
        dq_scr[...] = jnp.zeros_like(dq_scr)
        dv_scr[...] = jnp.zeros_like(dv_scr)

        zero = jnp.zeros((HEAD, HEAD), F32)

        def bwd_step(ii, carry):
            i = nc - 1 - ii
            new = []
            for (d, rev, _, _), dSt in zip(dirs, carry):
                ci = _scan_chunk(i, ncc, nc, rev)
                rows = _rows(ci)
                q, v, do = q_ref[rows, :], v_ref[rows, :], do_scr[rows, :]
                c, k = c_scr[d, rows, :], k_scr[d, rows, :]
                tot = _chunk_total(c, rev)
                etot = jnp.exp(tot)
                St = st_ref[0, d, ci]
                dSb = dSt.astype(BF16)
                do_b = do.astype(BF16)
                dq = _dot(do_b, St) * jnp.exp(c)
                dk = _dot(v.astype(BF16), dSb) * jnp.exp(tot - c)
                dv = _dot_nt(ke_scr[d, rows, :], dSb)
                dtot = (jnp.sum(St.astype(F32) * dSt, axis=0, keepdims=True) * etot
                        + jnp.sum(k * dk, axis=0, keepdims=True))
                for s in range(CHUNK):
                    E = _pair_decay(c, s, rev)
                    XE = E * k[s:s + 1, :]
                    a = jnp.sum(q * XE, axis=1, keepdims=True)
                    da = jnp.sum(do * v[s:s + 1, :], axis=1, keepdims=True)
                    dq = dq + da * XE
                    dk_row = jnp.sum(da * q * E, axis=0, keepdims=True)
                    dv_row = jnp.sum(a * do, axis=0, keepdims=True)
                    dk = dk + jnp.where(t16 == s, dk_row, 0.0)
                    dv = dv + jnp.where(t16 == s, dv_row, 0.0)
                dg_scr[d, rows, :] = _chunk_cumsum(q * dq - k * dk, not rev) + dtot
                dk_scr[d, rows, :] = dk
                dq_scr[rows, :] += dq
                dv_scr[rows, :] += dv
                new.append(dSt * etot + _dot_tn(do_b, qe_scr[d, rows, :]))
            return tuple(new)

        lax.fori_loop(0, nc, bwd_step, (zero, zero), unroll=2)

        for d, _, z_ref, dz_ref in dirs:
            lb = _lower_bound(lbl_ref, d)
            sg, f = _gate_terms(z_ref[...], lb)
            df = dg_scr[d] / f - dk_scr[d]
            dz_ref[...] = (df * (1.0 - lb) * sg * (1.0 - sg)).astype(BF16)
            dl0 = jnp.sum(df * (1.0 - sg), axis=0, keepdims=True) * lb * (1.0 - lb)
            dlbl_ref[d, 0:1, :] = dl0
            dlbl_ref[d, 1:2, :] = -dl0
        dq_ref[...] = dq_scr[...].astype(BF16)
        dv_ref[...] = dv_scr[...].astype(BF16)

    cb = HA // HEAD
    col = lambda kk: pl.BlockSpec((T, HEAD), lambda h: (0, kk * cb + h))
    tcol = pl.BlockSpec((T, HEAD), lambda h: (0, h))
    lcol = pl.BlockSpec((L, HEAD), lambda h: (0, h))
    outs = _pcall(
        body, name="hgrn_bwd", grid=(nh,),
        in_specs=[col(0), col(1), col(2), col(3), col(4),
                  pl.BlockSpec((2, 2, HEAD), lambda h: (0, 0, h)), pl.BlockSpec((1, HEAD), lambda h: (0, 0)),
                  lcol, lcol,
                  pl.BlockSpec((1, 2, nc, HEAD, HEAD), lambda h: (h, 0, 0, 0, 0), pipeline_mode=pl.Buffered(1))],
        out_specs=[tcol, tcol, tcol, tcol, tcol, pl.BlockSpec((2, 2, HEAD), lambda h: (0, 0, h)),
                   pl.BlockSpec((8, HEAD), lambda h: (0, 0))],
        out_shape=[jax.ShapeDtypeStruct((T, HA), BF16)] * 5 + [jax.ShapeDtypeStruct((2, 2, HA), F32),
                                                               jax.ShapeDtypeStruct((8, HEAD), F32)],
        scratch_shapes=[pltpu.VMEM((T, HEAD), F32),
                        pltpu.VMEM((2, T, HEAD), F32), pltpu.VMEM((2, T, HEAD), F32),
                        pltpu.VMEM((2, T, HEAD), BF16), pltpu.VMEM((2, T, HEAD), BF16),
                        pltpu.VMEM((2, T, HEAD), F32), pltpu.VMEM((2, T, HEAD), F32),
                        pltpu.VMEM((T, HEAD), F32), pltpu.VMEM((T, HEAD), F32)],
        compiler_params=_params(("arbitrary",)),
    )(p, p, p, p, p, lbl, ng, o, dya, st)
    return outs


def _swap_halves(t, lane):
    q = HEAD // 4
    return jnp.where((lane % (2 * q)) < q, pltpu.roll(t, HEAD - q, 1), pltpu.roll(t, q, 1))


def _qk_norm(t, g):
    r = lax.rsqrt(jnp.mean(t * t, axis=-1, keepdims=True) + EPS)
    return t * r, r


def _rope(t, cos, sin, lane):
    return t * cos + _swap_halves(t, lane) * sin


def _qk_norm_bwd(dy, th, r, g):
    dth = dy * g
    return r * (dth - th * jnp.mean(dth * th, axis=-1, keepdims=True)), jnp.sum(dy * th, axis=0, keepdims=True)


def _rope_bwd(dy, cos, sin, lane):
    return dy * cos + _swap_halves(dy * sin, lane)


def _na_geometry(L):
    n_rows = L // GRID_W
    kr = min(WIN_R, n_rows)
    return n_rows, kr


def _na_prep(q_ref, k_ref, v_ref, gq_ref, gk_ref, cos_ref, sin_ref, qs, ks, vs, n_ctx, L):
    lane = lax.broadcasted_iota(jnp.int32, (L, HEAD), 1)
    cos, sin = cos_ref[...], sin_ref[...]
    qh, _ = _qk_norm(q_ref[pl.ds(n_ctx, L), :], None)
    qs[...] = _rope(qh * gq_ref[...], cos, sin, lane).astype(BF16)
    kh, _ = _qk_norm(k_ref[pl.ds(n_ctx, L), :], None)
    ks[pl.ds(n_ctx, L), :] = _rope(kh * gk_ref[...], cos, sin, lane).astype(BF16)
    kc, _ = _qk_norm(k_ref[pl.ds(0, n_ctx), :], None)
    ks[pl.ds(0, n_ctx), :] = (kc * gk_ref[...]).astype(BF16)
    vs[...] = v_ref[...].astype(BF16)


def _na_scores(r, qs, ks, bias_ref, n_ctx, n_rows, kr):
    scale = HEAD ** -0.5
    r0 = jnp.clip(r - WIN_R // 2, 0, n_rows - kr)
    qrows = pl.ds(pl.multiple_of(r * GRID_W, GRID_W), GRID_W)
    krows = pl.ds(pl.multiple_of(n_ctx + r0 * GRID_W, GRID_W), kr * GRID_W)
    qv = qs[qrows, :]
    sb = _dot_nt(qv, ks[krows, :]) * scale
    b0 = r0 - r + (WIN_R - 1)
    sb = sb + jnp.concatenate([bias_ref[0, b0 + 2 * jj] for jj in range(kr // 2)], axis=1)
    sc = _dot_nt(qv, ks[pl.ds(0, n_ctx), :]) * scale
    m = jnp.maximum(jnp.max(sb, axis=1, keepdims=True), jnp.max(sc, axis=1, keepdims=True))
    eb, ec = jnp.exp(sb - m), jnp.exp(sc - m)
    inv = 1.0 / (jnp.sum(eb, axis=1, keepdims=True) + jnp.sum(ec, axis=1, keepdims=True))
    return eb * inv, ec * inv, qrows, krows, b0


def _na_fwd(p, bias, gq, gk, cos, sin, n_ctx, off, HB):
    T = p.shape[0]
    L = T - n_ctx
    nh = HB // HEAD
    n_rows, kr = _na_geometry(L)
    ob = off // HEAD

    def body(q_ref, k_ref, v_ref, bias_ref, gq_ref, gk_ref, cos_ref, sin_ref, y_ref, qs, ks, vs):
        _na_prep(q_ref, k_ref, v_ref, gq_ref, gk_ref, cos_ref, sin_ref, qs, ks, vs, n_ctx, L)

        def step(i, carry):
            for u in range(2):
                pb, pc, qrows, krows, _ = _na_scores(2 * i + u, qs, ks, bias_ref, n_ctx, n_rows, kr)
                y = _dot(pb.astype(BF16), vs[krows, :]) + _dot(pc.astype(BF16), vs[pl.ds(0, n_ctx), :])
                y_ref[qrows, :] = y.astype(BF16)
            return carry

        lax.fori_loop(0, n_rows // 2, step, 0)

    col = lambda kk: pl.BlockSpec((T, HEAD), lambda h: (0, ob + kk * nh + h))
    vec = pl.BlockSpec((1, HEAD), lambda h: (0, 0))
    tab = pl.BlockSpec((L, HEAD), lambda h: (0, 0))
    return _pcall(
        body, name="na_fwd", grid=(nh,),
        in_specs=[col(0), col(1), col(2), pl.BlockSpec((1,) + bias.shape[1:], lambda h: (h, 0, 0, 0)),
                  vec, vec, tab, tab],
        out_specs=pl.BlockSpec((L, HEAD), lambda h: (0, h)),
        out_shape=jax.ShapeDtypeStruct((L, HB), BF16),
        scratch_shapes=[pltpu.VMEM((L, HEAD), BF16), pltpu.VMEM((T, HEAD), BF16), pltpu.VMEM((T, HEAD), BF16)],
        compiler_params=_params(("parallel",)),
    )(p, p, p, bias, gq, gk, cos, sin)


def _na_bwd(p, bias, gq, gk, cos, sin, dyb, n_ctx, off, HB):
    T = p.shape[0]
    L = T - n_ctx
    nh = HB // HEAD
    n_rows, kr = _na_geometry(L)
    ob = off // HEAD
    scale = HEAD ** -0.5

    def body(q_ref, k_ref, v_ref, bias_ref, gq_ref, gk_ref, cos_ref, sin_ref, dy_ref,
             dq_ref, dk_ref, dv_ref, dbias_ref, dg_ref, qs, ks, vs, dqa, dka, dva):
        h = pl.program_id(0)
        _na_prep(q_ref, k_ref, v_ref, gq_ref, gk_ref, cos_ref, sin_ref, qs, ks, vs, n_ctx, L)
        dka[...] = jnp.zeros_like(dka)
        dva[...] = jnp.zeros_like(dva)
        dbias_ref[...] = jnp.zeros_like(dbias_ref)

        crows = pl.ds(0, n_ctx)

        def step(i, carry):
            done = []
            for u in range(2):
                pb, pc, qrows, krows, b0 = _na_scores(2 * i + u, qs, ks, bias_ref, n_ctx, n_rows, kr)
                do = dy_ref[qrows, :]
                qv = qs[qrows, :]
                dpb = _dot_nt(do, vs[krows, :])
                dpc = _dot_nt(do, vs[crows, :])
                delta = jnp.sum(pb * dpb, axis=1, keepdims=True) + jnp.sum(pc * dpc, axis=1, keepdims=True)
                dsb = pb * (dpb - delta)
                dsc = pc * (dpc - delta)
                dsb_b, dsc_b = dsb.astype(BF16), dsc.astype(BF16)
                dqa[qrows, :] = (_dot(dsb_b, ks[krows, :]) + _dot(dsc_b, ks[crows, :])) * scale
                done.append((krows, b0, dsb, _dot_tn(dsb_b, qv) * scale, _dot_tn(dsc_b, qv) * scale,
                             _dot_tn(pb.astype(BF16), do), _dot_tn(pc.astype(BF16), do)))
            for krows, b0, dsb, dkb, dkc, dvb, dvc in done:
                for jj in range(kr // 2):
                    dbias_ref[0, b0 + 2 * jj] += dsb[:, jj * 2 * GRID_W:(jj + 1) * 2 * GRID_W]
                dka[krows, :] += dkb
                dka[crows, :] += dkc
                dva[krows, :] += dvb
                dva[crows, :] += dvc
            return carry

        lax.fori_loop(0, n_rows // 2, step, 0)

        lane = lax.broadcasted_iota(jnp.int32, (L, HEAD), 1)
        cos, sin = cos_ref[...], sin_ref[...]
        lat, ctx = pl.ds(n_ctx, L), pl.ds(0, n_ctx)
        gqv, gkv = gq_ref[...], gk_ref[...]
        qh, rq = _qk_norm(q_ref[lat, :], None)
        dq, dgq = _qk_norm_bwd(_rope_bwd(dqa[...], cos, sin, lane), qh, rq, gqv)
        dq_ref[ctx, :] = jnp.zeros((n_ctx, HEAD), BF16)
        dq_ref[lat, :] = dq.astype(BF16)
        kh, rk = _qk_norm(k_ref[lat, :], None)
        dk, dgk = _qk_norm_bwd(_rope_bwd(dka[lat, :], cos, sin, lane), kh, rk, gkv)
        dk_ref[lat, :] = dk.astype(BF16)
        kch, rkc = _qk_norm(k_ref[ctx, :], None)
        dkc, dgkc = _qk_norm_bwd(dka[ctx, :], kch, rkc, gkv)
        dk_ref[ctx, :] = dkc.astype(BF16)
        dv_ref[...] = dva[...].astype(BF16)

        @pl.when(h == 0)
        def _():
            dg_ref[...] = jnp.zeros_like(dg_ref)

        dg_ref[0:1, :] += dgq
        dg_ref[1:2, :] += dgk + dgkc

    col = lambda kk: pl.BlockSpec((T, HEAD), lambda h: (0, ob + kk * nh + h))
    vec = pl.BlockSpec((1, HEAD), lambda h: (0, 0))
    tab = pl.BlockSpec((L, HEAD), lambda h: (0, 0))
    tcol = pl.BlockSpec((T, HEAD), lambda h: (0, h))
    bspec = pl.BlockSpec((1,) + bias.shape[1:], lambda h: (h, 0, 0, 0))
    return _pcall(
        body, name="na_bwd", grid=(nh,),
        in_specs=[col(0), col(1), col(2), bspec, vec, vec, tab, tab, pl.BlockSpec((L, HEAD), lambda h: (0, h))],
        out_specs=[tcol, tcol, tcol, bspec, pl.BlockSpec((8, HEAD), lambda h: (0, 0))],
        out_shape=[jax.ShapeDtypeStruct((T, HB), BF16)] * 3 + [jax.ShapeDtypeStruct(bias.shape, F32),
                                                               jax.ShapeDtypeStruct((8, HEAD), F32)],
        scratch_shapes=[pltpu.VMEM((L, HEAD), BF16), pltpu.VMEM((T, HEAD), BF16), pltpu.VMEM((T, HEAD), BF16),
                        pltpu.VMEM((L, HEAD), F32), pltpu.VMEM((T, HEAD), F32), pltpu.VMEM((T, HEAD), F32)],
        compiler_params=_params(("arbitrary",)),
    )(p, p, p, bias, gq, gk, cos, sin, dyb)


def _bias_tables():
    w = np.arange(GRID_W)
    col_start = np.clip(w - WIN_C // 2, 0, GRID_W - WIN_C)
    col_in = (w[None, :] >= col_start[:, None]) & (w[None, :] < col_start[:, None] + WIN_C)
    dc = np.clip(w[None, :] - w[:, None], -(WIN_C - 1), WIN_C - 1) + WIN_C - 1
    n_pair = 2 * WIN_R
    ridx = np.zeros((n_pair, GRID_W, 2 * GRID_W), np.int32)
    cidx = np.zeros((n_pair, GRID_W, 2 * GRID_W), np.int32)
    valid = np.zeros((n_pair, GRID_W, 2 * GRID_W), bool)
    for i in range(n_pair):
        for half in range(2):
            row = i + half
            sl = slice(half * GRID_W, (half + 1) * GRID_W)
            ridx[i, :, sl] = min(row, 2 * WIN_R - 2)
            cidx[i, :, sl] = dc
            valid[i, :, sl] = col_in & (row <= 2 * WIN_R - 2)
    return ridx, cidx, valid


def _bias_onehot():
    _, cidx, valid = _bias_tables()
    K = GRID_W * 2 * GRID_W
    oh = np.zeros((K, 128), np.float32)
    neg = np.full((1, K), NEG, np.float32)
    for cq in range(GRID_W):
        for ll in range(2 * GRID_W):
            if valid[0, cq, ll]:
                oh[cq * 2 * GRID_W + ll, (ll // GRID_W) * 64 + cidx[0, cq, ll]] = 1.0
                neg[0, cq * 2 * GRID_W + ll] = 0.0
    return oh, neg


def _expand_bias(table):
    H = table.shape[0]
    n_pair, n_dc = 2 * WIN_R, 2 * WIN_C - 1
    tp = jnp.pad(table, ((0, 0), (0, n_pair + 1 - table.shape[1]), (0, 64 - n_dc)))
    t2 = jnp.concatenate([tp[:, :n_pair], tp[:, 1:n_pair + 1]], axis=-1).reshape(H * n_pair, 128)
    oh, neg = _bias_onehot()

    def body(t_ref, oh_ref, neg_ref, o_ref):
        o_ref[...] = lax.dot_general(t_ref[...], oh_ref[...], (((1,), (1,)), ((), ())), precision=HI,
                                     preferred_element_type=F32) + neg_ref[...]

    out = _pcall(body, name="bias_expand", out_shape=jax.ShapeDtypeStruct((H * n_pair, oh.shape[0]), F32),
                         compiler_params=_params())(t2, jnp.asarray(oh), jnp.asarray(neg))
    return out.reshape(H, n_pair, GRID_W, 2 * GRID_W)


def _bias_grad(dbias):
    H = dbias.shape[0]
    n_pair, n_dc = 2 * WIN_R, 2 * WIN_C - 1
    K = GRID_W * 2 * GRID_W
    oh, _ = _bias_onehot()
    flat = dbias.reshape(H * n_pair, K)

    def body(d_ref, oh_ref, o_ref):
        o_ref[...] = jnp.dot(d_ref[...], oh_ref[...], precision=HI, preferred_element_type=F32)

    g = _pcall(body, name="bias_grad", out_shape=jax.ShapeDtypeStruct((H * n_pair, 128), F32),
                       compiler_params=_params())(flat, jnp.asarray(oh))
    g = g.reshape(H, n_pair, 128)
    left, right = g[:, :, :n_dc], g[:, :, 64:64 + n_dc]
    out = left[:, :n_pair - 1]
    return out.at[:, 1:].add(right[:, :n_pair - 2])


def _rope_tables(L):
    pos = np.arange(L)
    row = (pos // GRID_W).astype(np.float32)
    colp = (pos % GRID_W).astype(np.float32)
    half = HEAD // 2
    nf = half // 2
    inv = (ROPE_THETA ** (-np.arange(nf, dtype=np.float32) / nf)).astype(np.float32)

    def tabs(pv):
        ang = pv[:, None] * inv[None, :]
        c, s = np.cos(ang), np.sin(ang)
        return np.concatenate([c, c], axis=1), np.concatenate([-s, s], axis=1)

    cr, sr = tabs(row)
    cc, sc = tabs(colp)
    return (jnp.asarray(np.concatenate([cr, cc], axis=1), F32), jnp.asarray(np.concatenate([sr, sc], axis=1), F32))


def _adamw(w, g, m, v, name, after=None, copy_g=False):
    R, C = w.shape
    tr = _row_tile(R, C)
    c1 = 1.0 - ADAM_B1 ** ADAM_STEP
    c2 = 1.0 - ADAM_B2 ** ADAM_STEP
    deps = [] if after is None else [after]
    n_out = 4 if copy_g else 3

    def body(w_ref, g_ref, m_ref, v_ref, *rest):
        d_ref, mo_ref, vo_ref = rest[len(deps):len(deps) + 3]
        gv = g_ref[...]
        mn = ADAM_B1 * m_ref[...] + (1.0 - ADAM_B1) * gv
        vn = ADAM_B2 * v_ref[...] + (1.0 - ADAM_B2) * (gv * gv)
        mo_ref[...] = mn
        vo_ref[...] = vn
        d_ref[...] = -ADAM_LR * ((mn / c1) / (jnp.sqrt(vn / c2) + ADAM_EPS) + ADAM_WD * w_ref[...])
        if copy_g:
            rest[-1][...] = gv

    blk = pl.BlockSpec((tr, C), lambda i: (i, 0))
    return _pcall(
        body, name=name, grid=(R // tr,),
        in_specs=[blk] * 4 + [_ANY] * len(deps), out_specs=[blk] * n_out,
        out_shape=[jax.ShapeDtypeStruct((R, C), F32)] * n_out,
        compiler_params=_params(("parallel",)),
    )(w, g, m, v, *deps)


PACK_W = 1024


def _pack(parts):
    flat, offs, pos = [], [], 0
    for a in parts:
        n = a.size
        padn = -n % PACK_W
        flat.append(jnp.pad(a.reshape(-1).astype(F32), (0, padn)))
        offs.append((pos, n, a.shape))
        pos += n + padn
    tail = -pos % (8 * PACK_W)
    if tail:
        flat.append(jnp.zeros((tail,), F32))
    return jnp.concatenate(flat).reshape(-1, PACK_W), offs


def _unpack(buf, offs, i):
    pos, n, shape = offs[i]
    return buf.reshape(buf.shape[:-2] + (-1,))[..., pos:pos + n].reshape(buf.shape[:-2] + shape)


def kernel(x, c, ctx, c_ctx, ada_w, ada_b, norm1_g, norm2_g, w_in, hgrn_lb_logits, hgrn_norm_g, na_q_norm_g, na_k_norm_g, na_rel_bias, w_branch_a, w_branch_b, w_out, ffn_w1, ffn_w3, ffn_conv_w, ffn_conv_b, ffn_w2, loss_target, m_c_ctx, m_ada_w, m_ada_b, m_norm1_g, m_norm2_g, m_w_in, m_hgrn_lb_logits, m_hgrn_norm_g, m_na_q_norm_g, m_na_k_norm_g, m_na_rel_bias, m_w_branch_a, m_w_branch_b, m_w_out, m_ffn_w1, m_ffn_w3, m_ffn_conv_w, m_ffn_conv_b, m_ffn_w2, v_c_ctx, v_ada_w, v_ada_b, v_norm1_g, v_norm2_g, v_w_in, v_hgrn_lb_logits, v_hgrn_norm_g, v_na_q_norm_g, v_na_k_norm_g, v_na_rel_bias, v_w_branch_a, v_w_branch_b, v_w_out, v_ffn_w1, v_ffn_w3, v_ffn_conv_w, v_ffn_conv_b, v_ffn_w2):
    weights = dict(c_ctx=c_ctx, ada_w=ada_w, ada_b=ada_b, norm1_g=norm1_g, norm2_g=norm2_g, w_in=w_in,
                   hgrn_lb_logits=hgrn_lb_logits, hgrn_norm_g=hgrn_norm_g, na_q_norm_g=na_q_norm_g,
                   na_k_norm_g=na_k_norm_g, na_rel_bias=na_rel_bias, w_branch_a=w_branch_a, w_branch_b=w_branch_b,
                   w_out=w_out, ffn_w1=ffn_w1, ffn_w3=ffn_w3, ffn_conv_w=ffn_conv_w, ffn_conv_b=ffn_conv_b,
                   ffn_w2=ffn_w2)
    moms = dict(c_ctx=(m_c_ctx, v_c_ctx), ada_w=(m_ada_w, v_ada_w), ada_b=(m_ada_b, v_ada_b),
                norm1_g=(m_norm1_g, v_norm1_g), norm2_g=(m_norm2_g, v_norm2_g), w_in=(m_w_in, v_w_in),
                hgrn_lb_logits=(m_hgrn_lb_logits, v_hgrn_lb_logits), hgrn_norm_g=(m_hgrn_norm_g, v_hgrn_norm_g),
                na_q_norm_g=(m_na_q_norm_g, v_na_q_norm_g), na_k_norm_g=(m_na_k_norm_g, v_na_k_norm_g),
                na_rel_bias=(m_na_rel_bias, v_na_rel_bias), w_branch_a=(m_w_branch_a, v_w_branch_a),
                w_branch_b=(m_w_branch_b, v_w_branch_b), w_out=(m_w_out, v_w_out), ffn_w1=(m_ffn_w1, v_ffn_w1),
                ffn_w3=(m_ffn_w3, v_ffn_w3), ffn_conv_w=(m_ffn_conv_w, v_ffn_conv_w),
                ffn_conv_b=(m_ffn_conv_b, v_ffn_conv_b), ffn_w2=(m_ffn_w2, v_ffn_w2))
    order = list(weights)

    L, D = x.shape[1], x.shape[2]
    N = ctx.shape[1]
    T = N + L
    HA = w_branch_a.shape[1]
    HB = w_branch_b.shape[1]
    F = ffn_conv_b.shape[1]
    IN = 5 * HA + 3 * HB + 2 * D
    n_ada = ada_w.shape[2]
    ix, iy, ic = _pos()
    chip = 2 * ix + iy
    dev = 2 * chip + ic

    _PENDING.clear()
    pk0, offs0 = _pack([c[0], hgrn_lb_logits, ffn_conv_w[0]])
    g0 = _allgather8(pk0, "gather_small0")
    c_all = _unpack(g0, offs0, 0)
    lbl_parts = _unpack(g0, offs0, 1)
    lbl = jnp.concatenate([lbl_parts[2 * j] for j in range(N_CHIP)], axis=-1)
    cw_parts = _unpack(g0, offs0, 2)
    cw = jnp.concatenate([cw_parts[2 * j] for j in range(N_CHIP)], axis=-1)
    cw8 = jnp.pad(cw, ((0, 5), (0, 0)))

    cs = jnp.concatenate([c_all, c_ctx[None, :], jnp.zeros((7, D), F32)], axis=0)
    ada_b_mine = lax.dynamic_slice(ada_b, (0, chip * n_ada), (1, n_ada))
    mod_mine = _ada_fwd(cs, ada_w[0], ada_b_mine)
    gm = _allgather8(mod_mine, "gather_mod")
    mod = jnp.concatenate([gm[2 * j] for j in range(N_CHIP)], axis=-1)
    mod_l = lax.dynamic_slice(mod, (dev, 0), (1, N_MOD * D)).reshape(N_MOD, D)
    mod_c = mod[8].reshape(N_MOD, D)
    sh1, sc1, g1, sh2, sc2, g2 = [mod_l[i:i + 1] for i in range(N_MOD)]
    shift1 = jnp.concatenate([mod_c[0:1], sh1], axis=0)
    scale1 = jnp.concatenate([mod_c[1:2], sc1], axis=0)

    shards = [w_in[0], w_branch_a[0], w_branch_b[0], w_out[0], ffn_w1[0], ffn_w3[0], ffn_w2[0]]
    names = ["w_in", "w_a", "w_b", "w_out", "w1", "w3", "w2"]
    slots = [_cast_bf16_slot(s, "cast_" + nm) for s, nm in zip(shards, names)]
    gat_in = _gather_start("in", slots[0:1], gm)
    gat_mix = _gather_start("mix", slots[1:4])
    gat_ffn = _gather_start("ffn", slots[4:7])

    xall = jnp.concatenate([ctx[0], x[0]], axis=0)
    h_all = _rms1_fwd(xall, norm1_g, shift1, scale1, N)
    gat_in = _gather_mid(gat_in, h_all)
    (Win,) = _gather_finish(gat_in, h_all)
    p = _mm_nn(h_all, Win, F32, "mm_p")
    gat_mix = _gather_mid(gat_mix, p)
    y_a, o_a, st_a = _hgrn_fwd(p, lbl, hgrn_norm_g, N, HA)
    Wa, Wb, Wo = _gather_finish(gat_mix, y_a)
    Wo = Wo.reshape(1, D, D)
    gat_ffn = _gather_mid(gat_ffn, y_a)
    bias = _expand_bias(na_rel_bias[0])
    cos, sin = _rope_tables(L)
    off_na = 5 * HA
    y_b = _na_fwd(p, bias, na_q_norm_g, na_k_norm_g, cos, sin, N, off_na, HB)
    za = _mm_nn(y_a, Wa, F32, "mm_za")
    zb = _mm_nn(y_b, Wb, F32, "mm_zb")
    off_ga, off_gb = 5 * HA + 3 * HB, 5 * HA + 3 * HB + D
    z = _merge_fwd(za, zb, p, N, off_ga, off_gb)
    mo = _mm_nn(z, Wo, F32, "mm_mo")
    vec2 = jnp.concatenate([g1, norm2_g, sh2, sc2, jnp.zeros((4, D), F32)], axis=0)
    x_mid, h2 = _resid_rms2_fwd(x[0], mo, vec2)
    W1, W3, W2 = _gather_finish(gat_ffn, h2)
    W2 = W2.reshape(1, F, D)
    u1 = _mm_nn(h2, W1, F32, "mm_u1")
    u3 = _mm_nn(h2, W3, F32, "mm_u3")
    a = _convgate_fwd(u1, u3, cw8, ffn_conv_b)
    f = _mm_nn(a, W2, F32, "mm_f")
    dy, df, s_loss = _loss_head(x_mid, f, g2, loss_target[0])
    loss = lax.psum(s_loss[1, 0], ("x", "y", "c"))
    d_g2 = s_loss[0:1]

    gW2 = _mm_tn(a, df, 1, "mm_gw2").reshape(N_CHIP, F // N_CHIP, D)
    da = _mm_nt(df, W2, F32, "mm_da")
    du1, du3, s_conv = _convgate_bwd(u1, u3, da, cw8, ffn_conv_b)
    gW1 = _mm_tn(h2, du1, N_CHIP, "mm_gw1")
    gW3 = _mm_tn(h2, du3, N_CHIP, "mm_gw3")
    rs_ffn = _rs_start("ffn", [gW2, gW1, gW3])
    dh2a = _mm_nt(du1, W1, F32, "mm_dh2a")
    dh2b = _mm_nt(du3, W3, F32, "mm_dh2b")
    rs_ffn = _rs_scatter(rs_ffn, dh2b)
    dxm, dmo, s_rms2 = _resid_rms2_bwd(x_mid, dh2a, dh2b, dy, mo, vec2)
    gWo = _mm_tn(z, dmo, 1, "mm_gwo").reshape(N_CHIP, D // N_CHIP, D)
    dz = _mm_nt(dmo, Wo, F32, "mm_dz")
    dza, dzb, dga, dgb = _merge_bwd(dz, za, zb, p, N, off_ga, off_gb)
    gWa = _mm_tn(y_a, dza, N_CHIP, "mm_gwa")
    gWb = _mm_tn(y_b, dzb, N_CHIP, "mm_gwb")
    rs_mix = _rs_start("mix", [gWo, gWa, gWb])
    dya = _mm_nt(dza, Wa, F32, "mm_dya")
    dyb = _mm_nt(dzb, Wb, BF16, "mm_dyb")
    rs_mix = _rs_scatter(rs_mix, dyb)
    dq_a, dzf, dzbk, di_a, dog, dlbl, s_ng = _hgrn_bwd(p, lbl, hgrn_norm_g, o_a, dya, st_a, N, HA)
    rs_ffn = _rs_join(rs_ffn, dq_a)
    dq_n, dk_n, dv_n, dbias, s_qk = _na_bwd(p, bias, na_q_norm_g, na_k_norm_g, cos, sin, dyb, N, off_na, HB)
    rs_mix = _rs_join(rs_mix, dq_n)
    dp = jnp.concatenate([dq_a, dzf, dzbk, di_a, dog, dq_n, dk_n, dv_n, dga, dgb], axis=1)
    gWin = _mm_tn(h_all, dp, N_CHIP, "mm_gwin")
    rs_in = _rs_start("in", [gWin])
    dh = _mm_nt(dp, Win, F32, "mm_dh")
    grad_x, s_rms1 = _rms1_bwd(xall, dh, dxm, norm1_g, scale1, N)
    d_table = _bias_grad(dbias)

    zD = jnp.zeros((1, D), F32)
    dmod_l = jnp.concatenate([s_rms1[2:3], s_rms1[3:4], s_rms2[3:4], s_rms2[0:1], s_rms2[1:2], d_g2], axis=0)
    dmod_c = jnp.concatenate([s_rms1[0:1], s_rms1[1:2], zD, zD, zD, zD], axis=0)
    pk1, offs1 = _pack([dmod_l, dmod_c, s_rms1[4], s_rms2[2], dlbl, s_ng[0], s_qk[0], s_qk[1], d_table,
                        s_conv[0:3], s_conv[3]])
    g1all = _allgather8(pk1, "gather_small1")
    tot1 = _sum8(g1all, "sum_small1")
    dmod_rows = _unpack(g1all, offs1, 0).reshape(N_DEV, N_MOD * D)
    dmod_c_tot = _unpack(tot1, offs1, 1).reshape(1, N_MOD * D)
    dmod16 = jnp.concatenate([dmod_rows, dmod_c_tot, jnp.zeros((7, N_MOD * D), F32)], axis=0)
    dmod16_mine = lax.dynamic_slice(dmod16, (0, chip * n_ada), (16, n_ada))
    g_ada_w, dact = _ada_bwd(cs, ada_w[0], dmod16_mine)
    pk2, offs2 = _pack([dact[8]])
    g2all = _allgather8(pk2, "gather_small2")
    dact_rows = _unpack(g2all, offs2, 0)
    dact_sel = jnp.concatenate([dact_rows[2 * j][None] for j in range(N_CHIP)] + [jnp.zeros((4, D), F32)], axis=0)

    grads = {}
    grads["ada_w"] = g_ada_w[None]
    grads["ada_b"] = (_unpack(tot1, offs1, 0) + _unpack(tot1, offs1, 1)).reshape(1, N_MOD * D)
    grads["norm1_g"] = _unpack(tot1, offs1, 2)[None]
    grads["norm2_g"] = _unpack(tot1, offs1, 3)[None]
    g_lbl = _unpack(tot1, offs1, 4)
    n_lb = HA // N_CHIP
    grads["hgrn_lb_logits"] = lax.dynamic_slice(g_lbl, (0, 0, chip * n_lb), (2, 2, n_lb))
    grads["hgrn_norm_g"] = _unpack(tot1, offs1, 5)[None]
    grads["na_q_norm_g"] = _unpack(tot1, offs1, 6)[None]
    grads["na_k_norm_g"] = _unpack(tot1, offs1, 7)[None]
    grads["na_rel_bias"] = _unpack(tot1, offs1, 8)[None]
    g_cw = _unpack(tot1, offs1, 9)
    n_f = F // N_CHIP
    grads["ffn_conv_w"] = lax.dynamic_slice(g_cw, (0, chip * n_f), (3, n_f))[None]
    grads["ffn_conv_b"] = _unpack(tot1, offs1, 10)[None]

    g_c_ctx = _dsilu_rows(dact_sel, c_ctx[None, :], "grad_c_ctx")
    grads["c_ctx"] = g_c_ctx[0]

    rs_in = _rs_scatter(rs_in, g_c_ctx)
    big_names = ["ada_w", "w_in", "w_branch_a", "w_branch_b", "w_out", "ffn_w1", "ffn_w3", "ffn_w2"]
    small_names = [n for n in order if n not in big_names]
    delta, new_m, new_v = {}, {}, {}

    def update(nm, after=None):
        reduced = nm != "ada_w"
        d_, m_, v_, *g_ = _adamw(weights[nm][0], grads[nm][0], moms[nm][0][0], moms[nm][1][0], "adamw_" + nm,
                                 after, copy_g=reduced)
        delta[nm], new_m[nm], new_v[nm] = d_[None], m_[None], v_[None]
        if reduced:
            grads[nm] = g_[0][None]
        return d_

    last = update("ada_w")
    for nm, g in zip(["ffn_w2", "ffn_w1", "ffn_w3"], _rs_finish(rs_ffn, last)):
        grads[nm] = g[None]
        last = update(nm, last)
    for nm, g in zip(["w_out", "w_branch_a", "w_branch_b"], _rs_finish(rs_mix, last)):
        grads[nm] = g[None]
        last = update(nm, last)
    rs_in = _rs_join(rs_in, last)
    grads["w_in"] = _rs_finish(rs_in, last)[0][None]
    update("w_in")
    pw, offw = _pack([weights[n] for n in small_names])
    pg, _ = _pack([grads[n] for n in small_names])
    pm, _ = _pack([moms[n][0] for n in small_names])
    pv, _ = _pack([moms[n][1] for n in small_names])
    d_, m_, v_ = _adamw(pw, pg, pm, pv, "adamw_small")
    for i, nm in enumerate(small_names):
        delta[nm], new_m[nm], new_v[nm] = _unpack(d_, offw, i), _unpack(m_, offw, i), _unpack(v_, offw, i)

    return (loss, grad_x[None], *[grads[n] for n in order], *[delta[n] for n in order],
            *[new_m[n] for n in order], *[new_v[n] for n in order])


def _dsilu_rows(v, cv, name):
    D = v.shape[1]

    def body(v_ref, c_ref, o_ref):
        t = c_ref[...]
        s = _sigmoid(t)
        o_ref[...] = (((v_ref[0:1, :] + v_ref[1:2, :]) + v_ref[2:3, :]) + v_ref[3:4, :]) * (s * (1.0 + t * (1.0 - s)))

    return _pcall(body, name=name, out_shape=jax.ShapeDtypeStruct((1, D), F32),
                          compiler_params=_params())(v, cv)
```

```python
import functools

import numpy as np
import jax
import jax.numpy as jnp
from jax import lax
from jax.experimental import pallas as pl
from jax.experimental.pallas import tpu as pltpu

F32 = jnp.float32
BF16 = jnp.bfloat16
MESH = pl.DeviceIdType.MESH

HEAD = 128
GRID_W = 64
WIN_R = 8
WIN_C = 16
ROPE_THETA = 10000.0
EPS = 1e-6
N_MOD = 6
CHUNK = 16
ADAM_LR = 0.001
ADAM_B1 = 0.9
ADAM_B2 = 0.999
ADAM_EPS = 1e-08
ADAM_WD = 0.01
ADAM_STEP = 10
NEG = -1e30
VMEM_LIMIT = 56 * 1024 * 1024
N_DEV = 8
N_CHIP = 4
HI = lax.Precision.HIGHEST


def _pick(n, cands):
    for c in cands:
        if n % c == 0:
            return c
    return n


def _row_tile(rows, cols, target_bytes=1 << 20):
    want = max(16, target_bytes // (4 * cols))
    for t in (512, 256, 128, 64, 32, 16, 8):
        if t <= want and rows % t == 0:
            return t
    return rows


def _params(sem=None):
    return pltpu.CompilerParams(dimension_semantics=sem, vmem_limit_bytes=VMEM_LIMIT)


def _dot(a, b):
    return jnp.dot(a, b, preferred_element_type=F32)


def _dot_nt(a, b):
    return lax.dot_general(a, b, (((1,), (1,)), ((), ())), preferred_element_type=F32)


def _dot_tn(a, b):
    return lax.dot_general(a, b, (((0,), (0,)), ((), ())), preferred_element_type=F32)


def _sigmoid(x):
    return 1.0 / (1.0 + jnp.exp(-x))


def _col_tile(n):
    return n if n <= 1536 else _pick(n, (1024, 768, 512, 384, 256, 128))


def _mm_nn(x, w3, out_dtype, name):
    M, K = x.shape
    S, _, n = w3.shape
    tm = _pick(M, (768, 512, 256, 128, 64))
    tn = _col_tile(n)
    nb = n // tn

    def body(x_ref, w_ref, o_ref):
        o_ref[...] = _dot(x_ref[...].astype(BF16), w_ref[0]).astype(o_ref.dtype)

    return _pcall(
        body, name=name, grid=(M // tm, S * nb),
        in_specs=[pl.BlockSpec((tm, K), lambda i, j: (i, 0)),
                  pl.BlockSpec((1, K, tn), lambda i, j: (j // nb, 0, j % nb))],
        out_specs=pl.BlockSpec((tm, tn), lambda i, j: (i, j)),
        out_shape=jax.ShapeDtypeStruct((M, S * n), out_dtype),
        compiler_params=_params(("parallel", "parallel")),
    )(x, w3)


def _mm_nt(dy, w3, out_dtype, name):
    M = dy.shape[0]
    S, K, n = w3.shape
    tm = _pick(M, (768, 512, 256, 128, 64))
    tk = K if K <= 2048 else _pick(K, (1408, 1024, 512, 256, 128))
    tc = _col_tile(n)
    nb = n // tc
    nsteps = S * nb

    def body(dy_ref, w_ref, o_ref, acc_ref):
        s = pl.program_id(2)

        @pl.when(s == 0)
        def _():
            acc_ref[...] = jnp.zeros_like(acc_ref)

        acc_ref[...] += _dot_nt(dy_ref[...].astype(BF16), w_ref[0])

        @pl.when(s == nsteps - 1)
        def _():
            o_ref[...] = acc_ref[...].astype(o_ref.dtype)

    return _pcall(
        body, name=name, grid=(M // tm, K // tk, nsteps),
        in_specs=[pl.BlockSpec((tm, tc), lambda i, k, s: (i, s)),
                  pl.BlockSpec((1, tk, tc), lambda i, k, s: (s // nb, k, s % nb))],
        out_specs=pl.BlockSpec((tm, tk), lambda i, k, s: (i, k)),
        out_shape=jax.ShapeDtypeStruct((M, K), out_dtype),
        scratch_shapes=[pltpu.VMEM((tm, tk), F32)],
        compiler_params=_params(("parallel", "parallel", "arbitrary")),
    )(dy, w3)


def _mm_tn(x, dy, S, name):
    M, K = x.shape
    n = dy.shape[1] // S
    tk = _pick(K, (512, 256, 128))
    tn = _col_tile(n)
    nb = n // tn

    def body(x_ref, dy_ref, o_ref):
        o_ref[0] = _dot_tn(x_ref[...].astype(BF16), dy_ref[...].astype(BF16)).astype(BF16)

    return _pcall(
        body, name=name, grid=(S * nb, K // tk),
        in_specs=[pl.BlockSpec((M, tk), lambda j, k: (0, k)),
                  pl.BlockSpec((M, tn), lambda j, k: (0, j))],
        out_specs=pl.BlockSpec((1, tk, tn), lambda j, k: (j // nb, k, j % nb)),
        out_shape=jax.ShapeDtypeStruct((S, K, n), BF16),
        compiler_params=_params(("parallel", "parallel")),
    )(x, dy)


def _chip_index():
    return (2 * lax.axis_index("x") + lax.axis_index("y")).astype(jnp.int32).reshape(1)


def _cast_bf16_slot(w, name):
    R, C = w.shape
    tr = _row_tile(R, C, 2 << 20)

    def body(j_ref, w_ref, o_ref):
        o_ref[0] = w_ref[...].astype(BF16)

    return _pcall(
        body, name=name,
        grid_spec=pltpu.PrefetchScalarGridSpec(
            num_scalar_prefetch=1, grid=(R // tr,),
            in_specs=[pl.BlockSpec((tr, C), lambda i, j_ref: (i, 0))],
            out_specs=pl.BlockSpec((1, tr, C), lambda i, j_ref: (j_ref[0], i, 0))),
        out_shape=jax.ShapeDtypeStruct((N_CHIP, R, C), BF16),
        compiler_params=_params(("parallel",)),
    )(_chip_index(), w)


def _pos():
    return lax.axis_index("x"), lax.axis_index("y"), lax.axis_index("c")


def _other_chips(x, y):
    return [(x, 1 - y), (1 - x, y), (1 - x, 1 - y)]


def _allgather8(v, name):
    R, C = v.shape

    def body(x_ref, out_ref, send_sems, recv_sems, local_sem):
        x, y, c = _pos()
        me, sibling = (x, y, c), (x, y, 1 - c)
        chips = _other_chips(x, y)

        def slot(px, py, pc):
            return out_ref.at[4 * px + 2 * py + pc]

        def copy(k, block, to, src=None):
            return pltpu.make_async_remote_copy(
                src_ref=slot(*block) if src is None else src, dst_ref=slot(*block),
                send_sem=send_sems.at[k], recv_sem=recv_sems.at[k], device_id=to, device_id_type=MESH)

        mine = pltpu.make_async_copy(x_ref, slot(*me), local_sem)
        mine.start()
        first = [copy(0, me, sibling, src=x_ref)]
        first += [copy(1 + j, me, (*chip, c), src=x_ref) for j, chip in enumerate(chips)]
        for cp in first:
            cp.start()
        passed = [copy(4 + j, (*chip, c), sibling) for j, chip in enumerate(chips)]
        for j, chip in enumerate(chips):
            copy(1 + j, (*chip, c), me).wait_recv()
            passed[j].start()
        copy(0, sibling, me).wait_recv()
        for j, chip in enumerate(chips):
            copy(4 + j, (*chip, 1 - c), me).wait_recv()
        for cp in first + passed:
            cp.wait_send()
        mine.wait()

    return _pcall(
        body, name=name,
        out_shape=jax.ShapeDtypeStruct((N_DEV, R, C), v.dtype),
        in_specs=[pl.BlockSpec(memory_space=pltpu.VMEM)],
        out_specs=pl.BlockSpec(memory_space=pltpu.VMEM),
        scratch_shapes=[pltpu.SemaphoreType.DMA((7,)), pltpu.SemaphoreType.DMA((7,)), pltpu.SemaphoreType.DMA],
        compiler_params=pltpu.CompilerParams(vmem_limit_bytes=VMEM_LIMIT),
    )(v)


_HBM = pl.BlockSpec(memory_space=pltpu.HBM)
_SEM = pl.BlockSpec(memory_space=pltpu.SEMAPHORE)
_ANY = pl.BlockSpec(memory_space=pl.ANY)
_EFFECT = pltpu.SideEffectType.DATAFLOW_SIDE_EFFECTING
_PENDING = []


def _pcall(body, **kw):
    def run(*operands):
        if not _PENDING or "in_specs" not in kw:
            return pl.pallas_call(body, **kw)(*operands)
        deps = list(_PENDING)
        n = len(operands)

        def tied(*refs):
            return body(*refs[:n], *refs[n + len(deps):])

        return pl.pallas_call(tied, **{**kw, "in_specs": list(kw["in_specs"]) + [_ANY] * len(deps)})(*operands, *deps)
    return run


def _copies(plan, refs, send_sems, recv_sems):
    return [pltpu.make_async_remote_copy(src_ref=src, dst_ref=dst, send_sem=send_sems.at[k], recv_sem=recv_sems.at[k],
                                         device_id=dev, device_id_type=MESH)
            for k, (src, dst, dev) in enumerate(plan(refs))]


def _xfer_start(name, bufs, plan, n_copies, after=None):
    n = len(bufs)
    deps = list(_PENDING) + ([after] if after is not None else [])
    nd = len(deps)

    def body(*refs):
        for cp in _copies(plan, refs[:n], refs[n + nd], refs[n + nd + 1]):
            cp.start()
        refs[-1][...] = jnp.zeros_like(refs[-1])

    outs = pl.pallas_call(
        body, name=name,
        out_shape=(pltpu.SemaphoreType.DMA((n_copies,)), pltpu.SemaphoreType.DMA((n_copies,)),
                   *[pltpu.HBM(b.shape, b.dtype) for b in bufs], jax.ShapeDtypeStruct((8, 128), F32)),
        in_specs=[_HBM] * n + [_ANY] * nd,
        out_specs=(_SEM, _SEM, *[_HBM] * n, pl.BlockSpec(memory_space=pltpu.VMEM)),
        input_output_aliases={t: 2 + t for t in range(n)},
        compiler_params=pltpu.CompilerParams(has_side_effects=_EFFECT),
    )(*[pltpu.with_memory_space_constraint(b, pltpu.HBM) for b in bufs], *deps)
    _PENDING[:] = [outs[-1]]
    return (outs[0], outs[1]), list(outs[2:2 + n])


def _xfer_wait(name, sems, bufs, plan, after):
    n = len(bufs)

    def body(*refs):
        cps = _copies(plan, refs[:n], refs[n], refs[n + 1])
        for cp in cps:
            cp.wait_send()
        for cp in cps:
            cp.wait_recv()

    outs = pl.pallas_call(
        body, name=name,
        out_shape=tuple(pltpu.HBM(b.shape, b.dtype) for b in bufs),
        in_specs=[_HBM] * n + [_SEM, _SEM, _ANY],
        out_specs=tuple([_HBM] * n),
        input_output_aliases={t: t for t in range(n)},
        compiler_params=pltpu.CompilerParams(has_side_effects=_EFFECT),
    )(*bufs, sems[0], sems[1], after)
    return list(outs)


def _half(ref_rows, hc):
    h = ref_rows // 2
    return pl.ds(hc * h, h)


def _plan_gather_ici(bufs):
    x, y, c = _pos()
    j = 2 * x + y
    return [(b.at[j, _half(b.shape[1], c)], b.at[j, _half(b.shape[1], c)], (*chip, c))
            for b in bufs for chip in _other_chips(x, y)]


def _plan_gather_d2d(bufs):
    x, y, c = _pos()
    out = []
    for b in bufs:
        for chip in _other_chips(x, y):
            blk = b.at[2 * chip[0] + chip[1], _half(b.shape[1], c)]
            out.append((blk, blk, (x, y, 1 - c)))
    return out


def _plan_pair_swap(n):
    def plan(bufs):
        x, y, c = _pos()
        return [(g.at[:, _half(g.shape[1], 1 - c)], land, (x, y, 1 - c)) for g, land in zip(bufs[:n], bufs[n:])]
    return plan


def _plan_chip_scatter(n):
    def plan(bufs):
        x, y, c = _pos()
        return [(p.at[2 * chip[0] + chip[1]], land.at[k], (*chip, c))
                for p, land in zip(bufs[:n], bufs[n:]) for k, chip in enumerate(_other_chips(x, y))]
    return plan


def _plan_pair_join(bufs):
    x, y, c = _pos()
    return [(b.at[_half(b.shape[0], c)], b.at[_half(b.shape[0], c)], (x, y, 1 - c)) for b in bufs]


def _empty_hbm(shape, dtype):
    return pltpu.with_memory_space_constraint(lax.empty(shape, dtype), pltpu.HBM)


def _gather_start(tag, bufs, after=None):
    sems, bufs = _xfer_start(f"gather_ici_start_{tag}", bufs, _plan_gather_ici, 3 * len(bufs), after)
    return dict(tag=tag, sems=sems, bufs=bufs)


def _gather_mid(st, after):
    tag = st["tag"]
    bufs = _xfer_wait(f"gather_ici_wait_{tag}", st["sems"], st["bufs"], _plan_gather_ici, after)
    sems, bufs = _xfer_start(f"gather_d2d_start_{tag}", bufs, _plan_gather_d2d, 3 * len(bufs))
    return dict(tag=tag, sems=sems, bufs=bufs)


def _gather_finish(st, after):
    return _xfer_wait(f"gather_d2d_wait_{st['tag']}", st["sems"], st["bufs"], _plan_gather_d2d, after)


def _pair_add(g, r, name):
    S, R, C = g.shape
    h = R // 2
    tr = _row_tile(h, C)
    nb = h // tr

    def body(c_ref, g_ref, r_ref, o_ref):
        o_ref[...] = (g_ref[...].astype(F32) + r_ref[...].astype(F32)).astype(BF16)

    return _pcall(
        body, name=name,
        grid_spec=pltpu.PrefetchScalarGridSpec(
            num_scalar_prefetch=1, grid=(S, nb),
            in_specs=[pl.BlockSpec((1, tr, C), lambda s, i, c_ref: (s, c_ref[0] * nb + i, 0)),
                      pl.BlockSpec((1, tr, C), lambda s, i, c_ref: (s, i, 0))],
            out_specs=pl.BlockSpec((1, tr, C), lambda s, i, c_ref: (s, i, 0))),
        out_shape=jax.ShapeDtypeStruct((S, h, C), BF16),
        compiler_params=_params(("parallel", "parallel")),
    )(lax.axis_index("c").astype(jnp.int32).reshape(1), g, r)


def _chip_sum(p, rb, name):
    S, h, C = p.shape
    tr = _row_tile(h, C)
    nb = h // tr
    jc = jnp.concatenate([_chip_index(), lax.axis_index("c").astype(jnp.int32).reshape(1)])

    def body(jc_ref, p_ref, r_ref, o_ref):
        o_ref[...] = ((p_ref[0].astype(F32) + r_ref[0].astype(F32)) + r_ref[1].astype(F32)) + r_ref[2].astype(F32)

    return _pcall(
        body, name=name,
        grid_spec=pltpu.PrefetchScalarGridSpec(
            num_scalar_prefetch=1, grid=(nb,),
            in_specs=[pl.BlockSpec((1, tr, C), lambda i, jc_ref: (jc_ref[0], i, 0)),
                      pl.BlockSpec((3, tr, C), lambda i, jc_ref: (0, i, 0))],
            out_specs=pl.BlockSpec((tr, C), lambda i, jc_ref: (jc_ref[1] * nb + i, 0))),
        out_shape=jax.ShapeDtypeStruct((2 * h, C), F32),
        compiler_params=_params(("parallel",)),
    )(jc, p, rb)


def _rs_start(tag, gs):
    n = len(gs)
    lands = [_empty_hbm((g.shape[0], g.shape[1] // 2, g.shape[2]), g.dtype) for g in gs]
    sems, bufs = _xfer_start(f"rs_swap_start_{tag}", list(gs) + lands, _plan_pair_swap(n), n)
    return dict(tag=tag, n=n, sems=sems, bufs=bufs)


def _rs_scatter(st, after):
    tag, n = st["tag"], st["n"]
    bufs = _xfer_wait(f"rs_swap_wait_{tag}", st["sems"], st["bufs"], _plan_pair_swap(n), after)
    ps = [_pair_add(g, r, f"rs_pair_add_{tag}{t}") for t, (g, r) in enumerate(zip(bufs[:n], bufs[n:]))]
    lands = [_empty_hbm((3,) + p.shape[1:], p.dtype) for p in ps]
    sems, bufs = _xfer_start(f"rs_scatter_start_{tag}", ps + lands, _plan_chip_scatter(n), 3 * n)
    return dict(tag=tag, n=n, sems=sems, bufs=bufs)


def _rs_join(st, after):
    tag, n = st["tag"], st["n"]
    bufs = _xfer_wait(f"rs_scatter_wait_{tag}", st["sems"], st["bufs"], _plan_chip_scatter(n), after)
    fs = [_chip_sum(p, rb, f"rs_chip_sum_{tag}{t}") for t, (p, rb) in enumerate(zip(bufs[:n], bufs[n:]))]
    sems, bufs = _xfer_start(f"rs_join_start_{tag}", fs, _plan_pair_join, n)
    return dict(tag=tag, n=n, sems=sems, bufs=bufs)


def _rs_finish(st, after):
    return _xfer_wait(f"rs_join_wait_{st['tag']}", st["sems"], st["bufs"], _plan_pair_join, after)


def _sum8(g, name):
    _, R, C = g.shape

    def body(g_ref, o_ref):
        acc = g_ref[0]
        for d in range(1, N_DEV):
            acc = acc + g_ref[d]
        o_ref[...] = acc

    return _pcall(body, name=name, out_shape=jax.ShapeDtypeStruct((R, C), F32),
                          compiler_params=_params())(g)


def _ada_fwd(cs, w, b):
    D, n = w.shape
    tn = _pick(n, (512, 384, 256, 128))

    def body(c_ref, w_ref, b_ref, o_ref):
        cv = c_ref[...]
        a = (cv * _sigmoid(cv)).astype(BF16)
        o_ref[...] = _dot(a, w_ref[...].astype(BF16)) + b_ref[...]

    return _pcall(
        body, name="ada_fwd", grid=(n // tn,),
        in_specs=[pl.BlockSpec((16, D), lambda j: (0, 0)), pl.BlockSpec((D, tn), lambda j: (0, j)),
                  pl.BlockSpec((1, tn), lambda j: (0, j))],
        out_specs=pl.BlockSpec((16, tn), lambda j: (0, j)),
        out_shape=jax.ShapeDtypeStruct((16, n), F32),
        compiler_params=_params(("parallel",)),
    )(cs, w, b)


def _ada_bwd(cs, w, dmod):
    D, n = w.shape
    tn = _pick(n, (512, 384, 256, 128))

    def body(c_ref, w_ref, d_ref, gw_ref, da_ref):
        j = pl.program_id(0)
        cv = c_ref[...]
        a = cv * _sigmoid(cv)
        d = d_ref[...]
        gw_ref[...] = lax.dot_general(a, d, (((0,), (0,)), ((), ())), precision=HI, preferred_element_type=F32)

        @pl.when(j == 0)
        def _():
            da_ref[...] = jnp.zeros_like(da_ref)

        da_ref[...] += _dot_nt(d.astype(BF16), w_ref[...].astype(BF16))

    return _pcall(
        body, name="ada_bwd", grid=(n // tn,),
        in_specs=[pl.BlockSpec((16, D), lambda j: (0, 0)), pl.BlockSpec((D, tn), lambda j: (0, j)),
                  pl.BlockSpec((16, tn), lambda j: (0, j))],
        out_specs=[pl.BlockSpec((D, tn), lambda j: (0, j)), pl.BlockSpec((16, D), lambda j: (0, 0))],
        out_shape=[jax.ShapeDtypeStruct((D, n), F32), jax.ShapeDtypeStruct((16, D), F32)],
        compiler_params=_params(("arbitrary",)),
    )(cs, w, dmod)


def _rms1_fwd(xall, gain, shift2, scale2, n_ctx):
    T, D = xall.shape
    tb = _pick(n_ctx, (256, 128, 64, 32, 16))
    nctx = n_ctx // tb

    def body(x_ref, g_ref, sh_ref, sc_ref, o_ref):
        i = pl.program_id(0)
        xv = x_ref[...]
        r = lax.rsqrt(jnp.mean(xv * xv, axis=-1, keepdims=True) + EPS)
        nrm = xv * r * g_ref[...]
        lat = i >= nctx
        sh = jnp.where(lat, sh_ref[1:2, :], sh_ref[0:1, :])
        sc = jnp.where(lat, sc_ref[1:2, :], sc_ref[0:1, :])
        o_ref[...] = (nrm * (1.0 + sc) + sh).astype(BF16)

    vec = lambda r: pl.BlockSpec((r, D), lambda i: (0, 0))
    return _pcall(
        body, name="rms1_fwd", grid=(T // tb,),
        in_specs=[pl.BlockSpec((tb, D), lambda i: (i, 0)), vec(1), vec(2), vec(2)],
        out_specs=pl.BlockSpec((tb, D), lambda i: (i, 0)),
        out_shape=jax.ShapeDtypeStruct((T, D), BF16),
        compiler_params=_params(("parallel",)),
    )(xall, gain, shift2, scale2)


def _rms1_bwd(xall, dh, dxmid, gain, scale2, n_ctx):
    T, D = xall.shape
    L = T - n_ctx
    tb = _pick(n_ctx, (256, 128, 64, 32, 16))
    nctx = n_ctx // tb

    def body(x_ref, dh_ref, dxm_ref, g_ref, sc_ref, dx_ref, cs_ref):
        i = pl.program_id(0)
        lat = i >= nctx
        xv = x_ref[...]
        r = lax.rsqrt(jnp.mean(xv * xv, axis=-1, keepdims=True) + EPS)
        xh = xv * r
        g = g_ref[...]
        nrm = xh * g
        sc = jnp.where(lat, sc_ref[1:2, :], sc_ref[0:1, :])
        dhv = dh_ref[...]
        dn = dhv * (1.0 + sc)
        dxh = dn * g
        dxv = r * (dxh - xh * jnp.mean(dxh * xh, axis=-1, keepdims=True))
        s_sh = jnp.sum(dhv, axis=0, keepdims=True)
        s_sc = jnp.sum(dhv * nrm, axis=0, keepdims=True)
        s_g = jnp.sum(dn * xh, axis=0, keepdims=True)
        zero = jnp.zeros_like(s_sh)
        rows = lax.broadcasted_iota(jnp.int32, (8, D), 0)
        upd = jnp.where(rows == 0, jnp.where(lat, zero, s_sh),
              jnp.where(rows == 1, jnp.where(lat, zero, s_sc),
              jnp.where(rows == 2, jnp.where(lat, s_sh, zero),
              jnp.where(rows == 3, jnp.where(lat, s_sc, zero),
              jnp.where(rows == 4, s_g, 0.0)))))

        @pl.when(i == 0)
        def _():
            cs_ref[...] = jnp.zeros_like(cs_ref)

        cs_ref[...] += upd

        @pl.when(lat)
        def _():
            dx_ref[...] = dxv + dxm_ref[...]

    lat_blk = lambda i: (jnp.maximum(i - nctx, 0), 0)
    vec = lambda r: pl.BlockSpec((r, D), lambda i: (0, 0))
    return _pcall(
        body, name="rms1_bwd", grid=(T // tb,),
        in_specs=[pl.BlockSpec((tb, D), lambda i: (i, 0)), pl.BlockSpec((tb, D), lambda i: (i, 0)),
                  pl.BlockSpec((tb, D), lat_blk), vec(1), vec(2)],
        out_specs=[pl.BlockSpec((tb, D), lat_blk), vec(8)],
        out_shape=[jax.ShapeDtypeStruct((L, D), F32), jax.ShapeDtypeStruct((8, D), F32)],
        compiler_params=_params(("arbitrary",)),
    )(xall, dh, dxmid, gain, scale2)


def _resid_rms2_fwd(x, mo, vecs):
    L, D = x.shape
    tb = _pick(L, (256, 128, 64))

    def body(x_ref, mo_ref, v_ref, xm_ref, h_ref):
        xm = x_ref[...] + v_ref[0:1, :] * mo_ref[...]
        xm_ref[...] = xm
        r = lax.rsqrt(jnp.mean(xm * xm, axis=-1, keepdims=True) + EPS)
        h_ref[...] = (xm * r * v_ref[1:2, :] * (1.0 + v_ref[3:4, :]) + v_ref[2:3, :]).astype(BF16)

    blk = pl.BlockSpec((tb, D), lambda i: (i, 0))
    return _pcall(
        body, name="resid_rms2_fwd", grid=(L // tb,),
        in_specs=[blk, blk, pl.BlockSpec((8, D), lambda i: (0, 0))],
        out_specs=[blk, blk],
        out_shape=[jax.ShapeDtypeStruct((L, D), F32), jax.ShapeDtypeStruct((L, D), BF16)],
        compiler_params=_params(("parallel",)),
    )(x, mo, vecs)


def _resid_rms2_bwd(xmid, dh_a, dh_b, dy, mo, vecs):
    L, D = xmid.shape
    tb = _pick(L, (256, 128, 64))

    def body(xm_ref, da_ref, db_ref, dy_ref, mo_ref, v_ref, dxm_ref, dmo_ref, cs_ref):
        i = pl.program_id(0)
        xm = xm_ref[...]
        r = lax.rsqrt(jnp.mean(xm * xm, axis=-1, keepdims=True) + EPS)
        xh = xm * r
        g = v_ref[1:2, :]
        nrm = xh * g
        dhv = da_ref[...] + db_ref[...]
        dn = dhv * (1.0 + v_ref[3:4, :])
        dxh = dn * g
        dxm = dy_ref[...] + r * (dxh - xh * jnp.mean(dxh * xh, axis=-1, keepdims=True))
        dxm_ref[...] = dxm
        dmo_ref[...] = (dxm * v_ref[0:1, :]).astype(BF16)
        s0 = jnp.sum(dhv, axis=0, keepdims=True)
        s1 = jnp.sum(dhv * nrm, axis=0, keepdims=True)
        s2 = jnp.sum(dn * xh, axis=0, keepdims=True)
        s3 = jnp.sum(dxm * mo_ref[...], axis=0, keepdims=True)
        rows = lax.broadcasted_iota(jnp.int32, (8, D), 0)
        upd = jnp.where(rows == 0, s0, jnp.where(rows == 1, s1, jnp.where(rows == 2, s2,
              jnp.where(rows == 3, s3, 0.0))))

        @pl.when(i == 0)
        def _():
            cs_ref[...] = jnp.zeros_like(cs_ref)

        cs_ref[...] += upd

    blk = pl.BlockSpec((tb, D), lambda i: (i, 0))
    vec = pl.BlockSpec((8, D), lambda i: (0, 0))
    return _pcall(
        body, name="resid_rms2_bwd", grid=(L // tb,),
        in_specs=[blk, blk, blk, blk, blk, vec],
        out_specs=[blk, blk, vec],
        out_shape=[jax.ShapeDtypeStruct((L, D), F32), jax.ShapeDtypeStruct((L, D), BF16),
                   jax.ShapeDtypeStruct((8, D), F32)],
        compiler_params=_params(("arbitrary",)),
    )(xmid, dh_a, dh_b, dy, mo, vecs)


def _loss_head(xmid, f, g2, target):
    L, D = xmid.shape
    tb = _pick(L, (256, 128, 64))

    def body(xm_ref, f_ref, g_ref, t_ref, dy_ref, df_ref, s_ref):
        i = pl.program_id(0)
        fv = f_ref[...]
        g = g_ref[...]
        err = xm_ref[...] + g * fv - t_ref[...]
        dy = err * (1.0 / D)
        dy_ref[...] = dy
        df_ref[...] = (dy * g).astype(BF16)
        s0 = jnp.sum(dy * fv, axis=0, keepdims=True)
        part = 0.5 * jnp.sum(jnp.mean(err * err, axis=-1, keepdims=True), axis=0, keepdims=True)
        rows = lax.broadcasted_iota(jnp.int32, (8, D), 0)
        upd = jnp.where(rows == 0, s0, jnp.where(rows == 1, part, 0.0))

        @pl.when(i == 0)
        def _():
            s_ref[...] = jnp.zeros_like(s_ref)

        s_ref[...] += upd

    blk = pl.BlockSpec((tb, D), lambda i: (i, 0))
    return _pcall(
        body, name="loss_head", grid=(L // tb,),
        in_specs=[blk, blk, pl.BlockSpec((1, D), lambda i: (0, 0)), blk],
        out_specs=[blk, blk, pl.BlockSpec((8, D), lambda i: (0, 0))],
        out_shape=[jax.ShapeDtypeStruct((L, D), F32), jax.ShapeDtypeStruct((L, D), BF16),
                   jax.ShapeDtypeStruct((8, D), F32)],
        compiler_params=_params(("arbitrary",)),
    )(xmid, f, g2, target)


def _gate_cols(D, off):
    tc = _pick(np.gcd(D, off), (512, 256, 128))
    return tc, off // tc


def _merge_fwd(za, zb, p, n_ctx, off_a, off_b):
    L, D = za.shape
    tb = _pick(n_ctx, (256, 128, 64, 32, 16))
    nctx = n_ctx // tb
    tc, oa = _gate_cols(D, off_a)
    _, ob = _gate_cols(D, off_b)
    if off_b % tc:
        raise ValueError("gate column offsets must share a column tile")
    ob = off_b // tc

    def body(za_ref, zb_ref, ga_ref, gb_ref, z_ref):
        z_ref[...] = (_sigmoid(ga_ref[...]) * za_ref[...] + _sigmoid(gb_ref[...]) * zb_ref[...]).astype(BF16)

    blk = pl.BlockSpec((tb, tc), lambda i, j: (i, j))
    return _pcall(
        body, name="merge_fwd", grid=(L // tb, D // tc),
        in_specs=[blk, blk, pl.BlockSpec((tb, tc), lambda i, j: (i + nctx, oa + j)),
                  pl.BlockSpec((tb, tc), lambda i, j: (i + nctx, ob + j))],
        out_specs=blk,
        out_shape=jax.ShapeDtypeStruct((L, D), BF16),
        compiler_params=_params(("parallel", "parallel")),
    )(za, zb, p, p)


def _merge_bwd(dz, za, zb, p, n_ctx, off_a, off_b):
    L, D = za.shape
    T = L + n_ctx
    tb = _pick(n_ctx, (256, 128, 64, 32, 16))
    nctx = n_ctx // tb
    tc = _gate_cols(D, off_a)[0]
    oa, ob = off_a // tc, off_b // tc

    def body(dz_ref, za_ref, zb_ref, ga_ref, gb_ref, dza_ref, dzb_ref, dga_ref, dgb_ref):
        i = pl.program_id(1)

        @pl.when(i < nctx)
        def _():
            dga_ref[...] = jnp.zeros_like(dga_ref)
            dgb_ref[...] = jnp.zeros_like(dgb_ref)

        @pl.when(i >= nctx)
        def _():
            dzv = dz_ref[...]
            sa = _sigmoid(ga_ref[...])
            sb = _sigmoid(gb_ref[...])
            dza_ref[...] = (dzv * sa).astype(BF16)
            dzb_ref[...] = (dzv * sb).astype(BF16)
            dga_ref[...] = (dzv * za_ref[...] * sa * (1.0 - sa)).astype(BF16)
            dgb_ref[...] = (dzv * zb_ref[...] * sb * (1.0 - sb)).astype(BF16)

    lat = pl.BlockSpec((tb, tc), lambda j, i: (jnp.maximum(i - nctx, 0), j))
    allr = pl.BlockSpec((tb, tc), lambda j, i: (i, j))
    return _pcall(
        body, name="merge_bwd", grid=(D // tc, T // tb),
        in_specs=[lat, lat, lat, pl.BlockSpec((tb, tc), lambda j, i: (i, oa + j)),
                  pl.BlockSpec((tb, tc), lambda j, i: (i, ob + j))],
        out_specs=[lat, lat, allr, allr],
        out_shape=[jax.ShapeDtypeStruct((L, D), BF16), jax.ShapeDtypeStruct((L, D), BF16),
                   jax.ShapeDtypeStruct((T, D), BF16), jax.ShapeDtypeStruct((T, D), BF16)],
        compiler_params=_params(("arbitrary", "arbitrary")),
    )(dz, za, zb, p, p)


def _shift_down(u, rows):
    return jnp.where(rows == 0, 0.0, pltpu.roll(u, 1, 0))


def _shift_up(u, rows):
    n = u.shape[0]
    return jnp.where(rows == n - 1, 0.0, pltpu.roll(u, n - 1, 0))


def _convgate_fwd(u1, u3, cw, cb):
    L, F = u1.shape
    tc = _pick(F, (256, 128))

    def body(u1_ref, u3_ref, w_ref, b_ref, a_ref):
        u = u1_ref[...]
        rows = lax.broadcasted_iota(jnp.int32, u.shape, 0)
        cv = _shift_down(u, rows) * w_ref[0:1, :] + u * w_ref[1:2, :] + _shift_up(u, rows) * w_ref[2:3, :] + b_ref[...]
        a_ref[...] = (cv * _sigmoid(cv) * u3_ref[...]).astype(BF16)

    blk = pl.BlockSpec((L, tc), lambda j: (0, j))
    return _pcall(
        body, name="convgate_fwd", grid=(F // tc,),
        in_specs=[blk, blk, pl.BlockSpec((8, tc), lambda j: (0, j)), pl.BlockSpec((1, tc), lambda j: (0, j))],
        out_specs=blk,
        out_shape=jax.ShapeDtypeStruct((L, F), BF16),
        compiler_params=_params(("parallel",)),
    )(u1, u3, cw, cb)


def _convgate_bwd(u1, u3, da, cw, cb):
    L, F = u1.shape
    tc = _pick(F, (256, 128))

    def body(u1_ref, u3_ref, da_ref, w_ref, b_ref, du1_ref, du3_ref, s_ref):
        u = u1_ref[...]
        rows = lax.broadcasted_iota(jnp.int32, u.shape, 0)
        um, up = _shift_down(u, rows), _shift_up(u, rows)
        w0, w1, w2 = w_ref[0:1, :], w_ref[1:2, :], w_ref[2:3, :]
        cv = um * w0 + u * w1 + up * w2 + b_ref[...]
        s = _sigmoid(cv)
        dav = da_ref[...]
        du3_ref[...] = (dav * cv * s).astype(BF16)
        dcv = dav * u3_ref[...] * (s * (1.0 + cv * (1.0 - s)))
        du1_ref[...] = (_shift_up(dcv, rows) * w0 + dcv * w1 + _shift_down(dcv, rows) * w2).astype(BF16)
        r8 = lax.broadcasted_iota(jnp.int32, (8, tc), 0)
        s0 = jnp.sum(dcv * um, axis=0, keepdims=True)
        s1 = jnp.sum(dcv * u, axis=0, keepdims=True)
        s2 = jnp.sum(dcv * up, axis=0, keepdims=True)
        s3 = jnp.sum(dcv, axis=0, keepdims=True)
        s_ref[...] = jnp.where(r8 == 0, s0, jnp.where(r8 == 1, s1, jnp.where(r8 == 2, s2,
                     jnp.where(r8 == 3, s3, 0.0))))

    blk = pl.BlockSpec((L, tc), lambda j: (0, j))
    v8 = pl.BlockSpec((8, tc), lambda j: (0, j))
    return _pcall(
        body, name="convgate_bwd", grid=(F // tc,),
        in_specs=[blk, blk, blk, v8, pl.BlockSpec((1, tc), lambda j: (0, j))],
        out_specs=[blk, blk, v8],
        out_shape=[jax.ShapeDtypeStruct((L, F), BF16), jax.ShapeDtypeStruct((L, F), BF16),
                   jax.ShapeDtypeStruct((8, F), F32)],
        compiler_params=_params(("parallel",)),
    )(u1, u3, da, cw, cb)


def _lower_bound(lbl_ref, d):
    l0, l1 = lbl_ref[d, 0:1, :], lbl_ref[d, 1:2, :]
    m = jnp.maximum(l0, l1)
    e0, e1 = jnp.exp(l0 - m), jnp.exp(l1 - m)
    return e0 / (e0 + e1)


def _chunk_cumsum(x, rev):
    n = x.shape[0]
    r = lax.broadcasted_iota(jnp.int32, x.shape, 0) % CHUNK
    k = 1
    while k < CHUNK:
        if rev:
            x = x + jnp.where(r < CHUNK - k, pltpu.roll(x, n - k, 0), 0.0)
        else:
            x = x + jnp.where(r >= k, pltpu.roll(x, k, 0), 0.0)
        k *= 2
    return x


def _gate_terms(z, lb):
    sg = _sigmoid(z)
    f = lb + (1.0 - lb) * sg
    return sg, f


def _decay_terms(z, lb, rev):
    _, f = _gate_terms(z, lb)
    g = jnp.log(f)
    return 1.0 - f, _chunk_cumsum(g, rev), _chunk_cumsum(g, not rev) - g


def _chunk_total(c, rev):
    return c[0:1, :] if rev else c[CHUNK - 1:CHUNK, :]


def _pair_decay(c, s, rev):
    t = lax.broadcasted_iota(jnp.int32, (CHUNK, 1), 0)
    later = (t <= s) if rev else (t >= s)
    return jnp.where(later, jnp.exp(c - c[s:s + 1, :]), 0.0)


def _scan_chunk(i, n_ctx_chunks, n_chunks, rev):
    if not rev:
        return i
    return jnp.where(i < n_ctx_chunks, n_ctx_chunks - 1 - i, n_chunks + n_ctx_chunks - 1 - i)


def _rows(ci):
    return pl.ds(pl.multiple_of(ci * CHUNK, CHUNK), CHUNK)


def _hgrn_cols(HA):
    return HA // HEAD


def _hgrn_fwd(p, lbl, ng, n_ctx, HA):
    T = p.shape[0]
    L = T - n_ctx
    nh = _hgrn_cols(HA)
    nc, ncc = T // CHUNK, n_ctx // CHUNK

    def body(q_ref, zf_ref, zb_ref, v_ref, og_ref, lbl_ref, ng_ref, ya_ref, o_ref, st_ref,
             c_scr, k_scr, qe_scr, ke_scr, o_scr):
        dirs = ((0, False, zf_ref), (1, True, zb_ref))
        for d, rev, z_ref in dirs:
            k, c, rest = _decay_terms(z_ref[...], _lower_bound(lbl_ref, d), rev)
            c_scr[d] = c
            k_scr[d] = k
            qe_scr[d] = (q_ref[...] * jnp.exp(c)).astype(BF16)
            ke_scr[d] = (k * jnp.exp(rest)).astype(BF16)

        def step(i, states):
            new = []
            for (d, rev, _), St in zip(dirs, states):
                ci = _scan_chunk(i, ncc, nc, rev)
                rows = _rows(ci)
                q, v, c, k = q_ref[rows, :], v_ref[rows, :], c_scr[d, rows, :], k_scr[d, rows, :]
                st_ref[0, d, ci] = St.astype(BF16)
                o = jnp.zeros((CHUNK, HEAD), F32)
                for s in range(CHUNK):
                    E = _pair_decay(c, s, rev)
                    a = jnp.sum(q * E * k[s:s + 1, :], axis=1, keepdims=True)
                    o = o + a * v[s:s + 1, :]
                o_scr[d, rows, :] = o + _dot_nt(qe_scr[d, rows, :], St.astype(BF16))
                new.append(St * jnp.exp(_chunk_total(c, rev)) + _dot_tn(v.astype(BF16), ke_scr[d, rows, :]))
            return tuple(new)

        zero = jnp.zeros((HEAD, HEAD), F32)
        lax.fori_loop(0, nc, step, (zero, zero), unroll=2)

        o = o_scr[0, pl.ds(n_ctx, L), :] + o_scr[1, pl.ds(n_ctx, L), :]
        o_ref[...] = o
        r = lax.rsqrt(jnp.mean(o * o, axis=-1, keepdims=True) + EPS)
        og = og_ref[pl.ds(n_ctx, L), :]
        ya_ref[...] =(o * r * ng_ref[...] * (og * _sigmoid(og))).astype(BF16)

    cb = HA // HEAD
    col = lambda kk: pl.BlockSpec((T, HEAD), lambda h: (0, kk * cb + h))
    return _pcall(
        body, name="hgrn_fwd", grid=(nh,),
        in_specs=[col(0), col(1), col(2), col(3), col(4),
                  pl.BlockSpec((2, 2, HEAD), lambda h: (0, 0, h)), pl.BlockSpec((1, HEAD), lambda h: (0, 0))],
        out_specs=[pl.BlockSpec((L, HEAD), lambda h: (0, h)), pl.BlockSpec((L, HEAD), lambda h: (0, h)),
                   pl.BlockSpec((1, 2, nc, HEAD, HEAD), lambda h: (h, 0, 0, 0, 0))],
        out_shape=[jax.ShapeDtypeStruct((L, HA), BF16), jax.ShapeDtypeStruct((L, HA), F32),
                   jax.ShapeDtypeStruct((nh, 2, nc, HEAD, HEAD), BF16)],
        scratch_shapes=[pltpu.VMEM((2, T, HEAD), F32), pltpu.VMEM((2, T, HEAD), F32),
                        pltpu.VMEM((2, T, HEAD), BF16), pltpu.VMEM((2, T, HEAD), BF16),
                        pltpu.VMEM((2, T, HEAD), F32)],
        compiler_params=_params(("parallel",)),
    )(p, p, p, p, p, lbl, ng)


def _hgrn_bwd(p, lbl, ng, o, dya, st, n_ctx, HA):
    T = p.shape[0]
    L = T - n_ctx
    nh = _hgrn_cols(HA)
    nc, ncc = T // CHUNK, n_ctx // CHUNK

    def body(q_ref, zf_ref, zb_ref, v_ref, og_ref, lbl_ref, ng_ref, o_ref, dya_ref, st_ref,
             dq_ref, dzf_ref, dzb_ref, dv_ref, dog_ref, dlbl_ref, dng_ref,
             do_scr, c_scr, k_scr, qe_scr, ke_scr, dg_scr, dk_scr, dq_scr, dv_scr):
        h = pl.program_id(0)
        ov = o_ref[...]
        r = lax.rsqrt(jnp.mean(ov * ov, axis=-1, keepdims=True) + EPS)
        oh = ov * r
        ogv = og_ref[pl.ds(n_ctx, L), :]
        sg_o = _sigmoid(ogv)
        dyv = dya_ref[...]
        ngv = ng_ref[...]
        dog_ref[pl.ds(0, n_ctx), :] = jnp.zeros((n_ctx, HEAD), BF16)
        dog_ref[pl.ds(n_ctx, L), :] = (dyv * oh * ngv * (sg_o * (1.0 + ogv * (1.0 - sg_o)))).astype(BF16)
        don = dyv * (ogv * sg_o)
        dng = jnp.sum(don * oh, axis=0, keepdims=True)
        doh = don * ngv
        do_scr[pl.ds(0, n_ctx), :] = jnp.zeros((n_ctx, HEAD), F32)
        do_scr[pl.ds(n_ctx, L), :] = r * (doh - oh * jnp.mean(doh * oh, axis=-1, keepdims=True))

        @pl.when(h == 0)
        def _():
            dng_ref[...] = jnp.zeros_like(dng_ref)

        dng_ref[0:1, :] += dng

        t16 = lax.broadcasted_iota(jnp.int32, (CHUNK, HEAD), 0)
        dirs = ((0, False, zf_ref, dzf_ref), (1, True, zb_ref, dzb_ref))
        for d, rev, z_ref, _ in dirs:
            k, c, rest = _decay_terms(z_ref[...], _lower_bound(lbl_ref, d), rev)
            c_scr[d] = c
            k_scr[d] = k
            qe_scr[d] = (q_ref[...] * jnp.exp(c)).astype(BF16)
            ke_scr[d] = (k * jnp.exp(rest)).astype(BF16)
        dq_scr[...] = jnp.zeros_like(dq_scr)
        dv_scr[...] = jnp.zeros_like(dv_scr)

        zero = jnp.zeros((HEAD, HEAD), F32)

        def bwd_step(ii, carry):
            i = nc - 1 - ii
            new = []
            for (d, rev, _, _), dSt in zip(dirs, carry):
                ci = _scan_chunk(i, ncc, nc, rev)
                rows = _rows(ci)
                q, v, do = q_ref[rows, :], v_ref[rows, :], do_scr[rows, :]
                c, k = c_scr[d, rows, :], k_scr[d, rows, :]
                tot = _chunk_total(c, rev)
                etot = jnp.exp(tot)
                St = st_ref[0, d, ci]
                dSb = dSt.astype(BF16)
                do_b = do.astype(BF16)
                dq_x = _dot(do_b, St) * jnp.exp(c)
                dk_x = _dot(v.astype(BF16), dSb) * jnp.exp(tot - c)
                dv_x = _dot_nt(ke_scr[d, rows, :], dSb)
                dtot = (jnp.sum(St.astype(F32) * dSt, axis=0, keepdims=True) * etot
                        + jnp.sum(k * dk_x, axis=0, keepdims=True))
                dq = dk = dv = jnp.zeros((CHUNK, HEAD), F32)
                for s in range(CHUNK):
                    E = _pair_decay(c, s, rev)
                    XE = E * k[s:s + 1, :]
                    a = jnp.sum(q * XE, axis=1, keepdims=True)
                    da = jnp.sum(do * v[s:s + 1, :], axis=1, keepdims=True)
                    dq = dq + da * XE
                    dk_row = jnp.sum(da * q * E, axis=0, keepdims=True)
                    dv_row = jnp.sum(a * do, axis=0, keepdims=True)
                    dk = dk + jnp.where(t16 == s, dk_row, 0.0)
                    dv = dv + jnp.where(t16 == s, dv_row, 0.0)
                dq, dk, dv = dq + dq_x, dk + dk_x, dv + dv_x
                dg_scr[d, rows, :] = _chunk_cumsum(q * dq - k * dk, not rev) + dtot
                dk_scr[d, rows, :] = dk
                dq_scr[rows, :] += dq
                dv_scr[rows, :] += dv
                new.append(dSt * etot + _dot_tn(do_b, qe_scr[d, rows, :]))
            return tuple(new)

        lax.fori_loop(0, nc, bwd_step, (zero, zero), unroll=2)

        for d, _, z_ref, dz_ref in dirs:
            lb = _lower_bound(lbl_ref, d)
            sg, f = _gate_terms(z_ref[...], lb)
            df = dg_scr[d] / f - dk_scr[d]
            dz_ref[...] = (df * (1.0 - lb) * sg * (1.0 - sg)).astype(BF16)
            dl0 = jnp.sum(df * (1.0 - sg), axis=0, keepdims=True) * lb * (1.0 - lb)
            dlbl_ref[d, 0:1, :] = dl0
            dlbl_ref[d, 1:2, :] = -dl0
        dq_ref[...] = dq_scr[...].astype(BF16)
        dv_ref[...] = dv_scr[...].astype(BF16)

    cb = HA // HEAD
    col = lambda kk: pl.BlockSpec((T, HEAD), lambda h: (0, kk * cb + h))
    tcol = pl.BlockSpec((T, HEAD), lambda h: (0, h))
    lcol = pl.BlockSpec((L, HEAD), lambda h: (0, h))
    outs = _pcall(
        body, name="hgrn_bwd", grid=(nh,),
        in_specs=[col(0), col(1), col(2), col(3), col(4),
                  pl.BlockSpec((2, 2, HEAD), lambda h: (0, 0, h)), pl.BlockSpec((1, HEAD), lambda h: (0, 0)),
                  lcol, lcol,
                  pl.BlockSpec((1, 2, nc, HEAD, HEAD), lambda h: (h, 0, 0, 0, 0), pipeline_mode=pl.Buffered(1))],
        out_specs=[tcol, tcol, tcol, tcol, tcol, pl.BlockSpec((2, 2, HEAD), lambda h: (0, 0, h)),
                   pl.BlockSpec((8, HEAD), lambda h: (0, 0))],
        out_shape=[jax.ShapeDtypeStruct((T, HA), BF16)] * 5 + [jax.ShapeDtypeStruct((2, 2, HA), F32),
                                                               jax.ShapeDtypeStruct((8, HEAD), F32)],
        scratch_shapes=[pltpu.VMEM((T, HEAD), F32),
                        pltpu.VMEM((2, T, HEAD), F32), pltpu.VMEM((2, T, HEAD), F32),
                        pltpu.VMEM((2, T, HEAD), BF16), pltpu.VMEM((2, T, HEAD), BF16),
                        pltpu.VMEM((2, T, HEAD), F32), pltpu.VMEM((2, T, HEAD), F32),
                        pltpu.VMEM((T, HEAD), F32), pltpu.VMEM((T, HEAD), F32)],
        compiler_params=_params(("arbitrary",)),
    )(p, p, p, p, p, lbl, ng, o, dya, st)
    return outs


def _swap_halves(t, lane):
    q = HEAD // 4
    return jnp.where((lane % (2 * q)) < q, pltpu.roll(t, HEAD - q, 1), pltpu.roll(t, q, 1))


def _qk_norm(t, g):
    r = lax.rsqrt(jnp.mean(t * t, axis=-1, keepdims=True) + EPS)
    return t * r, r


def _rope(t, cos, sin, lane):
    return t * cos + _swap_halves(t, lane) * sin


def _qk_norm_bwd(dy, th, r, g):
    dth = dy * g
    return r * (dth - th * jnp.mean(dth * th, axis=-1, keepdims=True)), jnp.sum(dy * th, axis=0, keepdims=True)


def _rope_bwd(dy, cos, sin, lane):
    return dy * cos + _swap_halves(dy * sin, lane)


def _na_geometry(L):
    n_rows = L // GRID_W
    kr = min(WIN_R, n_rows)
    return n_rows, kr


def _na_prep(q_ref, k_ref, v_ref, gq_ref, gk_ref, cos_ref, sin_ref, qs, ks, vs, n_ctx, L):
    lane = lax.broadcasted_iota(jnp.int32, (L, HEAD), 1)
    cos, sin = cos_ref[...], sin_ref[...]
    qh, _ = _qk_norm(q_ref[pl.ds(n_ctx, L), :], None)
    qs[...] = _rope(qh * gq_ref[...], cos, sin, lane).astype(BF16)
    kh, _ = _qk_norm(k_ref[pl.ds(n_ctx, L), :], None)
    ks[pl.ds(n_ctx, L), :] = _rope(kh * gk_ref[...], cos, sin, lane).astype(BF16)
    kc, _ = _qk_norm(k_ref[pl.ds(0, n_ctx), :], None)
    ks[pl.ds(0, n_ctx), :] = (kc * gk_ref[...]).astype(BF16)
    vs[...] = v_ref[...].astype(BF16)


def _na_scores(r, qs, ks, bias_ref, n_ctx, n_rows, kr):
    scale = HEAD ** -0.5
    r0 = jnp.clip(r - WIN_R // 2, 0, n_rows - kr)
    qrows = pl.ds(pl.multiple_of(r * GRID_W, GRID_W), GRID_W)
    krows = pl.ds(pl.multiple_of(n_ctx + r0 * GRID_W, GRID_W), kr * GRID_W)
    qv = qs[qrows, :]
    sb = _dot_nt(qv, ks[krows, :]) * scale
    b0 = r0 - r + (WIN_R - 1)
    sb = sb + jnp.concatenate([bias_ref[0, b0 + 2 * jj] for jj in range(kr // 2)], axis=1)
    sc = _dot_nt(qv, ks[pl.ds(0, n_ctx), :]) * scale
    m = jnp.maximum(jnp.max(sb, axis=1, keepdims=True), jnp.max(sc, axis=1, keepdims=True))
    eb, ec = jnp.exp(sb - m), jnp.exp(sc - m)
    inv = 1.0 / (jnp.sum(eb, axis=1, keepdims=True) + jnp.sum(ec, axis=1, keepdims=True))
    return eb * inv, ec * inv, qrows, krows, b0


def _na_fwd(p, bias, gq, gk, cos, sin, n_ctx, off, HB):
    T = p.shape[0]
    L = T - n_ctx
    nh = HB // HEAD
    n_rows, kr = _na_geometry(L)
    ob = off // HEAD

    def body(q_ref, k_ref, v_ref, bias_ref, gq_ref, gk_ref, cos_ref, sin_ref, y_ref, qs, ks, vs):
        _na_prep(q_ref, k_ref, v_ref, gq_ref, gk_ref, cos_ref, sin_ref, qs, ks, vs, n_ctx, L)

        def step(i, carry):
            for u in range(2):
                pb, pc, qrows, krows, _ = _na_scores(2 * i + u, qs, ks, bias_ref, n_ctx, n_rows, kr)
                y = _dot(pb.astype(BF16), vs[krows, :]) + _dot(pc.astype(BF16), vs[pl.ds(0, n_ctx), :])
                y_ref[qrows, :] = y.astype(BF16)
            return carry

        lax.fori_loop(0, n_rows // 2, step, 0)

    col = lambda kk: pl.BlockSpec((T, HEAD), lambda h: (0, ob + kk * nh + h))
    vec = pl.BlockSpec((1, HEAD), lambda h: (0, 0))
    tab = pl.BlockSpec((L, HEAD), lambda h: (0, 0))
    return _pcall(
        body, name="na_fwd", grid=(nh,),
        in_specs=[col(0), col(1), col(2), pl.BlockSpec((1,) + bias.shape[1:], lambda h: (h, 0, 0, 0)),
                  vec, vec, tab, tab],
        out_specs=pl.BlockSpec((L, HEAD), lambda h: (0, h)),
        out_shape=jax.ShapeDtypeStruct((L, HB), BF16),
        scratch_shapes=[pltpu.VMEM((L, HEAD), BF16), pltpu.VMEM((T, HEAD), BF16), pltpu.VMEM((T, HEAD), BF16)],
        compiler_params=_params(("parallel",)),
    )(p, p, p, bias, gq, gk, cos, sin)


def _na_bwd(p, bias, gq, gk, cos, sin, dyb, n_ctx, off, HB):
    T = p.shape[0]
    L = T - n_ctx
    nh = HB // HEAD
    n_rows, kr = _na_geometry(L)
    ob = off // HEAD
    scale = HEAD ** -0.5

    def body(q_ref, k_ref, v_ref, bias_ref, gq_ref, gk_ref, cos_ref, sin_ref, dy_ref,
             dq_ref, dk_ref, dv_ref, dbias_ref, dg_ref, qs, ks, vs, dqa, dka, dva):
        h = pl.program_id(0)
        _na_prep(q_ref, k_ref, v_ref, gq_ref, gk_ref, cos_ref, sin_ref, qs, ks, vs, n_ctx, L)
        dka[...] = jnp.zeros_like(dka)
        dva[...] = jnp.zeros_like(dva)
        dbias_ref[...] = jnp.zeros_like(dbias_ref)

        crows = pl.ds(0, n_ctx)

        def step(i, carry):
            done = []
            for u in range(2):
                pb, pc, qrows, krows, b0 = _na_scores(2 * i + u, qs, ks, bias_ref, n_ctx, n_rows, kr)
                do = dy_ref[qrows, :]
                qv = qs[qrows, :]
                dpb = _dot_nt(do, vs[krows, :])
                dpc = _dot_nt(do, vs[crows, :])
                delta = jnp.sum(pb * dpb, axis=1, keepdims=True) + jnp.sum(pc * dpc, axis=1, keepdims=True)
                dsb = pb * (dpb - delta)
                dsc = pc * (dpc - delta)
                dsb_b, dsc_b = dsb.astype(BF16), dsc.astype(BF16)
                dqa[qrows, :] = (_dot(dsb_b, ks[krows, :]) + _dot(dsc_b, ks[crows, :])) * scale
                done.append((krows, b0, dsb, _dot_tn(dsb_b, qv) * scale, _dot_tn(dsc_b, qv) * scale,
                             _dot_tn(pb.astype(BF16), do), _dot_tn(pc.astype(BF16), do)))
            for krows, b0, dsb, dkb, dkc, dvb, dvc in done:
                for jj in range(kr // 2):
                    dbias_ref[0, b0 + 2 * jj] += dsb[:, jj * 2 * GRID_W:(jj + 1) * 2 * GRID_W]
                dka[krows, :] += dkb
                dka[crows, :] += dkc
                dva[krows, :] += dvb
                dva[crows, :] += dvc
            return carry

        lax.fori_loop(0, n_rows // 2, step, 0)

        lane = lax.broadcasted_iota(jnp.int32, (L, HEAD), 1)
        cos, sin = cos_ref[...], sin_ref[...]
        lat, ctx = pl.ds(n_ctx, L), pl.ds(0, n_ctx)
        gqv, gkv = gq_ref[...], gk_ref[...]
        qh, rq = _qk_norm(q_ref[lat, :], None)
        dq, dgq = _qk_norm_bwd(_rope_bwd(dqa[...], cos, sin, lane), qh, rq, gqv)
        dq_ref[ctx, :] = jnp.zeros((n_ctx, HEAD), BF16)
        dq_ref[lat, :] = dq.astype(BF16)
        kh, rk = _qk_norm(k_ref[lat, :], None)
        dk, dgk = _qk_norm_bwd(_rope_bwd(dka[lat, :], cos, sin, lane), kh, rk, gkv)
        dk_ref[lat, :] = dk.astype(BF16)
        kch, rkc = _qk_norm(k_ref[ctx, :], None)
        dkc, dgkc = _qk_norm_bwd(dka[ctx, :], kch, rkc, gkv)
        dk_ref[ctx, :] = dkc.astype(BF16)
        dv_ref[...] = dva[...].astype(BF16)

        @pl.when(h == 0)
        def _():
            dg_ref[...] = jnp.zeros_like(dg_ref)

        dg_ref[0:1, :] += dgq
        dg_ref[1:2, :] += dgk + dgkc

    col = lambda kk: pl.BlockSpec((T, HEAD), lambda h: (0, ob + kk * nh + h))
    vec = pl.BlockSpec((1, HEAD), lambda h: (0, 0))
    tab = pl.BlockSpec((L, HEAD), lambda h: (0, 0))
    tcol = pl.BlockSpec((T, HEAD), lambda h: (0, h))
    bspec = pl.BlockSpec((1,) + bias.shape[1:], lambda h: (h, 0, 0, 0))
    return _pcall(
        body, name="na_bwd", grid=(nh,),
        in_specs=[col(0), col(1), col(2), bspec, vec, vec, tab, tab, pl.BlockSpec((L, HEAD), lambda h: (0, h))],
        out_specs=[tcol, tcol, tcol, bspec, pl.BlockSpec((8, HEAD), lambda h: (0, 0))],
        out_shape=[jax.ShapeDtypeStruct((T, HB), BF16)] * 3 + [jax.ShapeDtypeStruct(bias.shape, F32),
                                                               jax.ShapeDtypeStruct((8, HEAD), F32)],
        scratch_shapes=[pltpu.VMEM((L, HEAD), BF16), pltpu.VMEM((T, HEAD), BF16), pltpu.VMEM((T, HEAD), BF16),
                        pltpu.VMEM((L, HEAD), F32), pltpu.VMEM((T, HEAD), F32), pltpu.VMEM((T, HEAD), F32)],
        compiler_params=_params(("arbitrary",)),
    )(p, p, p, bias, gq, gk, cos, sin, dyb)


def _bias_tables():
    w = np.arange(GRID_W)
    col_start = np.clip(w - WIN_C // 2, 0, GRID_W - WIN_C)
    col_in = (w[None, :] >= col_start[:, None]) & (w[None, :] < col_start[:, None] + WIN_C)
    dc = np.clip(w[None, :] - w[:, None], -(WIN_C - 1), WIN_C - 1) + WIN_C - 1
    n_pair = 2 * WIN_R
    ridx = np.zeros((n_pair, GRID_W, 2 * GRID_W), np.int32)
    cidx = np.zeros((n_pair, GRID_W, 2 * GRID_W), np.int32)
    valid = np.zeros((n_pair, GRID_W, 2 * GRID_W), bool)
    for i in range(n_pair):
        for half in range(2):
            row = i + half
            sl = slice(half * GRID_W, (half + 1) * GRID_W)
            ridx[i, :, sl] = min(row, 2 * WIN_R - 2)
            cidx[i, :, sl] = dc
            valid[i, :, sl] = col_in & (row <= 2 * WIN_R - 2)
    return ridx, cidx, valid


def _bias_onehot():
    _, cidx, valid = _bias_tables()
    K = GRID_W * 2 * GRID_W
    oh = np.zeros((K, 128), np.float32)
    neg = np.full((1, K), NEG, np.float32)
    for cq in range(GRID_W):
        for ll in range(2 * GRID_W):
            if valid[0, cq, ll]:
                oh[cq * 2 * GRID_W + ll, (ll // GRID_W) * 64 + cidx[0, cq, ll]] = 1.0
                neg[0, cq * 2 * GRID_W + ll] = 0.0
    return oh, neg


def _expand_bias(table):
    H = table.shape[0]
    n_pair, n_dc = 2 * WIN_R, 2 * WIN_C - 1
    tp = jnp.pad(table, ((0, 0), (0, n_pair + 1 - table.shape[1]), (0, 64 - n_dc)))
    t2 = jnp.concatenate([tp[:, :n_pair], tp[:, 1:n_pair + 1]], axis=-1).reshape(H * n_pair, 128)
    oh, neg = _bias_onehot()

    def body(t_ref, oh_ref, neg_ref, o_ref):
        o_ref[...] = lax.dot_general(t_ref[...], oh_ref[...], (((1,), (1,)), ((), ())), precision=HI,
                                     preferred_element_type=F32) + neg_ref[...]

    out = _pcall(body, name="bias_expand", out_shape=jax.ShapeDtypeStruct((H * n_pair, oh.shape[0]), F32),
                         compiler_params=_params())(t2, jnp.asarray(oh), jnp.asarray(neg))
    return out.reshape(H, n_pair, GRID_W, 2 * GRID_W)


def _bias_grad(dbias):
    H = dbias.shape[0]
    n_pair, n_dc = 2 * WIN_R, 2 * WIN_C - 1
    K = GRID_W * 2 * GRID_W
    oh, _ = _bias_onehot()
    flat = dbias.reshape(H * n_pair, K)

    def body(d_ref, oh_ref, o_ref):
        o_ref[...] = jnp.dot(d_ref[...], oh_ref[...], precision=HI, preferred_element_type=F32)

    g = _pcall(body, name="bias_grad", out_shape=jax.ShapeDtypeStruct((H * n_pair, 128), F32),
                       compiler_params=_params())(flat, jnp.asarray(oh))
    g = g.reshape(H, n_pair, 128)
    left, right = g[:, :, :n_dc], g[:, :, 64:64 + n_dc]
    out = left[:, :n_pair - 1]
    return out.at[:, 1:].add(right[:, :n_pair - 2])


def _rope_tables(L):
    pos = np.arange(L)
    row = (pos // GRID_W).astype(np.float32)
    colp = (pos % GRID_W).astype(np.float32)
    half = HEAD // 2
    nf = half // 2
    inv = (ROPE_THETA ** (-np.arange(nf, dtype=np.float32) / nf)).astype(np.float32)

    def tabs(pv):
        ang = pv[:, None] * inv[None, :]
        c, s = np.cos(ang), np.sin(ang)
        return np.concatenate([c, c], axis=1), np.concatenate([-s, s], axis=1)

    cr, sr = tabs(row)
    cc, sc = tabs(colp)
    return (jnp.asarray(np.concatenate([cr, cc], axis=1), F32), jnp.asarray(np.concatenate([sr, sc], axis=1), F32))


def _adamw(w, g, m, v, name, after=None, copy_g=False):
    R, C = w.shape
    tr = _row_tile(R, C)
    c1 = 1.0 - ADAM_B1 ** ADAM_STEP
    c2 = 1.0 - ADAM_B2 ** ADAM_STEP
    deps = [] if after is None else [after]
    n_out = 4 if copy_g else 3

    def body(w_ref, g_ref, m_ref, v_ref, *rest):
        d_ref, mo_ref, vo_ref = rest[len(deps):len(deps) + 3]
        gv = g_ref[...]
        mn = ADAM_B1 * m_ref[...] + (1.0 - ADAM_B1) * gv
        vn = ADAM_B2 * v_ref[...] + (1.0 - ADAM_B2) * (gv * gv)
        mo_ref[...] = mn
        vo_ref[...] = vn
        d_ref[...] = -ADAM_LR * ((mn / c1) / (jnp.sqrt(vn / c2) + ADAM_EPS) + ADAM_WD * w_ref[...])
        if copy_g:
            rest[-1][...] = gv

    blk = pl.BlockSpec((tr, C), lambda i: (i, 0))
    return _pcall(
        body, name=name, grid=(R // tr,),
        in_specs=[blk] * 4 + [_ANY] * len(deps), out_specs=[blk] * n_out,
        out_shape=[jax.ShapeDtypeStruct((R, C), F32)] * n_out,
        compiler_params=_params(("parallel",)),
    )(w, g, m, v, *deps)


PACK_W = 1024


def _pack(parts):
    flat, offs, pos = [], [], 0
    for a in parts:
        n = a.size
        padn = -n % PACK_W
        flat.append(jnp.pad(a.reshape(-1).astype(F32), (0, padn)))
        offs.append((pos, n, a.shape))
        pos += n + padn
    tail = -pos % (8 * PACK_W)
    if tail:
        flat.append(jnp.zeros((tail,), F32))
    return jnp.concatenate(flat).reshape(-1, PACK_W), offs


def _unpack(buf, offs, i):
    pos, n, shape = offs[i]
    return buf.reshape(buf.shape[:-2] + (-1,))[..., pos:pos + n].reshape(buf.shape[:-2] + shape)


def kernel(x, c, ctx, c_ctx, ada_w, ada_b, norm1_g, norm2_g, w_in, hgrn_lb_logits, hgrn_norm_g, na_q_norm_g, na_k_norm_g, na_rel_bias, w_branch_a, w_branch_b, w_out, ffn_w1, ffn_w3, ffn_conv_w, ffn_conv_b, ffn_w2, loss_target, m_c_ctx, m_ada_w, m_ada_b, m_norm1_g, m_norm2_g, m_w_in, m_hgrn_lb_logits, m_hgrn_norm_g, m_na_q_norm_g, m_na_k_norm_g, m_na_rel_bias, m_w_branch_a, m_w_branch_b, m_w_out, m_ffn_w1, m_ffn_w3, m_ffn_conv_w, m_ffn_conv_b, m_ffn_w2, v_c_ctx, v_ada_w, v_ada_b, v_norm1_g, v_norm2_g, v_w_in, v_hgrn_lb_logits, v_hgrn_norm_g, v_na_q_norm_g, v_na_k_norm_g, v_na_rel_bias, v_w_branch_a, v_w_branch_b, v_w_out, v_ffn_w1, v_ffn_w3, v_ffn_conv_w, v_ffn_conv_b, v_ffn_w2):
    weights = dict(c_ctx=c_ctx, ada_w=ada_w, ada_b=ada_b, norm1_g=norm1_g, norm2_g=norm2_g, w_in=w_in,
                   hgrn_lb_logits=hgrn_lb_logits, hgrn_norm_g=hgrn_norm_g, na_q_norm_g=na_q_norm_g,
                   na_k_norm_g=na_k_norm_g, na_rel_bias=na_rel_bias, w_branch_a=w_branch_a, w_branch_b=w_branch_b,
                   w_out=w_out, ffn_w1=ffn_w1, ffn_w3=ffn_w3, ffn_conv_w=ffn_conv_w, ffn_conv_b=ffn_conv_b,
                   ffn_w2=ffn_w2)
    moms = dict(c_ctx=(m_c_ctx, v_c_ctx), ada_w=(m_ada_w, v_ada_w), ada_b=(m_ada_b, v_ada_b),
                norm1_g=(m_norm1_g, v_norm1_g), norm2_g=(m_norm2_g, v_norm2_g), w_in=(m_w_in, v_w_in),
                hgrn_lb_logits=(m_hgrn_lb_logits, v_hgrn_lb_logits), hgrn_norm_g=(m_hgrn_norm_g, v_hgrn_norm_g),
                na_q_norm_g=(m_na_q_norm_g, v_na_q_norm_g), na_k_norm_g=(m_na_k_norm_g, v_na_k_norm_g),
                na_rel_bias=(m_na_rel_bias, v_na_rel_bias), w_branch_a=(m_w_branch_a, v_w_branch_a),
                w_branch_b=(m_w_branch_b, v_w_branch_b), w_out=(m_w_out, v_w_out), ffn_w1=(m_ffn_w1, v_ffn_w1),
                ffn_w3=(m_ffn_w3, v_ffn_w3), ffn_conv_w=(m_ffn_conv_w, v_ffn_conv_w),
                ffn_conv_b=(m_ffn_conv_b, v_ffn_conv_b), ffn_w2=(m_ffn_w2, v_ffn_w2))
    order = list(weights)

    L, D = x.shape[1], x.shape[2]
    N = ctx.shape[1]
    T = N + L
    HA = w_branch_a.shape[1]
    HB = w_branch_b.shape[1]
    F = ffn_conv_b.shape[1]
    IN = 5 * HA + 3 * HB + 2 * D
    n_ada = ada_w.shape[2]
    ix, iy, ic = _pos()
    chip = 2 * ix + iy
    dev = 2 * chip + ic

    _PENDING.clear()
    pk0, offs0 = _pack([c[0], hgrn_lb_logits, ffn_conv_w[0]])
    g0 = _allgather8(pk0, "gather_small0")
    c_all = _unpack(g0, offs0, 0)
    lbl_parts = _unpack(g0, offs0, 1)
    lbl = jnp.concatenate([lbl_parts[2 * j] for j in range(N_CHIP)], axis=-1)
    cw_parts = _unpack(g0, offs0, 2)
    cw = jnp.concatenate([cw_parts[2 * j] for j in range(N_CHIP)], axis=-1)
    cw8 = jnp.pad(cw, ((0, 5), (0, 0)))

    cs = jnp.concatenate([c_all, c_ctx[None, :], jnp.zeros((7, D), F32)], axis=0)
    ada_b_mine = lax.dynamic_slice(ada_b, (0, chip * n_ada), (1, n_ada))
    mod_mine = _ada_fwd(cs, ada_w[0], ada_b_mine)
    gm = _allgather8(mod_mine, "gather_mod")
    mod = jnp.concatenate([gm[2 * j] for j in range(N_CHIP)], axis=-1)
    mod_l = lax.dynamic_slice(mod, (dev, 0), (1, N_MOD * D)).reshape(N_MOD, D)
    mod_c = mod[8].reshape(N_MOD, D)
    sh1, sc1, g1, sh2, sc2, g2 = [mod_l[i:i + 1] for i in range(N_MOD)]
    shift1 = jnp.concatenate([mod_c[0:1], sh1], axis=0)
    scale1 = jnp.concatenate([mod_c[1:2], sc1], axis=0)

    shards = [w_in[0], w_branch_a[0], w_branch_b[0], w_out[0], ffn_w1[0], ffn_w3[0], ffn_w2[0]]
    names = ["w_in", "w_a", "w_b", "w_out", "w1", "w3", "w2"]
    slots = [_cast_bf16_slot(s, "cast_" + nm) for s, nm in zip(shards, names)]
    gat_in = _gather_start("in", slots[0:1], gm)
    gat_mix = _gather_start("mix", slots[1:4])
    gat_ffn = _gather_start("ffn", slots[4:7])

    xall = jnp.concatenate([ctx[0], x[0]], axis=0)
    h_all = _rms1_fwd(xall, norm1_g, shift1, scale1, N)
    gat_in = _gather_mid(gat_in, h_all)
    (Win,) = _gather_finish(gat_in, h_all)
    p = _mm_nn(h_all, Win, F32, "mm_p")
    gat_mix = _gather_mid(gat_mix, p)
    y_a, o_a, st_a = _hgrn_fwd(p, lbl, hgrn_norm_g, N, HA)
    Wa, Wb, Wo = _gather_finish(gat_mix, y_a)
    Wo = Wo.reshape(1, D, D)
    gat_ffn = _gather_mid(gat_ffn, y_a)
    bias = _expand_bias(na_rel_bias[0])
    cos, sin = _rope_tables(L)
    off_na = 5 * HA
    y_b = _na_fwd(p, bias, na_q_norm_g, na_k_norm_g, cos, sin, N, off_na, HB)
    za = _mm_nn(y_a, Wa, F32, "mm_za")
    zb = _mm_nn(y_b, Wb, F32, "mm_zb")
    off_ga, off_gb = 5 * HA + 3 * HB, 5 * HA + 3 * HB + D
    z = _merge_fwd(za, zb, p, N, off_ga, off_gb)
    mo = _mm_nn(z, Wo, F32, "mm_mo")
    vec2 = jnp.concatenate([g1, norm2_g, sh2, sc2, jnp.zeros((4, D), F32)], axis=0)
    x_mid, h2 = _resid_rms2_fwd(x[0], mo, vec2)
    W1, W3, W2 = _gather_finish(gat_ffn, h2)
    W2 = W2.reshape(1, F, D)
    u1 = _mm_nn(h2, W1, F32, "mm_u1")
    u3 = _mm_nn(h2, W3, F32, "mm_u3")
    a = _convgate_fwd(u1, u3, cw8, ffn_conv_b)
    f = _mm_nn(a, W2, F32, "mm_f")
    dy, df, s_loss = _loss_head(x_mid, f, g2, loss_target[0])
    loss = lax.psum(s_loss[1, 0], ("x", "y", "c"))
    d_g2 = s_loss[0:1]

    gW2 = _mm_tn(a, df, 1, "mm_gw2").reshape(N_CHIP, F // N_CHIP, D)
    da = _mm_nt(df, W2, F32, "mm_da")
    du1, du3, s_conv = _convgate_bwd(u1, u3, da, cw8, ffn_conv_b)
    gW1 = _mm_tn(h2, du1, N_CHIP, "mm_gw1")
    gW3 = _mm_tn(h2, du3, N_CHIP, "mm_gw3")
    rs_ffn = _rs_start("ffn", [gW2, gW1, gW3])
    dh2a = _mm_nt(du1, W1, F32, "mm_dh2a")
    dh2b = _mm_nt(du3, W3, F32, "mm_dh2b")
    rs_ffn = _rs_scatter(rs_ffn, dh2b)
    dxm, dmo, s_rms2 = _resid_rms2_bwd(x_mid, dh2a, dh2b, dy, mo, vec2)
    gWo = _mm_tn(z, dmo, 1, "mm_gwo").reshape(N_CHIP, D // N_CHIP, D)
    dz = _mm_nt(dmo, Wo, F32, "mm_dz")
    dza, dzb, dga, dgb = _merge_bwd(dz, za, zb, p, N, off_ga, off_gb)
    gWa = _mm_tn(y_a, dza, N_CHIP, "mm_gwa")
    gWb = _mm_tn(y_b, dzb, N_CHIP, "mm_gwb")
    rs_mix = _rs_start("mix", [gWo, gWa, gWb])
    dya = _mm_nt(dza, Wa, F32, "mm_dya")
    dyb = _mm_nt(dzb, Wb, BF16, "mm_dyb")
    rs_mix = _rs_scatter(rs_mix, dyb)
    dq_a, dzf, dzbk, di_a, dog, dlbl, s_ng = _hgrn_bwd(p, lbl, hgrn_norm_g, o_a, dya, st_a, N, HA)
    rs_ffn = _rs_join(rs_ffn, dq_a)
    dq_n, dk_n, dv_n, dbias, s_qk = _na_bwd(p, bias, na_q_norm_g, na_k_norm_g, cos, sin, dyb, N, off_na, HB)
    rs_mix = _rs_join(rs_mix, dq_n)
    dp = jnp.concatenate([dq_a, dzf, dzbk, di_a, dog, dq_n, dk_n, dv_n, dga, dgb], axis=1)
    gWin = _mm_tn(h_all, dp, N_CHIP, "mm_gwin")
    rs_in = _rs_start("in", [gWin])
    dh = _mm_nt(dp, Win, F32, "mm_dh")
    grad_x, s_rms1 = _rms1_bwd(xall, dh, dxm, norm1_g, scale1, N)
    d_table = _bias_grad(dbias)

    zD = jnp.zeros((1, D), F32)
    dmod_l = jnp.concatenate([s_rms1[2:3], s_rms1[3:4], s_rms2[3:4], s_rms2[0:1], s_rms2[1:2], d_g2], axis=0)
    dmod_c = jnp.concatenate([s_rms1[0:1], s_rms1[1:2], zD, zD, zD, zD], axis=0)
    pk1, offs1 = _pack([dmod_l, dmod_c, s_rms1[4], s_rms2[2], dlbl, s_ng[0], s_qk[0], s_qk[1], d_table,
                        s_conv[0:3], s_conv[3]])
    g1all = _allgather8(pk1, "gather_small1")
    tot1 = _sum8(g1all, "sum_small1")
    dmod_rows = _unpack(g1all, offs1, 0).reshape(N_DEV, N_MOD * D)
    dmod_c_tot = _unpack(tot1, offs1, 1).reshape(1, N_MOD * D)
    dmod16 = jnp.concatenate([dmod_rows, dmod_c_tot, jnp.zeros((7, N_MOD * D), F32)], axis=0)
    dmod16_mine = lax.dynamic_slice(dmod16, (0, chip * n_ada), (16, n_ada))
    g_ada_w, dact = _ada_bwd(cs, ada_w[0], dmod16_mine)
    pk2, offs2 = _pack([dact[8]])
    g2all = _allgather8(pk2, "gather_small2")
    dact_rows = _unpack(g2all, offs2, 0)
    dact_sel = jnp.concatenate([dact_rows[2 * j][None] for j in range(N_CHIP)] + [jnp.zeros((4, D), F32)], axis=0)

    grads = {}
    grads["ada_w"] = g_ada_w[None]
    grads["ada_b"] = (_unpack(tot1, offs1, 0) + _unpack(tot1, offs1, 1)).reshape(1, N_MOD * D)
    grads["norm1_g"] = _unpack(tot1, offs1, 2)[None]
    grads["norm2_g"] = _unpack(tot1, offs1, 3)[None]
    g_lbl = _unpack(tot1, offs1, 4)
    n_lb = HA // N_CHIP
    grads["hgrn_lb_logits"] = lax.dynamic_slice(g_lbl, (0, 0, chip * n_lb), (2, 2, n_lb))
    grads["hgrn_norm_g"] = _unpack(tot1, offs1, 5)[None]
    grads["na_q_norm_g"] = _unpack(tot1, offs1, 6)[None]
    grads["na_k_norm_g"] = _unpack(tot1, offs1, 7)[None]
    grads["na_rel_bias"] = _unpack(tot1, offs1, 8)[None]
    g_cw = _unpack(tot1, offs1, 9)
    n_f = F // N_CHIP
    grads["ffn_conv_w"] = lax.dynamic_slice(g_cw, (0, chip * n_f), (3, n_f))[None]
    grads["ffn_conv_b"] = _unpack(tot1, offs1, 10)[None]

    g_c_ctx = _dsilu_rows(dact_sel, c_ctx[None, :], "grad_c_ctx")
    grads["c_ctx"] = g_c_ctx[0]

    rs_in = _rs_scatter(rs_in, g_c_ctx)
    big_names = ["ada_w", "w_in", "w_branch_a", "w_branch_b", "w_out", "ffn_w1", "ffn_w3", "ffn_w2"]
    small_names = [n for n in order if n not in big_names]
    delta, new_m, new_v = {}, {}, {}

    def update(nm, after=None):
        reduced = nm != "ada_w"
        d_, m_, v_, *g_ = _adamw(weights[nm][0], grads[nm][0], moms[nm][0][0], moms[nm][1][0], "adamw_" + nm,
                                 after, copy_g=reduced)
        delta[nm], new_m[nm], new_v[nm] = d_[None], m_[None], v_[None]
        if reduced:
            grads[nm] = g_[0][None]
        return d_

    last = update("ada_w")
    for nm, g in zip(["ffn_w2", "ffn_w1", "ffn_w3"], _rs_finish(rs_ffn, last)):
        grads[nm] = g[None]
        last = update(nm, last)
    for nm, g in zip(["w_out", "w_branch_a", "w_branch_b"], _rs_finish(rs_mix, last)):
        grads[nm] = g[None]
        last = update(nm, last)
    rs_in = _rs_join(rs_in, last)
    grads["w_in"] = _rs_finish(rs_in, last)[0][None]
    update("w_in")
    pw, offw = _pack([weights[n] for n in small_names])
    pg, _ = _pack([grads[n] for n in small_names])
    pm, _ = _pack([moms[n][0] for n in small_names])
    pv, _ = _pack([moms[n][1] for n in small_names])
    d_, m_, v_ = _adamw(pw, pg, pm, pv, "adamw_small")
    for i, nm in enumerate(small_names):
        delta[nm], new_m[nm], new_v[nm] = _unpack(d_, offw, i), _unpack(m_, offw, i), _unpack(v_, offw, i)

    return (loss, grad_x[None], *[grads[n] for n in order], *[delta[n] for n in order],
            *[new_m[n] for n in order], *[new_v[n] for n in order])


def _dsilu_rows(v, cv, name):
    D = v.shape[1]

    def body(v_ref, c_ref, o_ref):
        t = c_ref[...]
        s = _sigmoid(t)
        o_ref[...] = (((v_ref[0:1, :] + v_ref[1:2, :]) + v_ref[2:3, :]) + v_ref[3:4, :]) * (s * (1.0 + t * (1.0 - s)))

    return _pcall(body, name=name, out_shape=jax.ShapeDtypeStruct((1, D), F32),
                          compiler_params=_params())(v, cv)
```

```python
import functools

import numpy as np
import jax
import jax.numpy as jnp
from jax import lax
from jax.experimental import pallas as pl
from jax.experimental.pallas import tpu as pltpu

F32 = jnp.float32
BF16 = jnp.bfloat16
MESH = pl.DeviceIdType.MESH

HEAD = 128
GRID_W = 64
WIN_R = 8
WIN_C = 16
ROPE_THETA = 10000.0
EPS = 1e-6
N_MOD = 6
CHUNK = 16
ADAM_LR = 0.001
ADAM_B1 = 0.9
ADAM_B2 = 0.999
ADAM_EPS = 1e-08
ADAM_WD = 0.01
ADAM_STEP = 10
NEG = -1e30
VMEM_LIMIT = 56 * 1024 * 1024
N_DEV = 8
N_CHIP = 4
HI = lax.Precision.HIGHEST


def _pick(n, cands):
    for c in cands:
        if n % c == 0:
            return c
    return n


def _row_tile(rows, cols, target_bytes=1 << 20):
    want = max(16, target_bytes // (4 * cols))
    for t in (512, 256, 128, 64, 32, 16, 8):
        if t <= want and rows % t == 0:
            return t
    return rows


def _params(sem=None):
    return pltpu.CompilerParams(dimension_semantics=sem, vmem_limit_bytes=VMEM_LIMIT)


def _dot(a, b):
    return jnp.dot(a, b, preferred_element_type=F32)


def _dot_nt(a, b):
    return lax.dot_general(a, b, (((1,), (1,)), ((), ())), preferred_element_type=F32)


def _dot_tn(a, b):
    return lax.dot_general(a, b, (((0,), (0,)), ((), ())), preferred_element_type=F32)


def _sigmoid(x):
    return 1.0 / (1.0 + jnp.exp(-x))


def _col_tile(n):
    return n if n <= 1536 else _pick(n, (1024, 768, 512, 384, 256, 128))


def _mm_nn(x, w3, out_dtype, name):
    M, K = x.shape
    S, _, n = w3.shape
    tm = _pick(M, (768, 512, 256, 128, 64))
    tn = _col_tile(n)
    nb = n // tn

    def body(x_ref, w_ref, o_ref):
        o_ref[...] = _dot(x_ref[...].astype(BF16), w_ref[0]).astype(o_ref.dtype)

    return _pcall(
        body, name=name, grid=(M // tm, S * nb),
        in_specs=[pl.BlockSpec((tm, K), lambda i, j: (i, 0)),
                  pl.BlockSpec((1, K, tn), lambda i, j: (j // nb, 0, j % nb))],
        out_specs=pl.BlockSpec((tm, tn), lambda i, j: (i, j)),
        out_shape=jax.ShapeDtypeStruct((M, S * n), out_dtype),
        compiler_params=_params(("parallel", "parallel")),
    )(x, w3)


def _mm_nt(dy, w3, out_dtype, name):
    M = dy.shape[0]
    S, K, n = w3.shape
    tm = _pick(M, (768, 512, 256, 128, 64))
    tk = K if K <= 2048 else _pick(K, (1408, 1024, 512, 256, 128))
    tc = n if n <= 2048 else _col_tile(n)
    nb = n // tc
    nsteps = S * nb

    def body(dy_ref, w_ref, o_ref, acc_ref):
        s = pl.program_id(2)

        @pl.when(s == 0)
        def _():
            acc_ref[...] = jnp.zeros_like(acc_ref)

        acc_ref[...] += _dot_nt(dy_ref[...].astype(BF16), w_ref[0])

        @pl.when(s == nsteps - 1)
        def _():
            o_ref[...] = acc_ref[...].astype(o_ref.dtype)

    return _pcall(
        body, name=name, grid=(M // tm, K // tk, nsteps),
        in_specs=[pl.BlockSpec((tm, tc), lambda i, k, s: (i, s)),
                  pl.BlockSpec((1, tk, tc), lambda i, k, s: (s // nb, k, s % nb))],
        out_specs=pl.BlockSpec((tm, tk), lambda i, k, s: (i, k)),
        out_shape=jax.ShapeDtypeStruct((M, K), out_dtype),
        scratch_shapes=[pltpu.VMEM((tm, tk), F32)],
        compiler_params=_params(("parallel", "parallel", "arbitrary")),
    )(dy, w3)


def _mm_tn(x, dy, S, name):
    M, K = x.shape
    n = dy.shape[1] // S
    tk = _pick(K, (512, 256, 128))
    tn = _col_tile(n)
    nb = n // tn

    def body(x_ref, dy_ref, o_ref):
        o_ref[0] = _dot_tn(x_ref[...].astype(BF16), dy_ref[...].astype(BF16)).astype(BF16)

    return _pcall(
        body, name=name, grid=(S * nb, K // tk),
        in_specs=[pl.BlockSpec((M, tk), lambda j, k: (0, k)),
                  pl.BlockSpec((M, tn), lambda j, k: (0, j))],
        out_specs=pl.BlockSpec((1, tk, tn), lambda j, k: (j // nb, k, j % nb)),
        out_shape=jax.ShapeDtypeStruct((S, K, n), BF16),
        compiler_params=_params(("parallel", "parallel")),
    )(x, dy)


def _chip_index():
    return (2 * lax.axis_index("x") + lax.axis_index("y")).astype(jnp.int32).reshape(1)


def _cast_bf16_slot(w, name):
    R, C = w.shape
    tr = _row_tile(R, C, 2 << 20)

    def body(j_ref, w_ref, o_ref):
        o_ref[0] = w_ref[...].astype(BF16)

    return _pcall(
        body, name=name,
        grid_spec=pltpu.PrefetchScalarGridSpec(
            num_scalar_prefetch=1, grid=(R // tr,),
            in_specs=[pl.BlockSpec((tr, C), lambda i, j_ref: (i, 0))],
            out_specs=pl.BlockSpec((1, tr, C), lambda i, j_ref: (j_ref[0], i, 0))),
        out_shape=jax.ShapeDtypeStruct((N_CHIP, R, C), BF16),
        compiler_params=_params(("parallel",)),
    )(_chip_index(), w)


def _pos():
    return lax.axis_index("x"), lax.axis_index("y"), lax.axis_index("c")


def _other_chips(x, y):
    return [(x, 1 - y), (1 - x, y), (1 - x, 1 - y)]


def _allgather8(v, name):
    R, C = v.shape

    def body(x_ref, out_ref, send_sems, recv_sems, local_sem):
        x, y, c = _pos()
        me, sibling = (x, y, c), (x, y, 1 - c)
        chips = _other_chips(x, y)

        def slot(px, py, pc):
            return out_ref.at[4 * px + 2 * py + pc]

        def copy(k, block, to, src=None):
            return pltpu.make_async_remote_copy(
                src_ref=slot(*block) if src is None else src, dst_ref=slot(*block),
                send_sem=send_sems.at[k], recv_sem=recv_sems.at[k], device_id=to, device_id_type=MESH)

        mine = pltpu.make_async_copy(x_ref, slot(*me), local_sem)
        mine.start()
        first = [copy(0, me, sibling, src=x_ref)]
        first += [copy(1 + j, me, (*chip, c), src=x_ref) for j, chip in enumerate(chips)]
        for cp in first:
            cp.start()
        passed = [copy(4 + j, (*chip, c), sibling) for j, chip in enumerate(chips)]
        for j, chip in enumerate(chips):
            copy(1 + j, (*chip, c), me).wait_recv()
            passed[j].start()
        copy(0, sibling, me).wait_recv()
        for j, chip in enumerate(chips):
            copy(4 + j, (*chip, 1 - c), me).wait_recv()
        for cp in first + passed:
            cp.wait_send()
        mine.wait()

    return _pcall(
        body, name=name,
        out_shape=jax.ShapeDtypeStruct((N_DEV, R, C), v.dtype),
        in_specs=[pl.BlockSpec(memory_space=pltpu.VMEM)],
        out_specs=pl.BlockSpec(memory_space=pltpu.VMEM),
        scratch_shapes=[pltpu.SemaphoreType.DMA((7,)), pltpu.SemaphoreType.DMA((7,)), pltpu.SemaphoreType.DMA],
        compiler_params=pltpu.CompilerParams(vmem_limit_bytes=VMEM_LIMIT),
    )(v)


_HBM = pl.BlockSpec(memory_space=pltpu.HBM)
_SEM = pl.BlockSpec(memory_space=pltpu.SEMAPHORE)
_ANY = pl.BlockSpec(memory_space=pl.ANY)
_EFFECT = pltpu.SideEffectType.DATAFLOW_SIDE_EFFECTING
_PENDING = []


def _pcall(body, **kw):
    def run(*operands):
        if not _PENDING or "in_specs" not in kw:
            return pl.pallas_call(body, **kw)(*operands)
        deps = list(_PENDING)
        n = len(operands)

        def tied(*refs):
            return body(*refs[:n], *refs[n + len(deps):])

        return pl.pallas_call(tied, **{**kw, "in_specs": list(kw["in_specs"]) + [_ANY] * len(deps)})(*operands, *deps)
    return run


def _copies(plan, refs, send_sems, recv_sems):
    return [pltpu.make_async_remote_copy(src_ref=src, dst_ref=dst, send_sem=send_sems.at[k], recv_sem=recv_sems.at[k],
                                         device_id=dev, device_id_type=MESH)
            for k, (src, dst, dev) in enumerate(plan(refs))]


def _xfer_start(name, bufs, plan, n_copies, after=None):
    n = len(bufs)
    deps = list(_PENDING) + ([after] if after is not None else [])
    nd = len(deps)

    def body(*refs):
        for cp in _copies(plan, refs[:n], refs[n + nd], refs[n + nd + 1]):
            cp.start()
        refs[-1][...] = jnp.zeros_like(refs[-1])

    outs = pl.pallas_call(
        body, name=name,
        out_shape=(pltpu.SemaphoreType.DMA((n_copies,)), pltpu.SemaphoreType.DMA((n_copies,)),
                   *[pltpu.HBM(b.shape, b.dtype) for b in bufs], jax.ShapeDtypeStruct((8, 128), F32)),
        in_specs=[_HBM] * n + [_ANY] * nd,
        out_specs=(_SEM, _SEM, *[_HBM] * n, pl.BlockSpec(memory_space=pltpu.VMEM)),
        input_output_aliases={t: 2 + t for t in range(n)},
        compiler_params=pltpu.CompilerParams(has_side_effects=_EFFECT),
    )(*[pltpu.with_memory_space_constraint(b, pltpu.HBM) for b in bufs], *deps)
    _PENDING[:] = [outs[-1]]
    return (outs[0], outs[1]), list(outs[2:2 + n])


def _xfer_wait(name, sems, bufs, plan, after):
    n = len(bufs)

    def body(*refs):
        cps = _copies(plan, refs[:n], refs[n], refs[n + 1])
        for cp in cps:
            cp.wait_send()
        for cp in cps:
            cp.wait_recv()

    outs = pl.pallas_call(
        body, name=name,
        out_shape=tuple(pltpu.HBM(b.shape, b.dtype) for b in bufs),
        in_specs=[_HBM] * n + [_SEM, _SEM, _ANY],
        out_specs=tuple([_HBM] * n),
        input_output_aliases={t: t for t in range(n)},
        compiler_params=pltpu.CompilerParams(has_side_effects=_EFFECT),
    )(*bufs, sems[0], sems[1], after)
    return list(outs)


def _half(ref_rows, hc):
    h = ref_rows // 2
    return pl.ds(hc * h, h)


def _plan_gather_ici(bufs):
    x, y, c = _pos()
    j = 2 * x + y
    return [(b.at[j, _half(b.shape[1], c)], b.at[j, _half(b.shape[1], c)], (*chip, c))
            for b in bufs for chip in _other_chips(x, y)]


def _plan_gather_d2d(bufs):
    x, y, c = _pos()
    out = []
    for b in bufs:
        for chip in _other_chips(x, y):
            blk = b.at[2 * chip[0] + chip[1], _half(b.shape[1], c)]
            out.append((blk, blk, (x, y, 1 - c)))
    return out


def _plan_pair_swap(n):
    def plan(bufs):
        x, y, c = _pos()
        return [(g.at[:, _half(g.shape[1], 1 - c)], land, (x, y, 1 - c)) for g, land in zip(bufs[:n], bufs[n:])]
    return plan


def _plan_chip_scatter(n):
    def plan(bufs):
        x, y, c = _pos()
        return [(p.at[2 * chip[0] + chip[1]], land.at[k], (*chip, c))
                for p, land in zip(bufs[:n], bufs[n:]) for k, chip in enumerate(_other_chips(x, y))]
    return plan


def _plan_pair_join(bufs):
    x, y, c = _pos()
    return [(b.at[_half(b.shape[0], c)], b.at[_half(b.shape[0], c)], (x, y, 1 - c)) for b in bufs]


def _empty_hbm(shape, dtype):
    return pltpu.with_memory_space_constraint(lax.empty(shape, dtype), pltpu.HBM)


def _gather_start(tag, bufs, after=None):
    sems, bufs = _xfer_start(f"gather_ici_start_{tag}", bufs, _plan_gather_ici, 3 * len(bufs), after)
    return dict(tag=tag, sems=sems, bufs=bufs)


def _gather_mid(st, after):
    tag = st["tag"]
    bufs = _xfer_wait(f"gather_ici_wait_{tag}", st["sems"], st["bufs"], _plan_gather_ici, after)
    sems, bufs = _xfer_start(f"gather_d2d_start_{tag}", bufs, _plan_gather_d2d, 3 * len(bufs))
    return dict(tag=tag, sems=sems, bufs=bufs)


def _gather_finish(st, after):
    return _xfer_wait(f"gather_d2d_wait_{st['tag']}", st["sems"], st["bufs"], _plan_gather_d2d, after)


def _pair_add(g, r, name):
    S, R, C = g.shape
    h = R // 2
    tr = _row_tile(h, C)
    nb = h // tr

    def body(c_ref, g_ref, r_ref, o_ref):
        o_ref[...] = (g_ref[...].astype(F32) + r_ref[...].astype(F32)).astype(BF16)

    return _pcall(
        body, name=name,
        grid_spec=pltpu.PrefetchScalarGridSpec(
            num_scalar_prefetch=1, grid=(S, nb),
            in_specs=[pl.BlockSpec((1, tr, C), lambda s, i, c_ref: (s, c_ref[0] * nb + i, 0)),
                      pl.BlockSpec((1, tr, C), lambda s, i, c_ref: (s, i, 0))],
            out_specs=pl.BlockSpec((1, tr, C), lambda s, i, c_ref: (s, i, 0))),
        out_shape=jax.ShapeDtypeStruct((S, h, C), BF16),
        compiler_params=_params(("parallel", "parallel")),
    )(lax.axis_index("c").astype(jnp.int32).reshape(1), g, r)


def _chip_sum(p, rb, name):
    S, h, C = p.shape
    tr = _row_tile(h, C)
    nb = h // tr
    jc = jnp.concatenate([_chip_index(), lax.axis_index("c").astype(jnp.int32).reshape(1)])

    def body(jc_ref, p_ref, r_ref, o_ref):
        o_ref[...] = ((p_ref[0].astype(F32) + r_ref[0].astype(F32)) + r_ref[1].astype(F32)) + r_ref[2].astype(F32)

    return _pcall(
        body, name=name,
        grid_spec=pltpu.PrefetchScalarGridSpec(
            num_scalar_prefetch=1, grid=(nb,),
            in_specs=[pl.BlockSpec((1, tr, C), lambda i, jc_ref: (jc_ref[0], i, 0)),
                      pl.BlockSpec((3, tr, C), lambda i, jc_ref: (0, i, 0))],
            out_specs=pl.BlockSpec((tr, C), lambda i, jc_ref: (jc_ref[1] * nb + i, 0))),
        out_shape=jax.ShapeDtypeStruct((2 * h, C), F32),
        compiler_params=_params(("parallel",)),
    )(jc, p, rb)


def _rs_start(tag, gs):
    n = len(gs)
    lands = [_empty_hbm((g.shape[0], g.shape[1] // 2, g.shape[2]), g.dtype) for g in gs]
    sems, bufs = _xfer_start(f"rs_swap_start_{tag}", list(gs) + lands, _plan_pair_swap(n), n)
    return dict(tag=tag, n=n, sems=sems, bufs=bufs)


def _rs_scatter(st, after):
    tag, n = st["tag"], st["n"]
    bufs = _xfer_wait(f"rs_swap_wait_{tag}", st["sems"], st["bufs"], _plan_pair_swap(n), after)
    ps = [_pair_add(g, r, f"rs_pair_add_{tag}{t}") for t, (g, r) in enumerate(zip(bufs[:n], bufs[n:]))]
    lands = [_empty_hbm((3,) + p.shape[1:], p.dtype) for p in ps]
    sems, bufs = _xfer_start(f"rs_scatter_start_{tag}", ps + lands, _plan_chip_scatter(n), 3 * n)
    return dict(tag=tag, n=n, sems=sems, bufs=bufs)


def _rs_join(st, after):
    tag, n = st["tag"], st["n"]
    bufs = _xfer_wait(f"rs_scatter_wait_{tag}", st["sems"], st["bufs"], _plan_chip_scatter(n), after)
    fs = [_chip_sum(p, rb, f"rs_chip_sum_{tag}{t}") for t, (p, rb) in enumerate(zip(bufs[:n], bufs[n:]))]
    sems, bufs = _xfer_start(f"rs_join_start_{tag}", fs, _plan_pair_join, n)
    return dict(tag=tag, n=n, sems=sems, bufs=bufs)


def _rs_finish(st, after):
    return _xfer_wait(f"rs_join_wait_{st['tag']}", st["sems"], st["bufs"], _plan_pair_join, after)


def _sum8(g, name):
    _, R, C = g.shape

    def body(g_ref, o_ref):
        acc = g_ref[0]
        for d in range(1, N_DEV):
            acc = acc + g_ref[d]
        o_ref[...] = acc

    return _pcall(body, name=name, out_shape=jax.ShapeDtypeStruct((R, C), F32),
                          compiler_params=_params())(g)


def _ada_fwd(cs, w, b):
    D, n = w.shape
    tn = _pick(n, (512, 384, 256, 128))

    def body(c_ref, w_ref, b_ref, o_ref):
        cv = c_ref[...]
        a = (cv * _sigmoid(cv)).astype(BF16)
        o_ref[...] = _dot(a, w_ref[...].astype(BF16)) + b_ref[...]

    return _pcall(
        body, name="ada_fwd", grid=(n // tn,),
        in_specs=[pl.BlockSpec((16, D), lambda j: (0, 0)), pl.BlockSpec((D, tn), lambda j: (0, j)),
                  pl.BlockSpec((1, tn), lambda j: (0, j))],
        out_specs=pl.BlockSpec((16, tn), lambda j: (0, j)),
        out_shape=jax.ShapeDtypeStruct((16, n), F32),
        compiler_params=_params(("parallel",)),
    )(cs, w, b)


def _ada_bwd(cs, w, dmod):
    D, n = w.shape
    tn = _pick(n, (512, 384, 256, 128))

    def body(c_ref, w_ref, d_ref, gw_ref, da_ref):
        j = pl.program_id(0)
        cv = c_ref[...]
        a = cv * _sigmoid(cv)
        d = d_ref[...]
        gw_ref[...] = lax.dot_general(a, d, (((0,), (0,)), ((), ())), precision=HI, preferred_element_type=F32)

        @pl.when(j == 0)
        def _():
            da_ref[...] = jnp.zeros_like(da_ref)

        da_ref[...] += _dot_nt(d.astype(BF16), w_ref[...].astype(BF16))

    return _pcall(
        body, name="ada_bwd", grid=(n // tn,),
        in_specs=[pl.BlockSpec((16, D), lambda j: (0, 0)), pl.BlockSpec((D, tn), lambda j: (0, j)),
                  pl.BlockSpec((16, tn), lambda j: (0, j))],
        out_specs=[pl.BlockSpec((D, tn), lambda j: (0, j)), pl.BlockSpec((16, D), lambda j: (0, 0))],
        out_shape=[jax.ShapeDtypeStruct((D, n), F32), jax.ShapeDtypeStruct((16, D), F32)],
        compiler_params=_params(("arbitrary",)),
    )(cs, w, dmod)


def _rms1_fwd(xall, gain, shift2, scale2, n_ctx):
    T, D = xall.shape
    tb = _pick(n_ctx, (256, 128, 64, 32, 16))
    nctx = n_ctx // tb

    def body(x_ref, g_ref, sh_ref, sc_ref, o_ref):
        i = pl.program_id(0)
        xv = x_ref[...]
        r = lax.rsqrt(jnp.mean(xv * xv, axis=-1, keepdims=True) + EPS)
        nrm = xv * r * g_ref[...]
        lat = i >= nctx
        sh = jnp.where(lat, sh_ref[1:2, :], sh_ref[0:1, :])
        sc = jnp.where(lat, sc_ref[1:2, :], sc_ref[0:1, :])
        o_ref[...] = (nrm * (1.0 + sc) + sh).astype(BF16)

    vec = lambda r: pl.BlockSpec((r, D), lambda i: (0, 0))
    return _pcall(
        body, name="rms1_fwd", grid=(T // tb,),
        in_specs=[pl.BlockSpec((tb, D), lambda i: (i, 0)), vec(1), vec(2), vec(2)],
        out_specs=pl.BlockSpec((tb, D), lambda i: (i, 0)),
        out_shape=jax.ShapeDtypeStruct((T, D), BF16),
        compiler_params=_params(("parallel",)),
    )(xall, gain, shift2, scale2)


def _rms1_bwd(xall, dh, dxmid, gain, scale2, n_ctx):
    T, D = xall.shape
    L = T - n_ctx
    tb = _pick(n_ctx, (256, 128, 64, 32, 16))
    nctx = n_ctx // tb

    def body(x_ref, dh_ref, dxm_ref, g_ref, sc_ref, dx_ref, cs_ref):
        i = pl.program_id(0)
        lat = i >= nctx
        xv = x_ref[...]
        r = lax.rsqrt(jnp.mean(xv * xv, axis=-1, keepdims=True) + EPS)
        xh = xv * r
        g = g_ref[...]
        nrm = xh * g
        sc = jnp.where(lat, sc_ref[1:2, :], sc_ref[0:1, :])
        dhv = dh_ref[...]
        dn = dhv * (1.0 + sc)
        dxh = dn * g
        dxv = r * (dxh - xh * jnp.mean(dxh * xh, axis=-1, keepdims=True))
        s_sh = jnp.sum(dhv, axis=0, keepdims=True)
        s_sc = jnp.sum(dhv * nrm, axis=0, keepdims=True)
        s_g = jnp.sum(dn * xh, axis=0, keepdims=True)
        zero = jnp.zeros_like(s_sh)
        rows = lax.broadcasted_iota(jnp.int32, (8, D), 0)
        upd = jnp.where(rows == 0, jnp.where(lat, zero, s_sh),
              jnp.where(rows == 1, jnp.where(lat, zero, s_sc),
              jnp.where(rows == 2, jnp.where(lat, s_sh, zero),
              jnp.where(rows == 3, jnp.where(lat, s_sc, zero),
              jnp.where(rows == 4, s_g, 0.0)))))

        @pl.when(i == 0)
        def _():
            cs_ref[...] = jnp.zeros_like(cs_ref)

        cs_ref[...] += upd

        @pl.when(lat)
        def _():
            dx_ref[...] = dxv + dxm_ref[...]

    lat_blk = lambda i: (jnp.maximum(i - nctx, 0), 0)
    vec = lambda r: pl.BlockSpec((r, D), lambda i: (0, 0))
    return _pcall(
        body, name="rms1_bwd", grid=(T // tb,),
        in_specs=[pl.BlockSpec((tb, D), lambda i: (i, 0)), pl.BlockSpec((tb, D), lambda i: (i, 0)),
                  pl.BlockSpec((tb, D), lat_blk), vec(1), vec(2)],
        out_specs=[pl.BlockSpec((tb, D), lat_blk), vec(8)],
        out_shape=[jax.ShapeDtypeStruct((L, D), F32), jax.ShapeDtypeStruct((8, D), F32)],
        compiler_params=_params(("arbitrary",)),
    )(xall, dh, dxmid, gain, scale2)


def _resid_rms2_fwd(x, mo, vecs):
    L, D = x.shape
    tb = _pick(L, (256, 128, 64))

    def body(x_ref, mo_ref, v_ref, xm_ref, h_ref):
        xm = x_ref[...] + v_ref[0:1, :] * mo_ref[...]
        xm_ref[...] = xm
        r = lax.rsqrt(jnp.mean(xm * xm, axis=-1, keepdims=True) + EPS)
        h_ref[...] = (xm * r * v_ref[1:2, :] * (1.0 + v_ref[3:4, :]) + v_ref[2:3, :]).astype(BF16)

    blk = pl.BlockSpec((tb, D), lambda i: (i, 0))
    return _pcall(
        body, name="resid_rms2_fwd", grid=(L // tb,),
        in_specs=[blk, blk, pl.BlockSpec((8, D), lambda i: (0, 0))],
        out_specs=[blk, blk],
        out_shape=[jax.ShapeDtypeStruct((L, D), F32), jax.ShapeDtypeStruct((L, D), BF16)],
        compiler_params=_params(("parallel",)),
    )(x, mo, vecs)


def _resid_rms2_bwd(xmid, dh_a, dh_b, dy, mo, vecs):
    L, D = xmid.shape
    tb = _pick(L, (256, 128, 64))

    def body(xm_ref, da_ref, db_ref, dy_ref, mo_ref, v_ref, dxm_ref, dmo_ref, cs_ref):
        i = pl.program_id(0)
        xm = xm_ref[...]
        r = lax.rsqrt(jnp.mean(xm * xm, axis=-1, keepdims=True) + EPS)
        xh = xm * r
        g = v_ref[1:2, :]
        nrm = xh * g
        dhv = da_ref[...] + db_ref[...]
        dn = dhv * (1.0 + v_ref[3:4, :])
        dxh = dn * g
        dxm = dy_ref[...] + r * (dxh - xh * jnp.mean(dxh * xh, axis=-1, keepdims=True))
        dxm_ref[...] = dxm
        dmo_ref[...] = (dxm * v_ref[0:1, :]).astype(BF16)
        s0 = jnp.sum(dhv, axis=0, keepdims=True)
        s1 = jnp.sum(dhv * nrm, axis=0, keepdims=True)
        s2 = jnp.sum(dn * xh, axis=0, keepdims=True)
        s3 = jnp.sum(dxm * mo_ref[...], axis=0, keepdims=True)
        rows = lax.broadcasted_iota(jnp.int32, (8, D), 0)
        upd = jnp.where(rows == 0, s0, jnp.where(rows == 1, s1, jnp.where(rows == 2, s2,
              jnp.where(rows == 3, s3, 0.0))))

        @pl.when(i == 0)
        def _():
            cs_ref[...] = jnp.zeros_like(cs_ref)

        cs_ref[...] += upd

    blk = pl.BlockSpec((tb, D), lambda i: (i, 0))
    vec = pl.BlockSpec((8, D), lambda i: (0, 0))
    return _pcall(
        body, name="resid_rms2_bwd", grid=(L // tb,),
        in_specs=[blk, blk, blk, blk, blk, vec],
        out_specs=[blk, blk, vec],
        out_shape=[jax.ShapeDtypeStruct((L, D), F32), jax.ShapeDtypeStruct((L, D), BF16),
                   jax.ShapeDtypeStruct((8, D), F32)],
        compiler_params=_params(("arbitrary",)),
    )(xmid, dh_a, dh_b, dy, mo, vecs)


def _loss_head(xmid, f, g2, target):
    L, D = xmid.shape
    tb = _pick(L, (256, 128, 64))

    def body(xm_ref, f_ref, g_ref, t_ref, dy_ref, df_ref, s_ref):
        i = pl.program_id(0)
        fv = f_ref[...]
        g = g_ref[...]
        err = xm_ref[...] + g * fv - t_ref[...]
        dy = err * (1.0 / D)
        dy_ref[...] = dy
        df_ref[...] = (dy * g).astype(BF16)
        s0 = jnp.sum(dy * fv, axis=0, keepdims=True)
        part = 0.5 * jnp.sum(jnp.mean(err * err, axis=-1, keepdims=True), axis=0, keepdims=True)
        rows = lax.broadcasted_iota(jnp.int32, (8, D), 0)
        upd = jnp.where(rows == 0, s0, jnp.where(rows == 1, part, 0.0))

        @pl.when(i == 0)
        def _():
            s_ref[...] = jnp.zeros_like(s_ref)

        s_ref[...] += upd

    blk = pl.BlockSpec((tb, D), lambda i: (i, 0))
    return _pcall(
        body, name="loss_head", grid=(L // tb,),
        in_specs=[blk, blk, pl.BlockSpec((1, D), lambda i: (0, 0)), blk],
        out_specs=[blk, blk, pl.BlockSpec((8, D), lambda i: (0, 0))],
        out_shape=[jax.ShapeDtypeStruct((L, D), F32), jax.ShapeDtypeStruct((L, D), BF16),
                   jax.ShapeDtypeStruct((8, D), F32)],
        compiler_params=_params(("arbitrary",)),
    )(xmid, f, g2, target)


def _gate_cols(D, off):
    tc = _pick(np.gcd(D, off), (512, 256, 128))
    return tc, off // tc


def _merge_fwd(za, zb, p, n_ctx, off_a, off_b):
    L, D = za.shape
    tb = _pick(n_ctx, (256, 128, 64, 32, 16))
    nctx = n_ctx // tb
    tc, oa = _gate_cols(D, off_a)
    _, ob = _gate_cols(D, off_b)
    if off_b % tc:
        raise ValueError("gate column offsets must share a column tile")
    ob = off_b // tc

    def body(za_ref, zb_ref, ga_ref, gb_ref, z_ref):
        z_ref[...] = (_sigmoid(ga_ref[...]) * za_ref[...].astype(F32)
                      + _sigmoid(gb_ref[...]) * zb_ref[...].astype(F32)).astype(BF16)

    blk = pl.BlockSpec((tb, tc), lambda i, j: (i, j))
    return _pcall(
        body, name="merge_fwd", grid=(L // tb, D // tc),
        in_specs=[blk, blk, pl.BlockSpec((tb, tc), lambda i, j: (i + nctx, oa + j)),
                  pl.BlockSpec((tb, tc), lambda i, j: (i + nctx, ob + j))],
        out_specs=blk,
        out_shape=jax.ShapeDtypeStruct((L, D), BF16),
        compiler_params=_params(("parallel", "parallel")),
    )(za, zb, p, p)


def _merge_bwd(dz, za, zb, p, n_ctx, off_a, off_b):
    L, D = za.shape
    T = L + n_ctx
    tb = _pick(n_ctx, (256, 128, 64, 32, 16))
    nctx = n_ctx // tb
    tc = _gate_cols(D, off_a)[0]
    oa, ob = off_a // tc, off_b // tc

    def body(dz_ref, za_ref, zb_ref, ga_ref, gb_ref, dza_ref, dzb_ref, dga_ref, dgb_ref):
        i = pl.program_id(1)

        @pl.when(i < nctx)
        def _():
            dga_ref[...] = jnp.zeros_like(dga_ref)
            dgb_ref[...] = jnp.zeros_like(dgb_ref)

        @pl.when(i >= nctx)
        def _():
            dzv = dz_ref[...].astype(F32)
            sa = _sigmoid(ga_ref[...])
            sb = _sigmoid(gb_ref[...])
            dza_ref[...] = (dzv * sa).astype(BF16)
            dzb_ref[...] = (dzv * sb).astype(BF16)
            dga_ref[...] = (dzv * za_ref[...].astype(F32) * sa * (1.0 - sa)).astype(BF16)
            dgb_ref[...] = (dzv * zb_ref[...].astype(F32) * sb * (1.0 - sb)).astype(BF16)

    lat = pl.BlockSpec((tb, tc), lambda j, i: (jnp.maximum(i - nctx, 0), j))
    allr = pl.BlockSpec((tb, tc), lambda j, i: (i, j))
    return _pcall(
        body, name="merge_bwd", grid=(D // tc, T // tb),
        in_specs=[lat, lat, lat, pl.BlockSpec((tb, tc), lambda j, i: (i, oa + j)),
                  pl.BlockSpec((tb, tc), lambda j, i: (i, ob + j))],
        out_specs=[lat, lat, allr, allr],
        out_shape=[jax.ShapeDtypeStruct((L, D), BF16), jax.ShapeDtypeStruct((L, D), BF16),
                   jax.ShapeDtypeStruct((T, D), BF16), jax.ShapeDtypeStruct((T, D), BF16)],
        compiler_params=_params(("arbitrary", "arbitrary")),
    )(dz, za, zb, p, p)


def _shift_down(u, rows):
    return jnp.where(rows == 0, 0.0, pltpu.roll(u, 1, 0))


def _shift_up(u, rows):
    n = u.shape[0]
    return jnp.where(rows == n - 1, 0.0, pltpu.roll(u, n - 1, 0))


def _convgate_fwd(u1, u3, cw, cb):
    L, F = u1.shape
    tc = _pick(F, (256, 128))

    def body(u1_ref, u3_ref, w_ref, b_ref, a_ref):
        u = u1_ref[...].astype(F32)
        rows = lax.broadcasted_iota(jnp.int32, u.shape, 0)
        cv = _shift_down(u, rows) * w_ref[0:1, :] + u * w_ref[1:2, :] + _shift_up(u, rows) * w_ref[2:3, :] + b_ref[...]
        a_ref[...] = (cv * _sigmoid(cv) * u3_ref[...].astype(F32)).astype(BF16)

    blk = pl.BlockSpec((L, tc), lambda j: (0, j))
    return _pcall(
        body, name="convgate_fwd", grid=(F // tc,),
        in_specs=[blk, blk, pl.BlockSpec((8, tc), lambda j: (0, j)), pl.BlockSpec((1, tc), lambda j: (0, j))],
        out_specs=blk,
        out_shape=jax.ShapeDtypeStruct((L, F), BF16),
        compiler_params=_params(("parallel",)),
    )(u1, u3, cw, cb)


def _convgate_bwd(u1, u3, da, cw, cb):
    L, F = u1.shape
    tc = _pick(F, (256, 128))

    def body(u1_ref, u3_ref, da_ref, w_ref, b_ref, du1_ref, du3_ref, s_ref):
        u = u1_ref[...].astype(F32)
        rows = lax.broadcasted_iota(jnp.int32, u.shape, 0)
        um, up = _shift_down(u, rows), _shift_up(u, rows)
        w0, w1, w2 = w_ref[0:1, :], w_ref[1:2, :], w_ref[2:3, :]
        cv = um * w0 + u * w1 + up * w2 + b_ref[...]
        s = _sigmoid(cv)
        dav = da_ref[...].astype(F32)
        du3_ref[...] = (dav * cv * s).astype(BF16)
        dcv = dav * u3_ref[...].astype(F32) * (s * (1.0 + cv * (1.0 - s)))
        du1_ref[...] = (_shift_up(dcv, rows) * w0 + dcv * w1 + _shift_down(dcv, rows) * w2).astype(BF16)
        r8 = lax.broadcasted_iota(jnp.int32, (8, tc), 0)
        s0 = jnp.sum(dcv * um, axis=0, keepdims=True)
        s1 = jnp.sum(dcv * u, axis=0, keepdims=True)
        s2 = jnp.sum(dcv * up, axis=0, keepdims=True)
        s3 = jnp.sum(dcv, axis=0, keepdims=True)
        s_ref[...] = jnp.where(r8 == 0, s0, jnp.where(r8 == 1, s1, jnp.where(r8 == 2, s2,
                     jnp.where(r8 == 3, s3, 0.0))))

    blk = pl.BlockSpec((L, tc), lambda j: (0, j))
    v8 = pl.BlockSpec((8, tc), lambda j: (0, j))
    return _pcall(
        body, name="convgate_bwd", grid=(F // tc,),
        in_specs=[blk, blk, blk, v8, pl.BlockSpec((1, tc), lambda j: (0, j))],
        out_specs=[blk, blk, v8],
        out_shape=[jax.ShapeDtypeStruct((L, F), BF16), jax.ShapeDtypeStruct((L, F), BF16),
                   jax.ShapeDtypeStruct((8, F), F32)],
        compiler_params=_params(("parallel",)),
    )(u1, u3, da, cw, cb)


def _lower_bound(lbl_ref, d):
    l0, l1 = lbl_ref[d, 0:1, :], lbl_ref[d, 1:2, :]
    m = jnp.maximum(l0, l1)
    e0, e1 = jnp.exp(l0 - m), jnp.exp(l1 - m)
    return e0 / (e0 + e1)


def _chunk_cumsum(x, rev):
    n = x.shape[0]
    r = lax.broadcasted_iota(jnp.int32, x.shape, 0) % CHUNK
    k = 1
    while k < CHUNK:
        if rev:
            x = x + jnp.where(r < CHUNK - k, pltpu.roll(x, n - k, 0), 0.0)
        else:
            x = x + jnp.where(r >= k, pltpu.roll(x, k, 0), 0.0)
        k *= 2
    return x


def _gate_terms(z, lb):
    sg = _sigmoid(z)
    f = lb + (1.0 - lb) * sg
    return sg, f


def _decay_terms(z, lb, rev):
    _, f = _gate_terms(z, lb)
    g = jnp.log(f)
    return 1.0 - f, _chunk_cumsum(g, rev), _chunk_cumsum(g, not rev) - g


def _chunk_total(c, rev):
    return c[0:1, :] if rev else c[CHUNK - 1:CHUNK, :]


def _pair_decay(c, s, rev):
    t = lax.broadcasted_iota(jnp.int32, (CHUNK, 1), 0)
    later = (t <= s) if rev else (t >= s)
    return jnp.where(later, jnp.exp(c - c[s:s + 1, :]), 0.0)


def _scan_chunk(i, n_ctx_chunks, n_chunks, rev):
    if not rev:
        return i
    return jnp.where(i < n_ctx_chunks, n_ctx_chunks - 1 - i, n_chunks + n_ctx_chunks - 1 - i)


def _rows(ci):
    return pl.ds(pl.multiple_of(ci * CHUNK, CHUNK), CHUNK)


def _hgrn_cols(HA):
    return HA // HEAD


def _hgrn_fwd(p, lbl, ng, n_ctx, HA):
    T = p.shape[0]
    L = T - n_ctx
    nh = _hgrn_cols(HA)
    nc, ncc = T // CHUNK, n_ctx // CHUNK

    def body(q_ref, zf_ref, zb_ref, v_ref, og_ref, lbl_ref, ng_ref, ya_ref, o_ref, st_ref,
             c_scr, k_scr, qe_scr, ke_scr, o_scr):
        dirs = ((0, False, zf_ref), (1, True, zb_ref))
        for d, rev, z_ref in dirs:
            k, c, rest = _decay_terms(z_ref[...], _lower_bound(lbl_ref, d), rev)
            c_scr[d] = c
            k_scr[d] = k
            qe_scr[d] = (q_ref[...] * jnp.exp(c)).astype(BF16)
            ke_scr[d] = (k * jnp.exp(rest)).astype(BF16)

        def step(i, states):
            new = []
            for (d, rev, _), St in zip(dirs, states):
                ci = _scan_chunk(i, ncc, nc, rev)
                rows = _rows(ci)
                q, v, c, k = q_ref[rows, :], v_ref[rows, :], c_scr[d, rows, :], k_scr[d, rows, :]
                st_ref[0, d, ci] = St.astype(BF16)
                o = jnp.zeros((CHUNK, HEAD), F32)
                for s in range(CHUNK):
                    E = _pair_decay(c, s, rev)
                    a = jnp.sum(q * E * k[s:s + 1, :], axis=1, keepdims=True)
                    o = o + a * v[s:s + 1, :]
                o_scr[d, rows, :] = o + _dot_nt(qe_scr[d, rows, :], St.astype(BF16))
                new.append(St * jnp.exp(_chunk_total(c, rev)) + _dot_tn(v.astype(BF16), ke_scr[d, rows, :]))
            return tuple(new)

        zero = jnp.zeros((HEAD, HEAD), F32)
        lax.fori_loop(0, nc, step, (zero, zero), unroll=2)

        o = o_scr[0, pl.ds(n_ctx, L), :] + o_scr[1, pl.ds(n_ctx, L), :]
        o_ref[...] = o
        r = lax.rsqrt(jnp.mean(o * o, axis=-1, keepdims=True) + EPS)
        og = og_ref[pl.ds(n_ctx, L), :]
        ya_ref[...] =(o * r * ng_ref[...] * (og * _sigmoid(og))).astype(BF16)

    cb = HA // HEAD
    col = lambda kk: pl.BlockSpec((T, HEAD), lambda h: (0, kk * cb + h))
    return _pcall(
        body, name="hgrn_fwd", grid=(nh,),
        in_specs=[col(0), col(1), col(2), col(3), col(4),
                  pl.BlockSpec((2, 2, HEAD), lambda h: (0, 0, h)), pl.BlockSpec((1, HEAD), lambda h: (0, 0))],
        out_specs=[pl.BlockSpec((L, HEAD), lambda h: (0, h)), pl.BlockSpec((L, HEAD), lambda h: (0, h)),
                   pl.BlockSpec((1, 2, nc, HEAD, HEAD), lambda h: (h, 0, 0, 0, 0))],
        out_shape=[jax.ShapeDtypeStruct((L, HA), BF16), jax.ShapeDtypeStruct((L, HA), F32),
                   jax.ShapeDtypeStruct((nh, 2, nc, HEAD, HEAD), BF16)],
        scratch_shapes=[pltpu.VMEM((2, T, HEAD), F32), pltpu.VMEM((2, T, HEAD), F32),
                        pltpu.VMEM((2, T, HEAD), BF16), pltpu.VMEM((2, T, HEAD), BF16),
                        pltpu.VMEM((2, T, HEAD), F32)],
        compiler_params=_params(("parallel",)),
    )(p, p, p, p, p, lbl, ng)


def _hgrn_bwd(p, lbl, ng, o, dya, st, n_ctx, HA):
    T = p.shape[0]
    L = T - n_ctx
    nh = _hgrn_cols(HA)
    nc, ncc = T // CHUNK, n_ctx // CHUNK

    def body(q_ref, zf_ref, zb_ref, v_ref, og_ref, lbl_ref, ng_ref, o_ref, dya_ref, st_ref,
             dq_ref, dzf_ref, dzb_ref, dv_ref, dog_ref, dlbl_ref, dng_ref,
             do_scr, c_scr, k_scr, qe_scr, ke_scr, dg_scr, dk_scr, dq_scr, dv_scr):
        h = pl.program_id(0)
        ov = o_ref[...]
        r = lax.rsqrt(jnp.mean(ov * ov, axis=-1, keepdims=True) + EPS)
        oh = ov * r
        ogv = og_ref[pl.ds(n_ctx, L), :]
        sg_o = _sigmoid(ogv)
        dyv = dya_ref[...]
        ngv = ng_ref[...]
        dog_ref[pl.ds(0, n_ctx), :] = jnp.zeros((n_ctx, HEAD), BF16)
        dog_ref[pl.ds(n_ctx, L), :] = (dyv * oh * ngv * (sg_o * (1.0 + ogv * (1.0 - sg_o)))).astype(BF16)
        don = dyv * (ogv * sg_o)
        dng = jnp.sum(don * oh, axis=0, keepdims=True)
        doh = don * ngv
        do_scr[pl.ds(0, n_ctx), :] = jnp.zeros((n_ctx, HEAD), F32)
        do_scr[pl.ds(n_ctx, L), :] = r * (doh - oh * jnp.mean(doh * oh, axis=-1, keepdims=True))

        @pl.when(h == 0)
        def _():
            dng_ref[...] = jnp.zeros_like(dng_ref)

        dng_ref[0:1, :] += dng

        t16 = lax.broadcasted_iota(jnp.int32, (CHUNK, HEAD), 0)
        dirs = ((0, False, zf_ref, dzf_ref), (1, True, zb_ref, dzb_ref))
        for d, rev, z_ref, _ in dirs:
            k, c, rest = _decay_terms(z_ref[...], _lower_bound(lbl_ref, d), rev)
            c_scr[d] = c
            k_scr[d] = k
            qe_scr[d] = (q_ref[...] * jnp.exp(c)).astype(BF16)
            ke_scr[d] = (k * jnp.exp(rest)).astype(BF16)
        dq_scr[...] = jnp.zeros_like(dq_scr)
        dv_scr[...] = jnp.zeros_like(dv_scr)

        zero = jnp.zeros((HEAD, HEAD), F32)

        def bwd_step(ii, carry):
            i = nc - 1 - ii
            new = []
            for (d, rev, _, _), dSt in zip(dirs, carry):
                ci = _scan_chunk(i, ncc, nc, rev)
                rows = _rows(ci)
                q, v, do = q_ref[rows, :], v_ref[rows, :], do_scr[rows, :]
                c, k = c_scr[d, rows, :], k_scr[d, rows, :]
                tot = _chunk_total(c, rev)
                etot = jnp.exp(tot)
                St = st_ref[0, d, ci]
                dSb = dSt.astype(BF16)
                do_b = do.astype(BF16)
                dq_x = _dot(do_b, St) * jnp.exp(c)
                dk_x = _dot(v.astype(BF16), dSb) * jnp.exp(tot - c)
                dv_x = _dot_nt(ke_scr[d, rows, :], dSb)
                dtot = (jnp.sum(St.astype(F32) * dSt, axis=0, keepdims=True) * etot
                        + jnp.sum(k * dk_x, axis=0, keepdims=True))
                dq = dk = dv = jnp.zeros((CHUNK, HEAD), F32)
                for s in range(CHUNK):
                    E = _pair_decay(c, s, rev)
                    XE = E * k[s:s + 1, :]
                    a = jnp.sum(q * XE, axis=1, keepdims=True)
                    da = jnp.sum(do * v[s:s + 1, :], axis=1, keepdims=True)
                    dq = dq + da * XE
                    dk_row = jnp.sum(da * q * E, axis=0, keepdims=True)
                    dv_row = jnp.sum(a * do, axis=0, keepdims=True)
                    dk = dk + jnp.where(t16 == s, dk_row, 0.0)
                    dv = dv + jnp.where(t16 == s, dv_row, 0.0)
                dq, dk, dv = dq + dq_x, dk + dk_x, dv + dv_x
                dg_scr[d, rows, :] = _chunk_cumsum(q * dq - k * dk, not rev) + dtot
                dk_scr[d, rows, :] = dk
                dq_scr[rows, :] += dq
                dv_scr[rows, :] += dv
                new.append(dSt * etot + _dot_tn(do_b, qe_scr[d, rows, :]))
            return tuple(new)

        lax.fori_loop(0, nc, bwd_step, (zero, zero), unroll=2)

        for d, _, z_ref, dz_ref in dirs:
            lb = _lower_bound(lbl_ref, d)
            sg, f = _gate_terms(z_ref[...], lb)
            df = dg_scr[d] / f - dk_scr[d]
            dz_ref[...] = (df * (1.0 - lb) * sg * (1.0 - sg)).astype(BF16)
            dl0 = jnp.sum(df * (1.0 - sg), axis=0, keepdims=True) * lb * (1.0 - lb)
            dlbl_ref[d, 0:1, :] = dl0
            dlbl_ref[d, 1:2, :] = -dl0
        dq_ref[...] = dq_scr[...].astype(BF16)
        dv_ref[...] = dv_scr[...].astype(BF16)

    cb = HA // HEAD
    col = lambda kk: pl.BlockSpec((T, HEAD), lambda h: (0, kk * cb + h))
    tcol = pl.BlockSpec((T, HEAD), lambda h: (0, h))
    lcol = pl.BlockSpec((L, HEAD), lambda h: (0, h))
    outs = _pcall(
        body, name="hgrn_bwd", grid=(nh,),
        in_specs=[col(0), col(1), col(2), col(3), col(4),
                  pl.BlockSpec((2, 2, HEAD), lambda h: (0, 0, h)), pl.BlockSpec((1, HEAD), lambda h: (0, 0)),
                  lcol, lcol,
                  pl.BlockSpec((1, 2, nc, HEAD, HEAD), lambda h: (h, 0, 0, 0, 0), pipeline_mode=pl.Buffered(1))],
        out_specs=[tcol, tcol, tcol, tcol, tcol, pl.BlockSpec((2, 2, HEAD), lambda h: (0, 0, h)),
                   pl.BlockSpec((8, HEAD), lambda h: (0, 0))],
        out_shape=[jax.ShapeDtypeStruct((T, HA), BF16)] * 5 + [jax.ShapeDtypeStruct((2, 2, HA), F32),
                                                               jax.ShapeDtypeStruct((8, HEAD), F32)],
        scratch_shapes=[pltpu.VMEM((T, HEAD), F32),
                        pltpu.VMEM((2, T, HEAD), F32), pltpu.VMEM((2, T, HEAD), F32),
                        pltpu.VMEM((2, T, HEAD), BF16), pltpu.VMEM((2, T, HEAD), BF16),
                        pltpu.VMEM((2, T, HEAD), F32), pltpu.VMEM((2, T, HEAD), F32),
                        pltpu.VMEM((T, HEAD), F32), pltpu.VMEM((T, HEAD), F32)],
        compiler_params=_params(("arbitrary",)),
    )(p, p, p, p, p, lbl, ng, o, dya, st)
    return outs


def _swap_halves(t, lane):
    q = HEAD // 4
    return jnp.where((lane % (2 * q)) < q, pltpu.roll(t, HEAD - q, 1), pltpu.roll(t, q, 1))


def _qk_norm(t, g):
    r = lax.rsqrt(jnp.mean(t * t, axis=-1, keepdims=True) + EPS)
    return t * r, r


def _rope(t, cos, sin, lane):
    return t * cos + _swap_halves(t, lane) * sin


def _qk_norm_bwd(dy, th, r, g):
    dth = dy * g
    return r * (dth - th * jnp.mean(dth * th, axis=-1, keepdims=True)), jnp.sum(dy * th, axis=0, keepdims=True)


def _rope_bwd(dy, cos, sin, lane):
    return dy * cos + _swap_halves(dy * sin, lane)


def _na_geometry(L):
    n_rows = L // GRID_W
    kr = min(WIN_R, n_rows)
    return n_rows, kr


def _na_prep(q_ref, k_ref, v_ref, gq_ref, gk_ref, cos_ref, sin_ref, qs, ks, vs, n_ctx, L):
    lane = lax.broadcasted_iota(jnp.int32, (L, HEAD), 1)
    cos, sin = cos_ref[...], sin_ref[...]
    qh, _ = _qk_norm(q_ref[pl.ds(n_ctx, L), :], None)
    qs[...] = _rope(qh * gq_ref[...], cos, sin, lane).astype(BF16)
    kh, _ = _qk_norm(k_ref[pl.ds(n_ctx, L), :], None)
    ks[pl.ds(n_ctx, L), :] = _rope(kh * gk_ref[...], cos, sin, lane).astype(BF16)
    kc, _ = _qk_norm(k_ref[pl.ds(0, n_ctx), :], None)
    ks[pl.ds(0, n_ctx), :] = (kc * gk_ref[...]).astype(BF16)
    vs[...] = v_ref[...].astype(BF16)


def _na_scores(r, qs, ks, bias_ref, n_ctx, n_rows, kr):
    scale = HEAD ** -0.5
    r0 = jnp.clip(r - WIN_R // 2, 0, n_rows - kr)
    qrows = pl.ds(pl.multiple_of(r * GRID_W, GRID_W), GRID_W)
    krows = pl.ds(pl.multiple_of(n_ctx + r0 * GRID_W, GRID_W), kr * GRID_W)
    qv = qs[qrows, :]
    sb = _dot_nt(qv, ks[krows, :]) * scale
    b0 = r0 - r + (WIN_R - 1)
    sb = sb + jnp.concatenate([bias_ref[0, b0 + 2 * jj] for jj in range(kr // 2)], axis=1)
    sc = _dot_nt(qv, ks[pl.ds(0, n_ctx), :]) * scale
    m = jnp.maximum(jnp.max(sb, axis=1, keepdims=True), jnp.max(sc, axis=1, keepdims=True))
    eb, ec = jnp.exp(sb - m), jnp.exp(sc - m)
    inv = 1.0 / (jnp.sum(eb, axis=1, keepdims=True) + jnp.sum(ec, axis=1, keepdims=True))
    return eb * inv, ec * inv, qrows, krows, b0


def _na_fwd(p, bias, gq, gk, cos, sin, n_ctx, off, HB):
    T = p.shape[0]
    L = T - n_ctx
    nh = HB // HEAD
    n_rows, kr = _na_geometry(L)
    ob = off // HEAD

    def body(q_ref, k_ref, v_ref, bias_ref, gq_ref, gk_ref, cos_ref, sin_ref, y_ref, qs, ks, vs):
        _na_prep(q_ref, k_ref, v_ref, gq_ref, gk_ref, cos_ref, sin_ref, qs, ks, vs, n_ctx, L)

        def step(i, carry):
            for u in range(2):
                pb, pc, qrows, krows, _ = _na_scores(2 * i + u, qs, ks, bias_ref, n_ctx, n_rows, kr)
                y = _dot(pb.astype(BF16), vs[krows, :]) + _dot(pc.astype(BF16), vs[pl.ds(0, n_ctx), :])
                y_ref[qrows, :] = y.astype(BF16)
            return carry

        lax.fori_loop(0, n_rows // 2, step, 0)

    col = lambda kk: pl.BlockSpec((T, HEAD), lambda h: (0, ob + kk * nh + h))
    vec = pl.BlockSpec((1, HEAD), lambda h: (0, 0))
    tab = pl.BlockSpec((L, HEAD), lambda h: (0, 0))
    return _pcall(
        body, name="na_fwd", grid=(nh,),
        in_specs=[col(0), col(1), col(2), pl.BlockSpec((1,) + bias.shape[1:], lambda h: (h, 0, 0, 0)),
                  vec, vec, tab, tab],
        out_specs=pl.BlockSpec((L, HEAD), lambda h: (0, h)),
        out_shape=jax.ShapeDtypeStruct((L, HB), BF16),
        scratch_shapes=[pltpu.VMEM((L, HEAD), BF16), pltpu.VMEM((T, HEAD), BF16), pltpu.VMEM((T, HEAD), BF16)],
        compiler_params=_params(("parallel",)),
    )(p, p, p, bias, gq, gk, cos, sin)


def _na_bwd(p, bias, gq, gk, cos, sin, dyb, n_ctx, off, HB):
    T = p.shape[0]
    L = T - n_ctx
    nh = HB // HEAD
    n_rows, kr = _na_geometry(L)
    ob = off // HEAD
    scale = HEAD ** -0.5

    def body(q_ref, k_ref, v_ref, bias_ref, gq_ref, gk_ref, cos_ref, sin_ref, dy_ref,
             dq_ref, dk_ref, dv_ref, dbias_ref, dg_ref, qs, ks, vs, dqa, dka, dva):
        h = pl.program_id(0)
        _na_prep(q_ref, k_ref, v_ref, gq_ref, gk_ref, cos_ref, sin_ref, qs, ks, vs, n_ctx, L)
        dka[...] = jnp.zeros_like(dka)
        dva[...] = jnp.zeros_like(dva)
        dbias_ref[...] = jnp.zeros_like(dbias_ref)

        crows = pl.ds(0, n_ctx)

        def step(i, carry):
            done = []
            for u in range(2):
                pb, pc, qrows, krows, b0 = _na_scores(2 * i + u, qs, ks, bias_ref, n_ctx, n_rows, kr)
                do = dy_ref[qrows, :]
                qv = qs[qrows, :]
                dpb = _dot_nt(do, vs[krows, :])
                dpc = _dot_nt(do, vs[crows, :])
                delta = jnp.sum(pb * dpb, axis=1, keepdims=True) + jnp.sum(pc * dpc, axis=1, keepdims=True)
                dsb = pb * (dpb - delta)
                dsc = pc * (dpc - delta)
                dsb_b, dsc_b = dsb.astype(BF16), dsc.astype(BF16)
                dqa[qrows, :] = (_dot(dsb_b, ks[krows, :]) + _dot(dsc_b, ks[crows, :])) * scale
                done.append((krows, b0, dsb, _dot_tn(dsb_b, qv) * scale, _dot_tn(dsc_b, qv) * scale,
                             _dot_tn(pb.astype(BF16), do), _dot_tn(pc.astype(BF16), do)))
            for krows, b0, dsb, dkb, dkc, dvb, dvc in done:
                for jj in range(kr // 2):
                    dbias_ref[0, b0 + 2 * jj] += dsb[:, jj * 2 * GRID_W:(jj + 1) * 2 * GRID_W]
                dka[krows, :] += dkb
                dka[crows, :] += dkc
                dva[krows, :] += dvb
                dva[crows, :] += dvc
            return carry

        lax.fori_loop(0, n_rows // 2, step, 0)

        lane = lax.broadcasted_iota(jnp.int32, (L, HEAD), 1)
        cos, sin = cos_ref[...], sin_ref[...]
        lat, ctx = pl.ds(n_ctx, L), pl.ds(0, n_ctx)
        gqv, gkv = gq_ref[...], gk_ref[...]
        qh, rq = _qk_norm(q_ref[lat, :], None)
        dq, dgq = _qk_norm_bwd(_rope_bwd(dqa[...], cos, sin, lane), qh, rq, gqv)
        dq_ref[ctx, :] = jnp.zeros((n_ctx, HEAD), BF16)
        dq_ref[lat, :] = dq.astype(BF16)
        kh, rk = _qk_norm(k_ref[lat, :], None)
        dk, dgk = _qk_norm_bwd(_rope_bwd(dka[lat, :], cos, sin, lane), kh, rk, gkv)
        dk_ref[lat, :] = dk.astype(BF16)
        kch, rkc = _qk_norm(k_ref[ctx, :], None)
        dkc, dgkc = _qk_norm_bwd(dka[ctx, :], kch, rkc, gkv)
        dk_ref[ctx, :] = dkc.astype(BF16)
        dv_ref[...] = dva[...].astype(BF16)

        @pl.when(h == 0)
        def _():
            dg_ref[...] = jnp.zeros_like(dg_ref)

        dg_ref[0:1, :] += dgq
        dg_ref[1:2, :] += dgk + dgkc

    col = lambda kk: pl.BlockSpec((T, HEAD), lambda h: (0, ob + kk * nh + h))
    vec = pl.BlockSpec((1, HEAD), lambda h: (0, 0))
    tab = pl.BlockSpec((L, HEAD), lambda h: (0, 0))
    tcol = pl.BlockSpec((T, HEAD), lambda h: (0, h))
    bspec = pl.BlockSpec((1,) + bias.shape[1:], lambda h: (h, 0, 0, 0))
    return _pcall(
        body, name="na_bwd", grid=(nh,),
        in_specs=[col(0), col(1), col(2), bspec, vec, vec, tab, tab, pl.BlockSpec((L, HEAD), lambda h: (0, h))],
        out_specs=[tcol, tcol, tcol, bspec, pl.BlockSpec((8, HEAD), lambda h: (0, 0))],
        out_shape=[jax.ShapeDtypeStruct((T, HB), BF16)] * 3 + [jax.ShapeDtypeStruct(bias.shape, F32),
                                                               jax.ShapeDtypeStruct((8, HEAD), F32)],
        scratch_shapes=[pltpu.VMEM((L, HEAD), BF16), pltpu.VMEM((T, HEAD), BF16), pltpu.VMEM((T, HEAD), BF16),
                        pltpu.VMEM((L, HEAD), F32), pltpu.VMEM((T, HEAD), F32), pltpu.VMEM((T, HEAD), F32)],
        compiler_params=_params(("arbitrary",)),
    )(p, p, p, bias, gq, gk, cos, sin, dyb)


def _bias_tables():
    w = np.arange(GRID_W)
    col_start = np.clip(w - WIN_C // 2, 0, GRID_W - WIN_C)
    col_in = (w[None, :] >= col_start[:, None]) & (w[None, :] < col_start[:, None] + WIN_C)
    dc = np.clip(w[None, :] - w[:, None], -(WIN_C - 1), WIN_C - 1) + WIN_C - 1
    n_pair = 2 * WIN_R
    ridx = np.zeros((n_pair, GRID_W, 2 * GRID_W), np.int32)
    cidx = np.zeros((n_pair, GRID_W, 2 * GRID_W), np.int32)
    valid = np.zeros((n_pair, GRID_W, 2 * GRID_W), bool)
    for i in range(n_pair):
        for half in range(2):
            row = i + half
            sl = slice(half * GRID_W, (half + 1) * GRID_W)
            ridx[i, :, sl] = min(row, 2 * WIN_R - 2)
            cidx[i, :, sl] = dc
            valid[i, :, sl] = col_in & (row <= 2 * WIN_R - 2)
    return ridx, cidx, valid


def _bias_onehot():
    _, cidx, valid = _bias_tables()
    K = GRID_W * 2 * GRID_W
    oh = np.zeros((K, 128), np.float32)
    neg = np.full((1, K), NEG, np.float32)
    for cq in range(GRID_W):
        for ll in range(2 * GRID_W):
            if valid[0, cq, ll]:
                oh[cq * 2 * GRID_W + ll, (ll // GRID_W) * 64 + cidx[0, cq, ll]] = 1.0
                neg[0, cq * 2 * GRID_W + ll] = 0.0
    return oh, neg


def _expand_bias(table):
    H = table.shape[0]
    n_pair, n_dc = 2 * WIN_R, 2 * WIN_C - 1
    tp = jnp.pad(table, ((0, 0), (0, n_pair + 1 - table.shape[1]), (0, 64 - n_dc)))
    t2 = jnp.concatenate([tp[:, :n_pair], tp[:, 1:n_pair + 1]], axis=-1).reshape(H * n_pair, 128)
    oh, neg = _bias_onehot()

    def body(t_ref, oh_ref, neg_ref, o_ref):
        o_ref[...] = lax.dot_general(t_ref[...], oh_ref[...], (((1,), (1,)), ((), ())), precision=HI,
                                     preferred_element_type=F32) + neg_ref[...]

    out = _pcall(body, name="bias_expand", out_shape=jax.ShapeDtypeStruct((H * n_pair, oh.shape[0]), F32),
                         compiler_params=_params())(t2, jnp.asarray(oh), jnp.asarray(neg))
    return out.reshape(H, n_pair, GRID_W, 2 * GRID_W)


def _bias_grad(dbias):
    H = dbias.shape[0]
    n_pair, n_dc = 2 * WIN_R, 2 * WIN_C - 1
    K = GRID_W * 2 * GRID_W
    oh, _ = _bias_onehot()
    flat = dbias.reshape(H * n_pair, K)

    def body(d_ref, oh_ref, o_ref):
        o_ref[...] = jnp.dot(d_ref[...], oh_ref[...], precision=HI, preferred_element_type=F32)

    g = _pcall(body, name="bias_grad", out_shape=jax.ShapeDtypeStruct((H * n_pair, 128), F32),
                       compiler_params=_params())(flat, jnp.asarray(oh))
    g = g.reshape(H, n_pair, 128)
    left, right = g[:, :, :n_dc], g[:, :, 64:64 + n_dc]
    out = left[:, :n_pair - 1]
    return out.at[:, 1:].add(right[:, :n_pair - 2])


def _rope_tables(L):
    pos = np.arange(L)
    row = (pos // GRID_W).astype(np.float32)
    colp = (pos % GRID_W).astype(np.float32)
    half = HEAD // 2
    nf = half // 2
    inv = (ROPE_THETA ** (-np.arange(nf, dtype=np.float32) / nf)).astype(np.float32)

    def tabs(pv):
        ang = pv[:, None] * inv[None, :]
        c, s = np.cos(ang), np.sin(ang)
        return np.concatenate([c, c], axis=1), np.concatenate([-s, s], axis=1)

    cr, sr = tabs(row)
    cc, sc = tabs(colp)
    return (jnp.asarray(np.concatenate([cr, cc], axis=1), F32), jnp.asarray(np.concatenate([sr, sc], axis=1), F32))


def _adamw(w, g, m, v, name, after=None, copy_g=False):
    R, C = w.shape
    tr = _row_tile(R, C)
    c1 = 1.0 - ADAM_B1 ** ADAM_STEP
    c2 = 1.0 - ADAM_B2 ** ADAM_STEP
    deps = [] if after is None else [after]
    n_out = 4 if copy_g else 3

    def body(w_ref, g_ref, m_ref, v_ref, *rest):
        d_ref, mo_ref, vo_ref = rest[len(deps):len(deps) + 3]
        gv = g_ref[...]
        mn = ADAM_B1 * m_ref[...] + (1.0 - ADAM_B1) * gv
        vn = ADAM_B2 * v_ref[...] + (1.0 - ADAM_B2) * (gv * gv)
        mo_ref[...] = mn
        vo_ref[...] = vn
        d_ref[...] = -ADAM_LR * ((mn / c1) / (jnp.sqrt(vn / c2) + ADAM_EPS) + ADAM_WD * w_ref[...])
        if copy_g:
            rest[-1][...] = gv

    blk = pl.BlockSpec((tr, C), lambda i: (i, 0))
    return _pcall(
        body, name=name, grid=(R // tr,),
        in_specs=[blk] * 4 + [_ANY] * len(deps), out_specs=[blk] * n_out,
        out_shape=[jax.ShapeDtypeStruct((R, C), F32)] * n_out,
        compiler_params=_params(("parallel",)),
    )(w, g, m, v, *deps)


PACK_W = 1024


def _pack(parts):
    flat, offs, pos = [], [], 0
    for a in parts:
        n = a.size
        padn = -n % PACK_W
        flat.append(jnp.pad(a.reshape(-1).astype(F32), (0, padn)))
        offs.append((pos, n, a.shape))
        pos += n + padn
    tail = -pos % (8 * PACK_W)
    if tail:
        flat.append(jnp.zeros((tail,), F32))
    return jnp.concatenate(flat).reshape(-1, PACK_W), offs


def _unpack(buf, offs, i):
    pos, n, shape = offs[i]
    return buf.reshape(buf.shape[:-2] + (-1,))[..., pos:pos + n].reshape(buf.shape[:-2] + shape)


def kernel(x, c, ctx, c_ctx, ada_w, ada_b, norm1_g, norm2_g, w_in, hgrn_lb_logits, hgrn_norm_g, na_q_norm_g, na_k_norm_g, na_rel_bias, w_branch_a, w_branch_b, w_out, ffn_w1, ffn_w3, ffn_conv_w, ffn_conv_b, ffn_w2, loss_target, m_c_ctx, m_ada_w, m_ada_b, m_norm1_g, m_norm2_g, m_w_in, m_hgrn_lb_logits, m_hgrn_norm_g, m_na_q_norm_g, m_na_k_norm_g, m_na_rel_bias, m_w_branch_a, m_w_branch_b, m_w_out, m_ffn_w1, m_ffn_w3, m_ffn_conv_w, m_ffn_conv_b, m_ffn_w2, v_c_ctx, v_ada_w, v_ada_b, v_norm1_g, v_norm2_g, v_w_in, v_hgrn_lb_logits, v_hgrn_norm_g, v_na_q_norm_g, v_na_k_norm_g, v_na_rel_bias, v_w_branch_a, v_w_branch_b, v_w_out, v_ffn_w1, v_ffn_w3, v_ffn_conv_w, v_ffn_conv_b, v_ffn_w2):
    weights = dict(c_ctx=c_ctx, ada_w=ada_w, ada_b=ada_b, norm1_g=norm1_g, norm2_g=norm2_g, w_in=w_in,
                   hgrn_lb_logits=hgrn_lb_logits, hgrn_norm_g=hgrn_norm_g, na_q_norm_g=na_q_norm_g,
                   na_k_norm_g=na_k_norm_g, na_rel_bias=na_rel_bias, w_branch_a=w_branch_a, w_branch_b=w_branch_b,
                   w_out=w_out, ffn_w1=ffn_w1, ffn_w3=ffn_w3, ffn_conv_w=ffn_conv_w, ffn_conv_b=ffn_conv_b,
                   ffn_w2=ffn_w2)
    moms = dict(c_ctx=(m_c_ctx, v_c_ctx), ada_w=(m_ada_w, v_ada_w), ada_b=(m_ada_b, v_ada_b),
                norm1_g=(m_norm1_g, v_norm1_g), norm2_g=(m_norm2_g, v_norm2_g), w_in=(m_w_in, v_w_in),
                hgrn_lb_logits=(m_hgrn_lb_logits, v_hgrn_lb_logits), hgrn_norm_g=(m_hgrn_norm_g, v_hgrn_norm_g),
                na_q_norm_g=(m_na_q_norm_g, v_na_q_norm_g), na_k_norm_g=(m_na_k_norm_g, v_na_k_norm_g),
                na_rel_bias=(m_na_rel_bias, v_na_rel_bias), w_branch_a=(m_w_branch_a, v_w_branch_a),
                w_branch_b=(m_w_branch_b, v_w_branch_b), w_out=(m_w_out, v_w_out), ffn_w1=(m_ffn_w1, v_ffn_w1),
                ffn_w3=(m_ffn_w3, v_ffn_w3), ffn_conv_w=(m_ffn_conv_w, v_ffn_conv_w),
                ffn_conv_b=(m_ffn_conv_b, v_ffn_conv_b), ffn_w2=(m_ffn_w2, v_ffn_w2))
    order = list(weights)

    L, D = x.shape[1], x.shape[2]
    N = ctx.shape[1]
    T = N + L
    HA = w_branch_a.shape[1]
    HB = w_branch_b.shape[1]
    F = ffn_conv_b.shape[1]
    IN = 5 * HA + 3 * HB + 2 * D
    n_ada = ada_w.shape[2]
    ix, iy, ic = _pos()
    chip = 2 * ix + iy
    dev = 2 * chip + ic

    _PENDING.clear()
    pk0, offs0 = _pack([c[0], hgrn_lb_logits, ffn_conv_w[0]])
    g0 = _allgather8(pk0, "gather_small0")
    c_all = _unpack(g0, offs0, 0)
    lbl_parts = _unpack(g0, offs0, 1)
    lbl = jnp.concatenate([lbl_parts[2 * j] for j in range(N_CHIP)], axis=-1)
    cw_parts = _unpack(g0, offs0, 2)
    cw = jnp.concatenate([cw_parts[2 * j] for j in range(N_CHIP)], axis=-1)
    cw8 = jnp.pad(cw, ((0, 5), (0, 0)))

    cs = jnp.concatenate([c_all, c_ctx[None, :], jnp.zeros((7, D), F32)], axis=0)
    ada_b_mine = lax.dynamic_slice(ada_b, (0, chip * n_ada), (1, n_ada))
    mod_mine = _ada_fwd(cs, ada_w[0], ada_b_mine)
    gm = _allgather8(mod_mine, "gather_mod")
    mod = jnp.concatenate([gm[2 * j] for j in range(N_CHIP)], axis=-1)
    mod_l = lax.dynamic_slice(mod, (dev, 0), (1, N_MOD * D)).reshape(N_MOD, D)
    mod_c = mod[8].reshape(N_MOD, D)
    sh1, sc1, g1, sh2, sc2, g2 = [mod_l[i:i + 1] for i in range(N_MOD)]
    shift1 = jnp.concatenate([mod_c[0:1], sh1], axis=0)
    scale1 = jnp.concatenate([mod_c[1:2], sc1], axis=0)

    shards = [w_in[0], w_branch_a[0], w_branch_b[0], w_out[0], ffn_w1[0], ffn_w3[0], ffn_w2[0]]
    names = ["w_in", "w_a", "w_b", "w_out", "w1", "w3", "w2"]
    slots = [_cast_bf16_slot(s, "cast_" + nm) for s, nm in zip(shards, names)]
    gat_in = _gather_start("in", slots[0:1], gm)
    gat_mix = _gather_start("mix", slots[1:4])
    gat_ffn = _gather_start("ffn", slots[4:7])

    xall = jnp.concatenate([ctx[0], x[0]], axis=0)
    h_all = _rms1_fwd(xall, norm1_g, shift1, scale1, N)
    gat_in = _gather_mid(gat_in, h_all)
    (Win,) = _gather_finish(gat_in, h_all)
    p = _mm_nn(h_all, Win, F32, "mm_p")
    gat_mix = _gather_mid(gat_mix, p)
    y_a, o_a, st_a = _hgrn_fwd(p, lbl, hgrn_norm_g, N, HA)
    Wa, Wb, Wo = _gather_finish(gat_mix, y_a)
    Wo = Wo.reshape(1, D, D)
    gat_ffn = _gather_mid(gat_ffn, y_a)
    bias = _expand_bias(na_rel_bias[0])
    cos, sin = _rope_tables(L)
    off_na = 5 * HA
    y_b = _na_fwd(p, bias, na_q_norm_g, na_k_norm_g, cos, sin, N, off_na, HB)
    za = _mm_nn(y_a, Wa, BF16, "mm_za")
    zb = _mm_nn(y_b, Wb, BF16, "mm_zb")
    off_ga, off_gb = 5 * HA + 3 * HB, 5 * HA + 3 * HB + D
    z = _merge_fwd(za, zb, p, N, off_ga, off_gb)
    mo = _mm_nn(z, Wo, F32, "mm_mo")
    vec2 = jnp.concatenate([g1, norm2_g, sh2, sc2, jnp.zeros((4, D), F32)], axis=0)
    x_mid, h2 = _resid_rms2_fwd(x[0], mo, vec2)
    W1, W3, W2 = _gather_finish(gat_ffn, h2)
    W2 = W2.reshape(1, F, D)
    u1 = _mm_nn(h2, W1, BF16, "mm_u1")
    u3 = _mm_nn(h2, W3, BF16, "mm_u3")
    a = _convgate_fwd(u1, u3, cw8, ffn_conv_b)
    f = _mm_nn(a, W2, F32, "mm_f")
    dy, df, s_loss = _loss_head(x_mid, f, g2, loss_target[0])
    loss = lax.psum(s_loss[1, 0], ("x", "y", "c"))
    d_g2 = s_loss[0:1]

    gW2 = _mm_tn(a, df, 1, "mm_gw2").reshape(N_CHIP, F // N_CHIP, D)
    da = _mm_nt(df, W2, BF16, "mm_da")
    du1, du3, s_conv = _convgate_bwd(u1, u3, da, cw8, ffn_conv_b)
    gW1 = _mm_tn(h2, du1, N_CHIP, "mm_gw1")
    gW3 = _mm_tn(h2, du3, N_CHIP, "mm_gw3")
    rs_ffn = _rs_start("ffn", [gW2, gW1, gW3])
    dh2a = _mm_nt(du1, W1, F32, "mm_dh2a")
    dh2b = _mm_nt(du3, W3, F32, "mm_dh2b")
    rs_ffn = _rs_scatter(rs_ffn, dh2b)
    dxm, dmo, s_rms2 = _resid_rms2_bwd(x_mid, dh2a, dh2b, dy, mo, vec2)
    gWo = _mm_tn(z, dmo, 1, "mm_gwo").reshape(N_CHIP, D // N_CHIP, D)
    dz = _mm_nt(dmo, Wo, BF16, "mm_dz")
    dza, dzb, dga, dgb = _merge_bwd(dz, za, zb, p, N, off_ga, off_gb)
    gWa = _mm_tn(y_a, dza, N_CHIP, "mm_gwa")
    gWb = _mm_tn(y_b, dzb, N_CHIP, "mm_gwb")
    rs_mix = _rs_start("mix", [gWo, gWa, gWb])
    dya = _mm_nt(dza, Wa, F32, "mm_dya")
    dyb = _mm_nt(dzb, Wb, BF16, "mm_dyb")
    rs_mix = _rs_scatter(rs_mix, dyb)
    dq_a, dzf, dzbk, di_a, dog, dlbl, s_ng = _hgrn_bwd(p, lbl, hgrn_norm_g, o_a, dya, st_a, N, HA)
    rs_ffn = _rs_join(rs_ffn, dq_a)
    dq_n, dk_n, dv_n, dbias, s_qk = _na_bwd(p, bias, na_q_norm_g, na_k_norm_g, cos, sin, dyb, N, off_na, HB)
    rs_mix = _rs_join(rs_mix, dq_n)
    dp = jnp.concatenate([dq_a, dzf, dzbk, di_a, dog, dq_n, dk_n, dv_n, dga, dgb], axis=1)
    gWin = _mm_tn(h_all, dp, N_CHIP, "mm_gwin")
    rs_in = _rs_start("in", [gWin])
    dh = _mm_nt(dp, Win, F32, "mm_dh")
    grad_x, s_rms1 = _rms1_bwd(xall, dh, dxm, norm1_g, scale1, N)
    d_table = _bias_grad(dbias)

    zD = jnp.zeros((1, D), F32)
    dmod_l = jnp.concatenate([s_rms1[2:3], s_rms1[3:4], s_rms2[3:4], s_rms2[0:1], s_rms2[1:2], d_g2], axis=0)
    dmod_c = jnp.concatenate([s_rms1[0:1], s_rms1[1:2], zD, zD, zD, zD], axis=0)
    pk1, offs1 = _pack([dmod_l, dmod_c, s_rms1[4], s_rms2[2], dlbl, s_ng[0], s_qk[0], s_qk[1], d_table,
                        s_conv[0:3], s_conv[3]])
    g1all = _allgather8(pk1, "gather_small1")
    tot1 = _sum8(g1all, "sum_small1")
    dmod_rows = _unpack(g1all, offs1, 0).reshape(N_DEV, N_MOD * D)
    dmod_c_tot = _unpack(tot1, offs1, 1).reshape(1, N_MOD * D)
    dmod16 = jnp.concatenate([dmod_rows, dmod_c_tot, jnp.zeros((7, N_MOD * D), F32)], axis=0)
    dmod16_mine = lax.dynamic_slice(dmod16, (0, chip * n_ada), (16, n_ada))
    g_ada_w, dact = _ada_bwd(cs, ada_w[0], dmod16_mine)
    pk2, offs2 = _pack([dact[8]])
    g2all = _allgather8(pk2, "gather_small2")
    dact_rows = _unpack(g2all, offs2, 0)
    dact_sel = jnp.concatenate([dact_rows[2 * j][None] for j in range(N_CHIP)] + [jnp.zeros((4, D), F32)], axis=0)

    grads = {}
    grads["ada_w"] = g_ada_w[None]
    grads["ada_b"] = (_unpack(tot1, offs1, 0) + _unpack(tot1, offs1, 1)).reshape(1, N_MOD * D)
    grads["norm1_g"] = _unpack(tot1, offs1, 2)[None]
    grads["norm2_g"] = _unpack(tot1, offs1, 3)[None]
    g_lbl = _unpack(tot1, offs1, 4)
    n_lb = HA // N_CHIP
    grads["hgrn_lb_logits"] = lax.dynamic_slice(g_lbl, (0, 0, chip * n_lb), (2, 2, n_lb))
    grads["hgrn_norm_g"] = _unpack(tot1, offs1, 5)[None]
    grads["na_q_norm_g"] = _unpack(tot1, offs1, 6)[None]
    grads["na_k_norm_g"] = _unpack(tot1, offs1, 7)[None]
    grads["na_rel_bias"] = _unpack(tot1, offs1, 8)[None]
    g_cw = _unpack(tot1, offs1, 9)
    n_f = F // N_CHIP
    grads["ffn_conv_w"] = lax.dynamic_slice(g_cw, (0, chip * n_f), (3, n_f))[None]
    grads["ffn_conv_b"] = _unpack(tot1, offs1, 10)[None]

    g_c_ctx = _dsilu_rows(dact_sel, c_ctx[None, :], "grad_c_ctx")
    grads["c_ctx"] = g_c_ctx[0]

    rs_in = _rs_scatter(rs_in, g_c_ctx)
    big_names = ["ada_w", "w_in", "w_branch_a", "w_branch_b", "w_out", "ffn_w1", "ffn_w3", "ffn_w2"]
    small_names = [n for n in order if n not in big_names]
    delta, new_m, new_v = {}, {}, {}

    def update(nm, after=None):
        reduced = nm != "ada_w"
        d_, m_, v_, *g_ = _adamw(weights[nm][0], grads[nm][0], moms[nm][0][0], moms[nm][1][0], "adamw_" + nm,
                                 after, copy_g=reduced)
        delta[nm], new_m[nm], new_v[nm] = d_[None], m_[None], v_[None]
        if reduced:
            grads[nm] = g_[0][None]
        return d_

    last = update("ada_w")
    for nm, g in zip(["ffn_w2", "ffn_w1", "ffn_w3"], _rs_finish(rs_ffn, last)):
        grads[nm] = g[None]
        last = update(nm, last)
    for nm, g in zip(["w_out", "w_branch_a", "w_branch_b"], _rs_finish(rs_mix, last)):
        grads[nm] = g[None]
        last = update(nm, last)
    rs_in = _rs_join(rs_in, last)
    grads["w_in"] = _rs_finish(rs_in, last)[0][None]
    update("w_in")
    pw, offw = _pack([weights[n] for n in small_names])
    pg, _ = _pack([grads[n] for n in small_names])
    pm, _ = _pack([moms[n][0] for n in small_names])
    pv, _ = _pack([moms[n][1] for n in small_names])
    d_, m_, v_ = _adamw(pw, pg, pm, pv, "adamw_small")
    for i, nm in enumerate(small_names):
        delta[nm], new_m[nm], new_v[nm] = _unpack(d_, offw, i), _unpack(m_, offw, i), _unpack(v_, offw, i)

    return (loss, grad_x[None], *[grads[n] for n in order], *[delta[n] for n in order],
            *[new_m[n] for n in order], *[new_v[n] for n in order])


def _dsilu_rows(v, cv, name):
    D = v.shape[1]

    def body(v_ref, c_ref, o_ref):
        t = c_ref[...]
        s = _sigmoid(t)
        o_ref[...] = (((v_ref[0:1, :] + v_ref[1:2, :]) + v_ref[2:3, :]) + v_ref[3:4, :]) * (s * (1.0 + t * (1.0 - s)))

    return _pcall(body, name=name, out_shape=jax.ShapeDtypeStruct((1, D), F32),
                          compiler_params=_params())(v, cv)
```

```python
import functools

import numpy as np
import jax
import jax.numpy as jnp
from jax import lax
from jax.experimental import pallas as pl
from jax.experimental.pallas import tpu as pltpu

F32 = jnp.float32
BF16 = jnp.bfloat16
MESH = pl.DeviceIdType.MESH

HEAD = 128
GRID_W = 64
WIN_R = 8
WIN_C = 16
ROPE_THETA = 10000.0
EPS = 1e-6
N_MOD = 6
CHUNK = 16
HGRN_UNROLL = 4
ADAM_LR = 0.001
ADAM_B1 = 0.9
ADAM_B2 = 0.999
ADAM_EPS = 1e-08
ADAM_WD = 0.01
ADAM_STEP = 10
NEG = -1e30
VMEM_LIMIT = 56 * 1024 * 1024
N_DEV = 8
N_CHIP = 4
HI = lax.Precision.HIGHEST


def _pick(n, cands):
    for c in cands:
        if n % c == 0:
            return c
    return n


def _row_tile(rows, cols, target_bytes=1 << 20):
    want = max(16, target_bytes // (4 * cols))
    for t in (512, 256, 128, 64, 32, 16, 8):
        if t <= want and rows % t == 0:
            return t
    return rows


def _params(sem=None):
    return pltpu.CompilerParams(dimension_semantics=sem, vmem_limit_bytes=VMEM_LIMIT)


def _dot(a, b):
    return jnp.dot(a, b, preferred_element_type=F32)


def _dot_nt(a, b):
    return lax.dot_general(a, b, (((1,), (1,)), ((), ())), preferred_element_type=F32)


def _dot_tn(a, b):
    return lax.dot_general(a, b, (((0,), (0,)), ((), ())), preferred_element_type=F32)


def _sigmoid(x):
    return 1.0 / (1.0 + jnp.exp(-x))


def _col_tile(n):
    return n if n <= 1536 else _pick(n, (1024, 768, 512, 384, 256, 128))


def _mm_nn(x, w3, out_dtype, name):
    M, K = x.shape
    S, _, n = w3.shape
    tm = _pick(M, (768, 512, 256, 128, 64))
    tn = _col_tile(n)
    nb = n // tn

    def body(x_ref, w_ref, o_ref):
        o_ref[...] = _dot(x_ref[...].astype(BF16), w_ref[0]).astype(o_ref.dtype)

    return _pcall(
        body, name=name, grid=(M // tm, S * nb),
        in_specs=[pl.BlockSpec((tm, K), lambda i, j: (i, 0)),
                  pl.BlockSpec((1, K, tn), lambda i, j: (j // nb, 0, j % nb))],
        out_specs=pl.BlockSpec((tm, tn), lambda i, j: (i, j)),
        out_shape=jax.ShapeDtypeStruct((M, S * n), out_dtype),
        compiler_params=_params(("parallel", "parallel")),
    )(x, w3)


def _mm_nt(dy, w3, out_dtype, name):
    M = dy.shape[0]
    S, K, n = w3.shape
    tm = _pick(M, (768, 512, 256, 128, 64))
    tk = K if K <= 2048 else _pick(K, (1408, 1024, 512, 256, 128))
    tc = n if n <= 2048 else _col_tile(n)
    nb = n // tc
    nsteps = S * nb

    def body(dy_ref, w_ref, o_ref, acc_ref):
        s = pl.program_id(2)

        @pl.when(s == 0)
        def _():
            acc_ref[...] = jnp.zeros_like(acc_ref)

        acc_ref[...] += _dot_nt(dy_ref[...].astype(BF16), w_ref[0])

        @pl.when(s == nsteps - 1)
        def _():
            o_ref[...] = acc_ref[...].astype(o_ref.dtype)

    return _pcall(
        body, name=name, grid=(M // tm, K // tk, nsteps),
        in_specs=[pl.BlockSpec((tm, tc), lambda i, k, s: (i, s)),
                  pl.BlockSpec((1, tk, tc), lambda i, k, s: (s // nb, k, s % nb))],
        out_specs=pl.BlockSpec((tm, tk), lambda i, k, s: (i, k)),
        out_shape=jax.ShapeDtypeStruct((M, K), out_dtype),
        scratch_shapes=[pltpu.VMEM((tm, tk), F32)],
        compiler_params=_params(("parallel", "parallel", "arbitrary")),
    )(dy, w3)


def _mm_tn(x, dy, S, name):
    M, K = x.shape
    n = dy.shape[1] // S
    tk = _pick(K, (512, 256, 128))
    tn = _col_tile(n)
    nb = n // tn

    def body(x_ref, dy_ref, o_ref):
        o_ref[0] = _dot_tn(x_ref[...].astype(BF16), dy_ref[...].astype(BF16)).astype(BF16)

    return _pcall(
        body, name=name, grid=(S * nb, K // tk),
        in_specs=[pl.BlockSpec((M, tk), lambda j, k: (0, k)),
                  pl.BlockSpec((M, tn), lambda j, k: (0, j))],
        out_specs=pl.BlockSpec((1, tk, tn), lambda j, k: (j // nb, k, j % nb)),
        out_shape=jax.ShapeDtypeStruct((S, K, n), BF16),
        compiler_params=_params(("parallel", "parallel")),
    )(x, dy)


def _chip_index():
    return (2 * lax.axis_index("x") + lax.axis_index("y")).astype(jnp.int32).reshape(1)


def _cast_bf16_slot(w, name):
    R, C = w.shape
    tr = _row_tile(R, C, 2 << 20)

    def body(j_ref, w_ref, o_ref):
        o_ref[0] = w_ref[...].astype(BF16)

    return _pcall(
        body, name=name,
        grid_spec=pltpu.PrefetchScalarGridSpec(
            num_scalar_prefetch=1, grid=(R // tr,),
            in_specs=[pl.BlockSpec((tr, C), lambda i, j_ref: (i, 0))],
            out_specs=pl.BlockSpec((1, tr, C), lambda i, j_ref: (j_ref[0], i, 0))),
        out_shape=jax.ShapeDtypeStruct((N_CHIP, R, C), BF16),
        compiler_params=_params(("parallel",)),
    )(_chip_index(), w)


def _pos():
    return lax.axis_index("x"), lax.axis_index("y"), lax.axis_index("c")


def _other_chips(x, y):
    return [(x, 1 - y), (1 - x, y), (1 - x, 1 - y)]


def _allgather8(v, name):
    R, C = v.shape

    def body(x_ref, out_ref, send_sems, recv_sems, local_sem):
        x, y, c = _pos()
        me, sibling = (x, y, c), (x, y, 1 - c)
        chips = _other_chips(x, y)

        def slot(px, py, pc):
            return out_ref.at[4 * px + 2 * py + pc]

        def copy(k, block, to, src=None):
            return pltpu.make_async_remote_copy(
                src_ref=slot(*block) if src is None else src, dst_ref=slot(*block),
                send_sem=send_sems.at[k], recv_sem=recv_sems.at[k], device_id=to, device_id_type=MESH)

        mine = pltpu.make_async_copy(x_ref, slot(*me), local_sem)
        mine.start()
        first = [copy(0, me, sibling, src=x_ref)]
        first += [copy(1 + j, me, (*chip, c), src=x_ref) for j, chip in enumerate(chips)]
        for cp in first:
            cp.start()
        passed = [copy(4 + j, (*chip, c), sibling) for j, chip in enumerate(chips)]
        for j, chip in enumerate(chips):
            copy(1 + j, (*chip, c), me).wait_recv()
            passed[j].start()
        copy(0, sibling, me).wait_recv()
        for j, chip in enumerate(chips):
            copy(4 + j, (*chip, 1 - c), me).wait_recv()
        for cp in first + passed:
            cp.wait_send()
        mine.wait()

    return _pcall(
        body, name=name,
        out_shape=jax.ShapeDtypeStruct((N_DEV, R, C), v.dtype),
        in_specs=[pl.BlockSpec(memory_space=pltpu.VMEM)],
        out_specs=pl.BlockSpec(memory_space=pltpu.VMEM),
        scratch_shapes=[pltpu.SemaphoreType.DMA((7,)), pltpu.SemaphoreType.DMA((7,)), pltpu.SemaphoreType.DMA],
        compiler_params=pltpu.CompilerParams(vmem_limit_bytes=VMEM_LIMIT),
    )(v)


_HBM = pl.BlockSpec(memory_space=pltpu.HBM)
_SEM = pl.BlockSpec(memory_space=pltpu.SEMAPHORE)
_ANY = pl.BlockSpec(memory_space=pl.ANY)
_EFFECT = pltpu.SideEffectType.DATAFLOW_SIDE_EFFECTING
_PENDING = []


def _pcall(body, **kw):
    def run(*operands):
        if not _PENDING or "in_specs" not in kw:
            return pl.pallas_call(body, **kw)(*operands)
        deps = list(_PENDING)
        n = len(operands)

        def tied(*refs):
            return body(*refs[:n], *refs[n + len(deps):])

        return pl.pallas_call(tied, **{**kw, "in_specs": list(kw["in_specs"]) + [_ANY] * len(deps)})(*operands, *deps)
    return run


def _copies(plan, refs, send_sems, recv_sems):
    return [pltpu.make_async_remote_copy(src_ref=src, dst_ref=dst, send_sem=send_sems.at[k], recv_sem=recv_sems.at[k],
                                         device_id=dev, device_id_type=MESH)
            for k, (src, dst, dev) in enumerate(plan(refs))]


def _xfer_start(name, bufs, plan, n_copies, after=None):
    n = len(bufs)
    deps = list(_PENDING) + ([after] if after is not None else [])
    nd = len(deps)

    def body(*refs):
        for cp in _copies(plan, refs[:n], refs[n + nd], refs[n + nd + 1]):
            cp.start()
        refs[-1][...] = jnp.zeros_like(refs[-1])

    outs = pl.pallas_call(
        body, name=name,
        out_shape=(pltpu.SemaphoreType.DMA((n_copies,)), pltpu.SemaphoreType.DMA((n_copies,)),
                   *[pltpu.HBM(b.shape, b.dtype) for b in bufs], jax.ShapeDtypeStruct((8, 128), F32)),
        in_specs=[_HBM] * n + [_ANY] * nd,
        out_specs=(_SEM, _SEM, *[_HBM] * n, pl.BlockSpec(memory_space=pltpu.VMEM)),
        input_output_aliases={t: 2 + t for t in range(n)},
        compiler_params=pltpu.CompilerParams(has_side_effects=_EFFECT),
    )(*[pltpu.with_memory_space_constraint(b, pltpu.HBM) for b in bufs], *deps)
    _PENDING[:] = [outs[-1]]
    return (outs[0], outs[1]), list(outs[2:2 + n])


def _xfer_wait(name, sems, bufs, plan, after):
    n = len(bufs)

    def body(*refs):
        cps = _copies(plan, refs[:n], refs[n], refs[n + 1])
        for cp in cps:
            cp.wait_send()
        for cp in cps:
            cp.wait_recv()

    outs = pl.pallas_call(
        body, name=name,
        out_shape=tuple(pltpu.HBM(b.shape, b.dtype) for b in bufs),
        in_specs=[_HBM] * n + [_SEM, _SEM, _ANY],
        out_specs=tuple([_HBM] * n),
        input_output_aliases={t: t for t in range(n)},
        compiler_params=pltpu.CompilerParams(has_side_effects=_EFFECT),
    )(*bufs, sems[0], sems[1], after)
    return list(outs)


def _half(ref_rows, hc):
    h = ref_rows // 2
    return pl.ds(hc * h, h)


def _plan_gather_ici(bufs):
    x, y, c = _pos()
    j = 2 * x + y
    return [(b.at[j, _half(b.shape[1], c)], b.at[j, _half(b.shape[1], c)], (*chip, c))
            for b in bufs for chip in _other_chips(x, y)]


def _plan_gather_d2d(bufs):
    x, y, c = _pos()
    out = []
    for b in bufs:
        for chip in _other_chips(x, y):
            blk = b.at[2 * chip[0] + chip[1], _half(b.shape[1], c)]
            out.append((blk, blk, (x, y, 1 - c)))
    return out


def _plan_pair_swap(n):
    def plan(bufs):
        x, y, c = _pos()
        return [(g.at[:, _half(g.shape[1], 1 - c)], land, (x, y, 1 - c)) for g, land in zip(bufs[:n], bufs[n:])]
    return plan


def _plan_chip_scatter(n):
    def plan(bufs):
        x, y, c = _pos()
        return [(p.at[2 * chip[0] + chip[1]], land.at[k], (*chip, c))
                for p, land in zip(bufs[:n], bufs[n:]) for k, chip in enumerate(_other_chips(x, y))]
    return plan


def _plan_pair_join(bufs):
    x, y, c = _pos()
    return [(b.at[_half(b.shape[0], c)], b.at[_half(b.shape[0], c)], (x, y, 1 - c)) for b in bufs]


def _empty_hbm(shape, dtype):
    return pltpu.with_memory_space_constraint(lax.empty(shape, dtype), pltpu.HBM)


def _gather_start(tag, bufs, after=None):
    sems, bufs = _xfer_start(f"gather_ici_start_{tag}", bufs, _plan_gather_ici, 3 * len(bufs), after)
    return dict(tag=tag, sems=sems, bufs=bufs)


def _gather_mid(st, after):
    tag = st["tag"]
    bufs = _xfer_wait(f"gather_ici_wait_{tag}", st["sems"], st["bufs"], _plan_gather_ici, after)
    sems, bufs = _xfer_start(f"gather_d2d_start_{tag}", bufs, _plan_gather_d2d, 3 * len(bufs))
    return dict(tag=tag, sems=sems, bufs=bufs)


def _gather_finish(st, after):
    return _xfer_wait(f"gather_d2d_wait_{st['tag']}", st["sems"], st["bufs"], _plan_gather_d2d, after)


def _pair_add(g, r, name):
    S, R, C = g.shape
    h = R // 2
    tr = _row_tile(h, C)
    nb = h // tr

    def body(c_ref, g_ref, r_ref, o_ref):
        o_ref[...] = (g_ref[...].astype(F32) + r_ref[...].astype(F32)).astype(BF16)

    return _pcall(
        body, name=name,
        grid_spec=pltpu.PrefetchScalarGridSpec(
            num_scalar_prefetch=1, grid=(S, nb),
            in_specs=[pl.BlockSpec((1, tr, C), lambda s, i, c_ref: (s, c_ref[0] * nb + i, 0)),
                      pl.BlockSpec((1, tr, C), lambda s, i, c_ref: (s, i, 0))],
            out_specs=pl.BlockSpec((1, tr, C), lambda s, i, c_ref: (s, i, 0))),
        out_shape=jax.ShapeDtypeStruct((S, h, C), BF16),
        compiler_params=_params(("parallel", "parallel")),
    )(lax.axis_index("c").astype(jnp.int32).reshape(1), g, r)


def _chip_sum(p, rb, name):
    S, h, C = p.shape
    tr = _row_tile(h, C)
    nb = h // tr
    jc = jnp.concatenate([_chip_index(), lax.axis_index("c").astype(jnp.int32).reshape(1)])

    def body(jc_ref, p_ref, r_ref, o_ref):
        o_ref[...] = ((p_ref[0].astype(F32) + r_ref[0].astype(F32)) + r_ref[1].astype(F32)) + r_ref[2].astype(F32)

    return _pcall(
        body, name=name,
        grid_spec=pltpu.PrefetchScalarGridSpec(
            num_scalar_prefetch=1, grid=(nb,),
            in_specs=[pl.BlockSpec((1, tr, C), lambda i, jc_ref: (jc_ref[0], i, 0)),
                      pl.BlockSpec((3, tr, C), lambda i, jc_ref: (0, i, 0))],
            out_specs=pl.BlockSpec((tr, C), lambda i, jc_ref: (jc_ref[1] * nb + i, 0))),
        out_shape=jax.ShapeDtypeStruct((2 * h, C), F32),
        compiler_params=_params(("parallel",)),
    )(jc, p, rb)


def _rs_start(tag, gs):
    n = len(gs)
    lands = [_empty_hbm((g.shape[0], g.shape[1] // 2, g.shape[2]), g.dtype) for g in gs]
    sems, bufs = _xfer_start(f"rs_swap_start_{tag}", list(gs) + lands, _plan_pair_swap(n), n)
    return dict(tag=tag, n=n, sems=sems, bufs=bufs)


def _rs_scatter(st, after):
    tag, n = st["tag"], st["n"]
    bufs = _xfer_wait(f"rs_swap_wait_{tag}", st["sems"], st["bufs"], _plan_pair_swap(n), after)
    ps = [_pair_add(g, r, f"rs_pair_add_{tag}{t}") for t, (g, r) in enumerate(zip(bufs[:n], bufs[n:]))]
    lands = [_empty_hbm((3,) + p.shape[1:], p.dtype) for p in ps]
    sems, bufs = _xfer_start(f"rs_scatter_start_{tag}", ps + lands, _plan_chip_scatter(n), 3 * n)
    return dict(tag=tag, n=n, sems=sems, bufs=bufs)


def _rs_join(st, after):
    tag, n = st["tag"], st["n"]
    bufs = _xfer_wait(f"rs_scatter_wait_{tag}", st["sems"], st["bufs"], _plan_chip_scatter(n), after)
    fs = [_chip_sum(p, rb, f"rs_chip_sum_{tag}{t}") for t, (p, rb) in enumerate(zip(bufs[:n], bufs[n:]))]
    sems, bufs = _xfer_start(f"rs_join_start_{tag}", fs, _plan_pair_join, n)
    return dict(tag=tag, n=n, sems=sems, bufs=bufs)


def _rs_finish(st, after):
    return _xfer_wait(f"rs_join_wait_{st['tag']}", st["sems"], st["bufs"], _plan_pair_join, after)


def _sum8(g, name):
    _, R, C = g.shape

    def body(g_ref, o_ref):
        acc = g_ref[0]
        for d in range(1, N_DEV):
            acc = acc + g_ref[d]
        o_ref[...] = acc

    return _pcall(body, name=name, out_shape=jax.ShapeDtypeStruct((R, C), F32),
                          compiler_params=_params())(g)


def _ada_fwd(cs, w, b):
    D, n = w.shape
    tn = _pick(n, (512, 384, 256, 128))

    def body(c_ref, w_ref, b_ref, o_ref):
        cv = c_ref[...]
        a = (cv * _sigmoid(cv)).astype(BF16)
        o_ref[...] = _dot(a, w_ref[...].astype(BF16)) + b_ref[...]

    return _pcall(
        body, name="ada_fwd", grid=(n // tn,),
        in_specs=[pl.BlockSpec((16, D), lambda j: (0, 0)), pl.BlockSpec((D, tn), lambda j: (0, j)),
                  pl.BlockSpec((1, tn), lambda j: (0, j))],
        out_specs=pl.BlockSpec((16, tn), lambda j: (0, j)),
        out_shape=jax.ShapeDtypeStruct((16, n), F32),
        compiler_params=_params(("parallel",)),
    )(cs, w, b)


def _ada_bwd(cs, w, dmod):
    D, n = w.shape
    tn = _pick(n, (512, 384, 256, 128))

    def body(c_ref, w_ref, d_ref, gw_ref, da_ref):
        j = pl.program_id(0)
        cv = c_ref[...]
        a = cv * _sigmoid(cv)
        d = d_ref[...]
        gw_ref[...] = lax.dot_general(a, d, (((0,), (0,)), ((), ())), precision=HI, preferred_element_type=F32)

        @pl.when(j == 0)
        def _():
            da_ref[...] = jnp.zeros_like(da_ref)

        da_ref[...] += _dot_nt(d.astype(BF16), w_ref[...].astype(BF16))

    return _pcall(
        body, name="ada_bwd", grid=(n // tn,),
        in_specs=[pl.BlockSpec((16, D), lambda j: (0, 0)), pl.BlockSpec((D, tn), lambda j: (0, j)),
                  pl.BlockSpec((16, tn), lambda j: (0, j))],
        out_specs=[pl.BlockSpec((D, tn), lambda j: (0, j)), pl.BlockSpec((16, D), lambda j: (0, 0))],
        out_shape=[jax.ShapeDtypeStruct((D, n), F32), jax.ShapeDtypeStruct((16, D), F32)],
        compiler_params=_params(("arbitrary",)),
    )(cs, w, dmod)


def _rms1_fwd(xall, gain, shift2, scale2, n_ctx):
    T, D = xall.shape
    tb = _pick(n_ctx, (256, 128, 64, 32, 16))
    nctx = n_ctx // tb

    def body(x_ref, g_ref, sh_ref, sc_ref, o_ref):
        i = pl.program_id(0)
        xv = x_ref[...]
        r = lax.rsqrt(jnp.mean(xv * xv, axis=-1, keepdims=True) + EPS)
        nrm = xv * r * g_ref[...]
        lat = i >= nctx
        sh = jnp.where(lat, sh_ref[1:2, :], sh_ref[0:1, :])
        sc = jnp.where(lat, sc_ref[1:2, :], sc_ref[0:1, :])
        o_ref[...] = (nrm * (1.0 + sc) + sh).astype(BF16)

    vec = lambda r: pl.BlockSpec((r, D), lambda i: (0, 0))
    return _pcall(
        body, name="rms1_fwd", grid=(T // tb,),
        in_specs=[pl.BlockSpec((tb, D), lambda i: (i, 0)), vec(1), vec(2), vec(2)],
        out_specs=pl.BlockSpec((tb, D), lambda i: (i, 0)),
        out_shape=jax.ShapeDtypeStruct((T, D), BF16),
        compiler_params=_params(("parallel",)),
    )(xall, gain, shift2, scale2)


def _rms1_bwd(xall, dh, dxmid, gain, scale2, n_ctx):
    T, D = xall.shape
    L = T - n_ctx
    tb = _pick(n_ctx, (256, 128, 64, 32, 16))
    nctx = n_ctx // tb

    def body(x_ref, dh_ref, dxm_ref, g_ref, sc_ref, dx_ref, cs_ref):
        i = pl.program_id(0)
        lat = i >= nctx
        xv = x_ref[...]
        r = lax.rsqrt(jnp.mean(xv * xv, axis=-1, keepdims=True) + EPS)
        xh = xv * r
        g = g_ref[...]
        nrm = xh * g
        sc = jnp.where(lat, sc_ref[1:2, :], sc_ref[0:1, :])
        dhv = dh_ref[...]
        dn = dhv * (1.0 + sc)
        dxh = dn * g
        dxv = r * (dxh - xh * jnp.mean(dxh * xh, axis=-1, keepdims=True))
        s_sh = jnp.sum(dhv, axis=0, keepdims=True)
        s_sc = jnp.sum(dhv * nrm, axis=0, keepdims=True)
        s_g = jnp.sum(dn * xh, axis=0, keepdims=True)
        zero = jnp.zeros_like(s_sh)
        rows = lax.broadcasted_iota(jnp.int32, (8, D), 0)
        upd = jnp.where(rows == 0, jnp.where(lat, zero, s_sh),
              jnp.where(rows == 1, jnp.where(lat, zero, s_sc),
              jnp.where(rows == 2, jnp.where(lat, s_sh, zero),
              jnp.where(rows == 3, jnp.where(lat, s_sc, zero),
              jnp.where(rows == 4, s_g, 0.0)))))

        @pl.when(i == 0)
        def _():
            cs_ref[...] = jnp.zeros_like(cs_ref)

        cs_ref[...] += upd

        @pl.when(lat)
        def _():
            dx_ref[...] = dxv + dxm_ref[...]

    lat_blk = lambda i: (jnp.maximum(i - nctx, 0), 0)
    vec = lambda r: pl.BlockSpec((r, D), lambda i: (0, 0))
    return _pcall(
        body, name="rms1_bwd", grid=(T // tb,),
        in_specs=[pl.BlockSpec((tb, D), lambda i: (i, 0)), pl.BlockSpec((tb, D), lambda i: (i, 0)),
                  pl.BlockSpec((tb, D), lat_blk), vec(1), vec(2)],
        out_specs=[pl.BlockSpec((tb, D), lat_blk), vec(8)],
        out_shape=[jax.ShapeDtypeStruct((L, D), F32), jax.ShapeDtypeStruct((8, D), F32)],
        compiler_params=_params(("arbitrary",)),
    )(xall, dh, dxmid, gain, scale2)


def _resid_rms2_fwd(x, mo, vecs):
    L, D = x.shape
    tb = _pick(L, (256, 128, 64))

    def body(x_ref, mo_ref, v_ref, xm_ref, h_ref):
        xm = x_ref[...] + v_ref[0:1, :] * mo_ref[...]
        xm_ref[...] = xm
        r = lax.rsqrt(jnp.mean(xm * xm, axis=-1, keepdims=True) + EPS)
        h_ref[...] = (xm * r * v_ref[1:2, :] * (1.0 + v_ref[3:4, :]) + v_ref[2:3, :]).astype(BF16)

    blk = pl.BlockSpec((tb, D), lambda i: (i, 0))
    return _pcall(
        body, name="resid_rms2_fwd", grid=(L // tb,),
        in_specs=[blk, blk, pl.BlockSpec((8, D), lambda i: (0, 0))],
        out_specs=[blk, blk],
        out_shape=[jax.ShapeDtypeStruct((L, D), F32), jax.ShapeDtypeStruct((L, D), BF16)],
        compiler_params=_params(("parallel",)),
    )(x, mo, vecs)


def _resid_rms2_bwd(xmid, dh_a, dh_b, dy, mo, vecs):
    L, D = xmid.shape
    tb = _pick(L, (256, 128, 64))

    def body(xm_ref, da_ref, db_ref, dy_ref, mo_ref, v_ref, dxm_ref, dmo_ref, cs_ref):
        i = pl.program_id(0)
        xm = xm_ref[...]
        r = lax.rsqrt(jnp.mean(xm * xm, axis=-1, keepdims=True) + EPS)
        xh = xm * r
        g = v_ref[1:2, :]
        nrm = xh * g
        dhv = da_ref[...] + db_ref[...]
        dn = dhv * (1.0 + v_ref[3:4, :])
        dxh = dn * g
        dxm = dy_ref[...] + r * (dxh - xh * jnp.mean(dxh * xh, axis=-1, keepdims=True))
        dxm_ref[...] = dxm
        dmo_ref[...] = (dxm * v_ref[0:1, :]).astype(BF16)
        s0 = jnp.sum(dhv, axis=0, keepdims=True)
        s1 = jnp.sum(dhv * nrm, axis=0, keepdims=True)
        s2 = jnp.sum(dn * xh, axis=0, keepdims=True)
        s3 = jnp.sum(dxm * mo_ref[...], axis=0, keepdims=True)
        rows = lax.broadcasted_iota(jnp.int32, (8, D), 0)
        upd = jnp.where(rows == 0, s0, jnp.where(rows == 1, s1, jnp.where(rows == 2, s2,
              jnp.where(rows == 3, s3, 0.0))))

        @pl.when(i == 0)
        def _():
            cs_ref[...] = jnp.zeros_like(cs_ref)

        cs_ref[...] += upd

    blk = pl.BlockSpec((tb, D), lambda i: (i, 0))
    vec = pl.BlockSpec((8, D), lambda i: (0, 0))
    return _pcall(
        body, name="resid_rms2_bwd", grid=(L // tb,),
        in_specs=[blk, blk, blk, blk, blk, vec],
        out_specs=[blk, blk, vec],
        out_shape=[jax.ShapeDtypeStruct((L, D), F32), jax.ShapeDtypeStruct((L, D), BF16),
                   jax.ShapeDtypeStruct((8, D), F32)],
        compiler_params=_params(("arbitrary",)),
    )(xmid, dh_a, dh_b, dy, mo, vecs)


def _loss_head(xmid, f, g2, target):
    L, D = xmid.shape
    tb = _pick(L, (256, 128, 64))

    def body(xm_ref, f_ref, g_ref, t_ref, dy_ref, df_ref, s_ref):
        i = pl.program_id(0)
        fv = f_ref[...]
        g = g_ref[...]
        err = xm_ref[...] + g * fv - t_ref[...]
        dy = err * (1.0 / D)
        dy_ref[...] = dy
        df_ref[...] = (dy * g).astype(BF16)
        s0 = jnp.sum(dy * fv, axis=0, keepdims=True)
        part = 0.5 * jnp.sum(jnp.mean(err * err, axis=-1, keepdims=True), axis=0, keepdims=True)
        rows = lax.broadcasted_iota(jnp.int32, (8, D), 0)
        upd = jnp.where(rows == 0, s0, jnp.where(rows == 1, part, 0.0))

        @pl.when(i == 0)
        def _():
            s_ref[...] = jnp.zeros_like(s_ref)

        s_ref[...] += upd

    blk = pl.BlockSpec((tb, D), lambda i: (i, 0))
    return _pcall(
        body, name="loss_head", grid=(L // tb,),
        in_specs=[blk, blk, pl.BlockSpec((1, D), lambda i: (0, 0)), blk],
        out_specs=[blk, blk, pl.BlockSpec((8, D), lambda i: (0, 0))],
        out_shape=[jax.ShapeDtypeStruct((L, D), F32), jax.ShapeDtypeStruct((L, D), BF16),
                   jax.ShapeDtypeStruct((8, D), F32)],
        compiler_params=_params(("arbitrary",)),
    )(xmid, f, g2, target)


def _gate_cols(D, off):
    tc = _pick(np.gcd(D, off), (512, 256, 128))
    return tc, off // tc


def _merge_fwd(za, zb, p, n_ctx, off_a, off_b):
    L, D = za.shape
    tb = _pick(n_ctx, (256, 128, 64, 32, 16))
    nctx = n_ctx // tb
    tc, oa = _gate_cols(D, off_a)
    _, ob = _gate_cols(D, off_b)
    if off_b % tc:
        raise ValueError("gate column offsets must share a column tile")
    ob = off_b // tc

    def body(za_ref, zb_ref, ga_ref, gb_ref, z_ref):
        z_ref[...] = (_sigmoid(ga_ref[...]) * za_ref[...].astype(F32)
                      + _sigmoid(gb_ref[...]) * zb_ref[...].astype(F32)).astype(BF16)

    blk = pl.BlockSpec((tb, tc), lambda i, j: (i, j))
    return _pcall(
        body, name="merge_fwd", grid=(L // tb, D // tc),
        in_specs=[blk, blk, pl.BlockSpec((tb, tc), lambda i, j: (i + nctx, oa + j)),
                  pl.BlockSpec((tb, tc), lambda i, j: (i + nctx, ob + j))],
        out_specs=blk,
        out_shape=jax.ShapeDtypeStruct((L, D), BF16),
        compiler_params=_params(("parallel", "parallel")),
    )(za, zb, p, p)


def _merge_bwd(dz, za, zb, p, n_ctx, off_a, off_b):
    L, D = za.shape
    T = L + n_ctx
    tb = _pick(n_ctx, (256, 128, 64, 32, 16))
    nctx = n_ctx // tb
    tc = _gate_cols(D, off_a)[0]
    oa, ob = off_a // tc, off_b // tc

    def body(dz_ref, za_ref, zb_ref, ga_ref, gb_ref, dza_ref, dzb_ref, dga_ref, dgb_ref):
        i = pl.program_id(1)

        @pl.when(i < nctx)
        def _():
            dga_ref[...] = jnp.zeros_like(dga_ref)
            dgb_ref[...] = jnp.zeros_like(dgb_ref)

        @pl.when(i >= nctx)
        def _():
            dzv = dz_ref[...].astype(F32)
            sa = _sigmoid(ga_ref[...])
            sb = _sigmoid(gb_ref[...])
            dza_ref[...] = (dzv * sa).astype(BF16)
            dzb_ref[...] = (dzv * sb).astype(BF16)
            dga_ref[...] = (dzv * za_ref[...].astype(F32) * sa * (1.0 - sa)).astype(BF16)
            dgb_ref[...] = (dzv * zb_ref[...].astype(F32) * sb * (1.0 - sb)).astype(BF16)

    lat = pl.BlockSpec((tb, tc), lambda j, i: (jnp.maximum(i - nctx, 0), j))
    allr = pl.BlockSpec((tb, tc), lambda j, i: (i, j))
    return _pcall(
        body, name="merge_bwd", grid=(D // tc, T // tb),
        in_specs=[lat, lat, lat, pl.BlockSpec((tb, tc), lambda j, i: (i, oa + j)),
                  pl.BlockSpec((tb, tc), lambda j, i: (i, ob + j))],
        out_specs=[lat, lat, allr, allr],
        out_shape=[jax.ShapeDtypeStruct((L, D), BF16), jax.ShapeDtypeStruct((L, D), BF16),
                   jax.ShapeDtypeStruct((T, D), BF16), jax.ShapeDtypeStruct((T, D), BF16)],
        compiler_params=_params(("arbitrary", "arbitrary")),
    )(dz, za, zb, p, p)


def _shift_down(u, rows):
    return jnp.where(rows == 0, 0.0, pltpu.roll(u, 1, 0))


def _shift_up(u, rows):
    n = u.shape[0]
    return jnp.where(rows == n - 1, 0.0, pltpu.roll(u, n - 1, 0))


def _convgate_fwd(u1, u3, cw, cb):
    L, F = u1.shape
    tc = _pick(F, (256, 128))

    def body(u1_ref, u3_ref, w_ref, b_ref, a_ref):
        u = u1_ref[...].astype(F32)
        rows = lax.broadcasted_iota(jnp.int32, u.shape, 0)
        cv = _shift_down(u, rows) * w_ref[0:1, :] + u * w_ref[1:2, :] + _shift_up(u, rows) * w_ref[2:3, :] + b_ref[...]
        a_ref[...] = (cv * _sigmoid(cv) * u3_ref[...].astype(F32)).astype(BF16)

    blk = pl.BlockSpec((L, tc), lambda j: (0, j))
    return _pcall(
        body, name="convgate_fwd", grid=(F // tc,),
        in_specs=[blk, blk, pl.BlockSpec((8, tc), lambda j: (0, j)), pl.BlockSpec((1, tc), lambda j: (0, j))],
        out_specs=blk,
        out_shape=jax.ShapeDtypeStruct((L, F), BF16),
        compiler_params=_params(("parallel",)),
    )(u1, u3, cw, cb)


def _convgate_bwd(u1, u3, da, cw, cb):
    L, F = u1.shape
    tc = _pick(F, (256, 128))

    def body(u1_ref, u3_ref, da_ref, w_ref, b_ref, du1_ref, du3_ref, s_ref):
        u = u1_ref[...].astype(F32)
        rows = lax.broadcasted_iota(jnp.int32, u.shape, 0)
        um, up = _shift_down(u, rows), _shift_up(u, rows)
        w0, w1, w2 = w_ref[0:1, :], w_ref[1:2, :], w_ref[2:3, :]
        cv = um * w0 + u * w1 + up * w2 + b_ref[...]
        s = _sigmoid(cv)
        dav = da_ref[...].astype(F32)
        du3_ref[...] = (dav * cv * s).astype(BF16)
        dcv = dav * u3_ref[...].astype(F32) * (s * (1.0 + cv * (1.0 - s)))
        du1_ref[...] = (_shift_up(dcv, rows) * w0 + dcv * w1 + _shift_down(dcv, rows) * w2).astype(BF16)
        r8 = lax.broadcasted_iota(jnp.int32, (8, tc), 0)
        s0 = jnp.sum(dcv * um, axis=0, keepdims=True)
        s1 = jnp.sum(dcv * u, axis=0, keepdims=True)
        s2 = jnp.sum(dcv * up, axis=0, keepdims=True)
        s3 = jnp.sum(dcv, axis=0, keepdims=True)
        s_ref[...] = jnp.where(r8 == 0, s0, jnp.where(r8 == 1, s1, jnp.where(r8 == 2, s2,
                     jnp.where(r8 == 3, s3, 0.0))))

    blk = pl.BlockSpec((L, tc), lambda j: (0, j))
    v8 = pl.BlockSpec((8, tc), lambda j: (0, j))
    return _pcall(
        body, name="convgate_bwd", grid=(F // tc,),
        in_specs=[blk, blk, blk, v8, pl.BlockSpec((1, tc), lambda j: (0, j))],
        out_specs=[blk, blk, v8],
        out_shape=[jax.ShapeDtypeStruct((L, F), BF16), jax.ShapeDtypeStruct((L, F), BF16),
                   jax.ShapeDtypeStruct((8, F), F32)],
        compiler_params=_params(("parallel",)),
    )(u1, u3, da, cw, cb)


def _lower_bound(lbl_ref, d):
    l0, l1 = lbl_ref[d, 0:1, :], lbl_ref[d, 1:2, :]
    m = jnp.maximum(l0, l1)
    e0, e1 = jnp.exp(l0 - m), jnp.exp(l1 - m)
    return e0 / (e0 + e1)


def _chunk_cumsum(x, rev):
    n = x.shape[0]
    r = lax.broadcasted_iota(jnp.int32, x.shape, 0) % CHUNK
    k = 1
    while k < CHUNK:
        if rev:
            x = x + jnp.where(r < CHUNK - k, pltpu.roll(x, n - k, 0), 0.0)
        else:
            x = x + jnp.where(r >= k, pltpu.roll(x, k, 0), 0.0)
        k *= 2
    return x


def _gate_terms(z, lb):
    sg = _sigmoid(z)
    f = lb + (1.0 - lb) * sg
    return sg, f


def _decay_terms(z, lb, rev):
    _, f = _gate_terms(z, lb)
    g = jnp.log(f)
    return 1.0 - f, _chunk_cumsum(g, rev), _chunk_cumsum(g, not rev) - g


def _chunk_total(c, rev):
    return c[0:1, :] if rev else c[CHUNK - 1:CHUNK, :]


def _pair_decay(c, s, rev):
    t = lax.broadcasted_iota(jnp.int32, (CHUNK, 1), 0)
    later = (t <= s) if rev else (t >= s)
    return jnp.where(later, jnp.exp(c - c[s:s + 1, :]), 0.0)


def _scan_chunk(i, n_ctx_chunks, n_chunks, rev):
    if not rev:
        return i
    return jnp.where(i < n_ctx_chunks, n_ctx_chunks - 1 - i, n_chunks + n_ctx_chunks - 1 - i)


def _rows(ci):
    return pl.ds(pl.multiple_of(ci * CHUNK, CHUNK), CHUNK)


def _hgrn_cols(HA):
    return HA // HEAD


def _hgrn_fwd(p, lbl, ng, n_ctx, HA):
    T = p.shape[0]
    L = T - n_ctx
    nh = _hgrn_cols(HA)
    nc, ncc = T // CHUNK, n_ctx // CHUNK

    def body(q_ref, zf_ref, zb_ref, v_ref, og_ref, lbl_ref, ng_ref, ya_ref, o_ref, st_ref,
             c_scr, k_scr, qe_scr, ke_scr, o_scr):
        dirs = ((0, False, zf_ref), (1, True, zb_ref))
        for d, rev, z_ref in dirs:
            k, c, rest = _decay_terms(z_ref[...], _lower_bound(lbl_ref, d), rev)
            c_scr[d] = c
            k_scr[d] = k
            qe_scr[d] = (q_ref[...] * jnp.exp(c)).astype(BF16)
            ke_scr[d] = (k * jnp.exp(rest)).astype(BF16)

        def step(i2, states):
            states = list(states)
            for u in range(HGRN_UNROLL):
                for d, rev, _ in dirs:
                    St = states[d]
                    ci = _scan_chunk(HGRN_UNROLL * i2 + u, ncc, nc, rev)
                    rows = _rows(ci)
                    q, v, c, k = q_ref[rows, :], v_ref[rows, :], c_scr[d, rows, :], k_scr[d, rows, :]
                    st_ref[0, d, ci] = St.astype(BF16)
                    o = jnp.zeros((CHUNK, HEAD), F32)
                    for s in range(CHUNK):
                        E = _pair_decay(c, s, rev)
                        a = jnp.sum(q * E * k[s:s + 1, :], axis=1, keepdims=True)
                        o = o + a * v[s:s + 1, :]
                    o_scr[d, rows, :] = o + _dot_nt(qe_scr[d, rows, :], St.astype(BF16))
                    states[d] = St * jnp.exp(_chunk_total(c, rev)) + _dot_tn(v.astype(BF16), ke_scr[d, rows, :])
            return tuple(states)

        if nc % HGRN_UNROLL:
            raise ValueError("the number of chunks must be a multiple of HGRN_UNROLL")
        zero = jnp.zeros((HEAD, HEAD), F32)
        lax.fori_loop(0, nc // HGRN_UNROLL, step, (zero, zero))

        o = o_scr[0, pl.ds(n_ctx, L), :] + o_scr[1, pl.ds(n_ctx, L), :]
        o_ref[...] = o
        r = lax.rsqrt(jnp.mean(o * o, axis=-1, keepdims=True) + EPS)
        og = og_ref[pl.ds(n_ctx, L), :]
        ya_ref[...] =(o * r * ng_ref[...] * (og * _sigmoid(og))).astype(BF16)

    cb = HA // HEAD
    col = lambda kk: pl.BlockSpec((T, HEAD), lambda h: (0, kk * cb + h))
    return _pcall(
        body, name="hgrn_fwd", grid=(nh,),
        in_specs=[col(0), col(1), col(2), col(3), col(4),
                  pl.BlockSpec((2, 2, HEAD), lambda h: (0, 0, h)), pl.BlockSpec((1, HEAD), lambda h: (0, 0))],
        out_specs=[pl.BlockSpec((L, HEAD), lambda h: (0, h)), pl.BlockSpec((L, HEAD), lambda h: (0, h)),
                   pl.BlockSpec((1, 2, nc, HEAD, HEAD), lambda h: (h, 0, 0, 0, 0))],
        out_shape=[jax.ShapeDtypeStruct((L, HA), BF16), jax.ShapeDtypeStruct((L, HA), F32),
                   jax.ShapeDtypeStruct((nh, 2, nc, HEAD, HEAD), BF16)],
        scratch_shapes=[pltpu.VMEM((2, T, HEAD), F32), pltpu.VMEM((2, T, HEAD), F32),
                        pltpu.VMEM((2, T, HEAD), BF16), pltpu.VMEM((2, T, HEAD), BF16),
                        pltpu.VMEM((2, T, HEAD), F32)],
        compiler_params=_params(("parallel",)),
    )(p, p, p, p, p, lbl, ng)


def _hgrn_bwd(p, lbl, ng, o, dya, st, n_ctx, HA):
    T = p.shape[0]
    L = T - n_ctx
    nh = _hgrn_cols(HA)
    nc, ncc = T // CHUNK, n_ctx // CHUNK

    def body(q_ref, zf_ref, zb_ref, v_ref, og_ref, lbl_ref, ng_ref, o_ref, dya_ref, st_ref,
             dq_ref, dzf_ref, dzb_ref, dv_ref, dog_ref, dlbl_ref, dng_ref,
             do_scr, c_scr, k_scr, qe_scr, ke_scr, dg_scr, dk_scr, dq_scr, dv_scr, row_scr):
        h = pl.program_id(0)
        ov = o_ref[...]
        r = lax.rsqrt(jnp.mean(ov * ov, axis=-1, keepdims=True) + EPS)
        oh = ov * r
        ogv = og_ref[pl.ds(n_ctx, L), :]
        sg_o = _sigmoid(ogv)
        dyv = dya_ref[...]
        ngv = ng_ref[...]
        dog_ref[pl.ds(0, n_ctx), :] = jnp.zeros((n_ctx, HEAD), BF16)
        dog_ref[pl.ds(n_ctx, L), :] = (dyv * oh * ngv * (sg_o * (1.0 + ogv * (1.0 - sg_o)))).astype(BF16)
        don = dyv * (ogv * sg_o)
        dng = jnp.sum(don * oh, axis=0, keepdims=True)
        doh = don * ngv
        do_scr[pl.ds(0, n_ctx), :] = jnp.zeros((n_ctx, HEAD), F32)
        do_scr[pl.ds(n_ctx, L), :] = r * (doh - oh * jnp.mean(doh * oh, axis=-1, keepdims=True))

        @pl.when(h == 0)
        def _():
            dng_ref[...] = jnp.zeros_like(dng_ref)

        dng_ref[0:1, :] += dng

        t16 = lax.broadcasted_iota(jnp.int32, (CHUNK, HEAD), 0)
        dirs = ((0, False, zf_ref, dzf_ref), (1, True, zb_ref, dzb_ref))
        for d, rev, z_ref, _ in dirs:
            k, c, rest = _decay_terms(z_ref[...], _lower_bound(lbl_ref, d), rev)
            c_scr[d] = c
            k_scr[d] = k
            qe_scr[d] = (q_ref[...] * jnp.exp(c)).astype(BF16)
            ke_scr[d] = (k * jnp.exp(rest)).astype(BF16)
        dq_scr[...] = jnp.zeros_like(dq_scr)
        dv_scr[...] = jnp.zeros_like(dv_scr)

        zero = jnp.zeros((HEAD, HEAD), F32)

        def bwd_chunk(i, carry, u):
            new = []
            for (d, rev, _, _), dSt in zip(dirs, carry):
                ci = _scan_chunk(i, ncc, nc, rev)
                rows = _rows(ci)
                q, v, do = q_ref[rows, :], v_ref[rows, :], do_scr[rows, :]
                c, k = c_scr[d, rows, :], k_scr[d, rows, :]
                tot = _chunk_total(c, rev)
                etot = jnp.exp(tot)
                St = st_ref[0, d, ci]
                dSb = dSt.astype(BF16)
                do_b = do.astype(BF16)
                dq_x = _dot(do_b, St) * jnp.exp(c)
                dk_x = _dot(v.astype(BF16), dSb) * jnp.exp(tot - c)
                dv_x = _dot_nt(ke_scr[d, rows, :], dSb)
                dtot = (jnp.sum(St.astype(F32) * dSt, axis=0, keepdims=True) * etot
                        + jnp.sum(k * dk_x, axis=0, keepdims=True))
                dq = jnp.zeros((CHUNK, HEAD), F32)
                for s in range(CHUNK):
                    E = _pair_decay(c, s, rev)
                    XE = E * k[s:s + 1, :]
                    a = jnp.sum(q * XE, axis=1, keepdims=True)
                    da = jnp.sum(do * v[s:s + 1, :], axis=1, keepdims=True)
                    dq = dq + da * XE
                    row_scr[u, d, 0, s:s + 1, :] = jnp.sum(da * q * E, axis=0, keepdims=True)
                    row_scr[u, d, 1, s:s + 1, :] = jnp.sum(a * do, axis=0, keepdims=True)
                dq, dk, dv = dq + dq_x, row_scr[u, d, 0] + dk_x, row_scr[u, d, 1] + dv_x
                dg_scr[d, rows, :] = _chunk_cumsum(q * dq - k * dk, not rev) + dtot
                dk_scr[d, rows, :] = dk
                dq_scr[rows, :] += dq
                dv_scr[rows, :] += dv
                new.append(dSt * etot + _dot_tn(do_b, qe_scr[d, rows, :]))
            return tuple(new)

        def bwd_step(i2, carry):
            for u in range(2):
                carry = bwd_chunk(nc - 1 - (2 * i2 + u), carry, u)
            return carry

        lax.fori_loop(0, nc // 2, bwd_step, (zero, zero))

        for d, _, z_ref, dz_ref in dirs:
            lb = _lower_bound(lbl_ref, d)
            sg, f = _gate_terms(z_ref[...], lb)
            df = dg_scr[d] / f - dk_scr[d]
            dz_ref[...] = (df * (1.0 - lb) * sg * (1.0 - sg)).astype(BF16)
            dl0 = jnp.sum(df * (1.0 - sg), axis=0, keepdims=True) * lb * (1.0 - lb)
            dlbl_ref[d, 0:1, :] = dl0
            dlbl_ref[d, 1:2, :] = -dl0
        dq_ref[...] = dq_scr[...].astype(BF16)
        dv_ref[...] = dv_scr[...].astype(BF16)

    cb = HA // HEAD
    col = lambda kk: pl.BlockSpec((T, HEAD), lambda h: (0, kk * cb + h))
    tcol = pl.BlockSpec((T, HEAD), lambda h: (0, h))
    lcol = pl.BlockSpec((L, HEAD), lambda h: (0, h))
    outs = _pcall(
        body, name="hgrn_bwd", grid=(nh,),
        in_specs=[col(0), col(1), col(2), col(3), col(4),
                  pl.BlockSpec((2, 2, HEAD), lambda h: (0, 0, h)), pl.BlockSpec((1, HEAD), lambda h: (0, 0)),
                  lcol, lcol,
                  pl.BlockSpec((1, 2, nc, HEAD, HEAD), lambda h: (h, 0, 0, 0, 0), pipeline_mode=pl.Buffered(1))],
        out_specs=[tcol, tcol, tcol, tcol, tcol, pl.BlockSpec((2, 2, HEAD), lambda h: (0, 0, h)),
                   pl.BlockSpec((8, HEAD), lambda h: (0, 0))],
        out_shape=[jax.ShapeDtypeStruct((T, HA), BF16)] * 5 + [jax.ShapeDtypeStruct((2, 2, HA), F32),
                                                               jax.ShapeDtypeStruct((8, HEAD), F32)],
        scratch_shapes=[pltpu.VMEM((T, HEAD), F32),
                        pltpu.VMEM((2, T, HEAD), F32), pltpu.VMEM((2, T, HEAD), F32),
                        pltpu.VMEM((2, T, HEAD), BF16), pltpu.VMEM((2, T, HEAD), BF16),
                        pltpu.VMEM((2, T, HEAD), F32), pltpu.VMEM((2, T, HEAD), F32),
                        pltpu.VMEM((T, HEAD), F32), pltpu.VMEM((T, HEAD), F32),
                        pltpu.VMEM((2, 2, 2, CHUNK, HEAD), F32)],
        compiler_params=_params(("arbitrary",)),
    )(p, p, p, p, p, lbl, ng, o, dya, st)
    return outs


def _swap_halves(t, lane):
    q = HEAD // 4
    return jnp.where((lane % (2 * q)) < q, pltpu.roll(t, HEAD - q, 1), pltpu.roll(t, q, 1))


def _qk_norm(t, g):
    r = lax.rsqrt(jnp.mean(t * t, axis=-1, keepdims=True) + EPS)
    return t * r, r


def _rope(t, cos, sin, lane):
    return t * cos + _swap_halves(t, lane) * sin


def _qk_norm_bwd(dy, th, r, g):
    dth = dy * g
    return r * (dth - th * jnp.mean(dth * th, axis=-1, keepdims=True)), jnp.sum(dy * th, axis=0, keepdims=True)


def _rope_bwd(dy, cos, sin, lane):
    return dy * cos + _swap_halves(dy * sin, lane)


def _na_geometry(L):
    n_rows = L // GRID_W
    kr = min(WIN_R, n_rows)
    return n_rows, kr


def _na_prep(q_ref, k_ref, v_ref, gq_ref, gk_ref, cos_ref, sin_ref, qs, ks, vs, n_ctx, L):
    lane = lax.broadcasted_iota(jnp.int32, (L, HEAD), 1)
    cos, sin = cos_ref[...], sin_ref[...]
    qh, _ = _qk_norm(q_ref[pl.ds(n_ctx, L), :], None)
    qs[...] = _rope(qh * gq_ref[...], cos, sin, lane).astype(BF16)
    kh, _ = _qk_norm(k_ref[pl.ds(n_ctx, L), :], None)
    ks[pl.ds(n_ctx, L), :] = _rope(kh * gk_ref[...], cos, sin, lane).astype(BF16)
    kc, _ = _qk_norm(k_ref[pl.ds(0, n_ctx), :], None)
    ks[pl.ds(0, n_ctx), :] = (kc * gk_ref[...]).astype(BF16)
    vs[...] = v_ref[...].astype(BF16)


def _na_scores(r, qs, ks, bias_ref, n_ctx, n_rows, kr):
    scale = HEAD ** -0.5
    r0 = jnp.clip(r - WIN_R // 2, 0, n_rows - kr)
    qrows = pl.ds(pl.multiple_of(r * GRID_W, GRID_W), GRID_W)
    krows = pl.ds(pl.multiple_of(n_ctx + r0 * GRID_W, GRID_W), kr * GRID_W)
    qv = qs[qrows, :]
    sb = _dot_nt(qv, ks[krows, :]) * scale
    b0 = r0 - r + (WIN_R - 1)
    sb = sb + jnp.concatenate([bias_ref[0, b0 + 2 * jj] for jj in range(kr // 2)], axis=1)
    sc = _dot_nt(qv, ks[pl.ds(0, n_ctx), :]) * scale
    m = jnp.maximum(jnp.max(sb, axis=1, keepdims=True), jnp.max(sc, axis=1, keepdims=True))
    eb, ec = jnp.exp(sb - m), jnp.exp(sc - m)
    inv = 1.0 / (jnp.sum(eb, axis=1, keepdims=True) + jnp.sum(ec, axis=1, keepdims=True))
    return eb * inv, ec * inv, qrows, krows, b0


def _na_fwd(p, bias, gq, gk, cos, sin, n_ctx, off, HB):
    T = p.shape[0]
    L = T - n_ctx
    nh = HB // HEAD
    n_rows, kr = _na_geometry(L)
    ob = off // HEAD

    def body(q_ref, k_ref, v_ref, bias_ref, gq_ref, gk_ref, cos_ref, sin_ref, y_ref, qs, ks, vs):
        _na_prep(q_ref, k_ref, v_ref, gq_ref, gk_ref, cos_ref, sin_ref, qs, ks, vs, n_ctx, L)

        def step(i, carry):
            for u in range(2):
                pb, pc, qrows, krows, _ = _na_scores(2 * i + u, qs, ks, bias_ref, n_ctx, n_rows, kr)
                y = _dot(pb.astype(BF16), vs[krows, :]) + _dot(pc.astype(BF16), vs[pl.ds(0, n_ctx), :])
                y_ref[qrows, :] = y.astype(BF16)
            return carry

        lax.fori_loop(0, n_rows // 2, step, 0)

    col = lambda kk: pl.BlockSpec((T, HEAD), lambda h: (0, ob + kk * nh + h))
    vec = pl.BlockSpec((1, HEAD), lambda h: (0, 0))
    tab = pl.BlockSpec((L, HEAD), lambda h: (0, 0))
    return _pcall(
        body, name="na_fwd", grid=(nh,),
        in_specs=[col(0), col(1), col(2), pl.BlockSpec((1,) + bias.shape[1:], lambda h: (h, 0, 0, 0)),
                  vec, vec, tab, tab],
        out_specs=pl.BlockSpec((L, HEAD), lambda h: (0, h)),
        out_shape=jax.ShapeDtypeStruct((L, HB), BF16),
        scratch_shapes=[pltpu.VMEM((L, HEAD), BF16), pltpu.VMEM((T, HEAD), BF16), pltpu.VMEM((T, HEAD), BF16)],
        compiler_params=_params(("parallel",)),
    )(p, p, p, bias, gq, gk, cos, sin)


def _na_bwd(p, bias, gq, gk, cos, sin, dyb, n_ctx, off, HB):
    T = p.shape[0]
    L = T - n_ctx
    nh = HB // HEAD
    n_rows, kr = _na_geometry(L)
    ob = off // HEAD
    scale = HEAD ** -0.5

    def body(q_ref, k_ref, v_ref, bias_ref, gq_ref, gk_ref, cos_ref, sin_ref, dy_ref,
             dq_ref, dk_ref, dv_ref, dbias_ref, dg_ref, qs, ks, vs, dqa, dka, dva):
        h = pl.program_id(0)
        _na_prep(q_ref, k_ref, v_ref, gq_ref, gk_ref, cos_ref, sin_ref, qs, ks, vs, n_ctx, L)
        dka[...] = jnp.zeros_like(dka)
        dva[...] = jnp.zeros_like(dva)
        dbias_ref[...] = jnp.zeros_like(dbias_ref)

        crows = pl.ds(0, n_ctx)

        def step(i, carry):
            done = []
            for u in range(2):
                pb, pc, qrows, krows, b0 = _na_scores(2 * i + u, qs, ks, bias_ref, n_ctx, n_rows, kr)
                do = dy_ref[qrows, :]
                qv = qs[qrows, :]
                dpb = _dot_nt(do, vs[krows, :])
                dpc = _dot_nt(do, vs[crows, :])
                delta = jnp.sum(pb * dpb, axis=1, keepdims=True) + jnp.sum(pc * dpc, axis=1, keepdims=True)
                dsb = pb * (dpb - delta)
                dsc = pc * (dpc - delta)
                dsb_b, dsc_b = dsb.astype(BF16), dsc.astype(BF16)
                dqa[qrows, :] = (_dot(dsb_b, ks[krows, :]) + _dot(dsc_b, ks[crows, :])) * scale
                done.append((krows, b0, dsb, _dot_tn(dsb_b, qv) * scale, _dot_tn(dsc_b, qv) * scale,
                             _dot_tn(pb.astype(BF16), do), _dot_tn(pc.astype(BF16), do)))
            for krows, b0, dsb, dkb, dkc, dvb, dvc in done:
                for jj in range(kr // 2):
                    dbias_ref[0, b0 + 2 * jj] += dsb[:, jj * 2 * GRID_W:(jj + 1) * 2 * GRID_W]
                dka[krows, :] += dkb
                dka[crows, :] += dkc
                dva[krows, :] += dvb
                dva[crows, :] += dvc
            return carry

        lax.fori_loop(0, n_rows // 2, step, 0)

        lane = lax.broadcasted_iota(jnp.int32, (L, HEAD), 1)
        cos, sin = cos_ref[...], sin_ref[...]
        lat, ctx = pl.ds(n_ctx, L), pl.ds(0, n_ctx)
        gqv, gkv = gq_ref[...], gk_ref[...]
        qh, rq = _qk_norm(q_ref[lat, :], None)
        dq, dgq = _qk_norm_bwd(_rope_bwd(dqa[...], cos, sin, lane), qh, rq, gqv)
        dq_ref[ctx, :] = jnp.zeros((n_ctx, HEAD), BF16)
        dq_ref[lat, :] = dq.astype(BF16)
        kh, rk = _qk_norm(k_ref[lat, :], None)
        dk, dgk = _qk_norm_bwd(_rope_bwd(dka[lat, :], cos, sin, lane), kh, rk, gkv)
        dk_ref[lat, :] = dk.astype(BF16)
        kch, rkc = _qk_norm(k_ref[ctx, :], None)
        dkc, dgkc = _qk_norm_bwd(dka[ctx, :], kch, rkc, gkv)
        dk_ref[ctx, :] = dkc.astype(BF16)
        dv_ref[...] = dva[...].astype(BF16)

        @pl.when(h == 0)
        def _():
            dg_ref[...] = jnp.zeros_like(dg_ref)

        dg_ref[0:1, :] += dgq
        dg_ref[1:2, :] += dgk + dgkc

    col = lambda kk: pl.BlockSpec((T, HEAD), lambda h: (0, ob + kk * nh + h))
    vec = pl.BlockSpec((1, HEAD), lambda h: (0, 0))
    tab = pl.BlockSpec((L, HEAD), lambda h: (0, 0))
    tcol = pl.BlockSpec((T, HEAD), lambda h: (0, h))
    bspec = pl.BlockSpec((1,) + bias.shape[1:], lambda h: (h, 0, 0, 0))
    return _pcall(
        body, name="na_bwd", grid=(nh,),
        in_specs=[col(0), col(1), col(2), bspec, vec, vec, tab, tab, pl.BlockSpec((L, HEAD), lambda h: (0, h))],
        out_specs=[tcol, tcol, tcol, bspec, pl.BlockSpec((8, HEAD), lambda h: (0, 0))],
        out_shape=[jax.ShapeDtypeStruct((T, HB), BF16)] * 3 + [jax.ShapeDtypeStruct(bias.shape, F32),
                                                               jax.ShapeDtypeStruct((8, HEAD), F32)],
        scratch_shapes=[pltpu.VMEM((L, HEAD), BF16), pltpu.VMEM((T, HEAD), BF16), pltpu.VMEM((T, HEAD), BF16),
                        pltpu.VMEM((L, HEAD), F32), pltpu.VMEM((T, HEAD), F32), pltpu.VMEM((T, HEAD), F32)],
        compiler_params=_params(("arbitrary",)),
    )(p, p, p, bias, gq, gk, cos, sin, dyb)


def _bias_tables():
    w = np.arange(GRID_W)
    col_start = np.clip(w - WIN_C // 2, 0, GRID_W - WIN_C)
    col_in = (w[None, :] >= col_start[:, None]) & (w[None, :] < col_start[:, None] + WIN_C)
    dc = np.clip(w[None, :] - w[:, None], -(WIN_C - 1), WIN_C - 1) + WIN_C - 1
    n_pair = 2 * WIN_R
    ridx = np.zeros((n_pair, GRID_W, 2 * GRID_W), np.int32)
    cidx = np.zeros((n_pair, GRID_W, 2 * GRID_W), np.int32)
    valid = np.zeros((n_pair, GRID_W, 2 * GRID_W), bool)
    for i in range(n_pair):
        for half in range(2):
            row = i + half
            sl = slice(half * GRID_W, (half + 1) * GRID_W)
            ridx[i, :, sl] = min(row, 2 * WIN_R - 2)
            cidx[i, :, sl] = dc
            valid[i, :, sl] = col_in & (row <= 2 * WIN_R - 2)
    return ridx, cidx, valid


def _bias_onehot():
    _, cidx, valid = _bias_tables()
    K = GRID_W * 2 * GRID_W
    oh = np.zeros((K, 128), np.float32)
    neg = np.full((1, K), NEG, np.float32)
    for cq in range(GRID_W):
        for ll in range(2 * GRID_W):
            if valid[0, cq, ll]:
                oh[cq * 2 * GRID_W + ll, (ll // GRID_W) * 64 + cidx[0, cq, ll]] = 1.0
                neg[0, cq * 2 * GRID_W + ll] = 0.0
    return oh, neg


def _expand_bias(table):
    H = table.shape[0]
    n_pair, n_dc = 2 * WIN_R, 2 * WIN_C - 1
    tp = jnp.pad(table, ((0, 0), (0, n_pair + 1 - table.shape[1]), (0, 64 - n_dc)))
    t2 = jnp.concatenate([tp[:, :n_pair], tp[:, 1:n_pair + 1]], axis=-1).reshape(H * n_pair, 128)
    oh, neg = _bias_onehot()

    def body(t_ref, oh_ref, neg_ref, o_ref):
        o_ref[...] = lax.dot_general(t_ref[...], oh_ref[...], (((1,), (1,)), ((), ())), precision=HI,
                                     preferred_element_type=F32) + neg_ref[...]

    out = _pcall(body, name="bias_expand", out_shape=jax.ShapeDtypeStruct((H * n_pair, oh.shape[0]), F32),
                         compiler_params=_params())(t2, jnp.asarray(oh), jnp.asarray(neg))
    return out.reshape(H, n_pair, GRID_W, 2 * GRID_W)


def _bias_grad(dbias):
    H = dbias.shape[0]
    n_pair, n_dc = 2 * WIN_R, 2 * WIN_C - 1
    K = GRID_W * 2 * GRID_W
    oh, _ = _bias_onehot()
    flat = dbias.reshape(H * n_pair, K)

    def body(d_ref, oh_ref, o_ref):
        o_ref[...] = jnp.dot(d_ref[...], oh_ref[...], precision=HI, preferred_element_type=F32)

    g = _pcall(body, name="bias_grad", out_shape=jax.ShapeDtypeStruct((H * n_pair, 128), F32),
                       compiler_params=_params())(flat, jnp.asarray(oh))
    g = g.reshape(H, n_pair, 128)
    left, right = g[:, :, :n_dc], g[:, :, 64:64 + n_dc]
    out = left[:, :n_pair - 1]
    return out.at[:, 1:].add(right[:, :n_pair - 2])


def _rope_tables(L):
    pos = np.arange(L)
    row = (pos // GRID_W).astype(np.float32)
    colp = (pos % GRID_W).astype(np.float32)
    half = HEAD // 2
    nf = half // 2
    inv = (ROPE_THETA ** (-np.arange(nf, dtype=np.float32) / nf)).astype(np.float32)

    def tabs(pv):
        ang = pv[:, None] * inv[None, :]
        c, s = np.cos(ang), np.sin(ang)
        return np.concatenate([c, c], axis=1), np.concatenate([-s, s], axis=1)

    cr, sr = tabs(row)
    cc, sc = tabs(colp)
    return (jnp.asarray(np.concatenate([cr, cc], axis=1), F32), jnp.asarray(np.concatenate([sr, sc], axis=1), F32))


def _adamw(w, g, m, v, name, after=None, copy_g=False):
    R, C = w.shape
    tr = _row_tile(R, C)
    c1 = 1.0 - ADAM_B1 ** ADAM_STEP
    c2 = 1.0 - ADAM_B2 ** ADAM_STEP
    deps = [] if after is None else [after]
    n_out = 4 if copy_g else 3

    def body(w_ref, g_ref, m_ref, v_ref, *rest):
        d_ref, mo_ref, vo_ref = rest[len(deps):len(deps) + 3]
        gv = g_ref[...]
        mn = ADAM_B1 * m_ref[...] + (1.0 - ADAM_B1) * gv
        vn = ADAM_B2 * v_ref[...] + (1.0 - ADAM_B2) * (gv * gv)
        mo_ref[...] = mn
        vo_ref[...] = vn
        d_ref[...] = -ADAM_LR * ((mn / c1) / (jnp.sqrt(vn / c2) + ADAM_EPS) + ADAM_WD * w_ref[...])
        if copy_g:
            rest[-1][...] = gv

    blk = pl.BlockSpec((tr, C), lambda i: (i, 0))
    return _pcall(
        body, name=name, grid=(R // tr,),
        in_specs=[blk] * 4 + [_ANY] * len(deps), out_specs=[blk] * n_out,
        out_shape=[jax.ShapeDtypeStruct((R, C), F32)] * n_out,
        compiler_params=_params(("parallel",)),
    )(w, g, m, v, *deps)


PACK_W = 1024


def _pack(parts):
    flat, offs, pos = [], [], 0
    for a in parts:
        n = a.size
        padn = -n % PACK_W
        flat.append(jnp.pad(a.reshape(-1).astype(F32), (0, padn)))
        offs.append((pos, n, a.shape))
        pos += n + padn
    tail = -pos % (8 * PACK_W)
    if tail:
        flat.append(jnp.zeros((tail,), F32))
    return jnp.concatenate(flat).reshape(-1, PACK_W), offs


def _unpack(buf, offs, i):
    pos, n, shape = offs[i]
    return buf.reshape(buf.shape[:-2] + (-1,))[..., pos:pos + n].reshape(buf.shape[:-2] + shape)


def kernel(x, c, ctx, c_ctx, ada_w, ada_b, norm1_g, norm2_g, w_in, hgrn_lb_logits, hgrn_norm_g, na_q_norm_g, na_k_norm_g, na_rel_bias, w_branch_a, w_branch_b, w_out, ffn_w1, ffn_w3, ffn_conv_w, ffn_conv_b, ffn_w2, loss_target, m_c_ctx, m_ada_w, m_ada_b, m_norm1_g, m_norm2_g, m_w_in, m_hgrn_lb_logits, m_hgrn_norm_g, m_na_q_norm_g, m_na_k_norm_g, m_na_rel_bias, m_w_branch_a, m_w_branch_b, m_w_out, m_ffn_w1, m_ffn_w3, m_ffn_conv_w, m_ffn_conv_b, m_ffn_w2, v_c_ctx, v_ada_w, v_ada_b, v_norm1_g, v_norm2_g, v_w_in, v_hgrn_lb_logits, v_hgrn_norm_g, v_na_q_norm_g, v_na_k_norm_g, v_na_rel_bias, v_w_branch_a, v_w_branch_b, v_w_out, v_ffn_w1, v_ffn_w3, v_ffn_conv_w, v_ffn_conv_b, v_ffn_w2):
    weights = dict(c_ctx=c_ctx, ada_w=ada_w, ada_b=ada_b, norm1_g=norm1_g, norm2_g=norm2_g, w_in=w_in,
                   hgrn_lb_logits=hgrn_lb_logits, hgrn_norm_g=hgrn_norm_g, na_q_norm_g=na_q_norm_g,
                   na_k_norm_g=na_k_norm_g, na_rel_bias=na_rel_bias, w_branch_a=w_branch_a, w_branch_b=w_branch_b,
                   w_out=w_out, ffn_w1=ffn_w1, ffn_w3=ffn_w3, ffn_conv_w=ffn_conv_w, ffn_conv_b=ffn_conv_b,
                   ffn_w2=ffn_w2)
    moms = dict(c_ctx=(m_c_ctx, v_c_ctx), ada_w=(m_ada_w, v_ada_w), ada_b=(m_ada_b, v_ada_b),
                norm1_g=(m_norm1_g, v_norm1_g), norm2_g=(m_norm2_g, v_norm2_g), w_in=(m_w_in, v_w_in),
                hgrn_lb_logits=(m_hgrn_lb_logits, v_hgrn_lb_logits), hgrn_norm_g=(m_hgrn_norm_g, v_hgrn_norm_g),
                na_q_norm_g=(m_na_q_norm_g, v_na_q_norm_g), na_k_norm_g=(m_na_k_norm_g, v_na_k_norm_g),
                na_rel_bias=(m_na_rel_bias, v_na_rel_bias), w_branch_a=(m_w_branch_a, v_w_branch_a),
                w_branch_b=(m_w_branch_b, v_w_branch_b), w_out=(m_w_out, v_w_out), ffn_w1=(m_ffn_w1, v_ffn_w1),
                ffn_w3=(m_ffn_w3, v_ffn_w3), ffn_conv_w=(m_ffn_conv_w, v_ffn_conv_w),
                ffn_conv_b=(m_ffn_conv_b, v_ffn_conv_b), ffn_w2=(m_ffn_w2, v_ffn_w2))
    order = list(weights)

    L, D = x.shape[1], x.shape[2]
    N = ctx.shape[1]
    T = N + L
    HA = w_branch_a.shape[1]
    HB = w_branch_b.shape[1]
    F = ffn_conv_b.shape[1]
    IN = 5 * HA + 3 * HB + 2 * D
    n_ada = ada_w.shape[2]
    ix, iy, ic = _pos()
    chip = 2 * ix + iy
    dev = 2 * chip + ic

    _PENDING.clear()
    pk0, offs0 = _pack([c[0], hgrn_lb_logits, ffn_conv_w[0]])
    g0 = _allgather8(pk0, "gather_small0")
    c_all = _unpack(g0, offs0, 0)
    lbl_parts = _unpack(g0, offs0, 1)
    lbl = jnp.concatenate([lbl_parts[2 * j] for j in range(N_CHIP)], axis=-1)
    cw_parts = _unpack(g0, offs0, 2)
    cw = jnp.concatenate([cw_parts[2 * j] for j in range(N_CHIP)], axis=-1)
    cw8 = jnp.pad(cw, ((0, 5), (0, 0)))

    cs = jnp.concatenate([c_all, c_ctx[None, :], jnp.zeros((7, D), F32)], axis=0)
    ada_b_mine = lax.dynamic_slice(ada_b, (0, chip * n_ada), (1, n_ada))
    mod_mine = _ada_fwd(cs, ada_w[0], ada_b_mine)
    gm = _allgather8(mod_mine, "gather_mod")
    mod = jnp.concatenate([gm[2 * j] for j in range(N_CHIP)], axis=-1)
    mod_l = lax.dynamic_slice(mod, (dev, 0), (1, N_MOD * D)).reshape(N_MOD, D)
    mod_c = mod[8].reshape(N_MOD, D)
    sh1, sc1, g1, sh2, sc2, g2 = [mod_l[i:i + 1] for i in range(N_MOD)]
    shift1 = jnp.concatenate([mod_c[0:1], sh1], axis=0)
    scale1 = jnp.concatenate([mod_c[1:2], sc1], axis=0)

    shards = [w_in[0], w_branch_a[0], w_branch_b[0], w_out[0], ffn_w1[0], ffn_w3[0], ffn_w2[0]]
    names = ["w_in", "w_a", "w_b", "w_out", "w1", "w3", "w2"]
    slots = [_cast_bf16_slot(s, "cast_" + nm) for s, nm in zip(shards, names)]
    gat_in = _gather_start("in", slots[0:1], gm)
    gat_mix = _gather_start("mix", slots[1:4])
    gat_ffn = _gather_start("ffn", slots[4:7])

    xall = jnp.concatenate([ctx[0], x[0]], axis=0)
    h_all = _rms1_fwd(xall, norm1_g, shift1, scale1, N)
    gat_in = _gather_mid(gat_in, h_all)
    (Win,) = _gather_finish(gat_in, h_all)
    p = _mm_nn(h_all, Win, F32, "mm_p")
    gat_mix = _gather_mid(gat_mix, p)
    y_a, o_a, st_a = _hgrn_fwd(p, lbl, hgrn_norm_g, N, HA)
    Wa, Wb, Wo = _gather_finish(gat_mix, y_a)
    Wo = Wo.reshape(1, D, D)
    gat_ffn = _gather_mid(gat_ffn, y_a)
    bias = _expand_bias(na_rel_bias[0])
    cos, sin = _rope_tables(L)
    off_na = 5 * HA
    y_b = _na_fwd(p, bias, na_q_norm_g, na_k_norm_g, cos, sin, N, off_na, HB)
    za = _mm_nn(y_a, Wa, BF16, "mm_za")
    zb = _mm_nn(y_b, Wb, BF16, "mm_zb")
    off_ga, off_gb = 5 * HA + 3 * HB, 5 * HA + 3 * HB + D
    z = _merge_fwd(za, zb, p, N, off_ga, off_gb)
    mo = _mm_nn(z, Wo, F32, "mm_mo")
    vec2 = jnp.concatenate([g1, norm2_g, sh2, sc2, jnp.zeros((4, D), F32)], axis=0)
    x_mid, h2 = _resid_rms2_fwd(x[0], mo, vec2)
    W1, W3, W2 = _gather_finish(gat_ffn, h2)
    W2 = W2.reshape(1, F, D)
    u1 = _mm_nn(h2, W1, BF16, "mm_u1")
    u3 = _mm_nn(h2, W3, BF16, "mm_u3")
    a = _convgate_fwd(u1, u3, cw8, ffn_conv_b)
    f = _mm_nn(a, W2, F32, "mm_f")
    dy, df, s_loss = _loss_head(x_mid, f, g2, loss_target[0])
    loss = lax.psum(s_loss[1, 0], ("x", "y", "c"))
    d_g2 = s_loss[0:1]

    gW2 = _mm_tn(a, df, 1, "mm_gw2").reshape(N_CHIP, F // N_CHIP, D)
    da = _mm_nt(df, W2, BF16, "mm_da")
    du1, du3, s_conv = _convgate_bwd(u1, u3, da, cw8, ffn_conv_b)
    gW1 = _mm_tn(h2, du1, N_CHIP, "mm_gw1")
    gW3 = _mm_tn(h2, du3, N_CHIP, "mm_gw3")
    rs_ffn = _rs_start("ffn", [gW2, gW1, gW3])
    dh2a = _mm_nt(du1, W1, F32, "mm_dh2a")
    dh2b = _mm_nt(du3, W3, F32, "mm_dh2b")
    rs_ffn = _rs_scatter(rs_ffn, dh2b)
    dxm, dmo, s_rms2 = _resid_rms2_bwd(x_mid, dh2a, dh2b, dy, mo, vec2)
    gWo = _mm_tn(z, dmo, 1, "mm_gwo").reshape(N_CHIP, D // N_CHIP, D)
    dz = _mm_nt(dmo, Wo, BF16, "mm_dz")
    dza, dzb, dga, dgb = _merge_bwd(dz, za, zb, p, N, off_ga, off_gb)
    gWa = _mm_tn(y_a, dza, N_CHIP, "mm_gwa")
    gWb = _mm_tn(y_b, dzb, N_CHIP, "mm_gwb")
    rs_mix = _rs_start("mix", [gWo, gWa, gWb])
    dya = _mm_nt(dza, Wa, F32, "mm_dya")
    dyb = _mm_nt(dzb, Wb, BF16, "mm_dyb")
    rs_mix = _rs_scatter(rs_mix, dyb)
    dq_a, dzf, dzbk, di_a, dog, dlbl, s_ng = _hgrn_bwd(p, lbl, hgrn_norm_g, o_a, dya, st_a, N, HA)
    rs_ffn = _rs_join(rs_ffn, dq_a)
    dq_n, dk_n, dv_n, dbias, s_qk = _na_bwd(p, bias, na_q_norm_g, na_k_norm_g, cos, sin, dyb, N, off_na, HB)
    rs_mix = _rs_join(rs_mix, dq_n)
    dp = jnp.concatenate([dq_a, dzf, dzbk, di_a, dog, dq_n, dk_n, dv_n, dga, dgb], axis=1)
    gWin = _mm_tn(h_all, dp, N_CHIP, "mm_gwin")
    rs_in = _rs_start("in", [gWin])
    dh = _mm_nt(dp, Win, F32, "mm_dh")
    grad_x, s_rms1 = _rms1_bwd(xall, dh, dxm, norm1_g, scale1, N)
    d_table = _bias_grad(dbias)

    zD = jnp.zeros((1, D), F32)
    dmod_l = jnp.concatenate([s_rms1[2:3], s_rms1[3:4], s_rms2[3:4], s_rms2[0:1], s_rms2[1:2], d_g2], axis=0)
    dmod_c = jnp.concatenate([s_rms1[0:1], s_rms1[1:2], zD, zD, zD, zD], axis=0)
    pk1, offs1 = _pack([dmod_l, dmod_c, s_rms1[4], s_rms2[2], dlbl, s_ng[0], s_qk[0], s_qk[1], d_table,
                        s_conv[0:3], s_conv[3]])
    g1all = _allgather8(pk1, "gather_small1")
    tot1 = _sum8(g1all, "sum_small1")
    dmod_rows = _unpack(g1all, offs1, 0).reshape(N_DEV, N_MOD * D)
    dmod_c_tot = _unpack(tot1, offs1, 1).reshape(1, N_MOD * D)
    dmod16 = jnp.concatenate([dmod_rows, dmod_c_tot, jnp.zeros((7, N_MOD * D), F32)], axis=0)
    dmod16_mine = lax.dynamic_slice(dmod16, (0, chip * n_ada), (16, n_ada))
    g_ada_w, dact = _ada_bwd(cs, ada_w[0], dmod16_mine)
    pk2, offs2 = _pack([dact[8]])
    g2all = _allgather8(pk2, "gather_small2")
    dact_rows = _unpack(g2all, offs2, 0)
    dact_sel = jnp.concatenate([dact_rows[2 * j][None] for j in range(N_CHIP)] + [jnp.zeros((4, D), F32)], axis=0)

    grads = {}
    grads["ada_w"] = g_ada_w[None]
    grads["ada_b"] = (_unpack(tot1, offs1, 0) + _unpack(tot1, offs1, 1)).reshape(1, N_MOD * D)
    grads["norm1_g"] = _unpack(tot1, offs1, 2)[None]
    grads["norm2_g"] = _unpack(tot1, offs1, 3)[None]
    g_lbl = _unpack(tot1, offs1, 4)
    n_lb = HA // N_CHIP
    grads["hgrn_lb_logits"] = lax.dynamic_slice(g_lbl, (0, 0, chip * n_lb), (2, 2, n_lb))
    grads["hgrn_norm_g"] = _unpack(tot1, offs1, 5)[None]
    grads["na_q_norm_g"] = _unpack(tot1, offs1, 6)[None]
    grads["na_k_norm_g"] = _unpack(tot1, offs1, 7)[None]
    grads["na_rel_bias"] = _unpack(tot1, offs1, 8)[None]
    g_cw = _unpack(tot1, offs1, 9)
    n_f = F // N_CHIP
    grads["ffn_conv_w"] = lax.dynamic_slice(g_cw, (0, chip * n_f), (3, n_f))[None]
    grads["ffn_conv_b"] = _unpack(tot1, offs1, 10)[None]

    g_c_ctx = _dsilu_rows(dact_sel, c_ctx[None, :], "grad_c_ctx")
    grads["c_ctx"] = g_c_ctx[0]

    rs_in = _rs_scatter(rs_in, g_c_ctx)
    big_names = ["ada_w", "w_in", "w_branch_a", "w_branch_b", "w_out", "ffn_w1", "ffn_w3", "ffn_w2"]
    small_names = [n for n in order if n not in big_names]
    delta, new_m, new_v = {}, {}, {}

    def update(nm, after=None):
        reduced = nm != "ada_w"
        d_, m_, v_, *g_ = _adamw(weights[nm][0], grads[nm][0], moms[nm][0][0], moms[nm][1][0], "adamw_" + nm,
                                 after, copy_g=reduced)
        delta[nm], new_m[nm], new_v[nm] = d_[None], m_[None], v_[None]
        if reduced:
            grads[nm] = g_[0][None]
        return d_

    last = update("ada_w")
    for nm, g in zip(["ffn_w2", "ffn_w1", "ffn_w3"], _rs_finish(rs_ffn, last)):
        grads[nm] = g[None]
        last = update(nm, last)
    for nm, g in zip(["w_out", "w_branch_a", "w_branch_b"], _rs_finish(rs_mix, last)):
        grads[nm] = g[None]
        last = update(nm, last)
    rs_in = _rs_join(rs_in, last)
    grads["w_in"] = _rs_finish(rs_in, last)[0][None]
    update("w_in")
    pw, offw = _pack([weights[n] for n in small_names])
    pg, _ = _pack([grads[n] for n in small_names])
    pm, _ = _pack([moms[n][0] for n in small_names])
    pv, _ = _pack([moms[n][1] for n in small_names])
    d_, m_, v_ = _adamw(pw, pg, pm, pv, "adamw_small")
    for i, nm in enumerate(small_names):
        delta[nm], new_m[nm], new_v[nm] = _unpack(d_, offw, i), _unpack(m_, offw, i), _unpack(v_, offw, i)

    return (loss, grad_x[None], *[grads[n] for n in order], *[delta[n] for n in order],
            *[new_m[n] for n in order], *[new_v[n] for n in order])


def _dsilu_rows(v, cv, name):
    D = v.shape[1]

    def body(v_ref, c_ref, o_ref):
        t = c_ref[...]
        s = _sigmoid(t)
        o_ref[...] = (((v_ref[0:1, :] + v_ref[1:2, :]) + v_ref[2:3, :]) + v_ref[3:4, :]) * (s * (1.0 + t * (1.0 - s)))

    return _pcall(body, name=name, out_shape=jax.ShapeDtypeStruct((1, D), F32),
                          compiler_params=_params())(v, cv)
```

```python
import functools

import numpy as np
import jax
import jax.numpy as jnp
from jax import lax
from jax.experimental import pallas as pl
from jax.experimental.pallas import tpu as pltpu

F32 = jnp.float32
BF16 = jnp.bfloat16
MESH = pl.DeviceIdType.MESH

HEAD = 128
GRID_W = 64
WIN_R = 8
WIN_C = 16
ROPE_THETA = 10000.0
EPS = 1e-6
N_MOD = 6
CHUNK = 16
HGRN_UNROLL = 4
ADAM_LR = 0.001
ADAM_B1 = 0.9
ADAM_B2 = 0.999
ADAM_EPS = 1e-08
ADAM_WD = 0.01
ADAM_STEP = 10
NEG = -1e30
VMEM_LIMIT = 56 * 1024 * 1024
N_DEV = 8
N_CHIP = 4
HI = lax.Precision.HIGHEST


def _pick(n, cands):
    for c in cands:
        if n % c == 0:
            return c
    return n


def _row_tile(rows, cols, target_bytes=1 << 20):
    want = max(16, target_bytes // (4 * cols))
    for t in (512, 256, 128, 64, 32, 16, 8):
        if t <= want and rows % t == 0:
            return t
    return rows


def _params(sem=None):
    return pltpu.CompilerParams(dimension_semantics=sem, vmem_limit_bytes=VMEM_LIMIT)


def _dot(a, b):
    return jnp.dot(a, b, preferred_element_type=F32)


def _dot_nt(a, b):
    return lax.dot_general(a, b, (((1,), (1,)), ((), ())), preferred_element_type=F32)


def _dot_tn(a, b):
    return lax.dot_general(a, b, (((0,), (0,)), ((), ())), preferred_element_type=F32)


def _sigmoid(x):
    return 1.0 / (1.0 + jnp.exp(-x))


def _col_tile(n):
    return n if n <= 1536 else _pick(n, (1024, 768, 512, 384, 256, 128))


def _mm_nn(x, w3, out_dtype, name):
    M, K = x.shape
    S, _, n = w3.shape
    tm = _pick(M, (768, 512, 256, 128, 64))
    tn = _col_tile(n)
    nb = n // tn

    def body(x_ref, w_ref, o_ref):
        o_ref[...] = _dot(x_ref[...].astype(BF16), w_ref[0]).astype(o_ref.dtype)

    return _pcall(
        body, name=name, grid=(M // tm, S * nb),
        in_specs=[pl.BlockSpec((tm, K), lambda i, j: (i, 0)),
                  pl.BlockSpec((1, K, tn), lambda i, j: (j // nb, 0, j % nb))],
        out_specs=pl.BlockSpec((tm, tn), lambda i, j: (i, j)),
        out_shape=jax.ShapeDtypeStruct((M, S * n), out_dtype),
        compiler_params=_params(("parallel", "parallel")),
    )(x, w3)


def _mm_nt(dy, w3, out_dtype, name):
    M = dy.shape[0]
    S, K, n = w3.shape
    tm = _pick(M, (768, 512, 256, 128, 64))
    tk = K if K <= 2048 else _pick(K, (1408, 1024, 512, 256, 128))
    tc = n if n <= 2048 else _col_tile(n)
    nb = n // tc
    nsteps = S * nb

    def body(dy_ref, w_ref, o_ref, acc_ref):
        s = pl.program_id(2)

        @pl.when(s == 0)
        def _():
            acc_ref[...] = jnp.zeros_like(acc_ref)

        acc_ref[...] += _dot_nt(dy_ref[...].astype(BF16), w_ref[0])

        @pl.when(s == nsteps - 1)
        def _():
            o_ref[...] = acc_ref[...].astype(o_ref.dtype)

    return _pcall(
        body, name=name, grid=(M // tm, K // tk, nsteps),
        in_specs=[pl.BlockSpec((tm, tc), lambda i, k, s: (i, s)),
                  pl.BlockSpec((1, tk, tc), lambda i, k, s: (s // nb, k, s % nb))],
        out_specs=pl.BlockSpec((tm, tk), lambda i, k, s: (i, k)),
        out_shape=jax.ShapeDtypeStruct((M, K), out_dtype),
        scratch_shapes=[pltpu.VMEM((tm, tk), F32)],
        compiler_params=_params(("parallel", "parallel", "arbitrary")),
    )(dy, w3)


def _mm_tn(x, dy, S, name):
    M, K = x.shape
    n = dy.shape[1] // S
    tk = _pick(K, (512, 256, 128))
    tn = _col_tile(n)
    nb = n // tn

    def body(x_ref, dy_ref, o_ref):
        o_ref[0] = _dot_tn(x_ref[...].astype(BF16), dy_ref[...].astype(BF16)).astype(BF16)

    return _pcall(
        body, name=name, grid=(S * nb, K // tk),
        in_specs=[pl.BlockSpec((M, tk), lambda j, k: (0, k)),
                  pl.BlockSpec((M, tn), lambda j, k: (0, j))],
        out_specs=pl.BlockSpec((1, tk, tn), lambda j, k: (j // nb, k, j % nb)),
        out_shape=jax.ShapeDtypeStruct((S, K, n), BF16),
        compiler_params=_params(("parallel", "parallel")),
    )(x, dy)


def _chip_index():
    return (2 * lax.axis_index("x") + lax.axis_index("y")).astype(jnp.int32).reshape(1)


def _cast_bf16_slot(w, name):
    R, C = w.shape
    tr = _row_tile(R, C, 2 << 20)

    def body(j_ref, w_ref, o_ref):
        o_ref[0] = w_ref[...].astype(BF16)

    return _pcall(
        body, name=name,
        grid_spec=pltpu.PrefetchScalarGridSpec(
            num_scalar_prefetch=1, grid=(R // tr,),
            in_specs=[pl.BlockSpec((tr, C), lambda i, j_ref: (i, 0))],
            out_specs=pl.BlockSpec((1, tr, C), lambda i, j_ref: (j_ref[0], i, 0))),
        out_shape=jax.ShapeDtypeStruct((N_CHIP, R, C), BF16),
        compiler_params=_params(("parallel",)),
    )(_chip_index(), w)


def _pos():
    return lax.axis_index("x"), lax.axis_index("y"), lax.axis_index("c")


def _other_chips(x, y):
    return [(x, 1 - y), (1 - x, y), (1 - x, 1 - y)]


def _allgather8(v, name):
    R, C = v.shape

    def body(x_ref, out_ref, send_sems, recv_sems, local_sem):
        x, y, c = _pos()
        me, sibling = (x, y, c), (x, y, 1 - c)
        chips = _other_chips(x, y)

        def slot(px, py, pc):
            return out_ref.at[4 * px + 2 * py + pc]

        def copy(k, block, to, src=None):
            return pltpu.make_async_remote_copy(
                src_ref=slot(*block) if src is None else src, dst_ref=slot(*block),
                send_sem=send_sems.at[k], recv_sem=recv_sems.at[k], device_id=to, device_id_type=MESH)

        mine = pltpu.make_async_copy(x_ref, slot(*me), local_sem)
        mine.start()
        first = [copy(0, me, sibling, src=x_ref)]
        first += [copy(1 + j, me, (*chip, c), src=x_ref) for j, chip in enumerate(chips)]
        for cp in first:
            cp.start()
        passed = [copy(4 + j, (*chip, c), sibling) for j, chip in enumerate(chips)]
        for j, chip in enumerate(chips):
            copy(1 + j, (*chip, c), me).wait_recv()
            passed[j].start()
        copy(0, sibling, me).wait_recv()
        for j, chip in enumerate(chips):
            copy(4 + j, (*chip, 1 - c), me).wait_recv()
        for cp in first + passed:
            cp.wait_send()
        mine.wait()

    return _pcall(
        body, name=name,
        out_shape=jax.ShapeDtypeStruct((N_DEV, R, C), v.dtype),
        in_specs=[pl.BlockSpec(memory_space=pltpu.VMEM)],
        out_specs=pl.BlockSpec(memory_space=pltpu.VMEM),
        scratch_shapes=[pltpu.SemaphoreType.DMA((7,)), pltpu.SemaphoreType.DMA((7,)), pltpu.SemaphoreType.DMA],
        compiler_params=pltpu.CompilerParams(vmem_limit_bytes=VMEM_LIMIT),
    )(v)


_HBM = pl.BlockSpec(memory_space=pltpu.HBM)
_SEM = pl.BlockSpec(memory_space=pltpu.SEMAPHORE)
_ANY = pl.BlockSpec(memory_space=pl.ANY)
_EFFECT = pltpu.SideEffectType.DATAFLOW_SIDE_EFFECTING
_PENDING = []


def _pcall(body, **kw):
    def run(*operands):
        if not _PENDING or "in_specs" not in kw:
            return pl.pallas_call(body, **kw)(*operands)
        deps = list(_PENDING)
        n = len(operands)

        def tied(*refs):
            return body(*refs[:n], *refs[n + len(deps):])

        return pl.pallas_call(tied, **{**kw, "in_specs": list(kw["in_specs"]) + [_ANY] * len(deps)})(*operands, *deps)
    return run


def _copies(plan, refs, send_sems, recv_sems):
    return [pltpu.make_async_remote_copy(src_ref=src, dst_ref=dst, send_sem=send_sems.at[k], recv_sem=recv_sems.at[k],
                                         device_id=dev, device_id_type=MESH)
            for k, (src, dst, dev) in enumerate(plan(refs))]


def _xfer_start(name, bufs, plan, n_copies, after=None):
    n = len(bufs)
    deps = list(_PENDING) + ([after] if after is not None else [])
    nd = len(deps)

    def body(*refs):
        for cp in _copies(plan, refs[:n], refs[n + nd], refs[n + nd + 1]):
            cp.start()
        refs[-1][...] = jnp.zeros_like(refs[-1])

    outs = pl.pallas_call(
        body, name=name,
        out_shape=(pltpu.SemaphoreType.DMA((n_copies,)), pltpu.SemaphoreType.DMA((n_copies,)),
                   *[pltpu.HBM(b.shape, b.dtype) for b in bufs], jax.ShapeDtypeStruct((8, 128), F32)),
        in_specs=[_HBM] * n + [_ANY] * nd,
        out_specs=(_SEM, _SEM, *[_HBM] * n, pl.BlockSpec(memory_space=pltpu.VMEM)),
        input_output_aliases={t: 2 + t for t in range(n)},
        compiler_params=pltpu.CompilerParams(has_side_effects=_EFFECT),
    )(*[pltpu.with_memory_space_constraint(b, pltpu.HBM) for b in bufs], *deps)
    _PENDING[:] = [outs[-1]]
    return (outs[0], outs[1]), list(outs[2:2 + n])


def _xfer_wait(name, sems, bufs, plan, after):
    n = len(bufs)

    def body(*refs):
        cps = _copies(plan, refs[:n], refs[n], refs[n + 1])
        for cp in cps:
            cp.wait_send()
        for cp in cps:
            cp.wait_recv()

    outs = pl.pallas_call(
        body, name=name,
        out_shape=tuple(pltpu.HBM(b.shape, b.dtype) for b in bufs),
        in_specs=[_HBM] * n + [_SEM, _SEM, _ANY],
        out_specs=tuple([_HBM] * n),
        input_output_aliases={t: t for t in range(n)},
        compiler_params=pltpu.CompilerParams(has_side_effects=_EFFECT),
    )(*bufs, sems[0], sems[1], after)
    return list(outs)


def _half(ref_rows, hc):
    h = ref_rows // 2
    return pl.ds(hc * h, h)


def _plan_gather_ici(bufs):
    x, y, c = _pos()
    j = 2 * x + y
    return [(b.at[j, _half(b.shape[1], c)], b.at[j, _half(b.shape[1], c)], (*chip, c))
            for b in bufs for chip in _other_chips(x, y)]


def _plan_gather_d2d(bufs):
    x, y, c = _pos()
    out = []
    for b in bufs:
        for chip in _other_chips(x, y):
            blk = b.at[2 * chip[0] + chip[1], _half(b.shape[1], c)]
            out.append((blk, blk, (x, y, 1 - c)))
    return out


def _plan_pair_swap(n):
    def plan(bufs):
        x, y, c = _pos()
        return [(g.at[:, _half(g.shape[1], 1 - c)], land, (x, y, 1 - c)) for g, land in zip(bufs[:n], bufs[n:])]
    return plan


def _plan_chip_scatter(n):
    def plan(bufs):
        x, y, c = _pos()
        return [(p.at[2 * chip[0] + chip[1]], land.at[k], (*chip, c))
                for p, land in zip(bufs[:n], bufs[n:]) for k, chip in enumerate(_other_chips(x, y))]
    return plan


def _plan_pair_join(bufs):
    x, y, c = _pos()
    return [(b.at[_half(b.shape[0], c)], b.at[_half(b.shape[0], c)], (x, y, 1 - c)) for b in bufs]


def _empty_hbm(shape, dtype):
    return pltpu.with_memory_space_constraint(lax.empty(shape, dtype), pltpu.HBM)


def _gather_start(tag, bufs, after=None):
    sems, bufs = _xfer_start(f"gather_ici_start_{tag}", bufs, _plan_gather_ici, 3 * len(bufs), after)
    return dict(tag=tag, sems=sems, bufs=bufs)


def _gather_mid(st, after):
    tag = st["tag"]
    bufs = _xfer_wait(f"gather_ici_wait_{tag}", st["sems"], st["bufs"], _plan_gather_ici, after)
    sems, bufs = _xfer_start(f"gather_d2d_start_{tag}", bufs, _plan_gather_d2d, 3 * len(bufs))
    return dict(tag=tag, sems=sems, bufs=bufs)


def _gather_finish(st, after):
    return _xfer_wait(f"gather_d2d_wait_{st['tag']}", st["sems"], st["bufs"], _plan_gather_d2d, after)


def _pair_add(g, r, name):
    S, R, C = g.shape
    h = R // 2
    tr = _row_tile(h, C)
    nb = h // tr

    def body(c_ref, g_ref, r_ref, o_ref):
        o_ref[...] = (g_ref[...].astype(F32) + r_ref[...].astype(F32)).astype(BF16)

    return _pcall(
        body, name=name,
        grid_spec=pltpu.PrefetchScalarGridSpec(
            num_scalar_prefetch=1, grid=(S, nb),
            in_specs=[pl.BlockSpec((1, tr, C), lambda s, i, c_ref: (s, c_ref[0] * nb + i, 0)),
                      pl.BlockSpec((1, tr, C), lambda s, i, c_ref: (s, i, 0))],
            out_specs=pl.BlockSpec((1, tr, C), lambda s, i, c_ref: (s, i, 0))),
        out_shape=jax.ShapeDtypeStruct((S, h, C), BF16),
        compiler_params=_params(("parallel", "parallel")),
    )(lax.axis_index("c").astype(jnp.int32).reshape(1), g, r)


def _chip_sum(p, rb, name):
    S, h, C = p.shape
    tr = _row_tile(h, C)
    nb = h // tr
    jc = jnp.concatenate([_chip_index(), lax.axis_index("c").astype(jnp.int32).reshape(1)])

    def body(jc_ref, p_ref, r_ref, o_ref):
        o_ref[...] = ((p_ref[0].astype(F32) + r_ref[0].astype(F32)) + r_ref[1].astype(F32)) + r_ref[2].astype(F32)

    return _pcall(
        body, name=name,
        grid_spec=pltpu.PrefetchScalarGridSpec(
            num_scalar_prefetch=1, grid=(nb,),
            in_specs=[pl.BlockSpec((1, tr, C), lambda i, jc_ref: (jc_ref[0], i, 0)),
                      pl.BlockSpec((3, tr, C), lambda i, jc_ref: (0, i, 0))],
            out_specs=pl.BlockSpec((tr, C), lambda i, jc_ref: (jc_ref[1] * nb + i, 0))),
        out_shape=jax.ShapeDtypeStruct((2 * h, C), F32),
        compiler_params=_params(("parallel",)),
    )(jc, p, rb)


def _rs_start(tag, gs):
    n = len(gs)
    lands = [_empty_hbm((g.shape[0], g.shape[1] // 2, g.shape[2]), g.dtype) for g in gs]
    sems, bufs = _xfer_start(f"rs_swap_start_{tag}", list(gs) + lands, _plan_pair_swap(n), n)
    return dict(tag=tag, n=n, sems=sems, bufs=bufs)


def _rs_scatter(st, after):
    tag, n = st["tag"], st["n"]
    bufs = _xfer_wait(f"rs_swap_wait_{tag}", st["sems"], st["bufs"], _plan_pair_swap(n), after)
    ps = [_pair_add(g, r, f"rs_pair_add_{tag}{t}") for t, (g, r) in enumerate(zip(bufs[:n], bufs[n:]))]
    lands = [_empty_hbm((3,) + p.shape[1:], p.dtype) for p in ps]
    sems, bufs = _xfer_start(f"rs_scatter_start_{tag}", ps + lands, _plan_chip_scatter(n), 3 * n)
    return dict(tag=tag, n=n, sems=sems, bufs=bufs)


def _rs_join(st, after):
    tag, n = st["tag"], st["n"]
    bufs = _xfer_wait(f"rs_scatter_wait_{tag}", st["sems"], st["bufs"], _plan_chip_scatter(n), after)
    fs = [_chip_sum(p, rb, f"rs_chip_sum_{tag}{t}") for t, (p, rb) in enumerate(zip(bufs[:n], bufs[n:]))]
    sems, bufs = _xfer_start(f"rs_join_start_{tag}", fs, _plan_pair_join, n)
    return dict(tag=tag, n=n, sems=sems, bufs=bufs)


def _rs_finish(st, after):
    return _xfer_wait(f"rs_join_wait_{st['tag']}", st["sems"], st["bufs"], _plan_pair_join, after)


def _sum8(g, name):
    _, R, C = g.shape

    def body(g_ref, o_ref):
        acc = g_ref[0]
        for d in range(1, N_DEV):
            acc = acc + g_ref[d]
        o_ref[...] = acc

    return _pcall(body, name=name, out_shape=jax.ShapeDtypeStruct((R, C), F32),
                          compiler_params=_params())(g)


def _ada_fwd(cs, w, b):
    D, n = w.shape
    tn = _pick(n, (512, 384, 256, 128))

    def body(c_ref, w_ref, b_ref, o_ref):
        cv = c_ref[...]
        a = (cv * _sigmoid(cv)).astype(BF16)
        o_ref[...] = _dot(a, w_ref[...].astype(BF16)) + b_ref[...]

    return _pcall(
        body, name="ada_fwd", grid=(n // tn,),
        in_specs=[pl.BlockSpec((16, D), lambda j: (0, 0)), pl.BlockSpec((D, tn), lambda j: (0, j)),
                  pl.BlockSpec((1, tn), lambda j: (0, j))],
        out_specs=pl.BlockSpec((16, tn), lambda j: (0, j)),
        out_shape=jax.ShapeDtypeStruct((16, n), F32),
        compiler_params=_params(("parallel",)),
    )(cs, w, b)


def _ada_bwd(cs, w, dmod):
    D, n = w.shape
    tn = _pick(n, (512, 384, 256, 128))

    def body(c_ref, w_ref, d_ref, gw_ref, da_ref):
        j = pl.program_id(0)
        cv = c_ref[...]
        a = cv * _sigmoid(cv)
        d = d_ref[...]
        gw_ref[...] = lax.dot_general(a, d, (((0,), (0,)), ((), ())), precision=HI, preferred_element_type=F32)

        @pl.when(j == 0)
        def _():
            da_ref[...] = jnp.zeros_like(da_ref)

        da_ref[...] += _dot_nt(d.astype(BF16), w_ref[...].astype(BF16))

    return _pcall(
        body, name="ada_bwd", grid=(n // tn,),
        in_specs=[pl.BlockSpec((16, D), lambda j: (0, 0)), pl.BlockSpec((D, tn), lambda j: (0, j)),
                  pl.BlockSpec((16, tn), lambda j: (0, j))],
        out_specs=[pl.BlockSpec((D, tn), lambda j: (0, j)), pl.BlockSpec((16, D), lambda j: (0, 0))],
        out_shape=[jax.ShapeDtypeStruct((D, n), F32), jax.ShapeDtypeStruct((16, D), F32)],
        compiler_params=_params(("arbitrary",)),
    )(cs, w, dmod)


def _rms1_fwd(xall, gain, shift2, scale2, n_ctx):
    T, D = xall.shape
    tb = _pick(n_ctx, (256, 128, 64, 32, 16))
    nctx = n_ctx // tb

    def body(x_ref, g_ref, sh_ref, sc_ref, o_ref):
        i = pl.program_id(0)
        xv = x_ref[...]
        r = lax.rsqrt(jnp.mean(xv * xv, axis=-1, keepdims=True) + EPS)
        nrm = xv * r * g_ref[...]
        lat = i >= nctx
        sh = jnp.where(lat, sh_ref[1:2, :], sh_ref[0:1, :])
        sc = jnp.where(lat, sc_ref[1:2, :], sc_ref[0:1, :])
        o_ref[...] = (nrm * (1.0 + sc) + sh).astype(BF16)

    vec = lambda r: pl.BlockSpec((r, D), lambda i: (0, 0))
    return _pcall(
        body, name="rms1_fwd", grid=(T // tb,),
        in_specs=[pl.BlockSpec((tb, D), lambda i: (i, 0)), vec(1), vec(2), vec(2)],
        out_specs=pl.BlockSpec((tb, D), lambda i: (i, 0)),
        out_shape=jax.ShapeDtypeStruct((T, D), BF16),
        compiler_params=_params(("parallel",)),
    )(xall, gain, shift2, scale2)


def _rms1_bwd(xall, dh, dxmid, gain, scale2, n_ctx):
    T, D = xall.shape
    L = T - n_ctx
    tb = _pick(n_ctx, (256, 128, 64, 32, 16))
    nctx = n_ctx // tb

    def body(x_ref, dh_ref, dxm_ref, g_ref, sc_ref, dx_ref, cs_ref):
        i = pl.program_id(0)
        lat = i >= nctx
        xv = x_ref[...]
        r = lax.rsqrt(jnp.mean(xv * xv, axis=-1, keepdims=True) + EPS)
        xh = xv * r
        g = g_ref[...]
        nrm = xh * g
        sc = jnp.where(lat, sc_ref[1:2, :], sc_ref[0:1, :])
        dhv = dh_ref[...]
        dn = dhv * (1.0 + sc)
        dxh = dn * g
        dxv = r * (dxh - xh * jnp.mean(dxh * xh, axis=-1, keepdims=True))
        s_sh = jnp.sum(dhv, axis=0, keepdims=True)
        s_sc = jnp.sum(dhv * nrm, axis=0, keepdims=True)
        s_g = jnp.sum(dn * xh, axis=0, keepdims=True)
        zero = jnp.zeros_like(s_sh)
        rows = lax.broadcasted_iota(jnp.int32, (8, D), 0)
        upd = jnp.where(rows == 0, jnp.where(lat, zero, s_sh),
              jnp.where(rows == 1, jnp.where(lat, zero, s_sc),
              jnp.where(rows == 2, jnp.where(lat, s_sh, zero),
              jnp.where(rows == 3, jnp.where(lat, s_sc, zero),
              jnp.where(rows == 4, s_g, 0.0)))))

        @pl.when(i == 0)
        def _():
            cs_ref[...] = jnp.zeros_like(cs_ref)

        cs_ref[...] += upd

        @pl.when(lat)
        def _():
            dx_ref[...] = dxv + dxm_ref[...]

    lat_blk = lambda i: (jnp.maximum(i - nctx, 0), 0)
    vec = lambda r: pl.BlockSpec((r, D), lambda i: (0, 0))
    return _pcall(
        body, name="rms1_bwd", grid=(T // tb,),
        in_specs=[pl.BlockSpec((tb, D), lambda i: (i, 0)), pl.BlockSpec((tb, D), lambda i: (i, 0)),
                  pl.BlockSpec((tb, D), lat_blk), vec(1), vec(2)],
        out_specs=[pl.BlockSpec((tb, D), lat_blk), vec(8)],
        out_shape=[jax.ShapeDtypeStruct((L, D), F32), jax.ShapeDtypeStruct((8, D), F32)],
        compiler_params=_params(("arbitrary",)),
    )(xall, dh, dxmid, gain, scale2)


def _resid_rms2_fwd(x, mo, vecs):
    L, D = x.shape
    tb = _pick(L, (256, 128, 64))

    def body(x_ref, mo_ref, v_ref, xm_ref, h_ref):
        xm = x_ref[...] + v_ref[0:1, :] * mo_ref[...]
        xm_ref[...] = xm
        r = lax.rsqrt(jnp.mean(xm * xm, axis=-1, keepdims=True) + EPS)
        h_ref[...] = (xm * r * v_ref[1:2, :] * (1.0 + v_ref[3:4, :]) + v_ref[2:3, :]).astype(BF16)

    blk = pl.BlockSpec((tb, D), lambda i: (i, 0))
    return _pcall(
        body, name="resid_rms2_fwd", grid=(L // tb,),
        in_specs=[blk, blk, pl.BlockSpec((8, D), lambda i: (0, 0))],
        out_specs=[blk, blk],
        out_shape=[jax.ShapeDtypeStruct((L, D), F32), jax.ShapeDtypeStruct((L, D), BF16)],
        compiler_params=_params(("parallel",)),
    )(x, mo, vecs)


def _resid_rms2_bwd(xmid, dh_a, dh_b, dy, mo, vecs):
    L, D = xmid.shape
    tb = _pick(L, (256, 128, 64))

    def body(xm_ref, da_ref, db_ref, dy_ref, mo_ref, v_ref, dxm_ref, dmo_ref, cs_ref):
        i = pl.program_id(0)
        xm = xm_ref[...]
        r = lax.rsqrt(jnp.mean(xm * xm, axis=-1, keepdims=True) + EPS)
        xh = xm * r
        g = v_ref[1:2, :]
        nrm = xh * g
        dhv = da_ref[...] + db_ref[...]
        dn = dhv * (1.0 + v_ref[3:4, :])
        dxh = dn * g
        dxm = dy_ref[...] + r * (dxh - xh * jnp.mean(dxh * xh, axis=-1, keepdims=True))
        dxm_ref[...] = dxm
        dmo_ref[...] = (dxm * v_ref[0:1, :]).astype(BF16)
        s0 = jnp.sum(dhv, axis=0, keepdims=True)
        s1 = jnp.sum(dhv * nrm, axis=0, keepdims=True)
        s2 = jnp.sum(dn * xh, axis=0, keepdims=True)
        s3 = jnp.sum(dxm * mo_ref[...], axis=0, keepdims=True)
        rows = lax.broadcasted_iota(jnp.int32, (8, D), 0)
        upd = jnp.where(rows == 0, s0, jnp.where(rows == 1, s1, jnp.where(rows == 2, s2,
              jnp.where(rows == 3, s3, 0.0))))

        @pl.when(i == 0)
        def _():
            cs_ref[...] = jnp.zeros_like(cs_ref)

        cs_ref[...] += upd

    blk = pl.BlockSpec((tb, D), lambda i: (i, 0))
    vec = pl.BlockSpec((8, D), lambda i: (0, 0))
    return _pcall(
        body, name="resid_rms2_bwd", grid=(L // tb,),
        in_specs=[blk, blk, blk, blk, blk, vec],
        out_specs=[blk, blk, vec],
        out_shape=[jax.ShapeDtypeStruct((L, D), F32), jax.ShapeDtypeStruct((L, D), BF16),
                   jax.ShapeDtypeStruct((8, D), F32)],
        compiler_params=_params(("arbitrary",)),
    )(xmid, dh_a, dh_b, dy, mo, vecs)


def _loss_head(xmid, f, g2, target):
    L, D = xmid.shape
    tb = _pick(L, (256, 128, 64))

    def body(xm_ref, f_ref, g_ref, t_ref, dy_ref, df_ref, s_ref):
        i = pl.program_id(0)
        fv = f_ref[...]
        g = g_ref[...]
        err = xm_ref[...] + g * fv - t_ref[...]
        dy = err * (1.0 / D)
        dy_ref[...] = dy
        df_ref[...] = (dy * g).astype(BF16)
        s0 = jnp.sum(dy * fv, axis=0, keepdims=True)
        part = 0.5 * jnp.sum(jnp.mean(err * err, axis=-1, keepdims=True), axis=0, keepdims=True)
        rows = lax.broadcasted_iota(jnp.int32, (8, D), 0)
        upd = jnp.where(rows == 0, s0, jnp.where(rows == 1, part, 0.0))

        @pl.when(i == 0)
        def _():
            s_ref[...] = jnp.zeros_like(s_ref)

        s_ref[...] += upd

    blk = pl.BlockSpec((tb, D), lambda i: (i, 0))
    return _pcall(
        body, name="loss_head", grid=(L // tb,),
        in_specs=[blk, blk, pl.BlockSpec((1, D), lambda i: (0, 0)), blk],
        out_specs=[blk, blk, pl.BlockSpec((8, D), lambda i: (0, 0))],
        out_shape=[jax.ShapeDtypeStruct((L, D), F32), jax.ShapeDtypeStruct((L, D), BF16),
                   jax.ShapeDtypeStruct((8, D), F32)],
        compiler_params=_params(("arbitrary",)),
    )(xmid, f, g2, target)


def _gate_cols(D, off):
    tc = _pick(np.gcd(D, off), (512, 256, 128))
    return tc, off // tc


def _merge_fwd(za, zb, p, n_ctx, off_a, off_b):
    L, D = za.shape
    tb = _pick(n_ctx, (256, 128, 64, 32, 16))
    nctx = n_ctx // tb
    tc, oa = _gate_cols(D, off_a)
    _, ob = _gate_cols(D, off_b)
    if off_b % tc:
        raise ValueError("gate column offsets must share a column tile")
    ob = off_b // tc

    def body(za_ref, zb_ref, ga_ref, gb_ref, z_ref):
        z_ref[...] = (_sigmoid(ga_ref[...]) * za_ref[...].astype(F32)
                      + _sigmoid(gb_ref[...]) * zb_ref[...].astype(F32)).astype(BF16)

    blk = pl.BlockSpec((tb, tc), lambda i, j: (i, j))
    return _pcall(
        body, name="merge_fwd", grid=(L // tb, D // tc),
        in_specs=[blk, blk, pl.BlockSpec((tb, tc), lambda i, j: (i + nctx, oa + j)),
                  pl.BlockSpec((tb, tc), lambda i, j: (i + nctx, ob + j))],
        out_specs=blk,
        out_shape=jax.ShapeDtypeStruct((L, D), BF16),
        compiler_params=_params(("parallel", "parallel")),
    )(za, zb, p, p)


def _merge_bwd(dz, za, zb, p, n_ctx, off_a, off_b):
    L, D = za.shape
    T = L + n_ctx
    tb = _pick(n_ctx, (256, 128, 64, 32, 16))
    nctx = n_ctx // tb
    tc = _gate_cols(D, off_a)[0]
    oa, ob = off_a // tc, off_b // tc

    def body(dz_ref, za_ref, zb_ref, ga_ref, gb_ref, dza_ref, dzb_ref, dga_ref, dgb_ref):
        i = pl.program_id(1)

        @pl.when(i < nctx)
        def _():
            dga_ref[...] = jnp.zeros_like(dga_ref)
            dgb_ref[...] = jnp.zeros_like(dgb_ref)

        @pl.when(i >= nctx)
        def _():
            dzv = dz_ref[...].astype(F32)
            sa = _sigmoid(ga_ref[...])
            sb = _sigmoid(gb_ref[...])
            dza_ref[...] = (dzv * sa).astype(BF16)
            dzb_ref[...] = (dzv * sb).astype(BF16)
            dga_ref[...] = (dzv * za_ref[...].astype(F32) * sa * (1.0 - sa)).astype(BF16)
            dgb_ref[...] = (dzv * zb_ref[...].astype(F32) * sb * (1.0 - sb)).astype(BF16)

    lat = pl.BlockSpec((tb, tc), lambda j, i: (jnp.maximum(i - nctx, 0), j))
    allr = pl.BlockSpec((tb, tc), lambda j, i: (i, j))
    return _pcall(
        body, name="merge_bwd", grid=(D // tc, T // tb),
        in_specs=[lat, lat, lat, pl.BlockSpec((tb, tc), lambda j, i: (i, oa + j)),
                  pl.BlockSpec((tb, tc), lambda j, i: (i, ob + j))],
        out_specs=[lat, lat, allr, allr],
        out_shape=[jax.ShapeDtypeStruct((L, D), BF16), jax.ShapeDtypeStruct((L, D), BF16),
                   jax.ShapeDtypeStruct((T, D), BF16), jax.ShapeDtypeStruct((T, D), BF16)],
        compiler_params=_params(("arbitrary", "arbitrary")),
    )(dz, za, zb, p, p)


def _shift_down(u, rows):
    return jnp.where(rows == 0, 0.0, pltpu.roll(u, 1, 0))


def _shift_up(u, rows):
    n = u.shape[0]
    return jnp.where(rows == n - 1, 0.0, pltpu.roll(u, n - 1, 0))


def _convgate_fwd(u1, u3, cw, cb):
    L, F = u1.shape
    tc = _pick(F, (256, 128))

    def body(u1_ref, u3_ref, w_ref, b_ref, a_ref):
        u = u1_ref[...].astype(F32)
        rows = lax.broadcasted_iota(jnp.int32, u.shape, 0)
        cv = _shift_down(u, rows) * w_ref[0:1, :] + u * w_ref[1:2, :] + _shift_up(u, rows) * w_ref[2:3, :] + b_ref[...]
        a_ref[...] = (cv * _sigmoid(cv) * u3_ref[...].astype(F32)).astype(BF16)

    blk = pl.BlockSpec((L, tc), lambda j: (0, j))
    return _pcall(
        body, name="convgate_fwd", grid=(F // tc,),
        in_specs=[blk, blk, pl.BlockSpec((8, tc), lambda j: (0, j)), pl.BlockSpec((1, tc), lambda j: (0, j))],
        out_specs=blk,
        out_shape=jax.ShapeDtypeStruct((L, F), BF16),
        compiler_params=_params(("parallel",)),
    )(u1, u3, cw, cb)


def _convgate_bwd(u1, u3, da, cw, cb):
    L, F = u1.shape
    tc = _pick(F, (256, 128))

    def body(u1_ref, u3_ref, da_ref, w_ref, b_ref, du1_ref, du3_ref, s_ref):
        u = u1_ref[...].astype(F32)
        rows = lax.broadcasted_iota(jnp.int32, u.shape, 0)
        um, up = _shift_down(u, rows), _shift_up(u, rows)
        w0, w1, w2 = w_ref[0:1, :], w_ref[1:2, :], w_ref[2:3, :]
        cv = um * w0 + u * w1 + up * w2 + b_ref[...]
        s = _sigmoid(cv)
        dav = da_ref[...].astype(F32)
        du3_ref[...] = (dav * cv * s).astype(BF16)
        dcv = dav * u3_ref[...].astype(F32) * (s * (1.0 + cv * (1.0 - s)))
        du1_ref[...] = (_shift_up(dcv, rows) * w0 + dcv * w1 + _shift_down(dcv, rows) * w2).astype(BF16)
        r8 = lax.broadcasted_iota(jnp.int32, (8, tc), 0)
        s0 = jnp.sum(dcv * um, axis=0, keepdims=True)
        s1 = jnp.sum(dcv * u, axis=0, keepdims=True)
        s2 = jnp.sum(dcv * up, axis=0, keepdims=True)
        s3 = jnp.sum(dcv, axis=0, keepdims=True)
        s_ref[...] = jnp.where(r8 == 0, s0, jnp.where(r8 == 1, s1, jnp.where(r8 == 2, s2,
                     jnp.where(r8 == 3, s3, 0.0))))

    blk = pl.BlockSpec((L, tc), lambda j: (0, j))
    v8 = pl.BlockSpec((8, tc), lambda j: (0, j))
    return _pcall(
        body, name="convgate_bwd", grid=(F // tc,),
        in_specs=[blk, blk, blk, v8, pl.BlockSpec((1, tc), lambda j: (0, j))],
        out_specs=[blk, blk, v8],
        out_shape=[jax.ShapeDtypeStruct((L, F), BF16), jax.ShapeDtypeStruct((L, F), BF16),
                   jax.ShapeDtypeStruct((8, F), F32)],
        compiler_params=_params(("parallel",)),
    )(u1, u3, da, cw, cb)


def _lower_bound(lbl_ref, d):
    l0, l1 = lbl_ref[d, 0:1, :], lbl_ref[d, 1:2, :]
    m = jnp.maximum(l0, l1)
    e0, e1 = jnp.exp(l0 - m), jnp.exp(l1 - m)
    return e0 / (e0 + e1)


def _chunk_cumsum(x, rev):
    n = x.shape[0]
    r = lax.broadcasted_iota(jnp.int32, x.shape, 0) % CHUNK
    k = 1
    while k < CHUNK:
        if rev:
            x = x + jnp.where(r < CHUNK - k, pltpu.roll(x, n - k, 0), 0.0)
        else:
            x = x + jnp.where(r >= k, pltpu.roll(x, k, 0), 0.0)
        k *= 2
    return x


def _gate_terms(z, lb):
    sg = _sigmoid(z)
    f = lb + (1.0 - lb) * sg
    return sg, f


def _decay_terms(z, lb, rev):
    _, f = _gate_terms(z, lb)
    g = jnp.log(f)
    return 1.0 - f, _chunk_cumsum(g, rev), _chunk_cumsum(g, not rev) - g


def _chunk_total(c, rev):
    return c[0:1, :] if rev else c[CHUNK - 1:CHUNK, :]


def _pair_decay(c, s, rev):
    t = lax.broadcasted_iota(jnp.int32, (CHUNK, 1), 0)
    later = (t <= s) if rev else (t >= s)
    return jnp.where(later, jnp.exp(c - c[s:s + 1, :]), 0.0)


def _scan_chunk(i, n_ctx_chunks, n_chunks, rev):
    if not rev:
        return i
    return jnp.where(i < n_ctx_chunks, n_ctx_chunks - 1 - i, n_chunks + n_ctx_chunks - 1 - i)


def _rows(ci):
    return pl.ds(pl.multiple_of(ci * CHUNK, CHUNK), CHUNK)


def _hgrn_cols(HA):
    return HA // HEAD


def _hgrn_fwd(p, lbl, ng, n_ctx, HA):
    T = p.shape[0]
    L = T - n_ctx
    nh = _hgrn_cols(HA)
    nc, ncc = T // CHUNK, n_ctx // CHUNK

    def body(q_ref, zf_ref, zb_ref, v_ref, og_ref, lbl_ref, ng_ref, ya_ref, o_ref, st_ref,
             c_scr, k_scr, qe_scr, ke_scr, o_scr):
        dirs = ((0, False, zf_ref), (1, True, zb_ref))
        for d, rev, z_ref in dirs:
            k, c, rest = _decay_terms(z_ref[...], _lower_bound(lbl_ref, d), rev)
            c_scr[d] = c
            k_scr[d] = k
            qe_scr[d] = (q_ref[...] * jnp.exp(c)).astype(BF16)
            ke_scr[d] = (k * jnp.exp(rest)).astype(BF16)

        def step(i2, states):
            states = list(states)
            for u in range(HGRN_UNROLL):
                for d, rev, _ in dirs:
                    St = states[d]
                    ci = _scan_chunk(HGRN_UNROLL * i2 + u, ncc, nc, rev)
                    rows = _rows(ci)
                    q, v, c, k = q_ref[rows, :], v_ref[rows, :], c_scr[d, rows, :], k_scr[d, rows, :]
                    st_ref[0, d, ci] = St.astype(BF16)
                    o = jnp.zeros((CHUNK, HEAD), F32)
                    for s in range(CHUNK):
                        E = _pair_decay(c, s, rev)
                        a = jnp.sum(q * E * k[s:s + 1, :], axis=1, keepdims=True)
                        o = o + a * v[s:s + 1, :]
                    o_scr[d, rows, :] = o + _dot_nt(qe_scr[d, rows, :], St.astype(BF16))
                    states[d] = St * jnp.exp(_chunk_total(c, rev)) + _dot_tn(v.astype(BF16), ke_scr[d, rows, :])
            return tuple(states)

        if nc % HGRN_UNROLL:
            raise ValueError("the number of chunks must be a multiple of HGRN_UNROLL")
        zero = jnp.zeros((HEAD, HEAD), F32)
        lax.fori_loop(0, nc // HGRN_UNROLL, step, (zero, zero))

        o = o_scr[0, pl.ds(n_ctx, L), :] + o_scr[1, pl.ds(n_ctx, L), :]
        o_ref[...] = o
        r = lax.rsqrt(jnp.mean(o * o, axis=-1, keepdims=True) + EPS)
        og = og_ref[pl.ds(n_ctx, L), :]
        ya_ref[...] =(o * r * ng_ref[...] * (og * _sigmoid(og))).astype(BF16)

    cb = HA // HEAD
    col = lambda kk: pl.BlockSpec((T, HEAD), lambda h: (0, kk * cb + h))
    return _pcall(
        body, name="hgrn_fwd", grid=(nh,),
        in_specs=[col(0), col(1), col(2), col(3), col(4),
                  pl.BlockSpec((2, 2, HEAD), lambda h: (0, 0, h)), pl.BlockSpec((1, HEAD), lambda h: (0, 0))],
        out_specs=[pl.BlockSpec((L, HEAD), lambda h: (0, h)), pl.BlockSpec((L, HEAD), lambda h: (0, h)),
                   pl.BlockSpec((1, 2, nc, HEAD, HEAD), lambda h: (h, 0, 0, 0, 0))],
        out_shape=[jax.ShapeDtypeStruct((L, HA), BF16), jax.ShapeDtypeStruct((L, HA), F32),
                   jax.ShapeDtypeStruct((nh, 2, nc, HEAD, HEAD), BF16)],
        scratch_shapes=[pltpu.VMEM((2, T, HEAD), F32), pltpu.VMEM((2, T, HEAD), F32),
                        pltpu.VMEM((2, T, HEAD), BF16), pltpu.VMEM((2, T, HEAD), BF16),
                        pltpu.VMEM((2, T, HEAD), F32)],
        compiler_params=_params(("parallel",)),
    )(p, p, p, p, p, lbl, ng)


def _hgrn_bwd(p, lbl, ng, o, dya, st, n_ctx, HA):
    T = p.shape[0]
    L = T - n_ctx
    nh = _hgrn_cols(HA)
    nc, ncc = T // CHUNK, n_ctx // CHUNK

    def body(q_ref, zf_ref, zb_ref, v_ref, og_ref, lbl_ref, ng_ref, o_ref, dya_ref, st_ref,
             dq_ref, dzf_ref, dzb_ref, dv_ref, dog_ref, dlbl_ref, dng_ref,
             do_scr, c_scr, k_scr, qe_scr, ke_scr, dg_scr, dk_scr, dq_scr, dv_scr, row_scr):
        h = pl.program_id(0)
        ov = o_ref[...]
        r = lax.rsqrt(jnp.mean(ov * ov, axis=-1, keepdims=True) + EPS)
        oh = ov * r
        ogv = og_ref[pl.ds(n_ctx, L), :]
        sg_o = _sigmoid(ogv)
        dyv = dya_ref[...]
        ngv = ng_ref[...]
        dog_ref[pl.ds(0, n_ctx), :] = jnp.zeros((n_ctx, HEAD), BF16)
        dog_ref[pl.ds(n_ctx, L), :] = (dyv * oh * ngv * (sg_o * (1.0 + ogv * (1.0 - sg_o)))).astype(BF16)
        don = dyv * (ogv * sg_o)
        dng = jnp.sum(don * oh, axis=0, keepdims=True)
        doh = don * ngv
        do_scr[pl.ds(0, n_ctx), :] = jnp.zeros((n_ctx, HEAD), F32)
        do_scr[pl.ds(n_ctx, L), :] = r * (doh - oh * jnp.mean(doh * oh, axis=-1, keepdims=True))

        @pl.when(h == 0)
        def _():
            dng_ref[...] = jnp.zeros_like(dng_ref)

        dng_ref[0:1, :] += dng

        t16 = lax.broadcasted_iota(jnp.int32, (CHUNK, HEAD), 0)
        dirs = ((0, False, zf_ref, dzf_ref), (1, True, zb_ref, dzb_ref))
        for d, rev, z_ref, _ in dirs:
            k, c, rest = _decay_terms(z_ref[...], _lower_bound(lbl_ref, d), rev)
            c_scr[d] = c
            k_scr[d] = k
            qe_scr[d] = (q_ref[...] * jnp.exp(c)).astype(BF16)
            ke_scr[d] = (k * jnp.exp(rest)).astype(BF16)
        dq_scr[...] = jnp.zeros_like(dq_scr)
        dv_scr[...] = jnp.zeros_like(dv_scr)

        zero = jnp.zeros((HEAD, HEAD), F32)

        def bwd_chunk(i, carry, u):
            new = []
            for (d, rev, _, _), dSt in zip(dirs, carry):
                ci = _scan_chunk(i, ncc, nc, rev)
                rows = _rows(ci)
                q, v, do = q_ref[rows, :], v_ref[rows, :], do_scr[rows, :]
                c, k = c_scr[d, rows, :], k_scr[d, rows, :]
                tot = _chunk_total(c, rev)
                etot = jnp.exp(tot)
                St = st_ref[0, d, ci]
                dSb = dSt.astype(BF16)
                do_b = do.astype(BF16)
                dq_x = _dot(do_b, St) * jnp.exp(c)
                dk_x = _dot(v.astype(BF16), dSb) * jnp.exp(tot - c)
                dv_x = _dot_nt(ke_scr[d, rows, :], dSb)
                dtot = (jnp.sum(St.astype(F32) * dSt, axis=0, keepdims=True) * etot
                        + jnp.sum(k * dk_x, axis=0, keepdims=True))
                dq = jnp.zeros((CHUNK, HEAD), F32)
                for s in range(CHUNK):
                    E = _pair_decay(c, s, rev)
                    XE = E * k[s:s + 1, :]
                    a = jnp.sum(q * XE, axis=1, keepdims=True)
                    da = jnp.sum(do * v[s:s + 1, :], axis=1, keepdims=True)
                    dq = dq + da * XE
                    row_scr[u, d, 0, s:s + 1, :] = jnp.sum(da * q * E, axis=0, keepdims=True)
                    row_scr[u, d, 1, s:s + 1, :] = jnp.sum(a * do, axis=0, keepdims=True)
                dq, dk, dv = dq + dq_x, row_scr[u, d, 0] + dk_x, row_scr[u, d, 1] + dv_x
                dg_scr[d, rows, :] = _chunk_cumsum(q * dq - k * dk, not rev) + dtot
                dk_scr[d, rows, :] = dk
                dq_scr[rows, :] += dq
                dv_scr[rows, :] += dv
                new.append(dSt * etot + _dot_tn(do_b, qe_scr[d, rows, :]))
            return tuple(new)

        def bwd_step(i2, carry):
            for u in range(2):
                carry = bwd_chunk(nc - 1 - (2 * i2 + u), carry, u)
            return carry

        lax.fori_loop(0, nc // 2, bwd_step, (zero, zero))

        for d, _, z_ref, dz_ref in dirs:
            lb = _lower_bound(lbl_ref, d)
            sg, f = _gate_terms(z_ref[...], lb)
            df = dg_scr[d] / f - dk_scr[d]
            dz_ref[...] = (df * (1.0 - lb) * sg * (1.0 - sg)).astype(BF16)
            dl0 = jnp.sum(df * (1.0 - sg), axis=0, keepdims=True) * lb * (1.0 - lb)
            dlbl_ref[d, 0:1, :] = dl0
            dlbl_ref[d, 1:2, :] = -dl0
        dq_ref[...] = dq_scr[...].astype(BF16)
        dv_ref[...] = dv_scr[...].astype(BF16)

    cb = HA // HEAD
    col = lambda kk: pl.BlockSpec((T, HEAD), lambda h: (0, kk * cb + h))
    tcol = pl.BlockSpec((T, HEAD), lambda h: (0, h))
    lcol = pl.BlockSpec((L, HEAD), lambda h: (0, h))
    outs = _pcall(
        body, name="hgrn_bwd", grid=(nh,),
        in_specs=[col(0), col(1), col(2), col(3), col(4),
                  pl.BlockSpec((2, 2, HEAD), lambda h: (0, 0, h)), pl.BlockSpec((1, HEAD), lambda h: (0, 0)),
                  lcol, lcol,
                  pl.BlockSpec((1, 2, nc, HEAD, HEAD), lambda h: (h, 0, 0, 0, 0), pipeline_mode=pl.Buffered(1))],
        out_specs=[tcol, tcol, tcol, tcol, tcol, pl.BlockSpec((2, 2, HEAD), lambda h: (0, 0, h)),
                   pl.BlockSpec((8, HEAD), lambda h: (0, 0))],
        out_shape=[jax.ShapeDtypeStruct((T, HA), BF16)] * 5 + [jax.ShapeDtypeStruct((2, 2, HA), F32),
                                                               jax.ShapeDtypeStruct((8, HEAD), F32)],
        scratch_shapes=[pltpu.VMEM((T, HEAD), F32),
                        pltpu.VMEM((2, T, HEAD), F32), pltpu.VMEM((2, T, HEAD), F32),
                        pltpu.VMEM((2, T, HEAD), BF16), pltpu.VMEM((2, T, HEAD), BF16),
                        pltpu.VMEM((2, T, HEAD), F32), pltpu.VMEM((2, T, HEAD), F32),
                        pltpu.VMEM((T, HEAD), F32), pltpu.VMEM((T, HEAD), F32),
                        pltpu.VMEM((2, 2, 2, CHUNK, HEAD), F32)],
        compiler_params=_params(("arbitrary",)),
    )(p, p, p, p, p, lbl, ng, o, dya, st)
    return outs


def _swap_halves(t, lane):
    q = HEAD // 4
    return jnp.where((lane % (2 * q)) < q, pltpu.roll(t, HEAD - q, 1), pltpu.roll(t, q, 1))


def _qk_norm(t, g):
    r = lax.rsqrt(jnp.mean(t * t, axis=-1, keepdims=True) + EPS)
    return t * r, r


def _rope(t, cos, sin, lane):
    return t * cos + _swap_halves(t, lane) * sin


def _qk_norm_bwd(dy, th, r, g):
    dth = dy * g
    return r * (dth - th * jnp.mean(dth * th, axis=-1, keepdims=True)), jnp.sum(dy * th, axis=0, keepdims=True)


def _rope_bwd(dy, cos, sin, lane):
    return dy * cos + _swap_halves(dy * sin, lane)


def _na_geometry(L):
    n_rows = L // GRID_W
    kr = min(WIN_R, n_rows)
    return n_rows, kr


def _na_prep(q_ref, k_ref, v_ref, gq_ref, gk_ref, cos_ref, sin_ref, qs, ks, vs, n_ctx, L):
    lane = lax.broadcasted_iota(jnp.int32, (L, HEAD), 1)
    cos, sin = cos_ref[...], sin_ref[...]
    qh, _ = _qk_norm(q_ref[pl.ds(n_ctx, L), :], None)
    qs[...] = _rope(qh * gq_ref[...], cos, sin, lane).astype(BF16)
    kh, _ = _qk_norm(k_ref[pl.ds(n_ctx, L), :], None)
    ks[pl.ds(n_ctx, L), :] = _rope(kh * gk_ref[...], cos, sin, lane).astype(BF16)
    kc, _ = _qk_norm(k_ref[pl.ds(0, n_ctx), :], None)
    ks[pl.ds(0, n_ctx), :] = (kc * gk_ref[...]).astype(BF16)
    vs[...] = v_ref[...].astype(BF16)


NA_RB = 4


def _na_band_rows(kr):
    return kr + NA_RB


def _na_scores(i, qs, ks, bias_ref, n_ctx, n_rows, kr):
    scale = HEAD ** -0.5
    kb = _na_band_rows(kr)
    rq = NA_RB * i
    r0 = jnp.clip(rq - WIN_R // 2, 0, n_rows - kb)
    qrows = pl.ds(pl.multiple_of(rq * GRID_W, NA_RB * GRID_W), NA_RB * GRID_W)
    krows = pl.ds(pl.multiple_of(n_ctx + r0 * GRID_W, GRID_W), kb * GRID_W)
    qv = qs[qrows, :]
    sb = _dot_nt(qv, ks[krows, :]) * scale
    band_row = lax.broadcasted_iota(jnp.int32, (GRID_W, kb * GRID_W), 1) // GRID_W
    parts, tiles = [], []
    for u in range(NA_RB):
        r_u = rq + u
        first = jnp.clip(r_u - WIN_R // 2, 0, n_rows - kr) - r0
        idx = [jnp.clip(r0 - r_u + (WIN_R - 1) + 2 * jj, 0, 2 * WIN_R - 1) for jj in range(kb // 2)]
        bias_u = jnp.concatenate([bias_ref[0, t] for t in idx], axis=1)
        inside = (band_row >= first) & (band_row < first + kr)
        parts.append(jnp.where(inside, sb[u * GRID_W:(u + 1) * GRID_W, :] + bias_u, NEG))
        tiles.append(idx)
    sb = jnp.concatenate(parts, axis=0)
    sc = _dot_nt(qv, ks[pl.ds(0, n_ctx), :]) * scale
    m = jnp.maximum(jnp.max(sb, axis=1, keepdims=True), jnp.max(sc, axis=1, keepdims=True))
    eb, ec = jnp.exp(sb - m), jnp.exp(sc - m)
    inv = 1.0 / (jnp.sum(eb, axis=1, keepdims=True) + jnp.sum(ec, axis=1, keepdims=True))
    return eb * inv, ec * inv, qrows, krows, tiles


def _na_fwd(p, bias, gq, gk, cos, sin, n_ctx, off, HB):
    T = p.shape[0]
    L = T - n_ctx
    nh = HB // HEAD
    n_rows, kr = _na_geometry(L)
    ob = off // HEAD

    def body(q_ref, k_ref, v_ref, bias_ref, gq_ref, gk_ref, cos_ref, sin_ref, y_ref, qs, ks, vs):
        _na_prep(q_ref, k_ref, v_ref, gq_ref, gk_ref, cos_ref, sin_ref, qs, ks, vs, n_ctx, L)

        def step(i, carry):
            pb, pc, qrows, krows, _ = _na_scores(i, qs, ks, bias_ref, n_ctx, n_rows, kr)
            y = _dot(pb.astype(BF16), vs[krows, :]) + _dot(pc.astype(BF16), vs[pl.ds(0, n_ctx), :])
            y_ref[qrows, :] = y.astype(BF16)
            return carry

        lax.fori_loop(0, n_rows // NA_RB, step, 0)

    col = lambda kk: pl.BlockSpec((T, HEAD), lambda h: (0, ob + kk * nh + h))
    vec = pl.BlockSpec((1, HEAD), lambda h: (0, 0))
    tab = pl.BlockSpec((L, HEAD), lambda h: (0, 0))
    return _pcall(
        body, name="na_fwd", grid=(nh,),
        in_specs=[col(0), col(1), col(2), pl.BlockSpec((1,) + bias.shape[1:], lambda h: (h, 0, 0, 0)),
                  vec, vec, tab, tab],
        out_specs=pl.BlockSpec((L, HEAD), lambda h: (0, h)),
        out_shape=jax.ShapeDtypeStruct((L, HB), BF16),
        scratch_shapes=[pltpu.VMEM((L, HEAD), BF16), pltpu.VMEM((T, HEAD), BF16), pltpu.VMEM((T, HEAD), BF16)],
        compiler_params=_params(("parallel",)),
    )(p, p, p, bias, gq, gk, cos, sin)


def _na_bwd(p, bias, gq, gk, cos, sin, dyb, n_ctx, off, HB):
    T = p.shape[0]
    L = T - n_ctx
    nh = HB // HEAD
    n_rows, kr = _na_geometry(L)
    ob = off // HEAD
    scale = HEAD ** -0.5

    def body(q_ref, k_ref, v_ref, bias_ref, gq_ref, gk_ref, cos_ref, sin_ref, dy_ref,
             dq_ref, dk_ref, dv_ref, dbias_ref, dg_ref, qs, ks, vs, dqa, dka, dva):
        h = pl.program_id(0)
        _na_prep(q_ref, k_ref, v_ref, gq_ref, gk_ref, cos_ref, sin_ref, qs, ks, vs, n_ctx, L)
        dka[...] = jnp.zeros_like(dka)
        dva[...] = jnp.zeros_like(dva)
        dbias_ref[...] = jnp.zeros_like(dbias_ref)

        crows = pl.ds(0, n_ctx)

        def step(i, carry):
            pb, pc, qrows, krows, tiles = _na_scores(i, qs, ks, bias_ref, n_ctx, n_rows, kr)
            do = dy_ref[qrows, :]
            qv = qs[qrows, :]
            dpb = _dot_nt(do, vs[krows, :])
            dpc = _dot_nt(do, vs[crows, :])
            delta = jnp.sum(pb * dpb, axis=1, keepdims=True) + jnp.sum(pc * dpc, axis=1, keepdims=True)
            dsb = pb * (dpb - delta)
            dsc = pc * (dpc - delta)
            dsb_b, dsc_b = dsb.astype(BF16), dsc.astype(BF16)
            dqa[qrows, :] = (_dot(dsb_b, ks[krows, :]) + _dot(dsc_b, ks[crows, :])) * scale
            dka[krows, :] += _dot_tn(dsb_b, qv) * scale
            dka[crows, :] += _dot_tn(dsc_b, qv) * scale
            dva[krows, :] += _dot_tn(pb.astype(BF16), do)
            dva[crows, :] += _dot_tn(pc.astype(BF16), do)
            for u, idx in enumerate(tiles):
                for jj, t in enumerate(idx):
                    dbias_ref[0, t] += dsb[u * GRID_W:(u + 1) * GRID_W, jj * 2 * GRID_W:(jj + 1) * 2 * GRID_W]
            return carry

        lax.fori_loop(0, n_rows // NA_RB, step, 0)

        lane = lax.broadcasted_iota(jnp.int32, (L, HEAD), 1)
        cos, sin = cos_ref[...], sin_ref[...]
        lat, ctx = pl.ds(n_ctx, L), pl.ds(0, n_ctx)
        gqv, gkv = gq_ref[...], gk_ref[...]
        qh, rq = _qk_norm(q_ref[lat, :], None)
        dq, dgq = _qk_norm_bwd(_rope_bwd(dqa[...], cos, sin, lane), qh, rq, gqv)
        dq_ref[ctx, :] = jnp.zeros((n_ctx, HEAD), BF16)
        dq_ref[lat, :] = dq.astype(BF16)
        kh, rk = _qk_norm(k_ref[lat, :], None)
        dk, dgk = _qk_norm_bwd(_rope_bwd(dka[lat, :], cos, sin, lane), kh, rk, gkv)
        dk_ref[lat, :] = dk.astype(BF16)
        kch, rkc = _qk_norm(k_ref[ctx, :], None)
        dkc, dgkc = _qk_norm_bwd(dka[ctx, :], kch, rkc, gkv)
        dk_ref[ctx, :] = dkc.astype(BF16)
        dv_ref[...] = dva[...].astype(BF16)

        @pl.when(h == 0)
        def _():
            dg_ref[...] = jnp.zeros_like(dg_ref)

        dg_ref[0:1, :] += dgq
        dg_ref[1:2, :] += dgk + dgkc

    col = lambda kk: pl.BlockSpec((T, HEAD), lambda h: (0, ob + kk * nh + h))
    vec = pl.BlockSpec((1, HEAD), lambda h: (0, 0))
    tab = pl.BlockSpec((L, HEAD), lambda h: (0, 0))
    tcol = pl.BlockSpec((T, HEAD), lambda h: (0, h))
    bspec = pl.BlockSpec((1,) + bias.shape[1:], lambda h: (h, 0, 0, 0))
    return _pcall(
        body, name="na_bwd", grid=(nh,),
        in_specs=[col(0), col(1), col(2), bspec, vec, vec, tab, tab, pl.BlockSpec((L, HEAD), lambda h: (0, h))],
        out_specs=[tcol, tcol, tcol, bspec, pl.BlockSpec((8, HEAD), lambda h: (0, 0))],
        out_shape=[jax.ShapeDtypeStruct((T, HB), BF16)] * 3 + [jax.ShapeDtypeStruct(bias.shape, F32),
                                                               jax.ShapeDtypeStruct((8, HEAD), F32)],
        scratch_shapes=[pltpu.VMEM((L, HEAD), BF16), pltpu.VMEM((T, HEAD), BF16), pltpu.VMEM((T, HEAD), BF16),
                        pltpu.VMEM((L, HEAD), F32), pltpu.VMEM((T, HEAD), F32), pltpu.VMEM((T, HEAD), F32)],
        compiler_params=_params(("arbitrary",)),
    )(p, p, p, bias, gq, gk, cos, sin, dyb)


def _bias_tables():
    w = np.arange(GRID_W)
    col_start = np.clip(w - WIN_C // 2, 0, GRID_W - WIN_C)
    col_in = (w[None, :] >= col_start[:, None]) & (w[None, :] < col_start[:, None] + WIN_C)
    dc = np.clip(w[None, :] - w[:, None], -(WIN_C - 1), WIN_C - 1) + WIN_C - 1
    n_pair = 2 * WIN_R
    ridx = np.zeros((n_pair, GRID_W, 2 * GRID_W), np.int32)
    cidx = np.zeros((n_pair, GRID_W, 2 * GRID_W), np.int32)
    valid = np.zeros((n_pair, GRID_W, 2 * GRID_W), bool)
    for i in range(n_pair):
        for half in range(2):
            row = i + half
            sl = slice(half * GRID_W, (half + 1) * GRID_W)
            ridx[i, :, sl] = min(row, 2 * WIN_R - 2)
            cidx[i, :, sl] = dc
            valid[i, :, sl] = col_in & (row <= 2 * WIN_R - 2)
    return ridx, cidx, valid


def _bias_onehot():
    _, cidx, valid = _bias_tables()
    K = GRID_W * 2 * GRID_W
    oh = np.zeros((K, 128), np.float32)
    neg = np.full((1, K), NEG, np.float32)
    for cq in range(GRID_W):
        for ll in range(2 * GRID_W):
            if valid[0, cq, ll]:
                oh[cq * 2 * GRID_W + ll, (ll // GRID_W) * 64 + cidx[0, cq, ll]] = 1.0
                neg[0, cq * 2 * GRID_W + ll] = 0.0
    return oh, neg


def _expand_bias(table):
    H = table.shape[0]
    n_pair, n_dc = 2 * WIN_R, 2 * WIN_C - 1
    tp = jnp.pad(table, ((0, 0), (0, n_pair + 1 - table.shape[1]), (0, 64 - n_dc)))
    t2 = jnp.concatenate([tp[:, :n_pair], tp[:, 1:n_pair + 1]], axis=-1).reshape(H * n_pair, 128)
    oh, neg = _bias_onehot()

    def body(t_ref, oh_ref, neg_ref, o_ref):
        o_ref[...] = lax.dot_general(t_ref[...], oh_ref[...], (((1,), (1,)), ((), ())), precision=HI,
                                     preferred_element_type=F32) + neg_ref[...]

    out = _pcall(body, name="bias_expand", out_shape=jax.ShapeDtypeStruct((H * n_pair, oh.shape[0]), F32),
                         compiler_params=_params())(t2, jnp.asarray(oh), jnp.asarray(neg))
    return out.reshape(H, n_pair, GRID_W, 2 * GRID_W)


def _bias_grad(dbias):
    H = dbias.shape[0]
    n_pair, n_dc = 2 * WIN_R, 2 * WIN_C - 1
    K = GRID_W * 2 * GRID_W
    oh, _ = _bias_onehot()
    flat = dbias.reshape(H * n_pair, K)

    def body(d_ref, oh_ref, o_ref):
        o_ref[...] = jnp.dot(d_ref[...], oh_ref[...], precision=HI, preferred_element_type=F32)

    g = _pcall(body, name="bias_grad", out_shape=jax.ShapeDtypeStruct((H * n_pair, 128), F32),
                       compiler_params=_params())(flat, jnp.asarray(oh))
    g = g.reshape(H, n_pair, 128)
    left, right = g[:, :, :n_dc], g[:, :, 64:64 + n_dc]
    out = left[:, :n_pair - 1]
    return out.at[:, 1:].add(right[:, :n_pair - 2])


def _rope_tables(L):
    pos = np.arange(L)
    row = (pos // GRID_W).astype(np.float32)
    colp = (pos % GRID_W).astype(np.float32)
    half = HEAD // 2
    nf = half // 2
    inv = (ROPE_THETA ** (-np.arange(nf, dtype=np.float32) / nf)).astype(np.float32)

    def tabs(pv):
        ang = pv[:, None] * inv[None, :]
        c, s = np.cos(ang), np.sin(ang)
        return np.concatenate([c, c], axis=1), np.concatenate([-s, s], axis=1)

    cr, sr = tabs(row)
    cc, sc = tabs(colp)
    return (jnp.asarray(np.concatenate([cr, cc], axis=1), F32), jnp.asarray(np.concatenate([sr, sc], axis=1), F32))


def _adamw(w, g, m, v, name, after=None, copy_g=False):
    R, C = w.shape
    tr = _row_tile(R, C)
    c1 = 1.0 - ADAM_B1 ** ADAM_STEP
    c2 = 1.0 - ADAM_B2 ** ADAM_STEP
    deps = [] if after is None else [after]
    n_out = 4 if copy_g else 3

    def body(w_ref, g_ref, m_ref, v_ref, *rest):
        d_ref, mo_ref, vo_ref = rest[len(deps):len(deps) + 3]
        gv = g_ref[...]
        mn = ADAM_B1 * m_ref[...] + (1.0 - ADAM_B1) * gv
        vn = ADAM_B2 * v_ref[...] + (1.0 - ADAM_B2) * (gv * gv)
        mo_ref[...] = mn
        vo_ref[...] = vn
        d_ref[...] = -ADAM_LR * ((mn / c1) / (jnp.sqrt(vn / c2) + ADAM_EPS) + ADAM_WD * w_ref[...])
        if copy_g:
            rest[-1][...] = gv

    blk = pl.BlockSpec((tr, C), lambda i: (i, 0))
    return _pcall(
        body, name=name, grid=(R // tr,),
        in_specs=[blk] * 4 + [_ANY] * len(deps), out_specs=[blk] * n_out,
        out_shape=[jax.ShapeDtypeStruct((R, C), F32)] * n_out,
        compiler_params=_params(("parallel",)),
    )(w, g, m, v, *deps)


PACK_W = 1024


def _pack(parts):
    flat, offs, pos = [], [], 0
    for a in parts:
        n = a.size
        padn = -n % PACK_W
        flat.append(jnp.pad(a.reshape(-1).astype(F32), (0, padn)))
        offs.append((pos, n, a.shape))
        pos += n + padn
    tail = -pos % (8 * PACK_W)
    if tail:
        flat.append(jnp.zeros((tail,), F32))
    return jnp.concatenate(flat).reshape(-1, PACK_W), offs


def _unpack(buf, offs, i):
    pos, n, shape = offs[i]
    return buf.reshape(buf.shape[:-2] + (-1,))[..., pos:pos + n].reshape(buf.shape[:-2] + shape)


def kernel(x, c, ctx, c_ctx, ada_w, ada_b, norm1_g, norm2_g, w_in, hgrn_lb_logits, hgrn_norm_g, na_q_norm_g, na_k_norm_g, na_rel_bias, w_branch_a, w_branch_b, w_out, ffn_w1, ffn_w3, ffn_conv_w, ffn_conv_b, ffn_w2, loss_target, m_c_ctx, m_ada_w, m_ada_b, m_norm1_g, m_norm2_g, m_w_in, m_hgrn_lb_logits, m_hgrn_norm_g, m_na_q_norm_g, m_na_k_norm_g, m_na_rel_bias, m_w_branch_a, m_w_branch_b, m_w_out, m_ffn_w1, m_ffn_w3, m_ffn_conv_w, m_ffn_conv_b, m_ffn_w2, v_c_ctx, v_ada_w, v_ada_b, v_norm1_g, v_norm2_g, v_w_in, v_hgrn_lb_logits, v_hgrn_norm_g, v_na_q_norm_g, v_na_k_norm_g, v_na_rel_bias, v_w_branch_a, v_w_branch_b, v_w_out, v_ffn_w1, v_ffn_w3, v_ffn_conv_w, v_ffn_conv_b, v_ffn_w2):
    weights = dict(c_ctx=c_ctx, ada_w=ada_w, ada_b=ada_b, norm1_g=norm1_g, norm2_g=norm2_g, w_in=w_in,
                   hgrn_lb_logits=hgrn_lb_logits, hgrn_norm_g=hgrn_norm_g, na_q_norm_g=na_q_norm_g,
                   na_k_norm_g=na_k_norm_g, na_rel_bias=na_rel_bias, w_branch_a=w_branch_a, w_branch_b=w_branch_b,
                   w_out=w_out, ffn_w1=ffn_w1, ffn_w3=ffn_w3, ffn_conv_w=ffn_conv_w, ffn_conv_b=ffn_conv_b,
                   ffn_w2=ffn_w2)
    moms = dict(c_ctx=(m_c_ctx, v_c_ctx), ada_w=(m_ada_w, v_ada_w), ada_b=(m_ada_b, v_ada_b),
                norm1_g=(m_norm1_g, v_norm1_g), norm2_g=(m_norm2_g, v_norm2_g), w_in=(m_w_in, v_w_in),
                hgrn_lb_logits=(m_hgrn_lb_logits, v_hgrn_lb_logits), hgrn_norm_g=(m_hgrn_norm_g, v_hgrn_norm_g),
                na_q_norm_g=(m_na_q_norm_g, v_na_q_norm_g), na_k_norm_g=(m_na_k_norm_g, v_na_k_norm_g),
                na_rel_bias=(m_na_rel_bias, v_na_rel_bias), w_branch_a=(m_w_branch_a, v_w_branch_a),
                w_branch_b=(m_w_branch_b, v_w_branch_b), w_out=(m_w_out, v_w_out), ffn_w1=(m_ffn_w1, v_ffn_w1),
                ffn_w3=(m_ffn_w3, v_ffn_w3), ffn_conv_w=(m_ffn_conv_w, v_ffn_conv_w),
                ffn_conv_b=(m_ffn_conv_b, v_ffn_conv_b), ffn_w2=(m_ffn_w2, v_ffn_w2))
    order = list(weights)

    L, D = x.shape[1], x.shape[2]
    N = ctx.shape[1]
    T = N + L
    HA = w_branch_a.shape[1]
    HB = w_branch_b.shape[1]
    F = ffn_conv_b.shape[1]
    IN = 5 * HA + 3 * HB + 2 * D
    n_ada = ada_w.shape[2]
    ix, iy, ic = _pos()
    chip = 2 * ix + iy
    dev = 2 * chip + ic

    _PENDING.clear()
    pk0, offs0 = _pack([c[0], hgrn_lb_logits, ffn_conv_w[0]])
    g0 = _allgather8(pk0, "gather_small0")
    c_all = _unpack(g0, offs0, 0)
    lbl_parts = _unpack(g0, offs0, 1)
    lbl = jnp.concatenate([lbl_parts[2 * j] for j in range(N_CHIP)], axis=-1)
    cw_parts = _unpack(g0, offs0, 2)
    cw = jnp.concatenate([cw_parts[2 * j] for j in range(N_CHIP)], axis=-1)
    cw8 = jnp.pad(cw, ((0, 5), (0, 0)))

    cs = jnp.concatenate([c_all, c_ctx[None, :], jnp.zeros((7, D), F32)], axis=0)
    ada_b_mine = lax.dynamic_slice(ada_b, (0, chip * n_ada), (1, n_ada))
    mod_mine = _ada_fwd(cs, ada_w[0], ada_b_mine)
    gm = _allgather8(mod_mine, "gather_mod")
    mod = jnp.concatenate([gm[2 * j] for j in range(N_CHIP)], axis=-1)
    mod_l = lax.dynamic_slice(mod, (dev, 0), (1, N_MOD * D)).reshape(N_MOD, D)
    mod_c = mod[8].reshape(N_MOD, D)
    sh1, sc1, g1, sh2, sc2, g2 = [mod_l[i:i + 1] for i in range(N_MOD)]
    shift1 = jnp.concatenate([mod_c[0:1], sh1], axis=0)
    scale1 = jnp.concatenate([mod_c[1:2], sc1], axis=0)

    shards = [w_in[0], w_branch_a[0], w_branch_b[0], w_out[0], ffn_w1[0], ffn_w3[0], ffn_w2[0]]
    names = ["w_in", "w_a", "w_b", "w_out", "w1", "w3", "w2"]
    slots = [_cast_bf16_slot(s, "cast_" + nm) for s, nm in zip(shards, names)]
    gat_in = _gather_start("in", slots[0:1], gm)
    gat_mix = _gather_start("mix", slots[1:4])
    gat_ffn = _gather_start("ffn", slots[4:7])

    xall = jnp.concatenate([ctx[0], x[0]], axis=0)
    h_all = _rms1_fwd(xall, norm1_g, shift1, scale1, N)
    gat_in = _gather_mid(gat_in, h_all)
    (Win,) = _gather_finish(gat_in, h_all)
    p = _mm_nn(h_all, Win, F32, "mm_p")
    gat_mix = _gather_mid(gat_mix, p)
    y_a, o_a, st_a = _hgrn_fwd(p, lbl, hgrn_norm_g, N, HA)
    Wa, Wb, Wo = _gather_finish(gat_mix, y_a)
    Wo = Wo.reshape(1, D, D)
    bias = _expand_bias(na_rel_bias[0])
    cos, sin = _rope_tables(L)
    off_na = 5 * HA
    y_b = _na_fwd(p, bias, na_q_norm_g, na_k_norm_g, cos, sin, N, off_na, HB)
    gat_ffn = _gather_mid(gat_ffn, y_b)
    za = _mm_nn(y_a, Wa, BF16, "mm_za")
    zb = _mm_nn(y_b, Wb, BF16, "mm_zb")
    off_ga, off_gb = 5 * HA + 3 * HB, 5 * HA + 3 * HB + D
    z = _merge_fwd(za, zb, p, N, off_ga, off_gb)
    mo = _mm_nn(z, Wo, F32, "mm_mo")
    vec2 = jnp.concatenate([g1, norm2_g, sh2, sc2, jnp.zeros((4, D), F32)], axis=0)
    x_mid, h2 = _resid_rms2_fwd(x[0], mo, vec2)
    W1, W3, W2 = _gather_finish(gat_ffn, h2)
    W2 = W2.reshape(1, F, D)
    u1 = _mm_nn(h2, W1, BF16, "mm_u1")
    u3 = _mm_nn(h2, W3, BF16, "mm_u3")
    a = _convgate_fwd(u1, u3, cw8, ffn_conv_b)
    f = _mm_nn(a, W2, F32, "mm_f")
    dy, df, s_loss = _loss_head(x_mid, f, g2, loss_target[0])
    loss = lax.psum(s_loss[1, 0], ("x", "y", "c"))
    d_g2 = s_loss[0:1]

    gW2 = _mm_tn(a, df, 1, "mm_gw2").reshape(N_CHIP, F // N_CHIP, D)
    da = _mm_nt(df, W2, BF16, "mm_da")
    du1, du3, s_conv = _convgate_bwd(u1, u3, da, cw8, ffn_conv_b)
    gW1 = _mm_tn(h2, du1, N_CHIP, "mm_gw1")
    gW3 = _mm_tn(h2, du3, N_CHIP, "mm_gw3")
    rs_ffn = _rs_start("ffn", [gW2, gW1, gW3])
    dh2a = _mm_nt(du1, W1, F32, "mm_dh2a")
    dh2b = _mm_nt(du3, W3, F32, "mm_dh2b")
    rs_ffn = _rs_scatter(rs_ffn, dh2b)
    dxm, dmo, s_rms2 = _resid_rms2_bwd(x_mid, dh2a, dh2b, dy, mo, vec2)
    gWo = _mm_tn(z, dmo, 1, "mm_gwo").reshape(N_CHIP, D // N_CHIP, D)
    dz = _mm_nt(dmo, Wo, BF16, "mm_dz")
    dza, dzb, dga, dgb = _merge_bwd(dz, za, zb, p, N, off_ga, off_gb)
    gWa = _mm_tn(y_a, dza, N_CHIP, "mm_gwa")
    gWb = _mm_tn(y_b, dzb, N_CHIP, "mm_gwb")
    rs_mix = _rs_start("mix", [gWo, gWa, gWb])
    dya = _mm_nt(dza, Wa, F32, "mm_dya")
    dyb = _mm_nt(dzb, Wb, BF16, "mm_dyb")
    rs_mix = _rs_scatter(rs_mix, dyb)
    dq_a, dzf, dzbk, di_a, dog, dlbl, s_ng = _hgrn_bwd(p, lbl, hgrn_norm_g, o_a, dya, st_a, N, HA)
    rs_ffn = _rs_join(rs_ffn, dq_a)
    dq_n, dk_n, dv_n, dbias, s_qk = _na_bwd(p, bias, na_q_norm_g, na_k_norm_g, cos, sin, dyb, N, off_na, HB)
    rs_mix = _rs_join(rs_mix, dq_n)
    dp = jnp.concatenate([dq_a, dzf, dzbk, di_a, dog, dq_n, dk_n, dv_n, dga, dgb], axis=1)
    gWin = _mm_tn(h_all, dp, N_CHIP, "mm_gwin")
    rs_in = _rs_start("in", [gWin])
    dh = _mm_nt(dp, Win, F32, "mm_dh")
    grad_x, s_rms1 = _rms1_bwd(xall, dh, dxm, norm1_g, scale1, N)
    d_table = _bias_grad(dbias)

    zD = jnp.zeros((1, D), F32)
    dmod_l = jnp.concatenate([s_rms1[2:3], s_rms1[3:4], s_rms2[3:4], s_rms2[0:1], s_rms2[1:2], d_g2], axis=0)
    dmod_c = jnp.concatenate([s_rms1[0:1], s_rms1[1:2], zD, zD, zD, zD], axis=0)
    pk1, offs1 = _pack([dmod_l, dmod_c, s_rms1[4], s_rms2[2], dlbl, s_ng[0], s_qk[0], s_qk[1], d_table,
                        s_conv[0:3], s_conv[3]])
    g1all = _allgather8(pk1, "gather_small1")
    tot1 = _sum8(g1all, "sum_small1")
    dmod_rows = _unpack(g1all, offs1, 0).reshape(N_DEV, N_MOD * D)
    dmod_c_tot = _unpack(tot1, offs1, 1).reshape(1, N_MOD * D)
    dmod16 = jnp.concatenate([dmod_rows, dmod_c_tot, jnp.zeros((7, N_MOD * D), F32)], axis=0)
    dmod16_mine = lax.dynamic_slice(dmod16, (0, chip * n_ada), (16, n_ada))
    g_ada_w, dact = _ada_bwd(cs, ada_w[0], dmod16_mine)
    pk2, offs2 = _pack([dact[8]])
    g2all = _allgather8(pk2, "gather_small2")
    dact_rows = _unpack(g2all, offs2, 0)
    dact_sel = jnp.concatenate([dact_rows[2 * j][None] for j in range(N_CHIP)] + [jnp.zeros((4, D), F32)], axis=0)

    grads = {}
    grads["ada_w"] = g_ada_w[None]
    grads["ada_b"] = (_unpack(tot1, offs1, 0) + _unpack(tot1, offs1, 1)).reshape(1, N_MOD * D)
    grads["norm1_g"] = _unpack(tot1, offs1, 2)[None]
    grads["norm2_g"] = _unpack(tot1, offs1, 3)[None]
    g_lbl = _unpack(tot1, offs1, 4)
    n_lb = HA // N_CHIP
    grads["hgrn_lb_logits"] = lax.dynamic_slice(g_lbl, (0, 0, chip * n_lb), (2, 2, n_lb))
    grads["hgrn_norm_g"] = _unpack(tot1, offs1, 5)[None]
    grads["na_q_norm_g"] = _unpack(tot1, offs1, 6)[None]
    grads["na_k_norm_g"] = _unpack(tot1, offs1, 7)[None]
    grads["na_rel_bias"] = _unpack(tot1, offs1, 8)[None]
    g_cw = _unpack(tot1, offs1, 9)
    n_f = F // N_CHIP
    grads["ffn_conv_w"] = lax.dynamic_slice(g_cw, (0, chip * n_f), (3, n_f))[None]
    grads["ffn_conv_b"] = _unpack(tot1, offs1, 10)[None]

    g_c_ctx = _dsilu_rows(dact_sel, c_ctx[None, :], "grad_c_ctx")
    grads["c_ctx"] = g_c_ctx[0]

    rs_in = _rs_scatter(rs_in, g_c_ctx)
    big_names = ["ada_w", "w_in", "w_branch_a", "w_branch_b", "w_out", "ffn_w1", "ffn_w3", "ffn_w2"]
    small_names = [n for n in order if n not in big_names]
    delta, new_m, new_v = {}, {}, {}

    def update(nm, after=None):
        reduced = nm != "ada_w"
        d_, m_, v_, *g_ = _adamw(weights[nm][0], grads[nm][0], moms[nm][0][0], moms[nm][1][0], "adamw_" + nm,
                                 after, copy_g=reduced)
        delta[nm], new_m[nm], new_v[nm] = d_[None], m_[None], v_[None]
        if reduced:
            grads[nm] = g_[0][None]
        return d_

    last = update("ada_w")
    for nm, g in zip(["ffn_w2", "ffn_w1", "ffn_w3"], _rs_finish(rs_ffn, last)):
        grads[nm] = g[None]
        last = update(nm, last)
    for nm, g in zip(["w_out", "w_branch_a", "w_branch_b"], _rs_finish(rs_mix, last)):
        grads[nm] = g[None]
        last = update(nm, last)
    rs_in = _rs_join(rs_in, last)
    grads["w_in"] = _rs_finish(rs_in, last)[0][None]
    update("w_in")
    pw, offw = _pack([weights[n] for n in small_names])
    pg, _ = _pack([grads[n] for n in small_names])
    pm, _ = _pack([moms[n][0] for n in small_names])
    pv, _ = _pack([moms[n][1] for n in small_names])
    d_, m_, v_ = _adamw(pw, pg, pm, pv, "adamw_small")
    for i, nm in enumerate(small_names):
        delta[nm], new_m[nm], new_v[nm] = _unpack(d_, offw, i), _unpack(m_, offw, i), _unpack(v_, offw, i)

    return (loss, grad_x[None], *[grads[n] for n in order], *[delta[n] for n in order],
            *[new_m[n] for n in order], *[new_v[n] for n in order])


def _dsilu_rows(v, cv, name):
    D = v.shape[1]

    def body(v_ref, c_ref, o_ref):
        t = c_ref[...]
        s = _sigmoid(t)
        o_ref[...] = (((v_ref[0:1, :] + v_ref[1:2, :]) + v_ref[2:3, :]) + v_ref[3:4, :]) * (s * (1.0 + t * (1.0 - s)))

    return _pcall(body, name=name, out_shape=jax.ShapeDtypeStruct((1, D), F32),
                          compiler_params=_params())(v, cv)
```

```python
import functools

import numpy as np
import jax
import jax.numpy as jnp
from jax import lax
from jax.experimental import pallas as pl
from jax.experimental.pallas import tpu as pltpu

F32 = jnp.float32
BF16 = jnp.bfloat16
MESH = pl.DeviceIdType.MESH

HEAD = 128
GRID_W = 64
WIN_R = 8
WIN_C = 16
ROPE_THETA = 10000.0
EPS = 1e-6
N_MOD = 6
CHUNK = 16
HGRN_UNROLL = 4
ADAM_LR = 0.001
ADAM_B1 = 0.9
ADAM_B2 = 0.999
ADAM_EPS = 1e-08
ADAM_WD = 0.01
ADAM_STEP = 10
NEG = -1e30
VMEM_LIMIT = 56 * 1024 * 1024
N_DEV = 8
N_CHIP = 4
HI = lax.Precision.HIGHEST


def _pick(n, cands):
    for c in cands:
        if n % c == 0:
            return c
    return n


def _row_tile(rows, cols, target_bytes=1 << 20):
    want = max(16, target_bytes // (4 * cols))
    for t in (512, 256, 128, 64, 32, 16, 8):
        if t <= want and rows % t == 0:
            return t
    return rows


def _params(sem=None):
    return pltpu.CompilerParams(dimension_semantics=sem, vmem_limit_bytes=VMEM_LIMIT)


def _dot(a, b):
    return jnp.dot(a, b, preferred_element_type=F32)


def _dot_nt(a, b):
    return lax.dot_general(a, b, (((1,), (1,)), ((), ())), preferred_element_type=F32)


def _dot_tn(a, b):
    return lax.dot_general(a, b, (((0,), (0,)), ((), ())), preferred_element_type=F32)


def _sigmoid(x):
    return 1.0 / (1.0 + jnp.exp(-x))


def _col_tile(n):
    return n if n <= 1536 else _pick(n, (1024, 768, 512, 384, 256, 128))


def _mm_nn(x, w3, out_dtype, name):
    M, K = x.shape
    S, _, n = w3.shape
    tm = _pick(M, (768, 512, 256, 128, 64))
    tn = _col_tile(n)
    nb = n // tn

    def body(x_ref, w_ref, o_ref):
        o_ref[...] = _dot(x_ref[...].astype(BF16), w_ref[0]).astype(o_ref.dtype)

    return _pcall(
        body, name=name, grid=(M // tm, S * nb),
        in_specs=[pl.BlockSpec((tm, K), lambda i, j: (i, 0)),
                  pl.BlockSpec((1, K, tn), lambda i, j: (j // nb, 0, j % nb))],
        out_specs=pl.BlockSpec((tm, tn), lambda i, j: (i, j)),
        out_shape=jax.ShapeDtypeStruct((M, S * n), out_dtype),
        compiler_params=_params(("parallel", "parallel")),
    )(x, w3)


def _mm_nt(dy, w3, out_dtype, name):
    M = dy.shape[0]
    S, K, n = w3.shape
    tm = _pick(M, (768, 512, 256, 128, 64))
    tk = K if K <= 2048 else _pick(K, (1408, 1024, 512, 256, 128))
    tc = n if n <= 2048 else _col_tile(n)
    nb = n // tc
    nsteps = S * nb

    def body(dy_ref, w_ref, o_ref, acc_ref):
        s = pl.program_id(2)

        @pl.when(s == 0)
        def _():
            acc_ref[...] = jnp.zeros_like(acc_ref)

        acc_ref[...] += _dot_nt(dy_ref[...].astype(BF16), w_ref[0])

        @pl.when(s == nsteps - 1)
        def _():
            o_ref[...] = acc_ref[...].astype(o_ref.dtype)

    return _pcall(
        body, name=name, grid=(M // tm, K // tk, nsteps),
        in_specs=[pl.BlockSpec((tm, tc), lambda i, k, s: (i, s)),
                  pl.BlockSpec((1, tk, tc), lambda i, k, s: (s // nb, k, s % nb))],
        out_specs=pl.BlockSpec((tm, tk), lambda i, k, s: (i, k)),
        out_shape=jax.ShapeDtypeStruct((M, K), out_dtype),
        scratch_shapes=[pltpu.VMEM((tm, tk), F32)],
        compiler_params=_params(("parallel", "parallel", "arbitrary")),
    )(dy, w3)


def _mm_tn(x, dy, S, name):
    M, K = x.shape
    n = dy.shape[1] // S
    tk = _pick(K, (512, 256, 128))
    tn = _col_tile(n)
    nb = n // tn

    def body(x_ref, dy_ref, o_ref):
        o_ref[0] = _dot_tn(x_ref[...].astype(BF16), dy_ref[...].astype(BF16)).astype(BF16)

    return _pcall(
        body, name=name, grid=(S * nb, K // tk),
        in_specs=[pl.BlockSpec((M, tk), lambda j, k: (0, k)),
                  pl.BlockSpec((M, tn), lambda j, k: (0, j))],
        out_specs=pl.BlockSpec((1, tk, tn), lambda j, k: (j // nb, k, j % nb)),
        out_shape=jax.ShapeDtypeStruct((S, K, n), BF16),
        compiler_params=_params(("parallel", "parallel")),
    )(x, dy)


def _chip_index():
    return (2 * lax.axis_index("x") + lax.axis_index("y")).astype(jnp.int32).reshape(1)


def _cast_bf16_slot(w, name):
    R, C = w.shape
    tr = _row_tile(R, C, 2 << 20)

    def body(j_ref, w_ref, o_ref):
        o_ref[0] = w_ref[...].astype(BF16)

    return _pcall(
        body, name=name,
        grid_spec=pltpu.PrefetchScalarGridSpec(
            num_scalar_prefetch=1, grid=(R // tr,),
            in_specs=[pl.BlockSpec((tr, C), lambda i, j_ref: (i, 0))],
            out_specs=pl.BlockSpec((1, tr, C), lambda i, j_ref: (j_ref[0], i, 0))),
        out_shape=jax.ShapeDtypeStruct((N_CHIP, R, C), BF16),
        compiler_params=_params(("parallel",)),
    )(_chip_index(), w)


def _pos():
    return lax.axis_index("x"), lax.axis_index("y"), lax.axis_index("c")


def _other_chips(x, y):
    return [(x, 1 - y), (1 - x, y), (1 - x, 1 - y)]


def _allgather8(v, name):
    R, C = v.shape

    def body(x_ref, out_ref, send_sems, recv_sems, local_sem):
        x, y, c = _pos()
        me, sibling = (x, y, c), (x, y, 1 - c)
        chips = _other_chips(x, y)

        def slot(px, py, pc):
            return out_ref.at[4 * px + 2 * py + pc]

        def copy(k, block, to, src=None):
            return pltpu.make_async_remote_copy(
                src_ref=slot(*block) if src is None else src, dst_ref=slot(*block),
                send_sem=send_sems.at[k], recv_sem=recv_sems.at[k], device_id=to, device_id_type=MESH)

        mine = pltpu.make_async_copy(x_ref, slot(*me), local_sem)
        mine.start()
        first = [copy(0, me, sibling, src=x_ref)]
        first += [copy(1 + j, me, (*chip, c), src=x_ref) for j, chip in enumerate(chips)]
        for cp in first:
            cp.start()
        passed = [copy(4 + j, (*chip, c), sibling) for j, chip in enumerate(chips)]
        for j, chip in enumerate(chips):
            copy(1 + j, (*chip, c), me).wait_recv()
            passed[j].start()
        copy(0, sibling, me).wait_recv()
        for j, chip in enumerate(chips):
            copy(4 + j, (*chip, 1 - c), me).wait_recv()
        for cp in first + passed:
            cp.wait_send()
        mine.wait()

    return _pcall(
        body, name=name,
        out_shape=jax.ShapeDtypeStruct((N_DEV, R, C), v.dtype),
        in_specs=[pl.BlockSpec(memory_space=pltpu.VMEM)],
        out_specs=pl.BlockSpec(memory_space=pltpu.VMEM),
        scratch_shapes=[pltpu.SemaphoreType.DMA((7,)), pltpu.SemaphoreType.DMA((7,)), pltpu.SemaphoreType.DMA],
        compiler_params=pltpu.CompilerParams(vmem_limit_bytes=VMEM_LIMIT),
    )(v)


_HBM = pl.BlockSpec(memory_space=pltpu.HBM)
_SEM = pl.BlockSpec(memory_space=pltpu.SEMAPHORE)
_ANY = pl.BlockSpec(memory_space=pl.ANY)
_EFFECT = pltpu.SideEffectType.DATAFLOW_SIDE_EFFECTING
_PENDING = []


def _pcall(body, **kw):
    def run(*operands):
        if not _PENDING or "in_specs" not in kw:
            return pl.pallas_call(body, **kw)(*operands)
        deps = list(_PENDING)
        n = len(operands)

        def tied(*refs):
            return body(*refs[:n], *refs[n + len(deps):])

        return pl.pallas_call(tied, **{**kw, "in_specs": list(kw["in_specs"]) + [_ANY] * len(deps)})(*operands, *deps)
    return run


def _copies(plan, refs, send_sems, recv_sems):
    return [pltpu.make_async_remote_copy(src_ref=src, dst_ref=dst, send_sem=send_sems.at[k], recv_sem=recv_sems.at[k],
                                         device_id=dev, device_id_type=MESH)
            for k, (src, dst, dev) in enumerate(plan(refs))]


def _xfer_start(name, bufs, plan, n_copies, after=None):
    n = len(bufs)
    deps = list(_PENDING) + ([after] if after is not None else [])
    nd = len(deps)

    def body(*refs):
        for cp in _copies(plan, refs[:n], refs[n + nd], refs[n + nd + 1]):
            cp.start()
        refs[-1][...] = jnp.zeros_like(refs[-1])

    outs = pl.pallas_call(
        body, name=name,
        out_shape=(pltpu.SemaphoreType.DMA((n_copies,)), pltpu.SemaphoreType.DMA((n_copies,)),
                   *[pltpu.HBM(b.shape, b.dtype) for b in bufs], jax.ShapeDtypeStruct((8, 128), F32)),
        in_specs=[_HBM] * n + [_ANY] * nd,
        out_specs=(_SEM, _SEM, *[_HBM] * n, pl.BlockSpec(memory_space=pltpu.VMEM)),
        input_output_aliases={t: 2 + t for t in range(n)},
        compiler_params=pltpu.CompilerParams(has_side_effects=_EFFECT),
    )(*[pltpu.with_memory_space_constraint(b, pltpu.HBM) for b in bufs], *deps)
    _PENDING[:] = [outs[-1]]
    return (outs[0], outs[1]), list(outs[2:2 + n])


def _xfer_wait(name, sems, bufs, plan, after):
    n = len(bufs)
    after = tuple(after) if isinstance(after, (tuple, list)) else (after,)

    def body(*refs):
        cps = _copies(plan, refs[:n], refs[n], refs[n + 1])
        for cp in cps:
            cp.wait_send()
        for cp in cps:
            cp.wait_recv()

    outs = pl.pallas_call(
        body, name=name,
        out_shape=tuple(pltpu.HBM(b.shape, b.dtype) for b in bufs),
        in_specs=[_HBM] * n + [_SEM, _SEM] + [_ANY] * len(after),
        out_specs=tuple([_HBM] * n),
        input_output_aliases={t: t for t in range(n)},
        compiler_params=pltpu.CompilerParams(has_side_effects=_EFFECT),
    )(*bufs, sems[0], sems[1], *after)
    return list(outs)


def _half(ref_rows, hc):
    h = ref_rows // 2
    return pl.ds(hc * h, h)


def _plan_gather_ici(bufs):
    x, y, c = _pos()
    j = 2 * x + y
    return [(b.at[j, _half(b.shape[1], c)], b.at[j, _half(b.shape[1], c)], (*chip, c))
            for b in bufs for chip in _other_chips(x, y)]


def _plan_gather_d2d(bufs):
    x, y, c = _pos()
    out = []
    for b in bufs:
        for chip in _other_chips(x, y):
            blk = b.at[2 * chip[0] + chip[1], _half(b.shape[1], c)]
            out.append((blk, blk, (x, y, 1 - c)))
    return out


def _plan_pair_swap(n):
    def plan(bufs):
        x, y, c = _pos()
        return [(g.at[:, _half(g.shape[1], 1 - c)], land, (x, y, 1 - c)) for g, land in zip(bufs[:n], bufs[n:])]
    return plan


def _plan_chip_scatter(n):
    def plan(bufs):
        x, y, c = _pos()
        return [(p.at[2 * chip[0] + chip[1]], land.at[k], (*chip, c))
                for p, land in zip(bufs[:n], bufs[n:]) for k, chip in enumerate(_other_chips(x, y))]
    return plan


def _plan_pair_join(bufs):
    x, y, c = _pos()
    return [(b.at[_half(b.shape[0], c)], b.at[_half(b.shape[0], c)], (x, y, 1 - c)) for b in bufs]


def _empty_hbm(shape, dtype):
    return pltpu.with_memory_space_constraint(lax.empty(shape, dtype), pltpu.HBM)


def _gather_start(tag, bufs, after=None):
    sems, bufs = _xfer_start(f"gather_ici_start_{tag}", bufs, _plan_gather_ici, 3 * len(bufs), after)
    return dict(tag=tag, sems=sems, bufs=bufs)


def _gather_mid(st, after):
    tag = st["tag"]
    bufs = _xfer_wait(f"gather_ici_wait_{tag}", st["sems"], st["bufs"], _plan_gather_ici, after)
    sems, bufs = _xfer_start(f"gather_d2d_start_{tag}", bufs, _plan_gather_d2d, 3 * len(bufs))
    return dict(tag=tag, sems=sems, bufs=bufs)


def _gather_finish(st, after):
    return _xfer_wait(f"gather_d2d_wait_{st['tag']}", st["sems"], st["bufs"], _plan_gather_d2d, after)


def _pair_add(g, r, name):
    S, R, C = g.shape
    h = R // 2
    tr = _row_tile(h, C)
    nb = h // tr

    def body(c_ref, g_ref, r_ref, o_ref):
        o_ref[...] = (g_ref[...].astype(F32) + r_ref[...].astype(F32)).astype(BF16)

    return _pcall(
        body, name=name,
        grid_spec=pltpu.PrefetchScalarGridSpec(
            num_scalar_prefetch=1, grid=(S, nb),
            in_specs=[pl.BlockSpec((1, tr, C), lambda s, i, c_ref: (s, c_ref[0] * nb + i, 0)),
                      pl.BlockSpec((1, tr, C), lambda s, i, c_ref: (s, i, 0))],
            out_specs=pl.BlockSpec((1, tr, C), lambda s, i, c_ref: (s, i, 0))),
        out_shape=jax.ShapeDtypeStruct((S, h, C), BF16),
        compiler_params=_params(("parallel", "parallel")),
    )(lax.axis_index("c").astype(jnp.int32).reshape(1), g, r)


def _chip_sum(p, rb, name):
    S, h, C = p.shape
    tr = _row_tile(h, C)
    nb = h // tr
    jc = jnp.concatenate([_chip_index(), lax.axis_index("c").astype(jnp.int32).reshape(1)])

    def body(jc_ref, p_ref, r_ref, o_ref):
        o_ref[...] = ((p_ref[0].astype(F32) + r_ref[0].astype(F32)) + r_ref[1].astype(F32)) + r_ref[2].astype(F32)

    return _pcall(
        body, name=name,
        grid_spec=pltpu.PrefetchScalarGridSpec(
            num_scalar_prefetch=1, grid=(nb,),
            in_specs=[pl.BlockSpec((1, tr, C), lambda i, jc_ref: (jc_ref[0], i, 0)),
                      pl.BlockSpec((3, tr, C), lambda i, jc_ref: (0, i, 0))],
            out_specs=pl.BlockSpec((tr, C), lambda i, jc_ref: (jc_ref[1] * nb + i, 0))),
        out_shape=jax.ShapeDtypeStruct((2 * h, C), F32),
        compiler_params=_params(("parallel",)),
    )(jc, p, rb)


def _rs_start(tag, gs):
    n = len(gs)
    lands = [_empty_hbm((g.shape[0], g.shape[1] // 2, g.shape[2]), g.dtype) for g in gs]
    sems, bufs = _xfer_start(f"rs_swap_start_{tag}", list(gs) + lands, _plan_pair_swap(n), n)
    return dict(tag=tag, n=n, sems=sems, bufs=bufs)


def _rs_scatter(st, after):
    tag, n = st["tag"], st["n"]
    bufs = _xfer_wait(f"rs_swap_wait_{tag}", st["sems"], st["bufs"], _plan_pair_swap(n), after)
    ps = [_pair_add(g, r, f"rs_pair_add_{tag}{t}") for t, (g, r) in enumerate(zip(bufs[:n], bufs[n:]))]
    lands = [_empty_hbm((3,) + p.shape[1:], p.dtype) for p in ps]
    sems, bufs = _xfer_start(f"rs_scatter_start_{tag}", ps + lands, _plan_chip_scatter(n), 3 * n)
    return dict(tag=tag, n=n, sems=sems, bufs=bufs)


def _rs_join(st, after):
    tag, n = st["tag"], st["n"]
    bufs = _xfer_wait(f"rs_scatter_wait_{tag}", st["sems"], st["bufs"], _plan_chip_scatter(n), after)
    fs = [_chip_sum(p, rb, f"rs_chip_sum_{tag}{t}") for t, (p, rb) in enumerate(zip(bufs[:n], bufs[n:]))]
    sems, bufs = _xfer_start(f"rs_join_start_{tag}", fs, _plan_pair_join, n)
    return dict(tag=tag, n=n, sems=sems, bufs=bufs)


def _rs_finish(st, after):
    return _xfer_wait(f"rs_join_wait_{st['tag']}", st["sems"], st["bufs"], _plan_pair_join, after)


def _sum8(g, name):
    _, R, C = g.shape

    def body(g_ref, o_ref):
        acc = g_ref[0]
        for d in range(1, N_DEV):
            acc = acc + g_ref[d]
        o_ref[...] = acc

    return _pcall(body, name=name, out_shape=jax.ShapeDtypeStruct((R, C), F32),
                          compiler_params=_params())(g)


def _ada_fwd(cs, w, b):
    D, n = w.shape
    tn = _pick(n, (512, 384, 256, 128))

    def body(c_ref, w_ref, b_ref, o_ref):
        cv = c_ref[...]
        a = (cv * _sigmoid(cv)).astype(BF16)
        o_ref[...] = _dot(a, w_ref[...].astype(BF16)) + b_ref[...]

    return _pcall(
        body, name="ada_fwd", grid=(n // tn,),
        in_specs=[pl.BlockSpec((16, D), lambda j: (0, 0)), pl.BlockSpec((D, tn), lambda j: (0, j)),
                  pl.BlockSpec((1, tn), lambda j: (0, j))],
        out_specs=pl.BlockSpec((16, tn), lambda j: (0, j)),
        out_shape=jax.ShapeDtypeStruct((16, n), F32),
        compiler_params=_params(("parallel",)),
    )(cs, w, b)


def _ada_bwd(cs, w, dmod):
    D, n = w.shape
    tn = _pick(n, (512, 384, 256, 128))

    def body(c_ref, w_ref, d_ref, gw_ref, da_ref):
        j = pl.program_id(0)
        cv = c_ref[...]
        a = cv * _sigmoid(cv)
        d = d_ref[...]
        gw_ref[...] = lax.dot_general(a, d, (((0,), (0,)), ((), ())), precision=HI, preferred_element_type=F32)

        @pl.when(j == 0)
        def _():
            da_ref[...] = jnp.zeros_like(da_ref)

        da_ref[...] += _dot_nt(d.astype(BF16), w_ref[...].astype(BF16))

    return _pcall(
        body, name="ada_bwd", grid=(n // tn,),
        in_specs=[pl.BlockSpec((16, D), lambda j: (0, 0)), pl.BlockSpec((D, tn), lambda j: (0, j)),
                  pl.BlockSpec((16, tn), lambda j: (0, j))],
        out_specs=[pl.BlockSpec((D, tn), lambda j: (0, j)), pl.BlockSpec((16, D), lambda j: (0, 0))],
        out_shape=[jax.ShapeDtypeStruct((D, n), F32), jax.ShapeDtypeStruct((16, D), F32)],
        compiler_params=_params(("arbitrary",)),
    )(cs, w, dmod)


def _rms1_fwd(xall, gain, shift2, scale2, n_ctx):
    T, D = xall.shape
    tb = _pick(n_ctx, (256, 128, 64, 32, 16))
    nctx = n_ctx // tb

    def body(x_ref, g_ref, sh_ref, sc_ref, o_ref):
        i = pl.program_id(0)
        xv = x_ref[...]
        r = lax.rsqrt(jnp.mean(xv * xv, axis=-1, keepdims=True) + EPS)
        nrm = xv * r * g_ref[...]
        lat = i >= nctx
        sh = jnp.where(lat, sh_ref[1:2, :], sh_ref[0:1, :])
        sc = jnp.where(lat, sc_ref[1:2, :], sc_ref[0:1, :])
        o_ref[...] = (nrm * (1.0 + sc) + sh).astype(BF16)

    vec = lambda r: pl.BlockSpec((r, D), lambda i: (0, 0))
    return _pcall(
        body, name="rms1_fwd", grid=(T // tb,),
        in_specs=[pl.BlockSpec((tb, D), lambda i: (i, 0)), vec(1), vec(2), vec(2)],
        out_specs=pl.BlockSpec((tb, D), lambda i: (i, 0)),
        out_shape=jax.ShapeDtypeStruct((T, D), BF16),
        compiler_params=_params(("parallel",)),
    )(xall, gain, shift2, scale2)


def _rms1_bwd(xall, dh, dxmid, gain, scale2, n_ctx):
    T, D = xall.shape
    L = T - n_ctx
    tb = _pick(n_ctx, (256, 128, 64, 32, 16))
    nctx = n_ctx // tb

    def body(x_ref, dh_ref, dxm_ref, g_ref, sc_ref, dx_ref, cs_ref):
        i = pl.program_id(0)
        lat = i >= nctx
        xv = x_ref[...]
        r = lax.rsqrt(jnp.mean(xv * xv, axis=-1, keepdims=True) + EPS)
        xh = xv * r
        g = g_ref[...]
        nrm = xh * g
        sc = jnp.where(lat, sc_ref[1:2, :], sc_ref[0:1, :])
        dhv = dh_ref[...]
        dn = dhv * (1.0 + sc)
        dxh = dn * g
        dxv = r * (dxh - xh * jnp.mean(dxh * xh, axis=-1, keepdims=True))
        s_sh = jnp.sum(dhv, axis=0, keepdims=True)
        s_sc = jnp.sum(dhv * nrm, axis=0, keepdims=True)
        s_g = jnp.sum(dn * xh, axis=0, keepdims=True)
        zero = jnp.zeros_like(s_sh)
        rows = lax.broadcasted_iota(jnp.int32, (8, D), 0)
        upd = jnp.where(rows == 0, jnp.where(lat, zero, s_sh),
              jnp.where(rows == 1, jnp.where(lat, zero, s_sc),
              jnp.where(rows == 2, jnp.where(lat, s_sh, zero),
              jnp.where(rows == 3, jnp.where(lat, s_sc, zero),
              jnp.where(rows == 4, s_g, 0.0)))))

        @pl.when(i == 0)
        def _():
            cs_ref[...] = jnp.zeros_like(cs_ref)

        cs_ref[...] += upd

        @pl.when(lat)
        def _():
            dx_ref[...] = dxv + dxm_ref[...]

    lat_blk = lambda i: (jnp.maximum(i - nctx, 0), 0)
    vec = lambda r: pl.BlockSpec((r, D), lambda i: (0, 0))
    return _pcall(
        body, name="rms1_bwd", grid=(T // tb,),
        in_specs=[pl.BlockSpec((tb, D), lambda i: (i, 0)), pl.BlockSpec((tb, D), lambda i: (i, 0)),
                  pl.BlockSpec((tb, D), lat_blk), vec(1), vec(2)],
        out_specs=[pl.BlockSpec((tb, D), lat_blk), vec(8)],
        out_shape=[jax.ShapeDtypeStruct((L, D), F32), jax.ShapeDtypeStruct((8, D), F32)],
        compiler_params=_params(("arbitrary",)),
    )(xall, dh, dxmid, gain, scale2)


def _resid_rms2_fwd(x, mo, vecs):
    L, D = x.shape
    tb = _pick(L, (256, 128, 64))

    def body(x_ref, mo_ref, v_ref, xm_ref, h_ref):
        xm = x_ref[...] + v_ref[0:1, :] * mo_ref[...]
        xm_ref[...] = xm
        r = lax.rsqrt(jnp.mean(xm * xm, axis=-1, keepdims=True) + EPS)
        h_ref[...] = (xm * r * v_ref[1:2, :] * (1.0 + v_ref[3:4, :]) + v_ref[2:3, :]).astype(BF16)

    blk = pl.BlockSpec((tb, D), lambda i: (i, 0))
    return _pcall(
        body, name="resid_rms2_fwd", grid=(L // tb,),
        in_specs=[blk, blk, pl.BlockSpec((8, D), lambda i: (0, 0))],
        out_specs=[blk, blk],
        out_shape=[jax.ShapeDtypeStruct((L, D), F32), jax.ShapeDtypeStruct((L, D), BF16)],
        compiler_params=_params(("parallel",)),
    )(x, mo, vecs)


def _resid_rms2_bwd(xmid, dh_a, dh_b, dy, mo, vecs):
    L, D = xmid.shape
    tb = _pick(L, (256, 128, 64))

    def body(xm_ref, da_ref, db_ref, dy_ref, mo_ref, v_ref, dxm_ref, dmo_ref, cs_ref):
        i = pl.program_id(0)
        xm = xm_ref[...]
        r = lax.rsqrt(jnp.mean(xm * xm, axis=-1, keepdims=True) + EPS)
        xh = xm * r
        g = v_ref[1:2, :]
        nrm = xh * g
        dhv = da_ref[...] + db_ref[...]
        dn = dhv * (1.0 + v_ref[3:4, :])
        dxh = dn * g
        dxm = dy_ref[...] + r * (dxh - xh * jnp.mean(dxh * xh, axis=-1, keepdims=True))
        dxm_ref[...] = dxm
        dmo_ref[...] = (dxm * v_ref[0:1, :]).astype(BF16)
        s0 = jnp.sum(dhv, axis=0, keepdims=True)
        s1 = jnp.sum(dhv * nrm, axis=0, keepdims=True)
        s2 = jnp.sum(dn * xh, axis=0, keepdims=True)
        s3 = jnp.sum(dxm * mo_ref[...], axis=0, keepdims=True)
        rows = lax.broadcasted_iota(jnp.int32, (8, D), 0)
        upd = jnp.where(rows == 0, s0, jnp.where(rows == 1, s1, jnp.where(rows == 2, s2,
              jnp.where(rows == 3, s3, 0.0))))

        @pl.when(i == 0)
        def _():
            cs_ref[...] = jnp.zeros_like(cs_ref)

        cs_ref[...] += upd

    blk = pl.BlockSpec((tb, D), lambda i: (i, 0))
    vec = pl.BlockSpec((8, D), lambda i: (0, 0))
    return _pcall(
        body, name="resid_rms2_bwd", grid=(L // tb,),
        in_specs=[blk, blk, blk, blk, blk, vec],
        out_specs=[blk, blk, vec],
        out_shape=[jax.ShapeDtypeStruct((L, D), F32), jax.ShapeDtypeStruct((L, D), BF16),
                   jax.ShapeDtypeStruct((8, D), F32)],
        compiler_params=_params(("arbitrary",)),
    )(xmid, dh_a, dh_b, dy, mo, vecs)


def _loss_head(xmid, f, g2, target):
    L, D = xmid.shape
    tb = _pick(L, (256, 128, 64))

    def body(xm_ref, f_ref, g_ref, t_ref, dy_ref, df_ref, s_ref):
        i = pl.program_id(0)
        fv = f_ref[...]
        g = g_ref[...]
        err = xm_ref[...] + g * fv - t_ref[...]
        dy = err * (1.0 / D)
        dy_ref[...] = dy
        df_ref[...] = (dy * g).astype(BF16)
        s0 = jnp.sum(dy * fv, axis=0, keepdims=True)
        part = 0.5 * jnp.sum(jnp.mean(err * err, axis=-1, keepdims=True), axis=0, keepdims=True)
        rows = lax.broadcasted_iota(jnp.int32, (8, D), 0)
        upd = jnp.where(rows == 0, s0, jnp.where(rows == 1, part, 0.0))

        @pl.when(i == 0)
        def _():
            s_ref[...] = jnp.zeros_like(s_ref)

        s_ref[...] += upd

    blk = pl.BlockSpec((tb, D), lambda i: (i, 0))
    return _pcall(
        body, name="loss_head", grid=(L // tb,),
        in_specs=[blk, blk, pl.BlockSpec((1, D), lambda i: (0, 0)), blk],
        out_specs=[blk, blk, pl.BlockSpec((8, D), lambda i: (0, 0))],
        out_shape=[jax.ShapeDtypeStruct((L, D), F32), jax.ShapeDtypeStruct((L, D), BF16),
                   jax.ShapeDtypeStruct((8, D), F32)],
        compiler_params=_params(("arbitrary",)),
    )(xmid, f, g2, target)


def _gate_cols(D, off):
    tc = _pick(np.gcd(D, off), (512, 256, 128))
    return tc, off // tc


def _merge_fwd(za, zb, p, n_ctx, off_a, off_b):
    L, D = za.shape
    tb = _pick(n_ctx, (256, 128, 64, 32, 16))
    nctx = n_ctx // tb
    tc, oa = _gate_cols(D, off_a)
    _, ob = _gate_cols(D, off_b)
    if off_b % tc:
        raise ValueError("gate column offsets must share a column tile")
    ob = off_b // tc

    def body(za_ref, zb_ref, ga_ref, gb_ref, z_ref):
        z_ref[...] = (_sigmoid(ga_ref[...]) * za_ref[...].astype(F32)
                      + _sigmoid(gb_ref[...]) * zb_ref[...].astype(F32)).astype(BF16)

    blk = pl.BlockSpec((tb, tc), lambda i, j: (i, j))
    return _pcall(
        body, name="merge_fwd", grid=(L // tb, D // tc),
        in_specs=[blk, blk, pl.BlockSpec((tb, tc), lambda i, j: (i + nctx, oa + j)),
                  pl.BlockSpec((tb, tc), lambda i, j: (i + nctx, ob + j))],
        out_specs=blk,
        out_shape=jax.ShapeDtypeStruct((L, D), BF16),
        compiler_params=_params(("parallel", "parallel")),
    )(za, zb, p, p)


def _merge_bwd(dz, za, zb, p, n_ctx, off_a, off_b):
    L, D = za.shape
    T = L + n_ctx
    tb = _pick(n_ctx, (256, 128, 64, 32, 16))
    nctx = n_ctx // tb
    tc = _gate_cols(D, off_a)[0]
    oa, ob = off_a // tc, off_b // tc

    def body(dz_ref, za_ref, zb_ref, ga_ref, gb_ref, dza_ref, dzb_ref, dga_ref, dgb_ref):
        i = pl.program_id(1)

        @pl.when(i < nctx)
        def _():
            dga_ref[...] = jnp.zeros_like(dga_ref)
            dgb_ref[...] = jnp.zeros_like(dgb_ref)

        @pl.when(i >= nctx)
        def _():
            dzv = dz_ref[...].astype(F32)
            sa = _sigmoid(ga_ref[...])
            sb = _sigmoid(gb_ref[...])
            dza_ref[...] = (dzv * sa).astype(BF16)
            dzb_ref[...] = (dzv * sb).astype(BF16)
            dga_ref[...] = (dzv * za_ref[...].astype(F32) * sa * (1.0 - sa)).astype(BF16)
            dgb_ref[...] = (dzv * zb_ref[...].astype(F32) * sb * (1.0 - sb)).astype(BF16)

    lat = pl.BlockSpec((tb, tc), lambda j, i: (jnp.maximum(i - nctx, 0), j))
    allr = pl.BlockSpec((tb, tc), lambda j, i: (i, j))
    return _pcall(
        body, name="merge_bwd", grid=(D // tc, T // tb),
        in_specs=[lat, lat, lat, pl.BlockSpec((tb, tc), lambda j, i: (i, oa + j)),
                  pl.BlockSpec((tb, tc), lambda j, i: (i, ob + j))],
        out_specs=[lat, lat, allr, allr],
        out_shape=[jax.ShapeDtypeStruct((L, D), BF16), jax.ShapeDtypeStruct((L, D), BF16),
                   jax.ShapeDtypeStruct((T, D), BF16), jax.ShapeDtypeStruct((T, D), BF16)],
        compiler_params=_params(("arbitrary", "arbitrary")),
    )(dz, za, zb, p, p)


def _shift_down(u, rows):
    return jnp.where(rows == 0, 0.0, pltpu.roll(u, 1, 0))


def _shift_up(u, rows):
    n = u.shape[0]
    return jnp.where(rows == n - 1, 0.0, pltpu.roll(u, n - 1, 0))


def _convgate_fwd(u1, u3, cw, cb):
    L, F = u1.shape
    tc = _pick(F, (256, 128))

    def body(u1_ref, u3_ref, w_ref, b_ref, a_ref):
        u = u1_ref[...].astype(F32)
        rows = lax.broadcasted_iota(jnp.int32, u.shape, 0)
        cv = _shift_down(u, rows) * w_ref[0:1, :] + u * w_ref[1:2, :] + _shift_up(u, rows) * w_ref[2:3, :] + b_ref[...]
        a_ref[...] = (cv * _sigmoid(cv) * u3_ref[...].astype(F32)).astype(BF16)

    blk = pl.BlockSpec((L, tc), lambda j: (0, j))
    return _pcall(
        body, name="convgate_fwd", grid=(F // tc,),
        in_specs=[blk, blk, pl.BlockSpec((8, tc), lambda j: (0, j)), pl.BlockSpec((1, tc), lambda j: (0, j))],
        out_specs=blk,
        out_shape=jax.ShapeDtypeStruct((L, F), BF16),
        compiler_params=_params(("parallel",)),
    )(u1, u3, cw, cb)


def _convgate_bwd(u1, u3, da, cw, cb):
    L, F = u1.shape
    tc = _pick(F, (256, 128))

    def body(u1_ref, u3_ref, da_ref, w_ref, b_ref, du1_ref, du3_ref, s_ref):
        u = u1_ref[...].astype(F32)
        rows = lax.broadcasted_iota(jnp.int32, u.shape, 0)
        um, up = _shift_down(u, rows), _shift_up(u, rows)
        w0, w1, w2 = w_ref[0:1, :], w_ref[1:2, :], w_ref[2:3, :]
        cv = um * w0 + u * w1 + up * w2 + b_ref[...]
        s = _sigmoid(cv)
        dav = da_ref[...].astype(F32)
        du3_ref[...] = (dav * cv * s).astype(BF16)
        dcv = dav * u3_ref[...].astype(F32) * (s * (1.0 + cv * (1.0 - s)))
        du1_ref[...] = (_shift_up(dcv, rows) * w0 + dcv * w1 + _shift_down(dcv, rows) * w2).astype(BF16)
        r8 = lax.broadcasted_iota(jnp.int32, (8, tc), 0)
        s0 = jnp.sum(dcv * um, axis=0, keepdims=True)
        s1 = jnp.sum(dcv * u, axis=0, keepdims=True)
        s2 = jnp.sum(dcv * up, axis=0, keepdims=True)
        s3 = jnp.sum(dcv, axis=0, keepdims=True)
        s_ref[...] = jnp.where(r8 == 0, s0, jnp.where(r8 == 1, s1, jnp.where(r8 == 2, s2,
                     jnp.where(r8 == 3, s3, 0.0))))

    blk = pl.BlockSpec((L, tc), lambda j: (0, j))
    v8 = pl.BlockSpec((8, tc), lambda j: (0, j))
    return _pcall(
        body, name="convgate_bwd", grid=(F // tc,),
        in_specs=[blk, blk, blk, v8, pl.BlockSpec((1, tc), lambda j: (0, j))],
        out_specs=[blk, blk, v8],
        out_shape=[jax.ShapeDtypeStruct((L, F), BF16), jax.ShapeDtypeStruct((L, F), BF16),
                   jax.ShapeDtypeStruct((8, F), F32)],
        compiler_params=_params(("parallel",)),
    )(u1, u3, da, cw, cb)


def _lower_bound(lbl_ref, d):
    l0, l1 = lbl_ref[d, 0:1, :], lbl_ref[d, 1:2, :]
    m = jnp.maximum(l0, l1)
    e0, e1 = jnp.exp(l0 - m), jnp.exp(l1 - m)
    return e0 / (e0 + e1)


def _chunk_cumsum(x, rev):
    n = x.shape[0]
    r = lax.broadcasted_iota(jnp.int32, x.shape, 0) % CHUNK
    k = 1
    while k < CHUNK:
        if rev:
            x = x + jnp.where(r < CHUNK - k, pltpu.roll(x, n - k, 0), 0.0)
        else:
            x = x + jnp.where(r >= k, pltpu.roll(x, k, 0), 0.0)
        k *= 2
    return x


def _gate_terms(z, lb):
    sg = _sigmoid(z)
    f = lb + (1.0 - lb) * sg
    return sg, f


def _decay_terms(z, lb, rev):
    _, f = _gate_terms(z, lb)
    g = jnp.log(f)
    return 1.0 - f, _chunk_cumsum(g, rev), _chunk_cumsum(g, not rev) - g


def _chunk_total(c, rev):
    return c[0:1, :] if rev else c[CHUNK - 1:CHUNK, :]


def _pair_decay(c, s, rev):
    t = lax.broadcasted_iota(jnp.int32, (CHUNK, 1), 0)
    later = (t <= s) if rev else (t >= s)
    return jnp.where(later, jnp.exp(c - c[s:s + 1, :]), 0.0)


def _scan_chunk(i, n_ctx_chunks, n_chunks, rev):
    if not rev:
        return i
    return jnp.where(i < n_ctx_chunks, n_ctx_chunks - 1 - i, n_chunks + n_ctx_chunks - 1 - i)


def _rows(ci):
    return pl.ds(pl.multiple_of(ci * CHUNK, CHUNK), CHUNK)


def _hgrn_cols(HA):
    return HA // HEAD


def _hgrn_fwd(p, lbl, ng, n_ctx, HA):
    T = p.shape[0]
    L = T - n_ctx
    nh = _hgrn_cols(HA)
    nc, ncc = T // CHUNK, n_ctx // CHUNK

    def body(q_ref, zf_ref, zb_ref, v_ref, og_ref, lbl_ref, ng_ref, ya_ref, o_ref, st_ref,
             c_scr, k_scr, qe_scr, ke_scr, o_scr):
        dirs = ((0, False, zf_ref), (1, True, zb_ref))
        for d, rev, z_ref in dirs:
            k, c, rest = _decay_terms(z_ref[...], _lower_bound(lbl_ref, d), rev)
            c_scr[d] = c
            k_scr[d] = k
            qe_scr[d] = (q_ref[...] * jnp.exp(c)).astype(BF16)
            ke_scr[d] = (k * jnp.exp(rest)).astype(BF16)

        def step(i2, states):
            states = list(states)
            for u in range(HGRN_UNROLL):
                for d, rev, _ in dirs:
                    St = states[d]
                    ci = _scan_chunk(HGRN_UNROLL * i2 + u, ncc, nc, rev)
                    rows = _rows(ci)
                    q, v, c, k = q_ref[rows, :], v_ref[rows, :], c_scr[d, rows, :], k_scr[d, rows, :]
                    st_ref[0, d, ci] = St.astype(BF16)
                    o = jnp.zeros((CHUNK, HEAD), F32)
                    for s in range(CHUNK):
                        E = _pair_decay(c, s, rev)
                        a = jnp.sum(q * E * k[s:s + 1, :], axis=1, keepdims=True)
                        o = o + a * v[s:s + 1, :]
                    o_scr[d, rows, :] = o + _dot_nt(qe_scr[d, rows, :], St.astype(BF16))
                    states[d] = St * jnp.exp(_chunk_total(c, rev)) + _dot_tn(v.astype(BF16), ke_scr[d, rows, :])
            return tuple(states)

        if nc % HGRN_UNROLL:
            raise ValueError("the number of chunks must be a multiple of HGRN_UNROLL")
        zero = jnp.zeros((HEAD, HEAD), F32)
        lax.fori_loop(0, nc // HGRN_UNROLL, step, (zero, zero))

        o = o_scr[0, pl.ds(n_ctx, L), :] + o_scr[1, pl.ds(n_ctx, L), :]
        o_ref[...] = o
        r = lax.rsqrt(jnp.mean(o * o, axis=-1, keepdims=True) + EPS)
        og = og_ref[pl.ds(n_ctx, L), :]
        ya_ref[...] =(o * r * ng_ref[...] * (og * _sigmoid(og))).astype(BF16)

    cb = HA // HEAD
    col = lambda kk: pl.BlockSpec((T, HEAD), lambda h: (0, kk * cb + h))
    return _pcall(
        body, name="hgrn_fwd", grid=(nh,),
        in_specs=[col(0), col(1), col(2), col(3), col(4),
                  pl.BlockSpec((2, 2, HEAD), lambda h: (0, 0, h)), pl.BlockSpec((1, HEAD), lambda h: (0, 0))],
        out_specs=[pl.BlockSpec((L, HEAD), lambda h: (0, h)), pl.BlockSpec((L, HEAD), lambda h: (0, h)),
                   pl.BlockSpec((1, 2, nc, HEAD, HEAD), lambda h: (h, 0, 0, 0, 0))],
        out_shape=[jax.ShapeDtypeStruct((L, HA), BF16), jax.ShapeDtypeStruct((L, HA), F32),
                   jax.ShapeDtypeStruct((nh, 2, nc, HEAD, HEAD), BF16)],
        scratch_shapes=[pltpu.VMEM((2, T, HEAD), F32), pltpu.VMEM((2, T, HEAD), F32),
                        pltpu.VMEM((2, T, HEAD), BF16), pltpu.VMEM((2, T, HEAD), BF16),
                        pltpu.VMEM((2, T, HEAD), F32)],
        compiler_params=_params(("parallel",)),
    )(p, p, p, p, p, lbl, ng)


def _hgrn_bwd(p, lbl, ng, o, dya, st, n_ctx, HA):
    T = p.shape[0]
    L = T - n_ctx
    nh = _hgrn_cols(HA)
    nc, ncc = T // CHUNK, n_ctx // CHUNK

    def body(q_ref, zf_ref, zb_ref, v_ref, og_ref, lbl_ref, ng_ref, o_ref, dya_ref, st_ref,
             dq_ref, dzf_ref, dzb_ref, dv_ref, dog_ref, dlbl_ref, dng_ref,
             do_scr, c_scr, k_scr, qe_scr, ke_scr, dg_scr, dk_scr, dq_scr, dv_scr, row_scr):
        h = pl.program_id(0)
        ov = o_ref[...]
        r = lax.rsqrt(jnp.mean(ov * ov, axis=-1, keepdims=True) + EPS)
        oh = ov * r
        ogv = og_ref[pl.ds(n_ctx, L), :]
        sg_o = _sigmoid(ogv)
        dyv = dya_ref[...]
        ngv = ng_ref[...]
        dog_ref[pl.ds(0, n_ctx), :] = jnp.zeros((n_ctx, HEAD), BF16)
        dog_ref[pl.ds(n_ctx, L), :] = (dyv * oh * ngv * (sg_o * (1.0 + ogv * (1.0 - sg_o)))).astype(BF16)
        don = dyv * (ogv * sg_o)
        dng = jnp.sum(don * oh, axis=0, keepdims=True)
        doh = don * ngv
        do_scr[pl.ds(0, n_ctx), :] = jnp.zeros((n_ctx, HEAD), F32)
        do_scr[pl.ds(n_ctx, L), :] = r * (doh - oh * jnp.mean(doh * oh, axis=-1, keepdims=True))

        @pl.when(h == 0)
        def _():
            dng_ref[...] = jnp.zeros_like(dng_ref)

        dng_ref[0:1, :] += dng

        t16 = lax.broadcasted_iota(jnp.int32, (CHUNK, HEAD), 0)
        dirs = ((0, False, zf_ref, dzf_ref), (1, True, zb_ref, dzb_ref))
        for d, rev, z_ref, _ in dirs:
            k, c, rest = _decay_terms(z_ref[...], _lower_bound(lbl_ref, d), rev)
            c_scr[d] = c
            k_scr[d] = k
            qe_scr[d] = (q_ref[...] * jnp.exp(c)).astype(BF16)
            ke_scr[d] = (k * jnp.exp(rest)).astype(BF16)
        dq_scr[...] = jnp.zeros_like(dq_scr)
        dv_scr[...] = jnp.zeros_like(dv_scr)

        zero = jnp.zeros((HEAD, HEAD), F32)

        def bwd_chunk(i, carry, u):
            new = []
            for (d, rev, _, _), dSt in zip(dirs, carry):
                ci = _scan_chunk(i, ncc, nc, rev)
                rows = _rows(ci)
                q, v, do = q_ref[rows, :], v_ref[rows, :], do_scr[rows, :]
                c, k = c_scr[d, rows, :], k_scr[d, rows, :]
                tot = _chunk_total(c, rev)
                etot = jnp.exp(tot)
                St = st_ref[0, d, ci]
                dSb = dSt.astype(BF16)
                do_b = do.astype(BF16)
                dq_x = _dot(do_b, St) * jnp.exp(c)
                dk_x = _dot(v.astype(BF16), dSb) * jnp.exp(tot - c)
                dv_x = _dot_nt(ke_scr[d, rows, :], dSb)
                dtot = (jnp.sum(St.astype(F32) * dSt, axis=0, keepdims=True) * etot
                        + jnp.sum(k * dk_x, axis=0, keepdims=True))
                dq = jnp.zeros((CHUNK, HEAD), F32)
                for s in range(CHUNK):
                    E = _pair_decay(c, s, rev)
                    XE = E * k[s:s + 1, :]
                    a = jnp.sum(q * XE, axis=1, keepdims=True)
                    da = jnp.sum(do * v[s:s + 1, :], axis=1, keepdims=True)
                    dq = dq + da * XE
                    row_scr[u, d, 0, s:s + 1, :] = jnp.sum(da * q * E, axis=0, keepdims=True)
                    row_scr[u, d, 1, s:s + 1, :] = jnp.sum(a * do, axis=0, keepdims=True)
                dq, dk, dv = dq + dq_x, row_scr[u, d, 0] + dk_x, row_scr[u, d, 1] + dv_x
                dg_scr[d, rows, :] = _chunk_cumsum(q * dq - k * dk, not rev) + dtot
                dk_scr[d, rows, :] = dk
                dq_scr[rows, :] += dq
                dv_scr[rows, :] += dv
                new.append(dSt * etot + _dot_tn(do_b, qe_scr[d, rows, :]))
            return tuple(new)

        def bwd_step(i2, carry):
            for u in range(2):
                carry = bwd_chunk(nc - 1 - (2 * i2 + u), carry, u)
            return carry

        lax.fori_loop(0, nc // 2, bwd_step, (zero, zero))

        for d, _, z_ref, dz_ref in dirs:
            lb = _lower_bound(lbl_ref, d)
            sg, f = _gate_terms(z_ref[...], lb)
            df = dg_scr[d] / f - dk_scr[d]
            dz_ref[...] = (df * (1.0 - lb) * sg * (1.0 - sg)).astype(BF16)
            dl0 = jnp.sum(df * (1.0 - sg), axis=0, keepdims=True) * lb * (1.0 - lb)
            dlbl_ref[d, 0:1, :] = dl0
            dlbl_ref[d, 1:2, :] = -dl0
        dq_ref[...] = dq_scr[...].astype(BF16)
        dv_ref[...] = dv_scr[...].astype(BF16)

    cb = HA // HEAD
    col = lambda kk: pl.BlockSpec((T, HEAD), lambda h: (0, kk * cb + h))
    tcol = pl.BlockSpec((T, HEAD), lambda h: (0, h))
    lcol = pl.BlockSpec((L, HEAD), lambda h: (0, h))
    outs = _pcall(
        body, name="hgrn_bwd", grid=(nh,),
        in_specs=[col(0), col(1), col(2), col(3), col(4),
                  pl.BlockSpec((2, 2, HEAD), lambda h: (0, 0, h)), pl.BlockSpec((1, HEAD), lambda h: (0, 0)),
                  lcol, lcol,
                  pl.BlockSpec((1, 2, nc, HEAD, HEAD), lambda h: (h, 0, 0, 0, 0), pipeline_mode=pl.Buffered(1))],
        out_specs=[tcol, tcol, tcol, tcol, tcol, pl.BlockSpec((2, 2, HEAD), lambda h: (0, 0, h)),
                   pl.BlockSpec((8, HEAD), lambda h: (0, 0))],
        out_shape=[jax.ShapeDtypeStruct((T, HA), BF16)] * 5 + [jax.ShapeDtypeStruct((2, 2, HA), F32),
                                                               jax.ShapeDtypeStruct((8, HEAD), F32)],
        scratch_shapes=[pltpu.VMEM((T, HEAD), F32),
                        pltpu.VMEM((2, T, HEAD), F32), pltpu.VMEM((2, T, HEAD), F32),
                        pltpu.VMEM((2, T, HEAD), BF16), pltpu.VMEM((2, T, HEAD), BF16),
                        pltpu.VMEM((2, T, HEAD), F32), pltpu.VMEM((2, T, HEAD), F32),
                        pltpu.VMEM((T, HEAD), F32), pltpu.VMEM((T, HEAD), F32),
                        pltpu.VMEM((2, 2, 2, CHUNK, HEAD), F32)],
        compiler_params=_params(("arbitrary",)),
    )(p, p, p, p, p, lbl, ng, o, dya, st)
    return outs


def _swap_halves(t, lane):
    q = HEAD // 4
    return jnp.where((lane % (2 * q)) < q, pltpu.roll(t, HEAD - q, 1), pltpu.roll(t, q, 1))


def _qk_norm(t, g):
    r = lax.rsqrt(jnp.mean(t * t, axis=-1, keepdims=True) + EPS)
    return t * r, r


def _rope(t, cos, sin, lane):
    return t * cos + _swap_halves(t, lane) * sin


def _qk_norm_bwd(dy, th, r, g):
    dth = dy * g
    return r * (dth - th * jnp.mean(dth * th, axis=-1, keepdims=True)), jnp.sum(dy * th, axis=0, keepdims=True)


def _rope_bwd(dy, cos, sin, lane):
    return dy * cos + _swap_halves(dy * sin, lane)


def _na_geometry(L):
    n_rows = L // GRID_W
    kr = min(WIN_R, n_rows)
    return n_rows, kr


def _na_prep(q_ref, k_ref, v_ref, gq_ref, gk_ref, cos_ref, sin_ref, qs, ks, vs, n_ctx, L):
    lane = lax.broadcasted_iota(jnp.int32, (L, HEAD), 1)
    cos, sin = cos_ref[...], sin_ref[...]
    qh, _ = _qk_norm(q_ref[pl.ds(n_ctx, L), :], None)
    qs[...] = _rope(qh * gq_ref[...], cos, sin, lane).astype(BF16)
    kh, _ = _qk_norm(k_ref[pl.ds(n_ctx, L), :], None)
    ks[pl.ds(n_ctx, L), :] = _rope(kh * gk_ref[...], cos, sin, lane).astype(BF16)
    kc, _ = _qk_norm(k_ref[pl.ds(0, n_ctx), :], None)
    ks[pl.ds(0, n_ctx), :] = (kc * gk_ref[...]).astype(BF16)
    vs[...] = v_ref[...].astype(BF16)


NA_RB = 4


def _na_band_rows(kr):
    return kr + NA_RB


def _na_scores(i, qs, ks, bias_ref, n_ctx, n_rows, kr):
    scale = HEAD ** -0.5
    kb = _na_band_rows(kr)
    rq = NA_RB * i
    r0 = jnp.clip(rq - WIN_R // 2, 0, n_rows - kb)
    qrows = pl.ds(pl.multiple_of(rq * GRID_W, NA_RB * GRID_W), NA_RB * GRID_W)
    krows = pl.ds(pl.multiple_of(n_ctx + r0 * GRID_W, GRID_W), kb * GRID_W)
    qv = qs[qrows, :]
    sb = _dot_nt(qv, ks[krows, :]) * scale
    band_row = lax.broadcasted_iota(jnp.int32, (GRID_W, kb * GRID_W), 1) // GRID_W
    parts, tiles = [], []
    for u in range(NA_RB):
        r_u = rq + u
        first = jnp.clip(r_u - WIN_R // 2, 0, n_rows - kr) - r0
        idx = [jnp.clip(r0 - r_u + (WIN_R - 1) + 2 * jj, 0, 2 * WIN_R - 1) for jj in range(kb // 2)]
        bias_u = jnp.concatenate([bias_ref[0, t] for t in idx], axis=1)
        inside = (band_row >= first) & (band_row < first + kr)
        parts.append(jnp.where(inside, sb[u * GRID_W:(u + 1) * GRID_W, :] + bias_u, NEG))
        tiles.append(idx)
    sb = jnp.concatenate(parts, axis=0)
    sc = _dot_nt(qv, ks[pl.ds(0, n_ctx), :]) * scale
    m = jnp.maximum(jnp.max(sb, axis=1, keepdims=True), jnp.max(sc, axis=1, keepdims=True))
    eb, ec = jnp.exp(sb - m), jnp.exp(sc - m)
    inv = 1.0 / (jnp.sum(eb, axis=1, keepdims=True) + jnp.sum(ec, axis=1, keepdims=True))
    return eb * inv, ec * inv, qrows, krows, tiles


def _na_fwd(p, bias, gq, gk, cos, sin, n_ctx, off, HB):
    T = p.shape[0]
    L = T - n_ctx
    nh = HB // HEAD
    n_rows, kr = _na_geometry(L)
    ob = off // HEAD

    def body(q_ref, k_ref, v_ref, bias_ref, gq_ref, gk_ref, cos_ref, sin_ref, y_ref, qs, ks, vs):
        _na_prep(q_ref, k_ref, v_ref, gq_ref, gk_ref, cos_ref, sin_ref, qs, ks, vs, n_ctx, L)

        def step(i, carry):
            pb, pc, qrows, krows, _ = _na_scores(i, qs, ks, bias_ref, n_ctx, n_rows, kr)
            y = _dot(pb.astype(BF16), vs[krows, :]) + _dot(pc.astype(BF16), vs[pl.ds(0, n_ctx), :])
            y_ref[qrows, :] = y.astype(BF16)
            return carry

        lax.fori_loop(0, n_rows // NA_RB, step, 0)

    col = lambda kk: pl.BlockSpec((T, HEAD), lambda h: (0, ob + kk * nh + h))
    vec = pl.BlockSpec((1, HEAD), lambda h: (0, 0))
    tab = pl.BlockSpec((L, HEAD), lambda h: (0, 0))
    return _pcall(
        body, name="na_fwd", grid=(nh,),
        in_specs=[col(0), col(1), col(2), pl.BlockSpec((1,) + bias.shape[1:], lambda h: (h, 0, 0, 0)),
                  vec, vec, tab, tab],
        out_specs=pl.BlockSpec((L, HEAD), lambda h: (0, h)),
        out_shape=jax.ShapeDtypeStruct((L, HB), BF16),
        scratch_shapes=[pltpu.VMEM((L, HEAD), BF16), pltpu.VMEM((T, HEAD), BF16), pltpu.VMEM((T, HEAD), BF16)],
        compiler_params=_params(("parallel",)),
    )(p, p, p, bias, gq, gk, cos, sin)


def _na_bwd(p, bias, gq, gk, cos, sin, dyb, n_ctx, off, HB):
    T = p.shape[0]
    L = T - n_ctx
    nh = HB // HEAD
    n_rows, kr = _na_geometry(L)
    ob = off // HEAD
    scale = HEAD ** -0.5

    def body(q_ref, k_ref, v_ref, bias_ref, gq_ref, gk_ref, cos_ref, sin_ref, dy_ref,
             dq_ref, dk_ref, dv_ref, dbias_ref, dg_ref, qs, ks, vs, dqa, dka, dva):
        h = pl.program_id(0)
        _na_prep(q_ref, k_ref, v_ref, gq_ref, gk_ref, cos_ref, sin_ref, qs, ks, vs, n_ctx, L)
        dka[...] = jnp.zeros_like(dka)
        dva[...] = jnp.zeros_like(dva)
        dbias_ref[...] = jnp.zeros_like(dbias_ref)

        crows = pl.ds(0, n_ctx)

        def step(i, carry):
            pb, pc, qrows, krows, tiles = _na_scores(i, qs, ks, bias_ref, n_ctx, n_rows, kr)
            do = dy_ref[qrows, :]
            qv = qs[qrows, :]
            dpb = _dot_nt(do, vs[krows, :])
            dpc = _dot_nt(do, vs[crows, :])
            delta = jnp.sum(pb * dpb, axis=1, keepdims=True) + jnp.sum(pc * dpc, axis=1, keepdims=True)
            dsb = pb * (dpb - delta)
            dsc = pc * (dpc - delta)
            dsb_b, dsc_b = dsb.astype(BF16), dsc.astype(BF16)
            dqa[qrows, :] = (_dot(dsb_b, ks[krows, :]) + _dot(dsc_b, ks[crows, :])) * scale
            dka[krows, :] += _dot_tn(dsb_b, qv) * scale
            dka[crows, :] += _dot_tn(dsc_b, qv) * scale
            dva[krows, :] += _dot_tn(pb.astype(BF16), do)
            dva[crows, :] += _dot_tn(pc.astype(BF16), do)
            for u, idx in enumerate(tiles):
                for jj, t in enumerate(idx):
                    dbias_ref[0, t] += dsb[u * GRID_W:(u + 1) * GRID_W, jj * 2 * GRID_W:(jj + 1) * 2 * GRID_W]
            return carry

        lax.fori_loop(0, n_rows // NA_RB, step, 0)

        lane = lax.broadcasted_iota(jnp.int32, (L, HEAD), 1)
        cos, sin = cos_ref[...], sin_ref[...]
        lat, ctx = pl.ds(n_ctx, L), pl.ds(0, n_ctx)
        gqv, gkv = gq_ref[...], gk_ref[...]
        qh, rq = _qk_norm(q_ref[lat, :], None)
        dq, dgq = _qk_norm_bwd(_rope_bwd(dqa[...], cos, sin, lane), qh, rq, gqv)
        dq_ref[ctx, :] = jnp.zeros((n_ctx, HEAD), BF16)
        dq_ref[lat, :] = dq.astype(BF16)
        kh, rk = _qk_norm(k_ref[lat, :], None)
        dk, dgk = _qk_norm_bwd(_rope_bwd(dka[lat, :], cos, sin, lane), kh, rk, gkv)
        dk_ref[lat, :] = dk.astype(BF16)
        kch, rkc = _qk_norm(k_ref[ctx, :], None)
        dkc, dgkc = _qk_norm_bwd(dka[ctx, :], kch, rkc, gkv)
        dk_ref[ctx, :] = dkc.astype(BF16)
        dv_ref[...] = dva[...].astype(BF16)

        @pl.when(h == 0)
        def _():
            dg_ref[...] = jnp.zeros_like(dg_ref)

        dg_ref[0:1, :] += dgq
        dg_ref[1:2, :] += dgk + dgkc

    col = lambda kk: pl.BlockSpec((T, HEAD), lambda h: (0, ob + kk * nh + h))
    vec = pl.BlockSpec((1, HEAD), lambda h: (0, 0))
    tab = pl.BlockSpec((L, HEAD), lambda h: (0, 0))
    tcol = pl.BlockSpec((T, HEAD), lambda h: (0, h))
    bspec = pl.BlockSpec((1,) + bias.shape[1:], lambda h: (h, 0, 0, 0))
    return _pcall(
        body, name="na_bwd", grid=(nh,),
        in_specs=[col(0), col(1), col(2), bspec, vec, vec, tab, tab, pl.BlockSpec((L, HEAD), lambda h: (0, h))],
        out_specs=[tcol, tcol, tcol, bspec, pl.BlockSpec((8, HEAD), lambda h: (0, 0))],
        out_shape=[jax.ShapeDtypeStruct((T, HB), BF16)] * 3 + [jax.ShapeDtypeStruct(bias.shape, F32),
                                                               jax.ShapeDtypeStruct((8, HEAD), F32)],
        scratch_shapes=[pltpu.VMEM((L, HEAD), BF16), pltpu.VMEM((T, HEAD), BF16), pltpu.VMEM((T, HEAD), BF16),
                        pltpu.VMEM((L, HEAD), F32), pltpu.VMEM((T, HEAD), F32), pltpu.VMEM((T, HEAD), F32)],
        compiler_params=_params(("arbitrary",)),
    )(p, p, p, bias, gq, gk, cos, sin, dyb)


def _bias_tables():
    w = np.arange(GRID_W)
    col_start = np.clip(w - WIN_C // 2, 0, GRID_W - WIN_C)
    col_in = (w[None, :] >= col_start[:, None]) & (w[None, :] < col_start[:, None] + WIN_C)
    dc = np.clip(w[None, :] - w[:, None], -(WIN_C - 1), WIN_C - 1) + WIN_C - 1
    n_pair = 2 * WIN_R
    ridx = np.zeros((n_pair, GRID_W, 2 * GRID_W), np.int32)
    cidx = np.zeros((n_pair, GRID_W, 2 * GRID_W), np.int32)
    valid = np.zeros((n_pair, GRID_W, 2 * GRID_W), bool)
    for i in range(n_pair):
        for half in range(2):
            row = i + half
            sl = slice(half * GRID_W, (half + 1) * GRID_W)
            ridx[i, :, sl] = min(row, 2 * WIN_R - 2)
            cidx[i, :, sl] = dc
            valid[i, :, sl] = col_in & (row <= 2 * WIN_R - 2)
    return ridx, cidx, valid


def _bias_onehot():
    _, cidx, valid = _bias_tables()
    K = GRID_W * 2 * GRID_W
    oh = np.zeros((K, 128), np.float32)
    neg = np.full((1, K), NEG, np.float32)
    for cq in range(GRID_W):
        for ll in range(2 * GRID_W):
            if valid[0, cq, ll]:
                oh[cq * 2 * GRID_W + ll, (ll // GRID_W) * 64 + cidx[0, cq, ll]] = 1.0
                neg[0, cq * 2 * GRID_W + ll] = 0.0
    return oh, neg


def _expand_bias(table):
    H = table.shape[0]
    n_pair, n_dc = 2 * WIN_R, 2 * WIN_C - 1
    tp = jnp.pad(table, ((0, 0), (0, n_pair + 1 - table.shape[1]), (0, 64 - n_dc)))
    t2 = jnp.concatenate([tp[:, :n_pair], tp[:, 1:n_pair + 1]], axis=-1).reshape(H * n_pair, 128)
    oh, neg = _bias_onehot()

    def body(t_ref, oh_ref, neg_ref, o_ref):
        o_ref[...] = lax.dot_general(t_ref[...], oh_ref[...], (((1,), (1,)), ((), ())), precision=HI,
                                     preferred_element_type=F32) + neg_ref[...]

    out = _pcall(body, name="bias_expand", out_shape=jax.ShapeDtypeStruct((H * n_pair, oh.shape[0]), F32),
                         compiler_params=_params())(t2, jnp.asarray(oh), jnp.asarray(neg))
    return out.reshape(H, n_pair, GRID_W, 2 * GRID_W)


def _bias_grad(dbias):
    H = dbias.shape[0]
    n_pair, n_dc = 2 * WIN_R, 2 * WIN_C - 1
    K = GRID_W * 2 * GRID_W
    oh, _ = _bias_onehot()
    flat = dbias.reshape(H * n_pair, K)

    def body(d_ref, oh_ref, o_ref):
        o_ref[...] = jnp.dot(d_ref[...], oh_ref[...], precision=HI, preferred_element_type=F32)

    g = _pcall(body, name="bias_grad", out_shape=jax.ShapeDtypeStruct((H * n_pair, 128), F32),
                       compiler_params=_params())(flat, jnp.asarray(oh))
    g = g.reshape(H, n_pair, 128)
    left, right = g[:, :, :n_dc], g[:, :, 64:64 + n_dc]
    out = left[:, :n_pair - 1]
    return out.at[:, 1:].add(right[:, :n_pair - 2])


def _rope_tables(L):
    pos = np.arange(L)
    row = (pos // GRID_W).astype(np.float32)
    colp = (pos % GRID_W).astype(np.float32)
    half = HEAD // 2
    nf = half // 2
    inv = (ROPE_THETA ** (-np.arange(nf, dtype=np.float32) / nf)).astype(np.float32)

    def tabs(pv):
        ang = pv[:, None] * inv[None, :]
        c, s = np.cos(ang), np.sin(ang)
        return np.concatenate([c, c], axis=1), np.concatenate([-s, s], axis=1)

    cr, sr = tabs(row)
    cc, sc = tabs(colp)
    return (jnp.asarray(np.concatenate([cr, cc], axis=1), F32), jnp.asarray(np.concatenate([sr, sc], axis=1), F32))


def _adamw(w, g, m, v, name, after=None, copy_g=False):
    R, C = w.shape
    tr = _row_tile(R, C)
    c1 = 1.0 - ADAM_B1 ** ADAM_STEP
    c2 = 1.0 - ADAM_B2 ** ADAM_STEP
    deps = [] if after is None else [after]
    n_out = 4 if copy_g else 3

    def body(w_ref, g_ref, m_ref, v_ref, *rest):
        d_ref, mo_ref, vo_ref = rest[len(deps):len(deps) + 3]
        gv = g_ref[...]
        mn = ADAM_B1 * m_ref[...] + (1.0 - ADAM_B1) * gv
        vn = ADAM_B2 * v_ref[...] + (1.0 - ADAM_B2) * (gv * gv)
        mo_ref[...] = mn
        vo_ref[...] = vn
        d_ref[...] = -ADAM_LR * ((mn / c1) / (jnp.sqrt(vn / c2) + ADAM_EPS) + ADAM_WD * w_ref[...])
        if copy_g:
            rest[-1][...] = gv

    blk = pl.BlockSpec((tr, C), lambda i: (i, 0))
    return _pcall(
        body, name=name, grid=(R // tr,),
        in_specs=[blk] * 4 + [_ANY] * len(deps), out_specs=[blk] * n_out,
        out_shape=[jax.ShapeDtypeStruct((R, C), F32)] * n_out,
        compiler_params=_params(("parallel",)),
    )(w, g, m, v, *deps)


PACK_W = 1024


def _pack(parts):
    flat, offs, pos = [], [], 0
    for a in parts:
        n = a.size
        padn = -n % PACK_W
        flat.append(jnp.pad(a.reshape(-1).astype(F32), (0, padn)))
        offs.append((pos, n, a.shape))
        pos += n + padn
    tail = -pos % (8 * PACK_W)
    if tail:
        flat.append(jnp.zeros((tail,), F32))
    return jnp.concatenate(flat).reshape(-1, PACK_W), offs


def _unpack(buf, offs, i):
    pos, n, shape = offs[i]
    return buf.reshape(buf.shape[:-2] + (-1,))[..., pos:pos + n].reshape(buf.shape[:-2] + shape)


def kernel(x, c, ctx, c_ctx, ada_w, ada_b, norm1_g, norm2_g, w_in, hgrn_lb_logits, hgrn_norm_g, na_q_norm_g, na_k_norm_g, na_rel_bias, w_branch_a, w_branch_b, w_out, ffn_w1, ffn_w3, ffn_conv_w, ffn_conv_b, ffn_w2, loss_target, m_c_ctx, m_ada_w, m_ada_b, m_norm1_g, m_norm2_g, m_w_in, m_hgrn_lb_logits, m_hgrn_norm_g, m_na_q_norm_g, m_na_k_norm_g, m_na_rel_bias, m_w_branch_a, m_w_branch_b, m_w_out, m_ffn_w1, m_ffn_w3, m_ffn_conv_w, m_ffn_conv_b, m_ffn_w2, v_c_ctx, v_ada_w, v_ada_b, v_norm1_g, v_norm2_g, v_w_in, v_hgrn_lb_logits, v_hgrn_norm_g, v_na_q_norm_g, v_na_k_norm_g, v_na_rel_bias, v_w_branch_a, v_w_branch_b, v_w_out, v_ffn_w1, v_ffn_w3, v_ffn_conv_w, v_ffn_conv_b, v_ffn_w2):
    weights = dict(c_ctx=c_ctx, ada_w=ada_w, ada_b=ada_b, norm1_g=norm1_g, norm2_g=norm2_g, w_in=w_in,
                   hgrn_lb_logits=hgrn_lb_logits, hgrn_norm_g=hgrn_norm_g, na_q_norm_g=na_q_norm_g,
                   na_k_norm_g=na_k_norm_g, na_rel_bias=na_rel_bias, w_branch_a=w_branch_a, w_branch_b=w_branch_b,
                   w_out=w_out, ffn_w1=ffn_w1, ffn_w3=ffn_w3, ffn_conv_w=ffn_conv_w, ffn_conv_b=ffn_conv_b,
                   ffn_w2=ffn_w2)
    moms = dict(c_ctx=(m_c_ctx, v_c_ctx), ada_w=(m_ada_w, v_ada_w), ada_b=(m_ada_b, v_ada_b),
                norm1_g=(m_norm1_g, v_norm1_g), norm2_g=(m_norm2_g, v_norm2_g), w_in=(m_w_in, v_w_in),
                hgrn_lb_logits=(m_hgrn_lb_logits, v_hgrn_lb_logits), hgrn_norm_g=(m_hgrn_norm_g, v_hgrn_norm_g),
                na_q_norm_g=(m_na_q_norm_g, v_na_q_norm_g), na_k_norm_g=(m_na_k_norm_g, v_na_k_norm_g),
                na_rel_bias=(m_na_rel_bias, v_na_rel_bias), w_branch_a=(m_w_branch_a, v_w_branch_a),
                w_branch_b=(m_w_branch_b, v_w_branch_b), w_out=(m_w_out, v_w_out), ffn_w1=(m_ffn_w1, v_ffn_w1),
                ffn_w3=(m_ffn_w3, v_ffn_w3), ffn_conv_w=(m_ffn_conv_w, v_ffn_conv_w),
                ffn_conv_b=(m_ffn_conv_b, v_ffn_conv_b), ffn_w2=(m_ffn_w2, v_ffn_w2))
    order = list(weights)

    L, D = x.shape[1], x.shape[2]
    N = ctx.shape[1]
    T = N + L
    HA = w_branch_a.shape[1]
    HB = w_branch_b.shape[1]
    F = ffn_conv_b.shape[1]
    IN = 5 * HA + 3 * HB + 2 * D
    n_ada = ada_w.shape[2]
    ix, iy, ic = _pos()
    chip = 2 * ix + iy
    dev = 2 * chip + ic

    _PENDING.clear()
    pk0, offs0 = _pack([c[0], hgrn_lb_logits, ffn_conv_w[0]])
    g0 = _allgather8(pk0, "gather_small0")
    c_all = _unpack(g0, offs0, 0)
    lbl_parts = _unpack(g0, offs0, 1)
    lbl = jnp.concatenate([lbl_parts[2 * j] for j in range(N_CHIP)], axis=-1)
    cw_parts = _unpack(g0, offs0, 2)
    cw = jnp.concatenate([cw_parts[2 * j] for j in range(N_CHIP)], axis=-1)
    cw8 = jnp.pad(cw, ((0, 5), (0, 0)))

    cs = jnp.concatenate([c_all, c_ctx[None, :], jnp.zeros((7, D), F32)], axis=0)
    ada_b_mine = lax.dynamic_slice(ada_b, (0, chip * n_ada), (1, n_ada))
    mod_mine = _ada_fwd(cs, ada_w[0], ada_b_mine)
    gm = _allgather8(mod_mine, "gather_mod")
    mod = jnp.concatenate([gm[2 * j] for j in range(N_CHIP)], axis=-1)
    mod_l = lax.dynamic_slice(mod, (dev, 0), (1, N_MOD * D)).reshape(N_MOD, D)
    mod_c = mod[8].reshape(N_MOD, D)
    sh1, sc1, g1, sh2, sc2, g2 = [mod_l[i:i + 1] for i in range(N_MOD)]
    shift1 = jnp.concatenate([mod_c[0:1], sh1], axis=0)
    scale1 = jnp.concatenate([mod_c[1:2], sc1], axis=0)

    shards = [w_in[0], w_branch_a[0], w_branch_b[0], w_out[0], ffn_w1[0], ffn_w3[0], ffn_w2[0]]
    names = ["w_in", "w_a", "w_b", "w_out", "w1", "w3", "w2"]
    slots = [_cast_bf16_slot(s, "cast_" + nm) for s, nm in zip(shards, names)]
    gat_in = _gather_start("in", slots[0:1], gm)
    gat_mix = _gather_start("mix", slots[1:4])
    gat_ffn = _gather_start("ffn", slots[4:7])

    xall = jnp.concatenate([ctx[0], x[0]], axis=0)
    h_all = _rms1_fwd(xall, norm1_g, shift1, scale1, N)
    gat_in = _gather_mid(gat_in, h_all)
    (Win,) = _gather_finish(gat_in, h_all)
    p = _mm_nn(h_all, Win, F32, "mm_p")
    gat_mix = _gather_mid(gat_mix, p)
    y_a, o_a, st_a = _hgrn_fwd(p, lbl, hgrn_norm_g, N, HA)
    Wa, Wb, Wo = _gather_finish(gat_mix, y_a)
    Wo = Wo.reshape(1, D, D)
    bias = _expand_bias(na_rel_bias[0])
    cos, sin = _rope_tables(L)
    off_na = 5 * HA
    y_b = _na_fwd(p, bias, na_q_norm_g, na_k_norm_g, cos, sin, N, off_na, HB)
    gat_ffn = _gather_mid(gat_ffn, (y_a, y_b))
    za = _mm_nn(y_a, Wa, BF16, "mm_za")
    zb = _mm_nn(y_b, Wb, BF16, "mm_zb")
    off_ga, off_gb = 5 * HA + 3 * HB, 5 * HA + 3 * HB + D
    z = _merge_fwd(za, zb, p, N, off_ga, off_gb)
    mo = _mm_nn(z, Wo, F32, "mm_mo")
    vec2 = jnp.concatenate([g1, norm2_g, sh2, sc2, jnp.zeros((4, D), F32)], axis=0)
    x_mid, h2 = _resid_rms2_fwd(x[0], mo, vec2)
    W1, W3, W2 = _gather_finish(gat_ffn, h2)
    W2 = W2.reshape(1, F, D)
    u1 = _mm_nn(h2, W1, BF16, "mm_u1")
    u3 = _mm_nn(h2, W3, BF16, "mm_u3")
    a = _convgate_fwd(u1, u3, cw8, ffn_conv_b)
    f = _mm_nn(a, W2, F32, "mm_f")
    dy, df, s_loss = _loss_head(x_mid, f, g2, loss_target[0])
    loss = lax.psum(s_loss[1, 0], ("x", "y", "c"))
    d_g2 = s_loss[0:1]

    gW2 = _mm_tn(a, df, 1, "mm_gw2").reshape(N_CHIP, F // N_CHIP, D)
    da = _mm_nt(df, W2, BF16, "mm_da")
    du1, du3, s_conv = _convgate_bwd(u1, u3, da, cw8, ffn_conv_b)
    gW1 = _mm_tn(h2, du1, N_CHIP, "mm_gw1")
    gW3 = _mm_tn(h2, du3, N_CHIP, "mm_gw3")
    rs_ffn = _rs_start("ffn", [gW2, gW1, gW3])
    dh2a = _mm_nt(du1, W1, F32, "mm_dh2a")
    dh2b = _mm_nt(du3, W3, F32, "mm_dh2b")
    rs_ffn = _rs_scatter(rs_ffn, dh2b)
    dxm, dmo, s_rms2 = _resid_rms2_bwd(x_mid, dh2a, dh2b, dy, mo, vec2)
    gWo = _mm_tn(z, dmo, 1, "mm_gwo").reshape(N_CHIP, D // N_CHIP, D)
    dz = _mm_nt(dmo, Wo, BF16, "mm_dz")
    dza, dzb, dga, dgb = _merge_bwd(dz, za, zb, p, N, off_ga, off_gb)
    gWa = _mm_tn(y_a, dza, N_CHIP, "mm_gwa")
    gWb = _mm_tn(y_b, dzb, N_CHIP, "mm_gwb")
    rs_mix = _rs_start("mix", [gWo, gWa, gWb])
    dya = _mm_nt(dza, Wa, F32, "mm_dya")
    dyb = _mm_nt(dzb, Wb, BF16, "mm_dyb")
    rs_mix = _rs_scatter(rs_mix, dyb)
    dq_a, dzf, dzbk, di_a, dog, dlbl, s_ng = _hgrn_bwd(p, lbl, hgrn_norm_g, o_a, dya, st_a, N, HA)
    rs_ffn = _rs_join(rs_ffn, dq_a)
    dq_n, dk_n, dv_n, dbias, s_qk = _na_bwd(p, bias, na_q_norm_g, na_k_norm_g, cos, sin, dyb, N, off_na, HB)
    rs_mix = _rs_join(rs_mix, dq_n)
    dp = jnp.concatenate([dq_a, dzf, dzbk, di_a, dog, dq_n, dk_n, dv_n, dga, dgb], axis=1)
    gWin = _mm_tn(h_all, dp, N_CHIP, "mm_gwin")
    rs_in = _rs_start("in", [gWin])
    rs_in = _rs_scatter(rs_in, gWin)
    dh = _mm_nt(dp, Win, F32, "mm_dh")
    grad_x, s_rms1 = _rms1_bwd(xall, dh, dxm, norm1_g, scale1, N)
    d_table = _bias_grad(dbias)

    grads = {}
    big_names = ["ada_w", "w_in", "w_branch_a", "w_branch_b", "w_out", "ffn_w1", "ffn_w3", "ffn_w2"]
    small_names = [n for n in order if n not in big_names]
    delta, new_m, new_v = {}, {}, {}

    def update(nm, after=None):
        reduced = nm != "ada_w"
        d_, m_, v_, *g_ = _adamw(weights[nm][0], grads[nm][0], moms[nm][0][0], moms[nm][1][0], "adamw_" + nm,
                                 after, copy_g=reduced)
        delta[nm], new_m[nm], new_v[nm] = d_[None], m_[None], v_[None]
        if reduced:
            grads[nm] = g_[0][None]
        return d_

    last = grad_x
    for nm, g in zip(["ffn_w2", "ffn_w1", "ffn_w3"], _rs_finish(rs_ffn, last)):
        grads[nm] = g[None]
        last = update(nm, last)
    for nm, g in zip(["w_out", "w_branch_a", "w_branch_b"], _rs_finish(rs_mix, last)):
        grads[nm] = g[None]
        last = update(nm, last)
    rs_in = _rs_join(rs_in, last)

    zD = jnp.zeros((1, D), F32)
    dmod_l = jnp.concatenate([s_rms1[2:3], s_rms1[3:4], s_rms2[3:4], s_rms2[0:1], s_rms2[1:2], d_g2], axis=0)
    dmod_c = jnp.concatenate([s_rms1[0:1], s_rms1[1:2], zD, zD, zD, zD], axis=0)
    pk1, offs1 = _pack([dmod_l, dmod_c, s_rms1[4], s_rms2[2], dlbl, s_ng[0], s_qk[0], s_qk[1], d_table,
                        s_conv[0:3], s_conv[3]])
    g1all = _allgather8(pk1, "gather_small1")
    tot1 = _sum8(g1all, "sum_small1")
    dmod_rows = _unpack(g1all, offs1, 0).reshape(N_DEV, N_MOD * D)
    dmod_c_tot = _unpack(tot1, offs1, 1).reshape(1, N_MOD * D)
    dmod16 = jnp.concatenate([dmod_rows, dmod_c_tot, jnp.zeros((7, N_MOD * D), F32)], axis=0)
    dmod16_mine = lax.dynamic_slice(dmod16, (0, chip * n_ada), (16, n_ada))
    g_ada_w, dact = _ada_bwd(cs, ada_w[0], dmod16_mine)
    pk2, offs2 = _pack([dact[8]])
    g2all = _allgather8(pk2, "gather_small2")
    dact_rows = _unpack(g2all, offs2, 0)
    dact_sel = jnp.concatenate([dact_rows[2 * j][None] for j in range(N_CHIP)] + [jnp.zeros((4, D), F32)], axis=0)

    grads["ada_w"] = g_ada_w[None]
    grads["ada_b"] =(_unpack(tot1, offs1, 0) + _unpack(tot1, offs1, 1)).reshape(1, N_MOD * D)
    grads["norm1_g"] = _unpack(tot1, offs1, 2)[None]
    grads["norm2_g"] = _unpack(tot1, offs1, 3)[None]
    g_lbl = _unpack(tot1, offs1, 4)
    n_lb = HA // N_CHIP
    grads["hgrn_lb_logits"] = lax.dynamic_slice(g_lbl, (0, 0, chip * n_lb), (2, 2, n_lb))
    grads["hgrn_norm_g"] = _unpack(tot1, offs1, 5)[None]
    grads["na_q_norm_g"] = _unpack(tot1, offs1, 6)[None]
    grads["na_k_norm_g"] = _unpack(tot1, offs1, 7)[None]
    grads["na_rel_bias"] = _unpack(tot1, offs1, 8)[None]
    g_cw = _unpack(tot1, offs1, 9)
    n_f = F // N_CHIP
    grads["ffn_conv_w"] = lax.dynamic_slice(g_cw, (0, chip * n_f), (3, n_f))[None]
    grads["ffn_conv_b"] = _unpack(tot1, offs1, 10)[None]

    g_c_ctx = _dsilu_rows(dact_sel, c_ctx[None, :], "grad_c_ctx")
    grads["c_ctx"] = g_c_ctx[0]

    last = update("ada_w", g_c_ctx)
    pw, offw = _pack([weights[n] for n in small_names])
    pg, _ = _pack([grads[n] for n in small_names])
    pm, _ = _pack([moms[n][0] for n in small_names])
    pv, _ = _pack([moms[n][1] for n in small_names])
    d_, m_, v_ = _adamw(pw, pg, pm, pv, "adamw_small", last)
    for i, nm in enumerate(small_names):
        delta[nm], new_m[nm], new_v[nm] = _unpack(d_, offw, i), _unpack(m_, offw, i), _unpack(v_, offw, i)
    grads["w_in"] = _rs_finish(rs_in, d_)[0][None]
    update("w_in")

    return (loss, grad_x[None], *[grads[n] for n in order], *[delta[n] for n in order],
            *[new_m[n] for n in order], *[new_v[n] for n in order])


def _dsilu_rows(v, cv, name):
    D = v.shape[1]

    def body(v_ref, c_ref, o_ref):
        t = c_ref[...]
        s = _sigmoid(t)
        o_ref[...] = (((v_ref[0:1, :] + v_ref[1:2, :]) + v_ref[2:3, :]) + v_ref[3:4, :]) * (s * (1.0 + t * (1.0 - s)))

    return _pcall(body, name=name, out_shape=jax.ShapeDtypeStruct((1, D), F32),
                          compiler_params=_params())(v, cv)
```

```python
import functools

import numpy as np
import jax
import jax.numpy as jnp
from jax import lax
from jax.experimental import pallas as pl
from jax.experimental.pallas import tpu as pltpu

F32 = jnp.float32
BF16 = jnp.bfloat16
MESH = pl.DeviceIdType.MESH

HEAD = 128
GRID_W = 64
WIN_R = 8
WIN_C = 16
ROPE_THETA = 10000.0
EPS = 1e-6
N_MOD = 6
CHUNK = 16
HGRN_UNROLL = 4
ADAM_LR = 0.001
ADAM_B1 = 0.9
ADAM_B2 = 0.999
ADAM_EPS = 1e-08
ADAM_WD = 0.01
ADAM_STEP = 10
NEG = -1e30
VMEM_LIMIT = 56 * 1024 * 1024
N_DEV = 8
N_CHIP = 4
HI = lax.Precision.HIGHEST


def _pick(n, cands):
    for c in cands:
        if n % c == 0:
            return c
    return n


def _row_tile(rows, cols, target_bytes=1 << 20):
    want = max(16, target_bytes // (4 * cols))
    for t in (512, 256, 128, 64, 32, 16, 8):
        if t <= want and rows % t == 0:
            return t
    return rows


def _params(sem=None):
    return pltpu.CompilerParams(dimension_semantics=sem, vmem_limit_bytes=VMEM_LIMIT)


def _dot(a, b):
    return jnp.dot(a, b, preferred_element_type=F32)


def _dot_nt(a, b):
    return lax.dot_general(a, b, (((1,), (1,)), ((), ())), preferred_element_type=F32)


def _dot_tn(a, b):
    return lax.dot_general(a, b, (((0,), (0,)), ((), ())), preferred_element_type=F32)


def _sigmoid(x):
    return 1.0 / (1.0 + jnp.exp(-x))


def _col_tile(n):
    return n if n <= 1536 else _pick(n, (1024, 768, 512, 384, 256, 128))


def _mm_nn(x, w3, out_dtype, name):
    M, K = x.shape
    S, _, n = w3.shape
    tm = _pick(M, (768, 512, 256, 128, 64))
    tn = _col_tile(n)
    nb = n // tn

    def body(x_ref, w_ref, o_ref):
        o_ref[...] = _dot(x_ref[...].astype(BF16), w_ref[0]).astype(o_ref.dtype)

    return _pcall(
        body, name=name, grid=(M // tm, S * nb),
        in_specs=[pl.BlockSpec((tm, K), lambda i, j: (i, 0)),
                  pl.BlockSpec((1, K, tn), lambda i, j: (j // nb, 0, j % nb))],
        out_specs=pl.BlockSpec((tm, tn), lambda i, j: (i, j)),
        out_shape=jax.ShapeDtypeStruct((M, S * n), out_dtype),
        compiler_params=_params(("parallel", "parallel")),
    )(x, w3)


def _mm_nn_sel(x, w3, sel, out_dtype, name, prev=None):
    M, K = x.shape
    S, _, n = w3.shape
    tm = _pick(M, (768, 512, 256, 128, 64))
    tn = _col_tile(n)
    nb = n // tn
    k = sel.shape[0]

    def body(sel_ref, x_ref, w_ref, *rest):
        rest[-1][...] = _dot(x_ref[...].astype(BF16), w_ref[0]).astype(out_dtype)

    in_specs = [pl.BlockSpec((tm, K), lambda i, j, sel_ref: (i, 0)),
                pl.BlockSpec((1, K, tn), lambda i, j, sel_ref: (sel_ref[j // nb], 0, j % nb))]
    operands = [sel, x, w3]
    if prev is not None:
        in_specs.append(_ANY)
        operands.append(prev)
    return pl.pallas_call(
        body, name=name,
        grid_spec=pltpu.PrefetchScalarGridSpec(
            num_scalar_prefetch=1, grid=(M // tm, k * nb), in_specs=in_specs,
            out_specs=pl.BlockSpec((tm, tn), lambda i, j, sel_ref: (i, sel_ref[j // nb] * nb + j % nb))),
        out_shape=jax.ShapeDtypeStruct((M, S * n), out_dtype),
        input_output_aliases={} if prev is None else {3: 0},
        compiler_params=_params(("parallel", "parallel")),
    )(*operands)


def _mm_nt(dy, w3, out_dtype, name):
    M = dy.shape[0]
    S, K, n = w3.shape
    tm = _pick(M, (768, 512, 256, 128, 64))
    tk = K if K <= 2048 else _pick(K, (1408, 1024, 512, 256, 128))
    tc = n if n <= 2048 else _col_tile(n)
    nb = n // tc
    nsteps = S * nb

    def body(dy_ref, w_ref, o_ref, acc_ref):
        s = pl.program_id(2)

        @pl.when(s == 0)
        def _():
            acc_ref[...] = jnp.zeros_like(acc_ref)

        acc_ref[...] += _dot_nt(dy_ref[...].astype(BF16), w_ref[0])

        @pl.when(s == nsteps - 1)
        def _():
            o_ref[...] = acc_ref[...].astype(o_ref.dtype)

    return _pcall(
        body, name=name, grid=(M // tm, K // tk, nsteps),
        in_specs=[pl.BlockSpec((tm, tc), lambda i, k, s: (i, s)),
                  pl.BlockSpec((1, tk, tc), lambda i, k, s: (s // nb, k, s % nb))],
        out_specs=pl.BlockSpec((tm, tk), lambda i, k, s: (i, k)),
        out_shape=jax.ShapeDtypeStruct((M, K), out_dtype),
        scratch_shapes=[pltpu.VMEM((tm, tk), F32)],
        compiler_params=_params(("parallel", "parallel", "arbitrary")),
    )(dy, w3)


def _mm_tn(x, dy, S, name):
    M, K = x.shape
    n = dy.shape[1] // S
    tk = _pick(K, (512, 256, 128))
    tn = _col_tile(n)
    nb = n // tn

    def body(x_ref, dy_ref, o_ref):
        o_ref[0] = _dot_tn(x_ref[...].astype(BF16), dy_ref[...].astype(BF16)).astype(BF16)

    return _pcall(
        body, name=name, grid=(S * nb, K // tk),
        in_specs=[pl.BlockSpec((M, tk), lambda j, k: (0, k)),
                  pl.BlockSpec((M, tn), lambda j, k: (0, j))],
        out_specs=pl.BlockSpec((1, tk, tn), lambda j, k: (j // nb, k, j % nb)),
        out_shape=jax.ShapeDtypeStruct((S, K, n), BF16),
        compiler_params=_params(("parallel", "parallel")),
    )(x, dy)


def _chip_index():
    return (2 * lax.axis_index("x") + lax.axis_index("y")).astype(jnp.int32).reshape(1)


def _cast_bf16_slot(w, name):
    R, C = w.shape
    tr = _row_tile(R, C, 2 << 20)

    def body(j_ref, w_ref, o_ref):
        o_ref[0] = w_ref[...].astype(BF16)

    return _pcall(
        body, name=name,
        grid_spec=pltpu.PrefetchScalarGridSpec(
            num_scalar_prefetch=1, grid=(R // tr,),
            in_specs=[pl.BlockSpec((tr, C), lambda i, j_ref: (i, 0))],
            out_specs=pl.BlockSpec((1, tr, C), lambda i, j_ref: (j_ref[0], i, 0))),
        out_shape=jax.ShapeDtypeStruct((N_CHIP, R, C), BF16),
        compiler_params=_params(("parallel",)),
    )(_chip_index(), w)


def _pos():
    return lax.axis_index("x"), lax.axis_index("y"), lax.axis_index("c")


def _other_chips(x, y):
    return [(x, 1 - y), (1 - x, y), (1 - x, 1 - y)]


def _allgather8(v, name):
    R, C = v.shape

    def body(x_ref, out_ref, send_sems, recv_sems, local_sem):
        x, y, c = _pos()
        me, sibling = (x, y, c), (x, y, 1 - c)
        chips = _other_chips(x, y)

        def slot(px, py, pc):
            return out_ref.at[4 * px + 2 * py + pc]

        def copy(k, block, to, src=None):
            return pltpu.make_async_remote_copy(
                src_ref=slot(*block) if src is None else src, dst_ref=slot(*block),
                send_sem=send_sems.at[k], recv_sem=recv_sems.at[k], device_id=to, device_id_type=MESH)

        mine = pltpu.make_async_copy(x_ref, slot(*me), local_sem)
        mine.start()
        first = [copy(0, me, sibling, src=x_ref)]
        first += [copy(1 + j, me, (*chip, c), src=x_ref) for j, chip in enumerate(chips)]
        for cp in first:
            cp.start()
        passed = [copy(4 + j, (*chip, c), sibling) for j, chip in enumerate(chips)]
        for j, chip in enumerate(chips):
            copy(1 + j, (*chip, c), me).wait_recv()
            passed[j].start()
        copy(0, sibling, me).wait_recv()
        for j, chip in enumerate(chips):
            copy(4 + j, (*chip, 1 - c), me).wait_recv()
        for cp in first + passed:
            cp.wait_send()
        mine.wait()

    return _pcall(
        body, name=name,
        out_shape=jax.ShapeDtypeStruct((N_DEV, R, C), v.dtype),
        in_specs=[pl.BlockSpec(memory_space=pltpu.VMEM)],
        out_specs=pl.BlockSpec(memory_space=pltpu.VMEM),
        scratch_shapes=[pltpu.SemaphoreType.DMA((7,)), pltpu.SemaphoreType.DMA((7,)), pltpu.SemaphoreType.DMA],
        compiler_params=pltpu.CompilerParams(vmem_limit_bytes=VMEM_LIMIT),
    )(v)


_HBM = pl.BlockSpec(memory_space=pltpu.HBM)
_SEM = pl.BlockSpec(memory_space=pltpu.SEMAPHORE)
_ANY = pl.BlockSpec(memory_space=pl.ANY)
_EFFECT = pltpu.SideEffectType.DATAFLOW_SIDE_EFFECTING
_PENDING = []


def _pcall(body, **kw):
    def run(*operands):
        if not _PENDING or "in_specs" not in kw:
            return pl.pallas_call(body, **kw)(*operands)
        deps = list(_PENDING)
        n = len(operands)

        def tied(*refs):
            return body(*refs[:n], *refs[n + len(deps):])

        return pl.pallas_call(tied, **{**kw, "in_specs": list(kw["in_specs"]) + [_ANY] * len(deps)})(*operands, *deps)
    return run


def _copies(plan, refs, send_sems, recv_sems):
    return [pltpu.make_async_remote_copy(src_ref=src, dst_ref=dst, send_sem=send_sems.at[k], recv_sem=recv_sems.at[k],
                                         device_id=dev, device_id_type=MESH)
            for k, (src, dst, dev) in enumerate(plan(refs))]


def _xfer_start(name, bufs, plan, n_copies, after=None):
    n = len(bufs)
    deps = list(_PENDING) + ([after] if after is not None else [])
    nd = len(deps)

    def body(*refs):
        for cp in _copies(plan, refs[:n], refs[n + nd], refs[n + nd + 1]):
            cp.start()
        refs[-1][...] = jnp.zeros_like(refs[-1])

    outs = pl.pallas_call(
        body, name=name,
        out_shape=(pltpu.SemaphoreType.DMA((n_copies,)), pltpu.SemaphoreType.DMA((n_copies,)),
                   *[pltpu.HBM(b.shape, b.dtype) for b in bufs], jax.ShapeDtypeStruct((8, 128), F32)),
        in_specs=[_HBM] * n + [_ANY] * nd,
        out_specs=(_SEM, _SEM, *[_HBM] * n, pl.BlockSpec(memory_space=pltpu.VMEM)),
        input_output_aliases={t: 2 + t for t in range(n)},
        compiler_params=pltpu.CompilerParams(has_side_effects=_EFFECT),
    )(*[pltpu.with_memory_space_constraint(b, pltpu.HBM) for b in bufs], *deps)
    _PENDING[:] = [outs[-1]]
    return (outs[0], outs[1]), list(outs[2:2 + n])


def _xfer_wait(name, sems, bufs, plan, after):
    n = len(bufs)
    after = tuple(after) if isinstance(after, (tuple, list)) else (after,)

    def body(*refs):
        cps = _copies(plan, refs[:n], refs[n], refs[n + 1])
        for cp in cps:
            cp.wait_send()
        for cp in cps:
            cp.wait_recv()

    outs = pl.pallas_call(
        body, name=name,
        out_shape=tuple(pltpu.HBM(b.shape, b.dtype) for b in bufs),
        in_specs=[_HBM] * n + [_SEM, _SEM] + [_ANY] * len(after),
        out_specs=tuple([_HBM] * n),
        input_output_aliases={t: t for t in range(n)},
        compiler_params=pltpu.CompilerParams(has_side_effects=_EFFECT),
    )(*bufs, sems[0], sems[1], *after)
    return list(outs)


def _half(ref_rows, hc):
    h = ref_rows // 2
    return pl.ds(hc * h, h)


ALL_CHIPS = (0, 1, 2)
NEIGHBOURS = (0, 1)
DIAGONAL = (2,)


def _plan_gather_ici(which):
    def plan(bufs):
        x, y, c = _pos()
        j = 2 * x + y
        chips = _other_chips(x, y)
        return [(b.at[j, _half(b.shape[1], c)], b.at[j, _half(b.shape[1], c)], (*chips[k], c))
                for b in bufs for k in which]
    return plan


def _plan_gather_d2d(which):
    def plan(bufs):
        x, y, c = _pos()
        chips = _other_chips(x, y)
        out = []
        for b in bufs:
            for k in which:
                blk = b.at[2 * chips[k][0] + chips[k][1], _half(b.shape[1], c)]
                out.append((blk, blk, (x, y, 1 - c)))
        return out
    return plan


def _plan_pair_swap(n):
    def plan(bufs):
        x, y, c = _pos()
        return [(g.at[:, _half(g.shape[1], 1 - c)], land, (x, y, 1 - c)) for g, land in zip(bufs[:n], bufs[n:])]
    return plan


def _plan_chip_scatter(n):
    def plan(bufs):
        x, y, c = _pos()
        return [(p.at[2 * chip[0] + chip[1]], land.at[k], (*chip, c))
                for p, land in zip(bufs[:n], bufs[n:]) for k, chip in enumerate(_other_chips(x, y))]
    return plan


def _plan_pair_join(bufs):
    x, y, c = _pos()
    return [(b.at[_half(b.shape[0], c)], b.at[_half(b.shape[0], c)], (x, y, 1 - c)) for b in bufs]


def _empty_hbm(shape, dtype):
    return pltpu.with_memory_space_constraint(lax.empty(shape, dtype), pltpu.HBM)


def _gather_start(tag, bufs, after=None):
    sems, bufs = _xfer_start(f"gather_ici_start_{tag}", bufs, _plan_gather_ici(ALL_CHIPS), 3 * len(bufs), after)
    return dict(tag=tag, sems=sems, bufs=bufs)


def _gather_mid(st, after):
    tag = st["tag"]
    bufs = _xfer_wait(f"gather_ici_wait_{tag}", st["sems"], st["bufs"], _plan_gather_ici(ALL_CHIPS), after)
    sems, bufs = _xfer_start(f"gather_d2d_start_{tag}", bufs, _plan_gather_d2d(ALL_CHIPS), 3 * len(bufs))
    return dict(tag=tag, sems=sems, bufs=bufs)


def _gather_finish(st, after):
    return _xfer_wait(f"gather_d2d_wait_{st['tag']}", st["sems"], st["bufs"], _plan_gather_d2d(ALL_CHIPS), after)


def _d2d_hand_over(tag, sems, bufs, which, after):
    bufs = _xfer_wait(f"gather_ici_wait_{tag}", sems, bufs, _plan_gather_ici(which), after)
    sems, bufs = _xfer_start(f"gather_d2d_start_{tag}", bufs, _plan_gather_d2d(which), len(which) * len(bufs))
    return _xfer_wait(f"gather_d2d_wait_{tag}", sems, bufs, _plan_gather_d2d(which), after)


def _pair_add(g, r, name):
    S, R, C = g.shape
    h = R // 2
    tr = _row_tile(h, C)
    nb = h // tr

    def body(c_ref, g_ref, r_ref, o_ref):
        o_ref[...] = (g_ref[...].astype(F32) + r_ref[...].astype(F32)).astype(BF16)

    return _pcall(
        body, name=name,
        grid_spec=pltpu.PrefetchScalarGridSpec(
            num_scalar_prefetch=1, grid=(S, nb),
            in_specs=[pl.BlockSpec((1, tr, C), lambda s, i, c_ref: (s, c_ref[0] * nb + i, 0)),
                      pl.BlockSpec((1, tr, C), lambda s, i, c_ref: (s, i, 0))],
            out_specs=pl.BlockSpec((1, tr, C), lambda s, i, c_ref: (s, i, 0))),
        out_shape=jax.ShapeDtypeStruct((S, h, C), BF16),
        compiler_params=_params(("parallel", "parallel")),
    )(lax.axis_index("c").astype(jnp.int32).reshape(1), g, r)


def _chip_sum(p, rb, name):
    S, h, C = p.shape
    tr = _row_tile(h, C)
    nb = h // tr
    jc = jnp.concatenate([_chip_index(), lax.axis_index("c").astype(jnp.int32).reshape(1)])

    def body(jc_ref, p_ref, r_ref, o_ref):
        o_ref[...] = ((p_ref[0].astype(F32) + r_ref[0].astype(F32)) + r_ref[1].astype(F32)) + r_ref[2].astype(F32)

    return _pcall(
        body, name=name,
        grid_spec=pltpu.PrefetchScalarGridSpec(
            num_scalar_prefetch=1, grid=(nb,),
            in_specs=[pl.BlockSpec((1, tr, C), lambda i, jc_ref: (jc_ref[0], i, 0)),
                      pl.BlockSpec((3, tr, C), lambda i, jc_ref: (0, i, 0))],
            out_specs=pl.BlockSpec((tr, C), lambda i, jc_ref: (jc_ref[1] * nb + i, 0))),
        out_shape=jax.ShapeDtypeStruct((2 * h, C), F32),
        compiler_params=_params(("parallel",)),
    )(jc, p, rb)


def _rs_start(tag, gs):
    n = len(gs)
    lands = [_empty_hbm((g.shape[0], g.shape[1] // 2, g.shape[2]), g.dtype) for g in gs]
    sems, bufs = _xfer_start(f"rs_swap_start_{tag}", list(gs) + lands, _plan_pair_swap(n), n)
    return dict(tag=tag, n=n, sems=sems, bufs=bufs)


def _rs_scatter(st, after):
    tag, n = st["tag"], st["n"]
    bufs = _xfer_wait(f"rs_swap_wait_{tag}", st["sems"], st["bufs"], _plan_pair_swap(n), after)
    ps = [_pair_add(g, r, f"rs_pair_add_{tag}{t}") for t, (g, r) in enumerate(zip(bufs[:n], bufs[n:]))]
    lands = [_empty_hbm((3,) + p.shape[1:], p.dtype) for p in ps]
    sems, bufs = _xfer_start(f"rs_scatter_start_{tag}", ps + lands, _plan_chip_scatter(n), 3 * n)
    return dict(tag=tag, n=n, sems=sems, bufs=bufs)


def _rs_join(st, after):
    tag, n = st["tag"], st["n"]
    bufs = _xfer_wait(f"rs_scatter_wait_{tag}", st["sems"], st["bufs"], _plan_chip_scatter(n), after)
    fs = [_chip_sum(p, rb, f"rs_chip_sum_{tag}{t}") for t, (p, rb) in enumerate(zip(bufs[:n], bufs[n:]))]
    sems, bufs = _xfer_start(f"rs_join_start_{tag}", fs, _plan_pair_join, n)
    return dict(tag=tag, n=n, sems=sems, bufs=bufs)


def _rs_finish(st, after):
    return _xfer_wait(f"rs_join_wait_{st['tag']}", st["sems"], st["bufs"], _plan_pair_join, after)


def _sum8(g, name):
    _, R, C = g.shape

    def body(g_ref, o_ref):
        acc = g_ref[0]
        for d in range(1, N_DEV):
            acc = acc + g_ref[d]
        o_ref[...] = acc

    return _pcall(body, name=name, out_shape=jax.ShapeDtypeStruct((R, C), F32),
                          compiler_params=_params())(g)


def _ada_fwd(cs, w, b):
    D, n = w.shape
    tn = _pick(n, (512, 384, 256, 128))

    def body(c_ref, w_ref, b_ref, o_ref):
        cv = c_ref[...]
        a = (cv * _sigmoid(cv)).astype(BF16)
        o_ref[...] = _dot(a, w_ref[...].astype(BF16)) + b_ref[...]

    return _pcall(
        body, name="ada_fwd", grid=(n // tn,),
        in_specs=[pl.BlockSpec((16, D), lambda j: (0, 0)), pl.BlockSpec((D, tn), lambda j: (0, j)),
                  pl.BlockSpec((1, tn), lambda j: (0, j))],
        out_specs=pl.BlockSpec((16, tn), lambda j: (0, j)),
        out_shape=jax.ShapeDtypeStruct((16, n), F32),
        compiler_params=_params(("parallel",)),
    )(cs, w, b)


def _ada_bwd(cs, w, dmod):
    D, n = w.shape
    tn = _pick(n, (512, 384, 256, 128))

    def body(c_ref, w_ref, d_ref, gw_ref, da_ref):
        j = pl.program_id(0)
        cv = c_ref[...]
        a = cv * _sigmoid(cv)
        d = d_ref[...]
        gw_ref[...] = lax.dot_general(a, d, (((0,), (0,)), ((), ())), precision=HI, preferred_element_type=F32)

        @pl.when(j == 0)
        def _():
            da_ref[...] = jnp.zeros_like(da_ref)

        da_ref[...] += _dot_nt(d.astype(BF16), w_ref[...].astype(BF16))

    return _pcall(
        body, name="ada_bwd", grid=(n // tn,),
        in_specs=[pl.BlockSpec((16, D), lambda j: (0, 0)), pl.BlockSpec((D, tn), lambda j: (0, j)),
                  pl.BlockSpec((16, tn), lambda j: (0, j))],
        out_specs=[pl.BlockSpec((D, tn), lambda j: (0, j)), pl.BlockSpec((16, D), lambda j: (0, 0))],
        out_shape=[jax.ShapeDtypeStruct((D, n), F32), jax.ShapeDtypeStruct((16, D), F32)],
        compiler_params=_params(("arbitrary",)),
    )(cs, w, dmod)


def _rms1_fwd(xall, gain, shift2, scale2, n_ctx):
    T, D = xall.shape
    tb = _pick(n_ctx, (256, 128, 64, 32, 16))
    nctx = n_ctx // tb

    def body(x_ref, g_ref, sh_ref, sc_ref, o_ref):
        i = pl.program_id(0)
        xv = x_ref[...]
        r = lax.rsqrt(jnp.mean(xv * xv, axis=-1, keepdims=True) + EPS)
        nrm = xv * r * g_ref[...]
        lat = i >= nctx
        sh = jnp.where(lat, sh_ref[1:2, :], sh_ref[0:1, :])
        sc = jnp.where(lat, sc_ref[1:2, :], sc_ref[0:1, :])
        o_ref[...] = (nrm * (1.0 + sc) + sh).astype(BF16)

    vec = lambda r: pl.BlockSpec((r, D), lambda i: (0, 0))
    return _pcall(
        body, name="rms1_fwd", grid=(T // tb,),
        in_specs=[pl.BlockSpec((tb, D), lambda i: (i, 0)), vec(1), vec(2), vec(2)],
        out_specs=pl.BlockSpec((tb, D), lambda i: (i, 0)),
        out_shape=jax.ShapeDtypeStruct((T, D), BF16),
        compiler_params=_params(("parallel",)),
    )(xall, gain, shift2, scale2)


def _rms1_bwd(xall, dh, dxmid, gain, scale2, n_ctx):
    T, D = xall.shape
    L = T - n_ctx
    tb = _pick(n_ctx, (256, 128, 64, 32, 16))
    nctx = n_ctx // tb

    def body(x_ref, dh_ref, dxm_ref, g_ref, sc_ref, dx_ref, cs_ref):
        i = pl.program_id(0)
        lat = i >= nctx
        xv = x_ref[...]
        r = lax.rsqrt(jnp.mean(xv * xv, axis=-1, keepdims=True) + EPS)
        xh = xv * r
        g = g_ref[...]
        nrm = xh * g
        sc = jnp.where(lat, sc_ref[1:2, :], sc_ref[0:1, :])
        dhv = dh_ref[...]
        dn = dhv * (1.0 + sc)
        dxh = dn * g
        dxv = r * (dxh - xh * jnp.mean(dxh * xh, axis=-1, keepdims=True))
        s_sh = jnp.sum(dhv, axis=0, keepdims=True)
        s_sc = jnp.sum(dhv * nrm, axis=0, keepdims=True)
        s_g = jnp.sum(dn * xh, axis=0, keepdims=True)
        zero = jnp.zeros_like(s_sh)
        rows = lax.broadcasted_iota(jnp.int32, (8, D), 0)
        upd = jnp.where(rows == 0, jnp.where(lat, zero, s_sh),
              jnp.where(rows == 1, jnp.where(lat, zero, s_sc),
              jnp.where(rows == 2, jnp.where(lat, s_sh, zero),
              jnp.where(rows == 3, jnp.where(lat, s_sc, zero),
              jnp.where(rows == 4, s_g, 0.0)))))

        @pl.when(i == 0)
        def _():
            cs_ref[...] = jnp.zeros_like(cs_ref)

        cs_ref[...] += upd

        @pl.when(lat)
        def _():
            dx_ref[...] = dxv + dxm_ref[...]

    lat_blk = lambda i: (jnp.maximum(i - nctx, 0), 0)
    vec = lambda r: pl.BlockSpec((r, D), lambda i: (0, 0))
    return _pcall(
        body, name="rms1_bwd", grid=(T // tb,),
        in_specs=[pl.BlockSpec((tb, D), lambda i: (i, 0)), pl.BlockSpec((tb, D), lambda i: (i, 0)),
                  pl.BlockSpec((tb, D), lat_blk), vec(1), vec(2)],
        out_specs=[pl.BlockSpec((tb, D), lat_blk), vec(8)],
        out_shape=[jax.ShapeDtypeStruct((L, D), F32), jax.ShapeDtypeStruct((8, D), F32)],
        compiler_params=_params(("arbitrary",)),
    )(xall, dh, dxmid, gain, scale2)


def _resid_rms2_fwd(x, mo, vecs):
    L, D = x.shape
    tb = _pick(L, (256, 128, 64))

    def body(x_ref, mo_ref, v_ref, xm_ref, h_ref):
        xm = x_ref[...] + v_ref[0:1, :] * mo_ref[...]
        xm_ref[...] = xm
        r = lax.rsqrt(jnp.mean(xm * xm, axis=-1, keepdims=True) + EPS)
        h_ref[...] = (xm * r * v_ref[1:2, :] * (1.0 + v_ref[3:4, :]) + v_ref[2:3, :]).astype(BF16)

    blk = pl.BlockSpec((tb, D), lambda i: (i, 0))
    return _pcall(
        body, name="resid_rms2_fwd", grid=(L // tb,),
        in_specs=[blk, blk, pl.BlockSpec((8, D), lambda i: (0, 0))],
        out_specs=[blk, blk],
        out_shape=[jax.ShapeDtypeStruct((L, D), F32), jax.ShapeDtypeStruct((L, D), BF16)],
        compiler_params=_params(("parallel",)),
    )(x, mo, vecs)


def _resid_rms2_bwd(xmid, dh_a, dh_b, dy, mo, vecs):
    L, D = xmid.shape
    tb = _pick(L, (256, 128, 64))

    def body(xm_ref, da_ref, db_ref, dy_ref, mo_ref, v_ref, dxm_ref, dmo_ref, cs_ref):
        i = pl.program_id(0)
        xm = xm_ref[...]
        r = lax.rsqrt(jnp.mean(xm * xm, axis=-1, keepdims=True) + EPS)
        xh = xm * r
        g = v_ref[1:2, :]
        nrm = xh * g
        dhv = da_ref[...] + db_ref[...]
        dn = dhv * (1.0 + v_ref[3:4, :])
        dxh = dn * g
        dxm = dy_ref[...] + r * (dxh - xh * jnp.mean(dxh * xh, axis=-1, keepdims=True))
        dxm_ref[...] = dxm
        dmo_ref[...] = (dxm * v_ref[0:1, :]).astype(BF16)
        s0 = jnp.sum(dhv, axis=0, keepdims=True)
        s1 = jnp.sum(dhv * nrm, axis=0, keepdims=True)
        s2 = jnp.sum(dn * xh, axis=0, keepdims=True)
        s3 = jnp.sum(dxm * mo_ref[...], axis=0, keepdims=True)
        rows = lax.broadcasted_iota(jnp.int32, (8, D), 0)
        upd = jnp.where(rows == 0, s0, jnp.where(rows == 1, s1, jnp.where(rows == 2, s2,
              jnp.where(rows == 3, s3, 0.0))))

        @pl.when(i == 0)
        def _():
            cs_ref[...] = jnp.zeros_like(cs_ref)

        cs_ref[...] += upd

    blk = pl.BlockSpec((tb, D), lambda i: (i, 0))
    vec = pl.BlockSpec((8, D), lambda i: (0, 0))
    return _pcall(
        body, name="resid_rms2_bwd", grid=(L // tb,),
        in_specs=[blk, blk, blk, blk, blk, vec],
        out_specs=[blk, blk, vec],
        out_shape=[jax.ShapeDtypeStruct((L, D), F32), jax.ShapeDtypeStruct((L, D), BF16),
                   jax.ShapeDtypeStruct((8, D), F32)],
        compiler_params=_params(("arbitrary",)),
    )(xmid, dh_a, dh_b, dy, mo, vecs)


def _loss_head(xmid, f, g2, target):
    L, D = xmid.shape
    tb = _pick(L, (256, 128, 64))

    def body(xm_ref, f_ref, g_ref, t_ref, dy_ref, df_ref, s_ref):
        i = pl.program_id(0)
        fv = f_ref[...]
        g = g_ref[...]
        err = xm_ref[...] + g * fv - t_ref[...]
        dy = err * (1.0 / D)
        dy_ref[...] = dy
        df_ref[...] = (dy * g).astype(BF16)
        s0 = jnp.sum(dy * fv, axis=0, keepdims=True)
        part = 0.5 * jnp.sum(jnp.mean(err * err, axis=-1, keepdims=True), axis=0, keepdims=True)
        rows = lax.broadcasted_iota(jnp.int32, (8, D), 0)
        upd = jnp.where(rows == 0, s0, jnp.where(rows == 1, part, 0.0))

        @pl.when(i == 0)
        def _():
            s_ref[...] = jnp.zeros_like(s_ref)

        s_ref[...] += upd

    blk = pl.BlockSpec((tb, D), lambda i: (i, 0))
    return _pcall(
        body, name="loss_head", grid=(L // tb,),
        in_specs=[blk, blk, pl.BlockSpec((1, D), lambda i: (0, 0)), blk],
        out_specs=[blk, blk, pl.BlockSpec((8, D), lambda i: (0, 0))],
        out_shape=[jax.ShapeDtypeStruct((L, D), F32), jax.ShapeDtypeStruct((L, D), BF16),
                   jax.ShapeDtypeStruct((8, D), F32)],
        compiler_params=_params(("arbitrary",)),
    )(xmid, f, g2, target)


def _gate_cols(D, off):
    tc = _pick(np.gcd(D, off), (512, 256, 128))
    return tc, off // tc


def _merge_fwd(za, zb, p, n_ctx, off_a, off_b):
    L, D = za.shape
    tb = _pick(n_ctx, (256, 128, 64, 32, 16))
    nctx = n_ctx // tb
    tc, oa = _gate_cols(D, off_a)
    _, ob = _gate_cols(D, off_b)
    if off_b % tc:
        raise ValueError("gate column offsets must share a column tile")
    ob = off_b // tc

    def body(za_ref, zb_ref, ga_ref, gb_ref, z_ref):
        z_ref[...] = (_sigmoid(ga_ref[...]) * za_ref[...].astype(F32)
                      + _sigmoid(gb_ref[...]) * zb_ref[...].astype(F32)).astype(BF16)

    blk = pl.BlockSpec((tb, tc), lambda i, j: (i, j))
    return _pcall(
        body, name="merge_fwd", grid=(L // tb, D // tc),
        in_specs=[blk, blk, pl.BlockSpec((tb, tc), lambda i, j: (i + nctx, oa + j)),
                  pl.BlockSpec((tb, tc), lambda i, j: (i + nctx, ob + j))],
        out_specs=blk,
        out_shape=jax.ShapeDtypeStruct((L, D), BF16),
        compiler_params=_params(("parallel", "parallel")),
    )(za, zb, p, p)


def _merge_bwd(dz, za, zb, p, n_ctx, off_a, off_b):
    L, D = za.shape
    T = L + n_ctx
    tb = _pick(n_ctx, (256, 128, 64, 32, 16))
    nctx = n_ctx // tb
    tc = _gate_cols(D, off_a)[0]
    oa, ob = off_a // tc, off_b // tc

    def body(dz_ref, za_ref, zb_ref, ga_ref, gb_ref, dza_ref, dzb_ref, dga_ref, dgb_ref):
        i = pl.program_id(1)

        @pl.when(i < nctx)
        def _():
            dga_ref[...] = jnp.zeros_like(dga_ref)
            dgb_ref[...] = jnp.zeros_like(dgb_ref)

        @pl.when(i >= nctx)
        def _():
            dzv = dz_ref[...].astype(F32)
            sa = _sigmoid(ga_ref[...])
            sb = _sigmoid(gb_ref[...])
            dza_ref[...] = (dzv * sa).astype(BF16)
            dzb_ref[...] = (dzv * sb).astype(BF16)
            dga_ref[...] = (dzv * za_ref[...].astype(F32) * sa * (1.0 - sa)).astype(BF16)
            dgb_ref[...] = (dzv * zb_ref[...].astype(F32) * sb * (1.0 - sb)).astype(BF16)

    lat = pl.BlockSpec((tb, tc), lambda j, i: (jnp.maximum(i - nctx, 0), j))
    allr = pl.BlockSpec((tb, tc), lambda j, i: (i, j))
    return _pcall(
        body, name="merge_bwd", grid=(D // tc, T // tb),
        in_specs=[lat, lat, lat, pl.BlockSpec((tb, tc), lambda j, i: (i, oa + j)),
                  pl.BlockSpec((tb, tc), lambda j, i: (i, ob + j))],
        out_specs=[lat, lat, allr, allr],
        out_shape=[jax.ShapeDtypeStruct((L, D), BF16), jax.ShapeDtypeStruct((L, D), BF16),
                   jax.ShapeDtypeStruct((T, D), BF16), jax.ShapeDtypeStruct((T, D), BF16)],
        compiler_params=_params(("arbitrary", "arbitrary")),
    )(dz, za, zb, p, p)


def _shift_down(u, rows):
    return jnp.where(rows == 0, 0.0, pltpu.roll(u, 1, 0))


def _shift_up(u, rows):
    n = u.shape[0]
    return jnp.where(rows == n - 1, 0.0, pltpu.roll(u, n - 1, 0))


def _convgate_fwd(u1, u3, cw, cb):
    L, F = u1.shape
    tc = _pick(F, (256, 128))

    def body(u1_ref, u3_ref, w_ref, b_ref, a_ref):
        u = u1_ref[...].astype(F32)
        rows = lax.broadcasted_iota(jnp.int32, u.shape, 0)
        cv = _shift_down(u, rows) * w_ref[0:1, :] + u * w_ref[1:2, :] + _shift_up(u, rows) * w_ref[2:3, :] + b_ref[...]
        a_ref[...] = (cv * _sigmoid(cv) * u3_ref[...].astype(F32)).astype(BF16)

    blk = pl.BlockSpec((L, tc), lambda j: (0, j))
    return _pcall(
        body, name="convgate_fwd", grid=(F // tc,),
        in_specs=[blk, blk, pl.BlockSpec((8, tc), lambda j: (0, j)), pl.BlockSpec((1, tc), lambda j: (0, j))],
        out_specs=blk,
        out_shape=jax.ShapeDtypeStruct((L, F), BF16),
        compiler_params=_params(("parallel",)),
    )(u1, u3, cw, cb)


def _convgate_bwd(u1, u3, da, cw, cb):
    L, F = u1.shape
    tc = _pick(F, (256, 128))

    def body(u1_ref, u3_ref, da_ref, w_ref, b_ref, du1_ref, du3_ref, s_ref):
        u = u1_ref[...].astype(F32)
        rows = lax.broadcasted_iota(jnp.int32, u.shape, 0)
        um, up = _shift_down(u, rows), _shift_up(u, rows)
        w0, w1, w2 = w_ref[0:1, :], w_ref[1:2, :], w_ref[2:3, :]
        cv = um * w0 + u * w1 + up * w2 + b_ref[...]
        s = _sigmoid(cv)
        dav = da_ref[...].astype(F32)
        du3_ref[...] = (dav * cv * s).astype(BF16)
        dcv = dav * u3_ref[...].astype(F32) * (s * (1.0 + cv * (1.0 - s)))
        du1_ref[...] = (_shift_up(dcv, rows) * w0 + dcv * w1 + _shift_down(dcv, rows) * w2).astype(BF16)
        r8 = lax.broadcasted_iota(jnp.int32, (8, tc), 0)
        s0 = jnp.sum(dcv * um, axis=0, keepdims=True)
        s1 = jnp.sum(dcv * u, axis=0, keepdims=True)
        s2 = jnp.sum(dcv * up, axis=0, keepdims=True)
        s3 = jnp.sum(dcv, axis=0, keepdims=True)
        s_ref[...] = jnp.where(r8 == 0, s0, jnp.where(r8 == 1, s1, jnp.where(r8 == 2, s2,
                     jnp.where(r8 == 3, s3, 0.0))))

    blk = pl.BlockSpec((L, tc), lambda j: (0, j))
    v8 = pl.BlockSpec((8, tc), lambda j: (0, j))
    return _pcall(
        body, name="convgate_bwd", grid=(F // tc,),
        in_specs=[blk, blk, blk, v8, pl.BlockSpec((1, tc), lambda j: (0, j))],
        out_specs=[blk, blk, v8],
        out_shape=[jax.ShapeDtypeStruct((L, F), BF16), jax.ShapeDtypeStruct((L, F), BF16),
                   jax.ShapeDtypeStruct((8, F), F32)],
        compiler_params=_params(("parallel",)),
    )(u1, u3, da, cw, cb)


def _lower_bound(lbl_ref, d):
    l0, l1 = lbl_ref[d, 0:1, :], lbl_ref[d, 1:2, :]
    m = jnp.maximum(l0, l1)
    e0, e1 = jnp.exp(l0 - m), jnp.exp(l1 - m)
    return e0 / (e0 + e1)


def _chunk_cumsum(x, rev):
    n = x.shape[0]
    r = lax.broadcasted_iota(jnp.int32, x.shape, 0) % CHUNK
    k = 1
    while k < CHUNK:
        if rev:
            x = x + jnp.where(r < CHUNK - k, pltpu.roll(x, n - k, 0), 0.0)
        else:
            x = x + jnp.where(r >= k, pltpu.roll(x, k, 0), 0.0)
        k *= 2
    return x


def _gate_terms(z, lb):
    sg = _sigmoid(z)
    f = lb + (1.0 - lb) * sg
    return sg, f


def _decay_terms(z, lb, rev):
    _, f = _gate_terms(z, lb)
    g = jnp.log(f)
    return 1.0 - f, _chunk_cumsum(g, rev), _chunk_cumsum(g, not rev) - g


def _chunk_total(c, rev):
    return c[0:1, :] if rev else c[CHUNK - 1:CHUNK, :]


def _pair_decay(c, s, rev):
    t = lax.broadcasted_iota(jnp.int32, (CHUNK, 1), 0)
    later = (t <= s) if rev else (t >= s)
    return jnp.where(later, jnp.exp(c - c[s:s + 1, :]), 0.0)


def _scan_chunk(i, n_ctx_chunks, n_chunks, rev):
    if not rev:
        return i
    return jnp.where(i < n_ctx_chunks, n_ctx_chunks - 1 - i, n_chunks + n_ctx_chunks - 1 - i)


def _rows(ci):
    return pl.ds(pl.multiple_of(ci * CHUNK, CHUNK), CHUNK)


def _hgrn_cols(HA):
    return HA // HEAD


def _hgrn_fwd(p, lbl, ng, n_ctx, HA):
    T = p.shape[0]
    L = T - n_ctx
    nh = _hgrn_cols(HA)
    nc, ncc = T // CHUNK, n_ctx // CHUNK

    def body(q_ref, zf_ref, zb_ref, v_ref, og_ref, lbl_ref, ng_ref, ya_ref, o_ref, st_ref,
             c_scr, k_scr, qe_scr, ke_scr, o_scr):
        dirs = ((0, False, zf_ref), (1, True, zb_ref))
        for d, rev, z_ref in dirs:
            k, c, rest = _decay_terms(z_ref[...], _lower_bound(lbl_ref, d), rev)
            c_scr[d] = c
            k_scr[d] = k
            qe_scr[d] = (q_ref[...] * jnp.exp(c)).astype(BF16)
            ke_scr[d] = (k * jnp.exp(rest)).astype(BF16)

        def step(i2, states):
            states = list(states)
            for u in range(HGRN_UNROLL):
                for d, rev, _ in dirs:
                    St = states[d]
                    ci = _scan_chunk(HGRN_UNROLL * i2 + u, ncc, nc, rev)
                    rows = _rows(ci)
                    q, v, c, k = q_ref[rows, :], v_ref[rows, :], c_scr[d, rows, :], k_scr[d, rows, :]
                    st_ref[0, d, ci] = St.astype(BF16)
                    o = jnp.zeros((CHUNK, HEAD), F32)
                    for s in range(CHUNK):
                        E = _pair_decay(c, s, rev)
                        a = jnp.sum(q * E * k[s:s + 1, :], axis=1, keepdims=True)
                        o = o + a * v[s:s + 1, :]
                    o_scr[d, rows, :] = o + _dot_nt(qe_scr[d, rows, :], St.astype(BF16))
                    states[d] = St * jnp.exp(_chunk_total(c, rev)) + _dot_tn(v.astype(BF16), ke_scr[d, rows, :])
            return tuple(states)

        if nc % HGRN_UNROLL:
            raise ValueError("the number of chunks must be a multiple of HGRN_UNROLL")
        zero = jnp.zeros((HEAD, HEAD), F32)
        lax.fori_loop(0, nc // HGRN_UNROLL, step, (zero, zero))

        o = o_scr[0, pl.ds(n_ctx, L), :] + o_scr[1, pl.ds(n_ctx, L), :]
        o_ref[...] = o
        r = lax.rsqrt(jnp.mean(o * o, axis=-1, keepdims=True) + EPS)
        og = og_ref[pl.ds(n_ctx, L), :]
        ya_ref[...] =(o * r * ng_ref[...] * (og * _sigmoid(og))).astype(BF16)

    cb = HA // HEAD
    col = lambda kk: pl.BlockSpec((T, HEAD), lambda h: (0, kk * cb + h))
    return _pcall(
        body, name="hgrn_fwd", grid=(nh,),
        in_specs=[col(0), col(1), col(2), col(3), col(4),
                  pl.BlockSpec((2, 2, HEAD), lambda h: (0, 0, h)), pl.BlockSpec((1, HEAD), lambda h: (0, 0))],
        out_specs=[pl.BlockSpec((L, HEAD), lambda h: (0, h)), pl.BlockSpec((L, HEAD), lambda h: (0, h)),
                   pl.BlockSpec((1, 2, nc, HEAD, HEAD), lambda h: (h, 0, 0, 0, 0))],
        out_shape=[jax.ShapeDtypeStruct((L, HA), BF16), jax.ShapeDtypeStruct((L, HA), F32),
                   jax.ShapeDtypeStruct((nh, 2, nc, HEAD, HEAD), BF16)],
        scratch_shapes=[pltpu.VMEM((2, T, HEAD), F32), pltpu.VMEM((2, T, HEAD), F32),
                        pltpu.VMEM((2, T, HEAD), BF16), pltpu.VMEM((2, T, HEAD), BF16),
                        pltpu.VMEM((2, T, HEAD), F32)],
        compiler_params=_params(("parallel",)),
    )(p, p, p, p, p, lbl, ng)


def _hgrn_bwd(p, lbl, ng, o, dya, st, n_ctx, HA):
    T = p.shape[0]
    L = T - n_ctx
    nh = _hgrn_cols(HA)
    nc, ncc = T // CHUNK, n_ctx // CHUNK

    def body(q_ref, zf_ref, zb_ref, v_ref, og_ref, lbl_ref, ng_ref, o_ref, dya_ref, st_ref,
             dq_ref, dzf_ref, dzb_ref, dv_ref, dog_ref, dlbl_ref, dng_ref,
             do_scr, c_scr, k_scr, qe_scr, ke_scr, dg_scr, dk_scr, dq_scr, dv_scr, row_scr):
        h = pl.program_id(0)
        ov = o_ref[...]
        r = lax.rsqrt(jnp.mean(ov * ov, axis=-1, keepdims=True) + EPS)
        oh = ov * r
        ogv = og_ref[pl.ds(n_ctx, L), :]
        sg_o = _sigmoid(ogv)
        dyv = dya_ref[...]
        ngv = ng_ref[...]
        dog_ref[pl.ds(0, n_ctx), :] = jnp.zeros((n_ctx, HEAD), BF16)
        dog_ref[pl.ds(n_ctx, L), :] = (dyv * oh * ngv * (sg_o * (1.0 + ogv * (1.0 - sg_o)))).astype(BF16)
        don = dyv * (ogv * sg_o)
        dng = jnp.sum(don * oh, axis=0, keepdims=True)
        doh = don * ngv
        do_scr[pl.ds(0, n_ctx), :] = jnp.zeros((n_ctx, HEAD), F32)
        do_scr[pl.ds(n_ctx, L), :] = r * (doh - oh * jnp.mean(doh * oh, axis=-1, keepdims=True))

        @pl.when(h == 0)
        def _():
            dng_ref[...] = jnp.zeros_like(dng_ref)

        dng_ref[0:1, :] += dng

        t16 = lax.broadcasted_iota(jnp.int32, (CHUNK, HEAD), 0)
        dirs = ((0, False, zf_ref, dzf_ref), (1, True, zb_ref, dzb_ref))
        for d, rev, z_ref, _ in dirs:
            k, c, rest = _decay_terms(z_ref[...], _lower_bound(lbl_ref, d), rev)
            c_scr[d] = c
            k_scr[d] = k
            qe_scr[d] = (q_ref[...] * jnp.exp(c)).astype(BF16)
            ke_scr[d] = (k * jnp.exp(rest)).astype(BF16)
        dq_scr[...] = jnp.zeros_like(dq_scr)
        dv_scr[...] = jnp.zeros_like(dv_scr)

        zero = jnp.zeros((HEAD, HEAD), F32)

        def bwd_chunk(i, carry, u):
            new = []
            for (d, rev, _, _), dSt in zip(dirs, carry):
                ci = _scan_chunk(i, ncc, nc, rev)
                rows = _rows(ci)
                q, v, do = q_ref[rows, :], v_ref[rows, :], do_scr[rows, :]
                c, k = c_scr[d, rows, :], k_scr[d, rows, :]
                tot = _chunk_total(c, rev)
                etot = jnp.exp(tot)
                St = st_ref[0, d, ci]
                dSb = dSt.astype(BF16)
                do_b = do.astype(BF16)
                dq_x = _dot(do_b, St) * jnp.exp(c)
                dk_x = _dot(v.astype(BF16), dSb) * jnp.exp(tot - c)
                dv_x = _dot_nt(ke_scr[d, rows, :], dSb)
                dtot = (jnp.sum(St.astype(F32) * dSt, axis=0, keepdims=True) * etot
                        + jnp.sum(k * dk_x, axis=0, keepdims=True))
                dq = jnp.zeros((CHUNK, HEAD), F32)
                for s in range(CHUNK):
                    E = _pair_decay(c, s, rev)
                    XE = E * k[s:s + 1, :]
                    a = jnp.sum(q * XE, axis=1, keepdims=True)
                    da = jnp.sum(do * v[s:s + 1, :], axis=1, keepdims=True)
                    dq = dq + da * XE
                    row_scr[u, d, 0, s:s + 1, :] = jnp.sum(da * q * E, axis=0, keepdims=True)
                    row_scr[u, d, 1, s:s + 1, :] = jnp.sum(a * do, axis=0, keepdims=True)
                dq, dk, dv = dq + dq_x, row_scr[u, d, 0] + dk_x, row_scr[u, d, 1] + dv_x
                dg_scr[d, rows, :] = _chunk_cumsum(q * dq - k * dk, not rev) + dtot
                dk_scr[d, rows, :] = dk
                dq_scr[rows, :] += dq
                dv_scr[rows, :] += dv
                new.append(dSt * etot + _dot_tn(do_b, qe_scr[d, rows, :]))
            return tuple(new)

        def bwd_step(i2, carry):
            for u in range(2):
                carry = bwd_chunk(nc - 1 - (2 * i2 + u), carry, u)
            return carry

        lax.fori_loop(0, nc // 2, bwd_step, (zero, zero))

        for d, _, z_ref, dz_ref in dirs:
            lb = _lower_bound(lbl_ref, d)
            sg, f = _gate_terms(z_ref[...], lb)
            df = dg_scr[d] / f - dk_scr[d]
            dz_ref[...] = (df * (1.0 - lb) * sg * (1.0 - sg)).astype(BF16)
            dl0 = jnp.sum(df * (1.0 - sg), axis=0, keepdims=True) * lb * (1.0 - lb)
            dlbl_ref[d, 0:1, :] = dl0
            dlbl_ref[d, 1:2, :] = -dl0
        dq_ref[...] = dq_scr[...].astype(BF16)
        dv_ref[...] = dv_scr[...].astype(BF16)

    cb = HA // HEAD
    col = lambda kk: pl.BlockSpec((T, HEAD), lambda h: (0, kk * cb + h))
    tcol = pl.BlockSpec((T, HEAD), lambda h: (0, h))
    lcol = pl.BlockSpec((L, HEAD), lambda h: (0, h))
    outs = _pcall(
        body, name="hgrn_bwd", grid=(nh,),
        in_specs=[col(0), col(1), col(2), col(3), col(4),
                  pl.BlockSpec((2, 2, HEAD), lambda h: (0, 0, h)), pl.BlockSpec((1, HEAD), lambda h: (0, 0)),
                  lcol, lcol,
                  pl.BlockSpec((1, 2, nc, HEAD, HEAD), lambda h: (h, 0, 0, 0, 0), pipeline_mode=pl.Buffered(1))],
        out_specs=[tcol, tcol, tcol, tcol, tcol, pl.BlockSpec((2, 2, HEAD), lambda h: (0, 0, h)),
                   pl.BlockSpec((8, HEAD), lambda h: (0, 0))],
        out_shape=[jax.ShapeDtypeStruct((T, HA), BF16)] * 5 + [jax.ShapeDtypeStruct((2, 2, HA), F32),
                                                               jax.ShapeDtypeStruct((8, HEAD), F32)],
        scratch_shapes=[pltpu.VMEM((T, HEAD), F32),
                        pltpu.VMEM((2, T, HEAD), F32), pltpu.VMEM((2, T, HEAD), F32),
                        pltpu.VMEM((2, T, HEAD), BF16), pltpu.VMEM((2, T, HEAD), BF16),
                        pltpu.VMEM((2, T, HEAD), F32), pltpu.VMEM((2, T, HEAD), F32),
                        pltpu.VMEM((T, HEAD), F32), pltpu.VMEM((T, HEAD), F32),
                        pltpu.VMEM((2, 2, 2, CHUNK, HEAD), F32)],
        compiler_params=_params(("arbitrary",)),
    )(p, p, p, p, p, lbl, ng, o, dya, st)
    return outs


def _swap_halves(t, lane):
    q = HEAD // 4
    return jnp.where((lane % (2 * q)) < q, pltpu.roll(t, HEAD - q, 1), pltpu.roll(t, q, 1))


def _qk_norm(t, g):
    r = lax.rsqrt(jnp.mean(t * t, axis=-1, keepdims=True) + EPS)
    return t * r, r


def _rope(t, cos, sin, lane):
    return t * cos + _swap_halves(t, lane) * sin


def _qk_norm_bwd(dy, th, r, g):
    dth = dy * g
    return r * (dth - th * jnp.mean(dth * th, axis=-1, keepdims=True)), jnp.sum(dy * th, axis=0, keepdims=True)


def _rope_bwd(dy, cos, sin, lane):
    return dy * cos + _swap_halves(dy * sin, lane)


def _na_geometry(L):
    n_rows = L // GRID_W
    kr = min(WIN_R, n_rows)
    return n_rows, kr


def _na_prep(q_ref, k_ref, v_ref, gq_ref, gk_ref, cos_ref, sin_ref, qs, ks, vs, n_ctx, L):
    lane = lax.broadcasted_iota(jnp.int32, (L, HEAD), 1)
    cos, sin = cos_ref[...], sin_ref[...]
    qh, _ = _qk_norm(q_ref[pl.ds(n_ctx, L), :], None)
    qs[...] = _rope(qh * gq_ref[...], cos, sin, lane).astype(BF16)
    kh, _ = _qk_norm(k_ref[pl.ds(n_ctx, L), :], None)
    ks[pl.ds(n_ctx, L), :] = _rope(kh * gk_ref[...], cos, sin, lane).astype(BF16)
    kc, _ = _qk_norm(k_ref[pl.ds(0, n_ctx), :], None)
    ks[pl.ds(0, n_ctx), :] = (kc * gk_ref[...]).astype(BF16)
    vs[...] = v_ref[...].astype(BF16)


NA_RB = 4


def _na_band_rows(kr):
    return kr + NA_RB


def _na_scores(i, qs, ks, bias_ref, n_ctx, n_rows, kr):
    scale = HEAD ** -0.5
    kb = _na_band_rows(kr)
    rq = NA_RB * i
    r0 = jnp.clip(rq - WIN_R // 2, 0, n_rows - kb)
    qrows = pl.ds(pl.multiple_of(rq * GRID_W, NA_RB * GRID_W), NA_RB * GRID_W)
    krows = pl.ds(pl.multiple_of(n_ctx + r0 * GRID_W, GRID_W), kb * GRID_W)
    qv = qs[qrows, :]
    sb = _dot_nt(qv, ks[krows, :]) * scale
    band_row = lax.broadcasted_iota(jnp.int32, (GRID_W, kb * GRID_W), 1) // GRID_W
    parts, tiles = [], []
    for u in range(NA_RB):
        r_u = rq + u
        first = jnp.clip(r_u - WIN_R // 2, 0, n_rows - kr) - r0
        idx = [jnp.clip(r0 - r_u + (WIN_R - 1) + 2 * jj, 0, 2 * WIN_R - 1) for jj in range(kb // 2)]
        bias_u = jnp.concatenate([bias_ref[0, t] for t in idx], axis=1)
        inside = (band_row >= first) & (band_row < first + kr)
        parts.append(jnp.where(inside, sb[u * GRID_W:(u + 1) * GRID_W, :] + bias_u, NEG))
        tiles.append(idx)
    sb = jnp.concatenate(parts, axis=0)
    sc = _dot_nt(qv, ks[pl.ds(0, n_ctx), :]) * scale
    m = jnp.maximum(jnp.max(sb, axis=1, keepdims=True), jnp.max(sc, axis=1, keepdims=True))
    eb, ec = jnp.exp(sb - m), jnp.exp(sc - m)
    inv = 1.0 / (jnp.sum(eb, axis=1, keepdims=True) + jnp.sum(ec, axis=1, keepdims=True))
    return eb * inv, ec * inv, qrows, krows, tiles


def _na_fwd(p, bias, gq, gk, cos, sin, n_ctx, off, HB):
    T = p.shape[0]
    L = T - n_ctx
    nh = HB // HEAD
    n_rows, kr = _na_geometry(L)
    ob = off // HEAD

    def body(q_ref, k_ref, v_ref, bias_ref, gq_ref, gk_ref, cos_ref, sin_ref, y_ref, qs, ks, vs):
        _na_prep(q_ref, k_ref, v_ref, gq_ref, gk_ref, cos_ref, sin_ref, qs, ks, vs, n_ctx, L)

        def step(i, carry):
            pb, pc, qrows, krows, _ = _na_scores(i, qs, ks, bias_ref, n_ctx, n_rows, kr)
            y = _dot(pb.astype(BF16), vs[krows, :]) + _dot(pc.astype(BF16), vs[pl.ds(0, n_ctx), :])
            y_ref[qrows, :] = y.astype(BF16)
            return carry

        lax.fori_loop(0, n_rows // NA_RB, step, 0)

    col = lambda kk: pl.BlockSpec((T, HEAD), lambda h: (0, ob + kk * nh + h))
    vec = pl.BlockSpec((1, HEAD), lambda h: (0, 0))
    tab = pl.BlockSpec((L, HEAD), lambda h: (0, 0))
    return _pcall(
        body, name="na_fwd", grid=(nh,),
        in_specs=[col(0), col(1), col(2), pl.BlockSpec((1,) + bias.shape[1:], lambda h: (h, 0, 0, 0)),
                  vec, vec, tab, tab],
        out_specs=pl.BlockSpec((L, HEAD), lambda h: (0, h)),
        out_shape=jax.ShapeDtypeStruct((L, HB), BF16),
        scratch_shapes=[pltpu.VMEM((L, HEAD), BF16), pltpu.VMEM((T, HEAD), BF16), pltpu.VMEM((T, HEAD), BF16)],
        compiler_params=_params(("parallel",)),
    )(p, p, p, bias, gq, gk, cos, sin)


def _na_bwd(p, bias, gq, gk, cos, sin, dyb, n_ctx, off, HB):
    T = p.shape[0]
    L = T - n_ctx
    nh = HB // HEAD
    n_rows, kr = _na_geometry(L)
    ob = off // HEAD
    scale = HEAD ** -0.5

    def body(q_ref, k_ref, v_ref, bias_ref, gq_ref, gk_ref, cos_ref, sin_ref, dy_ref,
             dq_ref, dk_ref, dv_ref, dbias_ref, dg_ref, qs, ks, vs, dqa, dka, dva):
        h = pl.program_id(0)
        _na_prep(q_ref, k_ref, v_ref, gq_ref, gk_ref, cos_ref, sin_ref, qs, ks, vs, n_ctx, L)
        dka[...] = jnp.zeros_like(dka)
        dva[...] = jnp.zeros_like(dva)
        dbias_ref[...] = jnp.zeros_like(dbias_ref)

        crows = pl.ds(0, n_ctx)

        def step(i, carry):
            pb, pc, qrows, krows, tiles = _na_scores(i, qs, ks, bias_ref, n_ctx, n_rows, kr)
            do = dy_ref[qrows, :]
            qv = qs[qrows, :]
            dpb = _dot_nt(do, vs[krows, :])
            dpc = _dot_nt(do, vs[crows, :])
            delta = jnp.sum(pb * dpb, axis=1, keepdims=True) + jnp.sum(pc * dpc, axis=1, keepdims=True)
            dsb = pb * (dpb - delta)
            dsc = pc * (dpc - delta)
            dsb_b, dsc_b = dsb.astype(BF16), dsc.astype(BF16)
            dqa[qrows, :] = (_dot(dsb_b, ks[krows, :]) + _dot(dsc_b, ks[crows, :])) * scale
            dka[krows, :] += _dot_tn(dsb_b, qv) * scale
            dka[crows, :] += _dot_tn(dsc_b, qv) * scale
            dva[krows, :] += _dot_tn(pb.astype(BF16), do)
            dva[crows, :] += _dot_tn(pc.astype(BF16), do)
            for u, idx in enumerate(tiles):
                for jj, t in enumerate(idx):
                    dbias_ref[0, t] += dsb[u * GRID_W:(u + 1) * GRID_W, jj * 2 * GRID_W:(jj + 1) * 2 * GRID_W]
            return carry

        lax.fori_loop(0, n_rows // NA_RB, step, 0)

        lane = lax.broadcasted_iota(jnp.int32, (L, HEAD), 1)
        cos, sin = cos_ref[...], sin_ref[...]
        lat, ctx = pl.ds(n_ctx, L), pl.ds(0, n_ctx)
        gqv, gkv = gq_ref[...], gk_ref[...]
        qh, rq = _qk_norm(q_ref[lat, :], None)
        dq, dgq = _qk_norm_bwd(_rope_bwd(dqa[...], cos, sin, lane), qh, rq, gqv)
        dq_ref[ctx, :] = jnp.zeros((n_ctx, HEAD), BF16)
        dq_ref[lat, :] = dq.astype(BF16)
        kh, rk = _qk_norm(k_ref[lat, :], None)
        dk, dgk = _qk_norm_bwd(_rope_bwd(dka[lat, :], cos, sin, lane), kh, rk, gkv)
        dk_ref[lat, :] = dk.astype(BF16)
        kch, rkc = _qk_norm(k_ref[ctx, :], None)
        dkc, dgkc = _qk_norm_bwd(dka[ctx, :], kch, rkc, gkv)
        dk_ref[ctx, :] = dkc.astype(BF16)
        dv_ref[...] = dva[...].astype(BF16)

        @pl.when(h == 0)
        def _():
            dg_ref[...] = jnp.zeros_like(dg_ref)

        dg_ref[0:1, :] += dgq
        dg_ref[1:2, :] += dgk + dgkc

    col = lambda kk: pl.BlockSpec((T, HEAD), lambda h: (0, ob + kk * nh + h))
    vec = pl.BlockSpec((1, HEAD), lambda h: (0, 0))
    tab = pl.BlockSpec((L, HEAD), lambda h: (0, 0))
    tcol = pl.BlockSpec((T, HEAD), lambda h: (0, h))
    bspec = pl.BlockSpec((1,) + bias.shape[1:], lambda h: (h, 0, 0, 0))
    return _pcall(
        body, name="na_bwd", grid=(nh,),
        in_specs=[col(0), col(1), col(2), bspec, vec, vec, tab, tab, pl.BlockSpec((L, HEAD), lambda h: (0, h))],
        out_specs=[tcol, tcol, tcol, bspec, pl.BlockSpec((8, HEAD), lambda h: (0, 0))],
        out_shape=[jax.ShapeDtypeStruct((T, HB), BF16)] * 3 + [jax.ShapeDtypeStruct(bias.shape, F32),
                                                               jax.ShapeDtypeStruct((8, HEAD), F32)],
        scratch_shapes=[pltpu.VMEM((L, HEAD), BF16), pltpu.VMEM((T, HEAD), BF16), pltpu.VMEM((T, HEAD), BF16),
                        pltpu.VMEM((L, HEAD), F32), pltpu.VMEM((T, HEAD), F32), pltpu.VMEM((T, HEAD), F32)],
        compiler_params=_params(("arbitrary",)),
    )(p, p, p, bias, gq, gk, cos, sin, dyb)


def _bias_tables():
    w = np.arange(GRID_W)
    col_start = np.clip(w - WIN_C // 2, 0, GRID_W - WIN_C)
    col_in = (w[None, :] >= col_start[:, None]) & (w[None, :] < col_start[:, None] + WIN_C)
    dc = np.clip(w[None, :] - w[:, None], -(WIN_C - 1), WIN_C - 1) + WIN_C - 1
    n_pair = 2 * WIN_R
    ridx = np.zeros((n_pair, GRID_W, 2 * GRID_W), np.int32)
    cidx = np.zeros((n_pair, GRID_W, 2 * GRID_W), np.int32)
    valid = np.zeros((n_pair, GRID_W, 2 * GRID_W), bool)
    for i in range(n_pair):
        for half in range(2):
            row = i + half
            sl = slice(half * GRID_W, (half + 1) * GRID_W)
            ridx[i, :, sl] = min(row, 2 * WIN_R - 2)
            cidx[i, :, sl] = dc
            valid[i, :, sl] = col_in & (row <= 2 * WIN_R - 2)
    return ridx, cidx, valid


def _bias_onehot():
    _, cidx, valid = _bias_tables()
    K = GRID_W * 2 * GRID_W
    oh = np.zeros((K, 128), np.float32)
    neg = np.full((1, K), NEG, np.float32)
    for cq in range(GRID_W):
        for ll in range(2 * GRID_W):
            if valid[0, cq, ll]:
                oh[cq * 2 * GRID_W + ll, (ll // GRID_W) * 64 + cidx[0, cq, ll]] = 1.0
                neg[0, cq * 2 * GRID_W + ll] = 0.0
    return oh, neg


def _expand_bias(table):
    H = table.shape[0]
    n_pair, n_dc = 2 * WIN_R, 2 * WIN_C - 1
    tp = jnp.pad(table, ((0, 0), (0, n_pair + 1 - table.shape[1]), (0, 64 - n_dc)))
    t2 = jnp.concatenate([tp[:, :n_pair], tp[:, 1:n_pair + 1]], axis=-1).reshape(H * n_pair, 128)
    oh, neg = _bias_onehot()

    def body(t_ref, oh_ref, neg_ref, o_ref):
        o_ref[...] = lax.dot_general(t_ref[...], oh_ref[...], (((1,), (1,)), ((), ())), precision=HI,
                                     preferred_element_type=F32) + neg_ref[...]

    out = _pcall(body, name="bias_expand", out_shape=jax.ShapeDtypeStruct((H * n_pair, oh.shape[0]), F32),
                         compiler_params=_params())(t2, jnp.asarray(oh), jnp.asarray(neg))
    return out.reshape(H, n_pair, GRID_W, 2 * GRID_W)


def _bias_grad(dbias):
    H = dbias.shape[0]
    n_pair, n_dc = 2 * WIN_R, 2 * WIN_C - 1
    K = GRID_W * 2 * GRID_W
    oh, _ = _bias_onehot()
    flat = dbias.reshape(H * n_pair, K)

    def body(d_ref, oh_ref, o_ref):
        o_ref[...] = jnp.dot(d_ref[...], oh_ref[...], precision=HI, preferred_element_type=F32)

    g = _pcall(body, name="bias_grad", out_shape=jax.ShapeDtypeStruct((H * n_pair, 128), F32),
                       compiler_params=_params())(flat, jnp.asarray(oh))
    g = g.reshape(H, n_pair, 128)
    left, right = g[:, :, :n_dc], g[:, :, 64:64 + n_dc]
    out = left[:, :n_pair - 1]
    return out.at[:, 1:].add(right[:, :n_pair - 2])


def _rope_tables(L):
    pos = np.arange(L)
    row = (pos // GRID_W).astype(np.float32)
    colp = (pos % GRID_W).astype(np.float32)
    half = HEAD // 2
    nf = half // 2
    inv = (ROPE_THETA ** (-np.arange(nf, dtype=np.float32) / nf)).astype(np.float32)

    def tabs(pv):
        ang = pv[:, None] * inv[None, :]
        c, s = np.cos(ang), np.sin(ang)
        return np.concatenate([c, c], axis=1), np.concatenate([-s, s], axis=1)

    cr, sr = tabs(row)
    cc, sc = tabs(colp)
    return (jnp.asarray(np.concatenate([cr, cc], axis=1), F32), jnp.asarray(np.concatenate([sr, sc], axis=1), F32))


def _adamw(w, g, m, v, name, after=None, copy_g=False):
    R, C = w.shape
    tr = _row_tile(R, C)
    c1 = 1.0 - ADAM_B1 ** ADAM_STEP
    c2 = 1.0 - ADAM_B2 ** ADAM_STEP
    deps = [] if after is None else [after]
    n_out = 4 if copy_g else 3

    def body(w_ref, g_ref, m_ref, v_ref, *rest):
        d_ref, mo_ref, vo_ref = rest[len(deps):len(deps) + 3]
        gv = g_ref[...]
        mn = ADAM_B1 * m_ref[...] + (1.0 - ADAM_B1) * gv
        vn = ADAM_B2 * v_ref[...] + (1.0 - ADAM_B2) * (gv * gv)
        mo_ref[...] = mn
        vo_ref[...] = vn
        d_ref[...] = -ADAM_LR * ((mn / c1) / (jnp.sqrt(vn / c2) + ADAM_EPS) + ADAM_WD * w_ref[...])
        if copy_g:
            rest[-1][...] = gv

    blk = pl.BlockSpec((tr, C), lambda i: (i, 0))
    return _pcall(
        body, name=name, grid=(R // tr,),
        in_specs=[blk] * 4 + [_ANY] * len(deps), out_specs=[blk] * n_out,
        out_shape=[jax.ShapeDtypeStruct((R, C), F32)] * n_out,
        compiler_params=_params(("parallel",)),
    )(w, g, m, v, *deps)


PACK_W = 1024


def _pack(parts):
    flat, offs, pos = [], [], 0
    for a in parts:
        n = a.size
        padn = -n % PACK_W
        flat.append(jnp.pad(a.reshape(-1).astype(F32), (0, padn)))
        offs.append((pos, n, a.shape))
        pos += n + padn
    tail = -pos % (8 * PACK_W)
    if tail:
        flat.append(jnp.zeros((tail,), F32))
    return jnp.concatenate(flat).reshape(-1, PACK_W), offs


def _unpack(buf, offs, i):
    pos, n, shape = offs[i]
    return buf.reshape(buf.shape[:-2] + (-1,))[..., pos:pos + n].reshape(buf.shape[:-2] + shape)


def kernel(x, c, ctx, c_ctx, ada_w, ada_b, norm1_g, norm2_g, w_in, hgrn_lb_logits, hgrn_norm_g, na_q_norm_g, na_k_norm_g, na_rel_bias, w_branch_a, w_branch_b, w_out, ffn_w1, ffn_w3, ffn_conv_w, ffn_conv_b, ffn_w2, loss_target, m_c_ctx, m_ada_w, m_ada_b, m_norm1_g, m_norm2_g, m_w_in, m_hgrn_lb_logits, m_hgrn_norm_g, m_na_q_norm_g, m_na_k_norm_g, m_na_rel_bias, m_w_branch_a, m_w_branch_b, m_w_out, m_ffn_w1, m_ffn_w3, m_ffn_conv_w, m_ffn_conv_b, m_ffn_w2, v_c_ctx, v_ada_w, v_ada_b, v_norm1_g, v_norm2_g, v_w_in, v_hgrn_lb_logits, v_hgrn_norm_g, v_na_q_norm_g, v_na_k_norm_g, v_na_rel_bias, v_w_branch_a, v_w_branch_b, v_w_out, v_ffn_w1, v_ffn_w3, v_ffn_conv_w, v_ffn_conv_b, v_ffn_w2):
    weights = dict(c_ctx=c_ctx, ada_w=ada_w, ada_b=ada_b, norm1_g=norm1_g, norm2_g=norm2_g, w_in=w_in,
                   hgrn_lb_logits=hgrn_lb_logits, hgrn_norm_g=hgrn_norm_g, na_q_norm_g=na_q_norm_g,
                   na_k_norm_g=na_k_norm_g, na_rel_bias=na_rel_bias, w_branch_a=w_branch_a, w_branch_b=w_branch_b,
                   w_out=w_out, ffn_w1=ffn_w1, ffn_w3=ffn_w3, ffn_conv_w=ffn_conv_w, ffn_conv_b=ffn_conv_b,
                   ffn_w2=ffn_w2)
    moms = dict(c_ctx=(m_c_ctx, v_c_ctx), ada_w=(m_ada_w, v_ada_w), ada_b=(m_ada_b, v_ada_b),
                norm1_g=(m_norm1_g, v_norm1_g), norm2_g=(m_norm2_g, v_norm2_g), w_in=(m_w_in, v_w_in),
                hgrn_lb_logits=(m_hgrn_lb_logits, v_hgrn_lb_logits), hgrn_norm_g=(m_hgrn_norm_g, v_hgrn_norm_g),
                na_q_norm_g=(m_na_q_norm_g, v_na_q_norm_g), na_k_norm_g=(m_na_k_norm_g, v_na_k_norm_g),
                na_rel_bias=(m_na_rel_bias, v_na_rel_bias), w_branch_a=(m_w_branch_a, v_w_branch_a),
                w_branch_b=(m_w_branch_b, v_w_branch_b), w_out=(m_w_out, v_w_out), ffn_w1=(m_ffn_w1, v_ffn_w1),
                ffn_w3=(m_ffn_w3, v_ffn_w3), ffn_conv_w=(m_ffn_conv_w, v_ffn_conv_w),
                ffn_conv_b=(m_ffn_conv_b, v_ffn_conv_b), ffn_w2=(m_ffn_w2, v_ffn_w2))
    order = list(weights)

    L, D = x.shape[1], x.shape[2]
    N = ctx.shape[1]
    T = N + L
    HA = w_branch_a.shape[1]
    HB = w_branch_b.shape[1]
    F = ffn_conv_b.shape[1]
    IN = 5 * HA + 3 * HB + 2 * D
    n_ada = ada_w.shape[2]
    ix, iy, ic = _pos()
    chip = 2 * ix + iy
    dev = 2 * chip + ic

    _PENDING.clear()
    pk0, offs0 = _pack([c[0], hgrn_lb_logits, ffn_conv_w[0]])
    g0 = _allgather8(pk0, "gather_small0")
    c_all = _unpack(g0, offs0, 0)
    lbl_parts = _unpack(g0, offs0, 1)
    lbl = jnp.concatenate([lbl_parts[2 * j] for j in range(N_CHIP)], axis=-1)
    cw_parts = _unpack(g0, offs0, 2)
    cw = jnp.concatenate([cw_parts[2 * j] for j in range(N_CHIP)], axis=-1)
    cw8 = jnp.pad(cw, ((0, 5), (0, 0)))

    cs = jnp.concatenate([c_all, c_ctx[None, :], jnp.zeros((7, D), F32)], axis=0)
    ada_b_mine = lax.dynamic_slice(ada_b, (0, chip * n_ada), (1, n_ada))
    mod_mine = _ada_fwd(cs, ada_w[0], ada_b_mine)
    gm = _allgather8(mod_mine, "gather_mod")
    mod = jnp.concatenate([gm[2 * j] for j in range(N_CHIP)], axis=-1)
    mod_l = lax.dynamic_slice(mod, (dev, 0), (1, N_MOD * D)).reshape(N_MOD, D)
    mod_c = mod[8].reshape(N_MOD, D)
    sh1, sc1, g1, sh2, sc2, g2 = [mod_l[i:i + 1] for i in range(N_MOD)]
    shift1 = jnp.concatenate([mod_c[0:1], sh1], axis=0)
    scale1 = jnp.concatenate([mod_c[1:2], sc1], axis=0)

    shards = [w_in[0], w_branch_a[0], w_branch_b[0], w_out[0], ffn_w1[0], ffn_w3[0], ffn_w2[0]]
    names = ["w_in", "w_a", "w_b", "w_out", "w1", "w3", "w2"]
    slots = [_cast_bf16_slot(s, "cast_" + nm) for s, nm in zip(shards, names)]
    sem_nb, win_buf = _xfer_start("gather_ici_start_in_nbr", slots[0:1], _plan_gather_ici(NEIGHBOURS), 2, gm)
    sem_dg, win_buf = _xfer_start("gather_ici_start_in_diag", win_buf, _plan_gather_ici(DIAGONAL), 1)
    gat_mix = _gather_start("mix", slots[1:4])
    gat_ffn = _gather_start("ffn", slots[4:7])

    xall = jnp.concatenate([ctx[0], x[0]], axis=0)
    h_all = _rms1_fwd(xall, norm1_g, shift1, scale1, N)
    chip_i = chip.astype(jnp.int32)
    p = _mm_nn_sel(h_all, win_buf[0], chip_i.reshape(1), F32, "mm_p_own")
    win_buf = _d2d_hand_over("in_nbr", sem_nb, win_buf, NEIGHBOURS, p)
    p = _mm_nn_sel(h_all, win_buf[0], jnp.stack([chip_i ^ 1, chip_i ^ 2]), F32, "mm_p_nbr", p)
    win_buf = _d2d_hand_over("in_diag", sem_dg, win_buf, DIAGONAL, p)
    p = _mm_nn_sel(h_all, win_buf[0], (chip_i ^ 3).reshape(1), F32, "mm_p_diag", p)
    Win = win_buf[0]
    gat_mix = _gather_mid(gat_mix, p)
    y_a, o_a, st_a = _hgrn_fwd(p, lbl, hgrn_norm_g, N, HA)
    Wa, Wb, Wo = _gather_finish(gat_mix, y_a)
    Wo = Wo.reshape(1, D, D)
    bias = _expand_bias(na_rel_bias[0])
    cos, sin = _rope_tables(L)
    off_na = 5 * HA
    y_b = _na_fwd(p, bias, na_q_norm_g, na_k_norm_g, cos, sin, N, off_na, HB)
    gat_ffn = _gather_mid(gat_ffn, (y_a, y_b))
    za = _mm_nn(y_a, Wa, BF16, "mm_za")
    zb = _mm_nn(y_b, Wb, BF16, "mm_zb")
    off_ga, off_gb = 5 * HA + 3 * HB, 5 * HA + 3 * HB + D
    z = _merge_fwd(za, zb, p, N, off_ga, off_gb)
    mo = _mm_nn(z, Wo, F32, "mm_mo")
    vec2 = jnp.concatenate([g1, norm2_g, sh2, sc2, jnp.zeros((4, D), F32)], axis=0)
    x_mid, h2 = _resid_rms2_fwd(x[0], mo, vec2)
    W1, W3, W2 = _gather_finish(gat_ffn, h2)
    W2 = W2.reshape(1, F, D)
    u1 = _mm_nn(h2, W1, BF16, "mm_u1")
    u3 = _mm_nn(h2, W3, BF16, "mm_u3")
    a = _convgate_fwd(u1, u3, cw8, ffn_conv_b)
    f = _mm_nn(a, W2, F32, "mm_f")
    dy, df, s_loss = _loss_head(x_mid, f, g2, loss_target[0])
    loss = lax.psum(s_loss[1, 0], ("x", "y", "c"))
    d_g2 = s_loss[0:1]

    gW2 = _mm_tn(a, df, 1, "mm_gw2").reshape(N_CHIP, F // N_CHIP, D)
    da = _mm_nt(df, W2, BF16, "mm_da")
    du1, du3, s_conv = _convgate_bwd(u1, u3, da, cw8, ffn_conv_b)
    gW1 = _mm_tn(h2, du1, N_CHIP, "mm_gw1")
    gW3 = _mm_tn(h2, du3, N_CHIP, "mm_gw3")
    rs_ffn = _rs_start("ffn", [gW2, gW1, gW3])
    dh2a = _mm_nt(du1, W1, F32, "mm_dh2a")
    dh2b = _mm_nt(du3, W3, F32, "mm_dh2b")
    rs_ffn = _rs_scatter(rs_ffn, dh2b)
    dxm, dmo, s_rms2 = _resid_rms2_bwd(x_mid, dh2a, dh2b, dy, mo, vec2)
    gWo = _mm_tn(z, dmo, 1, "mm_gwo").reshape(N_CHIP, D // N_CHIP, D)
    dz = _mm_nt(dmo, Wo, BF16, "mm_dz")
    dza, dzb, dga, dgb = _merge_bwd(dz, za, zb, p, N, off_ga, off_gb)
    gWa = _mm_tn(y_a, dza, N_CHIP, "mm_gwa")
    gWb = _mm_tn(y_b, dzb, N_CHIP, "mm_gwb")
    rs_mix = _rs_start("mix", [gWo, gWa, gWb])
    dya = _mm_nt(dza, Wa, F32, "mm_dya")
    dyb = _mm_nt(dzb, Wb, BF16, "mm_dyb")
    rs_mix = _rs_scatter(rs_mix, dyb)
    dq_a, dzf, dzbk, di_a, dog, dlbl, s_ng = _hgrn_bwd(p, lbl, hgrn_norm_g, o_a, dya, st_a, N, HA)
    rs_ffn = _rs_join(rs_ffn, dq_a)
    dq_n, dk_n, dv_n, dbias, s_qk = _na_bwd(p, bias, na_q_norm_g, na_k_norm_g, cos, sin, dyb, N, off_na, HB)
    rs_mix = _rs_join(rs_mix, dq_n)
    dp = jnp.concatenate([dq_a, dzf, dzbk, di_a, dog, dq_n, dk_n, dv_n, dga, dgb], axis=1)
    gWin = _mm_tn(h_all, dp, N_CHIP, "mm_gwin")
    rs_in = _rs_start("in", [gWin])
    rs_in = _rs_scatter(rs_in, _PENDING[0])
    dh = _mm_nt(dp, Win, F32, "mm_dh")
    grad_x, s_rms1 = _rms1_bwd(xall, dh, dxm, norm1_g, scale1, N)
    d_table = _bias_grad(dbias)

    grads = {}
    big_names = ["ada_w", "w_in", "w_branch_a", "w_branch_b", "w_out", "ffn_w1", "ffn_w3", "ffn_w2"]
    small_names = [n for n in order if n not in big_names]
    delta, new_m, new_v = {}, {}, {}

    def update(nm, after=None):
        reduced = nm != "ada_w"
        d_, m_, v_, *g_ = _adamw(weights[nm][0], grads[nm][0], moms[nm][0][0], moms[nm][1][0], "adamw_" + nm,
                                 after, copy_g=reduced)
        delta[nm], new_m[nm], new_v[nm] = d_[None], m_[None], v_[None]
        if reduced:
            grads[nm] = g_[0][None]
        return d_

    last = grad_x
    for nm, g in zip(["ffn_w2", "ffn_w1", "ffn_w3"], _rs_finish(rs_ffn, last)):
        grads[nm] = g[None]
        last = update(nm, last)
    for nm, g in zip(["w_out", "w_branch_a", "w_branch_b"], _rs_finish(rs_mix, last)):
        grads[nm] = g[None]
        last = update(nm, last)
    rs_in = _rs_join(rs_in, last)

    zD = jnp.zeros((1, D), F32)
    dmod_l = jnp.concatenate([s_rms1[2:3], s_rms1[3:4], s_rms2[3:4], s_rms2[0:1], s_rms2[1:2], d_g2], axis=0)
    dmod_c = jnp.concatenate([s_rms1[0:1], s_rms1[1:2], zD, zD, zD, zD], axis=0)
    pk1, offs1 = _pack([dmod_l, dmod_c, s_rms1[4], s_rms2[2], dlbl, s_ng[0], s_qk[0], s_qk[1], d_table,
                        s_conv[0:3], s_conv[3]])
    g1all = _allgather8(pk1, "gather_small1")
    tot1 = _sum8(g1all, "sum_small1")
    dmod_rows = _unpack(g1all, offs1, 0).reshape(N_DEV, N_MOD * D)
    dmod_c_tot = _unpack(tot1, offs1, 1).reshape(1, N_MOD * D)
    dmod16 = jnp.concatenate([dmod_rows, dmod_c_tot, jnp.zeros((7, N_MOD * D), F32)], axis=0)
    dmod16_mine = lax.dynamic_slice(dmod16, (0, chip * n_ada), (16, n_ada))
    g_ada_w, dact = _ada_bwd(cs, ada_w[0], dmod16_mine)
    pk2, offs2 = _pack([dact[8]])
    g2all = _allgather8(pk2, "gather_small2")
    dact_rows = _unpack(g2all, offs2, 0)
    dact_sel = jnp.concatenate([dact_rows[2 * j][None] for j in range(N_CHIP)] + [jnp.zeros((4, D), F32)], axis=0)

    grads["ada_w"] = g_ada_w[None]
    grads["ada_b"] =(_unpack(tot1, offs1, 0) + _unpack(tot1, offs1, 1)).reshape(1, N_MOD * D)
    grads["norm1_g"] = _unpack(tot1, offs1, 2)[None]
    grads["norm2_g"] = _unpack(tot1, offs1, 3)[None]
    g_lbl = _unpack(tot1, offs1, 4)
    n_lb = HA // N_CHIP
    grads["hgrn_lb_logits"] = lax.dynamic_slice(g_lbl, (0, 0, chip * n_lb), (2, 2, n_lb))
    grads["hgrn_norm_g"] = _unpack(tot1, offs1, 5)[None]
    grads["na_q_norm_g"] = _unpack(tot1, offs1, 6)[None]
    grads["na_k_norm_g"] = _unpack(tot1, offs1, 7)[None]
    grads["na_rel_bias"] = _unpack(tot1, offs1, 8)[None]
    g_cw = _unpack(tot1, offs1, 9)
    n_f = F // N_CHIP
    grads["ffn_conv_w"] = lax.dynamic_slice(g_cw, (0, chip * n_f), (3, n_f))[None]
    grads["ffn_conv_b"] = _unpack(tot1, offs1, 10)[None]

    g_c_ctx = _dsilu_rows(dact_sel, c_ctx[None, :], "grad_c_ctx")
    grads["c_ctx"] = g_c_ctx[0]

    last = update("ada_w", g_c_ctx)
    pw, offw = _pack([weights[n] for n in small_names])
    pg, _ = _pack([grads[n] for n in small_names])
    pm, _ = _pack([moms[n][0] for n in small_names])
    pv, _ = _pack([moms[n][1] for n in small_names])
    d_, m_, v_ = _adamw(pw, pg, pm, pv, "adamw_small", last)
    for i, nm in enumerate(small_names):
        delta[nm], new_m[nm], new_v[nm] = _unpack(d_, offw, i), _unpack(m_, offw, i), _unpack(v_, offw, i)
    grads["w_in"] = _rs_finish(rs_in, d_)[0][None]
    update("w_in")

    return (loss, grad_x[None], *[grads[n] for n in order], *[delta[n] for n in order],
            *[new_m[n] for n in order], *[new_v[n] for n in order])


def _dsilu_rows(v, cv, name):
    D = v.shape[1]

    def body(v_ref, c_ref, o_ref):
        t = c_ref[...]
        s = _sigmoid(t)
        o_ref[...] = (((v_ref[0:1, :] + v_ref[1:2, :]) + v_ref[2:3, :]) + v_ref[3:4, :]) * (s * (1.0 + t * (1.0 - s)))

    return _pcall(body, name=name, out_shape=jax.ShapeDtypeStruct((1, D), F32),
                          compiler_params=_params())(v, cv)
```

```python
import functools

import numpy as np
import jax
import jax.numpy as jnp
from jax import lax
from jax.experimental import pallas as pl
from jax.experimental.pallas import tpu as pltpu

F32 = jnp.float32
BF16 = jnp.bfloat16
MESH = pl.DeviceIdType.MESH

HEAD = 128
GRID_W = 64
WIN_R = 8
WIN_C = 16
ROPE_THETA = 10000.0
EPS = 1e-6
N_MOD = 6
CHUNK = 16
HGRN_UNROLL = 4
ADAM_LR = 0.001
ADAM_B1 = 0.9
ADAM_B2 = 0.999
ADAM_EPS = 1e-08
ADAM_WD = 0.01
ADAM_STEP = 10
NEG = -1e30
VMEM_LIMIT = 56 * 1024 * 1024
N_DEV = 8
N_CHIP = 4
HI = lax.Precision.HIGHEST


def _pick(n, cands):
    for c in cands:
        if n % c == 0:
            return c
    return n


def _row_tile(rows, cols, target_bytes=1 << 20):
    want = max(16, target_bytes // (4 * cols))
    for t in (512, 256, 128, 64, 32, 16, 8):
        if t <= want and rows % t == 0:
            return t
    return rows


def _params(sem=None):
    return pltpu.CompilerParams(dimension_semantics=sem, vmem_limit_bytes=VMEM_LIMIT)


def _dot(a, b):
    return jnp.dot(a, b, preferred_element_type=F32)


def _dot_nt(a, b):
    return lax.dot_general(a, b, (((1,), (1,)), ((), ())), preferred_element_type=F32)


def _dot_tn(a, b):
    return lax.dot_general(a, b, (((0,), (0,)), ((), ())), preferred_element_type=F32)


def _sigmoid(x):
    return 1.0 / (1.0 + jnp.exp(-x))


def _col_tile(n):
    return n if n <= 1536 else _pick(n, (1024, 768, 512, 384, 256, 128))


def _mm_nn(x, w3, out_dtype, name):
    M, K = x.shape
    S, _, n = w3.shape
    tm = _pick(M, (768, 512, 256, 128, 64))
    tn = _col_tile(n)
    nb = n // tn

    def body(x_ref, w_ref, o_ref):
        o_ref[...] = _dot(x_ref[...].astype(BF16), w_ref[0]).astype(o_ref.dtype)

    return _pcall(
        body, name=name, grid=(M // tm, S * nb),
        in_specs=[pl.BlockSpec((tm, K), lambda i, j: (i, 0)),
                  pl.BlockSpec((1, K, tn), lambda i, j: (j // nb, 0, j % nb))],
        out_specs=pl.BlockSpec((tm, tn), lambda i, j: (i, j)),
        out_shape=jax.ShapeDtypeStruct((M, S * n), out_dtype),
        compiler_params=_params(("parallel", "parallel")),
    )(x, w3)


def _mm_nn_sel(x, w3, sel, out_dtype, name, prev=None, n_shards=None):
    M, K = x.shape
    S, _, n = w3.shape
    n_shards = S if n_shards is None else n_shards
    tm = _pick(M, (768, 512, 256, 128, 64))
    tn = _col_tile(n)
    nb = n // tn
    k = sel.shape[1]

    def body(sel_ref, x_ref, w_ref, *rest):
        rest[-1][...] = _dot(x_ref[...].astype(BF16), w_ref[0]).astype(out_dtype)

    in_specs = [pl.BlockSpec((tm, K), lambda i, j, sel_ref: (i, 0)),
                pl.BlockSpec((1, K, tn), lambda i, j, sel_ref: (sel_ref[0, j // nb], 0, j % nb))]
    operands = [sel, x, w3]
    if prev is not None:
        in_specs.append(_ANY)
        operands.append(prev)
    return pl.pallas_call(
        body, name=name,
        grid_spec=pltpu.PrefetchScalarGridSpec(
            num_scalar_prefetch=1, grid=(M // tm, k * nb), in_specs=in_specs,
            out_specs=pl.BlockSpec((tm, tn), lambda i, j, sel_ref: (i, sel_ref[1, j // nb] * nb + j % nb))),
        out_shape=jax.ShapeDtypeStruct((M, n_shards * n), out_dtype),
        input_output_aliases={} if prev is None else {3: 0},
        compiler_params=_params(("parallel", "parallel")),
    )(*operands)


def _mm_nt(dy, w3, out_dtype, name):
    M = dy.shape[0]
    S, K, n = w3.shape
    tm = _pick(M, (768, 512, 256, 128, 64))
    tk = K if K <= 2048 else _pick(K, (1408, 1024, 512, 256, 128))
    tc = n if n <= 2048 else _col_tile(n)
    nb = n // tc
    nsteps = S * nb

    def body(dy_ref, w_ref, o_ref, acc_ref):
        s = pl.program_id(2)

        @pl.when(s == 0)
        def _():
            acc_ref[...] = jnp.zeros_like(acc_ref)

        acc_ref[...] += _dot_nt(dy_ref[...].astype(BF16), w_ref[0])

        @pl.when(s == nsteps - 1)
        def _():
            o_ref[...] = acc_ref[...].astype(o_ref.dtype)

    return _pcall(
        body, name=name, grid=(M // tm, K // tk, nsteps),
        in_specs=[pl.BlockSpec((tm, tc), lambda i, k, s: (i, s)),
                  pl.BlockSpec((1, tk, tc), lambda i, k, s: (s // nb, k, s % nb))],
        out_specs=pl.BlockSpec((tm, tk), lambda i, k, s: (i, k)),
        out_shape=jax.ShapeDtypeStruct((M, K), out_dtype),
        scratch_shapes=[pltpu.VMEM((tm, tk), F32)],
        compiler_params=_params(("parallel", "parallel", "arbitrary")),
    )(dy, w3)


def _mm_tn(x, dy, S, name):
    M, K = x.shape
    n = dy.shape[1] // S
    tk = _pick(K, (512, 256, 128))
    tn = _col_tile(n)
    nb = n // tn

    def body(x_ref, dy_ref, o_ref):
        o_ref[0] = _dot_tn(x_ref[...].astype(BF16), dy_ref[...].astype(BF16)).astype(BF16)

    return _pcall(
        body, name=name, grid=(S * nb, K // tk),
        in_specs=[pl.BlockSpec((M, tk), lambda j, k: (0, k)),
                  pl.BlockSpec((M, tn), lambda j, k: (0, j))],
        out_specs=pl.BlockSpec((1, tk, tn), lambda j, k: (j // nb, k, j % nb)),
        out_shape=jax.ShapeDtypeStruct((S, K, n), BF16),
        compiler_params=_params(("parallel", "parallel")),
    )(x, dy)


def _chip_index():
    return (2 * lax.axis_index("x") + lax.axis_index("y")).astype(jnp.int32).reshape(1)


def _cast_bf16_slot(w, name, second_copy=False):
    R, C = w.shape
    tr = _row_tile(R, C, 2 << 20)

    def body(j_ref, w_ref, o_ref, *more):
        v = w_ref[...].astype(BF16)
        o_ref[0] = v
        if second_copy:
            more[0][...] = v

    slot_spec = pl.BlockSpec((1, tr, C), lambda i, j_ref: (j_ref[0], i, 0))
    plain_spec = pl.BlockSpec((tr, C), lambda i, j_ref: (i, 0))
    slot_shape = jax.ShapeDtypeStruct((N_CHIP, R, C), BF16)
    return _pcall(
        body, name=name,
        grid_spec=pltpu.PrefetchScalarGridSpec(
            num_scalar_prefetch=1, grid=(R // tr,),
            in_specs=[plain_spec],
            out_specs=[slot_spec, plain_spec] if second_copy else slot_spec),
        out_shape=[slot_shape, jax.ShapeDtypeStruct((R, C), BF16)] if second_copy else slot_shape,
        compiler_params=_params(("parallel",)),
    )(_chip_index(), w)


def _fill_slot(buf, shard, slot):
    _, R, C = buf.shape
    tr = _row_tile(R, C, 2 << 20)

    def body(s_ref, shard_ref, buf_ref, o_ref):
        o_ref[0] = shard_ref[...]

    return pl.pallas_call(
        body, name="fill_slot",
        grid_spec=pltpu.PrefetchScalarGridSpec(
            num_scalar_prefetch=1, grid=(R // tr,),
            in_specs=[pl.BlockSpec((tr, C), lambda i, s_ref: (i, 0)), pl.BlockSpec(memory_space=pl.ANY)],
            out_specs=pl.BlockSpec((1, tr, C), lambda i, s_ref: (s_ref[0], i, 0))),
        out_shape=jax.ShapeDtypeStruct(buf.shape, buf.dtype),
        input_output_aliases={2: 0},
        compiler_params=_params(("parallel",)),
    )(slot.astype(jnp.int32).reshape(1), shard, buf)


def _pos():
    return lax.axis_index("x"), lax.axis_index("y"), lax.axis_index("c")


def _other_chips(x, y):
    return [(x, 1 - y), (1 - x, y), (1 - x, 1 - y)]


def _allgather8(v, name):
    R, C = v.shape

    def body(x_ref, out_ref, send_sems, recv_sems, local_sem):
        x, y, c = _pos()
        me, sibling = (x, y, c), (x, y, 1 - c)
        chips = _other_chips(x, y)

        def slot(px, py, pc):
            return out_ref.at[4 * px + 2 * py + pc]

        def copy(k, block, to, src=None):
            return pltpu.make_async_remote_copy(
                src_ref=slot(*block) if src is None else src, dst_ref=slot(*block),
                send_sem=send_sems.at[k], recv_sem=recv_sems.at[k], device_id=to, device_id_type=MESH)

        mine = pltpu.make_async_copy(x_ref, slot(*me), local_sem)
        mine.start()
        first = [copy(0, me, sibling, src=x_ref)]
        first += [copy(1 + j, me, (*chip, c), src=x_ref) for j, chip in enumerate(chips)]
        for cp in first:
            cp.start()
        passed = [copy(4 + j, (*chip, c), sibling) for j, chip in enumerate(chips)]
        for j, chip in enumerate(chips):
            copy(1 + j, (*chip, c), me).wait_recv()
            passed[j].start()
        copy(0, sibling, me).wait_recv()
        for j, chip in enumerate(chips):
            copy(4 + j, (*chip, 1 - c), me).wait_recv()
        for cp in first + passed:
            cp.wait_send()
        mine.wait()

    return _pcall(
        body, name=name,
        out_shape=jax.ShapeDtypeStruct((N_DEV, R, C), v.dtype),
        in_specs=[pl.BlockSpec(memory_space=pltpu.VMEM)],
        out_specs=pl.BlockSpec(memory_space=pltpu.VMEM),
        scratch_shapes=[pltpu.SemaphoreType.DMA((7,)), pltpu.SemaphoreType.DMA((7,)), pltpu.SemaphoreType.DMA],
        compiler_params=pltpu.CompilerParams(vmem_limit_bytes=VMEM_LIMIT),
    )(v)


_HBM = pl.BlockSpec(memory_space=pltpu.HBM)
_SEM = pl.BlockSpec(memory_space=pltpu.SEMAPHORE)
_ANY = pl.BlockSpec(memory_space=pl.ANY)
_EFFECT = pltpu.SideEffectType.DATAFLOW_SIDE_EFFECTING
_PENDING = []


def _pcall(body, **kw):
    def run(*operands):
        if not _PENDING or "in_specs" not in kw:
            return pl.pallas_call(body, **kw)(*operands)
        deps = list(_PENDING)
        n = len(operands)

        def tied(*refs):
            return body(*refs[:n], *refs[n + len(deps):])

        return pl.pallas_call(tied, **{**kw, "in_specs": list(kw["in_specs"]) + [_ANY] * len(deps)})(*operands, *deps)
    return run


def _copies(plan, refs, send_sems, recv_sems):
    return [pltpu.make_async_remote_copy(src_ref=src, dst_ref=dst, send_sem=send_sems.at[k], recv_sem=recv_sems.at[k],
                                         device_id=dev, device_id_type=MESH)
            for k, (src, dst, dev) in enumerate(plan(refs))]


def _xfer_start(name, bufs, plan, n_copies, after=None):
    n = len(bufs)
    deps = list(_PENDING) + ([after] if after is not None else [])
    nd = len(deps)

    def body(*refs):
        for cp in _copies(plan, refs[:n], refs[n + nd], refs[n + nd + 1]):
            cp.start()
        refs[-1][...] = jnp.zeros_like(refs[-1])

    outs = pl.pallas_call(
        body, name=name,
        out_shape=(pltpu.SemaphoreType.DMA((n_copies,)), pltpu.SemaphoreType.DMA((n_copies,)),
                   *[pltpu.HBM(b.shape, b.dtype) for b in bufs], jax.ShapeDtypeStruct((8, 128), F32)),
        in_specs=[_HBM] * n + [_ANY] * nd,
        out_specs=(_SEM, _SEM, *[_HBM] * n, pl.BlockSpec(memory_space=pltpu.VMEM)),
        input_output_aliases={t: 2 + t for t in range(n)},
        compiler_params=pltpu.CompilerParams(has_side_effects=_EFFECT),
    )(*[pltpu.with_memory_space_constraint(b, pltpu.HBM) for b in bufs], *deps)
    _PENDING[:] = [outs[-1]]
    return (outs[0], outs[1]), list(outs[2:2 + n])


def _xfer_wait(name, sems, bufs, plan, after):
    n = len(bufs)
    after = tuple(after) if isinstance(after, (tuple, list)) else (after,)

    def body(*refs):
        cps = _copies(plan, refs[:n], refs[n], refs[n + 1])
        for cp in cps:
            cp.wait_send()
        for cp in cps:
            cp.wait_recv()

    outs = pl.pallas_call(
        body, name=name,
        out_shape=tuple(pltpu.HBM(b.shape, b.dtype) for b in bufs),
        in_specs=[_HBM] * n + [_SEM, _SEM] + [_ANY] * len(after),
        out_specs=tuple([_HBM] * n),
        input_output_aliases={t: t for t in range(n)},
        compiler_params=pltpu.CompilerParams(has_side_effects=_EFFECT),
    )(*bufs, sems[0], sems[1], *after)
    return list(outs)


def _half(ref_rows, hc):
    h = ref_rows // 2
    return pl.ds(hc * h, h)


ALL_CHIPS = (0, 1, 2)
NEIGHBOURS = (0, 1)
DIAGONAL = (2,)


def _plan_gather_ici(which):
    def plan(bufs):
        x, y, c = _pos()
        j = 2 * x + y
        chips = _other_chips(x, y)
        return [(b.at[j, _half(b.shape[1], c)], b.at[j, _half(b.shape[1], c)], (*chips[k], c))
                for b in bufs for k in which]
    return plan


def _plan_gather_d2d(which):
    def plan(bufs):
        x, y, c = _pos()
        chips = _other_chips(x, y)
        out = []
        for b in bufs:
            for k in which:
                blk = b.at[2 * chips[k][0] + chips[k][1], _half(b.shape[1], c)]
                out.append((blk, blk, (x, y, 1 - c)))
        return out
    return plan


def _plan_diag_ici(bufs):
    x, y, c = _pos()
    src, land = bufs
    return [(src.at[_half(src.shape[0], c)], land.at[_half(land.shape[0], c)], (1 - x, 1 - y, c))]


def _plan_diag_d2d(bufs):
    x, y, c = _pos()
    blk = bufs[0].at[_half(bufs[0].shape[0], c)]
    return [(blk, blk, (x, y, 1 - c))]


def _plan_pair_swap(n):
    def plan(bufs):
        x, y, c = _pos()
        return [(g.at[:, _half(g.shape[1], 1 - c)], land, (x, y, 1 - c)) for g, land in zip(bufs[:n], bufs[n:])]
    return plan


def _plan_chip_scatter(n):
    def plan(bufs):
        x, y, c = _pos()
        return [(p.at[2 * chip[0] + chip[1]], land.at[k], (*chip, c))
                for p, land in zip(bufs[:n], bufs[n:]) for k, chip in enumerate(_other_chips(x, y))]
    return plan


def _plan_pair_join(bufs):
    x, y, c = _pos()
    return [(b.at[_half(b.shape[0], c)], b.at[_half(b.shape[0], c)], (x, y, 1 - c)) for b in bufs]


def _empty_hbm(shape, dtype):
    return pltpu.with_memory_space_constraint(lax.empty(shape, dtype), pltpu.HBM)


def _gather_start(tag, bufs, after=None):
    sems, bufs = _xfer_start(f"gather_ici_start_{tag}", bufs, _plan_gather_ici(ALL_CHIPS), 3 * len(bufs), after)
    return dict(tag=tag, sems=sems, bufs=bufs)


def _gather_mid(st, after):
    tag = st["tag"]
    bufs = _xfer_wait(f"gather_ici_wait_{tag}", st["sems"], st["bufs"], _plan_gather_ici(ALL_CHIPS), after)
    sems, bufs = _xfer_start(f"gather_d2d_start_{tag}", bufs, _plan_gather_d2d(ALL_CHIPS), 3 * len(bufs))
    return dict(tag=tag, sems=sems, bufs=bufs)


def _gather_finish(st, after):
    return _xfer_wait(f"gather_d2d_wait_{st['tag']}", st["sems"], st["bufs"], _plan_gather_d2d(ALL_CHIPS), after)


def _d2d_hand_over(tag, sems, bufs, which, after):
    bufs = _xfer_wait(f"gather_ici_wait_{tag}", sems, bufs, _plan_gather_ici(which), after)
    sems, bufs = _xfer_start(f"gather_d2d_start_{tag}", bufs, _plan_gather_d2d(which), len(which) * len(bufs))
    return _xfer_wait(f"gather_d2d_wait_{tag}", sems, bufs, _plan_gather_d2d(which), after)


def _pair_add(g, r, name):
    S, R, C = g.shape
    h = R // 2
    tr = _row_tile(h, C)
    nb = h // tr

    def body(c_ref, g_ref, r_ref, o_ref):
        o_ref[...] = (g_ref[...].astype(F32) + r_ref[...].astype(F32)).astype(BF16)

    return _pcall(
        body, name=name,
        grid_spec=pltpu.PrefetchScalarGridSpec(
            num_scalar_prefetch=1, grid=(S, nb),
            in_specs=[pl.BlockSpec((1, tr, C), lambda s, i, c_ref: (s, c_ref[0] * nb + i, 0)),
                      pl.BlockSpec((1, tr, C), lambda s, i, c_ref: (s, i, 0))],
            out_specs=pl.BlockSpec((1, tr, C), lambda s, i, c_ref: (s, i, 0))),
        out_shape=jax.ShapeDtypeStruct((S, h, C), BF16),
        compiler_params=_params(("parallel", "parallel")),
    )(lax.axis_index("c").astype(jnp.int32).reshape(1), g, r)


def _chip_sum(p, rb, name):
    S, h, C = p.shape
    tr = _row_tile(h, C)
    nb = h // tr
    jc = jnp.concatenate([_chip_index(), lax.axis_index("c").astype(jnp.int32).reshape(1)])

    def body(jc_ref, p_ref, r_ref, o_ref):
        o_ref[...] = ((p_ref[0].astype(F32) + r_ref[0].astype(F32)) + r_ref[1].astype(F32)) + r_ref[2].astype(F32)

    return _pcall(
        body, name=name,
        grid_spec=pltpu.PrefetchScalarGridSpec(
            num_scalar_prefetch=1, grid=(nb,),
            in_specs=[pl.BlockSpec((1, tr, C), lambda i, jc_ref: (jc_ref[0], i, 0)),
                      pl.BlockSpec((3, tr, C), lambda i, jc_ref: (0, i, 0))],
            out_specs=pl.BlockSpec((tr, C), lambda i, jc_ref: (jc_ref[1] * nb + i, 0))),
        out_shape=jax.ShapeDtypeStruct((2 * h, C), F32),
        compiler_params=_params(("parallel",)),
    )(jc, p, rb)


def _rs_start(tag, gs):
    n = len(gs)
    lands = [_empty_hbm((g.shape[0], g.shape[1] // 2, g.shape[2]), g.dtype) for g in gs]
    sems, bufs = _xfer_start(f"rs_swap_start_{tag}", list(gs) + lands, _plan_pair_swap(n), n)
    return dict(tag=tag, n=n, sems=sems, bufs=bufs)


def _rs_scatter(st, after):
    tag, n = st["tag"], st["n"]
    bufs = _xfer_wait(f"rs_swap_wait_{tag}", st["sems"], st["bufs"], _plan_pair_swap(n), after)
    ps = [_pair_add(g, r, f"rs_pair_add_{tag}{t}") for t, (g, r) in enumerate(zip(bufs[:n], bufs[n:]))]
    lands = [_empty_hbm((3,) + p.shape[1:], p.dtype) for p in ps]
    sems, bufs = _xfer_start(f"rs_scatter_start_{tag}", ps + lands, _plan_chip_scatter(n), 3 * n)
    return dict(tag=tag, n=n, sems=sems, bufs=bufs)


def _rs_join(st, after):
    tag, n = st["tag"], st["n"]
    bufs = _xfer_wait(f"rs_scatter_wait_{tag}", st["sems"], st["bufs"], _plan_chip_scatter(n), after)
    fs = [_chip_sum(p, rb, f"rs_chip_sum_{tag}{t}") for t, (p, rb) in enumerate(zip(bufs[:n], bufs[n:]))]
    sems, bufs = _xfer_start(f"rs_join_start_{tag}", fs, _plan_pair_join, n)
    return dict(tag=tag, n=n, sems=sems, bufs=bufs)


def _rs_finish(st, after):
    return _xfer_wait(f"rs_join_wait_{st['tag']}", st["sems"], st["bufs"], _plan_pair_join, after)


def _sum8(g, name):
    _, R, C = g.shape

    def body(g_ref, o_ref):
        acc = g_ref[0]
        for d in range(1, N_DEV):
            acc = acc + g_ref[d]
        o_ref[...] = acc

    return _pcall(body, name=name, out_shape=jax.ShapeDtypeStruct((R, C), F32),
                          compiler_params=_params())(g)


def _ada_fwd(cs, w, b):
    D, n = w.shape
    tn = _pick(n, (512, 384, 256, 128))

    def body(c_ref, w_ref, b_ref, o_ref):
        cv = c_ref[...]
        a = (cv * _sigmoid(cv)).astype(BF16)
        o_ref[...] = _dot(a, w_ref[...].astype(BF16)) + b_ref[...]

    return _pcall(
        body, name="ada_fwd", grid=(n // tn,),
        in_specs=[pl.BlockSpec((16, D), lambda j: (0, 0)), pl.BlockSpec((D, tn), lambda j: (0, j)),
                  pl.BlockSpec((1, tn), lambda j: (0, j))],
        out_specs=pl.BlockSpec((16, tn), lambda j: (0, j)),
        out_shape=jax.ShapeDtypeStruct((16, n), F32),
        compiler_params=_params(("parallel",)),
    )(cs, w, b)


def _ada_bwd(cs, w, dmod):
    D, n = w.shape
    tn = _pick(n, (512, 384, 256, 128))

    def body(c_ref, w_ref, d_ref, gw_ref, da_ref):
        j = pl.program_id(0)
        cv = c_ref[...]
        a = cv * _sigmoid(cv)
        d = d_ref[...]
        gw_ref[...] = lax.dot_general(a, d, (((0,), (0,)), ((), ())), precision=HI, preferred_element_type=F32)

        @pl.when(j == 0)
        def _():
            da_ref[...] = jnp.zeros_like(da_ref)

        da_ref[...] += _dot_nt(d.astype(BF16), w_ref[...].astype(BF16))

    return _pcall(
        body, name="ada_bwd", grid=(n // tn,),
        in_specs=[pl.BlockSpec((16, D), lambda j: (0, 0)), pl.BlockSpec((D, tn), lambda j: (0, j)),
                  pl.BlockSpec((16, tn), lambda j: (0, j))],
        out_specs=[pl.BlockSpec((D, tn), lambda j: (0, j)), pl.BlockSpec((16, D), lambda j: (0, 0))],
        out_shape=[jax.ShapeDtypeStruct((D, n), F32), jax.ShapeDtypeStruct((16, D), F32)],
        compiler_params=_params(("arbitrary",)),
    )(cs, w, dmod)


def _rms1_fwd(xall, gain, shift2, scale2, n_ctx):
    T, D = xall.shape
    tb = _pick(n_ctx, (256, 128, 64, 32, 16))
    nctx = n_ctx // tb

    def body(x_ref, g_ref, sh_ref, sc_ref, o_ref):
        i = pl.program_id(0)
        xv = x_ref[...]
        r = lax.rsqrt(jnp.mean(xv * xv, axis=-1, keepdims=True) + EPS)
        nrm = xv * r * g_ref[...]
        lat = i >= nctx
        sh = jnp.where(lat, sh_ref[1:2, :], sh_ref[0:1, :])
        sc = jnp.where(lat, sc_ref[1:2, :], sc_ref[0:1, :])
        o_ref[...] = (nrm * (1.0 + sc) + sh).astype(BF16)

    vec = lambda r: pl.BlockSpec((r, D), lambda i: (0, 0))
    return _pcall(
        body, name="rms1_fwd", grid=(T // tb,),
        in_specs=[pl.BlockSpec((tb, D), lambda i: (i, 0)), vec(1), vec(2), vec(2)],
        out_specs=pl.BlockSpec((tb, D), lambda i: (i, 0)),
        out_shape=jax.ShapeDtypeStruct((T, D), BF16),
        compiler_params=_params(("parallel",)),
    )(xall, gain, shift2, scale2)


def _rms1_bwd(xall, dh, dxmid, gain, scale2, n_ctx):
    T, D = xall.shape
    L = T - n_ctx
    tb = _pick(n_ctx, (256, 128, 64, 32, 16))
    nctx = n_ctx // tb

    def body(x_ref, dh_ref, dxm_ref, g_ref, sc_ref, dx_ref, cs_ref):
        i = pl.program_id(0)
        lat = i >= nctx
        xv = x_ref[...]
        r = lax.rsqrt(jnp.mean(xv * xv, axis=-1, keepdims=True) + EPS)
        xh = xv * r
        g = g_ref[...]
        nrm = xh * g
        sc = jnp.where(lat, sc_ref[1:2, :], sc_ref[0:1, :])
        dhv = dh_ref[...]
        dn = dhv * (1.0 + sc)
        dxh = dn * g
        dxv = r * (dxh - xh * jnp.mean(dxh * xh, axis=-1, keepdims=True))
        s_sh = jnp.sum(dhv, axis=0, keepdims=True)
        s_sc = jnp.sum(dhv * nrm, axis=0, keepdims=True)
        s_g = jnp.sum(dn * xh, axis=0, keepdims=True)
        zero = jnp.zeros_like(s_sh)
        rows = lax.broadcasted_iota(jnp.int32, (8, D), 0)
        upd = jnp.where(rows == 0, jnp.where(lat, zero, s_sh),
              jnp.where(rows == 1, jnp.where(lat, zero, s_sc),
              jnp.where(rows == 2, jnp.where(lat, s_sh, zero),
              jnp.where(rows == 3, jnp.where(lat, s_sc, zero),
              jnp.where(rows == 4, s_g, 0.0)))))

        @pl.when(i == 0)
        def _():
            cs_ref[...] = jnp.zeros_like(cs_ref)

        cs_ref[...] += upd

        @pl.when(lat)
        def _():
            dx_ref[...] = dxv + dxm_ref[...]

    lat_blk = lambda i: (jnp.maximum(i - nctx, 0), 0)
    vec = lambda r: pl.BlockSpec((r, D), lambda i: (0, 0))
    return _pcall(
        body, name="rms1_bwd", grid=(T // tb,),
        in_specs=[pl.BlockSpec((tb, D), lambda i: (i, 0)), pl.BlockSpec((tb, D), lambda i: (i, 0)),
                  pl.BlockSpec((tb, D), lat_blk), vec(1), vec(2)],
        out_specs=[pl.BlockSpec((tb, D), lat_blk), vec(8)],
        out_shape=[jax.ShapeDtypeStruct((L, D), F32), jax.ShapeDtypeStruct((8, D), F32)],
        compiler_params=_params(("arbitrary",)),
    )(xall, dh, dxmid, gain, scale2)


def _resid_rms2_fwd(x, mo, vecs):
    L, D = x.shape
    tb = _pick(L, (256, 128, 64))

    def body(x_ref, mo_ref, v_ref, xm_ref, h_ref):
        xm = x_ref[...] + v_ref[0:1, :] * mo_ref[...]
        xm_ref[...] = xm
        r = lax.rsqrt(jnp.mean(xm * xm, axis=-1, keepdims=True) + EPS)
        h_ref[...] = (xm * r * v_ref[1:2, :] * (1.0 + v_ref[3:4, :]) + v_ref[2:3, :]).astype(BF16)

    blk = pl.BlockSpec((tb, D), lambda i: (i, 0))
    return _pcall(
        body, name="resid_rms2_fwd", grid=(L // tb,),
        in_specs=[blk, blk, pl.BlockSpec((8, D), lambda i: (0, 0))],
        out_specs=[blk, blk],
        out_shape=[jax.ShapeDtypeStruct((L, D), F32), jax.ShapeDtypeStruct((L, D), BF16)],
        compiler_params=_params(("parallel",)),
    )(x, mo, vecs)


def _resid_rms2_bwd(xmid, dh_a, dh_b, dy, mo, vecs):
    L, D = xmid.shape
    tb = _pick(L, (256, 128, 64))

    def body(xm_ref, da_ref, db_ref, dy_ref, mo_ref, v_ref, dxm_ref, dmo_ref, cs_ref):
        i = pl.program_id(0)
        xm = xm_ref[...]
        r = lax.rsqrt(jnp.mean(xm * xm, axis=-1, keepdims=True) + EPS)
        xh = xm * r
        g = v_ref[1:2, :]
        nrm = xh * g
        dhv = da_ref[...] + db_ref[...]
        dn = dhv * (1.0 + v_ref[3:4, :])
        dxh = dn * g
        dxm = dy_ref[...] + r * (dxh - xh * jnp.mean(dxh * xh, axis=-1, keepdims=True))
        dxm_ref[...] = dxm
        dmo_ref[...] = (dxm * v_ref[0:1, :]).astype(BF16)
        s0 = jnp.sum(dhv, axis=0, keepdims=True)
        s1 = jnp.sum(dhv * nrm, axis=0, keepdims=True)
        s2 = jnp.sum(dn * xh, axis=0, keepdims=True)
        s3 = jnp.sum(dxm * mo_ref[...], axis=0, keepdims=True)
        rows = lax.broadcasted_iota(jnp.int32, (8, D), 0)
        upd = jnp.where(rows == 0, s0, jnp.where(rows == 1, s1, jnp.where(rows == 2, s2,
              jnp.where(rows == 3, s3, 0.0))))

        @pl.when(i == 0)
        def _():
            cs_ref[...] = jnp.zeros_like(cs_ref)

        cs_ref[...] += upd

    blk = pl.BlockSpec((tb, D), lambda i: (i, 0))
    vec = pl.BlockSpec((8, D), lambda i: (0, 0))
    return _pcall(
        body, name="resid_rms2_bwd", grid=(L // tb,),
        in_specs=[blk, blk, blk, blk, blk, vec],
        out_specs=[blk, blk, vec],
        out_shape=[jax.ShapeDtypeStruct((L, D), F32), jax.ShapeDtypeStruct((L, D), BF16),
                   jax.ShapeDtypeStruct((8, D), F32)],
        compiler_params=_params(("arbitrary",)),
    )(xmid, dh_a, dh_b, dy, mo, vecs)


def _loss_head(xmid, f, g2, target):
    L, D = xmid.shape
    tb = _pick(L, (256, 128, 64))

    def body(xm_ref, f_ref, g_ref, t_ref, dy_ref, df_ref, s_ref):
        i = pl.program_id(0)
        fv = f_ref[...]
        g = g_ref[...]
        err = xm_ref[...] + g * fv - t_ref[...]
        dy = err * (1.0 / D)
        dy_ref[...] = dy
        df_ref[...] = (dy * g).astype(BF16)
        s0 = jnp.sum(dy * fv, axis=0, keepdims=True)
        part = 0.5 * jnp.sum(jnp.mean(err * err, axis=-1, keepdims=True), axis=0, keepdims=True)
        rows = lax.broadcasted_iota(jnp.int32, (8, D), 0)
        upd = jnp.where(rows == 0, s0, jnp.where(rows == 1, part, 0.0))

        @pl.when(i == 0)
        def _():
            s_ref[...] = jnp.zeros_like(s_ref)

        s_ref[...] += upd

    blk = pl.BlockSpec((tb, D), lambda i: (i, 0))
    return _pcall(
        body, name="loss_head", grid=(L // tb,),
        in_specs=[blk, blk, pl.BlockSpec((1, D), lambda i: (0, 0)), blk],
        out_specs=[blk, blk, pl.BlockSpec((8, D), lambda i: (0, 0))],
        out_shape=[jax.ShapeDtypeStruct((L, D), F32), jax.ShapeDtypeStruct((L, D), BF16),
                   jax.ShapeDtypeStruct((8, D), F32)],
        compiler_params=_params(("arbitrary",)),
    )(xmid, f, g2, target)


def _gate_cols(D, off):
    tc = _pick(np.gcd(D, off), (512, 256, 128))
    return tc, off // tc


def _merge_fwd(za, zb, p, n_ctx, off_a, off_b):
    L, D = za.shape
    tb = _pick(n_ctx, (256, 128, 64, 32, 16))
    nctx = n_ctx // tb
    tc, oa = _gate_cols(D, off_a)
    _, ob = _gate_cols(D, off_b)
    if off_b % tc:
        raise ValueError("gate column offsets must share a column tile")
    ob = off_b // tc

    def body(za_ref, zb_ref, ga_ref, gb_ref, z_ref):
        z_ref[...] = (_sigmoid(ga_ref[...]) * za_ref[...].astype(F32)
                      + _sigmoid(gb_ref[...]) * zb_ref[...].astype(F32)).astype(BF16)

    blk = pl.BlockSpec((tb, tc), lambda i, j: (i, j))
    return _pcall(
        body, name="merge_fwd", grid=(L // tb, D // tc),
        in_specs=[blk, blk, pl.BlockSpec((tb, tc), lambda i, j: (i + nctx, oa + j)),
                  pl.BlockSpec((tb, tc), lambda i, j: (i + nctx, ob + j))],
        out_specs=blk,
        out_shape=jax.ShapeDtypeStruct((L, D), BF16),
        compiler_params=_params(("parallel", "parallel")),
    )(za, zb, p, p)


def _merge_bwd(dz, za, zb, p, n_ctx, off_a, off_b):
    L, D = za.shape
    T = L + n_ctx
    tb = _pick(n_ctx, (256, 128, 64, 32, 16))
    nctx = n_ctx // tb
    tc = _gate_cols(D, off_a)[0]
    oa, ob = off_a // tc, off_b // tc

    def body(dz_ref, za_ref, zb_ref, ga_ref, gb_ref, dza_ref, dzb_ref, dga_ref, dgb_ref):
        i = pl.program_id(1)

        @pl.when(i < nctx)
        def _():
            dga_ref[...] = jnp.zeros_like(dga_ref)
            dgb_ref[...] = jnp.zeros_like(dgb_ref)

        @pl.when(i >= nctx)
        def _():
            dzv = dz_ref[...].astype(F32)
            sa = _sigmoid(ga_ref[...])
            sb = _sigmoid(gb_ref[...])
            dza_ref[...] = (dzv * sa).astype(BF16)
            dzb_ref[...] = (dzv * sb).astype(BF16)
            dga_ref[...] = (dzv * za_ref[...].astype(F32) * sa * (1.0 - sa)).astype(BF16)
            dgb_ref[...] = (dzv * zb_ref[...].astype(F32) * sb * (1.0 - sb)).astype(BF16)

    lat = pl.BlockSpec((tb, tc), lambda j, i: (jnp.maximum(i - nctx, 0), j))
    allr = pl.BlockSpec((tb, tc), lambda j, i: (i, j))
    return _pcall(
        body, name="merge_bwd", grid=(D // tc, T // tb),
        in_specs=[lat, lat, lat, pl.BlockSpec((tb, tc), lambda j, i: (i, oa + j)),
                  pl.BlockSpec((tb, tc), lambda j, i: (i, ob + j))],
        out_specs=[lat, lat, allr, allr],
        out_shape=[jax.ShapeDtypeStruct((L, D), BF16), jax.ShapeDtypeStruct((L, D), BF16),
                   jax.ShapeDtypeStruct((T, D), BF16), jax.ShapeDtypeStruct((T, D), BF16)],
        compiler_params=_params(("arbitrary", "arbitrary")),
    )(dz, za, zb, p, p)


def _shift_down(u, rows):
    return jnp.where(rows == 0, 0.0, pltpu.roll(u, 1, 0))


def _shift_up(u, rows):
    n = u.shape[0]
    return jnp.where(rows == n - 1, 0.0, pltpu.roll(u, n - 1, 0))


def _convgate_fwd(u1, u3, cw, cb):
    L, F = u1.shape
    tc = _pick(F, (256, 128))

    def body(u1_ref, u3_ref, w_ref, b_ref, a_ref):
        u = u1_ref[...].astype(F32)
        rows = lax.broadcasted_iota(jnp.int32, u.shape, 0)
        cv = _shift_down(u, rows) * w_ref[0:1, :] + u * w_ref[1:2, :] + _shift_up(u, rows) * w_ref[2:3, :] + b_ref[...]
        a_ref[...] = (cv * _sigmoid(cv) * u3_ref[...].astype(F32)).astype(BF16)

    blk = pl.BlockSpec((L, tc), lambda j: (0, j))
    return _pcall(
        body, name="convgate_fwd", grid=(F // tc,),
        in_specs=[blk, blk, pl.BlockSpec((8, tc), lambda j: (0, j)), pl.BlockSpec((1, tc), lambda j: (0, j))],
        out_specs=blk,
        out_shape=jax.ShapeDtypeStruct((L, F), BF16),
        compiler_params=_params(("parallel",)),
    )(u1, u3, cw, cb)


def _convgate_bwd(u1, u3, da, cw, cb):
    L, F = u1.shape
    tc = _pick(F, (256, 128))

    def body(u1_ref, u3_ref, da_ref, w_ref, b_ref, du1_ref, du3_ref, s_ref):
        u = u1_ref[...].astype(F32)
        rows = lax.broadcasted_iota(jnp.int32, u.shape, 0)
        um, up = _shift_down(u, rows), _shift_up(u, rows)
        w0, w1, w2 = w_ref[0:1, :], w_ref[1:2, :], w_ref[2:3, :]
        cv = um * w0 + u * w1 + up * w2 + b_ref[...]
        s = _sigmoid(cv)
        dav = da_ref[...].astype(F32)
        du3_ref[...] = (dav * cv * s).astype(BF16)
        dcv = dav * u3_ref[...].astype(F32) * (s * (1.0 + cv * (1.0 - s)))
        du1_ref[...] = (_shift_up(dcv, rows) * w0 + dcv * w1 + _shift_down(dcv, rows) * w2).astype(BF16)
        r8 = lax.broadcasted_iota(jnp.int32, (8, tc), 0)
        s0 = jnp.sum(dcv * um, axis=0, keepdims=True)
        s1 = jnp.sum(dcv * u, axis=0, keepdims=True)
        s2 = jnp.sum(dcv * up, axis=0, keepdims=True)
        s3 = jnp.sum(dcv, axis=0, keepdims=True)
        s_ref[...] = jnp.where(r8 == 0, s0, jnp.where(r8 == 1, s1, jnp.where(r8 == 2, s2,
                     jnp.where(r8 == 3, s3, 0.0))))

    blk = pl.BlockSpec((L, tc), lambda j: (0, j))
    v8 = pl.BlockSpec((8, tc), lambda j: (0, j))
    return _pcall(
        body, name="convgate_bwd", grid=(F // tc,),
        in_specs=[blk, blk, blk, v8, pl.BlockSpec((1, tc), lambda j: (0, j))],
        out_specs=[blk, blk, v8],
        out_shape=[jax.ShapeDtypeStruct((L, F), BF16), jax.ShapeDtypeStruct((L, F), BF16),
                   jax.ShapeDtypeStruct((8, F), F32)],
        compiler_params=_params(("parallel",)),
    )(u1, u3, da, cw, cb)


def _lower_bound(lbl_ref, d):
    l0, l1 = lbl_ref[d, 0:1, :], lbl_ref[d, 1:2, :]
    m = jnp.maximum(l0, l1)
    e0, e1 = jnp.exp(l0 - m), jnp.exp(l1 - m)
    return e0 / (e0 + e1)


def _chunk_cumsum(x, rev):
    n = x.shape[0]
    r = lax.broadcasted_iota(jnp.int32, x.shape, 0) % CHUNK
    k = 1
    while k < CHUNK:
        if rev:
            x = x + jnp.where(r < CHUNK - k, pltpu.roll(x, n - k, 0), 0.0)
        else:
            x = x + jnp.where(r >= k, pltpu.roll(x, k, 0), 0.0)
        k *= 2
    return x


def _gate_terms(z, lb):
    sg = _sigmoid(z)
    f = lb + (1.0 - lb) * sg
    return sg, f


def _decay_terms(z, lb, rev):
    _, f = _gate_terms(z, lb)
    g = jnp.log(f)
    return 1.0 - f, _chunk_cumsum(g, rev), _chunk_cumsum(g, not rev) - g


def _chunk_total(c, rev):
    return c[0:1, :] if rev else c[CHUNK - 1:CHUNK, :]


def _pair_decay(c, s, rev):
    t = lax.broadcasted_iota(jnp.int32, (CHUNK, 1), 0)
    later = (t <= s) if rev else (t >= s)
    return jnp.where(later, jnp.exp(c - c[s:s + 1, :]), 0.0)


def _scan_chunk(i, n_ctx_chunks, n_chunks, rev):
    if not rev:
        return i
    return jnp.where(i < n_ctx_chunks, n_ctx_chunks - 1 - i, n_chunks + n_ctx_chunks - 1 - i)


def _rows(ci):
    return pl.ds(pl.multiple_of(ci * CHUNK, CHUNK), CHUNK)


def _hgrn_cols(HA):
    return HA // HEAD


def _hgrn_fwd(p, lbl, ng, n_ctx, HA):
    T = p.shape[0]
    L = T - n_ctx
    nh = _hgrn_cols(HA)
    nc, ncc = T // CHUNK, n_ctx // CHUNK

    def body(q_ref, zf_ref, zb_ref, v_ref, og_ref, lbl_ref, ng_ref, ya_ref, o_ref, st_ref,
             c_scr, k_scr, qe_scr, ke_scr, o_scr):
        dirs = ((0, False, zf_ref), (1, True, zb_ref))
        for d, rev, z_ref in dirs:
            k, c, rest = _decay_terms(z_ref[...], _lower_bound(lbl_ref, d), rev)
            c_scr[d] = c
            k_scr[d] = k
            qe_scr[d] = (q_ref[...] * jnp.exp(c)).astype(BF16)
            ke_scr[d] = (k * jnp.exp(rest)).astype(BF16)

        def step(i2, states):
            states = list(states)
            for u in range(HGRN_UNROLL):
                for d, rev, _ in dirs:
                    St = states[d]
                    ci = _scan_chunk(HGRN_UNROLL * i2 + u, ncc, nc, rev)
                    rows = _rows(ci)
                    q, v, c, k = q_ref[rows, :], v_ref[rows, :], c_scr[d, rows, :], k_scr[d, rows, :]
                    st_ref[0, d, ci] = St.astype(BF16)
                    o = jnp.zeros((CHUNK, HEAD), F32)
                    for s in range(CHUNK):
                        E = _pair_decay(c, s, rev)
                        a = jnp.sum(q * E * k[s:s + 1, :], axis=1, keepdims=True)
                        o = o + a * v[s:s + 1, :]
                    o_scr[d, rows, :] = o + _dot_nt(qe_scr[d, rows, :], St.astype(BF16))
                    states[d] = St * jnp.exp(_chunk_total(c, rev)) + _dot_tn(v.astype(BF16), ke_scr[d, rows, :])
            return tuple(states)

        if nc % HGRN_UNROLL:
            raise ValueError("the number of chunks must be a multiple of HGRN_UNROLL")
        zero = jnp.zeros((HEAD, HEAD), F32)
        lax.fori_loop(0, nc // HGRN_UNROLL, step, (zero, zero))

        o = o_scr[0, pl.ds(n_ctx, L), :] + o_scr[1, pl.ds(n_ctx, L), :]
        o_ref[...] = o
        r = lax.rsqrt(jnp.mean(o * o, axis=-1, keepdims=True) + EPS)
        og = og_ref[pl.ds(n_ctx, L), :]
        ya_ref[...] =(o * r * ng_ref[...] * (og * _sigmoid(og))).astype(BF16)

    cb = HA // HEAD
    col = lambda kk: pl.BlockSpec((T, HEAD), lambda h: (0, kk * cb + h))
    return _pcall(
        body, name="hgrn_fwd", grid=(nh,),
        in_specs=[col(0), col(1), col(2), col(3), col(4),
                  pl.BlockSpec((2, 2, HEAD), lambda h: (0, 0, h)), pl.BlockSpec((1, HEAD), lambda h: (0, 0))],
        out_specs=[pl.BlockSpec((L, HEAD), lambda h: (0, h)), pl.BlockSpec((L, HEAD), lambda h: (0, h)),
                   pl.BlockSpec((1, 2, nc, HEAD, HEAD), lambda h: (h, 0, 0, 0, 0))],
        out_shape=[jax.ShapeDtypeStruct((L, HA), BF16), jax.ShapeDtypeStruct((L, HA), F32),
                   jax.ShapeDtypeStruct((nh, 2, nc, HEAD, HEAD), BF16)],
        scratch_shapes=[pltpu.VMEM((2, T, HEAD), F32), pltpu.VMEM((2, T, HEAD), F32),
                        pltpu.VMEM((2, T, HEAD), BF16), pltpu.VMEM((2, T, HEAD), BF16),
                        pltpu.VMEM((2, T, HEAD), F32)],
        compiler_params=_params(("parallel",)),
    )(p, p, p, p, p, lbl, ng)


def _hgrn_bwd(p, lbl, ng, o, dya, st, n_ctx, HA):
    T = p.shape[0]
    L = T - n_ctx
    nh = _hgrn_cols(HA)
    nc, ncc = T // CHUNK, n_ctx // CHUNK

    def body(q_ref, zf_ref, zb_ref, v_ref, og_ref, lbl_ref, ng_ref, o_ref, dya_ref, st_ref,
             dq_ref, dzf_ref, dzb_ref, dv_ref, dog_ref, dlbl_ref, dng_ref,
             do_scr, c_scr, k_scr, qe_scr, ke_scr, dg_scr, dk_scr, dq_scr, dv_scr, row_scr):
        h = pl.program_id(0)
        ov = o_ref[...]
        r = lax.rsqrt(jnp.mean(ov * ov, axis=-1, keepdims=True) + EPS)
        oh = ov * r
        ogv = og_ref[pl.ds(n_ctx, L), :]
        sg_o = _sigmoid(ogv)
        dyv = dya_ref[...]
        ngv = ng_ref[...]
        dog_ref[pl.ds(0, n_ctx), :] = jnp.zeros((n_ctx, HEAD), BF16)
        dog_ref[pl.ds(n_ctx, L), :] = (dyv * oh * ngv * (sg_o * (1.0 + ogv * (1.0 - sg_o)))).astype(BF16)
        don = dyv * (ogv * sg_o)
        dng = jnp.sum(don * oh, axis=0, keepdims=True)
        doh = don * ngv
        do_scr[pl.ds(0, n_ctx), :] = jnp.zeros((n_ctx, HEAD), F32)
        do_scr[pl.ds(n_ctx, L), :] = r * (doh - oh * jnp.mean(doh * oh, axis=-1, keepdims=True))

        @pl.when(h == 0)
        def _():
            dng_ref[...] = jnp.zeros_like(dng_ref)

        dng_ref[0:1, :] += dng

        t16 = lax.broadcasted_iota(jnp.int32, (CHUNK, HEAD), 0)
        dirs = ((0, False, zf_ref, dzf_ref), (1, True, zb_ref, dzb_ref))
        for d, rev, z_ref, _ in dirs:
            k, c, rest = _decay_terms(z_ref[...], _lower_bound(lbl_ref, d), rev)
            c_scr[d] = c
            k_scr[d] = k
            qe_scr[d] = (q_ref[...] * jnp.exp(c)).astype(BF16)
            ke_scr[d] = (k * jnp.exp(rest)).astype(BF16)
        dq_scr[...] = jnp.zeros_like(dq_scr)
        dv_scr[...] = jnp.zeros_like(dv_scr)

        zero = jnp.zeros((HEAD, HEAD), F32)

        def bwd_chunk(i, carry, u):
            new = []
            for (d, rev, _, _), dSt in zip(dirs, carry):
                ci = _scan_chunk(i, ncc, nc, rev)
                rows = _rows(ci)
                q, v, do = q_ref[rows, :], v_ref[rows, :], do_scr[rows, :]
                c, k = c_scr[d, rows, :], k_scr[d, rows, :]
                tot = _chunk_total(c, rev)
                etot = jnp.exp(tot)
                St = st_ref[0, d, ci]
                dSb = dSt.astype(BF16)
                do_b = do.astype(BF16)
                dq_x = _dot(do_b, St) * jnp.exp(c)
                dk_x = _dot(v.astype(BF16), dSb) * jnp.exp(tot - c)
                dv_x = _dot_nt(ke_scr[d, rows, :], dSb)
                dtot = (jnp.sum(St.astype(F32) * dSt, axis=0, keepdims=True) * etot
                        + jnp.sum(k * dk_x, axis=0, keepdims=True))
                dq = jnp.zeros((CHUNK, HEAD), F32)
                for s in range(CHUNK):
                    E = _pair_decay(c, s, rev)
                    XE = E * k[s:s + 1, :]
                    a = jnp.sum(q * XE, axis=1, keepdims=True)
                    da = jnp.sum(do * v[s:s + 1, :], axis=1, keepdims=True)
                    dq = dq + da * XE
                    row_scr[u, d, 0, s:s + 1, :] = jnp.sum(da * q * E, axis=0, keepdims=True)
                    row_scr[u, d, 1, s:s + 1, :] = jnp.sum(a * do, axis=0, keepdims=True)
                dq, dk, dv = dq + dq_x, row_scr[u, d, 0] + dk_x, row_scr[u, d, 1] + dv_x
                dg_scr[d, rows, :] = _chunk_cumsum(q * dq - k * dk, not rev) + dtot
                dk_scr[d, rows, :] = dk
                dq_scr[rows, :] += dq
                dv_scr[rows, :] += dv
                new.append(dSt * etot + _dot_tn(do_b, qe_scr[d, rows, :]))
            return tuple(new)

        def bwd_step(i2, carry):
            for u in range(2):
                carry = bwd_chunk(nc - 1 - (2 * i2 + u), carry, u)
            return carry

        lax.fori_loop(0, nc // 2, bwd_step, (zero, zero))

        for d, _, z_ref, dz_ref in dirs:
            lb = _lower_bound(lbl_ref, d)
            sg, f = _gate_terms(z_ref[...], lb)
            df = dg_scr[d] / f - dk_scr[d]
            dz_ref[...] = (df * (1.0 - lb) * sg * (1.0 - sg)).astype(BF16)
            dl0 = jnp.sum(df * (1.0 - sg), axis=0, keepdims=True) * lb * (1.0 - lb)
            dlbl_ref[d, 0:1, :] = dl0
            dlbl_ref[d, 1:2, :] = -dl0
        dq_ref[...] = dq_scr[...].astype(BF16)
        dv_ref[...] = dv_scr[...].astype(BF16)

    cb = HA // HEAD
    col = lambda kk: pl.BlockSpec((T, HEAD), lambda h: (0, kk * cb + h))
    tcol = pl.BlockSpec((T, HEAD), lambda h: (0, h))
    lcol = pl.BlockSpec((L, HEAD), lambda h: (0, h))
    outs = _pcall(
        body, name="hgrn_bwd", grid=(nh,),
        in_specs=[col(0), col(1), col(2), col(3), col(4),
                  pl.BlockSpec((2, 2, HEAD), lambda h: (0, 0, h)), pl.BlockSpec((1, HEAD), lambda h: (0, 0)),
                  lcol, lcol,
                  pl.BlockSpec((1, 2, nc, HEAD, HEAD), lambda h: (h, 0, 0, 0, 0), pipeline_mode=pl.Buffered(1))],
        out_specs=[tcol, tcol, tcol, tcol, tcol, pl.BlockSpec((2, 2, HEAD), lambda h: (0, 0, h)),
                   pl.BlockSpec((8, HEAD), lambda h: (0, 0))],
        out_shape=[jax.ShapeDtypeStruct((T, HA), BF16)] * 5 + [jax.ShapeDtypeStruct((2, 2, HA), F32),
                                                               jax.ShapeDtypeStruct((8, HEAD), F32)],
        scratch_shapes=[pltpu.VMEM((T, HEAD), F32),
                        pltpu.VMEM((2, T, HEAD), F32), pltpu.VMEM((2, T, HEAD), F32),
                        pltpu.VMEM((2, T, HEAD), BF16), pltpu.VMEM((2, T, HEAD), BF16),
                        pltpu.VMEM((2, T, HEAD), F32), pltpu.VMEM((2, T, HEAD), F32),
                        pltpu.VMEM((T, HEAD), F32), pltpu.VMEM((T, HEAD), F32),
                        pltpu.VMEM((2, 2, 2, CHUNK, HEAD), F32)],
        compiler_params=_params(("arbitrary",)),
    )(p, p, p, p, p, lbl, ng, o, dya, st)
    return outs


def _swap_halves(t, lane):
    q = HEAD // 4
    return jnp.where((lane % (2 * q)) < q, pltpu.roll(t, HEAD - q, 1), pltpu.roll(t, q, 1))


def _qk_norm(t, g):
    r = lax.rsqrt(jnp.mean(t * t, axis=-1, keepdims=True) + EPS)
    return t * r, r


def _rope(t, cos, sin, lane):
    return t * cos + _swap_halves(t, lane) * sin


def _qk_norm_bwd(dy, th, r, g):
    dth = dy * g
    return r * (dth - th * jnp.mean(dth * th, axis=-1, keepdims=True)), jnp.sum(dy * th, axis=0, keepdims=True)


def _rope_bwd(dy, cos, sin, lane):
    return dy * cos + _swap_halves(dy * sin, lane)


def _na_geometry(L):
    n_rows = L // GRID_W
    kr = min(WIN_R, n_rows)
    return n_rows, kr


def _na_prep(q_ref, k_ref, v_ref, gq_ref, gk_ref, cos_ref, sin_ref, qs, ks, vs, n_ctx, L):
    lane = lax.broadcasted_iota(jnp.int32, (L, HEAD), 1)
    cos, sin = cos_ref[...], sin_ref[...]
    qh, _ = _qk_norm(q_ref[pl.ds(n_ctx, L), :], None)
    qs[...] = _rope(qh * gq_ref[...], cos, sin, lane).astype(BF16)
    kh, _ = _qk_norm(k_ref[pl.ds(n_ctx, L), :], None)
    ks[pl.ds(n_ctx, L), :] = _rope(kh * gk_ref[...], cos, sin, lane).astype(BF16)
    kc, _ = _qk_norm(k_ref[pl.ds(0, n_ctx), :], None)
    ks[pl.ds(0, n_ctx), :] = (kc * gk_ref[...]).astype(BF16)
    vs[...] = v_ref[...].astype(BF16)


NA_RB = 4


def _na_band_rows(kr):
    return kr + NA_RB


def _na_scores(i, qs, ks, bias_ref, n_ctx, n_rows, kr):
    scale = HEAD ** -0.5
    kb = _na_band_rows(kr)
    rq = NA_RB * i
    r0 = jnp.clip(rq - WIN_R // 2, 0, n_rows - kb)
    qrows = pl.ds(pl.multiple_of(rq * GRID_W, NA_RB * GRID_W), NA_RB * GRID_W)
    krows = pl.ds(pl.multiple_of(n_ctx + r0 * GRID_W, GRID_W), kb * GRID_W)
    qv = qs[qrows, :]
    sb = _dot_nt(qv, ks[krows, :]) * scale
    band_row = lax.broadcasted_iota(jnp.int32, (GRID_W, kb * GRID_W), 1) // GRID_W
    parts, tiles = [], []
    for u in range(NA_RB):
        r_u = rq + u
        first = jnp.clip(r_u - WIN_R // 2, 0, n_rows - kr) - r0
        idx = [jnp.clip(r0 - r_u + (WIN_R - 1) + 2 * jj, 0, 2 * WIN_R - 1) for jj in range(kb // 2)]
        bias_u = jnp.concatenate([bias_ref[0, t] for t in idx], axis=1)
        inside = (band_row >= first) & (band_row < first + kr)
        parts.append(jnp.where(inside, sb[u * GRID_W:(u + 1) * GRID_W, :] + bias_u, NEG))
        tiles.append(idx)
    sb = jnp.concatenate(parts, axis=0)
    sc = _dot_nt(qv, ks[pl.ds(0, n_ctx), :]) * scale
    m = jnp.maximum(jnp.max(sb, axis=1, keepdims=True), jnp.max(sc, axis=1, keepdims=True))
    eb, ec = jnp.exp(sb - m), jnp.exp(sc - m)
    inv = 1.0 / (jnp.sum(eb, axis=1, keepdims=True) + jnp.sum(ec, axis=1, keepdims=True))
    return eb * inv, ec * inv, qrows, krows, tiles


def _na_fwd(p, bias, gq, gk, cos, sin, n_ctx, off, HB):
    T = p.shape[0]
    L = T - n_ctx
    nh = HB // HEAD
    n_rows, kr = _na_geometry(L)
    ob = off // HEAD

    def body(q_ref, k_ref, v_ref, bias_ref, gq_ref, gk_ref, cos_ref, sin_ref, y_ref, qs, ks, vs):
        _na_prep(q_ref, k_ref, v_ref, gq_ref, gk_ref, cos_ref, sin_ref, qs, ks, vs, n_ctx, L)

        def step(i, carry):
            pb, pc, qrows, krows, _ = _na_scores(i, qs, ks, bias_ref, n_ctx, n_rows, kr)
            y = _dot(pb.astype(BF16), vs[krows, :]) + _dot(pc.astype(BF16), vs[pl.ds(0, n_ctx), :])
            y_ref[qrows, :] = y.astype(BF16)
            return carry

        lax.fori_loop(0, n_rows // NA_RB, step, 0)

    col = lambda kk: pl.BlockSpec((T, HEAD), lambda h: (0, ob + kk * nh + h))
    vec = pl.BlockSpec((1, HEAD), lambda h: (0, 0))
    tab = pl.BlockSpec((L, HEAD), lambda h: (0, 0))
    return _pcall(
        body, name="na_fwd", grid=(nh,),
        in_specs=[col(0), col(1), col(2), pl.BlockSpec((1,) + bias.shape[1:], lambda h: (h, 0, 0, 0)),
                  vec, vec, tab, tab],
        out_specs=pl.BlockSpec((L, HEAD), lambda h: (0, h)),
        out_shape=jax.ShapeDtypeStruct((L, HB), BF16),
        scratch_shapes=[pltpu.VMEM((L, HEAD), BF16), pltpu.VMEM((T, HEAD), BF16), pltpu.VMEM((T, HEAD), BF16)],
        compiler_params=_params(("parallel",)),
    )(p, p, p, bias, gq, gk, cos, sin)


def _na_bwd(p, bias, gq, gk, cos, sin, dyb, n_ctx, off, HB):
    T = p.shape[0]
    L = T - n_ctx
    nh = HB // HEAD
    n_rows, kr = _na_geometry(L)
    ob = off // HEAD
    scale = HEAD ** -0.5

    def body(q_ref, k_ref, v_ref, bias_ref, gq_ref, gk_ref, cos_ref, sin_ref, dy_ref,
             dq_ref, dk_ref, dv_ref, dbias_ref, dg_ref, qs, ks, vs, dqa, dka, dva):
        h = pl.program_id(0)
        _na_prep(q_ref, k_ref, v_ref, gq_ref, gk_ref, cos_ref, sin_ref, qs, ks, vs, n_ctx, L)
        dka[...] = jnp.zeros_like(dka)
        dva[...] = jnp.zeros_like(dva)
        dbias_ref[...] = jnp.zeros_like(dbias_ref)

        crows = pl.ds(0, n_ctx)

        def step(i, carry):
            pb, pc, qrows, krows, tiles = _na_scores(i, qs, ks, bias_ref, n_ctx, n_rows, kr)
            do = dy_ref[qrows, :]
            qv = qs[qrows, :]
            dpb = _dot_nt(do, vs[krows, :])
            dpc = _dot_nt(do, vs[crows, :])
            delta = jnp.sum(pb * dpb, axis=1, keepdims=True) + jnp.sum(pc * dpc, axis=1, keepdims=True)
            dsb = pb * (dpb - delta)
            dsc = pc * (dpc - delta)
            dsb_b, dsc_b = dsb.astype(BF16), dsc.astype(BF16)
            dqa[qrows, :] = (_dot(dsb_b, ks[krows, :]) + _dot(dsc_b, ks[crows, :])) * scale
            dka[krows, :] += _dot_tn(dsb_b, qv) * scale
            dka[crows, :] += _dot_tn(dsc_b, qv) * scale
            dva[krows, :] += _dot_tn(pb.astype(BF16), do)
            dva[crows, :] += _dot_tn(pc.astype(BF16), do)
            for u, idx in enumerate(tiles):
                for jj, t in enumerate(idx):
                    dbias_ref[0, t] += dsb[u * GRID_W:(u + 1) * GRID_W, jj * 2 * GRID_W:(jj + 1) * 2 * GRID_W]
            return carry

        lax.fori_loop(0, n_rows // NA_RB, step, 0)

        lane = lax.broadcasted_iota(jnp.int32, (L, HEAD), 1)
        cos, sin = cos_ref[...], sin_ref[...]
        lat, ctx = pl.ds(n_ctx, L), pl.ds(0, n_ctx)
        gqv, gkv = gq_ref[...], gk_ref[...]
        qh, rq = _qk_norm(q_ref[lat, :], None)
        dq, dgq = _qk_norm_bwd(_rope_bwd(dqa[...], cos, sin, lane), qh, rq, gqv)
        dq_ref[ctx, :] = jnp.zeros((n_ctx, HEAD), BF16)
        dq_ref[lat, :] = dq.astype(BF16)
        kh, rk = _qk_norm(k_ref[lat, :], None)
        dk, dgk = _qk_norm_bwd(_rope_bwd(dka[lat, :], cos, sin, lane), kh, rk, gkv)
        dk_ref[lat, :] = dk.astype(BF16)
        kch, rkc = _qk_norm(k_ref[ctx, :], None)
        dkc, dgkc = _qk_norm_bwd(dka[ctx, :], kch, rkc, gkv)
        dk_ref[ctx, :] = dkc.astype(BF16)
        dv_ref[...] = dva[...].astype(BF16)

        @pl.when(h == 0)
        def _():
            dg_ref[...] = jnp.zeros_like(dg_ref)

        dg_ref[0:1, :] += dgq
        dg_ref[1:2, :] += dgk + dgkc

    col = lambda kk: pl.BlockSpec((T, HEAD), lambda h: (0, ob + kk * nh + h))
    vec = pl.BlockSpec((1, HEAD), lambda h: (0, 0))
    tab = pl.BlockSpec((L, HEAD), lambda h: (0, 0))
    tcol = pl.BlockSpec((T, HEAD), lambda h: (0, h))
    bspec = pl.BlockSpec((1,) + bias.shape[1:], lambda h: (h, 0, 0, 0))
    return _pcall(
        body, name="na_bwd", grid=(nh,),
        in_specs=[col(0), col(1), col(2), bspec, vec, vec, tab, tab, pl.BlockSpec((L, HEAD), lambda h: (0, h))],
        out_specs=[tcol, tcol, tcol, bspec, pl.BlockSpec((8, HEAD), lambda h: (0, 0))],
        out_shape=[jax.ShapeDtypeStruct((T, HB), BF16)] * 3 + [jax.ShapeDtypeStruct(bias.shape, F32),
                                                               jax.ShapeDtypeStruct((8, HEAD), F32)],
        scratch_shapes=[pltpu.VMEM((L, HEAD), BF16), pltpu.VMEM((T, HEAD), BF16), pltpu.VMEM((T, HEAD), BF16),
                        pltpu.VMEM((L, HEAD), F32), pltpu.VMEM((T, HEAD), F32), pltpu.VMEM((T, HEAD), F32)],
        compiler_params=_params(("arbitrary",)),
    )(p, p, p, bias, gq, gk, cos, sin, dyb)


def _bias_tables():
    w = np.arange(GRID_W)
    col_start = np.clip(w - WIN_C // 2, 0, GRID_W - WIN_C)
    col_in = (w[None, :] >= col_start[:, None]) & (w[None, :] < col_start[:, None] + WIN_C)
    dc = np.clip(w[None, :] - w[:, None], -(WIN_C - 1), WIN_C - 1) + WIN_C - 1
    n_pair = 2 * WIN_R
    ridx = np.zeros((n_pair, GRID_W, 2 * GRID_W), np.int32)
    cidx = np.zeros((n_pair, GRID_W, 2 * GRID_W), np.int32)
    valid = np.zeros((n_pair, GRID_W, 2 * GRID_W), bool)
    for i in range(n_pair):
        for half in range(2):
            row = i + half
            sl = slice(half * GRID_W, (half + 1) * GRID_W)
            ridx[i, :, sl] = min(row, 2 * WIN_R - 2)
            cidx[i, :, sl] = dc
            valid[i, :, sl] = col_in & (row <= 2 * WIN_R - 2)
    return ridx, cidx, valid


def _bias_onehot():
    _, cidx, valid = _bias_tables()
    K = GRID_W * 2 * GRID_W
    oh = np.zeros((K, 128), np.float32)
    neg = np.full((1, K), NEG, np.float32)
    for cq in range(GRID_W):
        for ll in range(2 * GRID_W):
            if valid[0, cq, ll]:
                oh[cq * 2 * GRID_W + ll, (ll // GRID_W) * 64 + cidx[0, cq, ll]] = 1.0
                neg[0, cq * 2 * GRID_W + ll] = 0.0
    return oh, neg


def _expand_bias(table):
    H = table.shape[0]
    n_pair, n_dc = 2 * WIN_R, 2 * WIN_C - 1
    tp = jnp.pad(table, ((0, 0), (0, n_pair + 1 - table.shape[1]), (0, 64 - n_dc)))
    t2 = jnp.concatenate([tp[:, :n_pair], tp[:, 1:n_pair + 1]], axis=-1).reshape(H * n_pair, 128)
    oh, neg = _bias_onehot()

    def body(t_ref, oh_ref, neg_ref, o_ref):
        o_ref[...] = lax.dot_general(t_ref[...], oh_ref[...], (((1,), (1,)), ((), ())), precision=HI,
                                     preferred_element_type=F32) + neg_ref[...]

    out = _pcall(body, name="bias_expand", out_shape=jax.ShapeDtypeStruct((H * n_pair, oh.shape[0]), F32),
                         compiler_params=_params())(t2, jnp.asarray(oh), jnp.asarray(neg))
    return out.reshape(H, n_pair, GRID_W, 2 * GRID_W)


def _bias_grad(dbias):
    H = dbias.shape[0]
    n_pair, n_dc = 2 * WIN_R, 2 * WIN_C - 1
    K = GRID_W * 2 * GRID_W
    oh, _ = _bias_onehot()
    flat = dbias.reshape(H * n_pair, K)

    def body(d_ref, oh_ref, o_ref):
        o_ref[...] = jnp.dot(d_ref[...], oh_ref[...], precision=HI, preferred_element_type=F32)

    g = _pcall(body, name="bias_grad", out_shape=jax.ShapeDtypeStruct((H * n_pair, 128), F32),
                       compiler_params=_params())(flat, jnp.asarray(oh))
    g = g.reshape(H, n_pair, 128)
    left, right = g[:, :, :n_dc], g[:, :, 64:64 + n_dc]
    out = left[:, :n_pair - 1]
    return out.at[:, 1:].add(right[:, :n_pair - 2])


def _rope_tables(L):
    pos = np.arange(L)
    row = (pos // GRID_W).astype(np.float32)
    colp = (pos % GRID_W).astype(np.float32)
    half = HEAD // 2
    nf = half // 2
    inv = (ROPE_THETA ** (-np.arange(nf, dtype=np.float32) / nf)).astype(np.float32)

    def tabs(pv):
        ang = pv[:, None] * inv[None, :]
        c, s = np.cos(ang), np.sin(ang)
        return np.concatenate([c, c], axis=1), np.concatenate([-s, s], axis=1)

    cr, sr = tabs(row)
    cc, sc = tabs(colp)
    return (jnp.asarray(np.concatenate([cr, cc], axis=1), F32), jnp.asarray(np.concatenate([sr, sc], axis=1), F32))


def _adamw(w, g, m, v, name, after=None, copy_g=False):
    R, C = w.shape
    tr = _row_tile(R, C)
    c1 = 1.0 - ADAM_B1 ** ADAM_STEP
    c2 = 1.0 - ADAM_B2 ** ADAM_STEP
    deps = [] if after is None else [after]
    n_out = 4 if copy_g else 3

    def body(w_ref, g_ref, m_ref, v_ref, *rest):
        d_ref, mo_ref, vo_ref = rest[len(deps):len(deps) + 3]
        gv = g_ref[...]
        mn = ADAM_B1 * m_ref[...] + (1.0 - ADAM_B1) * gv
        vn = ADAM_B2 * v_ref[...] + (1.0 - ADAM_B2) * (gv * gv)
        mo_ref[...] = mn
        vo_ref[...] = vn
        d_ref[...] = -ADAM_LR * ((mn / c1) / (jnp.sqrt(vn / c2) + ADAM_EPS) + ADAM_WD * w_ref[...])
        if copy_g:
            rest[-1][...] = gv

    blk = pl.BlockSpec((tr, C), lambda i: (i, 0))
    return _pcall(
        body, name=name, grid=(R // tr,),
        in_specs=[blk] * 4 + [_ANY] * len(deps), out_specs=[blk] * n_out,
        out_shape=[jax.ShapeDtypeStruct((R, C), F32)] * n_out,
        compiler_params=_params(("parallel",)),
    )(w, g, m, v, *deps)


PACK_W = 1024


def _pack(parts):
    flat, offs, pos = [], [], 0
    for a in parts:
        n = a.size
        padn = -n % PACK_W
        flat.append(jnp.pad(a.reshape(-1).astype(F32), (0, padn)))
        offs.append((pos, n, a.shape))
        pos += n + padn
    tail = -pos % (8 * PACK_W)
    if tail:
        flat.append(jnp.zeros((tail,), F32))
    return jnp.concatenate(flat).reshape(-1, PACK_W), offs


def _unpack(buf, offs, i):
    pos, n, shape = offs[i]
    return buf.reshape(buf.shape[:-2] + (-1,))[..., pos:pos + n].reshape(buf.shape[:-2] + shape)


def kernel(x, c, ctx, c_ctx, ada_w, ada_b, norm1_g, norm2_g, w_in, hgrn_lb_logits, hgrn_norm_g, na_q_norm_g, na_k_norm_g, na_rel_bias, w_branch_a, w_branch_b, w_out, ffn_w1, ffn_w3, ffn_conv_w, ffn_conv_b, ffn_w2, loss_target, m_c_ctx, m_ada_w, m_ada_b, m_norm1_g, m_norm2_g, m_w_in, m_hgrn_lb_logits, m_hgrn_norm_g, m_na_q_norm_g, m_na_k_norm_g, m_na_rel_bias, m_w_branch_a, m_w_branch_b, m_w_out, m_ffn_w1, m_ffn_w3, m_ffn_conv_w, m_ffn_conv_b, m_ffn_w2, v_c_ctx, v_ada_w, v_ada_b, v_norm1_g, v_norm2_g, v_w_in, v_hgrn_lb_logits, v_hgrn_norm_g, v_na_q_norm_g, v_na_k_norm_g, v_na_rel_bias, v_w_branch_a, v_w_branch_b, v_w_out, v_ffn_w1, v_ffn_w3, v_ffn_conv_w, v_ffn_conv_b, v_ffn_w2):
    weights = dict(c_ctx=c_ctx, ada_w=ada_w, ada_b=ada_b, norm1_g=norm1_g, norm2_g=norm2_g, w_in=w_in,
                   hgrn_lb_logits=hgrn_lb_logits, hgrn_norm_g=hgrn_norm_g, na_q_norm_g=na_q_norm_g,
                   na_k_norm_g=na_k_norm_g, na_rel_bias=na_rel_bias, w_branch_a=w_branch_a, w_branch_b=w_branch_b,
                   w_out=w_out, ffn_w1=ffn_w1, ffn_w3=ffn_w3, ffn_conv_w=ffn_conv_w, ffn_conv_b=ffn_conv_b,
                   ffn_w2=ffn_w2)
    moms = dict(c_ctx=(m_c_ctx, v_c_ctx), ada_w=(m_ada_w, v_ada_w), ada_b=(m_ada_b, v_ada_b),
                norm1_g=(m_norm1_g, v_norm1_g), norm2_g=(m_norm2_g, v_norm2_g), w_in=(m_w_in, v_w_in),
                hgrn_lb_logits=(m_hgrn_lb_logits, v_hgrn_lb_logits), hgrn_norm_g=(m_hgrn_norm_g, v_hgrn_norm_g),
                na_q_norm_g=(m_na_q_norm_g, v_na_q_norm_g), na_k_norm_g=(m_na_k_norm_g, v_na_k_norm_g),
                na_rel_bias=(m_na_rel_bias, v_na_rel_bias), w_branch_a=(m_w_branch_a, v_w_branch_a),
                w_branch_b=(m_w_branch_b, v_w_branch_b), w_out=(m_w_out, v_w_out), ffn_w1=(m_ffn_w1, v_ffn_w1),
                ffn_w3=(m_ffn_w3, v_ffn_w3), ffn_conv_w=(m_ffn_conv_w, v_ffn_conv_w),
                ffn_conv_b=(m_ffn_conv_b, v_ffn_conv_b), ffn_w2=(m_ffn_w2, v_ffn_w2))
    order = list(weights)

    L, D = x.shape[1], x.shape[2]
    N = ctx.shape[1]
    T = N + L
    HA = w_branch_a.shape[1]
    HB = w_branch_b.shape[1]
    F = ffn_conv_b.shape[1]
    IN = 5 * HA + 3 * HB + 2 * D
    n_ada = ada_w.shape[2]
    ix, iy, ic = _pos()
    chip = 2 * ix + iy
    dev = 2 * chip + ic

    _PENDING.clear()
    pk0, offs0 = _pack([c[0], hgrn_lb_logits, ffn_conv_w[0]])
    g0 = _allgather8(pk0, "gather_small0")
    c_all = _unpack(g0, offs0, 0)
    lbl_parts = _unpack(g0, offs0, 1)
    lbl = jnp.concatenate([lbl_parts[2 * j] for j in range(N_CHIP)], axis=-1)
    cw_parts = _unpack(g0, offs0, 2)
    cw = jnp.concatenate([cw_parts[2 * j] for j in range(N_CHIP)], axis=-1)
    cw8 = jnp.pad(cw, ((0, 5), (0, 0)))

    cs = jnp.concatenate([c_all, c_ctx[None, :], jnp.zeros((7, D), F32)], axis=0)
    ada_b_mine = lax.dynamic_slice(ada_b, (0, chip * n_ada), (1, n_ada))
    mod_mine = _ada_fwd(cs, ada_w[0], ada_b_mine)
    gm = _allgather8(mod_mine, "gather_mod")
    mod = jnp.concatenate([gm[2 * j] for j in range(N_CHIP)], axis=-1)
    mod_l = lax.dynamic_slice(mod, (dev, 0), (1, N_MOD * D)).reshape(N_MOD, D)
    mod_c = mod[8].reshape(N_MOD, D)
    sh1, sc1, g1, sh2, sc2, g2 = [mod_l[i:i + 1] for i in range(N_MOD)]
    shift1 = jnp.concatenate([mod_c[0:1], sh1], axis=0)
    scale1 = jnp.concatenate([mod_c[1:2], sc1], axis=0)

    shards = [w_in[0], w_branch_a[0], w_branch_b[0], w_out[0], ffn_w1[0], ffn_w3[0], ffn_w2[0]]
    names = ["w_in", "w_a", "w_b", "w_out", "w1", "w3", "w2"]
    slots = [_cast_bf16_slot(s, "cast_" + nm, second_copy=(nm == "w_in")) for s, nm in zip(shards, names)]
    win_slots, win_own = slots[0]
    sem_nb, win_buf = _xfer_start("gather_ici_start_in_nbr", [win_slots], _plan_gather_ici(NEIGHBOURS), 2, gm)
    sem_dg, diag_buf = _xfer_start("gather_ici_start_in_diag", [win_own, _empty_hbm(win_own.shape, BF16)],
                                   _plan_diag_ici, 1)
    gat_mix = _gather_start("mix", slots[1:4])
    gat_ffn = _gather_start("ffn", slots[4:7])

    xall = jnp.concatenate([ctx[0], x[0]], axis=0)
    h_all = _rms1_fwd(xall, norm1_g, shift1, scale1, N)
    chip_i = chip.astype(jnp.int32)
    same = lambda ids: jnp.stack([jnp.stack(ids), jnp.stack(ids)])
    p = _mm_nn_sel(h_all, win_buf[0], same([chip_i]), F32, "mm_p_own")
    win_buf = _d2d_hand_over("in_nbr", sem_nb, win_buf, NEIGHBOURS, p)
    p = _mm_nn_sel(h_all, win_buf[0], same([chip_i ^ 1, chip_i ^ 2]), F32, "mm_p_nbr", p)
    _, diag_land = _xfer_wait("gather_ici_wait_in_diag", sem_dg, diag_buf, _plan_diag_ici, p)
    sem_dg, diag_land = _xfer_start("gather_d2d_start_in_diag", [diag_land], _plan_diag_d2d, 1)
    (diag_land,) = _xfer_wait("gather_d2d_wait_in_diag", sem_dg, diag_land, _plan_diag_d2d, p)
    sel_diag = jnp.stack([jnp.zeros((1,), jnp.int32), (chip_i ^ 3).reshape(1)])
    p = _mm_nn_sel(h_all, diag_land[None], sel_diag, F32, "mm_p_diag", p, n_shards=N_CHIP)
    Win = _fill_slot(win_buf[0], diag_land, chip_i ^ 3)
    gat_mix = _gather_mid(gat_mix, p)
    y_a, o_a, st_a = _hgrn_fwd(p, lbl, hgrn_norm_g, N, HA)
    Wa, Wb, Wo = _gather_finish(gat_mix, y_a)
    Wo = Wo.reshape(1, D, D)
    bias = _expand_bias(na_rel_bias[0])
    cos, sin = _rope_tables(L)
    off_na = 5 * HA
    y_b = _na_fwd(p, bias, na_q_norm_g, na_k_norm_g, cos, sin, N, off_na, HB)
    gat_ffn = _gather_mid(gat_ffn, (y_a, y_b))
    za = _mm_nn(y_a, Wa, BF16, "mm_za")
    zb = _mm_nn(y_b, Wb, BF16, "mm_zb")
    off_ga, off_gb = 5 * HA + 3 * HB, 5 * HA + 3 * HB + D
    z = _merge_fwd(za, zb, p, N, off_ga, off_gb)
    mo = _mm_nn(z, Wo, F32, "mm_mo")
    vec2 = jnp.concatenate([g1, norm2_g, sh2, sc2, jnp.zeros((4, D), F32)], axis=0)
    x_mid, h2 = _resid_rms2_fwd(x[0], mo, vec2)
    W1, W3, W2 = _gather_finish(gat_ffn, h2)
    W2 = W2.reshape(1, F, D)
    u1 = _mm_nn(h2, W1, BF16, "mm_u1")
    u3 = _mm_nn(h2, W3, BF16, "mm_u3")
    a = _convgate_fwd(u1, u3, cw8, ffn_conv_b)
    f = _mm_nn(a, W2, F32, "mm_f")
    dy, df, s_loss = _loss_head(x_mid, f, g2, loss_target[0])
    loss = lax.psum(s_loss[1, 0], ("x", "y", "c"))
    d_g2 = s_loss[0:1]

    gW2 = _mm_tn(a, df, 1, "mm_gw2").reshape(N_CHIP, F // N_CHIP, D)
    da = _mm_nt(df, W2, BF16, "mm_da")
    du1, du3, s_conv = _convgate_bwd(u1, u3, da, cw8, ffn_conv_b)
    gW1 = _mm_tn(h2, du1, N_CHIP, "mm_gw1")
    gW3 = _mm_tn(h2, du3, N_CHIP, "mm_gw3")
    rs_ffn = _rs_start("ffn", [gW2, gW1, gW3])
    dh2a = _mm_nt(du1, W1, F32, "mm_dh2a")
    dh2b = _mm_nt(du3, W3, F32, "mm_dh2b")
    rs_ffn = _rs_scatter(rs_ffn, dh2b)
    dxm, dmo, s_rms2 = _resid_rms2_bwd(x_mid, dh2a, dh2b, dy, mo, vec2)
    gWo = _mm_tn(z, dmo, 1, "mm_gwo").reshape(N_CHIP, D // N_CHIP, D)
    dz = _mm_nt(dmo, Wo, BF16, "mm_dz")
    dza, dzb, dga, dgb = _merge_bwd(dz, za, zb, p, N, off_ga, off_gb)
    gWa = _mm_tn(y_a, dza, N_CHIP, "mm_gwa")
    gWb = _mm_tn(y_b, dzb, N_CHIP, "mm_gwb")
    rs_mix = _rs_start("mix", [gWo, gWa, gWb])
    dya = _mm_nt(dza, Wa, F32, "mm_dya")
    dyb = _mm_nt(dzb, Wb, BF16, "mm_dyb")
    rs_mix = _rs_scatter(rs_mix, dyb)
    dq_a, dzf, dzbk, di_a, dog, dlbl, s_ng = _hgrn_bwd(p, lbl, hgrn_norm_g, o_a, dya, st_a, N, HA)
    rs_ffn = _rs_join(rs_ffn, dq_a)
    dq_n, dk_n, dv_n, dbias, s_qk = _na_bwd(p, bias, na_q_norm_g, na_k_norm_g, cos, sin, dyb, N, off_na, HB)
    rs_mix = _rs_join(rs_mix, dq_n)
    dp = jnp.concatenate([dq_a, dzf, dzbk, di_a, dog, dq_n, dk_n, dv_n, dga, dgb], axis=1)
    gWin = _mm_tn(h_all, dp, N_CHIP, "mm_gwin")
    rs_in = _rs_start("in", [gWin])
    rs_in = _rs_scatter(rs_in, _PENDING[0])
    dh = _mm_nt(dp, Win, F32, "mm_dh")
    grad_x, s_rms1 = _rms1_bwd(xall, dh, dxm, norm1_g, scale1, N)
    d_table = _bias_grad(dbias)

    grads = {}
    big_names = ["ada_w", "w_in", "w_branch_a", "w_branch_b", "w_out", "ffn_w1", "ffn_w3", "ffn_w2"]
    small_names = [n for n in order if n not in big_names]
    delta, new_m, new_v = {}, {}, {}

    def update(nm, after=None):
        reduced = nm != "ada_w"
        d_, m_, v_, *g_ = _adamw(weights[nm][0], grads[nm][0], moms[nm][0][0], moms[nm][1][0], "adamw_" + nm,
                                 after, copy_g=reduced)
        delta[nm], new_m[nm], new_v[nm] = d_[None], m_[None], v_[None]
        if reduced:
            grads[nm] = g_[0][None]
        return d_

    last = grad_x
    for nm, g in zip(["ffn_w2", "ffn_w1", "ffn_w3"], _rs_finish(rs_ffn, last)):
        grads[nm] = g[None]
        last = update(nm, last)
    for nm, g in zip(["w_out", "w_branch_a", "w_branch_b"], _rs_finish(rs_mix, last)):
        grads[nm] = g[None]
        last = update(nm, last)
    rs_in = _rs_join(rs_in, last)

    zD = jnp.zeros((1, D), F32)
    dmod_l = jnp.concatenate([s_rms1[2:3], s_rms1[3:4], s_rms2[3:4], s_rms2[0:1], s_rms2[1:2], d_g2], axis=0)
    dmod_c = jnp.concatenate([s_rms1[0:1], s_rms1[1:2], zD, zD, zD, zD], axis=0)
    pk1, offs1 = _pack([dmod_l, dmod_c, s_rms1[4], s_rms2[2], dlbl, s_ng[0], s_qk[0], s_qk[1], d_table,
                        s_conv[0:3], s_conv[3]])
    g1all = _allgather8(pk1, "gather_small1")
    tot1 = _sum8(g1all, "sum_small1")
    dmod_rows = _unpack(g1all, offs1, 0).reshape(N_DEV, N_MOD * D)
    dmod_c_tot = _unpack(tot1, offs1, 1).reshape(1, N_MOD * D)
    dmod16 = jnp.concatenate([dmod_rows, dmod_c_tot, jnp.zeros((7, N_MOD * D), F32)], axis=0)
    dmod16_mine = lax.dynamic_slice(dmod16, (0, chip * n_ada), (16, n_ada))
    g_ada_w, dact = _ada_bwd(cs, ada_w[0], dmod16_mine)
    pk2, offs2 = _pack([dact[8]])
    g2all = _allgather8(pk2, "gather_small2")
    dact_rows = _unpack(g2all, offs2, 0)
    dact_sel = jnp.concatenate([dact_rows[2 * j][None] for j in range(N_CHIP)] + [jnp.zeros((4, D), F32)], axis=0)

    grads["ada_w"] = g_ada_w[None]
    grads["ada_b"] =(_unpack(tot1, offs1, 0) + _unpack(tot1, offs1, 1)).reshape(1, N_MOD * D)
    grads["norm1_g"] = _unpack(tot1, offs1, 2)[None]
    grads["norm2_g"] = _unpack(tot1, offs1, 3)[None]
    g_lbl = _unpack(tot1, offs1, 4)
    n_lb = HA // N_CHIP
    grads["hgrn_lb_logits"] = lax.dynamic_slice(g_lbl, (0, 0, chip * n_lb), (2, 2, n_lb))
    grads["hgrn_norm_g"] = _unpack(tot1, offs1, 5)[None]
    grads["na_q_norm_g"] = _unpack(tot1, offs1, 6)[None]
    grads["na_k_norm_g"] = _unpack(tot1, offs1, 7)[None]
    grads["na_rel_bias"] = _unpack(tot1, offs1, 8)[None]
    g_cw = _unpack(tot1, offs1, 9)
    n_f = F // N_CHIP
    grads["ffn_conv_w"] = lax.dynamic_slice(g_cw, (0, chip * n_f), (3, n_f))[None]
    grads["ffn_conv_b"] = _unpack(tot1, offs1, 10)[None]

    g_c_ctx = _dsilu_rows(dact_sel, c_ctx[None, :], "grad_c_ctx")
    grads["c_ctx"] = g_c_ctx[0]

    last = update("ada_w", g_c_ctx)
    pw, offw = _pack([weights[n] for n in small_names])
    pg, _ = _pack([grads[n] for n in small_names])
    pm, _ = _pack([moms[n][0] for n in small_names])
    pv, _ = _pack([moms[n][1] for n in small_names])
    d_, m_, v_ = _adamw(pw, pg, pm, pv, "adamw_small", last)
    for i, nm in enumerate(small_names):
        delta[nm], new_m[nm], new_v[nm] = _unpack(d_, offw, i), _unpack(m_, offw, i), _unpack(v_, offw, i)
    grads["w_in"] = _rs_finish(rs_in, d_)[0][None]
    update("w_in")

    return (loss, grad_x[None], *[grads[n] for n in order], *[delta[n] for n in order],
            *[new_m[n] for n in order], *[new_v[n] for n in order])


def _dsilu_rows(v, cv, name):
    D = v.shape[1]

    def body(v_ref, c_ref, o_ref):
        t = c_ref[...]
        s = _sigmoid(t)
        o_ref[...] = (((v_ref[0:1, :] + v_ref[1:2, :]) + v_ref[2:3, :]) + v_ref[3:4, :]) * (s * (1.0 + t * (1.0 - s)))

    return _pcall(body, name=name, out_shape=jax.ShapeDtypeStruct((1, D), F32),
                          compiler_params=_params())(v, cv)
```

```python
import functools

import numpy as np
import jax
import jax.numpy as jnp
from jax import lax
from jax.experimental import pallas as pl
from jax.experimental.pallas import tpu as pltpu

F32 = jnp.float32
BF16 = jnp.bfloat16
MESH = pl.DeviceIdType.MESH

HEAD = 128
GRID_W = 64
WIN_R = 8
WIN_C = 16
ROPE_THETA = 10000.0
EPS = 1e-6
N_MOD = 6
CHUNK = 16
HGRN_UNROLL = 4
ADAM_LR = 0.001
ADAM_B1 = 0.9
ADAM_B2 = 0.999
ADAM_EPS = 1e-08
ADAM_WD = 0.01
ADAM_STEP = 10
NEG = -1e30
VMEM_LIMIT = 56 * 1024 * 1024
N_DEV = 8
N_CHIP = 4
HI = lax.Precision.HIGHEST


def _pick(n, cands):
    for c in cands:
        if n % c == 0:
            return c
    return n


def _row_tile(rows, cols, target_bytes=1 << 20):
    want = max(16, target_bytes // (4 * cols))
    for t in (512, 256, 128, 64, 32, 16, 8):
        if t <= want and rows % t == 0:
            return t
    return rows


def _params(sem=None):
    return pltpu.CompilerParams(dimension_semantics=sem, vmem_limit_bytes=VMEM_LIMIT)


def _dot(a, b):
    return jnp.dot(a, b, preferred_element_type=F32)


def _dot_nt(a, b):
    return lax.dot_general(a, b, (((1,), (1,)), ((), ())), preferred_element_type=F32)


def _dot_tn(a, b):
    return lax.dot_general(a, b, (((0,), (0,)), ((), ())), preferred_element_type=F32)


def _sigmoid(x):
    return 1.0 / (1.0 + jnp.exp(-x))


def _col_tile(n):
    return n if n <= 1536 else _pick(n, (1024, 768, 512, 384, 256, 128))


def _mm_nn(x, w3, out_dtype, name):
    M, K = x.shape
    S, _, n = w3.shape
    tm = _pick(M, (768, 512, 256, 128, 64))
    tn = _col_tile(n)
    nb = n // tn

    def body(x_ref, w_ref, o_ref):
        o_ref[...] = _dot(x_ref[...].astype(BF16), w_ref[0]).astype(o_ref.dtype)

    return _pcall(
        body, name=name, grid=(M // tm, S * nb),
        in_specs=[pl.BlockSpec((tm, K), lambda i, j: (i, 0)),
                  pl.BlockSpec((1, K, tn), lambda i, j: (j // nb, 0, j % nb))],
        out_specs=pl.BlockSpec((tm, tn), lambda i, j: (i, j)),
        out_shape=jax.ShapeDtypeStruct((M, S * n), out_dtype),
        compiler_params=_params(("parallel", "parallel")),
    )(x, w3)


def _mm_nn_sel(x, w3, sel, out_dtype, name, prev=None):
    M, K = x.shape
    S, _, n = w3.shape
    tm = _pick(M, (768, 512, 256, 128, 64))
    tn = _col_tile(n)
    nb = n // tn
    k = sel.shape[1]

    def body(sel_ref, x_ref, w_ref, *rest):
        rest[-1][...] = _dot(x_ref[...].astype(BF16), w_ref[0]).astype(out_dtype)

    in_specs = [pl.BlockSpec((tm, K), lambda i, j, sel_ref: (i, 0)),
                pl.BlockSpec((1, K, tn), lambda i, j, sel_ref: (sel_ref[0, j // nb], 0, j % nb))]
    operands = [sel, x, w3]
    if prev is not None:
        in_specs.append(_ANY)
        operands.append(prev)
    return pl.pallas_call(
        body, name=name,
        grid_spec=pltpu.PrefetchScalarGridSpec(
            num_scalar_prefetch=1, grid=(M // tm, k * nb), in_specs=in_specs,
            out_specs=pl.BlockSpec((tm, tn), lambda i, j, sel_ref: (i, sel_ref[1, j // nb] * nb + j % nb))),
        out_shape=jax.ShapeDtypeStruct((M, S * n), out_dtype),
        input_output_aliases={} if prev is None else {3: 0},
        compiler_params=_params(("parallel", "parallel")),
    )(*operands)


def _mm_nt(dy, w3, out_dtype, name):
    M = dy.shape[0]
    S, K, n = w3.shape
    tm = _pick(M, (768, 512, 256, 128, 64))
    tk = K if K <= 2048 else _pick(K, (1408, 1024, 512, 256, 128))
    tc = n if n <= 2048 else _col_tile(n)
    nb = n // tc
    nsteps = S * nb

    def body(dy_ref, w_ref, o_ref, acc_ref):
        s = pl.program_id(2)

        @pl.when(s == 0)
        def _():
            acc_ref[...] = jnp.zeros_like(acc_ref)

        acc_ref[...] += _dot_nt(dy_ref[...].astype(BF16), w_ref[0])

        @pl.when(s == nsteps - 1)
        def _():
            o_ref[...] = acc_ref[...].astype(o_ref.dtype)

    return _pcall(
        body, name=name, grid=(M // tm, K // tk, nsteps),
        in_specs=[pl.BlockSpec((tm, tc), lambda i, k, s: (i, s)),
                  pl.BlockSpec((1, tk, tc), lambda i, k, s: (s // nb, k, s % nb))],
        out_specs=pl.BlockSpec((tm, tk), lambda i, k, s: (i, k)),
        out_shape=jax.ShapeDtypeStruct((M, K), out_dtype),
        scratch_shapes=[pltpu.VMEM((tm, tk), F32)],
        compiler_params=_params(("parallel", "parallel", "arbitrary")),
    )(dy, w3)


def _mm_tn(x, dy, S, name):
    M, K = x.shape
    n = dy.shape[1] // S
    tk = _pick(K, (512, 256, 128))
    tn = _col_tile(n)
    nb = n // tn

    def body(x_ref, dy_ref, o_ref):
        o_ref[0] = _dot_tn(x_ref[...].astype(BF16), dy_ref[...].astype(BF16)).astype(BF16)

    return _pcall(
        body, name=name, grid=(S * nb, K // tk),
        in_specs=[pl.BlockSpec((M, tk), lambda j, k: (0, k)),
                  pl.BlockSpec((M, tn), lambda j, k: (0, j))],
        out_specs=pl.BlockSpec((1, tk, tn), lambda j, k: (j // nb, k, j % nb)),
        out_shape=jax.ShapeDtypeStruct((S, K, n), BF16),
        compiler_params=_params(("parallel", "parallel")),
    )(x, dy)


def _chip_index():
    return (2 * lax.axis_index("x") + lax.axis_index("y")).astype(jnp.int32).reshape(1)


def _cast_bf16_slot(w, name):
    R, C = w.shape
    tr = _row_tile(R, C, 2 << 20)

    def body(j_ref, w_ref, o_ref):
        o_ref[0] = w_ref[...].astype(BF16)

    return _pcall(
        body, name=name,
        grid_spec=pltpu.PrefetchScalarGridSpec(
            num_scalar_prefetch=1, grid=(R // tr,),
            in_specs=[pl.BlockSpec((tr, C), lambda i, j_ref: (i, 0))],
            out_specs=pl.BlockSpec((1, tr, C), lambda i, j_ref: (j_ref[0], i, 0))),
        out_shape=jax.ShapeDtypeStruct((N_CHIP, R, C), BF16),
        compiler_params=_params(("parallel",)),
    )(_chip_index(), w)


def _pos():
    return lax.axis_index("x"), lax.axis_index("y"), lax.axis_index("c")


def _other_chips(x, y):
    return [(x, 1 - y), (1 - x, y), (1 - x, 1 - y)]


def _allgather8(v, name):
    R, C = v.shape

    def body(x_ref, out_ref, send_sems, recv_sems, local_sem):
        x, y, c = _pos()
        me, sibling = (x, y, c), (x, y, 1 - c)
        chips = _other_chips(x, y)

        def slot(px, py, pc):
            return out_ref.at[4 * px + 2 * py + pc]

        def copy(k, block, to, src=None):
            return pltpu.make_async_remote_copy(
                src_ref=slot(*block) if src is None else src, dst_ref=slot(*block),
                send_sem=send_sems.at[k], recv_sem=recv_sems.at[k], device_id=to, device_id_type=MESH)

        mine = pltpu.make_async_copy(x_ref, slot(*me), local_sem)
        mine.start()
        first = [copy(0, me, sibling, src=x_ref)]
        first += [copy(1 + j, me, (*chip, c), src=x_ref) for j, chip in enumerate(chips)]
        for cp in first:
            cp.start()
        passed = [copy(4 + j, (*chip, c), sibling) for j, chip in enumerate(chips)]
        for j, chip in enumerate(chips):
            copy(1 + j, (*chip, c), me).wait_recv()
            passed[j].start()
        copy(0, sibling, me).wait_recv()
        for j, chip in enumerate(chips):
            copy(4 + j, (*chip, 1 - c), me).wait_recv()
        for cp in first + passed:
            cp.wait_send()
        mine.wait()

    return _pcall(
        body, name=name,
        out_shape=jax.ShapeDtypeStruct((N_DEV, R, C), v.dtype),
        in_specs=[pl.BlockSpec(memory_space=pltpu.VMEM)],
        out_specs=pl.BlockSpec(memory_space=pltpu.VMEM),
        scratch_shapes=[pltpu.SemaphoreType.DMA((7,)), pltpu.SemaphoreType.DMA((7,)), pltpu.SemaphoreType.DMA],
        compiler_params=pltpu.CompilerParams(vmem_limit_bytes=VMEM_LIMIT),
    )(v)


_HBM = pl.BlockSpec(memory_space=pltpu.HBM)
_SEM = pl.BlockSpec(memory_space=pltpu.SEMAPHORE)
_ANY = pl.BlockSpec(memory_space=pl.ANY)
_EFFECT = pltpu.SideEffectType.DATAFLOW_SIDE_EFFECTING
_PENDING = []


def _pcall(body, **kw):
    def run(*operands):
        if not _PENDING or "in_specs" not in kw:
            return pl.pallas_call(body, **kw)(*operands)
        deps = list(_PENDING)
        n = len(operands)

        def tied(*refs):
            return body(*refs[:n], *refs[n + len(deps):])

        return pl.pallas_call(tied, **{**kw, "in_specs": list(kw["in_specs"]) + [_ANY] * len(deps)})(*operands, *deps)
    return run


def _copies(plan, refs, send_sems, recv_sems):
    return [pltpu.make_async_remote_copy(src_ref=src, dst_ref=dst, send_sem=send_sems.at[k], recv_sem=recv_sems.at[k],
                                         device_id=dev, device_id_type=MESH)
            for k, (src, dst, dev) in enumerate(plan(refs))]


def _xfer_start(name, bufs, plan, n_copies, after=None):
    n = len(bufs)
    deps = list(_PENDING) + ([after] if after is not None else [])
    nd = len(deps)

    def body(*refs):
        for cp in _copies(plan, refs[:n], refs[n + nd], refs[n + nd + 1]):
            cp.start()
        refs[-1][...] = jnp.zeros_like(refs[-1])

    outs = pl.pallas_call(
        body, name=name,
        out_shape=(pltpu.SemaphoreType.DMA((n_copies,)), pltpu.SemaphoreType.DMA((n_copies,)),
                   *[pltpu.HBM(b.shape, b.dtype) for b in bufs], jax.ShapeDtypeStruct((8, 128), F32)),
        in_specs=[_HBM] * n + [_ANY] * nd,
        out_specs=(_SEM, _SEM, *[_HBM] * n, pl.BlockSpec(memory_space=pltpu.VMEM)),
        input_output_aliases={t: 2 + t for t in range(n)},
        compiler_params=pltpu.CompilerParams(has_side_effects=_EFFECT),
    )(*[pltpu.with_memory_space_constraint(b, pltpu.HBM) for b in bufs], *deps)
    _PENDING[:] = [outs[-1]]
    return (outs[0], outs[1]), list(outs[2:2 + n])


def _xfer_wait(name, sems, bufs, plan, after):
    n = len(bufs)
    after = tuple(after) if isinstance(after, (tuple, list)) else (after,)

    def body(*refs):
        cps = _copies(plan, refs[:n], refs[n], refs[n + 1])
        for cp in cps:
            cp.wait_send()
        for cp in cps:
            cp.wait_recv()

    outs = pl.pallas_call(
        body, name=name,
        out_shape=tuple(pltpu.HBM(b.shape, b.dtype) for b in bufs),
        in_specs=[_HBM] * n + [_SEM, _SEM] + [_ANY] * len(after),
        out_specs=tuple([_HBM] * n),
        input_output_aliases={t: t for t in range(n)},
        compiler_params=pltpu.CompilerParams(has_side_effects=_EFFECT),
    )(*bufs, sems[0], sems[1], *after)
    return list(outs)


def _half(ref_rows, hc):
    h = ref_rows // 2
    return pl.ds(hc * h, h)


ALL_CHIPS = (0, 1, 2)
NEIGHBOURS = (0, 1)
DIAGONAL = (2,)


def _plan_gather_ici(which):
    def plan(bufs):
        x, y, c = _pos()
        j = 2 * x + y
        chips = _other_chips(x, y)
        return [(b.at[j, _half(b.shape[1], c)], b.at[j, _half(b.shape[1], c)], (*chips[k], c))
                for b in bufs for k in which]
    return plan


def _plan_gather_d2d(which):
    def plan(bufs):
        x, y, c = _pos()
        chips = _other_chips(x, y)
        out = []
        for b in bufs:
            for k in which:
                blk = b.at[2 * chips[k][0] + chips[k][1], _half(b.shape[1], c)]
                out.append((blk, blk, (x, y, 1 - c)))
        return out
    return plan


def _plan_pair_swap(n):
    def plan(bufs):
        x, y, c = _pos()
        return [(g.at[:, _half(g.shape[1], 1 - c)], land, (x, y, 1 - c)) for g, land in zip(bufs[:n], bufs[n:])]
    return plan


def _plan_chip_scatter(n):
    def plan(bufs):
        x, y, c = _pos()
        return [(p.at[2 * chip[0] + chip[1]], land.at[k], (*chip, c))
                for p, land in zip(bufs[:n], bufs[n:]) for k, chip in enumerate(_other_chips(x, y))]
    return plan


def _plan_pair_join(bufs):
    x, y, c = _pos()
    return [(b.at[_half(b.shape[0], c)], b.at[_half(b.shape[0], c)], (x, y, 1 - c)) for b in bufs]


def _empty_hbm(shape, dtype):
    return pltpu.with_memory_space_constraint(lax.empty(shape, dtype), pltpu.HBM)


def _gather_start(tag, bufs, after=None):
    sems, bufs = _xfer_start(f"gather_ici_start_{tag}", bufs, _plan_gather_ici(ALL_CHIPS), 3 * len(bufs), after)
    return dict(tag=tag, sems=sems, bufs=bufs)


def _gather_mid(st, after):
    tag = st["tag"]
    bufs = _xfer_wait(f"gather_ici_wait_{tag}", st["sems"], st["bufs"], _plan_gather_ici(ALL_CHIPS), after)
    sems, bufs = _xfer_start(f"gather_d2d_start_{tag}", bufs, _plan_gather_d2d(ALL_CHIPS), 3 * len(bufs))
    return dict(tag=tag, sems=sems, bufs=bufs)


def _gather_finish(st, after):
    return _xfer_wait(f"gather_d2d_wait_{st['tag']}", st["sems"], st["bufs"], _plan_gather_d2d(ALL_CHIPS), after)


def _d2d_hand_over(tag, sems, bufs, which, after):
    bufs = _xfer_wait(f"gather_ici_wait_{tag}", sems, bufs, _plan_gather_ici(which), after)
    sems, bufs = _xfer_start(f"gather_d2d_start_{tag}", bufs, _plan_gather_d2d(which), len(which) * len(bufs))
    return _xfer_wait(f"gather_d2d_wait_{tag}", sems, bufs, _plan_gather_d2d(which), after)


def _pair_add(g, r, name):
    S, R, C = g.shape
    h = R // 2
    tr = _row_tile(h, C)
    nb = h // tr

    def body(c_ref, g_ref, r_ref, o_ref):
        o_ref[...] = (g_ref[...].astype(F32) + r_ref[...].astype(F32)).astype(BF16)

    return _pcall(
        body, name=name,
        grid_spec=pltpu.PrefetchScalarGridSpec(
            num_scalar_prefetch=1, grid=(S, nb),
            in_specs=[pl.BlockSpec((1, tr, C), lambda s, i, c_ref: (s, c_ref[0] * nb + i, 0)),
                      pl.BlockSpec((1, tr, C), lambda s, i, c_ref: (s, i, 0))],
            out_specs=pl.BlockSpec((1, tr, C), lambda s, i, c_ref: (s, i, 0))),
        out_shape=jax.ShapeDtypeStruct((S, h, C), BF16),
        compiler_params=_params(("parallel", "parallel")),
    )(lax.axis_index("c").astype(jnp.int32).reshape(1), g, r)


def _chip_sum(p, rb, name):
    S, h, C = p.shape
    tr = _row_tile(h, C)
    nb = h // tr
    jc = jnp.concatenate([_chip_index(), lax.axis_index("c").astype(jnp.int32).reshape(1)])

    def body(jc_ref, p_ref, r_ref, o_ref):
        o_ref[...] = ((p_ref[0].astype(F32) + r_ref[0].astype(F32)) + r_ref[1].astype(F32)) + r_ref[2].astype(F32)

    return _pcall(
        body, name=name,
        grid_spec=pltpu.PrefetchScalarGridSpec(
            num_scalar_prefetch=1, grid=(nb,),
            in_specs=[pl.BlockSpec((1, tr, C), lambda i, jc_ref: (jc_ref[0], i, 0)),
                      pl.BlockSpec((3, tr, C), lambda i, jc_ref: (0, i, 0))],
            out_specs=pl.BlockSpec((tr, C), lambda i, jc_ref: (jc_ref[1] * nb + i, 0))),
        out_shape=jax.ShapeDtypeStruct((2 * h, C), F32),
        compiler_params=_params(("parallel",)),
    )(jc, p, rb)


def _rs_start(tag, gs):
    n = len(gs)
    lands = [_empty_hbm((g.shape[0], g.shape[1] // 2, g.shape[2]), g.dtype) for g in gs]
    sems, bufs = _xfer_start(f"rs_swap_start_{tag}", list(gs) + lands, _plan_pair_swap(n), n)
    return dict(tag=tag, n=n, sems=sems, bufs=bufs)


def _rs_scatter(st, after):
    tag, n = st["tag"], st["n"]
    bufs = _xfer_wait(f"rs_swap_wait_{tag}", st["sems"], st["bufs"], _plan_pair_swap(n), after)
    ps = [_pair_add(g, r, f"rs_pair_add_{tag}{t}") for t, (g, r) in enumerate(zip(bufs[:n], bufs[n:]))]
    lands = [_empty_hbm((3,) + p.shape[1:], p.dtype) for p in ps]
    sems, bufs = _xfer_start(f"rs_scatter_start_{tag}", ps + lands, _plan_chip_scatter(n), 3 * n)
    return dict(tag=tag, n=n, sems=sems, bufs=bufs)


def _rs_join(st, after):
    tag, n = st["tag"], st["n"]
    bufs = _xfer_wait(f"rs_scatter_wait_{tag}", st["sems"], st["bufs"], _plan_chip_scatter(n), after)
    fs = [_chip_sum(p, rb, f"rs_chip_sum_{tag}{t}") for t, (p, rb) in enumerate(zip(bufs[:n], bufs[n:]))]
    sems, bufs = _xfer_start(f"rs_join_start_{tag}", fs, _plan_pair_join, n)
    return dict(tag=tag, n=n, sems=sems, bufs=bufs)


def _rs_finish(st, after):
    return _xfer_wait(f"rs_join_wait_{st['tag']}", st["sems"], st["bufs"], _plan_pair_join, after)


def _sum8(g, name):
    _, R, C = g.shape

    def body(g_ref, o_ref):
        acc = g_ref[0]
        for d in range(1, N_DEV):
            acc = acc + g_ref[d]
        o_ref[...] = acc

    return _pcall(body, name=name, out_shape=jax.ShapeDtypeStruct((R, C), F32),
                          compiler_params=_params())(g)


def _ada_fwd(cs, w, b):
    D, n = w.shape
    tn = _pick(n, (512, 384, 256, 128))

    def body(c_ref, w_ref, b_ref, o_ref):
        cv = c_ref[...]
        a = (cv * _sigmoid(cv)).astype(BF16)
        o_ref[...] = _dot(a, w_ref[...].astype(BF16)) + b_ref[...]

    return _pcall(
        body, name="ada_fwd", grid=(n // tn,),
        in_specs=[pl.BlockSpec((16, D), lambda j: (0, 0)), pl.BlockSpec((D, tn), lambda j: (0, j)),
                  pl.BlockSpec((1, tn), lambda j: (0, j))],
        out_specs=pl.BlockSpec((16, tn), lambda j: (0, j)),
        out_shape=jax.ShapeDtypeStruct((16, n), F32),
        compiler_params=_params(("parallel",)),
    )(cs, w, b)


def _ada_bwd(cs, w, dmod):
    D, n = w.shape
    tn = _pick(n, (512, 384, 256, 128))

    def body(c_ref, w_ref, d_ref, gw_ref, da_ref):
        j = pl.program_id(0)
        cv = c_ref[...]
        a = cv * _sigmoid(cv)
        d = d_ref[...]
        gw_ref[...] = lax.dot_general(a, d, (((0,), (0,)), ((), ())), precision=HI, preferred_element_type=F32)

        @pl.when(j == 0)
        def _():
            da_ref[...] = jnp.zeros_like(da_ref)

        da_ref[...] += _dot_nt(d.astype(BF16), w_ref[...].astype(BF16))

    return _pcall(
        body, name="ada_bwd", grid=(n // tn,),
        in_specs=[pl.BlockSpec((16, D), lambda j: (0, 0)), pl.BlockSpec((D, tn), lambda j: (0, j)),
                  pl.BlockSpec((16, tn), lambda j: (0, j))],
        out_specs=[pl.BlockSpec((D, tn), lambda j: (0, j)), pl.BlockSpec((16, D), lambda j: (0, 0))],
        out_shape=[jax.ShapeDtypeStruct((D, n), F32), jax.ShapeDtypeStruct((16, D), F32)],
        compiler_params=_params(("arbitrary",)),
    )(cs, w, dmod)


def _rms1_fwd(xall, gain, shift2, scale2, n_ctx):
    T, D = xall.shape
    tb = _pick(n_ctx, (256, 128, 64, 32, 16))
    nctx = n_ctx // tb

    def body(x_ref, g_ref, sh_ref, sc_ref, o_ref):
        i = pl.program_id(0)
        xv = x_ref[...]
        r = lax.rsqrt(jnp.mean(xv * xv, axis=-1, keepdims=True) + EPS)
        nrm = xv * r * g_ref[...]
        lat = i >= nctx
        sh = jnp.where(lat, sh_ref[1:2, :], sh_ref[0:1, :])
        sc = jnp.where(lat, sc_ref[1:2, :], sc_ref[0:1, :])
        o_ref[...] = (nrm * (1.0 + sc) + sh).astype(BF16)

    vec = lambda r: pl.BlockSpec((r, D), lambda i: (0, 0))
    return _pcall(
        body, name="rms1_fwd", grid=(T // tb,),
        in_specs=[pl.BlockSpec((tb, D), lambda i: (i, 0)), vec(1), vec(2), vec(2)],
        out_specs=pl.BlockSpec((tb, D), lambda i: (i, 0)),
        out_shape=jax.ShapeDtypeStruct((T, D), BF16),
        compiler_params=_params(("parallel",)),
    )(xall, gain, shift2, scale2)


def _rms1_bwd(xall, dh, dxmid, gain, scale2, n_ctx):
    T, D = xall.shape
    L = T - n_ctx
    tb = _pick(n_ctx, (256, 128, 64, 32, 16))
    nctx = n_ctx // tb

    def body(x_ref, dh_ref, dxm_ref, g_ref, sc_ref, dx_ref, cs_ref):
        i = pl.program_id(0)
        lat = i >= nctx
        xv = x_ref[...]
        r = lax.rsqrt(jnp.mean(xv * xv, axis=-1, keepdims=True) + EPS)
        xh = xv * r
        g = g_ref[...]
        nrm = xh * g
        sc = jnp.where(lat, sc_ref[1:2, :], sc_ref[0:1, :])
        dhv = dh_ref[...]
        dn = dhv * (1.0 + sc)
        dxh = dn * g
        dxv = r * (dxh - xh * jnp.mean(dxh * xh, axis=-1, keepdims=True))
        s_sh = jnp.sum(dhv, axis=0, keepdims=True)
        s_sc = jnp.sum(dhv * nrm, axis=0, keepdims=True)
        s_g = jnp.sum(dn * xh, axis=0, keepdims=True)
        zero = jnp.zeros_like(s_sh)
        rows = lax.broadcasted_iota(jnp.int32, (8, D), 0)
        upd = jnp.where(rows == 0, jnp.where(lat, zero, s_sh),
              jnp.where(rows == 1, jnp.where(lat, zero, s_sc),
              jnp.where(rows == 2, jnp.where(lat, s_sh, zero),
              jnp.where(rows == 3, jnp.where(lat, s_sc, zero),
              jnp.where(rows == 4, s_g, 0.0)))))

        @pl.when(i == 0)
        def _():
            cs_ref[...] = jnp.zeros_like(cs_ref)

        cs_ref[...] += upd

        @pl.when(lat)
        def _():
            dx_ref[...] = dxv + dxm_ref[...]

    lat_blk = lambda i: (jnp.maximum(i - nctx, 0), 0)
    vec = lambda r: pl.BlockSpec((r, D), lambda i: (0, 0))
    return _pcall(
        body, name="rms1_bwd", grid=(T // tb,),
        in_specs=[pl.BlockSpec((tb, D), lambda i: (i, 0)), pl.BlockSpec((tb, D), lambda i: (i, 0)),
                  pl.BlockSpec((tb, D), lat_blk), vec(1), vec(2)],
        out_specs=[pl.BlockSpec((tb, D), lat_blk), vec(8)],
        out_shape=[jax.ShapeDtypeStruct((L, D), F32), jax.ShapeDtypeStruct((8, D), F32)],
        compiler_params=_params(("arbitrary",)),
    )(xall, dh, dxmid, gain, scale2)


def _resid_rms2_fwd(x, mo, vecs):
    L, D = x.shape
    tb = _pick(L, (256, 128, 64))

    def body(x_ref, mo_ref, v_ref, xm_ref, h_ref):
        xm = x_ref[...] + v_ref[0:1, :] * mo_ref[...]
        xm_ref[...] = xm
        r = lax.rsqrt(jnp.mean(xm * xm, axis=-1, keepdims=True) + EPS)
        h_ref[...] = (xm * r * v_ref[1:2, :] * (1.0 + v_ref[3:4, :]) + v_ref[2:3, :]).astype(BF16)

    blk = pl.BlockSpec((tb, D), lambda i: (i, 0))
    return _pcall(
        body, name="resid_rms2_fwd", grid=(L // tb,),
        in_specs=[blk, blk, pl.BlockSpec((8, D), lambda i: (0, 0))],
        out_specs=[blk, blk],
        out_shape=[jax.ShapeDtypeStruct((L, D), F32), jax.ShapeDtypeStruct((L, D), BF16)],
        compiler_params=_params(("parallel",)),
    )(x, mo, vecs)


def _resid_rms2_bwd(xmid, dh_a, dh_b, dy, mo, vecs):
    L, D = xmid.shape
    tb = _pick(L, (256, 128, 64))

    def body(xm_ref, da_ref, db_ref, dy_ref, mo_ref, v_ref, dxm_ref, dmo_ref, cs_ref):
        i = pl.program_id(0)
        xm = xm_ref[...]
        r = lax.rsqrt(jnp.mean(xm * xm, axis=-1, keepdims=True) + EPS)
        xh = xm * r
        g = v_ref[1:2, :]
        nrm = xh * g
        dhv = da_ref[...] + db_ref[...]
        dn = dhv * (1.0 + v_ref[3:4, :])
        dxh = dn * g
        dxm = dy_ref[...] + r * (dxh - xh * jnp.mean(dxh * xh, axis=-1, keepdims=True))
        dxm_ref[...] = dxm
        dmo_ref[...] = (dxm * v_ref[0:1, :]).astype(BF16)
        s0 = jnp.sum(dhv, axis=0, keepdims=True)
        s1 = jnp.sum(dhv * nrm, axis=0, keepdims=True)
        s2 = jnp.sum(dn * xh, axis=0, keepdims=True)
        s3 = jnp.sum(dxm * mo_ref[...], axis=0, keepdims=True)
        rows = lax.broadcasted_iota(jnp.int32, (8, D), 0)
        upd = jnp.where(rows == 0, s0, jnp.where(rows == 1, s1, jnp.where(rows == 2, s2,
              jnp.where(rows == 3, s3, 0.0))))

        @pl.when(i == 0)
        def _():
            cs_ref[...] = jnp.zeros_like(cs_ref)

        cs_ref[...] += upd

    blk = pl.BlockSpec((tb, D), lambda i: (i, 0))
    vec = pl.BlockSpec((8, D), lambda i: (0, 0))
    return _pcall(
        body, name="resid_rms2_bwd", grid=(L // tb,),
        in_specs=[blk, blk, blk, blk, blk, vec],
        out_specs=[blk, blk, vec],
        out_shape=[jax.ShapeDtypeStruct((L, D), F32), jax.ShapeDtypeStruct((L, D), BF16),
                   jax.ShapeDtypeStruct((8, D), F32)],
        compiler_params=_params(("arbitrary",)),
    )(xmid, dh_a, dh_b, dy, mo, vecs)


def _loss_head(xmid, f, g2, target):
    L, D = xmid.shape
    tb = _pick(L, (256, 128, 64))

    def body(xm_ref, f_ref, g_ref, t_ref, dy_ref, df_ref, s_ref):
        i = pl.program_id(0)
        fv = f_ref[...]
        g = g_ref[...]
        err = xm_ref[...] + g * fv - t_ref[...]
        dy = err * (1.0 / D)
        dy_ref[...] = dy
        df_ref[...] = (dy * g).astype(BF16)
        s0 = jnp.sum(dy * fv, axis=0, keepdims=True)
        part = 0.5 * jnp.sum(jnp.mean(err * err, axis=-1, keepdims=True), axis=0, keepdims=True)
        rows = lax.broadcasted_iota(jnp.int32, (8, D), 0)
        upd = jnp.where(rows == 0, s0, jnp.where(rows == 1, part, 0.0))

        @pl.when(i == 0)
        def _():
            s_ref[...] = jnp.zeros_like(s_ref)

        s_ref[...] += upd

    blk = pl.BlockSpec((tb, D), lambda i: (i, 0))
    return _pcall(
        body, name="loss_head", grid=(L // tb,),
        in_specs=[blk, blk, pl.BlockSpec((1, D), lambda i: (0, 0)), blk],
        out_specs=[blk, blk, pl.BlockSpec((8, D), lambda i: (0, 0))],
        out_shape=[jax.ShapeDtypeStruct((L, D), F32), jax.ShapeDtypeStruct((L, D), BF16),
                   jax.ShapeDtypeStruct((8, D), F32)],
        compiler_params=_params(("arbitrary",)),
    )(xmid, f, g2, target)


def _gate_cols(D, off):
    tc = _pick(np.gcd(D, off), (512, 256, 128))
    return tc, off // tc


def _merge_fwd(za, zb, p, n_ctx, off_a, off_b):
    L, D = za.shape
    tb = _pick(n_ctx, (256, 128, 64, 32, 16))
    nctx = n_ctx // tb
    tc, oa = _gate_cols(D, off_a)
    _, ob = _gate_cols(D, off_b)
    if off_b % tc:
        raise ValueError("gate column offsets must share a column tile")
    ob = off_b // tc

    def body(za_ref, zb_ref, ga_ref, gb_ref, z_ref):
        z_ref[...] = (_sigmoid(ga_ref[...]) * za_ref[...].astype(F32)
                      + _sigmoid(gb_ref[...]) * zb_ref[...].astype(F32)).astype(BF16)

    blk = pl.BlockSpec((tb, tc), lambda i, j: (i, j))
    return _pcall(
        body, name="merge_fwd", grid=(L // tb, D // tc),
        in_specs=[blk, blk, pl.BlockSpec((tb, tc), lambda i, j: (i + nctx, oa + j)),
                  pl.BlockSpec((tb, tc), lambda i, j: (i + nctx, ob + j))],
        out_specs=blk,
        out_shape=jax.ShapeDtypeStruct((L, D), BF16),
        compiler_params=_params(("parallel", "parallel")),
    )(za, zb, p, p)


def _merge_bwd(dz, za, zb, p, n_ctx, off_a, off_b):
    L, D = za.shape
    T = L + n_ctx
    tb = _pick(n_ctx, (256, 128, 64, 32, 16))
    nctx = n_ctx // tb
    tc = _gate_cols(D, off_a)[0]
    oa, ob = off_a // tc, off_b // tc

    def body(dz_ref, za_ref, zb_ref, ga_ref, gb_ref, dza_ref, dzb_ref, dga_ref, dgb_ref):
        i = pl.program_id(1)

        @pl.when(i < nctx)
        def _():
            dga_ref[...] = jnp.zeros_like(dga_ref)
            dgb_ref[...] = jnp.zeros_like(dgb_ref)

        @pl.when(i >= nctx)
        def _():
            dzv = dz_ref[...].astype(F32)
            sa = _sigmoid(ga_ref[...])
            sb = _sigmoid(gb_ref[...])
            dza_ref[...] = (dzv * sa).astype(BF16)
            dzb_ref[...] = (dzv * sb).astype(BF16)
            dga_ref[...] = (dzv * za_ref[...].astype(F32) * sa * (1.0 - sa)).astype(BF16)
            dgb_ref[...] = (dzv * zb_ref[...].astype(F32) * sb * (1.0 - sb)).astype(BF16)

    lat = pl.BlockSpec((tb, tc), lambda j, i: (jnp.maximum(i - nctx, 0), j))
    allr = pl.BlockSpec((tb, tc), lambda j, i: (i, j))
    return _pcall(
        body, name="merge_bwd", grid=(D // tc, T // tb),
        in_specs=[lat, lat, lat, pl.BlockSpec((tb, tc), lambda j, i: (i, oa + j)),
                  pl.BlockSpec((tb, tc), lambda j, i: (i, ob + j))],
        out_specs=[lat, lat, allr, allr],
        out_shape=[jax.ShapeDtypeStruct((L, D), BF16), jax.ShapeDtypeStruct((L, D), BF16),
                   jax.ShapeDtypeStruct((T, D), BF16), jax.ShapeDtypeStruct((T, D), BF16)],
        compiler_params=_params(("arbitrary", "arbitrary")),
    )(dz, za, zb, p, p)


def _shift_down(u, rows):
    return jnp.where(rows == 0, 0.0, pltpu.roll(u, 1, 0))


def _shift_up(u, rows):
    n = u.shape[0]
    return jnp.where(rows == n - 1, 0.0, pltpu.roll(u, n - 1, 0))


def _convgate_fwd(u1, u3, cw, cb):
    L, F = u1.shape
    tc = _pick(F, (256, 128))

    def body(u1_ref, u3_ref, w_ref, b_ref, a_ref):
        u = u1_ref[...].astype(F32)
        rows = lax.broadcasted_iota(jnp.int32, u.shape, 0)
        cv = _shift_down(u, rows) * w_ref[0:1, :] + u * w_ref[1:2, :] + _shift_up(u, rows) * w_ref[2:3, :] + b_ref[...]
        a_ref[...] = (cv * _sigmoid(cv) * u3_ref[...].astype(F32)).astype(BF16)

    blk = pl.BlockSpec((L, tc), lambda j: (0, j))
    return _pcall(
        body, name="convgate_fwd", grid=(F // tc,),
        in_specs=[blk, blk, pl.BlockSpec((8, tc), lambda j: (0, j)), pl.BlockSpec((1, tc), lambda j: (0, j))],
        out_specs=blk,
        out_shape=jax.ShapeDtypeStruct((L, F), BF16),
        compiler_params=_params(("parallel",)),
    )(u1, u3, cw, cb)


def _convgate_bwd(u1, u3, da, cw, cb):
    L, F = u1.shape
    tc = _pick(F, (256, 128))

    def body(u1_ref, u3_ref, da_ref, w_ref, b_ref, du1_ref, du3_ref, s_ref):
        u = u1_ref[...].astype(F32)
        rows = lax.broadcasted_iota(jnp.int32, u.shape, 0)
        um, up = _shift_down(u, rows), _shift_up(u, rows)
        w0, w1, w2 = w_ref[0:1, :], w_ref[1:2, :], w_ref[2:3, :]
        cv = um * w0 + u * w1 + up * w2 + b_ref[...]
        s = _sigmoid(cv)
        dav = da_ref[...].astype(F32)
        du3_ref[...] = (dav * cv * s).astype(BF16)
        dcv = dav * u3_ref[...].astype(F32) * (s * (1.0 + cv * (1.0 - s)))
        du1_ref[...] = (_shift_up(dcv, rows) * w0 + dcv * w1 + _shift_down(dcv, rows) * w2).astype(BF16)
        r8 = lax.broadcasted_iota(jnp.int32, (8, tc), 0)
        s0 = jnp.sum(dcv * um, axis=0, keepdims=True)
        s1 = jnp.sum(dcv * u, axis=0, keepdims=True)
        s2 = jnp.sum(dcv * up, axis=0, keepdims=True)
        s3 = jnp.sum(dcv, axis=0, keepdims=True)
        s_ref[...] = jnp.where(r8 == 0, s0, jnp.where(r8 == 1, s1, jnp.where(r8 == 2, s2,
                     jnp.where(r8 == 3, s3, 0.0))))

    blk = pl.BlockSpec((L, tc), lambda j: (0, j))
    v8 = pl.BlockSpec((8, tc), lambda j: (0, j))
    return _pcall(
        body, name="convgate_bwd", grid=(F // tc,),
        in_specs=[blk, blk, blk, v8, pl.BlockSpec((1, tc), lambda j: (0, j))],
        out_specs=[blk, blk, v8],
        out_shape=[jax.ShapeDtypeStruct((L, F), BF16), jax.ShapeDtypeStruct((L, F), BF16),
                   jax.ShapeDtypeStruct((8, F), F32)],
        compiler_params=_params(("parallel",)),
    )(u1, u3, da, cw, cb)


def _lower_bound(lbl_ref, d):
    l0, l1 = lbl_ref[d, 0:1, :], lbl_ref[d, 1:2, :]
    m = jnp.maximum(l0, l1)
    e0, e1 = jnp.exp(l0 - m), jnp.exp(l1 - m)
    return e0 / (e0 + e1)


def _chunk_cumsum(x, rev):
    n = x.shape[0]
    r = lax.broadcasted_iota(jnp.int32, x.shape, 0) % CHUNK
    k = 1
    while k < CHUNK:
        if rev:
            x = x + jnp.where(r < CHUNK - k, pltpu.roll(x, n - k, 0), 0.0)
        else:
            x = x + jnp.where(r >= k, pltpu.roll(x, k, 0), 0.0)
        k *= 2
    return x


def _gate_terms(z, lb):
    sg = _sigmoid(z)
    f = lb + (1.0 - lb) * sg
    return sg, f


def _decay_terms(z, lb, rev):
    _, f = _gate_terms(z, lb)
    g = jnp.log(f)
    return 1.0 - f, _chunk_cumsum(g, rev), _chunk_cumsum(g, not rev) - g


def _chunk_total(c, rev):
    return c[0:1, :] if rev else c[CHUNK - 1:CHUNK, :]


def _pair_decay(c, s, rev):
    t = lax.broadcasted_iota(jnp.int32, (CHUNK, 1), 0)
    later = (t <= s) if rev else (t >= s)
    return jnp.where(later, jnp.exp(c - c[s:s + 1, :]), 0.0)


def _scan_chunk(i, n_ctx_chunks, n_chunks, rev):
    if not rev:
        return i
    return jnp.where(i < n_ctx_chunks, n_ctx_chunks - 1 - i, n_chunks + n_ctx_chunks - 1 - i)


def _rows(ci):
    return pl.ds(pl.multiple_of(ci * CHUNK, CHUNK), CHUNK)


def _hgrn_cols(HA):
    return HA // HEAD


def _hgrn_fwd(p, lbl, ng, n_ctx, HA):
    T = p.shape[0]
    L = T - n_ctx
    nh = _hgrn_cols(HA)
    nc, ncc = T // CHUNK, n_ctx // CHUNK

    def body(q_ref, zf_ref, zb_ref, v_ref, og_ref, lbl_ref, ng_ref, ya_ref, o_ref, st_ref,
             c_scr, k_scr, qe_scr, ke_scr, o_scr):
        dirs = ((0, False, zf_ref), (1, True, zb_ref))
        for d, rev, z_ref in dirs:
            k, c, rest = _decay_terms(z_ref[...], _lower_bound(lbl_ref, d), rev)
            c_scr[d] = c
            k_scr[d] = k
            qe_scr[d] = (q_ref[...] * jnp.exp(c)).astype(BF16)
            ke_scr[d] = (k * jnp.exp(rest)).astype(BF16)

        def step(i2, states):
            states = list(states)
            for u in range(HGRN_UNROLL):
                for d, rev, _ in dirs:
                    St = states[d]
                    ci = _scan_chunk(HGRN_UNROLL * i2 + u, ncc, nc, rev)
                    rows = _rows(ci)
                    q, v, c, k = q_ref[rows, :], v_ref[rows, :], c_scr[d, rows, :], k_scr[d, rows, :]
                    st_ref[0, d, ci] = St.astype(BF16)
                    o = jnp.zeros((CHUNK, HEAD), F32)
                    for s in range(CHUNK):
                        E = _pair_decay(c, s, rev)
                        a = jnp.sum(q * E * k[s:s + 1, :], axis=1, keepdims=True)
                        o = o + a * v[s:s + 1, :]
                    o_scr[d, rows, :] = o + _dot_nt(qe_scr[d, rows, :], St.astype(BF16))
                    states[d] = St * jnp.exp(_chunk_total(c, rev)) + _dot_tn(v.astype(BF16), ke_scr[d, rows, :])
            return tuple(states)

        if nc % HGRN_UNROLL:
            raise ValueError("the number of chunks must be a multiple of HGRN_UNROLL")
        zero = jnp.zeros((HEAD, HEAD), F32)
        lax.fori_loop(0, nc // HGRN_UNROLL, step, (zero, zero))

        o = o_scr[0, pl.ds(n_ctx, L), :] + o_scr[1, pl.ds(n_ctx, L), :]
        o_ref[...] = o
        r = lax.rsqrt(jnp.mean(o * o, axis=-1, keepdims=True) + EPS)
        og = og_ref[pl.ds(n_ctx, L), :]
        ya_ref[...] =(o * r * ng_ref[...] * (og * _sigmoid(og))).astype(BF16)

    cb = HA // HEAD
    col = lambda kk: pl.BlockSpec((T, HEAD), lambda h: (0, kk * cb + h))
    return _pcall(
        body, name="hgrn_fwd", grid=(nh,),
        in_specs=[col(0), col(1), col(2), col(3), col(4),
                  pl.BlockSpec((2, 2, HEAD), lambda h: (0, 0, h)), pl.BlockSpec((1, HEAD), lambda h: (0, 0))],
        out_specs=[pl.BlockSpec((L, HEAD), lambda h: (0, h)), pl.BlockSpec((L, HEAD), lambda h: (0, h)),
                   pl.BlockSpec((1, 2, nc, HEAD, HEAD), lambda h: (h, 0, 0, 0, 0))],
        out_shape=[jax.ShapeDtypeStruct((L, HA), BF16), jax.ShapeDtypeStruct((L, HA), F32),
                   jax.ShapeDtypeStruct((nh, 2, nc, HEAD, HEAD), BF16)],
        scratch_shapes=[pltpu.VMEM((2, T, HEAD), F32), pltpu.VMEM((2, T, HEAD), F32),
                        pltpu.VMEM((2, T, HEAD), BF16), pltpu.VMEM((2, T, HEAD), BF16),
                        pltpu.VMEM((2, T, HEAD), F32)],
        compiler_params=_params(("parallel",)),
    )(p, p, p, p, p, lbl, ng)


def _hgrn_bwd(p, lbl, ng, o, dya, st, n_ctx, HA):
    T = p.shape[0]
    L = T - n_ctx
    nh = _hgrn_cols(HA)
    nc, ncc = T // CHUNK, n_ctx // CHUNK

    def body(q_ref, zf_ref, zb_ref, v_ref, og_ref, lbl_ref, ng_ref, o_ref, dya_ref, st_ref,
             dq_ref, dzf_ref, dzb_ref, dv_ref, dog_ref, dlbl_ref, dng_ref,
             do_scr, c_scr, k_scr, qe_scr, ke_scr, dg_scr, dk_scr, dq_scr, dv_scr, row_scr):
        h = pl.program_id(0)
        ov = o_ref[...]
        r = lax.rsqrt(jnp.mean(ov * ov, axis=-1, keepdims=True) + EPS)
        oh = ov * r
        ogv = og_ref[pl.ds(n_ctx, L), :]
        sg_o = _sigmoid(ogv)
        dyv = dya_ref[...]
        ngv = ng_ref[...]
        dog_ref[pl.ds(0, n_ctx), :] = jnp.zeros((n_ctx, HEAD), BF16)
        dog_ref[pl.ds(n_ctx, L), :] = (dyv * oh * ngv * (sg_o * (1.0 + ogv * (1.0 - sg_o)))).astype(BF16)
        don = dyv * (ogv * sg_o)
        dng = jnp.sum(don * oh, axis=0, keepdims=True)
        doh = don * ngv
        do_scr[pl.ds(0, n_ctx), :] = jnp.zeros((n_ctx, HEAD), F32)
        do_scr[pl.ds(n_ctx, L), :] = r * (doh - oh * jnp.mean(doh * oh, axis=-1, keepdims=True))

        @pl.when(h == 0)
        def _():
            dng_ref[...] = jnp.zeros_like(dng_ref)

        dng_ref[0:1, :] += dng

        t16 = lax.broadcasted_iota(jnp.int32, (CHUNK, HEAD), 0)
        dirs = ((0, False, zf_ref, dzf_ref), (1, True, zb_ref, dzb_ref))
        for d, rev, z_ref, _ in dirs:
            k, c, rest = _decay_terms(z_ref[...], _lower_bound(lbl_ref, d), rev)
            c_scr[d] = c
            k_scr[d] = k
            qe_scr[d] = (q_ref[...] * jnp.exp(c)).astype(BF16)
            ke_scr[d] = (k * jnp.exp(rest)).astype(BF16)
        dq_scr[...] = jnp.zeros_like(dq_scr)
        dv_scr[...] = jnp.zeros_like(dv_scr)

        zero = jnp.zeros((HEAD, HEAD), F32)

        def bwd_chunk(i, carry, u):
            new = []
            for (d, rev, _, _), dSt in zip(dirs, carry):
                ci = _scan_chunk(i, ncc, nc, rev)
                rows = _rows(ci)
                q, v, do = q_ref[rows, :], v_ref[rows, :], do_scr[rows, :]
                c, k = c_scr[d, rows, :], k_scr[d, rows, :]
                tot = _chunk_total(c, rev)
                etot = jnp.exp(tot)
                St = st_ref[0, d, ci]
                dSb = dSt.astype(BF16)
                do_b = do.astype(BF16)
                dq_x = _dot(do_b, St) * jnp.exp(c)
                dk_x = _dot(v.astype(BF16), dSb) * jnp.exp(tot - c)
                dv_x = _dot_nt(ke_scr[d, rows, :], dSb)
                dtot = (jnp.sum(St.astype(F32) * dSt, axis=0, keepdims=True) * etot
                        + jnp.sum(k * dk_x, axis=0, keepdims=True))
                dq = jnp.zeros((CHUNK, HEAD), F32)
                for s in range(CHUNK):
                    E = _pair_decay(c, s, rev)
                    XE = E * k[s:s + 1, :]
                    a = jnp.sum(q * XE, axis=1, keepdims=True)
                    da = jnp.sum(do * v[s:s + 1, :], axis=1, keepdims=True)
                    dq = dq + da * XE
                    row_scr[u, d, 0, s:s + 1, :] = jnp.sum(da * q * E, axis=0, keepdims=True)
                    row_scr[u, d, 1, s:s + 1, :] = jnp.sum(a * do, axis=0, keepdims=True)
                dq, dk, dv = dq + dq_x, row_scr[u, d, 0] + dk_x, row_scr[u, d, 1] + dv_x
                dg_scr[d, rows, :] = _chunk_cumsum(q * dq - k * dk, not rev) + dtot
                dk_scr[d, rows, :] = dk
                dq_scr[rows, :] += dq
                dv_scr[rows, :] += dv
                new.append(dSt * etot + _dot_tn(do_b, qe_scr[d, rows, :]))
            return tuple(new)

        def bwd_step(i2, carry):
            for u in range(2):
                carry = bwd_chunk(nc - 1 - (2 * i2 + u), carry, u)
            return carry

        lax.fori_loop(0, nc // 2, bwd_step, (zero, zero))

        for d, _, z_ref, dz_ref in dirs:
            lb = _lower_bound(lbl_ref, d)
            sg, f = _gate_terms(z_ref[...], lb)
            df = dg_scr[d] / f - dk_scr[d]
            dz_ref[...] = (df * (1.0 - lb) * sg * (1.0 - sg)).astype(BF16)
            dl0 = jnp.sum(df * (1.0 - sg), axis=0, keepdims=True) * lb * (1.0 - lb)
            dlbl_ref[d, 0:1, :] = dl0
            dlbl_ref[d, 1:2, :] = -dl0
        dq_ref[...] = dq_scr[...].astype(BF16)
        dv_ref[...] = dv_scr[...].astype(BF16)

    cb = HA // HEAD
    col = lambda kk: pl.BlockSpec((T, HEAD), lambda h: (0, kk * cb + h))
    tcol = pl.BlockSpec((T, HEAD), lambda h: (0, h))
    lcol = pl.BlockSpec((L, HEAD), lambda h: (0, h))
    outs = _pcall(
        body, name="hgrn_bwd", grid=(nh,),
        in_specs=[col(0), col(1), col(2), col(3), col(4),
                  pl.BlockSpec((2, 2, HEAD), lambda h: (0, 0, h)), pl.BlockSpec((1, HEAD), lambda h: (0, 0)),
                  lcol, lcol,
                  pl.BlockSpec((1, 2, nc, HEAD, HEAD), lambda h: (h, 0, 0, 0, 0), pipeline_mode=pl.Buffered(1))],
        out_specs=[tcol, tcol, tcol, tcol, tcol, pl.BlockSpec((2, 2, HEAD), lambda h: (0, 0, h)),
                   pl.BlockSpec((8, HEAD), lambda h: (0, 0))],
        out_shape=[jax.ShapeDtypeStruct((T, HA), BF16)] * 5 + [jax.ShapeDtypeStruct((2, 2, HA), F32),
                                                               jax.ShapeDtypeStruct((8, HEAD), F32)],
        scratch_shapes=[pltpu.VMEM((T, HEAD), F32),
                        pltpu.VMEM((2, T, HEAD), F32), pltpu.VMEM((2, T, HEAD), F32),
                        pltpu.VMEM((2, T, HEAD), BF16), pltpu.VMEM((2, T, HEAD), BF16),
                        pltpu.VMEM((2, T, HEAD), F32), pltpu.VMEM((2, T, HEAD), F32),
                        pltpu.VMEM((T, HEAD), F32), pltpu.VMEM((T, HEAD), F32),
                        pltpu.VMEM((2, 2, 2, CHUNK, HEAD), F32)],
        compiler_params=_params(("arbitrary",)),
    )(p, p, p, p, p, lbl, ng, o, dya, st)
    return outs


def _swap_halves(t, lane):
    q = HEAD // 4
    return jnp.where((lane % (2 * q)) < q, pltpu.roll(t, HEAD - q, 1), pltpu.roll(t, q, 1))


def _qk_norm(t, g):
    r = lax.rsqrt(jnp.mean(t * t, axis=-1, keepdims=True) + EPS)
    return t * r, r


def _rope(t, cos, sin, lane):
    return t * cos + _swap_halves(t, lane) * sin


def _qk_norm_bwd(dy, th, r, g):
    dth = dy * g
    return r * (dth - th * jnp.mean(dth * th, axis=-1, keepdims=True)), jnp.sum(dy * th, axis=0, keepdims=True)


def _rope_bwd(dy, cos, sin, lane):
    return dy * cos + _swap_halves(dy * sin, lane)


def _na_geometry(L):
    n_rows = L // GRID_W
    kr = min(WIN_R, n_rows)
    return n_rows, kr


def _na_prep(q_ref, k_ref, v_ref, gq_ref, gk_ref, cos_ref, sin_ref, qs, ks, vs, n_ctx, L):
    lane = lax.broadcasted_iota(jnp.int32, (L, HEAD), 1)
    cos, sin = cos_ref[...], sin_ref[...]
    qh, _ = _qk_norm(q_ref[pl.ds(n_ctx, L), :], None)
    qs[...] = _rope(qh * gq_ref[...], cos, sin, lane).astype(BF16)
    kh, _ = _qk_norm(k_ref[pl.ds(n_ctx, L), :], None)
    ks[pl.ds(n_ctx, L), :] = _rope(kh * gk_ref[...], cos, sin, lane).astype(BF16)
    kc, _ = _qk_norm(k_ref[pl.ds(0, n_ctx), :], None)
    ks[pl.ds(0, n_ctx), :] = (kc * gk_ref[...]).astype(BF16)
    vs[...] = v_ref[...].astype(BF16)


NA_RB = 4


def _na_band_rows(kr):
    return kr + NA_RB


def _na_scores(i, qs, ks, bias_ref, n_ctx, n_rows, kr):
    scale = HEAD ** -0.5
    kb = _na_band_rows(kr)
    rq = NA_RB * i
    r0 = jnp.clip(rq - WIN_R // 2, 0, n_rows - kb)
    qrows = pl.ds(pl.multiple_of(rq * GRID_W, NA_RB * GRID_W), NA_RB * GRID_W)
    krows = pl.ds(pl.multiple_of(n_ctx + r0 * GRID_W, GRID_W), kb * GRID_W)
    qv = qs[qrows, :]
    sb = _dot_nt(qv, ks[krows, :]) * scale
    band_row = lax.broadcasted_iota(jnp.int32, (GRID_W, kb * GRID_W), 1) // GRID_W
    parts, tiles = [], []
    for u in range(NA_RB):
        r_u = rq + u
        first = jnp.clip(r_u - WIN_R // 2, 0, n_rows - kr) - r0
        idx = [jnp.clip(r0 - r_u + (WIN_R - 1) + 2 * jj, 0, 2 * WIN_R - 1) for jj in range(kb // 2)]
        bias_u = jnp.concatenate([bias_ref[0, t] for t in idx], axis=1)
        inside = (band_row >= first) & (band_row < first + kr)
        parts.append(jnp.where(inside, sb[u * GRID_W:(u + 1) * GRID_W, :] + bias_u, NEG))
        tiles.append(idx)
    sb = jnp.concatenate(parts, axis=0)
    sc = _dot_nt(qv, ks[pl.ds(0, n_ctx), :]) * scale
    m = jnp.maximum(jnp.max(sb, axis=1, keepdims=True), jnp.max(sc, axis=1, keepdims=True))
    eb, ec = jnp.exp(sb - m), jnp.exp(sc - m)
    inv = 1.0 / (jnp.sum(eb, axis=1, keepdims=True) + jnp.sum(ec, axis=1, keepdims=True))
    return eb * inv, ec * inv, qrows, krows, tiles


def _na_fwd(p, bias, gq, gk, cos, sin, n_ctx, off, HB):
    T = p.shape[0]
    L = T - n_ctx
    nh = HB // HEAD
    n_rows, kr = _na_geometry(L)
    ob = off // HEAD

    def body(q_ref, k_ref, v_ref, bias_ref, gq_ref, gk_ref, cos_ref, sin_ref, y_ref, qs, ks, vs):
        _na_prep(q_ref, k_ref, v_ref, gq_ref, gk_ref, cos_ref, sin_ref, qs, ks, vs, n_ctx, L)

        def step(i, carry):
            pb, pc, qrows, krows, _ = _na_scores(i, qs, ks, bias_ref, n_ctx, n_rows, kr)
            y = _dot(pb.astype(BF16), vs[krows, :]) + _dot(pc.astype(BF16), vs[pl.ds(0, n_ctx), :])
            y_ref[qrows, :] = y.astype(BF16)
            return carry

        lax.fori_loop(0, n_rows // NA_RB, step, 0)

    col = lambda kk: pl.BlockSpec((T, HEAD), lambda h: (0, ob + kk * nh + h))
    vec = pl.BlockSpec((1, HEAD), lambda h: (0, 0))
    tab = pl.BlockSpec((L, HEAD), lambda h: (0, 0))
    return _pcall(
        body, name="na_fwd", grid=(nh,),
        in_specs=[col(0), col(1), col(2), pl.BlockSpec((1,) + bias.shape[1:], lambda h: (h, 0, 0, 0)),
                  vec, vec, tab, tab],
        out_specs=pl.BlockSpec((L, HEAD), lambda h: (0, h)),
        out_shape=jax.ShapeDtypeStruct((L, HB), BF16),
        scratch_shapes=[pltpu.VMEM((L, HEAD), BF16), pltpu.VMEM((T, HEAD), BF16), pltpu.VMEM((T, HEAD), BF16)],
        compiler_params=_params(("parallel",)),
    )(p, p, p, bias, gq, gk, cos, sin)


def _na_bwd(p, bias, gq, gk, cos, sin, dyb, n_ctx, off, HB):
    T = p.shape[0]
    L = T - n_ctx
    nh = HB // HEAD
    n_rows, kr = _na_geometry(L)
    ob = off // HEAD
    scale = HEAD ** -0.5

    def body(q_ref, k_ref, v_ref, bias_ref, gq_ref, gk_ref, cos_ref, sin_ref, dy_ref,
             dq_ref, dk_ref, dv_ref, dbias_ref, dg_ref, qs, ks, vs, dqa, dka, dva):
        h = pl.program_id(0)
        _na_prep(q_ref, k_ref, v_ref, gq_ref, gk_ref, cos_ref, sin_ref, qs, ks, vs, n_ctx, L)
        dka[...] = jnp.zeros_like(dka)
        dva[...] = jnp.zeros_like(dva)
        dbias_ref[...] = jnp.zeros_like(dbias_ref)

        crows = pl.ds(0, n_ctx)

        def step(i, carry):
            pb, pc, qrows, krows, tiles = _na_scores(i, qs, ks, bias_ref, n_ctx, n_rows, kr)
            do = dy_ref[qrows, :]
            qv = qs[qrows, :]
            dpb = _dot_nt(do, vs[krows, :])
            dpc = _dot_nt(do, vs[crows, :])
            delta = jnp.sum(pb * dpb, axis=1, keepdims=True) + jnp.sum(pc * dpc, axis=1, keepdims=True)
            dsb = pb * (dpb - delta)
            dsc = pc * (dpc - delta)
            dsb_b, dsc_b = dsb.astype(BF16), dsc.astype(BF16)
            dqa[qrows, :] = (_dot(dsb_b, ks[krows, :]) + _dot(dsc_b, ks[crows, :])) * scale
            dka[krows, :] += _dot_tn(dsb_b, qv) * scale
            dka[crows, :] += _dot_tn(dsc_b, qv) * scale
            dva[krows, :] += _dot_tn(pb.astype(BF16), do)
            dva[crows, :] += _dot_tn(pc.astype(BF16), do)
            for u, idx in enumerate(tiles):
                for jj, t in enumerate(idx):
                    dbias_ref[0, t] += dsb[u * GRID_W:(u + 1) * GRID_W, jj * 2 * GRID_W:(jj + 1) * 2 * GRID_W]
            return carry

        lax.fori_loop(0, n_rows // NA_RB, step, 0)

        lane = lax.broadcasted_iota(jnp.int32, (L, HEAD), 1)
        cos, sin = cos_ref[...], sin_ref[...]
        lat, ctx = pl.ds(n_ctx, L), pl.ds(0, n_ctx)
        gqv, gkv = gq_ref[...], gk_ref[...]
        qh, rq = _qk_norm(q_ref[lat, :], None)
        dq, dgq = _qk_norm_bwd(_rope_bwd(dqa[...], cos, sin, lane), qh, rq, gqv)
        dq_ref[ctx, :] = jnp.zeros((n_ctx, HEAD), BF16)
        dq_ref[lat, :] = dq.astype(BF16)
        kh, rk = _qk_norm(k_ref[lat, :], None)
        dk, dgk = _qk_norm_bwd(_rope_bwd(dka[lat, :], cos, sin, lane), kh, rk, gkv)
        dk_ref[lat, :] = dk.astype(BF16)
        kch, rkc = _qk_norm(k_ref[ctx, :], None)
        dkc, dgkc = _qk_norm_bwd(dka[ctx, :], kch, rkc, gkv)
        dk_ref[ctx, :] = dkc.astype(BF16)
        dv_ref[...] = dva[...].astype(BF16)

        @pl.when(h == 0)
        def _():
            dg_ref[...] = jnp.zeros_like(dg_ref)

        dg_ref[0:1, :] += dgq
        dg_ref[1:2, :] += dgk + dgkc

    col = lambda kk: pl.BlockSpec((T, HEAD), lambda h: (0, ob + kk * nh + h))
    vec = pl.BlockSpec((1, HEAD), lambda h: (0, 0))
    tab = pl.BlockSpec((L, HEAD), lambda h: (0, 0))
    tcol = pl.BlockSpec((T, HEAD), lambda h: (0, h))
    bspec = pl.BlockSpec((1,) + bias.shape[1:], lambda h: (h, 0, 0, 0))
    return _pcall(
        body, name="na_bwd", grid=(nh,),
        in_specs=[col(0), col(1), col(2), bspec, vec, vec, tab, tab, pl.BlockSpec((L, HEAD), lambda h: (0, h))],
        out_specs=[tcol, tcol, tcol, bspec, pl.BlockSpec((8, HEAD), lambda h: (0, 0))],
        out_shape=[jax.ShapeDtypeStruct((T, HB), BF16)] * 3 + [jax.ShapeDtypeStruct(bias.shape, F32),
                                                               jax.ShapeDtypeStruct((8, HEAD), F32)],
        scratch_shapes=[pltpu.VMEM((L, HEAD), BF16), pltpu.VMEM((T, HEAD), BF16), pltpu.VMEM((T, HEAD), BF16),
                        pltpu.VMEM((L, HEAD), F32), pltpu.VMEM((T, HEAD), F32), pltpu.VMEM((T, HEAD), F32)],
        compiler_params=_params(("arbitrary",)),
    )(p, p, p, bias, gq, gk, cos, sin, dyb)


def _bias_tables():
    w = np.arange(GRID_W)
    col_start = np.clip(w - WIN_C // 2, 0, GRID_W - WIN_C)
    col_in = (w[None, :] >= col_start[:, None]) & (w[None, :] < col_start[:, None] + WIN_C)
    dc = np.clip(w[None, :] - w[:, None], -(WIN_C - 1), WIN_C - 1) + WIN_C - 1
    n_pair = 2 * WIN_R
    ridx = np.zeros((n_pair, GRID_W, 2 * GRID_W), np.int32)
    cidx = np.zeros((n_pair, GRID_W, 2 * GRID_W), np.int32)
    valid = np.zeros((n_pair, GRID_W, 2 * GRID_W), bool)
    for i in range(n_pair):
        for half in range(2):
            row = i + half
            sl = slice(half * GRID_W, (half + 1) * GRID_W)
            ridx[i, :, sl] = min(row, 2 * WIN_R - 2)
            cidx[i, :, sl] = dc
            valid[i, :, sl] = col_in & (row <= 2 * WIN_R - 2)
    return ridx, cidx, valid


def _bias_onehot():
    _, cidx, valid = _bias_tables()
    K = GRID_W * 2 * GRID_W
    oh = np.zeros((K, 128), np.float32)
    neg = np.full((1, K), NEG, np.float32)
    for cq in range(GRID_W):
        for ll in range(2 * GRID_W):
            if valid[0, cq, ll]:
                oh[cq * 2 * GRID_W + ll, (ll // GRID_W) * 64 + cidx[0, cq, ll]] = 1.0
                neg[0, cq * 2 * GRID_W + ll] = 0.0
    return oh, neg


def _expand_bias(table):
    H = table.shape[0]
    n_pair, n_dc = 2 * WIN_R, 2 * WIN_C - 1
    tp = jnp.pad(table, ((0, 0), (0, n_pair + 1 - table.shape[1]), (0, 64 - n_dc)))
    t2 = jnp.concatenate([tp[:, :n_pair], tp[:, 1:n_pair + 1]], axis=-1).reshape(H * n_pair, 128)
    oh, neg = _bias_onehot()

    def body(t_ref, oh_ref, neg_ref, o_ref):
        o_ref[...] = lax.dot_general(t_ref[...], oh_ref[...], (((1,), (1,)), ((), ())), precision=HI,
                                     preferred_element_type=F32) + neg_ref[...]

    out = _pcall(body, name="bias_expand", out_shape=jax.ShapeDtypeStruct((H * n_pair, oh.shape[0]), F32),
                         compiler_params=_params())(t2, jnp.asarray(oh), jnp.asarray(neg))
    return out.reshape(H, n_pair, GRID_W, 2 * GRID_W)


def _bias_grad(dbias):
    H = dbias.shape[0]
    n_pair, n_dc = 2 * WIN_R, 2 * WIN_C - 1
    K = GRID_W * 2 * GRID_W
    oh, _ = _bias_onehot()
    flat = dbias.reshape(H * n_pair, K)

    def body(d_ref, oh_ref, o_ref):
        o_ref[...] = jnp.dot(d_ref[...], oh_ref[...], precision=HI, preferred_element_type=F32)

    g = _pcall(body, name="bias_grad", out_shape=jax.ShapeDtypeStruct((H * n_pair, 128), F32),
                       compiler_params=_params())(flat, jnp.asarray(oh))
    g = g.reshape(H, n_pair, 128)
    left, right = g[:, :, :n_dc], g[:, :, 64:64 + n_dc]
    out = left[:, :n_pair - 1]
    return out.at[:, 1:].add(right[:, :n_pair - 2])


def _rope_tables(L):
    pos = np.arange(L)
    row = (pos // GRID_W).astype(np.float32)
    colp = (pos % GRID_W).astype(np.float32)
    half = HEAD // 2
    nf = half // 2
    inv = (ROPE_THETA ** (-np.arange(nf, dtype=np.float32) / nf)).astype(np.float32)

    def tabs(pv):
        ang = pv[:, None] * inv[None, :]
        c, s = np.cos(ang), np.sin(ang)
        return np.concatenate([c, c], axis=1), np.concatenate([-s, s], axis=1)

    cr, sr = tabs(row)
    cc, sc = tabs(colp)
    return (jnp.asarray(np.concatenate([cr, cc], axis=1), F32), jnp.asarray(np.concatenate([sr, sc], axis=1), F32))


def _adamw(w, g, m, v, name, after=None, copy_g=False):
    R, C = w.shape
    tr = _row_tile(R, C)
    c1 = 1.0 - ADAM_B1 ** ADAM_STEP
    c2 = 1.0 - ADAM_B2 ** ADAM_STEP
    deps = [] if after is None else [after]
    n_out = 4 if copy_g else 3

    def body(w_ref, g_ref, m_ref, v_ref, *rest):
        d_ref, mo_ref, vo_ref = rest[len(deps):len(deps) + 3]
        gv = g_ref[...]
        mn = ADAM_B1 * m_ref[...] + (1.0 - ADAM_B1) * gv
        vn = ADAM_B2 * v_ref[...] + (1.0 - ADAM_B2) * (gv * gv)
        mo_ref[...] = mn
        vo_ref[...] = vn
        d_ref[...] = -ADAM_LR * ((mn / c1) / (jnp.sqrt(vn / c2) + ADAM_EPS) + ADAM_WD * w_ref[...])
        if copy_g:
            rest[-1][...] = gv

    blk = pl.BlockSpec((tr, C), lambda i: (i, 0))
    return _pcall(
        body, name=name, grid=(R // tr,),
        in_specs=[blk] * 4 + [_ANY] * len(deps), out_specs=[blk] * n_out,
        out_shape=[jax.ShapeDtypeStruct((R, C), F32)] * n_out,
        compiler_params=_params(("parallel",)),
    )(w, g, m, v, *deps)


PACK_W = 1024


def _pack(parts):
    flat, offs, pos = [], [], 0
    for a in parts:
        n = a.size
        padn = -n % PACK_W
        flat.append(jnp.pad(a.reshape(-1).astype(F32), (0, padn)))
        offs.append((pos, n, a.shape))
        pos += n + padn
    tail = -pos % (8 * PACK_W)
    if tail:
        flat.append(jnp.zeros((tail,), F32))
    return jnp.concatenate(flat).reshape(-1, PACK_W), offs


def _unpack(buf, offs, i):
    pos, n, shape = offs[i]
    return buf.reshape(buf.shape[:-2] + (-1,))[..., pos:pos + n].reshape(buf.shape[:-2] + shape)


def kernel(x, c, ctx, c_ctx, ada_w, ada_b, norm1_g, norm2_g, w_in, hgrn_lb_logits, hgrn_norm_g, na_q_norm_g, na_k_norm_g, na_rel_bias, w_branch_a, w_branch_b, w_out, ffn_w1, ffn_w3, ffn_conv_w, ffn_conv_b, ffn_w2, loss_target, m_c_ctx, m_ada_w, m_ada_b, m_norm1_g, m_norm2_g, m_w_in, m_hgrn_lb_logits, m_hgrn_norm_g, m_na_q_norm_g, m_na_k_norm_g, m_na_rel_bias, m_w_branch_a, m_w_branch_b, m_w_out, m_ffn_w1, m_ffn_w3, m_ffn_conv_w, m_ffn_conv_b, m_ffn_w2, v_c_ctx, v_ada_w, v_ada_b, v_norm1_g, v_norm2_g, v_w_in, v_hgrn_lb_logits, v_hgrn_norm_g, v_na_q_norm_g, v_na_k_norm_g, v_na_rel_bias, v_w_branch_a, v_w_branch_b, v_w_out, v_ffn_w1, v_ffn_w3, v_ffn_conv_w, v_ffn_conv_b, v_ffn_w2):
    weights = dict(c_ctx=c_ctx, ada_w=ada_w, ada_b=ada_b, norm1_g=norm1_g, norm2_g=norm2_g, w_in=w_in,
                   hgrn_lb_logits=hgrn_lb_logits, hgrn_norm_g=hgrn_norm_g, na_q_norm_g=na_q_norm_g,
                   na_k_norm_g=na_k_norm_g, na_rel_bias=na_rel_bias, w_branch_a=w_branch_a, w_branch_b=w_branch_b,
                   w_out=w_out, ffn_w1=ffn_w1, ffn_w3=ffn_w3, ffn_conv_w=ffn_conv_w, ffn_conv_b=ffn_conv_b,
                   ffn_w2=ffn_w2)
    moms = dict(c_ctx=(m_c_ctx, v_c_ctx), ada_w=(m_ada_w, v_ada_w), ada_b=(m_ada_b, v_ada_b),
                norm1_g=(m_norm1_g, v_norm1_g), norm2_g=(m_norm2_g, v_norm2_g), w_in=(m_w_in, v_w_in),
                hgrn_lb_logits=(m_hgrn_lb_logits, v_hgrn_lb_logits), hgrn_norm_g=(m_hgrn_norm_g, v_hgrn_norm_g),
                na_q_norm_g=(m_na_q_norm_g, v_na_q_norm_g), na_k_norm_g=(m_na_k_norm_g, v_na_k_norm_g),
                na_rel_bias=(m_na_rel_bias, v_na_rel_bias), w_branch_a=(m_w_branch_a, v_w_branch_a),
                w_branch_b=(m_w_branch_b, v_w_branch_b), w_out=(m_w_out, v_w_out), ffn_w1=(m_ffn_w1, v_ffn_w1),
                ffn_w3=(m_ffn_w3, v_ffn_w3), ffn_conv_w=(m_ffn_conv_w, v_ffn_conv_w),
                ffn_conv_b=(m_ffn_conv_b, v_ffn_conv_b), ffn_w2=(m_ffn_w2, v_ffn_w2))
    order = list(weights)

    L, D = x.shape[1], x.shape[2]
    N = ctx.shape[1]
    T = N + L
    HA = w_branch_a.shape[1]
    HB = w_branch_b.shape[1]
    F = ffn_conv_b.shape[1]
    IN = 5 * HA + 3 * HB + 2 * D
    n_ada = ada_w.shape[2]
    ix, iy, ic = _pos()
    chip = 2 * ix + iy
    dev = 2 * chip + ic

    _PENDING.clear()
    pk0, offs0 = _pack([c[0], hgrn_lb_logits, ffn_conv_w[0]])
    g0 = _allgather8(pk0, "gather_small0")
    c_all = _unpack(g0, offs0, 0)
    lbl_parts = _unpack(g0, offs0, 1)
    lbl = jnp.concatenate([lbl_parts[2 * j] for j in range(N_CHIP)], axis=-1)
    cw_parts = _unpack(g0, offs0, 2)
    cw = jnp.concatenate([cw_parts[2 * j] for j in range(N_CHIP)], axis=-1)
    cw8 = jnp.pad(cw, ((0, 5), (0, 0)))

    cs = jnp.concatenate([c_all, c_ctx[None, :], jnp.zeros((7, D), F32)], axis=0)
    ada_b_mine = lax.dynamic_slice(ada_b, (0, chip * n_ada), (1, n_ada))
    mod_mine = _ada_fwd(cs, ada_w[0], ada_b_mine)
    gm = _allgather8(mod_mine, "gather_mod")
    mod = jnp.concatenate([gm[2 * j] for j in range(N_CHIP)], axis=-1)
    mod_l = lax.dynamic_slice(mod, (dev, 0), (1, N_MOD * D)).reshape(N_MOD, D)
    mod_c = mod[8].reshape(N_MOD, D)
    sh1, sc1, g1, sh2, sc2, g2 = [mod_l[i:i + 1] for i in range(N_MOD)]
    shift1 = jnp.concatenate([mod_c[0:1], sh1], axis=0)
    scale1 = jnp.concatenate([mod_c[1:2], sc1], axis=0)

    shards = [w_in[0], w_branch_a[0], w_branch_b[0], w_out[0], ffn_w1[0], ffn_w3[0], ffn_w2[0]]
    names = ["w_in", "w_a", "w_b", "w_out", "w1", "w3", "w2"]
    slots = [_cast_bf16_slot(s, "cast_" + nm) for s, nm in zip(shards, names)]
    sem_nb, win_buf = _xfer_start("gather_ici_start_in_nbr", slots[0:1], _plan_gather_ici(NEIGHBOURS), 2, gm)

    xall = jnp.concatenate([ctx[0], x[0]], axis=0)
    h_all = _rms1_fwd(xall, norm1_g, shift1, scale1, N)
    chip_i = chip.astype(jnp.int32)
    same = lambda ids: jnp.stack([jnp.stack(ids), jnp.stack(ids)])
    p = _mm_nn_sel(h_all, win_buf[0], same([chip_i]), F32, "mm_p_own")
    win_buf = _xfer_wait("gather_ici_wait_in_nbr", sem_nb, win_buf, _plan_gather_ici(NEIGHBOURS), p)
    sem_nb, win_buf = _xfer_start("gather_d2d_start_in_nbr", win_buf, _plan_gather_d2d(NEIGHBOURS), 2)
    sem_dg, win_buf = _xfer_start("gather_ici_start_in_diag", win_buf, _plan_gather_ici(DIAGONAL), 1)
    gat_mix = _gather_start("mix", slots[1:4])
    gat_ffn = _gather_start("ffn", slots[4:7])
    win_buf = _xfer_wait("gather_d2d_wait_in_nbr", sem_nb, win_buf, _plan_gather_d2d(NEIGHBOURS), _PENDING[0])
    p = _mm_nn_sel(h_all, win_buf[0], same([chip_i ^ 1, chip_i ^ 2]), F32, "mm_p_nbr", p)
    win_buf = _d2d_hand_over("in_diag", sem_dg, win_buf, DIAGONAL, p)
    p = _mm_nn_sel(h_all, win_buf[0], same([chip_i ^ 3]), F32, "mm_p_diag", p)
    Win = win_buf[0]
    gat_mix = _gather_mid(gat_mix, p)
    y_a, o_a, st_a = _hgrn_fwd(p, lbl, hgrn_norm_g, N, HA)
    Wa, Wb, Wo = _gather_finish(gat_mix, y_a)
    Wo = Wo.reshape(1, D, D)
    bias = _expand_bias(na_rel_bias[0])
    cos, sin = _rope_tables(L)
    off_na = 5 * HA
    y_b = _na_fwd(p, bias, na_q_norm_g, na_k_norm_g, cos, sin, N, off_na, HB)
    gat_ffn = _gather_mid(gat_ffn, (y_a, y_b))
    za = _mm_nn(y_a, Wa, BF16, "mm_za")
    zb = _mm_nn(y_b, Wb, BF16, "mm_zb")
    off_ga, off_gb = 5 * HA + 3 * HB, 5 * HA + 3 * HB + D
    z = _merge_fwd(za, zb, p, N, off_ga, off_gb)
    mo = _mm_nn(z, Wo, F32, "mm_mo")
    vec2 = jnp.concatenate([g1, norm2_g, sh2, sc2, jnp.zeros((4, D), F32)], axis=0)
    x_mid, h2 = _resid_rms2_fwd(x[0], mo, vec2)
    W1, W3, W2 = _gather_finish(gat_ffn, h2)
    W2 = W2.reshape(1, F, D)
    u1 = _mm_nn(h2, W1, BF16, "mm_u1")
    u3 = _mm_nn(h2, W3, BF16, "mm_u3")
    a = _convgate_fwd(u1, u3, cw8, ffn_conv_b)
    f = _mm_nn(a, W2, F32, "mm_f")
    dy, df, s_loss = _loss_head(x_mid, f, g2, loss_target[0])
    loss = lax.psum(s_loss[1, 0], ("x", "y", "c"))
    d_g2 = s_loss[0:1]

    gW2 = _mm_tn(a, df, 1, "mm_gw2").reshape(N_CHIP, F // N_CHIP, D)
    da = _mm_nt(df, W2, BF16, "mm_da")
    du1, du3, s_conv = _convgate_bwd(u1, u3, da, cw8, ffn_conv_b)
    gW1 = _mm_tn(h2, du1, N_CHIP, "mm_gw1")
    gW3 = _mm_tn(h2, du3, N_CHIP, "mm_gw3")
    rs_ffn = _rs_start("ffn", [gW2, gW1, gW3])
    dh2a = _mm_nt(du1, W1, F32, "mm_dh2a")
    dh2b = _mm_nt(du3, W3, F32, "mm_dh2b")
    rs_ffn = _rs_scatter(rs_ffn, dh2b)
    dxm, dmo, s_rms2 = _resid_rms2_bwd(x_mid, dh2a, dh2b, dy, mo, vec2)
    gWo = _mm_tn(z, dmo, 1, "mm_gwo").reshape(N_CHIP, D // N_CHIP, D)
    dz = _mm_nt(dmo, Wo, BF16, "mm_dz")
    dza, dzb, dga, dgb = _merge_bwd(dz, za, zb, p, N, off_ga, off_gb)
    gWa = _mm_tn(y_a, dza, N_CHIP, "mm_gwa")
    gWb = _mm_tn(y_b, dzb, N_CHIP, "mm_gwb")
    rs_mix = _rs_start("mix", [gWo, gWa, gWb])
    dya = _mm_nt(dza, Wa, F32, "mm_dya")
    dyb = _mm_nt(dzb, Wb, BF16, "mm_dyb")
    rs_mix = _rs_scatter(rs_mix, dyb)
    dq_a, dzf, dzbk, di_a, dog, dlbl, s_ng = _hgrn_bwd(p, lbl, hgrn_norm_g, o_a, dya, st_a, N, HA)
    rs_ffn = _rs_join(rs_ffn, dq_a)
    dq_n, dk_n, dv_n, dbias, s_qk = _na_bwd(p, bias, na_q_norm_g, na_k_norm_g, cos, sin, dyb, N, off_na, HB)
    rs_mix = _rs_join(rs_mix, dq_n)
    dp = jnp.concatenate([dq_a, dzf, dzbk, di_a, dog, dq_n, dk_n, dv_n, dga, dgb], axis=1)
    gWin = _mm_tn(h_all, dp, N_CHIP, "mm_gwin")
    rs_in = _rs_start("in", [gWin])
    rs_in = _rs_scatter(rs_in, _PENDING[0])
    dh = _mm_nt(dp, Win, F32, "mm_dh")
    grad_x, s_rms1 = _rms1_bwd(xall, dh, dxm, norm1_g, scale1, N)
    d_table = _bias_grad(dbias)

    grads = {}
    big_names = ["ada_w", "w_in", "w_branch_a", "w_branch_b", "w_out", "ffn_w1", "ffn_w3", "ffn_w2"]
    small_names = [n for n in order if n not in big_names]
    delta, new_m, new_v = {}, {}, {}

    def update(nm, after=None):
        reduced = nm != "ada_w"
        d_, m_, v_, *g_ = _adamw(weights[nm][0], grads[nm][0], moms[nm][0][0], moms[nm][1][0], "adamw_" + nm,
                                 after, copy_g=reduced)
        delta[nm], new_m[nm], new_v[nm] = d_[None], m_[None], v_[None]
        if reduced:
            grads[nm] = g_[0][None]
        return d_

    last = grad_x
    for nm, g in zip(["ffn_w2", "ffn_w1", "ffn_w3"], _rs_finish(rs_ffn, last)):
        grads[nm] = g[None]
        last = update(nm, last)
    for nm, g in zip(["w_out", "w_branch_a", "w_branch_b"], _rs_finish(rs_mix, last)):
        grads[nm] = g[None]
        last = update(nm, last)
    rs_in = _rs_join(rs_in, last)

    zD = jnp.zeros((1, D), F32)
    dmod_l = jnp.concatenate([s_rms1[2:3], s_rms1[3:4], s_rms2[3:4], s_rms2[0:1], s_rms2[1:2], d_g2], axis=0)
    dmod_c = jnp.concatenate([s_rms1[0:1], s_rms1[1:2], zD, zD, zD, zD], axis=0)
    pk1, offs1 = _pack([dmod_l, dmod_c, s_rms1[4], s_rms2[2], dlbl, s_ng[0], s_qk[0], s_qk[1], d_table,
                        s_conv[0:3], s_conv[3]])
    g1all = _allgather8(pk1, "gather_small1")
    tot1 = _sum8(g1all, "sum_small1")
    dmod_rows = _unpack(g1all, offs1, 0).reshape(N_DEV, N_MOD * D)
    dmod_c_tot = _unpack(tot1, offs1, 1).reshape(1, N_MOD * D)
    dmod16 = jnp.concatenate([dmod_rows, dmod_c_tot, jnp.zeros((7, N_MOD * D), F32)], axis=0)
    dmod16_mine = lax.dynamic_slice(dmod16, (0, chip * n_ada), (16, n_ada))
    g_ada_w, dact = _ada_bwd(cs, ada_w[0], dmod16_mine)
    pk2, offs2 = _pack([dact[8]])
    g2all = _allgather8(pk2, "gather_small2")
    dact_rows = _unpack(g2all, offs2, 0)
    dact_sel = jnp.concatenate([dact_rows[2 * j][None] for j in range(N_CHIP)] + [jnp.zeros((4, D), F32)], axis=0)

    grads["ada_w"] = g_ada_w[None]
    grads["ada_b"] =(_unpack(tot1, offs1, 0) + _unpack(tot1, offs1, 1)).reshape(1, N_MOD * D)
    grads["norm1_g"] = _unpack(tot1, offs1, 2)[None]
    grads["norm2_g"] = _unpack(tot1, offs1, 3)[None]
    g_lbl = _unpack(tot1, offs1, 4)
    n_lb = HA // N_CHIP
    grads["hgrn_lb_logits"] = lax.dynamic_slice(g_lbl, (0, 0, chip * n_lb), (2, 2, n_lb))
    grads["hgrn_norm_g"] = _unpack(tot1, offs1, 5)[None]
    grads["na_q_norm_g"] = _unpack(tot1, offs1, 6)[None]
    grads["na_k_norm_g"] = _unpack(tot1, offs1, 7)[None]
    grads["na_rel_bias"] = _unpack(tot1, offs1, 8)[None]
    g_cw = _unpack(tot1, offs1, 9)
    n_f = F // N_CHIP
    grads["ffn_conv_w"] = lax.dynamic_slice(g_cw, (0, chip * n_f), (3, n_f))[None]
    grads["ffn_conv_b"] = _unpack(tot1, offs1, 10)[None]

    g_c_ctx = _dsilu_rows(dact_sel, c_ctx[None, :], "grad_c_ctx")
    grads["c_ctx"] = g_c_ctx[0]

    last = update("ada_w", g_c_ctx)
    pw, offw = _pack([weights[n] for n in small_names])
    pg, _ = _pack([grads[n] for n in small_names])
    pm, _ = _pack([moms[n][0] for n in small_names])
    pv, _ = _pack([moms[n][1] for n in small_names])
    d_, m_, v_ = _adamw(pw, pg, pm, pv, "adamw_small", last)
    for i, nm in enumerate(small_names):
        delta[nm], new_m[nm], new_v[nm] = _unpack(d_, offw, i), _unpack(m_, offw, i), _unpack(v_, offw, i)
    grads["w_in"] = _rs_finish(rs_in, d_)[0][None]
    update("w_in")

    return (loss, grad_x[None], *[grads[n] for n in order], *[delta[n] for n in order],
            *[new_m[n] for n in order], *[new_v[n] for n in order])


def _dsilu_rows(v, cv, name):
    D = v.shape[1]

    def body(v_ref, c_ref, o_ref):
        t = c_ref[...]
        s = _sigmoid(t)
        o_ref[...] = (((v_ref[0:1, :] + v_ref[1:2, :]) + v_ref[2:3, :]) + v_ref[3:4, :]) * (s * (1.0 + t * (1.0 - s)))

    return _pcall(body, name=name, out_shape=jax.ShapeDtypeStruct((1, D), F32),
                          compiler_params=_params())(v, cv)
```

```python
import functools

import numpy as np
import jax
import jax.numpy as jnp
from jax import lax
from jax.experimental import pallas as pl
from jax.experimental.pallas import tpu as pltpu

F32 = jnp.float32
BF16 = jnp.bfloat16
MESH = pl.DeviceIdType.MESH

HEAD = 128
GRID_W = 64
WIN_R = 8
WIN_C = 16
ROPE_THETA = 10000.0
EPS = 1e-6
N_MOD = 6
CHUNK = 16
HGRN_UNROLL = 4
ADAM_LR = 0.001
ADAM_B1 = 0.9
ADAM_B2 = 0.999
ADAM_EPS = 1e-08
ADAM_WD = 0.01
ADAM_STEP = 10
NEG = -1e30
VMEM_LIMIT = 56 * 1024 * 1024
N_DEV = 8
N_CHIP = 4
HI = lax.Precision.HIGHEST


def _pick(n, cands):
    for c in cands:
        if n % c == 0:
            return c
    return n


def _row_tile(rows, cols, target_bytes=1 << 20):
    want = max(16, target_bytes // (4 * cols))
    for t in (512, 256, 128, 64, 32, 16, 8):
        if t <= want and rows % t == 0:
            return t
    return rows


def _params(sem=None):
    return pltpu.CompilerParams(dimension_semantics=sem, vmem_limit_bytes=VMEM_LIMIT)


def _dot(a, b):
    return jnp.dot(a, b, preferred_element_type=F32)


def _dot_nt(a, b):
    return lax.dot_general(a, b, (((1,), (1,)), ((), ())), preferred_element_type=F32)


def _dot_tn(a, b):
    return lax.dot_general(a, b, (((0,), (0,)), ((), ())), preferred_element_type=F32)


def _sigmoid(x):
    return 1.0 / (1.0 + jnp.exp(-x))


def _col_tile(n):
    return n if n <= 1536 else _pick(n, (1024, 768, 512, 384, 256, 128))


def _mm_nn(x, w3, out_dtype, name):
    M, K = x.shape
    S, _, n = w3.shape
    tm = _pick(M, (768, 512, 256, 128, 64))
    tn = _col_tile(n)
    nb = n // tn

    def body(x_ref, w_ref, o_ref):
        o_ref[...] = _dot(x_ref[...].astype(BF16), w_ref[0]).astype(o_ref.dtype)

    return _pcall(
        body, name=name, grid=(M // tm, S * nb),
        in_specs=[pl.BlockSpec((tm, K), lambda i, j: (i, 0)),
                  pl.BlockSpec((1, K, tn), lambda i, j: (j // nb, 0, j % nb))],
        out_specs=pl.BlockSpec((tm, tn), lambda i, j: (i, j)),
        out_shape=jax.ShapeDtypeStruct((M, S * n), out_dtype),
        compiler_params=_params(("parallel", "parallel")),
    )(x, w3)


def _mm_nn_sel(x, w3, sel, out_dtype, name, prev=None):
    M, K = x.shape
    S, _, n = w3.shape
    tm = _pick(M, (768, 512, 256, 128, 64))
    tn = _col_tile(n)
    nb = n // tn
    k = sel.shape[1]

    def body(sel_ref, x_ref, w_ref, *rest):
        rest[-1][...] = _dot(x_ref[...].astype(BF16), w_ref[0]).astype(out_dtype)

    in_specs = [pl.BlockSpec((tm, K), lambda i, j, sel_ref: (i, 0)),
                pl.BlockSpec((1, K, tn), lambda i, j, sel_ref: (sel_ref[0, j // nb], 0, j % nb))]
    operands = [sel, x, w3]
    if prev is not None:
        in_specs.append(_ANY)
        operands.append(prev)
    return pl.pallas_call(
        body, name=name,
        grid_spec=pltpu.PrefetchScalarGridSpec(
            num_scalar_prefetch=1, grid=(M // tm, k * nb), in_specs=in_specs,
            out_specs=pl.BlockSpec((tm, tn), lambda i, j, sel_ref: (i, sel_ref[1, j // nb] * nb + j % nb))),
        out_shape=jax.ShapeDtypeStruct((M, S * n), out_dtype),
        input_output_aliases={} if prev is None else {3: 0},
        compiler_params=_params(("parallel", "parallel")),
    )(*operands)


def _mm_nt(dy, w3, out_dtype, name):
    M = dy.shape[0]
    S, K, n = w3.shape
    tm = _pick(M, (768, 512, 256, 128, 64))
    tk = K if K <= 2048 else _pick(K, (1408, 1024, 512, 256, 128))
    tc = n if n <= 2048 else _col_tile(n)
    nb = n // tc
    nsteps = S * nb

    def body(dy_ref, w_ref, o_ref, acc_ref):
        s = pl.program_id(2)

        @pl.when(s == 0)
        def _():
            acc_ref[...] = jnp.zeros_like(acc_ref)

        acc_ref[...] += _dot_nt(dy_ref[...].astype(BF16), w_ref[0])

        @pl.when(s == nsteps - 1)
        def _():
            o_ref[...] = acc_ref[...].astype(o_ref.dtype)

    return _pcall(
        body, name=name, grid=(M // tm, K // tk, nsteps),
        in_specs=[pl.BlockSpec((tm, tc), lambda i, k, s: (i, s)),
                  pl.BlockSpec((1, tk, tc), lambda i, k, s: (s // nb, k, s % nb))],
        out_specs=pl.BlockSpec((tm, tk), lambda i, k, s: (i, k)),
        out_shape=jax.ShapeDtypeStruct((M, K), out_dtype),
        scratch_shapes=[pltpu.VMEM((tm, tk), F32)],
        compiler_params=_params(("parallel", "parallel", "arbitrary")),
    )(dy, w3)


def _mm_tn(x, dy, S, name):
    M, K = x.shape
    n = dy.shape[1] // S
    tk = _pick(K, (512, 256, 128))
    tn = _col_tile(n)
    nb = n // tn

    def body(x_ref, dy_ref, o_ref):
        o_ref[0] = _dot_tn(x_ref[...].astype(BF16), dy_ref[...].astype(BF16)).astype(BF16)

    return _pcall(
        body, name=name, grid=(S * nb, K // tk),
        in_specs=[pl.BlockSpec((M, tk), lambda j, k: (0, k)),
                  pl.BlockSpec((M, tn), lambda j, k: (0, j))],
        out_specs=pl.BlockSpec((1, tk, tn), lambda j, k: (j // nb, k, j % nb)),
        out_shape=jax.ShapeDtypeStruct((S, K, n), BF16),
        compiler_params=_params(("parallel", "parallel")),
    )(x, dy)


def _chip_index():
    return (2 * lax.axis_index("x") + lax.axis_index("y")).astype(jnp.int32).reshape(1)


def _cast_bf16_slot(w, name):
    R, C = w.shape
    tr = _row_tile(R, C, 2 << 20)

    def body(j_ref, w_ref, o_ref):
        o_ref[0] = w_ref[...].astype(BF16)

    return _pcall(
        body, name=name,
        grid_spec=pltpu.PrefetchScalarGridSpec(
            num_scalar_prefetch=1, grid=(R // tr,),
            in_specs=[pl.BlockSpec((tr, C), lambda i, j_ref: (i, 0))],
            out_specs=pl.BlockSpec((1, tr, C), lambda i, j_ref: (j_ref[0], i, 0))),
        out_shape=jax.ShapeDtypeStruct((N_CHIP, R, C), BF16),
        compiler_params=_params(("parallel",)),
    )(_chip_index(), w)


def _pos():
    return lax.axis_index("x"), lax.axis_index("y"), lax.axis_index("c")


def _other_chips(x, y):
    return [(x, 1 - y), (1 - x, y), (1 - x, 1 - y)]


def _allgather8(v, name):
    R, C = v.shape

    def body(x_ref, out_ref, send_sems, recv_sems, local_sem):
        x, y, c = _pos()
        me, sibling = (x, y, c), (x, y, 1 - c)
        chips = _other_chips(x, y)

        def slot(px, py, pc):
            return out_ref.at[4 * px + 2 * py + pc]

        def copy(k, block, to, src=None):
            return pltpu.make_async_remote_copy(
                src_ref=slot(*block) if src is None else src, dst_ref=slot(*block),
                send_sem=send_sems.at[k], recv_sem=recv_sems.at[k], device_id=to, device_id_type=MESH)

        mine = pltpu.make_async_copy(x_ref, slot(*me), local_sem)
        mine.start()
        first = [copy(0, me, sibling, src=x_ref)]
        first += [copy(1 + j, me, (*chip, c), src=x_ref) for j, chip in enumerate(chips)]
        for cp in first:
            cp.start()
        passed = [copy(4 + j, (*chip, c), sibling) for j, chip in enumerate(chips)]
        for j, chip in enumerate(chips):
            copy(1 + j, (*chip, c), me).wait_recv()
            passed[j].start()
        copy(0, sibling, me).wait_recv()
        for j, chip in enumerate(chips):
            copy(4 + j, (*chip, 1 - c), me).wait_recv()
        for cp in first + passed:
            cp.wait_send()
        mine.wait()

    return _pcall(
        body, name=name,
        out_shape=jax.ShapeDtypeStruct((N_DEV, R, C), v.dtype),
        in_specs=[pl.BlockSpec(memory_space=pltpu.VMEM)],
        out_specs=pl.BlockSpec(memory_space=pltpu.VMEM),
        scratch_shapes=[pltpu.SemaphoreType.DMA((7,)), pltpu.SemaphoreType.DMA((7,)), pltpu.SemaphoreType.DMA],
        compiler_params=pltpu.CompilerParams(vmem_limit_bytes=VMEM_LIMIT),
    )(v)


_HBM = pl.BlockSpec(memory_space=pltpu.HBM)
_SEM = pl.BlockSpec(memory_space=pltpu.SEMAPHORE)
_ANY = pl.BlockSpec(memory_space=pl.ANY)
_EFFECT = pltpu.SideEffectType.DATAFLOW_SIDE_EFFECTING
_PENDING = []


def _pcall(body, **kw):
    def run(*operands):
        if not _PENDING or "in_specs" not in kw:
            return pl.pallas_call(body, **kw)(*operands)
        deps = list(_PENDING)
        n = len(operands)

        def tied(*refs):
            return body(*refs[:n], *refs[n + len(deps):])

        return pl.pallas_call(tied, **{**kw, "in_specs": list(kw["in_specs"]) + [_ANY] * len(deps)})(*operands, *deps)
    return run


def _copies(plan, refs, send_sems, recv_sems):
    return [pltpu.make_async_remote_copy(src_ref=src, dst_ref=dst, send_sem=send_sems.at[k], recv_sem=recv_sems.at[k],
                                         device_id=dev, device_id_type=MESH)
            for k, (src, dst, dev) in enumerate(plan(refs))]


def _xfer_start(name, bufs, plan, n_copies, after=None):
    n = len(bufs)
    deps = list(_PENDING) + ([after] if after is not None else [])
    nd = len(deps)

    def body(*refs):
        for cp in _copies(plan, refs[:n], refs[n + nd], refs[n + nd + 1]):
            cp.start()
        refs[-1][...] = jnp.zeros_like(refs[-1])

    outs = pl.pallas_call(
        body, name=name,
        out_shape=(pltpu.SemaphoreType.DMA((n_copies,)), pltpu.SemaphoreType.DMA((n_copies,)),
                   *[pltpu.HBM(b.shape, b.dtype) for b in bufs], jax.ShapeDtypeStruct((8, 128), F32)),
        in_specs=[_HBM] * n + [_ANY] * nd,
        out_specs=(_SEM, _SEM, *[_HBM] * n, pl.BlockSpec(memory_space=pltpu.VMEM)),
        input_output_aliases={t: 2 + t for t in range(n)},
        compiler_params=pltpu.CompilerParams(has_side_effects=_EFFECT),
    )(*[pltpu.with_memory_space_constraint(b, pltpu.HBM) for b in bufs], *deps)
    _PENDING[:] = [outs[-1]]
    return (outs[0], outs[1]), list(outs[2:2 + n])


def _xfer_wait(name, sems, bufs, plan, after):
    n = len(bufs)
    after = tuple(after) if isinstance(after, (tuple, list)) else (after,)

    def body(*refs):
        cps = _copies(plan, refs[:n], refs[n], refs[n + 1])
        for cp in cps:
            cp.wait_send()
        for cp in cps:
            cp.wait_recv()

    outs = pl.pallas_call(
        body, name=name,
        out_shape=tuple(pltpu.HBM(b.shape, b.dtype) for b in bufs),
        in_specs=[_HBM] * n + [_SEM, _SEM] + [_ANY] * len(after),
        out_specs=tuple([_HBM] * n),
        input_output_aliases={t: t for t in range(n)},
        compiler_params=pltpu.CompilerParams(has_side_effects=_EFFECT),
    )(*bufs, sems[0], sems[1], *after)
    return list(outs)


def _half(ref_rows, hc):
    h = ref_rows // 2
    return pl.ds(hc * h, h)


ALL_CHIPS = (0, 1, 2)
NEIGHBOURS = (0, 1)
DIAGONAL = (2,)


def _plan_gather_ici(which):
    def plan(bufs):
        x, y, c = _pos()
        j = 2 * x + y
        chips = _other_chips(x, y)
        return [(b.at[j, _half(b.shape[1], c)], b.at[j, _half(b.shape[1], c)], (*chips[k], c))
                for b in bufs for k in which]
    return plan


def _plan_gather_d2d(which):
    def plan(bufs):
        x, y, c = _pos()
        chips = _other_chips(x, y)
        out = []
        for b in bufs:
            for k in which:
                blk = b.at[2 * chips[k][0] + chips[k][1], _half(b.shape[1], c)]
                out.append((blk, blk, (x, y, 1 - c)))
        return out
    return plan


def _plan_pair_swap(n):
    def plan(bufs):
        x, y, c = _pos()
        return [(g.at[:, _half(g.shape[1], 1 - c)], land, (x, y, 1 - c)) for g, land in zip(bufs[:n], bufs[n:])]
    return plan


def _plan_chip_scatter(n):
    def plan(bufs):
        x, y, c = _pos()
        return [(p.at[2 * chip[0] + chip[1]], land.at[k], (*chip, c))
                for p, land in zip(bufs[:n], bufs[n:]) for k, chip in enumerate(_other_chips(x, y))]
    return plan


def _plan_pair_join(bufs):
    x, y, c = _pos()
    return [(b.at[_half(b.shape[0], c)], b.at[_half(b.shape[0], c)], (x, y, 1 - c)) for b in bufs]


def _empty_hbm(shape, dtype):
    return pltpu.with_memory_space_constraint(lax.empty(shape, dtype), pltpu.HBM)


def _gather_start(tag, bufs, after=None):
    sems, bufs = _xfer_start(f"gather_ici_start_{tag}", bufs, _plan_gather_ici(ALL_CHIPS), 3 * len(bufs), after)
    return dict(tag=tag, sems=sems, bufs=bufs)


def _gather_mid(st, after):
    tag = st["tag"]
    bufs = _xfer_wait(f"gather_ici_wait_{tag}", st["sems"], st["bufs"], _plan_gather_ici(ALL_CHIPS), after)
    sems, bufs = _xfer_start(f"gather_d2d_start_{tag}", bufs, _plan_gather_d2d(ALL_CHIPS), 3 * len(bufs))
    return dict(tag=tag, sems=sems, bufs=bufs)


def _gather_finish(st, after):
    return _xfer_wait(f"gather_d2d_wait_{st['tag']}", st["sems"], st["bufs"], _plan_gather_d2d(ALL_CHIPS), after)


def _d2d_hand_over(tag, sems, bufs, which, after):
    bufs = _xfer_wait(f"gather_ici_wait_{tag}", sems, bufs, _plan_gather_ici(which), after)
    sems, bufs = _xfer_start(f"gather_d2d_start_{tag}", bufs, _plan_gather_d2d(which), len(which) * len(bufs))
    return _xfer_wait(f"gather_d2d_wait_{tag}", sems, bufs, _plan_gather_d2d(which), after)


def _pair_add(g, r, name):
    S, R, C = g.shape
    h = R // 2
    tr = _row_tile(h, C)
    nb = h // tr

    def body(c_ref, g_ref, r_ref, o_ref):
        o_ref[...] = (g_ref[...].astype(F32) + r_ref[...].astype(F32)).astype(BF16)

    return _pcall(
        body, name=name,
        grid_spec=pltpu.PrefetchScalarGridSpec(
            num_scalar_prefetch=1, grid=(S, nb),
            in_specs=[pl.BlockSpec((1, tr, C), lambda s, i, c_ref: (s, c_ref[0] * nb + i, 0)),
                      pl.BlockSpec((1, tr, C), lambda s, i, c_ref: (s, i, 0))],
            out_specs=pl.BlockSpec((1, tr, C), lambda s, i, c_ref: (s, i, 0))),
        out_shape=jax.ShapeDtypeStruct((S, h, C), BF16),
        compiler_params=_params(("parallel", "parallel")),
    )(lax.axis_index("c").astype(jnp.int32).reshape(1), g, r)


def _chip_sum(p, rb, name):
    S, h, C = p.shape
    tr = _row_tile(h, C)
    nb = h // tr
    jc = jnp.concatenate([_chip_index(), lax.axis_index("c").astype(jnp.int32).reshape(1)])

    def body(jc_ref, p_ref, r_ref, o_ref):
        o_ref[...] = ((p_ref[0].astype(F32) + r_ref[0].astype(F32)) + r_ref[1].astype(F32)) + r_ref[2].astype(F32)

    return _pcall(
        body, name=name,
        grid_spec=pltpu.PrefetchScalarGridSpec(
            num_scalar_prefetch=1, grid=(nb,),
            in_specs=[pl.BlockSpec((1, tr, C), lambda i, jc_ref: (jc_ref[0], i, 0)),
                      pl.BlockSpec((3, tr, C), lambda i, jc_ref: (0, i, 0))],
            out_specs=pl.BlockSpec((tr, C), lambda i, jc_ref: (jc_ref[1] * nb + i, 0))),
        out_shape=jax.ShapeDtypeStruct((2 * h, C), F32),
        compiler_params=_params(("parallel",)),
    )(jc, p, rb)


def _rs_start(tag, gs):
    n = len(gs)
    lands = [_empty_hbm((g.shape[0], g.shape[1] // 2, g.shape[2]), g.dtype) for g in gs]
    sems, bufs = _xfer_start(f"rs_swap_start_{tag}", list(gs) + lands, _plan_pair_swap(n), n)
    return dict(tag=tag, n=n, sems=sems, bufs=bufs)


def _rs_scatter(st, after):
    tag, n = st["tag"], st["n"]
    bufs = _xfer_wait(f"rs_swap_wait_{tag}", st["sems"], st["bufs"], _plan_pair_swap(n), after)
    ps = [_pair_add(g, r, f"rs_pair_add_{tag}{t}") for t, (g, r) in enumerate(zip(bufs[:n], bufs[n:]))]
    lands = [_empty_hbm((3,) + p.shape[1:], p.dtype) for p in ps]
    sems, bufs = _xfer_start(f"rs_scatter_start_{tag}", ps + lands, _plan_chip_scatter(n), 3 * n)
    return dict(tag=tag, n=n, sems=sems, bufs=bufs)


def _rs_join(st, after):
    tag, n = st["tag"], st["n"]
    bufs = _xfer_wait(f"rs_scatter_wait_{tag}", st["sems"], st["bufs"], _plan_chip_scatter(n), after)
    fs = [_chip_sum(p, rb, f"rs_chip_sum_{tag}{t}") for t, (p, rb) in enumerate(zip(bufs[:n], bufs[n:]))]
    sems, bufs = _xfer_start(f"rs_join_start_{tag}", fs, _plan_pair_join, n)
    return dict(tag=tag, n=n, sems=sems, bufs=bufs)


def _rs_finish(st, after):
    return _xfer_wait(f"rs_join_wait_{st['tag']}", st["sems"], st["bufs"], _plan_pair_join, after)


def _sum8(g, name):
    _, R, C = g.shape

    def body(g_ref, o_ref):
        acc = g_ref[0]
        for d in range(1, N_DEV):
            acc = acc + g_ref[d]
        o_ref[...] = acc

    return _pcall(body, name=name, out_shape=jax.ShapeDtypeStruct((R, C), F32),
                          compiler_params=_params())(g)


def _ada_fwd(cs, w, b):
    D, n = w.shape
    tn = _pick(n, (512, 384, 256, 128))

    def body(c_ref, w_ref, b_ref, o_ref):
        cv = c_ref[...]
        a = (cv * _sigmoid(cv)).astype(BF16)
        o_ref[...] = _dot(a, w_ref[...].astype(BF16)) + b_ref[...]

    return _pcall(
        body, name="ada_fwd", grid=(n // tn,),
        in_specs=[pl.BlockSpec((16, D), lambda j: (0, 0)), pl.BlockSpec((D, tn), lambda j: (0, j)),
                  pl.BlockSpec((1, tn), lambda j: (0, j))],
        out_specs=pl.BlockSpec((16, tn), lambda j: (0, j)),
        out_shape=jax.ShapeDtypeStruct((16, n), F32),
        compiler_params=_params(("parallel",)),
    )(cs, w, b)


def _ada_bwd(cs, w, dmod):
    D, n = w.shape
    tn = _pick(n, (512, 384, 256, 128))

    def body(c_ref, w_ref, d_ref, gw_ref, da_ref):
        j = pl.program_id(0)
        cv = c_ref[...]
        a = cv * _sigmoid(cv)
        d = d_ref[...]
        gw_ref[...] = lax.dot_general(a, d, (((0,), (0,)), ((), ())), precision=HI, preferred_element_type=F32)

        @pl.when(j == 0)
        def _():
            da_ref[...] = jnp.zeros_like(da_ref)

        da_ref[...] += _dot_nt(d.astype(BF16), w_ref[...].astype(BF16))

    return _pcall(
        body, name="ada_bwd", grid=(n // tn,),
        in_specs=[pl.BlockSpec((16, D), lambda j: (0, 0)), pl.BlockSpec((D, tn), lambda j: (0, j)),
                  pl.BlockSpec((16, tn), lambda j: (0, j))],
        out_specs=[pl.BlockSpec((D, tn), lambda j: (0, j)), pl.BlockSpec((16, D), lambda j: (0, 0))],
        out_shape=[jax.ShapeDtypeStruct((D, n), F32), jax.ShapeDtypeStruct((16, D), F32)],
        compiler_params=_params(("arbitrary",)),
    )(cs, w, dmod)


def _rms1_fwd(xall, gain, shift2, scale2, n_ctx):
    T, D = xall.shape
    tb = _pick(n_ctx, (256, 128, 64, 32, 16))
    nctx = n_ctx // tb

    def body(x_ref, g_ref, sh_ref, sc_ref, o_ref):
        i = pl.program_id(0)
        xv = x_ref[...]
        r = lax.rsqrt(jnp.mean(xv * xv, axis=-1, keepdims=True) + EPS)
        nrm = xv * r * g_ref[...]
        lat = i >= nctx
        sh = jnp.where(lat, sh_ref[1:2, :], sh_ref[0:1, :])
        sc = jnp.where(lat, sc_ref[1:2, :], sc_ref[0:1, :])
        o_ref[...] = (nrm * (1.0 + sc) + sh).astype(BF16)

    vec = lambda r: pl.BlockSpec((r, D), lambda i: (0, 0))
    return _pcall(
        body, name="rms1_fwd", grid=(T // tb,),
        in_specs=[pl.BlockSpec((tb, D), lambda i: (i, 0)), vec(1), vec(2), vec(2)],
        out_specs=pl.BlockSpec((tb, D), lambda i: (i, 0)),
        out_shape=jax.ShapeDtypeStruct((T, D), BF16),
        compiler_params=_params(("parallel",)),
    )(xall, gain, shift2, scale2)


def _rms1_bwd(xall, dh, dxmid, gain, scale2, n_ctx):
    T, D = xall.shape
    L = T - n_ctx
    tb = _pick(n_ctx, (256, 128, 64, 32, 16))
    nctx = n_ctx // tb

    def body(x_ref, dh_ref, dxm_ref, g_ref, sc_ref, dx_ref, cs_ref):
        i = pl.program_id(0)
        lat = i >= nctx
        xv = x_ref[...]
        r = lax.rsqrt(jnp.mean(xv * xv, axis=-1, keepdims=True) + EPS)
        xh = xv * r
        g = g_ref[...]
        nrm = xh * g
        sc = jnp.where(lat, sc_ref[1:2, :], sc_ref[0:1, :])
        dhv = dh_ref[...]
        dn = dhv * (1.0 + sc)
        dxh = dn * g
        dxv = r * (dxh - xh * jnp.mean(dxh * xh, axis=-1, keepdims=True))
        s_sh = jnp.sum(dhv, axis=0, keepdims=True)
        s_sc = jnp.sum(dhv * nrm, axis=0, keepdims=True)
        s_g = jnp.sum(dn * xh, axis=0, keepdims=True)
        zero = jnp.zeros_like(s_sh)
        rows = lax.broadcasted_iota(jnp.int32, (8, D), 0)
        upd = jnp.where(rows == 0, jnp.where(lat, zero, s_sh),
              jnp.where(rows == 1, jnp.where(lat, zero, s_sc),
              jnp.where(rows == 2, jnp.where(lat, s_sh, zero),
              jnp.where(rows == 3, jnp.where(lat, s_sc, zero),
              jnp.where(rows == 4, s_g, 0.0)))))

        @pl.when(i == 0)
        def _():
            cs_ref[...] = jnp.zeros_like(cs_ref)

        cs_ref[...] += upd

        @pl.when(lat)
        def _():
            dx_ref[...] = dxv + dxm_ref[...]

    lat_blk = lambda i: (jnp.maximum(i - nctx, 0), 0)
    vec = lambda r: pl.BlockSpec((r, D), lambda i: (0, 0))
    return _pcall(
        body, name="rms1_bwd", grid=(T // tb,),
        in_specs=[pl.BlockSpec((tb, D), lambda i: (i, 0)), pl.BlockSpec((tb, D), lambda i: (i, 0)),
                  pl.BlockSpec((tb, D), lat_blk), vec(1), vec(2)],
        out_specs=[pl.BlockSpec((tb, D), lat_blk), vec(8)],
        out_shape=[jax.ShapeDtypeStruct((L, D), F32), jax.ShapeDtypeStruct((8, D), F32)],
        compiler_params=_params(("arbitrary",)),
    )(xall, dh, dxmid, gain, scale2)


def _resid_rms2_fwd(x, mo, vecs):
    L, D = x.shape
    tb = _pick(L, (256, 128, 64))

    def body(x_ref, mo_ref, v_ref, xm_ref, h_ref):
        xm = x_ref[...] + v_ref[0:1, :] * mo_ref[...]
        xm_ref[...] = xm
        r = lax.rsqrt(jnp.mean(xm * xm, axis=-1, keepdims=True) + EPS)
        h_ref[...] = (xm * r * v_ref[1:2, :] * (1.0 + v_ref[3:4, :]) + v_ref[2:3, :]).astype(BF16)

    blk = pl.BlockSpec((tb, D), lambda i: (i, 0))
    return _pcall(
        body, name="resid_rms2_fwd", grid=(L // tb,),
        in_specs=[blk, blk, pl.BlockSpec((8, D), lambda i: (0, 0))],
        out_specs=[blk, blk],
        out_shape=[jax.ShapeDtypeStruct((L, D), F32), jax.ShapeDtypeStruct((L, D), BF16)],
        compiler_params=_params(("parallel",)),
    )(x, mo, vecs)


def _resid_rms2_bwd(xmid, dh_a, dh_b, dy, mo, vecs):
    L, D = xmid.shape
    tb = _pick(L, (256, 128, 64))

    def body(xm_ref, da_ref, db_ref, dy_ref, mo_ref, v_ref, dxm_ref, dmo_ref, cs_ref):
        i = pl.program_id(0)
        xm = xm_ref[...]
        r = lax.rsqrt(jnp.mean(xm * xm, axis=-1, keepdims=True) + EPS)
        xh = xm * r
        g = v_ref[1:2, :]
        nrm = xh * g
        dhv = da_ref[...] + db_ref[...]
        dn = dhv * (1.0 + v_ref[3:4, :])
        dxh = dn * g
        dxm = dy_ref[...] + r * (dxh - xh * jnp.mean(dxh * xh, axis=-1, keepdims=True))
        dxm_ref[...] = dxm
        dmo_ref[...] = (dxm * v_ref[0:1, :]).astype(BF16)
        s0 = jnp.sum(dhv, axis=0, keepdims=True)
        s1 = jnp.sum(dhv * nrm, axis=0, keepdims=True)
        s2 = jnp.sum(dn * xh, axis=0, keepdims=True)
        s3 = jnp.sum(dxm * mo_ref[...], axis=0, keepdims=True)
        rows = lax.broadcasted_iota(jnp.int32, (8, D), 0)
        upd = jnp.where(rows == 0, s0, jnp.where(rows == 1, s1, jnp.where(rows == 2, s2,
              jnp.where(rows == 3, s3, 0.0))))

        @pl.when(i == 0)
        def _():
            cs_ref[...] = jnp.zeros_like(cs_ref)

        cs_ref[...] += upd

    blk = pl.BlockSpec((tb, D), lambda i: (i, 0))
    vec = pl.BlockSpec((8, D), lambda i: (0, 0))
    return _pcall(
        body, name="resid_rms2_bwd", grid=(L // tb,),
        in_specs=[blk, blk, blk, blk, blk, vec],
        out_specs=[blk, blk, vec],
        out_shape=[jax.ShapeDtypeStruct((L, D), F32), jax.ShapeDtypeStruct((L, D), BF16),
                   jax.ShapeDtypeStruct((8, D), F32)],
        compiler_params=_params(("arbitrary",)),
    )(xmid, dh_a, dh_b, dy, mo, vecs)


def _loss_head(xmid, f, g2, target):
    L, D = xmid.shape
    tb = _pick(L, (256, 128, 64))

    def body(xm_ref, f_ref, g_ref, t_ref, dy_ref, df_ref, s_ref):
        i = pl.program_id(0)
        fv = f_ref[...]
        g = g_ref[...]
        err = xm_ref[...] + g * fv - t_ref[...]
        dy = err * (1.0 / D)
        dy_ref[...] = dy
        df_ref[...] = (dy * g).astype(BF16)
        s0 = jnp.sum(dy * fv, axis=0, keepdims=True)
        part = 0.5 * jnp.sum(jnp.mean(err * err, axis=-1, keepdims=True), axis=0, keepdims=True)
        rows = lax.broadcasted_iota(jnp.int32, (8, D), 0)
        upd = jnp.where(rows == 0, s0, jnp.where(rows == 1, part, 0.0))

        @pl.when(i == 0)
        def _():
            s_ref[...] = jnp.zeros_like(s_ref)

        s_ref[...] += upd

    blk = pl.BlockSpec((tb, D), lambda i: (i, 0))
    return _pcall(
        body, name="loss_head", grid=(L // tb,),
        in_specs=[blk, blk, pl.BlockSpec((1, D), lambda i: (0, 0)), blk],
        out_specs=[blk, blk, pl.BlockSpec((8, D), lambda i: (0, 0))],
        out_shape=[jax.ShapeDtypeStruct((L, D), F32), jax.ShapeDtypeStruct((L, D), BF16),
                   jax.ShapeDtypeStruct((8, D), F32)],
        compiler_params=_params(("arbitrary",)),
    )(xmid, f, g2, target)


def _gate_cols(D, off):
    tc = _pick(np.gcd(D, off), (512, 256, 128))
    return tc, off // tc


def _merge_fwd(za, zb, p, n_ctx, off_a, off_b):
    L, D = za.shape
    tb = _pick(n_ctx, (256, 128, 64, 32, 16))
    nctx = n_ctx // tb
    tc, oa = _gate_cols(D, off_a)
    _, ob = _gate_cols(D, off_b)
    if off_b % tc:
        raise ValueError("gate column offsets must share a column tile")
    ob = off_b // tc

    def body(za_ref, zb_ref, ga_ref, gb_ref, z_ref):
        z_ref[...] = (_sigmoid(ga_ref[...]) * za_ref[...].astype(F32)
                      + _sigmoid(gb_ref[...]) * zb_ref[...].astype(F32)).astype(BF16)

    blk = pl.BlockSpec((tb, tc), lambda i, j: (i, j))
    return _pcall(
        body, name="merge_fwd", grid=(L // tb, D // tc),
        in_specs=[blk, blk, pl.BlockSpec((tb, tc), lambda i, j: (i + nctx, oa + j)),
                  pl.BlockSpec((tb, tc), lambda i, j: (i + nctx, ob + j))],
        out_specs=blk,
        out_shape=jax.ShapeDtypeStruct((L, D), BF16),
        compiler_params=_params(("parallel", "parallel")),
    )(za, zb, p, p)


def _merge_bwd(dz, za, zb, p, n_ctx, off_a, off_b):
    L, D = za.shape
    T = L + n_ctx
    tb = _pick(n_ctx, (256, 128, 64, 32, 16))
    nctx = n_ctx // tb
    tc = _gate_cols(D, off_a)[0]
    oa, ob = off_a // tc, off_b // tc

    def body(dz_ref, za_ref, zb_ref, ga_ref, gb_ref, dza_ref, dzb_ref, dga_ref, dgb_ref):
        i = pl.program_id(1)

        @pl.when(i < nctx)
        def _():
            dga_ref[...] = jnp.zeros_like(dga_ref)
            dgb_ref[...] = jnp.zeros_like(dgb_ref)

        @pl.when(i >= nctx)
        def _():
            dzv = dz_ref[...].astype(F32)
            sa = _sigmoid(ga_ref[...])
            sb = _sigmoid(gb_ref[...])
            dza_ref[...] = (dzv * sa).astype(BF16)
            dzb_ref[...] = (dzv * sb).astype(BF16)
            dga_ref[...] = (dzv * za_ref[...].astype(F32) * sa * (1.0 - sa)).astype(BF16)
            dgb_ref[...] = (dzv * zb_ref[...].astype(F32) * sb * (1.0 - sb)).astype(BF16)

    lat = pl.BlockSpec((tb, tc), lambda j, i: (jnp.maximum(i - nctx, 0), j))
    allr = pl.BlockSpec((tb, tc), lambda j, i: (i, j))
    return _pcall(
        body, name="merge_bwd", grid=(D // tc, T // tb),
        in_specs=[lat, lat, lat, pl.BlockSpec((tb, tc), lambda j, i: (i, oa + j)),
                  pl.BlockSpec((tb, tc), lambda j, i: (i, ob + j))],
        out_specs=[lat, lat, allr, allr],
        out_shape=[jax.ShapeDtypeStruct((L, D), BF16), jax.ShapeDtypeStruct((L, D), BF16),
                   jax.ShapeDtypeStruct((T, D), BF16), jax.ShapeDtypeStruct((T, D), BF16)],
        compiler_params=_params(("arbitrary", "arbitrary")),
    )(dz, za, zb, p, p)


def _shift_down(u, rows):
    return jnp.where(rows == 0, 0.0, pltpu.roll(u, 1, 0))


def _shift_up(u, rows):
    n = u.shape[0]
    return jnp.where(rows == n - 1, 0.0, pltpu.roll(u, n - 1, 0))


def _convgate_fwd(u1, u3, cw, cb):
    L, F = u1.shape
    tc = _pick(F, (256, 128))

    def body(u1_ref, u3_ref, w_ref, b_ref, a_ref):
        u = u1_ref[...].astype(F32)
        rows = lax.broadcasted_iota(jnp.int32, u.shape, 0)
        cv = _shift_down(u, rows) * w_ref[0:1, :] + u * w_ref[1:2, :] + _shift_up(u, rows) * w_ref[2:3, :] + b_ref[...]
        a_ref[...] = (cv * _sigmoid(cv) * u3_ref[...].astype(F32)).astype(BF16)

    blk = pl.BlockSpec((L, tc), lambda j: (0, j))
    return _pcall(
        body, name="convgate_fwd", grid=(F // tc,),
        in_specs=[blk, blk, pl.BlockSpec((8, tc), lambda j: (0, j)), pl.BlockSpec((1, tc), lambda j: (0, j))],
        out_specs=blk,
        out_shape=jax.ShapeDtypeStruct((L, F), BF16),
        compiler_params=_params(("parallel",)),
    )(u1, u3, cw, cb)


def _convgate_bwd(u1, u3, da, cw, cb):
    L, F = u1.shape
    tc = _pick(F, (256, 128))

    def body(u1_ref, u3_ref, da_ref, w_ref, b_ref, du1_ref, du3_ref, s_ref):
        u = u1_ref[...].astype(F32)
        rows = lax.broadcasted_iota(jnp.int32, u.shape, 0)
        um, up = _shift_down(u, rows), _shift_up(u, rows)
        w0, w1, w2 = w_ref[0:1, :], w_ref[1:2, :], w_ref[2:3, :]
        cv = um * w0 + u * w1 + up * w2 + b_ref[...]
        s = _sigmoid(cv)
        dav = da_ref[...].astype(F32)
        du3_ref[...] = (dav * cv * s).astype(BF16)
        dcv = dav * u3_ref[...].astype(F32) * (s * (1.0 + cv * (1.0 - s)))
        du1_ref[...] = (_shift_up(dcv, rows) * w0 + dcv * w1 + _shift_down(dcv, rows) * w2).astype(BF16)
        r8 = lax.broadcasted_iota(jnp.int32, (8, tc), 0)
        s0 = jnp.sum(dcv * um, axis=0, keepdims=True)
        s1 = jnp.sum(dcv * u, axis=0, keepdims=True)
        s2 = jnp.sum(dcv * up, axis=0, keepdims=True)
        s3 = jnp.sum(dcv, axis=0, keepdims=True)
        s_ref[...] = jnp.where(r8 == 0, s0, jnp.where(r8 == 1, s1, jnp.where(r8 == 2, s2,
                     jnp.where(r8 == 3, s3, 0.0))))

    blk = pl.BlockSpec((L, tc), lambda j: (0, j))
    v8 = pl.BlockSpec((8, tc), lambda j: (0, j))
    return _pcall(
        body, name="convgate_bwd", grid=(F // tc,),
        in_specs=[blk, blk, blk, v8, pl.BlockSpec((1, tc), lambda j: (0, j))],
        out_specs=[blk, blk, v8],
        out_shape=[jax.ShapeDtypeStruct((L, F), BF16), jax.ShapeDtypeStruct((L, F), BF16),
                   jax.ShapeDtypeStruct((8, F), F32)],
        compiler_params=_params(("parallel",)),
    )(u1, u3, da, cw, cb)


def _lower_bound(lbl_ref, d):
    l0, l1 = lbl_ref[d, 0:1, :], lbl_ref[d, 1:2, :]
    m = jnp.maximum(l0, l1)
    e0, e1 = jnp.exp(l0 - m), jnp.exp(l1 - m)
    return e0 / (e0 + e1)


def _chunk_cumsum(x, rev):
    n = x.shape[0]
    r = lax.broadcasted_iota(jnp.int32, x.shape, 0) % CHUNK
    k = 1
    while k < CHUNK:
        if rev:
            x = x + jnp.where(r < CHUNK - k, pltpu.roll(x, n - k, 0), 0.0)
        else:
            x = x + jnp.where(r >= k, pltpu.roll(x, k, 0), 0.0)
        k *= 2
    return x


def _gate_terms(z, lb):
    sg = _sigmoid(z)
    f = lb + (1.0 - lb) * sg
    return sg, f


def _decay_terms(z, lb, rev):
    _, f = _gate_terms(z, lb)
    g = jnp.log(f)
    return 1.0 - f, _chunk_cumsum(g, rev), _chunk_cumsum(g, not rev) - g


def _chunk_total(c, rev):
    return c[0:1, :] if rev else c[CHUNK - 1:CHUNK, :]


def _pair_decay(c, s, rev):
    t = lax.broadcasted_iota(jnp.int32, (CHUNK, 1), 0)
    later = (t <= s) if rev else (t >= s)
    return jnp.where(later, jnp.exp(c - c[s:s + 1, :]), 0.0)


def _scan_chunk(i, n_ctx_chunks, n_chunks, rev):
    if not rev:
        return i
    return jnp.where(i < n_ctx_chunks, n_ctx_chunks - 1 - i, n_chunks + n_ctx_chunks - 1 - i)


def _rows(ci):
    return pl.ds(pl.multiple_of(ci * CHUNK, CHUNK), CHUNK)


def _hgrn_cols(HA):
    return HA // HEAD


def _hgrn_fwd(p, lbl, ng, n_ctx, HA):
    T = p.shape[0]
    L = T - n_ctx
    nh = _hgrn_cols(HA)
    nc, ncc = T // CHUNK, n_ctx // CHUNK

    def body(q_ref, zf_ref, zb_ref, v_ref, og_ref, lbl_ref, ng_ref, ya_ref, o_ref, st_ref,
             c_scr, k_scr, qe_scr, ke_scr, o_scr):
        dirs = ((0, False, zf_ref), (1, True, zb_ref))
        for d, rev, z_ref in dirs:
            k, c, rest = _decay_terms(z_ref[...], _lower_bound(lbl_ref, d), rev)
            c_scr[d] = c
            k_scr[d] = k
            qe_scr[d] = (q_ref[...] * jnp.exp(c)).astype(BF16)
            ke_scr[d] = (k * jnp.exp(rest)).astype(BF16)

        def step(i2, states):
            states = list(states)
            for u in range(HGRN_UNROLL):
                for d, rev, _ in dirs:
                    St = states[d]
                    ci = _scan_chunk(HGRN_UNROLL * i2 + u, ncc, nc, rev)
                    rows = _rows(ci)
                    q, v, c, k = q_ref[rows, :], v_ref[rows, :], c_scr[d, rows, :], k_scr[d, rows, :]
                    st_ref[0, d, ci] = St.astype(BF16)
                    o = jnp.zeros((CHUNK, HEAD), F32)
                    for s in range(CHUNK):
                        E = _pair_decay(c, s, rev)
                        a = jnp.sum(q * E * k[s:s + 1, :], axis=1, keepdims=True)
                        o = o + a * v[s:s + 1, :]
                    o_scr[d, rows, :] = o + _dot_nt(qe_scr[d, rows, :], St.astype(BF16))
                    states[d] = St * jnp.exp(_chunk_total(c, rev)) + _dot_tn(v.astype(BF16), ke_scr[d, rows, :])
            return tuple(states)

        if nc % HGRN_UNROLL:
            raise ValueError("the number of chunks must be a multiple of HGRN_UNROLL")
        zero = jnp.zeros((HEAD, HEAD), F32)
        lax.fori_loop(0, nc // HGRN_UNROLL, step, (zero, zero))

        o = o_scr[0, pl.ds(n_ctx, L), :] + o_scr[1, pl.ds(n_ctx, L), :]
        o_ref[...] = o
        r = lax.rsqrt(jnp.mean(o * o, axis=-1, keepdims=True) + EPS)
        og = og_ref[pl.ds(n_ctx, L), :]
        ya_ref[...] =(o * r * ng_ref[...] * (og * _sigmoid(og))).astype(BF16)

    cb = HA // HEAD
    col = lambda kk: pl.BlockSpec((T, HEAD), lambda h: (0, kk * cb + h))
    return _pcall(
        body, name="hgrn_fwd", grid=(nh,),
        in_specs=[col(0), col(1), col(2), col(3), col(4),
                  pl.BlockSpec((2, 2, HEAD), lambda h: (0, 0, h)), pl.BlockSpec((1, HEAD), lambda h: (0, 0))],
        out_specs=[pl.BlockSpec((L, HEAD), lambda h: (0, h)), pl.BlockSpec((L, HEAD), lambda h: (0, h)),
                   pl.BlockSpec((1, 2, nc, HEAD, HEAD), lambda h: (h, 0, 0, 0, 0))],
        out_shape=[jax.ShapeDtypeStruct((L, HA), BF16), jax.ShapeDtypeStruct((L, HA), F32),
                   jax.ShapeDtypeStruct((nh, 2, nc, HEAD, HEAD), BF16)],
        scratch_shapes=[pltpu.VMEM((2, T, HEAD), F32), pltpu.VMEM((2, T, HEAD), F32),
                        pltpu.VMEM((2, T, HEAD), BF16), pltpu.VMEM((2, T, HEAD), BF16),
                        pltpu.VMEM((2, T, HEAD), F32)],
        compiler_params=_params(("parallel",)),
    )(p, p, p, p, p, lbl, ng)


def _hgrn_bwd(p, lbl, ng, o, dya, st, n_ctx, HA):
    T = p.shape[0]
    L = T - n_ctx
    nh = _hgrn_cols(HA)
    nc, ncc = T // CHUNK, n_ctx // CHUNK

    def body(q_ref, zf_ref, zb_ref, v_ref, og_ref, lbl_ref, ng_ref, o_ref, dya_ref, st_ref,
             dq_ref, dzf_ref, dzb_ref, dv_ref, dog_ref, dlbl_ref, dng_ref,
             do_scr, c_scr, k_scr, qe_scr, ke_scr, dg_scr, dk_scr, dq_scr, dv_scr, row_scr):
        h = pl.program_id(0)
        ov = o_ref[...]
        r = lax.rsqrt(jnp.mean(ov * ov, axis=-1, keepdims=True) + EPS)
        oh = ov * r
        ogv = og_ref[pl.ds(n_ctx, L), :]
        sg_o = _sigmoid(ogv)
        dyv = dya_ref[...]
        ngv = ng_ref[...]
        dog_ref[pl.ds(0, n_ctx), :] = jnp.zeros((n_ctx, HEAD), BF16)
        dog_ref[pl.ds(n_ctx, L), :] = (dyv * oh * ngv * (sg_o * (1.0 + ogv * (1.0 - sg_o)))).astype(BF16)
        don = dyv * (ogv * sg_o)
        dng = jnp.sum(don * oh, axis=0, keepdims=True)
        doh = don * ngv
        do_scr[pl.ds(0, n_ctx), :] = jnp.zeros((n_ctx, HEAD), F32)
        do_scr[pl.ds(n_ctx, L), :] = r * (doh - oh * jnp.mean(doh * oh, axis=-1, keepdims=True))

        @pl.when(h == 0)
        def _():
            dng_ref[...] = jnp.zeros_like(dng_ref)

        dng_ref[0:1, :] += dng

        t16 = lax.broadcasted_iota(jnp.int32, (CHUNK, HEAD), 0)
        dirs = ((0, False, zf_ref, dzf_ref), (1, True, zb_ref, dzb_ref))
        for d, rev, z_ref, _ in dirs:
            k, c, rest = _decay_terms(z_ref[...], _lower_bound(lbl_ref, d), rev)
            c_scr[d] = c
            k_scr[d] = k
            qe_scr[d] = (q_ref[...] * jnp.exp(c)).astype(BF16)
            ke_scr[d] = (k * jnp.exp(rest)).astype(BF16)
        dq_scr[...] = jnp.zeros_like(dq_scr)
        dv_scr[...] = jnp.zeros_like(dv_scr)

        zero = jnp.zeros((HEAD, HEAD), F32)

        def bwd_chunk(i, carry, u):
            new = []
            for (d, rev, _, _), dSt in zip(dirs, carry):
                ci = _scan_chunk(i, ncc, nc, rev)
                rows = _rows(ci)
                q, v, do = q_ref[rows, :], v_ref[rows, :], do_scr[rows, :]
                c, k = c_scr[d, rows, :], k_scr[d, rows, :]
                tot = _chunk_total(c, rev)
                etot = jnp.exp(tot)
                St = st_ref[0, d, ci]
                dSb = dSt.astype(BF16)
                do_b = do.astype(BF16)
                dq_x = _dot(do_b, St) * jnp.exp(c)
                dk_x = _dot(v.astype(BF16), dSb) * jnp.exp(tot - c)
                dv_x = _dot_nt(ke_scr[d, rows, :], dSb)
                dtot = (jnp.sum(St.astype(F32) * dSt, axis=0, keepdims=True) * etot
                        + jnp.sum(k * dk_x, axis=0, keepdims=True))
                dq = jnp.zeros((CHUNK, HEAD), F32)
                for s in range(CHUNK):
                    E = _pair_decay(c, s, rev)
                    XE = E * k[s:s + 1, :]
                    a = jnp.sum(q * XE, axis=1, keepdims=True)
                    da = jnp.sum(do * v[s:s + 1, :], axis=1, keepdims=True)
                    dq = dq + da * XE
                    row_scr[u, d, 0, s:s + 1, :] = jnp.sum(da * q * E, axis=0, keepdims=True)
                    row_scr[u, d, 1, s:s + 1, :] = jnp.sum(a * do, axis=0, keepdims=True)
                dq, dk, dv = dq + dq_x, row_scr[u, d, 0] + dk_x, row_scr[u, d, 1] + dv_x
                dg_scr[d, rows, :] = _chunk_cumsum(q * dq - k * dk, not rev) + dtot
                dk_scr[d, rows, :] = dk
                dq_scr[rows, :] += dq
                dv_scr[rows, :] += dv
                new.append(dSt * etot + _dot_tn(do_b, qe_scr[d, rows, :]))
            return tuple(new)

        def bwd_step(i2, carry):
            for u in range(2):
                carry = bwd_chunk(nc - 1 - (2 * i2 + u), carry, u)
            return carry

        lax.fori_loop(0, nc // 2, bwd_step, (zero, zero))

        for d, _, z_ref, dz_ref in dirs:
            lb = _lower_bound(lbl_ref, d)
            sg, f = _gate_terms(z_ref[...], lb)
            df = dg_scr[d] / f - dk_scr[d]
            dz_ref[...] = (df * (1.0 - lb) * sg * (1.0 - sg)).astype(BF16)
            dl0 = jnp.sum(df * (1.0 - sg), axis=0, keepdims=True) * lb * (1.0 - lb)
            dlbl_ref[d, 0:1, :] = dl0
            dlbl_ref[d, 1:2, :] = -dl0
        dq_ref[...] = dq_scr[...].astype(BF16)
        dv_ref[...] = dv_scr[...].astype(BF16)

    cb = HA // HEAD
    col = lambda kk: pl.BlockSpec((T, HEAD), lambda h: (0, kk * cb + h))
    tcol = pl.BlockSpec((T, HEAD), lambda h: (0, h))
    lcol = pl.BlockSpec((L, HEAD), lambda h: (0, h))
    outs = _pcall(
        body, name="hgrn_bwd", grid=(nh,),
        in_specs=[col(0), col(1), col(2), col(3), col(4),
                  pl.BlockSpec((2, 2, HEAD), lambda h: (0, 0, h)), pl.BlockSpec((1, HEAD), lambda h: (0, 0)),
                  lcol, lcol,
                  pl.BlockSpec((1, 2, nc, HEAD, HEAD), lambda h: (h, 0, 0, 0, 0), pipeline_mode=pl.Buffered(1))],
        out_specs=[tcol, tcol, tcol, tcol, tcol, pl.BlockSpec((2, 2, HEAD), lambda h: (0, 0, h)),
                   pl.BlockSpec((8, HEAD), lambda h: (0, 0))],
        out_shape=[jax.ShapeDtypeStruct((T, HA), BF16)] * 5 + [jax.ShapeDtypeStruct((2, 2, HA), F32),
                                                               jax.ShapeDtypeStruct((8, HEAD), F32)],
        scratch_shapes=[pltpu.VMEM((T, HEAD), F32),
                        pltpu.VMEM((2, T, HEAD), F32), pltpu.VMEM((2, T, HEAD), F32),
                        pltpu.VMEM((2, T, HEAD), BF16), pltpu.VMEM((2, T, HEAD), BF16),
                        pltpu.VMEM((2, T, HEAD), F32), pltpu.VMEM((2, T, HEAD), F32),
                        pltpu.VMEM((T, HEAD), F32), pltpu.VMEM((T, HEAD), F32),
                        pltpu.VMEM((2, 2, 2, CHUNK, HEAD), F32)],
        compiler_params=_params(("arbitrary",)),
    )(p, p, p, p, p, lbl, ng, o, dya, st)
    return outs


def _swap_halves(t, lane):
    q = HEAD // 4
    return jnp.where((lane % (2 * q)) < q, pltpu.roll(t, HEAD - q, 1), pltpu.roll(t, q, 1))


def _qk_norm(t, g):
    r = lax.rsqrt(jnp.mean(t * t, axis=-1, keepdims=True) + EPS)
    return t * r, r


def _rope(t, cos, sin, lane):
    return t * cos + _swap_halves(t, lane) * sin


def _qk_norm_bwd(dy, th, r, g):
    dth = dy * g
    return r * (dth - th * jnp.mean(dth * th, axis=-1, keepdims=True)), jnp.sum(dy * th, axis=0, keepdims=True)


def _rope_bwd(dy, cos, sin, lane):
    return dy * cos + _swap_halves(dy * sin, lane)


def _na_geometry(L):
    n_rows = L // GRID_W
    kr = min(WIN_R, n_rows)
    return n_rows, kr


def _na_prep(q_ref, k_ref, v_ref, gq_ref, gk_ref, cos_ref, sin_ref, qs, ks, vs, n_ctx, L):
    lane = lax.broadcasted_iota(jnp.int32, (L, HEAD), 1)
    cos, sin = cos_ref[...], sin_ref[...]
    qh, _ = _qk_norm(q_ref[pl.ds(n_ctx, L), :], None)
    qs[...] = _rope(qh * gq_ref[...], cos, sin, lane).astype(BF16)
    kh, _ = _qk_norm(k_ref[pl.ds(n_ctx, L), :], None)
    ks[pl.ds(n_ctx, L), :] = _rope(kh * gk_ref[...], cos, sin, lane).astype(BF16)
    kc, _ = _qk_norm(k_ref[pl.ds(0, n_ctx), :], None)
    ks[pl.ds(0, n_ctx), :] = (kc * gk_ref[...]).astype(BF16)
    vs[...] = v_ref[...].astype(BF16)


NA_RB = 4


def _na_band_rows(kr):
    return kr + NA_RB


def _na_scores(i, qs, ks, bias_ref, n_ctx, n_rows, kr):
    scale = HEAD ** -0.5
    kb = _na_band_rows(kr)
    rq = NA_RB * i
    r0 = jnp.clip(rq - WIN_R // 2, 0, n_rows - kb)
    qrows = pl.ds(pl.multiple_of(rq * GRID_W, NA_RB * GRID_W), NA_RB * GRID_W)
    krows = pl.ds(pl.multiple_of(n_ctx + r0 * GRID_W, GRID_W), kb * GRID_W)
    qv = qs[qrows, :]
    sb = _dot_nt(qv, ks[krows, :]) * scale
    band_row = lax.broadcasted_iota(jnp.int32, (GRID_W, kb * GRID_W), 1) // GRID_W
    parts, tiles = [], []
    for u in range(NA_RB):
        r_u = rq + u
        first = jnp.clip(r_u - WIN_R // 2, 0, n_rows - kr) - r0
        idx = [jnp.clip(r0 - r_u + (WIN_R - 1) + 2 * jj, 0, 2 * WIN_R - 1) for jj in range(kb // 2)]
        bias_u = jnp.concatenate([bias_ref[0, t] for t in idx], axis=1)
        inside = (band_row >= first) & (band_row < first + kr)
        parts.append(jnp.where(inside, sb[u * GRID_W:(u + 1) * GRID_W, :] + bias_u, NEG))
        tiles.append(idx)
    sb = jnp.concatenate(parts, axis=0)
    sc = _dot_nt(qv, ks[pl.ds(0, n_ctx), :]) * scale
    m = jnp.maximum(jnp.max(sb, axis=1, keepdims=True), jnp.max(sc, axis=1, keepdims=True))
    eb, ec = jnp.exp(sb - m), jnp.exp(sc - m)
    inv = 1.0 / (jnp.sum(eb, axis=1, keepdims=True) + jnp.sum(ec, axis=1, keepdims=True))
    return eb * inv, ec * inv, qrows, krows, tiles


def _na_fwd(p, bias, gq, gk, cos, sin, n_ctx, off, HB):
    T = p.shape[0]
    L = T - n_ctx
    nh = HB // HEAD
    n_rows, kr = _na_geometry(L)
    ob = off // HEAD

    def body(q_ref, k_ref, v_ref, bias_ref, gq_ref, gk_ref, cos_ref, sin_ref, y_ref, qs, ks, vs):
        _na_prep(q_ref, k_ref, v_ref, gq_ref, gk_ref, cos_ref, sin_ref, qs, ks, vs, n_ctx, L)

        def step(i, carry):
            pb, pc, qrows, krows, _ = _na_scores(i, qs, ks, bias_ref, n_ctx, n_rows, kr)
            y = _dot(pb.astype(BF16), vs[krows, :]) + _dot(pc.astype(BF16), vs[pl.ds(0, n_ctx), :])
            y_ref[qrows, :] = y.astype(BF16)
            return carry

        lax.fori_loop(0, n_rows // NA_RB, step, 0)

    col = lambda kk: pl.BlockSpec((T, HEAD), lambda h: (0, ob + kk * nh + h))
    vec = pl.BlockSpec((1, HEAD), lambda h: (0, 0))
    tab = pl.BlockSpec((L, HEAD), lambda h: (0, 0))
    return _pcall(
        body, name="na_fwd", grid=(nh,),
        in_specs=[col(0), col(1), col(2), pl.BlockSpec((1,) + bias.shape[1:], lambda h: (h, 0, 0, 0)),
                  vec, vec, tab, tab],
        out_specs=pl.BlockSpec((L, HEAD), lambda h: (0, h)),
        out_shape=jax.ShapeDtypeStruct((L, HB), BF16),
        scratch_shapes=[pltpu.VMEM((L, HEAD), BF16), pltpu.VMEM((T, HEAD), BF16), pltpu.VMEM((T, HEAD), BF16)],
        compiler_params=_params(("parallel",)),
    )(p, p, p, bias, gq, gk, cos, sin)


def _na_bwd(p, bias, gq, gk, cos, sin, dyb, n_ctx, off, HB):
    T = p.shape[0]
    L = T - n_ctx
    nh = HB // HEAD
    n_rows, kr = _na_geometry(L)
    ob = off // HEAD
    scale = HEAD ** -0.5

    def body(q_ref, k_ref, v_ref, bias_ref, gq_ref, gk_ref, cos_ref, sin_ref, dy_ref,
             dq_ref, dk_ref, dv_ref, dbias_ref, dg_ref, qs, ks, vs, dqa, dka, dva):
        h = pl.program_id(0)
        _na_prep(q_ref, k_ref, v_ref, gq_ref, gk_ref, cos_ref, sin_ref, qs, ks, vs, n_ctx, L)
        dka[...] = jnp.zeros_like(dka)
        dva[...] = jnp.zeros_like(dva)
        dbias_ref[...] = jnp.zeros_like(dbias_ref)

        crows = pl.ds(0, n_ctx)

        def step(i, carry):
            pb, pc, qrows, krows, tiles = _na_scores(i, qs, ks, bias_ref, n_ctx, n_rows, kr)
            do = dy_ref[qrows, :]
            qv = qs[qrows, :]
            dpb = _dot_nt(do, vs[krows, :])
            dpc = _dot_nt(do, vs[crows, :])
            delta = jnp.sum(pb * dpb, axis=1, keepdims=True) + jnp.sum(pc * dpc, axis=1, keepdims=True)
            dsb = pb * (dpb - delta)
            dsc = pc * (dpc - delta)
            dsb_b, dsc_b = dsb.astype(BF16), dsc.astype(BF16)
            dqa[qrows, :] = (_dot(dsb_b, ks[krows, :]) + _dot(dsc_b, ks[crows, :])) * scale
            dka[krows, :] += _dot_tn(dsb_b, qv) * scale
            dka[crows, :] += _dot_tn(dsc_b, qv) * scale
            dva[krows, :] += _dot_tn(pb.astype(BF16), do)
            dva[crows, :] += _dot_tn(pc.astype(BF16), do)
            for u, idx in enumerate(tiles):
                for jj, t in enumerate(idx):
                    dbias_ref[0, t] += dsb[u * GRID_W:(u + 1) * GRID_W, jj * 2 * GRID_W:(jj + 1) * 2 * GRID_W]
            return carry

        lax.fori_loop(0, n_rows // NA_RB, step, 0)

        lane = lax.broadcasted_iota(jnp.int32, (L, HEAD), 1)
        cos, sin = cos_ref[...], sin_ref[...]
        lat, ctx = pl.ds(n_ctx, L), pl.ds(0, n_ctx)
        gqv, gkv = gq_ref[...], gk_ref[...]
        qh, rq = _qk_norm(q_ref[lat, :], None)
        dq, dgq = _qk_norm_bwd(_rope_bwd(dqa[...], cos, sin, lane), qh, rq, gqv)
        dq_ref[ctx, :] = jnp.zeros((n_ctx, HEAD), BF16)
        dq_ref[lat, :] = dq.astype(BF16)
        kh, rk = _qk_norm(k_ref[lat, :], None)
        dk, dgk = _qk_norm_bwd(_rope_bwd(dka[lat, :], cos, sin, lane), kh, rk, gkv)
        dk_ref[lat, :] = dk.astype(BF16)
        kch, rkc = _qk_norm(k_ref[ctx, :], None)
        dkc, dgkc = _qk_norm_bwd(dka[ctx, :], kch, rkc, gkv)
        dk_ref[ctx, :] = dkc.astype(BF16)
        dv_ref[...] = dva[...].astype(BF16)

        @pl.when(h == 0)
        def _():
            dg_ref[...] = jnp.zeros_like(dg_ref)

        dg_ref[0:1, :] += dgq
        dg_ref[1:2, :] += dgk + dgkc

    col = lambda kk: pl.BlockSpec((T, HEAD), lambda h: (0, ob + kk * nh + h))
    vec = pl.BlockSpec((1, HEAD), lambda h: (0, 0))
    tab = pl.BlockSpec((L, HEAD), lambda h: (0, 0))
    tcol = pl.BlockSpec((T, HEAD), lambda h: (0, h))
    bspec = pl.BlockSpec((1,) + bias.shape[1:], lambda h: (h, 0, 0, 0))
    return _pcall(
        body, name="na_bwd", grid=(nh,),
        in_specs=[col(0), col(1), col(2), bspec, vec, vec, tab, tab, pl.BlockSpec((L, HEAD), lambda h: (0, h))],
        out_specs=[tcol, tcol, tcol, bspec, pl.BlockSpec((8, HEAD), lambda h: (0, 0))],
        out_shape=[jax.ShapeDtypeStruct((T, HB), BF16)] * 3 + [jax.ShapeDtypeStruct(bias.shape, F32),
                                                               jax.ShapeDtypeStruct((8, HEAD), F32)],
        scratch_shapes=[pltpu.VMEM((L, HEAD), BF16), pltpu.VMEM((T, HEAD), BF16), pltpu.VMEM((T, HEAD), BF16),
                        pltpu.VMEM((L, HEAD), F32), pltpu.VMEM((T, HEAD), F32), pltpu.VMEM((T, HEAD), F32)],
        compiler_params=_params(("arbitrary",)),
    )(p, p, p, bias, gq, gk, cos, sin, dyb)


def _bias_tables():
    w = np.arange(GRID_W)
    col_start = np.clip(w - WIN_C // 2, 0, GRID_W - WIN_C)
    col_in = (w[None, :] >= col_start[:, None]) & (w[None, :] < col_start[:, None] + WIN_C)
    dc = np.clip(w[None, :] - w[:, None], -(WIN_C - 1), WIN_C - 1) + WIN_C - 1
    n_pair = 2 * WIN_R
    ridx = np.zeros((n_pair, GRID_W, 2 * GRID_W), np.int32)
    cidx = np.zeros((n_pair, GRID_W, 2 * GRID_W), np.int32)
    valid = np.zeros((n_pair, GRID_W, 2 * GRID_W), bool)
    for i in range(n_pair):
        for half in range(2):
            row = i + half
            sl = slice(half * GRID_W, (half + 1) * GRID_W)
            ridx[i, :, sl] = min(row, 2 * WIN_R - 2)
            cidx[i, :, sl] = dc
            valid[i, :, sl] = col_in & (row <= 2 * WIN_R - 2)
    return ridx, cidx, valid


def _bias_onehot():
    _, cidx, valid = _bias_tables()
    K = GRID_W * 2 * GRID_W
    oh = np.zeros((K, 128), np.float32)
    neg = np.full((1, K), NEG, np.float32)
    for cq in range(GRID_W):
        for ll in range(2 * GRID_W):
            if valid[0, cq, ll]:
                oh[cq * 2 * GRID_W + ll, (ll // GRID_W) * 64 + cidx[0, cq, ll]] = 1.0
                neg[0, cq * 2 * GRID_W + ll] = 0.0
    return oh, neg


def _expand_bias(table):
    H = table.shape[0]
    n_pair, n_dc = 2 * WIN_R, 2 * WIN_C - 1
    tp = jnp.pad(table, ((0, 0), (0, n_pair + 1 - table.shape[1]), (0, 64 - n_dc)))
    t2 = jnp.concatenate([tp[:, :n_pair], tp[:, 1:n_pair + 1]], axis=-1).reshape(H * n_pair, 128)
    oh, neg = _bias_onehot()

    def body(t_ref, oh_ref, neg_ref, o_ref):
        o_ref[...] = lax.dot_general(t_ref[...], oh_ref[...], (((1,), (1,)), ((), ())), precision=HI,
                                     preferred_element_type=F32) + neg_ref[...]

    out = _pcall(body, name="bias_expand", out_shape=jax.ShapeDtypeStruct((H * n_pair, oh.shape[0]), F32),
                         compiler_params=_params())(t2, jnp.asarray(oh), jnp.asarray(neg))
    return out.reshape(H, n_pair, GRID_W, 2 * GRID_W)


def _bias_grad(dbias):
    H = dbias.shape[0]
    n_pair, n_dc = 2 * WIN_R, 2 * WIN_C - 1
    K = GRID_W * 2 * GRID_W
    oh, _ = _bias_onehot()
    flat = dbias.reshape(H * n_pair, K)

    def body(d_ref, oh_ref, o_ref):
        o_ref[...] = jnp.dot(d_ref[...], oh_ref[...], precision=HI, preferred_element_type=F32)

    g = _pcall(body, name="bias_grad", out_shape=jax.ShapeDtypeStruct((H * n_pair, 128), F32),
                       compiler_params=_params())(flat, jnp.asarray(oh))
    g = g.reshape(H, n_pair, 128)
    left, right = g[:, :, :n_dc], g[:, :, 64:64 + n_dc]
    out = left[:, :n_pair - 1]
    return out.at[:, 1:].add(right[:, :n_pair - 2])


def _rope_tables(L):
    pos = np.arange(L)
    row = (pos // GRID_W).astype(np.float32)
    colp = (pos % GRID_W).astype(np.float32)
    half = HEAD // 2
    nf = half // 2
    inv = (ROPE_THETA ** (-np.arange(nf, dtype=np.float32) / nf)).astype(np.float32)

    def tabs(pv):
        ang = pv[:, None] * inv[None, :]
        c, s = np.cos(ang), np.sin(ang)
        return np.concatenate([c, c], axis=1), np.concatenate([-s, s], axis=1)

    cr, sr = tabs(row)
    cc, sc = tabs(colp)
    return (jnp.asarray(np.concatenate([cr, cc], axis=1), F32), jnp.asarray(np.concatenate([sr, sc], axis=1), F32))


def _adamw(w, g, m, v, name, after=None, copy_g=False):
    R, C = w.shape
    tr = _row_tile(R, C)
    c1 = 1.0 - ADAM_B1 ** ADAM_STEP
    c2 = 1.0 - ADAM_B2 ** ADAM_STEP
    deps = [] if after is None else [after]
    n_out = 4 if copy_g else 3

    def body(w_ref, g_ref, m_ref, v_ref, *rest):
        d_ref, mo_ref, vo_ref = rest[len(deps):len(deps) + 3]
        gv = g_ref[...]
        mn = ADAM_B1 * m_ref[...] + (1.0 - ADAM_B1) * gv
        vn = ADAM_B2 * v_ref[...] + (1.0 - ADAM_B2) * (gv * gv)
        mo_ref[...] = mn
        vo_ref[...] = vn
        d_ref[...] = -ADAM_LR * ((mn / c1) / (jnp.sqrt(vn / c2) + ADAM_EPS) + ADAM_WD * w_ref[...])
        if copy_g:
            rest[-1][...] = gv

    blk = pl.BlockSpec((tr, C), lambda i: (i, 0))
    return _pcall(
        body, name=name, grid=(R // tr,),
        in_specs=[blk] * 4 + [_ANY] * len(deps), out_specs=[blk] * n_out,
        out_shape=[jax.ShapeDtypeStruct((R, C), F32)] * n_out,
        compiler_params=_params(("parallel",)),
    )(w, g, m, v, *deps)


PACK_W = 1024


def _pack(parts):
    flat, offs, pos = [], [], 0
    for a in parts:
        n = a.size
        padn = -n % PACK_W
        flat.append(jnp.pad(a.reshape(-1).astype(F32), (0, padn)))
        offs.append((pos, n, a.shape))
        pos += n + padn
    tail = -pos % (8 * PACK_W)
    if tail:
        flat.append(jnp.zeros((tail,), F32))
    return jnp.concatenate(flat).reshape(-1, PACK_W), offs


def _unpack(buf, offs, i):
    pos, n, shape = offs[i]
    return buf.reshape(buf.shape[:-2] + (-1,))[..., pos:pos + n].reshape(buf.shape[:-2] + shape)


def kernel(x, c, ctx, c_ctx, ada_w, ada_b, norm1_g, norm2_g, w_in, hgrn_lb_logits, hgrn_norm_g, na_q_norm_g, na_k_norm_g, na_rel_bias, w_branch_a, w_branch_b, w_out, ffn_w1, ffn_w3, ffn_conv_w, ffn_conv_b, ffn_w2, loss_target, m_c_ctx, m_ada_w, m_ada_b, m_norm1_g, m_norm2_g, m_w_in, m_hgrn_lb_logits, m_hgrn_norm_g, m_na_q_norm_g, m_na_k_norm_g, m_na_rel_bias, m_w_branch_a, m_w_branch_b, m_w_out, m_ffn_w1, m_ffn_w3, m_ffn_conv_w, m_ffn_conv_b, m_ffn_w2, v_c_ctx, v_ada_w, v_ada_b, v_norm1_g, v_norm2_g, v_w_in, v_hgrn_lb_logits, v_hgrn_norm_g, v_na_q_norm_g, v_na_k_norm_g, v_na_rel_bias, v_w_branch_a, v_w_branch_b, v_w_out, v_ffn_w1, v_ffn_w3, v_ffn_conv_w, v_ffn_conv_b, v_ffn_w2):
    weights = dict(c_ctx=c_ctx, ada_w=ada_w, ada_b=ada_b, norm1_g=norm1_g, norm2_g=norm2_g, w_in=w_in,
                   hgrn_lb_logits=hgrn_lb_logits, hgrn_norm_g=hgrn_norm_g, na_q_norm_g=na_q_norm_g,
                   na_k_norm_g=na_k_norm_g, na_rel_bias=na_rel_bias, w_branch_a=w_branch_a, w_branch_b=w_branch_b,
                   w_out=w_out, ffn_w1=ffn_w1, ffn_w3=ffn_w3, ffn_conv_w=ffn_conv_w, ffn_conv_b=ffn_conv_b,
                   ffn_w2=ffn_w2)
    moms = dict(c_ctx=(m_c_ctx, v_c_ctx), ada_w=(m_ada_w, v_ada_w), ada_b=(m_ada_b, v_ada_b),
                norm1_g=(m_norm1_g, v_norm1_g), norm2_g=(m_norm2_g, v_norm2_g), w_in=(m_w_in, v_w_in),
                hgrn_lb_logits=(m_hgrn_lb_logits, v_hgrn_lb_logits), hgrn_norm_g=(m_hgrn_norm_g, v_hgrn_norm_g),
                na_q_norm_g=(m_na_q_norm_g, v_na_q_norm_g), na_k_norm_g=(m_na_k_norm_g, v_na_k_norm_g),
                na_rel_bias=(m_na_rel_bias, v_na_rel_bias), w_branch_a=(m_w_branch_a, v_w_branch_a),
                w_branch_b=(m_w_branch_b, v_w_branch_b), w_out=(m_w_out, v_w_out), ffn_w1=(m_ffn_w1, v_ffn_w1),
                ffn_w3=(m_ffn_w3, v_ffn_w3), ffn_conv_w=(m_ffn_conv_w, v_ffn_conv_w),
                ffn_conv_b=(m_ffn_conv_b, v_ffn_conv_b), ffn_w2=(m_ffn_w2, v_ffn_w2))
    order = list(weights)

    L, D = x.shape[1], x.shape[2]
    N = ctx.shape[1]
    T = N + L
    HA = w_branch_a.shape[1]
    HB = w_branch_b.shape[1]
    F = ffn_conv_b.shape[1]
    IN = 5 * HA + 3 * HB + 2 * D
    n_ada = ada_w.shape[2]
    ix, iy, ic = _pos()
    chip = 2 * ix + iy
    dev = 2 * chip + ic

    _PENDING.clear()
    pk0, offs0 = _pack([c[0], hgrn_lb_logits, ffn_conv_w[0]])
    g0 = _allgather8(pk0, "gather_small0")
    c_all = _unpack(g0, offs0, 0)
    lbl_parts = _unpack(g0, offs0, 1)
    lbl = jnp.concatenate([lbl_parts[2 * j] for j in range(N_CHIP)], axis=-1)
    cw_parts = _unpack(g0, offs0, 2)
    cw = jnp.concatenate([cw_parts[2 * j] for j in range(N_CHIP)], axis=-1)
    cw8 = jnp.pad(cw, ((0, 5), (0, 0)))

    cs = jnp.concatenate([c_all, c_ctx[None, :], jnp.zeros((7, D), F32)], axis=0)
    ada_b_mine = lax.dynamic_slice(ada_b, (0, chip * n_ada), (1, n_ada))
    mod_mine = _ada_fwd(cs, ada_w[0], ada_b_mine)
    gm = _allgather8(mod_mine, "gather_mod")
    mod = jnp.concatenate([gm[2 * j] for j in range(N_CHIP)], axis=-1)
    mod_l = lax.dynamic_slice(mod, (dev, 0), (1, N_MOD * D)).reshape(N_MOD, D)
    mod_c = mod[8].reshape(N_MOD, D)
    sh1, sc1, g1, sh2, sc2, g2 = [mod_l[i:i + 1] for i in range(N_MOD)]
    shift1 = jnp.concatenate([mod_c[0:1], sh1], axis=0)
    scale1 = jnp.concatenate([mod_c[1:2], sc1], axis=0)

    shards = [w_in[0], w_branch_a[0], w_branch_b[0], w_out[0], ffn_w1[0], ffn_w3[0], ffn_w2[0]]
    names = ["w_in", "w_a", "w_b", "w_out", "w1", "w3", "w2"]
    slots = [_cast_bf16_slot(s, "cast_" + nm) for s, nm in zip(shards, names)]
    sem_nb, win_buf = _xfer_start("gather_ici_start_in_nbr", slots[0:1], _plan_gather_ici(NEIGHBOURS), 2, gm)

    xall = jnp.concatenate([ctx[0], x[0]], axis=0)
    h_all = _rms1_fwd(xall, norm1_g, shift1, scale1, N)
    chip_i = chip.astype(jnp.int32)
    same = lambda ids: jnp.stack([jnp.stack(ids), jnp.stack(ids)])
    p = _mm_nn_sel(h_all, win_buf[0], same([chip_i]), F32, "mm_p_own")
    win_buf = _xfer_wait("gather_ici_wait_in_nbr", sem_nb, win_buf, _plan_gather_ici(NEIGHBOURS),
                         (p, *slots[1:]))
    sem_nb, win_buf = _xfer_start("gather_d2d_start_in_nbr", win_buf, _plan_gather_d2d(NEIGHBOURS), 2)
    sem_dg, win_buf = _xfer_start("gather_ici_start_in_diag", win_buf, _plan_gather_ici(DIAGONAL), 1)
    gat_mix = _gather_start("mix", slots[1:4])
    gat_ffn = _gather_start("ffn", slots[4:7])
    win_buf = _xfer_wait("gather_d2d_wait_in_nbr", sem_nb, win_buf, _plan_gather_d2d(NEIGHBOURS), _PENDING[0])
    p = _mm_nn_sel(h_all, win_buf[0], same([chip_i ^ 1, chip_i ^ 2]), F32, "mm_p_nbr", p)
    win_buf = _d2d_hand_over("in_diag", sem_dg, win_buf, DIAGONAL, p)
    p = _mm_nn_sel(h_all, win_buf[0], same([chip_i ^ 3]), F32, "mm_p_diag", p)
    Win = win_buf[0]
    bias = _expand_bias(na_rel_bias[0])
    cos, sin = _rope_tables(L)
    off_na = 5 * HA
    y_b = _na_fwd(p, bias, na_q_norm_g, na_k_norm_g, cos, sin, N, off_na, HB)
    gat_mix = _gather_mid(gat_mix, y_b)
    y_a, o_a, st_a = _hgrn_fwd(p, lbl, hgrn_norm_g, N, HA)
    Wa, Wb, Wo = _gather_finish(gat_mix, (y_a, y_b))
    Wo = Wo.reshape(1, D, D)
    za = _mm_nn(y_a, Wa, BF16, "mm_za")
    zb = _mm_nn(y_b, Wb, BF16, "mm_zb")
    gat_ffn = _gather_mid(gat_ffn, zb)
    off_ga, off_gb = 5 * HA + 3 * HB, 5 * HA + 3 * HB + D
    z = _merge_fwd(za, zb, p, N, off_ga, off_gb)
    mo = _mm_nn(z, Wo, F32, "mm_mo")
    vec2 = jnp.concatenate([g1, norm2_g, sh2, sc2, jnp.zeros((4, D), F32)], axis=0)
    x_mid, h2 = _resid_rms2_fwd(x[0], mo, vec2)
    W1, W3, W2 = _gather_finish(gat_ffn, h2)
    W2 = W2.reshape(1, F, D)
    u1 = _mm_nn(h2, W1, BF16, "mm_u1")
    u3 = _mm_nn(h2, W3, BF16, "mm_u3")
    a = _convgate_fwd(u1, u3, cw8, ffn_conv_b)
    f = _mm_nn(a, W2, F32, "mm_f")
    dy, df, s_loss = _loss_head(x_mid, f, g2, loss_target[0])
    loss = lax.psum(s_loss[1, 0], ("x", "y", "c"))
    d_g2 = s_loss[0:1]

    gW2 = _mm_tn(a, df, 1, "mm_gw2").reshape(N_CHIP, F // N_CHIP, D)
    da = _mm_nt(df, W2, BF16, "mm_da")
    du1, du3, s_conv = _convgate_bwd(u1, u3, da, cw8, ffn_conv_b)
    gW1 = _mm_tn(h2, du1, N_CHIP, "mm_gw1")
    gW3 = _mm_tn(h2, du3, N_CHIP, "mm_gw3")
    rs_ffn = _rs_start("ffn", [gW2, gW1, gW3])
    dh2a = _mm_nt(du1, W1, F32, "mm_dh2a")
    dh2b = _mm_nt(du3, W3, F32, "mm_dh2b")
    rs_ffn = _rs_scatter(rs_ffn, dh2b)
    dxm, dmo, s_rms2 = _resid_rms2_bwd(x_mid, dh2a, dh2b, dy, mo, vec2)
    gWo = _mm_tn(z, dmo, 1, "mm_gwo").reshape(N_CHIP, D // N_CHIP, D)
    dz = _mm_nt(dmo, Wo, BF16, "mm_dz")
    dza, dzb, dga, dgb = _merge_bwd(dz, za, zb, p, N, off_ga, off_gb)
    gWa = _mm_tn(y_a, dza, N_CHIP, "mm_gwa")
    gWb = _mm_tn(y_b, dzb, N_CHIP, "mm_gwb")
    rs_mix = _rs_start("mix", [gWo, gWa, gWb])
    dya = _mm_nt(dza, Wa, F32, "mm_dya")
    dyb = _mm_nt(dzb, Wb, BF16, "mm_dyb")
    rs_mix = _rs_scatter(rs_mix, dyb)
    dq_a, dzf, dzbk, di_a, dog, dlbl, s_ng = _hgrn_bwd(p, lbl, hgrn_norm_g, o_a, dya, st_a, N, HA)
    rs_ffn = _rs_join(rs_ffn, dq_a)
    dq_n, dk_n, dv_n, dbias, s_qk = _na_bwd(p, bias, na_q_norm_g, na_k_norm_g, cos, sin, dyb, N, off_na, HB)
    rs_mix = _rs_join(rs_mix, dq_n)
    dp = jnp.concatenate([dq_a, dzf, dzbk, di_a, dog, dq_n, dk_n, dv_n, dga, dgb], axis=1)
    gWin = _mm_tn(h_all, dp, N_CHIP, "mm_gwin")
    rs_in = _rs_start("in", [gWin])
    rs_in = _rs_scatter(rs_in, _PENDING[0])
    dh = _mm_nt(dp, Win, F32, "mm_dh")
    grad_x, s_rms1 = _rms1_bwd(xall, dh, dxm, norm1_g, scale1, N)
    d_table = _bias_grad(dbias)

    grads = {}
    big_names = ["ada_w", "w_in", "w_branch_a", "w_branch_b", "w_out", "ffn_w1", "ffn_w3", "ffn_w2"]
    small_names = [n for n in order if n not in big_names]
    delta, new_m, new_v = {}, {}, {}

    def update(nm, after=None):
        reduced = nm != "ada_w"
        d_, m_, v_, *g_ = _adamw(weights[nm][0], grads[nm][0], moms[nm][0][0], moms[nm][1][0], "adamw_" + nm,
                                 after, copy_g=reduced)
        delta[nm], new_m[nm], new_v[nm] = d_[None], m_[None], v_[None]
        if reduced:
            grads[nm] = g_[0][None]
        return d_

    last = grad_x
    for nm, g in zip(["ffn_w2", "ffn_w1", "ffn_w3"], _rs_finish(rs_ffn, last)):
        grads[nm] = g[None]
        last = update(nm, last)
    for nm, g in zip(["w_out", "w_branch_a", "w_branch_b"], _rs_finish(rs_mix, last)):
        grads[nm] = g[None]
        last = update(nm, last)
    rs_in = _rs_join(rs_in, last)

    zD = jnp.zeros((1, D), F32)
    dmod_l = jnp.concatenate([s_rms1[2:3], s_rms1[3:4], s_rms2[3:4], s_rms2[0:1], s_rms2[1:2], d_g2], axis=0)
    dmod_c = jnp.concatenate([s_rms1[0:1], s_rms1[1:2], zD, zD, zD, zD], axis=0)
    pk1, offs1 = _pack([dmod_l, dmod_c, s_rms1[4], s_rms2[2], dlbl, s_ng[0], s_qk[0], s_qk[1], d_table,
                        s_conv[0:3], s_conv[3]])
    g1all = _allgather8(pk1, "gather_small1")
    tot1 = _sum8(g1all, "sum_small1")
    dmod_rows = _unpack(g1all, offs1, 0).reshape(N_DEV, N_MOD * D)
    dmod_c_tot = _unpack(tot1, offs1, 1).reshape(1, N_MOD * D)
    dmod16 = jnp.concatenate([dmod_rows, dmod_c_tot, jnp.zeros((7, N_MOD * D), F32)], axis=0)
    dmod16_mine = lax.dynamic_slice(dmod16, (0, chip * n_ada), (16, n_ada))
    g_ada_w, dact = _ada_bwd(cs, ada_w[0], dmod16_mine)
    pk2, offs2 = _pack([dact[8]])
    g2all = _allgather8(pk2, "gather_small2")
    dact_rows = _unpack(g2all, offs2, 0)
    dact_sel = jnp.concatenate([dact_rows[2 * j][None] for j in range(N_CHIP)] + [jnp.zeros((4, D), F32)], axis=0)

    grads["ada_w"] = g_ada_w[None]
    grads["ada_b"] =(_unpack(tot1, offs1, 0) + _unpack(tot1, offs1, 1)).reshape(1, N_MOD * D)
    grads["norm1_g"] = _unpack(tot1, offs1, 2)[None]
    grads["norm2_g"] = _unpack(tot1, offs1, 3)[None]
    g_lbl = _unpack(tot1, offs1, 4)
    n_lb = HA // N_CHIP
    grads["hgrn_lb_logits"] = lax.dynamic_slice(g_lbl, (0, 0, chip * n_lb), (2, 2, n_lb))
    grads["hgrn_norm_g"] = _unpack(tot1, offs1, 5)[None]
    grads["na_q_norm_g"] = _unpack(tot1, offs1, 6)[None]
    grads["na_k_norm_g"] = _unpack(tot1, offs1, 7)[None]
    grads["na_rel_bias"] = _unpack(tot1, offs1, 8)[None]
    g_cw = _unpack(tot1, offs1, 9)
    n_f = F // N_CHIP
    grads["ffn_conv_w"] = lax.dynamic_slice(g_cw, (0, chip * n_f), (3, n_f))[None]
    grads["ffn_conv_b"] = _unpack(tot1, offs1, 10)[None]

    g_c_ctx = _dsilu_rows(dact_sel, c_ctx[None, :], "grad_c_ctx")
    grads["c_ctx"] = g_c_ctx[0]

    last = update("ada_w", g_c_ctx)
    pw, offw = _pack([weights[n] for n in small_names])
    pg, _ = _pack([grads[n] for n in small_names])
    pm, _ = _pack([moms[n][0] for n in small_names])
    pv, _ = _pack([moms[n][1] for n in small_names])
    d_, m_, v_ = _adamw(pw, pg, pm, pv, "adamw_small", last)
    for i, nm in enumerate(small_names):
        delta[nm], new_m[nm], new_v[nm] = _unpack(d_, offw, i), _unpack(m_, offw, i), _unpack(v_, offw, i)
    grads["w_in"] = _rs_finish(rs_in, d_)[0][None]
    update("w_in")

    return (loss, grad_x[None], *[grads[n] for n in order], *[delta[n] for n in order],
            *[new_m[n] for n in order], *[new_v[n] for n in order])


def _dsilu_rows(v, cv, name):
    D = v.shape[1]

    def body(v_ref, c_ref, o_ref):
        t = c_ref[...]
        s = _sigmoid(t)
        o_ref[...] = (((v_ref[0:1, :] + v_ref[1:2, :]) + v_ref[2:3, :]) + v_ref[3:4, :]) * (s * (1.0 + t * (1.0 - s)))

    return _pcall(body, name=name, out_shape=jax.ShapeDtypeStruct((1, D), F32),
                          compiler_params=_params())(v, cv)
```

```python
import functools

import numpy as np
import jax
import jax.numpy as jnp
from jax import lax
from jax.experimental import pallas as pl
from jax.experimental.pallas import tpu as pltpu

F32 = jnp.float32
BF16 = jnp.bfloat16
MESH = pl.DeviceIdType.MESH

HEAD = 128
GRID_W = 64
WIN_R = 8
WIN_C = 16
ROPE_THETA = 10000.0
EPS = 1e-6
N_MOD = 6
CHUNK = 16
HGRN_UNROLL = 4
ADAM_LR = 0.001
ADAM_B1 = 0.9
ADAM_B2 = 0.999
ADAM_EPS = 1e-08
ADAM_WD = 0.01
ADAM_STEP = 10
NEG = -1e30
VMEM_LIMIT = 56 * 1024 * 1024
N_DEV = 8
N_CHIP = 4
HI = lax.Precision.HIGHEST


def _pick(n, cands):
    for c in cands:
        if n % c == 0:
            return c
    return n


def _row_tile(rows, cols, target_bytes=1 << 20):
    want = max(16, target_bytes // (4 * cols))
    for t in (512, 256, 128, 64, 32, 16, 8):
        if t <= want and rows % t == 0:
            return t
    return rows


def _params(sem=None):
    return pltpu.CompilerParams(dimension_semantics=sem, vmem_limit_bytes=VMEM_LIMIT)


def _dot(a, b):
    return jnp.dot(a, b, preferred_element_type=F32)


def _dot_nt(a, b):
    return lax.dot_general(a, b, (((1,), (1,)), ((), ())), preferred_element_type=F32)


def _dot_tn(a, b):
    return lax.dot_general(a, b, (((0,), (0,)), ((), ())), preferred_element_type=F32)


def _sigmoid(x):
    return 1.0 / (1.0 + jnp.exp(-x))


def _col_tile(n):
    return n if n <= 1536 else _pick(n, (1024, 768, 512, 384, 256, 128))


def _mm_nn(x, w3, out_dtype, name):
    M, K = x.shape
    S, _, n = w3.shape
    tm = _pick(M, (768, 512, 256, 128, 64))
    tn = _col_tile(n)
    nb = n // tn

    def body(x_ref, w_ref, o_ref):
        o_ref[...] = _dot(x_ref[...].astype(BF16), w_ref[0]).astype(o_ref.dtype)

    return _pcall(
        body, name=name, grid=(M // tm, S * nb),
        in_specs=[pl.BlockSpec((tm, K), lambda i, j: (i, 0)),
                  pl.BlockSpec((1, K, tn), lambda i, j: (j // nb, 0, j % nb))],
        out_specs=pl.BlockSpec((tm, tn), lambda i, j: (i, j)),
        out_shape=jax.ShapeDtypeStruct((M, S * n), out_dtype),
        compiler_params=_params(("parallel", "parallel")),
    )(x, w3)


def _mm_nn_sel(x, w3, sel, out_dtype, name, prev=None):
    M, K = x.shape
    S, _, n = w3.shape
    tm = _pick(M, (768, 512, 256, 128, 64))
    tn = _col_tile(n)
    nb = n // tn
    k = sel.shape[1]

    def body(sel_ref, x_ref, w_ref, *rest):
        rest[-1][...] = _dot(x_ref[...].astype(BF16), w_ref[0]).astype(out_dtype)

    in_specs = [pl.BlockSpec((tm, K), lambda i, j, sel_ref: (i, 0)),
                pl.BlockSpec((1, K, tn), lambda i, j, sel_ref: (sel_ref[0, j // nb], 0, j % nb))]
    operands = [sel, x, w3]
    if prev is not None:
        in_specs.append(_ANY)
        operands.append(prev)
    return pl.pallas_call(
        body, name=name,
        grid_spec=pltpu.PrefetchScalarGridSpec(
            num_scalar_prefetch=1, grid=(M // tm, k * nb), in_specs=in_specs,
            out_specs=pl.BlockSpec((tm, tn), lambda i, j, sel_ref: (i, sel_ref[1, j // nb] * nb + j % nb))),
        out_shape=jax.ShapeDtypeStruct((M, S * n), out_dtype),
        input_output_aliases={} if prev is None else {3: 0},
        compiler_params=_params(("parallel", "parallel")),
    )(*operands)


def _mm_nt(dy, w3, out_dtype, name):
    M = dy.shape[0]
    S, K, n = w3.shape
    tm = _pick(M, (768, 512, 256, 128, 64))
    tk = K if K <= 2048 else _pick(K, (1408, 1024, 512, 256, 128))
    tc = n if n <= 2048 else _col_tile(n)
    nb = n // tc
    nsteps = S * nb

    def body(dy_ref, w_ref, o_ref, acc_ref):
        s = pl.program_id(2)

        @pl.when(s == 0)
        def _():
            acc_ref[...] = jnp.zeros_like(acc_ref)

        acc_ref[...] += _dot_nt(dy_ref[...].astype(BF16), w_ref[0])

        @pl.when(s == nsteps - 1)
        def _():
            o_ref[...] = acc_ref[...].astype(o_ref.dtype)

    return _pcall(
        body, name=name, grid=(M // tm, K // tk, nsteps),
        in_specs=[pl.BlockSpec((tm, tc), lambda i, k, s: (i, s)),
                  pl.BlockSpec((1, tk, tc), lambda i, k, s: (s // nb, k, s % nb))],
        out_specs=pl.BlockSpec((tm, tk), lambda i, k, s: (i, k)),
        out_shape=jax.ShapeDtypeStruct((M, K), out_dtype),
        scratch_shapes=[pltpu.VMEM((tm, tk), F32)],
        compiler_params=_params(("parallel", "parallel", "arbitrary")),
    )(dy, w3)


def _mm_tn(x, dy, S, name):
    M, K = x.shape
    n = dy.shape[1] // S
    tk = _pick(K, (512, 256, 128))
    tn = _col_tile(n)
    nb = n // tn

    def body(x_ref, dy_ref, o_ref):
        o_ref[0] = _dot_tn(x_ref[...].astype(BF16), dy_ref[...].astype(BF16)).astype(BF16)

    return _pcall(
        body, name=name, grid=(S * nb, K // tk),
        in_specs=[pl.BlockSpec((M, tk), lambda j, k: (0, k)),
                  pl.BlockSpec((M, tn), lambda j, k: (0, j))],
        out_specs=pl.BlockSpec((1, tk, tn), lambda j, k: (j // nb, k, j % nb)),
        out_shape=jax.ShapeDtypeStruct((S, K, n), BF16),
        compiler_params=_params(("parallel", "parallel")),
    )(x, dy)


def _chip_index():
    return (2 * lax.axis_index("x") + lax.axis_index("y")).astype(jnp.int32).reshape(1)


def _cast_bf16_slot(w, name):
    R, C = w.shape
    tr = _row_tile(R, C, 2 << 20)

    def body(j_ref, w_ref, o_ref):
        o_ref[0] = w_ref[...].astype(BF16)

    return _pcall(
        body, name=name,
        grid_spec=pltpu.PrefetchScalarGridSpec(
            num_scalar_prefetch=1, grid=(R // tr,),
            in_specs=[pl.BlockSpec((tr, C), lambda i, j_ref: (i, 0))],
            out_specs=pl.BlockSpec((1, tr, C), lambda i, j_ref: (j_ref[0], i, 0))),
        out_shape=jax.ShapeDtypeStruct((N_CHIP, R, C), BF16),
        compiler_params=_params(("parallel",)),
    )(_chip_index(), w)


def _pos():
    return lax.axis_index("x"), lax.axis_index("y"), lax.axis_index("c")


def _other_chips(x, y):
    return [(x, 1 - y), (1 - x, y), (1 - x, 1 - y)]


def _allgather8(v, name):
    R, C = v.shape

    def body(x_ref, out_ref, send_sems, recv_sems, local_sem):
        x, y, c = _pos()
        me, sibling = (x, y, c), (x, y, 1 - c)
        chips = _other_chips(x, y)

        def slot(px, py, pc):
            return out_ref.at[4 * px + 2 * py + pc]

        def copy(k, block, to, src=None):
            return pltpu.make_async_remote_copy(
                src_ref=slot(*block) if src is None else src, dst_ref=slot(*block),
                send_sem=send_sems.at[k], recv_sem=recv_sems.at[k], device_id=to, device_id_type=MESH)

        mine = pltpu.make_async_copy(x_ref, slot(*me), local_sem)
        mine.start()
        first = [copy(0, me, sibling, src=x_ref)]
        first += [copy(1 + j, me, (*chip, c), src=x_ref) for j, chip in enumerate(chips)]
        for cp in first:
            cp.start()
        passed = [copy(4 + j, (*chip, c), sibling) for j, chip in enumerate(chips)]
        for j, chip in enumerate(chips):
            copy(1 + j, (*chip, c), me).wait_recv()
            passed[j].start()
        copy(0, sibling, me).wait_recv()
        for j, chip in enumerate(chips):
            copy(4 + j, (*chip, 1 - c), me).wait_recv()
        for cp in first + passed:
            cp.wait_send()
        mine.wait()

    return _pcall(
        body, name=name,
        out_shape=jax.ShapeDtypeStruct((N_DEV, R, C), v.dtype),
        in_specs=[pl.BlockSpec(memory_space=pltpu.VMEM)],
        out_specs=pl.BlockSpec(memory_space=pltpu.VMEM),
        scratch_shapes=[pltpu.SemaphoreType.DMA((7,)), pltpu.SemaphoreType.DMA((7,)), pltpu.SemaphoreType.DMA],
        compiler_params=pltpu.CompilerParams(vmem_limit_bytes=VMEM_LIMIT),
    )(v)


_HBM = pl.BlockSpec(memory_space=pltpu.HBM)
_SEM = pl.BlockSpec(memory_space=pltpu.SEMAPHORE)
_ANY = pl.BlockSpec(memory_space=pl.ANY)
_EFFECT = pltpu.SideEffectType.DATAFLOW_SIDE_EFFECTING
_PENDING = []


def _pcall(body, **kw):
    def run(*operands):
        if not _PENDING or "in_specs" not in kw:
            return pl.pallas_call(body, **kw)(*operands)
        deps = list(_PENDING)
        n = len(operands)

        def tied(*refs):
            return body(*refs[:n], *refs[n + len(deps):])

        return pl.pallas_call(tied, **{**kw, "in_specs": list(kw["in_specs"]) + [_ANY] * len(deps)})(*operands, *deps)
    return run


def _copies(plan, refs, send_sems, recv_sems):
    return [pltpu.make_async_remote_copy(src_ref=src, dst_ref=dst, send_sem=send_sems.at[k], recv_sem=recv_sems.at[k],
                                         device_id=dev, device_id_type=MESH)
            for k, (src, dst, dev) in enumerate(plan(refs))]


def _xfer_start(name, bufs, plan, n_copies, after=None):
    n = len(bufs)
    deps = list(_PENDING) + ([after] if after is not None else [])
    nd = len(deps)

    def body(*refs):
        for cp in _copies(plan, refs[:n], refs[n + nd], refs[n + nd + 1]):
            cp.start()
        refs[-1][...] = jnp.zeros_like(refs[-1])

    outs = pl.pallas_call(
        body, name=name,
        out_shape=(pltpu.SemaphoreType.DMA((n_copies,)), pltpu.SemaphoreType.DMA((n_copies,)),
                   *[pltpu.HBM(b.shape, b.dtype) for b in bufs], jax.ShapeDtypeStruct((8, 128), F32)),
        in_specs=[_HBM] * n + [_ANY] * nd,
        out_specs=(_SEM, _SEM, *[_HBM] * n, pl.BlockSpec(memory_space=pltpu.VMEM)),
        input_output_aliases={t: 2 + t for t in range(n)},
        compiler_params=pltpu.CompilerParams(has_side_effects=_EFFECT),
    )(*[pltpu.with_memory_space_constraint(b, pltpu.HBM) for b in bufs], *deps)
    _PENDING[:] = [outs[-1]]
    return (outs[0], outs[1]), list(outs[2:2 + n])


def _xfer_wait(name, sems, bufs, plan, after):
    n = len(bufs)
    after = tuple(after) if isinstance(after, (tuple, list)) else (after,)

    def body(*refs):
        cps = _copies(plan, refs[:n], refs[n], refs[n + 1])
        for cp in cps:
            cp.wait_send()
        for cp in cps:
            cp.wait_recv()

    outs = pl.pallas_call(
        body, name=name,
        out_shape=tuple(pltpu.HBM(b.shape, b.dtype) for b in bufs),
        in_specs=[_HBM] * n + [_SEM, _SEM] + [_ANY] * len(after),
        out_specs=tuple([_HBM] * n),
        input_output_aliases={t: t for t in range(n)},
        compiler_params=pltpu.CompilerParams(has_side_effects=_EFFECT),
    )(*bufs, sems[0], sems[1], *after)
    return list(outs)


def _half(ref_rows, hc):
    h = ref_rows // 2
    return pl.ds(hc * h, h)


ALL_CHIPS = (0, 1, 2)
NEIGHBOURS = (0, 1)
DIAGONAL = (2,)


def _plan_gather_ici(which):
    def plan(bufs):
        x, y, c = _pos()
        j = 2 * x + y
        chips = _other_chips(x, y)
        return [(b.at[j, _half(b.shape[1], c)], b.at[j, _half(b.shape[1], c)], (*chips[k], c))
                for b in bufs for k in which]
    return plan


def _plan_gather_d2d(which):
    def plan(bufs):
        x, y, c = _pos()
        chips = _other_chips(x, y)
        out = []
        for b in bufs:
            for k in which:
                blk = b.at[2 * chips[k][0] + chips[k][1], _half(b.shape[1], c)]
                out.append((blk, blk, (x, y, 1 - c)))
        return out
    return plan


def _plan_pair_swap(n):
    def plan(bufs):
        x, y, c = _pos()
        return [(g.at[:, _half(g.shape[1], 1 - c)], land, (x, y, 1 - c)) for g, land in zip(bufs[:n], bufs[n:])]
    return plan


def _plan_chip_scatter(n):
    def plan(bufs):
        x, y, c = _pos()
        return [(p.at[2 * chip[0] + chip[1]], land.at[k], (*chip, c))
                for p, land in zip(bufs[:n], bufs[n:]) for k, chip in enumerate(_other_chips(x, y))]
    return plan


def _plan_pair_join(bufs):
    x, y, c = _pos()
    return [(b.at[_half(b.shape[0], c)], b.at[_half(b.shape[0], c)], (x, y, 1 - c)) for b in bufs]


def _empty_hbm(shape, dtype):
    return pltpu.with_memory_space_constraint(lax.empty(shape, dtype), pltpu.HBM)


def _gather_start(tag, bufs, after=None):
    sems, bufs = _xfer_start(f"gather_ici_start_{tag}", bufs, _plan_gather_ici(ALL_CHIPS), 3 * len(bufs), after)
    return dict(tag=tag, sems=sems, bufs=bufs)


def _gather_mid(st, after):
    tag = st["tag"]
    bufs = _xfer_wait(f"gather_ici_wait_{tag}", st["sems"], st["bufs"], _plan_gather_ici(ALL_CHIPS), after)
    sems, bufs = _xfer_start(f"gather_d2d_start_{tag}", bufs, _plan_gather_d2d(ALL_CHIPS), 3 * len(bufs))
    return dict(tag=tag, sems=sems, bufs=bufs)


def _gather_finish(st, after):
    return _xfer_wait(f"gather_d2d_wait_{st['tag']}", st["sems"], st["bufs"], _plan_gather_d2d(ALL_CHIPS), after)


def _d2d_hand_over(tag, sems, bufs, which, after):
    bufs = _xfer_wait(f"gather_ici_wait_{tag}", sems, bufs, _plan_gather_ici(which), after)
    sems, bufs = _xfer_start(f"gather_d2d_start_{tag}", bufs, _plan_gather_d2d(which), len(which) * len(bufs))
    return _xfer_wait(f"gather_d2d_wait_{tag}", sems, bufs, _plan_gather_d2d(which), after)


def _pair_add(g, r, name):
    S, R, C = g.shape
    h = R // 2
    tr = _row_tile(h, C)
    nb = h // tr

    def body(c_ref, g_ref, r_ref, o_ref):
        o_ref[...] = (g_ref[...].astype(F32) + r_ref[...].astype(F32)).astype(BF16)

    return _pcall(
        body, name=name,
        grid_spec=pltpu.PrefetchScalarGridSpec(
            num_scalar_prefetch=1, grid=(S, nb),
            in_specs=[pl.BlockSpec((1, tr, C), lambda s, i, c_ref: (s, c_ref[0] * nb + i, 0)),
                      pl.BlockSpec((1, tr, C), lambda s, i, c_ref: (s, i, 0))],
            out_specs=pl.BlockSpec((1, tr, C), lambda s, i, c_ref: (s, i, 0))),
        out_shape=jax.ShapeDtypeStruct((S, h, C), BF16),
        compiler_params=_params(("parallel", "parallel")),
    )(lax.axis_index("c").astype(jnp.int32).reshape(1), g, r)


def _chip_sum(p, rb, name):
    S, h, C = p.shape
    tr = _row_tile(h, C)
    nb = h // tr
    jc = jnp.concatenate([_chip_index(), lax.axis_index("c").astype(jnp.int32).reshape(1)])

    def body(jc_ref, p_ref, r_ref, o_ref):
        o_ref[...] = ((p_ref[0].astype(F32) + r_ref[0].astype(F32)) + r_ref[1].astype(F32)) + r_ref[2].astype(F32)

    return _pcall(
        body, name=name,
        grid_spec=pltpu.PrefetchScalarGridSpec(
            num_scalar_prefetch=1, grid=(nb,),
            in_specs=[pl.BlockSpec((1, tr, C), lambda i, jc_ref: (jc_ref[0], i, 0)),
                      pl.BlockSpec((3, tr, C), lambda i, jc_ref: (0, i, 0))],
            out_specs=pl.BlockSpec((tr, C), lambda i, jc_ref: (jc_ref[1] * nb + i, 0))),
        out_shape=jax.ShapeDtypeStruct((2 * h, C), F32),
        compiler_params=_params(("parallel",)),
    )(jc, p, rb)


def _rs_start(tag, gs):
    n = len(gs)
    lands = [_empty_hbm((g.shape[0], g.shape[1] // 2, g.shape[2]), g.dtype) for g in gs]
    sems, bufs = _xfer_start(f"rs_swap_start_{tag}", list(gs) + lands, _plan_pair_swap(n), n)
    return dict(tag=tag, n=n, sems=sems, bufs=bufs)


def _rs_scatter(st, after):
    tag, n = st["tag"], st["n"]
    bufs = _xfer_wait(f"rs_swap_wait_{tag}", st["sems"], st["bufs"], _plan_pair_swap(n), after)
    ps = [_pair_add(g, r, f"rs_pair_add_{tag}{t}") for t, (g, r) in enumerate(zip(bufs[:n], bufs[n:]))]
    lands = [_empty_hbm((3,) + p.shape[1:], p.dtype) for p in ps]
    sems, bufs = _xfer_start(f"rs_scatter_start_{tag}", ps + lands, _plan_chip_scatter(n), 3 * n)
    return dict(tag=tag, n=n, sems=sems, bufs=bufs)


def _rs_join(st, after):
    tag, n = st["tag"], st["n"]
    bufs = _xfer_wait(f"rs_scatter_wait_{tag}", st["sems"], st["bufs"], _plan_chip_scatter(n), after)
    fs = [_chip_sum(p, rb, f"rs_chip_sum_{tag}{t}") for t, (p, rb) in enumerate(zip(bufs[:n], bufs[n:]))]
    sems, bufs = _xfer_start(f"rs_join_start_{tag}", fs, _plan_pair_join, n)
    return dict(tag=tag, n=n, sems=sems, bufs=bufs)


def _rs_finish(st, after):
    return _xfer_wait(f"rs_join_wait_{st['tag']}", st["sems"], st["bufs"], _plan_pair_join, after)


def _sum8(g, name):
    _, R, C = g.shape

    def body(g_ref, o_ref):
        acc = g_ref[0]
        for d in range(1, N_DEV):
            acc = acc + g_ref[d]
        o_ref[...] = acc

    return _pcall(body, name=name, out_shape=jax.ShapeDtypeStruct((R, C), F32),
                          compiler_params=_params())(g)


def _ada_fwd(cs, w, b):
    D, n = w.shape
    tn = _pick(n, (512, 384, 256, 128))

    def body(c_ref, w_ref, b_ref, o_ref):
        cv = c_ref[...]
        a = (cv * _sigmoid(cv)).astype(BF16)
        o_ref[...] = _dot(a, w_ref[...].astype(BF16)) + b_ref[...]

    return _pcall(
        body, name="ada_fwd", grid=(n // tn,),
        in_specs=[pl.BlockSpec((16, D), lambda j: (0, 0)), pl.BlockSpec((D, tn), lambda j: (0, j)),
                  pl.BlockSpec((1, tn), lambda j: (0, j))],
        out_specs=pl.BlockSpec((16, tn), lambda j: (0, j)),
        out_shape=jax.ShapeDtypeStruct((16, n), F32),
        compiler_params=_params(("parallel",)),
    )(cs, w, b)


def _ada_bwd(cs, w, dmod):
    D, n = w.shape
    tn = _pick(n, (512, 384, 256, 128))

    def body(c_ref, w_ref, d_ref, gw_ref, da_ref):
        j = pl.program_id(0)
        cv = c_ref[...]
        a = cv * _sigmoid(cv)
        d = d_ref[...]
        gw_ref[...] = lax.dot_general(a, d, (((0,), (0,)), ((), ())), precision=HI, preferred_element_type=F32)

        @pl.when(j == 0)
        def _():
            da_ref[...] = jnp.zeros_like(da_ref)

        da_ref[...] += _dot_nt(d.astype(BF16), w_ref[...].astype(BF16))

    return _pcall(
        body, name="ada_bwd", grid=(n // tn,),
        in_specs=[pl.BlockSpec((16, D), lambda j: (0, 0)), pl.BlockSpec((D, tn), lambda j: (0, j)),
                  pl.BlockSpec((16, tn), lambda j: (0, j))],
        out_specs=[pl.BlockSpec((D, tn), lambda j: (0, j)), pl.BlockSpec((16, D), lambda j: (0, 0))],
        out_shape=[jax.ShapeDtypeStruct((D, n), F32), jax.ShapeDtypeStruct((16, D), F32)],
        compiler_params=_params(("arbitrary",)),
    )(cs, w, dmod)


def _rms1_fwd(xall, gain, shift2, scale2, n_ctx):
    T, D = xall.shape
    tb = _pick(n_ctx, (256, 128, 64, 32, 16))
    nctx = n_ctx // tb

    def body(x_ref, g_ref, sh_ref, sc_ref, o_ref):
        i = pl.program_id(0)
        xv = x_ref[...]
        r = lax.rsqrt(jnp.mean(xv * xv, axis=-1, keepdims=True) + EPS)
        nrm = xv * r * g_ref[...]
        lat = i >= nctx
        sh = jnp.where(lat, sh_ref[1:2, :], sh_ref[0:1, :])
        sc = jnp.where(lat, sc_ref[1:2, :], sc_ref[0:1, :])
        o_ref[...] = (nrm * (1.0 + sc) + sh).astype(BF16)

    vec = lambda r: pl.BlockSpec((r, D), lambda i: (0, 0))
    return _pcall(
        body, name="rms1_fwd", grid=(T // tb,),
        in_specs=[pl.BlockSpec((tb, D), lambda i: (i, 0)), vec(1), vec(2), vec(2)],
        out_specs=pl.BlockSpec((tb, D), lambda i: (i, 0)),
        out_shape=jax.ShapeDtypeStruct((T, D), BF16),
        compiler_params=_params(("parallel",)),
    )(xall, gain, shift2, scale2)


def _rms1_bwd(xall, dh, dxmid, gain, scale2, n_ctx):
    T, D = xall.shape
    L = T - n_ctx
    tb = _pick(n_ctx, (256, 128, 64, 32, 16))
    nctx = n_ctx // tb

    def body(x_ref, dh_ref, dxm_ref, g_ref, sc_ref, dx_ref, cs_ref):
        i = pl.program_id(0)
        lat = i >= nctx
        xv = x_ref[...]
        r = lax.rsqrt(jnp.mean(xv * xv, axis=-1, keepdims=True) + EPS)
        xh = xv * r
        g = g_ref[...]
        nrm = xh * g
        sc = jnp.where(lat, sc_ref[1:2, :], sc_ref[0:1, :])
        dhv = dh_ref[...]
        dn = dhv * (1.0 + sc)
        dxh = dn * g
        dxv = r * (dxh - xh * jnp.mean(dxh * xh, axis=-1, keepdims=True))
        s_sh = jnp.sum(dhv, axis=0, keepdims=True)
        s_sc = jnp.sum(dhv * nrm, axis=0, keepdims=True)
        s_g = jnp.sum(dn * xh, axis=0, keepdims=True)
        zero = jnp.zeros_like(s_sh)
        rows = lax.broadcasted_iota(jnp.int32, (8, D), 0)
        upd = jnp.where(rows == 0, jnp.where(lat, zero, s_sh),
              jnp.where(rows == 1, jnp.where(lat, zero, s_sc),
              jnp.where(rows == 2, jnp.where(lat, s_sh, zero),
              jnp.where(rows == 3, jnp.where(lat, s_sc, zero),
              jnp.where(rows == 4, s_g, 0.0)))))

        @pl.when(i == 0)
        def _():
            cs_ref[...] = jnp.zeros_like(cs_ref)

        cs_ref[...] += upd

        @pl.when(lat)
        def _():
            dx_ref[...] = dxv + dxm_ref[...]

    lat_blk = lambda i: (jnp.maximum(i - nctx, 0), 0)
    vec = lambda r: pl.BlockSpec((r, D), lambda i: (0, 0))
    return _pcall(
        body, name="rms1_bwd", grid=(T // tb,),
        in_specs=[pl.BlockSpec((tb, D), lambda i: (i, 0)), pl.BlockSpec((tb, D), lambda i: (i, 0)),
                  pl.BlockSpec((tb, D), lat_blk), vec(1), vec(2)],
        out_specs=[pl.BlockSpec((tb, D), lat_blk), vec(8)],
        out_shape=[jax.ShapeDtypeStruct((L, D), F32), jax.ShapeDtypeStruct((8, D), F32)],
        compiler_params=_params(("arbitrary",)),
    )(xall, dh, dxmid, gain, scale2)


def _resid_rms2_fwd(x, mo, vecs):
    L, D = x.shape
    tb = _pick(L, (256, 128, 64))

    def body(x_ref, mo_ref, v_ref, xm_ref, h_ref):
        xm = x_ref[...] + v_ref[0:1, :] * mo_ref[...]
        xm_ref[...] = xm
        r = lax.rsqrt(jnp.mean(xm * xm, axis=-1, keepdims=True) + EPS)
        h_ref[...] = (xm * r * v_ref[1:2, :] * (1.0 + v_ref[3:4, :]) + v_ref[2:3, :]).astype(BF16)

    blk = pl.BlockSpec((tb, D), lambda i: (i, 0))
    return _pcall(
        body, name="resid_rms2_fwd", grid=(L // tb,),
        in_specs=[blk, blk, pl.BlockSpec((8, D), lambda i: (0, 0))],
        out_specs=[blk, blk],
        out_shape=[jax.ShapeDtypeStruct((L, D), F32), jax.ShapeDtypeStruct((L, D), BF16)],
        compiler_params=_params(("parallel",)),
    )(x, mo, vecs)


def _resid_rms2_bwd(xmid, dh_a, dh_b, dy, mo, vecs):
    L, D = xmid.shape
    tb = _pick(L, (256, 128, 64))

    def body(xm_ref, da_ref, db_ref, dy_ref, mo_ref, v_ref, dxm_ref, dmo_ref, cs_ref):
        i = pl.program_id(0)
        xm = xm_ref[...]
        r = lax.rsqrt(jnp.mean(xm * xm, axis=-1, keepdims=True) + EPS)
        xh = xm * r
        g = v_ref[1:2, :]
        nrm = xh * g
        dhv = da_ref[...] + db_ref[...]
        dn = dhv * (1.0 + v_ref[3:4, :])
        dxh = dn * g
        dxm = dy_ref[...] + r * (dxh - xh * jnp.mean(dxh * xh, axis=-1, keepdims=True))
        dxm_ref[...] = dxm
        dmo_ref[...] = (dxm * v_ref[0:1, :]).astype(BF16)
        s0 = jnp.sum(dhv, axis=0, keepdims=True)
        s1 = jnp.sum(dhv * nrm, axis=0, keepdims=True)
        s2 = jnp.sum(dn * xh, axis=0, keepdims=True)
        s3 = jnp.sum(dxm * mo_ref[...], axis=0, keepdims=True)
        rows = lax.broadcasted_iota(jnp.int32, (8, D), 0)
        upd = jnp.where(rows == 0, s0, jnp.where(rows == 1, s1, jnp.where(rows == 2, s2,
              jnp.where(rows == 3, s3, 0.0))))

        @pl.when(i == 0)
        def _():
            cs_ref[...] = jnp.zeros_like(cs_ref)

        cs_ref[...] += upd

    blk = pl.BlockSpec((tb, D), lambda i: (i, 0))
    vec = pl.BlockSpec((8, D), lambda i: (0, 0))
    return _pcall(
        body, name="resid_rms2_bwd", grid=(L // tb,),
        in_specs=[blk, blk, blk, blk, blk, vec],
        out_specs=[blk, blk, vec],
        out_shape=[jax.ShapeDtypeStruct((L, D), F32), jax.ShapeDtypeStruct((L, D), BF16),
                   jax.ShapeDtypeStruct((8, D), F32)],
        compiler_params=_params(("arbitrary",)),
    )(xmid, dh_a, dh_b, dy, mo, vecs)


def _loss_head(xmid, f, g2, target):
    L, D = xmid.shape
    tb = _pick(L, (256, 128, 64))

    def body(xm_ref, f_ref, g_ref, t_ref, dy_ref, df_ref, s_ref):
        i = pl.program_id(0)
        fv = f_ref[...]
        g = g_ref[...]
        err = xm_ref[...] + g * fv - t_ref[...]
        dy = err * (1.0 / D)
        dy_ref[...] = dy
        df_ref[...] = (dy * g).astype(BF16)
        s0 = jnp.sum(dy * fv, axis=0, keepdims=True)
        part = 0.5 * jnp.sum(jnp.mean(err * err, axis=-1, keepdims=True), axis=0, keepdims=True)
        rows = lax.broadcasted_iota(jnp.int32, (8, D), 0)
        upd = jnp.where(rows == 0, s0, jnp.where(rows == 1, part, 0.0))

        @pl.when(i == 0)
        def _():
            s_ref[...] = jnp.zeros_like(s_ref)

        s_ref[...] += upd

    blk = pl.BlockSpec((tb, D), lambda i: (i, 0))
    return _pcall(
        body, name="loss_head", grid=(L // tb,),
        in_specs=[blk, blk, pl.BlockSpec((1, D), lambda i: (0, 0)), blk],
        out_specs=[blk, blk, pl.BlockSpec((8, D), lambda i: (0, 0))],
        out_shape=[jax.ShapeDtypeStruct((L, D), F32), jax.ShapeDtypeStruct((L, D), BF16),
                   jax.ShapeDtypeStruct((8, D), F32)],
        compiler_params=_params(("arbitrary",)),
    )(xmid, f, g2, target)


def _gate_cols(D, off):
    tc = _pick(np.gcd(D, off), (512, 256, 128))
    return tc, off // tc


def _merge_fwd(za, zb, p, n_ctx, off_a, off_b):
    L, D = za.shape
    tb = _pick(n_ctx, (256, 128, 64, 32, 16))
    nctx = n_ctx // tb
    tc, oa = _gate_cols(D, off_a)
    _, ob = _gate_cols(D, off_b)
    if off_b % tc:
        raise ValueError("gate column offsets must share a column tile")
    ob = off_b // tc

    def body(za_ref, zb_ref, ga_ref, gb_ref, z_ref):
        z_ref[...] = (_sigmoid(ga_ref[...]) * za_ref[...].astype(F32)
                      + _sigmoid(gb_ref[...]) * zb_ref[...].astype(F32)).astype(BF16)

    blk = pl.BlockSpec((tb, tc), lambda i, j: (i, j))
    return _pcall(
        body, name="merge_fwd", grid=(L // tb, D // tc),
        in_specs=[blk, blk, pl.BlockSpec((tb, tc), lambda i, j: (i + nctx, oa + j)),
                  pl.BlockSpec((tb, tc), lambda i, j: (i + nctx, ob + j))],
        out_specs=blk,
        out_shape=jax.ShapeDtypeStruct((L, D), BF16),
        compiler_params=_params(("parallel", "parallel")),
    )(za, zb, p, p)


def _merge_bwd(dz, za, zb, p, n_ctx, off_a, off_b):
    L, D = za.shape
    T = L + n_ctx
    tb = _pick(n_ctx, (256, 128, 64, 32, 16))
    nctx = n_ctx // tb
    tc = _gate_cols(D, off_a)[0]
    oa, ob = off_a // tc, off_b // tc

    def body(dz_ref, za_ref, zb_ref, ga_ref, gb_ref, dza_ref, dzb_ref, dga_ref, dgb_ref):
        i = pl.program_id(1)

        @pl.when(i < nctx)
        def _():
            dga_ref[...] = jnp.zeros_like(dga_ref)
            dgb_ref[...] = jnp.zeros_like(dgb_ref)

        @pl.when(i >= nctx)
        def _():
            dzv = dz_ref[...].astype(F32)
            sa = _sigmoid(ga_ref[...])
            sb = _sigmoid(gb_ref[...])
            dza_ref[...] = (dzv * sa).astype(BF16)
            dzb_ref[...] = (dzv * sb).astype(BF16)
            dga_ref[...] = (dzv * za_ref[...].astype(F32) * sa * (1.0 - sa)).astype(BF16)
            dgb_ref[...] = (dzv * zb_ref[...].astype(F32) * sb * (1.0 - sb)).astype(BF16)

    lat = pl.BlockSpec((tb, tc), lambda j, i: (jnp.maximum(i - nctx, 0), j))
    allr = pl.BlockSpec((tb, tc), lambda j, i: (i, j))
    return _pcall(
        body, name="merge_bwd", grid=(D // tc, T // tb),
        in_specs=[lat, lat, lat, pl.BlockSpec((tb, tc), lambda j, i: (i, oa + j)),
                  pl.BlockSpec((tb, tc), lambda j, i: (i, ob + j))],
        out_specs=[lat, lat, allr, allr],
        out_shape=[jax.ShapeDtypeStruct((L, D), BF16), jax.ShapeDtypeStruct((L, D), BF16),
                   jax.ShapeDtypeStruct((T, D), BF16), jax.ShapeDtypeStruct((T, D), BF16)],
        compiler_params=_params(("arbitrary", "arbitrary")),
    )(dz, za, zb, p, p)


def _shift_down(u, rows):
    return jnp.where(rows == 0, 0.0, pltpu.roll(u, 1, 0))


def _shift_up(u, rows):
    n = u.shape[0]
    return jnp.where(rows == n - 1, 0.0, pltpu.roll(u, n - 1, 0))


def _convgate_fwd(u1, u3, cw, cb):
    L, F = u1.shape
    tc = _pick(F, (256, 128))

    def body(u1_ref, u3_ref, w_ref, b_ref, a_ref):
        u = u1_ref[...].astype(F32)
        rows = lax.broadcasted_iota(jnp.int32, u.shape, 0)
        cv = _shift_down(u, rows) * w_ref[0:1, :] + u * w_ref[1:2, :] + _shift_up(u, rows) * w_ref[2:3, :] + b_ref[...]
        a_ref[...] = (cv * _sigmoid(cv) * u3_ref[...].astype(F32)).astype(BF16)

    blk = pl.BlockSpec((L, tc), lambda j: (0, j))
    return _pcall(
        body, name="convgate_fwd", grid=(F // tc,),
        in_specs=[blk, blk, pl.BlockSpec((8, tc), lambda j: (0, j)), pl.BlockSpec((1, tc), lambda j: (0, j))],
        out_specs=blk,
        out_shape=jax.ShapeDtypeStruct((L, F), BF16),
        compiler_params=_params(("parallel",)),
    )(u1, u3, cw, cb)


def _convgate_bwd(u1, u3, da, cw, cb):
    L, F = u1.shape
    tc = _pick(F, (256, 128))

    def body(u1_ref, u3_ref, da_ref, w_ref, b_ref, du1_ref, du3_ref, s_ref):
        u = u1_ref[...].astype(F32)
        rows = lax.broadcasted_iota(jnp.int32, u.shape, 0)
        um, up = _shift_down(u, rows), _shift_up(u, rows)
        w0, w1, w2 = w_ref[0:1, :], w_ref[1:2, :], w_ref[2:3, :]
        cv = um * w0 + u * w1 + up * w2 + b_ref[...]
        s = _sigmoid(cv)
        dav = da_ref[...].astype(F32)
        du3_ref[...] = (dav * cv * s).astype(BF16)
        dcv = dav * u3_ref[...].astype(F32) * (s * (1.0 + cv * (1.0 - s)))
        du1_ref[...] = (_shift_up(dcv, rows) * w0 + dcv * w1 + _shift_down(dcv, rows) * w2).astype(BF16)
        r8 = lax.broadcasted_iota(jnp.int32, (8, tc), 0)
        s0 = jnp.sum(dcv * um, axis=0, keepdims=True)
        s1 = jnp.sum(dcv * u, axis=0, keepdims=True)
        s2 = jnp.sum(dcv * up, axis=0, keepdims=True)
        s3 = jnp.sum(dcv, axis=0, keepdims=True)
        s_ref[...] = jnp.where(r8 == 0, s0, jnp.where(r8 == 1, s1, jnp.where(r8 == 2, s2,
                     jnp.where(r8 == 3, s3, 0.0))))

    blk = pl.BlockSpec((L, tc), lambda j: (0, j))
    v8 = pl.BlockSpec((8, tc), lambda j: (0, j))
    return _pcall(
        body, name="convgate_bwd", grid=(F // tc,),
        in_specs=[blk, blk, blk, v8, pl.BlockSpec((1, tc), lambda j: (0, j))],
        out_specs=[blk, blk, v8],
        out_shape=[jax.ShapeDtypeStruct((L, F), BF16), jax.ShapeDtypeStruct((L, F), BF16),
                   jax.ShapeDtypeStruct((8, F), F32)],
        compiler_params=_params(("parallel",)),
    )(u1, u3, da, cw, cb)


def _lower_bound(lbl_ref, d):
    l0, l1 = lbl_ref[d, 0:1, :], lbl_ref[d, 1:2, :]
    m = jnp.maximum(l0, l1)
    e0, e1 = jnp.exp(l0 - m), jnp.exp(l1 - m)
    return e0 / (e0 + e1)


def _chunk_cumsum(x, rev):
    n = x.shape[0]
    r = lax.broadcasted_iota(jnp.int32, x.shape, 0) % CHUNK
    k = 1
    while k < CHUNK:
        if rev:
            x = x + jnp.where(r < CHUNK - k, pltpu.roll(x, n - k, 0), 0.0)
        else:
            x = x + jnp.where(r >= k, pltpu.roll(x, k, 0), 0.0)
        k *= 2
    return x


def _gate_terms(z, lb):
    sg = _sigmoid(z)
    f = lb + (1.0 - lb) * sg
    return sg, f


def _decay_terms(z, lb, rev):
    _, f = _gate_terms(z, lb)
    g = jnp.log(f)
    return 1.0 - f, _chunk_cumsum(g, rev), _chunk_cumsum(g, not rev) - g


def _chunk_total(c, rev):
    return c[0:1, :] if rev else c[CHUNK - 1:CHUNK, :]


def _pair_decay(c, s, rev):
    t = lax.broadcasted_iota(jnp.int32, (CHUNK, 1), 0)
    later = (t <= s) if rev else (t >= s)
    return jnp.where(later, jnp.exp(c - c[s:s + 1, :]), 0.0)


def _scan_chunk(i, n_ctx_chunks, n_chunks, rev):
    if not rev:
        return i
    return jnp.where(i < n_ctx_chunks, n_ctx_chunks - 1 - i, n_chunks + n_ctx_chunks - 1 - i)


def _rows(ci):
    return pl.ds(pl.multiple_of(ci * CHUNK, CHUNK), CHUNK)


def _hgrn_cols(HA):
    return HA // HEAD


def _hgrn_fwd(p, lbl, ng, n_ctx, HA):
    T = p.shape[0]
    L = T - n_ctx
    nh = _hgrn_cols(HA)
    nc, ncc = T // CHUNK, n_ctx // CHUNK

    def body(q_ref, zf_ref, zb_ref, v_ref, og_ref, lbl_ref, ng_ref, ya_ref, o_ref, st_ref,
             c_scr, k_scr, qe_scr, ke_scr, o_scr):
        dirs = ((0, False, zf_ref), (1, True, zb_ref))
        for d, rev, z_ref in dirs:
            k, c, rest = _decay_terms(z_ref[...], _lower_bound(lbl_ref, d), rev)
            c_scr[d] = c
            k_scr[d] = k
            qe_scr[d] = (q_ref[...] * jnp.exp(c)).astype(BF16)
            ke_scr[d] = (k * jnp.exp(rest)).astype(BF16)

        def step(i2, states):
            states = list(states)
            for u in range(HGRN_UNROLL):
                for d, rev, _ in dirs:
                    St = states[d]
                    ci = _scan_chunk(HGRN_UNROLL * i2 + u, ncc, nc, rev)
                    rows = _rows(ci)
                    q, v, c, k = q_ref[rows, :], v_ref[rows, :], c_scr[d, rows, :], k_scr[d, rows, :]
                    st_ref[0, d, ci] = St.astype(BF16)
                    o = jnp.zeros((CHUNK, HEAD), F32)
                    for s in range(CHUNK):
                        E = _pair_decay(c, s, rev)
                        a = jnp.sum(q * E * k[s:s + 1, :], axis=1, keepdims=True)
                        o = o + a * v[s:s + 1, :]
                    o_scr[d, rows, :] = o + _dot_nt(qe_scr[d, rows, :], St.astype(BF16))
                    states[d] = St * jnp.exp(_chunk_total(c, rev)) + _dot_tn(v.astype(BF16), ke_scr[d, rows, :])
            return tuple(states)

        if nc % HGRN_UNROLL:
            raise ValueError("the number of chunks must be a multiple of HGRN_UNROLL")
        zero = jnp.zeros((HEAD, HEAD), F32)
        lax.fori_loop(0, nc // HGRN_UNROLL, step, (zero, zero))

        o = o_scr[0, pl.ds(n_ctx, L), :] + o_scr[1, pl.ds(n_ctx, L), :]
        o_ref[...] = o
        r = lax.rsqrt(jnp.mean(o * o, axis=-1, keepdims=True) + EPS)
        og = og_ref[pl.ds(n_ctx, L), :]
        ya_ref[...] =(o * r * ng_ref[...] * (og * _sigmoid(og))).astype(BF16)

    cb = HA // HEAD
    col = lambda kk: pl.BlockSpec((T, HEAD), lambda h: (0, kk * cb + h))
    return _pcall(
        body, name="hgrn_fwd", grid=(nh,),
        in_specs=[col(0), col(1), col(2), col(3), col(4),
                  pl.BlockSpec((2, 2, HEAD), lambda h: (0, 0, h)), pl.BlockSpec((1, HEAD), lambda h: (0, 0))],
        out_specs=[pl.BlockSpec((L, HEAD), lambda h: (0, h)), pl.BlockSpec((L, HEAD), lambda h: (0, h)),
                   pl.BlockSpec((1, 2, nc, HEAD, HEAD), lambda h: (h, 0, 0, 0, 0))],
        out_shape=[jax.ShapeDtypeStruct((L, HA), BF16), jax.ShapeDtypeStruct((L, HA), F32),
                   jax.ShapeDtypeStruct((nh, 2, nc, HEAD, HEAD), BF16)],
        scratch_shapes=[pltpu.VMEM((2, T, HEAD), F32), pltpu.VMEM((2, T, HEAD), F32),
                        pltpu.VMEM((2, T, HEAD), BF16), pltpu.VMEM((2, T, HEAD), BF16),
                        pltpu.VMEM((2, T, HEAD), F32)],
        compiler_params=_params(("parallel",)),
    )(p, p, p, p, p, lbl, ng)


def _hgrn_bwd(p, lbl, ng, o, dya, st, n_ctx, HA):
    T = p.shape[0]
    L = T - n_ctx
    nh = _hgrn_cols(HA)
    nc, ncc = T // CHUNK, n_ctx // CHUNK

    def body(q_ref, zf_ref, zb_ref, v_ref, og_ref, lbl_ref, ng_ref, o_ref, dya_ref, st_ref,
             dq_ref, dzf_ref, dzb_ref, dv_ref, dog_ref, dlbl_ref, dng_ref,
             do_scr, c_scr, k_scr, qe_scr, ke_scr, dg_scr, dk_scr, dq_scr, dv_scr, row_scr):
        h = pl.program_id(0)
        ov = o_ref[...]
        r = lax.rsqrt(jnp.mean(ov * ov, axis=-1, keepdims=True) + EPS)
        oh = ov * r
        ogv = og_ref[pl.ds(n_ctx, L), :]
        sg_o = _sigmoid(ogv)
        dyv = dya_ref[...]
        ngv = ng_ref[...]
        dog_ref[pl.ds(0, n_ctx), :] = jnp.zeros((n_ctx, HEAD), BF16)
        dog_ref[pl.ds(n_ctx, L), :] = (dyv * oh * ngv * (sg_o * (1.0 + ogv * (1.0 - sg_o)))).astype(BF16)
        don = dyv * (ogv * sg_o)
        dng = jnp.sum(don * oh, axis=0, keepdims=True)
        doh = don * ngv
        do_scr[pl.ds(0, n_ctx), :] = jnp.zeros((n_ctx, HEAD), F32)
        do_scr[pl.ds(n_ctx, L), :] = r * (doh - oh * jnp.mean(doh * oh, axis=-1, keepdims=True))

        @pl.when(h == 0)
        def _():
            dng_ref[...] = jnp.zeros_like(dng_ref)

        dng_ref[0:1, :] += dng

        t16 = lax.broadcasted_iota(jnp.int32, (CHUNK, HEAD), 0)
        dirs = ((0, False, zf_ref, dzf_ref), (1, True, zb_ref, dzb_ref))
        for d, rev, z_ref, _ in dirs:
            k, c, rest = _decay_terms(z_ref[...], _lower_bound(lbl_ref, d), rev)
            c_scr[d] = c
            k_scr[d] = k
            qe_scr[d] = (q_ref[...] * jnp.exp(c)).astype(BF16)
            ke_scr[d] = (k * jnp.exp(rest)).astype(BF16)
        dq_scr[...] = jnp.zeros_like(dq_scr)
        dv_scr[...] = jnp.zeros_like(dv_scr)

        zero = jnp.zeros((HEAD, HEAD), F32)

        def bwd_chunk(i, carry, u):
            new = []
            for (d, rev, _, _), dSt in zip(dirs, carry):
                ci = _scan_chunk(i, ncc, nc, rev)
                rows = _rows(ci)
                q, v, do = q_ref[rows, :], v_ref[rows, :], do_scr[rows, :]
                c, k = c_scr[d, rows, :], k_scr[d, rows, :]
                tot = _chunk_total(c, rev)
                etot = jnp.exp(tot)
                St = st_ref[0, d, ci]
                dSb = dSt.astype(BF16)
                do_b = do.astype(BF16)
                dq_x = _dot(do_b, St) * jnp.exp(c)
                dk_x = _dot(v.astype(BF16), dSb) * jnp.exp(tot - c)
                dv_x = _dot_nt(ke_scr[d, rows, :], dSb)
                dtot = (jnp.sum(St.astype(F32) * dSt, axis=0, keepdims=True) * etot
                        + jnp.sum(k * dk_x, axis=0, keepdims=True))
                dq = jnp.zeros((CHUNK, HEAD), F32)
                for s in range(CHUNK):
                    E = _pair_decay(c, s, rev)
                    XE = E * k[s:s + 1, :]
                    a = jnp.sum(q * XE, axis=1, keepdims=True)
                    da = jnp.sum(do * v[s:s + 1, :], axis=1, keepdims=True)
                    dq = dq + da * XE
                    row_scr[u, d, 0, s:s + 1, :] = jnp.sum(da * q * E, axis=0, keepdims=True)
                    row_scr[u, d, 1, s:s + 1, :] = jnp.sum(a * do, axis=0, keepdims=True)
                dq, dk, dv = dq + dq_x, row_scr[u, d, 0] + dk_x, row_scr[u, d, 1] + dv_x
                dg_scr[d, rows, :] = _chunk_cumsum(q * dq - k * dk, not rev) + dtot
                dk_scr[d, rows, :] = dk
                dq_scr[rows, :] += dq
                dv_scr[rows, :] += dv
                new.append(dSt * etot + _dot_tn(do_b, qe_scr[d, rows, :]))
            return tuple(new)

        def bwd_step(i2, carry):
            for u in range(2):
                carry = bwd_chunk(nc - 1 - (2 * i2 + u), carry, u)
            return carry

        lax.fori_loop(0, nc // 2, bwd_step, (zero, zero))

        for d, _, z_ref, dz_ref in dirs:
            lb = _lower_bound(lbl_ref, d)
            sg, f = _gate_terms(z_ref[...], lb)
            df = dg_scr[d] / f - dk_scr[d]
            dz_ref[...] = (df * (1.0 - lb) * sg * (1.0 - sg)).astype(BF16)
            dl0 = jnp.sum(df * (1.0 - sg), axis=0, keepdims=True) * lb * (1.0 - lb)
            dlbl_ref[d, 0:1, :] = dl0
            dlbl_ref[d, 1:2, :] = -dl0
        dq_ref[...] = dq_scr[...].astype(BF16)
        dv_ref[...] = dv_scr[...].astype(BF16)

    cb = HA // HEAD
    col = lambda kk: pl.BlockSpec((T, HEAD), lambda h: (0, kk * cb + h))
    tcol = pl.BlockSpec((T, HEAD), lambda h: (0, h))
    lcol = pl.BlockSpec((L, HEAD), lambda h: (0, h))
    outs = _pcall(
        body, name="hgrn_bwd", grid=(nh,),
        in_specs=[col(0), col(1), col(2), col(3), col(4),
                  pl.BlockSpec((2, 2, HEAD), lambda h: (0, 0, h)), pl.BlockSpec((1, HEAD), lambda h: (0, 0)),
                  lcol, lcol,
                  pl.BlockSpec((1, 2, nc, HEAD, HEAD), lambda h: (h, 0, 0, 0, 0), pipeline_mode=pl.Buffered(1))],
        out_specs=[tcol, tcol, tcol, tcol, tcol, pl.BlockSpec((2, 2, HEAD), lambda h: (0, 0, h)),
                   pl.BlockSpec((8, HEAD), lambda h: (0, 0))],
        out_shape=[jax.ShapeDtypeStruct((T, HA), BF16)] * 5 + [jax.ShapeDtypeStruct((2, 2, HA), F32),
                                                               jax.ShapeDtypeStruct((8, HEAD), F32)],
        scratch_shapes=[pltpu.VMEM((T, HEAD), F32),
                        pltpu.VMEM((2, T, HEAD), F32), pltpu.VMEM((2, T, HEAD), F32),
                        pltpu.VMEM((2, T, HEAD), BF16), pltpu.VMEM((2, T, HEAD), BF16),
                        pltpu.VMEM((2, T, HEAD), F32), pltpu.VMEM((2, T, HEAD), F32),
                        pltpu.VMEM((T, HEAD), F32), pltpu.VMEM((T, HEAD), F32),
                        pltpu.VMEM((2, 2, 2, CHUNK, HEAD), F32)],
        compiler_params=_params(("arbitrary",)),
    )(p, p, p, p, p, lbl, ng, o, dya, st)
    return outs


def _swap_halves(t, lane):
    q = HEAD // 4
    return jnp.where((lane % (2 * q)) < q, pltpu.roll(t, HEAD - q, 1), pltpu.roll(t, q, 1))


def _qk_norm(t, g):
    r = lax.rsqrt(jnp.mean(t * t, axis=-1, keepdims=True) + EPS)
    return t * r, r


def _rope(t, cos, sin, lane):
    return t * cos + _swap_halves(t, lane) * sin


def _qk_norm_bwd(dy, th, r, g):
    dth = dy * g
    return r * (dth - th * jnp.mean(dth * th, axis=-1, keepdims=True)), jnp.sum(dy * th, axis=0, keepdims=True)


def _rope_bwd(dy, cos, sin, lane):
    return dy * cos + _swap_halves(dy * sin, lane)


def _na_geometry(L):
    n_rows = L // GRID_W
    kr = min(WIN_R, n_rows)
    return n_rows, kr


def _na_prep(q_ref, k_ref, v_ref, gq_ref, gk_ref, cos_ref, sin_ref, qs, ks, vs, n_ctx, L):
    lane = lax.broadcasted_iota(jnp.int32, (L, HEAD), 1)
    cos, sin = cos_ref[...], sin_ref[...]
    qh, _ = _qk_norm(q_ref[pl.ds(n_ctx, L), :], None)
    qs[...] = _rope(qh * gq_ref[...], cos, sin, lane).astype(BF16)
    kh, _ = _qk_norm(k_ref[pl.ds(n_ctx, L), :], None)
    ks[pl.ds(n_ctx, L), :] = _rope(kh * gk_ref[...], cos, sin, lane).astype(BF16)
    kc, _ = _qk_norm(k_ref[pl.ds(0, n_ctx), :], None)
    ks[pl.ds(0, n_ctx), :] = (kc * gk_ref[...]).astype(BF16)
    vs[...] = v_ref[...].astype(BF16)


NA_RB = 4


def _na_band_rows(kr):
    return kr + NA_RB


def _na_scores(i, qs, ks, bias_ref, n_ctx, n_rows, kr):
    scale = HEAD ** -0.5
    kb = _na_band_rows(kr)
    rq = NA_RB * i
    r0 = jnp.clip(rq - WIN_R // 2, 0, n_rows - kb)
    qrows = pl.ds(pl.multiple_of(rq * GRID_W, NA_RB * GRID_W), NA_RB * GRID_W)
    krows = pl.ds(pl.multiple_of(n_ctx + r0 * GRID_W, GRID_W), kb * GRID_W)
    qv = qs[qrows, :]
    sb = _dot_nt(qv, ks[krows, :]) * scale
    band_row = lax.broadcasted_iota(jnp.int32, (GRID_W, kb * GRID_W), 1) // GRID_W
    parts, tiles = [], []
    for u in range(NA_RB):
        r_u = rq + u
        first = jnp.clip(r_u - WIN_R // 2, 0, n_rows - kr) - r0
        idx = [jnp.clip(r0 - r_u + (WIN_R - 1) + 2 * jj, 0, 2 * WIN_R - 1) for jj in range(kb // 2)]
        bias_u = jnp.concatenate([bias_ref[0, t] for t in idx], axis=1)
        inside = (band_row >= first) & (band_row < first + kr)
        parts.append(jnp.where(inside, sb[u * GRID_W:(u + 1) * GRID_W, :] + bias_u, NEG))
        tiles.append(idx)
    sb = jnp.concatenate(parts, axis=0)
    sc = _dot_nt(qv, ks[pl.ds(0, n_ctx), :]) * scale
    m = jnp.maximum(jnp.max(sb, axis=1, keepdims=True), jnp.max(sc, axis=1, keepdims=True))
    eb, ec = jnp.exp(sb - m), jnp.exp(sc - m)
    inv = 1.0 / (jnp.sum(eb, axis=1, keepdims=True) + jnp.sum(ec, axis=1, keepdims=True))
    return eb * inv, ec * inv, qrows, krows, tiles


def _na_fwd(p, bias, gq, gk, cos, sin, n_ctx, off, HB):
    T = p.shape[0]
    L = T - n_ctx
    nh = HB // HEAD
    n_rows, kr = _na_geometry(L)
    ob = off // HEAD

    def body(q_ref, k_ref, v_ref, bias_ref, gq_ref, gk_ref, cos_ref, sin_ref, y_ref, qs, ks, vs):
        _na_prep(q_ref, k_ref, v_ref, gq_ref, gk_ref, cos_ref, sin_ref, qs, ks, vs, n_ctx, L)

        def step(i, carry):
            pb, pc, qrows, krows, _ = _na_scores(i, qs, ks, bias_ref, n_ctx, n_rows, kr)
            y = _dot(pb.astype(BF16), vs[krows, :]) + _dot(pc.astype(BF16), vs[pl.ds(0, n_ctx), :])
            y_ref[qrows, :] = y.astype(BF16)
            return carry

        lax.fori_loop(0, n_rows // NA_RB, step, 0)

    col = lambda kk: pl.BlockSpec((T, HEAD), lambda h: (0, ob + kk * nh + h))
    vec = pl.BlockSpec((1, HEAD), lambda h: (0, 0))
    tab = pl.BlockSpec((L, HEAD), lambda h: (0, 0))
    return _pcall(
        body, name="na_fwd", grid=(nh,),
        in_specs=[col(0), col(1), col(2), pl.BlockSpec((1,) + bias.shape[1:], lambda h: (h, 0, 0, 0)),
                  vec, vec, tab, tab],
        out_specs=pl.BlockSpec((L, HEAD), lambda h: (0, h)),
        out_shape=jax.ShapeDtypeStruct((L, HB), BF16),
        scratch_shapes=[pltpu.VMEM((L, HEAD), BF16), pltpu.VMEM((T, HEAD), BF16), pltpu.VMEM((T, HEAD), BF16)],
        compiler_params=_params(("parallel",)),
    )(p, p, p, bias, gq, gk, cos, sin)


def _na_bwd(p, bias, gq, gk, cos, sin, dyb, n_ctx, off, HB):
    T = p.shape[0]
    L = T - n_ctx
    nh = HB // HEAD
    n_rows, kr = _na_geometry(L)
    ob = off // HEAD
    scale = HEAD ** -0.5

    def body(q_ref, k_ref, v_ref, bias_ref, gq_ref, gk_ref, cos_ref, sin_ref, dy_ref,
             dq_ref, dk_ref, dv_ref, dbias_ref, dg_ref, qs, ks, vs, dqa, dka, dva):
        h = pl.program_id(0)
        _na_prep(q_ref, k_ref, v_ref, gq_ref, gk_ref, cos_ref, sin_ref, qs, ks, vs, n_ctx, L)
        dka[...] = jnp.zeros_like(dka)
        dva[...] = jnp.zeros_like(dva)
        dbias_ref[...] = jnp.zeros_like(dbias_ref)

        crows = pl.ds(0, n_ctx)

        def step(i, carry):
            pb, pc, qrows, krows, tiles = _na_scores(i, qs, ks, bias_ref, n_ctx, n_rows, kr)
            do = dy_ref[qrows, :]
            qv = qs[qrows, :]
            dpb = _dot_nt(do, vs[krows, :])
            dpc = _dot_nt(do, vs[crows, :])
            delta = jnp.sum(pb * dpb, axis=1, keepdims=True) + jnp.sum(pc * dpc, axis=1, keepdims=True)
            dsb = pb * (dpb - delta)
            dsc = pc * (dpc - delta)
            dsb_b, dsc_b = dsb.astype(BF16), dsc.astype(BF16)
            dqa[qrows, :] = (_dot(dsb_b, ks[krows, :]) + _dot(dsc_b, ks[crows, :])) * scale
            dka[krows, :] += _dot_tn(dsb_b, qv) * scale
            dka[crows, :] += _dot_tn(dsc_b, qv) * scale
            dva[krows, :] += _dot_tn(pb.astype(BF16), do)
            dva[crows, :] += _dot_tn(pc.astype(BF16), do)
            for u, idx in enumerate(tiles):
                for jj, t in enumerate(idx):
                    dbias_ref[0, t] += dsb[u * GRID_W:(u + 1) * GRID_W, jj * 2 * GRID_W:(jj + 1) * 2 * GRID_W]
            return carry

        lax.fori_loop(0, n_rows // NA_RB, step, 0)

        lane = lax.broadcasted_iota(jnp.int32, (L, HEAD), 1)
        cos, sin = cos_ref[...], sin_ref[...]
        lat, ctx = pl.ds(n_ctx, L), pl.ds(0, n_ctx)
        gqv, gkv = gq_ref[...], gk_ref[...]
        qh, rq = _qk_norm(q_ref[lat, :], None)
        dq, dgq = _qk_norm_bwd(_rope_bwd(dqa[...], cos, sin, lane), qh, rq, gqv)
        dq_ref[ctx, :] = jnp.zeros((n_ctx, HEAD), BF16)
        dq_ref[lat, :] = dq.astype(BF16)
        kh, rk = _qk_norm(k_ref[lat, :], None)
        dk, dgk = _qk_norm_bwd(_rope_bwd(dka[lat, :], cos, sin, lane), kh, rk, gkv)
        dk_ref[lat, :] = dk.astype(BF16)
        kch, rkc = _qk_norm(k_ref[ctx, :], None)
        dkc, dgkc = _qk_norm_bwd(dka[ctx, :], kch, rkc, gkv)
        dk_ref[ctx, :] = dkc.astype(BF16)
        dv_ref[...] = dva[...].astype(BF16)

        @pl.when(h == 0)
        def _():
            dg_ref[...] = jnp.zeros_like(dg_ref)

        dg_ref[0:1, :] += dgq
        dg_ref[1:2, :] += dgk + dgkc

    col = lambda kk: pl.BlockSpec((T, HEAD), lambda h: (0, ob + kk * nh + h))
    vec = pl.BlockSpec((1, HEAD), lambda h: (0, 0))
    tab = pl.BlockSpec((L, HEAD), lambda h: (0, 0))
    tcol = pl.BlockSpec((T, HEAD), lambda h: (0, h))
    bspec = pl.BlockSpec((1,) + bias.shape[1:], lambda h: (h, 0, 0, 0))
    return _pcall(
        body, name="na_bwd", grid=(nh,),
        in_specs=[col(0), col(1), col(2), bspec, vec, vec, tab, tab, pl.BlockSpec((L, HEAD), lambda h: (0, h))],
        out_specs=[tcol, tcol, tcol, bspec, pl.BlockSpec((8, HEAD), lambda h: (0, 0))],
        out_shape=[jax.ShapeDtypeStruct((T, HB), BF16)] * 3 + [jax.ShapeDtypeStruct(bias.shape, F32),
                                                               jax.ShapeDtypeStruct((8, HEAD), F32)],
        scratch_shapes=[pltpu.VMEM((L, HEAD), BF16), pltpu.VMEM((T, HEAD), BF16), pltpu.VMEM((T, HEAD), BF16),
                        pltpu.VMEM((L, HEAD), F32), pltpu.VMEM((T, HEAD), F32), pltpu.VMEM((T, HEAD), F32)],
        compiler_params=_params(("arbitrary",)),
    )(p, p, p, bias, gq, gk, cos, sin, dyb)


def _bias_tables():
    w = np.arange(GRID_W)
    col_start = np.clip(w - WIN_C // 2, 0, GRID_W - WIN_C)
    col_in = (w[None, :] >= col_start[:, None]) & (w[None, :] < col_start[:, None] + WIN_C)
    dc = np.clip(w[None, :] - w[:, None], -(WIN_C - 1), WIN_C - 1) + WIN_C - 1
    n_pair = 2 * WIN_R
    ridx = np.zeros((n_pair, GRID_W, 2 * GRID_W), np.int32)
    cidx = np.zeros((n_pair, GRID_W, 2 * GRID_W), np.int32)
    valid = np.zeros((n_pair, GRID_W, 2 * GRID_W), bool)
    for i in range(n_pair):
        for half in range(2):
            row = i + half
            sl = slice(half * GRID_W, (half + 1) * GRID_W)
            ridx[i, :, sl] = min(row, 2 * WIN_R - 2)
            cidx[i, :, sl] = dc
            valid[i, :, sl] = col_in & (row <= 2 * WIN_R - 2)
    return ridx, cidx, valid


def _bias_onehot():
    _, cidx, valid = _bias_tables()
    K = GRID_W * 2 * GRID_W
    oh = np.zeros((K, 128), np.float32)
    neg = np.full((1, K), NEG, np.float32)
    for cq in range(GRID_W):
        for ll in range(2 * GRID_W):
            if valid[0, cq, ll]:
                oh[cq * 2 * GRID_W + ll, (ll // GRID_W) * 64 + cidx[0, cq, ll]] = 1.0
                neg[0, cq * 2 * GRID_W + ll] = 0.0
    return oh, neg


def _expand_bias(table):
    H = table.shape[0]
    n_pair, n_dc = 2 * WIN_R, 2 * WIN_C - 1
    tp = jnp.pad(table, ((0, 0), (0, n_pair + 1 - table.shape[1]), (0, 64 - n_dc)))
    t2 = jnp.concatenate([tp[:, :n_pair], tp[:, 1:n_pair + 1]], axis=-1).reshape(H * n_pair, 128)
    oh, neg = _bias_onehot()

    def body(t_ref, oh_ref, neg_ref, o_ref):
        o_ref[...] = lax.dot_general(t_ref[...], oh_ref[...], (((1,), (1,)), ((), ())), precision=HI,
                                     preferred_element_type=F32) + neg_ref[...]

    out = _pcall(body, name="bias_expand", out_shape=jax.ShapeDtypeStruct((H * n_pair, oh.shape[0]), F32),
                         compiler_params=_params())(t2, jnp.asarray(oh), jnp.asarray(neg))
    return out.reshape(H, n_pair, GRID_W, 2 * GRID_W)


def _bias_grad(dbias):
    H = dbias.shape[0]
    n_pair, n_dc = 2 * WIN_R, 2 * WIN_C - 1
    K = GRID_W * 2 * GRID_W
    oh, _ = _bias_onehot()
    flat = dbias.reshape(H * n_pair, K)

    def body(d_ref, oh_ref, o_ref):
        o_ref[...] = jnp.dot(d_ref[...], oh_ref[...], precision=HI, preferred_element_type=F32)

    g = _pcall(body, name="bias_grad", out_shape=jax.ShapeDtypeStruct((H * n_pair, 128), F32),
                       compiler_params=_params())(flat, jnp.asarray(oh))
    g = g.reshape(H, n_pair, 128)
    left, right = g[:, :, :n_dc], g[:, :, 64:64 + n_dc]
    out = left[:, :n_pair - 1]
    return out.at[:, 1:].add(right[:, :n_pair - 2])


def _rope_tables(L):
    pos = np.arange(L)
    row = (pos // GRID_W).astype(np.float32)
    colp = (pos % GRID_W).astype(np.float32)
    half = HEAD // 2
    nf = half // 2
    inv = (ROPE_THETA ** (-np.arange(nf, dtype=np.float32) / nf)).astype(np.float32)

    def tabs(pv):
        ang = pv[:, None] * inv[None, :]
        c, s = np.cos(ang), np.sin(ang)
        return np.concatenate([c, c], axis=1), np.concatenate([-s, s], axis=1)

    cr, sr = tabs(row)
    cc, sc = tabs(colp)
    return (jnp.asarray(np.concatenate([cr, cc], axis=1), F32), jnp.asarray(np.concatenate([sr, sc], axis=1), F32))


def _adamw(w, g, m, v, name, after=None, copy_g=False):
    R, C = w.shape
    tr = _row_tile(R, C)
    c1 = 1.0 - ADAM_B1 ** ADAM_STEP
    c2 = 1.0 - ADAM_B2 ** ADAM_STEP
    deps = [] if after is None else [after]
    n_out = 4 if copy_g else 3

    def body(w_ref, g_ref, m_ref, v_ref, *rest):
        d_ref, mo_ref, vo_ref = rest[len(deps):len(deps) + 3]
        gv = g_ref[...]
        mn = ADAM_B1 * m_ref[...] + (1.0 - ADAM_B1) * gv
        vn = ADAM_B2 * v_ref[...] + (1.0 - ADAM_B2) * (gv * gv)
        mo_ref[...] = mn
        vo_ref[...] = vn
        d_ref[...] = -ADAM_LR * ((mn / c1) / (jnp.sqrt(vn / c2) + ADAM_EPS) + ADAM_WD * w_ref[...])
        if copy_g:
            rest[-1][...] = gv

    blk = pl.BlockSpec((tr, C), lambda i: (i, 0))
    return _pcall(
        body, name=name, grid=(R // tr,),
        in_specs=[blk] * 4 + [_ANY] * len(deps), out_specs=[blk] * n_out,
        out_shape=[jax.ShapeDtypeStruct((R, C), F32)] * n_out,
        compiler_params=_params(("parallel",)),
    )(w, g, m, v, *deps)


PACK_W = 1024


def _pack(parts):
    flat, offs, pos = [], [], 0
    for a in parts:
        n = a.size
        padn = -n % PACK_W
        flat.append(jnp.pad(a.reshape(-1).astype(F32), (0, padn)))
        offs.append((pos, n, a.shape))
        pos += n + padn
    tail = -pos % (8 * PACK_W)
    if tail:
        flat.append(jnp.zeros((tail,), F32))
    return jnp.concatenate(flat).reshape(-1, PACK_W), offs


def _unpack(buf, offs, i):
    pos, n, shape = offs[i]
    return buf.reshape(buf.shape[:-2] + (-1,))[..., pos:pos + n].reshape(buf.shape[:-2] + shape)


def kernel(x, c, ctx, c_ctx, ada_w, ada_b, norm1_g, norm2_g, w_in, hgrn_lb_logits, hgrn_norm_g, na_q_norm_g, na_k_norm_g, na_rel_bias, w_branch_a, w_branch_b, w_out, ffn_w1, ffn_w3, ffn_conv_w, ffn_conv_b, ffn_w2, loss_target, m_c_ctx, m_ada_w, m_ada_b, m_norm1_g, m_norm2_g, m_w_in, m_hgrn_lb_logits, m_hgrn_norm_g, m_na_q_norm_g, m_na_k_norm_g, m_na_rel_bias, m_w_branch_a, m_w_branch_b, m_w_out, m_ffn_w1, m_ffn_w3, m_ffn_conv_w, m_ffn_conv_b, m_ffn_w2, v_c_ctx, v_ada_w, v_ada_b, v_norm1_g, v_norm2_g, v_w_in, v_hgrn_lb_logits, v_hgrn_norm_g, v_na_q_norm_g, v_na_k_norm_g, v_na_rel_bias, v_w_branch_a, v_w_branch_b, v_w_out, v_ffn_w1, v_ffn_w3, v_ffn_conv_w, v_ffn_conv_b, v_ffn_w2):
    weights = dict(c_ctx=c_ctx, ada_w=ada_w, ada_b=ada_b, norm1_g=norm1_g, norm2_g=norm2_g, w_in=w_in,
                   hgrn_lb_logits=hgrn_lb_logits, hgrn_norm_g=hgrn_norm_g, na_q_norm_g=na_q_norm_g,
                   na_k_norm_g=na_k_norm_g, na_rel_bias=na_rel_bias, w_branch_a=w_branch_a, w_branch_b=w_branch_b,
                   w_out=w_out, ffn_w1=ffn_w1, ffn_w3=ffn_w3, ffn_conv_w=ffn_conv_w, ffn_conv_b=ffn_conv_b,
                   ffn_w2=ffn_w2)
    moms = dict(c_ctx=(m_c_ctx, v_c_ctx), ada_w=(m_ada_w, v_ada_w), ada_b=(m_ada_b, v_ada_b),
                norm1_g=(m_norm1_g, v_norm1_g), norm2_g=(m_norm2_g, v_norm2_g), w_in=(m_w_in, v_w_in),
                hgrn_lb_logits=(m_hgrn_lb_logits, v_hgrn_lb_logits), hgrn_norm_g=(m_hgrn_norm_g, v_hgrn_norm_g),
                na_q_norm_g=(m_na_q_norm_g, v_na_q_norm_g), na_k_norm_g=(m_na_k_norm_g, v_na_k_norm_g),
                na_rel_bias=(m_na_rel_bias, v_na_rel_bias), w_branch_a=(m_w_branch_a, v_w_branch_a),
                w_branch_b=(m_w_branch_b, v_w_branch_b), w_out=(m_w_out, v_w_out), ffn_w1=(m_ffn_w1, v_ffn_w1),
                ffn_w3=(m_ffn_w3, v_ffn_w3), ffn_conv_w=(m_ffn_conv_w, v_ffn_conv_w),
                ffn_conv_b=(m_ffn_conv_b, v_ffn_conv_b), ffn_w2=(m_ffn_w2, v_ffn_w2))
    order = list(weights)

    L, D = x.shape[1], x.shape[2]
    N = ctx.shape[1]
    T = N + L
    HA = w_branch_a.shape[1]
    HB = w_branch_b.shape[1]
    F = ffn_conv_b.shape[1]
    IN = 5 * HA + 3 * HB + 2 * D
    n_ada = ada_w.shape[2]
    ix, iy, ic = _pos()
    chip = 2 * ix + iy
    dev = 2 * chip + ic

    _PENDING.clear()
    pk0, offs0 = _pack([c[0], hgrn_lb_logits, ffn_conv_w[0]])
    g0 = _allgather8(pk0, "gather_small0")
    c_all = _unpack(g0, offs0, 0)
    lbl_parts = _unpack(g0, offs0, 1)
    lbl = jnp.concatenate([lbl_parts[2 * j] for j in range(N_CHIP)], axis=-1)
    cw_parts = _unpack(g0, offs0, 2)
    cw = jnp.concatenate([cw_parts[2 * j] for j in range(N_CHIP)], axis=-1)
    cw8 = jnp.pad(cw, ((0, 5), (0, 0)))

    cs = jnp.concatenate([c_all, c_ctx[None, :], jnp.zeros((7, D), F32)], axis=0)
    ada_b_mine = lax.dynamic_slice(ada_b, (0, chip * n_ada), (1, n_ada))
    mod_mine = _ada_fwd(cs, ada_w[0], ada_b_mine)
    gm = _allgather8(mod_mine, "gather_mod")
    mod = jnp.concatenate([gm[2 * j] for j in range(N_CHIP)], axis=-1)
    mod_l = lax.dynamic_slice(mod, (dev, 0), (1, N_MOD * D)).reshape(N_MOD, D)
    mod_c = mod[8].reshape(N_MOD, D)
    sh1, sc1, g1, sh2, sc2, g2 = [mod_l[i:i + 1] for i in range(N_MOD)]
    shift1 = jnp.concatenate([mod_c[0:1], sh1], axis=0)
    scale1 = jnp.concatenate([mod_c[1:2], sc1], axis=0)

    shards = [w_in[0], w_branch_a[0], w_branch_b[0], w_out[0], ffn_w1[0], ffn_w3[0], ffn_w2[0]]
    names = ["w_in", "w_a", "w_b", "w_out", "w1", "w3", "w2"]
    slots = [_cast_bf16_slot(s, "cast_" + nm) for s, nm in zip(shards, names)]
    sem_nb, win_buf = _xfer_start("gather_ici_start_in_nbr", slots[0:1], _plan_gather_ici(NEIGHBOURS), 2, gm)

    xall = jnp.concatenate([ctx[0], x[0]], axis=0)
    h_all = _rms1_fwd(xall, norm1_g, shift1, scale1, N)
    chip_i = chip.astype(jnp.int32)
    same = lambda ids: jnp.stack([jnp.stack(ids), jnp.stack(ids)])
    p = _mm_nn_sel(h_all, win_buf[0], same([chip_i]), F32, "mm_p_own")
    win_buf = _xfer_wait("gather_ici_wait_in_nbr", sem_nb, win_buf, _plan_gather_ici(NEIGHBOURS),
                         (p, *slots[1:]))
    sem_nb, win_buf = _xfer_start("gather_d2d_start_in_nbr", win_buf, _plan_gather_d2d(NEIGHBOURS), 2)
    sem_dg, win_buf = _xfer_start("gather_ici_start_in_diag", win_buf, _plan_gather_ici(DIAGONAL), 1)
    gat_mix = _gather_start("mix", slots[1:4])
    gat_ffn = _gather_start("ffn", slots[4:6])
    gat_ffn2 = _gather_start("ffn2", slots[6:7])
    win_buf = _xfer_wait("gather_d2d_wait_in_nbr", sem_nb, win_buf, _plan_gather_d2d(NEIGHBOURS), _PENDING[0])
    p = _mm_nn_sel(h_all, win_buf[0], same([chip_i ^ 1, chip_i ^ 2]), F32, "mm_p_nbr", p)
    win_buf = _d2d_hand_over("in_diag", sem_dg, win_buf, DIAGONAL, p)
    p = _mm_nn_sel(h_all, win_buf[0], same([chip_i ^ 3]), F32, "mm_p_diag", p)
    Win = win_buf[0]
    bias = _expand_bias(na_rel_bias[0])
    cos, sin = _rope_tables(L)
    off_na = 5 * HA
    y_b = _na_fwd(p, bias, na_q_norm_g, na_k_norm_g, cos, sin, N, off_na, HB)
    gat_mix = _gather_mid(gat_mix, y_b)
    y_a, o_a, st_a = _hgrn_fwd(p, lbl, hgrn_norm_g, N, HA)
    Wa, Wb, Wo = _gather_finish(gat_mix, (y_a, y_b))
    Wo = Wo.reshape(1, D, D)
    za = _mm_nn(y_a, Wa, BF16, "mm_za")
    zb = _mm_nn(y_b, Wb, BF16, "mm_zb")
    off_ga, off_gb = 5 * HA + 3 * HB, 5 * HA + 3 * HB + D
    z = _merge_fwd(za, zb, p, N, off_ga, off_gb)
    gat_ffn = _gather_mid(gat_ffn, z)
    mo = _mm_nn(z, Wo, F32, "mm_mo")
    vec2 = jnp.concatenate([g1, norm2_g, sh2, sc2, jnp.zeros((4, D), F32)], axis=0)
    x_mid, h2 = _resid_rms2_fwd(x[0], mo, vec2)
    W1, W3 = _gather_finish(gat_ffn, h2)
    gat_ffn2 = _gather_mid(gat_ffn2, h2)
    u1 = _mm_nn(h2, W1, BF16, "mm_u1")
    u3 = _mm_nn(h2, W3, BF16, "mm_u3")
    (W2,) = _gather_finish(gat_ffn2, (u1, u3))
    W2 = W2.reshape(1, F, D)
    a = _convgate_fwd(u1, u3, cw8, ffn_conv_b)
    f = _mm_nn(a, W2, F32, "mm_f")
    dy, df, s_loss = _loss_head(x_mid, f, g2, loss_target[0])
    loss = lax.psum(s_loss[1, 0], ("x", "y", "c"))
    d_g2 = s_loss[0:1]

    gW2 = _mm_tn(a, df, 1, "mm_gw2").reshape(N_CHIP, F // N_CHIP, D)
    da = _mm_nt(df, W2, BF16, "mm_da")
    du1, du3, s_conv = _convgate_bwd(u1, u3, da, cw8, ffn_conv_b)
    gW1 = _mm_tn(h2, du1, N_CHIP, "mm_gw1")
    gW3 = _mm_tn(h2, du3, N_CHIP, "mm_gw3")
    rs_ffn = _rs_start("ffn", [gW2, gW1, gW3])
    dh2a = _mm_nt(du1, W1, F32, "mm_dh2a")
    dh2b = _mm_nt(du3, W3, F32, "mm_dh2b")
    rs_ffn = _rs_scatter(rs_ffn, dh2b)
    dxm, dmo, s_rms2 = _resid_rms2_bwd(x_mid, dh2a, dh2b, dy, mo, vec2)
    gWo = _mm_tn(z, dmo, 1, "mm_gwo").reshape(N_CHIP, D // N_CHIP, D)
    dz = _mm_nt(dmo, Wo, BF16, "mm_dz")
    dza, dzb, dga, dgb = _merge_bwd(dz, za, zb, p, N, off_ga, off_gb)
    gWa = _mm_tn(y_a, dza, N_CHIP, "mm_gwa")
    gWb = _mm_tn(y_b, dzb, N_CHIP, "mm_gwb")
    rs_mix = _rs_start("mix", [gWo, gWa, gWb])
    dya = _mm_nt(dza, Wa, F32, "mm_dya")
    dyb = _mm_nt(dzb, Wb, BF16, "mm_dyb")
    rs_mix = _rs_scatter(rs_mix, dyb)
    dq_a, dzf, dzbk, di_a, dog, dlbl, s_ng = _hgrn_bwd(p, lbl, hgrn_norm_g, o_a, dya, st_a, N, HA)
    rs_ffn = _rs_join(rs_ffn, dq_a)
    dq_n, dk_n, dv_n, dbias, s_qk = _na_bwd(p, bias, na_q_norm_g, na_k_norm_g, cos, sin, dyb, N, off_na, HB)
    rs_mix = _rs_join(rs_mix, dq_n)
    dp = jnp.concatenate([dq_a, dzf, dzbk, di_a, dog, dq_n, dk_n, dv_n, dga, dgb], axis=1)
    gWin = _mm_tn(h_all, dp, N_CHIP, "mm_gwin")
    rs_in = _rs_start("in", [gWin])
    rs_in = _rs_scatter(rs_in, _PENDING[0])
    dh = _mm_nt(dp, Win, F32, "mm_dh")
    grad_x, s_rms1 = _rms1_bwd(xall, dh, dxm, norm1_g, scale1, N)
    d_table = _bias_grad(dbias)

    grads = {}
    big_names = ["ada_w", "w_in", "w_branch_a", "w_branch_b", "w_out", "ffn_w1", "ffn_w3", "ffn_w2"]
    small_names = [n for n in order if n not in big_names]
    delta, new_m, new_v = {}, {}, {}

    def update(nm, after=None):
        reduced = nm != "ada_w"
        d_, m_, v_, *g_ = _adamw(weights[nm][0], grads[nm][0], moms[nm][0][0], moms[nm][1][0], "adamw_" + nm,
                                 after, copy_g=reduced)
        delta[nm], new_m[nm], new_v[nm] = d_[None], m_[None], v_[None]
        if reduced:
            grads[nm] = g_[0][None]
        return d_

    last = grad_x
    for nm, g in zip(["ffn_w2", "ffn_w1", "ffn_w3"], _rs_finish(rs_ffn, last)):
        grads[nm] = g[None]
        last = update(nm, last)
    for nm, g in zip(["w_out", "w_branch_a", "w_branch_b"], _rs_finish(rs_mix, last)):
        grads[nm] = g[None]
        last = update(nm, last)
    rs_in = _rs_join(rs_in, last)

    zD = jnp.zeros((1, D), F32)
    dmod_l = jnp.concatenate([s_rms1[2:3], s_rms1[3:4], s_rms2[3:4], s_rms2[0:1], s_rms2[1:2], d_g2], axis=0)
    dmod_c = jnp.concatenate([s_rms1[0:1], s_rms1[1:2], zD, zD, zD, zD], axis=0)
    pk1, offs1 = _pack([dmod_l, dmod_c, s_rms1[4], s_rms2[2], dlbl, s_ng[0], s_qk[0], s_qk[1], d_table,
                        s_conv[0:3], s_conv[3]])
    g1all = _allgather8(pk1, "gather_small1")
    tot1 = _sum8(g1all, "sum_small1")
    dmod_rows = _unpack(g1all, offs1, 0).reshape(N_DEV, N_MOD * D)
    dmod_c_tot = _unpack(tot1, offs1, 1).reshape(1, N_MOD * D)
    dmod16 = jnp.concatenate([dmod_rows, dmod_c_tot, jnp.zeros((7, N_MOD * D), F32)], axis=0)
    dmod16_mine = lax.dynamic_slice(dmod16, (0, chip * n_ada), (16, n_ada))
    g_ada_w, dact = _ada_bwd(cs, ada_w[0], dmod16_mine)
    pk2, offs2 = _pack([dact[8]])
    g2all = _allgather8(pk2, "gather_small2")
    dact_rows = _unpack(g2all, offs2, 0)
    dact_sel = jnp.concatenate([dact_rows[2 * j][None] for j in range(N_CHIP)] + [jnp.zeros((4, D), F32)], axis=0)

    grads["ada_w"] = g_ada_w[None]
    grads["ada_b"] =(_unpack(tot1, offs1, 0) + _unpack(tot1, offs1, 1)).reshape(1, N_MOD * D)
    grads["norm1_g"] = _unpack(tot1, offs1, 2)[None]
    grads["norm2_g"] = _unpack(tot1, offs1, 3)[None]
    g_lbl = _unpack(tot1, offs1, 4)
    n_lb = HA // N_CHIP
    grads["hgrn_lb_logits"] = lax.dynamic_slice(g_lbl, (0, 0, chip * n_lb), (2, 2, n_lb))
    grads["hgrn_norm_g"] = _unpack(tot1, offs1, 5)[None]
    grads["na_q_norm_g"] = _unpack(tot1, offs1, 6)[None]
    grads["na_k_norm_g"] = _unpack(tot1, offs1, 7)[None]
    grads["na_rel_bias"] = _unpack(tot1, offs1, 8)[None]
    g_cw = _unpack(tot1, offs1, 9)
    n_f = F // N_CHIP
    grads["ffn_conv_w"] = lax.dynamic_slice(g_cw, (0, chip * n_f), (3, n_f))[None]
    grads["ffn_conv_b"] = _unpack(tot1, offs1, 10)[None]

    g_c_ctx = _dsilu_rows(dact_sel, c_ctx[None, :], "grad_c_ctx")
    grads["c_ctx"] = g_c_ctx[0]

    last = update("ada_w", g_c_ctx)
    pw, offw = _pack([weights[n] for n in small_names])
    pg, _ = _pack([grads[n] for n in small_names])
    pm, _ = _pack([moms[n][0] for n in small_names])
    pv, _ = _pack([moms[n][1] for n in small_names])
    d_, m_, v_ = _adamw(pw, pg, pm, pv, "adamw_small", last)
    for i, nm in enumerate(small_names):
        delta[nm], new_m[nm], new_v[nm] = _unpack(d_, offw, i), _unpack(m_, offw, i), _unpack(v_, offw, i)
    grads["w_in"] = _rs_finish(rs_in, d_)[0][None]
    update("w_in")

    return (loss, grad_x[None], *[grads[n] for n in order], *[delta[n] for n in order],
            *[new_m[n] for n in order], *[new_v[n] for n in order])


def _dsilu_rows(v, cv, name):
    D = v.shape[1]

    def body(v_ref, c_ref, o_ref):
        t = c_ref[...]
        s = _sigmoid(t)
        o_ref[...] = (((v_ref[0:1, :] + v_ref[1:2, :]) + v_ref[2:3, :]) + v_ref[3:4, :]) * (s * (1.0 + t * (1.0 - s)))

    return _pcall(body, name=name, out_shape=jax.ShapeDtypeStruct((1, D), F32),
                          compiler_params=_params())(v, cv)
```

```python
import functools

import numpy as np
import jax
import jax.numpy as jnp
from jax import lax
from jax.experimental import pallas as pl
from jax.experimental.pallas import tpu as pltpu

F32 = jnp.float32
BF16 = jnp.bfloat16
MESH = pl.DeviceIdType.MESH

HEAD = 128
GRID_W = 64
WIN_R = 8
WIN_C = 16
ROPE_THETA = 10000.0
EPS = 1e-6
N_MOD = 6
CHUNK = 16
HGRN_UNROLL = 4
ADAM_LR = 0.001
ADAM_B1 = 0.9
ADAM_B2 = 0.999
ADAM_EPS = 1e-08
ADAM_WD = 0.01
ADAM_STEP = 10
NEG = -1e30
VMEM_LIMIT = 56 * 1024 * 1024
N_DEV = 8
N_CHIP = 4
HI = lax.Precision.HIGHEST


def _pick(n, cands):
    for c in cands:
        if n % c == 0:
            return c
    return n


def _row_tile(rows, cols, target_bytes=1 << 20):
    want = max(16, target_bytes // (4 * cols))
    for t in (512, 256, 128, 64, 32, 16, 8):
        if t <= want and rows % t == 0:
            return t
    return rows


def _params(sem=None):
    return pltpu.CompilerParams(dimension_semantics=sem, vmem_limit_bytes=VMEM_LIMIT)


def _dot(a, b):
    return jnp.dot(a, b, preferred_element_type=F32)


def _dot_nt(a, b):
    return lax.dot_general(a, b, (((1,), (1,)), ((), ())), preferred_element_type=F32)


def _dot_tn(a, b):
    return lax.dot_general(a, b, (((0,), (0,)), ((), ())), preferred_element_type=F32)


def _sigmoid(x):
    return 1.0 / (1.0 + jnp.exp(-x))


def _col_tile(n):
    return n if n <= 1536 else _pick(n, (1024, 768, 512, 384, 256, 128))


def _mm_nn(x, w3, out_dtype, name):
    M, K = x.shape
    S, _, n = w3.shape
    tm = _pick(M, (768, 512, 256, 128, 64))
    tn = _col_tile(n)
    nb = n // tn

    def body(x_ref, w_ref, o_ref):
        o_ref[...] = _dot(x_ref[...].astype(BF16), w_ref[0]).astype(o_ref.dtype)

    return _pcall(
        body, name=name, grid=(M // tm, S * nb),
        in_specs=[pl.BlockSpec((tm, K), lambda i, j: (i, 0)),
                  pl.BlockSpec((1, K, tn), lambda i, j: (j // nb, 0, j % nb))],
        out_specs=pl.BlockSpec((tm, tn), lambda i, j: (i, j)),
        out_shape=jax.ShapeDtypeStruct((M, S * n), out_dtype),
        compiler_params=_params(("parallel", "parallel")),
    )(x, w3)


def _mm_nn_sel(x, w3, sel, out_dtype, name, prev=None):
    M, K = x.shape
    S, _, n = w3.shape
    tm = _pick(M, (768, 512, 256, 128, 64))
    tn = _col_tile(n)
    nb = n // tn
    k = sel.shape[1]

    def body(sel_ref, x_ref, w_ref, *rest):
        rest[-1][...] = _dot(x_ref[...].astype(BF16), w_ref[0]).astype(out_dtype)

    in_specs = [pl.BlockSpec((tm, K), lambda i, j, sel_ref: (i, 0)),
                pl.BlockSpec((1, K, tn), lambda i, j, sel_ref: (sel_ref[0, j // nb], 0, j % nb))]
    operands = [sel, x, w3]
    if prev is not None:
        in_specs.append(_ANY)
        operands.append(prev)
    return pl.pallas_call(
        body, name=name,
        grid_spec=pltpu.PrefetchScalarGridSpec(
            num_scalar_prefetch=1, grid=(M // tm, k * nb), in_specs=in_specs,
            out_specs=pl.BlockSpec((tm, tn), lambda i, j, sel_ref: (i, sel_ref[1, j // nb] * nb + j % nb))),
        out_shape=jax.ShapeDtypeStruct((M, S * n), out_dtype),
        input_output_aliases={} if prev is None else {3: 0},
        compiler_params=_params(("parallel", "parallel")),
    )(*operands)


def _mm_nt(dy, w3, out_dtype, name):
    M = dy.shape[0]
    S, K, n = w3.shape
    tm = _pick(M, (768, 512, 256, 128, 64))
    tk = K if K <= 2048 else _pick(K, (1408, 1024, 512, 256, 128))
    tc = n if n <= 2048 else _col_tile(n)
    nb = n // tc
    nsteps = S * nb

    def body(dy_ref, w_ref, o_ref, acc_ref):
        s = pl.program_id(2)

        @pl.when(s == 0)
        def _():
            acc_ref[...] = jnp.zeros_like(acc_ref)

        acc_ref[...] += _dot_nt(dy_ref[...].astype(BF16), w_ref[0])

        @pl.when(s == nsteps - 1)
        def _():
            o_ref[...] = acc_ref[...].astype(o_ref.dtype)

    return _pcall(
        body, name=name, grid=(M // tm, K // tk, nsteps),
        in_specs=[pl.BlockSpec((tm, tc), lambda i, k, s: (i, s)),
                  pl.BlockSpec((1, tk, tc), lambda i, k, s: (s // nb, k, s % nb))],
        out_specs=pl.BlockSpec((tm, tk), lambda i, k, s: (i, k)),
        out_shape=jax.ShapeDtypeStruct((M, K), out_dtype),
        scratch_shapes=[pltpu.VMEM((tm, tk), F32)],
        compiler_params=_params(("parallel", "parallel", "arbitrary")),
    )(dy, w3)


def _mm_tn(x, dy, S, name):
    M, K = x.shape
    n = dy.shape[1] // S
    tk = _pick(K, (512, 256, 128))
    tn = _col_tile(n)
    nb = n // tn

    def body(x_ref, dy_ref, o_ref):
        o_ref[0] = _dot_tn(x_ref[...].astype(BF16), dy_ref[...].astype(BF16)).astype(BF16)

    return _pcall(
        body, name=name, grid=(S * nb, K // tk),
        in_specs=[pl.BlockSpec((M, tk), lambda j, k: (0, k)),
                  pl.BlockSpec((M, tn), lambda j, k: (0, j))],
        out_specs=pl.BlockSpec((1, tk, tn), lambda j, k: (j // nb, k, j % nb)),
        out_shape=jax.ShapeDtypeStruct((S, K, n), BF16),
        compiler_params=_params(("parallel", "parallel")),
    )(x, dy)


def _chip_index():
    return (2 * lax.axis_index("x") + lax.axis_index("y")).astype(jnp.int32).reshape(1)


def _cast_bf16_slot(w, name):
    R, C = w.shape
    tr = _row_tile(R, C, 2 << 20)

    def body(j_ref, w_ref, o_ref):
        o_ref[0] = w_ref[...].astype(BF16)

    return _pcall(
        body, name=name,
        grid_spec=pltpu.PrefetchScalarGridSpec(
            num_scalar_prefetch=1, grid=(R // tr,),
            in_specs=[pl.BlockSpec((tr, C), lambda i, j_ref: (i, 0))],
            out_specs=pl.BlockSpec((1, tr, C), lambda i, j_ref: (j_ref[0], i, 0))),
        out_shape=jax.ShapeDtypeStruct((N_CHIP, R, C), BF16),
        compiler_params=_params(("parallel",)),
    )(_chip_index(), w)


def _pos():
    return lax.axis_index("x"), lax.axis_index("y"), lax.axis_index("c")


def _other_chips(x, y):
    return [(x, 1 - y), (1 - x, y), (1 - x, 1 - y)]


def _allgather8(v, name):
    R, C = v.shape

    def body(x_ref, out_ref, send_sems, recv_sems, local_sem):
        x, y, c = _pos()
        me, sibling = (x, y, c), (x, y, 1 - c)
        chips = _other_chips(x, y)

        def slot(px, py, pc):
            return out_ref.at[4 * px + 2 * py + pc]

        def copy(k, block, to, src=None):
            return pltpu.make_async_remote_copy(
                src_ref=slot(*block) if src is None else src, dst_ref=slot(*block),
                send_sem=send_sems.at[k], recv_sem=recv_sems.at[k], device_id=to, device_id_type=MESH)

        mine = pltpu.make_async_copy(x_ref, slot(*me), local_sem)
        mine.start()
        first = [copy(0, me, sibling, src=x_ref)]
        first += [copy(1 + j, me, (*chip, c), src=x_ref) for j, chip in enumerate(chips)]
        for cp in first:
            cp.start()
        passed = [copy(4 + j, (*chip, c), sibling) for j, chip in enumerate(chips)]
        for j, chip in enumerate(chips):
            copy(1 + j, (*chip, c), me).wait_recv()
            passed[j].start()
        copy(0, sibling, me).wait_recv()
        for j, chip in enumerate(chips):
            copy(4 + j, (*chip, 1 - c), me).wait_recv()
        for cp in first + passed:
            cp.wait_send()
        mine.wait()

    return _pcall(
        body, name=name,
        out_shape=jax.ShapeDtypeStruct((N_DEV, R, C), v.dtype),
        in_specs=[pl.BlockSpec(memory_space=pltpu.VMEM)],
        out_specs=pl.BlockSpec(memory_space=pltpu.VMEM),
        scratch_shapes=[pltpu.SemaphoreType.DMA((7,)), pltpu.SemaphoreType.DMA((7,)), pltpu.SemaphoreType.DMA],
        compiler_params=pltpu.CompilerParams(vmem_limit_bytes=VMEM_LIMIT),
    )(v)


_HBM = pl.BlockSpec(memory_space=pltpu.HBM)
_SEM = pl.BlockSpec(memory_space=pltpu.SEMAPHORE)
_ANY = pl.BlockSpec(memory_space=pl.ANY)
_EFFECT = pltpu.SideEffectType.DATAFLOW_SIDE_EFFECTING
_PENDING = []


def _pcall(body, **kw):
    def run(*operands):
        if not _PENDING or "in_specs" not in kw:
            return pl.pallas_call(body, **kw)(*operands)
        deps = list(_PENDING)
        n = len(operands)

        def tied(*refs):
            return body(*refs[:n], *refs[n + len(deps):])

        return pl.pallas_call(tied, **{**kw, "in_specs": list(kw["in_specs"]) + [_ANY] * len(deps)})(*operands, *deps)
    return run


def _copies(plan, refs, send_sems, recv_sems):
    return [pltpu.make_async_remote_copy(src_ref=src, dst_ref=dst, send_sem=send_sems.at[k], recv_sem=recv_sems.at[k],
                                         device_id=dev, device_id_type=MESH)
            for k, (src, dst, dev) in enumerate(plan(refs))]


def _xfer_start(name, bufs, plan, n_copies, after=None):
    n = len(bufs)
    deps = list(_PENDING) + ([after] if after is not None else [])
    nd = len(deps)

    def body(*refs):
        for cp in _copies(plan, refs[:n], refs[n + nd], refs[n + nd + 1]):
            cp.start()
        refs[-1][...] = jnp.zeros_like(refs[-1])

    outs = pl.pallas_call(
        body, name=name,
        out_shape=(pltpu.SemaphoreType.DMA((n_copies,)), pltpu.SemaphoreType.DMA((n_copies,)),
                   *[pltpu.HBM(b.shape, b.dtype) for b in bufs], jax.ShapeDtypeStruct((8, 128), F32)),
        in_specs=[_HBM] * n + [_ANY] * nd,
        out_specs=(_SEM, _SEM, *[_HBM] * n, pl.BlockSpec(memory_space=pltpu.VMEM)),
        input_output_aliases={t: 2 + t for t in range(n)},
        compiler_params=pltpu.CompilerParams(has_side_effects=_EFFECT),
    )(*[pltpu.with_memory_space_constraint(b, pltpu.HBM) for b in bufs], *deps)
    _PENDING[:] = [outs[-1]]
    return (outs[0], outs[1]), list(outs[2:2 + n])


def _xfer_wait(name, sems, bufs, plan, after):
    n = len(bufs)
    after = tuple(after) if isinstance(after, (tuple, list)) else (after,)

    def body(*refs):
        cps = _copies(plan, refs[:n], refs[n], refs[n + 1])
        for cp in cps:
            cp.wait_send()
        for cp in cps:
            cp.wait_recv()

    outs = pl.pallas_call(
        body, name=name,
        out_shape=tuple(pltpu.HBM(b.shape, b.dtype) for b in bufs),
        in_specs=[_HBM] * n + [_SEM, _SEM] + [_ANY] * len(after),
        out_specs=tuple([_HBM] * n),
        input_output_aliases={t: t for t in range(n)},
        compiler_params=pltpu.CompilerParams(has_side_effects=_EFFECT),
    )(*bufs, sems[0], sems[1], *after)
    return list(outs)


def _half(ref_rows, hc):
    h = ref_rows // 2
    return pl.ds(hc * h, h)


ALL_CHIPS = (0, 1, 2)
NEIGHBOURS = (0, 1)
DIAGONAL = (2,)


def _plan_gather_ici(which):
    def plan(bufs):
        x, y, c = _pos()
        j = 2 * x + y
        chips = _other_chips(x, y)
        return [(b.at[j, _half(b.shape[1], c)], b.at[j, _half(b.shape[1], c)], (*chips[k], c))
                for b in bufs for k in which]
    return plan


def _plan_gather_d2d(which):
    def plan(bufs):
        x, y, c = _pos()
        chips = _other_chips(x, y)
        out = []
        for b in bufs:
            for k in which:
                blk = b.at[2 * chips[k][0] + chips[k][1], _half(b.shape[1], c)]
                out.append((blk, blk, (x, y, 1 - c)))
        return out
    return plan


def _plan_pair_swap(n):
    def plan(bufs):
        x, y, c = _pos()
        return [(g.at[:, _half(g.shape[1], 1 - c)], land, (x, y, 1 - c)) for g, land in zip(bufs[:n], bufs[n:])]
    return plan


def _plan_chip_scatter(n):
    def plan(bufs):
        x, y, c = _pos()
        return [(p.at[2 * chip[0] + chip[1]], land.at[k], (*chip, c))
                for p, land in zip(bufs[:n], bufs[n:]) for k, chip in enumerate(_other_chips(x, y))]
    return plan


def _plan_pair_join(bufs):
    x, y, c = _pos()
    return [(b.at[_half(b.shape[0], c)], b.at[_half(b.shape[0], c)], (x, y, 1 - c)) for b in bufs]


def _empty_hbm(shape, dtype):
    return pltpu.with_memory_space_constraint(lax.empty(shape, dtype), pltpu.HBM)


def _gather_start(tag, bufs, after=None):
    sems, bufs = _xfer_start(f"gather_ici_start_{tag}", bufs, _plan_gather_ici(ALL_CHIPS), 3 * len(bufs), after)
    return dict(tag=tag, sems=sems, bufs=bufs)


def _gather_mid(st, after):
    tag = st["tag"]
    bufs = _xfer_wait(f"gather_ici_wait_{tag}", st["sems"], st["bufs"], _plan_gather_ici(ALL_CHIPS), after)
    sems, bufs = _xfer_start(f"gather_d2d_start_{tag}", bufs, _plan_gather_d2d(ALL_CHIPS), 3 * len(bufs))
    return dict(tag=tag, sems=sems, bufs=bufs)


def _gather_finish(st, after):
    return _xfer_wait(f"gather_d2d_wait_{st['tag']}", st["sems"], st["bufs"], _plan_gather_d2d(ALL_CHIPS), after)


def _d2d_hand_over(tag, sems, bufs, which, after):
    bufs = _xfer_wait(f"gather_ici_wait_{tag}", sems, bufs, _plan_gather_ici(which), after)
    sems, bufs = _xfer_start(f"gather_d2d_start_{tag}", bufs, _plan_gather_d2d(which), len(which) * len(bufs))
    return _xfer_wait(f"gather_d2d_wait_{tag}", sems, bufs, _plan_gather_d2d(which), after)


def _pair_add(g, r, name):
    S, R, C = g.shape
    h = R // 2
    tr = _row_tile(h, C)
    nb = h // tr

    def body(c_ref, g_ref, r_ref, o_ref):
        o_ref[...] = (g_ref[...].astype(F32) + r_ref[...].astype(F32)).astype(BF16)

    return _pcall(
        body, name=name,
        grid_spec=pltpu.PrefetchScalarGridSpec(
            num_scalar_prefetch=1, grid=(S, nb),
            in_specs=[pl.BlockSpec((1, tr, C), lambda s, i, c_ref: (s, c_ref[0] * nb + i, 0)),
                      pl.BlockSpec((1, tr, C), lambda s, i, c_ref: (s, i, 0))],
            out_specs=pl.BlockSpec((1, tr, C), lambda s, i, c_ref: (s, i, 0))),
        out_shape=jax.ShapeDtypeStruct((S, h, C), BF16),
        compiler_params=_params(("parallel", "parallel")),
    )(lax.axis_index("c").astype(jnp.int32).reshape(1), g, r)


def _chip_sum(p, rb, name):
    S, h, C = p.shape
    tr = _row_tile(h, C)
    nb = h // tr
    jc = jnp.concatenate([_chip_index(), lax.axis_index("c").astype(jnp.int32).reshape(1)])

    def body(jc_ref, p_ref, r_ref, o_ref):
        o_ref[...] = ((p_ref[0].astype(F32) + r_ref[0].astype(F32)) + r_ref[1].astype(F32)) + r_ref[2].astype(F32)

    return _pcall(
        body, name=name,
        grid_spec=pltpu.PrefetchScalarGridSpec(
            num_scalar_prefetch=1, grid=(nb,),
            in_specs=[pl.BlockSpec((1, tr, C), lambda i, jc_ref: (jc_ref[0], i, 0)),
                      pl.BlockSpec((3, tr, C), lambda i, jc_ref: (0, i, 0))],
            out_specs=pl.BlockSpec((tr, C), lambda i, jc_ref: (jc_ref[1] * nb + i, 0))),
        out_shape=jax.ShapeDtypeStruct((2 * h, C), F32),
        compiler_params=_params(("parallel",)),
    )(jc, p, rb)


def _rs_start(tag, gs):
    n = len(gs)
    lands = [_empty_hbm((g.shape[0], g.shape[1] // 2, g.shape[2]), g.dtype) for g in gs]
    sems, bufs = _xfer_start(f"rs_swap_start_{tag}", list(gs) + lands, _plan_pair_swap(n), n)
    return dict(tag=tag, n=n, sems=sems, bufs=bufs)


def _rs_scatter(st, after):
    tag, n = st["tag"], st["n"]
    bufs = _xfer_wait(f"rs_swap_wait_{tag}", st["sems"], st["bufs"], _plan_pair_swap(n), after)
    ps = [_pair_add(g, r, f"rs_pair_add_{tag}{t}") for t, (g, r) in enumerate(zip(bufs[:n], bufs[n:]))]
    lands = [_empty_hbm((3,) + p.shape[1:], p.dtype) for p in ps]
    sems, bufs = _xfer_start(f"rs_scatter_start_{tag}", ps + lands, _plan_chip_scatter(n), 3 * n)
    return dict(tag=tag, n=n, sems=sems, bufs=bufs)


def _rs_join(st, after):
    tag, n = st["tag"], st["n"]
    bufs = _xfer_wait(f"rs_scatter_wait_{tag}", st["sems"], st["bufs"], _plan_chip_scatter(n), after)
    fs = [_chip_sum(p, rb, f"rs_chip_sum_{tag}{t}") for t, (p, rb) in enumerate(zip(bufs[:n], bufs[n:]))]
    sems, bufs = _xfer_start(f"rs_join_start_{tag}", fs, _plan_pair_join, n)
    return dict(tag=tag, n=n, sems=sems, bufs=bufs)


def _rs_finish(st, after):
    return _xfer_wait(f"rs_join_wait_{st['tag']}", st["sems"], st["bufs"], _plan_pair_join, after)


def _sum8(g, name):
    _, R, C = g.shape

    def body(g_ref, o_ref):
        acc = g_ref[0]
        for d in range(1, N_DEV):
            acc = acc + g_ref[d]
        o_ref[...] = acc

    return _pcall(body, name=name, out_shape=jax.ShapeDtypeStruct((R, C), F32),
                          compiler_params=_params())(g)


def _ada_fwd(cs, w, b):
    D, n = w.shape
    tn = _pick(n, (512, 384, 256, 128))

    def body(c_ref, w_ref, b_ref, o_ref):
        cv = c_ref[...]
        a = (cv * _sigmoid(cv)).astype(BF16)
        o_ref[...] = _dot(a, w_ref[...].astype(BF16)) + b_ref[...]

    return _pcall(
        body, name="ada_fwd", grid=(n // tn,),
        in_specs=[pl.BlockSpec((16, D), lambda j: (0, 0)), pl.BlockSpec((D, tn), lambda j: (0, j)),
                  pl.BlockSpec((1, tn), lambda j: (0, j))],
        out_specs=pl.BlockSpec((16, tn), lambda j: (0, j)),
        out_shape=jax.ShapeDtypeStruct((16, n), F32),
        compiler_params=_params(("parallel",)),
    )(cs, w, b)


def _ada_bwd(cs, w, dmod):
    D, n = w.shape
    tn = _pick(n, (512, 384, 256, 128))

    def body(c_ref, w_ref, d_ref, gw_ref, da_ref):
        j = pl.program_id(0)
        cv = c_ref[...]
        a = cv * _sigmoid(cv)
        d = d_ref[...]
        gw_ref[...] = lax.dot_general(a, d, (((0,), (0,)), ((), ())), precision=HI, preferred_element_type=F32)

        @pl.when(j == 0)
        def _():
            da_ref[...] = jnp.zeros_like(da_ref)

        da_ref[...] += _dot_nt(d.astype(BF16), w_ref[...].astype(BF16))

    return _pcall(
        body, name="ada_bwd", grid=(n // tn,),
        in_specs=[pl.BlockSpec((16, D), lambda j: (0, 0)), pl.BlockSpec((D, tn), lambda j: (0, j)),
                  pl.BlockSpec((16, tn), lambda j: (0, j))],
        out_specs=[pl.BlockSpec((D, tn), lambda j: (0, j)), pl.BlockSpec((16, D), lambda j: (0, 0))],
        out_shape=[jax.ShapeDtypeStruct((D, n), F32), jax.ShapeDtypeStruct((16, D), F32)],
        compiler_params=_params(("arbitrary",)),
    )(cs, w, dmod)


def _rms1_fwd(xall, gain, shift2, scale2, n_ctx):
    T, D = xall.shape
    tb = _pick(n_ctx, (256, 128, 64, 32, 16))
    nctx = n_ctx // tb

    def body(x_ref, g_ref, sh_ref, sc_ref, o_ref):
        i = pl.program_id(0)
        xv = x_ref[...]
        r = lax.rsqrt(jnp.mean(xv * xv, axis=-1, keepdims=True) + EPS)
        nrm = xv * r * g_ref[...]
        lat = i >= nctx
        sh = jnp.where(lat, sh_ref[1:2, :], sh_ref[0:1, :])
        sc = jnp.where(lat, sc_ref[1:2, :], sc_ref[0:1, :])
        o_ref[...] = (nrm * (1.0 + sc) + sh).astype(BF16)

    vec = lambda r: pl.BlockSpec((r, D), lambda i: (0, 0))
    return _pcall(
        body, name="rms1_fwd", grid=(T // tb,),
        in_specs=[pl.BlockSpec((tb, D), lambda i: (i, 0)), vec(1), vec(2), vec(2)],
        out_specs=pl.BlockSpec((tb, D), lambda i: (i, 0)),
        out_shape=jax.ShapeDtypeStruct((T, D), BF16),
        compiler_params=_params(("parallel",)),
    )(xall, gain, shift2, scale2)


def _rms1_bwd(xall, dh, dxmid, gain, scale2, n_ctx):
    T, D = xall.shape
    L = T - n_ctx
    tb = _pick(n_ctx, (256, 128, 64, 32, 16))
    nctx = n_ctx // tb

    def body(x_ref, dh_ref, dxm_ref, g_ref, sc_ref, dx_ref, cs_ref):
        i = pl.program_id(0)
        lat = i >= nctx
        xv = x_ref[...]
        r = lax.rsqrt(jnp.mean(xv * xv, axis=-1, keepdims=True) + EPS)
        xh = xv * r
        g = g_ref[...]
        nrm = xh * g
        sc = jnp.where(lat, sc_ref[1:2, :], sc_ref[0:1, :])
        dhv = dh_ref[...]
        dn = dhv * (1.0 + sc)
        dxh = dn * g
        dxv = r * (dxh - xh * jnp.mean(dxh * xh, axis=-1, keepdims=True))
        s_sh = jnp.sum(dhv, axis=0, keepdims=True)
        s_sc = jnp.sum(dhv * nrm, axis=0, keepdims=True)
        s_g = jnp.sum(dn * xh, axis=0, keepdims=True)
        zero = jnp.zeros_like(s_sh)
        rows = lax.broadcasted_iota(jnp.int32, (8, D), 0)
        upd = jnp.where(rows == 0, jnp.where(lat, zero, s_sh),
              jnp.where(rows == 1, jnp.where(lat, zero, s_sc),
              jnp.where(rows == 2, jnp.where(lat, s_sh, zero),
              jnp.where(rows == 3, jnp.where(lat, s_sc, zero),
              jnp.where(rows == 4, s_g, 0.0)))))

        @pl.when(i == 0)
        def _():
            cs_ref[...] = jnp.zeros_like(cs_ref)

        cs_ref[...] += upd

        @pl.when(lat)
        def _():
            dx_ref[...] = dxv + dxm_ref[...]

    lat_blk = lambda i: (jnp.maximum(i - nctx, 0), 0)
    vec = lambda r: pl.BlockSpec((r, D), lambda i: (0, 0))
    return _pcall(
        body, name="rms1_bwd", grid=(T // tb,),
        in_specs=[pl.BlockSpec((tb, D), lambda i: (i, 0)), pl.BlockSpec((tb, D), lambda i: (i, 0)),
                  pl.BlockSpec((tb, D), lat_blk), vec(1), vec(2)],
        out_specs=[pl.BlockSpec((tb, D), lat_blk), vec(8)],
        out_shape=[jax.ShapeDtypeStruct((L, D), F32), jax.ShapeDtypeStruct((8, D), F32)],
        compiler_params=_params(("arbitrary",)),
    )(xall, dh, dxmid, gain, scale2)


def _resid_rms2_fwd(x, mo, vecs):
    L, D = x.shape
    tb = _pick(L, (256, 128, 64))

    def body(x_ref, mo_ref, v_ref, xm_ref, h_ref):
        xm = x_ref[...] + v_ref[0:1, :] * mo_ref[...]
        xm_ref[...] = xm
        r = lax.rsqrt(jnp.mean(xm * xm, axis=-1, keepdims=True) + EPS)
        h_ref[...] = (xm * r * v_ref[1:2, :] * (1.0 + v_ref[3:4, :]) + v_ref[2:3, :]).astype(BF16)

    blk = pl.BlockSpec((tb, D), lambda i: (i, 0))
    return _pcall(
        body, name="resid_rms2_fwd", grid=(L // tb,),
        in_specs=[blk, blk, pl.BlockSpec((8, D), lambda i: (0, 0))],
        out_specs=[blk, blk],
        out_shape=[jax.ShapeDtypeStruct((L, D), F32), jax.ShapeDtypeStruct((L, D), BF16)],
        compiler_params=_params(("parallel",)),
    )(x, mo, vecs)


def _resid_rms2_bwd(xmid, dh_a, dh_b, dy, mo, vecs):
    L, D = xmid.shape
    tb = _pick(L, (256, 128, 64))

    def body(xm_ref, da_ref, db_ref, dy_ref, mo_ref, v_ref, dxm_ref, dmo_ref, cs_ref):
        i = pl.program_id(0)
        xm = xm_ref[...]
        r = lax.rsqrt(jnp.mean(xm * xm, axis=-1, keepdims=True) + EPS)
        xh = xm * r
        g = v_ref[1:2, :]
        nrm = xh * g
        dhv = da_ref[...] + db_ref[...]
        dn = dhv * (1.0 + v_ref[3:4, :])
        dxh = dn * g
        dxm = dy_ref[...] + r * (dxh - xh * jnp.mean(dxh * xh, axis=-1, keepdims=True))
        dxm_ref[...] = dxm
        dmo_ref[...] = (dxm * v_ref[0:1, :]).astype(BF16)
        s0 = jnp.sum(dhv, axis=0, keepdims=True)
        s1 = jnp.sum(dhv * nrm, axis=0, keepdims=True)
        s2 = jnp.sum(dn * xh, axis=0, keepdims=True)
        s3 = jnp.sum(dxm * mo_ref[...], axis=0, keepdims=True)
        rows = lax.broadcasted_iota(jnp.int32, (8, D), 0)
        upd = jnp.where(rows == 0, s0, jnp.where(rows == 1, s1, jnp.where(rows == 2, s2,
              jnp.where(rows == 3, s3, 0.0))))

        @pl.when(i == 0)
        def _():
            cs_ref[...] = jnp.zeros_like(cs_ref)

        cs_ref[...] += upd

    blk = pl.BlockSpec((tb, D), lambda i: (i, 0))
    vec = pl.BlockSpec((8, D), lambda i: (0, 0))
    return _pcall(
        body, name="resid_rms2_bwd", grid=(L // tb,),
        in_specs=[blk, blk, blk, blk, blk, vec],
        out_specs=[blk, blk, vec],
        out_shape=[jax.ShapeDtypeStruct((L, D), F32), jax.ShapeDtypeStruct((L, D), BF16),
                   jax.ShapeDtypeStruct((8, D), F32)],
        compiler_params=_params(("arbitrary",)),
    )(xmid, dh_a, dh_b, dy, mo, vecs)


def _loss_head(xmid, f, g2, target):
    L, D = xmid.shape
    tb = _pick(L, (256, 128, 64))

    def body(xm_ref, f_ref, g_ref, t_ref, dy_ref, df_ref, s_ref):
        i = pl.program_id(0)
        fv = f_ref[...]
        g = g_ref[...]
        err = xm_ref[...] + g * fv - t_ref[...]
        dy = err * (1.0 / D)
        dy_ref[...] = dy
        df_ref[...] = (dy * g).astype(BF16)
        s0 = jnp.sum(dy * fv, axis=0, keepdims=True)
        part = 0.5 * jnp.sum(jnp.mean(err * err, axis=-1, keepdims=True), axis=0, keepdims=True)
        rows = lax.broadcasted_iota(jnp.int32, (8, D), 0)
        upd = jnp.where(rows == 0, s0, jnp.where(rows == 1, part, 0.0))

        @pl.when(i == 0)
        def _():
            s_ref[...] = jnp.zeros_like(s_ref)

        s_ref[...] += upd

    blk = pl.BlockSpec((tb, D), lambda i: (i, 0))
    return _pcall(
        body, name="loss_head", grid=(L // tb,),
        in_specs=[blk, blk, pl.BlockSpec((1, D), lambda i: (0, 0)), blk],
        out_specs=[blk, blk, pl.BlockSpec((8, D), lambda i: (0, 0))],
        out_shape=[jax.ShapeDtypeStruct((L, D), F32), jax.ShapeDtypeStruct((L, D), BF16),
                   jax.ShapeDtypeStruct((8, D), F32)],
        compiler_params=_params(("arbitrary",)),
    )(xmid, f, g2, target)


def _gate_cols(D, off):
    tc = _pick(np.gcd(D, off), (512, 256, 128))
    return tc, off // tc


def _merge_fwd(za, zb, p, n_ctx, off_a, off_b):
    L, D = za.shape
    tb = _pick(n_ctx, (256, 128, 64, 32, 16))
    nctx = n_ctx // tb
    tc, oa = _gate_cols(D, off_a)
    _, ob = _gate_cols(D, off_b)
    if off_b % tc:
        raise ValueError("gate column offsets must share a column tile")
    ob = off_b // tc

    def body(za_ref, zb_ref, ga_ref, gb_ref, z_ref):
        z_ref[...] = (_sigmoid(ga_ref[...]) * za_ref[...].astype(F32)
                      + _sigmoid(gb_ref[...]) * zb_ref[...].astype(F32)).astype(BF16)

    blk = pl.BlockSpec((tb, tc), lambda i, j: (i, j))
    return _pcall(
        body, name="merge_fwd", grid=(L // tb, D // tc),
        in_specs=[blk, blk, pl.BlockSpec((tb, tc), lambda i, j: (i + nctx, oa + j)),
                  pl.BlockSpec((tb, tc), lambda i, j: (i + nctx, ob + j))],
        out_specs=blk,
        out_shape=jax.ShapeDtypeStruct((L, D), BF16),
        compiler_params=_params(("parallel", "parallel")),
    )(za, zb, p, p)


def _merge_bwd(dz, za, zb, p, n_ctx, off_a, off_b):
    L, D = za.shape
    T = L + n_ctx
    tb = _pick(n_ctx, (256, 128, 64, 32, 16))
    nctx = n_ctx // tb
    tc = _gate_cols(D, off_a)[0]
    oa, ob = off_a // tc, off_b // tc

    def body(dz_ref, za_ref, zb_ref, ga_ref, gb_ref, dza_ref, dzb_ref, dga_ref, dgb_ref):
        i = pl.program_id(1)

        @pl.when(i < nctx)
        def _():
            dga_ref[...] = jnp.zeros_like(dga_ref)
            dgb_ref[...] = jnp.zeros_like(dgb_ref)

        @pl.when(i >= nctx)
        def _():
            dzv = dz_ref[...].astype(F32)
            sa = _sigmoid(ga_ref[...])
            sb = _sigmoid(gb_ref[...])
            dza_ref[...] = (dzv * sa).astype(BF16)
            dzb_ref[...] = (dzv * sb).astype(BF16)
            dga_ref[...] = (dzv * za_ref[...].astype(F32) * sa * (1.0 - sa)).astype(BF16)
            dgb_ref[...] = (dzv * zb_ref[...].astype(F32) * sb * (1.0 - sb)).astype(BF16)

    lat = pl.BlockSpec((tb, tc), lambda j, i: (jnp.maximum(i - nctx, 0), j))
    allr = pl.BlockSpec((tb, tc), lambda j, i: (i, j))
    return _pcall(
        body, name="merge_bwd", grid=(D // tc, T // tb),
        in_specs=[lat, lat, lat, pl.BlockSpec((tb, tc), lambda j, i: (i, oa + j)),
                  pl.BlockSpec((tb, tc), lambda j, i: (i, ob + j))],
        out_specs=[lat, lat, allr, allr],
        out_shape=[jax.ShapeDtypeStruct((L, D), BF16), jax.ShapeDtypeStruct((L, D), BF16),
                   jax.ShapeDtypeStruct((T, D), BF16), jax.ShapeDtypeStruct((T, D), BF16)],
        compiler_params=_params(("arbitrary", "arbitrary")),
    )(dz, za, zb, p, p)


def _shift_down(u, rows):
    return jnp.where(rows == 0, 0.0, pltpu.roll(u, 1, 0))


def _shift_up(u, rows):
    n = u.shape[0]
    return jnp.where(rows == n - 1, 0.0, pltpu.roll(u, n - 1, 0))


def _convgate_fwd(u1, u3, cw, cb):
    L, F = u1.shape
    tc = _pick(F, (256, 128))

    def body(u1_ref, u3_ref, w_ref, b_ref, a_ref):
        u = u1_ref[...].astype(F32)
        rows = lax.broadcasted_iota(jnp.int32, u.shape, 0)
        cv = _shift_down(u, rows) * w_ref[0:1, :] + u * w_ref[1:2, :] + _shift_up(u, rows) * w_ref[2:3, :] + b_ref[...]
        a_ref[...] = (cv * _sigmoid(cv) * u3_ref[...].astype(F32)).astype(BF16)

    blk = pl.BlockSpec((L, tc), lambda j: (0, j))
    return _pcall(
        body, name="convgate_fwd", grid=(F // tc,),
        in_specs=[blk, blk, pl.BlockSpec((8, tc), lambda j: (0, j)), pl.BlockSpec((1, tc), lambda j: (0, j))],
        out_specs=blk,
        out_shape=jax.ShapeDtypeStruct((L, F), BF16),
        compiler_params=_params(("parallel",)),
    )(u1, u3, cw, cb)


def _convgate_bwd(u1, u3, da, cw, cb):
    L, F = u1.shape
    tc = _pick(F, (256, 128))

    def body(u1_ref, u3_ref, da_ref, w_ref, b_ref, du1_ref, du3_ref, s_ref):
        u = u1_ref[...].astype(F32)
        rows = lax.broadcasted_iota(jnp.int32, u.shape, 0)
        um, up = _shift_down(u, rows), _shift_up(u, rows)
        w0, w1, w2 = w_ref[0:1, :], w_ref[1:2, :], w_ref[2:3, :]
        cv = um * w0 + u * w1 + up * w2 + b_ref[...]
        s = _sigmoid(cv)
        dav = da_ref[...].astype(F32)
        du3_ref[...] = (dav * cv * s).astype(BF16)
        dcv = dav * u3_ref[...].astype(F32) * (s * (1.0 + cv * (1.0 - s)))
        du1_ref[...] = (_shift_up(dcv, rows) * w0 + dcv * w1 + _shift_down(dcv, rows) * w2).astype(BF16)
        r8 = lax.broadcasted_iota(jnp.int32, (8, tc), 0)
        s0 = jnp.sum(dcv * um, axis=0, keepdims=True)
        s1 = jnp.sum(dcv * u, axis=0, keepdims=True)
        s2 = jnp.sum(dcv * up, axis=0, keepdims=True)
        s3 = jnp.sum(dcv, axis=0, keepdims=True)
        s_ref[...] = jnp.where(r8 == 0, s0, jnp.where(r8 == 1, s1, jnp.where(r8 == 2, s2,
                     jnp.where(r8 == 3, s3, 0.0))))

    blk = pl.BlockSpec((L, tc), lambda j: (0, j))
    v8 = pl.BlockSpec((8, tc), lambda j: (0, j))
    return _pcall(
        body, name="convgate_bwd", grid=(F // tc,),
        in_specs=[blk, blk, blk, v8, pl.BlockSpec((1, tc), lambda j: (0, j))],
        out_specs=[blk, blk, v8],
        out_shape=[jax.ShapeDtypeStruct((L, F), BF16), jax.ShapeDtypeStruct((L, F), BF16),
                   jax.ShapeDtypeStruct((8, F), F32)],
        compiler_params=_params(("parallel",)),
    )(u1, u3, da, cw, cb)


def _lower_bound(lbl_ref, d):
    l0, l1 = lbl_ref[d, 0:1, :], lbl_ref[d, 1:2, :]
    m = jnp.maximum(l0, l1)
    e0, e1 = jnp.exp(l0 - m), jnp.exp(l1 - m)
    return e0 / (e0 + e1)


def _chunk_cumsum(x, rev):
    n = x.shape[0]
    r = lax.broadcasted_iota(jnp.int32, x.shape, 0) % CHUNK
    k = 1
    while k < CHUNK:
        if rev:
            x = x + jnp.where(r < CHUNK - k, pltpu.roll(x, n - k, 0), 0.0)
        else:
            x = x + jnp.where(r >= k, pltpu.roll(x, k, 0), 0.0)
        k *= 2
    return x


def _gate_terms(z, lb):
    sg = _sigmoid(z)
    f = lb + (1.0 - lb) * sg
    return sg, f


def _decay_terms(z, lb, rev):
    _, f = _gate_terms(z, lb)
    g = jnp.log(f)
    return 1.0 - f, _chunk_cumsum(g, rev), _chunk_cumsum(g, not rev) - g


def _chunk_total(c, rev):
    return c[0:1, :] if rev else c[CHUNK - 1:CHUNK, :]


def _pair_decay(c, s, rev):
    t = lax.broadcasted_iota(jnp.int32, (CHUNK, 1), 0)
    later = (t <= s) if rev else (t >= s)
    return jnp.where(later, jnp.exp(c - c[s:s + 1, :]), 0.0)


def _scan_chunk(i, n_ctx_chunks, n_chunks, rev):
    if not rev:
        return i
    return jnp.where(i < n_ctx_chunks, n_ctx_chunks - 1 - i, n_chunks + n_ctx_chunks - 1 - i)


def _rows(ci):
    return pl.ds(pl.multiple_of(ci * CHUNK, CHUNK), CHUNK)


def _hgrn_cols(HA):
    return HA // HEAD


def _hgrn_fwd(p, lbl, ng, n_ctx, HA):
    T = p.shape[0]
    L = T - n_ctx
    nh = _hgrn_cols(HA)
    nc, ncc = T // CHUNK, n_ctx // CHUNK

    def body(q_ref, zf_ref, zb_ref, v_ref, og_ref, lbl_ref, ng_ref, ya_ref, o_ref, st_ref,
             c_scr, k_scr, qe_scr, ke_scr, o_scr):
        dirs = ((0, False, zf_ref), (1, True, zb_ref))
        for d, rev, z_ref in dirs:
            k, c, rest = _decay_terms(z_ref[...], _lower_bound(lbl_ref, d), rev)
            c_scr[d] = c
            k_scr[d] = k
            qe_scr[d] = (q_ref[...] * jnp.exp(c)).astype(BF16)
            ke_scr[d] = (k * jnp.exp(rest)).astype(BF16)

        def step(i2, states):
            states = list(states)
            for u in range(HGRN_UNROLL):
                for d, rev, _ in dirs:
                    St = states[d]
                    ci = _scan_chunk(HGRN_UNROLL * i2 + u, ncc, nc, rev)
                    rows = _rows(ci)
                    q, v, c, k = q_ref[rows, :], v_ref[rows, :], c_scr[d, rows, :], k_scr[d, rows, :]
                    st_ref[0, d, ci] = St.astype(BF16)
                    o = jnp.zeros((CHUNK, HEAD), F32)
                    for s in range(CHUNK):
                        E = _pair_decay(c, s, rev)
                        a = jnp.sum(q * E * k[s:s + 1, :], axis=1, keepdims=True)
                        o = o + a * v[s:s + 1, :]
                    o_scr[d, rows, :] = o + _dot_nt(qe_scr[d, rows, :], St.astype(BF16))
                    states[d] = St * jnp.exp(_chunk_total(c, rev)) + _dot_tn(v.astype(BF16), ke_scr[d, rows, :])
            return tuple(states)

        if nc % HGRN_UNROLL:
            raise ValueError("the number of chunks must be a multiple of HGRN_UNROLL")
        zero = jnp.zeros((HEAD, HEAD), F32)
        lax.fori_loop(0, nc // HGRN_UNROLL, step, (zero, zero))

        o = o_scr[0, pl.ds(n_ctx, L), :] + o_scr[1, pl.ds(n_ctx, L), :]
        o_ref[...] = o
        r = lax.rsqrt(jnp.mean(o * o, axis=-1, keepdims=True) + EPS)
        og = og_ref[pl.ds(n_ctx, L), :]
        ya_ref[...] =(o * r * ng_ref[...] * (og * _sigmoid(og))).astype(BF16)

    cb = HA // HEAD
    col = lambda kk: pl.BlockSpec((T, HEAD), lambda h: (0, kk * cb + h))
    return _pcall(
        body, name="hgrn_fwd", grid=(nh,),
        in_specs=[col(0), col(1), col(2), col(3), col(4),
                  pl.BlockSpec((2, 2, HEAD), lambda h: (0, 0, h)), pl.BlockSpec((1, HEAD), lambda h: (0, 0))],
        out_specs=[pl.BlockSpec((L, HEAD), lambda h: (0, h)), pl.BlockSpec((L, HEAD), lambda h: (0, h)),
                   pl.BlockSpec((1, 2, nc, HEAD, HEAD), lambda h: (h, 0, 0, 0, 0))],
        out_shape=[jax.ShapeDtypeStruct((L, HA), BF16), jax.ShapeDtypeStruct((L, HA), F32),
                   jax.ShapeDtypeStruct((nh, 2, nc, HEAD, HEAD), BF16)],
        scratch_shapes=[pltpu.VMEM((2, T, HEAD), F32), pltpu.VMEM((2, T, HEAD), F32),
                        pltpu.VMEM((2, T, HEAD), BF16), pltpu.VMEM((2, T, HEAD), BF16),
                        pltpu.VMEM((2, T, HEAD), F32)],
        compiler_params=_params(("parallel",)),
    )(p, p, p, p, p, lbl, ng)


def _hgrn_bwd(p, lbl, ng, o, dya, st, n_ctx, HA):
    T = p.shape[0]
    L = T - n_ctx
    nh = _hgrn_cols(HA)
    nc, ncc = T // CHUNK, n_ctx // CHUNK

    def body(q_ref, zf_ref, zb_ref, v_ref, og_ref, lbl_ref, ng_ref, o_ref, dya_ref, st_ref,
             dq_ref, dzf_ref, dzb_ref, dv_ref, dog_ref, dlbl_ref, dng_ref,
             do_scr, c_scr, k_scr, qe_scr, ke_scr, dg_scr, dk_scr, dq_scr, dv_scr, row_scr):
        h = pl.program_id(0)
        ov = o_ref[...]
        r = lax.rsqrt(jnp.mean(ov * ov, axis=-1, keepdims=True) + EPS)
        oh = ov * r
        ogv = og_ref[pl.ds(n_ctx, L), :]
        sg_o = _sigmoid(ogv)
        dyv = dya_ref[...]
        ngv = ng_ref[...]
        dog_ref[pl.ds(0, n_ctx), :] = jnp.zeros((n_ctx, HEAD), BF16)
        dog_ref[pl.ds(n_ctx, L), :] = (dyv * oh * ngv * (sg_o * (1.0 + ogv * (1.0 - sg_o)))).astype(BF16)
        don = dyv * (ogv * sg_o)
        dng = jnp.sum(don * oh, axis=0, keepdims=True)
        doh = don * ngv
        do_scr[pl.ds(0, n_ctx), :] = jnp.zeros((n_ctx, HEAD), F32)
        do_scr[pl.ds(n_ctx, L), :] = r * (doh - oh * jnp.mean(doh * oh, axis=-1, keepdims=True))

        @pl.when(h == 0)
        def _():
            dng_ref[...] = jnp.zeros_like(dng_ref)

        dng_ref[0:1, :] += dng

        t16 = lax.broadcasted_iota(jnp.int32, (CHUNK, HEAD), 0)
        dirs = ((0, False, zf_ref, dzf_ref), (1, True, zb_ref, dzb_ref))
        for d, rev, z_ref, _ in dirs:
            k, c, rest = _decay_terms(z_ref[...], _lower_bound(lbl_ref, d), rev)
            c_scr[d] = c
            k_scr[d] = k
            qe_scr[d] = (q_ref[...] * jnp.exp(c)).astype(BF16)
            ke_scr[d] = (k * jnp.exp(rest)).astype(BF16)
        dq_scr[...] = jnp.zeros_like(dq_scr)
        dv_scr[...] = jnp.zeros_like(dv_scr)

        zero = jnp.zeros((HEAD, HEAD), F32)

        def bwd_chunk(i, carry, u):
            new = []
            for (d, rev, _, _), dSt in zip(dirs, carry):
                ci = _scan_chunk(i, ncc, nc, rev)
                rows = _rows(ci)
                q, v, do = q_ref[rows, :], v_ref[rows, :], do_scr[rows, :]
                c, k = c_scr[d, rows, :], k_scr[d, rows, :]
                tot = _chunk_total(c, rev)
                etot = jnp.exp(tot)
                St = st_ref[0, d, ci]
                dSb = dSt.astype(BF16)
                do_b = do.astype(BF16)
                dq_x = _dot(do_b, St) * jnp.exp(c)
                dk_x = _dot(v.astype(BF16), dSb) * jnp.exp(tot - c)
                dv_x = _dot_nt(ke_scr[d, rows, :], dSb)
                dtot = (jnp.sum(St.astype(F32) * dSt, axis=0, keepdims=True) * etot
                        + jnp.sum(k * dk_x, axis=0, keepdims=True))
                dq = jnp.zeros((CHUNK, HEAD), F32)
                for s in range(CHUNK):
                    E = _pair_decay(c, s, rev)
                    XE = E * k[s:s + 1, :]
                    a = jnp.sum(q * XE, axis=1, keepdims=True)
                    da = jnp.sum(do * v[s:s + 1, :], axis=1, keepdims=True)
                    dq = dq + da * XE
                    row_scr[u, d, 0, s:s + 1, :] = jnp.sum(da * q * E, axis=0, keepdims=True)
                    row_scr[u, d, 1, s:s + 1, :] = jnp.sum(a * do, axis=0, keepdims=True)
                dq, dk, dv = dq + dq_x, row_scr[u, d, 0] + dk_x, row_scr[u, d, 1] + dv_x
                dg_scr[d, rows, :] = _chunk_cumsum(q * dq - k * dk, not rev) + dtot
                dk_scr[d, rows, :] = dk
                dq_scr[rows, :] += dq
                dv_scr[rows, :] += dv
                new.append(dSt * etot + _dot_tn(do_b, qe_scr[d, rows, :]))
            return tuple(new)

        def bwd_step(i2, carry):
            for u in range(2):
                carry = bwd_chunk(nc - 1 - (2 * i2 + u), carry, u)
            return carry

        lax.fori_loop(0, nc // 2, bwd_step, (zero, zero))

        for d, _, z_ref, dz_ref in dirs:
            lb = _lower_bound(lbl_ref, d)
            sg, f = _gate_terms(z_ref[...], lb)
            df = dg_scr[d] / f - dk_scr[d]
            dz_ref[...] = (df * (1.0 - lb) * sg * (1.0 - sg)).astype(BF16)
            dl0 = jnp.sum(df * (1.0 - sg), axis=0, keepdims=True) * lb * (1.0 - lb)
            dlbl_ref[d, 0:1, :] = dl0
            dlbl_ref[d, 1:2, :] = -dl0
        dq_ref[...] = dq_scr[...].astype(BF16)
        dv_ref[...] = dv_scr[...].astype(BF16)

    cb = HA // HEAD
    col = lambda kk: pl.BlockSpec((T, HEAD), lambda h: (0, kk * cb + h))
    tcol = pl.BlockSpec((T, HEAD), lambda h: (0, h))
    lcol = pl.BlockSpec((L, HEAD), lambda h: (0, h))
    outs = _pcall(
        body, name="hgrn_bwd", grid=(nh,),
        in_specs=[col(0), col(1), col(2), col(3), col(4),
                  pl.BlockSpec((2, 2, HEAD), lambda h: (0, 0, h)), pl.BlockSpec((1, HEAD), lambda h: (0, 0)),
                  lcol, lcol,
                  pl.BlockSpec((1, 2, nc, HEAD, HEAD), lambda h: (h, 0, 0, 0, 0), pipeline_mode=pl.Buffered(1))],
        out_specs=[tcol, tcol, tcol, tcol, tcol, pl.BlockSpec((2, 2, HEAD), lambda h: (0, 0, h)),
                   pl.BlockSpec((8, HEAD), lambda h: (0, 0))],
        out_shape=[jax.ShapeDtypeStruct((T, HA), BF16)] * 5 + [jax.ShapeDtypeStruct((2, 2, HA), F32),
                                                               jax.ShapeDtypeStruct((8, HEAD), F32)],
        scratch_shapes=[pltpu.VMEM((T, HEAD), F32),
                        pltpu.VMEM((2, T, HEAD), F32), pltpu.VMEM((2, T, HEAD), F32),
                        pltpu.VMEM((2, T, HEAD), BF16), pltpu.VMEM((2, T, HEAD), BF16),
                        pltpu.VMEM((2, T, HEAD), F32), pltpu.VMEM((2, T, HEAD), F32),
                        pltpu.VMEM((T, HEAD), F32), pltpu.VMEM((T, HEAD), F32),
                        pltpu.VMEM((2, 2, 2, CHUNK, HEAD), F32)],
        compiler_params=_params(("arbitrary",)),
    )(p, p, p, p, p, lbl, ng, o, dya, st)
    return outs


def _swap_halves(t, lane):
    q = HEAD // 4
    return jnp.where((lane % (2 * q)) < q, pltpu.roll(t, HEAD - q, 1), pltpu.roll(t, q, 1))


def _qk_norm(t, g):
    r = lax.rsqrt(jnp.mean(t * t, axis=-1, keepdims=True) + EPS)
    return t * r, r


def _rope(t, cos, sin, lane):
    return t * cos + _swap_halves(t, lane) * sin


def _qk_norm_bwd(dy, th, r, g):
    dth = dy * g
    return r * (dth - th * jnp.mean(dth * th, axis=-1, keepdims=True)), jnp.sum(dy * th, axis=0, keepdims=True)


def _rope_bwd(dy, cos, sin, lane):
    return dy * cos + _swap_halves(dy * sin, lane)


def _na_geometry(L):
    n_rows = L // GRID_W
    kr = min(WIN_R, n_rows)
    return n_rows, kr


def _na_prep(q_ref, k_ref, v_ref, gq_ref, gk_ref, cos_ref, sin_ref, qs, ks, vs, n_ctx, L):
    lane = lax.broadcasted_iota(jnp.int32, (L, HEAD), 1)
    cos, sin = cos_ref[...], sin_ref[...]
    qh, _ = _qk_norm(q_ref[pl.ds(n_ctx, L), :], None)
    qs[...] = _rope(qh * gq_ref[...], cos, sin, lane).astype(BF16)
    kh, _ = _qk_norm(k_ref[pl.ds(n_ctx, L), :], None)
    ks[pl.ds(n_ctx, L), :] = _rope(kh * gk_ref[...], cos, sin, lane).astype(BF16)
    kc, _ = _qk_norm(k_ref[pl.ds(0, n_ctx), :], None)
    ks[pl.ds(0, n_ctx), :] = (kc * gk_ref[...]).astype(BF16)
    vs[...] = v_ref[...].astype(BF16)


NA_RB = 4


def _na_band_rows(kr):
    return kr + NA_RB


def _na_scores(i, qs, ks, bias_ref, n_ctx, n_rows, kr):
    scale = HEAD ** -0.5
    kb = _na_band_rows(kr)
    rq = NA_RB * i
    r0 = jnp.clip(rq - WIN_R // 2, 0, n_rows - kb)
    qrows = pl.ds(pl.multiple_of(rq * GRID_W, NA_RB * GRID_W), NA_RB * GRID_W)
    krows = pl.ds(pl.multiple_of(n_ctx + r0 * GRID_W, GRID_W), kb * GRID_W)
    qv = qs[qrows, :]
    sb = _dot_nt(qv, ks[krows, :]) * scale
    band_row = lax.broadcasted_iota(jnp.int32, (GRID_W, kb * GRID_W), 1) // GRID_W
    parts, tiles = [], []
    for u in range(NA_RB):
        r_u = rq + u
        first = jnp.clip(r_u - WIN_R // 2, 0, n_rows - kr) - r0
        idx = [jnp.clip(r0 - r_u + (WIN_R - 1) + 2 * jj, 0, 2 * WIN_R - 1) for jj in range(kb // 2)]
        bias_u = jnp.concatenate([bias_ref[0, t] for t in idx], axis=1)
        inside = (band_row >= first) & (band_row < first + kr)
        parts.append(jnp.where(inside, sb[u * GRID_W:(u + 1) * GRID_W, :] + bias_u, NEG))
        tiles.append(idx)
    sb = jnp.concatenate(parts, axis=0)
    sc = _dot_nt(qv, ks[pl.ds(0, n_ctx), :]) * scale
    m = jnp.maximum(jnp.max(sb, axis=1, keepdims=True), jnp.max(sc, axis=1, keepdims=True))
    eb, ec = jnp.exp(sb - m), jnp.exp(sc - m)
    inv = 1.0 / (jnp.sum(eb, axis=1, keepdims=True) + jnp.sum(ec, axis=1, keepdims=True))
    return eb * inv, ec * inv, qrows, krows, tiles


def _na_fwd(p, bias, gq, gk, cos, sin, n_ctx, off, HB):
    T = p.shape[0]
    L = T - n_ctx
    nh = HB // HEAD
    n_rows, kr = _na_geometry(L)
    ob = off // HEAD

    def body(q_ref, k_ref, v_ref, bias_ref, gq_ref, gk_ref, cos_ref, sin_ref, y_ref, qs, ks, vs):
        _na_prep(q_ref, k_ref, v_ref, gq_ref, gk_ref, cos_ref, sin_ref, qs, ks, vs, n_ctx, L)

        def step(i, carry):
            pb, pc, qrows, krows, _ = _na_scores(i, qs, ks, bias_ref, n_ctx, n_rows, kr)
            y = _dot(pb.astype(BF16), vs[krows, :]) + _dot(pc.astype(BF16), vs[pl.ds(0, n_ctx), :])
            y_ref[qrows, :] = y.astype(BF16)
            return carry

        lax.fori_loop(0, n_rows // NA_RB, step, 0)

    col = lambda kk: pl.BlockSpec((T, HEAD), lambda h: (0, ob + kk * nh + h))
    vec = pl.BlockSpec((1, HEAD), lambda h: (0, 0))
    tab = pl.BlockSpec((L, HEAD), lambda h: (0, 0))
    return _pcall(
        body, name="na_fwd", grid=(nh,),
        in_specs=[col(0), col(1), col(2), pl.BlockSpec((1,) + bias.shape[1:], lambda h: (h, 0, 0, 0)),
                  vec, vec, tab, tab],
        out_specs=pl.BlockSpec((L, HEAD), lambda h: (0, h)),
        out_shape=jax.ShapeDtypeStruct((L, HB), BF16),
        scratch_shapes=[pltpu.VMEM((L, HEAD), BF16), pltpu.VMEM((T, HEAD), BF16), pltpu.VMEM((T, HEAD), BF16)],
        compiler_params=_params(("parallel",)),
    )(p, p, p, bias, gq, gk, cos, sin)


def _na_bwd(p, bias, gq, gk, cos, sin, dyb, n_ctx, off, HB):
    T = p.shape[0]
    L = T - n_ctx
    nh = HB // HEAD
    n_rows, kr = _na_geometry(L)
    ob = off // HEAD
    scale = HEAD ** -0.5

    def body(q_ref, k_ref, v_ref, bias_ref, gq_ref, gk_ref, cos_ref, sin_ref, dy_ref,
             dq_ref, dk_ref, dv_ref, dbias_ref, dg_ref, qs, ks, vs, dqa, dka, dva):
        h = pl.program_id(0)
        _na_prep(q_ref, k_ref, v_ref, gq_ref, gk_ref, cos_ref, sin_ref, qs, ks, vs, n_ctx, L)
        dka[...] = jnp.zeros_like(dka)
        dva[...] = jnp.zeros_like(dva)
        dbias_ref[...] = jnp.zeros_like(dbias_ref)

        crows = pl.ds(0, n_ctx)

        def step(i, carry):
            pb, pc, qrows, krows, tiles = _na_scores(i, qs, ks, bias_ref, n_ctx, n_rows, kr)
            do = dy_ref[qrows, :]
            qv = qs[qrows, :]
            dpb = _dot_nt(do, vs[krows, :])
            dpc = _dot_nt(do, vs[crows, :])
            delta = jnp.sum(pb * dpb, axis=1, keepdims=True) + jnp.sum(pc * dpc, axis=1, keepdims=True)
            dsb = pb * (dpb - delta)
            dsc = pc * (dpc - delta)
            dsb_b, dsc_b = dsb.astype(BF16), dsc.astype(BF16)
            dqa[qrows, :] = (_dot(dsb_b, ks[krows, :]) + _dot(dsc_b, ks[crows, :])) * scale
            dka[krows, :] += _dot_tn(dsb_b, qv) * scale
            dka[crows, :] += _dot_tn(dsc_b, qv) * scale
            dva[krows, :] += _dot_tn(pb.astype(BF16), do)
            dva[crows, :] += _dot_tn(pc.astype(BF16), do)
            for u, idx in enumerate(tiles):
                for jj, t in enumerate(idx):
                    dbias_ref[0, t] += dsb[u * GRID_W:(u + 1) * GRID_W, jj * 2 * GRID_W:(jj + 1) * 2 * GRID_W]
            return carry

        lax.fori_loop(0, n_rows // NA_RB, step, 0)

        lane = lax.broadcasted_iota(jnp.int32, (L, HEAD), 1)
        cos, sin = cos_ref[...], sin_ref[...]
        lat, ctx = pl.ds(n_ctx, L), pl.ds(0, n_ctx)
        gqv, gkv = gq_ref[...], gk_ref[...]
        qh, rq = _qk_norm(q_ref[lat, :], None)
        dq, dgq = _qk_norm_bwd(_rope_bwd(dqa[...], cos, sin, lane), qh, rq, gqv)
        dq_ref[ctx, :] = jnp.zeros((n_ctx, HEAD), BF16)
        dq_ref[lat, :] = dq.astype(BF16)
        kh, rk = _qk_norm(k_ref[lat, :], None)
        dk, dgk = _qk_norm_bwd(_rope_bwd(dka[lat, :], cos, sin, lane), kh, rk, gkv)
        dk_ref[lat, :] = dk.astype(BF16)
        kch, rkc = _qk_norm(k_ref[ctx, :], None)
        dkc, dgkc = _qk_norm_bwd(dka[ctx, :], kch, rkc, gkv)
        dk_ref[ctx, :] = dkc.astype(BF16)
        dv_ref[...] = dva[...].astype(BF16)

        @pl.when(h == 0)
        def _():
            dg_ref[...] = jnp.zeros_like(dg_ref)

        dg_ref[0:1, :] += dgq
        dg_ref[1:2, :] += dgk + dgkc

    col = lambda kk: pl.BlockSpec((T, HEAD), lambda h: (0, ob + kk * nh + h))
    vec = pl.BlockSpec((1, HEAD), lambda h: (0, 0))
    tab = pl.BlockSpec((L, HEAD), lambda h: (0, 0))
    tcol = pl.BlockSpec((T, HEAD), lambda h: (0, h))
    bspec = pl.BlockSpec((1,) + bias.shape[1:], lambda h: (h, 0, 0, 0))
    return _pcall(
        body, name="na_bwd", grid=(nh,),
        in_specs=[col(0), col(1), col(2), bspec, vec, vec, tab, tab, pl.BlockSpec((L, HEAD), lambda h: (0, h))],
        out_specs=[tcol, tcol, tcol, bspec, pl.BlockSpec((8, HEAD), lambda h: (0, 0))],
        out_shape=[jax.ShapeDtypeStruct((T, HB), BF16)] * 3 + [jax.ShapeDtypeStruct(bias.shape, F32),
                                                               jax.ShapeDtypeStruct((8, HEAD), F32)],
        scratch_shapes=[pltpu.VMEM((L, HEAD), BF16), pltpu.VMEM((T, HEAD), BF16), pltpu.VMEM((T, HEAD), BF16),
                        pltpu.VMEM((L, HEAD), F32), pltpu.VMEM((T, HEAD), F32), pltpu.VMEM((T, HEAD), F32)],
        compiler_params=_params(("arbitrary",)),
    )(p, p, p, bias, gq, gk, cos, sin, dyb)


def _bias_tables():
    w = np.arange(GRID_W)
    col_start = np.clip(w - WIN_C // 2, 0, GRID_W - WIN_C)
    col_in = (w[None, :] >= col_start[:, None]) & (w[None, :] < col_start[:, None] + WIN_C)
    dc = np.clip(w[None, :] - w[:, None], -(WIN_C - 1), WIN_C - 1) + WIN_C - 1
    n_pair = 2 * WIN_R
    ridx = np.zeros((n_pair, GRID_W, 2 * GRID_W), np.int32)
    cidx = np.zeros((n_pair, GRID_W, 2 * GRID_W), np.int32)
    valid = np.zeros((n_pair, GRID_W, 2 * GRID_W), bool)
    for i in range(n_pair):
        for half in range(2):
            row = i + half
            sl = slice(half * GRID_W, (half + 1) * GRID_W)
            ridx[i, :, sl] = min(row, 2 * WIN_R - 2)
            cidx[i, :, sl] = dc
            valid[i, :, sl] = col_in & (row <= 2 * WIN_R - 2)
    return ridx, cidx, valid


def _bias_onehot():
    _, cidx, valid = _bias_tables()
    K = GRID_W * 2 * GRID_W
    oh = np.zeros((K, 128), np.float32)
    neg = np.full((1, K), NEG, np.float32)
    for cq in range(GRID_W):
        for ll in range(2 * GRID_W):
            if valid[0, cq, ll]:
                oh[cq * 2 * GRID_W + ll, (ll // GRID_W) * 64 + cidx[0, cq, ll]] = 1.0
                neg[0, cq * 2 * GRID_W + ll] = 0.0
    return oh, neg


def _expand_bias(table):
    H = table.shape[0]
    n_pair, n_dc = 2 * WIN_R, 2 * WIN_C - 1
    tp = jnp.pad(table, ((0, 0), (0, n_pair + 1 - table.shape[1]), (0, 64 - n_dc)))
    t2 = jnp.concatenate([tp[:, :n_pair], tp[:, 1:n_pair + 1]], axis=-1).reshape(H * n_pair, 128)
    oh, neg = _bias_onehot()

    def body(t_ref, oh_ref, neg_ref, o_ref):
        o_ref[...] = lax.dot_general(t_ref[...], oh_ref[...], (((1,), (1,)), ((), ())), precision=HI,
                                     preferred_element_type=F32) + neg_ref[...]

    out = _pcall(body, name="bias_expand", out_shape=jax.ShapeDtypeStruct((H * n_pair, oh.shape[0]), F32),
                         compiler_params=_params())(t2, jnp.asarray(oh), jnp.asarray(neg))
    return out.reshape(H, n_pair, GRID_W, 2 * GRID_W)


def _bias_grad(dbias):
    H = dbias.shape[0]
    n_pair, n_dc = 2 * WIN_R, 2 * WIN_C - 1
    K = GRID_W * 2 * GRID_W
    oh, _ = _bias_onehot()
    flat = dbias.reshape(H * n_pair, K)

    def body(d_ref, oh_ref, o_ref):
        o_ref[...] = jnp.dot(d_ref[...], oh_ref[...], precision=HI, preferred_element_type=F32)

    g = _pcall(body, name="bias_grad", out_shape=jax.ShapeDtypeStruct((H * n_pair, 128), F32),
                       compiler_params=_params())(flat, jnp.asarray(oh))
    g = g.reshape(H, n_pair, 128)
    left, right = g[:, :, :n_dc], g[:, :, 64:64 + n_dc]
    out = left[:, :n_pair - 1]
    return out.at[:, 1:].add(right[:, :n_pair - 2])


def _rope_tables(L):
    pos = np.arange(L)
    row = (pos // GRID_W).astype(np.float32)
    colp = (pos % GRID_W).astype(np.float32)
    half = HEAD // 2
    nf = half // 2
    inv = (ROPE_THETA ** (-np.arange(nf, dtype=np.float32) / nf)).astype(np.float32)

    def tabs(pv):
        ang = pv[:, None] * inv[None, :]
        c, s = np.cos(ang), np.sin(ang)
        return np.concatenate([c, c], axis=1), np.concatenate([-s, s], axis=1)

    cr, sr = tabs(row)
    cc, sc = tabs(colp)
    return (jnp.asarray(np.concatenate([cr, cc], axis=1), F32), jnp.asarray(np.concatenate([sr, sc], axis=1), F32))


def _adamw(w, g, m, v, name, after=None, copy_g=False):
    R, C = w.shape
    tr = _row_tile(R, C)
    c1 = 1.0 - ADAM_B1 ** ADAM_STEP
    c2 = 1.0 - ADAM_B2 ** ADAM_STEP
    deps = [] if after is None else [after]
    n_out = 4 if copy_g else 3

    def body(w_ref, g_ref, m_ref, v_ref, *rest):
        d_ref, mo_ref, vo_ref = rest[len(deps):len(deps) + 3]
        gv = g_ref[...]
        mn = ADAM_B1 * m_ref[...] + (1.0 - ADAM_B1) * gv
        vn = ADAM_B2 * v_ref[...] + (1.0 - ADAM_B2) * (gv * gv)
        mo_ref[...] = mn
        vo_ref[...] = vn
        d_ref[...] = -ADAM_LR * ((mn / c1) / (jnp.sqrt(vn / c2) + ADAM_EPS) + ADAM_WD * w_ref[...])
        if copy_g:
            rest[-1][...] = gv

    blk = pl.BlockSpec((tr, C), lambda i: (i, 0))
    return _pcall(
        body, name=name, grid=(R // tr,),
        in_specs=[blk] * 4 + [_ANY] * len(deps), out_specs=[blk] * n_out,
        out_shape=[jax.ShapeDtypeStruct((R, C), F32)] * n_out,
        compiler_params=_params(("parallel",)),
    )(w, g, m, v, *deps)


PACK_W = 1024


def _pack(parts):
    flat, offs, pos = [], [], 0
    for a in parts:
        n = a.size
        padn = -n % PACK_W
        flat.append(jnp.pad(a.reshape(-1).astype(F32), (0, padn)))
        offs.append((pos, n, a.shape))
        pos += n + padn
    tail = -pos % (8 * PACK_W)
    if tail:
        flat.append(jnp.zeros((tail,), F32))
    return jnp.concatenate(flat).reshape(-1, PACK_W), offs


def _unpack(buf, offs, i):
    pos, n, shape = offs[i]
    return buf.reshape(buf.shape[:-2] + (-1,))[..., pos:pos + n].reshape(buf.shape[:-2] + shape)


def kernel(x, c, ctx, c_ctx, ada_w, ada_b, norm1_g, norm2_g, w_in, hgrn_lb_logits, hgrn_norm_g, na_q_norm_g, na_k_norm_g, na_rel_bias, w_branch_a, w_branch_b, w_out, ffn_w1, ffn_w3, ffn_conv_w, ffn_conv_b, ffn_w2, loss_target, m_c_ctx, m_ada_w, m_ada_b, m_norm1_g, m_norm2_g, m_w_in, m_hgrn_lb_logits, m_hgrn_norm_g, m_na_q_norm_g, m_na_k_norm_g, m_na_rel_bias, m_w_branch_a, m_w_branch_b, m_w_out, m_ffn_w1, m_ffn_w3, m_ffn_conv_w, m_ffn_conv_b, m_ffn_w2, v_c_ctx, v_ada_w, v_ada_b, v_norm1_g, v_norm2_g, v_w_in, v_hgrn_lb_logits, v_hgrn_norm_g, v_na_q_norm_g, v_na_k_norm_g, v_na_rel_bias, v_w_branch_a, v_w_branch_b, v_w_out, v_ffn_w1, v_ffn_w3, v_ffn_conv_w, v_ffn_conv_b, v_ffn_w2):
    weights = dict(c_ctx=c_ctx, ada_w=ada_w, ada_b=ada_b, norm1_g=norm1_g, norm2_g=norm2_g, w_in=w_in,
                   hgrn_lb_logits=hgrn_lb_logits, hgrn_norm_g=hgrn_norm_g, na_q_norm_g=na_q_norm_g,
                   na_k_norm_g=na_k_norm_g, na_rel_bias=na_rel_bias, w_branch_a=w_branch_a, w_branch_b=w_branch_b,
                   w_out=w_out, ffn_w1=ffn_w1, ffn_w3=ffn_w3, ffn_conv_w=ffn_conv_w, ffn_conv_b=ffn_conv_b,
                   ffn_w2=ffn_w2)
    moms = dict(c_ctx=(m_c_ctx, v_c_ctx), ada_w=(m_ada_w, v_ada_w), ada_b=(m_ada_b, v_ada_b),
                norm1_g=(m_norm1_g, v_norm1_g), norm2_g=(m_norm2_g, v_norm2_g), w_in=(m_w_in, v_w_in),
                hgrn_lb_logits=(m_hgrn_lb_logits, v_hgrn_lb_logits), hgrn_norm_g=(m_hgrn_norm_g, v_hgrn_norm_g),
                na_q_norm_g=(m_na_q_norm_g, v_na_q_norm_g), na_k_norm_g=(m_na_k_norm_g, v_na_k_norm_g),
                na_rel_bias=(m_na_rel_bias, v_na_rel_bias), w_branch_a=(m_w_branch_a, v_w_branch_a),
                w_branch_b=(m_w_branch_b, v_w_branch_b), w_out=(m_w_out, v_w_out), ffn_w1=(m_ffn_w1, v_ffn_w1),
                ffn_w3=(m_ffn_w3, v_ffn_w3), ffn_conv_w=(m_ffn_conv_w, v_ffn_conv_w),
                ffn_conv_b=(m_ffn_conv_b, v_ffn_conv_b), ffn_w2=(m_ffn_w2, v_ffn_w2))
    order = list(weights)

    L, D = x.shape[1], x.shape[2]
    N = ctx.shape[1]
    T = N + L
    HA = w_branch_a.shape[1]
    HB = w_branch_b.shape[1]
    F = ffn_conv_b.shape[1]
    IN = 5 * HA + 3 * HB + 2 * D
    n_ada = ada_w.shape[2]
    ix, iy, ic = _pos()
    chip = 2 * ix + iy
    dev = 2 * chip + ic

    _PENDING.clear()
    pk0, offs0 = _pack([c[0], hgrn_lb_logits, ffn_conv_w[0]])
    g0 = _allgather8(pk0, "gather_small0")
    c_all = _unpack(g0, offs0, 0)
    lbl_parts = _unpack(g0, offs0, 1)
    lbl = jnp.concatenate([lbl_parts[2 * j] for j in range(N_CHIP)], axis=-1)
    cw_parts = _unpack(g0, offs0, 2)
    cw = jnp.concatenate([cw_parts[2 * j] for j in range(N_CHIP)], axis=-1)
    cw8 = jnp.pad(cw, ((0, 5), (0, 0)))

    cs = jnp.concatenate([c_all, c_ctx[None, :], jnp.zeros((7, D), F32)], axis=0)
    ada_b_mine = lax.dynamic_slice(ada_b, (0, chip * n_ada), (1, n_ada))
    mod_mine = _ada_fwd(cs, ada_w[0], ada_b_mine)
    gm = _allgather8(mod_mine, "gather_mod")
    mod = jnp.concatenate([gm[2 * j] for j in range(N_CHIP)], axis=-1)
    mod_l = lax.dynamic_slice(mod, (dev, 0), (1, N_MOD * D)).reshape(N_MOD, D)
    mod_c = mod[8].reshape(N_MOD, D)
    sh1, sc1, g1, sh2, sc2, g2 = [mod_l[i:i + 1] for i in range(N_MOD)]
    shift1 = jnp.concatenate([mod_c[0:1], sh1], axis=0)
    scale1 = jnp.concatenate([mod_c[1:2], sc1], axis=0)

    shards = [w_in[0], w_branch_a[0], w_branch_b[0], w_out[0], ffn_w1[0], ffn_w3[0], ffn_w2[0]]
    names = ["w_in", "w_a", "w_b", "w_out", "w1", "w3", "w2"]
    slots = [_cast_bf16_slot(s, "cast_" + nm) for s, nm in zip(shards, names)]
    sem_nb, win_buf = _xfer_start("gather_ici_start_in_nbr", slots[0:1], _plan_gather_ici(NEIGHBOURS), 2, gm)

    xall = jnp.concatenate([ctx[0], x[0]], axis=0)
    h_all = _rms1_fwd(xall, norm1_g, shift1, scale1, N)
    chip_i = chip.astype(jnp.int32)
    same = lambda ids: jnp.stack([jnp.stack(ids), jnp.stack(ids)])
    p = _mm_nn_sel(h_all, win_buf[0], same([chip_i]), F32, "mm_p_own")
    bias = _expand_bias(na_rel_bias[0])
    win_buf = _xfer_wait("gather_ici_wait_in_nbr", sem_nb, win_buf, _plan_gather_ici(NEIGHBOURS),
                         (p, bias, *slots[1:]))
    sem_nb, win_buf = _xfer_start("gather_d2d_start_in_nbr", win_buf, _plan_gather_d2d(NEIGHBOURS), 2)
    sem_dg, win_buf = _xfer_start("gather_ici_start_in_diag", win_buf, _plan_gather_ici(DIAGONAL), 1)
    gat_mix = _gather_start("mix", slots[1:4])
    gat_ffn = _gather_start("ffn", slots[4:6])
    gat_ffn2 = _gather_start("ffn2", slots[6:7])
    win_buf = _xfer_wait("gather_d2d_wait_in_nbr", sem_nb, win_buf, _plan_gather_d2d(NEIGHBOURS), _PENDING[0])
    p = _mm_nn_sel(h_all, win_buf[0], same([chip_i ^ 1, chip_i ^ 2]), F32, "mm_p_nbr", p)
    win_buf = _d2d_hand_over("in_diag", sem_dg, win_buf, DIAGONAL, p)
    p = _mm_nn_sel(h_all, win_buf[0], same([chip_i ^ 3]), F32, "mm_p_diag", p)
    Win = win_buf[0]
    cos, sin = _rope_tables(L)
    off_na = 5 * HA
    y_b = _na_fwd(p, bias, na_q_norm_g, na_k_norm_g, cos, sin, N, off_na, HB)
    gat_mix = _gather_mid(gat_mix, y_b)
    y_a, o_a, st_a = _hgrn_fwd(p, lbl, hgrn_norm_g, N, HA)
    Wa, Wb, Wo = _gather_finish(gat_mix, (y_a, y_b))
    Wo = Wo.reshape(1, D, D)
    za = _mm_nn(y_a, Wa, BF16, "mm_za")
    zb = _mm_nn(y_b, Wb, BF16, "mm_zb")
    off_ga, off_gb = 5 * HA + 3 * HB, 5 * HA + 3 * HB + D
    z = _merge_fwd(za, zb, p, N, off_ga, off_gb)
    gat_ffn = _gather_mid(gat_ffn, z)
    mo = _mm_nn(z, Wo, F32, "mm_mo")
    vec2 = jnp.concatenate([g1, norm2_g, sh2, sc2, jnp.zeros((4, D), F32)], axis=0)
    x_mid, h2 = _resid_rms2_fwd(x[0], mo, vec2)
    W1, W3 = _gather_finish(gat_ffn, h2)
    gat_ffn2 = _gather_mid(gat_ffn2, h2)
    u1 = _mm_nn(h2, W1, BF16, "mm_u1")
    u3 = _mm_nn(h2, W3, BF16, "mm_u3")
    (W2,) = _gather_finish(gat_ffn2, (u1, u3))
    W2 = W2.reshape(1, F, D)
    a = _convgate_fwd(u1, u3, cw8, ffn_conv_b)
    f = _mm_nn(a, W2, F32, "mm_f")
    dy, df, s_loss = _loss_head(x_mid, f, g2, loss_target[0])
    loss = lax.psum(s_loss[1, 0], ("x", "y", "c"))
    d_g2 = s_loss[0:1]

    gW2 = _mm_tn(a, df, 1, "mm_gw2").reshape(N_CHIP, F // N_CHIP, D)
    da = _mm_nt(df, W2, BF16, "mm_da")
    du1, du3, s_conv = _convgate_bwd(u1, u3, da, cw8, ffn_conv_b)
    gW1 = _mm_tn(h2, du1, N_CHIP, "mm_gw1")
    gW3 = _mm_tn(h2, du3, N_CHIP, "mm_gw3")
    rs_ffn = _rs_start("ffn", [gW2, gW1, gW3])
    dh2a = _mm_nt(du1, W1, F32, "mm_dh2a")
    dh2b = _mm_nt(du3, W3, F32, "mm_dh2b")
    rs_ffn = _rs_scatter(rs_ffn, dh2b)
    dxm, dmo, s_rms2 = _resid_rms2_bwd(x_mid, dh2a, dh2b, dy, mo, vec2)
    gWo = _mm_tn(z, dmo, 1, "mm_gwo").reshape(N_CHIP, D // N_CHIP, D)
    dz = _mm_nt(dmo, Wo, BF16, "mm_dz")
    dza, dzb, dga, dgb = _merge_bwd(dz, za, zb, p, N, off_ga, off_gb)
    gWa = _mm_tn(y_a, dza, N_CHIP, "mm_gwa")
    gWb = _mm_tn(y_b, dzb, N_CHIP, "mm_gwb")
    rs_mix = _rs_start("mix", [gWo, gWa, gWb])
    dya = _mm_nt(dza, Wa, F32, "mm_dya")
    dyb = _mm_nt(dzb, Wb, BF16, "mm_dyb")
    rs_mix = _rs_scatter(rs_mix, dyb)
    dq_a, dzf, dzbk, di_a, dog, dlbl, s_ng = _hgrn_bwd(p, lbl, hgrn_norm_g, o_a, dya, st_a, N, HA)
    rs_ffn = _rs_join(rs_ffn, dq_a)
    dq_n, dk_n, dv_n, dbias, s_qk = _na_bwd(p, bias, na_q_norm_g, na_k_norm_g, cos, sin, dyb, N, off_na, HB)
    rs_mix = _rs_join(rs_mix, dq_n)
    dp = jnp.concatenate([dq_a, dzf, dzbk, di_a, dog, dq_n, dk_n, dv_n, dga, dgb], axis=1)
    gWin = _mm_tn(h_all, dp, N_CHIP, "mm_gwin")
    rs_in = _rs_start("in", [gWin])
    rs_in = _rs_scatter(rs_in, _PENDING[0])
    dh = _mm_nt(dp, Win, F32, "mm_dh")
    grad_x, s_rms1 = _rms1_bwd(xall, dh, dxm, norm1_g, scale1, N)
    d_table = _bias_grad(dbias)

    grads = {}
    big_names = ["ada_w", "w_in", "w_branch_a", "w_branch_b", "w_out", "ffn_w1", "ffn_w3", "ffn_w2"]
    small_names = [n for n in order if n not in big_names]
    delta, new_m, new_v = {}, {}, {}

    def update(nm, after=None):
        reduced = nm != "ada_w"
        d_, m_, v_, *g_ = _adamw(weights[nm][0], grads[nm][0], moms[nm][0][0], moms[nm][1][0], "adamw_" + nm,
                                 after, copy_g=reduced)
        delta[nm], new_m[nm], new_v[nm] = d_[None], m_[None], v_[None]
        if reduced:
            grads[nm] = g_[0][None]
        return d_

    last = grad_x
    for nm, g in zip(["ffn_w2", "ffn_w1", "ffn_w3"], _rs_finish(rs_ffn, last)):
        grads[nm] = g[None]
        last = update(nm, last)
    for nm, g in zip(["w_out", "w_branch_a", "w_branch_b"], _rs_finish(rs_mix, last)):
        grads[nm] = g[None]
        last = update(nm, last)
    rs_in = _rs_join(rs_in, last)

    zD = jnp.zeros((1, D), F32)
    dmod_l = jnp.concatenate([s_rms1[2:3], s_rms1[3:4], s_rms2[3:4], s_rms2[0:1], s_rms2[1:2], d_g2], axis=0)
    dmod_c = jnp.concatenate([s_rms1[0:1], s_rms1[1:2], zD, zD, zD, zD], axis=0)
    pk1, offs1 = _pack([dmod_l, dmod_c, s_rms1[4], s_rms2[2], dlbl, s_ng[0], s_qk[0], s_qk[1], d_table,
                        s_conv[0:3], s_conv[3]])
    g1all = _allgather8(pk1, "gather_small1")
    tot1 = _sum8(g1all, "sum_small1")
    dmod_rows = _unpack(g1all, offs1, 0).reshape(N_DEV, N_MOD * D)
    dmod_c_tot = _unpack(tot1, offs1, 1).reshape(1, N_MOD * D)
    dmod16 = jnp.concatenate([dmod_rows, dmod_c_tot, jnp.zeros((7, N_MOD * D), F32)], axis=0)
    dmod16_mine = lax.dynamic_slice(dmod16, (0, chip * n_ada), (16, n_ada))
    g_ada_w, dact = _ada_bwd(cs, ada_w[0], dmod16_mine)
    pk2, offs2 = _pack([dact[8]])
    g2all = _allgather8(pk2, "gather_small2")
    dact_rows = _unpack(g2all, offs2, 0)
    dact_sel = jnp.concatenate([dact_rows[2 * j][None] for j in range(N_CHIP)] + [jnp.zeros((4, D), F32)], axis=0)

    grads["ada_w"] = g_ada_w[None]
    grads["ada_b"] =(_unpack(tot1, offs1, 0) + _unpack(tot1, offs1, 1)).reshape(1, N_MOD * D)
    grads["norm1_g"] = _unpack(tot1, offs1, 2)[None]
    grads["norm2_g"] = _unpack(tot1, offs1, 3)[None]
    g_lbl = _unpack(tot1, offs1, 4)
    n_lb = HA // N_CHIP
    grads["hgrn_lb_logits"] = lax.dynamic_slice(g_lbl, (0, 0, chip * n_lb), (2, 2, n_lb))
    grads["hgrn_norm_g"] = _unpack(tot1, offs1, 5)[None]
    grads["na_q_norm_g"] = _unpack(tot1, offs1, 6)[None]
    grads["na_k_norm_g"] = _unpack(tot1, offs1, 7)[None]
    grads["na_rel_bias"] = _unpack(tot1, offs1, 8)[None]
    g_cw = _unpack(tot1, offs1, 9)
    n_f = F // N_CHIP
    grads["ffn_conv_w"] = lax.dynamic_slice(g_cw, (0, chip * n_f), (3, n_f))[None]
    grads["ffn_conv_b"] = _unpack(tot1, offs1, 10)[None]

    g_c_ctx = _dsilu_rows(dact_sel, c_ctx[None, :], "grad_c_ctx")
    grads["c_ctx"] = g_c_ctx[0]

    last = update("ada_w", g_c_ctx)
    pw, offw = _pack([weights[n] for n in small_names])
    pg, _ = _pack([grads[n] for n in small_names])
    pm, _ = _pack([moms[n][0] for n in small_names])
    pv, _ = _pack([moms[n][1] for n in small_names])
    d_, m_, v_ = _adamw(pw, pg, pm, pv, "adamw_small", last)
    for i, nm in enumerate(small_names):
        delta[nm], new_m[nm], new_v[nm] = _unpack(d_, offw, i), _unpack(m_, offw, i), _unpack(v_, offw, i)
    grads["w_in"] = _rs_finish(rs_in, d_)[0][None]
    update("w_in")

    return (loss, grad_x[None], *[grads[n] for n in order], *[delta[n] for n in order],
            *[new_m[n] for n in order], *[new_v[n] for n in order])


def _dsilu_rows(v, cv, name):
    D = v.shape[1]

    def body(v_ref, c_ref, o_ref):
        t = c_ref[...]
        s = _sigmoid(t)
        o_ref[...] = (((v_ref[0:1, :] + v_ref[1:2, :]) + v_ref[2:3, :]) + v_ref[3:4, :]) * (s * (1.0 + t * (1.0 - s)))

    return _pcall(body, name=name, out_shape=jax.ShapeDtypeStruct((1, D), F32),
                          compiler_params=_params())(v, cv)
```

```python
import functools

import numpy as np
import jax
import jax.numpy as jnp
from jax import lax
from jax.experimental import pallas as pl
from jax.experimental.pallas import tpu as pltpu

F32 = jnp.float32
BF16 = jnp.bfloat16
MESH = pl.DeviceIdType.MESH

HEAD = 128
GRID_W = 64
WIN_R = 8
WIN_C = 16
ROPE_THETA = 10000.0
EPS = 1e-6
N_MOD = 6
CHUNK = 16
HGRN_UNROLL = 4
ADAM_LR = 0.001
ADAM_B1 = 0.9
ADAM_B2 = 0.999
ADAM_EPS = 1e-08
ADAM_WD = 0.01
ADAM_STEP = 10
NEG = -1e30
VMEM_LIMIT = 56 * 1024 * 1024
N_DEV = 8
N_CHIP = 4
HI = lax.Precision.HIGHEST


def _pick(n, cands):
    for c in cands:
        if n % c == 0:
            return c
    return n


def _row_tile(rows, cols, target_bytes=1 << 20):
    want = max(16, target_bytes // (4 * cols))
    for t in (512, 256, 128, 64, 32, 16, 8):
        if t <= want and rows % t == 0:
            return t
    return rows


def _params(sem=None):
    return pltpu.CompilerParams(dimension_semantics=sem, vmem_limit_bytes=VMEM_LIMIT)


def _dot(a, b):
    return jnp.dot(a, b, preferred_element_type=F32)


def _dot_nt(a, b):
    return lax.dot_general(a, b, (((1,), (1,)), ((), ())), preferred_element_type=F32)


def _dot_tn(a, b):
    return lax.dot_general(a, b, (((0,), (0,)), ((), ())), preferred_element_type=F32)


def _sigmoid(x):
    return 1.0 / (1.0 + jnp.exp(-x))


def _col_tile(n):
    return n if n <= 1536 else _pick(n, (1024, 768, 512, 384, 256, 128))


def _mm_nn(x, w3, out_dtype, name):
    M, K = x.shape
    S, _, n = w3.shape
    tm = _pick(M, (768, 512, 256, 128, 64))
    tn = _col_tile(n)
    nb = n // tn

    def body(x_ref, w_ref, o_ref):
        o_ref[...] = _dot(x_ref[...].astype(BF16), w_ref[0]).astype(o_ref.dtype)

    return _pcall(
        body, name=name, grid=(M // tm, S * nb),
        in_specs=[pl.BlockSpec((tm, K), lambda i, j: (i, 0)),
                  pl.BlockSpec((1, K, tn), lambda i, j: (j // nb, 0, j % nb))],
        out_specs=pl.BlockSpec((tm, tn), lambda i, j: (i, j)),
        out_shape=jax.ShapeDtypeStruct((M, S * n), out_dtype),
        compiler_params=_params(("parallel", "parallel")),
    )(x, w3)


def _mm_nn_sel(x, w3, sel, out_dtype, name, prev=None):
    M, K = x.shape
    S, _, n = w3.shape
    tm = _pick(M, (768, 512, 256, 128, 64))
    tn = _col_tile(n)
    nb = n // tn
    k = sel.shape[1]

    def body(sel_ref, x_ref, w_ref, *rest):
        rest[-1][...] = _dot(x_ref[...].astype(BF16), w_ref[0]).astype(out_dtype)

    in_specs = [pl.BlockSpec((tm, K), lambda i, j, sel_ref: (i, 0)),
                pl.BlockSpec((1, K, tn), lambda i, j, sel_ref: (sel_ref[0, j // nb], 0, j % nb))]
    operands = [sel, x, w3]
    if prev is not None:
        in_specs.append(_ANY)
        operands.append(prev)
    return pl.pallas_call(
        body, name=name,
        grid_spec=pltpu.PrefetchScalarGridSpec(
            num_scalar_prefetch=1, grid=(M // tm, k * nb), in_specs=in_specs,
            out_specs=pl.BlockSpec((tm, tn), lambda i, j, sel_ref: (i, sel_ref[1, j // nb] * nb + j % nb))),
        out_shape=jax.ShapeDtypeStruct((M, S * n), out_dtype),
        input_output_aliases={} if prev is None else {3: 0},
        compiler_params=_params(("parallel", "parallel")),
    )(*operands)


def _mm_nt(dy, w3, out_dtype, name):
    M = dy.shape[0]
    S, K, n = w3.shape
    tm = _pick(M, (768, 512, 256, 128, 64))
    tk = K if K <= 2048 else _pick(K, (1408, 1024, 512, 256, 128))
    tc = n if n <= 2048 else _col_tile(n)
    nb = n // tc
    nsteps = S * nb

    def body(dy_ref, w_ref, o_ref, acc_ref):
        s = pl.program_id(2)

        @pl.when(s == 0)
        def _():
            acc_ref[...] = jnp.zeros_like(acc_ref)

        acc_ref[...] += _dot_nt(dy_ref[...].astype(BF16), w_ref[0])

        @pl.when(s == nsteps - 1)
        def _():
            o_ref[...] = acc_ref[...].astype(o_ref.dtype)

    return _pcall(
        body, name=name, grid=(M // tm, K // tk, nsteps),
        in_specs=[pl.BlockSpec((tm, tc), lambda i, k, s: (i, s)),
                  pl.BlockSpec((1, tk, tc), lambda i, k, s: (s // nb, k, s % nb))],
        out_specs=pl.BlockSpec((tm, tk), lambda i, k, s: (i, k)),
        out_shape=jax.ShapeDtypeStruct((M, K), out_dtype),
        scratch_shapes=[pltpu.VMEM((tm, tk), F32)],
        compiler_params=_params(("parallel", "parallel", "arbitrary")),
    )(dy, w3)


def _mm_tn(x, dy, S, name):
    M, K = x.shape
    n = dy.shape[1] // S
    tk = _pick(K, (512, 256, 128))
    tn = _col_tile(n)
    nb = n // tn

    def body(x_ref, dy_ref, o_ref):
        o_ref[0] = _dot_tn(x_ref[...].astype(BF16), dy_ref[...].astype(BF16)).astype(BF16)

    return _pcall(
        body, name=name, grid=(S * nb, K // tk),
        in_specs=[pl.BlockSpec((M, tk), lambda j, k: (0, k)),
                  pl.BlockSpec((M, tn), lambda j, k: (0, j))],
        out_specs=pl.BlockSpec((1, tk, tn), lambda j, k: (j // nb, k, j % nb)),
        out_shape=jax.ShapeDtypeStruct((S, K, n), BF16),
        compiler_params=_params(("parallel", "parallel")),
    )(x, dy)


def _chip_index():
    return (2 * lax.axis_index("x") + lax.axis_index("y")).astype(jnp.int32).reshape(1)


def _cast_bf16_slot(w, name):
    R, C = w.shape
    tr = _row_tile(R, C, 2 << 20)

    def body(j_ref, w_ref, o_ref):
        o_ref[0] = w_ref[...].astype(BF16)

    return _pcall(
        body, name=name,
        grid_spec=pltpu.PrefetchScalarGridSpec(
            num_scalar_prefetch=1, grid=(R // tr,),
            in_specs=[pl.BlockSpec((tr, C), lambda i, j_ref: (i, 0))],
            out_specs=pl.BlockSpec((1, tr, C), lambda i, j_ref: (j_ref[0], i, 0))),
        out_shape=jax.ShapeDtypeStruct((N_CHIP, R, C), BF16),
        compiler_params=_params(("parallel",)),
    )(_chip_index(), w)


def _pos():
    return lax.axis_index("x"), lax.axis_index("y"), lax.axis_index("c")


def _other_chips(x, y):
    return [(x, 1 - y), (1 - x, y), (1 - x, 1 - y)]


def _allgather8(v, name):
    R, C = v.shape

    def body(x_ref, out_ref, send_sems, recv_sems, local_sem):
        x, y, c = _pos()
        me, sibling = (x, y, c), (x, y, 1 - c)
        chips = _other_chips(x, y)

        def slot(px, py, pc):
            return out_ref.at[4 * px + 2 * py + pc]

        def copy(k, block, to, src=None):
            return pltpu.make_async_remote_copy(
                src_ref=slot(*block) if src is None else src, dst_ref=slot(*block),
                send_sem=send_sems.at[k], recv_sem=recv_sems.at[k], device_id=to, device_id_type=MESH)

        mine = pltpu.make_async_copy(x_ref, slot(*me), local_sem)
        mine.start()
        first = [copy(0, me, sibling, src=x_ref)]
        first += [copy(1 + j, me, (*chip, c), src=x_ref) for j, chip in enumerate(chips)]
        for cp in first:
            cp.start()
        passed = [copy(4 + j, (*chip, c), sibling) for j, chip in enumerate(chips)]
        for j, chip in enumerate(chips):
            copy(1 + j, (*chip, c), me).wait_recv()
            passed[j].start()
        copy(0, sibling, me).wait_recv()
        for j, chip in enumerate(chips):
            copy(4 + j, (*chip, 1 - c), me).wait_recv()
        for cp in first + passed:
            cp.wait_send()
        mine.wait()

    return _pcall(
        body, name=name,
        out_shape=jax.ShapeDtypeStruct((N_DEV, R, C), v.dtype),
        in_specs=[pl.BlockSpec(memory_space=pltpu.VMEM)],
        out_specs=pl.BlockSpec(memory_space=pltpu.VMEM),
        scratch_shapes=[pltpu.SemaphoreType.DMA((7,)), pltpu.SemaphoreType.DMA((7,)), pltpu.SemaphoreType.DMA],
        compiler_params=pltpu.CompilerParams(vmem_limit_bytes=VMEM_LIMIT),
    )(v)


_HBM = pl.BlockSpec(memory_space=pltpu.HBM)
_SEM = pl.BlockSpec(memory_space=pltpu.SEMAPHORE)
_ANY = pl.BlockSpec(memory_space=pl.ANY)
_EFFECT = pltpu.SideEffectType.DATAFLOW_SIDE_EFFECTING
_PENDING = []


def _pcall(body, **kw):
    def run(*operands):
        if not _PENDING or "in_specs" not in kw:
            return pl.pallas_call(body, **kw)(*operands)
        deps = list(_PENDING)
        n = len(operands)

        def tied(*refs):
            return body(*refs[:n], *refs[n + len(deps):])

        return pl.pallas_call(tied, **{**kw, "in_specs": list(kw["in_specs"]) + [_ANY] * len(deps)})(*operands, *deps)
    return run


def _copies(plan, refs, send_sems, recv_sems):
    return [pltpu.make_async_remote_copy(src_ref=src, dst_ref=dst, send_sem=send_sems.at[k], recv_sem=recv_sems.at[k],
                                         device_id=dev, device_id_type=MESH)
            for k, (src, dst, dev) in enumerate(plan(refs))]


def _xfer_start(name, bufs, plan, n_copies, after=None):
    n = len(bufs)
    deps = list(_PENDING) + ([after] if after is not None else [])
    nd = len(deps)

    def body(*refs):
        for cp in _copies(plan, refs[:n], refs[n + nd], refs[n + nd + 1]):
            cp.start()
        refs[-1][...] = jnp.zeros_like(refs[-1])

    outs = pl.pallas_call(
        body, name=name,
        out_shape=(pltpu.SemaphoreType.DMA((n_copies,)), pltpu.SemaphoreType.DMA((n_copies,)),
                   *[pltpu.HBM(b.shape, b.dtype) for b in bufs], jax.ShapeDtypeStruct((8, 128), F32)),
        in_specs=[_HBM] * n + [_ANY] * nd,
        out_specs=(_SEM, _SEM, *[_HBM] * n, pl.BlockSpec(memory_space=pltpu.VMEM)),
        input_output_aliases={t: 2 + t for t in range(n)},
        compiler_params=pltpu.CompilerParams(has_side_effects=_EFFECT),
    )(*[pltpu.with_memory_space_constraint(b, pltpu.HBM) for b in bufs], *deps)
    _PENDING[:] = [outs[-1]]
    return (outs[0], outs[1]), list(outs[2:2 + n])


def _xfer_wait(name, sems, bufs, plan, after):
    n = len(bufs)
    after = tuple(after) if isinstance(after, (tuple, list)) else (after,)

    def body(*refs):
        cps = _copies(plan, refs[:n], refs[n], refs[n + 1])
        for cp in cps:
            cp.wait_send()
        for cp in cps:
            cp.wait_recv()

    outs = pl.pallas_call(
        body, name=name,
        out_shape=tuple(pltpu.HBM(b.shape, b.dtype) for b in bufs),
        in_specs=[_HBM] * n + [_SEM, _SEM] + [_ANY] * len(after),
        out_specs=tuple([_HBM] * n),
        input_output_aliases={t: t for t in range(n)},
        compiler_params=pltpu.CompilerParams(has_side_effects=_EFFECT),
    )(*bufs, sems[0], sems[1], *after)
    return list(outs)


def _half(ref_rows, hc):
    h = ref_rows // 2
    return pl.ds(hc * h, h)


ALL_CHIPS = (0, 1, 2)
NEIGHBOURS = (0, 1)
DIAGONAL = (2,)


def _plan_gather_ici(which):
    def plan(bufs):
        x, y, c = _pos()
        j = 2 * x + y
        chips = _other_chips(x, y)
        return [(b.at[j, _half(b.shape[1], c)], b.at[j, _half(b.shape[1], c)], (*chips[k], c))
                for b in bufs for k in which]
    return plan


def _plan_gather_d2d(which):
    def plan(bufs):
        x, y, c = _pos()
        chips = _other_chips(x, y)
        out = []
        for b in bufs:
            for k in which:
                blk = b.at[2 * chips[k][0] + chips[k][1], _half(b.shape[1], c)]
                out.append((blk, blk, (x, y, 1 - c)))
        return out
    return plan


def _plan_pair_swap(n):
    def plan(bufs):
        x, y, c = _pos()
        return [(g.at[:, _half(g.shape[1], 1 - c)], land, (x, y, 1 - c)) for g, land in zip(bufs[:n], bufs[n:])]
    return plan


def _plan_chip_scatter(n):
    def plan(bufs):
        x, y, c = _pos()
        return [(p.at[2 * chip[0] + chip[1]], land.at[k], (*chip, c))
                for p, land in zip(bufs[:n], bufs[n:]) for k, chip in enumerate(_other_chips(x, y))]
    return plan


def _plan_pair_join(bufs):
    x, y, c = _pos()
    return [(b.at[_half(b.shape[0], c)], b.at[_half(b.shape[0], c)], (x, y, 1 - c)) for b in bufs]


def _empty_hbm(shape, dtype):
    return pltpu.with_memory_space_constraint(lax.empty(shape, dtype), pltpu.HBM)


def _gather_start(tag, bufs, after=None):
    sems, bufs = _xfer_start(f"gather_ici_start_{tag}", bufs, _plan_gather_ici(ALL_CHIPS), 3 * len(bufs), after)
    return dict(tag=tag, sems=sems, bufs=bufs)


def _gather_mid(st, after):
    tag = st["tag"]
    bufs = _xfer_wait(f"gather_ici_wait_{tag}", st["sems"], st["bufs"], _plan_gather_ici(ALL_CHIPS), after)
    sems, bufs = _xfer_start(f"gather_d2d_start_{tag}", bufs, _plan_gather_d2d(ALL_CHIPS), 3 * len(bufs))
    return dict(tag=tag, sems=sems, bufs=bufs)


def _gather_finish(st, after):
    return _xfer_wait(f"gather_d2d_wait_{st['tag']}", st["sems"], st["bufs"], _plan_gather_d2d(ALL_CHIPS), after)


def _d2d_hand_over(tag, sems, bufs, which, after):
    bufs = _xfer_wait(f"gather_ici_wait_{tag}", sems, bufs, _plan_gather_ici(which), after)
    sems, bufs = _xfer_start(f"gather_d2d_start_{tag}", bufs, _plan_gather_d2d(which), len(which) * len(bufs))
    return _xfer_wait(f"gather_d2d_wait_{tag}", sems, bufs, _plan_gather_d2d(which), after)


def _pair_add(g, r, name):
    S, R, C = g.shape
    h = R // 2
    tr = _row_tile(h, C)
    nb = h // tr

    def body(c_ref, g_ref, r_ref, o_ref):
        o_ref[...] = (g_ref[...].astype(F32) + r_ref[...].astype(F32)).astype(BF16)

    return _pcall(
        body, name=name,
        grid_spec=pltpu.PrefetchScalarGridSpec(
            num_scalar_prefetch=1, grid=(S, nb),
            in_specs=[pl.BlockSpec((1, tr, C), lambda s, i, c_ref: (s, c_ref[0] * nb + i, 0)),
                      pl.BlockSpec((1, tr, C), lambda s, i, c_ref: (s, i, 0))],
            out_specs=pl.BlockSpec((1, tr, C), lambda s, i, c_ref: (s, i, 0))),
        out_shape=jax.ShapeDtypeStruct((S, h, C), BF16),
        compiler_params=_params(("parallel", "parallel")),
    )(lax.axis_index("c").astype(jnp.int32).reshape(1), g, r)


def _chip_sum(p, rb, name):
    S, h, C = p.shape
    tr = _row_tile(h, C)
    nb = h // tr
    jc = jnp.concatenate([_chip_index(), lax.axis_index("c").astype(jnp.int32).reshape(1)])

    def body(jc_ref, p_ref, r_ref, o_ref):
        o_ref[...] = ((p_ref[0].astype(F32) + r_ref[0].astype(F32)) + r_ref[1].astype(F32)) + r_ref[2].astype(F32)

    return _pcall(
        body, name=name,
        grid_spec=pltpu.PrefetchScalarGridSpec(
            num_scalar_prefetch=1, grid=(nb,),
            in_specs=[pl.BlockSpec((1, tr, C), lambda i, jc_ref: (jc_ref[0], i, 0)),
                      pl.BlockSpec((3, tr, C), lambda i, jc_ref: (0, i, 0))],
            out_specs=pl.BlockSpec((tr, C), lambda i, jc_ref: (jc_ref[1] * nb + i, 0))),
        out_shape=jax.ShapeDtypeStruct((2 * h, C), F32),
        compiler_params=_params(("parallel",)),
    )(jc, p, rb)


def _rs_start(tag, gs):
    n = len(gs)
    lands = [_empty_hbm((g.shape[0], g.shape[1] // 2, g.shape[2]), g.dtype) for g in gs]
    sems, bufs = _xfer_start(f"rs_swap_start_{tag}", list(gs) + lands, _plan_pair_swap(n), n)
    return dict(tag=tag, n=n, sems=sems, bufs=bufs)


def _rs_scatter(st, after):
    tag, n = st["tag"], st["n"]
    bufs = _xfer_wait(f"rs_swap_wait_{tag}", st["sems"], st["bufs"], _plan_pair_swap(n), after)
    ps = [_pair_add(g, r, f"rs_pair_add_{tag}{t}") for t, (g, r) in enumerate(zip(bufs[:n], bufs[n:]))]
    lands = [_empty_hbm((3,) + p.shape[1:], p.dtype) for p in ps]
    sems, bufs = _xfer_start(f"rs_scatter_start_{tag}", ps + lands, _plan_chip_scatter(n), 3 * n)
    return dict(tag=tag, n=n, sems=sems, bufs=bufs)


def _rs_join(st, after):
    tag, n = st["tag"], st["n"]
    bufs = _xfer_wait(f"rs_scatter_wait_{tag}", st["sems"], st["bufs"], _plan_chip_scatter(n), after)
    fs = [_chip_sum(p, rb, f"rs_chip_sum_{tag}{t}") for t, (p, rb) in enumerate(zip(bufs[:n], bufs[n:]))]
    sems, bufs = _xfer_start(f"rs_join_start_{tag}", fs, _plan_pair_join, n)
    return dict(tag=tag, n=n, sems=sems, bufs=bufs)


def _rs_finish(st, after):
    return _xfer_wait(f"rs_join_wait_{st['tag']}", st["sems"], st["bufs"], _plan_pair_join, after)


def _sum8(g, name):
    _, R, C = g.shape

    def body(g_ref, o_ref):
        acc = g_ref[0]
        for d in range(1, N_DEV):
            acc = acc + g_ref[d]
        o_ref[...] = acc

    return _pcall(body, name=name, out_shape=jax.ShapeDtypeStruct((R, C), F32),
                          compiler_params=_params())(g)


def _ada_fwd(cs, w, b):
    D, n = w.shape
    tn = _pick(n, (512, 384, 256, 128))

    def body(c_ref, w_ref, b_ref, o_ref):
        cv = c_ref[...]
        a = (cv * _sigmoid(cv)).astype(BF16)
        o_ref[...] = _dot(a, w_ref[...].astype(BF16)) + b_ref[...]

    return _pcall(
        body, name="ada_fwd", grid=(n // tn,),
        in_specs=[pl.BlockSpec((16, D), lambda j: (0, 0)), pl.BlockSpec((D, tn), lambda j: (0, j)),
                  pl.BlockSpec((1, tn), lambda j: (0, j))],
        out_specs=pl.BlockSpec((16, tn), lambda j: (0, j)),
        out_shape=jax.ShapeDtypeStruct((16, n), F32),
        compiler_params=_params(("parallel",)),
    )(cs, w, b)


def _ada_bwd(cs, w, dmod):
    D, n = w.shape
    tn = _pick(n, (512, 384, 256, 128))

    def body(c_ref, w_ref, d_ref, gw_ref, da_ref):
        j = pl.program_id(0)
        cv = c_ref[...]
        a = cv * _sigmoid(cv)
        d = d_ref[...]
        gw_ref[...] = lax.dot_general(a, d, (((0,), (0,)), ((), ())), precision=HI, preferred_element_type=F32)

        @pl.when(j == 0)
        def _():
            da_ref[...] = jnp.zeros_like(da_ref)

        da_ref[...] += _dot_nt(d.astype(BF16), w_ref[...].astype(BF16))

    return _pcall(
        body, name="ada_bwd", grid=(n // tn,),
        in_specs=[pl.BlockSpec((16, D), lambda j: (0, 0)), pl.BlockSpec((D, tn), lambda j: (0, j)),
                  pl.BlockSpec((16, tn), lambda j: (0, j))],
        out_specs=[pl.BlockSpec((D, tn), lambda j: (0, j)), pl.BlockSpec((16, D), lambda j: (0, 0))],
        out_shape=[jax.ShapeDtypeStruct((D, n), F32), jax.ShapeDtypeStruct((16, D), F32)],
        compiler_params=_params(("arbitrary",)),
    )(cs, w, dmod)


def _rms1_fwd(xall, gain, shift2, scale2, n_ctx):
    T, D = xall.shape
    tb = _pick(n_ctx, (256, 128, 64, 32, 16))
    nctx = n_ctx // tb

    def body(x_ref, g_ref, sh_ref, sc_ref, o_ref):
        i = pl.program_id(0)
        xv = x_ref[...]
        r = lax.rsqrt(jnp.mean(xv * xv, axis=-1, keepdims=True) + EPS)
        nrm = xv * r * g_ref[...]
        lat = i >= nctx
        sh = jnp.where(lat, sh_ref[1:2, :], sh_ref[0:1, :])
        sc = jnp.where(lat, sc_ref[1:2, :], sc_ref[0:1, :])
        o_ref[...] = (nrm * (1.0 + sc) + sh).astype(BF16)

    vec = lambda r: pl.BlockSpec((r, D), lambda i: (0, 0))
    return _pcall(
        body, name="rms1_fwd", grid=(T // tb,),
        in_specs=[pl.BlockSpec((tb, D), lambda i: (i, 0)), vec(1), vec(2), vec(2)],
        out_specs=pl.BlockSpec((tb, D), lambda i: (i, 0)),
        out_shape=jax.ShapeDtypeStruct((T, D), BF16),
        compiler_params=_params(("parallel",)),
    )(xall, gain, shift2, scale2)


def _rms1_bwd(xall, dh, dxmid, gain, scale2, n_ctx):
    T, D = xall.shape
    L = T - n_ctx
    tb = _pick(n_ctx, (256, 128, 64, 32, 16))
    nctx = n_ctx // tb

    def body(x_ref, dh_ref, dxm_ref, g_ref, sc_ref, dx_ref, cs_ref):
        i = pl.program_id(0)
        lat = i >= nctx
        xv = x_ref[...]
        r = lax.rsqrt(jnp.mean(xv * xv, axis=-1, keepdims=True) + EPS)
        xh = xv * r
        g = g_ref[...]
        nrm = xh * g
        sc = jnp.where(lat, sc_ref[1:2, :], sc_ref[0:1, :])
        dhv = dh_ref[...]
        dn = dhv * (1.0 + sc)
        dxh = dn * g
        dxv = r * (dxh - xh * jnp.mean(dxh * xh, axis=-1, keepdims=True))
        s_sh = jnp.sum(dhv, axis=0, keepdims=True)
        s_sc = jnp.sum(dhv * nrm, axis=0, keepdims=True)
        s_g = jnp.sum(dn * xh, axis=0, keepdims=True)
        zero = jnp.zeros_like(s_sh)
        rows = lax.broadcasted_iota(jnp.int32, (8, D), 0)
        upd = jnp.where(rows == 0, jnp.where(lat, zero, s_sh),
              jnp.where(rows == 1, jnp.where(lat, zero, s_sc),
              jnp.where(rows == 2, jnp.where(lat, s_sh, zero),
              jnp.where(rows == 3, jnp.where(lat, s_sc, zero),
              jnp.where(rows == 4, s_g, 0.0)))))

        @pl.when(i == 0)
        def _():
            cs_ref[...] = jnp.zeros_like(cs_ref)

        cs_ref[...] += upd

        @pl.when(lat)
        def _():
            dx_ref[...] = dxv + dxm_ref[...]

    lat_blk = lambda i: (jnp.maximum(i - nctx, 0), 0)
    vec = lambda r: pl.BlockSpec((r, D), lambda i: (0, 0))
    return _pcall(
        body, name="rms1_bwd", grid=(T // tb,),
        in_specs=[pl.BlockSpec((tb, D), lambda i: (i, 0)), pl.BlockSpec((tb, D), lambda i: (i, 0)),
                  pl.BlockSpec((tb, D), lat_blk), vec(1), vec(2)],
        out_specs=[pl.BlockSpec((tb, D), lat_blk), vec(8)],
        out_shape=[jax.ShapeDtypeStruct((L, D), F32), jax.ShapeDtypeStruct((8, D), F32)],
        compiler_params=_params(("arbitrary",)),
    )(xall, dh, dxmid, gain, scale2)


def _resid_rms2_fwd(x, mo, vecs):
    L, D = x.shape
    tb = _pick(L, (256, 128, 64))

    def body(x_ref, mo_ref, v_ref, xm_ref, h_ref):
        xm = x_ref[...] + v_ref[0:1, :] * mo_ref[...]
        xm_ref[...] = xm
        r = lax.rsqrt(jnp.mean(xm * xm, axis=-1, keepdims=True) + EPS)
        h_ref[...] = (xm * r * v_ref[1:2, :] * (1.0 + v_ref[3:4, :]) + v_ref[2:3, :]).astype(BF16)

    blk = pl.BlockSpec((tb, D), lambda i: (i, 0))
    return _pcall(
        body, name="resid_rms2_fwd", grid=(L // tb,),
        in_specs=[blk, blk, pl.BlockSpec((8, D), lambda i: (0, 0))],
        out_specs=[blk, blk],
        out_shape=[jax.ShapeDtypeStruct((L, D), F32), jax.ShapeDtypeStruct((L, D), BF16)],
        compiler_params=_params(("parallel",)),
    )(x, mo, vecs)


def _resid_rms2_bwd(xmid, dh_a, dh_b, dy, mo, vecs):
    L, D = xmid.shape
    tb = _pick(L, (256, 128, 64))

    def body(xm_ref, da_ref, db_ref, dy_ref, mo_ref, v_ref, dxm_ref, dmo_ref, cs_ref):
        i = pl.program_id(0)
        xm = xm_ref[...]
        r = lax.rsqrt(jnp.mean(xm * xm, axis=-1, keepdims=True) + EPS)
        xh = xm * r
        g = v_ref[1:2, :]
        nrm = xh * g
        dhv = da_ref[...] + db_ref[...]
        dn = dhv * (1.0 + v_ref[3:4, :])
        dxh = dn * g
        dxm = dy_ref[...] + r * (dxh - xh * jnp.mean(dxh * xh, axis=-1, keepdims=True))
        dxm_ref[...] = dxm
        dmo_ref[...] = (dxm * v_ref[0:1, :]).astype(BF16)
        s0 = jnp.sum(dhv, axis=0, keepdims=True)
        s1 = jnp.sum(dhv * nrm, axis=0, keepdims=True)
        s2 = jnp.sum(dn * xh, axis=0, keepdims=True)
        s3 = jnp.sum(dxm * mo_ref[...], axis=0, keepdims=True)
        rows = lax.broadcasted_iota(jnp.int32, (8, D), 0)
        upd = jnp.where(rows == 0, s0, jnp.where(rows == 1, s1, jnp.where(rows == 2, s2,
              jnp.where(rows == 3, s3, 0.0))))

        @pl.when(i == 0)
        def _():
            cs_ref[...] = jnp.zeros_like(cs_ref)

        cs_ref[...] += upd

    blk = pl.BlockSpec((tb, D), lambda i: (i, 0))
    vec = pl.BlockSpec((8, D), lambda i: (0, 0))
    return _pcall(
        body, name="resid_rms2_bwd", grid=(L // tb,),
        in_specs=[blk, blk, blk, blk, blk, vec],
        out_specs=[blk, blk, vec],
        out_shape=[jax.ShapeDtypeStruct((L, D), F32), jax.ShapeDtypeStruct((L, D), BF16),
                   jax.ShapeDtypeStruct((8, D), F32)],
        compiler_params=_params(("arbitrary",)),
    )(xmid, dh_a, dh_b, dy, mo, vecs)


def _loss_head(xmid, f, g2, target):
    L, D = xmid.shape
    tb = _pick(L, (256, 128, 64))

    def body(xm_ref, f_ref, g_ref, t_ref, dy_ref, df_ref, s_ref):
        i = pl.program_id(0)
        fv = f_ref[...]
        g = g_ref[...]
        err = xm_ref[...] + g * fv - t_ref[...]
        dy = err * (1.0 / D)
        dy_ref[...] = dy
        df_ref[...] = (dy * g).astype(BF16)
        s0 = jnp.sum(dy * fv, axis=0, keepdims=True)
        part = 0.5 * jnp.sum(jnp.mean(err * err, axis=-1, keepdims=True), axis=0, keepdims=True)
        rows = lax.broadcasted_iota(jnp.int32, (8, D), 0)
        upd = jnp.where(rows == 0, s0, jnp.where(rows == 1, part, 0.0))

        @pl.when(i == 0)
        def _():
            s_ref[...] = jnp.zeros_like(s_ref)

        s_ref[...] += upd

    blk = pl.BlockSpec((tb, D), lambda i: (i, 0))
    return _pcall(
        body, name="loss_head", grid=(L // tb,),
        in_specs=[blk, blk, pl.BlockSpec((1, D), lambda i: (0, 0)), blk],
        out_specs=[blk, blk, pl.BlockSpec((8, D), lambda i: (0, 0))],
        out_shape=[jax.ShapeDtypeStruct((L, D), F32), jax.ShapeDtypeStruct((L, D), BF16),
                   jax.ShapeDtypeStruct((8, D), F32)],
        compiler_params=_params(("arbitrary",)),
    )(xmid, f, g2, target)


def _gate_cols(D, off):
    tc = _pick(np.gcd(D, off), (512, 256, 128))
    return tc, off // tc


def _merge_fwd(za, zb, p, n_ctx, off_a, off_b):
    L, D = za.shape
    tb = _pick(n_ctx, (256, 128, 64, 32, 16))
    nctx = n_ctx // tb
    tc, oa = _gate_cols(D, off_a)
    _, ob = _gate_cols(D, off_b)
    if off_b % tc:
        raise ValueError("gate column offsets must share a column tile")
    ob = off_b // tc

    def body(za_ref, zb_ref, ga_ref, gb_ref, z_ref):
        z_ref[...] = (_sigmoid(ga_ref[...]) * za_ref[...].astype(F32)
                      + _sigmoid(gb_ref[...]) * zb_ref[...].astype(F32)).astype(BF16)

    blk = pl.BlockSpec((tb, tc), lambda i, j: (i, j))
    return _pcall(
        body, name="merge_fwd", grid=(L // tb, D // tc),
        in_specs=[blk, blk, pl.BlockSpec((tb, tc), lambda i, j: (i + nctx, oa + j)),
                  pl.BlockSpec((tb, tc), lambda i, j: (i + nctx, ob + j))],
        out_specs=blk,
        out_shape=jax.ShapeDtypeStruct((L, D), BF16),
        compiler_params=_params(("parallel", "parallel")),
    )(za, zb, p, p)


def _merge_bwd(dz, za, zb, p, n_ctx, off_a, off_b):
    L, D = za.shape
    T = L + n_ctx
    tb = _pick(n_ctx, (256, 128, 64, 32, 16))
    nctx = n_ctx // tb
    tc = _gate_cols(D, off_a)[0]
    oa, ob = off_a // tc, off_b // tc

    def body(dz_ref, za_ref, zb_ref, ga_ref, gb_ref, dza_ref, dzb_ref, dga_ref, dgb_ref):
        i = pl.program_id(1)

        @pl.when(i < nctx)
        def _():
            dga_ref[...] = jnp.zeros_like(dga_ref)
            dgb_ref[...] = jnp.zeros_like(dgb_ref)

        @pl.when(i >= nctx)
        def _():
            dzv = dz_ref[...].astype(F32)
            sa = _sigmoid(ga_ref[...])
            sb = _sigmoid(gb_ref[...])
            dza_ref[...] = (dzv * sa).astype(BF16)
            dzb_ref[...] = (dzv * sb).astype(BF16)
            dga_ref[...] = (dzv * za_ref[...].astype(F32) * sa * (1.0 - sa)).astype(BF16)
            dgb_ref[...] = (dzv * zb_ref[...].astype(F32) * sb * (1.0 - sb)).astype(BF16)

    lat = pl.BlockSpec((tb, tc), lambda j, i: (jnp.maximum(i - nctx, 0), j))
    allr = pl.BlockSpec((tb, tc), lambda j, i: (i, j))
    return _pcall(
        body, name="merge_bwd", grid=(D // tc, T // tb),
        in_specs=[lat, lat, lat, pl.BlockSpec((tb, tc), lambda j, i: (i, oa + j)),
                  pl.BlockSpec((tb, tc), lambda j, i: (i, ob + j))],
        out_specs=[lat, lat, allr, allr],
        out_shape=[jax.ShapeDtypeStruct((L, D), BF16), jax.ShapeDtypeStruct((L, D), BF16),
                   jax.ShapeDtypeStruct((T, D), BF16), jax.ShapeDtypeStruct((T, D), BF16)],
        compiler_params=_params(("arbitrary", "arbitrary")),
    )(dz, za, zb, p, p)


def _shift_down(u, rows):
    return jnp.where(rows == 0, 0.0, pltpu.roll(u, 1, 0))


def _shift_up(u, rows):
    n = u.shape[0]
    return jnp.where(rows == n - 1, 0.0, pltpu.roll(u, n - 1, 0))


def _convgate_fwd(u1, u3, cw, cb):
    L, F = u1.shape
    tc = _pick(F, (256, 128))

    def body(u1_ref, u3_ref, w_ref, b_ref, a_ref):
        u = u1_ref[...].astype(F32)
        rows = lax.broadcasted_iota(jnp.int32, u.shape, 0)
        cv = _shift_down(u, rows) * w_ref[0:1, :] + u * w_ref[1:2, :] + _shift_up(u, rows) * w_ref[2:3, :] + b_ref[...]
        a_ref[...] = (cv * _sigmoid(cv) * u3_ref[...].astype(F32)).astype(BF16)

    blk = pl.BlockSpec((L, tc), lambda j: (0, j))
    return _pcall(
        body, name="convgate_fwd", grid=(F // tc,),
        in_specs=[blk, blk, pl.BlockSpec((8, tc), lambda j: (0, j)), pl.BlockSpec((1, tc), lambda j: (0, j))],
        out_specs=blk,
        out_shape=jax.ShapeDtypeStruct((L, F), BF16),
        compiler_params=_params(("parallel",)),
    )(u1, u3, cw, cb)


def _convgate_bwd(u1, u3, da, cw, cb):
    L, F = u1.shape
    tc = _pick(F, (256, 128))

    def body(u1_ref, u3_ref, da_ref, w_ref, b_ref, du1_ref, du3_ref, s_ref):
        u = u1_ref[...].astype(F32)
        rows = lax.broadcasted_iota(jnp.int32, u.shape, 0)
        um, up = _shift_down(u, rows), _shift_up(u, rows)
        w0, w1, w2 = w_ref[0:1, :], w_ref[1:2, :], w_ref[2:3, :]
        cv = um * w0 + u * w1 + up * w2 + b_ref[...]
        s = _sigmoid(cv)
        dav = da_ref[...].astype(F32)
        du3_ref[...] = (dav * cv * s).astype(BF16)
        dcv = dav * u3_ref[...].astype(F32) * (s * (1.0 + cv * (1.0 - s)))
        du1_ref[...] = (_shift_up(dcv, rows) * w0 + dcv * w1 + _shift_down(dcv, rows) * w2).astype(BF16)
        r8 = lax.broadcasted_iota(jnp.int32, (8, tc), 0)
        s0 = jnp.sum(dcv * um, axis=0, keepdims=True)
        s1 = jnp.sum(dcv * u, axis=0, keepdims=True)
        s2 = jnp.sum(dcv * up, axis=0, keepdims=True)
        s3 = jnp.sum(dcv, axis=0, keepdims=True)
        s_ref[...] = jnp.where(r8 == 0, s0, jnp.where(r8 == 1, s1, jnp.where(r8 == 2, s2,
                     jnp.where(r8 == 3, s3, 0.0))))

    blk = pl.BlockSpec((L, tc), lambda j: (0, j))
    v8 = pl.BlockSpec((8, tc), lambda j: (0, j))
    return _pcall(
        body, name="convgate_bwd", grid=(F // tc,),
        in_specs=[blk, blk, blk, v8, pl.BlockSpec((1, tc), lambda j: (0, j))],
        out_specs=[blk, blk, v8],
        out_shape=[jax.ShapeDtypeStruct((L, F), BF16), jax.ShapeDtypeStruct((L, F), BF16),
                   jax.ShapeDtypeStruct((8, F), F32)],
        compiler_params=_params(("parallel",)),
    )(u1, u3, da, cw, cb)


def _lower_bound(lbl_ref, d):
    l0, l1 = lbl_ref[d, 0:1, :], lbl_ref[d, 1:2, :]
    m = jnp.maximum(l0, l1)
    e0, e1 = jnp.exp(l0 - m), jnp.exp(l1 - m)
    return e0 / (e0 + e1)


def _chunk_cumsum(x, rev):
    n = x.shape[0]
    r = lax.broadcasted_iota(jnp.int32, x.shape, 0) % CHUNK
    k = 1
    while k < CHUNK:
        if rev:
            x = x + jnp.where(r < CHUNK - k, pltpu.roll(x, n - k, 0), 0.0)
        else:
            x = x + jnp.where(r >= k, pltpu.roll(x, k, 0), 0.0)
        k *= 2
    return x


def _gate_terms(z, lb):
    sg = _sigmoid(z)
    f = lb + (1.0 - lb) * sg
    return sg, f


def _decay_terms(z, lb, rev):
    _, f = _gate_terms(z, lb)
    g = jnp.log(f)
    return 1.0 - f, _chunk_cumsum(g, rev), _chunk_cumsum(g, not rev) - g


def _chunk_total(c, rev):
    return c[0:1, :] if rev else c[CHUNK - 1:CHUNK, :]


def _pair_decay(c, s, rev):
    t = lax.broadcasted_iota(jnp.int32, (CHUNK, 1), 0)
    later = (t <= s) if rev else (t >= s)
    return jnp.where(later, jnp.exp(c - c[s:s + 1, :]), 0.0)


def _scan_chunk(i, n_ctx_chunks, n_chunks, rev):
    if not rev:
        return i
    return jnp.where(i < n_ctx_chunks, n_ctx_chunks - 1 - i, n_chunks + n_ctx_chunks - 1 - i)


def _rows(ci):
    return pl.ds(pl.multiple_of(ci * CHUNK, CHUNK), CHUNK)


def _hgrn_cols(HA):
    return HA // HEAD


def _hgrn_fwd(p, lbl, ng, n_ctx, HA):
    T = p.shape[0]
    L = T - n_ctx
    nh = _hgrn_cols(HA)
    nc, ncc = T // CHUNK, n_ctx // CHUNK

    def body(q_ref, zf_ref, zb_ref, v_ref, og_ref, lbl_ref, ng_ref, ya_ref, o_ref, st_ref,
             c_scr, k_scr, qe_scr, ke_scr, o_scr):
        dirs = ((0, False, zf_ref), (1, True, zb_ref))
        for d, rev, z_ref in dirs:
            k, c, rest = _decay_terms(z_ref[...], _lower_bound(lbl_ref, d), rev)
            c_scr[d] = c
            k_scr[d] = k
            qe_scr[d] = (q_ref[...] * jnp.exp(c)).astype(BF16)
            ke_scr[d] = (k * jnp.exp(rest)).astype(BF16)

        def step(i2, states):
            states = list(states)
            for u in range(HGRN_UNROLL):
                for d, rev, _ in dirs:
                    St = states[d]
                    ci = _scan_chunk(HGRN_UNROLL * i2 + u, ncc, nc, rev)
                    rows = _rows(ci)
                    q, v, c, k = q_ref[rows, :], v_ref[rows, :], c_scr[d, rows, :], k_scr[d, rows, :]
                    st_ref[0, d, ci] = St.astype(BF16)
                    o = jnp.zeros((CHUNK, HEAD), F32)
                    for s in range(CHUNK):
                        E = _pair_decay(c, s, rev)
                        a = jnp.sum(q * E * k[s:s + 1, :], axis=1, keepdims=True)
                        o = o + a * v[s:s + 1, :]
                    o_scr[d, rows, :] = o + _dot_nt(qe_scr[d, rows, :], St.astype(BF16))
                    states[d] = St * jnp.exp(_chunk_total(c, rev)) + _dot_tn(v.astype(BF16), ke_scr[d, rows, :])
            return tuple(states)

        if nc % HGRN_UNROLL:
            raise ValueError("the number of chunks must be a multiple of HGRN_UNROLL")
        zero = jnp.zeros((HEAD, HEAD), F32)
        lax.fori_loop(0, nc // HGRN_UNROLL, step, (zero, zero))

        o = o_scr[0, pl.ds(n_ctx, L), :] + o_scr[1, pl.ds(n_ctx, L), :]
        o_ref[...] = o
        r = lax.rsqrt(jnp.mean(o * o, axis=-1, keepdims=True) + EPS)
        og = og_ref[pl.ds(n_ctx, L), :]
        ya_ref[...] =(o * r * ng_ref[...] * (og * _sigmoid(og))).astype(BF16)

    cb = HA // HEAD
    col = lambda kk: pl.BlockSpec((T, HEAD), lambda h: (0, kk * cb + h))
    return _pcall(
        body, name="hgrn_fwd", grid=(nh,),
        in_specs=[col(0), col(1), col(2), col(3), col(4),
                  pl.BlockSpec((2, 2, HEAD), lambda h: (0, 0, h)), pl.BlockSpec((1, HEAD), lambda h: (0, 0))],
        out_specs=[pl.BlockSpec((L, HEAD), lambda h: (0, h)), pl.BlockSpec((L, HEAD), lambda h: (0, h)),
                   pl.BlockSpec((1, 2, nc, HEAD, HEAD), lambda h: (h, 0, 0, 0, 0))],
        out_shape=[jax.ShapeDtypeStruct((L, HA), BF16), jax.ShapeDtypeStruct((L, HA), F32),
                   jax.ShapeDtypeStruct((nh, 2, nc, HEAD, HEAD), BF16)],
        scratch_shapes=[pltpu.VMEM((2, T, HEAD), F32), pltpu.VMEM((2, T, HEAD), F32),
                        pltpu.VMEM((2, T, HEAD), BF16), pltpu.VMEM((2, T, HEAD), BF16),
                        pltpu.VMEM((2, T, HEAD), F32)],
        compiler_params=_params(("parallel",)),
    )(p, p, p, p, p, lbl, ng)


def _hgrn_bwd(p, lbl, ng, o, dya, st, n_ctx, HA):
    T = p.shape[0]
    L = T - n_ctx
    nh = _hgrn_cols(HA)
    nc, ncc = T // CHUNK, n_ctx // CHUNK

    def body(q_ref, zf_ref, zb_ref, v_ref, og_ref, lbl_ref, ng_ref, o_ref, dya_ref, st_ref,
             dq_ref, dzf_ref, dzb_ref, dv_ref, dog_ref, dlbl_ref, dng_ref,
             do_scr, c_scr, k_scr, qe_scr, ke_scr, dg_scr, dk_scr, dq_scr, dv_scr, row_scr):
        h = pl.program_id(0)
        ov = o_ref[...]
        r = lax.rsqrt(jnp.mean(ov * ov, axis=-1, keepdims=True) + EPS)
        oh = ov * r
        ogv = og_ref[pl.ds(n_ctx, L), :]
        sg_o = _sigmoid(ogv)
        dyv = dya_ref[...]
        ngv = ng_ref[...]
        dog_ref[pl.ds(0, n_ctx), :] = jnp.zeros((n_ctx, HEAD), BF16)
        dog_ref[pl.ds(n_ctx, L), :] = (dyv * oh * ngv * (sg_o * (1.0 + ogv * (1.0 - sg_o)))).astype(BF16)
        don = dyv * (ogv * sg_o)
        dng = jnp.sum(don * oh, axis=0, keepdims=True)
        doh = don * ngv
        do_scr[pl.ds(0, n_ctx), :] = jnp.zeros((n_ctx, HEAD), F32)
        do_scr[pl.ds(n_ctx, L), :] = r * (doh - oh * jnp.mean(doh * oh, axis=-1, keepdims=True))

        @pl.when(h == 0)
        def _():
            dng_ref[...] = jnp.zeros_like(dng_ref)

        dng_ref[0:1, :] += dng

        t16 = lax.broadcasted_iota(jnp.int32, (CHUNK, HEAD), 0)
        dirs = ((0, False, zf_ref, dzf_ref), (1, True, zb_ref, dzb_ref))
        for d, rev, z_ref, _ in dirs:
            k, c, rest = _decay_terms(z_ref[...], _lower_bound(lbl_ref, d), rev)
            c_scr[d] = c
            k_scr[d] = k
            qe_scr[d] = (q_ref[...] * jnp.exp(c)).astype(BF16)
            ke_scr[d] = (k * jnp.exp(rest)).astype(BF16)
        dq_scr[...] = jnp.zeros_like(dq_scr)
        dv_scr[...] = jnp.zeros_like(dv_scr)

        zero = jnp.zeros((HEAD, HEAD), F32)

        def bwd_chunk(i, carry, u):
            new = []
            for (d, rev, _, _), dSt in zip(dirs, carry):
                ci = _scan_chunk(i, ncc, nc, rev)
                rows = _rows(ci)
                q, v, do = q_ref[rows, :], v_ref[rows, :], do_scr[rows, :]
                c, k = c_scr[d, rows, :], k_scr[d, rows, :]
                tot = _chunk_total(c, rev)
                etot = jnp.exp(tot)
                St = st_ref[0, d, ci]
                dSb = dSt.astype(BF16)
                do_b = do.astype(BF16)
                dq_x = _dot(do_b, St) * jnp.exp(c)
                dk_x = _dot(v.astype(BF16), dSb) * jnp.exp(tot - c)
                dv_x = _dot_nt(ke_scr[d, rows, :], dSb)
                dtot = (jnp.sum(St.astype(F32) * dSt, axis=0, keepdims=True) * etot
                        + jnp.sum(k * dk_x, axis=0, keepdims=True))
                dq = jnp.zeros((CHUNK, HEAD), F32)
                for s in range(CHUNK):
                    E = _pair_decay(c, s, rev)
                    XE = E * k[s:s + 1, :]
                    a = jnp.sum(q * XE, axis=1, keepdims=True)
                    da = jnp.sum(do * v[s:s + 1, :], axis=1, keepdims=True)
                    dq = dq + da * XE
                    row_scr[u, d, 0, s:s + 1, :] = jnp.sum(da * q * E, axis=0, keepdims=True)
                    row_scr[u, d, 1, s:s + 1, :] = jnp.sum(a * do, axis=0, keepdims=True)
                dq, dk, dv = dq + dq_x, row_scr[u, d, 0] + dk_x, row_scr[u, d, 1] + dv_x
                dg_scr[d, rows, :] = _chunk_cumsum(q * dq - k * dk, not rev) + dtot
                dk_scr[d, rows, :] = dk
                dq_scr[rows, :] += dq
                dv_scr[rows, :] += dv
                new.append(dSt * etot + _dot_tn(do_b, qe_scr[d, rows, :]))
            return tuple(new)

        def bwd_step(i2, carry):
            for u in range(2):
                carry = bwd_chunk(nc - 1 - (2 * i2 + u), carry, u)
            return carry

        lax.fori_loop(0, nc // 2, bwd_step, (zero, zero))

        for d, _, z_ref, dz_ref in dirs:
            lb = _lower_bound(lbl_ref, d)
            sg, f = _gate_terms(z_ref[...], lb)
            df = dg_scr[d] / f - dk_scr[d]
            dz_ref[...] = (df * (1.0 - lb) * sg * (1.0 - sg)).astype(BF16)
            dl0 = jnp.sum(df * (1.0 - sg), axis=0, keepdims=True) * lb * (1.0 - lb)
            dlbl_ref[d, 0:1, :] = dl0
            dlbl_ref[d, 1:2, :] = -dl0
        dq_ref[...] = dq_scr[...].astype(BF16)
        dv_ref[...] = dv_scr[...].astype(BF16)

    cb = HA // HEAD
    col = lambda kk: pl.BlockSpec((T, HEAD), lambda h: (0, kk * cb + h))
    tcol = pl.BlockSpec((T, HEAD), lambda h: (0, h))
    lcol = pl.BlockSpec((L, HEAD), lambda h: (0, h))
    outs = _pcall(
        body, name="hgrn_bwd", grid=(nh,),
        in_specs=[col(0), col(1), col(2), col(3), col(4),
                  pl.BlockSpec((2, 2, HEAD), lambda h: (0, 0, h)), pl.BlockSpec((1, HEAD), lambda h: (0, 0)),
                  lcol, lcol,
                  pl.BlockSpec((1, 2, nc, HEAD, HEAD), lambda h: (h, 0, 0, 0, 0), pipeline_mode=pl.Buffered(1))],
        out_specs=[tcol, tcol, tcol, tcol, tcol, pl.BlockSpec((2, 2, HEAD), lambda h: (0, 0, h)),
                   pl.BlockSpec((8, HEAD), lambda h: (0, 0))],
        out_shape=[jax.ShapeDtypeStruct((T, HA), BF16)] * 5 + [jax.ShapeDtypeStruct((2, 2, HA), F32),
                                                               jax.ShapeDtypeStruct((8, HEAD), F32)],
        scratch_shapes=[pltpu.VMEM((T, HEAD), F32),
                        pltpu.VMEM((2, T, HEAD), F32), pltpu.VMEM((2, T, HEAD), F32),
                        pltpu.VMEM((2, T, HEAD), BF16), pltpu.VMEM((2, T, HEAD), BF16),
                        pltpu.VMEM((2, T, HEAD), F32), pltpu.VMEM((2, T, HEAD), F32),
                        pltpu.VMEM((T, HEAD), F32), pltpu.VMEM((T, HEAD), F32),
                        pltpu.VMEM((2, 2, 2, CHUNK, HEAD), F32)],
        compiler_params=_params(("arbitrary",)),
    )(p, p, p, p, p, lbl, ng, o, dya, st)
    return outs


def _swap_halves(t, lane):
    q = HEAD // 4
    return jnp.where((lane % (2 * q)) < q, pltpu.roll(t, HEAD - q, 1), pltpu.roll(t, q, 1))


def _qk_norm(t, g):
    r = lax.rsqrt(jnp.mean(t * t, axis=-1, keepdims=True) + EPS)
    return t * r, r


def _rope(t, cos, sin, lane):
    return t * cos + _swap_halves(t, lane) * sin


def _qk_norm_bwd(dy, th, r, g):
    dth = dy * g
    return r * (dth - th * jnp.mean(dth * th, axis=-1, keepdims=True)), jnp.sum(dy * th, axis=0, keepdims=True)


def _rope_bwd(dy, cos, sin, lane):
    return dy * cos + _swap_halves(dy * sin, lane)


def _na_geometry(L):
    n_rows = L // GRID_W
    kr = min(WIN_R, n_rows)
    return n_rows, kr


def _na_prep(q_ref, k_ref, v_ref, gq_ref, gk_ref, cos_ref, sin_ref, qs, ks, vs, n_ctx, L):
    lane = lax.broadcasted_iota(jnp.int32, (L, HEAD), 1)
    cos, sin = cos_ref[...], sin_ref[...]
    qh, _ = _qk_norm(q_ref[pl.ds(n_ctx, L), :], None)
    qs[...] = _rope(qh * gq_ref[...], cos, sin, lane).astype(BF16)
    kh, _ = _qk_norm(k_ref[pl.ds(n_ctx, L), :], None)
    ks[pl.ds(n_ctx, L), :] = _rope(kh * gk_ref[...], cos, sin, lane).astype(BF16)
    kc, _ = _qk_norm(k_ref[pl.ds(0, n_ctx), :], None)
    ks[pl.ds(0, n_ctx), :] = (kc * gk_ref[...]).astype(BF16)
    vs[...] = v_ref[...].astype(BF16)


NA_RB = 4


def _na_band_rows(kr):
    return kr + NA_RB


def _na_scores(i, qs, ks, bias_ref, n_ctx, n_rows, kr):
    scale = HEAD ** -0.5
    kb = _na_band_rows(kr)
    rq = NA_RB * i
    r0 = jnp.clip(rq - WIN_R // 2, 0, n_rows - kb)
    qrows = pl.ds(pl.multiple_of(rq * GRID_W, NA_RB * GRID_W), NA_RB * GRID_W)
    krows = pl.ds(pl.multiple_of(n_ctx + r0 * GRID_W, GRID_W), kb * GRID_W)
    qv = qs[qrows, :]
    sb = _dot_nt(qv, ks[krows, :]) * scale
    band_row = lax.broadcasted_iota(jnp.int32, (GRID_W, kb * GRID_W), 1) // GRID_W
    parts, tiles = [], []
    for u in range(NA_RB):
        r_u = rq + u
        first = jnp.clip(r_u - WIN_R // 2, 0, n_rows - kr) - r0
        idx = [jnp.clip(r0 - r_u + (WIN_R - 1) + 2 * jj, 0, 2 * WIN_R - 1) for jj in range(kb // 2)]
        bias_u = jnp.concatenate([bias_ref[0, t] for t in idx], axis=1)
        inside = (band_row >= first) & (band_row < first + kr)
        parts.append(jnp.where(inside, sb[u * GRID_W:(u + 1) * GRID_W, :] + bias_u, NEG))
        tiles.append(idx)
    sb = jnp.concatenate(parts, axis=0)
    sc = _dot_nt(qv, ks[pl.ds(0, n_ctx), :]) * scale
    m = jnp.maximum(jnp.max(sb, axis=1, keepdims=True), jnp.max(sc, axis=1, keepdims=True))
    eb, ec = jnp.exp(sb - m), jnp.exp(sc - m)
    inv = 1.0 / (jnp.sum(eb, axis=1, keepdims=True) + jnp.sum(ec, axis=1, keepdims=True))
    return eb * inv, ec * inv, qrows, krows, tiles


def _na_fwd(p, bias, gq, gk, cos, sin, n_ctx, off, HB):
    T = p.shape[0]
    L = T - n_ctx
    nh = HB // HEAD
    n_rows, kr = _na_geometry(L)
    ob = off // HEAD

    def body(q_ref, k_ref, v_ref, bias_ref, gq_ref, gk_ref, cos_ref, sin_ref, y_ref, qs, ks, vs):
        _na_prep(q_ref, k_ref, v_ref, gq_ref, gk_ref, cos_ref, sin_ref, qs, ks, vs, n_ctx, L)

        def step(i, carry):
            pb, pc, qrows, krows, _ = _na_scores(i, qs, ks, bias_ref, n_ctx, n_rows, kr)
            y = _dot(pb.astype(BF16), vs[krows, :]) + _dot(pc.astype(BF16), vs[pl.ds(0, n_ctx), :])
            y_ref[qrows, :] = y.astype(BF16)
            return carry

        lax.fori_loop(0, n_rows // NA_RB, step, 0)

    col = lambda kk: pl.BlockSpec((T, HEAD), lambda h: (0, ob + kk * nh + h))
    vec = pl.BlockSpec((1, HEAD), lambda h: (0, 0))
    tab = pl.BlockSpec((L, HEAD), lambda h: (0, 0))
    return _pcall(
        body, name="na_fwd", grid=(nh,),
        in_specs=[col(0), col(1), col(2), pl.BlockSpec((1,) + bias.shape[1:], lambda h: (h, 0, 0, 0)),
                  vec, vec, tab, tab],
        out_specs=pl.BlockSpec((L, HEAD), lambda h: (0, h)),
        out_shape=jax.ShapeDtypeStruct((L, HB), BF16),
        scratch_shapes=[pltpu.VMEM((L, HEAD), BF16), pltpu.VMEM((T, HEAD), BF16), pltpu.VMEM((T, HEAD), BF16)],
        compiler_params=_params(("parallel",)),
    )(p, p, p, bias, gq, gk, cos, sin)


def _na_bwd(p, bias, gq, gk, cos, sin, dyb, n_ctx, off, HB):
    T = p.shape[0]
    L = T - n_ctx
    nh = HB // HEAD
    n_rows, kr = _na_geometry(L)
    ob = off // HEAD
    scale = HEAD ** -0.5

    def body(q_ref, k_ref, v_ref, bias_ref, gq_ref, gk_ref, cos_ref, sin_ref, dy_ref,
             dq_ref, dk_ref, dv_ref, dbias_ref, dg_ref, qs, ks, vs, dqa, dka, dva):
        h = pl.program_id(0)
        _na_prep(q_ref, k_ref, v_ref, gq_ref, gk_ref, cos_ref, sin_ref, qs, ks, vs, n_ctx, L)
        dka[...] = jnp.zeros_like(dka)
        dva[...] = jnp.zeros_like(dva)
        dbias_ref[...] = jnp.zeros_like(dbias_ref)

        crows = pl.ds(0, n_ctx)

        def step(i, carry):
            pb, pc, qrows, krows, tiles = _na_scores(i, qs, ks, bias_ref, n_ctx, n_rows, kr)
            do = dy_ref[qrows, :]
            qv = qs[qrows, :]
            dpb = _dot_nt(do, vs[krows, :])
            dpc = _dot_nt(do, vs[crows, :])
            delta = jnp.sum(pb * dpb, axis=1, keepdims=True) + jnp.sum(pc * dpc, axis=1, keepdims=True)
            dsb = pb * (dpb - delta)
            dsc = pc * (dpc - delta)
            dsb_b, dsc_b = dsb.astype(BF16), dsc.astype(BF16)
            dqa[qrows, :] = (_dot(dsb_b, ks[krows, :]) + _dot(dsc_b, ks[crows, :])) * scale
            dka[krows, :] += _dot_tn(dsb_b, qv) * scale
            dka[crows, :] += _dot_tn(dsc_b, qv) * scale
            dva[krows, :] += _dot_tn(pb.astype(BF16), do)
            dva[crows, :] += _dot_tn(pc.astype(BF16), do)
            for u, idx in enumerate(tiles):
                for jj, t in enumerate(idx):
                    dbias_ref[0, t] += dsb[u * GRID_W:(u + 1) * GRID_W, jj * 2 * GRID_W:(jj + 1) * 2 * GRID_W]
            return carry

        lax.fori_loop(0, n_rows // NA_RB, step, 0)

        lane = lax.broadcasted_iota(jnp.int32, (L, HEAD), 1)
        cos, sin = cos_ref[...], sin_ref[...]
        lat, ctx = pl.ds(n_ctx, L), pl.ds(0, n_ctx)
        gqv, gkv = gq_ref[...], gk_ref[...]
        qh, rq = _qk_norm(q_ref[lat, :], None)
        dq, dgq = _qk_norm_bwd(_rope_bwd(dqa[...], cos, sin, lane), qh, rq, gqv)
        dq_ref[ctx, :] = jnp.zeros((n_ctx, HEAD), BF16)
        dq_ref[lat, :] = dq.astype(BF16)
        kh, rk = _qk_norm(k_ref[lat, :], None)
        dk, dgk = _qk_norm_bwd(_rope_bwd(dka[lat, :], cos, sin, lane), kh, rk, gkv)
        dk_ref[lat, :] = dk.astype(BF16)
        kch, rkc = _qk_norm(k_ref[ctx, :], None)
        dkc, dgkc = _qk_norm_bwd(dka[ctx, :], kch, rkc, gkv)
        dk_ref[ctx, :] = dkc.astype(BF16)
        dv_ref[...] = dva[...].astype(BF16)

        @pl.when(h == 0)
        def _():
            dg_ref[...] = jnp.zeros_like(dg_ref)

        dg_ref[0:1, :] += dgq
        dg_ref[1:2, :] += dgk + dgkc

    col = lambda kk: pl.BlockSpec((T, HEAD), lambda h: (0, ob + kk * nh + h))
    vec = pl.BlockSpec((1, HEAD), lambda h: (0, 0))
    tab = pl.BlockSpec((L, HEAD), lambda h: (0, 0))
    tcol = pl.BlockSpec((T, HEAD), lambda h: (0, h))
    bspec = pl.BlockSpec((1,) + bias.shape[1:], lambda h: (h, 0, 0, 0))
    return _pcall(
        body, name="na_bwd", grid=(nh,),
        in_specs=[col(0), col(1), col(2), bspec, vec, vec, tab, tab, pl.BlockSpec((L, HEAD), lambda h: (0, h))],
        out_specs=[tcol, tcol, tcol, bspec, pl.BlockSpec((8, HEAD), lambda h: (0, 0))],
        out_shape=[jax.ShapeDtypeStruct((T, HB), BF16)] * 3 + [jax.ShapeDtypeStruct(bias.shape, F32),
                                                               jax.ShapeDtypeStruct((8, HEAD), F32)],
        scratch_shapes=[pltpu.VMEM((L, HEAD), BF16), pltpu.VMEM((T, HEAD), BF16), pltpu.VMEM((T, HEAD), BF16),
                        pltpu.VMEM((L, HEAD), F32), pltpu.VMEM((T, HEAD), F32), pltpu.VMEM((T, HEAD), F32)],
        compiler_params=_params(("arbitrary",)),
    )(p, p, p, bias, gq, gk, cos, sin, dyb)


def _bias_tables():
    w = np.arange(GRID_W)
    col_start = np.clip(w - WIN_C // 2, 0, GRID_W - WIN_C)
    col_in = (w[None, :] >= col_start[:, None]) & (w[None, :] < col_start[:, None] + WIN_C)
    dc = np.clip(w[None, :] - w[:, None], -(WIN_C - 1), WIN_C - 1) + WIN_C - 1
    n_pair = 2 * WIN_R
    ridx = np.zeros((n_pair, GRID_W, 2 * GRID_W), np.int32)
    cidx = np.zeros((n_pair, GRID_W, 2 * GRID_W), np.int32)
    valid = np.zeros((n_pair, GRID_W, 2 * GRID_W), bool)
    for i in range(n_pair):
        for half in range(2):
            row = i + half
            sl = slice(half * GRID_W, (half + 1) * GRID_W)
            ridx[i, :, sl] = min(row, 2 * WIN_R - 2)
            cidx[i, :, sl] = dc
            valid[i, :, sl] = col_in & (row <= 2 * WIN_R - 2)
    return ridx, cidx, valid


def _bias_onehot():
    _, cidx, valid = _bias_tables()
    K = GRID_W * 2 * GRID_W
    oh = np.zeros((K, 128), np.float32)
    neg = np.full((1, K), NEG, np.float32)
    for cq in range(GRID_W):
        for ll in range(2 * GRID_W):
            if valid[0, cq, ll]:
                oh[cq * 2 * GRID_W + ll, (ll // GRID_W) * 64 + cidx[0, cq, ll]] = 1.0
                neg[0, cq * 2 * GRID_W + ll] = 0.0
    return oh, neg


def _expand_bias(table):
    H = table.shape[0]
    n_pair, n_dc = 2 * WIN_R, 2 * WIN_C - 1
    tp = jnp.pad(table, ((0, 0), (0, n_pair + 1 - table.shape[1]), (0, 64 - n_dc)))
    t2 = jnp.concatenate([tp[:, :n_pair], tp[:, 1:n_pair + 1]], axis=-1).reshape(H * n_pair, 128)
    oh, neg = _bias_onehot()

    def body(t_ref, oh_ref, neg_ref, o_ref):
        o_ref[...] = lax.dot_general(t_ref[...], oh_ref[...], (((1,), (1,)), ((), ())), precision=HI,
                                     preferred_element_type=F32) + neg_ref[...]

    out = _pcall(body, name="bias_expand", out_shape=jax.ShapeDtypeStruct((H * n_pair, oh.shape[0]), F32),
                         compiler_params=_params())(t2, jnp.asarray(oh), jnp.asarray(neg))
    return out.reshape(H, n_pair, GRID_W, 2 * GRID_W)


def _bias_grad(dbias):
    H = dbias.shape[0]
    n_pair, n_dc = 2 * WIN_R, 2 * WIN_C - 1
    K = GRID_W * 2 * GRID_W
    oh, _ = _bias_onehot()
    flat = dbias.reshape(H * n_pair, K)

    def body(d_ref, oh_ref, o_ref):
        o_ref[...] = jnp.dot(d_ref[...], oh_ref[...], precision=HI, preferred_element_type=F32)

    g = _pcall(body, name="bias_grad", out_shape=jax.ShapeDtypeStruct((H * n_pair, 128), F32),
                       compiler_params=_params())(flat, jnp.asarray(oh))
    g = g.reshape(H, n_pair, 128)
    left, right = g[:, :, :n_dc], g[:, :, 64:64 + n_dc]
    out = left[:, :n_pair - 1]
    return out.at[:, 1:].add(right[:, :n_pair - 2])


def _rope_tables(L):
    pos = np.arange(L)
    row = (pos // GRID_W).astype(np.float32)
    colp = (pos % GRID_W).astype(np.float32)
    half = HEAD // 2
    nf = half // 2
    inv = (ROPE_THETA ** (-np.arange(nf, dtype=np.float32) / nf)).astype(np.float32)

    def tabs(pv):
        ang = pv[:, None] * inv[None, :]
        c, s = np.cos(ang), np.sin(ang)
        return np.concatenate([c, c], axis=1), np.concatenate([-s, s], axis=1)

    cr, sr = tabs(row)
    cc, sc = tabs(colp)
    return (jnp.asarray(np.concatenate([cr, cc], axis=1), F32), jnp.asarray(np.concatenate([sr, sc], axis=1), F32))


def _adamw(w, g, m, v, name, after=None, copy_g=False):
    R, C = w.shape
    tr = _row_tile(R, C)
    c1 = 1.0 - ADAM_B1 ** ADAM_STEP
    c2 = 1.0 - ADAM_B2 ** ADAM_STEP
    deps = [] if after is None else (list(after) if isinstance(after, (tuple, list)) else [after])
    n_out = 4 if copy_g else 3

    def body(w_ref, g_ref, m_ref, v_ref, *rest):
        d_ref, mo_ref, vo_ref = rest[len(deps):len(deps) + 3]
        gv = g_ref[...]
        mn = ADAM_B1 * m_ref[...] + (1.0 - ADAM_B1) * gv
        vn = ADAM_B2 * v_ref[...] + (1.0 - ADAM_B2) * (gv * gv)
        mo_ref[...] = mn
        vo_ref[...] = vn
        d_ref[...] = -ADAM_LR * ((mn / c1) / (jnp.sqrt(vn / c2) + ADAM_EPS) + ADAM_WD * w_ref[...])
        if copy_g:
            rest[-1][...] = gv

    blk = pl.BlockSpec((tr, C), lambda i: (i, 0))
    return _pcall(
        body, name=name, grid=(R // tr,),
        in_specs=[blk] * 4 + [_ANY] * len(deps), out_specs=[blk] * n_out,
        out_shape=[jax.ShapeDtypeStruct((R, C), F32)] * n_out,
        compiler_params=_params(("parallel",)),
    )(w, g, m, v, *deps)


PACK_W = 1024


def _pack(parts):
    flat, offs, pos = [], [], 0
    for a in parts:
        n = a.size
        padn = -n % PACK_W
        flat.append(jnp.pad(a.reshape(-1).astype(F32), (0, padn)))
        offs.append((pos, n, a.shape))
        pos += n + padn
    tail = -pos % (8 * PACK_W)
    if tail:
        flat.append(jnp.zeros((tail,), F32))
    return jnp.concatenate(flat).reshape(-1, PACK_W), offs


def _unpack(buf, offs, i):
    pos, n, shape = offs[i]
    return buf.reshape(buf.shape[:-2] + (-1,))[..., pos:pos + n].reshape(buf.shape[:-2] + shape)


def kernel(x, c, ctx, c_ctx, ada_w, ada_b, norm1_g, norm2_g, w_in, hgrn_lb_logits, hgrn_norm_g, na_q_norm_g, na_k_norm_g, na_rel_bias, w_branch_a, w_branch_b, w_out, ffn_w1, ffn_w3, ffn_conv_w, ffn_conv_b, ffn_w2, loss_target, m_c_ctx, m_ada_w, m_ada_b, m_norm1_g, m_norm2_g, m_w_in, m_hgrn_lb_logits, m_hgrn_norm_g, m_na_q_norm_g, m_na_k_norm_g, m_na_rel_bias, m_w_branch_a, m_w_branch_b, m_w_out, m_ffn_w1, m_ffn_w3, m_ffn_conv_w, m_ffn_conv_b, m_ffn_w2, v_c_ctx, v_ada_w, v_ada_b, v_norm1_g, v_norm2_g, v_w_in, v_hgrn_lb_logits, v_hgrn_norm_g, v_na_q_norm_g, v_na_k_norm_g, v_na_rel_bias, v_w_branch_a, v_w_branch_b, v_w_out, v_ffn_w1, v_ffn_w3, v_ffn_conv_w, v_ffn_conv_b, v_ffn_w2):
    weights = dict(c_ctx=c_ctx, ada_w=ada_w, ada_b=ada_b, norm1_g=norm1_g, norm2_g=norm2_g, w_in=w_in,
                   hgrn_lb_logits=hgrn_lb_logits, hgrn_norm_g=hgrn_norm_g, na_q_norm_g=na_q_norm_g,
                   na_k_norm_g=na_k_norm_g, na_rel_bias=na_rel_bias, w_branch_a=w_branch_a, w_branch_b=w_branch_b,
                   w_out=w_out, ffn_w1=ffn_w1, ffn_w3=ffn_w3, ffn_conv_w=ffn_conv_w, ffn_conv_b=ffn_conv_b,
                   ffn_w2=ffn_w2)
    moms = dict(c_ctx=(m_c_ctx, v_c_ctx), ada_w=(m_ada_w, v_ada_w), ada_b=(m_ada_b, v_ada_b),
                norm1_g=(m_norm1_g, v_norm1_g), norm2_g=(m_norm2_g, v_norm2_g), w_in=(m_w_in, v_w_in),
                hgrn_lb_logits=(m_hgrn_lb_logits, v_hgrn_lb_logits), hgrn_norm_g=(m_hgrn_norm_g, v_hgrn_norm_g),
                na_q_norm_g=(m_na_q_norm_g, v_na_q_norm_g), na_k_norm_g=(m_na_k_norm_g, v_na_k_norm_g),
                na_rel_bias=(m_na_rel_bias, v_na_rel_bias), w_branch_a=(m_w_branch_a, v_w_branch_a),
                w_branch_b=(m_w_branch_b, v_w_branch_b), w_out=(m_w_out, v_w_out), ffn_w1=(m_ffn_w1, v_ffn_w1),
                ffn_w3=(m_ffn_w3, v_ffn_w3), ffn_conv_w=(m_ffn_conv_w, v_ffn_conv_w),
                ffn_conv_b=(m_ffn_conv_b, v_ffn_conv_b), ffn_w2=(m_ffn_w2, v_ffn_w2))
    order = list(weights)

    L, D = x.shape[1], x.shape[2]
    N = ctx.shape[1]
    T = N + L
    HA = w_branch_a.shape[1]
    HB = w_branch_b.shape[1]
    F = ffn_conv_b.shape[1]
    IN = 5 * HA + 3 * HB + 2 * D
    n_ada = ada_w.shape[2]
    ix, iy, ic = _pos()
    chip = 2 * ix + iy
    dev = 2 * chip + ic

    _PENDING.clear()
    pk0, offs0 = _pack([c[0], hgrn_lb_logits, ffn_conv_w[0]])
    g0 = _allgather8(pk0, "gather_small0")
    c_all = _unpack(g0, offs0, 0)
    lbl_parts = _unpack(g0, offs0, 1)
    lbl = jnp.concatenate([lbl_parts[2 * j] for j in range(N_CHIP)], axis=-1)
    cw_parts = _unpack(g0, offs0, 2)
    cw = jnp.concatenate([cw_parts[2 * j] for j in range(N_CHIP)], axis=-1)
    cw8 = jnp.pad(cw, ((0, 5), (0, 0)))

    cs = jnp.concatenate([c_all, c_ctx[None, :], jnp.zeros((7, D), F32)], axis=0)
    ada_b_mine = lax.dynamic_slice(ada_b, (0, chip * n_ada), (1, n_ada))
    mod_mine = _ada_fwd(cs, ada_w[0], ada_b_mine)
    gm = _allgather8(mod_mine, "gather_mod")
    mod = jnp.concatenate([gm[2 * j] for j in range(N_CHIP)], axis=-1)
    mod_l = lax.dynamic_slice(mod, (dev, 0), (1, N_MOD * D)).reshape(N_MOD, D)
    mod_c = mod[8].reshape(N_MOD, D)
    sh1, sc1, g1, sh2, sc2, g2 = [mod_l[i:i + 1] for i in range(N_MOD)]
    shift1 = jnp.concatenate([mod_c[0:1], sh1], axis=0)
    scale1 = jnp.concatenate([mod_c[1:2], sc1], axis=0)

    shards = [w_in[0], w_branch_a[0], w_branch_b[0], w_out[0], ffn_w1[0], ffn_w3[0], ffn_w2[0]]
    names = ["w_in", "w_a", "w_b", "w_out", "w1", "w3", "w2"]
    slots = [_cast_bf16_slot(s, "cast_" + nm) for s, nm in zip(shards, names)]
    sem_nb, win_buf = _xfer_start("gather_ici_start_in_nbr", slots[0:1], _plan_gather_ici(NEIGHBOURS), 2, gm)

    xall = jnp.concatenate([ctx[0], x[0]], axis=0)
    h_all = _rms1_fwd(xall, norm1_g, shift1, scale1, N)
    chip_i = chip.astype(jnp.int32)
    same = lambda ids: jnp.stack([jnp.stack(ids), jnp.stack(ids)])
    p = _mm_nn_sel(h_all, win_buf[0], same([chip_i]), F32, "mm_p_own")
    bias = _expand_bias(na_rel_bias[0])
    win_buf = _xfer_wait("gather_ici_wait_in_nbr", sem_nb, win_buf, _plan_gather_ici(NEIGHBOURS),
                         (p, bias, *slots[1:]))
    sem_nb, win_buf = _xfer_start("gather_d2d_start_in_nbr", win_buf, _plan_gather_d2d(NEIGHBOURS), 2)
    sem_dg, win_buf = _xfer_start("gather_ici_start_in_diag", win_buf, _plan_gather_ici(DIAGONAL), 1)
    gat_mix = _gather_start("mix", slots[1:4])
    gat_ffn = _gather_start("ffn", slots[4:6])
    gat_ffn2 = _gather_start("ffn2", slots[6:7])
    win_buf = _xfer_wait("gather_d2d_wait_in_nbr", sem_nb, win_buf, _plan_gather_d2d(NEIGHBOURS), _PENDING[0])
    p = _mm_nn_sel(h_all, win_buf[0], same([chip_i ^ 1, chip_i ^ 2]), F32, "mm_p_nbr", p)
    win_buf = _d2d_hand_over("in_diag", sem_dg, win_buf, DIAGONAL, p)
    p = _mm_nn_sel(h_all, win_buf[0], same([chip_i ^ 3]), F32, "mm_p_diag", p)
    Win = win_buf[0]
    cos, sin = _rope_tables(L)
    off_na = 5 * HA
    y_b = _na_fwd(p, bias, na_q_norm_g, na_k_norm_g, cos, sin, N, off_na, HB)
    gat_mix = _gather_mid(gat_mix, y_b)
    y_a, o_a, st_a = _hgrn_fwd(p, lbl, hgrn_norm_g, N, HA)
    Wa, Wb, Wo = _gather_finish(gat_mix, (y_a, y_b))
    Wo = Wo.reshape(1, D, D)
    za = _mm_nn(y_a, Wa, BF16, "mm_za")
    zb = _mm_nn(y_b, Wb, BF16, "mm_zb")
    off_ga, off_gb = 5 * HA + 3 * HB, 5 * HA + 3 * HB + D
    z = _merge_fwd(za, zb, p, N, off_ga, off_gb)
    gat_ffn = _gather_mid(gat_ffn, z)
    mo = _mm_nn(z, Wo, F32, "mm_mo")
    vec2 = jnp.concatenate([g1, norm2_g, sh2, sc2, jnp.zeros((4, D), F32)], axis=0)
    x_mid, h2 = _resid_rms2_fwd(x[0], mo, vec2)
    W1, W3 = _gather_finish(gat_ffn, h2)
    gat_ffn2 = _gather_mid(gat_ffn2, h2)
    u1 = _mm_nn(h2, W1, BF16, "mm_u1")
    u3 = _mm_nn(h2, W3, BF16, "mm_u3")
    (W2,) = _gather_finish(gat_ffn2, (u1, u3))
    W2 = W2.reshape(1, F, D)
    a = _convgate_fwd(u1, u3, cw8, ffn_conv_b)
    f = _mm_nn(a, W2, F32, "mm_f")
    dy, df, s_loss = _loss_head(x_mid, f, g2, loss_target[0])
    loss = lax.psum(s_loss[1, 0], ("x", "y", "c"))
    d_g2 = s_loss[0:1]

    gW2 = _mm_tn(a, df, 1, "mm_gw2").reshape(N_CHIP, F // N_CHIP, D)
    da = _mm_nt(df, W2, BF16, "mm_da")
    du1, du3, s_conv = _convgate_bwd(u1, u3, da, cw8, ffn_conv_b)
    gW1 = _mm_tn(h2, du1, N_CHIP, "mm_gw1")
    gW3 = _mm_tn(h2, du3, N_CHIP, "mm_gw3")
    rs_ffn = _rs_start("ffn", [gW2, gW1, gW3])
    dh2a = _mm_nt(du1, W1, F32, "mm_dh2a")
    dh2b = _mm_nt(du3, W3, F32, "mm_dh2b")
    rs_ffn = _rs_scatter(rs_ffn, dh2b)
    dxm, dmo, s_rms2 = _resid_rms2_bwd(x_mid, dh2a, dh2b, dy, mo, vec2)
    gWo = _mm_tn(z, dmo, 1, "mm_gwo").reshape(N_CHIP, D // N_CHIP, D)
    dz = _mm_nt(dmo, Wo, BF16, "mm_dz")
    dza, dzb, dga, dgb = _merge_bwd(dz, za, zb, p, N, off_ga, off_gb)
    gWa = _mm_tn(y_a, dza, N_CHIP, "mm_gwa")
    gWb = _mm_tn(y_b, dzb, N_CHIP, "mm_gwb")
    rs_mix = _rs_start("mix", [gWo, gWa, gWb])
    dya = _mm_nt(dza, Wa, F32, "mm_dya")
    dyb = _mm_nt(dzb, Wb, BF16, "mm_dyb")
    rs_mix = _rs_scatter(rs_mix, dyb)
    dq_a, dzf, dzbk, di_a, dog, dlbl, s_ng = _hgrn_bwd(p, lbl, hgrn_norm_g, o_a, dya, st_a, N, HA)
    rs_ffn = _rs_join(rs_ffn, dq_a)
    dq_n, dk_n, dv_n, dbias, s_qk = _na_bwd(p, bias, na_q_norm_g, na_k_norm_g, cos, sin, dyb, N, off_na, HB)
    rs_mix = _rs_join(rs_mix, dq_n)
    dp = jnp.concatenate([dq_a, dzf, dzbk, di_a, dog, dq_n, dk_n, dv_n, dga, dgb], axis=1)
    gWin = _mm_tn(h_all, dp, N_CHIP, "mm_gwin")
    rs_in = _rs_start("in", [gWin])
    rs_in = _rs_scatter(rs_in, _PENDING[0])
    dh = _mm_nt(dp, Win, F32, "mm_dh")
    grad_x, s_rms1 = _rms1_bwd(xall, dh, dxm, norm1_g, scale1, N)
    d_table = _bias_grad(dbias)

    grads = {}
    big_names = ["ada_w", "w_in", "w_branch_a", "w_branch_b", "w_out", "ffn_w1", "ffn_w3", "ffn_w2"]
    small_names = [n for n in order if n not in big_names]
    delta, new_m, new_v = {}, {}, {}

    def update(nm, after=None):
        reduced = nm != "ada_w"
        d_, m_, v_, *g_ = _adamw(weights[nm][0], grads[nm][0], moms[nm][0][0], moms[nm][1][0], "adamw_" + nm,
                                 after, copy_g=reduced)
        delta[nm], new_m[nm], new_v[nm] = d_[None], m_[None], v_[None]
        if reduced:
            grads[nm] = g_[0][None]
        return d_

    last = grad_x
    for nm, g in zip(["ffn_w2", "ffn_w1", "ffn_w3"], _rs_finish(rs_ffn, last)):
        grads[nm] = g[None]
        last = update(nm, last)
    for nm, g in zip(["w_out", "w_branch_a", "w_branch_b"], _rs_finish(rs_mix, last)):
        grads[nm] = g[None]
        last = update(nm, last)
    rs_in = _rs_join(rs_in, last)

    zD = jnp.zeros((1, D), F32)
    dmod_l = jnp.concatenate([s_rms1[2:3], s_rms1[3:4], s_rms2[3:4], s_rms2[0:1], s_rms2[1:2], d_g2], axis=0)
    dmod_c = jnp.concatenate([s_rms1[0:1], s_rms1[1:2], zD, zD, zD, zD], axis=0)
    pk1, offs1 = _pack([dmod_l, dmod_c, s_rms1[4], s_rms2[2], dlbl, s_ng[0], s_qk[0], s_qk[1], d_table,
                        s_conv[0:3], s_conv[3]])
    g1all = _allgather8(pk1, "gather_small1")
    tot1 = _sum8(g1all, "sum_small1")
    dmod_rows = _unpack(g1all, offs1, 0).reshape(N_DEV, N_MOD * D)
    dmod_c_tot = _unpack(tot1, offs1, 1).reshape(1, N_MOD * D)
    dmod16 = jnp.concatenate([dmod_rows, dmod_c_tot, jnp.zeros((7, N_MOD * D), F32)], axis=0)
    dmod16_mine = lax.dynamic_slice(dmod16, (0, chip * n_ada), (16, n_ada))
    g_ada_w, dact = _ada_bwd(cs, ada_w[0], dmod16_mine)
    pk2, offs2 = _pack([dact[8]])
    g2all = _allgather8(pk2, "gather_small2")
    dact_rows = _unpack(g2all, offs2, 0)
    dact_sel = jnp.concatenate([dact_rows[2 * j][None] for j in range(N_CHIP)] + [jnp.zeros((4, D), F32)], axis=0)

    grads["ada_w"] = g_ada_w[None]
    grads["ada_b"] =(_unpack(tot1, offs1, 0) + _unpack(tot1, offs1, 1)).reshape(1, N_MOD * D)
    grads["norm1_g"] = _unpack(tot1, offs1, 2)[None]
    grads["norm2_g"] = _unpack(tot1, offs1, 3)[None]
    g_lbl = _unpack(tot1, offs1, 4)
    n_lb = HA // N_CHIP
    grads["hgrn_lb_logits"] = lax.dynamic_slice(g_lbl, (0, 0, chip * n_lb), (2, 2, n_lb))
    grads["hgrn_norm_g"] = _unpack(tot1, offs1, 5)[None]
    grads["na_q_norm_g"] = _unpack(tot1, offs1, 6)[None]
    grads["na_k_norm_g"] = _unpack(tot1, offs1, 7)[None]
    grads["na_rel_bias"] = _unpack(tot1, offs1, 8)[None]
    g_cw = _unpack(tot1, offs1, 9)
    n_f = F // N_CHIP
    grads["ffn_conv_w"] = lax.dynamic_slice(g_cw, (0, chip * n_f), (3, n_f))[None]
    grads["ffn_conv_b"] = _unpack(tot1, offs1, 10)[None]

    g_c_ctx = _dsilu_rows(dact_sel, c_ctx[None, :], "grad_c_ctx")
    grads["c_ctx"] = g_c_ctx[0]

    last = update("ada_w", g_c_ctx)
    pw, offw = _pack([weights[n] for n in small_names])
    pg, _ = _pack([grads[n] for n in small_names])
    pm, _ = _pack([moms[n][0] for n in small_names])
    pv, _ = _pack([moms[n][1] for n in small_names])
    d_, m_, v_ = _adamw(pw, pg, pm, pv, "adamw_small", (last, loss.reshape(1, 1)))
    for i, nm in enumerate(small_names):
        delta[nm], new_m[nm], new_v[nm] = _unpack(d_, offw, i), _unpack(m_, offw, i), _unpack(v_, offw, i)
    grads["w_in"] = _rs_finish(rs_in, d_)[0][None]
    update("w_in")

    return (loss, grad_x[None], *[grads[n] for n in order], *[delta[n] for n in order],
            *[new_m[n] for n in order], *[new_v[n] for n in order])


def _dsilu_rows(v, cv, name):
    D = v.shape[1]

    def body(v_ref, c_ref, o_ref):
        t = c_ref[...]
        s = _sigmoid(t)
        o_ref[...] = (((v_ref[0:1, :] + v_ref[1:2, :]) + v_ref[2:3, :]) + v_ref[3:4, :]) * (s * (1.0 + t * (1.0 - s)))

    return _pcall(body, name=name, out_shape=jax.ShapeDtypeStruct((1, D), F32),
                          compiler_params=_params())(v, cv)
```

```python
import functools

import numpy as np
import jax
import jax.numpy as jnp
from jax import lax
from jax.experimental import pallas as pl
from jax.experimental.pallas import tpu as pltpu

F32 = jnp.float32
BF16 = jnp.bfloat16
MESH = pl.DeviceIdType.MESH

HEAD = 128
GRID_W = 64
WIN_R = 8
WIN_C = 16
ROPE_THETA = 10000.0
EPS = 1e-6
N_MOD = 6
CHUNK = 16
HGRN_UNROLL = 4
ADAM_LR = 0.001
ADAM_B1 = 0.9
ADAM_B2 = 0.999
ADAM_EPS = 1e-08
ADAM_WD = 0.01
ADAM_STEP = 10
NEG = -1e30
VMEM_LIMIT = 56 * 1024 * 1024
N_DEV = 8
N_CHIP = 4
HI = lax.Precision.HIGHEST


def _pick(n, cands):
    for c in cands:
        if n % c == 0:
            return c
    return n


def _row_tile(rows, cols, target_bytes=1 << 20):
    want = max(16, target_bytes // (4 * cols))
    for t in (512, 256, 128, 64, 32, 16, 8):
        if t <= want and rows % t == 0:
            return t
    return rows


def _params(sem=None):
    return pltpu.CompilerParams(dimension_semantics=sem, vmem_limit_bytes=VMEM_LIMIT)


def _dot(a, b):
    return jnp.dot(a, b, preferred_element_type=F32)


def _dot_nt(a, b):
    return lax.dot_general(a, b, (((1,), (1,)), ((), ())), preferred_element_type=F32)


def _dot_tn(a, b):
    return lax.dot_general(a, b, (((0,), (0,)), ((), ())), preferred_element_type=F32)


def _sigmoid(x):
    return 1.0 / (1.0 + jnp.exp(-x))


def _col_tile(n):
    return n if n <= 1536 else _pick(n, (1024, 768, 512, 384, 256, 128))


def _mm_nn(x, w3, out_dtype, name):
    M, K = x.shape
    S, _, n = w3.shape
    tm = _pick(M, ((1024,) if K <= 2048 else ()) + (768, 512, 256, 128, 64))
    tn = _col_tile(n)
    nb = n // tn

    def body(x_ref, w_ref, o_ref):
        o_ref[...] = _dot(x_ref[...].astype(BF16), w_ref[0]).astype(o_ref.dtype)

    return _pcall(
        body, name=name, grid=(M // tm, S * nb),
        in_specs=[pl.BlockSpec((tm, K), lambda i, j: (i, 0)),
                  pl.BlockSpec((1, K, tn), lambda i, j: (j // nb, 0, j % nb))],
        out_specs=pl.BlockSpec((tm, tn), lambda i, j: (i, j)),
        out_shape=jax.ShapeDtypeStruct((M, S * n), out_dtype),
        compiler_params=_params(("parallel", "parallel")),
    )(x, w3)


def _mm_nn_sel(x, w3, sel, out_dtype, name, prev=None):
    M, K = x.shape
    S, _, n = w3.shape
    tm = _pick(M, (768, 512, 256, 128, 64))
    tn = _col_tile(n)
    nb = n // tn
    k = sel.shape[1]

    def body(sel_ref, x_ref, w_ref, *rest):
        rest[-1][...] = _dot(x_ref[...].astype(BF16), w_ref[0]).astype(out_dtype)

    in_specs = [pl.BlockSpec((tm, K), lambda i, j, sel_ref: (i, 0)),
                pl.BlockSpec((1, K, tn), lambda i, j, sel_ref: (sel_ref[0, j // nb], 0, j % nb))]
    operands = [sel, x, w3]
    if prev is not None:
        in_specs.append(_ANY)
        operands.append(prev)
    return pl.pallas_call(
        body, name=name,
        grid_spec=pltpu.PrefetchScalarGridSpec(
            num_scalar_prefetch=1, grid=(M // tm, k * nb), in_specs=in_specs,
            out_specs=pl.BlockSpec((tm, tn), lambda i, j, sel_ref: (i, sel_ref[1, j // nb] * nb + j % nb))),
        out_shape=jax.ShapeDtypeStruct((M, S * n), out_dtype),
        input_output_aliases={} if prev is None else {3: 0},
        compiler_params=_params(("parallel", "parallel")),
    )(*operands)


def _mm_nt(dy, w3, out_dtype, name):
    M = dy.shape[0]
    S, K, n = w3.shape
    tm = _pick(M, (768, 512, 256, 128, 64))
    tk = K if K <= 2048 else _pick(K, (1408, 1024, 512, 256, 128))
    tc = n if n <= 2048 else _col_tile(n)
    nb = n // tc
    nsteps = S * nb

    def body(dy_ref, w_ref, o_ref, acc_ref):
        s = pl.program_id(2)

        @pl.when(s == 0)
        def _():
            acc_ref[...] = jnp.zeros_like(acc_ref)

        acc_ref[...] += _dot_nt(dy_ref[...].astype(BF16), w_ref[0])

        @pl.when(s == nsteps - 1)
        def _():
            o_ref[...] = acc_ref[...].astype(o_ref.dtype)

    return _pcall(
        body, name=name, grid=(M // tm, K // tk, nsteps),
        in_specs=[pl.BlockSpec((tm, tc), lambda i, k, s: (i, s)),
                  pl.BlockSpec((1, tk, tc), lambda i, k, s: (s // nb, k, s % nb))],
        out_specs=pl.BlockSpec((tm, tk), lambda i, k, s: (i, k)),
        out_shape=jax.ShapeDtypeStruct((M, K), out_dtype),
        scratch_shapes=[pltpu.VMEM((tm, tk), F32)],
        compiler_params=_params(("parallel", "parallel", "arbitrary")),
    )(dy, w3)


def _mm_tn(x, dy, S, name):
    M, K = x.shape
    n = dy.shape[1] // S
    tk = _pick(K, (512, 256, 128))
    tn = _col_tile(n)
    nb = n // tn

    def body(x_ref, dy_ref, o_ref):
        o_ref[0] = _dot_tn(x_ref[...].astype(BF16), dy_ref[...].astype(BF16)).astype(BF16)

    return _pcall(
        body, name=name, grid=(S * nb, K // tk),
        in_specs=[pl.BlockSpec((M, tk), lambda j, k: (0, k)),
                  pl.BlockSpec((M, tn), lambda j, k: (0, j))],
        out_specs=pl.BlockSpec((1, tk, tn), lambda j, k: (j // nb, k, j % nb)),
        out_shape=jax.ShapeDtypeStruct((S, K, n), BF16),
        compiler_params=_params(("parallel", "parallel")),
    )(x, dy)


def _chip_index():
    return (2 * lax.axis_index("x") + lax.axis_index("y")).astype(jnp.int32).reshape(1)


def _cast_bf16_slot(w, name):
    R, C = w.shape
    tr = _row_tile(R, C, 2 << 20)

    def body(j_ref, w_ref, o_ref):
        o_ref[0] = w_ref[...].astype(BF16)

    return _pcall(
        body, name=name,
        grid_spec=pltpu.PrefetchScalarGridSpec(
            num_scalar_prefetch=1, grid=(R // tr,),
            in_specs=[pl.BlockSpec((tr, C), lambda i, j_ref: (i, 0))],
            out_specs=pl.BlockSpec((1, tr, C), lambda i, j_ref: (j_ref[0], i, 0))),
        out_shape=jax.ShapeDtypeStruct((N_CHIP, R, C), BF16),
        compiler_params=_params(("parallel",)),
    )(_chip_index(), w)


def _pos():
    return lax.axis_index("x"), lax.axis_index("y"), lax.axis_index("c")


def _other_chips(x, y):
    return [(x, 1 - y), (1 - x, y), (1 - x, 1 - y)]


def _allgather8(v, name):
    R, C = v.shape

    def body(x_ref, out_ref, send_sems, recv_sems, local_sem):
        x, y, c = _pos()
        me, sibling = (x, y, c), (x, y, 1 - c)
        chips = _other_chips(x, y)

        def slot(px, py, pc):
            return out_ref.at[4 * px + 2 * py + pc]

        def copy(k, block, to, src=None):
            return pltpu.make_async_remote_copy(
                src_ref=slot(*block) if src is None else src, dst_ref=slot(*block),
                send_sem=send_sems.at[k], recv_sem=recv_sems.at[k], device_id=to, device_id_type=MESH)

        mine = pltpu.make_async_copy(x_ref, slot(*me), local_sem)
        mine.start()
        first = [copy(0, me, sibling, src=x_ref)]
        first += [copy(1 + j, me, (*chip, c), src=x_ref) for j, chip in enumerate(chips)]
        for cp in first:
            cp.start()
        passed = [copy(4 + j, (*chip, c), sibling) for j, chip in enumerate(chips)]
        for j, chip in enumerate(chips):
            copy(1 + j, (*chip, c), me).wait_recv()
            passed[j].start()
        copy(0, sibling, me).wait_recv()
        for j, chip in enumerate(chips):
            copy(4 + j, (*chip, 1 - c), me).wait_recv()
        for cp in first + passed:
            cp.wait_send()
        mine.wait()

    return _pcall(
        body, name=name,
        out_shape=jax.ShapeDtypeStruct((N_DEV, R, C), v.dtype),
        in_specs=[pl.BlockSpec(memory_space=pltpu.VMEM)],
        out_specs=pl.BlockSpec(memory_space=pltpu.VMEM),
        scratch_shapes=[pltpu.SemaphoreType.DMA((7,)), pltpu.SemaphoreType.DMA((7,)), pltpu.SemaphoreType.DMA],
        compiler_params=pltpu.CompilerParams(vmem_limit_bytes=VMEM_LIMIT),
    )(v)


_HBM = pl.BlockSpec(memory_space=pltpu.HBM)
_SEM = pl.BlockSpec(memory_space=pltpu.SEMAPHORE)
_ANY = pl.BlockSpec(memory_space=pl.ANY)
_EFFECT = pltpu.SideEffectType.DATAFLOW_SIDE_EFFECTING
_PENDING = []


def _pcall(body, **kw):
    def run(*operands):
        if not _PENDING or "in_specs" not in kw:
            return pl.pallas_call(body, **kw)(*operands)
        deps = list(_PENDING)
        n = len(operands)

        def tied(*refs):
            return body(*refs[:n], *refs[n + len(deps):])

        return pl.pallas_call(tied, **{**kw, "in_specs": list(kw["in_specs"]) + [_ANY] * len(deps)})(*operands, *deps)
    return run


def _copies(plan, refs, send_sems, recv_sems):
    return [pltpu.make_async_remote_copy(src_ref=src, dst_ref=dst, send_sem=send_sems.at[k], recv_sem=recv_sems.at[k],
                                         device_id=dev, device_id_type=MESH)
            for k, (src, dst, dev) in enumerate(plan(refs))]


def _xfer_start(name, bufs, plan, n_copies, after=None):
    n = len(bufs)
    deps = list(_PENDING) + ([after] if after is not None else [])
    nd = len(deps)

    def body(*refs):
        for cp in _copies(plan, refs[:n], refs[n + nd], refs[n + nd + 1]):
            cp.start()
        refs[-1][...] = jnp.zeros_like(refs[-1])

    outs = pl.pallas_call(
        body, name=name,
        out_shape=(pltpu.SemaphoreType.DMA((n_copies,)), pltpu.SemaphoreType.DMA((n_copies,)),
                   *[pltpu.HBM(b.shape, b.dtype) for b in bufs], jax.ShapeDtypeStruct((8, 128), F32)),
        in_specs=[_HBM] * n + [_ANY] * nd,
        out_specs=(_SEM, _SEM, *[_HBM] * n, pl.BlockSpec(memory_space=pltpu.VMEM)),
        input_output_aliases={t: 2 + t for t in range(n)},
        compiler_params=pltpu.CompilerParams(has_side_effects=_EFFECT),
    )(*[pltpu.with_memory_space_constraint(b, pltpu.HBM) for b in bufs], *deps)
    _PENDING[:] = [outs[-1]]
    return (outs[0], outs[1]), list(outs[2:2 + n])


def _xfer_wait(name, sems, bufs, plan, after):
    n = len(bufs)
    after = tuple(after) if isinstance(after, (tuple, list)) else (after,)

    def body(*refs):
        cps = _copies(plan, refs[:n], refs[n], refs[n + 1])
        for cp in cps:
            cp.wait_send()
        for cp in cps:
            cp.wait_recv()

    outs = pl.pallas_call(
        body, name=name,
        out_shape=tuple(pltpu.HBM(b.shape, b.dtype) for b in bufs),
        in_specs=[_HBM] * n + [_SEM, _SEM] + [_ANY] * len(after),
        out_specs=tuple([_HBM] * n),
        input_output_aliases={t: t for t in range(n)},
        compiler_params=pltpu.CompilerParams(has_side_effects=_EFFECT),
    )(*bufs, sems[0], sems[1], *after)
    return list(outs)


def _half(ref_rows, hc):
    h = ref_rows // 2
    return pl.ds(hc * h, h)


ALL_CHIPS = (0, 1, 2)
NEIGHBOURS = (0, 1)
DIAGONAL = (2,)


def _plan_gather_ici(which):
    def plan(bufs):
        x, y, c = _pos()
        j = 2 * x + y
        chips = _other_chips(x, y)
        return [(b.at[j, _half(b.shape[1], c)], b.at[j, _half(b.shape[1], c)], (*chips[k], c))
                for b in bufs for k in which]
    return plan


def _plan_gather_d2d(which):
    def plan(bufs):
        x, y, c = _pos()
        chips = _other_chips(x, y)
        out = []
        for b in bufs:
            for k in which:
                blk = b.at[2 * chips[k][0] + chips[k][1], _half(b.shape[1], c)]
                out.append((blk, blk, (x, y, 1 - c)))
        return out
    return plan


def _plan_pair_swap(n):
    def plan(bufs):
        x, y, c = _pos()
        return [(g.at[:, _half(g.shape[1], 1 - c)], land, (x, y, 1 - c)) for g, land in zip(bufs[:n], bufs[n:])]
    return plan


def _plan_chip_scatter(n):
    def plan(bufs):
        x, y, c = _pos()
        return [(p.at[2 * chip[0] + chip[1]], land.at[k], (*chip, c))
                for p, land in zip(bufs[:n], bufs[n:]) for k, chip in enumerate(_other_chips(x, y))]
    return plan


def _plan_pair_join(bufs):
    x, y, c = _pos()
    return [(b.at[_half(b.shape[0], c)], b.at[_half(b.shape[0], c)], (x, y, 1 - c)) for b in bufs]


def _empty_hbm(shape, dtype):
    return pltpu.with_memory_space_constraint(lax.empty(shape, dtype), pltpu.HBM)


def _gather_start(tag, bufs, after=None):
    sems, bufs = _xfer_start(f"gather_ici_start_{tag}", bufs, _plan_gather_ici(ALL_CHIPS), 3 * len(bufs), after)
    return dict(tag=tag, sems=sems, bufs=bufs)


def _gather_mid(st, after):
    tag = st["tag"]
    bufs = _xfer_wait(f"gather_ici_wait_{tag}", st["sems"], st["bufs"], _plan_gather_ici(ALL_CHIPS), after)
    sems, bufs = _xfer_start(f"gather_d2d_start_{tag}", bufs, _plan_gather_d2d(ALL_CHIPS), 3 * len(bufs))
    return dict(tag=tag, sems=sems, bufs=bufs)


def _gather_finish(st, after):
    return _xfer_wait(f"gather_d2d_wait_{st['tag']}", st["sems"], st["bufs"], _plan_gather_d2d(ALL_CHIPS), after)


def _d2d_hand_over(tag, sems, bufs, which, after):
    bufs = _xfer_wait(f"gather_ici_wait_{tag}", sems, bufs, _plan_gather_ici(which), after)
    sems, bufs = _xfer_start(f"gather_d2d_start_{tag}", bufs, _plan_gather_d2d(which), len(which) * len(bufs))
    return _xfer_wait(f"gather_d2d_wait_{tag}", sems, bufs, _plan_gather_d2d(which), after)


def _pair_add(g, r, name):
    S, R, C = g.shape
    h = R // 2
    tr = _row_tile(h, C)
    nb = h // tr

    def body(c_ref, g_ref, r_ref, o_ref):
        o_ref[...] = (g_ref[...].astype(F32) + r_ref[...].astype(F32)).astype(BF16)

    return _pcall(
        body, name=name,
        grid_spec=pltpu.PrefetchScalarGridSpec(
            num_scalar_prefetch=1, grid=(S, nb),
            in_specs=[pl.BlockSpec((1, tr, C), lambda s, i, c_ref: (s, c_ref[0] * nb + i, 0)),
                      pl.BlockSpec((1, tr, C), lambda s, i, c_ref: (s, i, 0))],
            out_specs=pl.BlockSpec((1, tr, C), lambda s, i, c_ref: (s, i, 0))),
        out_shape=jax.ShapeDtypeStruct((S, h, C), BF16),
        compiler_params=_params(("parallel", "parallel")),
    )(lax.axis_index("c").astype(jnp.int32).reshape(1), g, r)


def _chip_sum(p, rb, name):
    S, h, C = p.shape
    tr = _row_tile(h, C)
    nb = h // tr
    jc = jnp.concatenate([_chip_index(), lax.axis_index("c").astype(jnp.int32).reshape(1)])

    def body(jc_ref, p_ref, r_ref, o_ref):
        o_ref[...] = ((p_ref[0].astype(F32) + r_ref[0].astype(F32)) + r_ref[1].astype(F32)) + r_ref[2].astype(F32)

    return _pcall(
        body, name=name,
        grid_spec=pltpu.PrefetchScalarGridSpec(
            num_scalar_prefetch=1, grid=(nb,),
            in_specs=[pl.BlockSpec((1, tr, C), lambda i, jc_ref: (jc_ref[0], i, 0)),
                      pl.BlockSpec((3, tr, C), lambda i, jc_ref: (0, i, 0))],
            out_specs=pl.BlockSpec((tr, C), lambda i, jc_ref: (jc_ref[1] * nb + i, 0))),
        out_shape=jax.ShapeDtypeStruct((2 * h, C), F32),
        compiler_params=_params(("parallel",)),
    )(jc, p, rb)


def _rs_start(tag, gs):
    n = len(gs)
    lands = [_empty_hbm((g.shape[0], g.shape[1] // 2, g.shape[2]), g.dtype) for g in gs]
    sems, bufs = _xfer_start(f"rs_swap_start_{tag}", list(gs) + lands, _plan_pair_swap(n), n)
    return dict(tag=tag, n=n, sems=sems, bufs=bufs)


def _rs_scatter(st, after):
    tag, n = st["tag"], st["n"]
    bufs = _xfer_wait(f"rs_swap_wait_{tag}", st["sems"], st["bufs"], _plan_pair_swap(n), after)
    ps = [_pair_add(g, r, f"rs_pair_add_{tag}{t}") for t, (g, r) in enumerate(zip(bufs[:n], bufs[n:]))]
    lands = [_empty_hbm((3,) + p.shape[1:], p.dtype) for p in ps]
    sems, bufs = _xfer_start(f"rs_scatter_start_{tag}", ps + lands, _plan_chip_scatter(n), 3 * n)
    return dict(tag=tag, n=n, sems=sems, bufs=bufs)


def _rs_join(st, after):
    tag, n = st["tag"], st["n"]
    bufs = _xfer_wait(f"rs_scatter_wait_{tag}", st["sems"], st["bufs"], _plan_chip_scatter(n), after)
    fs = [_chip_sum(p, rb, f"rs_chip_sum_{tag}{t}") for t, (p, rb) in enumerate(zip(bufs[:n], bufs[n:]))]
    sems, bufs = _xfer_start(f"rs_join_start_{tag}", fs, _plan_pair_join, n)
    return dict(tag=tag, n=n, sems=sems, bufs=bufs)


def _rs_finish(st, after):
    return _xfer_wait(f"rs_join_wait_{st['tag']}", st["sems"], st["bufs"], _plan_pair_join, after)


def _sum8(g, name):
    _, R, C = g.shape

    def body(g_ref, o_ref):
        acc = g_ref[0]
        for d in range(1, N_DEV):
            acc = acc + g_ref[d]
        o_ref[...] = acc

    return _pcall(body, name=name, out_shape=jax.ShapeDtypeStruct((R, C), F32),
                          compiler_params=_params())(g)


def _ada_fwd(cs, w, b):
    D, n = w.shape
    tn = _pick(n, (512, 384, 256, 128))

    def body(c_ref, w_ref, b_ref, o_ref):
        cv = c_ref[...]
        a = (cv * _sigmoid(cv)).astype(BF16)
        o_ref[...] = _dot(a, w_ref[...].astype(BF16)) + b_ref[...]

    return _pcall(
        body, name="ada_fwd", grid=(n // tn,),
        in_specs=[pl.BlockSpec((16, D), lambda j: (0, 0)), pl.BlockSpec((D, tn), lambda j: (0, j)),
                  pl.BlockSpec((1, tn), lambda j: (0, j))],
        out_specs=pl.BlockSpec((16, tn), lambda j: (0, j)),
        out_shape=jax.ShapeDtypeStruct((16, n), F32),
        compiler_params=_params(("parallel",)),
    )(cs, w, b)


def _ada_bwd(cs, w, dmod):
    D, n = w.shape
    tn = _pick(n, (512, 384, 256, 128))

    def body(c_ref, w_ref, d_ref, gw_ref, da_ref):
        j = pl.program_id(0)
        cv = c_ref[...]
        a = cv * _sigmoid(cv)
        d = d_ref[...]
        gw_ref[...] = lax.dot_general(a, d, (((0,), (0,)), ((), ())), precision=HI, preferred_element_type=F32)

        @pl.when(j == 0)
        def _():
            da_ref[...] = jnp.zeros_like(da_ref)

        da_ref[...] += _dot_nt(d.astype(BF16), w_ref[...].astype(BF16))

    return _pcall(
        body, name="ada_bwd", grid=(n // tn,),
        in_specs=[pl.BlockSpec((16, D), lambda j: (0, 0)), pl.BlockSpec((D, tn), lambda j: (0, j)),
                  pl.BlockSpec((16, tn), lambda j: (0, j))],
        out_specs=[pl.BlockSpec((D, tn), lambda j: (0, j)), pl.BlockSpec((16, D), lambda j: (0, 0))],
        out_shape=[jax.ShapeDtypeStruct((D, n), F32), jax.ShapeDtypeStruct((16, D), F32)],
        compiler_params=_params(("arbitrary",)),
    )(cs, w, dmod)


def _rms1_fwd(xall, gain, shift2, scale2, n_ctx):
    T, D = xall.shape
    tb = _pick(n_ctx, (256, 128, 64, 32, 16))
    nctx = n_ctx // tb

    def body(x_ref, g_ref, sh_ref, sc_ref, o_ref):
        i = pl.program_id(0)
        xv = x_ref[...]
        r = lax.rsqrt(jnp.mean(xv * xv, axis=-1, keepdims=True) + EPS)
        nrm = xv * r * g_ref[...]
        lat = i >= nctx
        sh = jnp.where(lat, sh_ref[1:2, :], sh_ref[0:1, :])
        sc = jnp.where(lat, sc_ref[1:2, :], sc_ref[0:1, :])
        o_ref[...] = (nrm * (1.0 + sc) + sh).astype(BF16)

    vec = lambda r: pl.BlockSpec((r, D), lambda i: (0, 0))
    return _pcall(
        body, name="rms1_fwd", grid=(T // tb,),
        in_specs=[pl.BlockSpec((tb, D), lambda i: (i, 0)), vec(1), vec(2), vec(2)],
        out_specs=pl.BlockSpec((tb, D), lambda i: (i, 0)),
        out_shape=jax.ShapeDtypeStruct((T, D), BF16),
        compiler_params=_params(("parallel",)),
    )(xall, gain, shift2, scale2)


def _rms1_bwd(xall, dh, dxmid, gain, scale2, n_ctx):
    T, D = xall.shape
    L = T - n_ctx
    tb = _pick(n_ctx, (256, 128, 64, 32, 16))
    nctx = n_ctx // tb

    def body(x_ref, dh_ref, dxm_ref, g_ref, sc_ref, dx_ref, cs_ref):
        i = pl.program_id(0)
        lat = i >= nctx
        xv = x_ref[...]
        r = lax.rsqrt(jnp.mean(xv * xv, axis=-1, keepdims=True) + EPS)
        xh = xv * r
        g = g_ref[...]
        nrm = xh * g
        sc = jnp.where(lat, sc_ref[1:2, :], sc_ref[0:1, :])
        dhv = dh_ref[...]
        dn = dhv * (1.0 + sc)
        dxh = dn * g
        dxv = r * (dxh - xh * jnp.mean(dxh * xh, axis=-1, keepdims=True))
        s_sh = jnp.sum(dhv, axis=0, keepdims=True)
        s_sc = jnp.sum(dhv * nrm, axis=0, keepdims=True)
        s_g = jnp.sum(dn * xh, axis=0, keepdims=True)
        zero = jnp.zeros_like(s_sh)
        rows = lax.broadcasted_iota(jnp.int32, (8, D), 0)
        upd = jnp.where(rows == 0, jnp.where(lat, zero, s_sh),
              jnp.where(rows == 1, jnp.where(lat, zero, s_sc),
              jnp.where(rows == 2, jnp.where(lat, s_sh, zero),
              jnp.where(rows == 3, jnp.where(lat, s_sc, zero),
              jnp.where(rows == 4, s_g, 0.0)))))

        @pl.when(i == 0)
        def _():
            cs_ref[...] = jnp.zeros_like(cs_ref)

        cs_ref[...] += upd

        @pl.when(lat)
        def _():
            dx_ref[...] = dxv + dxm_ref[...]

    lat_blk = lambda i: (jnp.maximum(i - nctx, 0), 0)
    vec = lambda r: pl.BlockSpec((r, D), lambda i: (0, 0))
    return _pcall(
        body, name="rms1_bwd", grid=(T // tb,),
        in_specs=[pl.BlockSpec((tb, D), lambda i: (i, 0)), pl.BlockSpec((tb, D), lambda i: (i, 0)),
                  pl.BlockSpec((tb, D), lat_blk), vec(1), vec(2)],
        out_specs=[pl.BlockSpec((tb, D), lat_blk), vec(8)],
        out_shape=[jax.ShapeDtypeStruct((L, D), F32), jax.ShapeDtypeStruct((8, D), F32)],
        compiler_params=_params(("arbitrary",)),
    )(xall, dh, dxmid, gain, scale2)


def _resid_rms2_fwd(x, mo, vecs):
    L, D = x.shape
    tb = _pick(L, (256, 128, 64))

    def body(x_ref, mo_ref, v_ref, xm_ref, h_ref):
        xm = x_ref[...] + v_ref[0:1, :] * mo_ref[...]
        xm_ref[...] = xm
        r = lax.rsqrt(jnp.mean(xm * xm, axis=-1, keepdims=True) + EPS)
        h_ref[...] = (xm * r * v_ref[1:2, :] * (1.0 + v_ref[3:4, :]) + v_ref[2:3, :]).astype(BF16)

    blk = pl.BlockSpec((tb, D), lambda i: (i, 0))
    return _pcall(
        body, name="resid_rms2_fwd", grid=(L // tb,),
        in_specs=[blk, blk, pl.BlockSpec((8, D), lambda i: (0, 0))],
        out_specs=[blk, blk],
        out_shape=[jax.ShapeDtypeStruct((L, D), F32), jax.ShapeDtypeStruct((L, D), BF16)],
        compiler_params=_params(("parallel",)),
    )(x, mo, vecs)


def _resid_rms2_bwd(xmid, dh_a, dh_b, dy, mo, vecs):
    L, D = xmid.shape
    tb = _pick(L, (256, 128, 64))

    def body(xm_ref, da_ref, db_ref, dy_ref, mo_ref, v_ref, dxm_ref, dmo_ref, cs_ref):
        i = pl.program_id(0)
        xm = xm_ref[...]
        r = lax.rsqrt(jnp.mean(xm * xm, axis=-1, keepdims=True) + EPS)
        xh = xm * r
        g = v_ref[1:2, :]
        nrm = xh * g
        dhv = da_ref[...] + db_ref[...]
        dn = dhv * (1.0 + v_ref[3:4, :])
        dxh = dn * g
        dxm = dy_ref[...] + r * (dxh - xh * jnp.mean(dxh * xh, axis=-1, keepdims=True))
        dxm_ref[...] = dxm
        dmo_ref[...] = (dxm * v_ref[0:1, :]).astype(BF16)
        s0 = jnp.sum(dhv, axis=0, keepdims=True)
        s1 = jnp.sum(dhv * nrm, axis=0, keepdims=True)
        s2 = jnp.sum(dn * xh, axis=0, keepdims=True)
        s3 = jnp.sum(dxm * mo_ref[...], axis=0, keepdims=True)
        rows = lax.broadcasted_iota(jnp.int32, (8, D), 0)
        upd = jnp.where(rows == 0, s0, jnp.where(rows == 1, s1, jnp.where(rows == 2, s2,
              jnp.where(rows == 3, s3, 0.0))))

        @pl.when(i == 0)
        def _():
            cs_ref[...] = jnp.zeros_like(cs_ref)

        cs_ref[...] += upd

    blk = pl.BlockSpec((tb, D), lambda i: (i, 0))
    vec = pl.BlockSpec((8, D), lambda i: (0, 0))
    return _pcall(
        body, name="resid_rms2_bwd", grid=(L // tb,),
        in_specs=[blk, blk, blk, blk, blk, vec],
        out_specs=[blk, blk, vec],
        out_shape=[jax.ShapeDtypeStruct((L, D), F32), jax.ShapeDtypeStruct((L, D), BF16),
                   jax.ShapeDtypeStruct((8, D), F32)],
        compiler_params=_params(("arbitrary",)),
    )(xmid, dh_a, dh_b, dy, mo, vecs)


def _loss_head(xmid, f, g2, target):
    L, D = xmid.shape
    tb = _pick(L, (256, 128, 64))

    def body(xm_ref, f_ref, g_ref, t_ref, dy_ref, df_ref, s_ref):
        i = pl.program_id(0)
        fv = f_ref[...]
        g = g_ref[...]
        err = xm_ref[...] + g * fv - t_ref[...]
        dy = err * (1.0 / D)
        dy_ref[...] = dy
        df_ref[...] = (dy * g).astype(BF16)
        s0 = jnp.sum(dy * fv, axis=0, keepdims=True)
        part = 0.5 * jnp.sum(jnp.mean(err * err, axis=-1, keepdims=True), axis=0, keepdims=True)
        rows = lax.broadcasted_iota(jnp.int32, (8, D), 0)
        upd = jnp.where(rows == 0, s0, jnp.where(rows == 1, part, 0.0))

        @pl.when(i == 0)
        def _():
            s_ref[...] = jnp.zeros_like(s_ref)

        s_ref[...] += upd

    blk = pl.BlockSpec((tb, D), lambda i: (i, 0))
    return _pcall(
        body, name="loss_head", grid=(L // tb,),
        in_specs=[blk, blk, pl.BlockSpec((1, D), lambda i: (0, 0)), blk],
        out_specs=[blk, blk, pl.BlockSpec((8, D), lambda i: (0, 0))],
        out_shape=[jax.ShapeDtypeStruct((L, D), F32), jax.ShapeDtypeStruct((L, D), BF16),
                   jax.ShapeDtypeStruct((8, D), F32)],
        compiler_params=_params(("arbitrary",)),
    )(xmid, f, g2, target)


def _gate_cols(D, off):
    tc = _pick(np.gcd(D, off), (512, 256, 128))
    return tc, off // tc


def _merge_fwd(za, zb, p, n_ctx, off_a, off_b):
    L, D = za.shape
    tb = _pick(n_ctx, (256, 128, 64, 32, 16))
    nctx = n_ctx // tb
    tc, oa = _gate_cols(D, off_a)
    _, ob = _gate_cols(D, off_b)
    if off_b % tc:
        raise ValueError("gate column offsets must share a column tile")
    ob = off_b // tc

    def body(za_ref, zb_ref, ga_ref, gb_ref, z_ref):
        z_ref[...] = (_sigmoid(ga_ref[...]) * za_ref[...].astype(F32)
                      + _sigmoid(gb_ref[...]) * zb_ref[...].astype(F32)).astype(BF16)

    blk = pl.BlockSpec((tb, tc), lambda i, j: (i, j))
    return _pcall(
        body, name="merge_fwd", grid=(L // tb, D // tc),
        in_specs=[blk, blk, pl.BlockSpec((tb, tc), lambda i, j: (i + nctx, oa + j)),
                  pl.BlockSpec((tb, tc), lambda i, j: (i + nctx, ob + j))],
        out_specs=blk,
        out_shape=jax.ShapeDtypeStruct((L, D), BF16),
        compiler_params=_params(("parallel", "parallel")),
    )(za, zb, p, p)


def _merge_bwd(dz, za, zb, p, n_ctx, off_a, off_b):
    L, D = za.shape
    T = L + n_ctx
    tb = _pick(n_ctx, (256, 128, 64, 32, 16))
    nctx = n_ctx // tb
    tc = _gate_cols(D, off_a)[0]
    oa, ob = off_a // tc, off_b // tc

    def body(dz_ref, za_ref, zb_ref, ga_ref, gb_ref, dza_ref, dzb_ref, dga_ref, dgb_ref):
        i = pl.program_id(1)

        @pl.when(i < nctx)
        def _():
            dga_ref[...] = jnp.zeros_like(dga_ref)
            dgb_ref[...] = jnp.zeros_like(dgb_ref)

        @pl.when(i >= nctx)
        def _():
            dzv = dz_ref[...].astype(F32)
            sa = _sigmoid(ga_ref[...])
            sb = _sigmoid(gb_ref[...])
            dza_ref[...] = (dzv * sa).astype(BF16)
            dzb_ref[...] = (dzv * sb).astype(BF16)
            dga_ref[...] = (dzv * za_ref[...].astype(F32) * sa * (1.0 - sa)).astype(BF16)
            dgb_ref[...] = (dzv * zb_ref[...].astype(F32) * sb * (1.0 - sb)).astype(BF16)

    lat = pl.BlockSpec((tb, tc), lambda j, i: (jnp.maximum(i - nctx, 0), j))
    allr = pl.BlockSpec((tb, tc), lambda j, i: (i, j))
    return _pcall(
        body, name="merge_bwd", grid=(D // tc, T // tb),
        in_specs=[lat, lat, lat, pl.BlockSpec((tb, tc), lambda j, i: (i, oa + j)),
                  pl.BlockSpec((tb, tc), lambda j, i: (i, ob + j))],
        out_specs=[lat, lat, allr, allr],
        out_shape=[jax.ShapeDtypeStruct((L, D), BF16), jax.ShapeDtypeStruct((L, D), BF16),
                   jax.ShapeDtypeStruct((T, D), BF16), jax.ShapeDtypeStruct((T, D), BF16)],
        compiler_params=_params(("arbitrary", "arbitrary")),
    )(dz, za, zb, p, p)


def _shift_down(u, rows):
    return jnp.where(rows == 0, 0.0, pltpu.roll(u, 1, 0))


def _shift_up(u, rows):
    n = u.shape[0]
    return jnp.where(rows == n - 1, 0.0, pltpu.roll(u, n - 1, 0))


def _convgate_fwd(u1, u3, cw, cb):
    L, F = u1.shape
    tc = _pick(F, (256, 128))

    def body(u1_ref, u3_ref, w_ref, b_ref, a_ref):
        u = u1_ref[...].astype(F32)
        rows = lax.broadcasted_iota(jnp.int32, u.shape, 0)
        cv = _shift_down(u, rows) * w_ref[0:1, :] + u * w_ref[1:2, :] + _shift_up(u, rows) * w_ref[2:3, :] + b_ref[...]
        a_ref[...] = (cv * _sigmoid(cv) * u3_ref[...].astype(F32)).astype(BF16)

    blk = pl.BlockSpec((L, tc), lambda j: (0, j))
    return _pcall(
        body, name="convgate_fwd", grid=(F // tc,),
        in_specs=[blk, blk, pl.BlockSpec((8, tc), lambda j: (0, j)), pl.BlockSpec((1, tc), lambda j: (0, j))],
        out_specs=blk,
        out_shape=jax.ShapeDtypeStruct((L, F), BF16),
        compiler_params=_params(("parallel",)),
    )(u1, u3, cw, cb)


def _convgate_bwd(u1, u3, da, cw, cb):
    L, F = u1.shape
    tc = _pick(F, (256, 128))

    def body(u1_ref, u3_ref, da_ref, w_ref, b_ref, du1_ref, du3_ref, s_ref):
        u = u1_ref[...].astype(F32)
        rows = lax.broadcasted_iota(jnp.int32, u.shape, 0)
        um, up = _shift_down(u, rows), _shift_up(u, rows)
        w0, w1, w2 = w_ref[0:1, :], w_ref[1:2, :], w_ref[2:3, :]
        cv = um * w0 + u * w1 + up * w2 + b_ref[...]
        s = _sigmoid(cv)
        dav = da_ref[...].astype(F32)
        du3_ref[...] = (dav * cv * s).astype(BF16)
        dcv = dav * u3_ref[...].astype(F32) * (s * (1.0 + cv * (1.0 - s)))
        du1_ref[...] = (_shift_up(dcv, rows) * w0 + dcv * w1 + _shift_down(dcv, rows) * w2).astype(BF16)
        r8 = lax.broadcasted_iota(jnp.int32, (8, tc), 0)
        s0 = jnp.sum(dcv * um, axis=0, keepdims=True)
        s1 = jnp.sum(dcv * u, axis=0, keepdims=True)
        s2 = jnp.sum(dcv * up, axis=0, keepdims=True)
        s3 = jnp.sum(dcv, axis=0, keepdims=True)
        s_ref[...] = jnp.where(r8 == 0, s0, jnp.where(r8 == 1, s1, jnp.where(r8 == 2, s2,
                     jnp.where(r8 == 3, s3, 0.0))))

    blk = pl.BlockSpec((L, tc), lambda j: (0, j))
    v8 = pl.BlockSpec((8, tc), lambda j: (0, j))
    return _pcall(
        body, name="convgate_bwd", grid=(F // tc,),
        in_specs=[blk, blk, blk, v8, pl.BlockSpec((1, tc), lambda j: (0, j))],
        out_specs=[blk, blk, v8],
        out_shape=[jax.ShapeDtypeStruct((L, F), BF16), jax.ShapeDtypeStruct((L, F), BF16),
                   jax.ShapeDtypeStruct((8, F), F32)],
        compiler_params=_params(("parallel",)),
    )(u1, u3, da, cw, cb)


def _lower_bound(lbl_ref, d):
    l0, l1 = lbl_ref[d, 0:1, :], lbl_ref[d, 1:2, :]
    m = jnp.maximum(l0, l1)
    e0, e1 = jnp.exp(l0 - m), jnp.exp(l1 - m)
    return e0 / (e0 + e1)


def _chunk_cumsum(x, rev):
    n = x.shape[0]
    r = lax.broadcasted_iota(jnp.int32, x.shape, 0) % CHUNK
    k = 1
    while k < CHUNK:
        if rev:
            x = x + jnp.where(r < CHUNK - k, pltpu.roll(x, n - k, 0), 0.0)
        else:
            x = x + jnp.where(r >= k, pltpu.roll(x, k, 0), 0.0)
        k *= 2
    return x


def _gate_terms(z, lb):
    sg = _sigmoid(z)
    f = lb + (1.0 - lb) * sg
    return sg, f


def _decay_terms(z, lb, rev):
    _, f = _gate_terms(z, lb)
    g = jnp.log(f)
    return 1.0 - f, _chunk_cumsum(g, rev), _chunk_cumsum(g, not rev) - g


def _chunk_total(c, rev):
    return c[0:1, :] if rev else c[CHUNK - 1:CHUNK, :]


def _pair_decay(c, s, rev):
    t = lax.broadcasted_iota(jnp.int32, (CHUNK, 1), 0)
    later = (t <= s) if rev else (t >= s)
    return jnp.where(later, jnp.exp(c - c[s:s + 1, :]), 0.0)


def _scan_chunk(i, n_ctx_chunks, n_chunks, rev):
    if not rev:
        return i
    return jnp.where(i < n_ctx_chunks, n_ctx_chunks - 1 - i, n_chunks + n_ctx_chunks - 1 - i)


def _rows(ci):
    return pl.ds(pl.multiple_of(ci * CHUNK, CHUNK), CHUNK)


def _hgrn_cols(HA):
    return HA // HEAD


def _hgrn_fwd(p, lbl, ng, n_ctx, HA):
    T = p.shape[0]
    L = T - n_ctx
    nh = _hgrn_cols(HA)
    nc, ncc = T // CHUNK, n_ctx // CHUNK

    def body(q_ref, zf_ref, zb_ref, v_ref, og_ref, lbl_ref, ng_ref, ya_ref, o_ref, st_ref,
             c_scr, k_scr, qe_scr, ke_scr, o_scr):
        dirs = ((0, False, zf_ref), (1, True, zb_ref))
        for d, rev, z_ref in dirs:
            k, c, rest = _decay_terms(z_ref[...], _lower_bound(lbl_ref, d), rev)
            c_scr[d] = c
            k_scr[d] = k
            qe_scr[d] = (q_ref[...] * jnp.exp(c)).astype(BF16)
            ke_scr[d] = (k * jnp.exp(rest)).astype(BF16)

        def step(i2, states):
            states = list(states)
            for u in range(HGRN_UNROLL):
                for d, rev, _ in dirs:
                    St = states[d]
                    ci = _scan_chunk(HGRN_UNROLL * i2 + u, ncc, nc, rev)
                    rows = _rows(ci)
                    q, v, c, k = q_ref[rows, :], v_ref[rows, :], c_scr[d, rows, :], k_scr[d, rows, :]
                    st_ref[0, d, ci] = St.astype(BF16)
                    o = jnp.zeros((CHUNK, HEAD), F32)
                    for s in range(CHUNK):
                        E = _pair_decay(c, s, rev)
                        a = jnp.sum(q * E * k[s:s + 1, :], axis=1, keepdims=True)
                        o = o + a * v[s:s + 1, :]
                    o_scr[d, rows, :] = o + _dot_nt(qe_scr[d, rows, :], St.astype(BF16))
                    states[d] = St * jnp.exp(_chunk_total(c, rev)) + _dot_tn(v.astype(BF16), ke_scr[d, rows, :])
            return tuple(states)

        if nc % HGRN_UNROLL:
            raise ValueError("the number of chunks must be a multiple of HGRN_UNROLL")
        zero = jnp.zeros((HEAD, HEAD), F32)
        lax.fori_loop(0, nc // HGRN_UNROLL, step, (zero, zero))

        o = o_scr[0, pl.ds(n_ctx, L), :] + o_scr[1, pl.ds(n_ctx, L), :]
        o_ref[...] = o
        r = lax.rsqrt(jnp.mean(o * o, axis=-1, keepdims=True) + EPS)
        og = og_ref[pl.ds(n_ctx, L), :]
        ya_ref[...] =(o * r * ng_ref[...] * (og * _sigmoid(og))).astype(BF16)

    cb = HA // HEAD
    col = lambda kk: pl.BlockSpec((T, HEAD), lambda h: (0, kk * cb + h))
    return _pcall(
        body, name="hgrn_fwd", grid=(nh,),
        in_specs=[col(0), col(1), col(2), col(3), col(4),
                  pl.BlockSpec((2, 2, HEAD), lambda h: (0, 0, h)), pl.BlockSpec((1, HEAD), lambda h: (0, 0))],
        out_specs=[pl.BlockSpec((L, HEAD), lambda h: (0, h)), pl.BlockSpec((L, HEAD), lambda h: (0, h)),
                   pl.BlockSpec((1, 2, nc, HEAD, HEAD), lambda h: (h, 0, 0, 0, 0))],
        out_shape=[jax.ShapeDtypeStruct((L, HA), BF16), jax.ShapeDtypeStruct((L, HA), F32),
                   jax.ShapeDtypeStruct((nh, 2, nc, HEAD, HEAD), BF16)],
        scratch_shapes=[pltpu.VMEM((2, T, HEAD), F32), pltpu.VMEM((2, T, HEAD), F32),
                        pltpu.VMEM((2, T, HEAD), BF16), pltpu.VMEM((2, T, HEAD), BF16),
                        pltpu.VMEM((2, T, HEAD), F32)],
        compiler_params=_params(("parallel",)),
    )(p, p, p, p, p, lbl, ng)


def _hgrn_bwd(p, lbl, ng, o, dya, st, n_ctx, HA):
    T = p.shape[0]
    L = T - n_ctx
    nh = _hgrn_cols(HA)
    nc, ncc = T // CHUNK, n_ctx // CHUNK

    def body(q_ref, zf_ref, zb_ref, v_ref, og_ref, lbl_ref, ng_ref, o_ref, dya_ref, st_ref,
             dq_ref, dzf_ref, dzb_ref, dv_ref, dog_ref, dlbl_ref, dng_ref,
             do_scr, c_scr, k_scr, qe_scr, ke_scr, dg_scr, dk_scr, dq_scr, dv_scr, row_scr):
        h = pl.program_id(0)
        ov = o_ref[...]
        r = lax.rsqrt(jnp.mean(ov * ov, axis=-1, keepdims=True) + EPS)
        oh = ov * r
        ogv = og_ref[pl.ds(n_ctx, L), :]
        sg_o = _sigmoid(ogv)
        dyv = dya_ref[...]
        ngv = ng_ref[...]
        dog_ref[pl.ds(0, n_ctx), :] = jnp.zeros((n_ctx, HEAD), BF16)
        dog_ref[pl.ds(n_ctx, L), :] = (dyv * oh * ngv * (sg_o * (1.0 + ogv * (1.0 - sg_o)))).astype(BF16)
        don = dyv * (ogv * sg_o)
        dng = jnp.sum(don * oh, axis=0, keepdims=True)
        doh = don * ngv
        do_scr[pl.ds(0, n_ctx), :] = jnp.zeros((n_ctx, HEAD), F32)
        do_scr[pl.ds(n_ctx, L), :] = r * (doh - oh * jnp.mean(doh * oh, axis=-1, keepdims=True))

        @pl.when(h == 0)
        def _():
            dng_ref[...] = jnp.zeros_like(dng_ref)

        dng_ref[0:1, :] += dng

        t16 = lax.broadcasted_iota(jnp.int32, (CHUNK, HEAD), 0)
        dirs = ((0, False, zf_ref, dzf_ref), (1, True, zb_ref, dzb_ref))
        for d, rev, z_ref, _ in dirs:
            k, c, rest = _decay_terms(z_ref[...], _lower_bound(lbl_ref, d), rev)
            c_scr[d] = c
            k_scr[d] = k
            qe_scr[d] = (q_ref[...] * jnp.exp(c)).astype(BF16)
            ke_scr[d] = (k * jnp.exp(rest)).astype(BF16)
        dq_scr[...] = jnp.zeros_like(dq_scr)
        dv_scr[...] = jnp.zeros_like(dv_scr)

        zero = jnp.zeros((HEAD, HEAD), F32)

        def bwd_chunk(i, carry, u):
            new = []
            for (d, rev, _, _), dSt in zip(dirs, carry):
                ci = _scan_chunk(i, ncc, nc, rev)
                rows = _rows(ci)
                q, v, do = q_ref[rows, :], v_ref[rows, :], do_scr[rows, :]
                c, k = c_scr[d, rows, :], k_scr[d, rows, :]
                tot = _chunk_total(c, rev)
                etot = jnp.exp(tot)
                St = st_ref[0, d, ci]
                dSb = dSt.astype(BF16)
                do_b = do.astype(BF16)
                dq_x = _dot(do_b, St) * jnp.exp(c)
                dk_x = _dot(v.astype(BF16), dSb) * jnp.exp(tot - c)
                dv_x = _dot_nt(ke_scr[d, rows, :], dSb)
                dtot = (jnp.sum(St.astype(F32) * dSt, axis=0, keepdims=True) * etot
                        + jnp.sum(k * dk_x, axis=0, keepdims=True))
                dq = jnp.zeros((CHUNK, HEAD), F32)
                for s in range(CHUNK):
                    E = _pair_decay(c, s, rev)
                    XE = E * k[s:s + 1, :]
                    a = jnp.sum(q * XE, axis=1, keepdims=True)
                    da = jnp.sum(do * v[s:s + 1, :], axis=1, keepdims=True)
                    dq = dq + da * XE
                    row_scr[u, d, 0, s:s + 1, :] = jnp.sum(da * q * E, axis=0, keepdims=True)
                    row_scr[u, d, 1, s:s + 1, :] = jnp.sum(a * do, axis=0, keepdims=True)
                dq, dk, dv = dq + dq_x, row_scr[u, d, 0] + dk_x, row_scr[u, d, 1] + dv_x
                dg_scr[d, rows, :] = _chunk_cumsum(q * dq - k * dk, not rev) + dtot
                dk_scr[d, rows, :] = dk
                dq_scr[rows, :] += dq
                dv_scr[rows, :] += dv
                new.append(dSt * etot + _dot_tn(do_b, qe_scr[d, rows, :]))
            return tuple(new)

        def bwd_step(i2, carry):
            for u in range(2):
                carry = bwd_chunk(nc - 1 - (2 * i2 + u), carry, u)
            return carry

        lax.fori_loop(0, nc // 2, bwd_step, (zero, zero))

        for d, _, z_ref, dz_ref in dirs:
            lb = _lower_bound(lbl_ref, d)
            sg, f = _gate_terms(z_ref[...], lb)
            df = dg_scr[d] / f - dk_scr[d]
            dz_ref[...] = (df * (1.0 - lb) * sg * (1.0 - sg)).astype(BF16)
            dl0 = jnp.sum(df * (1.0 - sg), axis=0, keepdims=True) * lb * (1.0 - lb)
            dlbl_ref[d, 0:1, :] = dl0
            dlbl_ref[d, 1:2, :] = -dl0
        dq_ref[...] = dq_scr[...].astype(BF16)
        dv_ref[...] = dv_scr[...].astype(BF16)

    cb = HA // HEAD
    col = lambda kk: pl.BlockSpec((T, HEAD), lambda h: (0, kk * cb + h))
    tcol = pl.BlockSpec((T, HEAD), lambda h: (0, h))
    lcol = pl.BlockSpec((L, HEAD), lambda h: (0, h))
    outs = _pcall(
        body, name="hgrn_bwd", grid=(nh,),
        in_specs=[col(0), col(1), col(2), col(3), col(4),
                  pl.BlockSpec((2, 2, HEAD), lambda h: (0, 0, h)), pl.BlockSpec((1, HEAD), lambda h: (0, 0)),
                  lcol, lcol,
                  pl.BlockSpec((1, 2, nc, HEAD, HEAD), lambda h: (h, 0, 0, 0, 0), pipeline_mode=pl.Buffered(1))],
        out_specs=[tcol, tcol, tcol, tcol, tcol, pl.BlockSpec((2, 2, HEAD), lambda h: (0, 0, h)),
                   pl.BlockSpec((8, HEAD), lambda h: (0, 0))],
        out_shape=[jax.ShapeDtypeStruct((T, HA), BF16)] * 5 + [jax.ShapeDtypeStruct((2, 2, HA), F32),
                                                               jax.ShapeDtypeStruct((8, HEAD), F32)],
        scratch_shapes=[pltpu.VMEM((T, HEAD), F32),
                        pltpu.VMEM((2, T, HEAD), F32), pltpu.VMEM((2, T, HEAD), F32),
                        pltpu.VMEM((2, T, HEAD), BF16), pltpu.VMEM((2, T, HEAD), BF16),
                        pltpu.VMEM((2, T, HEAD), F32), pltpu.VMEM((2, T, HEAD), F32),
                        pltpu.VMEM((T, HEAD), F32), pltpu.VMEM((T, HEAD), F32),
                        pltpu.VMEM((2, 2, 2, CHUNK, HEAD), F32)],
        compiler_params=_params(("arbitrary",)),
    )(p, p, p, p, p, lbl, ng, o, dya, st)
    return outs


def _swap_halves(t, lane):
    q = HEAD // 4
    return jnp.where((lane % (2 * q)) < q, pltpu.roll(t, HEAD - q, 1), pltpu.roll(t, q, 1))


def _qk_norm(t, g):
    r = lax.rsqrt(jnp.mean(t * t, axis=-1, keepdims=True) + EPS)
    return t * r, r


def _rope(t, cos, sin, lane):
    return t * cos + _swap_halves(t, lane) * sin


def _qk_norm_bwd(dy, th, r, g):
    dth = dy * g
    return r * (dth - th * jnp.mean(dth * th, axis=-1, keepdims=True)), jnp.sum(dy * th, axis=0, keepdims=True)


def _rope_bwd(dy, cos, sin, lane):
    return dy * cos + _swap_halves(dy * sin, lane)


def _na_geometry(L):
    n_rows = L // GRID_W
    kr = min(WIN_R, n_rows)
    return n_rows, kr


def _na_prep(q_ref, k_ref, v_ref, gq_ref, gk_ref, cos_ref, sin_ref, qs, ks, vs, n_ctx, L):
    lane = lax.broadcasted_iota(jnp.int32, (L, HEAD), 1)
    cos, sin = cos_ref[...], sin_ref[...]
    qh, _ = _qk_norm(q_ref[pl.ds(n_ctx, L), :], None)
    qs[...] = _rope(qh * gq_ref[...], cos, sin, lane).astype(BF16)
    kh, _ = _qk_norm(k_ref[pl.ds(n_ctx, L), :], None)
    ks[pl.ds(n_ctx, L), :] = _rope(kh * gk_ref[...], cos, sin, lane).astype(BF16)
    kc, _ = _qk_norm(k_ref[pl.ds(0, n_ctx), :], None)
    ks[pl.ds(0, n_ctx), :] = (kc * gk_ref[...]).astype(BF16)
    vs[...] = v_ref[...].astype(BF16)


NA_RB = 4


def _na_band_rows(kr):
    return kr + NA_RB


def _na_scores(i, qs, ks, bias_ref, n_ctx, n_rows, kr):
    scale = HEAD ** -0.5
    kb = _na_band_rows(kr)
    rq = NA_RB * i
    r0 = jnp.clip(rq - WIN_R // 2, 0, n_rows - kb)
    qrows = pl.ds(pl.multiple_of(rq * GRID_W, NA_RB * GRID_W), NA_RB * GRID_W)
    krows = pl.ds(pl.multiple_of(n_ctx + r0 * GRID_W, GRID_W), kb * GRID_W)
    qv = qs[qrows, :]
    sb = _dot_nt(qv, ks[krows, :]) * scale
    band_row = lax.broadcasted_iota(jnp.int32, (GRID_W, kb * GRID_W), 1) // GRID_W
    parts, tiles = [], []
    for u in range(NA_RB):
        r_u = rq + u
        first = jnp.clip(r_u - WIN_R // 2, 0, n_rows - kr) - r0
        idx = [jnp.clip(r0 - r_u + (WIN_R - 1) + 2 * jj, 0, 2 * WIN_R - 1) for jj in range(kb // 2)]
        bias_u = jnp.concatenate([bias_ref[0, t] for t in idx], axis=1)
        inside = (band_row >= first) & (band_row < first + kr)
        parts.append(jnp.where(inside, sb[u * GRID_W:(u + 1) * GRID_W, :] + bias_u, NEG))
        tiles.append(idx)
    sb = jnp.concatenate(parts, axis=0)
    sc = _dot_nt(qv, ks[pl.ds(0, n_ctx), :]) * scale
    m = jnp.maximum(jnp.max(sb, axis=1, keepdims=True), jnp.max(sc, axis=1, keepdims=True))
    eb, ec = jnp.exp(sb - m), jnp.exp(sc - m)
    inv = 1.0 / (jnp.sum(eb, axis=1, keepdims=True) + jnp.sum(ec, axis=1, keepdims=True))
    return eb * inv, ec * inv, qrows, krows, tiles


def _na_fwd(p, bias, gq, gk, cos, sin, n_ctx, off, HB):
    T = p.shape[0]
    L = T - n_ctx
    nh = HB // HEAD
    n_rows, kr = _na_geometry(L)
    ob = off // HEAD

    def body(q_ref, k_ref, v_ref, bias_ref, gq_ref, gk_ref, cos_ref, sin_ref, y_ref, qs, ks, vs):
        _na_prep(q_ref, k_ref, v_ref, gq_ref, gk_ref, cos_ref, sin_ref, qs, ks, vs, n_ctx, L)

        def step(i, carry):
            pb, pc, qrows, krows, _ = _na_scores(i, qs, ks, bias_ref, n_ctx, n_rows, kr)
            y = _dot(pb.astype(BF16), vs[krows, :]) + _dot(pc.astype(BF16), vs[pl.ds(0, n_ctx), :])
            y_ref[qrows, :] = y.astype(BF16)
            return carry

        lax.fori_loop(0, n_rows // NA_RB, step, 0)

    col = lambda kk: pl.BlockSpec((T, HEAD), lambda h: (0, ob + kk * nh + h))
    vec = pl.BlockSpec((1, HEAD), lambda h: (0, 0))
    tab = pl.BlockSpec((L, HEAD), lambda h: (0, 0))
    return _pcall(
        body, name="na_fwd", grid=(nh,),
        in_specs=[col(0), col(1), col(2), pl.BlockSpec((1,) + bias.shape[1:], lambda h: (h, 0, 0, 0)),
                  vec, vec, tab, tab],
        out_specs=pl.BlockSpec((L, HEAD), lambda h: (0, h)),
        out_shape=jax.ShapeDtypeStruct((L, HB), BF16),
        scratch_shapes=[pltpu.VMEM((L, HEAD), BF16), pltpu.VMEM((T, HEAD), BF16), pltpu.VMEM((T, HEAD), BF16)],
        compiler_params=_params(("parallel",)),
    )(p, p, p, bias, gq, gk, cos, sin)


def _na_bwd(p, bias, gq, gk, cos, sin, dyb, n_ctx, off, HB):
    T = p.shape[0]
    L = T - n_ctx
    nh = HB // HEAD
    n_rows, kr = _na_geometry(L)
    ob = off // HEAD
    scale = HEAD ** -0.5

    def body(q_ref, k_ref, v_ref, bias_ref, gq_ref, gk_ref, cos_ref, sin_ref, dy_ref,
             dq_ref, dk_ref, dv_ref, dbias_ref, dg_ref, qs, ks, vs, dqa, dka, dva):
        h = pl.program_id(0)
        _na_prep(q_ref, k_ref, v_ref, gq_ref, gk_ref, cos_ref, sin_ref, qs, ks, vs, n_ctx, L)
        dka[...] = jnp.zeros_like(dka)
        dva[...] = jnp.zeros_like(dva)
        dbias_ref[...] = jnp.zeros_like(dbias_ref)

        crows = pl.ds(0, n_ctx)

        def step(i, carry):
            pb, pc, qrows, krows, tiles = _na_scores(i, qs, ks, bias_ref, n_ctx, n_rows, kr)
            do = dy_ref[qrows, :]
            qv = qs[qrows, :]
            dpb = _dot_nt(do, vs[krows, :])
            dpc = _dot_nt(do, vs[crows, :])
            delta = jnp.sum(pb * dpb, axis=1, keepdims=True) + jnp.sum(pc * dpc, axis=1, keepdims=True)
            dsb = pb * (dpb - delta)
            dsc = pc * (dpc - delta)
            dsb_b, dsc_b = dsb.astype(BF16), dsc.astype(BF16)
            dqa[qrows, :] = (_dot(dsb_b, ks[krows, :]) + _dot(dsc_b, ks[crows, :])) * scale
            dka[krows, :] += _dot_tn(dsb_b, qv) * scale
            dka[crows, :] += _dot_tn(dsc_b, qv) * scale
            dva[krows, :] += _dot_tn(pb.astype(BF16), do)
            dva[crows, :] += _dot_tn(pc.astype(BF16), do)
            for u, idx in enumerate(tiles):
                for jj, t in enumerate(idx):
                    dbias_ref[0, t] += dsb[u * GRID_W:(u + 1) * GRID_W, jj * 2 * GRID_W:(jj + 1) * 2 * GRID_W]
            return carry

        lax.fori_loop(0, n_rows // NA_RB, step, 0)

        lane = lax.broadcasted_iota(jnp.int32, (L, HEAD), 1)
        cos, sin = cos_ref[...], sin_ref[...]
        lat, ctx = pl.ds(n_ctx, L), pl.ds(0, n_ctx)
        gqv, gkv = gq_ref[...], gk_ref[...]
        qh, rq = _qk_norm(q_ref[lat, :], None)
        dq, dgq = _qk_norm_bwd(_rope_bwd(dqa[...], cos, sin, lane), qh, rq, gqv)
        dq_ref[ctx, :] = jnp.zeros((n_ctx, HEAD), BF16)
        dq_ref[lat, :] = dq.astype(BF16)
        kh, rk = _qk_norm(k_ref[lat, :], None)
        dk, dgk = _qk_norm_bwd(_rope_bwd(dka[lat, :], cos, sin, lane), kh, rk, gkv)
        dk_ref[lat, :] = dk.astype(BF16)
        kch, rkc = _qk_norm(k_ref[ctx, :], None)
        dkc, dgkc = _qk_norm_bwd(dka[ctx, :], kch, rkc, gkv)
        dk_ref[ctx, :] = dkc.astype(BF16)
        dv_ref[...] = dva[...].astype(BF16)

        @pl.when(h == 0)
        def _():
            dg_ref[...] = jnp.zeros_like(dg_ref)

        dg_ref[0:1, :] += dgq
        dg_ref[1:2, :] += dgk + dgkc

    col = lambda kk: pl.BlockSpec((T, HEAD), lambda h: (0, ob + kk * nh + h))
    vec = pl.BlockSpec((1, HEAD), lambda h: (0, 0))
    tab = pl.BlockSpec((L, HEAD), lambda h: (0, 0))
    tcol = pl.BlockSpec((T, HEAD), lambda h: (0, h))
    bspec = pl.BlockSpec((1,) + bias.shape[1:], lambda h: (h, 0, 0, 0))
    return _pcall(
        body, name="na_bwd", grid=(nh,),
        in_specs=[col(0), col(1), col(2), bspec, vec, vec, tab, tab, pl.BlockSpec((L, HEAD), lambda h: (0, h))],
        out_specs=[tcol, tcol, tcol, bspec, pl.BlockSpec((8, HEAD), lambda h: (0, 0))],
        out_shape=[jax.ShapeDtypeStruct((T, HB), BF16)] * 3 + [jax.ShapeDtypeStruct(bias.shape, F32),
                                                               jax.ShapeDtypeStruct((8, HEAD), F32)],
        scratch_shapes=[pltpu.VMEM((L, HEAD), BF16), pltpu.VMEM((T, HEAD), BF16), pltpu.VMEM((T, HEAD), BF16),
                        pltpu.VMEM((L, HEAD), F32), pltpu.VMEM((T, HEAD), F32), pltpu.VMEM((T, HEAD), F32)],
        compiler_params=_params(("arbitrary",)),
    )(p, p, p, bias, gq, gk, cos, sin, dyb)


def _bias_tables():
    w = np.arange(GRID_W)
    col_start = np.clip(w - WIN_C // 2, 0, GRID_W - WIN_C)
    col_in = (w[None, :] >= col_start[:, None]) & (w[None, :] < col_start[:, None] + WIN_C)
    dc = np.clip(w[None, :] - w[:, None], -(WIN_C - 1), WIN_C - 1) + WIN_C - 1
    n_pair = 2 * WIN_R
    ridx = np.zeros((n_pair, GRID_W, 2 * GRID_W), np.int32)
    cidx = np.zeros((n_pair, GRID_W, 2 * GRID_W), np.int32)
    valid = np.zeros((n_pair, GRID_W, 2 * GRID_W), bool)
    for i in range(n_pair):
        for half in range(2):
            row = i + half
            sl = slice(half * GRID_W, (half + 1) * GRID_W)
            ridx[i, :, sl] = min(row, 2 * WIN_R - 2)
            cidx[i, :, sl] = dc
            valid[i, :, sl] = col_in & (row <= 2 * WIN_R - 2)
    return ridx, cidx, valid


def _bias_onehot():
    _, cidx, valid = _bias_tables()
    K = GRID_W * 2 * GRID_W
    oh = np.zeros((K, 128), np.float32)
    neg = np.full((1, K), NEG, np.float32)
    for cq in range(GRID_W):
        for ll in range(2 * GRID_W):
            if valid[0, cq, ll]:
                oh[cq * 2 * GRID_W + ll, (ll // GRID_W) * 64 + cidx[0, cq, ll]] = 1.0
                neg[0, cq * 2 * GRID_W + ll] = 0.0
    return oh, neg


def _expand_bias(table):
    H = table.shape[0]
    n_pair, n_dc = 2 * WIN_R, 2 * WIN_C - 1
    tp = jnp.pad(table, ((0, 0), (0, n_pair + 1 - table.shape[1]), (0, 64 - n_dc)))
    t2 = jnp.concatenate([tp[:, :n_pair], tp[:, 1:n_pair + 1]], axis=-1).reshape(H * n_pair, 128)
    oh, neg = _bias_onehot()

    def body(t_ref, oh_ref, neg_ref, o_ref):
        o_ref[...] = lax.dot_general(t_ref[...], oh_ref[...], (((1,), (1,)), ((), ())), precision=HI,
                                     preferred_element_type=F32) + neg_ref[...]

    out = _pcall(body, name="bias_expand", out_shape=jax.ShapeDtypeStruct((H * n_pair, oh.shape[0]), F32),
                         compiler_params=_params())(t2, jnp.asarray(oh), jnp.asarray(neg))
    return out.reshape(H, n_pair, GRID_W, 2 * GRID_W)


def _bias_grad(dbias):
    H = dbias.shape[0]
    n_pair, n_dc = 2 * WIN_R, 2 * WIN_C - 1
    K = GRID_W * 2 * GRID_W
    oh, _ = _bias_onehot()
    flat = dbias.reshape(H * n_pair, K)

    def body(d_ref, oh_ref, o_ref):
        o_ref[...] = jnp.dot(d_ref[...], oh_ref[...], precision=HI, preferred_element_type=F32)

    g = _pcall(body, name="bias_grad", out_shape=jax.ShapeDtypeStruct((H * n_pair, 128), F32),
                       compiler_params=_params())(flat, jnp.asarray(oh))
    g = g.reshape(H, n_pair, 128)
    left, right = g[:, :, :n_dc], g[:, :, 64:64 + n_dc]
    out = left[:, :n_pair - 1]
    return out.at[:, 1:].add(right[:, :n_pair - 2])


def _rope_tables(L):
    pos = np.arange(L)
    row = (pos // GRID_W).astype(np.float32)
    colp = (pos % GRID_W).astype(np.float32)
    half = HEAD // 2
    nf = half // 2
    inv = (ROPE_THETA ** (-np.arange(nf, dtype=np.float32) / nf)).astype(np.float32)

    def tabs(pv):
        ang = pv[:, None] * inv[None, :]
        c, s = np.cos(ang), np.sin(ang)
        return np.concatenate([c, c], axis=1), np.concatenate([-s, s], axis=1)

    cr, sr = tabs(row)
    cc, sc = tabs(colp)
    return (jnp.asarray(np.concatenate([cr, cc], axis=1), F32), jnp.asarray(np.concatenate([sr, sc], axis=1), F32))


def _adamw(w, g, m, v, name, after=None, copy_g=False):
    R, C = w.shape
    tr = _row_tile(R, C)
    c1 = 1.0 - ADAM_B1 ** ADAM_STEP
    c2 = 1.0 - ADAM_B2 ** ADAM_STEP
    deps = [] if after is None else [after]
    n_out = 4 if copy_g else 3

    def body(w_ref, g_ref, m_ref, v_ref, *rest):
        d_ref, mo_ref, vo_ref = rest[len(deps):len(deps) + 3]
        gv = g_ref[...]
        mn = ADAM_B1 * m_ref[...] + (1.0 - ADAM_B1) * gv
        vn = ADAM_B2 * v_ref[...] + (1.0 - ADAM_B2) * (gv * gv)
        mo_ref[...] = mn
        vo_ref[...] = vn
        d_ref[...] = -ADAM_LR * ((mn / c1) / (jnp.sqrt(vn / c2) + ADAM_EPS) + ADAM_WD * w_ref[...])
        if copy_g:
            rest[-1][...] = gv

    blk = pl.BlockSpec((tr, C), lambda i: (i, 0))
    return _pcall(
        body, name=name, grid=(R // tr,),
        in_specs=[blk] * 4 + [_ANY] * len(deps), out_specs=[blk] * n_out,
        out_shape=[jax.ShapeDtypeStruct((R, C), F32)] * n_out,
        compiler_params=_params(("parallel",)),
    )(w, g, m, v, *deps)


PACK_W = 1024


def _pack(parts):
    flat, offs, pos = [], [], 0
    for a in parts:
        n = a.size
        padn = -n % PACK_W
        flat.append(jnp.pad(a.reshape(-1).astype(F32), (0, padn)))
        offs.append((pos, n, a.shape))
        pos += n + padn
    tail = -pos % (8 * PACK_W)
    if tail:
        flat.append(jnp.zeros((tail,), F32))
    return jnp.concatenate(flat).reshape(-1, PACK_W), offs


def _unpack(buf, offs, i):
    pos, n, shape = offs[i]
    return buf.reshape(buf.shape[:-2] + (-1,))[..., pos:pos + n].reshape(buf.shape[:-2] + shape)


def kernel(x, c, ctx, c_ctx, ada_w, ada_b, norm1_g, norm2_g, w_in, hgrn_lb_logits, hgrn_norm_g, na_q_norm_g, na_k_norm_g, na_rel_bias, w_branch_a, w_branch_b, w_out, ffn_w1, ffn_w3, ffn_conv_w, ffn_conv_b, ffn_w2, loss_target, m_c_ctx, m_ada_w, m_ada_b, m_norm1_g, m_norm2_g, m_w_in, m_hgrn_lb_logits, m_hgrn_norm_g, m_na_q_norm_g, m_na_k_norm_g, m_na_rel_bias, m_w_branch_a, m_w_branch_b, m_w_out, m_ffn_w1, m_ffn_w3, m_ffn_conv_w, m_ffn_conv_b, m_ffn_w2, v_c_ctx, v_ada_w, v_ada_b, v_norm1_g, v_norm2_g, v_w_in, v_hgrn_lb_logits, v_hgrn_norm_g, v_na_q_norm_g, v_na_k_norm_g, v_na_rel_bias, v_w_branch_a, v_w_branch_b, v_w_out, v_ffn_w1, v_ffn_w3, v_ffn_conv_w, v_ffn_conv_b, v_ffn_w2):
    weights = dict(c_ctx=c_ctx, ada_w=ada_w, ada_b=ada_b, norm1_g=norm1_g, norm2_g=norm2_g, w_in=w_in,
                   hgrn_lb_logits=hgrn_lb_logits, hgrn_norm_g=hgrn_norm_g, na_q_norm_g=na_q_norm_g,
                   na_k_norm_g=na_k_norm_g, na_rel_bias=na_rel_bias, w_branch_a=w_branch_a, w_branch_b=w_branch_b,
                   w_out=w_out, ffn_w1=ffn_w1, ffn_w3=ffn_w3, ffn_conv_w=ffn_conv_w, ffn_conv_b=ffn_conv_b,
                   ffn_w2=ffn_w2)
    moms = dict(c_ctx=(m_c_ctx, v_c_ctx), ada_w=(m_ada_w, v_ada_w), ada_b=(m_ada_b, v_ada_b),
                norm1_g=(m_norm1_g, v_norm1_g), norm2_g=(m_norm2_g, v_norm2_g), w_in=(m_w_in, v_w_in),
                hgrn_lb_logits=(m_hgrn_lb_logits, v_hgrn_lb_logits), hgrn_norm_g=(m_hgrn_norm_g, v_hgrn_norm_g),
                na_q_norm_g=(m_na_q_norm_g, v_na_q_norm_g), na_k_norm_g=(m_na_k_norm_g, v_na_k_norm_g),
                na_rel_bias=(m_na_rel_bias, v_na_rel_bias), w_branch_a=(m_w_branch_a, v_w_branch_a),
                w_branch_b=(m_w_branch_b, v_w_branch_b), w_out=(m_w_out, v_w_out), ffn_w1=(m_ffn_w1, v_ffn_w1),
                ffn_w3=(m_ffn_w3, v_ffn_w3), ffn_conv_w=(m_ffn_conv_w, v_ffn_conv_w),
                ffn_conv_b=(m_ffn_conv_b, v_ffn_conv_b), ffn_w2=(m_ffn_w2, v_ffn_w2))
    order = list(weights)

    L, D = x.shape[1], x.shape[2]
    N = ctx.shape[1]
    T = N + L
    HA = w_branch_a.shape[1]
    HB = w_branch_b.shape[1]
    F = ffn_conv_b.shape[1]
    IN = 5 * HA + 3 * HB + 2 * D
    n_ada = ada_w.shape[2]
    ix, iy, ic = _pos()
    chip = 2 * ix + iy
    dev = 2 * chip + ic

    _PENDING.clear()
    pk0, offs0 = _pack([c[0], hgrn_lb_logits, ffn_conv_w[0]])
    g0 = _allgather8(pk0, "gather_small0")
    c_all = _unpack(g0, offs0, 0)
    lbl_parts = _unpack(g0, offs0, 1)
    lbl = jnp.concatenate([lbl_parts[2 * j] for j in range(N_CHIP)], axis=-1)
    cw_parts = _unpack(g0, offs0, 2)
    cw = jnp.concatenate([cw_parts[2 * j] for j in range(N_CHIP)], axis=-1)
    cw8 = jnp.pad(cw, ((0, 5), (0, 0)))

    cs = jnp.concatenate([c_all, c_ctx[None, :], jnp.zeros((7, D), F32)], axis=0)
    ada_b_mine = lax.dynamic_slice(ada_b, (0, chip * n_ada), (1, n_ada))
    mod_mine = _ada_fwd(cs, ada_w[0], ada_b_mine)
    gm = _allgather8(mod_mine, "gather_mod")
    mod = jnp.concatenate([gm[2 * j] for j in range(N_CHIP)], axis=-1)
    mod_l = lax.dynamic_slice(mod, (dev, 0), (1, N_MOD * D)).reshape(N_MOD, D)
    mod_c = mod[8].reshape(N_MOD, D)
    sh1, sc1, g1, sh2, sc2, g2 = [mod_l[i:i + 1] for i in range(N_MOD)]
    shift1 = jnp.concatenate([mod_c[0:1], sh1], axis=0)
    scale1 = jnp.concatenate([mod_c[1:2], sc1], axis=0)

    shards = [w_in[0], w_branch_a[0], w_branch_b[0], w_out[0], ffn_w1[0], ffn_w3[0], ffn_w2[0]]
    names = ["w_in", "w_a", "w_b", "w_out", "w1", "w3", "w2"]
    slots = [_cast_bf16_slot(s, "cast_" + nm) for s, nm in zip(shards, names)]
    sem_nb, win_buf = _xfer_start("gather_ici_start_in_nbr", slots[0:1], _plan_gather_ici(NEIGHBOURS), 2, gm)

    xall = jnp.concatenate([ctx[0], x[0]], axis=0)
    h_all = _rms1_fwd(xall, norm1_g, shift1, scale1, N)
    chip_i = chip.astype(jnp.int32)
    same = lambda ids: jnp.stack([jnp.stack(ids), jnp.stack(ids)])
    p = _mm_nn_sel(h_all, win_buf[0], same([chip_i]), F32, "mm_p_own")
    bias = _expand_bias(na_rel_bias[0])
    win_buf = _xfer_wait("gather_ici_wait_in_nbr", sem_nb, win_buf, _plan_gather_ici(NEIGHBOURS),
                         (p, bias, *slots[1:]))
    sem_nb, win_buf = _xfer_start("gather_d2d_start_in_nbr", win_buf, _plan_gather_d2d(NEIGHBOURS), 2)
    sem_dg, win_buf = _xfer_start("gather_ici_start_in_diag", win_buf, _plan_gather_ici(DIAGONAL), 1)
    gat_mix = _gather_start("mix", slots[1:4])
    gat_ffn = _gather_start("ffn", slots[4:6])
    gat_ffn2 = _gather_start("ffn2", slots[6:7])
    win_buf = _xfer_wait("gather_d2d_wait_in_nbr", sem_nb, win_buf, _plan_gather_d2d(NEIGHBOURS), _PENDING[0])
    p = _mm_nn_sel(h_all, win_buf[0], same([chip_i ^ 1, chip_i ^ 2]), F32, "mm_p_nbr", p)
    win_buf = _d2d_hand_over("in_diag", sem_dg, win_buf, DIAGONAL, p)
    p = _mm_nn_sel(h_all, win_buf[0], same([chip_i ^ 3]), F32, "mm_p_diag", p)
    Win = win_buf[0]
    cos, sin = _rope_tables(L)
    off_na = 5 * HA
    y_b = _na_fwd(p, bias, na_q_norm_g, na_k_norm_g, cos, sin, N, off_na, HB)
    gat_mix = _gather_mid(gat_mix, y_b)
    y_a, o_a, st_a = _hgrn_fwd(p, lbl, hgrn_norm_g, N, HA)
    Wa, Wb, Wo = _gather_finish(gat_mix, (y_a, y_b))
    Wo = Wo.reshape(1, D, D)
    za = _mm_nn(y_a, Wa, BF16, "mm_za")
    zb = _mm_nn(y_b, Wb, BF16, "mm_zb")
    off_ga, off_gb = 5 * HA + 3 * HB, 5 * HA + 3 * HB + D
    z = _merge_fwd(za, zb, p, N, off_ga, off_gb)
    gat_ffn = _gather_mid(gat_ffn, z)
    mo = _mm_nn(z, Wo, F32, "mm_mo")
    vec2 = jnp.concatenate([g1, norm2_g, sh2, sc2, jnp.zeros((4, D), F32)], axis=0)
    x_mid, h2 = _resid_rms2_fwd(x[0], mo, vec2)
    W1, W3 = _gather_finish(gat_ffn, h2)
    gat_ffn2 = _gather_mid(gat_ffn2, h2)
    u1 = _mm_nn(h2, W1, BF16, "mm_u1")
    u3 = _mm_nn(h2, W3, BF16, "mm_u3")
    (W2,) = _gather_finish(gat_ffn2, (u1, u3))
    W2 = W2.reshape(1, F, D)
    a = _convgate_fwd(u1, u3, cw8, ffn_conv_b)
    f = _mm_nn(a, W2, F32, "mm_f")
    dy, df, s_loss = _loss_head(x_mid, f, g2, loss_target[0])
    loss = lax.psum(s_loss[1, 0], ("x", "y", "c"))
    d_g2 = s_loss[0:1]

    gW2 = _mm_tn(a, df, 1, "mm_gw2").reshape(N_CHIP, F // N_CHIP, D)
    da = _mm_nt(df, W2, BF16, "mm_da")
    du1, du3, s_conv = _convgate_bwd(u1, u3, da, cw8, ffn_conv_b)
    gW1 = _mm_tn(h2, du1, N_CHIP, "mm_gw1")
    gW3 = _mm_tn(h2, du3, N_CHIP, "mm_gw3")
    rs_ffn = _rs_start("ffn", [gW2, gW1, gW3])
    dh2a = _mm_nt(du1, W1, F32, "mm_dh2a")
    dh2b = _mm_nt(du3, W3, F32, "mm_dh2b")
    rs_ffn = _rs_scatter(rs_ffn, dh2b)
    dxm, dmo, s_rms2 = _resid_rms2_bwd(x_mid, dh2a, dh2b, dy, mo, vec2)
    gWo = _mm_tn(z, dmo, 1, "mm_gwo").reshape(N_CHIP, D // N_CHIP, D)
    dz = _mm_nt(dmo, Wo, BF16, "mm_dz")
    dza, dzb, dga, dgb = _merge_bwd(dz, za, zb, p, N, off_ga, off_gb)
    gWa = _mm_tn(y_a, dza, N_CHIP, "mm_gwa")
    gWb = _mm_tn(y_b, dzb, N_CHIP, "mm_gwb")
    rs_mix = _rs_start("mix", [gWo, gWa, gWb])
    dya = _mm_nt(dza, Wa, F32, "mm_dya")
    dyb = _mm_nt(dzb, Wb, BF16, "mm_dyb")
    rs_mix = _rs_scatter(rs_mix, dyb)
    dq_a, dzf, dzbk, di_a, dog, dlbl, s_ng = _hgrn_bwd(p, lbl, hgrn_norm_g, o_a, dya, st_a, N, HA)
    rs_ffn = _rs_join(rs_ffn, dq_a)
    dq_n, dk_n, dv_n, dbias, s_qk = _na_bwd(p, bias, na_q_norm_g, na_k_norm_g, cos, sin, dyb, N, off_na, HB)
    rs_mix = _rs_join(rs_mix, dq_n)
    dp = jnp.concatenate([dq_a, dzf, dzbk, di_a, dog, dq_n, dk_n, dv_n, dga, dgb], axis=1)
    gWin = _mm_tn(h_all, dp, N_CHIP, "mm_gwin")
    rs_in = _rs_start("in", [gWin])
    rs_in = _rs_scatter(rs_in, _PENDING[0])
    dh = _mm_nt(dp, Win, F32, "mm_dh")
    grad_x, s_rms1 = _rms1_bwd(xall, dh, dxm, norm1_g, scale1, N)
    d_table = _bias_grad(dbias)

    grads = {}
    big_names = ["ada_w", "w_in", "w_branch_a", "w_branch_b", "w_out", "ffn_w1", "ffn_w3", "ffn_w2"]
    small_names = [n for n in order if n not in big_names]
    delta, new_m, new_v = {}, {}, {}

    def update(nm, after=None):
        reduced = nm != "ada_w"
        d_, m_, v_, *g_ = _adamw(weights[nm][0], grads[nm][0], moms[nm][0][0], moms[nm][1][0], "adamw_" + nm,
                                 after, copy_g=reduced)
        delta[nm], new_m[nm], new_v[nm] = d_[None], m_[None], v_[None]
        if reduced:
            grads[nm] = g_[0][None]
        return d_

    last = grad_x
    for nm, g in zip(["ffn_w2", "ffn_w1", "ffn_w3"], _rs_finish(rs_ffn, last)):
        grads[nm] = g[None]
        last = update(nm, last)
    for nm, g in zip(["w_out", "w_branch_a", "w_branch_b"], _rs_finish(rs_mix, last)):
        grads[nm] = g[None]
        last = update(nm, last)
    rs_in = _rs_join(rs_in, last)

    zD = jnp.zeros((1, D), F32)
    dmod_l = jnp.concatenate([s_rms1[2:3], s_rms1[3:4], s_rms2[3:4], s_rms2[0:1], s_rms2[1:2], d_g2], axis=0)
    dmod_c = jnp.concatenate([s_rms1[0:1], s_rms1[1:2], zD, zD, zD, zD], axis=0)
    pk1, offs1 = _pack([dmod_l, dmod_c, s_rms1[4], s_rms2[2], dlbl, s_ng[0], s_qk[0], s_qk[1], d_table,
                        s_conv[0:3], s_conv[3]])
    g1all = _allgather8(pk1, "gather_small1")
    tot1 = _sum8(g1all, "sum_small1")
    dmod_rows = _unpack(g1all, offs1, 0).reshape(N_DEV, N_MOD * D)
    dmod_c_tot = _unpack(tot1, offs1, 1).reshape(1, N_MOD * D)
    dmod16 = jnp.concatenate([dmod_rows, dmod_c_tot, jnp.zeros((7, N_MOD * D), F32)], axis=0)
    dmod16_mine = lax.dynamic_slice(dmod16, (0, chip * n_ada), (16, n_ada))
    g_ada_w, dact = _ada_bwd(cs, ada_w[0], dmod16_mine)
    pk2, offs2 = _pack([dact[8]])
    g2all = _allgather8(pk2, "gather_small2")
    dact_rows = _unpack(g2all, offs2, 0)
    dact_sel = jnp.concatenate([dact_rows[2 * j][None] for j in range(N_CHIP)] + [jnp.zeros((4, D), F32)], axis=0)

    grads["ada_w"] = g_ada_w[None]
    grads["ada_b"] =(_unpack(tot1, offs1, 0) + _unpack(tot1, offs1, 1)).reshape(1, N_MOD * D)
    grads["norm1_g"] = _unpack(tot1, offs1, 2)[None]
    grads["norm2_g"] = _unpack(tot1, offs1, 3)[None]
    g_lbl = _unpack(tot1, offs1, 4)
    n_lb = HA // N_CHIP
    grads["hgrn_lb_logits"] = lax.dynamic_slice(g_lbl, (0, 0, chip * n_lb), (2, 2, n_lb))
    grads["hgrn_norm_g"] = _unpack(tot1, offs1, 5)[None]
    grads["na_q_norm_g"] = _unpack(tot1, offs1, 6)[None]
    grads["na_k_norm_g"] = _unpack(tot1, offs1, 7)[None]
    grads["na_rel_bias"] = _unpack(tot1, offs1, 8)[None]
    g_cw = _unpack(tot1, offs1, 9)
    n_f = F // N_CHIP
    grads["ffn_conv_w"] = lax.dynamic_slice(g_cw, (0, chip * n_f), (3, n_f))[None]
    grads["ffn_conv_b"] = _unpack(tot1, offs1, 10)[None]

    g_c_ctx = _dsilu_rows(dact_sel, c_ctx[None, :], "grad_c_ctx")
    grads["c_ctx"] = g_c_ctx[0]

    last = update("ada_w", g_c_ctx)
    pw, offw = _pack([weights[n] for n in small_names])
    pg, _ = _pack([grads[n] for n in small_names])
    pm, _ = _pack([moms[n][0] for n in small_names])
    pv, _ = _pack([moms[n][1] for n in small_names])
    d_, m_, v_ = _adamw(pw, pg, pm, pv, "adamw_small", last)
    for i, nm in enumerate(small_names):
        delta[nm], new_m[nm], new_v[nm] = _unpack(d_, offw, i), _unpack(m_, offw, i), _unpack(v_, offw, i)
    grads["w_in"] = _rs_finish(rs_in, d_)[0][None]
    update("w_in")

    return (loss, grad_x[None], *[grads[n] for n in order], *[delta[n] for n in order],
            *[new_m[n] for n in order], *[new_v[n] for n in order])


def _dsilu_rows(v, cv, name):
    D = v.shape[1]

    def body(v_ref, c_ref, o_ref):
        t = c_ref[...]
        s = _sigmoid(t)
        o_ref[...] = (((v_ref[0:1, :] + v_ref[1:2, :]) + v_ref[2:3, :]) + v_ref[3:4, :]) * (s * (1.0 + t * (1.0 - s)))

    return _pcall(body, name=name, out_shape=jax.ShapeDtypeStruct((1, D), F32),
                          compiler_params=_params())(v, cv)
```

```python
import functools

import numpy as np
import jax
import jax.numpy as jnp
from jax import lax
from jax.experimental import pallas as pl
from jax.experimental.pallas import tpu as pltpu

F32 = jnp.float32
BF16 = jnp.bfloat16
MESH = pl.DeviceIdType.MESH

HEAD = 128
GRID_W = 64
WIN_R = 8
WIN_C = 16
ROPE_THETA = 10000.0
EPS = 1e-6
N_MOD = 6
CHUNK = 16
HGRN_UNROLL = 4
ADAM_LR = 0.001
ADAM_B1 = 0.9
ADAM_B2 = 0.999
ADAM_EPS = 1e-08
ADAM_WD = 0.01
ADAM_STEP = 10
NEG = -1e30
VMEM_LIMIT = 56 * 1024 * 1024
N_DEV = 8
N_CHIP = 4
HI = lax.Precision.HIGHEST


def _pick(n, cands):
    for c in cands:
        if n % c == 0:
            return c
    return n


def _row_tile(rows, cols, target_bytes=1 << 20):
    want = max(16, target_bytes // (4 * cols))
    for t in (512, 256, 128, 64, 32, 16, 8):
        if t <= want and rows % t == 0:
            return t
    return rows


def _params(sem=None):
    return pltpu.CompilerParams(dimension_semantics=sem, vmem_limit_bytes=VMEM_LIMIT)


def _dot(a, b):
    return jnp.dot(a, b, preferred_element_type=F32)


def _dot_nt(a, b):
    return lax.dot_general(a, b, (((1,), (1,)), ((), ())), preferred_element_type=F32)


def _dot_tn(a, b):
    return lax.dot_general(a, b, (((0,), (0,)), ((), ())), preferred_element_type=F32)


def _sigmoid(x):
    return 1.0 / (1.0 + jnp.exp(-x))


def _col_tile(n):
    return n if n <= 1536 else _pick(n, (1024, 768, 512, 384, 256, 128))


def _mm_nn(x, w3, out_dtype, name):
    M, K = x.shape
    S, _, n = w3.shape
    tm = _pick(M, ((1024,) if K <= 2048 else ()) + (768, 512, 256, 128, 64))
    tn = _col_tile(n)
    nb = n // tn

    def body(x_ref, w_ref, o_ref):
        o_ref[...] = _dot(x_ref[...].astype(BF16), w_ref[0]).astype(o_ref.dtype)

    return _pcall(
        body, name=name, grid=(M // tm, S * nb),
        in_specs=[pl.BlockSpec((tm, K), lambda i, j: (i, 0)),
                  pl.BlockSpec((1, K, tn), lambda i, j: (j // nb, 0, j % nb))],
        out_specs=pl.BlockSpec((tm, tn), lambda i, j: (i, j)),
        out_shape=jax.ShapeDtypeStruct((M, S * n), out_dtype),
        compiler_params=_params(("parallel", "parallel")),
    )(x, w3)


def _mm_nn_sel(x, w3, sel, out_dtype, name, prev=None):
    M, K = x.shape
    S, _, n = w3.shape
    tm = _pick(M, (768, 512, 256, 128, 64))
    tn = _col_tile(n)
    nb = n // tn
    k = sel.shape[1]

    def body(sel_ref, x_ref, w_ref, *rest):
        rest[-1][...] = _dot(x_ref[...].astype(BF16), w_ref[0]).astype(out_dtype)

    in_specs = [pl.BlockSpec((tm, K), lambda i, j, sel_ref: (i, 0)),
                pl.BlockSpec((1, K, tn), lambda i, j, sel_ref: (sel_ref[0, j // nb], 0, j % nb))]
    operands = [sel, x, w3]
    if prev is not None:
        in_specs.append(_ANY)
        operands.append(prev)
    return pl.pallas_call(
        body, name=name,
        grid_spec=pltpu.PrefetchScalarGridSpec(
            num_scalar_prefetch=1, grid=(M // tm, k * nb), in_specs=in_specs,
            out_specs=pl.BlockSpec((tm, tn), lambda i, j, sel_ref: (i, sel_ref[1, j // nb] * nb + j % nb))),
        out_shape=jax.ShapeDtypeStruct((M, S * n), out_dtype),
        input_output_aliases={} if prev is None else {3: 0},
        compiler_params=_params(("parallel", "parallel")),
    )(*operands)


def _mm_nt(dy, w3, out_dtype, name):
    M = dy.shape[0]
    S, K, n = w3.shape
    tm = _pick(M, (1024, 768, 512, 256, 128, 64))
    tk = K if K <= 2048 else _pick(K, (1408, 1024, 512, 256, 128))
    tc = n if n <= 2048 else _col_tile(n)
    nb = n // tc
    nsteps = S * nb

    def body(dy_ref, w_ref, o_ref, acc_ref):
        s = pl.program_id(2)

        @pl.when(s == 0)
        def _():
            acc_ref[...] = jnp.zeros_like(acc_ref)

        acc_ref[...] += _dot_nt(dy_ref[...].astype(BF16), w_ref[0])

        @pl.when(s == nsteps - 1)
        def _():
            o_ref[...] = acc_ref[...].astype(o_ref.dtype)

    return _pcall(
        body, name=name, grid=(M // tm, K // tk, nsteps),
        in_specs=[pl.BlockSpec((tm, tc), lambda i, k, s: (i, s)),
                  pl.BlockSpec((1, tk, tc), lambda i, k, s: (s // nb, k, s % nb))],
        out_specs=pl.BlockSpec((tm, tk), lambda i, k, s: (i, k)),
        out_shape=jax.ShapeDtypeStruct((M, K), out_dtype),
        scratch_shapes=[pltpu.VMEM((tm, tk), F32)],
        compiler_params=_params(("parallel", "parallel", "arbitrary")),
    )(dy, w3)


def _mm_tn(x, dy, S, name):
    M, K = x.shape
    n = dy.shape[1] // S
    tk = _pick(K, (512, 256, 128))
    tn = _col_tile(n)
    nb = n // tn

    def body(x_ref, dy_ref, o_ref):
        o_ref[0] = _dot_tn(x_ref[...].astype(BF16), dy_ref[...].astype(BF16)).astype(BF16)

    return _pcall(
        body, name=name, grid=(S * nb, K // tk),
        in_specs=[pl.BlockSpec((M, tk), lambda j, k: (0, k)),
                  pl.BlockSpec((M, tn), lambda j, k: (0, j))],
        out_specs=pl.BlockSpec((1, tk, tn), lambda j, k: (j // nb, k, j % nb)),
        out_shape=jax.ShapeDtypeStruct((S, K, n), BF16),
        compiler_params=_params(("parallel", "parallel")),
    )(x, dy)


def _chip_index():
    return (2 * lax.axis_index("x") + lax.axis_index("y")).astype(jnp.int32).reshape(1)


def _cast_bf16_slot(w, name):
    R, C = w.shape
    tr = _row_tile(R, C, 2 << 20)

    def body(j_ref, w_ref, o_ref):
        o_ref[0] = w_ref[...].astype(BF16)

    return _pcall(
        body, name=name,
        grid_spec=pltpu.PrefetchScalarGridSpec(
            num_scalar_prefetch=1, grid=(R // tr,),
            in_specs=[pl.BlockSpec((tr, C), lambda i, j_ref: (i, 0))],
            out_specs=pl.BlockSpec((1, tr, C), lambda i, j_ref: (j_ref[0], i, 0))),
        out_shape=jax.ShapeDtypeStruct((N_CHIP, R, C), BF16),
        compiler_params=_params(("parallel",)),
    )(_chip_index(), w)


def _pos():
    return lax.axis_index("x"), lax.axis_index("y"), lax.axis_index("c")


def _other_chips(x, y):
    return [(x, 1 - y), (1 - x, y), (1 - x, 1 - y)]


def _allgather8(v, name):
    R, C = v.shape

    def body(x_ref, out_ref, send_sems, recv_sems, local_sem):
        x, y, c = _pos()
        me, sibling = (x, y, c), (x, y, 1 - c)
        chips = _other_chips(x, y)

        def slot(px, py, pc):
            return out_ref.at[4 * px + 2 * py + pc]

        def copy(k, block, to, src=None):
            return pltpu.make_async_remote_copy(
                src_ref=slot(*block) if src is None else src, dst_ref=slot(*block),
                send_sem=send_sems.at[k], recv_sem=recv_sems.at[k], device_id=to, device_id_type=MESH)

        mine = pltpu.make_async_copy(x_ref, slot(*me), local_sem)
        mine.start()
        first = [copy(0, me, sibling, src=x_ref)]
        first += [copy(1 + j, me, (*chip, c), src=x_ref) for j, chip in enumerate(chips)]
        for cp in first:
            cp.start()
        passed = [copy(4 + j, (*chip, c), sibling) for j, chip in enumerate(chips)]
        for j, chip in enumerate(chips):
            copy(1 + j, (*chip, c), me).wait_recv()
            passed[j].start()
        copy(0, sibling, me).wait_recv()
        for j, chip in enumerate(chips):
            copy(4 + j, (*chip, 1 - c), me).wait_recv()
        for cp in first + passed:
            cp.wait_send()
        mine.wait()

    return _pcall(
        body, name=name,
        out_shape=jax.ShapeDtypeStruct((N_DEV, R, C), v.dtype),
        in_specs=[pl.BlockSpec(memory_space=pltpu.VMEM)],
        out_specs=pl.BlockSpec(memory_space=pltpu.VMEM),
        scratch_shapes=[pltpu.SemaphoreType.DMA((7,)), pltpu.SemaphoreType.DMA((7,)), pltpu.SemaphoreType.DMA],
        compiler_params=pltpu.CompilerParams(vmem_limit_bytes=VMEM_LIMIT),
    )(v)


_HBM = pl.BlockSpec(memory_space=pltpu.HBM)
_SEM = pl.BlockSpec(memory_space=pltpu.SEMAPHORE)
_ANY = pl.BlockSpec(memory_space=pl.ANY)
_EFFECT = pltpu.SideEffectType.DATAFLOW_SIDE_EFFECTING
_PENDING = []


def _pcall(body, **kw):
    def run(*operands):
        if not _PENDING or "in_specs" not in kw:
            return pl.pallas_call(body, **kw)(*operands)
        deps = list(_PENDING)
        n = len(operands)

        def tied(*refs):
            return body(*refs[:n], *refs[n + len(deps):])

        return pl.pallas_call(tied, **{**kw, "in_specs": list(kw["in_specs"]) + [_ANY] * len(deps)})(*operands, *deps)
    return run


def _copies(plan, refs, send_sems, recv_sems):
    return [pltpu.make_async_remote_copy(src_ref=src, dst_ref=dst, send_sem=send_sems.at[k], recv_sem=recv_sems.at[k],
                                         device_id=dev, device_id_type=MESH)
            for k, (src, dst, dev) in enumerate(plan(refs))]


def _xfer_start(name, bufs, plan, n_copies, after=None):
    n = len(bufs)
    deps = list(_PENDING) + ([after] if after is not None else [])
    nd = len(deps)

    def body(*refs):
        for cp in _copies(plan, refs[:n], refs[n + nd], refs[n + nd + 1]):
            cp.start()
        refs[-1][...] = jnp.zeros_like(refs[-1])

    outs = pl.pallas_call(
        body, name=name,
        out_shape=(pltpu.SemaphoreType.DMA((n_copies,)), pltpu.SemaphoreType.DMA((n_copies,)),
                   *[pltpu.HBM(b.shape, b.dtype) for b in bufs], jax.ShapeDtypeStruct((8, 128), F32)),
        in_specs=[_HBM] * n + [_ANY] * nd,
        out_specs=(_SEM, _SEM, *[_HBM] * n, pl.BlockSpec(memory_space=pltpu.VMEM)),
        input_output_aliases={t: 2 + t for t in range(n)},
        compiler_params=pltpu.CompilerParams(has_side_effects=_EFFECT),
    )(*[pltpu.with_memory_space_constraint(b, pltpu.HBM) for b in bufs], *deps)
    _PENDING[:] = [outs[-1]]
    return (outs[0], outs[1]), list(outs[2:2 + n])


def _xfer_wait(name, sems, bufs, plan, after):
    n = len(bufs)
    after = tuple(after) if isinstance(after, (tuple, list)) else (after,)

    def body(*refs):
        cps = _copies(plan, refs[:n], refs[n], refs[n + 1])
        for cp in cps:
            cp.wait_send()
        for cp in cps:
            cp.wait_recv()

    outs = pl.pallas_call(
        body, name=name,
        out_shape=tuple(pltpu.HBM(b.shape, b.dtype) for b in bufs),
        in_specs=[_HBM] * n + [_SEM, _SEM] + [_ANY] * len(after),
        out_specs=tuple([_HBM] * n),
        input_output_aliases={t: t for t in range(n)},
        compiler_params=pltpu.CompilerParams(has_side_effects=_EFFECT),
    )(*bufs, sems[0], sems[1], *after)
    return list(outs)


def _half(ref_rows, hc):
    h = ref_rows // 2
    return pl.ds(hc * h, h)


ALL_CHIPS = (0, 1, 2)
NEIGHBOURS = (0, 1)
DIAGONAL = (2,)


def _plan_gather_ici(which):
    def plan(bufs):
        x, y, c = _pos()
        j = 2 * x + y
        chips = _other_chips(x, y)
        return [(b.at[j, _half(b.shape[1], c)], b.at[j, _half(b.shape[1], c)], (*chips[k], c))
                for b in bufs for k in which]
    return plan


def _plan_gather_d2d(which):
    def plan(bufs):
        x, y, c = _pos()
        chips = _other_chips(x, y)
        out = []
        for b in bufs:
            for k in which:
                blk = b.at[2 * chips[k][0] + chips[k][1], _half(b.shape[1], c)]
                out.append((blk, blk, (x, y, 1 - c)))
        return out
    return plan


def _plan_pair_swap(n):
    def plan(bufs):
        x, y, c = _pos()
        return [(g.at[:, _half(g.shape[1], 1 - c)], land, (x, y, 1 - c)) for g, land in zip(bufs[:n], bufs[n:])]
    return plan


def _plan_chip_scatter(n):
    def plan(bufs):
        x, y, c = _pos()
        return [(p.at[2 * chip[0] + chip[1]], land.at[k], (*chip, c))
                for p, land in zip(bufs[:n], bufs[n:]) for k, chip in enumerate(_other_chips(x, y))]
    return plan


def _plan_pair_join(bufs):
    x, y, c = _pos()
    return [(b.at[_half(b.shape[0], c)], b.at[_half(b.shape[0], c)], (x, y, 1 - c)) for b in bufs]


def _empty_hbm(shape, dtype):
    return pltpu.with_memory_space_constraint(lax.empty(shape, dtype), pltpu.HBM)


def _gather_start(tag, bufs, after=None):
    sems, bufs = _xfer_start(f"gather_ici_start_{tag}", bufs, _plan_gather_ici(ALL_CHIPS), 3 * len(bufs), after)
    return dict(tag=tag, sems=sems, bufs=bufs)


def _gather_mid(st, after):
    tag = st["tag"]
    bufs = _xfer_wait(f"gather_ici_wait_{tag}", st["sems"], st["bufs"], _plan_gather_ici(ALL_CHIPS), after)
    sems, bufs = _xfer_start(f"gather_d2d_start_{tag}", bufs, _plan_gather_d2d(ALL_CHIPS), 3 * len(bufs))
    return dict(tag=tag, sems=sems, bufs=bufs)


def _gather_finish(st, after):
    return _xfer_wait(f"gather_d2d_wait_{st['tag']}", st["sems"], st["bufs"], _plan_gather_d2d(ALL_CHIPS), after)


def _d2d_hand_over(tag, sems, bufs, which, after):
    bufs = _xfer_wait(f"gather_ici_wait_{tag}", sems, bufs, _plan_gather_ici(which), after)
    sems, bufs = _xfer_start(f"gather_d2d_start_{tag}", bufs, _plan_gather_d2d(which), len(which) * len(bufs))
    return _xfer_wait(f"gather_d2d_wait_{tag}", sems, bufs, _plan_gather_d2d(which), after)


def _pair_add(g, r, name):
    S, R, C = g.shape
    h = R // 2
    tr = _row_tile(h, C)
    nb = h // tr

    def body(c_ref, g_ref, r_ref, o_ref):
        o_ref[...] = (g_ref[...].astype(F32) + r_ref[...].astype(F32)).astype(BF16)

    return _pcall(
        body, name=name,
        grid_spec=pltpu.PrefetchScalarGridSpec(
            num_scalar_prefetch=1, grid=(S, nb),
            in_specs=[pl.BlockSpec((1, tr, C), lambda s, i, c_ref: (s, c_ref[0] * nb + i, 0)),
                      pl.BlockSpec((1, tr, C), lambda s, i, c_ref: (s, i, 0))],
            out_specs=pl.BlockSpec((1, tr, C), lambda s, i, c_ref: (s, i, 0))),
        out_shape=jax.ShapeDtypeStruct((S, h, C), BF16),
        compiler_params=_params(("parallel", "parallel")),
    )(lax.axis_index("c").astype(jnp.int32).reshape(1), g, r)


def _chip_sum(p, rb, name):
    S, h, C = p.shape
    tr = _row_tile(h, C)
    nb = h // tr
    jc = jnp.concatenate([_chip_index(), lax.axis_index("c").astype(jnp.int32).reshape(1)])

    def body(jc_ref, p_ref, r_ref, o_ref):
        o_ref[...] = ((p_ref[0].astype(F32) + r_ref[0].astype(F32)) + r_ref[1].astype(F32)) + r_ref[2].astype(F32)

    return _pcall(
        body, name=name,
        grid_spec=pltpu.PrefetchScalarGridSpec(
            num_scalar_prefetch=1, grid=(nb,),
            in_specs=[pl.BlockSpec((1, tr, C), lambda i, jc_ref: (jc_ref[0], i, 0)),
                      pl.BlockSpec((3, tr, C), lambda i, jc_ref: (0, i, 0))],
            out_specs=pl.BlockSpec((tr, C), lambda i, jc_ref: (jc_ref[1] * nb + i, 0))),
        out_shape=jax.ShapeDtypeStruct((2 * h, C), F32),
        compiler_params=_params(("parallel",)),
    )(jc, p, rb)


def _rs_start(tag, gs):
    n = len(gs)
    lands = [_empty_hbm((g.shape[0], g.shape[1] // 2, g.shape[2]), g.dtype) for g in gs]
    sems, bufs = _xfer_start(f"rs_swap_start_{tag}", list(gs) + lands, _plan_pair_swap(n), n)
    return dict(tag=tag, n=n, sems=sems, bufs=bufs)


def _rs_scatter(st, after):
    tag, n = st["tag"], st["n"]
    bufs = _xfer_wait(f"rs_swap_wait_{tag}", st["sems"], st["bufs"], _plan_pair_swap(n), after)
    ps = [_pair_add(g, r, f"rs_pair_add_{tag}{t}") for t, (g, r) in enumerate(zip(bufs[:n], bufs[n:]))]
    lands = [_empty_hbm((3,) + p.shape[1:], p.dtype) for p in ps]
    sems, bufs = _xfer_start(f"rs_scatter_start_{tag}", ps + lands, _plan_chip_scatter(n), 3 * n)
    return dict(tag=tag, n=n, sems=sems, bufs=bufs)


def _rs_join(st, after):
    tag, n = st["tag"], st["n"]
    bufs = _xfer_wait(f"rs_scatter_wait_{tag}", st["sems"], st["bufs"], _plan_chip_scatter(n), after)
    fs = [_chip_sum(p, rb, f"rs_chip_sum_{tag}{t}") for t, (p, rb) in enumerate(zip(bufs[:n], bufs[n:]))]
    sems, bufs = _xfer_start(f"rs_join_start_{tag}", fs, _plan_pair_join, n)
    return dict(tag=tag, n=n, sems=sems, bufs=bufs)


def _rs_finish(st, after):
    return _xfer_wait(f"rs_join_wait_{st['tag']}", st["sems"], st["bufs"], _plan_pair_join, after)


def _sum8(g, name):
    _, R, C = g.shape

    def body(g_ref, o_ref):
        acc = g_ref[0]
        for d in range(1, N_DEV):
            acc = acc + g_ref[d]
        o_ref[...] = acc

    return _pcall(body, name=name, out_shape=jax.ShapeDtypeStruct((R, C), F32),
                          compiler_params=_params())(g)


def _ada_fwd(cs, w, b):
    D, n = w.shape
    tn = _pick(n, (512, 384, 256, 128))

    def body(c_ref, w_ref, b_ref, o_ref):
        cv = c_ref[...]
        a = (cv * _sigmoid(cv)).astype(BF16)
        o_ref[...] = _dot(a, w_ref[...].astype(BF16)) + b_ref[...]

    return _pcall(
        body, name="ada_fwd", grid=(n // tn,),
        in_specs=[pl.BlockSpec((16, D), lambda j: (0, 0)), pl.BlockSpec((D, tn), lambda j: (0, j)),
                  pl.BlockSpec((1, tn), lambda j: (0, j))],
        out_specs=pl.BlockSpec((16, tn), lambda j: (0, j)),
        out_shape=jax.ShapeDtypeStruct((16, n), F32),
        compiler_params=_params(("parallel",)),
    )(cs, w, b)


def _ada_bwd(cs, w, dmod):
    D, n = w.shape
    tn = _pick(n, (512, 384, 256, 128))

    def body(c_ref, w_ref, d_ref, gw_ref, da_ref):
        j = pl.program_id(0)
        cv = c_ref[...]
        a = cv * _sigmoid(cv)
        d = d_ref[...]
        gw_ref[...] = lax.dot_general(a, d, (((0,), (0,)), ((), ())), precision=HI, preferred_element_type=F32)

        @pl.when(j == 0)
        def _():
            da_ref[...] = jnp.zeros_like(da_ref)

        da_ref[...] += _dot_nt(d.astype(BF16), w_ref[...].astype(BF16))

    return _pcall(
        body, name="ada_bwd", grid=(n // tn,),
        in_specs=[pl.BlockSpec((16, D), lambda j: (0, 0)), pl.BlockSpec((D, tn), lambda j: (0, j)),
                  pl.BlockSpec((16, tn), lambda j: (0, j))],
        out_specs=[pl.BlockSpec((D, tn), lambda j: (0, j)), pl.BlockSpec((16, D), lambda j: (0, 0))],
        out_shape=[jax.ShapeDtypeStruct((D, n), F32), jax.ShapeDtypeStruct((16, D), F32)],
        compiler_params=_params(("arbitrary",)),
    )(cs, w, dmod)


def _rms1_fwd(xall, gain, shift2, scale2, n_ctx):
    T, D = xall.shape
    tb = _pick(n_ctx, (256, 128, 64, 32, 16))
    nctx = n_ctx // tb

    def body(x_ref, g_ref, sh_ref, sc_ref, o_ref):
        i = pl.program_id(0)
        xv = x_ref[...]
        r = lax.rsqrt(jnp.mean(xv * xv, axis=-1, keepdims=True) + EPS)
        nrm = xv * r * g_ref[...]
        lat = i >= nctx
        sh = jnp.where(lat, sh_ref[1:2, :], sh_ref[0:1, :])
        sc = jnp.where(lat, sc_ref[1:2, :], sc_ref[0:1, :])
        o_ref[...] = (nrm * (1.0 + sc) + sh).astype(BF16)

    vec = lambda r: pl.BlockSpec((r, D), lambda i: (0, 0))
    return _pcall(
        body, name="rms1_fwd", grid=(T // tb,),
        in_specs=[pl.BlockSpec((tb, D), lambda i: (i, 0)), vec(1), vec(2), vec(2)],
        out_specs=pl.BlockSpec((tb, D), lambda i: (i, 0)),
        out_shape=jax.ShapeDtypeStruct((T, D), BF16),
        compiler_params=_params(("parallel",)),
    )(xall, gain, shift2, scale2)


def _rms1_bwd(xall, dh, dxmid, gain, scale2, n_ctx):
    T, D = xall.shape
    L = T - n_ctx
    tb = _pick(n_ctx, (256, 128, 64, 32, 16))
    nctx = n_ctx // tb

    def body(x_ref, dh_ref, dxm_ref, g_ref, sc_ref, dx_ref, cs_ref):
        i = pl.program_id(0)
        lat = i >= nctx
        xv = x_ref[...]
        r = lax.rsqrt(jnp.mean(xv * xv, axis=-1, keepdims=True) + EPS)
        xh = xv * r
        g = g_ref[...]
        nrm = xh * g
        sc = jnp.where(lat, sc_ref[1:2, :], sc_ref[0:1, :])
        dhv = dh_ref[...]
        dn = dhv * (1.0 + sc)
        dxh = dn * g
        dxv = r * (dxh - xh * jnp.mean(dxh * xh, axis=-1, keepdims=True))
        s_sh = jnp.sum(dhv, axis=0, keepdims=True)
        s_sc = jnp.sum(dhv * nrm, axis=0, keepdims=True)
        s_g = jnp.sum(dn * xh, axis=0, keepdims=True)
        zero = jnp.zeros_like(s_sh)
        rows = lax.broadcasted_iota(jnp.int32, (8, D), 0)
        upd = jnp.where(rows == 0, jnp.where(lat, zero, s_sh),
              jnp.where(rows == 1, jnp.where(lat, zero, s_sc),
              jnp.where(rows == 2, jnp.where(lat, s_sh, zero),
              jnp.where(rows == 3, jnp.where(lat, s_sc, zero),
              jnp.where(rows == 4, s_g, 0.0)))))

        @pl.when(i == 0)
        def _():
            cs_ref[...] = jnp.zeros_like(cs_ref)

        cs_ref[...] += upd

        @pl.when(lat)
        def _():
            dx_ref[...] = dxv + dxm_ref[...]

    lat_blk = lambda i: (jnp.maximum(i - nctx, 0), 0)
    vec = lambda r: pl.BlockSpec((r, D), lambda i: (0, 0))
    return _pcall(
        body, name="rms1_bwd", grid=(T // tb,),
        in_specs=[pl.BlockSpec((tb, D), lambda i: (i, 0)), pl.BlockSpec((tb, D), lambda i: (i, 0)),
                  pl.BlockSpec((tb, D), lat_blk), vec(1), vec(2)],
        out_specs=[pl.BlockSpec((tb, D), lat_blk), vec(8)],
        out_shape=[jax.ShapeDtypeStruct((L, D), F32), jax.ShapeDtypeStruct((8, D), F32)],
        compiler_params=_params(("arbitrary",)),
    )(xall, dh, dxmid, gain, scale2)


def _resid_rms2_fwd(x, mo, vecs):
    L, D = x.shape
    tb = _pick(L, (256, 128, 64))

    def body(x_ref, mo_ref, v_ref, xm_ref, h_ref):
        xm = x_ref[...] + v_ref[0:1, :] * mo_ref[...]
        xm_ref[...] = xm
        r = lax.rsqrt(jnp.mean(xm * xm, axis=-1, keepdims=True) + EPS)
        h_ref[...] = (xm * r * v_ref[1:2, :] * (1.0 + v_ref[3:4, :]) + v_ref[2:3, :]).astype(BF16)

    blk = pl.BlockSpec((tb, D), lambda i: (i, 0))
    return _pcall(
        body, name="resid_rms2_fwd", grid=(L // tb,),
        in_specs=[blk, blk, pl.BlockSpec((8, D), lambda i: (0, 0))],
        out_specs=[blk, blk],
        out_shape=[jax.ShapeDtypeStruct((L, D), F32), jax.ShapeDtypeStruct((L, D), BF16)],
        compiler_params=_params(("parallel",)),
    )(x, mo, vecs)


def _resid_rms2_bwd(xmid, dh_a, dh_b, dy, mo, vecs):
    L, D = xmid.shape
    tb = _pick(L, (256, 128, 64))

    def body(xm_ref, da_ref, db_ref, dy_ref, mo_ref, v_ref, dxm_ref, dmo_ref, cs_ref):
        i = pl.program_id(0)
        xm = xm_ref[...]
        r = lax.rsqrt(jnp.mean(xm * xm, axis=-1, keepdims=True) + EPS)
        xh = xm * r
        g = v_ref[1:2, :]
        nrm = xh * g
        dhv = da_ref[...] + db_ref[...]
        dn = dhv * (1.0 + v_ref[3:4, :])
        dxh = dn * g
        dxm = dy_ref[...] + r * (dxh - xh * jnp.mean(dxh * xh, axis=-1, keepdims=True))
        dxm_ref[...] = dxm
        dmo_ref[...] = (dxm * v_ref[0:1, :]).astype(BF16)
        s0 = jnp.sum(dhv, axis=0, keepdims=True)
        s1 = jnp.sum(dhv * nrm, axis=0, keepdims=True)
        s2 = jnp.sum(dn * xh, axis=0, keepdims=True)
        s3 = jnp.sum(dxm * mo_ref[...], axis=0, keepdims=True)
        rows = lax.broadcasted_iota(jnp.int32, (8, D), 0)
        upd = jnp.where(rows == 0, s0, jnp.where(rows == 1, s1, jnp.where(rows == 2, s2,
              jnp.where(rows == 3, s3, 0.0))))

        @pl.when(i == 0)
        def _():
            cs_ref[...] = jnp.zeros_like(cs_ref)

        cs_ref[...] += upd

    blk = pl.BlockSpec((tb, D), lambda i: (i, 0))
    vec = pl.BlockSpec((8, D), lambda i: (0, 0))
    return _pcall(
        body, name="resid_rms2_bwd", grid=(L // tb,),
        in_specs=[blk, blk, blk, blk, blk, vec],
        out_specs=[blk, blk, vec],
        out_shape=[jax.ShapeDtypeStruct((L, D), F32), jax.ShapeDtypeStruct((L, D), BF16),
                   jax.ShapeDtypeStruct((8, D), F32)],
        compiler_params=_params(("arbitrary",)),
    )(xmid, dh_a, dh_b, dy, mo, vecs)


def _loss_head(xmid, f, g2, target):
    L, D = xmid.shape
    tb = _pick(L, (256, 128, 64))

    def body(xm_ref, f_ref, g_ref, t_ref, dy_ref, df_ref, s_ref):
        i = pl.program_id(0)
        fv = f_ref[...]
        g = g_ref[...]
        err = xm_ref[...] + g * fv - t_ref[...]
        dy = err * (1.0 / D)
        dy_ref[...] = dy
        df_ref[...] = (dy * g).astype(BF16)
        s0 = jnp.sum(dy * fv, axis=0, keepdims=True)
        part = 0.5 * jnp.sum(jnp.mean(err * err, axis=-1, keepdims=True), axis=0, keepdims=True)
        rows = lax.broadcasted_iota(jnp.int32, (8, D), 0)
        upd = jnp.where(rows == 0, s0, jnp.where(rows == 1, part, 0.0))

        @pl.when(i == 0)
        def _():
            s_ref[...] = jnp.zeros_like(s_ref)

        s_ref[...] += upd

    blk = pl.BlockSpec((tb, D), lambda i: (i, 0))
    return _pcall(
        body, name="loss_head", grid=(L // tb,),
        in_specs=[blk, blk, pl.BlockSpec((1, D), lambda i: (0, 0)), blk],
        out_specs=[blk, blk, pl.BlockSpec((8, D), lambda i: (0, 0))],
        out_shape=[jax.ShapeDtypeStruct((L, D), F32), jax.ShapeDtypeStruct((L, D), BF16),
                   jax.ShapeDtypeStruct((8, D), F32)],
        compiler_params=_params(("arbitrary",)),
    )(xmid, f, g2, target)


def _gate_cols(D, off):
    tc = _pick(np.gcd(D, off), (512, 256, 128))
    return tc, off // tc


def _merge_fwd(za, zb, p, n_ctx, off_a, off_b):
    L, D = za.shape
    tb = _pick(n_ctx, (256, 128, 64, 32, 16))
    nctx = n_ctx // tb
    tc, oa = _gate_cols(D, off_a)
    _, ob = _gate_cols(D, off_b)
    if off_b % tc:
        raise ValueError("gate column offsets must share a column tile")
    ob = off_b // tc

    def body(za_ref, zb_ref, ga_ref, gb_ref, z_ref):
        z_ref[...] = (_sigmoid(ga_ref[...]) * za_ref[...].astype(F32)
                      + _sigmoid(gb_ref[...]) * zb_ref[...].astype(F32)).astype(BF16)

    blk = pl.BlockSpec((tb, tc), lambda i, j: (i, j))
    return _pcall(
        body, name="merge_fwd", grid=(L // tb, D // tc),
        in_specs=[blk, blk, pl.BlockSpec((tb, tc), lambda i, j: (i + nctx, oa + j)),
                  pl.BlockSpec((tb, tc), lambda i, j: (i + nctx, ob + j))],
        out_specs=blk,
        out_shape=jax.ShapeDtypeStruct((L, D), BF16),
        compiler_params=_params(("parallel", "parallel")),
    )(za, zb, p, p)


def _merge_bwd(dz, za, zb, p, n_ctx, off_a, off_b):
    L, D = za.shape
    T = L + n_ctx
    tb = _pick(n_ctx, (256, 128, 64, 32, 16))
    nctx = n_ctx // tb
    tc = _gate_cols(D, off_a)[0]
    oa, ob = off_a // tc, off_b // tc

    def body(dz_ref, za_ref, zb_ref, ga_ref, gb_ref, dza_ref, dzb_ref, dga_ref, dgb_ref):
        i = pl.program_id(1)

        @pl.when(i < nctx)
        def _():
            dga_ref[...] = jnp.zeros_like(dga_ref)
            dgb_ref[...] = jnp.zeros_like(dgb_ref)

        @pl.when(i >= nctx)
        def _():
            dzv = dz_ref[...].astype(F32)
            sa = _sigmoid(ga_ref[...])
            sb = _sigmoid(gb_ref[...])
            dza_ref[...] = (dzv * sa).astype(BF16)
            dzb_ref[...] = (dzv * sb).astype(BF16)
            dga_ref[...] = (dzv * za_ref[...].astype(F32) * sa * (1.0 - sa)).astype(BF16)
            dgb_ref[...] = (dzv * zb_ref[...].astype(F32) * sb * (1.0 - sb)).astype(BF16)

    lat = pl.BlockSpec((tb, tc), lambda j, i: (jnp.maximum(i - nctx, 0), j))
    allr = pl.BlockSpec((tb, tc), lambda j, i: (i, j))
    return _pcall(
        body, name="merge_bwd", grid=(D // tc, T // tb),
        in_specs=[lat, lat, lat, pl.BlockSpec((tb, tc), lambda j, i: (i, oa + j)),
                  pl.BlockSpec((tb, tc), lambda j, i: (i, ob + j))],
        out_specs=[lat, lat, allr, allr],
        out_shape=[jax.ShapeDtypeStruct((L, D), BF16), jax.ShapeDtypeStruct((L, D), BF16),
                   jax.ShapeDtypeStruct((T, D), BF16), jax.ShapeDtypeStruct((T, D), BF16)],
        compiler_params=_params(("arbitrary", "arbitrary")),
    )(dz, za, zb, p, p)


def _shift_down(u, rows):
    return jnp.where(rows == 0, 0.0, pltpu.roll(u, 1, 0))


def _shift_up(u, rows):
    n = u.shape[0]
    return jnp.where(rows == n - 1, 0.0, pltpu.roll(u, n - 1, 0))


def _convgate_fwd(u1, u3, cw, cb):
    L, F = u1.shape
    tc = _pick(F, (256, 128))

    def body(u1_ref, u3_ref, w_ref, b_ref, a_ref):
        u = u1_ref[...].astype(F32)
        rows = lax.broadcasted_iota(jnp.int32, u.shape, 0)
        cv = _shift_down(u, rows) * w_ref[0:1, :] + u * w_ref[1:2, :] + _shift_up(u, rows) * w_ref[2:3, :] + b_ref[...]
        a_ref[...] = (cv * _sigmoid(cv) * u3_ref[...].astype(F32)).astype(BF16)

    blk = pl.BlockSpec((L, tc), lambda j: (0, j))
    return _pcall(
        body, name="convgate_fwd", grid=(F // tc,),
        in_specs=[blk, blk, pl.BlockSpec((8, tc), lambda j: (0, j)), pl.BlockSpec((1, tc), lambda j: (0, j))],
        out_specs=blk,
        out_shape=jax.ShapeDtypeStruct((L, F), BF16),
        compiler_params=_params(("parallel",)),
    )(u1, u3, cw, cb)


def _convgate_bwd(u1, u3, da, cw, cb):
    L, F = u1.shape
    tc = _pick(F, (256, 128))

    def body(u1_ref, u3_ref, da_ref, w_ref, b_ref, du1_ref, du3_ref, s_ref):
        u = u1_ref[...].astype(F32)
        rows = lax.broadcasted_iota(jnp.int32, u.shape, 0)
        um, up = _shift_down(u, rows), _shift_up(u, rows)
        w0, w1, w2 = w_ref[0:1, :], w_ref[1:2, :], w_ref[2:3, :]
        cv = um * w0 + u * w1 + up * w2 + b_ref[...]
        s = _sigmoid(cv)
        dav = da_ref[...].astype(F32)
        du3_ref[...] = (dav * cv * s).astype(BF16)
        dcv = dav * u3_ref[...].astype(F32) * (s * (1.0 + cv * (1.0 - s)))
        du1_ref[...] = (_shift_up(dcv, rows) * w0 + dcv * w1 + _shift_down(dcv, rows) * w2).astype(BF16)
        r8 = lax.broadcasted_iota(jnp.int32, (8, tc), 0)
        s0 = jnp.sum(dcv * um, axis=0, keepdims=True)
        s1 = jnp.sum(dcv * u, axis=0, keepdims=True)
        s2 = jnp.sum(dcv * up, axis=0, keepdims=True)
        s3 = jnp.sum(dcv, axis=0, keepdims=True)
        s_ref[...] = jnp.where(r8 == 0, s0, jnp.where(r8 == 1, s1, jnp.where(r8 == 2, s2,
                     jnp.where(r8 == 3, s3, 0.0))))

    blk = pl.BlockSpec((L, tc), lambda j: (0, j))
    v8 = pl.BlockSpec((8, tc), lambda j: (0, j))
    return _pcall(
        body, name="convgate_bwd", grid=(F // tc,),
        in_specs=[blk, blk, blk, v8, pl.BlockSpec((1, tc), lambda j: (0, j))],
        out_specs=[blk, blk, v8],
        out_shape=[jax.ShapeDtypeStruct((L, F), BF16), jax.ShapeDtypeStruct((L, F), BF16),
                   jax.ShapeDtypeStruct((8, F), F32)],
        compiler_params=_params(("parallel",)),
    )(u1, u3, da, cw, cb)


def _lower_bound(lbl_ref, d):
    l0, l1 = lbl_ref[d, 0:1, :], lbl_ref[d, 1:2, :]
    m = jnp.maximum(l0, l1)
    e0, e1 = jnp.exp(l0 - m), jnp.exp(l1 - m)
    return e0 / (e0 + e1)


def _chunk_cumsum(x, rev):
    n = x.shape[0]
    r = lax.broadcasted_iota(jnp.int32, x.shape, 0) % CHUNK
    k = 1
    while k < CHUNK:
        if rev:
            x = x + jnp.where(r < CHUNK - k, pltpu.roll(x, n - k, 0), 0.0)
        else:
            x = x + jnp.where(r >= k, pltpu.roll(x, k, 0), 0.0)
        k *= 2
    return x


def _gate_terms(z, lb):
    sg = _sigmoid(z)
    f = lb + (1.0 - lb) * sg
    return sg, f


def _decay_terms(z, lb, rev):
    _, f = _gate_terms(z, lb)
    g = jnp.log(f)
    return 1.0 - f, _chunk_cumsum(g, rev), _chunk_cumsum(g, not rev) - g


def _chunk_total(c, rev):
    return c[0:1, :] if rev else c[CHUNK - 1:CHUNK, :]


def _pair_decay(c, s, rev):
    t = lax.broadcasted_iota(jnp.int32, (CHUNK, 1), 0)
    later = (t <= s) if rev else (t >= s)
    return jnp.where(later, jnp.exp(c - c[s:s + 1, :]), 0.0)


def _scan_chunk(i, n_ctx_chunks, n_chunks, rev):
    if not rev:
        return i
    return jnp.where(i < n_ctx_chunks, n_ctx_chunks - 1 - i, n_chunks + n_ctx_chunks - 1 - i)


def _rows(ci):
    return pl.ds(pl.multiple_of(ci * CHUNK, CHUNK), CHUNK)


def _hgrn_cols(HA):
    return HA // HEAD


def _hgrn_fwd(p, lbl, ng, n_ctx, HA):
    T = p.shape[0]
    L = T - n_ctx
    nh = _hgrn_cols(HA)
    nc, ncc = T // CHUNK, n_ctx // CHUNK

    def body(q_ref, zf_ref, zb_ref, v_ref, og_ref, lbl_ref, ng_ref, ya_ref, o_ref, st_ref,
             c_scr, k_scr, qe_scr, ke_scr, o_scr):
        dirs = ((0, False, zf_ref), (1, True, zb_ref))
        for d, rev, z_ref in dirs:
            k, c, rest = _decay_terms(z_ref[...], _lower_bound(lbl_ref, d), rev)
            c_scr[d] = c
            k_scr[d] = k
            qe_scr[d] = (q_ref[...] * jnp.exp(c)).astype(BF16)
            ke_scr[d] = (k * jnp.exp(rest)).astype(BF16)

        def step(i2, states):
            states = list(states)
            for u in range(HGRN_UNROLL):
                for d, rev, _ in dirs:
                    St = states[d]
                    ci = _scan_chunk(HGRN_UNROLL * i2 + u, ncc, nc, rev)
                    rows = _rows(ci)
                    q, v, c, k = q_ref[rows, :], v_ref[rows, :], c_scr[d, rows, :], k_scr[d, rows, :]
                    st_ref[0, d, ci] = St.astype(BF16)
                    o = jnp.zeros((CHUNK, HEAD), F32)
                    for s in range(CHUNK):
                        E = _pair_decay(c, s, rev)
                        a = jnp.sum(q * E * k[s:s + 1, :], axis=1, keepdims=True)
                        o = o + a * v[s:s + 1, :]
                    o_scr[d, rows, :] = o + _dot_nt(qe_scr[d, rows, :], St.astype(BF16))
                    states[d] = St * jnp.exp(_chunk_total(c, rev)) + _dot_tn(v.astype(BF16), ke_scr[d, rows, :])
            return tuple(states)

        if nc % HGRN_UNROLL:
            raise ValueError("the number of chunks must be a multiple of HGRN_UNROLL")
        zero = jnp.zeros((HEAD, HEAD), F32)
        lax.fori_loop(0, nc // HGRN_UNROLL, step, (zero, zero))

        o = o_scr[0, pl.ds(n_ctx, L), :] + o_scr[1, pl.ds(n_ctx, L), :]
        o_ref[...] = o
        r = lax.rsqrt(jnp.mean(o * o, axis=-1, keepdims=True) + EPS)
        og = og_ref[pl.ds(n_ctx, L), :]
        ya_ref[...] =(o * r * ng_ref[...] * (og * _sigmoid(og))).astype(BF16)

    cb = HA // HEAD
    col = lambda kk: pl.BlockSpec((T, HEAD), lambda h: (0, kk * cb + h))
    return _pcall(
        body, name="hgrn_fwd", grid=(nh,),
        in_specs=[col(0), col(1), col(2), col(3), col(4),
                  pl.BlockSpec((2, 2, HEAD), lambda h: (0, 0, h)), pl.BlockSpec((1, HEAD), lambda h: (0, 0))],
        out_specs=[pl.BlockSpec((L, HEAD), lambda h: (0, h)), pl.BlockSpec((L, HEAD), lambda h: (0, h)),
                   pl.BlockSpec((1, 2, nc, HEAD, HEAD), lambda h: (h, 0, 0, 0, 0))],
        out_shape=[jax.ShapeDtypeStruct((L, HA), BF16), jax.ShapeDtypeStruct((L, HA), F32),
                   jax.ShapeDtypeStruct((nh, 2, nc, HEAD, HEAD), BF16)],
        scratch_shapes=[pltpu.VMEM((2, T, HEAD), F32), pltpu.VMEM((2, T, HEAD), F32),
                        pltpu.VMEM((2, T, HEAD), BF16), pltpu.VMEM((2, T, HEAD), BF16),
                        pltpu.VMEM((2, T, HEAD), F32)],
        compiler_params=_params(("parallel",)),
    )(p, p, p, p, p, lbl, ng)


def _hgrn_bwd(p, lbl, ng, o, dya, st, n_ctx, HA):
    T = p.shape[0]
    L = T - n_ctx
    nh = _hgrn_cols(HA)
    nc, ncc = T // CHUNK, n_ctx // CHUNK

    def body(q_ref, zf_ref, zb_ref, v_ref, og_ref, lbl_ref, ng_ref, o_ref, dya_ref, st_ref,
             dq_ref, dzf_ref, dzb_ref, dv_ref, dog_ref, dlbl_ref, dng_ref,
             do_scr, c_scr, k_scr, qe_scr, ke_scr, dg_scr, dk_scr, dq_scr, dv_scr, row_scr):
        h = pl.program_id(0)
        ov = o_ref[...]
        r = lax.rsqrt(jnp.mean(ov * ov, axis=-1, keepdims=True) + EPS)
        oh = ov * r
        ogv = og_ref[pl.ds(n_ctx, L), :]
        sg_o = _sigmoid(ogv)
        dyv = dya_ref[...]
        ngv = ng_ref[...]
        dog_ref[pl.ds(0, n_ctx), :] = jnp.zeros((n_ctx, HEAD), BF16)
        dog_ref[pl.ds(n_ctx, L), :] = (dyv * oh * ngv * (sg_o * (1.0 + ogv * (1.0 - sg_o)))).astype(BF16)
        don = dyv * (ogv * sg_o)
        dng = jnp.sum(don * oh, axis=0, keepdims=True)
        doh = don * ngv
        do_scr[pl.ds(0, n_ctx), :] = jnp.zeros((n_ctx, HEAD), F32)
        do_scr[pl.ds(n_ctx, L), :] = r * (doh - oh * jnp.mean(doh * oh, axis=-1, keepdims=True))

        @pl.when(h == 0)
        def _():
            dng_ref[...] = jnp.zeros_like(dng_ref)

        dng_ref[0:1, :] += dng

        t16 = lax.broadcasted_iota(jnp.int32, (CHUNK, HEAD), 0)
        dirs = ((0, False, zf_ref, dzf_ref), (1, True, zb_ref, dzb_ref))
        for d, rev, z_ref, _ in dirs:
            k, c, rest = _decay_terms(z_ref[...], _lower_bound(lbl_ref, d), rev)
            c_scr[d] = c
            k_scr[d] = k
            qe_scr[d] = (q_ref[...] * jnp.exp(c)).astype(BF16)
            ke_scr[d] = (k * jnp.exp(rest)).astype(BF16)
        dq_scr[...] = jnp.zeros_like(dq_scr)
        dv_scr[...] = jnp.zeros_like(dv_scr)

        zero = jnp.zeros((HEAD, HEAD), F32)

        def bwd_chunk(i, carry, u):
            new = []
            for (d, rev, _, _), dSt in zip(dirs, carry):
                ci = _scan_chunk(i, ncc, nc, rev)
                rows = _rows(ci)
                q, v, do = q_ref[rows, :], v_ref[rows, :], do_scr[rows, :]
                c, k = c_scr[d, rows, :], k_scr[d, rows, :]
                tot = _chunk_total(c, rev)
                etot = jnp.exp(tot)
                St = st_ref[0, d, ci]
                dSb = dSt.astype(BF16)
                do_b = do.astype(BF16)
                dq_x = _dot(do_b, St) * jnp.exp(c)
                dk_x = _dot(v.astype(BF16), dSb) * jnp.exp(tot - c)
                dv_x = _dot_nt(ke_scr[d, rows, :], dSb)
                dtot = (jnp.sum(St.astype(F32) * dSt, axis=0, keepdims=True) * etot
                        + jnp.sum(k * dk_x, axis=0, keepdims=True))
                dq = jnp.zeros((CHUNK, HEAD), F32)
                for s in range(CHUNK):
                    E = _pair_decay(c, s, rev)
                    XE = E * k[s:s + 1, :]
                    a = jnp.sum(q * XE, axis=1, keepdims=True)
                    da = jnp.sum(do * v[s:s + 1, :], axis=1, keepdims=True)
                    dq = dq + da * XE
                    row_scr[u, d, 0, s:s + 1, :] = jnp.sum(da * q * E, axis=0, keepdims=True)
                    row_scr[u, d, 1, s:s + 1, :] = jnp.sum(a * do, axis=0, keepdims=True)
                dq, dk, dv = dq + dq_x, row_scr[u, d, 0] + dk_x, row_scr[u, d, 1] + dv_x
                dg_scr[d, rows, :] = _chunk_cumsum(q * dq - k * dk, not rev) + dtot
                dk_scr[d, rows, :] = dk
                dq_scr[rows, :] += dq
                dv_scr[rows, :] += dv
                new.append(dSt * etot + _dot_tn(do_b, qe_scr[d, rows, :]))
            return tuple(new)

        def bwd_step(i2, carry):
            for u in range(2):
                carry = bwd_chunk(nc - 1 - (2 * i2 + u), carry, u)
            return carry

        lax.fori_loop(0, nc // 2, bwd_step, (zero, zero))

        for d, _, z_ref, dz_ref in dirs:
            lb = _lower_bound(lbl_ref, d)
            sg, f = _gate_terms(z_ref[...], lb)
            df = dg_scr[d] / f - dk_scr[d]
            dz_ref[...] = (df * (1.0 - lb) * sg * (1.0 - sg)).astype(BF16)
            dl0 = jnp.sum(df * (1.0 - sg), axis=0, keepdims=True) * lb * (1.0 - lb)
            dlbl_ref[d, 0:1, :] = dl0
            dlbl_ref[d, 1:2, :] = -dl0
        dq_ref[...] = dq_scr[...].astype(BF16)
        dv_ref[...] = dv_scr[...].astype(BF16)

    cb = HA // HEAD
    col = lambda kk: pl.BlockSpec((T, HEAD), lambda h: (0, kk * cb + h))
    tcol = pl.BlockSpec((T, HEAD), lambda h: (0, h))
    lcol = pl.BlockSpec((L, HEAD), lambda h: (0, h))
    outs = _pcall(
        body, name="hgrn_bwd", grid=(nh,),
        in_specs=[col(0), col(1), col(2), col(3), col(4),
                  pl.BlockSpec((2, 2, HEAD), lambda h: (0, 0, h)), pl.BlockSpec((1, HEAD), lambda h: (0, 0)),
                  lcol, lcol,
                  pl.BlockSpec((1, 2, nc, HEAD, HEAD), lambda h: (h, 0, 0, 0, 0), pipeline_mode=pl.Buffered(1))],
        out_specs=[tcol, tcol, tcol, tcol, tcol, pl.BlockSpec((2, 2, HEAD), lambda h: (0, 0, h)),
                   pl.BlockSpec((8, HEAD), lambda h: (0, 0))],
        out_shape=[jax.ShapeDtypeStruct((T, HA), BF16)] * 5 + [jax.ShapeDtypeStruct((2, 2, HA), F32),
                                                               jax.ShapeDtypeStruct((8, HEAD), F32)],
        scratch_shapes=[pltpu.VMEM((T, HEAD), F32),
                        pltpu.VMEM((2, T, HEAD), F32), pltpu.VMEM((2, T, HEAD), F32),
                        pltpu.VMEM((2, T, HEAD), BF16), pltpu.VMEM((2, T, HEAD), BF16),
                        pltpu.VMEM((2, T, HEAD), F32), pltpu.VMEM((2, T, HEAD), F32),
                        pltpu.VMEM((T, HEAD), F32), pltpu.VMEM((T, HEAD), F32),
                        pltpu.VMEM((2, 2, 2, CHUNK, HEAD), F32)],
        compiler_params=_params(("arbitrary",)),
    )(p, p, p, p, p, lbl, ng, o, dya, st)
    return outs


def _swap_halves(t, lane):
    q = HEAD // 4
    return jnp.where((lane % (2 * q)) < q, pltpu.roll(t, HEAD - q, 1), pltpu.roll(t, q, 1))


def _qk_norm(t, g):
    r = lax.rsqrt(jnp.mean(t * t, axis=-1, keepdims=True) + EPS)
    return t * r, r


def _rope(t, cos, sin, lane):
    return t * cos + _swap_halves(t, lane) * sin


def _qk_norm_bwd(dy, th, r, g):
    dth = dy * g
    return r * (dth - th * jnp.mean(dth * th, axis=-1, keepdims=True)), jnp.sum(dy * th, axis=0, keepdims=True)


def _rope_bwd(dy, cos, sin, lane):
    return dy * cos + _swap_halves(dy * sin, lane)


def _na_geometry(L):
    n_rows = L // GRID_W
    kr = min(WIN_R, n_rows)
    return n_rows, kr


def _na_prep(q_ref, k_ref, v_ref, gq_ref, gk_ref, cos_ref, sin_ref, qs, ks, vs, n_ctx, L):
    lane = lax.broadcasted_iota(jnp.int32, (L, HEAD), 1)
    cos, sin = cos_ref[...], sin_ref[...]
    qh, _ = _qk_norm(q_ref[pl.ds(n_ctx, L), :], None)
    qs[...] = _rope(qh * gq_ref[...], cos, sin, lane).astype(BF16)
    kh, _ = _qk_norm(k_ref[pl.ds(n_ctx, L), :], None)
    ks[pl.ds(n_ctx, L), :] = _rope(kh * gk_ref[...], cos, sin, lane).astype(BF16)
    kc, _ = _qk_norm(k_ref[pl.ds(0, n_ctx), :], None)
    ks[pl.ds(0, n_ctx), :] = (kc * gk_ref[...]).astype(BF16)
    vs[...] = v_ref[...].astype(BF16)


NA_RB = 4


def _na_band_rows(kr):
    return kr + NA_RB


def _na_scores(i, qs, ks, bias_ref, n_ctx, n_rows, kr):
    scale = HEAD ** -0.5
    kb = _na_band_rows(kr)
    rq = NA_RB * i
    r0 = jnp.clip(rq - WIN_R // 2, 0, n_rows - kb)
    qrows = pl.ds(pl.multiple_of(rq * GRID_W, NA_RB * GRID_W), NA_RB * GRID_W)
    krows = pl.ds(pl.multiple_of(n_ctx + r0 * GRID_W, GRID_W), kb * GRID_W)
    qv = qs[qrows, :]
    sb = _dot_nt(qv, ks[krows, :]) * scale
    band_row = lax.broadcasted_iota(jnp.int32, (GRID_W, kb * GRID_W), 1) // GRID_W
    parts, tiles = [], []
    for u in range(NA_RB):
        r_u = rq + u
        first = jnp.clip(r_u - WIN_R // 2, 0, n_rows - kr) - r0
        idx = [jnp.clip(r0 - r_u + (WIN_R - 1) + 2 * jj, 0, 2 * WIN_R - 1) for jj in range(kb // 2)]
        bias_u = jnp.concatenate([bias_ref[0, t] for t in idx], axis=1)
        inside = (band_row >= first) & (band_row < first + kr)
        parts.append(jnp.where(inside, sb[u * GRID_W:(u + 1) * GRID_W, :] + bias_u, NEG))
        tiles.append(idx)
    sb = jnp.concatenate(parts, axis=0)
    sc = _dot_nt(qv, ks[pl.ds(0, n_ctx), :]) * scale
    m = jnp.maximum(jnp.max(sb, axis=1, keepdims=True), jnp.max(sc, axis=1, keepdims=True))
    eb, ec = jnp.exp(sb - m), jnp.exp(sc - m)
    inv = 1.0 / (jnp.sum(eb, axis=1, keepdims=True) + jnp.sum(ec, axis=1, keepdims=True))
    return eb * inv, ec * inv, qrows, krows, tiles


def _na_fwd(p, bias, gq, gk, cos, sin, n_ctx, off, HB):
    T = p.shape[0]
    L = T - n_ctx
    nh = HB // HEAD
    n_rows, kr = _na_geometry(L)
    ob = off // HEAD

    def body(q_ref, k_ref, v_ref, bias_ref, gq_ref, gk_ref, cos_ref, sin_ref, y_ref, qs, ks, vs):
        _na_prep(q_ref, k_ref, v_ref, gq_ref, gk_ref, cos_ref, sin_ref, qs, ks, vs, n_ctx, L)

        def step(i, carry):
            pb, pc, qrows, krows, _ = _na_scores(i, qs, ks, bias_ref, n_ctx, n_rows, kr)
            y = _dot(pb.astype(BF16), vs[krows, :]) + _dot(pc.astype(BF16), vs[pl.ds(0, n_ctx), :])
            y_ref[qrows, :] = y.astype(BF16)
            return carry

        lax.fori_loop(0, n_rows // NA_RB, step, 0)

    col = lambda kk: pl.BlockSpec((T, HEAD), lambda h: (0, ob + kk * nh + h))
    vec = pl.BlockSpec((1, HEAD), lambda h: (0, 0))
    tab = pl.BlockSpec((L, HEAD), lambda h: (0, 0))
    return _pcall(
        body, name="na_fwd", grid=(nh,),
        in_specs=[col(0), col(1), col(2), pl.BlockSpec((1,) + bias.shape[1:], lambda h: (h, 0, 0, 0)),
                  vec, vec, tab, tab],
        out_specs=pl.BlockSpec((L, HEAD), lambda h: (0, h)),
        out_shape=jax.ShapeDtypeStruct((L, HB), BF16),
        scratch_shapes=[pltpu.VMEM((L, HEAD), BF16), pltpu.VMEM((T, HEAD), BF16), pltpu.VMEM((T, HEAD), BF16)],
        compiler_params=_params(("parallel",)),
    )(p, p, p, bias, gq, gk, cos, sin)


def _na_bwd(p, bias, gq, gk, cos, sin, dyb, n_ctx, off, HB):
    T = p.shape[0]
    L = T - n_ctx
    nh = HB // HEAD
    n_rows, kr = _na_geometry(L)
    ob = off // HEAD
    scale = HEAD ** -0.5

    def body(q_ref, k_ref, v_ref, bias_ref, gq_ref, gk_ref, cos_ref, sin_ref, dy_ref,
             dq_ref, dk_ref, dv_ref, dbias_ref, dg_ref, qs, ks, vs, dqa, dka, dva):
        h = pl.program_id(0)
        _na_prep(q_ref, k_ref, v_ref, gq_ref, gk_ref, cos_ref, sin_ref, qs, ks, vs, n_ctx, L)
        dka[...] = jnp.zeros_like(dka)
        dva[...] = jnp.zeros_like(dva)
        dbias_ref[...] = jnp.zeros_like(dbias_ref)

        crows = pl.ds(0, n_ctx)

        def step(i, carry):
            pb, pc, qrows, krows, tiles = _na_scores(i, qs, ks, bias_ref, n_ctx, n_rows, kr)
            do = dy_ref[qrows, :]
            qv = qs[qrows, :]
            dpb = _dot_nt(do, vs[krows, :])
            dpc = _dot_nt(do, vs[crows, :])
            delta = jnp.sum(pb * dpb, axis=1, keepdims=True) + jnp.sum(pc * dpc, axis=1, keepdims=True)
            dsb = pb * (dpb - delta)
            dsc = pc * (dpc - delta)
            dsb_b, dsc_b = dsb.astype(BF16), dsc.astype(BF16)
            dqa[qrows, :] = (_dot(dsb_b, ks[krows, :]) + _dot(dsc_b, ks[crows, :])) * scale
            dka[krows, :] += _dot_tn(dsb_b, qv) * scale
            dka[crows, :] += _dot_tn(dsc_b, qv) * scale
            dva[krows, :] += _dot_tn(pb.astype(BF16), do)
            dva[crows, :] += _dot_tn(pc.astype(BF16), do)
            for u, idx in enumerate(tiles):
                for jj, t in enumerate(idx):
                    dbias_ref[0, t] += dsb[u * GRID_W:(u + 1) * GRID_W, jj * 2 * GRID_W:(jj + 1) * 2 * GRID_W]
            return carry

        lax.fori_loop(0, n_rows // NA_RB, step, 0)

        lane = lax.broadcasted_iota(jnp.int32, (L, HEAD), 1)
        cos, sin = cos_ref[...], sin_ref[...]
        lat, ctx = pl.ds(n_ctx, L), pl.ds(0, n_ctx)
        gqv, gkv = gq_ref[...], gk_ref[...]
        qh, rq = _qk_norm(q_ref[lat, :], None)
        dq, dgq = _qk_norm_bwd(_rope_bwd(dqa[...], cos, sin, lane), qh, rq, gqv)
        dq_ref[ctx, :] = jnp.zeros((n_ctx, HEAD), BF16)
        dq_ref[lat, :] = dq.astype(BF16)
        kh, rk = _qk_norm(k_ref[lat, :], None)
        dk, dgk = _qk_norm_bwd(_rope_bwd(dka[lat, :], cos, sin, lane), kh, rk, gkv)
        dk_ref[lat, :] = dk.astype(BF16)
        kch, rkc = _qk_norm(k_ref[ctx, :], None)
        dkc, dgkc = _qk_norm_bwd(dka[ctx, :], kch, rkc, gkv)
        dk_ref[ctx, :] = dkc.astype(BF16)
        dv_ref[...] = dva[...].astype(BF16)

        @pl.when(h == 0)
        def _():
            dg_ref[...] = jnp.zeros_like(dg_ref)

        dg_ref[0:1, :] += dgq
        dg_ref[1:2, :] += dgk + dgkc

    col = lambda kk: pl.BlockSpec((T, HEAD), lambda h: (0, ob + kk * nh + h))
    vec = pl.BlockSpec((1, HEAD), lambda h: (0, 0))
    tab = pl.BlockSpec((L, HEAD), lambda h: (0, 0))
    tcol = pl.BlockSpec((T, HEAD), lambda h: (0, h))
    bspec = pl.BlockSpec((1,) + bias.shape[1:], lambda h: (h, 0, 0, 0))
    return _pcall(
        body, name="na_bwd", grid=(nh,),
        in_specs=[col(0), col(1), col(2), bspec, vec, vec, tab, tab, pl.BlockSpec((L, HEAD), lambda h: (0, h))],
        out_specs=[tcol, tcol, tcol, bspec, pl.BlockSpec((8, HEAD), lambda h: (0, 0))],
        out_shape=[jax.ShapeDtypeStruct((T, HB), BF16)] * 3 + [jax.ShapeDtypeStruct(bias.shape, F32),
                                                               jax.ShapeDtypeStruct((8, HEAD), F32)],
        scratch_shapes=[pltpu.VMEM((L, HEAD), BF16), pltpu.VMEM((T, HEAD), BF16), pltpu.VMEM((T, HEAD), BF16),
                        pltpu.VMEM((L, HEAD), F32), pltpu.VMEM((T, HEAD), F32), pltpu.VMEM((T, HEAD), F32)],
        compiler_params=_params(("arbitrary",)),
    )(p, p, p, bias, gq, gk, cos, sin, dyb)


def _bias_tables():
    w = np.arange(GRID_W)
    col_start = np.clip(w - WIN_C // 2, 0, GRID_W - WIN_C)
    col_in = (w[None, :] >= col_start[:, None]) & (w[None, :] < col_start[:, None] + WIN_C)
    dc = np.clip(w[None, :] - w[:, None], -(WIN_C - 1), WIN_C - 1) + WIN_C - 1
    n_pair = 2 * WIN_R
    ridx = np.zeros((n_pair, GRID_W, 2 * GRID_W), np.int32)
    cidx = np.zeros((n_pair, GRID_W, 2 * GRID_W), np.int32)
    valid = np.zeros((n_pair, GRID_W, 2 * GRID_W), bool)
    for i in range(n_pair):
        for half in range(2):
            row = i + half
            sl = slice(half * GRID_W, (half + 1) * GRID_W)
            ridx[i, :, sl] = min(row, 2 * WIN_R - 2)
            cidx[i, :, sl] = dc
            valid[i, :, sl] = col_in & (row <= 2 * WIN_R - 2)
    return ridx, cidx, valid


def _bias_onehot():
    _, cidx, valid = _bias_tables()
    K = GRID_W * 2 * GRID_W
    oh = np.zeros((K, 128), np.float32)
    neg = np.full((1, K), NEG, np.float32)
    for cq in range(GRID_W):
        for ll in range(2 * GRID_W):
            if valid[0, cq, ll]:
                oh[cq * 2 * GRID_W + ll, (ll // GRID_W) * 64 + cidx[0, cq, ll]] = 1.0
                neg[0, cq * 2 * GRID_W + ll] = 0.0
    return oh, neg


def _expand_bias(table):
    H = table.shape[0]
    n_pair, n_dc = 2 * WIN_R, 2 * WIN_C - 1
    tp = jnp.pad(table, ((0, 0), (0, n_pair + 1 - table.shape[1]), (0, 64 - n_dc)))
    t2 = jnp.concatenate([tp[:, :n_pair], tp[:, 1:n_pair + 1]], axis=-1).reshape(H * n_pair, 128)
    oh, neg = _bias_onehot()

    def body(t_ref, oh_ref, neg_ref, o_ref):
        o_ref[...] = lax.dot_general(t_ref[...], oh_ref[...], (((1,), (1,)), ((), ())), precision=HI,
                                     preferred_element_type=F32) + neg_ref[...]

    out = _pcall(body, name="bias_expand", out_shape=jax.ShapeDtypeStruct((H * n_pair, oh.shape[0]), F32),
                         compiler_params=_params())(t2, jnp.asarray(oh), jnp.asarray(neg))
    return out.reshape(H, n_pair, GRID_W, 2 * GRID_W)


def _bias_grad(dbias):
    H = dbias.shape[0]
    n_pair, n_dc = 2 * WIN_R, 2 * WIN_C - 1
    K = GRID_W * 2 * GRID_W
    oh, _ = _bias_onehot()
    flat = dbias.reshape(H * n_pair, K)

    def body(d_ref, oh_ref, o_ref):
        o_ref[...] = jnp.dot(d_ref[...], oh_ref[...], precision=HI, preferred_element_type=F32)

    g = _pcall(body, name="bias_grad", out_shape=jax.ShapeDtypeStruct((H * n_pair, 128), F32),
                       compiler_params=_params())(flat, jnp.asarray(oh))
    g = g.reshape(H, n_pair, 128)
    left, right = g[:, :, :n_dc], g[:, :, 64:64 + n_dc]
    out = left[:, :n_pair - 1]
    return out.at[:, 1:].add(right[:, :n_pair - 2])


def _rope_tables(L):
    pos = np.arange(L)
    row = (pos // GRID_W).astype(np.float32)
    colp = (pos % GRID_W).astype(np.float32)
    half = HEAD // 2
    nf = half // 2
    inv = (ROPE_THETA ** (-np.arange(nf, dtype=np.float32) / nf)).astype(np.float32)

    def tabs(pv):
        ang = pv[:, None] * inv[None, :]
        c, s = np.cos(ang), np.sin(ang)
        return np.concatenate([c, c], axis=1), np.concatenate([-s, s], axis=1)

    cr, sr = tabs(row)
    cc, sc = tabs(colp)
    return (jnp.asarray(np.concatenate([cr, cc], axis=1), F32), jnp.asarray(np.concatenate([sr, sc], axis=1), F32))


def _adamw(w, g, m, v, name, after=None, copy_g=False):
    R, C = w.shape
    tr = _row_tile(R, C)
    c1 = 1.0 - ADAM_B1 ** ADAM_STEP
    c2 = 1.0 - ADAM_B2 ** ADAM_STEP
    deps = [] if after is None else [after]
    n_out = 4 if copy_g else 3

    def body(w_ref, g_ref, m_ref, v_ref, *rest):
        d_ref, mo_ref, vo_ref = rest[len(deps):len(deps) + 3]
        gv = g_ref[...]
        mn = ADAM_B1 * m_ref[...] + (1.0 - ADAM_B1) * gv
        vn = ADAM_B2 * v_ref[...] + (1.0 - ADAM_B2) * (gv * gv)
        mo_ref[...] = mn
        vo_ref[...] = vn
        d_ref[...] = -ADAM_LR * ((mn / c1) / (jnp.sqrt(vn / c2) + ADAM_EPS) + ADAM_WD * w_ref[...])
        if copy_g:
            rest[-1][...] = gv

    blk = pl.BlockSpec((tr, C), lambda i: (i, 0))
    return _pcall(
        body, name=name, grid=(R // tr,),
        in_specs=[blk] * 4 + [_ANY] * len(deps), out_specs=[blk] * n_out,
        out_shape=[jax.ShapeDtypeStruct((R, C), F32)] * n_out,
        compiler_params=_params(("parallel",)),
    )(w, g, m, v, *deps)


PACK_W = 1024


def _pack(parts):
    flat, offs, pos = [], [], 0
    for a in parts:
        n = a.size
        padn = -n % PACK_W
        flat.append(jnp.pad(a.reshape(-1).astype(F32), (0, padn)))
        offs.append((pos, n, a.shape))
        pos += n + padn
    tail = -pos % (8 * PACK_W)
    if tail:
        flat.append(jnp.zeros((tail,), F32))
    return jnp.concatenate(flat).reshape(-1, PACK_W), offs


def _unpack(buf, offs, i):
    pos, n, shape = offs[i]
    return buf.reshape(buf.shape[:-2] + (-1,))[..., pos:pos + n].reshape(buf.shape[:-2] + shape)


def kernel(x, c, ctx, c_ctx, ada_w, ada_b, norm1_g, norm2_g, w_in, hgrn_lb_logits, hgrn_norm_g, na_q_norm_g, na_k_norm_g, na_rel_bias, w_branch_a, w_branch_b, w_out, ffn_w1, ffn_w3, ffn_conv_w, ffn_conv_b, ffn_w2, loss_target, m_c_ctx, m_ada_w, m_ada_b, m_norm1_g, m_norm2_g, m_w_in, m_hgrn_lb_logits, m_hgrn_norm_g, m_na_q_norm_g, m_na_k_norm_g, m_na_rel_bias, m_w_branch_a, m_w_branch_b, m_w_out, m_ffn_w1, m_ffn_w3, m_ffn_conv_w, m_ffn_conv_b, m_ffn_w2, v_c_ctx, v_ada_w, v_ada_b, v_norm1_g, v_norm2_g, v_w_in, v_hgrn_lb_logits, v_hgrn_norm_g, v_na_q_norm_g, v_na_k_norm_g, v_na_rel_bias, v_w_branch_a, v_w_branch_b, v_w_out, v_ffn_w1, v_ffn_w3, v_ffn_conv_w, v_ffn_conv_b, v_ffn_w2):
    weights = dict(c_ctx=c_ctx, ada_w=ada_w, ada_b=ada_b, norm1_g=norm1_g, norm2_g=norm2_g, w_in=w_in,
                   hgrn_lb_logits=hgrn_lb_logits, hgrn_norm_g=hgrn_norm_g, na_q_norm_g=na_q_norm_g,
                   na_k_norm_g=na_k_norm_g, na_rel_bias=na_rel_bias, w_branch_a=w_branch_a, w_branch_b=w_branch_b,
                   w_out=w_out, ffn_w1=ffn_w1, ffn_w3=ffn_w3, ffn_conv_w=ffn_conv_w, ffn_conv_b=ffn_conv_b,
                   ffn_w2=ffn_w2)
    moms = dict(c_ctx=(m_c_ctx, v_c_ctx), ada_w=(m_ada_w, v_ada_w), ada_b=(m_ada_b, v_ada_b),
                norm1_g=(m_norm1_g, v_norm1_g), norm2_g=(m_norm2_g, v_norm2_g), w_in=(m_w_in, v_w_in),
                hgrn_lb_logits=(m_hgrn_lb_logits, v_hgrn_lb_logits), hgrn_norm_g=(m_hgrn_norm_g, v_hgrn_norm_g),
                na_q_norm_g=(m_na_q_norm_g, v_na_q_norm_g), na_k_norm_g=(m_na_k_norm_g, v_na_k_norm_g),
                na_rel_bias=(m_na_rel_bias, v_na_rel_bias), w_branch_a=(m_w_branch_a, v_w_branch_a),
                w_branch_b=(m_w_branch_b, v_w_branch_b), w_out=(m_w_out, v_w_out), ffn_w1=(m_ffn_w1, v_ffn_w1),
                ffn_w3=(m_ffn_w3, v_ffn_w3), ffn_conv_w=(m_ffn_conv_w, v_ffn_conv_w),
                ffn_conv_b=(m_ffn_conv_b, v_ffn_conv_b), ffn_w2=(m_ffn_w2, v_ffn_w2))
    order = list(weights)

    L, D = x.shape[1], x.shape[2]
    N = ctx.shape[1]
    T = N + L
    HA = w_branch_a.shape[1]
    HB = w_branch_b.shape[1]
    F = ffn_conv_b.shape[1]
    IN = 5 * HA + 3 * HB + 2 * D
    n_ada = ada_w.shape[2]
    ix, iy, ic = _pos()
    chip = 2 * ix + iy
    dev = 2 * chip + ic

    _PENDING.clear()
    pk0, offs0 = _pack([c[0], hgrn_lb_logits, ffn_conv_w[0]])
    g0 = _allgather8(pk0, "gather_small0")
    c_all = _unpack(g0, offs0, 0)
    lbl_parts = _unpack(g0, offs0, 1)
    lbl = jnp.concatenate([lbl_parts[2 * j] for j in range(N_CHIP)], axis=-1)
    cw_parts = _unpack(g0, offs0, 2)
    cw = jnp.concatenate([cw_parts[2 * j] for j in range(N_CHIP)], axis=-1)
    cw8 = jnp.pad(cw, ((0, 5), (0, 0)))

    cs = jnp.concatenate([c_all, c_ctx[None, :], jnp.zeros((7, D), F32)], axis=0)
    ada_b_mine = lax.dynamic_slice(ada_b, (0, chip * n_ada), (1, n_ada))
    mod_mine = _ada_fwd(cs, ada_w[0], ada_b_mine)
    gm = _allgather8(mod_mine, "gather_mod")
    mod = jnp.concatenate([gm[2 * j] for j in range(N_CHIP)], axis=-1)
    mod_l = lax.dynamic_slice(mod, (dev, 0), (1, N_MOD * D)).reshape(N_MOD, D)
    mod_c = mod[8].reshape(N_MOD, D)
    sh1, sc1, g1, sh2, sc2, g2 = [mod_l[i:i + 1] for i in range(N_MOD)]
    shift1 = jnp.concatenate([mod_c[0:1], sh1], axis=0)
    scale1 = jnp.concatenate([mod_c[1:2], sc1], axis=0)

    shards = [w_in[0], w_branch_a[0], w_branch_b[0], w_out[0], ffn_w1[0], ffn_w3[0], ffn_w2[0]]
    names = ["w_in", "w_a", "w_b", "w_out", "w1", "w3", "w2"]
    slots = [_cast_bf16_slot(s, "cast_" + nm) for s, nm in zip(shards, names)]
    sem_nb, win_buf = _xfer_start("gather_ici_start_in_nbr", slots[0:1], _plan_gather_ici(NEIGHBOURS), 2, gm)

    xall = jnp.concatenate([ctx[0], x[0]], axis=0)
    h_all = _rms1_fwd(xall, norm1_g, shift1, scale1, N)
    chip_i = chip.astype(jnp.int32)
    same = lambda ids: jnp.stack([jnp.stack(ids), jnp.stack(ids)])
    p = _mm_nn_sel(h_all, win_buf[0], same([chip_i]), F32, "mm_p_own")
    bias = _expand_bias(na_rel_bias[0])
    win_buf = _xfer_wait("gather_ici_wait_in_nbr", sem_nb, win_buf, _plan_gather_ici(NEIGHBOURS),
                         (p, bias, *slots[1:]))
    sem_nb, win_buf = _xfer_start("gather_d2d_start_in_nbr", win_buf, _plan_gather_d2d(NEIGHBOURS), 2)
    sem_dg, win_buf = _xfer_start("gather_ici_start_in_diag", win_buf, _plan_gather_ici(DIAGONAL), 1)
    gat_mix = _gather_start("mix", slots[1:4])
    gat_ffn = _gather_start("ffn", slots[4:6])
    gat_ffn2 = _gather_start("ffn2", slots[6:7])
    win_buf = _xfer_wait("gather_d2d_wait_in_nbr", sem_nb, win_buf, _plan_gather_d2d(NEIGHBOURS), _PENDING[0])
    p = _mm_nn_sel(h_all, win_buf[0], same([chip_i ^ 1, chip_i ^ 2]), F32, "mm_p_nbr", p)
    win_buf = _d2d_hand_over("in_diag", sem_dg, win_buf, DIAGONAL, p)
    p = _mm_nn_sel(h_all, win_buf[0], same([chip_i ^ 3]), F32, "mm_p_diag", p)
    Win = win_buf[0]
    cos, sin = _rope_tables(L)
    off_na = 5 * HA
    y_b = _na_fwd(p, bias, na_q_norm_g, na_k_norm_g, cos, sin, N, off_na, HB)
    gat_mix = _gather_mid(gat_mix, y_b)
    y_a, o_a, st_a = _hgrn_fwd(p, lbl, hgrn_norm_g, N, HA)
    Wa, Wb, Wo = _gather_finish(gat_mix, (y_a, y_b))
    Wo = Wo.reshape(1, D, D)
    za = _mm_nn(y_a, Wa, BF16, "mm_za")
    zb = _mm_nn(y_b, Wb, BF16, "mm_zb")
    off_ga, off_gb = 5 * HA + 3 * HB, 5 * HA + 3 * HB + D
    z = _merge_fwd(za, zb, p, N, off_ga, off_gb)
    gat_ffn = _gather_mid(gat_ffn, z)
    mo = _mm_nn(z, Wo, F32, "mm_mo")
    vec2 = jnp.concatenate([g1, norm2_g, sh2, sc2, jnp.zeros((4, D), F32)], axis=0)
    x_mid, h2 = _resid_rms2_fwd(x[0], mo, vec2)
    W1, W3 = _gather_finish(gat_ffn, h2)
    gat_ffn2 = _gather_mid(gat_ffn2, h2)
    u1 = _mm_nn(h2, W1, BF16, "mm_u1")
    u3 = _mm_nn(h2, W3, BF16, "mm_u3")
    (W2,) = _gather_finish(gat_ffn2, (u1, u3))
    W2 = W2.reshape(1, F, D)
    a = _convgate_fwd(u1, u3, cw8, ffn_conv_b)
    f = _mm_nn(a, W2, F32, "mm_f")
    dy, df, s_loss = _loss_head(x_mid, f, g2, loss_target[0])
    loss = lax.psum(s_loss[1, 0], ("x", "y", "c"))
    d_g2 = s_loss[0:1]

    gW2 = _mm_tn(a, df, 1, "mm_gw2").reshape(N_CHIP, F // N_CHIP, D)
    da = _mm_nt(df, W2, BF16, "mm_da")
    du1, du3, s_conv = _convgate_bwd(u1, u3, da, cw8, ffn_conv_b)
    gW1 = _mm_tn(h2, du1, N_CHIP, "mm_gw1")
    gW3 = _mm_tn(h2, du3, N_CHIP, "mm_gw3")
    rs_ffn = _rs_start("ffn", [gW2, gW1, gW3])
    dh2a = _mm_nt(du1, W1, F32, "mm_dh2a")
    dh2b = _mm_nt(du3, W3, F32, "mm_dh2b")
    rs_ffn = _rs_scatter(rs_ffn, dh2b)
    dxm, dmo, s_rms2 = _resid_rms2_bwd(x_mid, dh2a, dh2b, dy, mo, vec2)
    gWo = _mm_tn(z, dmo, 1, "mm_gwo").reshape(N_CHIP, D // N_CHIP, D)
    dz = _mm_nt(dmo, Wo, BF16, "mm_dz")
    dza, dzb, dga, dgb = _merge_bwd(dz, za, zb, p, N, off_ga, off_gb)
    gWa = _mm_tn(y_a, dza, N_CHIP, "mm_gwa")
    gWb = _mm_tn(y_b, dzb, N_CHIP, "mm_gwb")
    rs_mix = _rs_start("mix", [gWo, gWa, gWb])
    dya = _mm_nt(dza, Wa, F32, "mm_dya")
    dyb = _mm_nt(dzb, Wb, BF16, "mm_dyb")
    rs_mix = _rs_scatter(rs_mix, dyb)
    dq_a, dzf, dzbk, di_a, dog, dlbl, s_ng = _hgrn_bwd(p, lbl, hgrn_norm_g, o_a, dya, st_a, N, HA)
    rs_ffn = _rs_join(rs_ffn, dq_a)
    dq_n, dk_n, dv_n, dbias, s_qk = _na_bwd(p, bias, na_q_norm_g, na_k_norm_g, cos, sin, dyb, N, off_na, HB)
    rs_mix = _rs_join(rs_mix, dq_n)
    dp = jnp.concatenate([dq_a, dzf, dzbk, di_a, dog, dq_n, dk_n, dv_n, dga, dgb], axis=1)
    gWin = _mm_tn(h_all, dp, N_CHIP, "mm_gwin")
    rs_in = _rs_start("in", [gWin])
    rs_in = _rs_scatter(rs_in, _PENDING[0])
    dh = _mm_nt(dp, Win, F32, "mm_dh")
    grad_x, s_rms1 = _rms1_bwd(xall, dh, dxm, norm1_g, scale1, N)
    d_table = _bias_grad(dbias)

    grads = {}
    big_names = ["ada_w", "w_in", "w_branch_a", "w_branch_b", "w_out", "ffn_w1", "ffn_w3", "ffn_w2"]
    small_names = [n for n in order if n not in big_names]
    delta, new_m, new_v = {}, {}, {}

    def update(nm, after=None):
        reduced = nm != "ada_w"
        d_, m_, v_, *g_ = _adamw(weights[nm][0], grads[nm][0], moms[nm][0][0], moms[nm][1][0], "adamw_" + nm,
                                 after, copy_g=reduced)
        delta[nm], new_m[nm], new_v[nm] = d_[None], m_[None], v_[None]
        if reduced:
            grads[nm] = g_[0][None]
        return d_

    last = grad_x
    for nm, g in zip(["ffn_w2", "ffn_w1", "ffn_w3"], _rs_finish(rs_ffn, last)):
        grads[nm] = g[None]
        last = update(nm, last)
    for nm, g in zip(["w_out", "w_branch_a", "w_branch_b"], _rs_finish(rs_mix, last)):
        grads[nm] = g[None]
        last = update(nm, last)
    rs_in = _rs_join(rs_in, last)

    zD = jnp.zeros((1, D), F32)
    dmod_l = jnp.concatenate([s_rms1[2:3], s_rms1[3:4], s_rms2[3:4], s_rms2[0:1], s_rms2[1:2], d_g2], axis=0)
    dmod_c = jnp.concatenate([s_rms1[0:1], s_rms1[1:2], zD, zD, zD, zD], axis=0)
    pk1, offs1 = _pack([dmod_l, dmod_c, s_rms1[4], s_rms2[2], dlbl, s_ng[0], s_qk[0], s_qk[1], d_table,
                        s_conv[0:3], s_conv[3]])
    g1all = _allgather8(pk1, "gather_small1")
    tot1 = _sum8(g1all, "sum_small1")
    dmod_rows = _unpack(g1all, offs1, 0).reshape(N_DEV, N_MOD * D)
    dmod_c_tot = _unpack(tot1, offs1, 1).reshape(1, N_MOD * D)
    dmod16 = jnp.concatenate([dmod_rows, dmod_c_tot, jnp.zeros((7, N_MOD * D), F32)], axis=0)
    dmod16_mine = lax.dynamic_slice(dmod16, (0, chip * n_ada), (16, n_ada))
    g_ada_w, dact = _ada_bwd(cs, ada_w[0], dmod16_mine)
    pk2, offs2 = _pack([dact[8]])
    g2all = _allgather8(pk2, "gather_small2")
    dact_rows = _unpack(g2all, offs2, 0)
    dact_sel = jnp.concatenate([dact_rows[2 * j][None] for j in range(N_CHIP)] + [jnp.zeros((4, D), F32)], axis=0)

    grads["ada_w"] = g_ada_w[None]
    grads["ada_b"] =(_unpack(tot1, offs1, 0) + _unpack(tot1, offs1, 1)).reshape(1, N_MOD * D)
    grads["norm1_g"] = _unpack(tot1, offs1, 2)[None]
    grads["norm2_g"] = _unpack(tot1, offs1, 3)[None]
    g_lbl = _unpack(tot1, offs1, 4)
    n_lb = HA // N_CHIP
    grads["hgrn_lb_logits"] = lax.dynamic_slice(g_lbl, (0, 0, chip * n_lb), (2, 2, n_lb))
    grads["hgrn_norm_g"] = _unpack(tot1, offs1, 5)[None]
    grads["na_q_norm_g"] = _unpack(tot1, offs1, 6)[None]
    grads["na_k_norm_g"] = _unpack(tot1, offs1, 7)[None]
    grads["na_rel_bias"] = _unpack(tot1, offs1, 8)[None]
    g_cw = _unpack(tot1, offs1, 9)
    n_f = F // N_CHIP
    grads["ffn_conv_w"] = lax.dynamic_slice(g_cw, (0, chip * n_f), (3, n_f))[None]
    grads["ffn_conv_b"] = _unpack(tot1, offs1, 10)[None]

    g_c_ctx = _dsilu_rows(dact_sel, c_ctx[None, :], "grad_c_ctx")
    grads["c_ctx"] = g_c_ctx[0]

    last = update("ada_w", g_c_ctx)
    pw, offw = _pack([weights[n] for n in small_names])
    pg, _ = _pack([grads[n] for n in small_names])
    pm, _ = _pack([moms[n][0] for n in small_names])
    pv, _ = _pack([moms[n][1] for n in small_names])
    d_, m_, v_ = _adamw(pw, pg, pm, pv, "adamw_small", last)
    for i, nm in enumerate(small_names):
        delta[nm], new_m[nm], new_v[nm] = _unpack(d_, offw, i), _unpack(m_, offw, i), _unpack(v_, offw, i)
    grads["w_in"] = _rs_finish(rs_in, d_)[0][None]
    update("w_in")

    return (loss, grad_x[None], *[grads[n] for n in order], *[delta[n] for n in order],
            *[new_m[n] for n in order], *[new_v[n] for n in order])


def _dsilu_rows(v, cv, name):
    D = v.shape[1]

    def body(v_ref, c_ref, o_ref):
        t = c_ref[...]
        s = _sigmoid(t)
        o_ref[...] = (((v_ref[0:1, :] + v_ref[1:2, :]) + v_ref[2:3, :]) + v_ref[3:4, :]) * (s * (1.0 + t * (1.0 - s)))

    return _pcall(body, name=name, out_shape=jax.ShapeDtypeStruct((1, D), F32),
                          compiler_params=_params())(v, cv)
```

```python
import functools

import numpy as np
import jax
import jax.numpy as jnp
from jax import lax
from jax.experimental import pallas as pl
from jax.experimental.pallas import tpu as pltpu

F32 = jnp.float32
BF16 = jnp.bfloat16
MESH = pl.DeviceIdType.MESH

HEAD = 128
GRID_W = 64
WIN_R = 8
WIN_C = 16
ROPE_THETA = 10000.0
EPS = 1e-6
N_MOD = 6
CHUNK = 16
HGRN_UNROLL = 4
ADAM_LR = 0.001
ADAM_B1 = 0.9
ADAM_B2 = 0.999
ADAM_EPS = 1e-08
ADAM_WD = 0.01
ADAM_STEP = 10
NEG = -1e30
VMEM_LIMIT = 56 * 1024 * 1024
N_DEV = 8
N_CHIP = 4
HI = lax.Precision.HIGHEST


def _pick(n, cands):
    for c in cands:
        if n % c == 0:
            return c
    return n


def _row_tile(rows, cols, target_bytes=1 << 20):
    want = max(16, target_bytes // (4 * cols))
    for t in (512, 256, 128, 64, 32, 16, 8):
        if t <= want and rows % t == 0:
            return t
    return rows


def _params(sem=None):
    return pltpu.CompilerParams(dimension_semantics=sem, vmem_limit_bytes=VMEM_LIMIT)


def _dot(a, b):
    return jnp.dot(a, b, preferred_element_type=F32)


def _dot_nt(a, b):
    return lax.dot_general(a, b, (((1,), (1,)), ((), ())), preferred_element_type=F32)


def _dot_tn(a, b):
    return lax.dot_general(a, b, (((0,), (0,)), ((), ())), preferred_element_type=F32)


def _sigmoid(x):
    return 1.0 / (1.0 + jnp.exp(-x))


def _col_tile(n):
    return n if n <= 1536 else _pick(n, (1024, 768, 512, 384, 256, 128))


def _mm_nn(x, w3, out_dtype, name):
    M, K = x.shape
    S, _, n = w3.shape
    tm = _pick(M, ((1024,) if K <= 2048 else ()) + (768, 512, 256, 128, 64))
    tn = _col_tile(n)
    nb = n // tn

    def body(x_ref, w_ref, o_ref):
        o_ref[...] = _dot(x_ref[...].astype(BF16), w_ref[0]).astype(o_ref.dtype)

    return _pcall(
        body, name=name, grid=(M // tm, S * nb),
        in_specs=[pl.BlockSpec((tm, K), lambda i, j: (i, 0)),
                  pl.BlockSpec((1, K, tn), lambda i, j: (j // nb, 0, j % nb))],
        out_specs=pl.BlockSpec((tm, tn), lambda i, j: (i, j)),
        out_shape=jax.ShapeDtypeStruct((M, S * n), out_dtype),
        compiler_params=_params(("parallel", "parallel")),
    )(x, w3)


def _mm_nn_sel(x, w3, sel, out_dtype, name, prev=None):
    M, K = x.shape
    S, _, n = w3.shape
    tm = _pick(M, (768, 512, 256, 128, 64))
    tn = _col_tile(n)
    nb = n // tn
    k = sel.shape[1]

    def body(sel_ref, x_ref, w_ref, *rest):
        rest[-1][...] = _dot(x_ref[...].astype(BF16), w_ref[0]).astype(out_dtype)

    in_specs = [pl.BlockSpec((tm, K), lambda i, j, sel_ref: (i, 0)),
                pl.BlockSpec((1, K, tn), lambda i, j, sel_ref: (sel_ref[0, j // nb], 0, j % nb))]
    operands = [sel, x, w3]
    if prev is not None:
        in_specs.append(_ANY)
        operands.append(prev)
    return pl.pallas_call(
        body, name=name,
        grid_spec=pltpu.PrefetchScalarGridSpec(
            num_scalar_prefetch=1, grid=(M // tm, k * nb), in_specs=in_specs,
            out_specs=pl.BlockSpec((tm, tn), lambda i, j, sel_ref: (i, sel_ref[1, j // nb] * nb + j % nb))),
        out_shape=jax.ShapeDtypeStruct((M, S * n), out_dtype),
        input_output_aliases={} if prev is None else {3: 0},
        compiler_params=_params(("parallel", "parallel")),
    )(*operands)


def _mm_nt(dy, w3, out_dtype, name):
    M = dy.shape[0]
    S, K, n = w3.shape
    tm = _pick(M, (1024, 768, 512, 256, 128, 64))
    tk = K if K <= 2048 else _pick(K, (1408, 1024, 512, 256, 128))
    tc = n if n <= 2048 else _col_tile(n)
    nb = n // tc
    nsteps = S * nb

    def body(dy_ref, w_ref, o_ref, acc_ref):
        s = pl.program_id(2)

        @pl.when(s == 0)
        def _():
            acc_ref[...] = jnp.zeros_like(acc_ref)

        acc_ref[...] += _dot_nt(dy_ref[...].astype(BF16), w_ref[0])

        @pl.when(s == nsteps - 1)
        def _():
            o_ref[...] = acc_ref[...].astype(o_ref.dtype)

    return _pcall(
        body, name=name, grid=(M // tm, K // tk, nsteps),
        in_specs=[pl.BlockSpec((tm, tc), lambda i, k, s: (i, s)),
                  pl.BlockSpec((1, tk, tc), lambda i, k, s: (s // nb, k, s % nb))],
        out_specs=pl.BlockSpec((tm, tk), lambda i, k, s: (i, k)),
        out_shape=jax.ShapeDtypeStruct((M, K), out_dtype),
        scratch_shapes=[pltpu.VMEM((tm, tk), F32)],
        compiler_params=_params(("parallel", "parallel", "arbitrary")),
    )(dy, w3)


def _mm_tn(x, dy, S, name):
    M, K = x.shape
    n = dy.shape[1] // S
    tk = _pick(K, (1024, 512, 256, 128))
    tn = _col_tile(n)
    nb = n // tn

    def body(x_ref, dy_ref, o_ref):
        o_ref[0] = _dot_tn(x_ref[...].astype(BF16), dy_ref[...].astype(BF16)).astype(BF16)

    return _pcall(
        body, name=name, grid=(S * nb, K // tk),
        in_specs=[pl.BlockSpec((M, tk), lambda j, k: (0, k)),
                  pl.BlockSpec((M, tn), lambda j, k: (0, j))],
        out_specs=pl.BlockSpec((1, tk, tn), lambda j, k: (j // nb, k, j % nb)),
        out_shape=jax.ShapeDtypeStruct((S, K, n), BF16),
        compiler_params=_params(("parallel", "parallel")),
    )(x, dy)


def _chip_index():
    return (2 * lax.axis_index("x") + lax.axis_index("y")).astype(jnp.int32).reshape(1)


def _cast_bf16_slot(w, name):
    R, C = w.shape
    tr = _row_tile(R, C, 2 << 20)

    def body(j_ref, w_ref, o_ref):
        o_ref[0] = w_ref[...].astype(BF16)

    return _pcall(
        body, name=name,
        grid_spec=pltpu.PrefetchScalarGridSpec(
            num_scalar_prefetch=1, grid=(R // tr,),
            in_specs=[pl.BlockSpec((tr, C), lambda i, j_ref: (i, 0))],
            out_specs=pl.BlockSpec((1, tr, C), lambda i, j_ref: (j_ref[0], i, 0))),
        out_shape=jax.ShapeDtypeStruct((N_CHIP, R, C), BF16),
        compiler_params=_params(("parallel",)),
    )(_chip_index(), w)


def _pos():
    return lax.axis_index("x"), lax.axis_index("y"), lax.axis_index("c")


def _other_chips(x, y):
    return [(x, 1 - y), (1 - x, y), (1 - x, 1 - y)]


def _allgather8(v, name):
    R, C = v.shape

    def body(x_ref, out_ref, send_sems, recv_sems, local_sem):
        x, y, c = _pos()
        me, sibling = (x, y, c), (x, y, 1 - c)
        chips = _other_chips(x, y)

        def slot(px, py, pc):
            return out_ref.at[4 * px + 2 * py + pc]

        def copy(k, block, to, src=None):
            return pltpu.make_async_remote_copy(
                src_ref=slot(*block) if src is None else src, dst_ref=slot(*block),
                send_sem=send_sems.at[k], recv_sem=recv_sems.at[k], device_id=to, device_id_type=MESH)

        mine = pltpu.make_async_copy(x_ref, slot(*me), local_sem)
        mine.start()
        first = [copy(0, me, sibling, src=x_ref)]
        first += [copy(1 + j, me, (*chip, c), src=x_ref) for j, chip in enumerate(chips)]
        for cp in first:
            cp.start()
        passed = [copy(4 + j, (*chip, c), sibling) for j, chip in enumerate(chips)]
        for j, chip in enumerate(chips):
            copy(1 + j, (*chip, c), me).wait_recv()
            passed[j].start()
        copy(0, sibling, me).wait_recv()
        for j, chip in enumerate(chips):
            copy(4 + j, (*chip, 1 - c), me).wait_recv()
        for cp in first + passed:
            cp.wait_send()
        mine.wait()

    return _pcall(
        body, name=name,
        out_shape=jax.ShapeDtypeStruct((N_DEV, R, C), v.dtype),
        in_specs=[pl.BlockSpec(memory_space=pltpu.VMEM)],
        out_specs=pl.BlockSpec(memory_space=pltpu.VMEM),
        scratch_shapes=[pltpu.SemaphoreType.DMA((7,)), pltpu.SemaphoreType.DMA((7,)), pltpu.SemaphoreType.DMA],
        compiler_params=pltpu.CompilerParams(vmem_limit_bytes=VMEM_LIMIT),
    )(v)


_HBM = pl.BlockSpec(memory_space=pltpu.HBM)
_SEM = pl.BlockSpec(memory_space=pltpu.SEMAPHORE)
_ANY = pl.BlockSpec(memory_space=pl.ANY)
_EFFECT = pltpu.SideEffectType.DATAFLOW_SIDE_EFFECTING
_PENDING = []


def _pcall(body, **kw):
    def run(*operands):
        if not _PENDING or "in_specs" not in kw:
            return pl.pallas_call(body, **kw)(*operands)
        deps = list(_PENDING)
        n = len(operands)

        def tied(*refs):
            return body(*refs[:n], *refs[n + len(deps):])

        return pl.pallas_call(tied, **{**kw, "in_specs": list(kw["in_specs"]) + [_ANY] * len(deps)})(*operands, *deps)
    return run


def _copies(plan, refs, send_sems, recv_sems):
    return [pltpu.make_async_remote_copy(src_ref=src, dst_ref=dst, send_sem=send_sems.at[k], recv_sem=recv_sems.at[k],
                                         device_id=dev, device_id_type=MESH)
            for k, (src, dst, dev) in enumerate(plan(refs))]


def _xfer_start(name, bufs, plan, n_copies, after=None):
    n = len(bufs)
    deps = list(_PENDING) + ([after] if after is not None else [])
    nd = len(deps)

    def body(*refs):
        for cp in _copies(plan, refs[:n], refs[n + nd], refs[n + nd + 1]):
            cp.start()
        refs[-1][...] = jnp.zeros_like(refs[-1])

    outs = pl.pallas_call(
        body, name=name,
        out_shape=(pltpu.SemaphoreType.DMA((n_copies,)), pltpu.SemaphoreType.DMA((n_copies,)),
                   *[pltpu.HBM(b.shape, b.dtype) for b in bufs], jax.ShapeDtypeStruct((8, 128), F32)),
        in_specs=[_HBM] * n + [_ANY] * nd,
        out_specs=(_SEM, _SEM, *[_HBM] * n, pl.BlockSpec(memory_space=pltpu.VMEM)),
        input_output_aliases={t: 2 + t for t in range(n)},
        compiler_params=pltpu.CompilerParams(has_side_effects=_EFFECT),
    )(*[pltpu.with_memory_space_constraint(b, pltpu.HBM) for b in bufs], *deps)
    _PENDING[:] = [outs[-1]]
    return (outs[0], outs[1]), list(outs[2:2 + n])


def _xfer_wait(name, sems, bufs, plan, after):
    n = len(bufs)
    after = tuple(after) if isinstance(after, (tuple, list)) else (after,)

    def body(*refs):
        cps = _copies(plan, refs[:n], refs[n], refs[n + 1])
        for cp in cps:
            cp.wait_send()
        for cp in cps:
            cp.wait_recv()

    outs = pl.pallas_call(
        body, name=name,
        out_shape=tuple(pltpu.HBM(b.shape, b.dtype) for b in bufs),
        in_specs=[_HBM] * n + [_SEM, _SEM] + [_ANY] * len(after),
        out_specs=tuple([_HBM] * n),
        input_output_aliases={t: t for t in range(n)},
        compiler_params=pltpu.CompilerParams(has_side_effects=_EFFECT),
    )(*bufs, sems[0], sems[1], *after)
    return list(outs)


def _half(ref_rows, hc):
    h = ref_rows // 2
    return pl.ds(hc * h, h)


ALL_CHIPS = (0, 1, 2)
NEIGHBOURS = (0, 1)
DIAGONAL = (2,)


def _plan_gather_ici(which):
    def plan(bufs):
        x, y, c = _pos()
        j = 2 * x + y
        chips = _other_chips(x, y)
        return [(b.at[j, _half(b.shape[1], c)], b.at[j, _half(b.shape[1], c)], (*chips[k], c))
                for b in bufs for k in which]
    return plan


def _plan_gather_d2d(which):
    def plan(bufs):
        x, y, c = _pos()
        chips = _other_chips(x, y)
        out = []
        for b in bufs:
            for k in which:
                blk = b.at[2 * chips[k][0] + chips[k][1], _half(b.shape[1], c)]
                out.append((blk, blk, (x, y, 1 - c)))
        return out
    return plan


def _plan_pair_swap(n):
    def plan(bufs):
        x, y, c = _pos()
        return [(g.at[:, _half(g.shape[1], 1 - c)], land, (x, y, 1 - c)) for g, land in zip(bufs[:n], bufs[n:])]
    return plan


def _plan_chip_scatter(n):
    def plan(bufs):
        x, y, c = _pos()
        return [(p.at[2 * chip[0] + chip[1]], land.at[k], (*chip, c))
                for p, land in zip(bufs[:n], bufs[n:]) for k, chip in enumerate(_other_chips(x, y))]
    return plan


def _plan_pair_join(bufs):
    x, y, c = _pos()
    return [(b.at[_half(b.shape[0], c)], b.at[_half(b.shape[0], c)], (x, y, 1 - c)) for b in bufs]


def _empty_hbm(shape, dtype):
    return pltpu.with_memory_space_constraint(lax.empty(shape, dtype), pltpu.HBM)


def _gather_start(tag, bufs, after=None):
    sems, bufs = _xfer_start(f"gather_ici_start_{tag}", bufs, _plan_gather_ici(ALL_CHIPS), 3 * len(bufs), after)
    return dict(tag=tag, sems=sems, bufs=bufs)


def _gather_mid(st, after):
    tag = st["tag"]
    bufs = _xfer_wait(f"gather_ici_wait_{tag}", st["sems"], st["bufs"], _plan_gather_ici(ALL_CHIPS), after)
    sems, bufs = _xfer_start(f"gather_d2d_start_{tag}", bufs, _plan_gather_d2d(ALL_CHIPS), 3 * len(bufs))
    return dict(tag=tag, sems=sems, bufs=bufs)


def _gather_finish(st, after):
    return _xfer_wait(f"gather_d2d_wait_{st['tag']}", st["sems"], st["bufs"], _plan_gather_d2d(ALL_CHIPS), after)


def _d2d_hand_over(tag, sems, bufs, which, after):
    bufs = _xfer_wait(f"gather_ici_wait_{tag}", sems, bufs, _plan_gather_ici(which), after)
    sems, bufs = _xfer_start(f"gather_d2d_start_{tag}", bufs, _plan_gather_d2d(which), len(which) * len(bufs))
    return _xfer_wait(f"gather_d2d_wait_{tag}", sems, bufs, _plan_gather_d2d(which), after)


def _pair_add(g, r, name):
    S, R, C = g.shape
    h = R // 2
    tr = _row_tile(h, C)
    nb = h // tr

    def body(c_ref, g_ref, r_ref, o_ref):
        o_ref[...] = (g_ref[...].astype(F32) + r_ref[...].astype(F32)).astype(BF16)

    return _pcall(
        body, name=name,
        grid_spec=pltpu.PrefetchScalarGridSpec(
            num_scalar_prefetch=1, grid=(S, nb),
            in_specs=[pl.BlockSpec((1, tr, C), lambda s, i, c_ref: (s, c_ref[0] * nb + i, 0)),
                      pl.BlockSpec((1, tr, C), lambda s, i, c_ref: (s, i, 0))],
            out_specs=pl.BlockSpec((1, tr, C), lambda s, i, c_ref: (s, i, 0))),
        out_shape=jax.ShapeDtypeStruct((S, h, C), BF16),
        compiler_params=_params(("parallel", "parallel")),
    )(lax.axis_index("c").astype(jnp.int32).reshape(1), g, r)


def _chip_sum(p, rb, name):
    S, h, C = p.shape
    tr = _row_tile(h, C)
    nb = h // tr
    jc = jnp.concatenate([_chip_index(), lax.axis_index("c").astype(jnp.int32).reshape(1)])

    def body(jc_ref, p_ref, r_ref, o_ref):
        o_ref[...] = ((p_ref[0].astype(F32) + r_ref[0].astype(F32)) + r_ref[1].astype(F32)) + r_ref[2].astype(F32)

    return _pcall(
        body, name=name,
        grid_spec=pltpu.PrefetchScalarGridSpec(
            num_scalar_prefetch=1, grid=(nb,),
            in_specs=[pl.BlockSpec((1, tr, C), lambda i, jc_ref: (jc_ref[0], i, 0)),
                      pl.BlockSpec((3, tr, C), lambda i, jc_ref: (0, i, 0))],
            out_specs=pl.BlockSpec((tr, C), lambda i, jc_ref: (jc_ref[1] * nb + i, 0))),
        out_shape=jax.ShapeDtypeStruct((2 * h, C), F32),
        compiler_params=_params(("parallel",)),
    )(jc, p, rb)


def _rs_start(tag, gs):
    n = len(gs)
    lands = [_empty_hbm((g.shape[0], g.shape[1] // 2, g.shape[2]), g.dtype) for g in gs]
    sems, bufs = _xfer_start(f"rs_swap_start_{tag}", list(gs) + lands, _plan_pair_swap(n), n)
    return dict(tag=tag, n=n, sems=sems, bufs=bufs)


def _rs_scatter(st, after):
    tag, n = st["tag"], st["n"]
    bufs = _xfer_wait(f"rs_swap_wait_{tag}", st["sems"], st["bufs"], _plan_pair_swap(n), after)
    ps = [_pair_add(g, r, f"rs_pair_add_{tag}{t}") for t, (g, r) in enumerate(zip(bufs[:n], bufs[n:]))]
    lands = [_empty_hbm((3,) + p.shape[1:], p.dtype) for p in ps]
    sems, bufs = _xfer_start(f"rs_scatter_start_{tag}", ps + lands, _plan_chip_scatter(n), 3 * n)
    return dict(tag=tag, n=n, sems=sems, bufs=bufs)


def _rs_join(st, after):
    tag, n = st["tag"], st["n"]
    bufs = _xfer_wait(f"rs_scatter_wait_{tag}", st["sems"], st["bufs"], _plan_chip_scatter(n), after)
    fs = [_chip_sum(p, rb, f"rs_chip_sum_{tag}{t}") for t, (p, rb) in enumerate(zip(bufs[:n], bufs[n:]))]
    sems, bufs = _xfer_start(f"rs_join_start_{tag}", fs, _plan_pair_join, n)
    return dict(tag=tag, n=n, sems=sems, bufs=bufs)


def _rs_finish(st, after):
    return _xfer_wait(f"rs_join_wait_{st['tag']}", st["sems"], st["bufs"], _plan_pair_join, after)


def _sum8(g, name):
    _, R, C = g.shape

    def body(g_ref, o_ref):
        acc = g_ref[0]
        for d in range(1, N_DEV):
            acc = acc + g_ref[d]
        o_ref[...] = acc

    return _pcall(body, name=name, out_shape=jax.ShapeDtypeStruct((R, C), F32),
                          compiler_params=_params())(g)


def _ada_fwd(cs, w, b):
    D, n = w.shape
    tn = _pick(n, (512, 384, 256, 128))

    def body(c_ref, w_ref, b_ref, o_ref):
        cv = c_ref[...]
        a = (cv * _sigmoid(cv)).astype(BF16)
        o_ref[...] = _dot(a, w_ref[...].astype(BF16)) + b_ref[...]

    return _pcall(
        body, name="ada_fwd", grid=(n // tn,),
        in_specs=[pl.BlockSpec((16, D), lambda j: (0, 0)), pl.BlockSpec((D, tn), lambda j: (0, j)),
                  pl.BlockSpec((1, tn), lambda j: (0, j))],
        out_specs=pl.BlockSpec((16, tn), lambda j: (0, j)),
        out_shape=jax.ShapeDtypeStruct((16, n), F32),
        compiler_params=_params(("parallel",)),
    )(cs, w, b)


def _ada_bwd(cs, w, dmod):
    D, n = w.shape
    tn = _pick(n, (512, 384, 256, 128))

    def body(c_ref, w_ref, d_ref, gw_ref, da_ref):
        j = pl.program_id(0)
        cv = c_ref[...]
        a = cv * _sigmoid(cv)
        d = d_ref[...]
        gw_ref[...] = lax.dot_general(a, d, (((0,), (0,)), ((), ())), precision=HI, preferred_element_type=F32)

        @pl.when(j == 0)
        def _():
            da_ref[...] = jnp.zeros_like(da_ref)

        da_ref[...] += _dot_nt(d.astype(BF16), w_ref[...].astype(BF16))

    return _pcall(
        body, name="ada_bwd", grid=(n // tn,),
        in_specs=[pl.BlockSpec((16, D), lambda j: (0, 0)), pl.BlockSpec((D, tn), lambda j: (0, j)),
                  pl.BlockSpec((16, tn), lambda j: (0, j))],
        out_specs=[pl.BlockSpec((D, tn), lambda j: (0, j)), pl.BlockSpec((16, D), lambda j: (0, 0))],
        out_shape=[jax.ShapeDtypeStruct((D, n), F32), jax.ShapeDtypeStruct((16, D), F32)],
        compiler_params=_params(("arbitrary",)),
    )(cs, w, dmod)


def _rms1_fwd(xall, gain, shift2, scale2, n_ctx):
    T, D = xall.shape
    tb = _pick(n_ctx, (256, 128, 64, 32, 16))
    nctx = n_ctx // tb

    def body(x_ref, g_ref, sh_ref, sc_ref, o_ref):
        i = pl.program_id(0)
        xv = x_ref[...]
        r = lax.rsqrt(jnp.mean(xv * xv, axis=-1, keepdims=True) + EPS)
        nrm = xv * r * g_ref[...]
        lat = i >= nctx
        sh = jnp.where(lat, sh_ref[1:2, :], sh_ref[0:1, :])
        sc = jnp.where(lat, sc_ref[1:2, :], sc_ref[0:1, :])
        o_ref[...] = (nrm * (1.0 + sc) + sh).astype(BF16)

    vec = lambda r: pl.BlockSpec((r, D), lambda i: (0, 0))
    return _pcall(
        body, name="rms1_fwd", grid=(T // tb,),
        in_specs=[pl.BlockSpec((tb, D), lambda i: (i, 0)), vec(1), vec(2), vec(2)],
        out_specs=pl.BlockSpec((tb, D), lambda i: (i, 0)),
        out_shape=jax.ShapeDtypeStruct((T, D), BF16),
        compiler_params=_params(("parallel",)),
    )(xall, gain, shift2, scale2)


def _rms1_bwd(xall, dh, dxmid, gain, scale2, n_ctx):
    T, D = xall.shape
    L = T - n_ctx
    tb = _pick(n_ctx, (256, 128, 64, 32, 16))
    nctx = n_ctx // tb

    def body(x_ref, dh_ref, dxm_ref, g_ref, sc_ref, dx_ref, cs_ref):
        i = pl.program_id(0)
        lat = i >= nctx
        xv = x_ref[...]
        r = lax.rsqrt(jnp.mean(xv * xv, axis=-1, keepdims=True) + EPS)
        xh = xv * r
        g = g_ref[...]
        nrm = xh * g
        sc = jnp.where(lat, sc_ref[1:2, :], sc_ref[0:1, :])
        dhv = dh_ref[...]
        dn = dhv * (1.0 + sc)
        dxh = dn * g
        dxv = r * (dxh - xh * jnp.mean(dxh * xh, axis=-1, keepdims=True))
        s_sh = jnp.sum(dhv, axis=0, keepdims=True)
        s_sc = jnp.sum(dhv * nrm, axis=0, keepdims=True)
        s_g = jnp.sum(dn * xh, axis=0, keepdims=True)
        zero = jnp.zeros_like(s_sh)
        rows = lax.broadcasted_iota(jnp.int32, (8, D), 0)
        upd = jnp.where(rows == 0, jnp.where(lat, zero, s_sh),
              jnp.where(rows == 1, jnp.where(lat, zero, s_sc),
              jnp.where(rows == 2, jnp.where(lat, s_sh, zero),
              jnp.where(rows == 3, jnp.where(lat, s_sc, zero),
              jnp.where(rows == 4, s_g, 0.0)))))

        @pl.when(i == 0)
        def _():
            cs_ref[...] = jnp.zeros_like(cs_ref)

        cs_ref[...] += upd

        @pl.when(lat)
        def _():
            dx_ref[...] = dxv + dxm_ref[...]

    lat_blk = lambda i: (jnp.maximum(i - nctx, 0), 0)
    vec = lambda r: pl.BlockSpec((r, D), lambda i: (0, 0))
    return _pcall(
        body, name="rms1_bwd", grid=(T // tb,),
        in_specs=[pl.BlockSpec((tb, D), lambda i: (i, 0)), pl.BlockSpec((tb, D), lambda i: (i, 0)),
                  pl.BlockSpec((tb, D), lat_blk), vec(1), vec(2)],
        out_specs=[pl.BlockSpec((tb, D), lat_blk), vec(8)],
        out_shape=[jax.ShapeDtypeStruct((L, D), F32), jax.ShapeDtypeStruct((8, D), F32)],
        compiler_params=_params(("arbitrary",)),
    )(xall, dh, dxmid, gain, scale2)


def _resid_rms2_fwd(x, mo, vecs):
    L, D = x.shape
    tb = _pick(L, (256, 128, 64))

    def body(x_ref, mo_ref, v_ref, xm_ref, h_ref):
        xm = x_ref[...] + v_ref[0:1, :] * mo_ref[...]
        xm_ref[...] = xm
        r = lax.rsqrt(jnp.mean(xm * xm, axis=-1, keepdims=True) + EPS)
        h_ref[...] = (xm * r * v_ref[1:2, :] * (1.0 + v_ref[3:4, :]) + v_ref[2:3, :]).astype(BF16)

    blk = pl.BlockSpec((tb, D), lambda i: (i, 0))
    return _pcall(
        body, name="resid_rms2_fwd", grid=(L // tb,),
        in_specs=[blk, blk, pl.BlockSpec((8, D), lambda i: (0, 0))],
        out_specs=[blk, blk],
        out_shape=[jax.ShapeDtypeStruct((L, D), F32), jax.ShapeDtypeStruct((L, D), BF16)],
        compiler_params=_params(("parallel",)),
    )(x, mo, vecs)


def _resid_rms2_bwd(xmid, dh_a, dh_b, dy, mo, vecs):
    L, D = xmid.shape
    tb = _pick(L, (256, 128, 64))

    def body(xm_ref, da_ref, db_ref, dy_ref, mo_ref, v_ref, dxm_ref, dmo_ref, cs_ref):
        i = pl.program_id(0)
        xm = xm_ref[...]
        r = lax.rsqrt(jnp.mean(xm * xm, axis=-1, keepdims=True) + EPS)
        xh = xm * r
        g = v_ref[1:2, :]
        nrm = xh * g
        dhv = da_ref[...] + db_ref[...]
        dn = dhv * (1.0 + v_ref[3:4, :])
        dxh = dn * g
        dxm = dy_ref[...] + r * (dxh - xh * jnp.mean(dxh * xh, axis=-1, keepdims=True))
        dxm_ref[...] = dxm
        dmo_ref[...] = (dxm * v_ref[0:1, :]).astype(BF16)
        s0 = jnp.sum(dhv, axis=0, keepdims=True)
        s1 = jnp.sum(dhv * nrm, axis=0, keepdims=True)
        s2 = jnp.sum(dn * xh, axis=0, keepdims=True)
        s3 = jnp.sum(dxm * mo_ref[...], axis=0, keepdims=True)
        rows = lax.broadcasted_iota(jnp.int32, (8, D), 0)
        upd = jnp.where(rows == 0, s0, jnp.where(rows == 1, s1, jnp.where(rows == 2, s2,
              jnp.where(rows == 3, s3, 0.0))))

        @pl.when(i == 0)
        def _():
            cs_ref[...] = jnp.zeros_like(cs_ref)

        cs_ref[...] += upd

    blk = pl.BlockSpec((tb, D), lambda i: (i, 0))
    vec = pl.BlockSpec((8, D), lambda i: (0, 0))
    return _pcall(
        body, name="resid_rms2_bwd", grid=(L // tb,),
        in_specs=[blk, blk, blk, blk, blk, vec],
        out_specs=[blk, blk, vec],
        out_shape=[jax.ShapeDtypeStruct((L, D), F32), jax.ShapeDtypeStruct((L, D), BF16),
                   jax.ShapeDtypeStruct((8, D), F32)],
        compiler_params=_params(("arbitrary",)),
    )(xmid, dh_a, dh_b, dy, mo, vecs)


def _loss_head(xmid, f, g2, target):
    L, D = xmid.shape
    tb = _pick(L, (256, 128, 64))

    def body(xm_ref, f_ref, g_ref, t_ref, dy_ref, df_ref, s_ref):
        i = pl.program_id(0)
        fv = f_ref[...]
        g = g_ref[...]
        err = xm_ref[...] + g * fv - t_ref[...]
        dy = err * (1.0 / D)
        dy_ref[...] = dy
        df_ref[...] = (dy * g).astype(BF16)
        s0 = jnp.sum(dy * fv, axis=0, keepdims=True)
        part = 0.5 * jnp.sum(jnp.mean(err * err, axis=-1, keepdims=True), axis=0, keepdims=True)
        rows = lax.broadcasted_iota(jnp.int32, (8, D), 0)
        upd = jnp.where(rows == 0, s0, jnp.where(rows == 1, part, 0.0))

        @pl.when(i == 0)
        def _():
            s_ref[...] = jnp.zeros_like(s_ref)

        s_ref[...] += upd

    blk = pl.BlockSpec((tb, D), lambda i: (i, 0))
    return _pcall(
        body, name="loss_head", grid=(L // tb,),
        in_specs=[blk, blk, pl.BlockSpec((1, D), lambda i: (0, 0)), blk],
        out_specs=[blk, blk, pl.BlockSpec((8, D), lambda i: (0, 0))],
        out_shape=[jax.ShapeDtypeStruct((L, D), F32), jax.ShapeDtypeStruct((L, D), BF16),
                   jax.ShapeDtypeStruct((8, D), F32)],
        compiler_params=_params(("arbitrary",)),
    )(xmid, f, g2, target)


def _gate_cols(D, off):
    tc = _pick(np.gcd(D, off), (512, 256, 128))
    return tc, off // tc


def _merge_fwd(za, zb, p, n_ctx, off_a, off_b):
    L, D = za.shape
    tb = _pick(n_ctx, (256, 128, 64, 32, 16))
    nctx = n_ctx // tb
    tc, oa = _gate_cols(D, off_a)
    _, ob = _gate_cols(D, off_b)
    if off_b % tc:
        raise ValueError("gate column offsets must share a column tile")
    ob = off_b // tc

    def body(za_ref, zb_ref, ga_ref, gb_ref, z_ref):
        z_ref[...] = (_sigmoid(ga_ref[...]) * za_ref[...].astype(F32)
                      + _sigmoid(gb_ref[...]) * zb_ref[...].astype(F32)).astype(BF16)

    blk = pl.BlockSpec((tb, tc), lambda i, j: (i, j))
    return _pcall(
        body, name="merge_fwd", grid=(L // tb, D // tc),
        in_specs=[blk, blk, pl.BlockSpec((tb, tc), lambda i, j: (i + nctx, oa + j)),
                  pl.BlockSpec((tb, tc), lambda i, j: (i + nctx, ob + j))],
        out_specs=blk,
        out_shape=jax.ShapeDtypeStruct((L, D), BF16),
        compiler_params=_params(("parallel", "parallel")),
    )(za, zb, p, p)


def _merge_bwd(dz, za, zb, p, n_ctx, off_a, off_b):
    L, D = za.shape
    T = L + n_ctx
    tb = _pick(n_ctx, (256, 128, 64, 32, 16))
    nctx = n_ctx // tb
    tc = _gate_cols(D, off_a)[0]
    oa, ob = off_a // tc, off_b // tc

    def body(dz_ref, za_ref, zb_ref, ga_ref, gb_ref, dza_ref, dzb_ref, dga_ref, dgb_ref):
        i = pl.program_id(1)

        @pl.when(i < nctx)
        def _():
            dga_ref[...] = jnp.zeros_like(dga_ref)
            dgb_ref[...] = jnp.zeros_like(dgb_ref)

        @pl.when(i >= nctx)
        def _():
            dzv = dz_ref[...].astype(F32)
            sa = _sigmoid(ga_ref[...])
            sb = _sigmoid(gb_ref[...])
            dza_ref[...] = (dzv * sa).astype(BF16)
            dzb_ref[...] = (dzv * sb).astype(BF16)
            dga_ref[...] = (dzv * za_ref[...].astype(F32) * sa * (1.0 - sa)).astype(BF16)
            dgb_ref[...] = (dzv * zb_ref[...].astype(F32) * sb * (1.0 - sb)).astype(BF16)

    lat = pl.BlockSpec((tb, tc), lambda j, i: (jnp.maximum(i - nctx, 0), j))
    allr = pl.BlockSpec((tb, tc), lambda j, i: (i, j))
    return _pcall(
        body, name="merge_bwd", grid=(D // tc, T // tb),
        in_specs=[lat, lat, lat, pl.BlockSpec((tb, tc), lambda j, i: (i, oa + j)),
                  pl.BlockSpec((tb, tc), lambda j, i: (i, ob + j))],
        out_specs=[lat, lat, allr, allr],
        out_shape=[jax.ShapeDtypeStruct((L, D), BF16), jax.ShapeDtypeStruct((L, D), BF16),
                   jax.ShapeDtypeStruct((T, D), BF16), jax.ShapeDtypeStruct((T, D), BF16)],
        compiler_params=_params(("arbitrary", "arbitrary")),
    )(dz, za, zb, p, p)


def _shift_down(u, rows):
    return jnp.where(rows == 0, 0.0, pltpu.roll(u, 1, 0))


def _shift_up(u, rows):
    n = u.shape[0]
    return jnp.where(rows == n - 1, 0.0, pltpu.roll(u, n - 1, 0))


def _convgate_fwd(u1, u3, cw, cb):
    L, F = u1.shape
    tc = _pick(F, (256, 128))

    def body(u1_ref, u3_ref, w_ref, b_ref, a_ref):
        u = u1_ref[...].astype(F32)
        rows = lax.broadcasted_iota(jnp.int32, u.shape, 0)
        cv = _shift_down(u, rows) * w_ref[0:1, :] + u * w_ref[1:2, :] + _shift_up(u, rows) * w_ref[2:3, :] + b_ref[...]
        a_ref[...] = (cv * _sigmoid(cv) * u3_ref[...].astype(F32)).astype(BF16)

    blk = pl.BlockSpec((L, tc), lambda j: (0, j))
    return _pcall(
        body, name="convgate_fwd", grid=(F // tc,),
        in_specs=[blk, blk, pl.BlockSpec((8, tc), lambda j: (0, j)), pl.BlockSpec((1, tc), lambda j: (0, j))],
        out_specs=blk,
        out_shape=jax.ShapeDtypeStruct((L, F), BF16),
        compiler_params=_params(("parallel",)),
    )(u1, u3, cw, cb)


def _convgate_bwd(u1, u3, da, cw, cb):
    L, F = u1.shape
    tc = _pick(F, (256, 128))

    def body(u1_ref, u3_ref, da_ref, w_ref, b_ref, du1_ref, du3_ref, s_ref):
        u = u1_ref[...].astype(F32)
        rows = lax.broadcasted_iota(jnp.int32, u.shape, 0)
        um, up = _shift_down(u, rows), _shift_up(u, rows)
        w0, w1, w2 = w_ref[0:1, :], w_ref[1:2, :], w_ref[2:3, :]
        cv = um * w0 + u * w1 + up * w2 + b_ref[...]
        s = _sigmoid(cv)
        dav = da_ref[...].astype(F32)
        du3_ref[...] = (dav * cv * s).astype(BF16)
        dcv = dav * u3_ref[...].astype(F32) * (s * (1.0 + cv * (1.0 - s)))
        du1_ref[...] = (_shift_up(dcv, rows) * w0 + dcv * w1 + _shift_down(dcv, rows) * w2).astype(BF16)
        r8 = lax.broadcasted_iota(jnp.int32, (8, tc), 0)
        s0 = jnp.sum(dcv * um, axis=0, keepdims=True)
        s1 = jnp.sum(dcv * u, axis=0, keepdims=True)
        s2 = jnp.sum(dcv * up, axis=0, keepdims=True)
        s3 = jnp.sum(dcv, axis=0, keepdims=True)
        s_ref[...] = jnp.where(r8 == 0, s0, jnp.where(r8 == 1, s1, jnp.where(r8 == 2, s2,
                     jnp.where(r8 == 3, s3, 0.0))))

    blk = pl.BlockSpec((L, tc), lambda j: (0, j))
    v8 = pl.BlockSpec((8, tc), lambda j: (0, j))
    return _pcall(
        body, name="convgate_bwd", grid=(F // tc,),
        in_specs=[blk, blk, blk, v8, pl.BlockSpec((1, tc), lambda j: (0, j))],
        out_specs=[blk, blk, v8],
        out_shape=[jax.ShapeDtypeStruct((L, F), BF16), jax.ShapeDtypeStruct((L, F), BF16),
                   jax.ShapeDtypeStruct((8, F), F32)],
        compiler_params=_params(("parallel",)),
    )(u1, u3, da, cw, cb)


def _lower_bound(lbl_ref, d):
    l0, l1 = lbl_ref[d, 0:1, :], lbl_ref[d, 1:2, :]
    m = jnp.maximum(l0, l1)
    e0, e1 = jnp.exp(l0 - m), jnp.exp(l1 - m)
    return e0 / (e0 + e1)


def _chunk_cumsum(x, rev):
    n = x.shape[0]
    r = lax.broadcasted_iota(jnp.int32, x.shape, 0) % CHUNK
    k = 1
    while k < CHUNK:
        if rev:
            x = x + jnp.where(r < CHUNK - k, pltpu.roll(x, n - k, 0), 0.0)
        else:
            x = x + jnp.where(r >= k, pltpu.roll(x, k, 0), 0.0)
        k *= 2
    return x


def _gate_terms(z, lb):
    sg = _sigmoid(z)
    f = lb + (1.0 - lb) * sg
    return sg, f


def _decay_terms(z, lb, rev):
    _, f = _gate_terms(z, lb)
    g = jnp.log(f)
    return 1.0 - f, _chunk_cumsum(g, rev), _chunk_cumsum(g, not rev) - g


def _chunk_total(c, rev):
    return c[0:1, :] if rev else c[CHUNK - 1:CHUNK, :]


def _pair_decay(c, s, rev):
    t = lax.broadcasted_iota(jnp.int32, (CHUNK, 1), 0)
    later = (t <= s) if rev else (t >= s)
    return jnp.where(later, jnp.exp(c - c[s:s + 1, :]), 0.0)


def _scan_chunk(i, n_ctx_chunks, n_chunks, rev):
    if not rev:
        return i
    return jnp.where(i < n_ctx_chunks, n_ctx_chunks - 1 - i, n_chunks + n_ctx_chunks - 1 - i)


def _rows(ci):
    return pl.ds(pl.multiple_of(ci * CHUNK, CHUNK), CHUNK)


def _hgrn_cols(HA):
    return HA // HEAD


def _hgrn_fwd(p, lbl, ng, n_ctx, HA):
    T = p.shape[0]
    L = T - n_ctx
    nh = _hgrn_cols(HA)
    nc, ncc = T // CHUNK, n_ctx // CHUNK

    def body(q_ref, zf_ref, zb_ref, v_ref, og_ref, lbl_ref, ng_ref, ya_ref, o_ref, st_ref,
             c_scr, k_scr, qe_scr, ke_scr, o_scr):
        dirs = ((0, False, zf_ref), (1, True, zb_ref))
        for d, rev, z_ref in dirs:
            k, c, rest = _decay_terms(z_ref[...], _lower_bound(lbl_ref, d), rev)
            c_scr[d] = c
            k_scr[d] = k
            qe_scr[d] = (q_ref[...] * jnp.exp(c)).astype(BF16)
            ke_scr[d] = (k * jnp.exp(rest)).astype(BF16)

        def step(i2, states):
            states = list(states)
            for u in range(HGRN_UNROLL):
                for d, rev, _ in dirs:
                    St = states[d]
                    ci = _scan_chunk(HGRN_UNROLL * i2 + u, ncc, nc, rev)
                    rows = _rows(ci)
                    q, v, c, k = q_ref[rows, :], v_ref[rows, :], c_scr[d, rows, :], k_scr[d, rows, :]
                    st_ref[0, d, ci] = St.astype(BF16)
                    o = jnp.zeros((CHUNK, HEAD), F32)
                    for s in range(CHUNK):
                        E = _pair_decay(c, s, rev)
                        a = jnp.sum(q * E * k[s:s + 1, :], axis=1, keepdims=True)
                        o = o + a * v[s:s + 1, :]
                    o_scr[d, rows, :] = o + _dot_nt(qe_scr[d, rows, :], St.astype(BF16))
                    states[d] = St * jnp.exp(_chunk_total(c, rev)) + _dot_tn(v.astype(BF16), ke_scr[d, rows, :])
            return tuple(states)

        if nc % HGRN_UNROLL:
            raise ValueError("the number of chunks must be a multiple of HGRN_UNROLL")
        zero = jnp.zeros((HEAD, HEAD), F32)
        lax.fori_loop(0, nc // HGRN_UNROLL, step, (zero, zero))

        o = o_scr[0, pl.ds(n_ctx, L), :] + o_scr[1, pl.ds(n_ctx, L), :]
        o_ref[...] = o
        r = lax.rsqrt(jnp.mean(o * o, axis=-1, keepdims=True) + EPS)
        og = og_ref[pl.ds(n_ctx, L), :]
        ya_ref[...] =(o * r * ng_ref[...] * (og * _sigmoid(og))).astype(BF16)

    cb = HA // HEAD
    col = lambda kk: pl.BlockSpec((T, HEAD), lambda h: (0, kk * cb + h))
    return _pcall(
        body, name="hgrn_fwd", grid=(nh,),
        in_specs=[col(0), col(1), col(2), col(3), col(4),
                  pl.BlockSpec((2, 2, HEAD), lambda h: (0, 0, h)), pl.BlockSpec((1, HEAD), lambda h: (0, 0))],
        out_specs=[pl.BlockSpec((L, HEAD), lambda h: (0, h)), pl.BlockSpec((L, HEAD), lambda h: (0, h)),
                   pl.BlockSpec((1, 2, nc, HEAD, HEAD), lambda h: (h, 0, 0, 0, 0))],
        out_shape=[jax.ShapeDtypeStruct((L, HA), BF16), jax.ShapeDtypeStruct((L, HA), F32),
                   jax.ShapeDtypeStruct((nh, 2, nc, HEAD, HEAD), BF16)],
        scratch_shapes=[pltpu.VMEM((2, T, HEAD), F32), pltpu.VMEM((2, T, HEAD), F32),
                        pltpu.VMEM((2, T, HEAD), BF16), pltpu.VMEM((2, T, HEAD), BF16),
                        pltpu.VMEM((2, T, HEAD), F32)],
        compiler_params=_params(("parallel",)),
    )(p, p, p, p, p, lbl, ng)


def _hgrn_bwd(p, lbl, ng, o, dya, st, n_ctx, HA):
    T = p.shape[0]
    L = T - n_ctx
    nh = _hgrn_cols(HA)
    nc, ncc = T // CHUNK, n_ctx // CHUNK

    def body(q_ref, zf_ref, zb_ref, v_ref, og_ref, lbl_ref, ng_ref, o_ref, dya_ref, st_ref,
             dq_ref, dzf_ref, dzb_ref, dv_ref, dog_ref, dlbl_ref, dng_ref,
             do_scr, c_scr, k_scr, qe_scr, ke_scr, dg_scr, dk_scr, dq_scr, dv_scr, row_scr):
        h = pl.program_id(0)
        ov = o_ref[...]
        r = lax.rsqrt(jnp.mean(ov * ov, axis=-1, keepdims=True) + EPS)
        oh = ov * r
        ogv = og_ref[pl.ds(n_ctx, L), :]
        sg_o = _sigmoid(ogv)
        dyv = dya_ref[...]
        ngv = ng_ref[...]
        dog_ref[pl.ds(0, n_ctx), :] = jnp.zeros((n_ctx, HEAD), BF16)
        dog_ref[pl.ds(n_ctx, L), :] = (dyv * oh * ngv * (sg_o * (1.0 + ogv * (1.0 - sg_o)))).astype(BF16)
        don = dyv * (ogv * sg_o)
        dng = jnp.sum(don * oh, axis=0, keepdims=True)
        doh = don * ngv
        do_scr[pl.ds(0, n_ctx), :] = jnp.zeros((n_ctx, HEAD), F32)
        do_scr[pl.ds(n_ctx, L), :] = r * (doh - oh * jnp.mean(doh * oh, axis=-1, keepdims=True))

        @pl.when(h == 0)
        def _():
            dng_ref[...] = jnp.zeros_like(dng_ref)

        dng_ref[0:1, :] += dng

        t16 = lax.broadcasted_iota(jnp.int32, (CHUNK, HEAD), 0)
        dirs = ((0, False, zf_ref, dzf_ref), (1, True, zb_ref, dzb_ref))
        for d, rev, z_ref, _ in dirs:
            k, c, rest = _decay_terms(z_ref[...], _lower_bound(lbl_ref, d), rev)
            c_scr[d] = c
            k_scr[d] = k
            qe_scr[d] = (q_ref[...] * jnp.exp(c)).astype(BF16)
            ke_scr[d] = (k * jnp.exp(rest)).astype(BF16)
        dq_scr[...] = jnp.zeros_like(dq_scr)
        dv_scr[...] = jnp.zeros_like(dv_scr)

        zero = jnp.zeros((HEAD, HEAD), F32)

        def bwd_chunk(i, carry, u):
            new = []
            for (d, rev, _, _), dSt in zip(dirs, carry):
                ci = _scan_chunk(i, ncc, nc, rev)
                rows = _rows(ci)
                q, v, do = q_ref[rows, :], v_ref[rows, :], do_scr[rows, :]
                c, k = c_scr[d, rows, :], k_scr[d, rows, :]
                tot = _chunk_total(c, rev)
                etot = jnp.exp(tot)
                St = st_ref[0, d, ci]
                dSb = dSt.astype(BF16)
                do_b = do.astype(BF16)
                dq_x = _dot(do_b, St) * jnp.exp(c)
                dk_x = _dot(v.astype(BF16), dSb) * jnp.exp(tot - c)
                dv_x = _dot_nt(ke_scr[d, rows, :], dSb)
                dtot = (jnp.sum(St.astype(F32) * dSt, axis=0, keepdims=True) * etot
                        + jnp.sum(k * dk_x, axis=0, keepdims=True))
                dq = jnp.zeros((CHUNK, HEAD), F32)
                for s in range(CHUNK):
                    E = _pair_decay(c, s, rev)
                    XE = E * k[s:s + 1, :]
                    a = jnp.sum(q * XE, axis=1, keepdims=True)
                    da = jnp.sum(do * v[s:s + 1, :], axis=1, keepdims=True)
                    dq = dq + da * XE
                    row_scr[u, d, 0, s:s + 1, :] = jnp.sum(da * q * E, axis=0, keepdims=True)
                    row_scr[u, d, 1, s:s + 1, :] = jnp.sum(a * do, axis=0, keepdims=True)
                dq, dk, dv = dq + dq_x, row_scr[u, d, 0] + dk_x, row_scr[u, d, 1] + dv_x
                dg_scr[d, rows, :] = _chunk_cumsum(q * dq - k * dk, not rev) + dtot
                dk_scr[d, rows, :] = dk
                dq_scr[rows, :] += dq
                dv_scr[rows, :] += dv
                new.append(dSt * etot + _dot_tn(do_b, qe_scr[d, rows, :]))
            return tuple(new)

        def bwd_step(i2, carry):
            for u in range(2):
                carry = bwd_chunk(nc - 1 - (2 * i2 + u), carry, u)
            return carry

        lax.fori_loop(0, nc // 2, bwd_step, (zero, zero))

        for d, _, z_ref, dz_ref in dirs:
            lb = _lower_bound(lbl_ref, d)
            sg, f = _gate_terms(z_ref[...], lb)
            df = dg_scr[d] / f - dk_scr[d]
            dz_ref[...] = (df * (1.0 - lb) * sg * (1.0 - sg)).astype(BF16)
            dl0 = jnp.sum(df * (1.0 - sg), axis=0, keepdims=True) * lb * (1.0 - lb)
            dlbl_ref[d, 0:1, :] = dl0
            dlbl_ref[d, 1:2, :] = -dl0
        dq_ref[...] = dq_scr[...].astype(BF16)
        dv_ref[...] = dv_scr[...].astype(BF16)

    cb = HA // HEAD
    col = lambda kk: pl.BlockSpec((T, HEAD), lambda h: (0, kk * cb + h))
    tcol = pl.BlockSpec((T, HEAD), lambda h: (0, h))
    lcol = pl.BlockSpec((L, HEAD), lambda h: (0, h))
    outs = _pcall(
        body, name="hgrn_bwd", grid=(nh,),
        in_specs=[col(0), col(1), col(2), col(3), col(4),
                  pl.BlockSpec((2, 2, HEAD), lambda h: (0, 0, h)), pl.BlockSpec((1, HEAD), lambda h: (0, 0)),
                  lcol, lcol,
                  pl.BlockSpec((1, 2, nc, HEAD, HEAD), lambda h: (h, 0, 0, 0, 0), pipeline_mode=pl.Buffered(1))],
        out_specs=[tcol, tcol, tcol, tcol, tcol, pl.BlockSpec((2, 2, HEAD), lambda h: (0, 0, h)),
                   pl.BlockSpec((8, HEAD), lambda h: (0, 0))],
        out_shape=[jax.ShapeDtypeStruct((T, HA), BF16)] * 5 + [jax.ShapeDtypeStruct((2, 2, HA), F32),
                                                               jax.ShapeDtypeStruct((8, HEAD), F32)],
        scratch_shapes=[pltpu.VMEM((T, HEAD), F32),
                        pltpu.VMEM((2, T, HEAD), F32), pltpu.VMEM((2, T, HEAD), F32),
                        pltpu.VMEM((2, T, HEAD), BF16), pltpu.VMEM((2, T, HEAD), BF16),
                        pltpu.VMEM((2, T, HEAD), F32), pltpu.VMEM((2, T, HEAD), F32),
                        pltpu.VMEM((T, HEAD), F32), pltpu.VMEM((T, HEAD), F32),
                        pltpu.VMEM((2, 2, 2, CHUNK, HEAD), F32)],
        compiler_params=_params(("arbitrary",)),
    )(p, p, p, p, p, lbl, ng, o, dya, st)
    return outs


def _swap_halves(t, lane):
    q = HEAD // 4
    return jnp.where((lane % (2 * q)) < q, pltpu.roll(t, HEAD - q, 1), pltpu.roll(t, q, 1))


def _qk_norm(t, g):
    r = lax.rsqrt(jnp.mean(t * t, axis=-1, keepdims=True) + EPS)
    return t * r, r


def _rope(t, cos, sin, lane):
    return t * cos + _swap_halves(t, lane) * sin


def _qk_norm_bwd(dy, th, r, g):
    dth = dy * g
    return r * (dth - th * jnp.mean(dth * th, axis=-1, keepdims=True)), jnp.sum(dy * th, axis=0, keepdims=True)


def _rope_bwd(dy, cos, sin, lane):
    return dy * cos + _swap_halves(dy * sin, lane)


def _na_geometry(L):
    n_rows = L // GRID_W
    kr = min(WIN_R, n_rows)
    return n_rows, kr


def _na_prep(q_ref, k_ref, v_ref, gq_ref, gk_ref, cos_ref, sin_ref, qs, ks, vs, n_ctx, L):
    lane = lax.broadcasted_iota(jnp.int32, (L, HEAD), 1)
    cos, sin = cos_ref[...], sin_ref[...]
    qh, _ = _qk_norm(q_ref[pl.ds(n_ctx, L), :], None)
    qs[...] = _rope(qh * gq_ref[...], cos, sin, lane).astype(BF16)
    kh, _ = _qk_norm(k_ref[pl.ds(n_ctx, L), :], None)
    ks[pl.ds(n_ctx, L), :] = _rope(kh * gk_ref[...], cos, sin, lane).astype(BF16)
    kc, _ = _qk_norm(k_ref[pl.ds(0, n_ctx), :], None)
    ks[pl.ds(0, n_ctx), :] = (kc * gk_ref[...]).astype(BF16)
    vs[...] = v_ref[...].astype(BF16)


NA_RB = 4


def _na_band_rows(kr):
    return kr + NA_RB


def _na_scores(i, qs, ks, bias_ref, n_ctx, n_rows, kr):
    scale = HEAD ** -0.5
    kb = _na_band_rows(kr)
    rq = NA_RB * i
    r0 = jnp.clip(rq - WIN_R // 2, 0, n_rows - kb)
    qrows = pl.ds(pl.multiple_of(rq * GRID_W, NA_RB * GRID_W), NA_RB * GRID_W)
    krows = pl.ds(pl.multiple_of(n_ctx + r0 * GRID_W, GRID_W), kb * GRID_W)
    qv = qs[qrows, :]
    sb = _dot_nt(qv, ks[krows, :]) * scale
    band_row = lax.broadcasted_iota(jnp.int32, (GRID_W, kb * GRID_W), 1) // GRID_W
    parts, tiles = [], []
    for u in range(NA_RB):
        r_u = rq + u
        first = jnp.clip(r_u - WIN_R // 2, 0, n_rows - kr) - r0
        idx = [jnp.clip(r0 - r_u + (WIN_R - 1) + 2 * jj, 0, 2 * WIN_R - 1) for jj in range(kb // 2)]
        bias_u = jnp.concatenate([bias_ref[0, t] for t in idx], axis=1)
        inside = (band_row >= first) & (band_row < first + kr)
        parts.append(jnp.where(inside, sb[u * GRID_W:(u + 1) * GRID_W, :] + bias_u, NEG))
        tiles.append(idx)
    sb = jnp.concatenate(parts, axis=0)
    sc = _dot_nt(qv, ks[pl.ds(0, n_ctx), :]) * scale
    m = jnp.maximum(jnp.max(sb, axis=1, keepdims=True), jnp.max(sc, axis=1, keepdims=True))
    eb, ec = jnp.exp(sb - m), jnp.exp(sc - m)
    inv = 1.0 / (jnp.sum(eb, axis=1, keepdims=True) + jnp.sum(ec, axis=1, keepdims=True))
    return eb * inv, ec * inv, qrows, krows, tiles


def _na_fwd(p, bias, gq, gk, cos, sin, n_ctx, off, HB):
    T = p.shape[0]
    L = T - n_ctx
    nh = HB // HEAD
    n_rows, kr = _na_geometry(L)
    ob = off // HEAD

    def body(q_ref, k_ref, v_ref, bias_ref, gq_ref, gk_ref, cos_ref, sin_ref, y_ref, qs, ks, vs):
        _na_prep(q_ref, k_ref, v_ref, gq_ref, gk_ref, cos_ref, sin_ref, qs, ks, vs, n_ctx, L)

        def step(i, carry):
            pb, pc, qrows, krows, _ = _na_scores(i, qs, ks, bias_ref, n_ctx, n_rows, kr)
            y = _dot(pb.astype(BF16), vs[krows, :]) + _dot(pc.astype(BF16), vs[pl.ds(0, n_ctx), :])
            y_ref[qrows, :] = y.astype(BF16)
            return carry

        lax.fori_loop(0, n_rows // NA_RB, step, 0)

    col = lambda kk: pl.BlockSpec((T, HEAD), lambda h: (0, ob + kk * nh + h))
    vec = pl.BlockSpec((1, HEAD), lambda h: (0, 0))
    tab = pl.BlockSpec((L, HEAD), lambda h: (0, 0))
    return _pcall(
        body, name="na_fwd", grid=(nh,),
        in_specs=[col(0), col(1), col(2), pl.BlockSpec((1,) + bias.shape[1:], lambda h: (h, 0, 0, 0)),
                  vec, vec, tab, tab],
        out_specs=pl.BlockSpec((L, HEAD), lambda h: (0, h)),
        out_shape=jax.ShapeDtypeStruct((L, HB), BF16),
        scratch_shapes=[pltpu.VMEM((L, HEAD), BF16), pltpu.VMEM((T, HEAD), BF16), pltpu.VMEM((T, HEAD), BF16)],
        compiler_params=_params(("parallel",)),
    )(p, p, p, bias, gq, gk, cos, sin)


def _na_bwd(p, bias, gq, gk, cos, sin, dyb, n_ctx, off, HB):
    T = p.shape[0]
    L = T - n_ctx
    nh = HB // HEAD
    n_rows, kr = _na_geometry(L)
    ob = off // HEAD
    scale = HEAD ** -0.5

    def body(q_ref, k_ref, v_ref, bias_ref, gq_ref, gk_ref, cos_ref, sin_ref, dy_ref,
             dq_ref, dk_ref, dv_ref, dbias_ref, dg_ref, qs, ks, vs, dqa, dka, dva):
        h = pl.program_id(0)
        _na_prep(q_ref, k_ref, v_ref, gq_ref, gk_ref, cos_ref, sin_ref, qs, ks, vs, n_ctx, L)
        dka[...] = jnp.zeros_like(dka)
        dva[...] = jnp.zeros_like(dva)
        dbias_ref[...] = jnp.zeros_like(dbias_ref)

        crows = pl.ds(0, n_ctx)

        def step(i, carry):
            pb, pc, qrows, krows, tiles = _na_scores(i, qs, ks, bias_ref, n_ctx, n_rows, kr)
            do = dy_ref[qrows, :]
            qv = qs[qrows, :]
            dpb = _dot_nt(do, vs[krows, :])
            dpc = _dot_nt(do, vs[crows, :])
            delta = jnp.sum(pb * dpb, axis=1, keepdims=True) + jnp.sum(pc * dpc, axis=1, keepdims=True)
            dsb = pb * (dpb - delta)
            dsc = pc * (dpc - delta)
            dsb_b, dsc_b = dsb.astype(BF16), dsc.astype(BF16)
            dqa[qrows, :] = (_dot(dsb_b, ks[krows, :]) + _dot(dsc_b, ks[crows, :])) * scale
            dka[krows, :] += _dot_tn(dsb_b, qv) * scale
            dka[crows, :] += _dot_tn(dsc_b, qv) * scale
            dva[krows, :] += _dot_tn(pb.astype(BF16), do)
            dva[crows, :] += _dot_tn(pc.astype(BF16), do)
            for u, idx in enumerate(tiles):
                for jj, t in enumerate(idx):
                    dbias_ref[0, t] += dsb[u * GRID_W:(u + 1) * GRID_W, jj * 2 * GRID_W:(jj + 1) * 2 * GRID_W]
            return carry

        lax.fori_loop(0, n_rows // NA_RB, step, 0)

        lane = lax.broadcasted_iota(jnp.int32, (L, HEAD), 1)
        cos, sin = cos_ref[...], sin_ref[...]
        lat, ctx = pl.ds(n_ctx, L), pl.ds(0, n_ctx)
        gqv, gkv = gq_ref[...], gk_ref[...]
        qh, rq = _qk_norm(q_ref[lat, :], None)
        dq, dgq = _qk_norm_bwd(_rope_bwd(dqa[...], cos, sin, lane), qh, rq, gqv)
        dq_ref[ctx, :] = jnp.zeros((n_ctx, HEAD), BF16)
        dq_ref[lat, :] = dq.astype(BF16)
        kh, rk = _qk_norm(k_ref[lat, :], None)
        dk, dgk = _qk_norm_bwd(_rope_bwd(dka[lat, :], cos, sin, lane), kh, rk, gkv)
        dk_ref[lat, :] = dk.astype(BF16)
        kch, rkc = _qk_norm(k_ref[ctx, :], None)
        dkc, dgkc = _qk_norm_bwd(dka[ctx, :], kch, rkc, gkv)
        dk_ref[ctx, :] = dkc.astype(BF16)
        dv_ref[...] = dva[...].astype(BF16)

        @pl.when(h == 0)
        def _():
            dg_ref[...] = jnp.zeros_like(dg_ref)

        dg_ref[0:1, :] += dgq
        dg_ref[1:2, :] += dgk + dgkc

    col = lambda kk: pl.BlockSpec((T, HEAD), lambda h: (0, ob + kk * nh + h))
    vec = pl.BlockSpec((1, HEAD), lambda h: (0, 0))
    tab = pl.BlockSpec((L, HEAD), lambda h: (0, 0))
    tcol = pl.BlockSpec((T, HEAD), lambda h: (0, h))
    bspec = pl.BlockSpec((1,) + bias.shape[1:], lambda h: (h, 0, 0, 0))
    return _pcall(
        body, name="na_bwd", grid=(nh,),
        in_specs=[col(0), col(1), col(2), bspec, vec, vec, tab, tab, pl.BlockSpec((L, HEAD), lambda h: (0, h))],
        out_specs=[tcol, tcol, tcol, bspec, pl.BlockSpec((8, HEAD), lambda h: (0, 0))],
        out_shape=[jax.ShapeDtypeStruct((T, HB), BF16)] * 3 + [jax.ShapeDtypeStruct(bias.shape, F32),
                                                               jax.ShapeDtypeStruct((8, HEAD), F32)],
        scratch_shapes=[pltpu.VMEM((L, HEAD), BF16), pltpu.VMEM((T, HEAD), BF16), pltpu.VMEM((T, HEAD), BF16),
                        pltpu.VMEM((L, HEAD), F32), pltpu.VMEM((T, HEAD), F32), pltpu.VMEM((T, HEAD), F32)],
        compiler_params=_params(("arbitrary",)),
    )(p, p, p, bias, gq, gk, cos, sin, dyb)


def _bias_tables():
    w = np.arange(GRID_W)
    col_start = np.clip(w - WIN_C // 2, 0, GRID_W - WIN_C)
    col_in = (w[None, :] >= col_start[:, None]) & (w[None, :] < col_start[:, None] + WIN_C)
    dc = np.clip(w[None, :] - w[:, None], -(WIN_C - 1), WIN_C - 1) + WIN_C - 1
    n_pair = 2 * WIN_R
    ridx = np.zeros((n_pair, GRID_W, 2 * GRID_W), np.int32)
    cidx = np.zeros((n_pair, GRID_W, 2 * GRID_W), np.int32)
    valid = np.zeros((n_pair, GRID_W, 2 * GRID_W), bool)
    for i in range(n_pair):
        for half in range(2):
            row = i + half
            sl = slice(half * GRID_W, (half + 1) * GRID_W)
            ridx[i, :, sl] = min(row, 2 * WIN_R - 2)
            cidx[i, :, sl] = dc
            valid[i, :, sl] = col_in & (row <= 2 * WIN_R - 2)
    return ridx, cidx, valid


def _bias_onehot():
    _, cidx, valid = _bias_tables()
    K = GRID_W * 2 * GRID_W
    oh = np.zeros((K, 128), np.float32)
    neg = np.full((1, K), NEG, np.float32)
    for cq in range(GRID_W):
        for ll in range(2 * GRID_W):
            if valid[0, cq, ll]:
                oh[cq * 2 * GRID_W + ll, (ll // GRID_W) * 64 + cidx[0, cq, ll]] = 1.0
                neg[0, cq * 2 * GRID_W + ll] = 0.0
    return oh, neg


def _expand_bias(table):
    H = table.shape[0]
    n_pair, n_dc = 2 * WIN_R, 2 * WIN_C - 1
    tp = jnp.pad(table, ((0, 0), (0, n_pair + 1 - table.shape[1]), (0, 64 - n_dc)))
    t2 = jnp.concatenate([tp[:, :n_pair], tp[:, 1:n_pair + 1]], axis=-1).reshape(H * n_pair, 128)
    oh, neg = _bias_onehot()

    def body(t_ref, oh_ref, neg_ref, o_ref):
        o_ref[...] = lax.dot_general(t_ref[...], oh_ref[...], (((1,), (1,)), ((), ())), precision=HI,
                                     preferred_element_type=F32) + neg_ref[...]

    out = _pcall(body, name="bias_expand", out_shape=jax.ShapeDtypeStruct((H * n_pair, oh.shape[0]), F32),
                         compiler_params=_params())(t2, jnp.asarray(oh), jnp.asarray(neg))
    return out.reshape(H, n_pair, GRID_W, 2 * GRID_W)


def _bias_grad(dbias):
    H = dbias.shape[0]
    n_pair, n_dc = 2 * WIN_R, 2 * WIN_C - 1
    K = GRID_W * 2 * GRID_W
    oh, _ = _bias_onehot()
    flat = dbias.reshape(H * n_pair, K)

    def body(d_ref, oh_ref, o_ref):
        o_ref[...] = jnp.dot(d_ref[...], oh_ref[...], precision=HI, preferred_element_type=F32)

    g = _pcall(body, name="bias_grad", out_shape=jax.ShapeDtypeStruct((H * n_pair, 128), F32),
                       compiler_params=_params())(flat, jnp.asarray(oh))
    g = g.reshape(H, n_pair, 128)
    left, right = g[:, :, :n_dc], g[:, :, 64:64 + n_dc]
    out = left[:, :n_pair - 1]
    return out.at[:, 1:].add(right[:, :n_pair - 2])


def _rope_tables(L):
    pos = np.arange(L)
    row = (pos // GRID_W).astype(np.float32)
    colp = (pos % GRID_W).astype(np.float32)
    half = HEAD // 2
    nf = half // 2
    inv = (ROPE_THETA ** (-np.arange(nf, dtype=np.float32) / nf)).astype(np.float32)

    def tabs(pv):
        ang = pv[:, None] * inv[None, :]
        c, s = np.cos(ang), np.sin(ang)
        return np.concatenate([c, c], axis=1), np.concatenate([-s, s], axis=1)

    cr, sr = tabs(row)
    cc, sc = tabs(colp)
    return (jnp.asarray(np.concatenate([cr, cc], axis=1), F32), jnp.asarray(np.concatenate([sr, sc], axis=1), F32))


def _adamw(w, g, m, v, name, after=None, copy_g=False):
    R, C = w.shape
    tr = _row_tile(R, C)
    c1 = 1.0 - ADAM_B1 ** ADAM_STEP
    c2 = 1.0 - ADAM_B2 ** ADAM_STEP
    deps = [] if after is None else [after]
    n_out = 4 if copy_g else 3

    def body(w_ref, g_ref, m_ref, v_ref, *rest):
        d_ref, mo_ref, vo_ref = rest[len(deps):len(deps) + 3]
        gv = g_ref[...]
        mn = ADAM_B1 * m_ref[...] + (1.0 - ADAM_B1) * gv
        vn = ADAM_B2 * v_ref[...] + (1.0 - ADAM_B2) * (gv * gv)
        mo_ref[...] = mn
        vo_ref[...] = vn
        d_ref[...] = -ADAM_LR * ((mn / c1) / (jnp.sqrt(vn / c2) + ADAM_EPS) + ADAM_WD * w_ref[...])
        if copy_g:
            rest[-1][...] = gv

    blk = pl.BlockSpec((tr, C), lambda i: (i, 0))
    return _pcall(
        body, name=name, grid=(R // tr,),
        in_specs=[blk] * 4 + [_ANY] * len(deps), out_specs=[blk] * n_out,
        out_shape=[jax.ShapeDtypeStruct((R, C), F32)] * n_out,
        compiler_params=_params(("parallel",)),
    )(w, g, m, v, *deps)


PACK_W = 1024


def _pack(parts):
    flat, offs, pos = [], [], 0
    for a in parts:
        n = a.size
        padn = -n % PACK_W
        flat.append(jnp.pad(a.reshape(-1).astype(F32), (0, padn)))
        offs.append((pos, n, a.shape))
        pos += n + padn
    tail = -pos % (8 * PACK_W)
    if tail:
        flat.append(jnp.zeros((tail,), F32))
    return jnp.concatenate(flat).reshape(-1, PACK_W), offs


def _unpack(buf, offs, i):
    pos, n, shape = offs[i]
    return buf.reshape(buf.shape[:-2] + (-1,))[..., pos:pos + n].reshape(buf.shape[:-2] + shape)


def kernel(x, c, ctx, c_ctx, ada_w, ada_b, norm1_g, norm2_g, w_in, hgrn_lb_logits, hgrn_norm_g, na_q_norm_g, na_k_norm_g, na_rel_bias, w_branch_a, w_branch_b, w_out, ffn_w1, ffn_w3, ffn_conv_w, ffn_conv_b, ffn_w2, loss_target, m_c_ctx, m_ada_w, m_ada_b, m_norm1_g, m_norm2_g, m_w_in, m_hgrn_lb_logits, m_hgrn_norm_g, m_na_q_norm_g, m_na_k_norm_g, m_na_rel_bias, m_w_branch_a, m_w_branch_b, m_w_out, m_ffn_w1, m_ffn_w3, m_ffn_conv_w, m_ffn_conv_b, m_ffn_w2, v_c_ctx, v_ada_w, v_ada_b, v_norm1_g, v_norm2_g, v_w_in, v_hgrn_lb_logits, v_hgrn_norm_g, v_na_q_norm_g, v_na_k_norm_g, v_na_rel_bias, v_w_branch_a, v_w_branch_b, v_w_out, v_ffn_w1, v_ffn_w3, v_ffn_conv_w, v_ffn_conv_b, v_ffn_w2):
    weights = dict(c_ctx=c_ctx, ada_w=ada_w, ada_b=ada_b, norm1_g=norm1_g, norm2_g=norm2_g, w_in=w_in,
                   hgrn_lb_logits=hgrn_lb_logits, hgrn_norm_g=hgrn_norm_g, na_q_norm_g=na_q_norm_g,
                   na_k_norm_g=na_k_norm_g, na_rel_bias=na_rel_bias, w_branch_a=w_branch_a, w_branch_b=w_branch_b,
                   w_out=w_out, ffn_w1=ffn_w1, ffn_w3=ffn_w3, ffn_conv_w=ffn_conv_w, ffn_conv_b=ffn_conv_b,
                   ffn_w2=ffn_w2)
    moms = dict(c_ctx=(m_c_ctx, v_c_ctx), ada_w=(m_ada_w, v_ada_w), ada_b=(m_ada_b, v_ada_b),
                norm1_g=(m_norm1_g, v_norm1_g), norm2_g=(m_norm2_g, v_norm2_g), w_in=(m_w_in, v_w_in),
                hgrn_lb_logits=(m_hgrn_lb_logits, v_hgrn_lb_logits), hgrn_norm_g=(m_hgrn_norm_g, v_hgrn_norm_g),
                na_q_norm_g=(m_na_q_norm_g, v_na_q_norm_g), na_k_norm_g=(m_na_k_norm_g, v_na_k_norm_g),
                na_rel_bias=(m_na_rel_bias, v_na_rel_bias), w_branch_a=(m_w_branch_a, v_w_branch_a),
                w_branch_b=(m_w_branch_b, v_w_branch_b), w_out=(m_w_out, v_w_out), ffn_w1=(m_ffn_w1, v_ffn_w1),
                ffn_w3=(m_ffn_w3, v_ffn_w3), ffn_conv_w=(m_ffn_conv_w, v_ffn_conv_w),
                ffn_conv_b=(m_ffn_conv_b, v_ffn_conv_b), ffn_w2=(m_ffn_w2, v_ffn_w2))
    order = list(weights)

    L, D = x.shape[1], x.shape[2]
    N = ctx.shape[1]
    T = N + L
    HA = w_branch_a.shape[1]
    HB = w_branch_b.shape[1]
    F = ffn_conv_b.shape[1]
    IN = 5 * HA + 3 * HB + 2 * D
    n_ada = ada_w.shape[2]
    ix, iy, ic = _pos()
    chip = 2 * ix + iy
    dev = 2 * chip + ic

    _PENDING.clear()
    pk0, offs0 = _pack([c[0], hgrn_lb_logits, ffn_conv_w[0]])
    g0 = _allgather8(pk0, "gather_small0")
    c_all = _unpack(g0, offs0, 0)
    lbl_parts = _unpack(g0, offs0, 1)
    lbl = jnp.concatenate([lbl_parts[2 * j] for j in range(N_CHIP)], axis=-1)
    cw_parts = _unpack(g0, offs0, 2)
    cw = jnp.concatenate([cw_parts[2 * j] for j in range(N_CHIP)], axis=-1)
    cw8 = jnp.pad(cw, ((0, 5), (0, 0)))

    cs = jnp.concatenate([c_all, c_ctx[None, :], jnp.zeros((7, D), F32)], axis=0)
    ada_b_mine = lax.dynamic_slice(ada_b, (0, chip * n_ada), (1, n_ada))
    mod_mine = _ada_fwd(cs, ada_w[0], ada_b_mine)
    gm = _allgather8(mod_mine, "gather_mod")
    mod = jnp.concatenate([gm[2 * j] for j in range(N_CHIP)], axis=-1)
    mod_l = lax.dynamic_slice(mod, (dev, 0), (1, N_MOD * D)).reshape(N_MOD, D)
    mod_c = mod[8].reshape(N_MOD, D)
    sh1, sc1, g1, sh2, sc2, g2 = [mod_l[i:i + 1] for i in range(N_MOD)]
    shift1 = jnp.concatenate([mod_c[0:1], sh1], axis=0)
    scale1 = jnp.concatenate([mod_c[1:2], sc1], axis=0)

    shards = [w_in[0], w_branch_a[0], w_branch_b[0], w_out[0], ffn_w1[0], ffn_w3[0], ffn_w2[0]]
    names = ["w_in", "w_a", "w_b", "w_out", "w1", "w3", "w2"]
    slots = [_cast_bf16_slot(s, "cast_" + nm) for s, nm in zip(shards, names)]
    sem_nb, win_buf = _xfer_start("gather_ici_start_in_nbr", slots[0:1], _plan_gather_ici(NEIGHBOURS), 2, gm)

    xall = jnp.concatenate([ctx[0], x[0]], axis=0)
    h_all = _rms1_fwd(xall, norm1_g, shift1, scale1, N)
    chip_i = chip.astype(jnp.int32)
    same = lambda ids: jnp.stack([jnp.stack(ids), jnp.stack(ids)])
    p = _mm_nn_sel(h_all, win_buf[0], same([chip_i]), F32, "mm_p_own")
    bias = _expand_bias(na_rel_bias[0])
    win_buf = _xfer_wait("gather_ici_wait_in_nbr", sem_nb, win_buf, _plan_gather_ici(NEIGHBOURS),
                         (p, bias, *slots[1:]))
    sem_nb, win_buf = _xfer_start("gather_d2d_start_in_nbr", win_buf, _plan_gather_d2d(NEIGHBOURS), 2)
    sem_dg, win_buf = _xfer_start("gather_ici_start_in_diag", win_buf, _plan_gather_ici(DIAGONAL), 1)
    gat_mix = _gather_start("mix", slots[1:4])
    gat_ffn = _gather_start("ffn", slots[4:6])
    gat_ffn2 = _gather_start("ffn2", slots[6:7])
    win_buf = _xfer_wait("gather_d2d_wait_in_nbr", sem_nb, win_buf, _plan_gather_d2d(NEIGHBOURS), _PENDING[0])
    p = _mm_nn_sel(h_all, win_buf[0], same([chip_i ^ 1, chip_i ^ 2]), F32, "mm_p_nbr", p)
    win_buf = _d2d_hand_over("in_diag", sem_dg, win_buf, DIAGONAL, p)
    p = _mm_nn_sel(h_all, win_buf[0], same([chip_i ^ 3]), F32, "mm_p_diag", p)
    Win = win_buf[0]
    cos, sin = _rope_tables(L)
    off_na = 5 * HA
    y_b = _na_fwd(p, bias, na_q_norm_g, na_k_norm_g, cos, sin, N, off_na, HB)
    gat_mix = _gather_mid(gat_mix, y_b)
    y_a, o_a, st_a = _hgrn_fwd(p, lbl, hgrn_norm_g, N, HA)
    Wa, Wb, Wo = _gather_finish(gat_mix, (y_a, y_b))
    Wo = Wo.reshape(1, D, D)
    za = _mm_nn(y_a, Wa, BF16, "mm_za")
    zb = _mm_nn(y_b, Wb, BF16, "mm_zb")
    off_ga, off_gb = 5 * HA + 3 * HB, 5 * HA + 3 * HB + D
    z = _merge_fwd(za, zb, p, N, off_ga, off_gb)
    gat_ffn = _gather_mid(gat_ffn, z)
    mo = _mm_nn(z, Wo, F32, "mm_mo")
    vec2 = jnp.concatenate([g1, norm2_g, sh2, sc2, jnp.zeros((4, D), F32)], axis=0)
    x_mid, h2 = _resid_rms2_fwd(x[0], mo, vec2)
    W1, W3 = _gather_finish(gat_ffn, h2)
    gat_ffn2 = _gather_mid(gat_ffn2, h2)
    u1 = _mm_nn(h2, W1, BF16, "mm_u1")
    u3 = _mm_nn(h2, W3, BF16, "mm_u3")
    (W2,) = _gather_finish(gat_ffn2, (u1, u3))
    W2 = W2.reshape(1, F, D)
    a = _convgate_fwd(u1, u3, cw8, ffn_conv_b)
    f = _mm_nn(a, W2, F32, "mm_f")
    dy, df, s_loss = _loss_head(x_mid, f, g2, loss_target[0])
    loss = lax.psum(s_loss[1, 0], ("x", "y", "c"))
    d_g2 = s_loss[0:1]

    gW2 = _mm_tn(a, df, 1, "mm_gw2").reshape(N_CHIP, F // N_CHIP, D)
    da = _mm_nt(df, W2, BF16, "mm_da")
    du1, du3, s_conv = _convgate_bwd(u1, u3, da, cw8, ffn_conv_b)
    gW1 = _mm_tn(h2, du1, N_CHIP, "mm_gw1")
    gW3 = _mm_tn(h2, du3, N_CHIP, "mm_gw3")
    rs_ffn = _rs_start("ffn", [gW2, gW1, gW3])
    dh2a = _mm_nt(du1, W1, F32, "mm_dh2a")
    dh2b = _mm_nt(du3, W3, F32, "mm_dh2b")
    rs_ffn = _rs_scatter(rs_ffn, dh2b)
    dxm, dmo, s_rms2 = _resid_rms2_bwd(x_mid, dh2a, dh2b, dy, mo, vec2)
    gWo = _mm_tn(z, dmo, 1, "mm_gwo").reshape(N_CHIP, D // N_CHIP, D)
    dz = _mm_nt(dmo, Wo, BF16, "mm_dz")
    dza, dzb, dga, dgb = _merge_bwd(dz, za, zb, p, N, off_ga, off_gb)
    gWa = _mm_tn(y_a, dza, N_CHIP, "mm_gwa")
    gWb = _mm_tn(y_b, dzb, N_CHIP, "mm_gwb")
    rs_mix = _rs_start("mix", [gWo, gWa, gWb])
    dya = _mm_nt(dza, Wa, F32, "mm_dya")
    dyb = _mm_nt(dzb, Wb, BF16, "mm_dyb")
    rs_mix = _rs_scatter(rs_mix, dyb)
    dq_a, dzf, dzbk, di_a, dog, dlbl, s_ng = _hgrn_bwd(p, lbl, hgrn_norm_g, o_a, dya, st_a, N, HA)
    rs_ffn = _rs_join(rs_ffn, dq_a)
    dq_n, dk_n, dv_n, dbias, s_qk = _na_bwd(p, bias, na_q_norm_g, na_k_norm_g, cos, sin, dyb, N, off_na, HB)
    rs_mix = _rs_join(rs_mix, dq_n)
    dp = jnp.concatenate([dq_a, dzf, dzbk, di_a, dog, dq_n, dk_n, dv_n, dga, dgb], axis=1)
    gWin = _mm_tn(h_all, dp, N_CHIP, "mm_gwin")
    rs_in = _rs_start("in", [gWin])
    rs_in = _rs_scatter(rs_in, _PENDING[0])
    dh = _mm_nt(dp, Win, F32, "mm_dh")
    grad_x, s_rms1 = _rms1_bwd(xall, dh, dxm, norm1_g, scale1, N)
    d_table = _bias_grad(dbias)

    grads = {}
    big_names = ["ada_w", "w_in", "w_branch_a", "w_branch_b", "w_out", "ffn_w1", "ffn_w3", "ffn_w2"]
    small_names = [n for n in order if n not in big_names]
    delta, new_m, new_v = {}, {}, {}

    def update(nm, after=None):
        reduced = nm != "ada_w"
        d_, m_, v_, *g_ = _adamw(weights[nm][0], grads[nm][0], moms[nm][0][0], moms[nm][1][0], "adamw_" + nm,
                                 after, copy_g=reduced)
        delta[nm], new_m[nm], new_v[nm] = d_[None], m_[None], v_[None]
        if reduced:
            grads[nm] = g_[0][None]
        return d_

    last = grad_x
    for nm, g in zip(["ffn_w2", "ffn_w1", "ffn_w3"], _rs_finish(rs_ffn, last)):
        grads[nm] = g[None]
        last = update(nm, last)
    for nm, g in zip(["w_out", "w_branch_a", "w_branch_b"], _rs_finish(rs_mix, last)):
        grads[nm] = g[None]
        last = update(nm, last)
    rs_in = _rs_join(rs_in, last)

    zD = jnp.zeros((1, D), F32)
    dmod_l = jnp.concatenate([s_rms1[2:3], s_rms1[3:4], s_rms2[3:4], s_rms2[0:1], s_rms2[1:2], d_g2], axis=0)
    dmod_c = jnp.concatenate([s_rms1[0:1], s_rms1[1:2], zD, zD, zD, zD], axis=0)
    pk1, offs1 = _pack([dmod_l, dmod_c, s_rms1[4], s_rms2[2], dlbl, s_ng[0], s_qk[0], s_qk[1], d_table,
                        s_conv[0:3], s_conv[3]])
    g1all = _allgather8(pk1, "gather_small1")
    tot1 = _sum8(g1all, "sum_small1")
    dmod_rows = _unpack(g1all, offs1, 0).reshape(N_DEV, N_MOD * D)
    dmod_c_tot = _unpack(tot1, offs1, 1).reshape(1, N_MOD * D)
    dmod16 = jnp.concatenate([dmod_rows, dmod_c_tot, jnp.zeros((7, N_MOD * D), F32)], axis=0)
    dmod16_mine = lax.dynamic_slice(dmod16, (0, chip * n_ada), (16, n_ada))
    g_ada_w, dact = _ada_bwd(cs, ada_w[0], dmod16_mine)
    pk2, offs2 = _pack([dact[8]])
    g2all = _allgather8(pk2, "gather_small2")
    dact_rows = _unpack(g2all, offs2, 0)
    dact_sel = jnp.concatenate([dact_rows[2 * j][None] for j in range(N_CHIP)] + [jnp.zeros((4, D), F32)], axis=0)

    grads["ada_w"] = g_ada_w[None]
    grads["ada_b"] =(_unpack(tot1, offs1, 0) + _unpack(tot1, offs1, 1)).reshape(1, N_MOD * D)
    grads["norm1_g"] = _unpack(tot1, offs1, 2)[None]
    grads["norm2_g"] = _unpack(tot1, offs1, 3)[None]
    g_lbl = _unpack(tot1, offs1, 4)
    n_lb = HA // N_CHIP
    grads["hgrn_lb_logits"] = lax.dynamic_slice(g_lbl, (0, 0, chip * n_lb), (2, 2, n_lb))
    grads["hgrn_norm_g"] = _unpack(tot1, offs1, 5)[None]
    grads["na_q_norm_g"] = _unpack(tot1, offs1, 6)[None]
    grads["na_k_norm_g"] = _unpack(tot1, offs1, 7)[None]
    grads["na_rel_bias"] = _unpack(tot1, offs1, 8)[None]
    g_cw = _unpack(tot1, offs1, 9)
    n_f = F // N_CHIP
    grads["ffn_conv_w"] = lax.dynamic_slice(g_cw, (0, chip * n_f), (3, n_f))[None]
    grads["ffn_conv_b"] = _unpack(tot1, offs1, 10)[None]

    g_c_ctx = _dsilu_rows(dact_sel, c_ctx[None, :], "grad_c_ctx")
    grads["c_ctx"] = g_c_ctx[0]

    last = update("ada_w", g_c_ctx)
    pw, offw = _pack([weights[n] for n in small_names])
    pg, _ = _pack([grads[n] for n in small_names])
    pm, _ = _pack([moms[n][0] for n in small_names])
    pv, _ = _pack([moms[n][1] for n in small_names])
    d_, m_, v_ = _adamw(pw, pg, pm, pv, "adamw_small", last)
    for i, nm in enumerate(small_names):
        delta[nm], new_m[nm], new_v[nm] = _unpack(d_, offw, i), _unpack(m_, offw, i), _unpack(v_, offw, i)
    grads["w_in"] = _rs_finish(rs_in, d_)[0][None]
    update("w_in")

    return (loss, grad_x[None], *[grads[n] for n in order], *[delta[n] for n in order],
            *[new_m[n] for n in order], *[new_v[n] for n in order])


def _dsilu_rows(v, cv, name):
    D = v.shape[1]

    def body(v_ref, c_ref, o_ref):
        t = c_ref[...]
        s = _sigmoid(t)
        o_ref[...] = (((v_ref[0:1, :] + v_ref[1:2, :]) + v_ref[2:3, :]) + v_ref[3:4, :]) * (s * (1.0 + t * (1.0 - s)))

    return _pcall(body, name=name, out_shape=jax.ShapeDtypeStruct((1, D), F32),
                          compiler_params=_params())(v, cv)
```
